```python
import math
import jax, jax.numpy as jnp
from jax import lax
import numpy as np

D_MODEL = 1024
BATCH = 8
SEQ = 8192
DEPTH = 1

CHUNK = 64
CONV_K = 4
EPS = 1e-6

SSD_HEADS = 16
SSD_HEAD_DIM = 64
SSD_WIDTH = SSD_HEADS * SSD_HEAD_DIM
SSD_GROUPS = 2
SSD_STATE = 128
SSD_CONV_DIM = SSD_WIDTH + 2 * SSD_GROUPS * SSD_STATE

GDN_HEADS = 8
GDN_DK = 128
GDN_DV = 128
GDN_KW = GDN_HEADS * GDN_DK
GDN_VW = GDN_HEADS * GDN_DV
GDN_CONV_DIM = 2 * GDN_KW + GDN_VW

MIX_WIDTH = SSD_WIDTH + GDN_VW
IN_SIZES = (SSD_WIDTH, SSD_CONV_DIM, SSD_HEADS, GDN_VW, GDN_CONV_DIM, GDN_HEADS, GDN_HEADS)
IN_DIM = SSD_WIDTH + SSD_CONV_DIM + SSD_HEADS + GDN_VW + GDN_CONV_DIM + 2 * GDN_HEADS

kernel_name = "hybrid_ssd_gated_deltanet_parallel_heads"


def rmsnorm(x, w):
    xf = x.astype(jnp.float32)
    return xf * lax.rsqrt(jnp.mean(xf * xf, axis=-1, keepdims=True) + EPS) * w.astype(jnp.float32)


def l2norm(x):
    return x * lax.rsqrt(jnp.sum(x * x, axis=-1, keepdims=True) + EPS)


def causal_dwconv(x, w):
    return lax.conv_general_dilated(
        x, w[:, None, :].astype(x.dtype), window_strides=(1,), padding=[(CONV_K - 1, 0)],
        dimension_numbers=("NWC", "WIO", "NWC"), feature_group_count=x.shape[-1])


def ssd_chunked(x, dt, a_neg, bm, cm):
    b, t, h, p = x.shape
    g, n = bm.shape[-2:]
    r = h // g
    c = t // CHUNK
    xs = (x * dt[..., None]).reshape(b, c, CHUNK, g, r, p)
    la = jnp.moveaxis((dt * a_neg).reshape(b, c, CHUNK, g, r), 2, -1)
    la_cum = jnp.cumsum(la, axis=-1)
    bc = bm.reshape(b, c, CHUNK, g, n)
    cc = cm.reshape(b, c, CHUNK, g, n)
    idx = jnp.arange(CHUNK)
    incl = idx[:, None] >= idx[None, :]
    lmat = jnp.exp(jnp.where(incl, la_cum[..., :, None] - la_cum[..., None, :], -jnp.inf))
    cb = jnp.einsum("bclgn,bcsgn->bcgls", cc, bc)
    y_diag = jnp.einsum("bcgrls,bcsgrp->bclgrp", cb[:, :, :, None] * lmat, xs)
    states = jnp.einsum("bclgn,bcgrl,bclgrp->bcgrpn", bc, jnp.exp(la_cum[..., -1:] - la_cum), xs)
    chunk_decay = jnp.exp(la_cum[..., -1])

    def step(hst, inp):
        st, dec = inp
        return hst * dec[..., None, None] + st, hst

    h0 = jnp.zeros((b, g, r, p, n), jnp.float32)
    _, prev = lax.scan(step, h0, (jnp.moveaxis(states, 1, 0), jnp.moveaxis(chunk_decay, 1, 0)))
    prev = jnp.moveaxis(prev, 0, 1)
    y_off = jnp.einsum("bclgn,bcgrpn,bcgrl->bclgrp", cc, prev, jnp.exp(la_cum))
    return (y_diag + y_off).reshape(b, t, h, p)


def gated_delta_chunked(q, k, v, g, beta):
    b, t, h, dk = q.shape
    dv = v.shape[-1]
    c = t // CHUNK

    def blk(u):
        return jnp.moveaxis(u.reshape(b, c, CHUNK, h, *u.shape[3:]), 3, 1)

    q, k, v, g, beta = blk(q), blk(k), blk(v), blk(g), blk(beta)
    g_cum = jnp.cumsum(g, axis=-1)
    idx = jnp.arange(CHUNK)
    incl = idx[:, None] >= idx[None, :]
    strict = idx[:, None] > idx[None, :]
    decay = jnp.exp(jnp.where(incl, g_cum[..., :, None] - g_cum[..., None, :], -jnp.inf))
    k_beta = k * beta[..., None]
    a_mat = jnp.where(strict, jnp.einsum("bhcld,bhcsd->bhcls", k_beta, k) * decay, 0.0)
    lhs = a_mat + jnp.eye(CHUNK, dtype=a_mat.dtype)
    rhs = jnp.concatenate([v * beta[..., None], k_beta * jnp.exp(g_cum)[..., None]], axis=-1)
    sol = lax.linalg.triangular_solve(lhs, rhs, left_side=True, lower=True, unit_diagonal=True)
    u_val, w_dec = sol[..., :dv], sol[..., dv:]
    attn = jnp.einsum("bhcld,bhcsd->bhcls", q, k) * decay
    q_dec = q * jnp.exp(g_cum)[..., None]
    k_dec = k * jnp.exp(g_cum[..., -1:] - g_cum)[..., None]
    g_last = jnp.exp(g_cum[..., -1])

    def step(s, inp):
        u_c, w_c, qd_c, kd_c, at_c, gl_c = inp
        v_new = u_c - jnp.einsum("bhld,bhde->bhle", w_c, s)
        o = jnp.einsum("bhld,bhde->bhle", qd_c, s) + jnp.einsum("bhls,bhse->bhle", at_c, v_new)
        s = s * gl_c[..., None, None] + jnp.einsum("bhld,bhle->bhde", kd_c, v_new)
        return s, o

    xs = tuple(jnp.moveaxis(u, 2, 0) for u in (u_val, w_dec, q_dec, k_dec, attn, g_last))
    s0 = jnp.zeros((b, h, dk, dv), jnp.float32)
    _, o = lax.scan(step, s0, xs)
    o = jnp.moveaxis(jnp.moveaxis(o, 0, 2), 1, 3)
    return o.reshape(b, t, h, dv)


def hybrid_layer(hid, norm_w, w_in, ssd_conv_w, ssd_conv_b, ssd_dt_bias, ssd_a_log, ssd_d,
                 ssd_norm_w, gdn_conv_w, gdn_dt_bias, gdn_a_log, gdn_norm_w, w_out):
    b, t, _ = hid.shape
    u = rmsnorm(hid, norm_w).astype(hid.dtype)
    proj = u @ w_in
    splits = np.cumsum(IN_SIZES)[:-1].tolist()
    z, xbc, dt_raw, gate, qkv, a_raw, b_raw = jnp.split(proj, splits, axis=-1)

    xbc = jax.nn.silu(causal_dwconv(xbc, ssd_conv_w) + ssd_conv_b).astype(jnp.float32)
    xs, bm, cm = jnp.split(xbc, [SSD_WIDTH, SSD_WIDTH + SSD_GROUPS * SSD_STATE], axis=-1)
    xs = xs.reshape(b, t, SSD_HEADS, SSD_HEAD_DIM)
    bm = bm.reshape(b, t, SSD_GROUPS, SSD_STATE)
    cm = cm.reshape(b, t, SSD_GROUPS, SSD_STATE)
    dt = jax.nn.softplus(dt_raw.astype(jnp.float32) + ssd_dt_bias.astype(jnp.float32))
    a_neg = -jnp.exp(ssd_a_log.astype(jnp.float32))
    y = ssd_chunked(xs, dt, a_neg, bm, cm) + ssd_d.astype(jnp.float32)[:, None] * xs
    yg = y.reshape(b, t, SSD_GROUPS, SSD_WIDTH // SSD_GROUPS) * jax.nn.silu(
        z.astype(jnp.float32).reshape(b, t, SSD_GROUPS, SSD_WIDTH // SSD_GROUPS))
    yg = yg * lax.rsqrt(jnp.mean(yg * yg, axis=-1, keepdims=True) + EPS)
    y_ssd = yg.reshape(b, t, SSD_WIDTH) * ssd_norm_w.astype(jnp.float32)

    qkv = jax.nn.silu(causal_dwconv(qkv, gdn_conv_w)).astype(jnp.float32)
    q, k, v = jnp.split(qkv, [GDN_KW, 2 * GDN_KW], axis=-1)
    q = l2norm(q.reshape(b, t, GDN_HEADS, GDN_DK)) * (GDN_DK ** -0.5)
    k = l2norm(k.reshape(b, t, GDN_HEADS, GDN_DK))
    v = v.reshape(b, t, GDN_HEADS, GDN_DV)
    beta = jax.nn.sigmoid(b_raw.astype(jnp.float32))
    g = -jnp.exp(gdn_a_log.astype(jnp.float32)) * jax.nn.softplus(
        a_raw.astype(jnp.float32) + gdn_dt_bias.astype(jnp.float32))
    o = gated_delta_chunked(q, k, v, g, beta)
    o = rmsnorm(o, gdn_norm_w) * jax.nn.silu(gate.astype(jnp.float32).reshape(b, t, GDN_HEADS, GDN_DV))
    y_gdn = o.reshape(b, t, GDN_VW)

    mix = jnp.concatenate([y_ssd, y_gdn], axis=-1).astype(hid.dtype)
    return hid + mix @ w_out


def _inv_softplus_dt(key, shape):
    lo, hi = math.log(1e-3), math.log(1e-1)
    dt = jnp.exp(jax.random.uniform(key, shape, jnp.float32) * (hi - lo) + lo)
    dt = jnp.maximum(dt, 1e-4)
    return dt + jnp.log(-jnp.expm1(-dt))


def _fwd_setup_inputs(seed: int = 0) -> dict:
    key = jax.random.key(seed)
    ks = jax.random.split(key, 16)
    f32 = jnp.float32
    L = DEPTH
    x = jax.random.normal(ks[0], (BATCH, SEQ, D_MODEL), f32)
    norm_w = 1.0 + 0.02 * jax.random.normal(ks[1], (L, D_MODEL), f32)
    w_in = jax.random.normal(ks[2], (L, D_MODEL, IN_DIM), f32) * D_MODEL ** -0.5
    ssd_conv_w = jax.random.normal(ks[3], (L, CONV_K, SSD_CONV_DIM), f32) * CONV_K ** -0.5
    ssd_conv_b = 0.02 * jax.random.normal(ks[4], (L, SSD_CONV_DIM), f32)
    ssd_dt_bias = _inv_softplus_dt(ks[5], (L, SSD_HEADS))
    ssd_a_log = jnp.log(jax.random.uniform(ks[6], (L, SSD_HEADS), f32, 1.0, 16.0))
    ssd_d = 1.0 + 0.02 * jax.random.normal(ks[7], (L, SSD_HEADS), f32)
    ssd_norm_w = 1.0 + 0.02 * jax.random.normal(ks[8], (L, SSD_WIDTH), f32)
    gdn_conv_w = jax.random.normal(ks[9], (L, CONV_K, GDN_CONV_DIM), f32) * CONV_K ** -0.5
    gdn_dt_bias = _inv_softplus_dt(ks[10], (L, GDN_HEADS))
    gdn_a_log = jnp.log(jax.random.uniform(ks[11], (L, GDN_HEADS), f32, 1.0, 16.0))
    gdn_norm_w = 1.0 + 0.02 * jax.random.normal(ks[12], (L, GDN_DV), f32)
    w_out = jax.random.normal(ks[13], (L, MIX_WIDTH, D_MODEL), f32) * MIX_WIDTH ** -0.5
    final_norm_w = 1.0 + 0.02 * jax.random.normal(ks[14], (D_MODEL,), f32)
    return {"x": x, "norm_w": norm_w, "w_in": w_in, "ssd_conv_w": ssd_conv_w,
            "ssd_conv_b": ssd_conv_b, "ssd_dt_bias": ssd_dt_bias, "ssd_a_log": ssd_a_log,
            "ssd_d": ssd_d, "ssd_norm_w": ssd_norm_w, "gdn_conv_w": gdn_conv_w,
            "gdn_dt_bias": gdn_dt_bias, "gdn_a_log": gdn_a_log, "gdn_norm_w": gdn_norm_w,
            "w_out": w_out, "final_norm_w": final_norm_w}


def _fwd_reference(x, norm_w, w_in, ssd_conv_w, ssd_conv_b, ssd_dt_bias, ssd_a_log, ssd_d, ssd_norm_w,
              gdn_conv_w, gdn_dt_bias, gdn_a_log, gdn_norm_w, w_out, final_norm_w):
    hid = x
    for i in range(DEPTH):
        hid = hybrid_layer(hid, norm_w[i], w_in[i], ssd_conv_w[i], ssd_conv_b[i], ssd_dt_bias[i],
                           ssd_a_log[i], ssd_d[i], ssd_norm_w[i], gdn_conv_w[i], gdn_dt_bias[i],
                           gdn_a_log[i], gdn_norm_w[i], w_out[i])
    return rmsnorm(hid, final_norm_w).astype(x.dtype)


import jax as _jax
import jax.numpy as _jnp

TWIN_FORMAT = 'train_step'
FWD_PARAMS = ['x', 'norm_w', 'w_in', 'ssd_conv_w', 'ssd_conv_b', 'ssd_dt_bias', 'ssd_a_log', 'ssd_d', 'ssd_norm_w', 'gdn_conv_w', 'gdn_dt_bias', 'gdn_a_log', 'gdn_norm_w', 'w_out', 'final_norm_w']
TWIN_WEIGHTS = ['norm_w', 'w_in', 'ssd_conv_w', 'ssd_conv_b', 'ssd_dt_bias', 'ssd_a_log', 'ssd_d', 'ssd_norm_w', 'gdn_conv_w', 'gdn_dt_bias', 'gdn_a_log', 'gdn_norm_w', 'w_out', 'final_norm_w']
TWIN_DIFF_INPUT = 'x'
TWIN_INPUTS = ['x', 'norm_w', 'w_in', 'ssd_conv_w', 'ssd_conv_b', 'ssd_dt_bias', 'ssd_a_log', 'ssd_d', 'ssd_norm_w', 'gdn_conv_w', 'gdn_dt_bias', 'gdn_a_log', 'gdn_norm_w', 'w_out', 'final_norm_w', 'loss_target', 'm_norm_w', 'm_w_in', 'm_ssd_conv_w', 'm_ssd_conv_b', 'm_ssd_dt_bias', 'm_ssd_a_log', 'm_ssd_d', 'm_ssd_norm_w', 'm_gdn_conv_w', 'm_gdn_dt_bias', 'm_gdn_a_log', 'm_gdn_norm_w', 'm_w_out', 'm_final_norm_w', 'v_norm_w', 'v_w_in', 'v_ssd_conv_w', 'v_ssd_conv_b', 'v_ssd_dt_bias', 'v_ssd_a_log', 'v_ssd_d', 'v_ssd_norm_w', 'v_gdn_conv_w', 'v_gdn_dt_bias', 'v_gdn_a_log', 'v_gdn_norm_w', 'v_w_out', 'v_final_norm_w']
TWIN_OUTPUTS = ['loss', 'grad_x', 'grad_norm_w', 'grad_w_in', 'grad_ssd_conv_w', 'grad_ssd_conv_b', 'grad_ssd_dt_bias', 'grad_ssd_a_log', 'grad_ssd_d', 'grad_ssd_norm_w', 'grad_gdn_conv_w', 'grad_gdn_dt_bias', 'grad_gdn_a_log', 'grad_gdn_norm_w', 'grad_w_out', 'grad_final_norm_w', 'delta_norm_w', 'delta_w_in', 'delta_ssd_conv_w', 'delta_ssd_conv_b', 'delta_ssd_dt_bias', 'delta_ssd_a_log', 'delta_ssd_d', 'delta_ssd_norm_w', 'delta_gdn_conv_w', 'delta_gdn_dt_bias', 'delta_gdn_a_log', 'delta_gdn_norm_w', 'delta_w_out', 'delta_final_norm_w', 'new_m_norm_w', 'new_m_w_in', 'new_m_ssd_conv_w', 'new_m_ssd_conv_b', 'new_m_ssd_dt_bias', 'new_m_ssd_a_log', 'new_m_ssd_d', 'new_m_ssd_norm_w', 'new_m_gdn_conv_w', 'new_m_gdn_dt_bias', 'new_m_gdn_a_log', 'new_m_gdn_norm_w', 'new_m_w_out', 'new_m_final_norm_w', 'new_v_norm_w', 'new_v_w_in', 'new_v_ssd_conv_w', 'new_v_ssd_conv_b', 'new_v_ssd_dt_bias', 'new_v_ssd_a_log', 'new_v_ssd_d', 'new_v_ssd_norm_w', 'new_v_gdn_conv_w', 'new_v_gdn_dt_bias', 'new_v_gdn_a_log', 'new_v_gdn_norm_w', 'new_v_w_out', 'new_v_final_norm_w']
TWIN_LEAF_KINDS = {'loss': 'loss', 'grad_x': 'grad_x', 'grad_norm_w': 'grad_w', 'grad_w_in': 'grad_w', 'grad_ssd_conv_w': 'grad_w', 'grad_ssd_conv_b': 'grad_w', 'grad_ssd_dt_bias': 'grad_w', 'grad_ssd_a_log': 'grad_w', 'grad_ssd_d': 'grad_w', 'grad_ssd_norm_w': 'grad_w', 'grad_gdn_conv_w': 'grad_w', 'grad_gdn_dt_bias': 'grad_w', 'grad_gdn_a_log': 'grad_w', 'grad_gdn_norm_w': 'grad_w', 'grad_w_out': 'grad_w', 'grad_final_norm_w': 'grad_w', 'delta_norm_w': 'delta_w', 'delta_w_in': 'delta_w', 'delta_ssd_conv_w': 'delta_w', 'delta_ssd_conv_b': 'delta_w', 'delta_ssd_dt_bias': 'delta_w', 'delta_ssd_a_log': 'delta_w', 'delta_ssd_d': 'delta_w', 'delta_ssd_norm_w': 'delta_w', 'delta_gdn_conv_w': 'delta_w', 'delta_gdn_dt_bias': 'delta_w', 'delta_gdn_a_log': 'delta_w', 'delta_gdn_norm_w': 'delta_w', 'delta_w_out': 'delta_w', 'delta_final_norm_w': 'delta_w', 'new_m_norm_w': 'new_m', 'new_m_w_in': 'new_m', 'new_m_ssd_conv_w': 'new_m', 'new_m_ssd_conv_b': 'new_m', 'new_m_ssd_dt_bias': 'new_m', 'new_m_ssd_a_log': 'new_m', 'new_m_ssd_d': 'new_m', 'new_m_ssd_norm_w': 'new_m', 'new_m_gdn_conv_w': 'new_m', 'new_m_gdn_dt_bias': 'new_m', 'new_m_gdn_a_log': 'new_m', 'new_m_gdn_norm_w': 'new_m', 'new_m_w_out': 'new_m', 'new_m_final_norm_w': 'new_m', 'new_v_norm_w': 'new_v', 'new_v_w_in': 'new_v', 'new_v_ssd_conv_w': 'new_v', 'new_v_ssd_conv_b': 'new_v', 'new_v_ssd_dt_bias': 'new_v', 'new_v_ssd_a_log': 'new_v', 'new_v_ssd_d': 'new_v', 'new_v_ssd_norm_w': 'new_v', 'new_v_gdn_conv_w': 'new_v', 'new_v_gdn_dt_bias': 'new_v', 'new_v_gdn_a_log': 'new_v', 'new_v_gdn_norm_w': 'new_v', 'new_v_w_out': 'new_v', 'new_v_final_norm_w': 'new_v'}


def _forward(args):
    return _fwd_reference(*[args[k] for k in FWD_PARAMS])


def _output_shape():
    def fwd():
        inp = _fwd_setup_inputs(0)
        return _fwd_reference(*[inp[k] for k in FWD_PARAMS])
    out = _jax.eval_shape(fwd)
    return out.shape, out.dtype

N_MICROBATCH = 1
ADAM_LR = 0.001
ADAM_B1 = 0.9
ADAM_B2 = 0.999
ADAM_EPS = 1e-08
ADAM_WD = 0.01
ADAM_STEP = 10
PER_EXAMPLE_BATCH_AXIS = {'x': 0, 'loss_target': 0}
SHARED_INPUTS = []
_WEIGHT_DTYPES = {'norm_w': _jnp.float32, 'w_in': _jnp.float32, 'ssd_conv_w': _jnp.float32, 'ssd_conv_b': _jnp.float32, 'ssd_dt_bias': _jnp.float32, 'ssd_a_log': _jnp.float32, 'ssd_d': _jnp.float32, 'ssd_norm_w': _jnp.float32, 'gdn_conv_w': _jnp.float32, 'gdn_dt_bias': _jnp.float32, 'gdn_a_log': _jnp.float32, 'gdn_norm_w': _jnp.float32, 'w_out': _jnp.float32, 'final_norm_w': _jnp.float32}
MOMENT_SCALE = {'norm_w': 2.540162e-01, 'w_in': 9.761329e-02, 'ssd_conv_w': 1.229862e-01, 'ssd_conv_b': 1.534474e-01, 'ssd_dt_bias': 3.521406e-01, 'ssd_a_log': 4.290028e-01, 'ssd_d': 7.356955e-01, 'ssd_norm_w': 1.381076e-01, 'gdn_conv_w': 6.271213e-02, 'gdn_dt_bias': 3.518254e-01, 'gdn_a_log': 3.726198e-01, 'gdn_norm_w': 2.255879e-01, 'w_out': 1.613314e-01, 'final_norm_w': 6.402238e+01}


def _to_microbatches(a, axis):
    t = _jnp.moveaxis(a, axis, 0)
    t = t.reshape((N_MICROBATCH, t.shape[0] // N_MICROBATCH) + t.shape[1:])
    return _jnp.moveaxis(t, 1, axis + 1)


def setup_inputs(seed: int = 0) -> dict:
    inp = _fwd_setup_inputs(seed)
    key = _jax.random.fold_in(_jax.random.key(seed), 7919)
    shape, _ = _output_shape()
    out = dict(inp)
    out["loss_target"] = _jax.random.normal(_jax.random.fold_in(key, 0), shape, _jnp.float32)
    for i, name in enumerate(TWIN_WEIGHTS):
        w = inp[name].astype(_jnp.float32)
        if MOMENT_SCALE is None:
            s = _jnp.sqrt(_jnp.mean(_jnp.square(w)) + 1e-30)
        else:
            s = MOMENT_SCALE[name]
        km, kv = _jax.random.split(_jax.random.fold_in(key, i + 1))
        out[name] = w
        out["m_" + name] = s * _jax.random.normal(km, w.shape, _jnp.float32)
        out["v_" + name] = (s * s) * _jax.random.uniform(kv, w.shape, _jnp.float32, 0.5, 1.5)
    if N_MICROBATCH > 1:
        for name, axis in PER_EXAMPLE_BATCH_AXIS.items():
            out[name] = _to_microbatches(out[name], axis)
    return {'x': out['x'], 'norm_w': out['norm_w'], 'w_in': out['w_in'], 'ssd_conv_w': out['ssd_conv_w'], 'ssd_conv_b': out['ssd_conv_b'], 'ssd_dt_bias': out['ssd_dt_bias'], 'ssd_a_log': out['ssd_a_log'], 'ssd_d': out['ssd_d'], 'ssd_norm_w': out['ssd_norm_w'], 'gdn_conv_w': out['gdn_conv_w'], 'gdn_dt_bias': out['gdn_dt_bias'], 'gdn_a_log': out['gdn_a_log'], 'gdn_norm_w': out['gdn_norm_w'], 'w_out': out['w_out'], 'final_norm_w': out['final_norm_w'], 'loss_target': out['loss_target'], 'm_norm_w': out['m_norm_w'], 'm_w_in': out['m_w_in'], 'm_ssd_conv_w': out['m_ssd_conv_w'], 'm_ssd_conv_b': out['m_ssd_conv_b'], 'm_ssd_dt_bias': out['m_ssd_dt_bias'], 'm_ssd_a_log': out['m_ssd_a_log'], 'm_ssd_d': out['m_ssd_d'], 'm_ssd_norm_w': out['m_ssd_norm_w'], 'm_gdn_conv_w': out['m_gdn_conv_w'], 'm_gdn_dt_bias': out['m_gdn_dt_bias'], 'm_gdn_a_log': out['m_gdn_a_log'], 'm_gdn_norm_w': out['m_gdn_norm_w'], 'm_w_out': out['m_w_out'], 'm_final_norm_w': out['m_final_norm_w'], 'v_norm_w': out['v_norm_w'], 'v_w_in': out['v_w_in'], 'v_ssd_conv_w': out['v_ssd_conv_w'], 'v_ssd_conv_b': out['v_ssd_conv_b'], 'v_ssd_dt_bias': out['v_ssd_dt_bias'], 'v_ssd_a_log': out['v_ssd_a_log'], 'v_ssd_d': out['v_ssd_d'], 'v_ssd_norm_w': out['v_ssd_norm_w'], 'v_gdn_conv_w': out['v_gdn_conv_w'], 'v_gdn_dt_bias': out['v_gdn_dt_bias'], 'v_gdn_a_log': out['v_gdn_a_log'], 'v_gdn_norm_w': out['v_gdn_norm_w'], 'v_w_out': out['v_w_out'], 'v_final_norm_w': out['v_final_norm_w']}


def _loss(weights, diff, rest, loss_target):
    with _jax.named_scope("forward"):
        args = {**rest, TWIN_DIFF_INPUT: diff, **{k: w.astype(_WEIGHT_DTYPES[k]) for k, w in weights.items()}}
        y = _forward(args)
    with _jax.named_scope("loss_head"):
        err = _jnp.square(y.astype(_jnp.float32) - loss_target)
        return 0.5 * _jnp.sum(_jnp.mean(err, axis=-1)) if err.ndim else 0.5 * err


def _adamw(w, g, m, v):
    m = ADAM_B1 * m + (1.0 - ADAM_B1) * g
    v = ADAM_B2 * v + (1.0 - ADAM_B2) * _jnp.square(g)
    m_hat = m / (1.0 - ADAM_B1 ** ADAM_STEP)
    v_hat = v / (1.0 - ADAM_B2 ** ADAM_STEP)
    delta = -ADAM_LR * (m_hat / (_jnp.sqrt(v_hat) + ADAM_EPS) + ADAM_WD * w)
    return delta, m, v


def reference(x, norm_w, w_in, ssd_conv_w, ssd_conv_b, ssd_dt_bias, ssd_a_log, ssd_d, ssd_norm_w, gdn_conv_w, gdn_dt_bias, gdn_a_log, gdn_norm_w, w_out, final_norm_w, loss_target, m_norm_w, m_w_in, m_ssd_conv_w, m_ssd_conv_b, m_ssd_dt_bias, m_ssd_a_log, m_ssd_d, m_ssd_norm_w, m_gdn_conv_w, m_gdn_dt_bias, m_gdn_a_log, m_gdn_norm_w, m_w_out, m_final_norm_w, v_norm_w, v_w_in, v_ssd_conv_w, v_ssd_conv_b, v_ssd_dt_bias, v_ssd_a_log, v_ssd_d, v_ssd_norm_w, v_gdn_conv_w, v_gdn_dt_bias, v_gdn_a_log, v_gdn_norm_w, v_w_out, v_final_norm_w):
    given = dict(x=x, norm_w=norm_w, w_in=w_in, ssd_conv_w=ssd_conv_w, ssd_conv_b=ssd_conv_b, ssd_dt_bias=ssd_dt_bias, ssd_a_log=ssd_a_log, ssd_d=ssd_d, ssd_norm_w=ssd_norm_w, gdn_conv_w=gdn_conv_w, gdn_dt_bias=gdn_dt_bias, gdn_a_log=gdn_a_log, gdn_norm_w=gdn_norm_w, w_out=w_out, final_norm_w=final_norm_w, loss_target=loss_target, m_norm_w=m_norm_w, m_w_in=m_w_in, m_ssd_conv_w=m_ssd_conv_w, m_ssd_conv_b=m_ssd_conv_b, m_ssd_dt_bias=m_ssd_dt_bias, m_ssd_a_log=m_ssd_a_log, m_ssd_d=m_ssd_d, m_ssd_norm_w=m_ssd_norm_w, m_gdn_conv_w=m_gdn_conv_w, m_gdn_dt_bias=m_gdn_dt_bias, m_gdn_a_log=m_gdn_a_log, m_gdn_norm_w=m_gdn_norm_w, m_w_out=m_w_out, m_final_norm_w=m_final_norm_w, v_norm_w=v_norm_w, v_w_in=v_w_in, v_ssd_conv_w=v_ssd_conv_w, v_ssd_conv_b=v_ssd_conv_b, v_ssd_dt_bias=v_ssd_dt_bias, v_ssd_a_log=v_ssd_a_log, v_ssd_d=v_ssd_d, v_ssd_norm_w=v_ssd_norm_w, v_gdn_conv_w=v_gdn_conv_w, v_gdn_dt_bias=v_gdn_dt_bias, v_gdn_a_log=v_gdn_a_log, v_gdn_norm_w=v_gdn_norm_w, v_w_out=v_w_out, v_final_norm_w=v_final_norm_w)
    weights = {n: given[n] for n in TWIN_WEIGHTS}
    shared = {n: given[n] for n in SHARED_INPUTS}
    per_example = {n: given[n] for n in ['x']}
    grad_fn = _jax.value_and_grad(_loss, argnums=(0, 1))

    def one_microbatch(ex, loss_target):
        ex = dict(ex)
        diff = ex.pop(TWIN_DIFF_INPUT)
        return grad_fn(weights, diff, {**shared, **ex}, loss_target)

    if N_MICROBATCH == 1:
        loss, (grad_w, grad_x) = one_microbatch(per_example, given["loss_target"])
    else:
        def body(carry, xs):
            loss_sum, grad_sum = carry
            l_k, (gw_k, gx_k) = one_microbatch(xs[0], xs[1])
            with _jax.named_scope("update"):
                return (loss_sum + l_k, _jax.tree.map(_jnp.add, grad_sum, gw_k)), gx_k

        init = (_jnp.zeros((), _jnp.float32), _jax.tree.map(_jnp.zeros_like, weights))
        (loss, grad_w), grad_x = _jax.lax.scan(body, init, (per_example, given["loss_target"]))
    with _jax.named_scope("update"):
        delta_w, new_m, new_v = {}, {}, {}
        for n in TWIN_WEIGHTS:
            delta_w[n], new_m[n], new_v[n] = _adamw(weights[n], grad_w[n], given["m_" + n], given["v_" + n])
    return (loss, grad_x, *[grad_w[n] for n in TWIN_WEIGHTS], *[delta_w[n] for n in TWIN_WEIGHTS],
            *[new_m[n] for n in TWIN_WEIGHTS], *[new_v[n] for n in TWIN_WEIGHTS])
```

```python
import functools

import numpy as np
import jax
import jax.numpy as jnp
from jax import lax
from jax.experimental import pallas as pl
from jax.experimental.pallas import tpu as pltpu

F32 = jnp.float32
MXU_DTYPE = jnp.bfloat16
MESH = pl.DeviceIdType.MESH

D_MODEL = 1024
CHUNK = 64
EPS = 1e-6
SSD_HEADS, SSD_GROUPS, SSD_STATE = 16, 2, 128
SSD_WIDTH, SSD_CONV = 1024, 1536
SSD_GW = SSD_WIDTH // SSD_GROUPS
SSD_GC = SSD_GW + 2 * SSD_STATE
GDN_HEADS, GDN_DK, GDN_DV = 8, 128, 128
GDN_W, GDN_CONV = 1024, 3072
GDN_HC = 2 * GDN_DK + GDN_DV
IN_DIM = 6688
MAIN = 6656
LANES = 128
COL_Z, COL_SSD, COL_GATE, COL_GDN = 0, 1024, 2560, 3584
LANE_GA, LANE_GB = 16, 24
N_DEV, N_CHIP = 8, 4
VMEM_LIMIT = 52 * 1024 * 1024

ADAM_LR, ADAM_B1, ADAM_B2, ADAM_EPS, ADAM_WD, ADAM_STEP = 0.001, 0.9, 0.999, 1e-08, 0.01, 10


def _ssd_conv_segments():
    segs = []
    for g in range(SSD_GROUPS):
        segs.append((g * SSD_GW, SSD_GW))
        segs.append((SSD_WIDTH + g * SSD_STATE, SSD_STATE))
        segs.append((SSD_WIDTH + SSD_GROUPS * SSD_STATE + g * SSD_STATE, SSD_STATE))
    return segs


def _gdn_conv_segments():
    segs = []
    for h in range(GDN_HEADS):
        segs.append((h * GDN_DK, GDN_DK))
        segs.append((GDN_W + h * GDN_DK, GDN_DK))
        segs.append((2 * GDN_W + h * GDN_DV, GDN_DV))
    return segs


def _main_segments():
    o_xbc, o_gate, o_qkv = 1024, 2576, 3600
    segs = [(0, 1024)]
    segs += [(o_xbc + s, w) for s, w in _ssd_conv_segments()]
    segs += [(o_gate, 1024)]
    segs += [(o_qkv + s, w) for s, w in _gdn_conv_segments()]
    return segs


def _take(a, segs):
    return jnp.concatenate([a[..., s:s + w] for s, w in segs], axis=-1)


def _untake(a, segs):
    pos, placed = 0, []
    for s, w in segs:
        placed.append((s, pos, w))
        pos += w
    placed.sort()
    return jnp.concatenate([a[..., p:p + w] for _, p, w in placed], axis=-1)


def _raw_dot(a, b, ca, cb, exact):
    if exact:
        return lax.dot_general(a.astype(F32), b.astype(F32), (((ca,), (cb,)), ((), ())),
                               precision=lax.Precision.HIGHEST, preferred_element_type=F32)
    return lax.dot_general(a.astype(MXU_DTYPE), b.astype(MXU_DTYPE), (((ca,), (cb,)), ((), ())),
                           preferred_element_type=F32)


@functools.partial(jax.custom_vjp, nondiff_argnums=(2,))
def mm_nn(a, b, exact=False):
    return _raw_dot(a, b, 1, 0, exact)


@functools.partial(jax.custom_vjp, nondiff_argnums=(2,))
def mm_nt(a, b, exact=False):
    return _raw_dot(a, b, 1, 1, exact)


@functools.partial(jax.custom_vjp, nondiff_argnums=(2,))
def mm_tn(a, b, exact=False):
    return _raw_dot(a, b, 0, 0, exact)


mm_nn.defvjp(lambda a, b, e: (_raw_dot(a, b, 1, 0, e), (a, b)),
             lambda e, r, g: (mm_nt(g, r[1], e), mm_tn(r[0], g, e)))
mm_nt.defvjp(lambda a, b, e: (_raw_dot(a, b, 1, 1, e), (a, b)),
             lambda e, r, g: (mm_nn(g, r[1], e), mm_tn(g, r[0], e)))
mm_tn.defvjp(lambda a, b, e: (_raw_dot(a, b, 0, 0, e), (a, b)),
             lambda e, r, g: (mm_nt(r[1], g, e), mm_nn(r[0], g, e)))


@jax.custom_jvp
def sigmoid(x):
    e = jnp.exp(-jnp.abs(x))
    return jnp.where(x >= 0, 1.0, e) / (1.0 + e)


@sigmoid.defjvp
def _sigmoid_jvp(p, t):
    s = sigmoid(p[0])
    return s, t[0] * s * (1.0 - s)


@jax.custom_jvp
def softplus(x):
    return jnp.maximum(x, 0.0) + jnp.log(1.0 + jnp.exp(-jnp.abs(x)))


@softplus.defjvp
def _softplus_jvp(p, t):
    return softplus(p[0]), t[0] * sigmoid(p[0])


def silu(x):
    return x * sigmoid(x)


def rmsnorm(x, w):
    return x * lax.rsqrt(jnp.mean(x * x, axis=-1, keepdims=True) + EPS) * w


def _iota(shape, dim):
    return lax.broadcasted_iota(jnp.int32, shape, dim)


def _tri_inv_impl(a):
    n = a.shape[0]
    r, c = _iota((n, n), 0), _iota((n, n), 1)
    eye = jnp.where(r == c, 1.0, 0.0).astype(F32)
    blockdiag = (r >> 4) == (c >> 4)
    dg = jnp.where(blockdiag, a, 0.0)
    off = a - dg
    dot = lambda u, v: _raw_dot(u, v, 1, 0, True)
    m = -dg
    p = eye + m
    m2 = dot(m, m)
    p = p + dot(p, m2)
    m4 = dot(m2, m2)
    p = p + dot(p, m4)
    m8 = dot(m4, m4)
    p = p + dot(p, m8)
    e = dot(p, off)
    e2 = dot(e, e)
    q = eye - e
    q = q + dot(q, e2)
    return dot(q, p)


@jax.custom_vjp
def tri_inv(a):
    return _tri_inv_impl(a)


def _tri_inv_fwd(a):
    t = _tri_inv_impl(a)
    return t, t


def _tri_inv_bwd(t, g):
    return (-mm_tn(t, mm_nt(g, t, True), True),)


tri_inv.defvjp(_tri_inv_fwd, _tri_inv_bwd)


def _chunk_masks():
    r, c = _iota((CHUNK, CHUNK), 0), _iota((CHUNK, CHUNK), 1)
    return r >= c, r > c, r == c


def _log_decay_cumsum(small, alog, dtb, incl):
    sp = softplus(small + dtb)
    la = -jnp.exp(alog) * sp
    tri = jnp.where(incl, 1.0, 0.0).astype(F32)
    return sp, mm_nn(tri, la, True)


def _col_of(x, lane_mask):
    return jnp.sum(jnp.where(lane_mask, x, 0.0), axis=1, keepdims=True)


def _decay_matrix(col, incl, eye):
    row = jnp.sum(jnp.where(eye, col, 0.0), axis=0, keepdims=True)
    return jnp.where(incl, jnp.exp(jnp.where(incl, col - row, 0.0)), 0.0)


def gdn_chunk(h, qr, kr, v, small, gate, normw, alog, dtb, state):
    incl, strict, eye = _chunk_masks()
    lane = _iota((1, LANES), 1)
    last = _iota((CHUNK, 1), 0) == CHUNK - 1
    _, lac = _log_decay_cumsum(small, alog, dtb, incl)
    gc = _col_of(lac, lane == LANE_GA + h)
    beta = sigmoid(_col_of(small, lane == LANE_GB + h))
    decay = _decay_matrix(gc, incl, eye)
    gl = jnp.sum(jnp.where(last, gc, 0.0), axis=0, keepdims=True)
    q = qr * lax.rsqrt(jnp.sum(qr * qr, axis=-1, keepdims=True) + EPS) * (GDN_DK ** -0.5)
    k = kr * lax.rsqrt(jnp.sum(kr * kr, axis=-1, keepdims=True) + EPS)
    kb = k * beta
    a = jnp.where(strict, mm_nt(kb, k) * decay, 0.0)
    t = tri_inv(a)
    eg = jnp.exp(gc)
    u = mm_nn(t, v * beta, True)
    w = mm_nn(t, kb * eg, True)
    attn = mm_nt(q, k) * decay
    v_new = u - mm_nn(w, state)
    o = mm_nn(q * eg, state) + mm_nn(attn, v_new)
    new_state = state * jnp.exp(gl) + mm_tn(k * jnp.exp(gl - gc), v_new)
    return rmsnorm(o, normw) * silu(gate), new_state


def ssd_chunk(g, xs, bm, cm, z, small, normw, alog, dtb, dvec, state):
    incl, _, eye = _chunk_masks()
    lane = _iota((1, LANES), 1)
    last = _iota((CHUNK, 1), 0) == CHUNK - 1
    hpg = SSD_HEADS // SSD_GROUPS
    sp, lac = _log_decay_cumsum(small, alog, dtb, incl)
    sel = jnp.where(_iota((LANES, SSD_GW), 0) == g * hpg + (_iota((LANES, SSD_GW), 1) >> 6), 1.0, 0.0).astype(F32)
    lac_last = jnp.sum(jnp.where(last, lac, 0.0), axis=0, keepdims=True)
    dt_e = mm_nn(sp, sel, True)
    elac_e = mm_nn(jnp.exp(lac), sel, True)
    toend_e = mm_nn(jnp.exp(lac_last - lac), sel, True)
    d_e = jnp.sum(mm_nn(jnp.broadcast_to(dvec, (8, LANES)), sel, True), axis=0, keepdims=True) * 0.125
    eye_l = _iota((LANES, LANES), 0) == _iota((LANES, LANES), 1)
    el_col = jnp.sum(jnp.where(eye_l, jnp.exp(lac_last), 0.0), axis=1, keepdims=True)
    row_scale = mm_tn(sel, jnp.broadcast_to(el_col, (LANES, LANES)), True)
    xdt = xs * dt_e
    cb = mm_nt(cm, bm)
    y = mm_nt(cm, state) * elac_e + xs * d_e
    col_head = _iota((1, SSD_GW), 1) >> 6
    for j in range(hpg):
        lm = _decay_matrix(_col_of(lac, lane == g * hpg + j), incl, eye)
        y = y + mm_nn(cb * lm, jnp.where(col_head == j, xdt, 0.0))
    new_state = state * row_scale + mm_tn(xdt * toend_e, bm)
    yg = y * silu(z)
    return rmsnorm(yg, normw), new_state


def _params(sem=None):
    return pltpu.CompilerParams(dimension_semantics=sem, vmem_limit_bytes=VMEM_LIMIT)


def _full(shape):
    n = len(shape)
    return pl.BlockSpec(shape, lambda *_: (0,) * n)


ANY = pl.BlockSpec(memory_space=pl.ANY)
HBM = pl.BlockSpec(memory_space=pltpu.HBM)


def in_proj(x, normw, w_main, w_small):
    t = x.shape[0]
    tm, tn = min(512, t), 512

    def body(x_ref, nw_ref, wm_ref, ws_ref, pm_ref, ps_ref, u_ref):
        @pl.when(pl.program_id(1) == 0)
        def _():
            u = rmsnorm(x_ref[...], nw_ref[...]).astype(MXU_DTYPE)
            u_ref[...] = u
            ps_ref[...] = _raw_dot(u, ws_ref[...], 1, 0, False)
        pm_ref[...] = _raw_dot(u_ref[...], wm_ref[...], 1, 0, False)

    return pl.pallas_call(
        body, name="in_proj", grid=(t // tm, MAIN // tn),
        in_specs=[pl.BlockSpec((tm, D_MODEL), lambda i, j: (i, 0)), _full((1, D_MODEL)),
                  pl.BlockSpec((D_MODEL, tn), lambda i, j: (0, j)), _full((D_MODEL, LANES))],
        out_specs=[pl.BlockSpec((tm, tn), lambda i, j: (i, j)), pl.BlockSpec((tm, LANES), lambda i, j: (i, 0)),
                   pl.BlockSpec((tm, D_MODEL), lambda i, j: (i, 0))],
        out_shape=[jax.ShapeDtypeStruct((t, MAIN), F32), jax.ShapeDtypeStruct((t, LANES), F32),
                   jax.ShapeDtypeStruct((t, D_MODEL), MXU_DTYPE)],
        compiler_params=_params(("arbitrary", "arbitrary")),
    )(x, normw, w_main, w_small)


CONV_TC = 512
HALO = 8


def _shift_down(cur, prev, s):
    rolled = pltpu.roll(cur, s, 0)
    top = jnp.where(_iota((HALO, cur.shape[1]), 0) < s, pltpu.roll(prev, s, 0), rolled[:HALO])
    return jnp.concatenate([top, rolled[HALO:]], axis=0)


def _shift_up(cur, nxt, s):
    n = cur.shape[0]
    rolled = pltpu.roll(cur, n - s, 0)
    bot = jnp.where(_iota((HALO, cur.shape[1]), 0) >= HALO - s, pltpu.roll(nxt, HALO - s, 0), rolled[n - HALO:])
    return jnp.concatenate([rolled[:n - HALO], bot], axis=0)


def _conv_pre(cur, prev, w_ref, b):
    acc = cur * w_ref[3:4, :] + b
    shifted = [cur]
    for s in (1, 2, 3):
        sh = _shift_down(cur, prev, s)
        shifted.append(sh)
        acc = acc + sh * w_ref[3 - s:4 - s, :]
    return acc, shifted


def conv_fwd(proj_main, col0, width, w, b, name):
    t = proj_main.shape[0]
    tt, c0 = min(512, t), col0 // CONV_TC

    def body(cur_ref, prev_ref, w_ref, b_ref, out_ref):
        prev = jnp.where(pl.program_id(0) > 0, prev_ref[...], 0.0)
        pre, _ = _conv_pre(cur_ref[...], prev, w_ref, b_ref[...])
        out_ref[...] = silu(pre)

    return pl.pallas_call(
        body, name=name, grid=(t // tt, width // CONV_TC),
        in_specs=[pl.BlockSpec((tt, CONV_TC), lambda i, j: (i, c0 + j)),
                  pl.BlockSpec((HALO, CONV_TC), lambda i, j: (jnp.maximum(i * (tt // HALO) - 1, 0), c0 + j)),
                  pl.BlockSpec((4, CONV_TC), lambda i, j: (0, j)), pl.BlockSpec((1, CONV_TC), lambda i, j: (0, j))],
        out_specs=pl.BlockSpec((tt, CONV_TC), lambda i, j: (i, j)),
        out_shape=jax.ShapeDtypeStruct((t, width), F32),
        compiler_params=_params(("arbitrary", "arbitrary")),
    )(proj_main, proj_main, w, b)


def conv_bwd_pre(proj_main, col0, width, w, b, dout, name):
    t = proj_main.shape[0]
    tt, c0 = min(512, t), col0 // CONV_TC

    def body(cur_ref, prev_ref, w_ref, b_ref, do_ref, dpre_ref, dwb_ref):
        i = pl.program_id(1)
        prev = jnp.where(i > 0, prev_ref[...], 0.0)
        pre, shifted = _conv_pre(cur_ref[...], prev, w_ref, b_ref[...])
        sg = sigmoid(pre)
        dpre = do_ref[...] * (sg * (1.0 + pre * (1.0 - sg)))
        dpre_ref[...] = dpre
        row = _iota((HALO, CONV_TC), 0)
        upd = jnp.where(row == 4, jnp.sum(dpre, axis=0, keepdims=True), 0.0)
        for s in range(4):
            upd = upd + jnp.where(row == 3 - s, jnp.sum(dpre * shifted[s], axis=0, keepdims=True), 0.0)

        @pl.when(i == 0)
        def _():
            dwb_ref[...] = upd

        @pl.when(i > 0)
        def _():
            dwb_ref[...] += upd

    return pl.pallas_call(
        body, name=name, grid=(width // CONV_TC, t // tt),
        in_specs=[pl.BlockSpec((tt, CONV_TC), lambda j, i: (i, c0 + j)),
                  pl.BlockSpec((HALO, CONV_TC), lambda j, i: (jnp.maximum(i * (tt // HALO) - 1, 0), c0 + j)),
                  pl.BlockSpec((4, CONV_TC), lambda j, i: (0, j)), pl.BlockSpec((1, CONV_TC), lambda j, i: (0, j)),
                  pl.BlockSpec((tt, CONV_TC), lambda j, i: (i, j))],
        out_specs=[pl.BlockSpec((tt, CONV_TC), lambda j, i: (i, j)), pl.BlockSpec((HALO, CONV_TC), lambda j, i: (0, j))],
        out_shape=[jax.ShapeDtypeStruct((t, width), F32), jax.ShapeDtypeStruct((HALO, width), F32)],
        compiler_params=_params(("arbitrary", "arbitrary")),
    )(proj_main, proj_main, w, b, dout)


def conv_bwd_x(dproj_main, col0, width, w, dpre, name):
    t = dpre.shape[0]
    tt, c0 = min(512, t), col0 // CONV_TC
    nt = t // tt

    def body(alias_ref, cur_ref, nxt_ref, w_ref, out_ref):
        del alias_ref
        nxt = jnp.where(pl.program_id(0) < nt - 1, nxt_ref[...], 0.0)
        cur = cur_ref[...]
        acc = cur * w_ref[3:4, :]
        for s in (1, 2, 3):
            acc = acc + _shift_up(cur, nxt, s) * w_ref[3 - s:4 - s, :]
        out_ref[...] = acc

    return pl.pallas_call(
        body, name=name, grid=(nt, width // CONV_TC),
        in_specs=[ANY, pl.BlockSpec((tt, CONV_TC), lambda i, j: (i, j)),
                  pl.BlockSpec((HALO, CONV_TC), lambda i, j: (jnp.minimum((i + 1) * (tt // HALO), t // HALO - 1), j)),
                  pl.BlockSpec((4, CONV_TC), lambda i, j: (0, j))],
        out_specs=pl.BlockSpec((tt, CONV_TC), lambda i, j: (i, c0 + j)),
        out_shape=jax.ShapeDtypeStruct(dproj_main.shape, F32),
        input_output_aliases={0: 0},
        compiler_params=_params(("arbitrary", "arbitrary")),
    )(dproj_main, dpre, dpre, w)


def _ssd_parts(xbc_ref):
    return xbc_ref[:, :SSD_GW], xbc_ref[:, SSD_GW:SSD_GW + SSD_STATE], xbc_ref[:, SSD_GW + SSD_STATE:]


def _gdn_parts(qkv_ref):
    return qkv_ref[:, :GDN_DK], qkv_ref[:, GDN_DK:2 * GDN_DK], qkv_ref[:, 2 * GDN_DK:]


def ssd_fwd(conv_ssd, proj_main, proj_small, normw, alog, dtb, dvec):
    t = conv_ssd.shape[0]
    nc = t // CHUNK

    def body(xbc_ref, z_ref, sm_ref, nw_ref, al_ref, db_ref, dv_ref, y_ref, hist_ref, state_ref):
        g = pl.program_id(1)

        @pl.when(pl.program_id(0) == 0)
        def _():
            state_ref[g] = jnp.zeros((SSD_GW, SSD_STATE), F32)

        state = state_ref[g]
        hist_ref[0, 0] = state
        y, new_state = ssd_chunk(g, *_ssd_parts(xbc_ref), z_ref[...], sm_ref[...], nw_ref[...], al_ref[...],
                                 db_ref[...], dv_ref[...], state)
        y_ref[...] = y.astype(MXU_DTYPE)
        state_ref[g] = new_state

    return pl.pallas_call(
        body, name="ssd_fwd", grid=(nc, SSD_GROUPS),
        in_specs=[pl.BlockSpec((CHUNK, SSD_GC), lambda c, g: (c, g)),
                  pl.BlockSpec((CHUNK, SSD_GW), lambda c, g: (c, COL_Z // SSD_GW + g)),
                  pl.BlockSpec((CHUNK, LANES), lambda c, g: (c, 0)),
                  pl.BlockSpec((1, SSD_GW), lambda c, g: (0, g)), _full((1, LANES)), _full((1, LANES)), _full((1, LANES))],
        out_specs=[pl.BlockSpec((CHUNK, SSD_GW), lambda c, g: (c, g)),
                   pl.BlockSpec((1, 1, SSD_GW, SSD_STATE), lambda c, g: (c, g, 0, 0))],
        out_shape=[jax.ShapeDtypeStruct((t, SSD_WIDTH), MXU_DTYPE),
                   jax.ShapeDtypeStruct((nc, SSD_GROUPS, SSD_GW, SSD_STATE), F32)],
        scratch_shapes=[pltpu.VMEM((SSD_GROUPS, SSD_GW, SSD_STATE), F32)],
        compiler_params=_params(("arbitrary", "arbitrary")),
    )(conv_ssd, proj_main, proj_small, normw, alog, dtb, dvec)


def _accumulate(ref, first, value):
    @pl.when(first)
    def _():
        ref[...] = value

    @pl.when(jnp.logical_not(first))
    def _():
        ref[...] += value


def ssd_bwd(conv_ssd, proj_main, proj_small, normw, alog, dtb, dvec, hist, dy):
    t = conv_ssd.shape[0]
    nc = t // CHUNK
    rev = lambda c: nc - 1 - c

    def body(xbc_ref, z_ref, sm_ref, nw_ref, al_ref, db_ref, dv_ref, hist_ref, dy_ref,
             dxbc_ref, dz_ref, dsm_ref, dnw_ref, dal_ref, ddb_ref, ddv_ref, dstate_ref):
        c, g = pl.program_id(0), pl.program_id(1)

        @pl.when(c == 0)
        def _():
            dstate_ref[g] = jnp.zeros((SSD_GW, SSD_STATE), F32)

        fn = functools.partial(ssd_chunk, g)
        _, vjp = jax.vjp(fn, *_ssd_parts(xbc_ref), z_ref[...], sm_ref[...], nw_ref[...], al_ref[...], db_ref[...],
                         dv_ref[...], hist_ref[0, 0])
        dxs, dbm, dcm, dz, dsm, dnw, dal, ddb, ddv, dstate = vjp((dy_ref[...], dstate_ref[g]))
        dxbc_ref[:, :SSD_GW] = dxs
        dxbc_ref[:, SSD_GW:SSD_GW + SSD_STATE] = dbm
        dxbc_ref[:, SSD_GW + SSD_STATE:] = dcm
        dz_ref[...] = dz
        dstate_ref[g] = dstate
        _accumulate(dsm_ref, g == 0, dsm)
        first = jnp.logical_and(c == 0, g == 0)
        _accumulate(dal_ref, first, dal)
        _accumulate(ddb_ref, first, ddb)
        _accumulate(ddv_ref, first, ddv)

        @pl.when(c == 0)
        def _():
            dnw_ref[g] = dnw

        @pl.when(c > 0)
        def _():
            dnw_ref[g] += dnw

    return pl.pallas_call(
        body, name="ssd_bwd", grid=(nc, SSD_GROUPS),
        in_specs=[pl.BlockSpec((CHUNK, SSD_GC), lambda c, g: (rev(c), g)),
                  pl.BlockSpec((CHUNK, SSD_GW), lambda c, g: (rev(c), COL_Z // SSD_GW + g)),
                  pl.BlockSpec((CHUNK, LANES), lambda c, g: (rev(c), 0)),
                  pl.BlockSpec((1, SSD_GW), lambda c, g: (0, g)), _full((1, LANES)), _full((1, LANES)), _full((1, LANES)),
                  pl.BlockSpec((1, 1, SSD_GW, SSD_STATE), lambda c, g: (rev(c), g, 0, 0)),
                  pl.BlockSpec((CHUNK, SSD_GW), lambda c, g: (rev(c), g))],
        out_specs=[pl.BlockSpec((CHUNK, SSD_GC), lambda c, g: (rev(c), g)),
                   pl.BlockSpec((CHUNK, SSD_GW), lambda c, g: (rev(c), COL_Z // SSD_GW + g)),
                   pl.BlockSpec((CHUNK, LANES), lambda c, g: (rev(c), 0)),
                   _full((SSD_GROUPS, 1, SSD_GW)), _full((1, LANES)), _full((1, LANES)), _full((1, LANES))],
        out_shape=[jax.ShapeDtypeStruct((t, SSD_CONV), F32), jax.ShapeDtypeStruct((t, MAIN), F32),
                   jax.ShapeDtypeStruct((t, LANES), F32), jax.ShapeDtypeStruct((SSD_GROUPS, 1, SSD_GW), F32),
                   jax.ShapeDtypeStruct((1, LANES), F32), jax.ShapeDtypeStruct((1, LANES), F32),
                   jax.ShapeDtypeStruct((1, LANES), F32)],
        scratch_shapes=[pltpu.VMEM((SSD_GROUPS, SSD_GW, SSD_STATE), F32)],
        compiler_params=_params(("arbitrary", "arbitrary")),
    )(conv_ssd, proj_main, proj_small, normw, alog, dtb, dvec, hist, dy)


def gdn_fwd(conv_gdn, proj_main, proj_small, normw, alog, dtb):
    t = conv_gdn.shape[0]
    nc = t // CHUNK

    def body(qkv_ref, gate_ref, sm_ref, nw_ref, al_ref, db_ref, y_ref, hist_ref, state_ref):
        h = pl.program_id(1)

        @pl.when(pl.program_id(0) == 0)
        def _():
            state_ref[h] = jnp.zeros((GDN_DK, GDN_DV), F32)

        state = state_ref[h]
        hist_ref[0, 0] = state
        y, new_state = gdn_chunk(h, *_gdn_parts(qkv_ref), sm_ref[...], gate_ref[...], nw_ref[...], al_ref[...],
                                 db_ref[...], state)
        y_ref[...] = y.astype(MXU_DTYPE)
        state_ref[h] = new_state

    return pl.pallas_call(
        body, name="gdn_fwd", grid=(nc, GDN_HEADS),
        in_specs=[pl.BlockSpec((CHUNK, GDN_HC), lambda c, h: (c, h)),
                  pl.BlockSpec((CHUNK, GDN_DV), lambda c, h: (c, COL_GATE // GDN_DV + h)),
                  pl.BlockSpec((CHUNK, LANES), lambda c, h: (c, 0)),
                  _full((1, GDN_DV)), _full((1, LANES)), _full((1, LANES))],
        out_specs=[pl.BlockSpec((CHUNK, GDN_DV), lambda c, h: (c, h)),
                   pl.BlockSpec((1, 1, GDN_DK, GDN_DV), lambda c, h: (c, h, 0, 0))],
        out_shape=[jax.ShapeDtypeStruct((t, GDN_W), MXU_DTYPE),
                   jax.ShapeDtypeStruct((nc, GDN_HEADS, GDN_DK, GDN_DV), F32)],
        scratch_shapes=[pltpu.VMEM((GDN_HEADS, GDN_DK, GDN_DV), F32)],
        compiler_params=_params(("arbitrary", "arbitrary")),
    )(conv_gdn, proj_main, proj_small, normw, alog, dtb)


def gdn_bwd(dproj_main, conv_gdn, proj_main, proj_small, normw, alog, dtb, hist, dy):
    t = conv_gdn.shape[0]
    nc = t // CHUNK
    rev = lambda c: nc - 1 - c

    def body(alias_ref, qkv_ref, gate_ref, sm_ref, nw_ref, al_ref, db_ref, hist_ref, dy_ref,
             dgate_ref, dqkv_ref, dsm_ref, dnw_ref, dal_ref, ddb_ref, dstate_ref):
        del alias_ref
        c, h = pl.program_id(0), pl.program_id(1)

        @pl.when(c == 0)
        def _():
            dstate_ref[h] = jnp.zeros((GDN_DK, GDN_DV), F32)

        fn = functools.partial(gdn_chunk, h)
        _, vjp = jax.vjp(fn, *_gdn_parts(qkv_ref), sm_ref[...], gate_ref[...], nw_ref[...], al_ref[...], db_ref[...],
                         hist_ref[0, 0])
        dq, dk, dv, dsm, dgate, dnw, dal, ddb, dstate = vjp((dy_ref[...], dstate_ref[h]))
        dqkv_ref[:, :GDN_DK] = dq
        dqkv_ref[:, GDN_DK:2 * GDN_DK] = dk
        dqkv_ref[:, 2 * GDN_DK:] = dv
        dgate_ref[...] = dgate
        dstate_ref[h] = dstate
        _accumulate(dsm_ref, h == 0, dsm)
        first = jnp.logical_and(c == 0, h == 0)
        _accumulate(dnw_ref, first, dnw)
        _accumulate(dal_ref, first, dal)
        _accumulate(ddb_ref, first, ddb)

    return pl.pallas_call(
        body, name="gdn_bwd", grid=(nc, GDN_HEADS),
        in_specs=[ANY, pl.BlockSpec((CHUNK, GDN_HC), lambda c, h: (rev(c), h)),
                  pl.BlockSpec((CHUNK, GDN_DV), lambda c, h: (rev(c), COL_GATE // GDN_DV + h)),
                  pl.BlockSpec((CHUNK, LANES), lambda c, h: (rev(c), 0)),
                  _full((1, GDN_DV)), _full((1, LANES)), _full((1, LANES)),
                  pl.BlockSpec((1, 1, GDN_DK, GDN_DV), lambda c, h: (rev(c), h, 0, 0)),
                  pl.BlockSpec((CHUNK, GDN_DV), lambda c, h: (rev(c), h))],
        out_specs=[pl.BlockSpec((CHUNK, GDN_DV), lambda c, h: (rev(c), COL_GATE // GDN_DV + h)),
                   pl.BlockSpec((CHUNK, GDN_HC), lambda c, h: (rev(c), h)),
                   pl.BlockSpec((CHUNK, LANES), lambda c, h: (rev(c), 0)),
                   _full((1, GDN_DV)), _full((1, LANES)), _full((1, LANES))],
        out_shape=[jax.ShapeDtypeStruct(dproj_main.shape, F32), jax.ShapeDtypeStruct((t, GDN_CONV), F32),
                   jax.ShapeDtypeStruct((t, LANES), F32), jax.ShapeDtypeStruct((1, GDN_DV), F32),
                   jax.ShapeDtypeStruct((1, LANES), F32), jax.ShapeDtypeStruct((1, LANES), F32)],
        scratch_shapes=[pltpu.VMEM((GDN_HEADS, GDN_DK, GDN_DV), F32)],
        input_output_aliases={0: 0},
        compiler_params=_params(("arbitrary", "arbitrary")),
    )(dproj_main, conv_gdn, proj_main, proj_small, normw, alog, dtb, hist, dy)


def out_proj_loss(x, y_ssd, y_gdn, w_out, final_w, target):
    t = x.shape[0]
    tm = min(256, t)

    def body(x_ref, ys_ref, yg_ref, wo_ref, fw_ref, tg_ref, loss_ref, dhid_ref, dys_ref, dyg_ref, dwo_ref, dfw_ref):
        i = pl.program_id(0)
        ys, yg = ys_ref[...], yg_ref[...]
        wo_s, wo_g = wo_ref[:SSD_WIDTH, :], wo_ref[SSD_WIDTH:, :]
        hid = x_ref[...] + _raw_dot(ys, wo_s, 1, 0, False) + _raw_dot(yg, wo_g, 1, 0, False)
        out, vjp = jax.vjp(rmsnorm, hid, fw_ref[...])
        err = out - tg_ref[...]
        loss = 0.5 * jnp.sum(jnp.mean(err * err, axis=-1, keepdims=True), axis=0, keepdims=True)
        dhid, dfw = vjp(err * (1.0 / D_MODEL))
        dhid_ref[...] = dhid
        dys_ref[...] = _raw_dot(dhid, wo_s, 1, 1, False)
        dyg_ref[...] = _raw_dot(dhid, wo_g, 1, 1, False)
        first = i == 0
        _accumulate(loss_ref, first, jnp.broadcast_to(loss, loss_ref.shape))
        _accumulate(dfw_ref, first, dfw)

        @pl.when(first)
        def _():
            dwo_ref[:SSD_WIDTH, :] = _raw_dot(ys, dhid, 0, 0, False)
            dwo_ref[SSD_WIDTH:, :] = _raw_dot(yg, dhid, 0, 0, False)

        @pl.when(i > 0)
        def _():
            dwo_ref[:SSD_WIDTH, :] += _raw_dot(ys, dhid, 0, 0, False)
            dwo_ref[SSD_WIDTH:, :] += _raw_dot(yg, dhid, 0, 0, False)

    row = lambda w: pl.BlockSpec((tm, w), lambda i: (i, 0))
    return pl.pallas_call(
        body, name="out_proj_loss", grid=(t // tm,),
        in_specs=[row(D_MODEL), row(SSD_WIDTH), row(GDN_W), _full((SSD_WIDTH + GDN_W, D_MODEL)), _full((1, D_MODEL)),
                  row(D_MODEL)],
        out_specs=[_full((8, LANES)), row(D_MODEL), row(SSD_WIDTH), row(GDN_W), _full((SSD_WIDTH + GDN_W, D_MODEL)),
                   _full((1, D_MODEL))],
        out_shape=[jax.ShapeDtypeStruct((8, LANES), F32), jax.ShapeDtypeStruct((t, D_MODEL), F32),
                   jax.ShapeDtypeStruct((t, SSD_WIDTH), F32), jax.ShapeDtypeStruct((t, GDN_W), F32),
                   jax.ShapeDtypeStruct((SSD_WIDTH + GDN_W, D_MODEL), F32), jax.ShapeDtypeStruct((1, D_MODEL), F32)],
        compiler_params=_params(("arbitrary",)),
    )(x, y_ssd, y_gdn, w_out, final_w, target)


def in_proj_bwd_x(x, normw, w_main, w_small, dproj_main, dsmall_a, dsmall_b, dhid):
    t = x.shape[0]
    tm, tk = min(512, t), 512
    nk = MAIN // tk

    def body(x_ref, nw_ref, wm_ref, ws_ref, dp_ref, da_ref, db_ref, dh_ref, gx_ref, dnw_ref, acc_ref):
        i, k = pl.program_id(0), pl.program_id(1)
        part = _raw_dot(dp_ref[...], wm_ref[...], 1, 1, False)

        @pl.when(k == 0)
        def _():
            acc_ref[...] = part + _raw_dot(da_ref[...] + db_ref[...], ws_ref[...], 1, 1, False)

        @pl.when(k > 0)
        def _():
            acc_ref[...] += part

        @pl.when(k == nk - 1)
        def _():
            _, vjp = jax.vjp(rmsnorm, x_ref[...], nw_ref[...])
            dx, dnw = vjp(acc_ref[...])
            gx_ref[...] = dx + dh_ref[...]
            _accumulate(dnw_ref, i == 0, dnw)

    row = lambda w: pl.BlockSpec((tm, w), lambda i, k: (i, 0))
    return pl.pallas_call(
        body, name="in_proj_bwd_x", grid=(t // tm, nk),
        in_specs=[row(D_MODEL), _full((1, D_MODEL)), pl.BlockSpec((D_MODEL, tk), lambda i, k: (0, k)),
                  _full((D_MODEL, LANES)), pl.BlockSpec((tm, tk), lambda i, k: (i, k)), row(LANES), row(LANES),
                  row(D_MODEL)],
        out_specs=[row(D_MODEL), _full((1, D_MODEL))],
        out_shape=[jax.ShapeDtypeStruct((t, D_MODEL), F32), jax.ShapeDtypeStruct((1, D_MODEL), F32)],
        scratch_shapes=[pltpu.VMEM((tm, D_MODEL), F32)],
        compiler_params=_params(("arbitrary", "arbitrary")),
    )(x, normw, w_main, w_small, dproj_main, dsmall_a, dsmall_b, dhid)


def in_proj_bwd_w(u, dproj_main, dsmall_a, dsmall_b):
    t = u.shape[0]
    tm, tn = min(512, t), 512

    def body(u_ref, dp_ref, da_ref, db_ref, dwm_ref, dws_ref):
        j, i = pl.program_id(0), pl.program_id(1)
        uu = u_ref[...]
        _accumulate(dwm_ref, i == 0, _raw_dot(uu, dp_ref[...], 0, 0, False))

        @pl.when(j == 0)
        def _():
            _accumulate(dws_ref, i == 0, _raw_dot(uu, da_ref[...] + db_ref[...], 0, 0, False))

    return pl.pallas_call(
        body, name="in_proj_bwd_w", grid=(MAIN // tn, t // tm),
        in_specs=[pl.BlockSpec((tm, D_MODEL), lambda j, i: (i, 0)), pl.BlockSpec((tm, tn), lambda j, i: (i, j)),
                  pl.BlockSpec((tm, LANES), lambda j, i: (i, 0)), pl.BlockSpec((tm, LANES), lambda j, i: (i, 0))],
        out_specs=[pl.BlockSpec((D_MODEL, tn), lambda j, i: (0, j)), _full((D_MODEL, LANES))],
        out_shape=[jax.ShapeDtypeStruct((D_MODEL, MAIN), F32), jax.ShapeDtypeStruct((D_MODEL, LANES), F32)],
        compiler_params=_params(("arbitrary", "arbitrary")),
    )(u, dproj_main, dsmall_a, dsmall_b)


def sum_slabs(a, name):
    n, rows, cols = a.shape
    tr = 64 if rows % 64 == 0 else rows

    def body(a_ref, o_ref):
        acc = a_ref[0]
        for d in range(1, n):
            acc = acc + a_ref[d]
        o_ref[...] = acc

    return pl.pallas_call(
        body, name=name, grid=(rows // tr,),
        in_specs=[pl.BlockSpec((n, tr, cols), lambda i: (0, i, 0))],
        out_specs=pl.BlockSpec((tr, cols), lambda i: (i, 0)),
        out_shape=jax.ShapeDtypeStruct((rows, cols), F32),
        compiler_params=_params(("arbitrary",)),
    )(a)


def adamw(w, g, m, v, name):
    rows, cols = w.shape
    tr = 128 if rows % 128 == 0 else rows

    def body(w_ref, g_ref, m_ref, v_ref, d_ref, nm_ref, nv_ref):
        gg = g_ref[...]
        nm = ADAM_B1 * m_ref[...] + (1.0 - ADAM_B1) * gg
        nv = ADAM_B2 * v_ref[...] + (1.0 - ADAM_B2) * (gg * gg)
        m_hat = nm / (1.0 - ADAM_B1 ** ADAM_STEP)
        v_hat = nv / (1.0 - ADAM_B2 ** ADAM_STEP)
        d_ref[...] = -ADAM_LR * (m_hat / (jnp.sqrt(v_hat) + ADAM_EPS) + ADAM_WD * w_ref[...])
        nm_ref[...] = nm
        nv_ref[...] = nv

    spec = pl.BlockSpec((tr, cols), lambda i: (i, 0))
    shp = jax.ShapeDtypeStruct((rows, cols), F32)
    return pl.pallas_call(
        body, name=name, grid=(rows // tr,), in_specs=[spec] * 4, out_specs=[spec] * 3, out_shape=[shp] * 3,
        compiler_params=_params(("arbitrary",)),
    )(w, g, m, v)


def _my_place():
    return lax.axis_index("x"), lax.axis_index("y"), lax.axis_index("c")


def gather_weights(shards):
    n = len(shards)

    def body(*refs):
        srcs, outs = refs[:n], refs[n:2 * n]
        send_sems, recv_sems, local_sems = refs[2 * n:]
        x, y, c = _my_place()
        me = 2 * x + y
        chips = [(1 - x, y), (x, 1 - y), (1 - x, 1 - y)]
        local = [pltpu.make_async_copy(srcs[i], outs[i].at[me], local_sems.at[i]) for i in range(n)]
        for cp in local:
            cp.start()
        sends = []
        for j, (px, py) in enumerate(chips):
            for i in range(n):
                sends.append(pltpu.make_async_remote_copy(
                    src_ref=srcs[i], dst_ref=outs[i].at[me], send_sem=send_sems.at[j * n + i],
                    recv_sem=recv_sems.at[j * n + i], device_id=(px, py, c), device_id_type=MESH))
        for cp in sends:
            cp.start()
        for j, (px, py) in enumerate(chips):
            for i in range(n):
                pltpu.make_async_remote_copy(
                    src_ref=srcs[i], dst_ref=outs[i].at[2 * px + py], send_sem=send_sems.at[j * n + i],
                    recv_sem=recv_sems.at[j * n + i], device_id=(px, py, c), device_id_type=MESH).wait_recv()
        for cp in sends:
            cp.wait_send()
        for cp in local:
            cp.wait()

    return pl.pallas_call(
        body, name="gather_weights",
        in_specs=[HBM] * n, out_specs=[HBM] * n,
        out_shape=[jax.ShapeDtypeStruct((N_CHIP,) + s.shape, s.dtype) for s in shards],
        scratch_shapes=[pltpu.SemaphoreType.DMA((3 * n,)), pltpu.SemaphoreType.DMA((3 * n,)),
                        pltpu.SemaphoreType.DMA((n,))],
    )(*shards)


def _peer(x, y, c, mask):
    mx, my, mc = (mask >> 2) & 1, (mask >> 1) & 1, mask & 1
    return (x ^ mx if mx else x, y ^ my if my else y, c ^ mc if mc else c)


def exchange_slabs(slabbed, replicated):
    ns, nr = len(slabbed), len(replicated)
    n = ns + nr

    def body(*refs):
        srcs, outs = refs[:n], refs[n:2 * n]
        send_sems, recv_sems, local_sems = refs[2 * n:]
        x, y, c = _my_place()
        me = 4 * x + 2 * y + c

        def piece(i, dev):
            return srcs[i].at[dev] if i < ns else srcs[i]

        local = [pltpu.make_async_copy(piece(i, me), outs[i].at[me], local_sems.at[i]) for i in range(n)]
        for cp in local:
            cp.start()
        sends = []
        for mask in range(1, N_DEV):
            px, py, pc = _peer(x, y, c, mask)
            dev = 4 * px + 2 * py + pc
            for i in range(n):
                k = (mask - 1) * n + i
                sends.append(pltpu.make_async_remote_copy(
                    src_ref=piece(i, dev), dst_ref=outs[i].at[me], send_sem=send_sems.at[k], recv_sem=recv_sems.at[k],
                    device_id=(px, py, pc), device_id_type=MESH))
        for cp in sends:
            cp.start()
        for mask in range(1, N_DEV):
            px, py, pc = _peer(x, y, c, mask)
            dev = 4 * px + 2 * py + pc
            for i in range(n):
                k = (mask - 1) * n + i
                pltpu.make_async_remote_copy(
                    src_ref=piece(i, dev), dst_ref=outs[i].at[dev], send_sem=send_sems.at[k], recv_sem=recv_sems.at[k],
                    device_id=(px, py, pc), device_id_type=MESH).wait_recv()
        for cp in sends:
            cp.wait_send()
        for cp in local:
            cp.wait()

    shapes = [jax.ShapeDtypeStruct(a.shape, a.dtype) for a in slabbed]
    shapes += [jax.ShapeDtypeStruct((N_DEV,) + a.shape, a.dtype) for a in replicated]
    return pl.pallas_call(
        body, name="exchange_slabs",
        in_specs=[HBM] * n, out_specs=[HBM] * n, out_shape=shapes,
        scratch_shapes=[pltpu.SemaphoreType.DMA((7 * n,)), pltpu.SemaphoreType.DMA((7 * n,)),
                        pltpu.SemaphoreType.DMA((n,))],
    )(*slabbed, *replicated)


def exchange_halves(halves):
    n = len(halves)

    def body(*refs):
        srcs, outs = refs[:n], refs[n:2 * n]
        send_sems, recv_sems, local_sems = refs[2 * n:]
        x, y, c = _my_place()
        local = [pltpu.make_async_copy(srcs[i], outs[i].at[c], local_sems.at[i]) for i in range(n)]
        for cp in local:
            cp.start()
        sends = [pltpu.make_async_remote_copy(
            src_ref=srcs[i], dst_ref=outs[i].at[c], send_sem=send_sems.at[i], recv_sem=recv_sems.at[i],
            device_id=(x, y, 1 - c), device_id_type=MESH) for i in range(n)]
        for cp in sends:
            cp.start()
        for i in range(n):
            pltpu.make_async_remote_copy(
                src_ref=srcs[i], dst_ref=outs[i].at[1 - c], send_sem=send_sems.at[i], recv_sem=recv_sems.at[i],
                device_id=(x, y, 1 - c), device_id_type=MESH).wait_recv()
        for cp in sends:
            cp.wait_send()
        for cp in local:
            cp.wait()

    return pl.pallas_call(
        body, name="exchange_halves",
        in_specs=[HBM] * n, out_specs=[HBM] * n,
        out_shape=[jax.ShapeDtypeStruct((2,) + a.shape, a.dtype) for a in halves],
        scratch_shapes=[pltpu.SemaphoreType.DMA((n,)), pltpu.SemaphoreType.DMA((n,)), pltpu.SemaphoreType.DMA((n,))],
    )(*halves)


def _pack(arrays):
    flat = []
    for a in arrays:
        v = a.reshape(-1).astype(F32)
        flat.append(jnp.pad(v, (0, (-v.shape[0]) % LANES)))
    v = jnp.concatenate(flat)
    v = jnp.pad(v, (0, (-v.shape[0]) % (8 * LANES)))
    return v.reshape(-1, LANES)


def _unpack(packed, shapes):
    v, out, pos = packed.reshape(-1), [], 0
    for s in shapes:
        n = int(np.prod(s))
        out.append(v[pos:pos + n].reshape(s))
        pos += n + (-n) % LANES
    return out


def _lanes(vec, start):
    n = vec.shape[-1]
    return jnp.pad(vec.reshape(1, n).astype(F32), ((0, 0), (start, LANES - start - n)))


def kernel(x, norm_w, w_in, ssd_conv_w, ssd_conv_b, ssd_dt_bias, ssd_a_log, ssd_d, ssd_norm_w, gdn_conv_w, gdn_dt_bias, gdn_a_log, gdn_norm_w, w_out, final_norm_w, loss_target, m_norm_w, m_w_in, m_ssd_conv_w, m_ssd_conv_b, m_ssd_dt_bias, m_ssd_a_log, m_ssd_d, m_ssd_norm_w, m_gdn_conv_w, m_gdn_dt_bias, m_gdn_a_log, m_gdn_norm_w, m_w_out, m_final_norm_w, v_norm_w, v_w_in, v_ssd_conv_w, v_ssd_conv_b, v_ssd_dt_bias, v_ssd_a_log, v_ssd_d, v_ssd_norm_w, v_gdn_conv_w, v_gdn_dt_bias, v_gdn_a_log, v_gdn_norm_w, v_w_out, v_final_norm_w):
    xs = x[0]
    target = loss_target[0]
    chip = 2 * lax.axis_index("x") + lax.axis_index("y")
    w_in_shard, w_out_shard = w_in[0], w_out[0]
    in_cols = w_in_shard.shape[1]
    out_rows = w_out_shard.shape[0]

    g_in, g_out, g_cs, g_cg = gather_weights(
        [w_in_shard.astype(MXU_DTYPE), w_out_shard.astype(MXU_DTYPE), ssd_conv_w[0], gdn_conv_w[0]])
    w_in_full = jnp.transpose(g_in, (1, 0, 2)).reshape(D_MODEL, IN_DIM)
    w_out_full = g_out.reshape(N_CHIP * out_rows, D_MODEL)
    cw_ssd = _take(jnp.transpose(g_cs, (1, 0, 2)).reshape(4, SSD_CONV), _ssd_conv_segments())
    cw_gdn = _take(jnp.transpose(g_cg, (1, 0, 2)).reshape(4, GDN_CONV), _gdn_conv_segments())
    cb_ssd = _take(ssd_conv_b, _ssd_conv_segments())
    cb_gdn = jnp.zeros((1, GDN_CONV), F32)
    w_main = _take(w_in_full, _main_segments())
    w_small = jnp.concatenate([w_in_full[:, 2560:2576], w_in_full[:, 6672:6688],
                               jnp.zeros((D_MODEL, LANES - 32), MXU_DTYPE)], axis=1)
    alog = _lanes(ssd_a_log, 0) + _lanes(gdn_a_log, LANE_GA)
    dtb = _lanes(ssd_dt_bias, 0) + _lanes(gdn_dt_bias, LANE_GA)
    dvec = _lanes(ssd_d, 0)
    fw = final_norm_w.reshape(1, D_MODEL)

    proj_main, proj_small, u = in_proj(xs, norm_w, w_main, w_small)
    conv_ssd = conv_fwd(proj_main, COL_SSD, SSD_CONV, cw_ssd, cb_ssd, "conv_fwd_ssd")
    conv_gdn = conv_fwd(proj_main, COL_GDN, GDN_CONV, cw_gdn, cb_gdn, "conv_fwd_gdn")
    y_ssd, hist_ssd = ssd_fwd(conv_ssd, proj_main, proj_small, ssd_norm_w, alog, dtb, dvec)
    y_gdn, hist_gdn = gdn_fwd(conv_gdn, proj_main, proj_small, gdn_norm_w, alog, dtb)

    loss_blk, dhid, dy_ssd, dy_gdn, d_w_out, d_fw = out_proj_loss(xs, y_ssd, y_gdn, w_out_full, fw, target)
    dconv_ssd, dproj_main, dsmall_ssd, d_ssd_nw, d_alog_s, d_dtb_s, d_dvec = ssd_bwd(
        conv_ssd, proj_main, proj_small, ssd_norm_w, alog, dtb, dvec, hist_ssd, dy_ssd)
    dproj_main, dconv_gdn, dsmall_gdn, d_gdn_nw, d_alog_g, d_dtb_g = gdn_bwd(
        dproj_main, conv_gdn, proj_main, proj_small, gdn_norm_w, alog, dtb, hist_gdn, dy_gdn)
    dpre_ssd, dwb_ssd = conv_bwd_pre(proj_main, COL_SSD, SSD_CONV, cw_ssd, cb_ssd, dconv_ssd, "conv_bwd_pre_ssd")
    dpre_gdn, dwb_gdn = conv_bwd_pre(proj_main, COL_GDN, GDN_CONV, cw_gdn, cb_gdn, dconv_gdn, "conv_bwd_pre_gdn")
    dproj_main = conv_bwd_x(dproj_main, COL_SSD, SSD_CONV, cw_ssd, dpre_ssd, "conv_bwd_x_ssd")
    dproj_main = conv_bwd_x(dproj_main, COL_GDN, GDN_CONV, cw_gdn, dpre_gdn, "conv_bwd_x_gdn")
    grad_x, d_norm_w = in_proj_bwd_x(xs, norm_w, w_main, w_small, dproj_main, dsmall_ssd, dsmall_gdn, dhid)
    d_w_main, d_w_small = in_proj_bwd_w(u, dproj_main, dsmall_ssd, dsmall_gdn)

    d_w_in = _untake(d_w_main, _main_segments())
    d_w_in = jnp.concatenate([d_w_in[:, :2560], d_w_small[:, 0:16], d_w_in[:, 2560:], d_w_small[:, 16:32]], axis=1)
    d_w_in = jnp.transpose(d_w_in.reshape(D_MODEL, N_CHIP, in_cols), (1, 0, 2))
    d_alog, d_dtb = d_alog_s + d_alog_g, d_dtb_s + d_dtb_g
    small_grads = [
        loss_blk[0:1, 0:1],
        d_norm_w,
        _untake(dwb_ssd[0:4], _ssd_conv_segments()),
        _untake(dwb_ssd[4:5], _ssd_conv_segments()),
        d_dtb[:, 0:SSD_HEADS], d_alog[:, 0:SSD_HEADS], d_dvec[:, 0:SSD_HEADS],
        d_ssd_nw.reshape(1, SSD_WIDTH),
        _untake(dwb_gdn[0:4], _gdn_conv_segments()),
        d_dtb[:, LANE_GA:LANE_GA + GDN_HEADS], d_alog[:, LANE_GA:LANE_GA + GDN_HEADS],
        d_gdn_nw, d_fw,
    ]
    small_shapes = [a.shape for a in small_grads]

    r_in, r_out, r_small = exchange_slabs(
        [d_w_in.reshape(N_DEV, D_MODEL // 2, in_cols), d_w_out.reshape(N_DEV, out_rows // 2, D_MODEL)],
        [_pack(small_grads)])
    half_in = sum_slabs(r_in, "sum_w_in")
    half_out = sum_slabs(r_out, "sum_w_out")
    small_sum = sum_slabs(r_small, "sum_small")
    full_in, full_out = exchange_halves([half_in, half_out])
    grad_w_in = full_in.reshape(D_MODEL, in_cols)
    grad_w_out = full_out.reshape(out_rows, D_MODEL)
    (loss, g_norm_w, g_ssd_cw, g_ssd_cb, g_ssd_dtb, g_ssd_alog, g_ssd_d, g_ssd_nw, g_gdn_cw, g_gdn_dtb, g_gdn_alog,
     g_gdn_nw, g_fw) = _unpack(small_sum, small_shapes)
    sc, gc = ssd_conv_w.shape[2], gdn_conv_w.shape[2]
    g_ssd_cw = lax.dynamic_slice_in_dim(g_ssd_cw, chip * sc, sc, axis=1)
    g_gdn_cw = lax.dynamic_slice_in_dim(g_gdn_cw, chip * gc, gc, axis=1)

    names = ["norm_w", "ssd_conv_w", "ssd_conv_b", "ssd_dt_bias", "ssd_a_log", "ssd_d", "ssd_norm_w", "gdn_conv_w",
             "gdn_dt_bias", "gdn_a_log", "gdn_norm_w", "final_norm_w"]
    ws = [norm_w, ssd_conv_w, ssd_conv_b, ssd_dt_bias, ssd_a_log, ssd_d, ssd_norm_w, gdn_conv_w, gdn_dt_bias,
          gdn_a_log, gdn_norm_w, final_norm_w]
    ms = [m_norm_w, m_ssd_conv_w, m_ssd_conv_b, m_ssd_dt_bias, m_ssd_a_log, m_ssd_d, m_ssd_norm_w, m_gdn_conv_w,
          m_gdn_dt_bias, m_gdn_a_log, m_gdn_norm_w, m_final_norm_w]
    vs = [v_norm_w, v_ssd_conv_w, v_ssd_conv_b, v_ssd_dt_bias, v_ssd_a_log, v_ssd_d, v_ssd_norm_w, v_gdn_conv_w,
          v_gdn_dt_bias, v_gdn_a_log, v_gdn_norm_w, v_final_norm_w]
    gs = [g_norm_w, g_ssd_cw, g_ssd_cb, g_ssd_dtb, g_ssd_alog, g_ssd_d, g_ssd_nw, g_gdn_cw, g_gdn_dtb, g_gdn_alog,
          g_gdn_nw, g_fw]
    gs = [g.reshape(w.shape) for g, w in zip(gs, ws)]
    shapes = [w.shape for w in ws]
    d_s, m_s, v_s = adamw(_pack(ws), _pack(gs), _pack(ms), _pack(vs), "adamw_small")
    delta = dict(zip(names, _unpack(d_s, shapes)))
    new_m = dict(zip(names, _unpack(m_s, shapes)))
    new_v = dict(zip(names, _unpack(v_s, shapes)))
    grads = dict(zip(names, gs))
    d_in, m_in, v_in = adamw(w_in_shard, grad_w_in, m_w_in[0], v_w_in[0], "adamw_w_in")
    d_out, m_out, v_out = adamw(w_out_shard, grad_w_out, m_w_out[0], v_w_out[0], "adamw_w_out")
    for tbl, a_in, a_out in ((grads, grad_w_in, grad_w_out), (delta, d_in, d_out), (new_m, m_in, m_out),
                             (new_v, v_in, v_out)):
        tbl["w_in"] = a_in[None]
        tbl["w_out"] = a_out[None]

    order = ["norm_w", "w_in", "ssd_conv_w", "ssd_conv_b", "ssd_dt_bias", "ssd_a_log", "ssd_d", "ssd_norm_w",
             "gdn_conv_w", "gdn_dt_bias", "gdn_a_log", "gdn_norm_w", "w_out", "final_norm_w"]
    return (loss.reshape(()), grad_x[None], *[grads[k] for k in order], *[delta[k] for k in order],
            *[new_m[k] for k in order], *[new_v[k] for k in order])
```

```python
import functools

import numpy as np
import jax
import jax.numpy as jnp
from jax import lax
from jax.experimental import pallas as pl
from jax.experimental.pallas import tpu as pltpu

F32 = jnp.float32
MXU_DTYPE = jnp.bfloat16
COMM_DTYPE = jnp.bfloat16
MESH = pl.DeviceIdType.MESH

D_MODEL = 1024
CHUNK = 64
EPS = 1e-6
SSD_HEADS, SSD_GROUPS, SSD_STATE = 16, 2, 128
SSD_WIDTH, SSD_CONV = 1024, 1536
SSD_GW = SSD_WIDTH // SSD_GROUPS
SSD_GC = SSD_GW + 2 * SSD_STATE
GDN_HEADS, GDN_DK, GDN_DV = 8, 128, 128
GDN_W, GDN_CONV = 1024, 3072
GDN_HC = 2 * GDN_DK + GDN_DV
IN_DIM = 6688
MAIN = 6656
LANES = 128
COL_Z, COL_GATE, COL_SSD, COL_GDN = 0, 1024, 2048, 3584
GDN_HB = 8
LANE_GA, LANE_GB = 16, 24
N_DEV, N_CHIP = 8, 4
VMEM_LIMIT = 52 * 1024 * 1024

ADAM_LR, ADAM_B1, ADAM_B2, ADAM_EPS, ADAM_WD, ADAM_STEP = 0.001, 0.9, 0.999, 1e-08, 0.01, 10


def _ssd_conv_segments():
    segs = []
    for g in range(SSD_GROUPS):
        segs.append((g * SSD_GW, SSD_GW))
        segs.append((SSD_WIDTH + g * SSD_STATE, SSD_STATE))
        segs.append((SSD_WIDTH + SSD_GROUPS * SSD_STATE + g * SSD_STATE, SSD_STATE))
    return segs


def _gdn_conv_segments():
    segs = []
    for h in range(GDN_HEADS):
        segs.append((h * GDN_DK, GDN_DK))
        segs.append((GDN_W + h * GDN_DK, GDN_DK))
        segs.append((2 * GDN_W + h * GDN_DV, GDN_DV))
    return segs


def _main_segments():
    o_xbc, o_gate, o_qkv = 1024, 2576, 3600
    segs = [(0, 1024), (o_gate, 1024)]
    segs += [(o_xbc + s, w) for s, w in _ssd_conv_segments()]
    segs += [(o_qkv + s, w) for s, w in _gdn_conv_segments()]
    return segs


def _take(a, segs):
    return jnp.concatenate([a[..., s:s + w] for s, w in segs], axis=-1)


def _untake(a, segs):
    pos, placed = 0, []
    for s, w in segs:
        placed.append((s, pos, w))
        pos += w
    placed.sort()
    return jnp.concatenate([a[..., p:p + w] for _, p, w in placed], axis=-1)


def _split(a, n):
    parts, rest = [], a.astype(F32)
    for i in range(n):
        p = rest.astype(MXU_DTYPE)
        parts.append(p)
        if i < n - 1:
            rest = rest - p.astype(F32)
    return parts


def _raw_dot(a, b, ca, cb, mode="bf16"):
    d = lambda u, v: lax.dot_general(u, v, (((ca,), (cb,)), ((), ())), preferred_element_type=F32)
    if mode == "bf16":
        return d(a.astype(MXU_DTYPE), b.astype(MXU_DTYPE))
    if mode == "x3":
        (ah, al), (bh, bl) = _split(a, 2), _split(b, 2)
        return d(ah, bh) + (d(ah, bl) + d(al, bh))
    if mode == "sel_a":
        a0 = a.astype(MXU_DTYPE)
        b1, b2, b3 = _split(b, 3)
        return d(a0, b1) + (d(a0, b2) + d(a0, b3))
    assert mode == "sel_b", mode
    b0 = b.astype(MXU_DTYPE)
    a1, a2, a3 = _split(a, 3)
    return d(a1, b0) + (d(a2, b0) + d(a3, b0))


@functools.partial(jax.custom_vjp, nondiff_argnums=(2,))
def mm_nn(a, b, mode="bf16"):
    return _raw_dot(a, b, 1, 0, mode)


@functools.partial(jax.custom_vjp, nondiff_argnums=(2,))
def mm_nt(a, b, mode="bf16"):
    return _raw_dot(a, b, 1, 1, mode)


@functools.partial(jax.custom_vjp, nondiff_argnums=(2,))
def mm_tn(a, b, mode="bf16"):
    return _raw_dot(a, b, 0, 0, mode)


_SAME = {"bf16": ("bf16", "bf16"), "x3": ("x3", "x3")}
_NN_BWD = dict(_SAME, sel_a=("bf16", "sel_a"), sel_b=("sel_b", "bf16"))
_NT_BWD = dict(_SAME, sel_a=("bf16", "sel_b"), sel_b=("sel_b", "bf16"))
_TN_BWD = dict(_SAME, sel_a=("bf16", "sel_a"), sel_b=("sel_a", "bf16"))
mm_nn.defvjp(lambda a, b, m: (_raw_dot(a, b, 1, 0, m), (a, b)),
             lambda m, r, g: (mm_nt(g, r[1], _NN_BWD[m][0]), mm_tn(r[0], g, _NN_BWD[m][1])))
mm_nt.defvjp(lambda a, b, m: (_raw_dot(a, b, 1, 1, m), (a, b)),
             lambda m, r, g: (mm_nn(g, r[1], _NT_BWD[m][0]), mm_tn(g, r[0], _NT_BWD[m][1])))
mm_tn.defvjp(lambda a, b, m: (_raw_dot(a, b, 0, 0, m), (a, b)),
             lambda m, r, g: (mm_nt(r[1], g, _TN_BWD[m][0]), mm_nn(r[0], g, _TN_BWD[m][1])))


@jax.custom_jvp
def sigmoid(x):
    return 1.0 / (1.0 + jnp.exp(-x))


@sigmoid.defjvp
def _sigmoid_jvp(p, t):
    s = sigmoid(p[0])
    return s, t[0] * s * (1.0 - s)


@jax.custom_jvp
def softplus(x):
    return jnp.maximum(x, 0.0) + jnp.log(1.0 + jnp.exp(-jnp.abs(x)))


@softplus.defjvp
def _softplus_jvp(p, t):
    return softplus(p[0]), t[0] * sigmoid(p[0])


def silu(x):
    return x * sigmoid(x)


def rmsnorm(x, w):
    return x * lax.rsqrt(jnp.mean(x * x, axis=-1, keepdims=True) + EPS) * w


def _iota(shape, dim):
    return lax.broadcasted_iota(jnp.int32, shape, dim)


def _tri_inv_impl(mats):
    n = mats[0].shape[0]
    r, c = _iota((n, n), 0), _iota((n, n), 1)
    eye = jnp.where(r == c, 1.0, 0.0).astype(F32)
    blockdiag = (r >> 4) == (c >> 4)
    dot = lambda u, v: _raw_dot(u, v, 1, 0, "x3")
    each = lambda f, *ls: [f(*xs) for xs in zip(*ls)]
    dg = each(lambda a: jnp.where(blockdiag, a, 0.0), mats)
    off = each(lambda a, d: a - d, mats, dg)
    m = each(lambda d: -d, dg)
    p = each(lambda x: eye + x, m)
    pw = m
    for _ in range(3):
        pw = each(lambda x: dot(x, x), pw)
        p = each(lambda x, y: x + dot(x, y), p, pw)
    e = each(dot, p, off)
    e2 = each(lambda x: dot(x, x), e)
    q = each(lambda x: eye - x, e)
    q = each(lambda x, y: x + dot(x, y), q, e2)
    return each(dot, q, p)


def _tri_inv_bwd(ts, gs):
    x = [mm_nt(g, t, "x3") for g, t in zip(gs, ts)]
    return [-mm_tn(t, y, "x3") for t, y in zip(ts, x)]


@jax.custom_vjp
def tri_inv(mats):
    return _tri_inv_impl(mats)


def _tri_inv_fwd(mats):
    ts = _tri_inv_impl(mats)
    return ts, ts


tri_inv.defvjp(_tri_inv_fwd, lambda ts, gs: (_tri_inv_bwd(ts, gs),))


@jax.custom_vjp
def tri_inv_saved(mats, ts):
    del mats
    return ts


tri_inv_saved.defvjp(lambda mats, ts: (ts, ts),
                     lambda ts, gs: (_tri_inv_bwd(ts, gs), [jnp.zeros_like(t) for t in ts]))


def _chunk_masks():
    r, c = _iota((CHUNK, CHUNK), 0), _iota((CHUNK, CHUNK), 1)
    return r >= c, r > c, r == c


def _log_decay_cumsum(small, alog, dtb, incl):
    sp = softplus(small + dtb)
    la = -jnp.exp(alog) * sp
    tri = jnp.where(incl, 1.0, 0.0).astype(F32)
    return sp, mm_nn(tri, la, "sel_a")


def _col_of(x, lane_mask):
    return jnp.sum(jnp.where(lane_mask, x, 0.0), axis=1, keepdims=True)


def _decay_matrix(col, incl, eye):
    row = jnp.sum(jnp.where(eye, col, 0.0), axis=0, keepdims=True)
    return jnp.where(incl, jnp.exp(jnp.where(incl, col - row, 0.0)), 0.0)


def gdn_chunk(h0, qs, ks, vs, small, gates, normw, alog, dtb, states, saved_t=None):
    incl, strict, eye = _chunk_masks()
    lane = _iota((1, LANES), 1)
    last = _iota((CHUNK, 1), 0) == CHUNK - 1
    _, lac = _log_decay_cumsum(small, alog, dtb, incl)
    heads = range(len(qs))
    each = lambda f, *ls: [f(*xs) for xs in zip(*ls)]
    gc = [_col_of(lac, lane == LANE_GA + h0 + j) for j in heads]
    beta = [sigmoid(_col_of(small, lane == LANE_GB + h0 + j)) for j in heads]
    decay = each(lambda x: _decay_matrix(x, incl, eye), gc)
    gl = each(lambda x: jnp.sum(jnp.where(last, x, 0.0), axis=0, keepdims=True), gc)
    q = each(lambda x: x * lax.rsqrt(jnp.sum(x * x, axis=-1, keepdims=True) + EPS) * (GDN_DK ** -0.5), qs)
    k = each(lambda x: x * lax.rsqrt(jnp.sum(x * x, axis=-1, keepdims=True) + EPS), ks)
    kb = each(lambda x, b: x * b, k, beta)
    a = each(lambda x, y, d: jnp.where(strict, mm_nt(x, y) * d, 0.0), kb, k, decay)
    t = tri_inv(a) if saved_t is None else tri_inv_saved(a, saved_t)
    eg = each(jnp.exp, gc)
    u = each(lambda x, v, b: mm_nn(x, v * b, "x3"), t, vs, beta)
    w = each(lambda x, y, e: mm_nn(x, y * e, "x3"), t, kb, eg)
    attn = each(lambda x, y, d: mm_nt(x, y) * d, q, k, decay)
    v_new = each(lambda x, y, s: x - mm_nn(y, s), u, w, states)
    o = each(lambda x, e, s, at, vn: mm_nn(x * e, s) + mm_nn(at, vn), q, eg, states, attn, v_new)
    new_states = each(lambda s, x, y, l, c: s * jnp.exp(l) + mm_tn(y * jnp.exp(l - c), x), states, v_new, k, gl, gc)
    ys = each(lambda x, gt: rmsnorm(x, normw) * silu(gt), o, gates)
    return ys, new_states, t


def ssd_chunk(g, xs, bm, cm, z, small, normw, alog, dtb, dvec, state):
    incl, _, eye = _chunk_masks()
    lane = _iota((1, LANES), 1)
    last = _iota((CHUNK, 1), 0) == CHUNK - 1
    hpg = SSD_HEADS // SSD_GROUPS
    sp, lac = _log_decay_cumsum(small, alog, dtb, incl)
    sel = jnp.where(_iota((LANES, SSD_GW), 0) == g * hpg + (_iota((LANES, SSD_GW), 1) >> 6), 1.0, 0.0).astype(F32)
    lac_last = jnp.sum(jnp.where(last, lac, 0.0), axis=0, keepdims=True)
    dt_e = mm_nn(sp, sel, "sel_b")
    elac_e = mm_nn(jnp.exp(lac), sel, "sel_b")
    toend_e = mm_nn(jnp.exp(lac_last - lac), sel, "sel_b")
    d_e = jnp.sum(mm_nn(jnp.broadcast_to(dvec, (8, LANES)), sel, "sel_b"), axis=0, keepdims=True) * 0.125
    row_head = _iota((SSD_GW, LANES), 1) == g * hpg + (_iota((SSD_GW, LANES), 0) >> 6)
    row_scale = jnp.sum(jnp.where(row_head, jnp.exp(lac_last), 0.0), axis=1, keepdims=True)
    xdt = xs * dt_e
    cb = mm_nt(cm, bm)
    y = mm_nt(cm, state) * elac_e + xs * d_e
    col_head = _iota((1, SSD_GW), 1) >> 6
    for j in range(hpg):
        lm = _decay_matrix(_col_of(lac, lane == g * hpg + j), incl, eye)
        y = y + mm_nn(cb * lm, jnp.where(col_head == j, xdt, 0.0))
    new_state = state * row_scale + mm_tn(xdt * toend_e, bm)
    yg = y * silu(z)
    return rmsnorm(yg, normw), new_state


def _params(sem=None):
    return pltpu.CompilerParams(dimension_semantics=sem, vmem_limit_bytes=VMEM_LIMIT)


def _full(shape):
    n = len(shape)
    return pl.BlockSpec(shape, lambda *_: (0,) * n)


ANY = pl.BlockSpec(memory_space=pl.ANY)
HBM = pl.BlockSpec(memory_space=pltpu.HBM)


def in_proj(x, normw, w_main, w_small):
    t = x.shape[0]
    tm, tn = min(512, t), 512

    def body(x_ref, nw_ref, wm_ref, ws_ref, pm_ref, ps_ref, u_ref):
        @pl.when(pl.program_id(1) == 0)
        def _():
            u = rmsnorm(x_ref[...], nw_ref[...]).astype(MXU_DTYPE)
            u_ref[...] = u
            ps_ref[...] = _raw_dot(u, ws_ref[...], 1, 0)
        pm_ref[...] = _raw_dot(u_ref[...], wm_ref[...], 1, 0)

    return pl.pallas_call(
        body, name="in_proj", grid=(t // tm, MAIN // tn),
        in_specs=[pl.BlockSpec((tm, D_MODEL), lambda i, j: (i, 0)), _full((1, D_MODEL)),
                  pl.BlockSpec((D_MODEL, tn), lambda i, j: (0, j)), _full((D_MODEL, LANES))],
        out_specs=[pl.BlockSpec((tm, tn), lambda i, j: (i, j)), pl.BlockSpec((tm, LANES), lambda i, j: (i, 0)),
                   pl.BlockSpec((tm, D_MODEL), lambda i, j: (i, 0))],
        out_shape=[jax.ShapeDtypeStruct((t, MAIN), F32), jax.ShapeDtypeStruct((t, LANES), F32),
                   jax.ShapeDtypeStruct((t, D_MODEL), MXU_DTYPE)],
        compiler_params=_params(("arbitrary", "arbitrary")),
    )(x, normw, w_main, w_small)


CONV_TC = 512
HALO = 8


def _shift_down(cur, prev, s):
    rolled = pltpu.roll(cur, s, 0)
    top = jnp.where(_iota((HALO, cur.shape[1]), 0) < s, pltpu.roll(prev, s, 0), rolled[:HALO])
    return jnp.concatenate([top, rolled[HALO:]], axis=0)


def _shift_up(cur, nxt, s):
    n = cur.shape[0]
    rolled = pltpu.roll(cur, n - s, 0)
    bot = jnp.where(_iota((HALO, cur.shape[1]), 0) >= HALO - s, pltpu.roll(nxt, HALO - s, 0), rolled[n - HALO:])
    return jnp.concatenate([rolled[:n - HALO], bot], axis=0)


def _conv_pre(cur, prev, w_ref, b):
    acc = cur * w_ref[3:4, :] + b
    shifted = [cur]
    for s in (1, 2, 3):
        sh = _shift_down(cur, prev, s)
        shifted.append(sh)
        acc = acc + sh * w_ref[3 - s:4 - s, :]
    return acc, shifted


def conv_fwd(proj_main, col0, width, w, b, name):
    t = proj_main.shape[0]
    tt, c0 = min(512, t), col0 // CONV_TC

    def body(cur_ref, prev_ref, w_ref, b_ref, out_ref):
        prev = jnp.where(pl.program_id(0) > 0, prev_ref[...], 0.0)
        pre, _ = _conv_pre(cur_ref[...], prev, w_ref, b_ref[...])
        out_ref[...] = silu(pre)

    return pl.pallas_call(
        body, name=name, grid=(t // tt, width // CONV_TC),
        in_specs=[pl.BlockSpec((tt, CONV_TC), lambda i, j: (i, c0 + j)),
                  pl.BlockSpec((HALO, CONV_TC), lambda i, j: (jnp.maximum(i * (tt // HALO) - 1, 0), c0 + j)),
                  pl.BlockSpec((4, CONV_TC), lambda i, j: (0, j)), pl.BlockSpec((1, CONV_TC), lambda i, j: (0, j))],
        out_specs=pl.BlockSpec((tt, CONV_TC), lambda i, j: (i, j)),
        out_shape=jax.ShapeDtypeStruct((t, width), F32),
        compiler_params=_params(("arbitrary", "arbitrary")),
    )(proj_main, proj_main, w, b)


def conv_bwd_pre(proj_main, col0, width, w, b, dout, name):
    t = proj_main.shape[0]
    tt, c0 = min(512, t), col0 // CONV_TC

    def body(cur_ref, prev_ref, w_ref, b_ref, do_ref, dpre_ref, dwb_ref):
        i = pl.program_id(1)
        prev = jnp.where(i > 0, prev_ref[...], 0.0)
        pre, shifted = _conv_pre(cur_ref[...], prev, w_ref, b_ref[...])
        sg = sigmoid(pre)
        dpre = do_ref[...] * (sg * (1.0 + pre * (1.0 - sg)))
        dpre_ref[...] = dpre
        row = _iota((HALO, CONV_TC), 0)
        upd = jnp.where(row == 4, jnp.sum(dpre, axis=0, keepdims=True), 0.0)
        for s in range(4):
            upd = upd + jnp.where(row == 3 - s, jnp.sum(dpre * shifted[s], axis=0, keepdims=True), 0.0)

        @pl.when(i == 0)
        def _():
            dwb_ref[...] = upd

        @pl.when(i > 0)
        def _():
            dwb_ref[...] += upd

    return pl.pallas_call(
        body, name=name, grid=(width // CONV_TC, t // tt),
        in_specs=[pl.BlockSpec((tt, CONV_TC), lambda j, i: (i, c0 + j)),
                  pl.BlockSpec((HALO, CONV_TC), lambda j, i: (jnp.maximum(i * (tt // HALO) - 1, 0), c0 + j)),
                  pl.BlockSpec((4, CONV_TC), lambda j, i: (0, j)), pl.BlockSpec((1, CONV_TC), lambda j, i: (0, j)),
                  pl.BlockSpec((tt, CONV_TC), lambda j, i: (i, j))],
        out_specs=[pl.BlockSpec((tt, CONV_TC), lambda j, i: (i, j)), pl.BlockSpec((HALO, CONV_TC), lambda j, i: (0, j))],
        out_shape=[jax.ShapeDtypeStruct((t, width), F32), jax.ShapeDtypeStruct((HALO, width), F32)],
        compiler_params=_params(("arbitrary", "arbitrary")),
    )(proj_main, proj_main, w, b, dout)


def conv_bwd_x(dproj_main, col0, width, w, dpre, name):
    t = dpre.shape[0]
    tt, c0 = min(512, t), col0 // CONV_TC
    nt = t // tt

    def body(alias_ref, cur_ref, nxt_ref, w_ref, out_ref):
        del alias_ref
        nxt = jnp.where(pl.program_id(0) < nt - 1, nxt_ref[...], 0.0)
        cur = cur_ref[...]
        acc = cur * w_ref[3:4, :]
        for s in (1, 2, 3):
            acc = acc + _shift_up(cur, nxt, s) * w_ref[3 - s:4 - s, :]
        out_ref[...] = acc

    return pl.pallas_call(
        body, name=name, grid=(nt, width // CONV_TC),
        in_specs=[ANY, pl.BlockSpec((tt, CONV_TC), lambda i, j: (i, j)),
                  pl.BlockSpec((HALO, CONV_TC), lambda i, j: (jnp.minimum((i + 1) * (tt // HALO), t // HALO - 1), j)),
                  pl.BlockSpec((4, CONV_TC), lambda i, j: (0, j))],
        out_specs=pl.BlockSpec((tt, CONV_TC), lambda i, j: (i, c0 + j)),
        out_shape=jax.ShapeDtypeStruct(dproj_main.shape, F32),
        input_output_aliases={0: 0},
        compiler_params=_params(("arbitrary", "arbitrary")),
    )(dproj_main, dpre, dpre, w)


def _ssd_parts(xbc_ref):
    return xbc_ref[:, :SSD_GW], xbc_ref[:, SSD_GW:SSD_GW + SSD_STATE], xbc_ref[:, SSD_GW + SSD_STATE:]


def _gdn_parts(qkv_ref):
    part = lambda o: [qkv_ref[:, j * GDN_HC + o:j * GDN_HC + o + GDN_DK] for j in range(GDN_HB)]
    return part(0), part(GDN_DK), part(2 * GDN_DK)


def _head_cols(ref):
    return [ref[:, j * GDN_DV:(j + 1) * GDN_DV] for j in range(GDN_HB)]


def _first_head():
    return 0 if GDN_HB == GDN_HEADS else pl.program_id(1) * GDN_HB


def ssd_fwd(conv_ssd, proj_main, proj_small, normw, alog, dtb, dvec):
    t = conv_ssd.shape[0]
    nc = t // CHUNK

    def body(xbc_ref, z_ref, sm_ref, nw_ref, al_ref, db_ref, dv_ref, y_ref, hist_ref, state_ref):
        g = pl.program_id(1)

        @pl.when(pl.program_id(0) == 0)
        def _():
            state_ref[g] = jnp.zeros((SSD_GW, SSD_STATE), F32)

        state = state_ref[g]
        hist_ref[0, 0] = state
        y, new_state = ssd_chunk(g, *_ssd_parts(xbc_ref), z_ref[...], sm_ref[...], nw_ref[...], al_ref[...],
                                 db_ref[...], dv_ref[...], state)
        y_ref[...] = y.astype(MXU_DTYPE)
        state_ref[g] = new_state

    return pl.pallas_call(
        body, name="ssd_fwd", grid=(nc, SSD_GROUPS),
        in_specs=[pl.BlockSpec((CHUNK, SSD_GC), lambda c, g: (c, g)),
                  pl.BlockSpec((CHUNK, SSD_GW), lambda c, g: (c, COL_Z // SSD_GW + g)),
                  pl.BlockSpec((CHUNK, LANES), lambda c, g: (c, 0)),
                  pl.BlockSpec((1, SSD_GW), lambda c, g: (0, g)), _full((1, LANES)), _full((1, LANES)), _full((1, LANES))],
        out_specs=[pl.BlockSpec((CHUNK, SSD_GW), lambda c, g: (c, g)),
                   pl.BlockSpec((1, 1, SSD_GW, SSD_STATE), lambda c, g: (c, g, 0, 0))],
        out_shape=[jax.ShapeDtypeStruct((t, SSD_WIDTH), MXU_DTYPE),
                   jax.ShapeDtypeStruct((nc, SSD_GROUPS, SSD_GW, SSD_STATE), F32)],
        scratch_shapes=[pltpu.VMEM((SSD_GROUPS, SSD_GW, SSD_STATE), F32)],
        compiler_params=_params(("arbitrary", "arbitrary")),
    )(conv_ssd, proj_main, proj_small, normw, alog, dtb, dvec)


def _accumulate(ref, first, value):
    @pl.when(first)
    def _():
        ref[...] = value

    @pl.when(jnp.logical_not(first))
    def _():
        ref[...] += value


def ssd_bwd(conv_ssd, proj_main, proj_small, normw, alog, dtb, dvec, hist, dy):
    t = conv_ssd.shape[0]
    nc = t // CHUNK
    rev = lambda c: nc - 1 - c

    def body(xbc_ref, z_ref, sm_ref, nw_ref, al_ref, db_ref, dv_ref, hist_ref, dy_ref,
             dxbc_ref, dz_ref, dsm_ref, dnw_ref, dal_ref, ddb_ref, ddv_ref, dstate_ref):
        c, g = pl.program_id(0), pl.program_id(1)

        @pl.when(c == 0)
        def _():
            dstate_ref[g] = jnp.zeros((SSD_GW, SSD_STATE), F32)

        fn = functools.partial(ssd_chunk, g)
        _, vjp = jax.vjp(fn, *_ssd_parts(xbc_ref), z_ref[...], sm_ref[...], nw_ref[...], al_ref[...], db_ref[...],
                         dv_ref[...], hist_ref[0, 0])
        dxs, dbm, dcm, dz, dsm, dnw, dal, ddb, ddv, dstate = vjp((dy_ref[...], dstate_ref[g]))
        dxbc_ref[:, :SSD_GW] = dxs
        dxbc_ref[:, SSD_GW:SSD_GW + SSD_STATE] = dbm
        dxbc_ref[:, SSD_GW + SSD_STATE:] = dcm
        dz_ref[...] = dz
        dstate_ref[g] = dstate
        _accumulate(dsm_ref, g == 0, dsm)
        first = jnp.logical_and(c == 0, g == 0)
        _accumulate(dal_ref, first, dal)
        _accumulate(ddb_ref, first, ddb)
        _accumulate(ddv_ref, first, ddv)

        @pl.when(c == 0)
        def _():
            dnw_ref[g] = dnw

        @pl.when(c > 0)
        def _():
            dnw_ref[g] += dnw

    return pl.pallas_call(
        body, name="ssd_bwd", grid=(nc, SSD_GROUPS),
        in_specs=[pl.BlockSpec((CHUNK, SSD_GC), lambda c, g: (rev(c), g)),
                  pl.BlockSpec((CHUNK, SSD_GW), lambda c, g: (rev(c), COL_Z // SSD_GW + g)),
                  pl.BlockSpec((CHUNK, LANES), lambda c, g: (rev(c), 0)),
                  pl.BlockSpec((1, SSD_GW), lambda c, g: (0, g)), _full((1, LANES)), _full((1, LANES)), _full((1, LANES)),
                  pl.BlockSpec((1, 1, SSD_GW, SSD_STATE), lambda c, g: (rev(c), g, 0, 0)),
                  pl.BlockSpec((CHUNK, SSD_GW), lambda c, g: (rev(c), g))],
        out_specs=[pl.BlockSpec((CHUNK, SSD_GC), lambda c, g: (rev(c), g)),
                   pl.BlockSpec((CHUNK, SSD_GW), lambda c, g: (rev(c), COL_Z // SSD_GW + g)),
                   pl.BlockSpec((CHUNK, LANES), lambda c, g: (rev(c), 0)),
                   _full((SSD_GROUPS, 1, SSD_GW)), _full((1, LANES)), _full((1, LANES)), _full((1, LANES))],
        out_shape=[jax.ShapeDtypeStruct((t, SSD_CONV), F32), jax.ShapeDtypeStruct((t, MAIN), F32),
                   jax.ShapeDtypeStruct((t, LANES), F32), jax.ShapeDtypeStruct((SSD_GROUPS, 1, SSD_GW), F32),
                   jax.ShapeDtypeStruct((1, LANES), F32), jax.ShapeDtypeStruct((1, LANES), F32),
                   jax.ShapeDtypeStruct((1, LANES), F32)],
        scratch_shapes=[pltpu.VMEM((SSD_GROUPS, SSD_GW, SSD_STATE), F32)],
        compiler_params=_params(("arbitrary", "arbitrary")),
    )(conv_ssd, proj_main, proj_small, normw, alog, dtb, dvec, hist, dy)


def gdn_fwd(conv_gdn, proj_main, proj_small, normw, alog, dtb):
    t = conv_gdn.shape[0]
    nc = t // CHUNK

    hb = GDN_HB
    gate_blk = COL_GATE // (GDN_DV * hb)

    def body(qkv_ref, gate_ref, sm_ref, nw_ref, al_ref, db_ref, y_ref, hist_ref, t_ref, state_ref):
        h0 = _first_head()

        @pl.when(pl.program_id(0) == 0)
        def _():
            for j in range(hb):
                state_ref[h0 + j] = jnp.zeros((GDN_DK, GDN_DV), F32)

        states = [state_ref[h0 + j] for j in range(hb)]
        for j in range(hb):
            hist_ref[0, j] = states[j]
        qs, ks, vs = _gdn_parts(qkv_ref)
        ys, new_states, ts = gdn_chunk(h0, qs, ks, vs, sm_ref[...], _head_cols(gate_ref), nw_ref[...], al_ref[...],
                                       db_ref[...], states)
        for j in range(hb):
            y_ref[:, j * GDN_DV:(j + 1) * GDN_DV] = ys[j].astype(MXU_DTYPE)
            state_ref[h0 + j] = new_states[j]
            t_ref[0, j] = ts[j]

    return pl.pallas_call(
        body, name="gdn_fwd", grid=(nc, GDN_HEADS // hb),
        in_specs=[pl.BlockSpec((CHUNK, GDN_HC * hb), lambda c, h: (c, h)),
                  pl.BlockSpec((CHUNK, GDN_DV * hb), lambda c, h: (c, gate_blk + h)),
                  pl.BlockSpec((CHUNK, LANES), lambda c, h: (c, 0)),
                  _full((1, GDN_DV)), _full((1, LANES)), _full((1, LANES))],
        out_specs=[pl.BlockSpec((CHUNK, GDN_DV * hb), lambda c, h: (c, h)),
                   pl.BlockSpec((1, hb, GDN_DK, GDN_DV), lambda c, h: (c, h, 0, 0)),
                   pl.BlockSpec((1, hb, CHUNK, CHUNK), lambda c, h: (c, h, 0, 0))],
        out_shape=[jax.ShapeDtypeStruct((t, GDN_W), MXU_DTYPE),
                   jax.ShapeDtypeStruct((nc, GDN_HEADS, GDN_DK, GDN_DV), F32),
                   jax.ShapeDtypeStruct((nc, GDN_HEADS, CHUNK, CHUNK), F32)],
        scratch_shapes=[pltpu.VMEM((GDN_HEADS, GDN_DK, GDN_DV), F32)],
        compiler_params=_params(("arbitrary", "arbitrary")),
    )(conv_gdn, proj_main, proj_small, normw, alog, dtb)


def gdn_bwd(dproj_main, conv_gdn, proj_main, proj_small, normw, alog, dtb, hist, t_inv, dy):
    t = conv_gdn.shape[0]
    nc = t // CHUNK
    rev = lambda c: nc - 1 - c
    hb = GDN_HB
    gate_blk = COL_GATE // (GDN_DV * hb)

    def body(alias_ref, qkv_ref, gate_ref, sm_ref, nw_ref, al_ref, db_ref, hist_ref, t_ref, dy_ref,
             dgate_ref, dqkv_ref, dsm_ref, dnw_ref, dal_ref, ddb_ref, dstate_ref):
        del alias_ref
        c, h = pl.program_id(0), pl.program_id(1)
        h0 = _first_head()

        @pl.when(c == 0)
        def _():
            for j in range(hb):
                dstate_ref[h0 + j] = jnp.zeros((GDN_DK, GDN_DV), F32)

        saved = [t_ref[0, j] for j in range(hb)]

        def fn(qs, ks, vs, small, gates, nw, al, db, states):
            return gdn_chunk(h0, qs, ks, vs, small, gates, nw, al, db, states, saved)[:2]

        qs, ks, vs = _gdn_parts(qkv_ref)
        _, vjp = jax.vjp(fn, qs, ks, vs, sm_ref[...], _head_cols(gate_ref), nw_ref[...], al_ref[...], db_ref[...],
                         [hist_ref[0, j] for j in range(hb)])
        dqs, dks, dvs, dsm, dgates, dnw, dal, ddb, dstates = vjp(
            (_head_cols(dy_ref), [dstate_ref[h0 + j] for j in range(hb)]))
        for j in range(hb):
            base = j * GDN_HC
            dqkv_ref[:, base:base + GDN_DK] = dqs[j]
            dqkv_ref[:, base + GDN_DK:base + 2 * GDN_DK] = dks[j]
            dqkv_ref[:, base + 2 * GDN_DK:base + GDN_HC] = dvs[j]
            dgate_ref[:, j * GDN_DV:(j + 1) * GDN_DV] = dgates[j]
            dstate_ref[h0 + j] = dstates[j]
        _accumulate(dsm_ref, h == 0, dsm)
        first = jnp.logical_and(c == 0, h == 0)
        _accumulate(dnw_ref, first, dnw)
        _accumulate(dal_ref, first, dal)
        _accumulate(ddb_ref, first, ddb)

    return pl.pallas_call(
        body, name="gdn_bwd", grid=(nc, GDN_HEADS // hb),
        in_specs=[ANY, pl.BlockSpec((CHUNK, GDN_HC * hb), lambda c, h: (rev(c), h)),
                  pl.BlockSpec((CHUNK, GDN_DV * hb), lambda c, h: (rev(c), gate_blk + h)),
                  pl.BlockSpec((CHUNK, LANES), lambda c, h: (rev(c), 0)),
                  _full((1, GDN_DV)), _full((1, LANES)), _full((1, LANES)),
                  pl.BlockSpec((1, hb, GDN_DK, GDN_DV), lambda c, h: (rev(c), h, 0, 0)),
                  pl.BlockSpec((1, hb, CHUNK, CHUNK), lambda c, h: (rev(c), h, 0, 0)),
                  pl.BlockSpec((CHUNK, GDN_DV * hb), lambda c, h: (rev(c), h))],
        out_specs=[pl.BlockSpec((CHUNK, GDN_DV * hb), lambda c, h: (rev(c), gate_blk + h)),
                   pl.BlockSpec((CHUNK, GDN_HC * hb), lambda c, h: (rev(c), h)),
                   pl.BlockSpec((CHUNK, LANES), lambda c, h: (rev(c), 0)),
                   _full((1, GDN_DV)), _full((1, LANES)), _full((1, LANES))],
        out_shape=[jax.ShapeDtypeStruct(dproj_main.shape, F32), jax.ShapeDtypeStruct((t, GDN_CONV), F32),
                   jax.ShapeDtypeStruct((t, LANES), F32), jax.ShapeDtypeStruct((1, GDN_DV), F32),
                   jax.ShapeDtypeStruct((1, LANES), F32), jax.ShapeDtypeStruct((1, LANES), F32)],
        scratch_shapes=[pltpu.VMEM((GDN_HEADS, GDN_DK, GDN_DV), F32)],
        input_output_aliases={0: 0},
        compiler_params=_params(("arbitrary", "arbitrary")),
    )(dproj_main, conv_gdn, proj_main, proj_small, normw, alog, dtb, hist, t_inv, dy)


def out_proj_loss(x, y_ssd, y_gdn, w_out, final_w, target):
    t = x.shape[0]
    tm = min(256, t)

    def body(x_ref, ys_ref, yg_ref, wo_ref, fw_ref, tg_ref, loss_ref, dhid_ref, dys_ref, dyg_ref, dwo_ref, dfw_ref):
        i = pl.program_id(0)
        ys, yg = ys_ref[...], yg_ref[...]
        wo_s, wo_g = wo_ref[:SSD_WIDTH, :], wo_ref[SSD_WIDTH:, :]
        hid = x_ref[...] + _raw_dot(ys, wo_s, 1, 0) + _raw_dot(yg, wo_g, 1, 0)
        out, vjp = jax.vjp(rmsnorm, hid, fw_ref[...])
        err = out - tg_ref[...]
        loss = 0.5 * jnp.sum(jnp.mean(err * err, axis=-1, keepdims=True), axis=0, keepdims=True)
        dhid, dfw = vjp(err * (1.0 / D_MODEL))
        dhid_ref[...] = dhid
        dys_ref[...] = _raw_dot(dhid, wo_s, 1, 1)
        dyg_ref[...] = _raw_dot(dhid, wo_g, 1, 1)
        first = i == 0
        _accumulate(loss_ref, first, jnp.broadcast_to(loss, loss_ref.shape))
        _accumulate(dfw_ref, first, dfw)

        @pl.when(first)
        def _():
            dwo_ref[:SSD_WIDTH, :] = _raw_dot(ys, dhid, 0, 0)
            dwo_ref[SSD_WIDTH:, :] = _raw_dot(yg, dhid, 0, 0)

        @pl.when(i > 0)
        def _():
            dwo_ref[:SSD_WIDTH, :] += _raw_dot(ys, dhid, 0, 0)
            dwo_ref[SSD_WIDTH:, :] += _raw_dot(yg, dhid, 0, 0)

    row = lambda w: pl.BlockSpec((tm, w), lambda i: (i, 0))
    return pl.pallas_call(
        body, name="out_proj_loss", grid=(t // tm,),
        in_specs=[row(D_MODEL), row(SSD_WIDTH), row(GDN_W), _full((SSD_WIDTH + GDN_W, D_MODEL)), _full((1, D_MODEL)),
                  row(D_MODEL)],
        out_specs=[_full((8, LANES)), row(D_MODEL), row(SSD_WIDTH), row(GDN_W), _full((SSD_WIDTH + GDN_W, D_MODEL)),
                   _full((1, D_MODEL))],
        out_shape=[jax.ShapeDtypeStruct((8, LANES), F32), jax.ShapeDtypeStruct((t, D_MODEL), F32),
                   jax.ShapeDtypeStruct((t, SSD_WIDTH), F32), jax.ShapeDtypeStruct((t, GDN_W), F32),
                   jax.ShapeDtypeStruct((SSD_WIDTH + GDN_W, D_MODEL), F32), jax.ShapeDtypeStruct((1, D_MODEL), F32)],
        compiler_params=_params(("arbitrary",)),
    )(x, y_ssd, y_gdn, w_out, final_w, target)


def in_proj_bwd_x(x, normw, w_main, w_small, dproj_main, dsmall_a, dsmall_b, dhid):
    t = x.shape[0]
    tm, tk = min(512, t), 512
    nk = MAIN // tk

    def body(x_ref, nw_ref, wm_ref, ws_ref, dp_ref, da_ref, db_ref, dh_ref, gx_ref, dnw_ref, acc_ref):
        i, k = pl.program_id(0), pl.program_id(1)
        part = _raw_dot(dp_ref[...], wm_ref[...], 1, 1)

        @pl.when(k == 0)
        def _():
            acc_ref[...] = part + _raw_dot(da_ref[...] + db_ref[...], ws_ref[...], 1, 1)

        @pl.when(k > 0)
        def _():
            acc_ref[...] += part

        @pl.when(k == nk - 1)
        def _():
            _, vjp = jax.vjp(rmsnorm, x_ref[...], nw_ref[...])
            dx, dnw = vjp(acc_ref[...])
            gx_ref[...] = dx + dh_ref[...]
            _accumulate(dnw_ref, i == 0, dnw)

    row = lambda w: pl.BlockSpec((tm, w), lambda i, k: (i, 0))
    return pl.pallas_call(
        body, name="in_proj_bwd_x", grid=(t // tm, nk),
        in_specs=[row(D_MODEL), _full((1, D_MODEL)), pl.BlockSpec((D_MODEL, tk), lambda i, k: (0, k)),
                  _full((D_MODEL, LANES)), pl.BlockSpec((tm, tk), lambda i, k: (i, k)), row(LANES), row(LANES),
                  row(D_MODEL)],
        out_specs=[row(D_MODEL), _full((1, D_MODEL))],
        out_shape=[jax.ShapeDtypeStruct((t, D_MODEL), F32), jax.ShapeDtypeStruct((1, D_MODEL), F32)],
        scratch_shapes=[pltpu.VMEM((tm, D_MODEL), F32)],
        compiler_params=_params(("arbitrary", "arbitrary")),
    )(x, normw, w_main, w_small, dproj_main, dsmall_a, dsmall_b, dhid)


def in_proj_bwd_w(u, dproj_main, dsmall_a, dsmall_b):
    t = u.shape[0]
    tm, tn = min(512, t), 512

    def body(u_ref, dp_ref, da_ref, db_ref, dwm_ref, dws_ref):
        j, i = pl.program_id(0), pl.program_id(1)
        uu = u_ref[...]
        _accumulate(dwm_ref, i == 0, _raw_dot(uu, dp_ref[...], 0, 0))

        @pl.when(j == 0)
        def _():
            _accumulate(dws_ref, i == 0, _raw_dot(uu, da_ref[...] + db_ref[...], 0, 0))

    return pl.pallas_call(
        body, name="in_proj_bwd_w", grid=(MAIN // tn, t // tm),
        in_specs=[pl.BlockSpec((tm, D_MODEL), lambda j, i: (i, 0)), pl.BlockSpec((tm, tn), lambda j, i: (i, j)),
                  pl.BlockSpec((tm, LANES), lambda j, i: (i, 0)), pl.BlockSpec((tm, LANES), lambda j, i: (i, 0))],
        out_specs=[pl.BlockSpec((D_MODEL, tn), lambda j, i: (0, j)), _full((D_MODEL, LANES))],
        out_shape=[jax.ShapeDtypeStruct((D_MODEL, MAIN), F32), jax.ShapeDtypeStruct((D_MODEL, LANES), F32)],
        compiler_params=_params(("arbitrary", "arbitrary")),
    )(u, dproj_main, dsmall_a, dsmall_b)


def sum_slabs(a, name):
    n, rows, cols = a.shape
    tr = 64 if rows % 64 == 0 else rows

    def body(a_ref, o_ref):
        acc = a_ref[0].astype(F32)
        for d in range(1, n):
            acc = acc + a_ref[d].astype(F32)
        o_ref[...] = acc

    return pl.pallas_call(
        body, name=name, grid=(rows // tr,),
        in_specs=[pl.BlockSpec((n, tr, cols), lambda i: (0, i, 0))],
        out_specs=pl.BlockSpec((tr, cols), lambda i: (i, 0)),
        out_shape=jax.ShapeDtypeStruct((rows, cols), F32),
        compiler_params=_params(("arbitrary",)),
    )(a)


def adamw(w, g, m, v, name):
    rows, cols = w.shape
    tr = 128 if rows % 128 == 0 else rows

    def body(w_ref, g_ref, m_ref, v_ref, d_ref, nm_ref, nv_ref):
        gg = g_ref[...]
        nm = ADAM_B1 * m_ref[...] + (1.0 - ADAM_B1) * gg
        nv = ADAM_B2 * v_ref[...] + (1.0 - ADAM_B2) * (gg * gg)
        m_hat = nm / (1.0 - ADAM_B1 ** ADAM_STEP)
        v_hat = nv / (1.0 - ADAM_B2 ** ADAM_STEP)
        d_ref[...] = -ADAM_LR * (m_hat / (jnp.sqrt(v_hat) + ADAM_EPS) + ADAM_WD * w_ref[...])
        nm_ref[...] = nm
        nv_ref[...] = nv

    spec = pl.BlockSpec((tr, cols), lambda i: (i, 0))
    shp = jax.ShapeDtypeStruct((rows, cols), F32)
    return pl.pallas_call(
        body, name=name, grid=(rows // tr,), in_specs=[spec] * 4, out_specs=[spec] * 3, out_shape=[shp] * 3,
        compiler_params=_params(("arbitrary",)),
    )(w, g, m, v)


def _my_place():
    return lax.axis_index("x"), lax.axis_index("y"), lax.axis_index("c")


def gather_weights(shards):
    n = len(shards)

    def body(*refs):
        srcs, outs = refs[:n], refs[n:2 * n]
        send_sems, recv_sems, local_sems = refs[2 * n:]
        x, y, c = _my_place()
        me = 2 * x + y
        chips = [(1 - x, y), (x, 1 - y), (1 - x, 1 - y)]
        local = [pltpu.make_async_copy(srcs[i], outs[i].at[me], local_sems.at[i]) for i in range(n)]
        for cp in local:
            cp.start()
        sends = []
        for j, (px, py) in enumerate(chips):
            for i in range(n):
                sends.append(pltpu.make_async_remote_copy(
                    src_ref=srcs[i], dst_ref=outs[i].at[me], send_sem=send_sems.at[j * n + i],
                    recv_sem=recv_sems.at[j * n + i], device_id=(px, py, c), device_id_type=MESH))
        for cp in sends:
            cp.start()
        for j, (px, py) in enumerate(chips):
            for i in range(n):
                pltpu.make_async_remote_copy(
                    src_ref=srcs[i], dst_ref=outs[i].at[2 * px + py], send_sem=send_sems.at[j * n + i],
                    recv_sem=recv_sems.at[j * n + i], device_id=(px, py, c), device_id_type=MESH).wait_recv()
        for cp in sends:
            cp.wait_send()
        for cp in local:
            cp.wait()

    return pl.pallas_call(
        body, name="gather_weights",
        in_specs=[HBM] * n, out_specs=[HBM] * n,
        out_shape=[jax.ShapeDtypeStruct((N_CHIP,) + s.shape, s.dtype) for s in shards],
        scratch_shapes=[pltpu.SemaphoreType.DMA((3 * n,)), pltpu.SemaphoreType.DMA((3 * n,)),
                        pltpu.SemaphoreType.DMA((n,))],
    )(*shards)


def _peer(x, y, c, mask):
    mx, my, mc = (mask >> 2) & 1, (mask >> 1) & 1, mask & 1
    return (x ^ mx if mx else x, y ^ my if my else y, c ^ mc if mc else c)


def exchange_slabs(slabbed, replicated):
    ns, nr = len(slabbed), len(replicated)
    n = ns + nr

    def body(*refs):
        srcs, outs = refs[:n], refs[n:2 * n]
        send_sems, recv_sems, local_sems = refs[2 * n:]
        x, y, c = _my_place()
        me = 4 * x + 2 * y + c

        def piece(i, dev):
            return srcs[i].at[dev] if i < ns else srcs[i]

        local = [pltpu.make_async_copy(piece(i, me), outs[i].at[me], local_sems.at[i]) for i in range(n)]
        for cp in local:
            cp.start()
        sends = []
        for mask in range(1, N_DEV):
            px, py, pc = _peer(x, y, c, mask)
            dev = 4 * px + 2 * py + pc
            for i in range(n):
                k = (mask - 1) * n + i
                sends.append(pltpu.make_async_remote_copy(
                    src_ref=piece(i, dev), dst_ref=outs[i].at[me], send_sem=send_sems.at[k], recv_sem=recv_sems.at[k],
                    device_id=(px, py, pc), device_id_type=MESH))
        for cp in sends:
            cp.start()
        for mask in range(1, N_DEV):
            px, py, pc = _peer(x, y, c, mask)
            dev = 4 * px + 2 * py + pc
            for i in range(n):
                k = (mask - 1) * n + i
                pltpu.make_async_remote_copy(
                    src_ref=piece(i, dev), dst_ref=outs[i].at[dev], send_sem=send_sems.at[k], recv_sem=recv_sems.at[k],
                    device_id=(px, py, pc), device_id_type=MESH).wait_recv()
        for cp in sends:
            cp.wait_send()
        for cp in local:
            cp.wait()

    shapes = [jax.ShapeDtypeStruct(a.shape, a.dtype) for a in slabbed]
    shapes += [jax.ShapeDtypeStruct((N_DEV,) + a.shape, a.dtype) for a in replicated]
    return pl.pallas_call(
        body, name="exchange_slabs",
        in_specs=[HBM] * n, out_specs=[HBM] * n, out_shape=shapes,
        scratch_shapes=[pltpu.SemaphoreType.DMA((7 * n,)), pltpu.SemaphoreType.DMA((7 * n,)),
                        pltpu.SemaphoreType.DMA((n,))],
    )(*slabbed, *replicated)


def exchange_halves(halves):
    n = len(halves)

    def body(*refs):
        srcs, outs = refs[:n], refs[n:2 * n]
        send_sems, recv_sems, local_sems = refs[2 * n:]
        x, y, c = _my_place()
        local = [pltpu.make_async_copy(srcs[i], outs[i].at[c], local_sems.at[i]) for i in range(n)]
        for cp in local:
            cp.start()
        sends = [pltpu.make_async_remote_copy(
            src_ref=srcs[i], dst_ref=outs[i].at[c], send_sem=send_sems.at[i], recv_sem=recv_sems.at[i],
            device_id=(x, y, 1 - c), device_id_type=MESH) for i in range(n)]
        for cp in sends:
            cp.start()
        for i in range(n):
            pltpu.make_async_remote_copy(
                src_ref=srcs[i], dst_ref=outs[i].at[1 - c], send_sem=send_sems.at[i], recv_sem=recv_sems.at[i],
                device_id=(x, y, 1 - c), device_id_type=MESH).wait_recv()
        for cp in sends:
            cp.wait_send()
        for cp in local:
            cp.wait()

    return pl.pallas_call(
        body, name="exchange_halves",
        in_specs=[HBM] * n, out_specs=[HBM] * n,
        out_shape=[jax.ShapeDtypeStruct((2,) + a.shape, a.dtype) for a in halves],
        scratch_shapes=[pltpu.SemaphoreType.DMA((n,)), pltpu.SemaphoreType.DMA((n,)), pltpu.SemaphoreType.DMA((n,))],
    )(*halves)


def _pack(arrays):
    flat = []
    for a in arrays:
        v = a.reshape(-1).astype(F32)
        flat.append(jnp.pad(v, (0, (-v.shape[0]) % LANES)))
    v = jnp.concatenate(flat)
    v = jnp.pad(v, (0, (-v.shape[0]) % (8 * LANES)))
    return v.reshape(-1, LANES)


def _unpack(packed, shapes):
    v, out, pos = packed.reshape(-1), [], 0
    for s in shapes:
        n = int(np.prod(s))
        out.append(v[pos:pos + n].reshape(s))
        pos += n + (-n) % LANES
    return out


def _lanes(vec, start):
    n = vec.shape[-1]
    return jnp.pad(vec.reshape(1, n).astype(F32), ((0, 0), (start, LANES - start - n)))


def kernel(x, norm_w, w_in, ssd_conv_w, ssd_conv_b, ssd_dt_bias, ssd_a_log, ssd_d, ssd_norm_w, gdn_conv_w, gdn_dt_bias, gdn_a_log, gdn_norm_w, w_out, final_norm_w, loss_target, m_norm_w, m_w_in, m_ssd_conv_w, m_ssd_conv_b, m_ssd_dt_bias, m_ssd_a_log, m_ssd_d, m_ssd_norm_w, m_gdn_conv_w, m_gdn_dt_bias, m_gdn_a_log, m_gdn_norm_w, m_w_out, m_final_norm_w, v_norm_w, v_w_in, v_ssd_conv_w, v_ssd_conv_b, v_ssd_dt_bias, v_ssd_a_log, v_ssd_d, v_ssd_norm_w, v_gdn_conv_w, v_gdn_dt_bias, v_gdn_a_log, v_gdn_norm_w, v_w_out, v_final_norm_w):
    xs = x[0]
    target = loss_target[0]
    chip = 2 * lax.axis_index("x") + lax.axis_index("y")
    w_in_shard, w_out_shard = w_in[0], w_out[0]
    in_cols = w_in_shard.shape[1]
    out_rows = w_out_shard.shape[0]

    g_in, g_out, g_cs, g_cg = gather_weights(
        [w_in_shard.astype(MXU_DTYPE), w_out_shard.astype(MXU_DTYPE), ssd_conv_w[0], gdn_conv_w[0]])
    w_in_full = jnp.transpose(g_in, (1, 0, 2)).reshape(D_MODEL, IN_DIM)
    w_out_full = g_out.reshape(N_CHIP * out_rows, D_MODEL)
    cw_ssd = _take(jnp.transpose(g_cs, (1, 0, 2)).reshape(4, SSD_CONV), _ssd_conv_segments())
    cw_gdn = _take(jnp.transpose(g_cg, (1, 0, 2)).reshape(4, GDN_CONV), _gdn_conv_segments())
    cb_ssd = _take(ssd_conv_b, _ssd_conv_segments())
    cb_gdn = jnp.zeros((1, GDN_CONV), F32)
    w_main = _take(w_in_full, _main_segments())
    w_small = jnp.concatenate([w_in_full[:, 2560:2576], w_in_full[:, 6672:6688],
                               jnp.zeros((D_MODEL, LANES - 32), MXU_DTYPE)], axis=1)
    alog = _lanes(ssd_a_log, 0) + _lanes(gdn_a_log, LANE_GA)
    dtb = _lanes(ssd_dt_bias, 0) + _lanes(gdn_dt_bias, LANE_GA)
    dvec = _lanes(ssd_d, 0)
    fw = final_norm_w.reshape(1, D_MODEL)

    proj_main, proj_small, u = in_proj(xs, norm_w, w_main, w_small)
    conv_ssd = conv_fwd(proj_main, COL_SSD, SSD_CONV, cw_ssd, cb_ssd, "conv_fwd_ssd")
    conv_gdn = conv_fwd(proj_main, COL_GDN, GDN_CONV, cw_gdn, cb_gdn, "conv_fwd_gdn")
    y_ssd, hist_ssd = ssd_fwd(conv_ssd, proj_main, proj_small, ssd_norm_w, alog, dtb, dvec)
    y_gdn, hist_gdn, tinv_gdn = gdn_fwd(conv_gdn, proj_main, proj_small, gdn_norm_w, alog, dtb)

    loss_blk, dhid, dy_ssd, dy_gdn, d_w_out, d_fw = out_proj_loss(xs, y_ssd, y_gdn, w_out_full, fw, target)
    dconv_ssd, dproj_main, dsmall_ssd, d_ssd_nw, d_alog_s, d_dtb_s, d_dvec = ssd_bwd(
        conv_ssd, proj_main, proj_small, ssd_norm_w, alog, dtb, dvec, hist_ssd, dy_ssd)
    dproj_main, dconv_gdn, dsmall_gdn, d_gdn_nw, d_alog_g, d_dtb_g = gdn_bwd(
        dproj_main, conv_gdn, proj_main, proj_small, gdn_norm_w, alog, dtb, hist_gdn, tinv_gdn, dy_gdn)
    dpre_ssd, dwb_ssd = conv_bwd_pre(proj_main, COL_SSD, SSD_CONV, cw_ssd, cb_ssd, dconv_ssd, "conv_bwd_pre_ssd")
    dpre_gdn, dwb_gdn = conv_bwd_pre(proj_main, COL_GDN, GDN_CONV, cw_gdn, cb_gdn, dconv_gdn, "conv_bwd_pre_gdn")
    dproj_main = conv_bwd_x(dproj_main, COL_SSD, SSD_CONV, cw_ssd, dpre_ssd, "conv_bwd_x_ssd")
    dproj_main = conv_bwd_x(dproj_main, COL_GDN, GDN_CONV, cw_gdn, dpre_gdn, "conv_bwd_x_gdn")
    grad_x, d_norm_w = in_proj_bwd_x(xs, norm_w, w_main, w_small, dproj_main, dsmall_ssd, dsmall_gdn, dhid)
    d_w_main, d_w_small = in_proj_bwd_w(u, dproj_main, dsmall_ssd, dsmall_gdn)

    d_w_in = _untake(d_w_main, _main_segments())
    d_w_in = jnp.concatenate([d_w_in[:, :2560], d_w_small[:, 0:16], d_w_in[:, 2560:], d_w_small[:, 16:32]], axis=1)
    d_w_in = jnp.transpose(d_w_in.reshape(D_MODEL, N_CHIP, in_cols), (1, 0, 2))
    d_alog, d_dtb = d_alog_s + d_alog_g, d_dtb_s + d_dtb_g
    small_grads = [
        loss_blk[0:1, 0:1],
        d_norm_w,
        _untake(dwb_ssd[0:4], _ssd_conv_segments()),
        _untake(dwb_ssd[4:5], _ssd_conv_segments()),
        d_dtb[:, 0:SSD_HEADS], d_alog[:, 0:SSD_HEADS], d_dvec[:, 0:SSD_HEADS],
        d_ssd_nw.reshape(1, SSD_WIDTH),
        _untake(dwb_gdn[0:4], _gdn_conv_segments()),
        d_dtb[:, LANE_GA:LANE_GA + GDN_HEADS], d_alog[:, LANE_GA:LANE_GA + GDN_HEADS],
        d_gdn_nw, d_fw,
    ]
    small_shapes = [a.shape for a in small_grads]

    r_in, r_out, r_small = exchange_slabs(
        [d_w_in.reshape(N_DEV, D_MODEL // 2, in_cols).astype(COMM_DTYPE),
         d_w_out.reshape(N_DEV, out_rows // 2, D_MODEL).astype(COMM_DTYPE)],
        [_pack(small_grads)])
    half_in = sum_slabs(r_in, "sum_w_in")
    half_out = sum_slabs(r_out, "sum_w_out")
    small_sum = sum_slabs(r_small, "sum_small")
    full_in, full_out = exchange_halves([half_in, half_out])
    grad_w_in = full_in.reshape(D_MODEL, in_cols)
    grad_w_out = full_out.reshape(out_rows, D_MODEL)
    (loss, g_norm_w, g_ssd_cw, g_ssd_cb, g_ssd_dtb, g_ssd_alog, g_ssd_d, g_ssd_nw, g_gdn_cw, g_gdn_dtb, g_gdn_alog,
     g_gdn_nw, g_fw) = _unpack(small_sum, small_shapes)
    sc, gc = ssd_conv_w.shape[2], gdn_conv_w.shape[2]
    g_ssd_cw = lax.dynamic_slice_in_dim(g_ssd_cw, chip * sc, sc, axis=1)
    g_gdn_cw = lax.dynamic_slice_in_dim(g_gdn_cw, chip * gc, gc, axis=1)

    names = ["norm_w", "ssd_conv_w", "ssd_conv_b", "ssd_dt_bias", "ssd_a_log", "ssd_d", "ssd_norm_w", "gdn_conv_w",
             "gdn_dt_bias", "gdn_a_log", "gdn_norm_w", "final_norm_w"]
    ws = [norm_w, ssd_conv_w, ssd_conv_b, ssd_dt_bias, ssd_a_log, ssd_d, ssd_norm_w, gdn_conv_w, gdn_dt_bias,
          gdn_a_log, gdn_norm_w, final_norm_w]
    ms = [m_norm_w, m_ssd_conv_w, m_ssd_conv_b, m_ssd_dt_bias, m_ssd_a_log, m_ssd_d, m_ssd_norm_w, m_gdn_conv_w,
          m_gdn_dt_bias, m_gdn_a_log, m_gdn_norm_w, m_final_norm_w]
    vs = [v_norm_w, v_ssd_conv_w, v_ssd_conv_b, v_ssd_dt_bias, v_ssd_a_log, v_ssd_d, v_ssd_norm_w, v_gdn_conv_w,
          v_gdn_dt_bias, v_gdn_a_log, v_gdn_norm_w, v_final_norm_w]
    gs = [g_norm_w, g_ssd_cw, g_ssd_cb, g_ssd_dtb, g_ssd_alog, g_ssd_d, g_ssd_nw, g_gdn_cw, g_gdn_dtb, g_gdn_alog,
          g_gdn_nw, g_fw]
    gs = [g.reshape(w.shape) for g, w in zip(gs, ws)]
    shapes = [w.shape for w in ws]
    d_s, m_s, v_s = adamw(_pack(ws), _pack(gs), _pack(ms), _pack(vs), "adamw_small")
    delta = dict(zip(names, _unpack(d_s, shapes)))
    new_m = dict(zip(names, _unpack(m_s, shapes)))
    new_v = dict(zip(names, _unpack(v_s, shapes)))
    grads = dict(zip(names, gs))
    d_in, m_in, v_in = adamw(w_in_shard, grad_w_in, m_w_in[0], v_w_in[0], "adamw_w_in")
    d_out, m_out, v_out = adamw(w_out_shard, grad_w_out, m_w_out[0], v_w_out[0], "adamw_w_out")
    for tbl, a_in, a_out in ((grads, grad_w_in, grad_w_out), (delta, d_in, d_out), (new_m, m_in, m_out),
                             (new_v, v_in, v_out)):
        tbl["w_in"] = a_in[None]
        tbl["w_out"] = a_out[None]

    order = ["norm_w", "w_in", "ssd_conv_w", "ssd_conv_b", "ssd_dt_bias", "ssd_a_log", "ssd_d", "ssd_norm_w",
             "gdn_conv_w", "gdn_dt_bias", "gdn_a_log", "gdn_norm_w", "w_out", "final_norm_w"]
    return (loss.reshape(()), grad_x[None], *[grads[k] for k in order], *[delta[k] for k in order],
            *[new_m[k] for k in order], *[new_v[k] for k in order])
```

```python
import functools

import numpy as np
import jax
import jax.numpy as jnp
from jax import lax
from jax.experimental import pallas as pl
from jax.experimental.pallas import tpu as pltpu

F32 = jnp.float32
MXU_DTYPE = jnp.bfloat16
COMM_DTYPE = jnp.bfloat16
MESH = pl.DeviceIdType.MESH

D_MODEL = 1024
CHUNK = 64
EPS = 1e-6
SSD_HEADS, SSD_GROUPS, SSD_STATE = 16, 2, 128
SSD_WIDTH, SSD_CONV = 1024, 1536
SSD_GW = SSD_WIDTH // SSD_GROUPS
SSD_GC = SSD_GW + 2 * SSD_STATE
GDN_HEADS, GDN_DK, GDN_DV = 8, 128, 128
GDN_W, GDN_CONV = 1024, 3072
GDN_HC = 2 * GDN_DK + GDN_DV
IN_DIM = 6688
MAIN = 6656
LANES = 128
COL_Z, COL_GATE, COL_SSD, COL_GDN = 0, 1024, 2048, 3584
GDN_HB = 8
LANE_GA, LANE_GB = 16, 24
N_DEV, N_CHIP = 8, 4
VMEM_LIMIT = 52 * 1024 * 1024

ADAM_LR, ADAM_B1, ADAM_B2, ADAM_EPS, ADAM_WD, ADAM_STEP = 0.001, 0.9, 0.999, 1e-08, 0.01, 10


def _ssd_conv_segments():
    segs = []
    for g in range(SSD_GROUPS):
        segs.append((g * SSD_GW, SSD_GW))
        segs.append((SSD_WIDTH + g * SSD_STATE, SSD_STATE))
        segs.append((SSD_WIDTH + SSD_GROUPS * SSD_STATE + g * SSD_STATE, SSD_STATE))
    return segs


def _gdn_conv_segments():
    segs = []
    for h in range(GDN_HEADS):
        segs.append((h * GDN_DK, GDN_DK))
        segs.append((GDN_W + h * GDN_DK, GDN_DK))
        segs.append((2 * GDN_W + h * GDN_DV, GDN_DV))
    return segs


def _main_segments():
    o_xbc, o_gate, o_qkv = 1024, 2576, 3600
    segs = [(0, 1024), (o_gate, 1024)]
    segs += [(o_xbc + s, w) for s, w in _ssd_conv_segments()]
    segs += [(o_qkv + s, w) for s, w in _gdn_conv_segments()]
    return segs


def _take(a, segs):
    return jnp.concatenate([a[..., s:s + w] for s, w in segs], axis=-1)


def _untake(a, segs):
    pos, placed = 0, []
    for s, w in segs:
        placed.append((s, pos, w))
        pos += w
    placed.sort()
    return jnp.concatenate([a[..., p:p + w] for _, p, w in placed], axis=-1)


def _split(a, n):
    parts, rest = [], a.astype(F32)
    for i in range(n):
        p = rest.astype(MXU_DTYPE)
        parts.append(p)
        if i < n - 1:
            rest = rest - p.astype(F32)
    return parts


def _raw_dot(a, b, ca, cb, mode="bf16"):
    d = lambda u, v: lax.dot_general(u, v, (((ca,), (cb,)), ((), ())), preferred_element_type=F32)
    if mode == "bf16":
        return d(a.astype(MXU_DTYPE), b.astype(MXU_DTYPE))
    if mode == "x3":
        (ah, al), (bh, bl) = _split(a, 2), _split(b, 2)
        return d(ah, bh) + (d(ah, bl) + d(al, bh))
    if mode == "sel_a":
        a0 = a.astype(MXU_DTYPE)
        b1, b2, b3 = _split(b, 3)
        return d(a0, b1) + (d(a0, b2) + d(a0, b3))
    assert mode == "sel_b", mode
    b0 = b.astype(MXU_DTYPE)
    a1, a2, a3 = _split(a, 3)
    return d(a1, b0) + (d(a2, b0) + d(a3, b0))


@functools.partial(jax.custom_vjp, nondiff_argnums=(2,))
def mm_nn(a, b, mode="bf16"):
    return _raw_dot(a, b, 1, 0, mode)


@functools.partial(jax.custom_vjp, nondiff_argnums=(2,))
def mm_nt(a, b, mode="bf16"):
    return _raw_dot(a, b, 1, 1, mode)


@functools.partial(jax.custom_vjp, nondiff_argnums=(2,))
def mm_tn(a, b, mode="bf16"):
    return _raw_dot(a, b, 0, 0, mode)


_SAME = {"bf16": ("bf16", "bf16"), "x3": ("x3", "x3")}
_NN_BWD = dict(_SAME, sel_a=("bf16", "sel_a"), sel_b=("sel_b", "bf16"))
_NT_BWD = dict(_SAME, sel_a=("bf16", "sel_b"), sel_b=("sel_b", "bf16"))
_TN_BWD = dict(_SAME, sel_a=("bf16", "sel_a"), sel_b=("sel_a", "bf16"))
mm_nn.defvjp(lambda a, b, m: (_raw_dot(a, b, 1, 0, m), (a, b)),
             lambda m, r, g: (mm_nt(g, r[1], _NN_BWD[m][0]), mm_tn(r[0], g, _NN_BWD[m][1])))
mm_nt.defvjp(lambda a, b, m: (_raw_dot(a, b, 1, 1, m), (a, b)),
             lambda m, r, g: (mm_nn(g, r[1], _NT_BWD[m][0]), mm_tn(g, r[0], _NT_BWD[m][1])))
mm_tn.defvjp(lambda a, b, m: (_raw_dot(a, b, 0, 0, m), (a, b)),
             lambda m, r, g: (mm_nt(r[1], g, _TN_BWD[m][0]), mm_nn(r[0], g, _TN_BWD[m][1])))


@jax.custom_jvp
def sigmoid(x):
    return 1.0 / (1.0 + jnp.exp(-x))


@sigmoid.defjvp
def _sigmoid_jvp(p, t):
    s = sigmoid(p[0])
    return s, t[0] * s * (1.0 - s)


@jax.custom_jvp
def softplus(x):
    return jnp.maximum(x, 0.0) + jnp.log(1.0 + jnp.exp(-jnp.abs(x)))


@softplus.defjvp
def _softplus_jvp(p, t):
    return softplus(p[0]), t[0] * sigmoid(p[0])


def silu(x):
    return x * sigmoid(x)


def rmsnorm(x, w):
    return x * lax.rsqrt(jnp.mean(x * x, axis=-1, keepdims=True) + EPS) * w


def _iota(shape, dim):
    return lax.broadcasted_iota(jnp.int32, shape, dim)


def _tri_inv_impl(mats):
    n = mats[0].shape[0]
    r, c = _iota((n, n), 0), _iota((n, n), 1)
    eye = jnp.where(r == c, 1.0, 0.0).astype(F32)
    blockdiag = (r >> 4) == (c >> 4)
    dot = lambda u, v: _raw_dot(u, v, 1, 0, "x3")
    each = lambda f, *ls: [f(*xs) for xs in zip(*ls)]
    dg = each(lambda a: jnp.where(blockdiag, a, 0.0), mats)
    off = each(lambda a, d: a - d, mats, dg)
    m = each(lambda d: -d, dg)
    p = each(lambda x: eye + x, m)
    pw = m
    for _ in range(3):
        pw = each(lambda x: dot(x, x), pw)
        p = each(lambda x, y: x + dot(x, y), p, pw)
    e = each(dot, p, off)
    e2 = each(lambda x: dot(x, x), e)
    q = each(lambda x: eye - x, e)
    q = each(lambda x, y: x + dot(x, y), q, e2)
    return each(dot, q, p)


def _tri_inv_bwd(ts, gs):
    x = [mm_nt(g, t, "x3") for g, t in zip(gs, ts)]
    return [-mm_tn(t, y, "x3") for t, y in zip(ts, x)]


@jax.custom_vjp
def tri_inv(mats):
    return _tri_inv_impl(mats)


def _tri_inv_fwd(mats):
    ts = _tri_inv_impl(mats)
    return ts, ts


tri_inv.defvjp(_tri_inv_fwd, lambda ts, gs: (_tri_inv_bwd(ts, gs),))


@jax.custom_vjp
def tri_inv_saved(mats, ts):
    del mats
    return ts


tri_inv_saved.defvjp(lambda mats, ts: (ts, ts),
                     lambda ts, gs: (_tri_inv_bwd(ts, gs), [jnp.zeros_like(t) for t in ts]))


def _chunk_masks():
    r, c = _iota((CHUNK, CHUNK), 0), _iota((CHUNK, CHUNK), 1)
    return r >= c, r > c, r == c


def _log_decay_cumsum(small, alog, dtb, incl):
    sp = softplus(small + dtb)
    la = -jnp.exp(alog) * sp
    tri = jnp.where(incl, 1.0, 0.0).astype(F32)
    return sp, mm_nn(tri, la, "sel_a")


def _col_of(x, lane_mask):
    return jnp.sum(jnp.where(lane_mask, x, 0.0), axis=1, keepdims=True)


def _decay_matrix(col, incl, eye):
    row = jnp.sum(jnp.where(eye, col, 0.0), axis=0, keepdims=True)
    return jnp.where(incl, jnp.exp(jnp.where(incl, col - row, 0.0)), 0.0)


def gdn_chunk(h0, qs, ks, vs, small, gates, normw, alog, dtb, states, saved_t=None):
    incl, strict, eye = _chunk_masks()
    lane = _iota((1, LANES), 1)
    last = _iota((CHUNK, 1), 0) == CHUNK - 1
    _, lac = _log_decay_cumsum(small, alog, dtb, incl)
    heads = range(len(qs))
    each = lambda f, *ls: [f(*xs) for xs in zip(*ls)]
    gc = [_col_of(lac, lane == LANE_GA + h0 + j) for j in heads]
    beta = [sigmoid(_col_of(small, lane == LANE_GB + h0 + j)) for j in heads]
    decay = each(lambda x: _decay_matrix(x, incl, eye), gc)
    gl = each(lambda x: jnp.sum(jnp.where(last, x, 0.0), axis=0, keepdims=True), gc)
    q = each(lambda x: x * lax.rsqrt(jnp.sum(x * x, axis=-1, keepdims=True) + EPS) * (GDN_DK ** -0.5), qs)
    k = each(lambda x: x * lax.rsqrt(jnp.sum(x * x, axis=-1, keepdims=True) + EPS), ks)
    kb = each(lambda x, b: x * b, k, beta)
    a = each(lambda x, y, d: jnp.where(strict, mm_nt(x, y) * d, 0.0), kb, k, decay)
    t = tri_inv(a) if saved_t is None else tri_inv_saved(a, saved_t)
    eg = each(jnp.exp, gc)
    u = each(lambda x, v, b: mm_nn(x, v * b, "x3"), t, vs, beta)
    w = each(lambda x, y, e: mm_nn(x, y * e, "x3"), t, kb, eg)
    attn = each(lambda x, y, d: mm_nt(x, y) * d, q, k, decay)
    v_new = each(lambda x, y, s: x - mm_nn(y, s), u, w, states)
    o = each(lambda x, e, s, at, vn: mm_nn(x * e, s) + mm_nn(at, vn), q, eg, states, attn, v_new)
    new_states = each(lambda s, x, y, l, c: s * jnp.exp(l) + mm_tn(y * jnp.exp(l - c), x), states, v_new, k, gl, gc)
    ys = each(lambda x, gt: rmsnorm(x, normw) * silu(gt), o, gates)
    return ys, new_states, t


@jax.custom_vjp
def split_lanes(x):
    return [x[:, i * LANES:(i + 1) * LANES] for i in range(x.shape[1] // LANES)]


@jax.custom_vjp
def join_lanes(xs):
    return jnp.concatenate(xs, axis=1)


split_lanes.defvjp(lambda x: (split_lanes(x), None), lambda _, gs: (join_lanes(gs),))
join_lanes.defvjp(lambda xs: (join_lanes(xs), None), lambda _, g: (split_lanes(g),))


def ssd_chunk(g, xs, bm, cm, z, small, normw, alog, dtb, dvec, state):
    incl, _, eye = _chunk_masks()
    lane = _iota((1, LANES), 1)
    last = _iota((CHUNK, 1), 0) == CHUNK - 1
    hpg = SSD_HEADS // SSD_GROUPS
    sp, lac = _log_decay_cumsum(small, alog, dtb, incl)
    sel = jnp.where(_iota((LANES, SSD_GW), 0) == g * hpg + (_iota((LANES, SSD_GW), 1) >> 6), 1.0, 0.0).astype(F32)
    lac_last = jnp.sum(jnp.where(last, lac, 0.0), axis=0, keepdims=True)
    dt_e = mm_nn(sp, sel, "sel_b")
    elac_e = mm_nn(jnp.exp(lac), sel, "sel_b")
    toend_e = mm_nn(jnp.exp(lac_last - lac), sel, "sel_b")
    row8 = _iota((8, LANES), 0)
    two = jnp.where(row8 == 0, dvec, 0.0) + jnp.where(row8 == 1, jnp.exp(lac_last), 0.0)
    two_e = mm_nn(two, sel, "sel_b")
    row8e = _iota((8, SSD_GW), 0)
    d_e = jnp.sum(jnp.where(row8e == 0, two_e, 0.0), axis=0, keepdims=True)
    chunk_e = jnp.sum(jnp.where(row8e == 1, two_e, 0.0), axis=0, keepdims=True)
    xdt = xs * dt_e
    cb = mm_nt(cm, bm)
    y = mm_nn(cm, state) * elac_e + xs * d_e
    x_pairs = split_lanes(xdt)
    half = _iota((1, LANES), 1) >> 6
    lms = [_decay_matrix(_col_of(lac, lane == g * hpg + j), incl, eye) for j in range(hpg)]
    terms = [mm_nn(cb * lms[j], jnp.where(half == j % 2, x_pairs[j // 2], 0.0)) for j in range(hpg)]
    y = y + join_lanes([terms[2 * p] + terms[2 * p + 1] for p in range(hpg // 2)])
    new_state = state * chunk_e + mm_tn(bm, xdt * toend_e)
    yg = y * silu(z)
    return rmsnorm(yg, normw), new_state


def _params(sem=None):
    return pltpu.CompilerParams(dimension_semantics=sem, vmem_limit_bytes=VMEM_LIMIT)


def _full(shape):
    n = len(shape)
    return pl.BlockSpec(shape, lambda *_: (0,) * n)


ANY = pl.BlockSpec(memory_space=pl.ANY)
HBM = pl.BlockSpec(memory_space=pltpu.HBM)


def in_proj(x, normw, w_main, w_small):
    t = x.shape[0]
    tm, tn = min(512, t), 512

    def body(x_ref, nw_ref, wm_ref, ws_ref, pm_ref, ps_ref, u_ref):
        @pl.when(pl.program_id(1) == 0)
        def _():
            u = rmsnorm(x_ref[...], nw_ref[...]).astype(MXU_DTYPE)
            u_ref[...] = u
            ps_ref[...] = _raw_dot(u, ws_ref[...], 1, 0)
        pm_ref[...] = _raw_dot(u_ref[...], wm_ref[...], 1, 0)

    return pl.pallas_call(
        body, name="in_proj", grid=(t // tm, MAIN // tn),
        in_specs=[pl.BlockSpec((tm, D_MODEL), lambda i, j: (i, 0)), _full((1, D_MODEL)),
                  pl.BlockSpec((D_MODEL, tn), lambda i, j: (0, j)), _full((D_MODEL, LANES))],
        out_specs=[pl.BlockSpec((tm, tn), lambda i, j: (i, j)), pl.BlockSpec((tm, LANES), lambda i, j: (i, 0)),
                   pl.BlockSpec((tm, D_MODEL), lambda i, j: (i, 0))],
        out_shape=[jax.ShapeDtypeStruct((t, MAIN), F32), jax.ShapeDtypeStruct((t, LANES), F32),
                   jax.ShapeDtypeStruct((t, D_MODEL), MXU_DTYPE)],
        compiler_params=_params(("arbitrary", "arbitrary")),
    )(x, normw, w_main, w_small)


CONV_TC = 512
HALO = 8


def _shift_down(cur, prev, s):
    rolled = pltpu.roll(cur, s, 0)
    top = jnp.where(_iota((HALO, cur.shape[1]), 0) < s, pltpu.roll(prev, s, 0), rolled[:HALO])
    return jnp.concatenate([top, rolled[HALO:]], axis=0)


def _shift_up(cur, nxt, s):
    n = cur.shape[0]
    rolled = pltpu.roll(cur, n - s, 0)
    bot = jnp.where(_iota((HALO, cur.shape[1]), 0) >= HALO - s, pltpu.roll(nxt, HALO - s, 0), rolled[n - HALO:])
    return jnp.concatenate([rolled[:n - HALO], bot], axis=0)


def _conv_pre(cur, prev, w_ref, b):
    acc = cur * w_ref[3:4, :] + b
    shifted = [cur]
    for s in (1, 2, 3):
        sh = _shift_down(cur, prev, s)
        shifted.append(sh)
        acc = acc + sh * w_ref[3 - s:4 - s, :]
    return acc, shifted


def conv_fwd(proj_main, col0, width, w, b, name):
    t = proj_main.shape[0]
    tt, c0 = min(512, t), col0 // CONV_TC

    def body(cur_ref, prev_ref, w_ref, b_ref, out_ref):
        prev = jnp.where(pl.program_id(0) > 0, prev_ref[...], 0.0)
        pre, _ = _conv_pre(cur_ref[...], prev, w_ref, b_ref[...])
        out_ref[...] = silu(pre)

    return pl.pallas_call(
        body, name=name, grid=(t // tt, width // CONV_TC),
        in_specs=[pl.BlockSpec((tt, CONV_TC), lambda i, j: (i, c0 + j)),
                  pl.BlockSpec((HALO, CONV_TC), lambda i, j: (jnp.maximum(i * (tt // HALO) - 1, 0), c0 + j)),
                  pl.BlockSpec((4, CONV_TC), lambda i, j: (0, j)), pl.BlockSpec((1, CONV_TC), lambda i, j: (0, j))],
        out_specs=pl.BlockSpec((tt, CONV_TC), lambda i, j: (i, j)),
        out_shape=jax.ShapeDtypeStruct((t, width), F32),
        compiler_params=_params(("arbitrary", "arbitrary")),
    )(proj_main, proj_main, w, b)


def conv_bwd_pre(proj_main, col0, width, w, b, dout, name):
    t = proj_main.shape[0]
    tt, c0 = min(512, t), col0 // CONV_TC

    def body(cur_ref, prev_ref, w_ref, b_ref, do_ref, dpre_ref, dwb_ref):
        i = pl.program_id(1)
        prev = jnp.where(i > 0, prev_ref[...], 0.0)
        pre, shifted = _conv_pre(cur_ref[...], prev, w_ref, b_ref[...])
        sg = sigmoid(pre)
        dpre = do_ref[...] * (sg * (1.0 + pre * (1.0 - sg)))
        dpre_ref[...] = dpre
        row = _iota((HALO, CONV_TC), 0)
        upd = jnp.where(row == 4, jnp.sum(dpre, axis=0, keepdims=True), 0.0)
        for s in range(4):
            upd = upd + jnp.where(row == 3 - s, jnp.sum(dpre * shifted[s], axis=0, keepdims=True), 0.0)

        @pl.when(i == 0)
        def _():
            dwb_ref[...] = upd

        @pl.when(i > 0)
        def _():
            dwb_ref[...] += upd

    return pl.pallas_call(
        body, name=name, grid=(width // CONV_TC, t // tt),
        in_specs=[pl.BlockSpec((tt, CONV_TC), lambda j, i: (i, c0 + j)),
                  pl.BlockSpec((HALO, CONV_TC), lambda j, i: (jnp.maximum(i * (tt // HALO) - 1, 0), c0 + j)),
                  pl.BlockSpec((4, CONV_TC), lambda j, i: (0, j)), pl.BlockSpec((1, CONV_TC), lambda j, i: (0, j)),
                  pl.BlockSpec((tt, CONV_TC), lambda j, i: (i, j))],
        out_specs=[pl.BlockSpec((tt, CONV_TC), lambda j, i: (i, j)), pl.BlockSpec((HALO, CONV_TC), lambda j, i: (0, j))],
        out_shape=[jax.ShapeDtypeStruct((t, width), F32), jax.ShapeDtypeStruct((HALO, width), F32)],
        compiler_params=_params(("arbitrary", "arbitrary")),
    )(proj_main, proj_main, w, b, dout)


def conv_bwd_x(dproj_main, col0, width, w, dpre, name):
    t = dpre.shape[0]
    tt, c0 = min(512, t), col0 // CONV_TC
    nt = t // tt

    def body(alias_ref, cur_ref, nxt_ref, w_ref, out_ref):
        del alias_ref
        nxt = jnp.where(pl.program_id(0) < nt - 1, nxt_ref[...], 0.0)
        cur = cur_ref[...]
        acc = cur * w_ref[3:4, :]
        for s in (1, 2, 3):
            acc = acc + _shift_up(cur, nxt, s) * w_ref[3 - s:4 - s, :]
        out_ref[...] = acc

    return pl.pallas_call(
        body, name=name, grid=(nt, width // CONV_TC),
        in_specs=[ANY, pl.BlockSpec((tt, CONV_TC), lambda i, j: (i, j)),
                  pl.BlockSpec((HALO, CONV_TC), lambda i, j: (jnp.minimum((i + 1) * (tt // HALO), t // HALO - 1), j)),
                  pl.BlockSpec((4, CONV_TC), lambda i, j: (0, j))],
        out_specs=pl.BlockSpec((tt, CONV_TC), lambda i, j: (i, c0 + j)),
        out_shape=jax.ShapeDtypeStruct(dproj_main.shape, F32),
        input_output_aliases={0: 0},
        compiler_params=_params(("arbitrary", "arbitrary")),
    )(dproj_main, dpre, dpre, w)


def _ssd_parts(xbc_ref):
    return xbc_ref[:, :SSD_GW], xbc_ref[:, SSD_GW:SSD_GW + SSD_STATE], xbc_ref[:, SSD_GW + SSD_STATE:]


def _gdn_parts(qkv_ref):
    part = lambda o: [qkv_ref[:, j * GDN_HC + o:j * GDN_HC + o + GDN_DK] for j in range(GDN_HB)]
    return part(0), part(GDN_DK), part(2 * GDN_DK)


def _head_cols(ref):
    return [ref[:, j * GDN_DV:(j + 1) * GDN_DV] for j in range(GDN_HB)]


def _first_head():
    return 0 if GDN_HB == GDN_HEADS else pl.program_id(1) * GDN_HB


def ssd_fwd(conv_ssd, proj_main, proj_small, normw, alog, dtb, dvec):
    t = conv_ssd.shape[0]
    nc = t // CHUNK

    def body(xbc_ref, z_ref, sm_ref, nw_ref, al_ref, db_ref, dv_ref, y_ref, hist_ref, state_ref):
        g = pl.program_id(1)

        @pl.when(pl.program_id(0) == 0)
        def _():
            state_ref[g] = jnp.zeros((SSD_STATE, SSD_GW), F32)

        state = state_ref[g]
        hist_ref[0, 0] = state
        y, new_state = ssd_chunk(g, *_ssd_parts(xbc_ref), z_ref[...], sm_ref[...], nw_ref[...], al_ref[...],
                                 db_ref[...], dv_ref[...], state)
        y_ref[...] = y.astype(MXU_DTYPE)
        state_ref[g] = new_state

    return pl.pallas_call(
        body, name="ssd_fwd", grid=(nc, SSD_GROUPS),
        in_specs=[pl.BlockSpec((CHUNK, SSD_GC), lambda c, g: (c, g)),
                  pl.BlockSpec((CHUNK, SSD_GW), lambda c, g: (c, COL_Z // SSD_GW + g)),
                  pl.BlockSpec((CHUNK, LANES), lambda c, g: (c, 0)),
                  pl.BlockSpec((1, SSD_GW), lambda c, g: (0, g)), _full((1, LANES)), _full((1, LANES)), _full((1, LANES))],
        out_specs=[pl.BlockSpec((CHUNK, SSD_GW), lambda c, g: (c, g)),
                   pl.BlockSpec((1, 1, SSD_STATE, SSD_GW), lambda c, g: (c, g, 0, 0))],
        out_shape=[jax.ShapeDtypeStruct((t, SSD_WIDTH), MXU_DTYPE),
                   jax.ShapeDtypeStruct((nc, SSD_GROUPS, SSD_STATE, SSD_GW), F32)],
        scratch_shapes=[pltpu.VMEM((SSD_GROUPS, SSD_STATE, SSD_GW), F32)],
        compiler_params=_params(("arbitrary", "arbitrary")),
    )(conv_ssd, proj_main, proj_small, normw, alog, dtb, dvec)


def _accumulate(ref, first, value):
    @pl.when(first)
    def _():
        ref[...] = value

    @pl.when(jnp.logical_not(first))
    def _():
        ref[...] += value


def ssd_bwd(conv_ssd, proj_main, proj_small, normw, alog, dtb, dvec, hist, dy):
    t = conv_ssd.shape[0]
    nc = t // CHUNK
    rev = lambda c: nc - 1 - c

    def body(xbc_ref, z_ref, sm_ref, nw_ref, al_ref, db_ref, dv_ref, hist_ref, dy_ref,
             dxbc_ref, dz_ref, dsm_ref, dnw_ref, dal_ref, ddb_ref, ddv_ref, dstate_ref):
        c, g = pl.program_id(0), pl.program_id(1)

        @pl.when(c == 0)
        def _():
            dstate_ref[g] = jnp.zeros((SSD_STATE, SSD_GW), F32)

        fn = functools.partial(ssd_chunk, g)
        _, vjp = jax.vjp(fn, *_ssd_parts(xbc_ref), z_ref[...], sm_ref[...], nw_ref[...], al_ref[...], db_ref[...],
                         dv_ref[...], hist_ref[0, 0])
        dxs, dbm, dcm, dz, dsm, dnw, dal, ddb, ddv, dstate = vjp((dy_ref[...], dstate_ref[g]))
        dxbc_ref[:, :SSD_GW] = dxs
        dxbc_ref[:, SSD_GW:SSD_GW + SSD_STATE] = dbm
        dxbc_ref[:, SSD_GW + SSD_STATE:] = dcm
        dz_ref[...] = dz
        dstate_ref[g] = dstate
        _accumulate(dsm_ref, g == 0, dsm)
        first = jnp.logical_and(c == 0, g == 0)
        _accumulate(dal_ref, first, dal)
        _accumulate(ddb_ref, first, ddb)
        _accumulate(ddv_ref, first, ddv)

        @pl.when(c == 0)
        def _():
            dnw_ref[g] = dnw

        @pl.when(c > 0)
        def _():
            dnw_ref[g] += dnw

    return pl.pallas_call(
        body, name="ssd_bwd", grid=(nc, SSD_GROUPS),
        in_specs=[pl.BlockSpec((CHUNK, SSD_GC), lambda c, g: (rev(c), g)),
                  pl.BlockSpec((CHUNK, SSD_GW), lambda c, g: (rev(c), COL_Z // SSD_GW + g)),
                  pl.BlockSpec((CHUNK, LANES), lambda c, g: (rev(c), 0)),
                  pl.BlockSpec((1, SSD_GW), lambda c, g: (0, g)), _full((1, LANES)), _full((1, LANES)), _full((1, LANES)),
                  pl.BlockSpec((1, 1, SSD_STATE, SSD_GW), lambda c, g: (rev(c), g, 0, 0)),
                  pl.BlockSpec((CHUNK, SSD_GW), lambda c, g: (rev(c), g))],
        out_specs=[pl.BlockSpec((CHUNK, SSD_GC), lambda c, g: (rev(c), g)),
                   pl.BlockSpec((CHUNK, SSD_GW), lambda c, g: (rev(c), COL_Z // SSD_GW + g)),
                   pl.BlockSpec((CHUNK, LANES), lambda c, g: (rev(c), 0)),
                   _full((SSD_GROUPS, 1, SSD_GW)), _full((1, LANES)), _full((1, LANES)), _full((1, LANES))],
        out_shape=[jax.ShapeDtypeStruct((t, SSD_CONV), F32), jax.ShapeDtypeStruct((t, MAIN), F32),
                   jax.ShapeDtypeStruct((t, LANES), F32), jax.ShapeDtypeStruct((SSD_GROUPS, 1, SSD_GW), F32),
                   jax.ShapeDtypeStruct((1, LANES), F32), jax.ShapeDtypeStruct((1, LANES), F32),
                   jax.ShapeDtypeStruct((1, LANES), F32)],
        scratch_shapes=[pltpu.VMEM((SSD_GROUPS, SSD_STATE, SSD_GW), F32)],
        compiler_params=_params(("arbitrary", "arbitrary")),
    )(conv_ssd, proj_main, proj_small, normw, alog, dtb, dvec, hist, dy)


def gdn_fwd(conv_gdn, proj_main, proj_small, normw, alog, dtb):
    t = conv_gdn.shape[0]
    nc = t // CHUNK

    hb = GDN_HB
    gate_blk = COL_GATE // (GDN_DV * hb)

    def body(qkv_ref, gate_ref, sm_ref, nw_ref, al_ref, db_ref, y_ref, hist_ref, t_ref, state_ref):
        h0 = _first_head()

        @pl.when(pl.program_id(0) == 0)
        def _():
            for j in range(hb):
                state_ref[h0 + j] = jnp.zeros((GDN_DK, GDN_DV), F32)

        states = [state_ref[h0 + j] for j in range(hb)]
        for j in range(hb):
            hist_ref[0, j] = states[j]
        qs, ks, vs = _gdn_parts(qkv_ref)
        ys, new_states, ts = gdn_chunk(h0, qs, ks, vs, sm_ref[...], _head_cols(gate_ref), nw_ref[...], al_ref[...],
                                       db_ref[...], states)
        for j in range(hb):
            y_ref[:, j * GDN_DV:(j + 1) * GDN_DV] = ys[j].astype(MXU_DTYPE)
            state_ref[h0 + j] = new_states[j]
            t_ref[0, j] = ts[j]

    return pl.pallas_call(
        body, name="gdn_fwd", grid=(nc, GDN_HEADS // hb),
        in_specs=[pl.BlockSpec((CHUNK, GDN_HC * hb), lambda c, h: (c, h)),
                  pl.BlockSpec((CHUNK, GDN_DV * hb), lambda c, h: (c, gate_blk + h)),
                  pl.BlockSpec((CHUNK, LANES), lambda c, h: (c, 0)),
                  _full((1, GDN_DV)), _full((1, LANES)), _full((1, LANES))],
        out_specs=[pl.BlockSpec((CHUNK, GDN_DV * hb), lambda c, h: (c, h)),
                   pl.BlockSpec((1, hb, GDN_DK, GDN_DV), lambda c, h: (c, h, 0, 0)),
                   pl.BlockSpec((1, hb, CHUNK, CHUNK), lambda c, h: (c, h, 0, 0))],
        out_shape=[jax.ShapeDtypeStruct((t, GDN_W), MXU_DTYPE),
                   jax.ShapeDtypeStruct((nc, GDN_HEADS, GDN_DK, GDN_DV), F32),
                   jax.ShapeDtypeStruct((nc, GDN_HEADS, CHUNK, CHUNK), F32)],
        scratch_shapes=[pltpu.VMEM((GDN_HEADS, GDN_DK, GDN_DV), F32)],
        compiler_params=_params(("arbitrary", "arbitrary")),
    )(conv_gdn, proj_main, proj_small, normw, alog, dtb)


def gdn_bwd(dproj_main, conv_gdn, proj_main, proj_small, normw, alog, dtb, hist, t_inv, dy):
    t = conv_gdn.shape[0]
    nc = t // CHUNK
    rev = lambda c: nc - 1 - c
    hb = GDN_HB
    gate_blk = COL_GATE // (GDN_DV * hb)

    def body(alias_ref, qkv_ref, gate_ref, sm_ref, nw_ref, al_ref, db_ref, hist_ref, t_ref, dy_ref,
             dgate_ref, dqkv_ref, dsm_ref, dnw_ref, dal_ref, ddb_ref, dstate_ref):
        del alias_ref
        c, h = pl.program_id(0), pl.program_id(1)
        h0 = _first_head()

        @pl.when(c == 0)
        def _():
            for j in range(hb):
                dstate_ref[h0 + j] = jnp.zeros((GDN_DK, GDN_DV), F32)

        saved = [t_ref[0, j] for j in range(hb)]

        def fn(qs, ks, vs, small, gates, nw, al, db, states):
            return gdn_chunk(h0, qs, ks, vs, small, gates, nw, al, db, states, saved)[:2]

        qs, ks, vs = _gdn_parts(qkv_ref)
        _, vjp = jax.vjp(fn, qs, ks, vs, sm_ref[...], _head_cols(gate_ref), nw_ref[...], al_ref[...], db_ref[...],
                         [hist_ref[0, j] for j in range(hb)])
        dqs, dks, dvs, dsm, dgates, dnw, dal, ddb, dstates = vjp(
            (_head_cols(dy_ref), [dstate_ref[h0 + j] for j in range(hb)]))
        for j in range(hb):
            base = j * GDN_HC
            dqkv_ref[:, base:base + GDN_DK] = dqs[j]
            dqkv_ref[:, base + GDN_DK:base + 2 * GDN_DK] = dks[j]
            dqkv_ref[:, base + 2 * GDN_DK:base + GDN_HC] = dvs[j]
            dgate_ref[:, j * GDN_DV:(j + 1) * GDN_DV] = dgates[j]
            dstate_ref[h0 + j] = dstates[j]
        _accumulate(dsm_ref, h == 0, dsm)
        first = jnp.logical_and(c == 0, h == 0)
        _accumulate(dnw_ref, first, dnw)
        _accumulate(dal_ref, first, dal)
        _accumulate(ddb_ref, first, ddb)

    return pl.pallas_call(
        body, name="gdn_bwd", grid=(nc, GDN_HEADS // hb),
        in_specs=[ANY, pl.BlockSpec((CHUNK, GDN_HC * hb), lambda c, h: (rev(c), h)),
                  pl.BlockSpec((CHUNK, GDN_DV * hb), lambda c, h: (rev(c), gate_blk + h)),
                  pl.BlockSpec((CHUNK, LANES), lambda c, h: (rev(c), 0)),
                  _full((1, GDN_DV)), _full((1, LANES)), _full((1, LANES)),
                  pl.BlockSpec((1, hb, GDN_DK, GDN_DV), lambda c, h: (rev(c), h, 0, 0)),
                  pl.BlockSpec((1, hb, CHUNK, CHUNK), lambda c, h: (rev(c), h, 0, 0)),
                  pl.BlockSpec((CHUNK, GDN_DV * hb), lambda c, h: (rev(c), h))],
        out_specs=[pl.BlockSpec((CHUNK, GDN_DV * hb), lambda c, h: (rev(c), gate_blk + h)),
                   pl.BlockSpec((CHUNK, GDN_HC * hb), lambda c, h: (rev(c), h)),
                   pl.BlockSpec((CHUNK, LANES), lambda c, h: (rev(c), 0)),
                   _full((1, GDN_DV)), _full((1, LANES)), _full((1, LANES))],
        out_shape=[jax.ShapeDtypeStruct(dproj_main.shape, F32), jax.ShapeDtypeStruct((t, GDN_CONV), F32),
                   jax.ShapeDtypeStruct((t, LANES), F32), jax.ShapeDtypeStruct((1, GDN_DV), F32),
                   jax.ShapeDtypeStruct((1, LANES), F32), jax.ShapeDtypeStruct((1, LANES), F32)],
        scratch_shapes=[pltpu.VMEM((GDN_HEADS, GDN_DK, GDN_DV), F32)],
        input_output_aliases={0: 0},
        compiler_params=_params(("arbitrary", "arbitrary")),
    )(dproj_main, conv_gdn, proj_main, proj_small, normw, alog, dtb, hist, t_inv, dy)


def out_proj_loss(x, y_ssd, y_gdn, w_out, final_w, target):
    t = x.shape[0]
    tm = min(256, t)

    def body(x_ref, ys_ref, yg_ref, wo_ref, fw_ref, tg_ref, loss_ref, dhid_ref, dys_ref, dyg_ref, dwo_ref, dfw_ref):
        i = pl.program_id(0)
        ys, yg = ys_ref[...], yg_ref[...]
        wo_s, wo_g = wo_ref[:SSD_WIDTH, :], wo_ref[SSD_WIDTH:, :]
        hid = x_ref[...] + _raw_dot(ys, wo_s, 1, 0) + _raw_dot(yg, wo_g, 1, 0)
        out, vjp = jax.vjp(rmsnorm, hid, fw_ref[...])
        err = out - tg_ref[...]
        loss = 0.5 * jnp.sum(jnp.mean(err * err, axis=-1, keepdims=True), axis=0, keepdims=True)
        dhid, dfw = vjp(err * (1.0 / D_MODEL))
        dhid_ref[...] = dhid
        dys_ref[...] = _raw_dot(dhid, wo_s, 1, 1)
        dyg_ref[...] = _raw_dot(dhid, wo_g, 1, 1)
        first = i == 0
        _accumulate(loss_ref, first, jnp.broadcast_to(loss, loss_ref.shape))
        _accumulate(dfw_ref, first, dfw)

        @pl.when(first)
        def _():
            dwo_ref[:SSD_WIDTH, :] = _raw_dot(ys, dhid, 0, 0)
            dwo_ref[SSD_WIDTH:, :] = _raw_dot(yg, dhid, 0, 0)

        @pl.when(i > 0)
        def _():
            dwo_ref[:SSD_WIDTH, :] += _raw_dot(ys, dhid, 0, 0)
            dwo_ref[SSD_WIDTH:, :] += _raw_dot(yg, dhid, 0, 0)

    row = lambda w: pl.BlockSpec((tm, w), lambda i: (i, 0))
    return pl.pallas_call(
        body, name="out_proj_loss", grid=(t // tm,),
        in_specs=[row(D_MODEL), row(SSD_WIDTH), row(GDN_W), _full((SSD_WIDTH + GDN_W, D_MODEL)), _full((1, D_MODEL)),
                  row(D_MODEL)],
        out_specs=[_full((8, LANES)), row(D_MODEL), row(SSD_WIDTH), row(GDN_W), _full((SSD_WIDTH + GDN_W, D_MODEL)),
                   _full((1, D_MODEL))],
        out_shape=[jax.ShapeDtypeStruct((8, LANES), F32), jax.ShapeDtypeStruct((t, D_MODEL), F32),
                   jax.ShapeDtypeStruct((t, SSD_WIDTH), F32), jax.ShapeDtypeStruct((t, GDN_W), F32),
                   jax.ShapeDtypeStruct((SSD_WIDTH + GDN_W, D_MODEL), F32), jax.ShapeDtypeStruct((1, D_MODEL), F32)],
        compiler_params=_params(("arbitrary",)),
    )(x, y_ssd, y_gdn, w_out, final_w, target)


def in_proj_bwd_x(x, normw, w_main, w_small, dproj_main, dsmall_a, dsmall_b, dhid):
    t = x.shape[0]
    tm, tk = min(512, t), 512
    nk = MAIN // tk

    def body(x_ref, nw_ref, wm_ref, ws_ref, dp_ref, da_ref, db_ref, dh_ref, gx_ref, dnw_ref, acc_ref):
        i, k = pl.program_id(0), pl.program_id(1)
        part = _raw_dot(dp_ref[...], wm_ref[...], 1, 1)

        @pl.when(k == 0)
        def _():
            acc_ref[...] = part + _raw_dot(da_ref[...] + db_ref[...], ws_ref[...], 1, 1)

        @pl.when(k > 0)
        def _():
            acc_ref[...] += part

        @pl.when(k == nk - 1)
        def _():
            _, vjp = jax.vjp(rmsnorm, x_ref[...], nw_ref[...])
            dx, dnw = vjp(acc_ref[...])
            gx_ref[...] = dx + dh_ref[...]
            _accumulate(dnw_ref, i == 0, dnw)

    row = lambda w: pl.BlockSpec((tm, w), lambda i, k: (i, 0))
    return pl.pallas_call(
        body, name="in_proj_bwd_x", grid=(t // tm, nk),
        in_specs=[row(D_MODEL), _full((1, D_MODEL)), pl.BlockSpec((D_MODEL, tk), lambda i, k: (0, k)),
                  _full((D_MODEL, LANES)), pl.BlockSpec((tm, tk), lambda i, k: (i, k)), row(LANES), row(LANES),
                  row(D_MODEL)],
        out_specs=[row(D_MODEL), _full((1, D_MODEL))],
        out_shape=[jax.ShapeDtypeStruct((t, D_MODEL), F32), jax.ShapeDtypeStruct((1, D_MODEL), F32)],
        scratch_shapes=[pltpu.VMEM((tm, D_MODEL), F32)],
        compiler_params=_params(("arbitrary", "arbitrary")),
    )(x, normw, w_main, w_small, dproj_main, dsmall_a, dsmall_b, dhid)


def in_proj_bwd_w(u, dproj_main, dsmall_a, dsmall_b):
    t = u.shape[0]
    tm, tn = min(512, t), 512

    def body(u_ref, dp_ref, da_ref, db_ref, dwm_ref, dws_ref):
        j, i = pl.program_id(0), pl.program_id(1)
        uu = u_ref[...]
        _accumulate(dwm_ref, i == 0, _raw_dot(uu, dp_ref[...], 0, 0))

        @pl.when(j == 0)
        def _():
            _accumulate(dws_ref, i == 0, _raw_dot(uu, da_ref[...] + db_ref[...], 0, 0))

    return pl.pallas_call(
        body, name="in_proj_bwd_w", grid=(MAIN // tn, t // tm),
        in_specs=[pl.BlockSpec((tm, D_MODEL), lambda j, i: (i, 0)), pl.BlockSpec((tm, tn), lambda j, i: (i, j)),
                  pl.BlockSpec((tm, LANES), lambda j, i: (i, 0)), pl.BlockSpec((tm, LANES), lambda j, i: (i, 0))],
        out_specs=[pl.BlockSpec((D_MODEL, tn), lambda j, i: (0, j)), _full((D_MODEL, LANES))],
        out_shape=[jax.ShapeDtypeStruct((D_MODEL, MAIN), F32), jax.ShapeDtypeStruct((D_MODEL, LANES), F32)],
        compiler_params=_params(("arbitrary", "arbitrary")),
    )(u, dproj_main, dsmall_a, dsmall_b)


def sum_slabs(a, name):
    n, rows, cols = a.shape
    tr = 64 if rows % 64 == 0 else rows

    def body(a_ref, o_ref):
        acc = a_ref[0].astype(F32)
        for d in range(1, n):
            acc = acc + a_ref[d].astype(F32)
        o_ref[...] = acc

    return pl.pallas_call(
        body, name=name, grid=(rows // tr,),
        in_specs=[pl.BlockSpec((n, tr, cols), lambda i: (0, i, 0))],
        out_specs=pl.BlockSpec((tr, cols), lambda i: (i, 0)),
        out_shape=jax.ShapeDtypeStruct((rows, cols), F32),
        compiler_params=_params(("arbitrary",)),
    )(a)


def adamw(w, g, m, v, name):
    rows, cols = w.shape
    tr = 128 if rows % 128 == 0 else rows

    def body(w_ref, g_ref, m_ref, v_ref, d_ref, nm_ref, nv_ref):
        gg = g_ref[...]
        nm = ADAM_B1 * m_ref[...] + (1.0 - ADAM_B1) * gg
        nv = ADAM_B2 * v_ref[...] + (1.0 - ADAM_B2) * (gg * gg)
        m_hat = nm / (1.0 - ADAM_B1 ** ADAM_STEP)
        v_hat = nv / (1.0 - ADAM_B2 ** ADAM_STEP)
        d_ref[...] = -ADAM_LR * (m_hat / (jnp.sqrt(v_hat) + ADAM_EPS) + ADAM_WD * w_ref[...])
        nm_ref[...] = nm
        nv_ref[...] = nv

    spec = pl.BlockSpec((tr, cols), lambda i: (i, 0))
    shp = jax.ShapeDtypeStruct((rows, cols), F32)
    return pl.pallas_call(
        body, name=name, grid=(rows // tr,), in_specs=[spec] * 4, out_specs=[spec] * 3, out_shape=[shp] * 3,
        compiler_params=_params(("arbitrary",)),
    )(w, g, m, v)


def _my_place():
    return lax.axis_index("x"), lax.axis_index("y"), lax.axis_index("c")


def gather_weights(shards):
    n = len(shards)

    def body(*refs):
        srcs, outs = refs[:n], refs[n:2 * n]
        send_sems, recv_sems, local_sems = refs[2 * n:]
        x, y, c = _my_place()
        me = 2 * x + y
        chips = [(1 - x, y), (x, 1 - y), (1 - x, 1 - y)]
        local = [pltpu.make_async_copy(srcs[i], outs[i].at[me], local_sems.at[i]) for i in range(n)]
        for cp in local:
            cp.start()
        sends = []
        for j, (px, py) in enumerate(chips):
            for i in range(n):
                sends.append(pltpu.make_async_remote_copy(
                    src_ref=srcs[i], dst_ref=outs[i].at[me], send_sem=send_sems.at[j * n + i],
                    recv_sem=recv_sems.at[j * n + i], device_id=(px, py, c), device_id_type=MESH))
        for cp in sends:
            cp.start()
        for j, (px, py) in enumerate(chips):
            for i in range(n):
                pltpu.make_async_remote_copy(
                    src_ref=srcs[i], dst_ref=outs[i].at[2 * px + py], send_sem=send_sems.at[j * n + i],
                    recv_sem=recv_sems.at[j * n + i], device_id=(px, py, c), device_id_type=MESH).wait_recv()
        for cp in sends:
            cp.wait_send()
        for cp in local:
            cp.wait()

    return pl.pallas_call(
        body, name="gather_weights",
        in_specs=[HBM] * n, out_specs=[HBM] * n,
        out_shape=[jax.ShapeDtypeStruct((N_CHIP,) + s.shape, s.dtype) for s in shards],
        scratch_shapes=[pltpu.SemaphoreType.DMA((3 * n,)), pltpu.SemaphoreType.DMA((3 * n,)),
                        pltpu.SemaphoreType.DMA((n,))],
    )(*shards)


def _peer(x, y, c, mask):
    mx, my, mc = (mask >> 2) & 1, (mask >> 1) & 1, mask & 1
    return (x ^ mx if mx else x, y ^ my if my else y, c ^ mc if mc else c)


def exchange_slabs(slabbed, replicated):
    ns, nr = len(slabbed), len(replicated)
    n = ns + nr

    def body(*refs):
        srcs, outs = refs[:n], refs[n:2 * n]
        send_sems, recv_sems, local_sems = refs[2 * n:]
        x, y, c = _my_place()
        me = 4 * x + 2 * y + c

        def piece(i, dev):
            return srcs[i].at[dev] if i < ns else srcs[i]

        local = [pltpu.make_async_copy(piece(i, me), outs[i].at[me], local_sems.at[i]) for i in range(n)]
        for cp in local:
            cp.start()
        sends = []
        for mask in range(1, N_DEV):
            px, py, pc = _peer(x, y, c, mask)
            dev = 4 * px + 2 * py + pc
            for i in range(n):
                k = (mask - 1) * n + i
                sends.append(pltpu.make_async_remote_copy(
                    src_ref=piece(i, dev), dst_ref=outs[i].at[me], send_sem=send_sems.at[k], recv_sem=recv_sems.at[k],
                    device_id=(px, py, pc), device_id_type=MESH))
        for cp in sends:
            cp.start()
        for mask in range(1, N_DEV):
            px, py, pc = _peer(x, y, c, mask)
            dev = 4 * px + 2 * py + pc
            for i in range(n):
                k = (mask - 1) * n + i
                pltpu.make_async_remote_copy(
                    src_ref=piece(i, dev), dst_ref=outs[i].at[dev], send_sem=send_sems.at[k], recv_sem=recv_sems.at[k],
                    device_id=(px, py, pc), device_id_type=MESH).wait_recv()
        for cp in sends:
            cp.wait_send()
        for cp in local:
            cp.wait()

    shapes = [jax.ShapeDtypeStruct(a.shape, a.dtype) for a in slabbed]
    shapes += [jax.ShapeDtypeStruct((N_DEV,) + a.shape, a.dtype) for a in replicated]
    return pl.pallas_call(
        body, name="exchange_slabs",
        in_specs=[HBM] * n, out_specs=[HBM] * n, out_shape=shapes,
        scratch_shapes=[pltpu.SemaphoreType.DMA((7 * n,)), pltpu.SemaphoreType.DMA((7 * n,)),
                        pltpu.SemaphoreType.DMA((n,))],
    )(*slabbed, *replicated)


def exchange_halves(halves):
    n = len(halves)
    streams = 8

    def body(*refs):
        srcs, outs = refs[:n], refs[n:2 * n]
        send_sems, recv_sems, local_sems = refs[2 * n:]
        x, y, c = _my_place()
        local = [pltpu.make_async_copy(srcs[i], outs[i].at[c], local_sems.at[i]) for i in range(n)]
        for cp in local:
            cp.start()

        def chunk_copy(i, s, half):
            rows = halves[i].shape[0] // streams
            k = i * streams + s
            return pltpu.make_async_remote_copy(
                src_ref=srcs[i].at[pl.ds(s * rows, rows)], dst_ref=outs[i].at[half, pl.ds(s * rows, rows)],
                send_sem=send_sems.at[k], recv_sem=recv_sems.at[k], device_id=(x, y, 1 - c), device_id_type=MESH)

        sends = [chunk_copy(i, s, c) for i in range(n) for s in range(streams)]
        for cp in sends:
            cp.start()
        for i in range(n):
            for s in range(streams):
                chunk_copy(i, s, 1 - c).wait_recv()
        for cp in sends:
            cp.wait_send()
        for cp in local:
            cp.wait()

    return pl.pallas_call(
        body, name="exchange_halves",
        in_specs=[HBM] * n, out_specs=[HBM] * n,
        out_shape=[jax.ShapeDtypeStruct((2,) + a.shape, a.dtype) for a in halves],
        scratch_shapes=[pltpu.SemaphoreType.DMA((n * streams,)), pltpu.SemaphoreType.DMA((n * streams,)),
                        pltpu.SemaphoreType.DMA((n,))],
    )(*halves)


def _pack_cols(pieces):
    offs, pos = [], 0
    for a in pieces:
        offs.append(pos)
        pos += a.shape[1]
    rows8 = [jnp.pad(a.astype(F32), ((0, 8 - a.shape[0]), (0, 0))) for a in pieces]
    return jnp.concatenate(rows8, axis=1), offs


def adamw_many(ws, gs, ms, vs):
    n = len(ws)

    def body(*refs):
        w_r, g_r, m_r, v_r = refs[:n], refs[n:2 * n], refs[2 * n:3 * n], refs[3 * n:4 * n]
        d_o, m_o, v_o = refs[4 * n:5 * n], refs[5 * n:6 * n], refs[6 * n:7 * n]
        for i in range(n):
            gg = g_r[i][...]
            nm = ADAM_B1 * m_r[i][...] + (1.0 - ADAM_B1) * gg
            nv = ADAM_B2 * v_r[i][...] + (1.0 - ADAM_B2) * (gg * gg)
            m_hat = nm / (1.0 - ADAM_B1 ** ADAM_STEP)
            v_hat = nv / (1.0 - ADAM_B2 ** ADAM_STEP)
            d_o[i][...] = -ADAM_LR * (m_hat / (jnp.sqrt(v_hat) + ADAM_EPS) + ADAM_WD * w_r[i][...])
            m_o[i][...] = nm
            v_o[i][...] = nv

    shapes = [jax.ShapeDtypeStruct(w.shape, F32) for w in ws]
    out = pl.pallas_call(body, name="adamw_small", out_shape=shapes * 3,
                         compiler_params=pltpu.CompilerParams(vmem_limit_bytes=VMEM_LIMIT))(*ws, *gs, *ms, *vs)
    return out[:n], out[n:2 * n], out[2 * n:]


def _lanes(vec, start):
    n = vec.shape[-1]
    return jnp.pad(vec.reshape(1, n).astype(F32), ((0, 0), (start, LANES - start - n)))


def kernel(x, norm_w, w_in, ssd_conv_w, ssd_conv_b, ssd_dt_bias, ssd_a_log, ssd_d, ssd_norm_w, gdn_conv_w, gdn_dt_bias, gdn_a_log, gdn_norm_w, w_out, final_norm_w, loss_target, m_norm_w, m_w_in, m_ssd_conv_w, m_ssd_conv_b, m_ssd_dt_bias, m_ssd_a_log, m_ssd_d, m_ssd_norm_w, m_gdn_conv_w, m_gdn_dt_bias, m_gdn_a_log, m_gdn_norm_w, m_w_out, m_final_norm_w, v_norm_w, v_w_in, v_ssd_conv_w, v_ssd_conv_b, v_ssd_dt_bias, v_ssd_a_log, v_ssd_d, v_ssd_norm_w, v_gdn_conv_w, v_gdn_dt_bias, v_gdn_a_log, v_gdn_norm_w, v_w_out, v_final_norm_w):
    xs = x[0]
    target = loss_target[0]
    chip = 2 * lax.axis_index("x") + lax.axis_index("y")
    w_in_shard, w_out_shard = w_in[0], w_out[0]
    in_cols = w_in_shard.shape[1]
    out_rows = w_out_shard.shape[0]

    g_in, g_out, g_cs, g_cg = gather_weights(
        [w_in_shard.astype(MXU_DTYPE), w_out_shard.astype(MXU_DTYPE), ssd_conv_w[0], gdn_conv_w[0]])
    w_in_full = jnp.transpose(g_in, (1, 0, 2)).reshape(D_MODEL, IN_DIM)
    w_out_full = g_out.reshape(N_CHIP * out_rows, D_MODEL)
    cw_ssd = _take(jnp.transpose(g_cs, (1, 0, 2)).reshape(4, SSD_CONV), _ssd_conv_segments())
    cw_gdn = _take(jnp.transpose(g_cg, (1, 0, 2)).reshape(4, GDN_CONV), _gdn_conv_segments())
    cb_ssd = _take(ssd_conv_b, _ssd_conv_segments())
    cb_gdn = jnp.zeros((1, GDN_CONV), F32)
    w_main = _take(w_in_full, _main_segments())
    w_small = jnp.concatenate([w_in_full[:, 2560:2576], w_in_full[:, 6672:6688],
                               jnp.zeros((D_MODEL, LANES - 32), MXU_DTYPE)], axis=1)
    alog = _lanes(ssd_a_log, 0) + _lanes(gdn_a_log, LANE_GA)
    dtb = _lanes(ssd_dt_bias, 0) + _lanes(gdn_dt_bias, LANE_GA)
    dvec = _lanes(ssd_d, 0)
    fw = final_norm_w.reshape(1, D_MODEL)

    proj_main, proj_small, u = in_proj(xs, norm_w, w_main, w_small)
    conv_ssd = conv_fwd(proj_main, COL_SSD, SSD_CONV, cw_ssd, cb_ssd, "conv_fwd_ssd")
    conv_gdn = conv_fwd(proj_main, COL_GDN, GDN_CONV, cw_gdn, cb_gdn, "conv_fwd_gdn")
    y_ssd, hist_ssd = ssd_fwd(conv_ssd, proj_main, proj_small, ssd_norm_w, alog, dtb, dvec)
    y_gdn, hist_gdn, tinv_gdn = gdn_fwd(conv_gdn, proj_main, proj_small, gdn_norm_w, alog, dtb)

    loss_blk, dhid, dy_ssd, dy_gdn, d_w_out, d_fw = out_proj_loss(xs, y_ssd, y_gdn, w_out_full, fw, target)
    dconv_ssd, dproj_main, dsmall_ssd, d_ssd_nw, d_alog_s, d_dtb_s, d_dvec = ssd_bwd(
        conv_ssd, proj_main, proj_small, ssd_norm_w, alog, dtb, dvec, hist_ssd, dy_ssd)
    dproj_main, dconv_gdn, dsmall_gdn, d_gdn_nw, d_alog_g, d_dtb_g = gdn_bwd(
        dproj_main, conv_gdn, proj_main, proj_small, gdn_norm_w, alog, dtb, hist_gdn, tinv_gdn, dy_gdn)
    dpre_ssd, dwb_ssd = conv_bwd_pre(proj_main, COL_SSD, SSD_CONV, cw_ssd, cb_ssd, dconv_ssd, "conv_bwd_pre_ssd")
    dpre_gdn, dwb_gdn = conv_bwd_pre(proj_main, COL_GDN, GDN_CONV, cw_gdn, cb_gdn, dconv_gdn, "conv_bwd_pre_gdn")
    dproj_main = conv_bwd_x(dproj_main, COL_SSD, SSD_CONV, cw_ssd, dpre_ssd, "conv_bwd_x_ssd")
    dproj_main = conv_bwd_x(dproj_main, COL_GDN, GDN_CONV, cw_gdn, dpre_gdn, "conv_bwd_x_gdn")
    grad_x, d_norm_w = in_proj_bwd_x(xs, norm_w, w_main, w_small, dproj_main, dsmall_ssd, dsmall_gdn, dhid)
    d_w_main, d_w_small = in_proj_bwd_w(u, dproj_main, dsmall_ssd, dsmall_gdn)

    d_w_in = _untake(d_w_main, _main_segments())
    d_w_in = jnp.concatenate([d_w_in[:, :2560], d_w_small[:, 0:16], d_w_in[:, 2560:], d_w_small[:, 16:32]], axis=1)
    d_w_in = jnp.transpose(d_w_in.reshape(D_MODEL, N_CHIP, in_cols), (1, 0, 2))
    d_alog, d_dtb = d_alog_s + d_alog_g, d_dtb_s + d_dtb_g
    packed, (o_nw, o_cs, o_cg, o_snw, o_fw, o_al, o_db, o_dv, o_gnw, o_loss) = _pack_cols([
        d_norm_w, _untake(dwb_ssd, _ssd_conv_segments()), _untake(dwb_gdn, _gdn_conv_segments()),
        d_ssd_nw.reshape(1, SSD_WIDTH), d_fw, d_alog, d_dtb, d_dvec, d_gdn_nw, loss_blk])

    r_in, r_out, r_small = exchange_slabs(
        [d_w_in.reshape(N_DEV, D_MODEL // 2, in_cols).astype(COMM_DTYPE),
         d_w_out.reshape(N_DEV, out_rows // 2, D_MODEL).astype(COMM_DTYPE)],
        [packed])
    half_in = sum_slabs(r_in, "sum_w_in")
    half_out = sum_slabs(r_out, "sum_w_out")
    tot = sum_slabs(r_small, "sum_small")
    full_in, full_out = exchange_halves([half_in, half_out])
    grad_w_in = full_in.reshape(D_MODEL, in_cols)
    grad_w_out = full_out.reshape(out_rows, D_MODEL)
    loss = tot[0, o_loss]
    sc, gc = ssd_conv_w.shape[2], gdn_conv_w.shape[2]
    row = lambda off, n, r=0: tot[r:r + 1, off:off + n]
    gs = [row(o_nw, D_MODEL),
          lax.dynamic_slice(tot, (0, o_cs + chip * sc), (4, sc)),
          row(o_cs, SSD_CONV, 4),
          row(o_db, SSD_HEADS), row(o_al, SSD_HEADS), row(o_dv, SSD_HEADS),
          row(o_snw, SSD_WIDTH),
          lax.dynamic_slice(tot, (0, o_cg + chip * gc), (4, gc)),
          row(o_db + LANE_GA, GDN_HEADS), row(o_al + LANE_GA, GDN_HEADS),
          row(o_gnw, GDN_DV), row(o_fw, D_MODEL)]

    names = ["norm_w", "ssd_conv_w", "ssd_conv_b", "ssd_dt_bias", "ssd_a_log", "ssd_d", "ssd_norm_w", "gdn_conv_w",
             "gdn_dt_bias", "gdn_a_log", "gdn_norm_w", "final_norm_w"]
    ws = [norm_w, ssd_conv_w, ssd_conv_b, ssd_dt_bias, ssd_a_log, ssd_d, ssd_norm_w, gdn_conv_w, gdn_dt_bias,
          gdn_a_log, gdn_norm_w, final_norm_w]
    ms = [m_norm_w, m_ssd_conv_w, m_ssd_conv_b, m_ssd_dt_bias, m_ssd_a_log, m_ssd_d, m_ssd_norm_w, m_gdn_conv_w,
          m_gdn_dt_bias, m_gdn_a_log, m_gdn_norm_w, m_final_norm_w]
    vs = [v_norm_w, v_ssd_conv_w, v_ssd_conv_b, v_ssd_dt_bias, v_ssd_a_log, v_ssd_d, v_ssd_norm_w, v_gdn_conv_w,
          v_gdn_dt_bias, v_gdn_a_log, v_gdn_norm_w, v_final_norm_w]
    shapes = [w.shape for w in ws]
    flat = lambda arrs: [a.reshape(g.shape) for a, g in zip(arrs, gs)]
    d_s, m_s, v_s = adamw_many(flat(ws), gs, flat(ms), flat(vs))
    back = lambda arrs: dict(zip(names, [a.reshape(s) for a, s in zip(arrs, shapes)]))
    delta, new_m, new_v, grads = back(d_s), back(m_s), back(v_s), back(gs)
    d_in, m_in, v_in = adamw(w_in_shard, grad_w_in, m_w_in[0], v_w_in[0], "adamw_w_in")
    d_out, m_out, v_out = adamw(w_out_shard, grad_w_out, m_w_out[0], v_w_out[0], "adamw_w_out")
    for tbl, a_in, a_out in ((grads, grad_w_in, grad_w_out), (delta, d_in, d_out), (new_m, m_in, m_out),
                             (new_v, v_in, v_out)):
        tbl["w_in"] = a_in[None]
        tbl["w_out"] = a_out[None]

    order = ["norm_w", "w_in", "ssd_conv_w", "ssd_conv_b", "ssd_dt_bias", "ssd_a_log", "ssd_d", "ssd_norm_w",
             "gdn_conv_w", "gdn_dt_bias", "gdn_a_log", "gdn_norm_w", "w_out", "final_norm_w"]
    return (loss.reshape(()), grad_x[None], *[grads[k] for k in order], *[delta[k] for k in order],
            *[new_m[k] for k in order], *[new_v[k] for k in order])
```

```python
import functools

import jax
import jax.numpy as jnp
from jax import lax
from jax.experimental import pallas as pl
from jax.experimental.pallas import tpu as pltpu

F32 = jnp.float32
MXU_DTYPE = jnp.bfloat16
COMM_DTYPE = jnp.bfloat16
MESH = pl.DeviceIdType.MESH

D_MODEL = 1024
CHUNK = 64
EPS = 1e-6
SSD_HEADS, SSD_GROUPS, SSD_STATE = 16, 2, 128
SSD_WIDTH, SSD_CONV = 1024, 1536
SSD_GW = SSD_WIDTH // SSD_GROUPS
SSD_GC = SSD_GW + 2 * SSD_STATE
GDN_HEADS, GDN_DK, GDN_DV = 8, 128, 128
GDN_W, GDN_CONV = 1024, 3072
GDN_HC = 2 * GDN_DK + GDN_DV
IN_DIM = 6688
MAIN = 6656
LANES = 128
COL_Z, COL_GATE, COL_SSD, COL_GDN = 0, 1024, 2048, 3584
GDN_HB = 8
LANE_GA, LANE_GB = 16, 24
N_DEV, N_CHIP = 8, 4
VMEM_LIMIT = 52 * 1024 * 1024

ADAM_LR, ADAM_B1, ADAM_B2, ADAM_EPS, ADAM_WD, ADAM_STEP = 0.001, 0.9, 0.999, 1e-08, 0.01, 10


def _ssd_perm(a):
    lead, nb = a.shape[:-1], SSD_GROUPS * SSD_STATE
    x = a[..., :SSD_WIDTH].reshape(*lead, SSD_GROUPS, SSD_GW)
    b = a[..., SSD_WIDTH:SSD_WIDTH + nb].reshape(*lead, SSD_GROUPS, SSD_STATE)
    c = a[..., SSD_WIDTH + nb:].reshape(*lead, SSD_GROUPS, SSD_STATE)
    return jnp.concatenate([x, b, c], axis=-1).reshape(*lead, SSD_CONV)


def _ssd_unperm(a):
    lead = a.shape[:-1]
    g = a.reshape(*lead, SSD_GROUPS, SSD_GC)
    parts = [g[..., :SSD_GW], g[..., SSD_GW:SSD_GW + SSD_STATE], g[..., SSD_GW + SSD_STATE:]]
    return jnp.concatenate([p.reshape(*lead, -1) for p in parts], axis=-1)


def _gdn_perm(a):
    lead = a.shape[:-1]
    return jnp.swapaxes(a.reshape(*lead, 3, GDN_HEADS, GDN_DK), -3, -2).reshape(*lead, GDN_CONV)


def _gdn_unperm(a):
    lead = a.shape[:-1]
    return jnp.swapaxes(a.reshape(*lead, GDN_HEADS, 3, GDN_DK), -3, -2).reshape(*lead, GDN_CONV)


def _split(a, n):
    parts, rest = [], a.astype(F32)
    for i in range(n):
        p = rest.astype(MXU_DTYPE)
        parts.append(p)
        if i < n - 1:
            rest = rest - p.astype(F32)
    return parts


def _raw_dot(a, b, ca, cb, mode="bf16"):
    d = lambda u, v: lax.dot_general(u, v, (((ca,), (cb,)), ((), ())), preferred_element_type=F32)
    if mode == "bf16":
        return d(a.astype(MXU_DTYPE), b.astype(MXU_DTYPE))
    if mode == "x3":
        (ah, al), (bh, bl) = _split(a, 2), _split(b, 2)
        return d(ah, bh) + (d(ah, bl) + d(al, bh))
    if mode == "sel_a":
        a0 = a.astype(MXU_DTYPE)
        b1, b2, b3 = _split(b, 3)
        return d(a0, b1) + (d(a0, b2) + d(a0, b3))
    assert mode == "sel_b", mode
    b0 = b.astype(MXU_DTYPE)
    a1, a2, a3 = _split(a, 3)
    return d(a1, b0) + (d(a2, b0) + d(a3, b0))


@functools.partial(jax.custom_vjp, nondiff_argnums=(2,))
def mm_nn(a, b, mode="bf16"):
    return _raw_dot(a, b, 1, 0, mode)


@functools.partial(jax.custom_vjp, nondiff_argnums=(2,))
def mm_nt(a, b, mode="bf16"):
    return _raw_dot(a, b, 1, 1, mode)


@functools.partial(jax.custom_vjp, nondiff_argnums=(2,))
def mm_tn(a, b, mode="bf16"):
    return _raw_dot(a, b, 0, 0, mode)


_SAME = {"bf16": ("bf16", "bf16"), "x3": ("x3", "x3")}
_NN_BWD = dict(_SAME, sel_a=("bf16", "sel_a"), sel_b=("sel_b", "bf16"))
_NT_BWD = dict(_SAME, sel_a=("bf16", "sel_b"), sel_b=("sel_b", "bf16"))
_TN_BWD = dict(_SAME, sel_a=("bf16", "sel_a"), sel_b=("sel_a", "bf16"))
mm_nn.defvjp(lambda a, b, m: (_raw_dot(a, b, 1, 0, m), (a, b)),
             lambda m, r, g: (mm_nt(g, r[1], _NN_BWD[m][0]), mm_tn(r[0], g, _NN_BWD[m][1])))
mm_nt.defvjp(lambda a, b, m: (_raw_dot(a, b, 1, 1, m), (a, b)),
             lambda m, r, g: (mm_nn(g, r[1], _NT_BWD[m][0]), mm_tn(g, r[0], _NT_BWD[m][1])))
mm_tn.defvjp(lambda a, b, m: (_raw_dot(a, b, 0, 0, m), (a, b)),
             lambda m, r, g: (mm_nt(r[1], g, _TN_BWD[m][0]), mm_nn(r[0], g, _TN_BWD[m][1])))


@jax.custom_jvp
def sigmoid(x):
    return 1.0 / (1.0 + jnp.exp(-x))


@sigmoid.defjvp
def _sigmoid_jvp(p, t):
    s = sigmoid(p[0])
    return s, t[0] * s * (1.0 - s)


@jax.custom_jvp
def softplus(x):
    return jnp.maximum(x, 0.0) + jnp.log(1.0 + jnp.exp(-jnp.abs(x)))


@softplus.defjvp
def _softplus_jvp(p, t):
    return softplus(p[0]), t[0] * sigmoid(p[0])


def silu(x):
    return x * sigmoid(x)


def rmsnorm(x, w):
    return x * lax.rsqrt(jnp.mean(x * x, axis=-1, keepdims=True) + EPS) * w


def _iota(shape, dim):
    return lax.broadcasted_iota(jnp.int32, shape, dim)


def _tri_inv_impl(mats):
    n = mats[0].shape[0]
    r, c = _iota((n, n), 0), _iota((n, n), 1)
    eye = jnp.where(r == c, 1.0, 0.0).astype(F32)
    blockdiag = (r >> 4) == (c >> 4)
    dot = lambda u, v: _raw_dot(u, v, 1, 0, "x3")
    each = lambda f, *ls: [f(*xs) for xs in zip(*ls)]
    dg = each(lambda a: jnp.where(blockdiag, a, 0.0), mats)
    off = each(lambda a, d: a - d, mats, dg)
    m = each(lambda d: -d, dg)
    p = each(lambda x: eye + x, m)
    pw = m
    for _ in range(3):
        pw = each(lambda x: dot(x, x), pw)
        p = each(lambda x, y: x + dot(x, y), p, pw)
    e = each(dot, p, off)
    e2 = each(lambda x: dot(x, x), e)
    q = each(lambda x: eye - x, e)
    q = each(lambda x, y: x + dot(x, y), q, e2)
    return each(dot, q, p)


def _tri_inv_bwd(ts, gs):
    x = [mm_nt(g, t, "x3") for g, t in zip(gs, ts)]
    return [-mm_tn(t, y, "x3") for t, y in zip(ts, x)]


@jax.custom_vjp
def tri_inv(mats):
    return _tri_inv_impl(mats)


def _tri_inv_fwd(mats):
    ts = _tri_inv_impl(mats)
    return ts, ts


tri_inv.defvjp(_tri_inv_fwd, lambda ts, gs: (_tri_inv_bwd(ts, gs),))


@jax.custom_vjp
def tri_inv_saved(mats, ts):
    del mats
    return ts


tri_inv_saved.defvjp(lambda mats, ts: (ts, ts),
                     lambda ts, gs: (_tri_inv_bwd(ts, gs), [jnp.zeros_like(t) for t in ts]))


def _chunk_masks():
    r, c = _iota((CHUNK, CHUNK), 0), _iota((CHUNK, CHUNK), 1)
    return r >= c, r > c, r == c


def _log_decay_cumsum(small, alog, dtb, incl):
    sp = softplus(small + dtb)
    la = -jnp.exp(alog) * sp
    tri = jnp.where(incl, 1.0, 0.0).astype(F32)
    return sp, mm_nn(tri, la, "sel_a")


def _col_of(x, lane_mask):
    return jnp.sum(jnp.where(lane_mask, x, 0.0), axis=1, keepdims=True)


def _decay_matrix(col, incl, eye):
    row = jnp.sum(jnp.where(eye, col, 0.0), axis=0, keepdims=True)
    return jnp.where(incl, jnp.exp(jnp.where(incl, col - row, 0.0)), 0.0)


def gdn_chunk(h0, qs, ks, vs, small, gates, normw, alog, dtb, states, saved_t=None):
    incl, strict, eye = _chunk_masks()
    lane = _iota((1, LANES), 1)
    last = _iota((CHUNK, 1), 0) == CHUNK - 1
    _, lac = _log_decay_cumsum(small, alog, dtb, incl)
    heads = range(len(qs))
    each = lambda f, *ls: [f(*xs) for xs in zip(*ls)]
    gc = [_col_of(lac, lane == LANE_GA + h0 + j) for j in heads]
    beta = [sigmoid(_col_of(small, lane == LANE_GB + h0 + j)) for j in heads]
    decay = each(lambda x: _decay_matrix(x, incl, eye), gc)
    gl = each(lambda x: jnp.sum(jnp.where(last, x, 0.0), axis=0, keepdims=True), gc)
    q = each(lambda x: x * lax.rsqrt(jnp.sum(x * x, axis=-1, keepdims=True) + EPS) * (GDN_DK ** -0.5), qs)
    k = each(lambda x: x * lax.rsqrt(jnp.sum(x * x, axis=-1, keepdims=True) + EPS), ks)
    kb = each(lambda x, b: x * b, k, beta)
    a = each(lambda x, y, d: jnp.where(strict, mm_nt(x, y) * d, 0.0), kb, k, decay)
    t = tri_inv(a) if saved_t is None else tri_inv_saved(a, saved_t)
    eg = each(jnp.exp, gc)
    u = each(lambda x, v, b: mm_nn(x, v * b, "x3"), t, vs, beta)
    w = each(lambda x, y, e: mm_nn(x, y * e, "x3"), t, kb, eg)
    attn = each(lambda x, y, d: mm_nt(x, y) * d, q, k, decay)
    v_new = each(lambda x, y, s: x - mm_nn(y, s), u, w, states)
    o = each(lambda x, e, s, at, vn: mm_nn(x * e, s) + mm_nn(at, vn), q, eg, states, attn, v_new)
    new_states = each(lambda s, x, y, l, c: s * jnp.exp(l) + mm_tn(y * jnp.exp(l - c), x), states, v_new, k, gl, gc)
    ys = each(lambda x, gt: rmsnorm(x, normw) * silu(gt), o, gates)
    return ys, new_states, t


@jax.custom_vjp
def split_lanes(x):
    return [x[:, i * LANES:(i + 1) * LANES] for i in range(x.shape[1] // LANES)]


@jax.custom_vjp
def join_lanes(xs):
    return jnp.concatenate(xs, axis=1)


split_lanes.defvjp(lambda x: (split_lanes(x), None), lambda _, gs: (join_lanes(gs),))
join_lanes.defvjp(lambda xs: (join_lanes(xs), None), lambda _, g: (split_lanes(g),))


def ssd_chunk(g, xs, bm, cm, z, small, normw, alog, dtb, dvec, state):
    incl, _, eye = _chunk_masks()
    lane = _iota((1, LANES), 1)
    last = _iota((CHUNK, 1), 0) == CHUNK - 1
    hpg = SSD_HEADS // SSD_GROUPS
    sp, lac = _log_decay_cumsum(small, alog, dtb, incl)
    sel = jnp.where(_iota((LANES, SSD_GW), 0) == g * hpg + (_iota((LANES, SSD_GW), 1) >> 6), 1.0, 0.0).astype(F32)
    lac_last = jnp.sum(jnp.where(last, lac, 0.0), axis=0, keepdims=True)
    dt_e = mm_nn(sp, sel, "sel_b")
    elac_e = mm_nn(jnp.exp(lac), sel, "sel_b")
    toend_e = mm_nn(jnp.exp(lac_last - lac), sel, "sel_b")
    row8 = _iota((8, LANES), 0)
    two = jnp.where(row8 == 0, dvec, 0.0) + jnp.where(row8 == 1, jnp.exp(lac_last), 0.0)
    two_e = mm_nn(two, sel, "sel_b")
    row8e = _iota((8, SSD_GW), 0)
    d_e = jnp.sum(jnp.where(row8e == 0, two_e, 0.0), axis=0, keepdims=True)
    chunk_e = jnp.sum(jnp.where(row8e == 1, two_e, 0.0), axis=0, keepdims=True)
    xdt = xs * dt_e
    cb = mm_nt(cm, bm)
    y = mm_nn(cm, state) * elac_e + xs * d_e
    x_pairs = split_lanes(xdt)
    half = _iota((1, LANES), 1) >> 6
    lms = [_decay_matrix(_col_of(lac, lane == g * hpg + j), incl, eye) for j in range(hpg)]
    terms = [mm_nn(cb * lms[j], jnp.where(half == j % 2, x_pairs[j // 2], 0.0)) for j in range(hpg)]
    y = y + join_lanes([terms[2 * p] + terms[2 * p + 1] for p in range(hpg // 2)])
    new_state = state * chunk_e + mm_tn(bm, xdt * toend_e)
    yg = y * silu(z)
    return rmsnorm(yg, normw), new_state


def _params(sem=None):
    return pltpu.CompilerParams(dimension_semantics=sem, vmem_limit_bytes=VMEM_LIMIT)


def _full(shape):
    n = len(shape)
    return pl.BlockSpec(shape, lambda *_: (0,) * n)


ANY = pl.BlockSpec(memory_space=pl.ANY)
HBM = pl.BlockSpec(memory_space=pltpu.HBM)


def in_proj(x, normw, w_main, w_small):
    t = x.shape[0]
    tm, tn = min(512, t), 512

    def body(x_ref, nw_ref, wm_ref, ws_ref, pm_ref, ps_ref, u_ref):
        @pl.when(pl.program_id(1) == 0)
        def _():
            u = rmsnorm(x_ref[...], nw_ref[...]).astype(MXU_DTYPE)
            u_ref[...] = u
            ps_ref[...] = _raw_dot(u, ws_ref[...], 1, 0)
        pm_ref[...] = _raw_dot(u_ref[...], wm_ref[...], 1, 0)

    return pl.pallas_call(
        body, name="in_proj", grid=(t // tm, MAIN // tn),
        in_specs=[pl.BlockSpec((tm, D_MODEL), lambda i, j: (i, 0)), _full((1, D_MODEL)),
                  pl.BlockSpec((D_MODEL, tn), lambda i, j: (0, j)), _full((D_MODEL, LANES))],
        out_specs=[pl.BlockSpec((tm, tn), lambda i, j: (i, j)), pl.BlockSpec((tm, LANES), lambda i, j: (i, 0)),
                   pl.BlockSpec((tm, D_MODEL), lambda i, j: (i, 0))],
        out_shape=[jax.ShapeDtypeStruct((t, MAIN), F32), jax.ShapeDtypeStruct((t, LANES), F32),
                   jax.ShapeDtypeStruct((t, D_MODEL), MXU_DTYPE)],
        compiler_params=_params(("arbitrary", "arbitrary")),
    )(x, normw, w_main, w_small)


CONV_TC = 512
HALO = 8


def _shift_down(cur, prev, s):
    rolled = pltpu.roll(cur, s, 0)
    top = jnp.where(_iota((HALO, cur.shape[1]), 0) < s, pltpu.roll(prev, s, 0), rolled[:HALO])
    return jnp.concatenate([top, rolled[HALO:]], axis=0)


def _shift_up(cur, nxt, s):
    n = cur.shape[0]
    rolled = pltpu.roll(cur, n - s, 0)
    bot = jnp.where(_iota((HALO, cur.shape[1]), 0) >= HALO - s, pltpu.roll(nxt, HALO - s, 0), rolled[n - HALO:])
    return jnp.concatenate([rolled[:n - HALO], bot], axis=0)


def _conv_pre(cur, prev, w_ref, b):
    acc = cur * w_ref[3:4, :] + b
    shifted = [cur]
    for s in (1, 2, 3):
        sh = _shift_down(cur, prev, s)
        shifted.append(sh)
        acc = acc + sh * w_ref[3 - s:4 - s, :]
    return acc, shifted


def conv_fwd(proj_main, col0, width, w, b, name):
    t = proj_main.shape[0]
    tt, c0 = min(512, t), col0 // CONV_TC

    def body(cur_ref, prev_ref, w_ref, b_ref, out_ref):
        prev = jnp.where(pl.program_id(0) > 0, prev_ref[...], 0.0)
        pre, _ = _conv_pre(cur_ref[...], prev, w_ref, b_ref[...])
        out_ref[...] = silu(pre)

    return pl.pallas_call(
        body, name=name, grid=(t // tt, width // CONV_TC),
        in_specs=[pl.BlockSpec((tt, CONV_TC), lambda i, j: (i, c0 + j)),
                  pl.BlockSpec((HALO, CONV_TC), lambda i, j: (jnp.maximum(i * (tt // HALO) - 1, 0), c0 + j)),
                  pl.BlockSpec((4, CONV_TC), lambda i, j: (0, j)), pl.BlockSpec((1, CONV_TC), lambda i, j: (0, j))],
        out_specs=pl.BlockSpec((tt, CONV_TC), lambda i, j: (i, j)),
        out_shape=jax.ShapeDtypeStruct((t, width), F32),
        compiler_params=_params(("arbitrary", "arbitrary")),
    )(proj_main, proj_main, w, b)


def conv_bwd_pre(proj_main, col0, width, w, b, dout, name):
    t = proj_main.shape[0]
    tt, c0 = min(512, t), col0 // CONV_TC

    def body(cur_ref, prev_ref, w_ref, b_ref, do_ref, dpre_ref, dwb_ref):
        i = pl.program_id(1)
        prev = jnp.where(i > 0, prev_ref[...], 0.0)
        pre, shifted = _conv_pre(cur_ref[...], prev, w_ref, b_ref[...])
        sg = sigmoid(pre)
        dpre = do_ref[...] * (sg * (1.0 + pre * (1.0 - sg)))
        dpre_ref[...] = dpre
        row = _iota((HALO, CONV_TC), 0)
        upd = jnp.where(row == 4, jnp.sum(dpre, axis=0, keepdims=True), 0.0)
        for s in range(4):
            upd = upd + jnp.where(row == 3 - s, jnp.sum(dpre * shifted[s], axis=0, keepdims=True), 0.0)

        @pl.when(i == 0)
        def _():
            dwb_ref[...] = upd

        @pl.when(i > 0)
        def _():
            dwb_ref[...] += upd

    return pl.pallas_call(
        body, name=name, grid=(width // CONV_TC, t // tt),
        in_specs=[pl.BlockSpec((tt, CONV_TC), lambda j, i: (i, c0 + j)),
                  pl.BlockSpec((HALO, CONV_TC), lambda j, i: (jnp.maximum(i * (tt // HALO) - 1, 0), c0 + j)),
                  pl.BlockSpec((4, CONV_TC), lambda j, i: (0, j)), pl.BlockSpec((1, CONV_TC), lambda j, i: (0, j)),
                  pl.BlockSpec((tt, CONV_TC), lambda j, i: (i, j))],
        out_specs=[pl.BlockSpec((tt, CONV_TC), lambda j, i: (i, j)), pl.BlockSpec((HALO, CONV_TC), lambda j, i: (0, j))],
        out_shape=[jax.ShapeDtypeStruct((t, width), F32), jax.ShapeDtypeStruct((HALO, width), F32)],
        compiler_params=_params(("arbitrary", "arbitrary")),
    )(proj_main, proj_main, w, b, dout)


def conv_bwd_x(dproj_main, col0, width, w, dpre, name):
    t = dpre.shape[0]
    tt, c0 = min(512, t), col0 // CONV_TC
    nt = t // tt

    def body(alias_ref, cur_ref, nxt_ref, w_ref, out_ref):
        del alias_ref
        nxt = jnp.where(pl.program_id(0) < nt - 1, nxt_ref[...], 0.0)
        cur = cur_ref[...]
        acc = cur * w_ref[3:4, :]
        for s in (1, 2, 3):
            acc = acc + _shift_up(cur, nxt, s) * w_ref[3 - s:4 - s, :]
        out_ref[...] = acc

    return pl.pallas_call(
        body, name=name, grid=(nt, width // CONV_TC),
        in_specs=[ANY, pl.BlockSpec((tt, CONV_TC), lambda i, j: (i, j)),
                  pl.BlockSpec((HALO, CONV_TC), lambda i, j: (jnp.minimum((i + 1) * (tt // HALO), t // HALO - 1), j)),
                  pl.BlockSpec((4, CONV_TC), lambda i, j: (0, j))],
        out_specs=pl.BlockSpec((tt, CONV_TC), lambda i, j: (i, c0 + j)),
        out_shape=jax.ShapeDtypeStruct(dproj_main.shape, F32),
        input_output_aliases={0: 0},
        compiler_params=_params(("arbitrary", "arbitrary")),
    )(dproj_main, dpre, dpre, w)


def _ssd_parts(xbc_ref):
    return xbc_ref[:, :SSD_GW], xbc_ref[:, SSD_GW:SSD_GW + SSD_STATE], xbc_ref[:, SSD_GW + SSD_STATE:]


def _gdn_parts(qkv_ref):
    part = lambda o: [qkv_ref[:, j * GDN_HC + o:j * GDN_HC + o + GDN_DK] for j in range(GDN_HB)]
    return part(0), part(GDN_DK), part(2 * GDN_DK)


def _head_cols(ref):
    return [ref[:, j * GDN_DV:(j + 1) * GDN_DV] for j in range(GDN_HB)]


def _first_head():
    return 0 if GDN_HB == GDN_HEADS else pl.program_id(1) * GDN_HB


def ssd_fwd(conv_ssd, proj_main, proj_small, normw, alog, dtb, dvec):
    t = conv_ssd.shape[0]
    nc = t // CHUNK

    def body(xbc_ref, z_ref, sm_ref, nw_ref, al_ref, db_ref, dv_ref, y_ref, hist_ref, state_ref):
        g = pl.program_id(1)

        @pl.when(pl.program_id(0) == 0)
        def _():
            state_ref[g] = jnp.zeros((SSD_STATE, SSD_GW), F32)

        state = state_ref[g]
        hist_ref[0, 0] = state
        y, new_state = ssd_chunk(g, *_ssd_parts(xbc_ref), z_ref[...], sm_ref[...], nw_ref[...], al_ref[...],
                                 db_ref[...], dv_ref[...], state)
        y_ref[...] = y.astype(MXU_DTYPE)
        state_ref[g] = new_state

    return pl.pallas_call(
        body, name="ssd_fwd", grid=(nc, SSD_GROUPS),
        in_specs=[pl.BlockSpec((CHUNK, SSD_GC), lambda c, g: (c, g)),
                  pl.BlockSpec((CHUNK, SSD_GW), lambda c, g: (c, COL_Z // SSD_GW + g)),
                  pl.BlockSpec((CHUNK, LANES), lambda c, g: (c, 0)),
                  pl.BlockSpec((1, SSD_GW), lambda c, g: (0, g)), _full((1, LANES)), _full((1, LANES)), _full((1, LANES))],
        out_specs=[pl.BlockSpec((CHUNK, SSD_GW), lambda c, g: (c, g)),
                   pl.BlockSpec((1, 1, SSD_STATE, SSD_GW), lambda c, g: (c, g, 0, 0))],
        out_shape=[jax.ShapeDtypeStruct((t, SSD_WIDTH), MXU_DTYPE),
                   jax.ShapeDtypeStruct((nc, SSD_GROUPS, SSD_STATE, SSD_GW), F32)],
        scratch_shapes=[pltpu.VMEM((SSD_GROUPS, SSD_STATE, SSD_GW), F32)],
        compiler_params=_params(("arbitrary", "arbitrary")),
    )(conv_ssd, proj_main, proj_small, normw, alog, dtb, dvec)


def _accumulate(ref, first, value):
    @pl.when(first)
    def _():
        ref[...] = value

    @pl.when(jnp.logical_not(first))
    def _():
        ref[...] += value


def ssd_bwd(conv_ssd, proj_main, proj_small, normw, alog, dtb, dvec, hist, dy):
    t = conv_ssd.shape[0]
    nc = t // CHUNK
    rev = lambda c: nc - 1 - c

    def body(xbc_ref, z_ref, sm_ref, nw_ref, al_ref, db_ref, dv_ref, hist_ref, dy_ref,
             dxbc_ref, dz_ref, dsm_ref, dnw_ref, dal_ref, ddb_ref, ddv_ref, dstate_ref):
        c, g = pl.program_id(0), pl.program_id(1)

        @pl.when(c == 0)
        def _():
            dstate_ref[g] = jnp.zeros((SSD_STATE, SSD_GW), F32)

        fn = functools.partial(ssd_chunk, g)
        _, vjp = jax.vjp(fn, *_ssd_parts(xbc_ref), z_ref[...], sm_ref[...], nw_ref[...], al_ref[...], db_ref[...],
                         dv_ref[...], hist_ref[0, 0])
        dxs, dbm, dcm, dz, dsm, dnw, dal, ddb, ddv, dstate = vjp((dy_ref[...], dstate_ref[g]))
        dxbc_ref[:, :SSD_GW] = dxs
        dxbc_ref[:, SSD_GW:SSD_GW + SSD_STATE] = dbm
        dxbc_ref[:, SSD_GW + SSD_STATE:] = dcm
        dz_ref[...] = dz
        dstate_ref[g] = dstate
        _accumulate(dsm_ref, g == 0, dsm)
        first = jnp.logical_and(c == 0, g == 0)
        _accumulate(dal_ref, first, dal)
        _accumulate(ddb_ref, first, ddb)
        _accumulate(ddv_ref, first, ddv)

        @pl.when(c == 0)
        def _():
            dnw_ref[g] = dnw

        @pl.when(c > 0)
        def _():
            dnw_ref[g] += dnw

    return pl.pallas_call(
        body, name="ssd_bwd", grid=(nc, SSD_GROUPS),
        in_specs=[pl.BlockSpec((CHUNK, SSD_GC), lambda c, g: (rev(c), g)),
                  pl.BlockSpec((CHUNK, SSD_GW), lambda c, g: (rev(c), COL_Z // SSD_GW + g)),
                  pl.BlockSpec((CHUNK, LANES), lambda c, g: (rev(c), 0)),
                  pl.BlockSpec((1, SSD_GW), lambda c, g: (0, g)), _full((1, LANES)), _full((1, LANES)), _full((1, LANES)),
                  pl.BlockSpec((1, 1, SSD_STATE, SSD_GW), lambda c, g: (rev(c), g, 0, 0)),
                  pl.BlockSpec((CHUNK, SSD_GW), lambda c, g: (rev(c), g))],
        out_specs=[pl.BlockSpec((CHUNK, SSD_GC), lambda c, g: (rev(c), g)),
                   pl.BlockSpec((CHUNK, SSD_GW), lambda c, g: (rev(c), COL_Z // SSD_GW + g)),
                   pl.BlockSpec((CHUNK, LANES), lambda c, g: (rev(c), 0)),
                   _full((SSD_GROUPS, 1, SSD_GW)), _full((1, LANES)), _full((1, LANES)), _full((1, LANES))],
        out_shape=[jax.ShapeDtypeStruct((t, SSD_CONV), F32), jax.ShapeDtypeStruct((t, MAIN), F32),
                   jax.ShapeDtypeStruct((t, LANES), F32), jax.ShapeDtypeStruct((SSD_GROUPS, 1, SSD_GW), F32),
                   jax.ShapeDtypeStruct((1, LANES), F32), jax.ShapeDtypeStruct((1, LANES), F32),
                   jax.ShapeDtypeStruct((1, LANES), F32)],
        scratch_shapes=[pltpu.VMEM((SSD_GROUPS, SSD_STATE, SSD_GW), F32)],
        compiler_params=_params(("arbitrary", "arbitrary")),
    )(conv_ssd, proj_main, proj_small, normw, alog, dtb, dvec, hist, dy)


def gdn_fwd(conv_gdn, proj_main, proj_small, normw, alog, dtb):
    t = conv_gdn.shape[0]
    nc = t // CHUNK

    hb = GDN_HB
    gate_blk = COL_GATE // (GDN_DV * hb)

    def body(qkv_ref, gate_ref, sm_ref, nw_ref, al_ref, db_ref, y_ref, hist_ref, t_ref, state_ref):
        h0 = _first_head()

        @pl.when(pl.program_id(0) == 0)
        def _():
            for j in range(hb):
                state_ref[h0 + j] = jnp.zeros((GDN_DK, GDN_DV), F32)

        states = [state_ref[h0 + j] for j in range(hb)]
        for j in range(hb):
            hist_ref[0, j] = states[j]
        qs, ks, vs = _gdn_parts(qkv_ref)
        ys, new_states, ts = gdn_chunk(h0, qs, ks, vs, sm_ref[...], _head_cols(gate_ref), nw_ref[...], al_ref[...],
                                       db_ref[...], states)
        for j in range(hb):
            y_ref[:, j * GDN_DV:(j + 1) * GDN_DV] = ys[j].astype(MXU_DTYPE)
            state_ref[h0 + j] = new_states[j]
            t_ref[0, j] = ts[j]

    return pl.pallas_call(
        body, name="gdn_fwd", grid=(nc, GDN_HEADS // hb),
        in_specs=[pl.BlockSpec((CHUNK, GDN_HC * hb), lambda c, h: (c, h)),
                  pl.BlockSpec((CHUNK, GDN_DV * hb), lambda c, h: (c, gate_blk + h)),
                  pl.BlockSpec((CHUNK, LANES), lambda c, h: (c, 0)),
                  _full((1, GDN_DV)), _full((1, LANES)), _full((1, LANES))],
        out_specs=[pl.BlockSpec((CHUNK, GDN_DV * hb), lambda c, h: (c, h)),
                   pl.BlockSpec((1, hb, GDN_DK, GDN_DV), lambda c, h: (c, h, 0, 0)),
                   pl.BlockSpec((1, hb, CHUNK, CHUNK), lambda c, h: (c, h, 0, 0))],
        out_shape=[jax.ShapeDtypeStruct((t, GDN_W), MXU_DTYPE),
                   jax.ShapeDtypeStruct((nc, GDN_HEADS, GDN_DK, GDN_DV), F32),
                   jax.ShapeDtypeStruct((nc, GDN_HEADS, CHUNK, CHUNK), F32)],
        scratch_shapes=[pltpu.VMEM((GDN_HEADS, GDN_DK, GDN_DV), F32)],
        compiler_params=_params(("arbitrary", "arbitrary")),
    )(conv_gdn, proj_main, proj_small, normw, alog, dtb)


def gdn_bwd(dproj_main, conv_gdn, proj_main, proj_small, normw, alog, dtb, hist, t_inv, dy):
    t = conv_gdn.shape[0]
    nc = t // CHUNK
    rev = lambda c: nc - 1 - c
    hb = GDN_HB
    gate_blk = COL_GATE // (GDN_DV * hb)

    def body(alias_ref, qkv_ref, gate_ref, sm_ref, nw_ref, al_ref, db_ref, hist_ref, t_ref, dy_ref,
             dgate_ref, dqkv_ref, dsm_ref, dnw_ref, dal_ref, ddb_ref, dstate_ref):
        del alias_ref
        c, h = pl.program_id(0), pl.program_id(1)
        h0 = _first_head()

        @pl.when(c == 0)
        def _():
            for j in range(hb):
                dstate_ref[h0 + j] = jnp.zeros((GDN_DK, GDN_DV), F32)

        saved = [t_ref[0, j] for j in range(hb)]

        def fn(qs, ks, vs, small, gates, nw, al, db, states):
            return gdn_chunk(h0, qs, ks, vs, small, gates, nw, al, db, states, saved)[:2]

        qs, ks, vs = _gdn_parts(qkv_ref)
        _, vjp = jax.vjp(fn, qs, ks, vs, sm_ref[...], _head_cols(gate_ref), nw_ref[...], al_ref[...], db_ref[...],
                         [hist_ref[0, j] for j in range(hb)])
        dqs, dks, dvs, dsm, dgates, dnw, dal, ddb, dstates = vjp(
            (_head_cols(dy_ref), [dstate_ref[h0 + j] for j in range(hb)]))
        for j in range(hb):
            base = j * GDN_HC
            dqkv_ref[:, base:base + GDN_DK] = dqs[j]
            dqkv_ref[:, base + GDN_DK:base + 2 * GDN_DK] = dks[j]
            dqkv_ref[:, base + 2 * GDN_DK:base + GDN_HC] = dvs[j]
            dgate_ref[:, j * GDN_DV:(j + 1) * GDN_DV] = dgates[j]
            dstate_ref[h0 + j] = dstates[j]
        _accumulate(dsm_ref, h == 0, dsm)
        first = jnp.logical_and(c == 0, h == 0)
        _accumulate(dnw_ref, first, dnw)
        _accumulate(dal_ref, first, dal)
        _accumulate(ddb_ref, first, ddb)

    return pl.pallas_call(
        body, name="gdn_bwd", grid=(nc, GDN_HEADS // hb),
        in_specs=[ANY, pl.BlockSpec((CHUNK, GDN_HC * hb), lambda c, h: (rev(c), h)),
                  pl.BlockSpec((CHUNK, GDN_DV * hb), lambda c, h: (rev(c), gate_blk + h)),
                  pl.BlockSpec((CHUNK, LANES), lambda c, h: (rev(c), 0)),
                  _full((1, GDN_DV)), _full((1, LANES)), _full((1, LANES)),
                  pl.BlockSpec((1, hb, GDN_DK, GDN_DV), lambda c, h: (rev(c), h, 0, 0)),
                  pl.BlockSpec((1, hb, CHUNK, CHUNK), lambda c, h: (rev(c), h, 0, 0)),
                  pl.BlockSpec((CHUNK, GDN_DV * hb), lambda c, h: (rev(c), h))],
        out_specs=[pl.BlockSpec((CHUNK, GDN_DV * hb), lambda c, h: (rev(c), gate_blk + h)),
                   pl.BlockSpec((CHUNK, GDN_HC * hb), lambda c, h: (rev(c), h)),
                   pl.BlockSpec((CHUNK, LANES), lambda c, h: (rev(c), 0)),
                   _full((1, GDN_DV)), _full((1, LANES)), _full((1, LANES))],
        out_shape=[jax.ShapeDtypeStruct(dproj_main.shape, F32), jax.ShapeDtypeStruct((t, GDN_CONV), F32),
                   jax.ShapeDtypeStruct((t, LANES), F32), jax.ShapeDtypeStruct((1, GDN_DV), F32),
                   jax.ShapeDtypeStruct((1, LANES), F32), jax.ShapeDtypeStruct((1, LANES), F32)],
        scratch_shapes=[pltpu.VMEM((GDN_HEADS, GDN_DK, GDN_DV), F32)],
        input_output_aliases={0: 0},
        compiler_params=_params(("arbitrary", "arbitrary")),
    )(dproj_main, conv_gdn, proj_main, proj_small, normw, alog, dtb, hist, t_inv, dy)


def out_proj_loss(x, y_ssd, y_gdn, w_out, final_w, target):
    t = x.shape[0]
    tm = min(256, t)

    def body(x_ref, ys_ref, yg_ref, wo_ref, fw_ref, tg_ref, loss_ref, dhid_ref, dys_ref, dyg_ref, dwo_ref, dfw_ref):
        i = pl.program_id(0)
        ys, yg = ys_ref[...], yg_ref[...]
        wo_s, wo_g = wo_ref[:SSD_WIDTH, :], wo_ref[SSD_WIDTH:, :]
        hid = x_ref[...] + _raw_dot(ys, wo_s, 1, 0) + _raw_dot(yg, wo_g, 1, 0)
        out, vjp = jax.vjp(rmsnorm, hid, fw_ref[...])
        err = out - tg_ref[...]
        loss = 0.5 * jnp.sum(jnp.mean(err * err, axis=-1, keepdims=True), axis=0, keepdims=True)
        dhid, dfw = vjp(err * (1.0 / D_MODEL))
        dhid_ref[...] = dhid
        dys_ref[...] = _raw_dot(dhid, wo_s, 1, 1)
        dyg_ref[...] = _raw_dot(dhid, wo_g, 1, 1)
        first = i == 0
        _accumulate(loss_ref, first, jnp.broadcast_to(loss, loss_ref.shape))
        _accumulate(dfw_ref, first, dfw)

        @pl.when(first)
        def _():
            dwo_ref[:SSD_WIDTH, :] = _raw_dot(ys, dhid, 0, 0)
            dwo_ref[SSD_WIDTH:, :] = _raw_dot(yg, dhid, 0, 0)

        @pl.when(i > 0)
        def _():
            dwo_ref[:SSD_WIDTH, :] += _raw_dot(ys, dhid, 0, 0)
            dwo_ref[SSD_WIDTH:, :] += _raw_dot(yg, dhid, 0, 0)

    row = lambda w: pl.BlockSpec((tm, w), lambda i: (i, 0))
    return pl.pallas_call(
        body, name="out_proj_loss", grid=(t // tm,),
        in_specs=[row(D_MODEL), row(SSD_WIDTH), row(GDN_W), _full((SSD_WIDTH + GDN_W, D_MODEL)), _full((1, D_MODEL)),
                  row(D_MODEL)],
        out_specs=[_full((8, LANES)), row(D_MODEL), row(SSD_WIDTH), row(GDN_W), _full((SSD_WIDTH + GDN_W, D_MODEL)),
                   _full((1, D_MODEL))],
        out_shape=[jax.ShapeDtypeStruct((8, LANES), F32), jax.ShapeDtypeStruct((t, D_MODEL), F32),
                   jax.ShapeDtypeStruct((t, SSD_WIDTH), F32), jax.ShapeDtypeStruct((t, GDN_W), F32),
                   jax.ShapeDtypeStruct((SSD_WIDTH + GDN_W, D_MODEL), F32), jax.ShapeDtypeStruct((1, D_MODEL), F32)],
        compiler_params=_params(("arbitrary",)),
    )(x, y_ssd, y_gdn, w_out, final_w, target)


def in_proj_bwd_x(x, normw, w_main, w_small, dproj_main, dsmall_a, dsmall_b, dhid):
    t = x.shape[0]
    tm, tk = min(512, t), 512
    nk = MAIN // tk

    def body(x_ref, nw_ref, wm_ref, ws_ref, dp_ref, da_ref, db_ref, dh_ref, gx_ref, dnw_ref, acc_ref):
        i, k = pl.program_id(0), pl.program_id(1)
        part = _raw_dot(dp_ref[...], wm_ref[...], 1, 1)

        @pl.when(k == 0)
        def _():
            acc_ref[...] = part + _raw_dot(da_ref[...] + db_ref[...], ws_ref[...], 1, 1)

        @pl.when(k > 0)
        def _():
            acc_ref[...] += part

        @pl.when(k == nk - 1)
        def _():
            _, vjp = jax.vjp(rmsnorm, x_ref[...], nw_ref[...])
            dx, dnw = vjp(acc_ref[...])
            gx_ref[...] = dx + dh_ref[...]
            _accumulate(dnw_ref, i == 0, dnw)

    row = lambda w: pl.BlockSpec((tm, w), lambda i, k: (i, 0))
    return pl.pallas_call(
        body, name="in_proj_bwd_x", grid=(t // tm, nk),
        in_specs=[row(D_MODEL), _full((1, D_MODEL)), pl.BlockSpec((D_MODEL, tk), lambda i, k: (0, k)),
                  _full((D_MODEL, LANES)), pl.BlockSpec((tm, tk), lambda i, k: (i, k)), row(LANES), row(LANES),
                  row(D_MODEL)],
        out_specs=[row(D_MODEL), _full((1, D_MODEL))],
        out_shape=[jax.ShapeDtypeStruct((t, D_MODEL), F32), jax.ShapeDtypeStruct((1, D_MODEL), F32)],
        scratch_shapes=[pltpu.VMEM((tm, D_MODEL), F32)],
        compiler_params=_params(("arbitrary", "arbitrary")),
    )(x, normw, w_main, w_small, dproj_main, dsmall_a, dsmall_b, dhid)


def in_proj_bwd_w(u, dproj_main, dsmall_a, dsmall_b):
    t = u.shape[0]
    tm, tn = min(512, t), 512

    def body(u_ref, dp_ref, da_ref, db_ref, dwm_ref, dws_ref):
        j, i = pl.program_id(0), pl.program_id(1)
        uu = u_ref[...]
        _accumulate(dwm_ref, i == 0, _raw_dot(uu, dp_ref[...], 0, 0))

        @pl.when(j == 0)
        def _():
            _accumulate(dws_ref, i == 0, _raw_dot(uu, da_ref[...] + db_ref[...], 0, 0))

    return pl.pallas_call(
        body, name="in_proj_bwd_w", grid=(MAIN // tn, t // tm),
        in_specs=[pl.BlockSpec((tm, D_MODEL), lambda j, i: (i, 0)), pl.BlockSpec((tm, tn), lambda j, i: (i, j)),
                  pl.BlockSpec((tm, LANES), lambda j, i: (i, 0)), pl.BlockSpec((tm, LANES), lambda j, i: (i, 0))],
        out_specs=[pl.BlockSpec((D_MODEL, tn), lambda j, i: (0, j)), _full((D_MODEL, LANES))],
        out_shape=[jax.ShapeDtypeStruct((D_MODEL, MAIN), F32), jax.ShapeDtypeStruct((D_MODEL, LANES), F32)],
        compiler_params=_params(("arbitrary", "arbitrary")),
    )(u, dproj_main, dsmall_a, dsmall_b)


def sum_slabs(a, name):
    n, rows, cols = a.shape
    tr = 64 if rows % 64 == 0 else rows

    def body(a_ref, o_ref):
        acc = a_ref[0].astype(F32)
        for d in range(1, n):
            acc = acc + a_ref[d].astype(F32)
        o_ref[...] = acc

    return pl.pallas_call(
        body, name=name, grid=(rows // tr,),
        in_specs=[pl.BlockSpec((n, tr, cols), lambda i: (0, i, 0))],
        out_specs=pl.BlockSpec((tr, cols), lambda i: (i, 0)),
        out_shape=jax.ShapeDtypeStruct((rows, cols), F32),
        compiler_params=_params(("arbitrary",)),
    )(a)


def adamw(w, g, m, v, name):
    rows, cols = w.shape
    tr = 128 if rows % 128 == 0 else rows

    def body(w_ref, g_ref, m_ref, v_ref, d_ref, nm_ref, nv_ref):
        gg = g_ref[...]
        nm = ADAM_B1 * m_ref[...] + (1.0 - ADAM_B1) * gg
        nv = ADAM_B2 * v_ref[...] + (1.0 - ADAM_B2) * (gg * gg)
        m_hat = nm / (1.0 - ADAM_B1 ** ADAM_STEP)
        v_hat = nv / (1.0 - ADAM_B2 ** ADAM_STEP)
        d_ref[...] = -ADAM_LR * (m_hat / (jnp.sqrt(v_hat) + ADAM_EPS) + ADAM_WD * w_ref[...])
        nm_ref[...] = nm
        nv_ref[...] = nv

    spec = pl.BlockSpec((tr, cols), lambda i: (i, 0))
    shp = jax.ShapeDtypeStruct((rows, cols), F32)
    return pl.pallas_call(
        body, name=name, grid=(rows // tr,), in_specs=[spec] * 4, out_specs=[spec] * 3, out_shape=[shp] * 3,
        compiler_params=_params(("arbitrary",)),
    )(w, g, m, v)


def _my_place():
    return lax.axis_index("x"), lax.axis_index("y"), lax.axis_index("c")


def gather_weights(shards):
    n = len(shards)

    def body(*refs):
        srcs, outs = refs[:n], refs[n:2 * n]
        send_sems, recv_sems, local_sems = refs[2 * n:]
        x, y, c = _my_place()
        me = 2 * x + y
        chips = [(1 - x, y), (x, 1 - y), (1 - x, 1 - y)]
        local = [pltpu.make_async_copy(srcs[i], outs[i].at[me], local_sems.at[i]) for i in range(n)]
        for cp in local:
            cp.start()
        sends = []
        for j, (px, py) in enumerate(chips):
            for i in range(n):
                sends.append(pltpu.make_async_remote_copy(
                    src_ref=srcs[i], dst_ref=outs[i].at[me], send_sem=send_sems.at[j * n + i],
                    recv_sem=recv_sems.at[j * n + i], device_id=(px, py, c), device_id_type=MESH))
        for cp in sends:
            cp.start()
        for j, (px, py) in enumerate(chips):
            for i in range(n):
                pltpu.make_async_remote_copy(
                    src_ref=srcs[i], dst_ref=outs[i].at[2 * px + py], send_sem=send_sems.at[j * n + i],
                    recv_sem=recv_sems.at[j * n + i], device_id=(px, py, c), device_id_type=MESH).wait_recv()
        for cp in sends:
            cp.wait_send()
        for cp in local:
            cp.wait()

    return pl.pallas_call(
        body, name="gather_weights",
        in_specs=[HBM] * n, out_specs=[HBM] * n,
        out_shape=[jax.ShapeDtypeStruct((N_CHIP,) + s.shape, s.dtype) for s in shards],
        scratch_shapes=[pltpu.SemaphoreType.DMA((3 * n,)), pltpu.SemaphoreType.DMA((3 * n,)),
                        pltpu.SemaphoreType.DMA((n,))],
    )(*shards)


def _peer(x, y, c, mask):
    mx, my, mc = (mask >> 2) & 1, (mask >> 1) & 1, mask & 1
    return (x ^ mx if mx else x, y ^ my if my else y, c ^ mc if mc else c)


def exchange_slabs(slabbed, replicated):
    ns, nr = len(slabbed), len(replicated)
    n = ns + nr

    def body(*refs):
        srcs, outs = refs[:n], refs[n:2 * n]
        send_sems, recv_sems, local_sems = refs[2 * n:]
        x, y, c = _my_place()
        me = 4 * x + 2 * y + c

        def piece(i, dev):
            return srcs[i].at[dev] if i < ns else srcs[i]

        local = [pltpu.make_async_copy(piece(i, me), outs[i].at[me], local_sems.at[i]) for i in range(n)]
        for cp in local:
            cp.start()
        sends = []
        for mask in range(1, N_DEV):
            px, py, pc = _peer(x, y, c, mask)
            dev = 4 * px + 2 * py + pc
            for i in range(n):
                k = (mask - 1) * n + i
                sends.append(pltpu.make_async_remote_copy(
                    src_ref=piece(i, dev), dst_ref=outs[i].at[me], send_sem=send_sems.at[k], recv_sem=recv_sems.at[k],
                    device_id=(px, py, pc), device_id_type=MESH))
        for cp in sends:
            cp.start()
        for mask in range(1, N_DEV):
            px, py, pc = _peer(x, y, c, mask)
            dev = 4 * px + 2 * py + pc
            for i in range(n):
                k = (mask - 1) * n + i
                pltpu.make_async_remote_copy(
                    src_ref=piece(i, dev), dst_ref=outs[i].at[dev], send_sem=send_sems.at[k], recv_sem=recv_sems.at[k],
                    device_id=(px, py, pc), device_id_type=MESH).wait_recv()
        for cp in sends:
            cp.wait_send()
        for cp in local:
            cp.wait()

    shapes = [jax.ShapeDtypeStruct(a.shape, a.dtype) for a in slabbed]
    shapes += [jax.ShapeDtypeStruct((N_DEV,) + a.shape, a.dtype) for a in replicated]
    return pl.pallas_call(
        body, name="exchange_slabs",
        in_specs=[HBM] * n, out_specs=[HBM] * n, out_shape=shapes,
        scratch_shapes=[pltpu.SemaphoreType.DMA((7 * n,)), pltpu.SemaphoreType.DMA((7 * n,)),
                        pltpu.SemaphoreType.DMA((n,))],
    )(*slabbed, *replicated)


def exchange_halves(halves):
    n = len(halves)
    streams = 8

    def body(*refs):
        srcs, outs, mine, theirs = refs[:n], refs[n:2 * n], refs[2 * n:3 * n], refs[3 * n:4 * n]
        send_sems, recv_sems, in_sems, out_sems = refs[4 * n:]
        x, y, c = _my_place()
        loads = [pltpu.make_async_copy(srcs[i], mine[i], in_sems.at[i]) for i in range(n)]
        for cp in loads:
            cp.start()
        for cp in loads:
            cp.wait()

        def chunk_copy(i, s):
            rows = halves[i].shape[0] // streams
            k = i * streams + s
            return pltpu.make_async_remote_copy(
                src_ref=mine[i].at[pl.ds(s * rows, rows)], dst_ref=theirs[i].at[pl.ds(s * rows, rows)],
                send_sem=send_sems.at[k], recv_sem=recv_sems.at[k], device_id=(x, y, 1 - c), device_id_type=MESH)

        sends = [chunk_copy(i, s) for i in range(n) for s in range(streams)]
        for cp in sends:
            cp.start()
        own = [pltpu.make_async_copy(mine[i], outs[i].at[c], out_sems.at[i]) for i in range(n)]
        for cp in own:
            cp.start()
        for cp in sends:
            cp.wait_recv()
        got = [pltpu.make_async_copy(theirs[i], outs[i].at[1 - c], out_sems.at[n + i]) for i in range(n)]
        for cp in got:
            cp.start()
        for cp in sends:
            cp.wait_send()
        for cp in own + got:
            cp.wait()

    vmem = [pltpu.VMEM(a.shape, a.dtype) for a in halves]
    return pl.pallas_call(
        body, name="exchange_halves",
        in_specs=[HBM] * n, out_specs=[HBM] * n,
        out_shape=[jax.ShapeDtypeStruct((2,) + a.shape, a.dtype) for a in halves],
        scratch_shapes=vmem + vmem + [pltpu.SemaphoreType.DMA((n * streams,)), pltpu.SemaphoreType.DMA((n * streams,)),
                                      pltpu.SemaphoreType.DMA((n,)), pltpu.SemaphoreType.DMA((2 * n,))],
        compiler_params=pltpu.CompilerParams(vmem_limit_bytes=VMEM_LIMIT),
    )(*halves)


def _pack_cols(pieces):
    offs, pos = [], 0
    for a in pieces:
        offs.append(pos)
        pos += a.shape[1]
    rows8 = [jnp.pad(a.astype(F32), ((0, 8 - a.shape[0]), (0, 0))) for a in pieces]
    return jnp.concatenate(rows8, axis=1), offs


def adamw_many(ws, gs, ms, vs):
    n = len(ws)

    def body(*refs):
        w_r, g_r, m_r, v_r = refs[:n], refs[n:2 * n], refs[2 * n:3 * n], refs[3 * n:4 * n]
        d_o, m_o, v_o = refs[4 * n:5 * n], refs[5 * n:6 * n], refs[6 * n:7 * n]
        for i in range(n):
            gg = g_r[i][...]
            nm = ADAM_B1 * m_r[i][...] + (1.0 - ADAM_B1) * gg
            nv = ADAM_B2 * v_r[i][...] + (1.0 - ADAM_B2) * (gg * gg)
            m_hat = nm / (1.0 - ADAM_B1 ** ADAM_STEP)
            v_hat = nv / (1.0 - ADAM_B2 ** ADAM_STEP)
            d_o[i][...] = -ADAM_LR * (m_hat / (jnp.sqrt(v_hat) + ADAM_EPS) + ADAM_WD * w_r[i][...])
            m_o[i][...] = nm
            v_o[i][...] = nv

    shapes = [jax.ShapeDtypeStruct(w.shape, F32) for w in ws]
    out = pl.pallas_call(body, name="adamw_small", out_shape=shapes * 3,
                         compiler_params=pltpu.CompilerParams(vmem_limit_bytes=VMEM_LIMIT))(*ws, *gs, *ms, *vs)
    return out[:n], out[n:2 * n], out[2 * n:]


def _lanes(vec, start):
    n = vec.shape[-1]
    return jnp.pad(vec.reshape(1, n).astype(F32), ((0, 0), (start, LANES - start - n)))


def kernel(x, norm_w, w_in, ssd_conv_w, ssd_conv_b, ssd_dt_bias, ssd_a_log, ssd_d, ssd_norm_w, gdn_conv_w, gdn_dt_bias, gdn_a_log, gdn_norm_w, w_out, final_norm_w, loss_target, m_norm_w, m_w_in, m_ssd_conv_w, m_ssd_conv_b, m_ssd_dt_bias, m_ssd_a_log, m_ssd_d, m_ssd_norm_w, m_gdn_conv_w, m_gdn_dt_bias, m_gdn_a_log, m_gdn_norm_w, m_w_out, m_final_norm_w, v_norm_w, v_w_in, v_ssd_conv_w, v_ssd_conv_b, v_ssd_dt_bias, v_ssd_a_log, v_ssd_d, v_ssd_norm_w, v_gdn_conv_w, v_gdn_dt_bias, v_gdn_a_log, v_gdn_norm_w, v_w_out, v_final_norm_w):
    xs = x[0]
    target = loss_target[0]
    chip = 2 * lax.axis_index("x") + lax.axis_index("y")
    w_in_shard, w_out_shard = w_in[0], w_out[0]
    in_cols = w_in_shard.shape[1]
    out_rows = w_out_shard.shape[0]

    g_in, g_out, g_cs, g_cg = gather_weights(
        [w_in_shard.astype(MXU_DTYPE), w_out_shard.astype(MXU_DTYPE), ssd_conv_w[0], gdn_conv_w[0]])
    w_in_full = jnp.transpose(g_in, (1, 0, 2)).reshape(D_MODEL, IN_DIM)
    w_out_full = g_out.reshape(N_CHIP * out_rows, D_MODEL)
    cw_ssd = _ssd_perm(jnp.transpose(g_cs, (1, 0, 2)).reshape(4, SSD_CONV))
    cw_gdn = _gdn_perm(jnp.transpose(g_cg, (1, 0, 2)).reshape(4, GDN_CONV))
    cb_ssd = _ssd_perm(ssd_conv_b)
    cb_gdn = jnp.zeros((1, GDN_CONV), F32)
    o_xbc, o_dt, o_gate, o_qkv, o_ab = 1024, 2560, 2576, 3600, 6672
    w_main = jnp.concatenate([w_in_full[:, :o_xbc], w_in_full[:, o_gate:o_qkv], _ssd_perm(w_in_full[:, o_xbc:o_dt]),
                              _gdn_perm(w_in_full[:, o_qkv:o_ab])], axis=1)
    w_small = jnp.concatenate([w_in_full[:, o_dt:o_gate], w_in_full[:, o_ab:],
                               jnp.zeros((D_MODEL, LANES - 32), MXU_DTYPE)], axis=1)
    alog = _lanes(ssd_a_log, 0) + _lanes(gdn_a_log, LANE_GA)
    dtb = _lanes(ssd_dt_bias, 0) + _lanes(gdn_dt_bias, LANE_GA)
    dvec = _lanes(ssd_d, 0)
    fw = final_norm_w.reshape(1, D_MODEL)

    proj_main, proj_small, u = in_proj(xs, norm_w, w_main, w_small)
    conv_ssd = conv_fwd(proj_main, COL_SSD, SSD_CONV, cw_ssd, cb_ssd, "conv_fwd_ssd")
    conv_gdn = conv_fwd(proj_main, COL_GDN, GDN_CONV, cw_gdn, cb_gdn, "conv_fwd_gdn")
    y_ssd, hist_ssd = ssd_fwd(conv_ssd, proj_main, proj_small, ssd_norm_w, alog, dtb, dvec)
    y_gdn, hist_gdn, tinv_gdn = gdn_fwd(conv_gdn, proj_main, proj_small, gdn_norm_w, alog, dtb)

    loss_blk, dhid, dy_ssd, dy_gdn, d_w_out, d_fw = out_proj_loss(xs, y_ssd, y_gdn, w_out_full, fw, target)
    dconv_ssd, dproj_main, dsmall_ssd, d_ssd_nw, d_alog_s, d_dtb_s, d_dvec = ssd_bwd(
        conv_ssd, proj_main, proj_small, ssd_norm_w, alog, dtb, dvec, hist_ssd, dy_ssd)
    dproj_main, dconv_gdn, dsmall_gdn, d_gdn_nw, d_alog_g, d_dtb_g = gdn_bwd(
        dproj_main, conv_gdn, proj_main, proj_small, gdn_norm_w, alog, dtb, hist_gdn, tinv_gdn, dy_gdn)
    dpre_ssd, dwb_ssd = conv_bwd_pre(proj_main, COL_SSD, SSD_CONV, cw_ssd, cb_ssd, dconv_ssd, "conv_bwd_pre_ssd")
    dpre_gdn, dwb_gdn = conv_bwd_pre(proj_main, COL_GDN, GDN_CONV, cw_gdn, cb_gdn, dconv_gdn, "conv_bwd_pre_gdn")
    dproj_main = conv_bwd_x(dproj_main, COL_SSD, SSD_CONV, cw_ssd, dpre_ssd, "conv_bwd_x_ssd")
    dproj_main = conv_bwd_x(dproj_main, COL_GDN, GDN_CONV, cw_gdn, dpre_gdn, "conv_bwd_x_gdn")
    grad_x, d_norm_w = in_proj_bwd_x(xs, norm_w, w_main, w_small, dproj_main, dsmall_ssd, dsmall_gdn, dhid)
    d_w_main, d_w_small = in_proj_bwd_w(u, dproj_main, dsmall_ssd, dsmall_gdn)

    d_w_in = jnp.concatenate([d_w_main[:, :COL_GATE], _ssd_unperm(d_w_main[:, COL_SSD:COL_GDN]), d_w_small[:, 0:16],
                              d_w_main[:, COL_GATE:COL_SSD], _gdn_unperm(d_w_main[:, COL_GDN:]), d_w_small[:, 16:32]],
                             axis=1)
    d_w_in = jnp.transpose(d_w_in.reshape(D_MODEL, N_CHIP, in_cols), (1, 0, 2))
    d_alog, d_dtb = d_alog_s + d_alog_g, d_dtb_s + d_dtb_g
    packed, (o_nw, o_cs, o_cg, o_snw, o_fw, o_al, o_db, o_dv, o_gnw, o_loss) = _pack_cols([
        d_norm_w, _ssd_unperm(dwb_ssd), _gdn_unperm(dwb_gdn),
        d_ssd_nw.reshape(1, SSD_WIDTH), d_fw, d_alog, d_dtb, d_dvec, d_gdn_nw, loss_blk])

    r_in, r_out, r_small = exchange_slabs(
        [d_w_in.reshape(N_DEV, D_MODEL // 2, in_cols).astype(COMM_DTYPE),
         d_w_out.reshape(N_DEV, out_rows // 2, D_MODEL).astype(COMM_DTYPE)],
        [packed])
    half_in = sum_slabs(r_in, "sum_w_in")
    half_out = sum_slabs(r_out, "sum_w_out")
    tot = sum_slabs(r_small, "sum_small")
    full_in, full_out = exchange_halves([half_in, half_out])
    grad_w_in = full_in.reshape(D_MODEL, in_cols)
    grad_w_out = full_out.reshape(out_rows, D_MODEL)
    loss = tot[0, o_loss]
    sc, gc = ssd_conv_w.shape[2], gdn_conv_w.shape[2]
    row = lambda off, n, r=0: tot[r:r + 1, off:off + n]
    gs = [row(o_nw, D_MODEL),
          lax.dynamic_slice(tot, (0, o_cs + chip * sc), (4, sc)),
          row(o_cs, SSD_CONV, 4),
          row(o_db, SSD_HEADS), row(o_al, SSD_HEADS), row(o_dv, SSD_HEADS),
          row(o_snw, SSD_WIDTH),
          lax.dynamic_slice(tot, (0, o_cg + chip * gc), (4, gc)),
          row(o_db + LANE_GA, GDN_HEADS), row(o_al + LANE_GA, GDN_HEADS),
          row(o_gnw, GDN_DV), row(o_fw, D_MODEL)]

    names = ["norm_w", "ssd_conv_w", "ssd_conv_b", "ssd_dt_bias", "ssd_a_log", "ssd_d", "ssd_norm_w", "gdn_conv_w",
             "gdn_dt_bias", "gdn_a_log", "gdn_norm_w", "final_norm_w"]
    ws = [norm_w, ssd_conv_w, ssd_conv_b, ssd_dt_bias, ssd_a_log, ssd_d, ssd_norm_w, gdn_conv_w, gdn_dt_bias,
          gdn_a_log, gdn_norm_w, final_norm_w]
    ms = [m_norm_w, m_ssd_conv_w, m_ssd_conv_b, m_ssd_dt_bias, m_ssd_a_log, m_ssd_d, m_ssd_norm_w, m_gdn_conv_w,
          m_gdn_dt_bias, m_gdn_a_log, m_gdn_norm_w, m_final_norm_w]
    vs = [v_norm_w, v_ssd_conv_w, v_ssd_conv_b, v_ssd_dt_bias, v_ssd_a_log, v_ssd_d, v_ssd_norm_w, v_gdn_conv_w,
          v_gdn_dt_bias, v_gdn_a_log, v_gdn_norm_w, v_final_norm_w]
    shapes = [w.shape for w in ws]
    flat = lambda arrs: [a.reshape(g.shape) for a, g in zip(arrs, gs)]
    d_s, m_s, v_s = adamw_many(flat(ws), gs, flat(ms), flat(vs))
    back = lambda arrs: dict(zip(names, [a.reshape(s) for a, s in zip(arrs, shapes)]))
    delta, new_m, new_v, grads = back(d_s), back(m_s), back(v_s), back(gs)
    d_in, m_in, v_in = adamw(w_in_shard, grad_w_in, m_w_in[0], v_w_in[0], "adamw_w_in")
    d_out, m_out, v_out = adamw(w_out_shard, grad_w_out, m_w_out[0], v_w_out[0], "adamw_w_out")
    for tbl, a_in, a_out in ((grads, grad_w_in, grad_w_out), (delta, d_in, d_out), (new_m, m_in, m_out),
                             (new_v, v_in, v_out)):
        tbl["w_in"] = a_in[None]
        tbl["w_out"] = a_out[None]

    order = ["norm_w", "w_in", "ssd_conv_w", "ssd_conv_b", "ssd_dt_bias", "ssd_a_log", "ssd_d", "ssd_norm_w",
             "gdn_conv_w", "gdn_dt_bias", "gdn_a_log", "gdn_norm_w", "w_out", "final_norm_w"]
    return (loss.reshape(()), grad_x[None], *[grads[k] for k in order], *[delta[k] for k in order],
            *[new_m[k] for k in order], *[new_v[k] for k in order])
```

```python
import functools

import jax
import jax.numpy as jnp
from jax import lax
from jax.experimental import pallas as pl
from jax.experimental.pallas import tpu as pltpu

F32 = jnp.float32
MXU_DTYPE = jnp.bfloat16
COMM_DTYPE = jnp.bfloat16
MESH = pl.DeviceIdType.MESH

D_MODEL = 1024
CHUNK = 64
EPS = 1e-6
SSD_HEADS, SSD_GROUPS, SSD_STATE = 16, 2, 128
SSD_WIDTH, SSD_CONV = 1024, 1536
SSD_GW = SSD_WIDTH // SSD_GROUPS
SSD_GC = SSD_GW + 2 * SSD_STATE
GDN_HEADS, GDN_DK, GDN_DV = 8, 128, 128
GDN_W, GDN_CONV = 1024, 3072
GDN_HC = 2 * GDN_DK + GDN_DV
IN_DIM = 6688
MAIN = 6656
LANES = 128
COL_Z, COL_GATE, COL_SSD, COL_GDN = 0, 1024, 2048, 3584
GDN_HB = 8
LANE_GA, LANE_GB = 16, 24
N_DEV, N_CHIP = 8, 4
VMEM_LIMIT = 52 * 1024 * 1024

ADAM_LR, ADAM_B1, ADAM_B2, ADAM_EPS, ADAM_WD, ADAM_STEP = 0.001, 0.9, 0.999, 1e-08, 0.01, 10


def _ssd_perm(a):
    lead, nb = a.shape[:-1], SSD_GROUPS * SSD_STATE
    x = a[..., :SSD_WIDTH].reshape(*lead, SSD_GROUPS, SSD_GW)
    b = a[..., SSD_WIDTH:SSD_WIDTH + nb].reshape(*lead, SSD_GROUPS, SSD_STATE)
    c = a[..., SSD_WIDTH + nb:].reshape(*lead, SSD_GROUPS, SSD_STATE)
    return jnp.concatenate([x, b, c], axis=-1).reshape(*lead, SSD_CONV)


def _ssd_unperm(a):
    lead = a.shape[:-1]
    g = a.reshape(*lead, SSD_GROUPS, SSD_GC)
    parts = [g[..., :SSD_GW], g[..., SSD_GW:SSD_GW + SSD_STATE], g[..., SSD_GW + SSD_STATE:]]
    return jnp.concatenate([p.reshape(*lead, -1) for p in parts], axis=-1)


def _gdn_perm(a):
    lead = a.shape[:-1]
    return jnp.swapaxes(a.reshape(*lead, 3, GDN_HEADS, GDN_DK), -3, -2).reshape(*lead, GDN_CONV)


def _gdn_unperm(a):
    lead = a.shape[:-1]
    return jnp.swapaxes(a.reshape(*lead, GDN_HEADS, 3, GDN_DK), -3, -2).reshape(*lead, GDN_CONV)


def _split(a, n):
    parts, rest = [], a.astype(F32)
    for i in range(n):
        p = rest.astype(MXU_DTYPE)
        parts.append(p)
        if i < n - 1:
            rest = rest - p.astype(F32)
    return parts


def _raw_dot(a, b, ca, cb, mode="bf16"):
    d = lambda u, v: lax.dot_general(u, v, (((ca,), (cb,)), ((), ())), preferred_element_type=F32)
    if mode == "bf16":
        return d(a.astype(MXU_DTYPE), b.astype(MXU_DTYPE))
    if mode == "x3":
        (ah, al), (bh, bl) = _split(a, 2), _split(b, 2)
        return d(ah, bh) + (d(ah, bl) + d(al, bh))
    if mode == "sel_a":
        a0 = a.astype(MXU_DTYPE)
        b1, b2, b3 = _split(b, 3)
        return d(a0, b1) + (d(a0, b2) + d(a0, b3))
    assert mode == "sel_b", mode
    b0 = b.astype(MXU_DTYPE)
    a1, a2, a3 = _split(a, 3)
    return d(a1, b0) + (d(a2, b0) + d(a3, b0))


@functools.partial(jax.custom_vjp, nondiff_argnums=(2,))
def mm_nn(a, b, mode="bf16"):
    return _raw_dot(a, b, 1, 0, mode)


@functools.partial(jax.custom_vjp, nondiff_argnums=(2,))
def mm_nt(a, b, mode="bf16"):
    return _raw_dot(a, b, 1, 1, mode)


@functools.partial(jax.custom_vjp, nondiff_argnums=(2,))
def mm_tn(a, b, mode="bf16"):
    return _raw_dot(a, b, 0, 0, mode)


_SAME = {"bf16": ("bf16", "bf16"), "x3": ("x3", "x3")}
_NN_BWD = dict(_SAME, sel_a=("bf16", "sel_a"), sel_b=("sel_b", "bf16"))
_NT_BWD = dict(_SAME, sel_a=("bf16", "sel_b"), sel_b=("sel_b", "bf16"))
_TN_BWD = dict(_SAME, sel_a=("bf16", "sel_a"), sel_b=("sel_a", "bf16"))
mm_nn.defvjp(lambda a, b, m: (_raw_dot(a, b, 1, 0, m), (a, b)),
             lambda m, r, g: (mm_nt(g, r[1], _NN_BWD[m][0]), mm_tn(r[0], g, _NN_BWD[m][1])))
mm_nt.defvjp(lambda a, b, m: (_raw_dot(a, b, 1, 1, m), (a, b)),
             lambda m, r, g: (mm_nn(g, r[1], _NT_BWD[m][0]), mm_tn(g, r[0], _NT_BWD[m][1])))
mm_tn.defvjp(lambda a, b, m: (_raw_dot(a, b, 0, 0, m), (a, b)),
             lambda m, r, g: (mm_nt(r[1], g, _TN_BWD[m][0]), mm_nn(r[0], g, _TN_BWD[m][1])))


@jax.custom_jvp
def sigmoid(x):
    return 1.0 / (1.0 + jnp.exp(-x))


@sigmoid.defjvp
def _sigmoid_jvp(p, t):
    s = sigmoid(p[0])
    return s, t[0] * s * (1.0 - s)


@jax.custom_jvp
def softplus(x):
    return jnp.maximum(x, 0.0) + jnp.log(1.0 + jnp.exp(-jnp.abs(x)))


@softplus.defjvp
def _softplus_jvp(p, t):
    return softplus(p[0]), t[0] * sigmoid(p[0])


def silu(x):
    return x * sigmoid(x)


def rmsnorm(x, w):
    return x * lax.rsqrt(jnp.mean(x * x, axis=-1, keepdims=True) + EPS) * w


def _iota(shape, dim):
    return lax.broadcasted_iota(jnp.int32, shape, dim)


def _tri_inv_impl(mats):
    n = mats[0].shape[0]
    r, c = _iota((n, n), 0), _iota((n, n), 1)
    eye = jnp.where(r == c, 1.0, 0.0).astype(F32)
    blockdiag = (r >> 4) == (c >> 4)
    dot = lambda u, v: _raw_dot(u, v, 1, 0, "x3")
    each = lambda f, *ls: [f(*xs) for xs in zip(*ls)]
    dg = each(lambda a: jnp.where(blockdiag, a, 0.0), mats)
    off = each(lambda a, d: a - d, mats, dg)
    m = each(lambda d: -d, dg)
    p = each(lambda x: eye + x, m)
    pw = m
    for _ in range(3):
        pw = each(lambda x: dot(x, x), pw)
        p = each(lambda x, y: x + dot(x, y), p, pw)
    e = each(dot, p, off)
    e2 = each(lambda x: dot(x, x), e)
    q = each(lambda x: eye - x, e)
    q = each(lambda x, y: x + dot(x, y), q, e2)
    return each(dot, q, p)


def _tri_inv_bwd(ts, gs):
    x = [mm_nt(g, t, "x3") for g, t in zip(gs, ts)]
    return [-mm_tn(t, y, "x3") for t, y in zip(ts, x)]


@jax.custom_vjp
def tri_inv(mats):
    return _tri_inv_impl(mats)


def _tri_inv_fwd(mats):
    ts = _tri_inv_impl(mats)
    return ts, ts


tri_inv.defvjp(_tri_inv_fwd, lambda ts, gs: (_tri_inv_bwd(ts, gs),))


@jax.custom_vjp
def tri_inv_saved(mats, ts):
    del mats
    return ts


tri_inv_saved.defvjp(lambda mats, ts: (ts, ts),
                     lambda ts, gs: (_tri_inv_bwd(ts, gs), [jnp.zeros_like(t) for t in ts]))


def _chunk_masks():
    r, c = _iota((CHUNK, CHUNK), 0), _iota((CHUNK, CHUNK), 1)
    return r >= c, r > c, r == c


def _log_decay_cumsum(small, alog, dtb, incl):
    sp = softplus(small + dtb)
    la = -jnp.exp(alog) * sp
    tri = jnp.where(incl, 1.0, 0.0).astype(F32)
    return sp, mm_nn(tri, la, "sel_a")


def _col_of(x, lane_mask):
    return jnp.sum(jnp.where(lane_mask, x, 0.0), axis=1, keepdims=True)


def _decay_matrix(col, incl, eye):
    row = jnp.sum(jnp.where(eye, col, 0.0), axis=0, keepdims=True)
    return jnp.where(incl, jnp.exp(jnp.where(incl, col - row, 0.0)), 0.0)


def gdn_chunk(h0, qs, ks, vs, small, gates, normw, alog, dtb, states, saved_t=None):
    incl, strict, eye = _chunk_masks()
    lane = _iota((1, LANES), 1)
    last = _iota((CHUNK, 1), 0) == CHUNK - 1
    _, lac = _log_decay_cumsum(small, alog, dtb, incl)
    heads = range(len(qs))
    each = lambda f, *ls: [f(*xs) for xs in zip(*ls)]
    gc = [_col_of(lac, lane == LANE_GA + h0 + j) for j in heads]
    beta = [sigmoid(_col_of(small, lane == LANE_GB + h0 + j)) for j in heads]
    decay = each(lambda x: _decay_matrix(x, incl, eye), gc)
    gl = each(lambda x: jnp.sum(jnp.where(last, x, 0.0), axis=0, keepdims=True), gc)
    q = each(lambda x: x * lax.rsqrt(jnp.sum(x * x, axis=-1, keepdims=True) + EPS) * (GDN_DK ** -0.5), qs)
    k = each(lambda x: x * lax.rsqrt(jnp.sum(x * x, axis=-1, keepdims=True) + EPS), ks)
    kb = each(lambda x, b: x * b, k, beta)
    a = each(lambda x, y, d: jnp.where(strict, mm_nt(x, y) * d, 0.0), kb, k, decay)
    t = tri_inv(a) if saved_t is None else tri_inv_saved(a, saved_t)
    eg = each(jnp.exp, gc)
    u = each(lambda x, v, b: mm_nn(x, v * b, "x3"), t, vs, beta)
    w = each(lambda x, y, e: mm_nn(x, y * e, "x3"), t, kb, eg)
    attn = each(lambda x, y, d: mm_nt(x, y) * d, q, k, decay)
    v_new = each(lambda x, y, s: x - mm_nn(y, s), u, w, states)
    o = each(lambda x, e, s, at, vn: mm_nn(x * e, s) + mm_nn(at, vn), q, eg, states, attn, v_new)
    new_states = each(lambda s, x, y, l, c: s * jnp.exp(l) + mm_tn(y * jnp.exp(l - c), x), states, v_new, k, gl, gc)
    ys = each(lambda x, gt: rmsnorm(x, normw) * silu(gt), o, gates)
    return ys, new_states, t


@jax.custom_vjp
def split_lanes(x):
    return [x[:, i * LANES:(i + 1) * LANES] for i in range(x.shape[1] // LANES)]


@jax.custom_vjp
def join_lanes(xs):
    return jnp.concatenate(xs, axis=1)


split_lanes.defvjp(lambda x: (split_lanes(x), None), lambda _, gs: (join_lanes(gs),))
join_lanes.defvjp(lambda xs: (join_lanes(xs), None), lambda _, g: (split_lanes(g),))


def ssd_chunk(g, xs, bm, cm, z, small, normw, alog, dtb, dvec, state):
    incl, _, eye = _chunk_masks()
    lane = _iota((1, LANES), 1)
    last = _iota((CHUNK, 1), 0) == CHUNK - 1
    hpg = SSD_HEADS // SSD_GROUPS
    sp, lac = _log_decay_cumsum(small, alog, dtb, incl)
    sel = jnp.where(_iota((LANES, SSD_GW), 0) == g * hpg + (_iota((LANES, SSD_GW), 1) >> 6), 1.0, 0.0).astype(F32)
    lac_last = jnp.sum(jnp.where(last, lac, 0.0), axis=0, keepdims=True)
    dt_e = mm_nn(sp, sel, "sel_b")
    elac_e = mm_nn(jnp.exp(lac), sel, "sel_b")
    toend_e = mm_nn(jnp.exp(lac_last - lac), sel, "sel_b")
    row8 = _iota((8, LANES), 0)
    two = jnp.where(row8 == 0, dvec, 0.0) + jnp.where(row8 == 1, jnp.exp(lac_last), 0.0)
    two_e = mm_nn(two, sel, "sel_b")
    row8e = _iota((8, SSD_GW), 0)
    d_e = jnp.sum(jnp.where(row8e == 0, two_e, 0.0), axis=0, keepdims=True)
    chunk_e = jnp.sum(jnp.where(row8e == 1, two_e, 0.0), axis=0, keepdims=True)
    xdt = xs * dt_e
    cb = mm_nt(cm, bm)
    y = mm_nn(cm, state) * elac_e + xs * d_e
    x_pairs = split_lanes(xdt)
    half = _iota((1, LANES), 1) >> 6
    lms = [_decay_matrix(_col_of(lac, lane == g * hpg + j), incl, eye) for j in range(hpg)]
    terms = [mm_nn(cb * lms[j], jnp.where(half == j % 2, x_pairs[j // 2], 0.0)) for j in range(hpg)]
    y = y + join_lanes([terms[2 * p] + terms[2 * p + 1] for p in range(hpg // 2)])
    new_state = state * chunk_e + mm_tn(bm, xdt * toend_e)
    yg = y * silu(z)
    return rmsnorm(yg, normw), new_state


def _params(sem=None):
    return pltpu.CompilerParams(dimension_semantics=sem, vmem_limit_bytes=VMEM_LIMIT)


def _full(shape):
    n = len(shape)
    return pl.BlockSpec(shape, lambda *_: (0,) * n)


ANY = pl.BlockSpec(memory_space=pl.ANY)
HBM = pl.BlockSpec(memory_space=pltpu.HBM)


def in_proj(x, normw, w_main, w_small):
    t = x.shape[0]
    tm, tn = min(1024, t), 512

    def body(x_ref, nw_ref, wm_ref, ws_ref, pm_ref, ps_ref, u_ref):
        @pl.when(pl.program_id(1) == 0)
        def _():
            u = rmsnorm(x_ref[...], nw_ref[...]).astype(MXU_DTYPE)
            u_ref[...] = u
            ps_ref[...] = _raw_dot(u, ws_ref[...], 1, 0)
        pm_ref[...] = _raw_dot(u_ref[...], wm_ref[...], 1, 0)

    return pl.pallas_call(
        body, name="in_proj", grid=(t // tm, MAIN // tn),
        in_specs=[pl.BlockSpec((tm, D_MODEL), lambda i, j: (i, 0)), _full((1, D_MODEL)),
                  pl.BlockSpec((D_MODEL, tn), lambda i, j: (0, j)), _full((D_MODEL, LANES))],
        out_specs=[pl.BlockSpec((tm, tn), lambda i, j: (i, j)), pl.BlockSpec((tm, LANES), lambda i, j: (i, 0)),
                   pl.BlockSpec((tm, D_MODEL), lambda i, j: (i, 0))],
        out_shape=[jax.ShapeDtypeStruct((t, MAIN), F32), jax.ShapeDtypeStruct((t, LANES), F32),
                   jax.ShapeDtypeStruct((t, D_MODEL), MXU_DTYPE)],
        compiler_params=_params(("arbitrary", "arbitrary")),
    )(x, normw, w_main, w_small)


CONV_TC = 512
HALO = 8


def _shift_down(cur, prev, s):
    rolled = pltpu.roll(cur, s, 0)
    top = jnp.where(_iota((HALO, cur.shape[1]), 0) < s, pltpu.roll(prev, s, 0), rolled[:HALO])
    if cur.shape[0] == HALO:
        return top
    return jnp.concatenate([top, rolled[HALO:]], axis=0)


def _shift_up(cur, nxt, s):
    n = cur.shape[0]
    rolled = pltpu.roll(cur, n - s, 0)
    bot = jnp.where(_iota((HALO, cur.shape[1]), 0) >= HALO - s, pltpu.roll(nxt, HALO - s, 0), rolled[n - HALO:])
    return jnp.concatenate([rolled[:n - HALO], bot], axis=0)


def _conv_pre(cur, prev, w_ref, b):
    acc = cur * w_ref[3:4, :] + b
    shifted = [cur]
    for s in (1, 2, 3):
        sh = _shift_down(cur, prev, s)
        shifted.append(sh)
        acc = acc + sh * w_ref[3 - s:4 - s, :]
    return acc, shifted


def conv_fwd(proj_main, col0, width, w, b, name):
    t = proj_main.shape[0]
    tt, c0 = min(512, t), col0 // CONV_TC

    def body(cur_ref, prev_ref, w_ref, b_ref, out_ref):
        prev = jnp.where(pl.program_id(0) > 0, prev_ref[...], 0.0)
        pre, _ = _conv_pre(cur_ref[...], prev, w_ref, b_ref[...])
        out_ref[...] = silu(pre)

    return pl.pallas_call(
        body, name=name, grid=(t // tt, width // CONV_TC),
        in_specs=[pl.BlockSpec((tt, CONV_TC), lambda i, j: (i, c0 + j)),
                  pl.BlockSpec((HALO, CONV_TC), lambda i, j: (jnp.maximum(i * (tt // HALO) - 1, 0), c0 + j)),
                  pl.BlockSpec((4, CONV_TC), lambda i, j: (0, j)), pl.BlockSpec((1, CONV_TC), lambda i, j: (0, j))],
        out_specs=pl.BlockSpec((tt, CONV_TC), lambda i, j: (i, j)),
        out_shape=jax.ShapeDtypeStruct((t, width), F32),
        compiler_params=_params(("arbitrary", "arbitrary")),
    )(proj_main, proj_main, w, b)


def _dsilu(pre):
    sg = sigmoid(pre)
    return sg * (1.0 + pre * (1.0 - sg))


def conv_bwd(dproj_main, proj_main, col0, width, w, b, dout, name):
    t = proj_main.shape[0]
    tt, c0 = min(512, t), col0 // CONV_TC
    nt = t // tt
    after = lambda i: jnp.minimum((i + 1) * (tt // HALO), t // HALO - 1)

    def body(alias_ref, cur_ref, prev_ref, nxt_ref, w_ref, b_ref, do_ref, do_nxt_ref, dx_ref, dwb_ref):
        del alias_ref
        i = pl.program_id(1)
        cur, bias = cur_ref[...], b_ref[...]
        prev = jnp.where(i > 0, prev_ref[...], 0.0)
        pre, shifted = _conv_pre(cur, prev, w_ref, bias)
        dpre = do_ref[...] * _dsilu(pre)
        pre_nxt, _ = _conv_pre(nxt_ref[...], cur[tt - HALO:], w_ref, bias)
        dpre_nxt = jnp.where(i < nt - 1, do_nxt_ref[...] * _dsilu(pre_nxt), 0.0)
        dx = dpre * w_ref[3:4, :]
        for s in (1, 2, 3):
            dx = dx + _shift_up(dpre, dpre_nxt, s) * w_ref[3 - s:4 - s, :]
        dx_ref[...] = dx.astype(dx_ref.dtype)
        row = _iota((HALO, CONV_TC), 0)
        upd = jnp.where(row == 4, jnp.sum(dpre, axis=0, keepdims=True), 0.0)
        for s in range(4):
            upd = upd + jnp.where(row == 3 - s, jnp.sum(dpre * shifted[s], axis=0, keepdims=True), 0.0)
        _accumulate(dwb_ref, i == 0, upd)

    return pl.pallas_call(
        body, name=name, grid=(width // CONV_TC, nt),
        in_specs=[ANY, pl.BlockSpec((tt, CONV_TC), lambda j, i: (i, c0 + j)),
                  pl.BlockSpec((HALO, CONV_TC), lambda j, i: (jnp.maximum(i * (tt // HALO) - 1, 0), c0 + j)),
                  pl.BlockSpec((HALO, CONV_TC), lambda j, i: (after(i), c0 + j)),
                  pl.BlockSpec((4, CONV_TC), lambda j, i: (0, j)), pl.BlockSpec((1, CONV_TC), lambda j, i: (0, j)),
                  pl.BlockSpec((tt, CONV_TC), lambda j, i: (i, j)),
                  pl.BlockSpec((HALO, CONV_TC), lambda j, i: (after(i), j))],
        out_specs=[pl.BlockSpec((tt, CONV_TC), lambda j, i: (i, c0 + j)),
                   pl.BlockSpec((HALO, CONV_TC), lambda j, i: (0, j))],
        out_shape=[jax.ShapeDtypeStruct(dproj_main.shape, dproj_main.dtype), jax.ShapeDtypeStruct((HALO, width), F32)],
        input_output_aliases={0: 0},
        compiler_params=_params(("arbitrary", "arbitrary")),
    )(dproj_main, proj_main, proj_main, proj_main, w, b, dout, dout)


def _ssd_parts(xbc_ref):
    return xbc_ref[:, :SSD_GW], xbc_ref[:, SSD_GW:SSD_GW + SSD_STATE], xbc_ref[:, SSD_GW + SSD_STATE:]


def _gdn_parts(qkv_ref):
    part = lambda o: [qkv_ref[:, j * GDN_HC + o:j * GDN_HC + o + GDN_DK] for j in range(GDN_HB)]
    return part(0), part(GDN_DK), part(2 * GDN_DK)


def _head_cols(ref):
    return [ref[:, j * GDN_DV:(j + 1) * GDN_DV] for j in range(GDN_HB)]


def _first_head():
    return 0 if GDN_HB == GDN_HEADS else pl.program_id(1) * GDN_HB


def ssd_fwd(conv_ssd, proj_main, proj_small, normw, alog, dtb, dvec):
    t = conv_ssd.shape[0]
    nc = t // CHUNK

    def body(xbc_ref, z_ref, sm_ref, nw_ref, al_ref, db_ref, dv_ref, y_ref, hist_ref, state_ref):
        g = pl.program_id(1)

        @pl.when(pl.program_id(0) == 0)
        def _():
            state_ref[g] = jnp.zeros((SSD_STATE, SSD_GW), F32)

        state = state_ref[g]
        hist_ref[0, 0] = state
        y, new_state = ssd_chunk(g, *_ssd_parts(xbc_ref), z_ref[...], sm_ref[...], nw_ref[...], al_ref[...],
                                 db_ref[...], dv_ref[...], state)
        y_ref[...] = y.astype(MXU_DTYPE)
        state_ref[g] = new_state

    return pl.pallas_call(
        body, name="ssd_fwd", grid=(nc, SSD_GROUPS),
        in_specs=[pl.BlockSpec((CHUNK, SSD_GC), lambda c, g: (c, g)),
                  pl.BlockSpec((CHUNK, SSD_GW), lambda c, g: (c, COL_Z // SSD_GW + g)),
                  pl.BlockSpec((CHUNK, LANES), lambda c, g: (c, 0)),
                  pl.BlockSpec((1, SSD_GW), lambda c, g: (0, g)), _full((1, LANES)), _full((1, LANES)), _full((1, LANES))],
        out_specs=[pl.BlockSpec((CHUNK, SSD_GW), lambda c, g: (c, g)),
                   pl.BlockSpec((1, 1, SSD_STATE, SSD_GW), lambda c, g: (c, g, 0, 0))],
        out_shape=[jax.ShapeDtypeStruct((t, SSD_WIDTH), MXU_DTYPE),
                   jax.ShapeDtypeStruct((nc, SSD_GROUPS, SSD_STATE, SSD_GW), F32)],
        scratch_shapes=[pltpu.VMEM((SSD_GROUPS, SSD_STATE, SSD_GW), F32)],
        compiler_params=_params(("arbitrary", "arbitrary")),
    )(conv_ssd, proj_main, proj_small, normw, alog, dtb, dvec)


def _accumulate(ref, first, value):
    @pl.when(first)
    def _():
        ref[...] = value

    @pl.when(jnp.logical_not(first))
    def _():
        ref[...] += value


def ssd_bwd(conv_ssd, proj_main, proj_small, normw, alog, dtb, dvec, hist, dy):
    t = conv_ssd.shape[0]
    nc = t // CHUNK
    rev = lambda c: nc - 1 - c

    def body(xbc_ref, z_ref, sm_ref, nw_ref, al_ref, db_ref, dv_ref, hist_ref, dy_ref,
             dxbc_ref, dz_ref, dsm_ref, dnw_ref, dal_ref, ddb_ref, ddv_ref, dstate_ref):
        c, g = pl.program_id(0), pl.program_id(1)

        @pl.when(c == 0)
        def _():
            dstate_ref[g] = jnp.zeros((SSD_STATE, SSD_GW), F32)

        fn = functools.partial(ssd_chunk, g)
        _, vjp = jax.vjp(fn, *_ssd_parts(xbc_ref), z_ref[...], sm_ref[...], nw_ref[...], al_ref[...], db_ref[...],
                         dv_ref[...], hist_ref[0, 0])
        dxs, dbm, dcm, dz, dsm, dnw, dal, ddb, ddv, dstate = vjp((dy_ref[...], dstate_ref[g]))
        dxbc_ref[:, :SSD_GW] = dxs
        dxbc_ref[:, SSD_GW:SSD_GW + SSD_STATE] = dbm
        dxbc_ref[:, SSD_GW + SSD_STATE:] = dcm
        dz_ref[...] = dz.astype(dz_ref.dtype)
        dstate_ref[g] = dstate
        _accumulate(dsm_ref, g == 0, dsm)
        first = jnp.logical_and(c == 0, g == 0)
        _accumulate(dal_ref, first, dal)
        _accumulate(ddb_ref, first, ddb)
        _accumulate(ddv_ref, first, ddv)

        @pl.when(c == 0)
        def _():
            dnw_ref[g] = dnw

        @pl.when(c > 0)
        def _():
            dnw_ref[g] += dnw

    return pl.pallas_call(
        body, name="ssd_bwd", grid=(nc, SSD_GROUPS),
        in_specs=[pl.BlockSpec((CHUNK, SSD_GC), lambda c, g: (rev(c), g)),
                  pl.BlockSpec((CHUNK, SSD_GW), lambda c, g: (rev(c), COL_Z // SSD_GW + g)),
                  pl.BlockSpec((CHUNK, LANES), lambda c, g: (rev(c), 0)),
                  pl.BlockSpec((1, SSD_GW), lambda c, g: (0, g)), _full((1, LANES)), _full((1, LANES)), _full((1, LANES)),
                  pl.BlockSpec((1, 1, SSD_STATE, SSD_GW), lambda c, g: (rev(c), g, 0, 0)),
                  pl.BlockSpec((CHUNK, SSD_GW), lambda c, g: (rev(c), g))],
        out_specs=[pl.BlockSpec((CHUNK, SSD_GC), lambda c, g: (rev(c), g)),
                   pl.BlockSpec((CHUNK, SSD_GW), lambda c, g: (rev(c), COL_Z // SSD_GW + g)),
                   pl.BlockSpec((CHUNK, LANES), lambda c, g: (rev(c), 0)),
                   _full((SSD_GROUPS, 1, SSD_GW)), _full((1, LANES)), _full((1, LANES)), _full((1, LANES))],
        out_shape=[jax.ShapeDtypeStruct((t, SSD_CONV), F32), jax.ShapeDtypeStruct((t, MAIN), MXU_DTYPE),
                   jax.ShapeDtypeStruct((t, LANES), F32), jax.ShapeDtypeStruct((SSD_GROUPS, 1, SSD_GW), F32),
                   jax.ShapeDtypeStruct((1, LANES), F32), jax.ShapeDtypeStruct((1, LANES), F32),
                   jax.ShapeDtypeStruct((1, LANES), F32)],
        scratch_shapes=[pltpu.VMEM((SSD_GROUPS, SSD_STATE, SSD_GW), F32)],
        compiler_params=_params(("arbitrary", "arbitrary")),
    )(conv_ssd, proj_main, proj_small, normw, alog, dtb, dvec, hist, dy)


def gdn_fwd(conv_gdn, proj_main, proj_small, normw, alog, dtb):
    t = conv_gdn.shape[0]
    nc = t // CHUNK

    hb = GDN_HB
    gate_blk = COL_GATE // (GDN_DV * hb)

    def body(qkv_ref, gate_ref, sm_ref, nw_ref, al_ref, db_ref, y_ref, hist_ref, t_ref, state_ref):
        h0 = _first_head()

        @pl.when(pl.program_id(0) == 0)
        def _():
            for j in range(hb):
                state_ref[h0 + j] = jnp.zeros((GDN_DK, GDN_DV), F32)

        states = [state_ref[h0 + j] for j in range(hb)]
        for j in range(hb):
            hist_ref[0, j] = states[j]
        qs, ks, vs = _gdn_parts(qkv_ref)
        ys, new_states, ts = gdn_chunk(h0, qs, ks, vs, sm_ref[...], _head_cols(gate_ref), nw_ref[...], al_ref[...],
                                       db_ref[...], states)
        for j in range(hb):
            y_ref[:, j * GDN_DV:(j + 1) * GDN_DV] = ys[j].astype(MXU_DTYPE)
            state_ref[h0 + j] = new_states[j]
            t_ref[0, j] = ts[j]

    return pl.pallas_call(
        body, name="gdn_fwd", grid=(nc, GDN_HEADS // hb),
        in_specs=[pl.BlockSpec((CHUNK, GDN_HC * hb), lambda c, h: (c, h)),
                  pl.BlockSpec((CHUNK, GDN_DV * hb), lambda c, h: (c, gate_blk + h)),
                  pl.BlockSpec((CHUNK, LANES), lambda c, h: (c, 0)),
                  _full((1, GDN_DV)), _full((1, LANES)), _full((1, LANES))],
        out_specs=[pl.BlockSpec((CHUNK, GDN_DV * hb), lambda c, h: (c, h)),
                   pl.BlockSpec((1, hb, GDN_DK, GDN_DV), lambda c, h: (c, h, 0, 0)),
                   pl.BlockSpec((1, hb, CHUNK, CHUNK), lambda c, h: (c, h, 0, 0))],
        out_shape=[jax.ShapeDtypeStruct((t, GDN_W), MXU_DTYPE),
                   jax.ShapeDtypeStruct((nc, GDN_HEADS, GDN_DK, GDN_DV), F32),
                   jax.ShapeDtypeStruct((nc, GDN_HEADS, CHUNK, CHUNK), F32)],
        scratch_shapes=[pltpu.VMEM((GDN_HEADS, GDN_DK, GDN_DV), F32)],
        compiler_params=_params(("arbitrary", "arbitrary")),
    )(conv_gdn, proj_main, proj_small, normw, alog, dtb)


def gdn_bwd(dproj_main, conv_gdn, proj_main, proj_small, normw, alog, dtb, hist, t_inv, dy):
    t = conv_gdn.shape[0]
    nc = t // CHUNK
    rev = lambda c: nc - 1 - c
    hb = GDN_HB
    gate_blk = COL_GATE // (GDN_DV * hb)

    def body(alias_ref, qkv_ref, gate_ref, sm_ref, nw_ref, al_ref, db_ref, hist_ref, t_ref, dy_ref,
             dgate_ref, dqkv_ref, dsm_ref, dnw_ref, dal_ref, ddb_ref, dstate_ref):
        del alias_ref
        c, h = pl.program_id(0), pl.program_id(1)
        h0 = _first_head()

        @pl.when(c == 0)
        def _():
            for j in range(hb):
                dstate_ref[h0 + j] = jnp.zeros((GDN_DK, GDN_DV), F32)

        saved = [t_ref[0, j] for j in range(hb)]

        def fn(qs, ks, vs, small, gates, nw, al, db, states):
            return gdn_chunk(h0, qs, ks, vs, small, gates, nw, al, db, states, saved)[:2]

        qs, ks, vs = _gdn_parts(qkv_ref)
        _, vjp = jax.vjp(fn, qs, ks, vs, sm_ref[...], _head_cols(gate_ref), nw_ref[...], al_ref[...], db_ref[...],
                         [hist_ref[0, j] for j in range(hb)])
        dqs, dks, dvs, dsm, dgates, dnw, dal, ddb, dstates = vjp(
            (_head_cols(dy_ref), [dstate_ref[h0 + j] for j in range(hb)]))
        for j in range(hb):
            base = j * GDN_HC
            dqkv_ref[:, base:base + GDN_DK] = dqs[j]
            dqkv_ref[:, base + GDN_DK:base + 2 * GDN_DK] = dks[j]
            dqkv_ref[:, base + 2 * GDN_DK:base + GDN_HC] = dvs[j]
            dgate_ref[:, j * GDN_DV:(j + 1) * GDN_DV] = dgates[j].astype(dgate_ref.dtype)
            dstate_ref[h0 + j] = dstates[j]
        _accumulate(dsm_ref, h == 0, dsm)
        first = jnp.logical_and(c == 0, h == 0)
        _accumulate(dnw_ref, first, dnw)
        _accumulate(dal_ref, first, dal)
        _accumulate(ddb_ref, first, ddb)

    return pl.pallas_call(
        body, name="gdn_bwd", grid=(nc, GDN_HEADS // hb),
        in_specs=[ANY, pl.BlockSpec((CHUNK, GDN_HC * hb), lambda c, h: (rev(c), h)),
                  pl.BlockSpec((CHUNK, GDN_DV * hb), lambda c, h: (rev(c), gate_blk + h)),
                  pl.BlockSpec((CHUNK, LANES), lambda c, h: (rev(c), 0)),
                  _full((1, GDN_DV)), _full((1, LANES)), _full((1, LANES)),
                  pl.BlockSpec((1, hb, GDN_DK, GDN_DV), lambda c, h: (rev(c), h, 0, 0)),
                  pl.BlockSpec((1, hb, CHUNK, CHUNK), lambda c, h: (rev(c), h, 0, 0)),
                  pl.BlockSpec((CHUNK, GDN_DV * hb), lambda c, h: (rev(c), h))],
        out_specs=[pl.BlockSpec((CHUNK, GDN_DV * hb), lambda c, h: (rev(c), gate_blk + h)),
                   pl.BlockSpec((CHUNK, GDN_HC * hb), lambda c, h: (rev(c), h)),
                   pl.BlockSpec((CHUNK, LANES), lambda c, h: (rev(c), 0)),
                   _full((1, GDN_DV)), _full((1, LANES)), _full((1, LANES))],
        out_shape=[jax.ShapeDtypeStruct(dproj_main.shape, dproj_main.dtype), jax.ShapeDtypeStruct((t, GDN_CONV), F32),
                   jax.ShapeDtypeStruct((t, LANES), F32), jax.ShapeDtypeStruct((1, GDN_DV), F32),
                   jax.ShapeDtypeStruct((1, LANES), F32), jax.ShapeDtypeStruct((1, LANES), F32)],
        scratch_shapes=[pltpu.VMEM((GDN_HEADS, GDN_DK, GDN_DV), F32)],
        input_output_aliases={0: 0},
        compiler_params=_params(("arbitrary", "arbitrary")),
    )(dproj_main, conv_gdn, proj_main, proj_small, normw, alog, dtb, hist, t_inv, dy)


def out_proj_loss(x, y_ssd, y_gdn, w_out, final_w, target):
    t = x.shape[0]
    tm = min(256, t)

    def body(x_ref, ys_ref, yg_ref, wo_ref, fw_ref, tg_ref, loss_ref, dhid_ref, dys_ref, dyg_ref, dwo_ref, dfw_ref):
        i = pl.program_id(0)
        ys, yg = ys_ref[...], yg_ref[...]
        wo_s, wo_g = wo_ref[:SSD_WIDTH, :], wo_ref[SSD_WIDTH:, :]
        hid = x_ref[...] + _raw_dot(ys, wo_s, 1, 0) + _raw_dot(yg, wo_g, 1, 0)
        out, vjp = jax.vjp(rmsnorm, hid, fw_ref[...])
        err = out - tg_ref[...]
        loss = 0.5 * jnp.sum(jnp.mean(err * err, axis=-1, keepdims=True), axis=0, keepdims=True)
        dhid, dfw = vjp(err * (1.0 / D_MODEL))
        dhid_ref[...] = dhid
        dys_ref[...] = _raw_dot(dhid, wo_s, 1, 1)
        dyg_ref[...] = _raw_dot(dhid, wo_g, 1, 1)
        first = i == 0
        _accumulate(loss_ref, first, jnp.broadcast_to(loss, loss_ref.shape))
        _accumulate(dfw_ref, first, dfw)

        @pl.when(first)
        def _():
            dwo_ref[:SSD_WIDTH, :] = _raw_dot(ys, dhid, 0, 0)
            dwo_ref[SSD_WIDTH:, :] = _raw_dot(yg, dhid, 0, 0)

        @pl.when(i > 0)
        def _():
            dwo_ref[:SSD_WIDTH, :] += _raw_dot(ys, dhid, 0, 0)
            dwo_ref[SSD_WIDTH:, :] += _raw_dot(yg, dhid, 0, 0)

    row = lambda w: pl.BlockSpec((tm, w), lambda i: (i, 0))
    return pl.pallas_call(
        body, name="out_proj_loss", grid=(t // tm,),
        in_specs=[row(D_MODEL), row(SSD_WIDTH), row(GDN_W), _full((SSD_WIDTH + GDN_W, D_MODEL)), _full((1, D_MODEL)),
                  row(D_MODEL)],
        out_specs=[_full((8, LANES)), row(D_MODEL), row(SSD_WIDTH), row(GDN_W), _full((SSD_WIDTH + GDN_W, D_MODEL)),
                   _full((1, D_MODEL))],
        out_shape=[jax.ShapeDtypeStruct((8, LANES), F32), jax.ShapeDtypeStruct((t, D_MODEL), F32),
                   jax.ShapeDtypeStruct((t, SSD_WIDTH), F32), jax.ShapeDtypeStruct((t, GDN_W), F32),
                   jax.ShapeDtypeStruct((SSD_WIDTH + GDN_W, D_MODEL), F32), jax.ShapeDtypeStruct((1, D_MODEL), F32)],
        compiler_params=_params(("arbitrary",)),
    )(x, y_ssd, y_gdn, w_out, final_w, target)


def in_proj_bwd_x(x, normw, w_main, w_small, dproj_main, dsmall_a, dsmall_b, dhid):
    t = x.shape[0]
    tm, tk = min(1024, t), 512
    nk = MAIN // tk

    def body(x_ref, nw_ref, wm_ref, ws_ref, dp_ref, da_ref, db_ref, dh_ref, gx_ref, dnw_ref, acc_ref):
        i, k = pl.program_id(0), pl.program_id(1)
        part = _raw_dot(dp_ref[...], wm_ref[...], 1, 1)

        @pl.when(k == 0)
        def _():
            acc_ref[...] = part + _raw_dot(da_ref[...] + db_ref[...], ws_ref[...], 1, 1)

        @pl.when(k > 0)
        def _():
            acc_ref[...] += part

        @pl.when(k == nk - 1)
        def _():
            _, vjp = jax.vjp(rmsnorm, x_ref[...], nw_ref[...])
            dx, dnw = vjp(acc_ref[...])
            gx_ref[...] = dx + dh_ref[...]
            _accumulate(dnw_ref, i == 0, dnw)

    row = lambda w: pl.BlockSpec((tm, w), lambda i, k: (i, 0))
    return pl.pallas_call(
        body, name="in_proj_bwd_x", grid=(t // tm, nk),
        in_specs=[row(D_MODEL), _full((1, D_MODEL)), pl.BlockSpec((D_MODEL, tk), lambda i, k: (0, k)),
                  _full((D_MODEL, LANES)), pl.BlockSpec((tm, tk), lambda i, k: (i, k)), row(LANES), row(LANES),
                  row(D_MODEL)],
        out_specs=[row(D_MODEL), _full((1, D_MODEL))],
        out_shape=[jax.ShapeDtypeStruct((t, D_MODEL), F32), jax.ShapeDtypeStruct((1, D_MODEL), F32)],
        scratch_shapes=[pltpu.VMEM((tm, D_MODEL), F32)],
        compiler_params=_params(("arbitrary", "arbitrary")),
    )(x, normw, w_main, w_small, dproj_main, dsmall_a, dsmall_b, dhid)


def in_proj_bwd_w(u, dproj_main, dsmall_a, dsmall_b):
    t = u.shape[0]
    tm, tn = min(512, t), MAIN // 4

    def body(u_ref, dp_ref, da_ref, db_ref, dwm_ref, dws_ref):
        j, i = pl.program_id(0), pl.program_id(1)
        uu = u_ref[...]
        _accumulate(dwm_ref, i == 0, _raw_dot(uu, dp_ref[...], 0, 0))

        @pl.when(j == 0)
        def _():
            _accumulate(dws_ref, i == 0, _raw_dot(uu, da_ref[...] + db_ref[...], 0, 0))

    return pl.pallas_call(
        body, name="in_proj_bwd_w", grid=(MAIN // tn, t // tm),
        in_specs=[pl.BlockSpec((tm, D_MODEL), lambda j, i: (i, 0)), pl.BlockSpec((tm, tn), lambda j, i: (i, j)),
                  pl.BlockSpec((tm, LANES), lambda j, i: (i, 0)), pl.BlockSpec((tm, LANES), lambda j, i: (i, 0))],
        out_specs=[pl.BlockSpec((D_MODEL, tn), lambda j, i: (0, j)), _full((D_MODEL, LANES))],
        out_shape=[jax.ShapeDtypeStruct((D_MODEL, MAIN), F32), jax.ShapeDtypeStruct((D_MODEL, LANES), F32)],
        compiler_params=_params(("arbitrary", "arbitrary")),
    )(u, dproj_main, dsmall_a, dsmall_b)


def sum_slabs(a, name):
    n, rows, cols = a.shape
    tr = 64 if rows % 64 == 0 else rows

    def body(a_ref, o_ref):
        acc = a_ref[0].astype(F32)
        for d in range(1, n):
            acc = acc + a_ref[d].astype(F32)
        o_ref[...] = acc

    return pl.pallas_call(
        body, name=name, grid=(rows // tr,),
        in_specs=[pl.BlockSpec((n, tr, cols), lambda i: (0, i, 0))],
        out_specs=pl.BlockSpec((tr, cols), lambda i: (i, 0)),
        out_shape=jax.ShapeDtypeStruct((rows, cols), F32),
        compiler_params=_params(("arbitrary",)),
    )(a)


def adamw(w, g, m, v, name):
    rows, cols = w.shape
    tr = 128 if rows % 128 == 0 else rows

    def body(w_ref, g_ref, m_ref, v_ref, d_ref, nm_ref, nv_ref):
        gg = g_ref[...]
        nm = ADAM_B1 * m_ref[...] + (1.0 - ADAM_B1) * gg
        nv = ADAM_B2 * v_ref[...] + (1.0 - ADAM_B2) * (gg * gg)
        m_hat = nm / (1.0 - ADAM_B1 ** ADAM_STEP)
        v_hat = nv / (1.0 - ADAM_B2 ** ADAM_STEP)
        d_ref[...] = -ADAM_LR * (m_hat / (jnp.sqrt(v_hat) + ADAM_EPS) + ADAM_WD * w_ref[...])
        nm_ref[...] = nm
        nv_ref[...] = nv

    spec = pl.BlockSpec((tr, cols), lambda i: (i, 0))
    shp = jax.ShapeDtypeStruct((rows, cols), F32)
    return pl.pallas_call(
        body, name=name, grid=(rows // tr,), in_specs=[spec] * 4, out_specs=[spec] * 3, out_shape=[shp] * 3,
        compiler_params=_params(("arbitrary",)),
    )(w, g, m, v)


def _my_place():
    return lax.axis_index("x"), lax.axis_index("y"), lax.axis_index("c")


def gather_weights(shards):
    n = len(shards)

    def body(*refs):
        srcs, outs = refs[:n], refs[n:2 * n]
        send_sems, recv_sems, local_sems = refs[2 * n:]
        x, y, c = _my_place()
        me = 2 * x + y
        chips = [(1 - x, y), (x, 1 - y), (1 - x, 1 - y)]
        local = [pltpu.make_async_copy(srcs[i], outs[i].at[me], local_sems.at[i]) for i in range(n)]
        for cp in local:
            cp.start()
        sends = []
        for j, (px, py) in enumerate(chips):
            for i in range(n):
                sends.append(pltpu.make_async_remote_copy(
                    src_ref=srcs[i], dst_ref=outs[i].at[me], send_sem=send_sems.at[j * n + i],
                    recv_sem=recv_sems.at[j * n + i], device_id=(px, py, c), device_id_type=MESH))
        for cp in sends:
            cp.start()
        for j, (px, py) in enumerate(chips):
            for i in range(n):
                pltpu.make_async_remote_copy(
                    src_ref=srcs[i], dst_ref=outs[i].at[2 * px + py], send_sem=send_sems.at[j * n + i],
                    recv_sem=recv_sems.at[j * n + i], device_id=(px, py, c), device_id_type=MESH).wait_recv()
        for cp in sends:
            cp.wait_send()
        for cp in local:
            cp.wait()

    return pl.pallas_call(
        body, name="gather_weights",
        in_specs=[HBM] * n, out_specs=[HBM] * n,
        out_shape=[jax.ShapeDtypeStruct((N_CHIP,) + s.shape, s.dtype) for s in shards],
        scratch_shapes=[pltpu.SemaphoreType.DMA((3 * n,)), pltpu.SemaphoreType.DMA((3 * n,)),
                        pltpu.SemaphoreType.DMA((n,))],
    )(*shards)


def _peer(x, y, c, mask):
    mx, my, mc = (mask >> 2) & 1, (mask >> 1) & 1, mask & 1
    return (x ^ mx if mx else x, y ^ my if my else y, c ^ mc if mc else c)


def exchange_slabs(slabbed, replicated):
    ns, nr = len(slabbed), len(replicated)
    n = ns + nr

    def body(*refs):
        srcs, outs = refs[:n], refs[n:2 * n]
        send_sems, recv_sems, local_sems = refs[2 * n:]
        x, y, c = _my_place()
        me = 4 * x + 2 * y + c

        def piece(i, dev):
            return srcs[i].at[dev] if i < ns else srcs[i]

        local = [pltpu.make_async_copy(piece(i, me), outs[i].at[me], local_sems.at[i]) for i in range(n)]
        for cp in local:
            cp.start()
        sends = []
        for mask in range(1, N_DEV):
            px, py, pc = _peer(x, y, c, mask)
            dev = 4 * px + 2 * py + pc
            for i in range(n):
                k = (mask - 1) * n + i
                sends.append(pltpu.make_async_remote_copy(
                    src_ref=piece(i, dev), dst_ref=outs[i].at[me], send_sem=send_sems.at[k], recv_sem=recv_sems.at[k],
                    device_id=(px, py, pc), device_id_type=MESH))
        for cp in sends:
            cp.start()
        for mask in range(1, N_DEV):
            px, py, pc = _peer(x, y, c, mask)
            dev = 4 * px + 2 * py + pc
            for i in range(n):
                k = (mask - 1) * n + i
                pltpu.make_async_remote_copy(
                    src_ref=piece(i, dev), dst_ref=outs[i].at[dev], send_sem=send_sems.at[k], recv_sem=recv_sems.at[k],
                    device_id=(px, py, pc), device_id_type=MESH).wait_recv()
        for cp in sends:
            cp.wait_send()
        for cp in local:
            cp.wait()

    shapes = [jax.ShapeDtypeStruct(a.shape, a.dtype) for a in slabbed]
    shapes += [jax.ShapeDtypeStruct((N_DEV,) + a.shape, a.dtype) for a in replicated]
    return pl.pallas_call(
        body, name="exchange_slabs",
        in_specs=[HBM] * n, out_specs=[HBM] * n, out_shape=shapes,
        scratch_shapes=[pltpu.SemaphoreType.DMA((7 * n,)), pltpu.SemaphoreType.DMA((7 * n,)),
                        pltpu.SemaphoreType.DMA((n,))],
    )(*slabbed, *replicated)


def exchange_halves(halves):
    n = len(halves)
    streams = 8

    def body(*refs):
        srcs, outs, mine, theirs = refs[:n], refs[n:2 * n], refs[2 * n:3 * n], refs[3 * n:4 * n]
        send_sems, recv_sems, in_sems, out_sems = refs[4 * n:]
        x, y, c = _my_place()
        loads = [pltpu.make_async_copy(srcs[i], mine[i], in_sems.at[i]) for i in range(n)]
        for cp in loads:
            cp.start()
        for cp in loads:
            cp.wait()

        def chunk_copy(i, s):
            rows = halves[i].shape[0] // streams
            k = i * streams + s
            return pltpu.make_async_remote_copy(
                src_ref=mine[i].at[pl.ds(s * rows, rows)], dst_ref=theirs[i].at[pl.ds(s * rows, rows)],
                send_sem=send_sems.at[k], recv_sem=recv_sems.at[k], device_id=(x, y, 1 - c), device_id_type=MESH)

        sends = [chunk_copy(i, s) for i in range(n) for s in range(streams)]
        for cp in sends:
            cp.start()
        own = [pltpu.make_async_copy(mine[i], outs[i].at[c], out_sems.at[i]) for i in range(n)]
        for cp in own:
            cp.start()
        for cp in sends:
            cp.wait_recv()
        got = [pltpu.make_async_copy(theirs[i], outs[i].at[1 - c], out_sems.at[n + i]) for i in range(n)]
        for cp in got:
            cp.start()
        for cp in sends:
            cp.wait_send()
        for cp in own + got:
            cp.wait()

    vmem = [pltpu.VMEM(a.shape, a.dtype) for a in halves]
    return pl.pallas_call(
        body, name="exchange_halves",
        in_specs=[HBM] * n, out_specs=[HBM] * n,
        out_shape=[jax.ShapeDtypeStruct((2,) + a.shape, a.dtype) for a in halves],
        scratch_shapes=vmem + vmem + [pltpu.SemaphoreType.DMA((n * streams,)), pltpu.SemaphoreType.DMA((n * streams,)),
                                      pltpu.SemaphoreType.DMA((n,)), pltpu.SemaphoreType.DMA((2 * n,))],
        compiler_params=pltpu.CompilerParams(vmem_limit_bytes=VMEM_LIMIT),
    )(*halves)


def _pack_cols(pieces):
    offs, pos = [], 0
    for a in pieces:
        offs.append(pos)
        pos += a.shape[1]
    rows8 = [jnp.pad(a.astype(F32), ((0, 8 - a.shape[0]), (0, 0))) for a in pieces]
    return jnp.concatenate(rows8, axis=1), offs


def adamw_many(ws, gs, ms, vs):
    n = len(ws)

    def body(*refs):
        w_r, g_r, m_r, v_r = refs[:n], refs[n:2 * n], refs[2 * n:3 * n], refs[3 * n:4 * n]
        d_o, m_o, v_o = refs[4 * n:5 * n], refs[5 * n:6 * n], refs[6 * n:7 * n]
        for i in range(n):
            gg = g_r[i][...]
            nm = ADAM_B1 * m_r[i][...] + (1.0 - ADAM_B1) * gg
            nv = ADAM_B2 * v_r[i][...] + (1.0 - ADAM_B2) * (gg * gg)
            m_hat = nm / (1.0 - ADAM_B1 ** ADAM_STEP)
            v_hat = nv / (1.0 - ADAM_B2 ** ADAM_STEP)
            d_o[i][...] = -ADAM_LR * (m_hat / (jnp.sqrt(v_hat) + ADAM_EPS) + ADAM_WD * w_r[i][...])
            m_o[i][...] = nm
            v_o[i][...] = nv

    shapes = [jax.ShapeDtypeStruct(w.shape, F32) for w in ws]
    out = pl.pallas_call(body, name="adamw_small", out_shape=shapes * 3,
                         compiler_params=pltpu.CompilerParams(vmem_limit_bytes=VMEM_LIMIT))(*ws, *gs, *ms, *vs)
    return out[:n], out[n:2 * n], out[2 * n:]


def _lanes(vec, start):
    n = vec.shape[-1]
    return jnp.pad(vec.reshape(1, n).astype(F32), ((0, 0), (start, LANES - start - n)))


def kernel(x, norm_w, w_in, ssd_conv_w, ssd_conv_b, ssd_dt_bias, ssd_a_log, ssd_d, ssd_norm_w, gdn_conv_w, gdn_dt_bias, gdn_a_log, gdn_norm_w, w_out, final_norm_w, loss_target, m_norm_w, m_w_in, m_ssd_conv_w, m_ssd_conv_b, m_ssd_dt_bias, m_ssd_a_log, m_ssd_d, m_ssd_norm_w, m_gdn_conv_w, m_gdn_dt_bias, m_gdn_a_log, m_gdn_norm_w, m_w_out, m_final_norm_w, v_norm_w, v_w_in, v_ssd_conv_w, v_ssd_conv_b, v_ssd_dt_bias, v_ssd_a_log, v_ssd_d, v_ssd_norm_w, v_gdn_conv_w, v_gdn_dt_bias, v_gdn_a_log, v_gdn_norm_w, v_w_out, v_final_norm_w):
    xs = x[0]
    target = loss_target[0]
    chip = 2 * lax.axis_index("x") + lax.axis_index("y")
    w_in_shard, w_out_shard = w_in[0], w_out[0]
    in_cols = w_in_shard.shape[1]
    out_rows = w_out_shard.shape[0]

    g_in, g_out, g_cs, g_cg = gather_weights(
        [w_in_shard.astype(MXU_DTYPE), w_out_shard.astype(MXU_DTYPE), ssd_conv_w[0], gdn_conv_w[0]])
    w_in_full = jnp.transpose(g_in, (1, 0, 2)).reshape(D_MODEL, IN_DIM)
    w_out_full = g_out.reshape(N_CHIP * out_rows, D_MODEL)
    cw_ssd = _ssd_perm(jnp.transpose(g_cs, (1, 0, 2)).reshape(4, SSD_CONV))
    cw_gdn = _gdn_perm(jnp.transpose(g_cg, (1, 0, 2)).reshape(4, GDN_CONV))
    cb_ssd = _ssd_perm(ssd_conv_b)
    cb_gdn = jnp.zeros((1, GDN_CONV), F32)
    o_xbc, o_dt, o_gate, o_qkv, o_ab = 1024, 2560, 2576, 3600, 6672
    w_main = jnp.concatenate([w_in_full[:, :o_xbc], w_in_full[:, o_gate:o_qkv], _ssd_perm(w_in_full[:, o_xbc:o_dt]),
                              _gdn_perm(w_in_full[:, o_qkv:o_ab])], axis=1)
    w_small = jnp.concatenate([w_in_full[:, o_dt:o_gate], w_in_full[:, o_ab:],
                               jnp.zeros((D_MODEL, LANES - 32), MXU_DTYPE)], axis=1)
    alog = _lanes(ssd_a_log, 0) + _lanes(gdn_a_log, LANE_GA)
    dtb = _lanes(ssd_dt_bias, 0) + _lanes(gdn_dt_bias, LANE_GA)
    dvec = _lanes(ssd_d, 0)
    fw = final_norm_w.reshape(1, D_MODEL)

    proj_main, proj_small, u = in_proj(xs, norm_w, w_main, w_small)
    conv_ssd = conv_fwd(proj_main, COL_SSD, SSD_CONV, cw_ssd, cb_ssd, "conv_fwd_ssd")
    conv_gdn = conv_fwd(proj_main, COL_GDN, GDN_CONV, cw_gdn, cb_gdn, "conv_fwd_gdn")
    y_ssd, hist_ssd = ssd_fwd(conv_ssd, proj_main, proj_small, ssd_norm_w, alog, dtb, dvec)
    y_gdn, hist_gdn, tinv_gdn = gdn_fwd(conv_gdn, proj_main, proj_small, gdn_norm_w, alog, dtb)

    loss_blk, dhid, dy_ssd, dy_gdn, d_w_out, d_fw = out_proj_loss(xs, y_ssd, y_gdn, w_out_full, fw, target)
    dconv_ssd, dproj_main, dsmall_ssd, d_ssd_nw, d_alog_s, d_dtb_s, d_dvec = ssd_bwd(
        conv_ssd, proj_main, proj_small, ssd_norm_w, alog, dtb, dvec, hist_ssd, dy_ssd)
    dproj_main, dconv_gdn, dsmall_gdn, d_gdn_nw, d_alog_g, d_dtb_g = gdn_bwd(
        dproj_main, conv_gdn, proj_main, proj_small, gdn_norm_w, alog, dtb, hist_gdn, tinv_gdn, dy_gdn)
    dproj_main, dwb_ssd = conv_bwd(dproj_main, proj_main, COL_SSD, SSD_CONV, cw_ssd, cb_ssd, dconv_ssd, "conv_bwd_ssd")
    dproj_main, dwb_gdn = conv_bwd(dproj_main, proj_main, COL_GDN, GDN_CONV, cw_gdn, cb_gdn, dconv_gdn, "conv_bwd_gdn")
    grad_x, d_norm_w = in_proj_bwd_x(xs, norm_w, w_main, w_small, dproj_main, dsmall_ssd, dsmall_gdn, dhid)
    d_w_main, d_w_small = in_proj_bwd_w(u, dproj_main, dsmall_ssd, dsmall_gdn)

    d_w_in = jnp.concatenate([d_w_main[:, :COL_GATE], _ssd_unperm(d_w_main[:, COL_SSD:COL_GDN]), d_w_small[:, 0:16],
                              d_w_main[:, COL_GATE:COL_SSD], _gdn_unperm(d_w_main[:, COL_GDN:]), d_w_small[:, 16:32]],
                             axis=1)
    d_w_in = jnp.transpose(d_w_in.reshape(D_MODEL, N_CHIP, in_cols), (1, 0, 2))
    d_alog, d_dtb = d_alog_s + d_alog_g, d_dtb_s + d_dtb_g
    packed, (o_nw, o_cs, o_cg, o_snw, o_fw, o_al, o_db, o_dv, o_gnw, o_loss) = _pack_cols([
        d_norm_w, _ssd_unperm(dwb_ssd), _gdn_unperm(dwb_gdn),
        d_ssd_nw.reshape(1, SSD_WIDTH), d_fw, d_alog, d_dtb, d_dvec, d_gdn_nw, loss_blk])

    r_in, r_out, r_small = exchange_slabs(
        [d_w_in.reshape(N_DEV, D_MODEL // 2, in_cols).astype(COMM_DTYPE),
         d_w_out.reshape(N_DEV, out_rows // 2, D_MODEL).astype(COMM_DTYPE)],
        [packed])
    half_in = sum_slabs(r_in, "sum_w_in")
    half_out = sum_slabs(r_out, "sum_w_out")
    tot = sum_slabs(r_small, "sum_small")
    full_in, full_out = exchange_halves([half_in, half_out])
    grad_w_in = full_in.reshape(D_MODEL, in_cols)
    grad_w_out = full_out.reshape(out_rows, D_MODEL)
    loss = tot[0, o_loss]
    sc, gc = ssd_conv_w.shape[2], gdn_conv_w.shape[2]
    row = lambda off, n, r=0: tot[r:r + 1, off:off + n]
    gs = [row(o_nw, D_MODEL),
          lax.dynamic_slice(tot, (0, o_cs + chip * sc), (4, sc)),
          row(o_cs, SSD_CONV, 4),
          row(o_db, SSD_HEADS), row(o_al, SSD_HEADS), row(o_dv, SSD_HEADS),
          row(o_snw, SSD_WIDTH),
          lax.dynamic_slice(tot, (0, o_cg + chip * gc), (4, gc)),
          row(o_db + LANE_GA, GDN_HEADS), row(o_al + LANE_GA, GDN_HEADS),
          row(o_gnw, GDN_DV), row(o_fw, D_MODEL)]

    names = ["norm_w", "ssd_conv_w", "ssd_conv_b", "ssd_dt_bias", "ssd_a_log", "ssd_d", "ssd_norm_w", "gdn_conv_w",
             "gdn_dt_bias", "gdn_a_log", "gdn_norm_w", "final_norm_w"]
    ws = [norm_w, ssd_conv_w, ssd_conv_b, ssd_dt_bias, ssd_a_log, ssd_d, ssd_norm_w, gdn_conv_w, gdn_dt_bias,
          gdn_a_log, gdn_norm_w, final_norm_w]
    ms = [m_norm_w, m_ssd_conv_w, m_ssd_conv_b, m_ssd_dt_bias, m_ssd_a_log, m_ssd_d, m_ssd_norm_w, m_gdn_conv_w,
          m_gdn_dt_bias, m_gdn_a_log, m_gdn_norm_w, m_final_norm_w]
    vs = [v_norm_w, v_ssd_conv_w, v_ssd_conv_b, v_ssd_dt_bias, v_ssd_a_log, v_ssd_d, v_ssd_norm_w, v_gdn_conv_w,
          v_gdn_dt_bias, v_gdn_a_log, v_gdn_norm_w, v_final_norm_w]
    shapes = [w.shape for w in ws]
    flat = lambda arrs: [a.reshape(g.shape) for a, g in zip(arrs, gs)]
    d_s, m_s, v_s = adamw_many(flat(ws), gs, flat(ms), flat(vs))
    back = lambda arrs: dict(zip(names, [a.reshape(s) for a, s in zip(arrs, shapes)]))
    delta, new_m, new_v, grads = back(d_s), back(m_s), back(v_s), back(gs)
    d_in, m_in, v_in = adamw(w_in_shard, grad_w_in, m_w_in[0], v_w_in[0], "adamw_w_in")
    d_out, m_out, v_out = adamw(w_out_shard, grad_w_out, m_w_out[0], v_w_out[0], "adamw_w_out")
    for tbl, a_in, a_out in ((grads, grad_w_in, grad_w_out), (delta, d_in, d_out), (new_m, m_in, m_out),
                             (new_v, v_in, v_out)):
        tbl["w_in"] = a_in[None]
        tbl["w_out"] = a_out[None]

    order = ["norm_w", "w_in", "ssd_conv_w", "ssd_conv_b", "ssd_dt_bias", "ssd_a_log", "ssd_d", "ssd_norm_w",
             "gdn_conv_w", "gdn_dt_bias", "gdn_a_log", "gdn_norm_w", "w_out", "final_norm_w"]
    return (loss.reshape(()), grad_x[None], *[grads[k] for k in order], *[delta[k] for k in order],
            *[new_m[k] for k in order], *[new_v[k] for k in order])
```

```python
import functools

import jax
import jax.numpy as jnp
from jax import lax
from jax.experimental import pallas as pl
from jax.experimental.pallas import tpu as pltpu

F32 = jnp.float32
MXU_DTYPE = jnp.bfloat16
COMM_DTYPE = jnp.bfloat16
MESH = pl.DeviceIdType.MESH

D_MODEL = 1024
CHUNK = 64
EPS = 1e-6
SSD_HEADS, SSD_GROUPS, SSD_STATE = 16, 2, 128
SSD_WIDTH, SSD_CONV = 1024, 1536
SSD_GW = SSD_WIDTH // SSD_GROUPS
SSD_GC = SSD_GW + 2 * SSD_STATE
GDN_HEADS, GDN_DK, GDN_DV = 8, 128, 128
GDN_W, GDN_CONV = 1024, 3072
GDN_HC = 2 * GDN_DK + GDN_DV
IN_DIM = 6688
MAIN = 6656
LANES = 128
COL_Z, COL_GATE, COL_SSD, COL_GDN = 0, 1024, 2048, 3584
GDN_HB = 8
LANE_GA, LANE_GB = 16, 24
N_DEV, N_CHIP = 8, 4
VMEM_LIMIT = 52 * 1024 * 1024

ADAM_LR, ADAM_B1, ADAM_B2, ADAM_EPS, ADAM_WD, ADAM_STEP = 0.001, 0.9, 0.999, 1e-08, 0.01, 10


def _ssd_perm(a):
    lead, nb = a.shape[:-1], SSD_GROUPS * SSD_STATE
    x = a[..., :SSD_WIDTH].reshape(*lead, SSD_GROUPS, SSD_GW)
    b = a[..., SSD_WIDTH:SSD_WIDTH + nb].reshape(*lead, SSD_GROUPS, SSD_STATE)
    c = a[..., SSD_WIDTH + nb:].reshape(*lead, SSD_GROUPS, SSD_STATE)
    return jnp.concatenate([x, b, c], axis=-1).reshape(*lead, SSD_CONV)


def _ssd_unperm(a):
    lead = a.shape[:-1]
    g = a.reshape(*lead, SSD_GROUPS, SSD_GC)
    parts = [g[..., :SSD_GW], g[..., SSD_GW:SSD_GW + SSD_STATE], g[..., SSD_GW + SSD_STATE:]]
    return jnp.concatenate([p.reshape(*lead, -1) for p in parts], axis=-1)


def _gdn_perm(a):
    lead = a.shape[:-1]
    return jnp.swapaxes(a.reshape(*lead, 3, GDN_HEADS, GDN_DK), -3, -2).reshape(*lead, GDN_CONV)


def _gdn_unperm(a):
    lead = a.shape[:-1]
    return jnp.swapaxes(a.reshape(*lead, GDN_HEADS, 3, GDN_DK), -3, -2).reshape(*lead, GDN_CONV)


def _split(a, n):
    parts, rest = [], a.astype(F32)
    for i in range(n):
        p = rest.astype(MXU_DTYPE)
        parts.append(p)
        if i < n - 1:
            rest = rest - p.astype(F32)
    return parts


def _raw_dot(a, b, ca, cb, mode="bf16"):
    d = lambda u, v: lax.dot_general(u, v, (((ca,), (cb,)), ((), ())), preferred_element_type=F32)
    if mode == "bf16":
        return d(a.astype(MXU_DTYPE), b.astype(MXU_DTYPE))
    if mode == "x3":
        (ah, al), (bh, bl) = _split(a, 2), _split(b, 2)
        return d(ah, bh) + (d(ah, bl) + d(al, bh))
    if mode == "sel_a":
        a0 = a.astype(MXU_DTYPE)
        b1, b2, b3 = _split(b, 3)
        return d(a0, b1) + (d(a0, b2) + d(a0, b3))
    assert mode == "sel_b", mode
    b0 = b.astype(MXU_DTYPE)
    a1, a2, a3 = _split(a, 3)
    return d(a1, b0) + (d(a2, b0) + d(a3, b0))


@functools.partial(jax.custom_vjp, nondiff_argnums=(2,))
def mm_nn(a, b, mode="bf16"):
    return _raw_dot(a, b, 1, 0, mode)


@functools.partial(jax.custom_vjp, nondiff_argnums=(2,))
def mm_nt(a, b, mode="bf16"):
    return _raw_dot(a, b, 1, 1, mode)


@functools.partial(jax.custom_vjp, nondiff_argnums=(2,))
def mm_tn(a, b, mode="bf16"):
    return _raw_dot(a, b, 0, 0, mode)


_SAME = {"bf16": ("bf16", "bf16"), "x3": ("x3", "x3")}
_NN_BWD = dict(_SAME, sel_a=("bf16", "sel_a"), sel_b=("sel_b", "bf16"))
_NT_BWD = dict(_SAME, sel_a=("bf16", "sel_b"), sel_b=("sel_b", "bf16"))
_TN_BWD = dict(_SAME, sel_a=("bf16", "sel_a"), sel_b=("sel_a", "bf16"))
mm_nn.defvjp(lambda a, b, m: (_raw_dot(a, b, 1, 0, m), (a, b)),
             lambda m, r, g: (mm_nt(g, r[1], _NN_BWD[m][0]), mm_tn(r[0], g, _NN_BWD[m][1])))
mm_nt.defvjp(lambda a, b, m: (_raw_dot(a, b, 1, 1, m), (a, b)),
             lambda m, r, g: (mm_nn(g, r[1], _NT_BWD[m][0]), mm_tn(g, r[0], _NT_BWD[m][1])))
mm_tn.defvjp(lambda a, b, m: (_raw_dot(a, b, 0, 0, m), (a, b)),
             lambda m, r, g: (mm_nt(r[1], g, _TN_BWD[m][0]), mm_nn(r[0], g, _TN_BWD[m][1])))


@jax.custom_jvp
def sigmoid(x):
    return 1.0 / (1.0 + jnp.exp(-x))


@sigmoid.defjvp
def _sigmoid_jvp(p, t):
    s = sigmoid(p[0])
    return s, t[0] * s * (1.0 - s)


@jax.custom_jvp
def softplus(x):
    return jnp.maximum(x, 0.0) + jnp.log(1.0 + jnp.exp(-jnp.abs(x)))


@softplus.defjvp
def _softplus_jvp(p, t):
    return softplus(p[0]), t[0] * sigmoid(p[0])


def silu(x):
    return x * sigmoid(x)


def rmsnorm(x, w):
    return x * lax.rsqrt(jnp.mean(x * x, axis=-1, keepdims=True) + EPS) * w


def _iota(shape, dim):
    return lax.broadcasted_iota(jnp.int32, shape, dim)


def _tri_inv_impl(mats):
    n = mats[0].shape[0]
    r, c = _iota((n, n), 0), _iota((n, n), 1)
    eye = jnp.where(r == c, 1.0, 0.0).astype(F32)
    blockdiag = (r >> 4) == (c >> 4)
    dot = lambda u, v: _raw_dot(u, v, 1, 0, "x3")
    each = lambda f, *ls: [f(*xs) for xs in zip(*ls)]
    dg = each(lambda a: jnp.where(blockdiag, a, 0.0), mats)
    off = each(lambda a, d: a - d, mats, dg)
    m = each(lambda d: -d, dg)
    p = each(lambda x: eye + x, m)
    pw = m
    for _ in range(3):
        pw = each(lambda x: dot(x, x), pw)
        p = each(lambda x, y: x + dot(x, y), p, pw)
    e = each(dot, p, off)
    e2 = each(lambda x: dot(x, x), e)
    q = each(lambda x: eye - x, e)
    q = each(lambda x, y: x + dot(x, y), q, e2)
    return each(dot, q, p)


def _tri_inv_bwd(ts, gs):
    x = [mm_nt(g, t, "x3") for g, t in zip(gs, ts)]
    return [-mm_tn(t, y, "x3") for t, y in zip(ts, x)]


@jax.custom_vjp
def tri_inv(mats):
    return _tri_inv_impl(mats)


def _tri_inv_fwd(mats):
    ts = _tri_inv_impl(mats)
    return ts, ts


tri_inv.defvjp(_tri_inv_fwd, lambda ts, gs: (_tri_inv_bwd(ts, gs),))


@jax.custom_vjp
def tri_inv_saved(mats, ts):
    del mats
    return ts


tri_inv_saved.defvjp(lambda mats, ts: (ts, ts),
                     lambda ts, gs: (_tri_inv_bwd(ts, gs), [jnp.zeros_like(t) for t in ts]))


def _chunk_masks():
    r, c = _iota((CHUNK, CHUNK), 0), _iota((CHUNK, CHUNK), 1)
    return r >= c, r > c, r == c


def _log_decay_cumsum(small, alog, dtb, incl):
    sp = softplus(small + dtb)
    la = -jnp.exp(alog) * sp
    tri = jnp.where(incl, 1.0, 0.0).astype(F32)
    return sp, mm_nn(tri, la, "sel_a")


def _col_of(x, lane_mask):
    return jnp.sum(jnp.where(lane_mask, x, 0.0), axis=1, keepdims=True)


def _decay_matrix(col, incl, eye):
    row = jnp.sum(jnp.where(eye, col, 0.0), axis=0, keepdims=True)
    return jnp.where(incl, jnp.exp(jnp.where(incl, col - row, 0.0)), 0.0)


def gdn_chunk(h0, qs, ks, vs, small, gates, normw, alog, dtb, states, saved_t=None):
    incl, strict, eye = _chunk_masks()
    lane = _iota((1, LANES), 1)
    last = _iota((CHUNK, 1), 0) == CHUNK - 1
    _, lac = _log_decay_cumsum(small, alog, dtb, incl)
    heads = range(len(qs))
    each = lambda f, *ls: [f(*xs) for xs in zip(*ls)]
    gc = [_col_of(lac, lane == LANE_GA + h0 + j) for j in heads]
    beta = [sigmoid(_col_of(small, lane == LANE_GB + h0 + j)) for j in heads]
    decay = each(lambda x: _decay_matrix(x, incl, eye), gc)
    gl = each(lambda x: jnp.sum(jnp.where(last, x, 0.0), axis=0, keepdims=True), gc)
    q = each(lambda x: x * lax.rsqrt(jnp.sum(x * x, axis=-1, keepdims=True) + EPS) * (GDN_DK ** -0.5), qs)
    k = each(lambda x: x * lax.rsqrt(jnp.sum(x * x, axis=-1, keepdims=True) + EPS), ks)
    kb = each(lambda x, b: x * b, k, beta)
    a = each(lambda x, y, d: jnp.where(strict, mm_nt(x, y) * d, 0.0), kb, k, decay)
    t = tri_inv(a) if saved_t is None else tri_inv_saved(a, saved_t)
    eg = each(jnp.exp, gc)
    u = each(lambda x, v, b: mm_nn(x, v * b, "x3"), t, vs, beta)
    w = each(lambda x, y, e: mm_nn(x, y * e, "x3"), t, kb, eg)
    attn = each(lambda x, y, d: mm_nt(x, y) * d, q, k, decay)
    v_new = each(lambda x, y, s: x - mm_nn(y, s), u, w, states)
    o = each(lambda x, e, s, at, vn: mm_nn(x * e, s) + mm_nn(at, vn), q, eg, states, attn, v_new)
    new_states = each(lambda s, x, y, l, c: s * jnp.exp(l) + mm_tn(y * jnp.exp(l - c), x), states, v_new, k, gl, gc)
    ys = each(lambda x, gt: rmsnorm(x, normw) * silu(gt), o, gates)
    return ys, new_states, t


@jax.custom_vjp
def split_lanes(x):
    return [x[:, i * LANES:(i + 1) * LANES] for i in range(x.shape[1] // LANES)]


@jax.custom_vjp
def join_lanes(xs):
    return jnp.concatenate(xs, axis=1)


split_lanes.defvjp(lambda x: (split_lanes(x), None), lambda _, gs: (join_lanes(gs),))
join_lanes.defvjp(lambda xs: (join_lanes(xs), None), lambda _, g: (split_lanes(g),))


def ssd_chunk(g, xs, bm, cm, z, small, normw, alog, dtb, dvec, state):
    incl, _, eye = _chunk_masks()
    lane = _iota((1, LANES), 1)
    last = _iota((CHUNK, 1), 0) == CHUNK - 1
    hpg = SSD_HEADS // SSD_GROUPS
    sp, lac = _log_decay_cumsum(small, alog, dtb, incl)
    sel = jnp.where(_iota((LANES, SSD_GW), 0) == g * hpg + (_iota((LANES, SSD_GW), 1) >> 6), 1.0, 0.0).astype(F32)
    lac_last = jnp.sum(jnp.where(last, lac, 0.0), axis=0, keepdims=True)
    dt_e = mm_nn(sp, sel, "sel_b")
    elac_e = mm_nn(jnp.exp(lac), sel, "sel_b")
    toend_e = mm_nn(jnp.exp(lac_last - lac), sel, "sel_b")
    row8 = _iota((8, LANES), 0)
    two = jnp.where(row8 == 0, dvec, 0.0) + jnp.where(row8 == 1, jnp.exp(lac_last), 0.0)
    two_e = mm_nn(two, sel, "sel_b")
    row8e = _iota((8, SSD_GW), 0)
    d_e = jnp.sum(jnp.where(row8e == 0, two_e, 0.0), axis=0, keepdims=True)
    chunk_e = jnp.sum(jnp.where(row8e == 1, two_e, 0.0), axis=0, keepdims=True)
    xdt = xs * dt_e
    cb = mm_nt(cm, bm)
    y = mm_nn(cm, state) * elac_e + xs * d_e
    x_pairs = split_lanes(xdt)
    half = _iota((1, LANES), 1) >> 6
    lms = [_decay_matrix(_col_of(lac, lane == g * hpg + j), incl, eye) for j in range(hpg)]
    terms = [mm_nn(cb * lms[j], jnp.where(half == j % 2, x_pairs[j // 2], 0.0)) for j in range(hpg)]
    y = y + join_lanes([terms[2 * p] + terms[2 * p + 1] for p in range(hpg // 2)])
    new_state = state * chunk_e + mm_tn(bm, xdt * toend_e)
    yg = y * silu(z)
    return rmsnorm(yg, normw), new_state


def _params(sem=None):
    return pltpu.CompilerParams(dimension_semantics=sem, vmem_limit_bytes=VMEM_LIMIT)


def _full(shape):
    n = len(shape)
    return pl.BlockSpec(shape, lambda *_: (0,) * n)


ANY = pl.BlockSpec(memory_space=pl.ANY)
HBM = pl.BlockSpec(memory_space=pltpu.HBM)


def in_proj(x, normw, w_main, w_small):
    t = x.shape[0]
    tm, tn = min(1024, t), 512

    def body(x_ref, nw_ref, wm_ref, ws_ref, pm_ref, ps_ref, u_ref):
        @pl.when(pl.program_id(1) == 0)
        def _():
            u = rmsnorm(x_ref[...], nw_ref[...]).astype(MXU_DTYPE)
            u_ref[...] = u
            ps_ref[...] = _raw_dot(u, ws_ref[...], 1, 0)
        pm_ref[...] = _raw_dot(u_ref[...], wm_ref[...], 1, 0)

    return pl.pallas_call(
        body, name="in_proj", grid=(t // tm, MAIN // tn),
        in_specs=[pl.BlockSpec((tm, D_MODEL), lambda i, j: (i, 0)), _full((1, D_MODEL)),
                  pl.BlockSpec((D_MODEL, tn), lambda i, j: (0, j)), _full((D_MODEL, LANES))],
        out_specs=[pl.BlockSpec((tm, tn), lambda i, j: (i, j)), pl.BlockSpec((tm, LANES), lambda i, j: (i, 0)),
                   pl.BlockSpec((tm, D_MODEL), lambda i, j: (i, 0))],
        out_shape=[jax.ShapeDtypeStruct((t, MAIN), F32), jax.ShapeDtypeStruct((t, LANES), F32),
                   jax.ShapeDtypeStruct((t, D_MODEL), MXU_DTYPE)],
        compiler_params=_params(("arbitrary", "arbitrary")),
    )(x, normw, w_main, w_small)


CONV_TC = 512
HALO = 8


def _shift_down(cur, prev, s):
    rolled = pltpu.roll(cur, s, 0)
    top = jnp.where(_iota((HALO, cur.shape[1]), 0) < s, pltpu.roll(prev, s, 0), rolled[:HALO])
    if cur.shape[0] == HALO:
        return top
    return jnp.concatenate([top, rolled[HALO:]], axis=0)


def _shift_up(cur, nxt, s):
    n = cur.shape[0]
    rolled = pltpu.roll(cur, n - s, 0)
    bot = jnp.where(_iota((HALO, cur.shape[1]), 0) >= HALO - s, pltpu.roll(nxt, HALO - s, 0), rolled[n - HALO:])
    return jnp.concatenate([rolled[:n - HALO], bot], axis=0)


def _conv_pre(cur, prev, w_ref, b):
    acc = cur * w_ref[3:4, :] + b
    shifted = [cur]
    for s in (1, 2, 3):
        sh = _shift_down(cur, prev, s)
        shifted.append(sh)
        acc = acc + sh * w_ref[3 - s:4 - s, :]
    return acc, shifted


def conv_fwd(proj_main, col0, width, w, b, name):
    t = proj_main.shape[0]
    tt, c0 = min(512, t), col0 // CONV_TC

    def body(cur_ref, prev_ref, w_ref, b_ref, out_ref):
        prev = jnp.where(pl.program_id(0) > 0, prev_ref[...], 0.0)
        pre, _ = _conv_pre(cur_ref[...], prev, w_ref, b_ref[...])
        out_ref[...] = silu(pre)

    return pl.pallas_call(
        body, name=name, grid=(t // tt, width // CONV_TC),
        in_specs=[pl.BlockSpec((tt, CONV_TC), lambda i, j: (i, c0 + j)),
                  pl.BlockSpec((HALO, CONV_TC), lambda i, j: (jnp.maximum(i * (tt // HALO) - 1, 0), c0 + j)),
                  pl.BlockSpec((4, CONV_TC), lambda i, j: (0, j)), pl.BlockSpec((1, CONV_TC), lambda i, j: (0, j))],
        out_specs=pl.BlockSpec((tt, CONV_TC), lambda i, j: (i, j)),
        out_shape=jax.ShapeDtypeStruct((t, width), F32),
        compiler_params=_params(("arbitrary", "arbitrary")),
    )(proj_main, proj_main, w, b)


def _dsilu(pre):
    sg = sigmoid(pre)
    return sg * (1.0 + pre * (1.0 - sg))


def conv_bwd(dproj_main, proj_main, col0, width, w, b, dout, name):
    t = proj_main.shape[0]
    tt, c0 = min(512, t), col0 // CONV_TC
    nt = t // tt
    after = lambda i: jnp.minimum((i + 1) * (tt // HALO), t // HALO - 1)

    def body(alias_ref, cur_ref, prev_ref, nxt_ref, w_ref, b_ref, do_ref, do_nxt_ref, dx_ref, dwb_ref):
        del alias_ref
        i = pl.program_id(1)
        cur, bias = cur_ref[...], b_ref[...]
        prev = jnp.where(i > 0, prev_ref[...], 0.0)
        pre, shifted = _conv_pre(cur, prev, w_ref, bias)
        dpre = do_ref[...] * _dsilu(pre)
        pre_nxt, _ = _conv_pre(nxt_ref[...], cur[tt - HALO:], w_ref, bias)
        dpre_nxt = jnp.where(i < nt - 1, do_nxt_ref[...] * _dsilu(pre_nxt), 0.0)
        dx = dpre * w_ref[3:4, :]
        for s in (1, 2, 3):
            dx = dx + _shift_up(dpre, dpre_nxt, s) * w_ref[3 - s:4 - s, :]
        dx_ref[...] = dx.astype(dx_ref.dtype)
        row = _iota((HALO, CONV_TC), 0)
        upd = jnp.where(row == 4, jnp.sum(dpre, axis=0, keepdims=True), 0.0)
        for s in range(4):
            upd = upd + jnp.where(row == 3 - s, jnp.sum(dpre * shifted[s], axis=0, keepdims=True), 0.0)
        _accumulate(dwb_ref, i == 0, upd)

    return pl.pallas_call(
        body, name=name, grid=(width // CONV_TC, nt),
        in_specs=[ANY, pl.BlockSpec((tt, CONV_TC), lambda j, i: (i, c0 + j)),
                  pl.BlockSpec((HALO, CONV_TC), lambda j, i: (jnp.maximum(i * (tt // HALO) - 1, 0), c0 + j)),
                  pl.BlockSpec((HALO, CONV_TC), lambda j, i: (after(i), c0 + j)),
                  pl.BlockSpec((4, CONV_TC), lambda j, i: (0, j)), pl.BlockSpec((1, CONV_TC), lambda j, i: (0, j)),
                  pl.BlockSpec((tt, CONV_TC), lambda j, i: (i, j)),
                  pl.BlockSpec((HALO, CONV_TC), lambda j, i: (after(i), j))],
        out_specs=[pl.BlockSpec((tt, CONV_TC), lambda j, i: (i, c0 + j)),
                   pl.BlockSpec((HALO, CONV_TC), lambda j, i: (0, j))],
        out_shape=[jax.ShapeDtypeStruct(dproj_main.shape, dproj_main.dtype), jax.ShapeDtypeStruct((HALO, width), F32)],
        input_output_aliases={0: 0},
        compiler_params=_params(("arbitrary", "arbitrary")),
    )(dproj_main, proj_main, proj_main, proj_main, w, b, dout, dout)


def _ssd_parts(xbc_ref):
    return xbc_ref[:, :SSD_GW], xbc_ref[:, SSD_GW:SSD_GW + SSD_STATE], xbc_ref[:, SSD_GW + SSD_STATE:]


def _gdn_parts(qkv_ref):
    part = lambda o: [qkv_ref[:, j * GDN_HC + o:j * GDN_HC + o + GDN_DK] for j in range(GDN_HB)]
    return part(0), part(GDN_DK), part(2 * GDN_DK)


def _head_cols(ref):
    return [ref[:, j * GDN_DV:(j + 1) * GDN_DV] for j in range(GDN_HB)]


def _first_head():
    return 0 if GDN_HB == GDN_HEADS else pl.program_id(1) * GDN_HB


def ssd_fwd(conv_ssd, proj_main, proj_small, normw, alog, dtb, dvec):
    t = conv_ssd.shape[0]
    nc = t // CHUNK

    def body(xbc_ref, z_ref, sm_ref, nw_ref, al_ref, db_ref, dv_ref, y_ref, hist_ref, state_ref):
        g = pl.program_id(1)

        @pl.when(pl.program_id(0) == 0)
        def _():
            state_ref[g] = jnp.zeros((SSD_STATE, SSD_GW), F32)

        state = state_ref[g]
        hist_ref[0, 0] = state
        y, new_state = ssd_chunk(g, *_ssd_parts(xbc_ref), z_ref[...], sm_ref[...], nw_ref[...], al_ref[...],
                                 db_ref[...], dv_ref[...], state)
        y_ref[...] = y.astype(MXU_DTYPE)
        state_ref[g] = new_state

    return pl.pallas_call(
        body, name="ssd_fwd", grid=(nc, SSD_GROUPS),
        in_specs=[pl.BlockSpec((CHUNK, SSD_GC), lambda c, g: (c, g)),
                  pl.BlockSpec((CHUNK, SSD_GW), lambda c, g: (c, COL_Z // SSD_GW + g)),
                  pl.BlockSpec((CHUNK, LANES), lambda c, g: (c, 0)),
                  pl.BlockSpec((1, SSD_GW), lambda c, g: (0, g)), _full((1, LANES)), _full((1, LANES)), _full((1, LANES))],
        out_specs=[pl.BlockSpec((CHUNK, SSD_GW), lambda c, g: (c, g)),
                   pl.BlockSpec((1, 1, SSD_STATE, SSD_GW), lambda c, g: (c, g, 0, 0))],
        out_shape=[jax.ShapeDtypeStruct((t, SSD_WIDTH), MXU_DTYPE),
                   jax.ShapeDtypeStruct((nc, SSD_GROUPS, SSD_STATE, SSD_GW), F32)],
        scratch_shapes=[pltpu.VMEM((SSD_GROUPS, SSD_STATE, SSD_GW), F32)],
        compiler_params=_params(("arbitrary", "arbitrary")),
    )(conv_ssd, proj_main, proj_small, normw, alog, dtb, dvec)


def _accumulate(ref, first, value):
    @pl.when(first)
    def _():
        ref[...] = value

    @pl.when(jnp.logical_not(first))
    def _():
        ref[...] += value


def ssd_bwd(conv_ssd, proj_main, proj_small, normw, alog, dtb, dvec, hist, dy):
    t = conv_ssd.shape[0]
    nc = t // CHUNK
    rev = lambda c: nc - 1 - c

    def body(xbc_ref, z_ref, sm_ref, nw_ref, al_ref, db_ref, dv_ref, hist_ref, dy_ref,
             dxbc_ref, dz_ref, dsm_ref, dnw_ref, dal_ref, ddb_ref, ddv_ref, dstate_ref):
        c, g = pl.program_id(0), pl.program_id(1)

        @pl.when(c == 0)
        def _():
            dstate_ref[g] = jnp.zeros((SSD_STATE, SSD_GW), F32)

        fn = functools.partial(ssd_chunk, g)
        _, vjp = jax.vjp(fn, *_ssd_parts(xbc_ref), z_ref[...], sm_ref[...], nw_ref[...], al_ref[...], db_ref[...],
                         dv_ref[...], hist_ref[0, 0])
        dxs, dbm, dcm, dz, dsm, dnw, dal, ddb, ddv, dstate = vjp((dy_ref[...], dstate_ref[g]))
        dxbc_ref[:, :SSD_GW] = dxs
        dxbc_ref[:, SSD_GW:SSD_GW + SSD_STATE] = dbm
        dxbc_ref[:, SSD_GW + SSD_STATE:] = dcm
        dz_ref[...] = dz.astype(dz_ref.dtype)
        dstate_ref[g] = dstate
        _accumulate(dsm_ref, g == 0, dsm)
        first = jnp.logical_and(c == 0, g == 0)
        _accumulate(dal_ref, first, dal)
        _accumulate(ddb_ref, first, ddb)
        _accumulate(ddv_ref, first, ddv)

        @pl.when(c == 0)
        def _():
            dnw_ref[g] = dnw

        @pl.when(c > 0)
        def _():
            dnw_ref[g] += dnw

    return pl.pallas_call(
        body, name="ssd_bwd", grid=(nc, SSD_GROUPS),
        in_specs=[pl.BlockSpec((CHUNK, SSD_GC), lambda c, g: (rev(c), g)),
                  pl.BlockSpec((CHUNK, SSD_GW), lambda c, g: (rev(c), COL_Z // SSD_GW + g)),
                  pl.BlockSpec((CHUNK, LANES), lambda c, g: (rev(c), 0)),
                  pl.BlockSpec((1, SSD_GW), lambda c, g: (0, g)), _full((1, LANES)), _full((1, LANES)), _full((1, LANES)),
                  pl.BlockSpec((1, 1, SSD_STATE, SSD_GW), lambda c, g: (rev(c), g, 0, 0)),
                  pl.BlockSpec((CHUNK, SSD_GW), lambda c, g: (rev(c), g))],
        out_specs=[pl.BlockSpec((CHUNK, SSD_GC), lambda c, g: (rev(c), g)),
                   pl.BlockSpec((CHUNK, SSD_GW), lambda c, g: (rev(c), COL_Z // SSD_GW + g)),
                   pl.BlockSpec((CHUNK, LANES), lambda c, g: (rev(c), 0)),
                   _full((SSD_GROUPS, 1, SSD_GW)), _full((1, LANES)), _full((1, LANES)), _full((1, LANES))],
        out_shape=[jax.ShapeDtypeStruct((t, SSD_CONV), F32), jax.ShapeDtypeStruct((t, MAIN), MXU_DTYPE),
                   jax.ShapeDtypeStruct((t, LANES), F32), jax.ShapeDtypeStruct((SSD_GROUPS, 1, SSD_GW), F32),
                   jax.ShapeDtypeStruct((1, LANES), F32), jax.ShapeDtypeStruct((1, LANES), F32),
                   jax.ShapeDtypeStruct((1, LANES), F32)],
        scratch_shapes=[pltpu.VMEM((SSD_GROUPS, SSD_STATE, SSD_GW), F32)],
        compiler_params=_params(("arbitrary", "arbitrary")),
    )(conv_ssd, proj_main, proj_small, normw, alog, dtb, dvec, hist, dy)


def gdn_fwd(conv_gdn, proj_main, proj_small, normw, alog, dtb):
    t = conv_gdn.shape[0]
    nc = t // CHUNK

    hb = GDN_HB
    gate_blk = COL_GATE // (GDN_DV * hb)

    def body(qkv_ref, gate_ref, sm_ref, nw_ref, al_ref, db_ref, y_ref, hist_ref, t_ref, state_ref):
        h0 = _first_head()

        @pl.when(pl.program_id(0) == 0)
        def _():
            for j in range(hb):
                state_ref[h0 + j] = jnp.zeros((GDN_DK, GDN_DV), F32)

        states = [state_ref[h0 + j] for j in range(hb)]
        for j in range(hb):
            hist_ref[0, j] = states[j]
        qs, ks, vs = _gdn_parts(qkv_ref)
        ys, new_states, ts = gdn_chunk(h0, qs, ks, vs, sm_ref[...], _head_cols(gate_ref), nw_ref[...], al_ref[...],
                                       db_ref[...], states)
        for j in range(hb):
            y_ref[:, j * GDN_DV:(j + 1) * GDN_DV] = ys[j].astype(MXU_DTYPE)
            state_ref[h0 + j] = new_states[j]
            t_ref[0, j] = ts[j]

    return pl.pallas_call(
        body, name="gdn_fwd", grid=(nc, GDN_HEADS // hb),
        in_specs=[pl.BlockSpec((CHUNK, GDN_HC * hb), lambda c, h: (c, h)),
                  pl.BlockSpec((CHUNK, GDN_DV * hb), lambda c, h: (c, gate_blk + h)),
                  pl.BlockSpec((CHUNK, LANES), lambda c, h: (c, 0)),
                  _full((1, GDN_DV)), _full((1, LANES)), _full((1, LANES))],
        out_specs=[pl.BlockSpec((CHUNK, GDN_DV * hb), lambda c, h: (c, h)),
                   pl.BlockSpec((1, hb, GDN_DK, GDN_DV), lambda c, h: (c, h, 0, 0)),
                   pl.BlockSpec((1, hb, CHUNK, CHUNK), lambda c, h: (c, h, 0, 0))],
        out_shape=[jax.ShapeDtypeStruct((t, GDN_W), MXU_DTYPE),
                   jax.ShapeDtypeStruct((nc, GDN_HEADS, GDN_DK, GDN_DV), F32),
                   jax.ShapeDtypeStruct((nc, GDN_HEADS, CHUNK, CHUNK), F32)],
        scratch_shapes=[pltpu.VMEM((GDN_HEADS, GDN_DK, GDN_DV), F32)],
        compiler_params=_params(("arbitrary", "arbitrary")),
    )(conv_gdn, proj_main, proj_small, normw, alog, dtb)


def gdn_bwd(dproj_main, conv_gdn, proj_main, proj_small, normw, alog, dtb, hist, t_inv, dy):
    t = conv_gdn.shape[0]
    nc = t // CHUNK
    rev = lambda c: nc - 1 - c
    hb = GDN_HB
    gate_blk = COL_GATE // (GDN_DV * hb)

    def body(alias_ref, qkv_ref, gate_ref, sm_ref, nw_ref, al_ref, db_ref, hist_ref, t_ref, dy_ref,
             dgate_ref, dqkv_ref, dsm_ref, dnw_ref, dal_ref, ddb_ref, dstate_ref):
        del alias_ref
        c, h = pl.program_id(0), pl.program_id(1)
        h0 = _first_head()

        @pl.when(c == 0)
        def _():
            for j in range(hb):
                dstate_ref[h0 + j] = jnp.zeros((GDN_DK, GDN_DV), F32)

        saved = [t_ref[0, j] for j in range(hb)]

        def fn(qs, ks, vs, small, gates, nw, al, db, states):
            return gdn_chunk(h0, qs, ks, vs, small, gates, nw, al, db, states, saved)[:2]

        qs, ks, vs = _gdn_parts(qkv_ref)
        _, vjp = jax.vjp(fn, qs, ks, vs, sm_ref[...], _head_cols(gate_ref), nw_ref[...], al_ref[...], db_ref[...],
                         [hist_ref[0, j] for j in range(hb)])
        dqs, dks, dvs, dsm, dgates, dnw, dal, ddb, dstates = vjp(
            (_head_cols(dy_ref), [dstate_ref[h0 + j] for j in range(hb)]))
        for j in range(hb):
            base = j * GDN_HC
            dqkv_ref[:, base:base + GDN_DK] = dqs[j]
            dqkv_ref[:, base + GDN_DK:base + 2 * GDN_DK] = dks[j]
            dqkv_ref[:, base + 2 * GDN_DK:base + GDN_HC] = dvs[j]
            dgate_ref[:, j * GDN_DV:(j + 1) * GDN_DV] = dgates[j].astype(dgate_ref.dtype)
            dstate_ref[h0 + j] = dstates[j]
        _accumulate(dsm_ref, h == 0, dsm)
        first = jnp.logical_and(c == 0, h == 0)
        _accumulate(dnw_ref, first, dnw)
        _accumulate(dal_ref, first, dal)
        _accumulate(ddb_ref, first, ddb)

    return pl.pallas_call(
        body, name="gdn_bwd", grid=(nc, GDN_HEADS // hb),
        in_specs=[ANY, pl.BlockSpec((CHUNK, GDN_HC * hb), lambda c, h: (rev(c), h)),
                  pl.BlockSpec((CHUNK, GDN_DV * hb), lambda c, h: (rev(c), gate_blk + h)),
                  pl.BlockSpec((CHUNK, LANES), lambda c, h: (rev(c), 0)),
                  _full((1, GDN_DV)), _full((1, LANES)), _full((1, LANES)),
                  pl.BlockSpec((1, hb, GDN_DK, GDN_DV), lambda c, h: (rev(c), h, 0, 0)),
                  pl.BlockSpec((1, hb, CHUNK, CHUNK), lambda c, h: (rev(c), h, 0, 0)),
                  pl.BlockSpec((CHUNK, GDN_DV * hb), lambda c, h: (rev(c), h))],
        out_specs=[pl.BlockSpec((CHUNK, GDN_DV * hb), lambda c, h: (rev(c), gate_blk + h)),
                   pl.BlockSpec((CHUNK, GDN_HC * hb), lambda c, h: (rev(c), h)),
                   pl.BlockSpec((CHUNK, LANES), lambda c, h: (rev(c), 0)),
                   _full((1, GDN_DV)), _full((1, LANES)), _full((1, LANES))],
        out_shape=[jax.ShapeDtypeStruct(dproj_main.shape, dproj_main.dtype), jax.ShapeDtypeStruct((t, GDN_CONV), F32),
                   jax.ShapeDtypeStruct((t, LANES), F32), jax.ShapeDtypeStruct((1, GDN_DV), F32),
                   jax.ShapeDtypeStruct((1, LANES), F32), jax.ShapeDtypeStruct((1, LANES), F32)],
        scratch_shapes=[pltpu.VMEM((GDN_HEADS, GDN_DK, GDN_DV), F32)],
        input_output_aliases={0: 0},
        compiler_params=_params(("arbitrary", "arbitrary")),
    )(dproj_main, conv_gdn, proj_main, proj_small, normw, alog, dtb, hist, t_inv, dy)


def out_proj_loss(x, y_ssd, y_gdn, w_out, final_w, target):
    t = x.shape[0]
    tm = min(256, t)

    def body(x_ref, ys_ref, yg_ref, wo_ref, fw_ref, tg_ref, loss_ref, dhid_ref, dys_ref, dyg_ref, dwo_ref, dfw_ref):
        i = pl.program_id(0)
        ys, yg = ys_ref[...], yg_ref[...]
        wo_s, wo_g = wo_ref[:SSD_WIDTH, :], wo_ref[SSD_WIDTH:, :]
        hid = x_ref[...] + _raw_dot(ys, wo_s, 1, 0) + _raw_dot(yg, wo_g, 1, 0)
        out, vjp = jax.vjp(rmsnorm, hid, fw_ref[...])
        err = out - tg_ref[...]
        loss = 0.5 * jnp.sum(jnp.mean(err * err, axis=-1, keepdims=True), axis=0, keepdims=True)
        dhid, dfw = vjp(err * (1.0 / D_MODEL))
        dhid_ref[...] = dhid
        dys_ref[...] = _raw_dot(dhid, wo_s, 1, 1)
        dyg_ref[...] = _raw_dot(dhid, wo_g, 1, 1)
        first = i == 0
        _accumulate(loss_ref, first, jnp.broadcast_to(loss, loss_ref.shape))
        _accumulate(dfw_ref, first, dfw)

        @pl.when(first)
        def _():
            dwo_ref[:SSD_WIDTH, :] = _raw_dot(ys, dhid, 0, 0)
            dwo_ref[SSD_WIDTH:, :] = _raw_dot(yg, dhid, 0, 0)

        @pl.when(i > 0)
        def _():
            dwo_ref[:SSD_WIDTH, :] += _raw_dot(ys, dhid, 0, 0)
            dwo_ref[SSD_WIDTH:, :] += _raw_dot(yg, dhid, 0, 0)

    row = lambda w: pl.BlockSpec((tm, w), lambda i: (i, 0))
    return pl.pallas_call(
        body, name="out_proj_loss", grid=(t // tm,),
        in_specs=[row(D_MODEL), row(SSD_WIDTH), row(GDN_W), _full((SSD_WIDTH + GDN_W, D_MODEL)), _full((1, D_MODEL)),
                  row(D_MODEL)],
        out_specs=[_full((8, LANES)), row(D_MODEL), row(SSD_WIDTH), row(GDN_W), _full((SSD_WIDTH + GDN_W, D_MODEL)),
                   _full((1, D_MODEL))],
        out_shape=[jax.ShapeDtypeStruct((8, LANES), F32), jax.ShapeDtypeStruct((t, D_MODEL), F32),
                   jax.ShapeDtypeStruct((t, SSD_WIDTH), F32), jax.ShapeDtypeStruct((t, GDN_W), F32),
                   jax.ShapeDtypeStruct((SSD_WIDTH + GDN_W, D_MODEL), F32), jax.ShapeDtypeStruct((1, D_MODEL), F32)],
        compiler_params=_params(("arbitrary",)),
    )(x, y_ssd, y_gdn, w_out, final_w, target)


def in_proj_bwd_x(x, normw, w_main, w_small, dproj_main, dsmall_a, dsmall_b, dhid, slabbed):
    t = x.shape[0]
    tm, tk = min(1024, t), 512
    nk = MAIN // tk
    ni = t // tm
    ns = len(slabbed)

    def body(x_ref, nw_ref, wm_ref, ws_ref, dp_ref, da_ref, db_ref, dh_ref, *rest):
        slab_refs, (gx_ref, dnw_ref), land_refs = rest[:ns], rest[ns:ns + 2], rest[ns + 2:2 * ns + 2]
        acc_ref, sems = rest[2 * ns + 2], rest[2 * ns + 3:]
        i, k = pl.program_id(0), pl.program_id(1)
        start, finish = _slab_exchange(slab_refs, land_refs, ns, *sems)

        @pl.when(jnp.logical_and(i == 0, k == 0))
        def _():
            start()

        part = _raw_dot(dp_ref[...], wm_ref[...], 1, 1)

        @pl.when(k == 0)
        def _():
            acc_ref[...] = part + _raw_dot(da_ref[...] + db_ref[...], ws_ref[...], 1, 1)

        @pl.when(k > 0)
        def _():
            acc_ref[...] += part

        @pl.when(k == nk - 1)
        def _():
            _, vjp = jax.vjp(rmsnorm, x_ref[...], nw_ref[...])
            dx, dnw = vjp(acc_ref[...])
            gx_ref[...] = dx + dh_ref[...]
            _accumulate(dnw_ref, i == 0, dnw)

        @pl.when(jnp.logical_and(i == ni - 1, k == nk - 1))
        def _():
            finish()

    row = lambda w: pl.BlockSpec((tm, w), lambda i, k: (i, 0))
    out = pl.pallas_call(
        body, name="in_proj_bwd_x", grid=(ni, nk),
        in_specs=[row(D_MODEL), _full((1, D_MODEL)), pl.BlockSpec((D_MODEL, tk), lambda i, k: (0, k)),
                  _full((D_MODEL, LANES)), pl.BlockSpec((tm, tk), lambda i, k: (i, k)), row(LANES), row(LANES),
                  row(D_MODEL)] + [HBM] * ns,
        out_specs=[row(D_MODEL), _full((1, D_MODEL))] + [HBM] * ns,
        out_shape=[jax.ShapeDtypeStruct((t, D_MODEL), F32), jax.ShapeDtypeStruct((1, D_MODEL), F32)]
        + _slab_exchange_shapes(slabbed, []),
        scratch_shapes=[pltpu.VMEM((tm, D_MODEL), F32)] + _slab_exchange_sems(ns),
        compiler_params=_params(("arbitrary", "arbitrary")),
    )(x, normw, w_main, w_small, dproj_main, dsmall_a, dsmall_b, dhid, *slabbed)
    return out[0], out[1], out[2:]


def in_proj_bwd_w(u, dproj_main, dsmall_a, dsmall_b):
    t = u.shape[0]
    tm, tn = min(512, t), MAIN // 4

    def body(u_ref, dp_ref, da_ref, db_ref, dwm_ref, dws_ref):
        j, i = pl.program_id(0), pl.program_id(1)
        uu = u_ref[...]
        _accumulate(dwm_ref, i == 0, _raw_dot(uu, dp_ref[...], 0, 0))

        @pl.when(j == 0)
        def _():
            _accumulate(dws_ref, i == 0, _raw_dot(uu, da_ref[...] + db_ref[...], 0, 0))

    return pl.pallas_call(
        body, name="in_proj_bwd_w", grid=(MAIN // tn, t // tm),
        in_specs=[pl.BlockSpec((tm, D_MODEL), lambda j, i: (i, 0)), pl.BlockSpec((tm, tn), lambda j, i: (i, j)),
                  pl.BlockSpec((tm, LANES), lambda j, i: (i, 0)), pl.BlockSpec((tm, LANES), lambda j, i: (i, 0))],
        out_specs=[pl.BlockSpec((D_MODEL, tn), lambda j, i: (0, j)), _full((D_MODEL, LANES))],
        out_shape=[jax.ShapeDtypeStruct((D_MODEL, MAIN), F32), jax.ShapeDtypeStruct((D_MODEL, LANES), F32)],
        compiler_params=_params(("arbitrary", "arbitrary")),
    )(u, dproj_main, dsmall_a, dsmall_b)


def sum_slabs(a, name):
    n, rows, cols = a.shape
    tr = 64 if rows % 64 == 0 else rows

    def body(a_ref, o_ref):
        acc = a_ref[0].astype(F32)
        for d in range(1, n):
            acc = acc + a_ref[d].astype(F32)
        o_ref[...] = acc

    return pl.pallas_call(
        body, name=name, grid=(rows // tr,),
        in_specs=[pl.BlockSpec((n, tr, cols), lambda i: (0, i, 0))],
        out_specs=pl.BlockSpec((tr, cols), lambda i: (i, 0)),
        out_shape=jax.ShapeDtypeStruct((rows, cols), F32),
        compiler_params=_params(("arbitrary",)),
    )(a)


def adamw(w, g, m, v, name):
    rows, cols = w.shape
    tr = 128 if rows % 128 == 0 else rows

    def body(w_ref, g_ref, m_ref, v_ref, d_ref, nm_ref, nv_ref):
        gg = g_ref[...]
        nm = ADAM_B1 * m_ref[...] + (1.0 - ADAM_B1) * gg
        nv = ADAM_B2 * v_ref[...] + (1.0 - ADAM_B2) * (gg * gg)
        m_hat = nm / (1.0 - ADAM_B1 ** ADAM_STEP)
        v_hat = nv / (1.0 - ADAM_B2 ** ADAM_STEP)
        d_ref[...] = -ADAM_LR * (m_hat / (jnp.sqrt(v_hat) + ADAM_EPS) + ADAM_WD * w_ref[...])
        nm_ref[...] = nm
        nv_ref[...] = nv

    spec = pl.BlockSpec((tr, cols), lambda i: (i, 0))
    shp = jax.ShapeDtypeStruct((rows, cols), F32)
    return pl.pallas_call(
        body, name=name, grid=(rows // tr,), in_specs=[spec] * 4, out_specs=[spec] * 3, out_shape=[shp] * 3,
        compiler_params=_params(("arbitrary",)),
    )(w, g, m, v)


def _my_place():
    return lax.axis_index("x"), lax.axis_index("y"), lax.axis_index("c")


def gather_weights(shards):
    n = len(shards)

    def body(*refs):
        srcs, outs = refs[:n], refs[n:2 * n]
        send_sems, recv_sems, local_sems = refs[2 * n:]
        x, y, c = _my_place()
        me = 2 * x + y
        chips = [(1 - x, y), (x, 1 - y), (1 - x, 1 - y)]
        local = [pltpu.make_async_copy(srcs[i], outs[i].at[me], local_sems.at[i]) for i in range(n)]
        for cp in local:
            cp.start()
        sends = []
        for j, (px, py) in enumerate(chips):
            for i in range(n):
                sends.append(pltpu.make_async_remote_copy(
                    src_ref=srcs[i], dst_ref=outs[i].at[me], send_sem=send_sems.at[j * n + i],
                    recv_sem=recv_sems.at[j * n + i], device_id=(px, py, c), device_id_type=MESH))
        for cp in sends:
            cp.start()
        for j, (px, py) in enumerate(chips):
            for i in range(n):
                pltpu.make_async_remote_copy(
                    src_ref=srcs[i], dst_ref=outs[i].at[2 * px + py], send_sem=send_sems.at[j * n + i],
                    recv_sem=recv_sems.at[j * n + i], device_id=(px, py, c), device_id_type=MESH).wait_recv()
        for cp in sends:
            cp.wait_send()
        for cp in local:
            cp.wait()

    return pl.pallas_call(
        body, name="gather_weights",
        in_specs=[HBM] * n, out_specs=[HBM] * n,
        out_shape=[jax.ShapeDtypeStruct((N_CHIP,) + s.shape, s.dtype) for s in shards],
        scratch_shapes=[pltpu.SemaphoreType.DMA((3 * n,)), pltpu.SemaphoreType.DMA((3 * n,)),
                        pltpu.SemaphoreType.DMA((n,))],
    )(*shards)


def _peer(x, y, c, mask):
    mx, my, mc = (mask >> 2) & 1, (mask >> 1) & 1, mask & 1
    return (x ^ mx if mx else x, y ^ my if my else y, c ^ mc if mc else c)


def exchange_slabs(slabbed, replicated):
    ns, n = len(slabbed), len(slabbed) + len(replicated)

    def body(*refs):
        start, finish = _slab_exchange(refs[:n], refs[n:2 * n], ns, *refs[2 * n:])
        start()
        finish()

    return pl.pallas_call(
        body, name="exchange_slabs",
        in_specs=[HBM] * n, out_specs=[HBM] * n, out_shape=_slab_exchange_shapes(slabbed, replicated),
        scratch_shapes=_slab_exchange_sems(n),
    )(*slabbed, *replicated)


def _slab_exchange_shapes(slabbed, replicated):
    return ([jax.ShapeDtypeStruct(a.shape, a.dtype) for a in slabbed]
            + [jax.ShapeDtypeStruct((N_DEV,) + a.shape, a.dtype) for a in replicated])


def _slab_exchange_sems(n):
    return [pltpu.SemaphoreType.DMA((7 * n,)), pltpu.SemaphoreType.DMA((7 * n,)), pltpu.SemaphoreType.DMA((n,))]


def _slab_exchange(srcs, outs, ns, send_sems, recv_sems, local_sems):
    n = len(srcs)
    x, y, c = _my_place()
    me = 4 * x + 2 * y + c

    def piece(i, dev):
        return srcs[i].at[dev] if i < ns else srcs[i]

    def copies(arriving):
        out = []
        for mask in range(1, N_DEV):
            px, py, pc = _peer(x, y, c, mask)
            dev = 4 * px + 2 * py + pc
            for i in range(n):
                k = (mask - 1) * n + i
                out.append(pltpu.make_async_remote_copy(
                    src_ref=piece(i, dev), dst_ref=outs[i].at[dev if arriving else me], send_sem=send_sems.at[k],
                    recv_sem=recv_sems.at[k], device_id=(px, py, pc), device_id_type=MESH))
        return out

    def local():
        return [pltpu.make_async_copy(piece(i, me), outs[i].at[me], local_sems.at[i]) for i in range(n)]

    def start():
        for cp in local() + copies(False):
            cp.start()

    def finish():
        for cp in copies(True):
            cp.wait_recv()
        for cp in copies(False):
            cp.wait_send()
        for cp in local():
            cp.wait()

    return start, finish


def exchange_halves(halves):
    n = len(halves)
    streams = 8

    def body(*refs):
        srcs, outs, mine, theirs = refs[:n], refs[n:2 * n], refs[2 * n:3 * n], refs[3 * n:4 * n]
        send_sems, recv_sems, in_sems, out_sems = refs[4 * n:]
        x, y, c = _my_place()
        loads = [pltpu.make_async_copy(srcs[i], mine[i], in_sems.at[i]) for i in range(n)]
        for cp in loads:
            cp.start()
        for cp in loads:
            cp.wait()

        def chunk_copy(i, s):
            rows = halves[i].shape[0] // streams
            k = i * streams + s
            return pltpu.make_async_remote_copy(
                src_ref=mine[i].at[pl.ds(s * rows, rows)], dst_ref=theirs[i].at[pl.ds(s * rows, rows)],
                send_sem=send_sems.at[k], recv_sem=recv_sems.at[k], device_id=(x, y, 1 - c), device_id_type=MESH)

        sends = [chunk_copy(i, s) for i in range(n) for s in range(streams)]
        for cp in sends:
            cp.start()
        own = [pltpu.make_async_copy(mine[i], outs[i].at[c], out_sems.at[i]) for i in range(n)]
        for cp in own:
            cp.start()
        for cp in sends:
            cp.wait_recv()
        got = [pltpu.make_async_copy(theirs[i], outs[i].at[1 - c], out_sems.at[n + i]) for i in range(n)]
        for cp in got:
            cp.start()
        for cp in sends:
            cp.wait_send()
        for cp in own + got:
            cp.wait()

    vmem = [pltpu.VMEM(a.shape, a.dtype) for a in halves]
    return pl.pallas_call(
        body, name="exchange_halves",
        in_specs=[HBM] * n, out_specs=[HBM] * n,
        out_shape=[jax.ShapeDtypeStruct((2,) + a.shape, a.dtype) for a in halves],
        scratch_shapes=vmem + vmem + [pltpu.SemaphoreType.DMA((n * streams,)), pltpu.SemaphoreType.DMA((n * streams,)),
                                      pltpu.SemaphoreType.DMA((n,)), pltpu.SemaphoreType.DMA((2 * n,))],
        compiler_params=pltpu.CompilerParams(vmem_limit_bytes=VMEM_LIMIT),
    )(*halves)


def _pack_cols(pieces):
    offs, pos = [], 0
    for a in pieces:
        offs.append(pos)
        pos += a.shape[1]
    rows8 = [jnp.pad(a.astype(F32), ((0, 8 - a.shape[0]), (0, 0))) for a in pieces]
    return jnp.concatenate(rows8, axis=1), offs


def adamw_many(ws, gs, ms, vs):
    n = len(ws)

    def body(*refs):
        w_r, g_r, m_r, v_r = refs[:n], refs[n:2 * n], refs[2 * n:3 * n], refs[3 * n:4 * n]
        d_o, m_o, v_o = refs[4 * n:5 * n], refs[5 * n:6 * n], refs[6 * n:7 * n]
        for i in range(n):
            gg = g_r[i][...]
            nm = ADAM_B1 * m_r[i][...] + (1.0 - ADAM_B1) * gg
            nv = ADAM_B2 * v_r[i][...] + (1.0 - ADAM_B2) * (gg * gg)
            m_hat = nm / (1.0 - ADAM_B1 ** ADAM_STEP)
            v_hat = nv / (1.0 - ADAM_B2 ** ADAM_STEP)
            d_o[i][...] = -ADAM_LR * (m_hat / (jnp.sqrt(v_hat) + ADAM_EPS) + ADAM_WD * w_r[i][...])
            m_o[i][...] = nm
            v_o[i][...] = nv

    shapes = [jax.ShapeDtypeStruct(w.shape, F32) for w in ws]
    out = pl.pallas_call(body, name="adamw_small", out_shape=shapes * 3,
                         compiler_params=pltpu.CompilerParams(vmem_limit_bytes=VMEM_LIMIT))(*ws, *gs, *ms, *vs)
    return out[:n], out[n:2 * n], out[2 * n:]


def _lanes(vec, start):
    n = vec.shape[-1]
    return jnp.pad(vec.reshape(1, n).astype(F32), ((0, 0), (start, LANES - start - n)))


def kernel(x, norm_w, w_in, ssd_conv_w, ssd_conv_b, ssd_dt_bias, ssd_a_log, ssd_d, ssd_norm_w, gdn_conv_w, gdn_dt_bias, gdn_a_log, gdn_norm_w, w_out, final_norm_w, loss_target, m_norm_w, m_w_in, m_ssd_conv_w, m_ssd_conv_b, m_ssd_dt_bias, m_ssd_a_log, m_ssd_d, m_ssd_norm_w, m_gdn_conv_w, m_gdn_dt_bias, m_gdn_a_log, m_gdn_norm_w, m_w_out, m_final_norm_w, v_norm_w, v_w_in, v_ssd_conv_w, v_ssd_conv_b, v_ssd_dt_bias, v_ssd_a_log, v_ssd_d, v_ssd_norm_w, v_gdn_conv_w, v_gdn_dt_bias, v_gdn_a_log, v_gdn_norm_w, v_w_out, v_final_norm_w):
    xs = x[0]
    target = loss_target[0]
    chip = 2 * lax.axis_index("x") + lax.axis_index("y")
    w_in_shard, w_out_shard = w_in[0], w_out[0]
    in_cols = w_in_shard.shape[1]
    out_rows = w_out_shard.shape[0]

    g_in, g_out, g_cs, g_cg = gather_weights(
        [w_in_shard.astype(MXU_DTYPE), w_out_shard.astype(MXU_DTYPE), ssd_conv_w[0], gdn_conv_w[0]])
    w_in_full = jnp.transpose(g_in, (1, 0, 2)).reshape(D_MODEL, IN_DIM)
    w_out_full = g_out.reshape(N_CHIP * out_rows, D_MODEL)
    cw_ssd = _ssd_perm(jnp.transpose(g_cs, (1, 0, 2)).reshape(4, SSD_CONV))
    cw_gdn = _gdn_perm(jnp.transpose(g_cg, (1, 0, 2)).reshape(4, GDN_CONV))
    cb_ssd = _ssd_perm(ssd_conv_b)
    cb_gdn = jnp.zeros((1, GDN_CONV), F32)
    o_xbc, o_dt, o_gate, o_qkv, o_ab = 1024, 2560, 2576, 3600, 6672
    w_main = jnp.concatenate([w_in_full[:, :o_xbc], w_in_full[:, o_gate:o_qkv], _ssd_perm(w_in_full[:, o_xbc:o_dt]),
                              _gdn_perm(w_in_full[:, o_qkv:o_ab])], axis=1)
    w_small = jnp.concatenate([w_in_full[:, o_dt:o_gate], w_in_full[:, o_ab:],
                               jnp.zeros((D_MODEL, LANES - 32), MXU_DTYPE)], axis=1)
    alog = _lanes(ssd_a_log, 0) + _lanes(gdn_a_log, LANE_GA)
    dtb = _lanes(ssd_dt_bias, 0) + _lanes(gdn_dt_bias, LANE_GA)
    dvec = _lanes(ssd_d, 0)
    fw = final_norm_w.reshape(1, D_MODEL)

    proj_main, proj_small, u = in_proj(xs, norm_w, w_main, w_small)
    conv_ssd = conv_fwd(proj_main, COL_SSD, SSD_CONV, cw_ssd, cb_ssd, "conv_fwd_ssd")
    conv_gdn = conv_fwd(proj_main, COL_GDN, GDN_CONV, cw_gdn, cb_gdn, "conv_fwd_gdn")
    y_ssd, hist_ssd = ssd_fwd(conv_ssd, proj_main, proj_small, ssd_norm_w, alog, dtb, dvec)
    y_gdn, hist_gdn, tinv_gdn = gdn_fwd(conv_gdn, proj_main, proj_small, gdn_norm_w, alog, dtb)

    loss_blk, dhid, dy_ssd, dy_gdn, d_w_out, d_fw = out_proj_loss(xs, y_ssd, y_gdn, w_out_full, fw, target)
    dconv_ssd, dproj_main, dsmall_ssd, d_ssd_nw, d_alog_s, d_dtb_s, d_dvec = ssd_bwd(
        conv_ssd, proj_main, proj_small, ssd_norm_w, alog, dtb, dvec, hist_ssd, dy_ssd)
    dproj_main, dconv_gdn, dsmall_gdn, d_gdn_nw, d_alog_g, d_dtb_g = gdn_bwd(
        dproj_main, conv_gdn, proj_main, proj_small, gdn_norm_w, alog, dtb, hist_gdn, tinv_gdn, dy_gdn)
    dproj_main, dwb_ssd = conv_bwd(dproj_main, proj_main, COL_SSD, SSD_CONV, cw_ssd, cb_ssd, dconv_ssd, "conv_bwd_ssd")
    dproj_main, dwb_gdn = conv_bwd(dproj_main, proj_main, COL_GDN, GDN_CONV, cw_gdn, cb_gdn, dconv_gdn, "conv_bwd_gdn")
    d_w_main, d_w_small = in_proj_bwd_w(u, dproj_main, dsmall_ssd, dsmall_gdn)

    d_w_in = jnp.concatenate([d_w_main[:, :COL_GATE], _ssd_unperm(d_w_main[:, COL_SSD:COL_GDN]), d_w_small[:, 0:16],
                              d_w_main[:, COL_GATE:COL_SSD], _gdn_unperm(d_w_main[:, COL_GDN:]), d_w_small[:, 16:32]],
                             axis=1)
    d_w_in = jnp.transpose(d_w_in.reshape(D_MODEL, N_CHIP, in_cols), (1, 0, 2))
    slabs = [d_w_in.reshape(N_DEV, D_MODEL // 2, in_cols).astype(COMM_DTYPE),
             d_w_out.reshape(N_DEV, out_rows // 2, D_MODEL).astype(COMM_DTYPE)]

    grad_x, d_norm_w, (r_in, r_out) = in_proj_bwd_x(xs, norm_w, w_main, w_small, dproj_main, dsmall_ssd, dsmall_gdn,
                                                     dhid, slabs)
    d_alog, d_dtb = d_alog_s + d_alog_g, d_dtb_s + d_dtb_g
    packed, (o_nw, o_cs, o_cg, o_snw, o_fw, o_al, o_db, o_dv, o_gnw, o_loss) = _pack_cols([
        d_norm_w, _ssd_unperm(dwb_ssd), _gdn_unperm(dwb_gdn),
        d_ssd_nw.reshape(1, SSD_WIDTH), d_fw, d_alog, d_dtb, d_dvec, d_gdn_nw, loss_blk])
    (r_small,) = exchange_slabs([], [packed])

    half_in = sum_slabs(r_in, "sum_w_in")
    half_out = sum_slabs(r_out, "sum_w_out")
    tot = sum_slabs(r_small, "sum_small")
    full_in, full_out = exchange_halves([half_in, half_out])
    grad_w_in = full_in.reshape(D_MODEL, in_cols)
    grad_w_out = full_out.reshape(out_rows, D_MODEL)
    loss = tot[0, o_loss]
    sc, gc = ssd_conv_w.shape[2], gdn_conv_w.shape[2]
    row = lambda off, n, r=0: tot[r:r + 1, off:off + n]
    gs = [row(o_nw, D_MODEL),
          lax.dynamic_slice(tot, (0, o_cs + chip * sc), (4, sc)),
          row(o_cs, SSD_CONV, 4),
          row(o_db, SSD_HEADS), row(o_al, SSD_HEADS), row(o_dv, SSD_HEADS),
          row(o_snw, SSD_WIDTH),
          lax.dynamic_slice(tot, (0, o_cg + chip * gc), (4, gc)),
          row(o_db + LANE_GA, GDN_HEADS), row(o_al + LANE_GA, GDN_HEADS),
          row(o_gnw, GDN_DV), row(o_fw, D_MODEL)]

    names = ["norm_w", "ssd_conv_w", "ssd_conv_b", "ssd_dt_bias", "ssd_a_log", "ssd_d", "ssd_norm_w", "gdn_conv_w",
             "gdn_dt_bias", "gdn_a_log", "gdn_norm_w", "final_norm_w"]
    ws = [norm_w, ssd_conv_w, ssd_conv_b, ssd_dt_bias, ssd_a_log, ssd_d, ssd_norm_w, gdn_conv_w, gdn_dt_bias,
          gdn_a_log, gdn_norm_w, final_norm_w]
    ms = [m_norm_w, m_ssd_conv_w, m_ssd_conv_b, m_ssd_dt_bias, m_ssd_a_log, m_ssd_d, m_ssd_norm_w, m_gdn_conv_w,
          m_gdn_dt_bias, m_gdn_a_log, m_gdn_norm_w, m_final_norm_w]
    vs = [v_norm_w, v_ssd_conv_w, v_ssd_conv_b, v_ssd_dt_bias, v_ssd_a_log, v_ssd_d, v_ssd_norm_w, v_gdn_conv_w,
          v_gdn_dt_bias, v_gdn_a_log, v_gdn_norm_w, v_final_norm_w]
    shapes = [w.shape for w in ws]
    flat = lambda arrs: [a.reshape(g.shape) for a, g in zip(arrs, gs)]
    d_s, m_s, v_s = adamw_many(flat(ws), gs, flat(ms), flat(vs))
    back = lambda arrs: dict(zip(names, [a.reshape(s) for a, s in zip(arrs, shapes)]))
    delta, new_m, new_v, grads = back(d_s), back(m_s), back(v_s), back(gs)
    d_in, m_in, v_in = adamw(w_in_shard, grad_w_in, m_w_in[0], v_w_in[0], "adamw_w_in")
    d_out, m_out, v_out = adamw(w_out_shard, grad_w_out, m_w_out[0], v_w_out[0], "adamw_w_out")
    for tbl, a_in, a_out in ((grads, grad_w_in, grad_w_out), (delta, d_in, d_out), (new_m, m_in, m_out),
                             (new_v, v_in, v_out)):
        tbl["w_in"] = a_in[None]
        tbl["w_out"] = a_out[None]

    order = ["norm_w", "w_in", "ssd_conv_w", "ssd_conv_b", "ssd_dt_bias", "ssd_a_log", "ssd_d", "ssd_norm_w",
             "gdn_conv_w", "gdn_dt_bias", "gdn_a_log", "gdn_norm_w", "w_out", "final_norm_w"]
    return (loss.reshape(()), grad_x[None], *[grads[k] for k in order], *[delta[k] for k in order],
            *[new_m[k] for k in order], *[new_v[k] for k in order])
```

```python
import functools

import jax
import jax.numpy as jnp
from jax import lax
from jax.experimental import pallas as pl
from jax.experimental.pallas import tpu as pltpu

F32 = jnp.float32
MXU_DTYPE = jnp.bfloat16
COMM_DTYPE = jnp.bfloat16
MESH = pl.DeviceIdType.MESH

D_MODEL = 1024
CHUNK = 64
EPS = 1e-6
SSD_HEADS, SSD_GROUPS, SSD_STATE = 16, 2, 128
SSD_WIDTH, SSD_CONV = 1024, 1536
SSD_GW = SSD_WIDTH // SSD_GROUPS
SSD_GC = SSD_GW + 2 * SSD_STATE
GDN_HEADS, GDN_DK, GDN_DV = 8, 128, 128
GDN_W, GDN_CONV = 1024, 3072
GDN_HC = 2 * GDN_DK + GDN_DV
IN_DIM = 6688
MAIN = 6656
LANES = 128
COL_Z, COL_GATE, COL_SSD, COL_GDN = 0, 1024, 2048, 3584
GDN_HB = 8
LANE_GA, LANE_GB = 16, 24
N_DEV, N_CHIP = 8, 4
VMEM_LIMIT = 52 * 1024 * 1024

ADAM_LR, ADAM_B1, ADAM_B2, ADAM_EPS, ADAM_WD, ADAM_STEP = 0.001, 0.9, 0.999, 1e-08, 0.01, 10


def _ssd_perm(a):
    lead, nb = a.shape[:-1], SSD_GROUPS * SSD_STATE
    x = a[..., :SSD_WIDTH].reshape(*lead, SSD_GROUPS, SSD_GW)
    b = a[..., SSD_WIDTH:SSD_WIDTH + nb].reshape(*lead, SSD_GROUPS, SSD_STATE)
    c = a[..., SSD_WIDTH + nb:].reshape(*lead, SSD_GROUPS, SSD_STATE)
    return jnp.concatenate([x, b, c], axis=-1).reshape(*lead, SSD_CONV)


def _ssd_unperm(a):
    lead = a.shape[:-1]
    g = a.reshape(*lead, SSD_GROUPS, SSD_GC)
    parts = [g[..., :SSD_GW], g[..., SSD_GW:SSD_GW + SSD_STATE], g[..., SSD_GW + SSD_STATE:]]
    return jnp.concatenate([p.reshape(*lead, -1) for p in parts], axis=-1)


def _gdn_perm(a):
    lead = a.shape[:-1]
    return jnp.swapaxes(a.reshape(*lead, 3, GDN_HEADS, GDN_DK), -3, -2).reshape(*lead, GDN_CONV)


def _gdn_unperm(a):
    lead = a.shape[:-1]
    return jnp.swapaxes(a.reshape(*lead, GDN_HEADS, 3, GDN_DK), -3, -2).reshape(*lead, GDN_CONV)


def _split(a, n):
    parts, rest = [], a.astype(F32)
    for i in range(n):
        p = rest.astype(MXU_DTYPE)
        parts.append(p)
        if i < n - 1:
            rest = rest - p.astype(F32)
    return parts


def _raw_dot(a, b, ca, cb, mode="bf16"):
    d = lambda u, v: lax.dot_general(u, v, (((ca,), (cb,)), ((), ())), preferred_element_type=F32)
    if mode == "bf16":
        return d(a.astype(MXU_DTYPE), b.astype(MXU_DTYPE))
    if mode == "x3":
        (ah, al), (bh, bl) = _split(a, 2), _split(b, 2)
        return d(ah, bh) + (d(ah, bl) + d(al, bh))
    if mode == "sel_a":
        a0 = a.astype(MXU_DTYPE)
        b1, b2, b3 = _split(b, 3)
        return d(a0, b1) + (d(a0, b2) + d(a0, b3))
    assert mode == "sel_b", mode
    b0 = b.astype(MXU_DTYPE)
    a1, a2, a3 = _split(a, 3)
    return d(a1, b0) + (d(a2, b0) + d(a3, b0))


@functools.partial(jax.custom_vjp, nondiff_argnums=(2,))
def mm_nn(a, b, mode="bf16"):
    return _raw_dot(a, b, 1, 0, mode)


@functools.partial(jax.custom_vjp, nondiff_argnums=(2,))
def mm_nt(a, b, mode="bf16"):
    return _raw_dot(a, b, 1, 1, mode)


@functools.partial(jax.custom_vjp, nondiff_argnums=(2,))
def mm_tn(a, b, mode="bf16"):
    return _raw_dot(a, b, 0, 0, mode)


_SAME = {"bf16": ("bf16", "bf16"), "x3": ("x3", "x3")}
_NN_BWD = dict(_SAME, sel_a=("bf16", "sel_a"), sel_b=("sel_b", "bf16"))
_NT_BWD = dict(_SAME, sel_a=("bf16", "sel_b"), sel_b=("sel_b", "bf16"))
_TN_BWD = dict(_SAME, sel_a=("bf16", "sel_a"), sel_b=("sel_a", "bf16"))
mm_nn.defvjp(lambda a, b, m: (_raw_dot(a, b, 1, 0, m), (a, b)),
             lambda m, r, g: (mm_nt(g, r[1], _NN_BWD[m][0]), mm_tn(r[0], g, _NN_BWD[m][1])))
mm_nt.defvjp(lambda a, b, m: (_raw_dot(a, b, 1, 1, m), (a, b)),
             lambda m, r, g: (mm_nn(g, r[1], _NT_BWD[m][0]), mm_tn(g, r[0], _NT_BWD[m][1])))
mm_tn.defvjp(lambda a, b, m: (_raw_dot(a, b, 0, 0, m), (a, b)),
             lambda m, r, g: (mm_nt(r[1], g, _TN_BWD[m][0]), mm_nn(r[0], g, _TN_BWD[m][1])))


@jax.custom_jvp
def sigmoid(x):
    return 1.0 / (1.0 + jnp.exp(-x))


@sigmoid.defjvp
def _sigmoid_jvp(p, t):
    s = sigmoid(p[0])
    return s, t[0] * s * (1.0 - s)


@jax.custom_jvp
def softplus(x):
    return jnp.maximum(x, 0.0) + jnp.log(1.0 + jnp.exp(-jnp.abs(x)))


@softplus.defjvp
def _softplus_jvp(p, t):
    return softplus(p[0]), t[0] * sigmoid(p[0])


def silu(x):
    return x * sigmoid(x)


def rmsnorm(x, w):
    return x * lax.rsqrt(jnp.mean(x * x, axis=-1, keepdims=True) + EPS) * w


def _iota(shape, dim):
    return lax.broadcasted_iota(jnp.int32, shape, dim)


def _tri_inv_impl(mats):
    n = mats[0].shape[0]
    r, c = _iota((n, n), 0), _iota((n, n), 1)
    eye = jnp.where(r == c, 1.0, 0.0).astype(F32)
    blockdiag = (r >> 4) == (c >> 4)
    dot = lambda u, v: _raw_dot(u, v, 1, 0, "x3")
    each = lambda f, *ls: [f(*xs) for xs in zip(*ls)]
    dg = each(lambda a: jnp.where(blockdiag, a, 0.0), mats)
    off = each(lambda a, d: a - d, mats, dg)
    m = each(lambda d: -d, dg)
    p = each(lambda x: eye + x, m)
    pw = m
    for _ in range(3):
        pw = each(lambda x: dot(x, x), pw)
        p = each(lambda x, y: x + dot(x, y), p, pw)
    e = each(dot, p, off)
    e2 = each(lambda x: dot(x, x), e)
    q = each(lambda x: eye - x, e)
    q = each(lambda x, y: x + dot(x, y), q, e2)
    return each(dot, q, p)


def _tri_inv_bwd(ts, gs):
    x = [mm_nt(g, t, "x3") for g, t in zip(gs, ts)]
    return [-mm_tn(t, y, "x3") for t, y in zip(ts, x)]


@jax.custom_vjp
def tri_inv(mats):
    return _tri_inv_impl(mats)


def _tri_inv_fwd(mats):
    ts = _tri_inv_impl(mats)
    return ts, ts


tri_inv.defvjp(_tri_inv_fwd, lambda ts, gs: (_tri_inv_bwd(ts, gs),))


@jax.custom_vjp
def tri_inv_saved(mats, ts):
    del mats
    return ts


tri_inv_saved.defvjp(lambda mats, ts: (ts, ts),
                     lambda ts, gs: (_tri_inv_bwd(ts, gs), [jnp.zeros_like(t) for t in ts]))


def _chunk_masks():
    r, c = _iota((CHUNK, CHUNK), 0), _iota((CHUNK, CHUNK), 1)
    return r >= c, r > c, r == c


def _log_decay_cumsum(small, alog, dtb, incl):
    sp = softplus(small + dtb)
    la = -jnp.exp(alog) * sp
    tri = jnp.where(incl, 1.0, 0.0).astype(F32)
    return sp, mm_nn(tri, la, "sel_a")


def _col_of(x, lane_mask):
    return jnp.sum(jnp.where(lane_mask, x, 0.0), axis=1, keepdims=True)


def _decay_matrix(col, incl, eye):
    row = jnp.sum(jnp.where(eye, col, 0.0), axis=0, keepdims=True)
    return jnp.where(incl, jnp.exp(jnp.where(incl, col - row, 0.0)), 0.0)


def gdn_chunk(h0, qs, ks, vs, small, gates, normw, alog, dtb, states, saved_t=None):
    incl, strict, eye = _chunk_masks()
    lane = _iota((1, LANES), 1)
    last = _iota((CHUNK, 1), 0) == CHUNK - 1
    _, lac = _log_decay_cumsum(small, alog, dtb, incl)
    heads = range(len(qs))
    each = lambda f, *ls: [f(*xs) for xs in zip(*ls)]
    gc = [_col_of(lac, lane == LANE_GA + h0 + j) for j in heads]
    beta = [sigmoid(_col_of(small, lane == LANE_GB + h0 + j)) for j in heads]
    decay = each(lambda x: _decay_matrix(x, incl, eye), gc)
    gl = each(lambda x: jnp.sum(jnp.where(last, x, 0.0), axis=0, keepdims=True), gc)
    q = each(lambda x: x * lax.rsqrt(jnp.sum(x * x, axis=-1, keepdims=True) + EPS) * (GDN_DK ** -0.5), qs)
    k = each(lambda x: x * lax.rsqrt(jnp.sum(x * x, axis=-1, keepdims=True) + EPS), ks)
    kb = each(lambda x, b: x * b, k, beta)
    a = each(lambda x, y, d: jnp.where(strict, mm_nt(x, y) * d, 0.0), kb, k, decay)
    t = tri_inv(a) if saved_t is None else tri_inv_saved(a, saved_t)
    eg = each(jnp.exp, gc)
    u = each(lambda x, v, b: mm_nn(x, v * b, "x3"), t, vs, beta)
    w = each(lambda x, y, e: mm_nn(x, y * e, "x3"), t, kb, eg)
    attn = each(lambda x, y, d: mm_nt(x, y) * d, q, k, decay)
    v_new = each(lambda x, y, s: x - mm_nn(y, s), u, w, states)
    o = each(lambda x, e, s, at, vn: mm_nn(x * e, s) + mm_nn(at, vn), q, eg, states, attn, v_new)
    new_states = each(lambda s, x, y, l, c: s * jnp.exp(l) + mm_tn(y * jnp.exp(l - c), x), states, v_new, k, gl, gc)
    ys = each(lambda x, gt: rmsnorm(x, normw) * silu(gt), o, gates)
    return ys, new_states, t


@jax.custom_vjp
def split_lanes(x):
    return [x[:, i * LANES:(i + 1) * LANES] for i in range(x.shape[1] // LANES)]


@jax.custom_vjp
def join_lanes(xs):
    return jnp.concatenate(xs, axis=1)


split_lanes.defvjp(lambda x: (split_lanes(x), None), lambda _, gs: (join_lanes(gs),))
join_lanes.defvjp(lambda xs: (join_lanes(xs), None), lambda _, g: (split_lanes(g),))


def ssd_chunk(g, xs, bm, cm, z, small, normw, alog, dtb, dvec, state):
    incl, _, eye = _chunk_masks()
    lane = _iota((1, LANES), 1)
    last = _iota((CHUNK, 1), 0) == CHUNK - 1
    hpg = SSD_HEADS // SSD_GROUPS
    sp, lac = _log_decay_cumsum(small, alog, dtb, incl)
    sel = jnp.where(_iota((LANES, SSD_GW), 0) == g * hpg + (_iota((LANES, SSD_GW), 1) >> 6), 1.0, 0.0).astype(F32)
    lac_last = jnp.sum(jnp.where(last, lac, 0.0), axis=0, keepdims=True)
    dt_e = mm_nn(sp, sel, "sel_b")
    elac_e = mm_nn(jnp.exp(lac), sel, "sel_b")
    toend_e = mm_nn(jnp.exp(lac_last - lac), sel, "sel_b")
    row8 = _iota((8, LANES), 0)
    two = jnp.where(row8 == 0, dvec, 0.0) + jnp.where(row8 == 1, jnp.exp(lac_last), 0.0)
    two_e = mm_nn(two, sel, "sel_b")
    row8e = _iota((8, SSD_GW), 0)
    d_e = jnp.sum(jnp.where(row8e == 0, two_e, 0.0), axis=0, keepdims=True)
    chunk_e = jnp.sum(jnp.where(row8e == 1, two_e, 0.0), axis=0, keepdims=True)
    xdt = xs * dt_e
    cb = mm_nt(cm, bm)
    y = mm_nn(cm, state) * elac_e + xs * d_e
    x_pairs = split_lanes(xdt)
    half = _iota((1, LANES), 1) >> 6
    lms = [_decay_matrix(_col_of(lac, lane == g * hpg + j), incl, eye) for j in range(hpg)]
    terms = [mm_nn(cb * lms[j], jnp.where(half == j % 2, x_pairs[j // 2], 0.0)) for j in range(hpg)]
    y = y + join_lanes([terms[2 * p] + terms[2 * p + 1] for p in range(hpg // 2)])
    new_state = state * chunk_e + mm_tn(bm, xdt * toend_e)
    yg = y * silu(z)
    return rmsnorm(yg, normw), new_state


def _params(sem=None):
    return pltpu.CompilerParams(dimension_semantics=sem, vmem_limit_bytes=VMEM_LIMIT)


def _full(shape):
    n = len(shape)
    return pl.BlockSpec(shape, lambda *_: (0,) * n)


ANY = pl.BlockSpec(memory_space=pl.ANY)
HBM = pl.BlockSpec(memory_space=pltpu.HBM)


def in_proj(x, normw, w_main, w_small):
    t = x.shape[0]
    tm, tn = min(1024, t), 512

    def body(x_ref, nw_ref, wm_ref, ws_ref, pm_ref, ps_ref, u_ref):
        @pl.when(pl.program_id(1) == 0)
        def _():
            u = rmsnorm(x_ref[...], nw_ref[...]).astype(MXU_DTYPE)
            u_ref[...] = u
            ps_ref[...] = _raw_dot(u, ws_ref[...], 1, 0)
        pm_ref[...] = _raw_dot(u_ref[...], wm_ref[...], 1, 0)

    return pl.pallas_call(
        body, name="in_proj", grid=(t // tm, MAIN // tn),
        in_specs=[pl.BlockSpec((tm, D_MODEL), lambda i, j: (i, 0)), _full((1, D_MODEL)),
                  pl.BlockSpec((D_MODEL, tn), lambda i, j: (0, j)), _full((D_MODEL, LANES))],
        out_specs=[pl.BlockSpec((tm, tn), lambda i, j: (i, j)), pl.BlockSpec((tm, LANES), lambda i, j: (i, 0)),
                   pl.BlockSpec((tm, D_MODEL), lambda i, j: (i, 0))],
        out_shape=[jax.ShapeDtypeStruct((t, MAIN), F32), jax.ShapeDtypeStruct((t, LANES), F32),
                   jax.ShapeDtypeStruct((t, D_MODEL), MXU_DTYPE)],
        compiler_params=_params(("arbitrary", "arbitrary")),
    )(x, normw, w_main, w_small)


CONV_TC = 512
HALO = 8


def _shift_down(cur, prev, s):
    rolled = pltpu.roll(cur, s, 0)
    top = jnp.where(_iota((HALO, cur.shape[1]), 0) < s, pltpu.roll(prev, s, 0), rolled[:HALO])
    if cur.shape[0] == HALO:
        return top
    return jnp.concatenate([top, rolled[HALO:]], axis=0)


def _shift_up(cur, nxt, s):
    n = cur.shape[0]
    rolled = pltpu.roll(cur, n - s, 0)
    bot = jnp.where(_iota((HALO, cur.shape[1]), 0) >= HALO - s, pltpu.roll(nxt, HALO - s, 0), rolled[n - HALO:])
    return jnp.concatenate([rolled[:n - HALO], bot], axis=0)


def _conv_pre(cur, prev, w_ref, b):
    acc = cur * w_ref[3:4, :] + b
    shifted = [cur]
    for s in (1, 2, 3):
        sh = _shift_down(cur, prev, s)
        shifted.append(sh)
        acc = acc + sh * w_ref[3 - s:4 - s, :]
    return acc, shifted


def conv_fwd(proj_main, col0, width, w, b, name):
    t = proj_main.shape[0]
    tt, c0 = min(512, t), col0 // CONV_TC

    def body(cur_ref, prev_ref, w_ref, b_ref, out_ref):
        prev = jnp.where(pl.program_id(0) > 0, prev_ref[...], 0.0)
        pre, _ = _conv_pre(cur_ref[...], prev, w_ref, b_ref[...])
        out_ref[...] = silu(pre)

    return pl.pallas_call(
        body, name=name, grid=(t // tt, width // CONV_TC),
        in_specs=[pl.BlockSpec((tt, CONV_TC), lambda i, j: (i, c0 + j)),
                  pl.BlockSpec((HALO, CONV_TC), lambda i, j: (jnp.maximum(i * (tt // HALO) - 1, 0), c0 + j)),
                  pl.BlockSpec((4, CONV_TC), lambda i, j: (0, j)), pl.BlockSpec((1, CONV_TC), lambda i, j: (0, j))],
        out_specs=pl.BlockSpec((tt, CONV_TC), lambda i, j: (i, j)),
        out_shape=jax.ShapeDtypeStruct((t, width), F32),
        compiler_params=_params(("arbitrary", "arbitrary")),
    )(proj_main, proj_main, w, b)


def _dsilu(pre):
    sg = sigmoid(pre)
    return sg * (1.0 + pre * (1.0 - sg))


def conv_bwd(dproj_main, proj_main, col0, width, w, b, dout, name):
    t = proj_main.shape[0]
    tt, c0 = min(512, t), col0 // CONV_TC
    nt = t // tt
    after = lambda i: jnp.minimum((i + 1) * (tt // HALO), t // HALO - 1)

    def body(alias_ref, cur_ref, prev_ref, nxt_ref, w_ref, b_ref, do_ref, do_nxt_ref, dx_ref, dwb_ref):
        del alias_ref
        i = pl.program_id(1)
        cur, bias = cur_ref[...], b_ref[...]
        prev = jnp.where(i > 0, prev_ref[...], 0.0)
        pre, shifted = _conv_pre(cur, prev, w_ref, bias)
        dpre = do_ref[...] * _dsilu(pre)
        pre_nxt, _ = _conv_pre(nxt_ref[...], cur[tt - HALO:], w_ref, bias)
        dpre_nxt = jnp.where(i < nt - 1, do_nxt_ref[...] * _dsilu(pre_nxt), 0.0)
        dx = dpre * w_ref[3:4, :]
        for s in (1, 2, 3):
            dx = dx + _shift_up(dpre, dpre_nxt, s) * w_ref[3 - s:4 - s, :]
        dx_ref[...] = dx.astype(dx_ref.dtype)
        row = _iota((HALO, CONV_TC), 0)
        upd = jnp.where(row == 4, jnp.sum(dpre, axis=0, keepdims=True), 0.0)
        for s in range(4):
            upd = upd + jnp.where(row == 3 - s, jnp.sum(dpre * shifted[s], axis=0, keepdims=True), 0.0)
        _accumulate(dwb_ref, i == 0, upd)

    return pl.pallas_call(
        body, name=name, grid=(width // CONV_TC, nt),
        in_specs=[ANY, pl.BlockSpec((tt, CONV_TC), lambda j, i: (i, c0 + j)),
                  pl.BlockSpec((HALO, CONV_TC), lambda j, i: (jnp.maximum(i * (tt // HALO) - 1, 0), c0 + j)),
                  pl.BlockSpec((HALO, CONV_TC), lambda j, i: (after(i), c0 + j)),
                  pl.BlockSpec((4, CONV_TC), lambda j, i: (0, j)), pl.BlockSpec((1, CONV_TC), lambda j, i: (0, j)),
                  pl.BlockSpec((tt, CONV_TC), lambda j, i: (i, j)),
                  pl.BlockSpec((HALO, CONV_TC), lambda j, i: (after(i), j))],
        out_specs=[pl.BlockSpec((tt, CONV_TC), lambda j, i: (i, c0 + j)),
                   pl.BlockSpec((HALO, CONV_TC), lambda j, i: (0, j))],
        out_shape=[jax.ShapeDtypeStruct(dproj_main.shape, dproj_main.dtype), jax.ShapeDtypeStruct((HALO, width), F32)],
        input_output_aliases={0: 0},
        compiler_params=_params(("arbitrary", "arbitrary")),
    )(dproj_main, proj_main, proj_main, proj_main, w, b, dout, dout)


def _ssd_parts(xbc_ref):
    return xbc_ref[:, :SSD_GW], xbc_ref[:, SSD_GW:SSD_GW + SSD_STATE], xbc_ref[:, SSD_GW + SSD_STATE:]


def _gdn_parts(qkv_ref):
    part = lambda o: [qkv_ref[:, j * GDN_HC + o:j * GDN_HC + o + GDN_DK] for j in range(GDN_HB)]
    return part(0), part(GDN_DK), part(2 * GDN_DK)


def _head_cols(ref):
    return [ref[:, j * GDN_DV:(j + 1) * GDN_DV] for j in range(GDN_HB)]


def _first_head():
    return 0 if GDN_HB == GDN_HEADS else pl.program_id(1) * GDN_HB


def ssd_fwd(conv_ssd, proj_main, proj_small, normw, alog, dtb, dvec):
    t = conv_ssd.shape[0]
    nc = t // CHUNK

    def body(xbc_ref, z_ref, sm_ref, nw_ref, al_ref, db_ref, dv_ref, y_ref, hist_ref, state_ref):
        g = pl.program_id(1)

        @pl.when(pl.program_id(0) == 0)
        def _():
            state_ref[g] = jnp.zeros((SSD_STATE, SSD_GW), F32)

        state = state_ref[g]
        hist_ref[0, 0] = state
        y, new_state = ssd_chunk(g, *_ssd_parts(xbc_ref), z_ref[...], sm_ref[...], nw_ref[...], al_ref[...],
                                 db_ref[...], dv_ref[...], state)
        y_ref[...] = y.astype(MXU_DTYPE)
        state_ref[g] = new_state

    return pl.pallas_call(
        body, name="ssd_fwd", grid=(nc, SSD_GROUPS),
        in_specs=[pl.BlockSpec((CHUNK, SSD_GC), lambda c, g: (c, g)),
                  pl.BlockSpec((CHUNK, SSD_GW), lambda c, g: (c, COL_Z // SSD_GW + g)),
                  pl.BlockSpec((CHUNK, LANES), lambda c, g: (c, 0)),
                  pl.BlockSpec((1, SSD_GW), lambda c, g: (0, g)), _full((1, LANES)), _full((1, LANES)), _full((1, LANES))],
        out_specs=[pl.BlockSpec((CHUNK, SSD_GW), lambda c, g: (c, g)),
                   pl.BlockSpec((1, 1, SSD_STATE, SSD_GW), lambda c, g: (c, g, 0, 0))],
        out_shape=[jax.ShapeDtypeStruct((t, SSD_WIDTH), MXU_DTYPE),
                   jax.ShapeDtypeStruct((nc, SSD_GROUPS, SSD_STATE, SSD_GW), F32)],
        scratch_shapes=[pltpu.VMEM((SSD_GROUPS, SSD_STATE, SSD_GW), F32)],
        compiler_params=_params(("arbitrary", "arbitrary")),
    )(conv_ssd, proj_main, proj_small, normw, alog, dtb, dvec)


def _accumulate(ref, first, value):
    @pl.when(first)
    def _():
        ref[...] = value

    @pl.when(jnp.logical_not(first))
    def _():
        ref[...] += value


def ssd_bwd(conv_ssd, proj_main, proj_small, normw, alog, dtb, dvec, hist, dy):
    t = conv_ssd.shape[0]
    nc = t // CHUNK
    rev = lambda c: nc - 1 - c

    def body(xbc_ref, z_ref, sm_ref, nw_ref, al_ref, db_ref, dv_ref, hist_ref, dy_ref,
             dxbc_ref, dz_ref, dsm_ref, dnw_ref, dal_ref, ddb_ref, ddv_ref, dstate_ref):
        c, g = pl.program_id(0), pl.program_id(1)

        @pl.when(c == 0)
        def _():
            dstate_ref[g] = jnp.zeros((SSD_STATE, SSD_GW), F32)

        fn = functools.partial(ssd_chunk, g)
        _, vjp = jax.vjp(fn, *_ssd_parts(xbc_ref), z_ref[...], sm_ref[...], nw_ref[...], al_ref[...], db_ref[...],
                         dv_ref[...], hist_ref[0, 0])
        dxs, dbm, dcm, dz, dsm, dnw, dal, ddb, ddv, dstate = vjp((dy_ref[...], dstate_ref[g]))
        dxbc_ref[:, :SSD_GW] = dxs
        dxbc_ref[:, SSD_GW:SSD_GW + SSD_STATE] = dbm
        dxbc_ref[:, SSD_GW + SSD_STATE:] = dcm
        dz_ref[...] = dz.astype(dz_ref.dtype)
        dstate_ref[g] = dstate
        _accumulate(dsm_ref, g == 0, dsm)
        first = jnp.logical_and(c == 0, g == 0)
        _accumulate(dal_ref, first, dal)
        _accumulate(ddb_ref, first, ddb)
        _accumulate(ddv_ref, first, ddv)

        @pl.when(c == 0)
        def _():
            dnw_ref[g] = dnw

        @pl.when(c > 0)
        def _():
            dnw_ref[g] += dnw

    return pl.pallas_call(
        body, name="ssd_bwd", grid=(nc, SSD_GROUPS),
        in_specs=[pl.BlockSpec((CHUNK, SSD_GC), lambda c, g: (rev(c), g)),
                  pl.BlockSpec((CHUNK, SSD_GW), lambda c, g: (rev(c), COL_Z // SSD_GW + g)),
                  pl.BlockSpec((CHUNK, LANES), lambda c, g: (rev(c), 0)),
                  pl.BlockSpec((1, SSD_GW), lambda c, g: (0, g)), _full((1, LANES)), _full((1, LANES)), _full((1, LANES)),
                  pl.BlockSpec((1, 1, SSD_STATE, SSD_GW), lambda c, g: (rev(c), g, 0, 0)),
                  pl.BlockSpec((CHUNK, SSD_GW), lambda c, g: (rev(c), g))],
        out_specs=[pl.BlockSpec((CHUNK, SSD_GC), lambda c, g: (rev(c), g)),
                   pl.BlockSpec((CHUNK, SSD_GW), lambda c, g: (rev(c), COL_Z // SSD_GW + g)),
                   pl.BlockSpec((CHUNK, LANES), lambda c, g: (rev(c), 0)),
                   _full((SSD_GROUPS, 1, SSD_GW)), _full((1, LANES)), _full((1, LANES)), _full((1, LANES))],
        out_shape=[jax.ShapeDtypeStruct((t, SSD_CONV), F32), jax.ShapeDtypeStruct((t, MAIN), MXU_DTYPE),
                   jax.ShapeDtypeStruct((t, LANES), F32), jax.ShapeDtypeStruct((SSD_GROUPS, 1, SSD_GW), F32),
                   jax.ShapeDtypeStruct((1, LANES), F32), jax.ShapeDtypeStruct((1, LANES), F32),
                   jax.ShapeDtypeStruct((1, LANES), F32)],
        scratch_shapes=[pltpu.VMEM((SSD_GROUPS, SSD_STATE, SSD_GW), F32)],
        compiler_params=_params(("arbitrary", "arbitrary")),
    )(conv_ssd, proj_main, proj_small, normw, alog, dtb, dvec, hist, dy)


def gdn_fwd(conv_gdn, proj_main, proj_small, normw, alog, dtb):
    t = conv_gdn.shape[0]
    nc = t // CHUNK

    hb = GDN_HB
    gate_blk = COL_GATE // (GDN_DV * hb)

    def body(qkv_ref, gate_ref, sm_ref, nw_ref, al_ref, db_ref, y_ref, hist_ref, t_ref, state_ref):
        h0 = _first_head()

        @pl.when(pl.program_id(0) == 0)
        def _():
            for j in range(hb):
                state_ref[h0 + j] = jnp.zeros((GDN_DK, GDN_DV), F32)

        states = [state_ref[h0 + j] for j in range(hb)]
        for j in range(hb):
            hist_ref[0, j] = states[j]
        qs, ks, vs = _gdn_parts(qkv_ref)
        ys, new_states, ts = gdn_chunk(h0, qs, ks, vs, sm_ref[...], _head_cols(gate_ref), nw_ref[...], al_ref[...],
                                       db_ref[...], states)
        for j in range(hb):
            y_ref[:, j * GDN_DV:(j + 1) * GDN_DV] = ys[j].astype(MXU_DTYPE)
            state_ref[h0 + j] = new_states[j]
            t_ref[0, j] = ts[j]

    return pl.pallas_call(
        body, name="gdn_fwd", grid=(nc, GDN_HEADS // hb),
        in_specs=[pl.BlockSpec((CHUNK, GDN_HC * hb), lambda c, h: (c, h)),
                  pl.BlockSpec((CHUNK, GDN_DV * hb), lambda c, h: (c, gate_blk + h)),
                  pl.BlockSpec((CHUNK, LANES), lambda c, h: (c, 0)),
                  _full((1, GDN_DV)), _full((1, LANES)), _full((1, LANES))],
        out_specs=[pl.BlockSpec((CHUNK, GDN_DV * hb), lambda c, h: (c, h)),
                   pl.BlockSpec((1, hb, GDN_DK, GDN_DV), lambda c, h: (c, h, 0, 0)),
                   pl.BlockSpec((1, hb, CHUNK, CHUNK), lambda c, h: (c, h, 0, 0))],
        out_shape=[jax.ShapeDtypeStruct((t, GDN_W), MXU_DTYPE),
                   jax.ShapeDtypeStruct((nc, GDN_HEADS, GDN_DK, GDN_DV), F32),
                   jax.ShapeDtypeStruct((nc, GDN_HEADS, CHUNK, CHUNK), F32)],
        scratch_shapes=[pltpu.VMEM((GDN_HEADS, GDN_DK, GDN_DV), F32)],
        compiler_params=_params(("arbitrary", "arbitrary")),
    )(conv_gdn, proj_main, proj_small, normw, alog, dtb)


def gdn_bwd(dproj_main, conv_gdn, proj_main, proj_small, normw, alog, dtb, hist, t_inv, dy):
    t = conv_gdn.shape[0]
    nc = t // CHUNK
    rev = lambda c: nc - 1 - c
    hb = GDN_HB
    gate_blk = COL_GATE // (GDN_DV * hb)

    def body(alias_ref, qkv_ref, gate_ref, sm_ref, nw_ref, al_ref, db_ref, hist_ref, t_ref, dy_ref,
             dgate_ref, dqkv_ref, dsm_ref, dnw_ref, dal_ref, ddb_ref, dstate_ref):
        del alias_ref
        c, h = pl.program_id(0), pl.program_id(1)
        h0 = _first_head()

        @pl.when(c == 0)
        def _():
            for j in range(hb):
                dstate_ref[h0 + j] = jnp.zeros((GDN_DK, GDN_DV), F32)

        saved = [t_ref[0, j] for j in range(hb)]

        def fn(qs, ks, vs, small, gates, nw, al, db, states):
            return gdn_chunk(h0, qs, ks, vs, small, gates, nw, al, db, states, saved)[:2]

        qs, ks, vs = _gdn_parts(qkv_ref)
        _, vjp = jax.vjp(fn, qs, ks, vs, sm_ref[...], _head_cols(gate_ref), nw_ref[...], al_ref[...], db_ref[...],
                         [hist_ref[0, j] for j in range(hb)])
        dqs, dks, dvs, dsm, dgates, dnw, dal, ddb, dstates = vjp(
            (_head_cols(dy_ref), [dstate_ref[h0 + j] for j in range(hb)]))
        for j in range(hb):
            base = j * GDN_HC
            dqkv_ref[:, base:base + GDN_DK] = dqs[j]
            dqkv_ref[:, base + GDN_DK:base + 2 * GDN_DK] = dks[j]
            dqkv_ref[:, base + 2 * GDN_DK:base + GDN_HC] = dvs[j]
            dgate_ref[:, j * GDN_DV:(j + 1) * GDN_DV] = dgates[j].astype(dgate_ref.dtype)
            dstate_ref[h0 + j] = dstates[j]
        _accumulate(dsm_ref, h == 0, dsm)
        first = jnp.logical_and(c == 0, h == 0)
        _accumulate(dnw_ref, first, dnw)
        _accumulate(dal_ref, first, dal)
        _accumulate(ddb_ref, first, ddb)

    return pl.pallas_call(
        body, name="gdn_bwd", grid=(nc, GDN_HEADS // hb),
        in_specs=[ANY, pl.BlockSpec((CHUNK, GDN_HC * hb), lambda c, h: (rev(c), h)),
                  pl.BlockSpec((CHUNK, GDN_DV * hb), lambda c, h: (rev(c), gate_blk + h)),
                  pl.BlockSpec((CHUNK, LANES), lambda c, h: (rev(c), 0)),
                  _full((1, GDN_DV)), _full((1, LANES)), _full((1, LANES)),
                  pl.BlockSpec((1, hb, GDN_DK, GDN_DV), lambda c, h: (rev(c), h, 0, 0)),
                  pl.BlockSpec((1, hb, CHUNK, CHUNK), lambda c, h: (rev(c), h, 0, 0)),
                  pl.BlockSpec((CHUNK, GDN_DV * hb), lambda c, h: (rev(c), h))],
        out_specs=[pl.BlockSpec((CHUNK, GDN_DV * hb), lambda c, h: (rev(c), gate_blk + h)),
                   pl.BlockSpec((CHUNK, GDN_HC * hb), lambda c, h: (rev(c), h)),
                   pl.BlockSpec((CHUNK, LANES), lambda c, h: (rev(c), 0)),
                   _full((1, GDN_DV)), _full((1, LANES)), _full((1, LANES))],
        out_shape=[jax.ShapeDtypeStruct(dproj_main.shape, dproj_main.dtype), jax.ShapeDtypeStruct((t, GDN_CONV), F32),
                   jax.ShapeDtypeStruct((t, LANES), F32), jax.ShapeDtypeStruct((1, GDN_DV), F32),
                   jax.ShapeDtypeStruct((1, LANES), F32), jax.ShapeDtypeStruct((1, LANES), F32)],
        scratch_shapes=[pltpu.VMEM((GDN_HEADS, GDN_DK, GDN_DV), F32)],
        input_output_aliases={0: 0},
        compiler_params=_params(("arbitrary", "arbitrary")),
    )(dproj_main, conv_gdn, proj_main, proj_small, normw, alog, dtb, hist, t_inv, dy)


def out_proj_loss(x, y_ssd, y_gdn, w_out, final_w, target):
    t = x.shape[0]
    tm = min(256, t)

    def body(x_ref, ys_ref, yg_ref, wo_ref, fw_ref, tg_ref, loss_ref, dhid_ref, dys_ref, dyg_ref, dwo_ref, dfw_ref):
        i = pl.program_id(0)
        ys, yg = ys_ref[...], yg_ref[...]
        wo_s, wo_g = wo_ref[:SSD_WIDTH, :], wo_ref[SSD_WIDTH:, :]
        hid = x_ref[...] + _raw_dot(ys, wo_s, 1, 0) + _raw_dot(yg, wo_g, 1, 0)
        out, vjp = jax.vjp(rmsnorm, hid, fw_ref[...])
        err = out - tg_ref[...]
        loss = 0.5 * jnp.sum(jnp.mean(err * err, axis=-1, keepdims=True), axis=0, keepdims=True)
        dhid, dfw = vjp(err * (1.0 / D_MODEL))
        dhid_ref[...] = dhid
        dys_ref[...] = _raw_dot(dhid, wo_s, 1, 1)
        dyg_ref[...] = _raw_dot(dhid, wo_g, 1, 1)
        first = i == 0
        _accumulate(loss_ref, first, jnp.broadcast_to(loss, loss_ref.shape))
        _accumulate(dfw_ref, first, dfw)

        @pl.when(first)
        def _():
            dwo_ref[:SSD_WIDTH, :] = _raw_dot(ys, dhid, 0, 0)
            dwo_ref[SSD_WIDTH:, :] = _raw_dot(yg, dhid, 0, 0)

        @pl.when(i > 0)
        def _():
            dwo_ref[:SSD_WIDTH, :] += _raw_dot(ys, dhid, 0, 0)
            dwo_ref[SSD_WIDTH:, :] += _raw_dot(yg, dhid, 0, 0)

    row = lambda w: pl.BlockSpec((tm, w), lambda i: (i, 0))
    return pl.pallas_call(
        body, name="out_proj_loss", grid=(t // tm,),
        in_specs=[row(D_MODEL), row(SSD_WIDTH), row(GDN_W), _full((SSD_WIDTH + GDN_W, D_MODEL)), _full((1, D_MODEL)),
                  row(D_MODEL)],
        out_specs=[_full((8, LANES)), row(D_MODEL), row(SSD_WIDTH), row(GDN_W), _full((SSD_WIDTH + GDN_W, D_MODEL)),
                   _full((1, D_MODEL))],
        out_shape=[jax.ShapeDtypeStruct((8, LANES), F32), jax.ShapeDtypeStruct((t, D_MODEL), F32),
                   jax.ShapeDtypeStruct((t, SSD_WIDTH), F32), jax.ShapeDtypeStruct((t, GDN_W), F32),
                   jax.ShapeDtypeStruct((SSD_WIDTH + GDN_W, D_MODEL), F32), jax.ShapeDtypeStruct((1, D_MODEL), F32)],
        compiler_params=_params(("arbitrary",)),
    )(x, y_ssd, y_gdn, w_out, final_w, target)


def in_proj_bwd_x(x, normw, w_main, w_small, dproj_main, dsmall_a, dsmall_b, dhid, slabbed):
    t = x.shape[0]
    tm, tk = min(1024, t), 512
    nk = MAIN // tk
    ni = t // tm
    ns = len(slabbed)

    def body(x_ref, nw_ref, wm_ref, ws_ref, dp_ref, da_ref, db_ref, dh_ref, *rest):
        slab_refs, (gx_ref, dnw_ref), land_refs = rest[:ns], rest[ns:ns + 2], rest[ns + 2:2 * ns + 2]
        acc_ref, sems = rest[2 * ns + 2], rest[2 * ns + 3:]
        i, k = pl.program_id(0), pl.program_id(1)
        start, finish = _slab_exchange(slab_refs, land_refs, ns, *sems)

        @pl.when(jnp.logical_and(i == 0, k == 0))
        def _():
            start()

        part = _raw_dot(dp_ref[...], wm_ref[...], 1, 1)

        @pl.when(k == 0)
        def _():
            acc_ref[...] = part + _raw_dot(da_ref[...] + db_ref[...], ws_ref[...], 1, 1)

        @pl.when(k > 0)
        def _():
            acc_ref[...] += part

        @pl.when(k == nk - 1)
        def _():
            _, vjp = jax.vjp(rmsnorm, x_ref[...], nw_ref[...])
            dx, dnw = vjp(acc_ref[...])
            gx_ref[...] = dx + dh_ref[...]
            _accumulate(dnw_ref, i == 0, dnw)

        @pl.when(jnp.logical_and(i == ni - 1, k == nk - 1))
        def _():
            finish()

    row = lambda w: pl.BlockSpec((tm, w), lambda i, k: (i, 0))
    out = pl.pallas_call(
        body, name="in_proj_bwd_x", grid=(ni, nk),
        in_specs=[row(D_MODEL), _full((1, D_MODEL)), pl.BlockSpec((D_MODEL, tk), lambda i, k: (0, k)),
                  _full((D_MODEL, LANES)), pl.BlockSpec((tm, tk), lambda i, k: (i, k)), row(LANES), row(LANES),
                  row(D_MODEL)] + [HBM] * ns,
        out_specs=[row(D_MODEL), _full((1, D_MODEL))] + [HBM] * ns,
        out_shape=[jax.ShapeDtypeStruct((t, D_MODEL), F32), jax.ShapeDtypeStruct((1, D_MODEL), F32)]
        + _slab_exchange_shapes(slabbed, []),
        scratch_shapes=[pltpu.VMEM((tm, D_MODEL), F32)] + _slab_exchange_sems(ns),
        compiler_params=_params(("arbitrary", "arbitrary")),
    )(x, normw, w_main, w_small, dproj_main, dsmall_a, dsmall_b, dhid, *slabbed)
    return out[0], out[1], out[2:]


def in_proj_bwd_w(u, dproj_main, dsmall_a, dsmall_b):
    t = u.shape[0]
    tm, tn = min(512, t), MAIN // 4

    def body(u_ref, dp_ref, da_ref, db_ref, dwm_ref, dws_ref):
        j, i = pl.program_id(0), pl.program_id(1)
        uu = u_ref[...]
        _accumulate(dwm_ref, i == 0, _raw_dot(uu, dp_ref[...], 0, 0))

        @pl.when(j == 0)
        def _():
            _accumulate(dws_ref, i == 0, _raw_dot(uu, da_ref[...] + db_ref[...], 0, 0))

    return pl.pallas_call(
        body, name="in_proj_bwd_w", grid=(MAIN // tn, t // tm),
        in_specs=[pl.BlockSpec((tm, D_MODEL), lambda j, i: (i, 0)), pl.BlockSpec((tm, tn), lambda j, i: (i, j)),
                  pl.BlockSpec((tm, LANES), lambda j, i: (i, 0)), pl.BlockSpec((tm, LANES), lambda j, i: (i, 0))],
        out_specs=[pl.BlockSpec((D_MODEL, tn), lambda j, i: (0, j)), _full((D_MODEL, LANES))],
        out_shape=[jax.ShapeDtypeStruct((D_MODEL, MAIN), F32), jax.ShapeDtypeStruct((D_MODEL, LANES), F32)],
        compiler_params=_params(("arbitrary", "arbitrary")),
    )(u, dproj_main, dsmall_a, dsmall_b)


def sum_slabs(a, name):
    n, rows, cols = a.shape
    tr = 64 if rows % 64 == 0 else rows

    def body(a_ref, o_ref):
        acc = a_ref[0].astype(F32)
        for d in range(1, n):
            acc = acc + a_ref[d].astype(F32)
        o_ref[...] = acc

    return pl.pallas_call(
        body, name=name, grid=(rows // tr,),
        in_specs=[pl.BlockSpec((n, tr, cols), lambda i: (0, i, 0))],
        out_specs=pl.BlockSpec((tr, cols), lambda i: (i, 0)),
        out_shape=jax.ShapeDtypeStruct((rows, cols), F32),
        compiler_params=_params(("arbitrary",)),
    )(a)


def adamw(w, g, m, v, name):
    rows, cols = w.shape
    tr = 128 if rows % 128 == 0 else rows

    def body(w_ref, g_ref, m_ref, v_ref, d_ref, nm_ref, nv_ref):
        gg = g_ref[...]
        nm = ADAM_B1 * m_ref[...] + (1.0 - ADAM_B1) * gg
        nv = ADAM_B2 * v_ref[...] + (1.0 - ADAM_B2) * (gg * gg)
        m_hat = nm / (1.0 - ADAM_B1 ** ADAM_STEP)
        v_hat = nv / (1.0 - ADAM_B2 ** ADAM_STEP)
        d_ref[...] = -ADAM_LR * (m_hat / (jnp.sqrt(v_hat) + ADAM_EPS) + ADAM_WD * w_ref[...])
        nm_ref[...] = nm
        nv_ref[...] = nv

    spec = pl.BlockSpec((tr, cols), lambda i: (i, 0))
    shp = jax.ShapeDtypeStruct((rows, cols), F32)
    return pl.pallas_call(
        body, name=name, grid=(rows // tr,), in_specs=[spec] * 4, out_specs=[spec] * 3, out_shape=[shp] * 3,
        compiler_params=_params(("arbitrary",)),
    )(w, g, m, v)


def _my_place():
    return lax.axis_index("x"), lax.axis_index("y"), lax.axis_index("c")


def gather_weights(big, small):
    nb, n = len(big), len(big) + len(small)
    parts = 4

    def body(*refs):
        srcs, outs = refs[:n], refs[n:2 * n]
        land_a, land_b = refs[2 * n:2 * n + nb], refs[2 * n + nb:2 * n + 2 * nb]
        send_sems, recv_sems, fwd_send, fwd_recv, local_sems = refs[2 * n + 2 * nb:]
        x, y, c = _my_place()
        me = 2 * x + y
        chips = [(1 - x, y), (x, 1 - y), (1 - x, 1 - y)]
        half = [a.shape[0] // 2 for a in big]

        def ici(j, i):
            px, py = chips[j]
            if i < nb:
                src, dst = srcs[i].at[pl.ds(c * half[i], half[i])], land_a[i].at[j]
            else:
                src, dst = srcs[i], outs[i].at[me]
            return pltpu.make_async_remote_copy(src_ref=src, dst_ref=dst, send_sem=send_sems.at[j * n + i],
                                                recv_sem=recv_sems.at[j * n + i], device_id=(px, py, c),
                                                device_id_type=MESH)

        def ici_arrival(j, i):
            px, py = chips[j]
            dst = land_a[i].at[j] if i < nb else outs[i].at[2 * px + py]
            return pltpu.make_async_remote_copy(src_ref=dst, dst_ref=dst, send_sem=send_sems.at[j * n + i],
                                                recv_sem=recv_sems.at[j * n + i], device_id=(px, py, c),
                                                device_id_type=MESH)

        def forward(j, i, p):
            rows = half[i] // parts
            k = (j * nb + i) * parts + p
            return pltpu.make_async_remote_copy(
                src_ref=land_a[i].at[j, pl.ds(p * rows, rows)], dst_ref=land_b[i].at[j, pl.ds(p * rows, rows)],
                send_sem=fwd_send.at[k], recv_sem=fwd_recv.at[k], device_id=(x, y, 1 - c), device_id_type=MESH)

        def store(j, i, from_sibling):
            px, py = chips[j]
            buf, h = (land_b, 1 - c) if from_sibling else (land_a, c)
            k = n + (j * nb + i) * 2 + (1 if from_sibling else 0)
            return pltpu.make_async_copy(buf[i].at[j], outs[i].at[2 * px + py, pl.ds(h * half[i], half[i])],
                                         local_sems.at[k])

        own = [pltpu.make_async_copy(srcs[i], outs[i].at[me], local_sems.at[i]) for i in range(n)]
        sends = [ici(j, i) for j in range(3) for i in range(n)]
        for cp in own + sends:
            cp.start()
        pending = []
        for j in range(3):
            for i in range(n):
                ici_arrival(j, i).wait_recv()
                if i < nb:
                    fw = [forward(j, i, p) for p in range(parts)]
                    st = store(j, i, False)
                    for cp in fw + [st]:
                        cp.start()
                    pending += [cp.wait_send for cp in fw] + [st.wait]
        for j in range(3):
            for i in range(nb):
                for p in range(parts):
                    forward(j, i, p).wait_recv()
                st = store(j, i, True)
                st.start()
                pending.append(st.wait)
        for cp in sends:
            cp.wait_send()
        for wait in pending:
            wait()
        for cp in own:
            cp.wait()

    shards = list(big) + list(small)
    lands = [pltpu.VMEM((3, a.shape[0] // 2) + a.shape[1:], a.dtype) for a in big]
    return pl.pallas_call(
        body, name="gather_weights",
        in_specs=[HBM] * n, out_specs=[HBM] * n,
        out_shape=[jax.ShapeDtypeStruct((N_CHIP,) + s.shape, s.dtype) for s in shards],
        scratch_shapes=lands + lands + [
            pltpu.SemaphoreType.DMA((3 * n,)), pltpu.SemaphoreType.DMA((3 * n,)),
            pltpu.SemaphoreType.DMA((3 * nb * parts,)), pltpu.SemaphoreType.DMA((3 * nb * parts,)),
            pltpu.SemaphoreType.DMA((n + 6 * nb,))],
        compiler_params=pltpu.CompilerParams(vmem_limit_bytes=VMEM_LIMIT),
    )(*shards)


def _peer(x, y, c, mask):
    mx, my, mc = (mask >> 2) & 1, (mask >> 1) & 1, mask & 1
    return (x ^ mx if mx else x, y ^ my if my else y, c ^ mc if mc else c)


def _slab_exchange_shapes(slabbed, replicated):
    return ([jax.ShapeDtypeStruct(a.shape, a.dtype) for a in slabbed]
            + [jax.ShapeDtypeStruct((N_DEV,) + a.shape, a.dtype) for a in replicated])


def _slab_exchange_sems(n):
    return [pltpu.SemaphoreType.DMA((7 * n,)), pltpu.SemaphoreType.DMA((7 * n,)), pltpu.SemaphoreType.DMA((n,))]


def _slab_exchange(srcs, outs, ns, send_sems, recv_sems, local_sems):
    n = len(srcs)
    x, y, c = _my_place()
    me = 4 * x + 2 * y + c

    def piece(i, dev):
        return srcs[i].at[dev] if i < ns else srcs[i]

    def copies(arriving):
        out = []
        for mask in range(1, N_DEV):
            px, py, pc = _peer(x, y, c, mask)
            dev = 4 * px + 2 * py + pc
            for i in range(n):
                k = (mask - 1) * n + i
                out.append(pltpu.make_async_remote_copy(
                    src_ref=piece(i, dev), dst_ref=outs[i].at[dev if arriving else me], send_sem=send_sems.at[k],
                    recv_sem=recv_sems.at[k], device_id=(px, py, pc), device_id_type=MESH))
        return out

    def local():
        return [pltpu.make_async_copy(piece(i, me), outs[i].at[me], local_sems.at[i]) for i in range(n)]

    def start():
        for cp in local() + copies(False):
            cp.start()

    def finish():
        for cp in copies(True):
            cp.wait_recv()
        for cp in copies(False):
            cp.wait_send()
        for cp in local():
            cp.wait()

    return start, finish


def exchange_halves(halves, replicated):
    n, nr = len(halves), len(replicated)
    streams = 8

    def body(*refs):
        srcs, rep_srcs, outs, rep_outs = refs[:n], refs[n:n + nr], refs[n + nr:2 * n + nr], refs[2 * n + nr:2 * (n + nr)]
        refs = refs[2 * (n + nr):]
        mine, theirs = refs[:n], refs[n:2 * n]
        send_sems, recv_sems, in_sems, out_sems = refs[2 * n:2 * n + 4]
        rep_start, rep_finish = _slab_exchange(rep_srcs, rep_outs, 0, *refs[2 * n + 4:])
        rep_start()
        x, y, c = _my_place()
        loads = [pltpu.make_async_copy(srcs[i], mine[i], in_sems.at[i]) for i in range(n)]
        for cp in loads:
            cp.start()
        for cp in loads:
            cp.wait()

        def chunk_copy(i, s):
            rows = halves[i].shape[0] // streams
            k = i * streams + s
            return pltpu.make_async_remote_copy(
                src_ref=mine[i].at[pl.ds(s * rows, rows)], dst_ref=theirs[i].at[pl.ds(s * rows, rows)],
                send_sem=send_sems.at[k], recv_sem=recv_sems.at[k], device_id=(x, y, 1 - c), device_id_type=MESH)

        sends = [chunk_copy(i, s) for i in range(n) for s in range(streams)]
        for cp in sends:
            cp.start()
        own = [pltpu.make_async_copy(mine[i], outs[i].at[c], out_sems.at[i]) for i in range(n)]
        for cp in own:
            cp.start()
        for cp in sends:
            cp.wait_recv()
        got = [pltpu.make_async_copy(theirs[i], outs[i].at[1 - c], out_sems.at[n + i]) for i in range(n)]
        for cp in got:
            cp.start()
        for cp in sends:
            cp.wait_send()
        for cp in own + got:
            cp.wait()
        rep_finish()

    vmem = [pltpu.VMEM(a.shape, a.dtype) for a in halves]
    out = pl.pallas_call(
        body, name="exchange_halves",
        in_specs=[HBM] * (n + nr), out_specs=[HBM] * (n + nr),
        out_shape=[jax.ShapeDtypeStruct((2,) + a.shape, a.dtype) for a in halves]
        + _slab_exchange_shapes([], replicated),
        scratch_shapes=vmem + vmem + [pltpu.SemaphoreType.DMA((n * streams,)), pltpu.SemaphoreType.DMA((n * streams,)),
                                      pltpu.SemaphoreType.DMA((n,)), pltpu.SemaphoreType.DMA((2 * n,))]
        + _slab_exchange_sems(nr),
        compiler_params=pltpu.CompilerParams(vmem_limit_bytes=VMEM_LIMIT),
    )(*halves, *replicated)
    return out[:n], out[n:]


def _pack_cols(pieces):
    offs, pos = [], 0
    for a in pieces:
        offs.append(pos)
        pos += a.shape[1]
    rows8 = [jnp.pad(a.astype(F32), ((0, 8 - a.shape[0]), (0, 0))) for a in pieces]
    return jnp.concatenate(rows8, axis=1), offs


def adamw_many(ws, gs, ms, vs):
    n = len(ws)

    def body(*refs):
        w_r, g_r, m_r, v_r = refs[:n], refs[n:2 * n], refs[2 * n:3 * n], refs[3 * n:4 * n]
        d_o, m_o, v_o = refs[4 * n:5 * n], refs[5 * n:6 * n], refs[6 * n:7 * n]
        for i in range(n):
            gg = g_r[i][...]
            nm = ADAM_B1 * m_r[i][...] + (1.0 - ADAM_B1) * gg
            nv = ADAM_B2 * v_r[i][...] + (1.0 - ADAM_B2) * (gg * gg)
            m_hat = nm / (1.0 - ADAM_B1 ** ADAM_STEP)
            v_hat = nv / (1.0 - ADAM_B2 ** ADAM_STEP)
            d_o[i][...] = -ADAM_LR * (m_hat / (jnp.sqrt(v_hat) + ADAM_EPS) + ADAM_WD * w_r[i][...])
            m_o[i][...] = nm
            v_o[i][...] = nv

    shapes = [jax.ShapeDtypeStruct(w.shape, F32) for w in ws]
    out = pl.pallas_call(body, name="adamw_small", out_shape=shapes * 3,
                         compiler_params=pltpu.CompilerParams(vmem_limit_bytes=VMEM_LIMIT))(*ws, *gs, *ms, *vs)
    return out[:n], out[n:2 * n], out[2 * n:]


def _lanes(vec, start):
    n = vec.shape[-1]
    return jnp.pad(vec.reshape(1, n).astype(F32), ((0, 0), (start, LANES - start - n)))


def kernel(x, norm_w, w_in, ssd_conv_w, ssd_conv_b, ssd_dt_bias, ssd_a_log, ssd_d, ssd_norm_w, gdn_conv_w, gdn_dt_bias, gdn_a_log, gdn_norm_w, w_out, final_norm_w, loss_target, m_norm_w, m_w_in, m_ssd_conv_w, m_ssd_conv_b, m_ssd_dt_bias, m_ssd_a_log, m_ssd_d, m_ssd_norm_w, m_gdn_conv_w, m_gdn_dt_bias, m_gdn_a_log, m_gdn_norm_w, m_w_out, m_final_norm_w, v_norm_w, v_w_in, v_ssd_conv_w, v_ssd_conv_b, v_ssd_dt_bias, v_ssd_a_log, v_ssd_d, v_ssd_norm_w, v_gdn_conv_w, v_gdn_dt_bias, v_gdn_a_log, v_gdn_norm_w, v_w_out, v_final_norm_w):
    xs = x[0]
    target = loss_target[0]
    chip = 2 * lax.axis_index("x") + lax.axis_index("y")
    w_in_shard, w_out_shard = w_in[0], w_out[0]
    in_cols = w_in_shard.shape[1]
    out_rows = w_out_shard.shape[0]

    g_in, g_out, g_cs, g_cg = gather_weights(
        [w_in_shard.astype(MXU_DTYPE), w_out_shard.astype(MXU_DTYPE)], [ssd_conv_w[0], gdn_conv_w[0]])
    w_in_full = jnp.transpose(g_in, (1, 0, 2)).reshape(D_MODEL, IN_DIM)
    w_out_full = g_out.reshape(N_CHIP * out_rows, D_MODEL)
    cw_ssd = _ssd_perm(jnp.transpose(g_cs, (1, 0, 2)).reshape(4, SSD_CONV))
    cw_gdn = _gdn_perm(jnp.transpose(g_cg, (1, 0, 2)).reshape(4, GDN_CONV))
    cb_ssd = _ssd_perm(ssd_conv_b)
    cb_gdn = jnp.zeros((1, GDN_CONV), F32)
    o_xbc, o_dt, o_gate, o_qkv, o_ab = 1024, 2560, 2576, 3600, 6672
    w_main = jnp.concatenate([w_in_full[:, :o_xbc], w_in_full[:, o_gate:o_qkv], _ssd_perm(w_in_full[:, o_xbc:o_dt]),
                              _gdn_perm(w_in_full[:, o_qkv:o_ab])], axis=1)
    w_small = jnp.concatenate([w_in_full[:, o_dt:o_gate], w_in_full[:, o_ab:],
                               jnp.zeros((D_MODEL, LANES - 32), MXU_DTYPE)], axis=1)
    alog = _lanes(ssd_a_log, 0) + _lanes(gdn_a_log, LANE_GA)
    dtb = _lanes(ssd_dt_bias, 0) + _lanes(gdn_dt_bias, LANE_GA)
    dvec = _lanes(ssd_d, 0)
    fw = final_norm_w.reshape(1, D_MODEL)

    proj_main, proj_small, u = in_proj(xs, norm_w, w_main, w_small)
    conv_ssd = conv_fwd(proj_main, COL_SSD, SSD_CONV, cw_ssd, cb_ssd, "conv_fwd_ssd")
    conv_gdn = conv_fwd(proj_main, COL_GDN, GDN_CONV, cw_gdn, cb_gdn, "conv_fwd_gdn")
    y_ssd, hist_ssd = ssd_fwd(conv_ssd, proj_main, proj_small, ssd_norm_w, alog, dtb, dvec)
    y_gdn, hist_gdn, tinv_gdn = gdn_fwd(conv_gdn, proj_main, proj_small, gdn_norm_w, alog, dtb)

    loss_blk, dhid, dy_ssd, dy_gdn, d_w_out, d_fw = out_proj_loss(xs, y_ssd, y_gdn, w_out_full, fw, target)
    dconv_ssd, dproj_main, dsmall_ssd, d_ssd_nw, d_alog_s, d_dtb_s, d_dvec = ssd_bwd(
        conv_ssd, proj_main, proj_small, ssd_norm_w, alog, dtb, dvec, hist_ssd, dy_ssd)
    dproj_main, dconv_gdn, dsmall_gdn, d_gdn_nw, d_alog_g, d_dtb_g = gdn_bwd(
        dproj_main, conv_gdn, proj_main, proj_small, gdn_norm_w, alog, dtb, hist_gdn, tinv_gdn, dy_gdn)
    dproj_main, dwb_ssd = conv_bwd(dproj_main, proj_main, COL_SSD, SSD_CONV, cw_ssd, cb_ssd, dconv_ssd, "conv_bwd_ssd")
    dproj_main, dwb_gdn = conv_bwd(dproj_main, proj_main, COL_GDN, GDN_CONV, cw_gdn, cb_gdn, dconv_gdn, "conv_bwd_gdn")
    d_w_main, d_w_small = in_proj_bwd_w(u, dproj_main, dsmall_ssd, dsmall_gdn)

    d_w_in = jnp.concatenate([d_w_main[:, :COL_GATE], _ssd_unperm(d_w_main[:, COL_SSD:COL_GDN]), d_w_small[:, 0:16],
                              d_w_main[:, COL_GATE:COL_SSD], _gdn_unperm(d_w_main[:, COL_GDN:]), d_w_small[:, 16:32]],
                             axis=1)
    d_w_in = jnp.transpose(d_w_in.reshape(D_MODEL, N_CHIP, in_cols), (1, 0, 2))
    slabs = [d_w_in.reshape(N_DEV, D_MODEL // 2, in_cols).astype(COMM_DTYPE),
             d_w_out.reshape(N_DEV, out_rows // 2, D_MODEL).astype(COMM_DTYPE)]

    grad_x, d_norm_w, (r_in, r_out) = in_proj_bwd_x(xs, norm_w, w_main, w_small, dproj_main, dsmall_ssd, dsmall_gdn,
                                                     dhid, slabs)
    d_alog, d_dtb = d_alog_s + d_alog_g, d_dtb_s + d_dtb_g
    packed, (o_nw, o_cs, o_cg, o_snw, o_fw, o_al, o_db, o_dv, o_gnw, o_loss) = _pack_cols([
        d_norm_w, _ssd_unperm(dwb_ssd), _gdn_unperm(dwb_gdn),
        d_ssd_nw.reshape(1, SSD_WIDTH), d_fw, d_alog, d_dtb, d_dvec, d_gdn_nw, loss_blk])

    half_in = sum_slabs(r_in, "sum_w_in")
    half_out = sum_slabs(r_out, "sum_w_out")
    (full_in, full_out), (r_small,) = exchange_halves([half_in, half_out], [packed])
    tot = sum_slabs(r_small, "sum_small")
    grad_w_in = full_in.reshape(D_MODEL, in_cols)
    grad_w_out = full_out.reshape(out_rows, D_MODEL)
    loss = tot[0, o_loss]
    sc, gc = ssd_conv_w.shape[2], gdn_conv_w.shape[2]
    row = lambda off, n, r=0: tot[r:r + 1, off:off + n]
    gs = [row(o_nw, D_MODEL),
          lax.dynamic_slice(tot, (0, o_cs + chip * sc), (4, sc)),
          row(o_cs, SSD_CONV, 4),
          row(o_db, SSD_HEADS), row(o_al, SSD_HEADS), row(o_dv, SSD_HEADS),
          row(o_snw, SSD_WIDTH),
          lax.dynamic_slice(tot, (0, o_cg + chip * gc), (4, gc)),
          row(o_db + LANE_GA, GDN_HEADS), row(o_al + LANE_GA, GDN_HEADS),
          row(o_gnw, GDN_DV), row(o_fw, D_MODEL)]

    names = ["norm_w", "ssd_conv_w", "ssd_conv_b", "ssd_dt_bias", "ssd_a_log", "ssd_d", "ssd_norm_w", "gdn_conv_w",
             "gdn_dt_bias", "gdn_a_log", "gdn_norm_w", "final_norm_w"]
    ws = [norm_w, ssd_conv_w, ssd_conv_b, ssd_dt_bias, ssd_a_log, ssd_d, ssd_norm_w, gdn_conv_w, gdn_dt_bias,
          gdn_a_log, gdn_norm_w, final_norm_w]
    ms = [m_norm_w, m_ssd_conv_w, m_ssd_conv_b, m_ssd_dt_bias, m_ssd_a_log, m_ssd_d, m_ssd_norm_w, m_gdn_conv_w,
          m_gdn_dt_bias, m_gdn_a_log, m_gdn_norm_w, m_final_norm_w]
    vs = [v_norm_w, v_ssd_conv_w, v_ssd_conv_b, v_ssd_dt_bias, v_ssd_a_log, v_ssd_d, v_ssd_norm_w, v_gdn_conv_w,
          v_gdn_dt_bias, v_gdn_a_log, v_gdn_norm_w, v_final_norm_w]
    shapes = [w.shape for w in ws]
    flat = lambda arrs: [a.reshape(g.shape) for a, g in zip(arrs, gs)]
    d_s, m_s, v_s = adamw_many(flat(ws), gs, flat(ms), flat(vs))
    back = lambda arrs: dict(zip(names, [a.reshape(s) for a, s in zip(arrs, shapes)]))
    delta, new_m, new_v, grads = back(d_s), back(m_s), back(v_s), back(gs)
    d_in, m_in, v_in = adamw(w_in_shard, grad_w_in, m_w_in[0], v_w_in[0], "adamw_w_in")
    d_out, m_out, v_out = adamw(w_out_shard, grad_w_out, m_w_out[0], v_w_out[0], "adamw_w_out")
    for tbl, a_in, a_out in ((grads, grad_w_in, grad_w_out), (delta, d_in, d_out), (new_m, m_in, m_out),
                             (new_v, v_in, v_out)):
        tbl["w_in"] = a_in[None]
        tbl["w_out"] = a_out[None]

    order = ["norm_w", "w_in", "ssd_conv_w", "ssd_conv_b", "ssd_dt_bias", "ssd_a_log", "ssd_d", "ssd_norm_w",
             "gdn_conv_w", "gdn_dt_bias", "gdn_a_log", "gdn_norm_w", "w_out", "final_norm_w"]
    return (loss.reshape(()), grad_x[None], *[grads[k] for k in order], *[delta[k] for k in order],
            *[new_m[k] for k in order], *[new_v[k] for k in order])
```

```python
import functools

import jax
import jax.numpy as jnp
from jax import lax
from jax.experimental import pallas as pl
from jax.experimental.pallas import tpu as pltpu

F32 = jnp.float32
MXU_DTYPE = jnp.bfloat16
COMM_DTYPE = jnp.bfloat16
MESH = pl.DeviceIdType.MESH

D_MODEL = 1024
CHUNK = 64
EPS = 1e-6
SSD_HEADS, SSD_GROUPS, SSD_STATE = 16, 2, 128
SSD_WIDTH, SSD_CONV = 1024, 1536
SSD_GW = SSD_WIDTH // SSD_GROUPS
SSD_GC = SSD_GW + 2 * SSD_STATE
GDN_HEADS, GDN_DK, GDN_DV = 8, 128, 128
GDN_W, GDN_CONV = 1024, 3072
GDN_HC = 2 * GDN_DK + GDN_DV
IN_DIM = 6688
MAIN = 6656
LANES = 128
COL_Z, COL_GATE, COL_SSD, COL_GDN = 0, 1024, 2048, 3584
GDN_HB = 8
LANE_GA, LANE_GB = 16, 24
N_DEV, N_CHIP = 8, 4
VMEM_LIMIT = 52 * 1024 * 1024

ADAM_LR, ADAM_B1, ADAM_B2, ADAM_EPS, ADAM_WD, ADAM_STEP = 0.001, 0.9, 0.999, 1e-08, 0.01, 10


def _ssd_perm(a):
    lead, nb = a.shape[:-1], SSD_GROUPS * SSD_STATE
    x = a[..., :SSD_WIDTH].reshape(*lead, SSD_GROUPS, SSD_GW)
    b = a[..., SSD_WIDTH:SSD_WIDTH + nb].reshape(*lead, SSD_GROUPS, SSD_STATE)
    c = a[..., SSD_WIDTH + nb:].reshape(*lead, SSD_GROUPS, SSD_STATE)
    return jnp.concatenate([x, b, c], axis=-1).reshape(*lead, SSD_CONV)


def _ssd_unperm(a):
    lead = a.shape[:-1]
    g = a.reshape(*lead, SSD_GROUPS, SSD_GC)
    parts = [g[..., :SSD_GW], g[..., SSD_GW:SSD_GW + SSD_STATE], g[..., SSD_GW + SSD_STATE:]]
    return jnp.concatenate([p.reshape(*lead, -1) for p in parts], axis=-1)


def _gdn_perm(a):
    lead = a.shape[:-1]
    return jnp.swapaxes(a.reshape(*lead, 3, GDN_HEADS, GDN_DK), -3, -2).reshape(*lead, GDN_CONV)


def _gdn_unperm(a):
    lead = a.shape[:-1]
    return jnp.swapaxes(a.reshape(*lead, GDN_HEADS, 3, GDN_DK), -3, -2).reshape(*lead, GDN_CONV)


def _split(a, n):
    parts, rest = [], a.astype(F32)
    for i in range(n):
        p = rest.astype(MXU_DTYPE)
        parts.append(p)
        if i < n - 1:
            rest = rest - p.astype(F32)
    return parts


def _raw_dot(a, b, ca, cb, mode="bf16"):
    d = lambda u, v: lax.dot_general(u, v, (((ca,), (cb,)), ((), ())), preferred_element_type=F32)
    if mode == "bf16":
        return d(a.astype(MXU_DTYPE), b.astype(MXU_DTYPE))
    if mode == "x3":
        (ah, al), (bh, bl) = _split(a, 2), _split(b, 2)
        return d(ah, bh) + (d(ah, bl) + d(al, bh))
    if mode == "sel_a":
        a0 = a.astype(MXU_DTYPE)
        b1, b2, b3 = _split(b, 3)
        return d(a0, b1) + (d(a0, b2) + d(a0, b3))
    assert mode == "sel_b", mode
    b0 = b.astype(MXU_DTYPE)
    a1, a2, a3 = _split(a, 3)
    return d(a1, b0) + (d(a2, b0) + d(a3, b0))


@functools.partial(jax.custom_vjp, nondiff_argnums=(2,))
def mm_nn(a, b, mode="bf16"):
    return _raw_dot(a, b, 1, 0, mode)


@functools.partial(jax.custom_vjp, nondiff_argnums=(2,))
def mm_nt(a, b, mode="bf16"):
    return _raw_dot(a, b, 1, 1, mode)


@functools.partial(jax.custom_vjp, nondiff_argnums=(2,))
def mm_tn(a, b, mode="bf16"):
    return _raw_dot(a, b, 0, 0, mode)


_SAME = {"bf16": ("bf16", "bf16"), "x3": ("x3", "x3")}
_NN_BWD = dict(_SAME, sel_a=("bf16", "sel_a"), sel_b=("sel_b", "bf16"))
_NT_BWD = dict(_SAME, sel_a=("bf16", "sel_b"), sel_b=("sel_b", "bf16"))
_TN_BWD = dict(_SAME, sel_a=("bf16", "sel_a"), sel_b=("sel_a", "bf16"))
mm_nn.defvjp(lambda a, b, m: (_raw_dot(a, b, 1, 0, m), (a, b)),
             lambda m, r, g: (mm_nt(g, r[1], _NN_BWD[m][0]), mm_tn(r[0], g, _NN_BWD[m][1])))
mm_nt.defvjp(lambda a, b, m: (_raw_dot(a, b, 1, 1, m), (a, b)),
             lambda m, r, g: (mm_nn(g, r[1], _NT_BWD[m][0]), mm_tn(g, r[0], _NT_BWD[m][1])))
mm_tn.defvjp(lambda a, b, m: (_raw_dot(a, b, 0, 0, m), (a, b)),
             lambda m, r, g: (mm_nt(r[1], g, _TN_BWD[m][0]), mm_nn(r[0], g, _TN_BWD[m][1])))


@jax.custom_jvp
def sigmoid(x):
    return 1.0 / (1.0 + jnp.exp(-x))


@sigmoid.defjvp
def _sigmoid_jvp(p, t):
    s = sigmoid(p[0])
    return s, t[0] * s * (1.0 - s)


@jax.custom_jvp
def softplus(x):
    return jnp.maximum(x, 0.0) + jnp.log(1.0 + jnp.exp(-jnp.abs(x)))


@softplus.defjvp
def _softplus_jvp(p, t):
    return softplus(p[0]), t[0] * sigmoid(p[0])


def silu(x):
    return x * sigmoid(x)


def rmsnorm(x, w):
    return x * lax.rsqrt(jnp.mean(x * x, axis=-1, keepdims=True) + EPS) * w


def _iota(shape, dim):
    return lax.broadcasted_iota(jnp.int32, shape, dim)


def _tri_inv_impl(mats):
    n = mats[0].shape[0]
    r, c = _iota((n, n), 0), _iota((n, n), 1)
    eye = jnp.where(r == c, 1.0, 0.0).astype(F32)
    blockdiag = (r >> 4) == (c >> 4)
    dot = lambda u, v: _raw_dot(u, v, 1, 0, "x3")
    each = lambda f, *ls: [f(*xs) for xs in zip(*ls)]
    dg = each(lambda a: jnp.where(blockdiag, a, 0.0), mats)
    off = each(lambda a, d: a - d, mats, dg)
    m = each(lambda d: -d, dg)
    p = each(lambda x: eye + x, m)
    pw = m
    for _ in range(3):
        pw = each(lambda x: dot(x, x), pw)
        p = each(lambda x, y: x + dot(x, y), p, pw)
    e = each(dot, p, off)
    e2 = each(lambda x: dot(x, x), e)
    q = each(lambda x: eye - x, e)
    q = each(lambda x, y: x + dot(x, y), q, e2)
    return each(dot, q, p)


def _tri_inv_bwd(ts, gs):
    x = [mm_nt(g, t, "x3") for g, t in zip(gs, ts)]
    return [-mm_tn(t, y, "x3") for t, y in zip(ts, x)]


@jax.custom_vjp
def tri_inv(mats):
    return _tri_inv_impl(mats)


def _tri_inv_fwd(mats):
    ts = _tri_inv_impl(mats)
    return ts, ts


tri_inv.defvjp(_tri_inv_fwd, lambda ts, gs: (_tri_inv_bwd(ts, gs),))


@jax.custom_vjp
def tri_inv_saved(mats, ts):
    del mats
    return ts


tri_inv_saved.defvjp(lambda mats, ts: (ts, ts),
                     lambda ts, gs: (_tri_inv_bwd(ts, gs), [jnp.zeros_like(t) for t in ts]))


def _chunk_masks():
    r, c = _iota((CHUNK, CHUNK), 0), _iota((CHUNK, CHUNK), 1)
    return r >= c, r > c, r == c


def _log_decay_cumsum(small, alog, dtb, incl):
    sp = softplus(small + dtb)
    la = -jnp.exp(alog) * sp
    tri = jnp.where(incl, 1.0, 0.0).astype(F32)
    return sp, mm_nn(tri, la, "sel_a")


def _col_of(x, lane_mask):
    return jnp.sum(jnp.where(lane_mask, x, 0.0), axis=1, keepdims=True)


def _decay_matrix(col, incl, eye):
    row = jnp.sum(jnp.where(eye, col, 0.0), axis=0, keepdims=True)
    return jnp.where(incl, jnp.exp(jnp.where(incl, col - row, 0.0)), 0.0)


def gdn_chunk(h0, qs, ks, vs, small, gates, normw, alog, dtb, states, saved_t=None):
    incl, strict, eye = _chunk_masks()
    lane = _iota((1, LANES), 1)
    last = _iota((CHUNK, 1), 0) == CHUNK - 1
    _, lac = _log_decay_cumsum(small, alog, dtb, incl)
    heads = range(len(qs))
    each = lambda f, *ls: [f(*xs) for xs in zip(*ls)]
    gc = [_col_of(lac, lane == LANE_GA + h0 + j) for j in heads]
    beta = [sigmoid(_col_of(small, lane == LANE_GB + h0 + j)) for j in heads]
    decay = each(lambda x: _decay_matrix(x, incl, eye), gc)
    gl = each(lambda x: jnp.sum(jnp.where(last, x, 0.0), axis=0, keepdims=True), gc)
    q = each(lambda x: x * lax.rsqrt(jnp.sum(x * x, axis=-1, keepdims=True) + EPS) * (GDN_DK ** -0.5), qs)
    k = each(lambda x: x * lax.rsqrt(jnp.sum(x * x, axis=-1, keepdims=True) + EPS), ks)
    kb = each(lambda x, b: x * b, k, beta)
    a = each(lambda x, y, d: jnp.where(strict, mm_nt(x, y) * d, 0.0), kb, k, decay)
    t = tri_inv(a) if saved_t is None else tri_inv_saved(a, saved_t)
    eg = each(jnp.exp, gc)
    u = each(lambda x, v, b: mm_nn(x, v * b), t, vs, beta)
    w = each(lambda x, y, e: mm_nn(x, y * e), t, kb, eg)
    attn = each(lambda x, y, d: mm_nt(x, y) * d, q, k, decay)
    v_new = each(lambda x, y, s: x - mm_nn(y, s), u, w, states)
    o = each(lambda x, e, s, at, vn: mm_nn(x * e, s) + mm_nn(at, vn), q, eg, states, attn, v_new)
    new_states = each(lambda s, x, y, l, c: s * jnp.exp(l) + mm_tn(y * jnp.exp(l - c), x), states, v_new, k, gl, gc)
    ys = each(lambda x, gt: rmsnorm(x, normw) * silu(gt), o, gates)
    return ys, new_states, t


@jax.custom_vjp
def split_lanes(x):
    return [x[:, i * LANES:(i + 1) * LANES] for i in range(x.shape[1] // LANES)]


@jax.custom_vjp
def join_lanes(xs):
    return jnp.concatenate(xs, axis=1)


split_lanes.defvjp(lambda x: (split_lanes(x), None), lambda _, gs: (join_lanes(gs),))
join_lanes.defvjp(lambda xs: (join_lanes(xs), None), lambda _, g: (split_lanes(g),))


def ssd_chunk(xs, bm, cm, z, small, normw, alog, dtb, dvec, state):
    incl, _, eye = _chunk_masks()
    lane = _iota((1, LANES), 1)
    last = _iota((CHUNK, 1), 0) == CHUNK - 1
    hpg = SSD_HEADS // SSD_GROUPS
    groups = range(len(xs))
    each = lambda f, *ls: [f(*a) for a in zip(*ls)]
    sp, lac = _log_decay_cumsum(small, alog, dtb, incl)
    lac_last = jnp.sum(jnp.where(last, lac, 0.0), axis=0, keepdims=True)
    sel = [jnp.where(_iota((LANES, SSD_GW), 0) == g * hpg + (_iota((LANES, SSD_GW), 1) >> 6), 1.0, 0.0).astype(F32)
           for g in groups]
    expand = lambda v: [mm_nn(v, s, "sel_b") for s in sel]
    dt_e, elac_e, toend_e = expand(sp), expand(jnp.exp(lac)), expand(jnp.exp(lac_last - lac))
    row8, row8e = _iota((8, LANES), 0), _iota((8, SSD_GW), 0)
    two_e = expand(jnp.where(row8 == 0, dvec, 0.0) + jnp.where(row8 == 1, jnp.exp(lac_last), 0.0))
    d_e = each(lambda v: jnp.sum(jnp.where(row8e == 0, v, 0.0), axis=0, keepdims=True), two_e)
    chunk_e = each(lambda v: jnp.sum(jnp.where(row8e == 1, v, 0.0), axis=0, keepdims=True), two_e)
    xdt = each(lambda a, b: a * b, xs, dt_e)
    cb = each(mm_nt, cm, bm)
    y = each(lambda c_, st, el, x_, d_: mm_nn(c_, st) * el + x_ * d_, cm, state, elac_e, xs, d_e)
    x_pairs = each(split_lanes, xdt)
    half = _iota((1, LANES), 1) >> 6
    lms = [[_decay_matrix(_col_of(lac, lane == g * hpg + j), incl, eye) for j in range(hpg)] for g in groups]
    terms = [[mm_nn(cb[g] * lms[g][j], jnp.where(half == j % 2, x_pairs[g][j // 2], 0.0)) for j in range(hpg)]
             for g in groups]
    y = [y[g] + join_lanes([terms[g][2 * p] + terms[g][2 * p + 1] for p in range(hpg // 2)]) for g in groups]
    new_state = each(lambda st, ce, b_, xd, te: st * ce + mm_tn(b_, xd * te), state, chunk_e, bm, xdt, toend_e)
    out = each(lambda y_, z_, nw: rmsnorm(y_ * silu(z_), nw), y, z, normw)
    return out, new_state


def _params(sem=None):
    return pltpu.CompilerParams(dimension_semantics=sem, vmem_limit_bytes=VMEM_LIMIT)


def _full(shape):
    n = len(shape)
    return pl.BlockSpec(shape, lambda *_: (0,) * n)


ANY = pl.BlockSpec(memory_space=pl.ANY)
HBM = pl.BlockSpec(memory_space=pltpu.HBM)


def in_proj(x, normw, w_main, w_small):
    t = x.shape[0]
    tm, tn = min(1024, t), 512

    def body(x_ref, nw_ref, wm_ref, ws_ref, pm_ref, ps_ref, u_ref):
        @pl.when(pl.program_id(1) == 0)
        def _():
            u = rmsnorm(x_ref[...], nw_ref[...]).astype(MXU_DTYPE)
            u_ref[...] = u
            ps_ref[...] = _raw_dot(u, ws_ref[...], 1, 0)
        pm_ref[...] = _raw_dot(u_ref[...], wm_ref[...], 1, 0)

    return pl.pallas_call(
        body, name="in_proj", grid=(t // tm, MAIN // tn),
        in_specs=[pl.BlockSpec((tm, D_MODEL), lambda i, j: (i, 0)), _full((1, D_MODEL)),
                  pl.BlockSpec((D_MODEL, tn), lambda i, j: (0, j)), _full((D_MODEL, LANES))],
        out_specs=[pl.BlockSpec((tm, tn), lambda i, j: (i, j)), pl.BlockSpec((tm, LANES), lambda i, j: (i, 0)),
                   pl.BlockSpec((tm, D_MODEL), lambda i, j: (i, 0))],
        out_shape=[jax.ShapeDtypeStruct((t, MAIN), F32), jax.ShapeDtypeStruct((t, LANES), F32),
                   jax.ShapeDtypeStruct((t, D_MODEL), MXU_DTYPE)],
        compiler_params=_params(("arbitrary", "arbitrary")),
    )(x, normw, w_main, w_small)


CONV_TC = 512
HALO = 8


def _shift_down(cur, prev, s):
    rolled = pltpu.roll(cur, s, 0)
    top = jnp.where(_iota((HALO, cur.shape[1]), 0) < s, pltpu.roll(prev, s, 0), rolled[:HALO])
    if cur.shape[0] == HALO:
        return top
    return jnp.concatenate([top, rolled[HALO:]], axis=0)


def _shift_up(cur, nxt, s):
    n = cur.shape[0]
    rolled = pltpu.roll(cur, n - s, 0)
    bot = jnp.where(_iota((HALO, cur.shape[1]), 0) >= HALO - s, pltpu.roll(nxt, HALO - s, 0), rolled[n - HALO:])
    return jnp.concatenate([rolled[:n - HALO], bot], axis=0)


def _conv_pre(cur, prev, w_ref, b):
    acc = cur * w_ref[3:4, :] + b
    shifted = [cur]
    for s in (1, 2, 3):
        sh = _shift_down(cur, prev, s)
        shifted.append(sh)
        acc = acc + sh * w_ref[3 - s:4 - s, :]
    return acc, shifted


def conv_fwd(proj_main, col0, width, w, b, name):
    t = proj_main.shape[0]
    tt, c0 = min(512, t), col0 // CONV_TC

    def body(cur_ref, prev_ref, w_ref, b_ref, out_ref):
        prev = jnp.where(pl.program_id(0) > 0, prev_ref[...], 0.0)
        pre, _ = _conv_pre(cur_ref[...], prev, w_ref, b_ref[...])
        out_ref[...] = silu(pre)

    return pl.pallas_call(
        body, name=name, grid=(t // tt, width // CONV_TC),
        in_specs=[pl.BlockSpec((tt, CONV_TC), lambda i, j: (i, c0 + j)),
                  pl.BlockSpec((HALO, CONV_TC), lambda i, j: (jnp.maximum(i * (tt // HALO) - 1, 0), c0 + j)),
                  pl.BlockSpec((4, CONV_TC), lambda i, j: (0, j)), pl.BlockSpec((1, CONV_TC), lambda i, j: (0, j))],
        out_specs=pl.BlockSpec((tt, CONV_TC), lambda i, j: (i, j)),
        out_shape=jax.ShapeDtypeStruct((t, width), F32),
        compiler_params=_params(("arbitrary", "arbitrary")),
    )(proj_main, proj_main, w, b)


def _dsilu(pre):
    sg = sigmoid(pre)
    return sg * (1.0 + pre * (1.0 - sg))


def conv_bwd(dproj_main, proj_main, col0, width, w, b, dout, name):
    t = proj_main.shape[0]
    tt, c0 = min(512, t), col0 // CONV_TC
    nt = t // tt
    after = lambda i: jnp.minimum((i + 1) * (tt // HALO), t // HALO - 1)

    def body(alias_ref, cur_ref, prev_ref, nxt_ref, w_ref, b_ref, do_ref, do_nxt_ref, dx_ref, dwb_ref):
        del alias_ref
        i = pl.program_id(1)
        cur, bias = cur_ref[...], b_ref[...]
        prev = jnp.where(i > 0, prev_ref[...], 0.0)
        pre, shifted = _conv_pre(cur, prev, w_ref, bias)
        dpre = do_ref[...] * _dsilu(pre)
        pre_nxt, _ = _conv_pre(nxt_ref[...], cur[tt - HALO:], w_ref, bias)
        dpre_nxt = jnp.where(i < nt - 1, do_nxt_ref[...] * _dsilu(pre_nxt), 0.0)
        dx = dpre * w_ref[3:4, :]
        for s in (1, 2, 3):
            dx = dx + _shift_up(dpre, dpre_nxt, s) * w_ref[3 - s:4 - s, :]
        dx_ref[...] = dx.astype(dx_ref.dtype)
        row = _iota((HALO, CONV_TC), 0)
        upd = jnp.where(row == 4, jnp.sum(dpre, axis=0, keepdims=True), 0.0)
        for s in range(4):
            upd = upd + jnp.where(row == 3 - s, jnp.sum(dpre * shifted[s], axis=0, keepdims=True), 0.0)
        _accumulate(dwb_ref, i == 0, upd)

    return pl.pallas_call(
        body, name=name, grid=(width // CONV_TC, nt),
        in_specs=[ANY, pl.BlockSpec((tt, CONV_TC), lambda j, i: (i, c0 + j)),
                  pl.BlockSpec((HALO, CONV_TC), lambda j, i: (jnp.maximum(i * (tt // HALO) - 1, 0), c0 + j)),
                  pl.BlockSpec((HALO, CONV_TC), lambda j, i: (after(i), c0 + j)),
                  pl.BlockSpec((4, CONV_TC), lambda j, i: (0, j)), pl.BlockSpec((1, CONV_TC), lambda j, i: (0, j)),
                  pl.BlockSpec((tt, CONV_TC), lambda j, i: (i, j)),
                  pl.BlockSpec((HALO, CONV_TC), lambda j, i: (after(i), j))],
        out_specs=[pl.BlockSpec((tt, CONV_TC), lambda j, i: (i, c0 + j)),
                   pl.BlockSpec((HALO, CONV_TC), lambda j, i: (0, j))],
        out_shape=[jax.ShapeDtypeStruct(dproj_main.shape, dproj_main.dtype), jax.ShapeDtypeStruct((HALO, width), F32)],
        input_output_aliases={0: 0},
        compiler_params=_params(("arbitrary", "arbitrary")),
    )(dproj_main, proj_main, proj_main, proj_main, w, b, dout, dout)


def _ssd_parts(xbc_ref):
    part = lambda o, w: [xbc_ref[:, g * SSD_GC + o:g * SSD_GC + o + w] for g in range(SSD_GROUPS)]
    return part(0, SSD_GW), part(SSD_GW, SSD_STATE), part(SSD_GW + SSD_STATE, SSD_STATE)


def _group_cols(ref):
    return [ref[:, g * SSD_GW:(g + 1) * SSD_GW] for g in range(SSD_GROUPS)]


def _gdn_parts(qkv_ref):
    part = lambda o: [qkv_ref[:, j * GDN_HC + o:j * GDN_HC + o + GDN_DK] for j in range(GDN_HB)]
    return part(0), part(GDN_DK), part(2 * GDN_DK)


def _head_cols(ref):
    return [ref[:, j * GDN_DV:(j + 1) * GDN_DV] for j in range(GDN_HB)]


def _first_head():
    return 0 if GDN_HB == GDN_HEADS else pl.program_id(1) * GDN_HB


def ssd_fwd(conv_ssd, proj_main, proj_small, normw, alog, dtb, dvec):
    t = conv_ssd.shape[0]
    nc = t // CHUNK

    groups = range(SSD_GROUPS)

    def body(xbc_ref, z_ref, sm_ref, nw_ref, al_ref, db_ref, dv_ref, y_ref, hist_ref, state_ref):
        @pl.when(pl.program_id(0) == 0)
        def _():
            state_ref[...] = jnp.zeros(state_ref.shape, F32)

        states = [state_ref[g] for g in groups]
        for g in groups:
            hist_ref[0, g] = states[g]
        ys, new_states = ssd_chunk(*_ssd_parts(xbc_ref), _group_cols(z_ref), sm_ref[...], _group_cols(nw_ref),
                                   al_ref[...], db_ref[...], dv_ref[...], states)
        for g in groups:
            y_ref[:, g * SSD_GW:(g + 1) * SSD_GW] = ys[g].astype(MXU_DTYPE)
            state_ref[g] = new_states[g]

    return pl.pallas_call(
        body, name="ssd_fwd", grid=(nc,),
        in_specs=[pl.BlockSpec((CHUNK, SSD_CONV), lambda c: (c, 0)),
                  pl.BlockSpec((CHUNK, SSD_WIDTH), lambda c: (c, COL_Z // SSD_WIDTH)),
                  pl.BlockSpec((CHUNK, LANES), lambda c: (c, 0)),
                  _full((1, SSD_WIDTH)), _full((1, LANES)), _full((1, LANES)), _full((1, LANES))],
        out_specs=[pl.BlockSpec((CHUNK, SSD_WIDTH), lambda c: (c, 0)),
                   pl.BlockSpec((1, SSD_GROUPS, SSD_STATE, SSD_GW), lambda c: (c, 0, 0, 0))],
        out_shape=[jax.ShapeDtypeStruct((t, SSD_WIDTH), MXU_DTYPE),
                   jax.ShapeDtypeStruct((nc, SSD_GROUPS, SSD_STATE, SSD_GW), F32)],
        scratch_shapes=[pltpu.VMEM((SSD_GROUPS, SSD_STATE, SSD_GW), F32)],
        compiler_params=_params(("arbitrary",)),
    )(conv_ssd, proj_main, proj_small, normw, alog, dtb, dvec)


def _accumulate(ref, first, value):
    @pl.when(first)
    def _():
        ref[...] = value

    @pl.when(jnp.logical_not(first))
    def _():
        ref[...] += value


def ssd_bwd(conv_ssd, proj_main, proj_small, normw, alog, dtb, dvec, hist, dy):
    t = conv_ssd.shape[0]
    nc = t // CHUNK
    rev = lambda c: nc - 1 - c
    groups = range(SSD_GROUPS)

    def body(xbc_ref, z_ref, sm_ref, nw_ref, al_ref, db_ref, dv_ref, hist_ref, dy_ref,
             dxbc_ref, dz_ref, dsm_ref, dnw_ref, dal_ref, ddb_ref, ddv_ref, dstate_ref):
        first = pl.program_id(0) == 0

        @pl.when(first)
        def _():
            dstate_ref[...] = jnp.zeros(dstate_ref.shape, F32)

        _, vjp = jax.vjp(ssd_chunk, *_ssd_parts(xbc_ref), _group_cols(z_ref), sm_ref[...], _group_cols(nw_ref),
                         al_ref[...], db_ref[...], dv_ref[...], [hist_ref[0, g] for g in groups])
        dxs, dbm, dcm, dz, dsm, dnw, dal, ddb, ddv, dstate = vjp(
            (_group_cols(dy_ref), [dstate_ref[g] for g in groups]))
        for g in groups:
            base = g * SSD_GC
            dxbc_ref[:, base:base + SSD_GW] = dxs[g]
            dxbc_ref[:, base + SSD_GW:base + SSD_GW + SSD_STATE] = dbm[g]
            dxbc_ref[:, base + SSD_GW + SSD_STATE:base + SSD_GC] = dcm[g]
            dz_ref[:, g * SSD_GW:(g + 1) * SSD_GW] = dz[g].astype(dz_ref.dtype)
            dstate_ref[g] = dstate[g]
        dsm_ref[...] = dsm
        _accumulate(dnw_ref, first, join_lanes(dnw))
        _accumulate(dal_ref, first, dal)
        _accumulate(ddb_ref, first, ddb)
        _accumulate(ddv_ref, first, ddv)

    return pl.pallas_call(
        body, name="ssd_bwd", grid=(nc,),
        in_specs=[pl.BlockSpec((CHUNK, SSD_CONV), lambda c: (rev(c), 0)),
                  pl.BlockSpec((CHUNK, SSD_WIDTH), lambda c: (rev(c), COL_Z // SSD_WIDTH)),
                  pl.BlockSpec((CHUNK, LANES), lambda c: (rev(c), 0)),
                  _full((1, SSD_WIDTH)), _full((1, LANES)), _full((1, LANES)), _full((1, LANES)),
                  pl.BlockSpec((1, SSD_GROUPS, SSD_STATE, SSD_GW), lambda c: (rev(c), 0, 0, 0)),
                  pl.BlockSpec((CHUNK, SSD_WIDTH), lambda c: (rev(c), 0))],
        out_specs=[pl.BlockSpec((CHUNK, SSD_CONV), lambda c: (rev(c), 0)),
                   pl.BlockSpec((CHUNK, SSD_WIDTH), lambda c: (rev(c), COL_Z // SSD_WIDTH)),
                   pl.BlockSpec((CHUNK, LANES), lambda c: (rev(c), 0)),
                   _full((1, SSD_WIDTH)), _full((1, LANES)), _full((1, LANES)), _full((1, LANES))],
        out_shape=[jax.ShapeDtypeStruct((t, SSD_CONV), F32), jax.ShapeDtypeStruct((t, MAIN), MXU_DTYPE),
                   jax.ShapeDtypeStruct((t, LANES), F32), jax.ShapeDtypeStruct((1, SSD_WIDTH), F32),
                   jax.ShapeDtypeStruct((1, LANES), F32), jax.ShapeDtypeStruct((1, LANES), F32),
                   jax.ShapeDtypeStruct((1, LANES), F32)],
        scratch_shapes=[pltpu.VMEM((SSD_GROUPS, SSD_STATE, SSD_GW), F32)],
        compiler_params=_params(("arbitrary",)),
    )(conv_ssd, proj_main, proj_small, normw, alog, dtb, dvec, hist, dy)


def gdn_fwd(conv_gdn, proj_main, proj_small, normw, alog, dtb):
    t = conv_gdn.shape[0]
    nc = t // CHUNK

    hb = GDN_HB
    gate_blk = COL_GATE // (GDN_DV * hb)

    def body(qkv_ref, gate_ref, sm_ref, nw_ref, al_ref, db_ref, y_ref, hist_ref, t_ref, state_ref):
        h0 = _first_head()

        @pl.when(pl.program_id(0) == 0)
        def _():
            for j in range(hb):
                state_ref[h0 + j] = jnp.zeros((GDN_DK, GDN_DV), F32)

        states = [state_ref[h0 + j] for j in range(hb)]
        for j in range(hb):
            hist_ref[0, j] = states[j]
        qs, ks, vs = _gdn_parts(qkv_ref)
        ys, new_states, ts = gdn_chunk(h0, qs, ks, vs, sm_ref[...], _head_cols(gate_ref), nw_ref[...], al_ref[...],
                                       db_ref[...], states)
        for j in range(hb):
            y_ref[:, j * GDN_DV:(j + 1) * GDN_DV] = ys[j].astype(MXU_DTYPE)
            state_ref[h0 + j] = new_states[j]
            t_ref[0, j] = ts[j]

    return pl.pallas_call(
        body, name="gdn_fwd", grid=(nc, GDN_HEADS // hb),
        in_specs=[pl.BlockSpec((CHUNK, GDN_HC * hb), lambda c, h: (c, h)),
                  pl.BlockSpec((CHUNK, GDN_DV * hb), lambda c, h: (c, gate_blk + h)),
                  pl.BlockSpec((CHUNK, LANES), lambda c, h: (c, 0)),
                  _full((1, GDN_DV)), _full((1, LANES)), _full((1, LANES))],
        out_specs=[pl.BlockSpec((CHUNK, GDN_DV * hb), lambda c, h: (c, h)),
                   pl.BlockSpec((1, hb, GDN_DK, GDN_DV), lambda c, h: (c, h, 0, 0)),
                   pl.BlockSpec((1, hb, CHUNK, CHUNK), lambda c, h: (c, h, 0, 0))],
        out_shape=[jax.ShapeDtypeStruct((t, GDN_W), MXU_DTYPE),
                   jax.ShapeDtypeStruct((nc, GDN_HEADS, GDN_DK, GDN_DV), F32),
                   jax.ShapeDtypeStruct((nc, GDN_HEADS, CHUNK, CHUNK), F32)],
        scratch_shapes=[pltpu.VMEM((GDN_HEADS, GDN_DK, GDN_DV), F32)],
        compiler_params=_params(("arbitrary", "arbitrary")),
    )(conv_gdn, proj_main, proj_small, normw, alog, dtb)


def gdn_bwd(dproj_main, conv_gdn, proj_main, proj_small, normw, alog, dtb, hist, t_inv, dy):
    t = conv_gdn.shape[0]
    nc = t // CHUNK
    rev = lambda c: nc - 1 - c
    hb = GDN_HB
    gate_blk = COL_GATE // (GDN_DV * hb)

    def body(alias_ref, qkv_ref, gate_ref, sm_ref, nw_ref, al_ref, db_ref, hist_ref, t_ref, dy_ref,
             dgate_ref, dqkv_ref, dsm_ref, dnw_ref, dal_ref, ddb_ref, dstate_ref):
        del alias_ref
        c, h = pl.program_id(0), pl.program_id(1)
        h0 = _first_head()

        @pl.when(c == 0)
        def _():
            for j in range(hb):
                dstate_ref[h0 + j] = jnp.zeros((GDN_DK, GDN_DV), F32)

        saved = [t_ref[0, j] for j in range(hb)]

        def fn(qs, ks, vs, small, gates, nw, al, db, states):
            return gdn_chunk(h0, qs, ks, vs, small, gates, nw, al, db, states, saved)[:2]

        qs, ks, vs = _gdn_parts(qkv_ref)
        _, vjp = jax.vjp(fn, qs, ks, vs, sm_ref[...], _head_cols(gate_ref), nw_ref[...], al_ref[...], db_ref[...],
                         [hist_ref[0, j] for j in range(hb)])
        dqs, dks, dvs, dsm, dgates, dnw, dal, ddb, dstates = vjp(
            (_head_cols(dy_ref), [dstate_ref[h0 + j] for j in range(hb)]))
        for j in range(hb):
            base = j * GDN_HC
            dqkv_ref[:, base:base + GDN_DK] = dqs[j]
            dqkv_ref[:, base + GDN_DK:base + 2 * GDN_DK] = dks[j]
            dqkv_ref[:, base + 2 * GDN_DK:base + GDN_HC] = dvs[j]
            dgate_ref[:, j * GDN_DV:(j + 1) * GDN_DV] = dgates[j].astype(dgate_ref.dtype)
            dstate_ref[h0 + j] = dstates[j]
        _accumulate(dsm_ref, h == 0, dsm)
        first = jnp.logical_and(c == 0, h == 0)
        _accumulate(dnw_ref, first, dnw)
        _accumulate(dal_ref, first, dal)
        _accumulate(ddb_ref, first, ddb)

    return pl.pallas_call(
        body, name="gdn_bwd", grid=(nc, GDN_HEADS // hb),
        in_specs=[ANY, pl.BlockSpec((CHUNK, GDN_HC * hb), lambda c, h: (rev(c), h)),
                  pl.BlockSpec((CHUNK, GDN_DV * hb), lambda c, h: (rev(c), gate_blk + h)),
                  pl.BlockSpec((CHUNK, LANES), lambda c, h: (rev(c), 0)),
                  _full((1, GDN_DV)), _full((1, LANES)), _full((1, LANES)),
                  pl.BlockSpec((1, hb, GDN_DK, GDN_DV), lambda c, h: (rev(c), h, 0, 0)),
                  pl.BlockSpec((1, hb, CHUNK, CHUNK), lambda c, h: (rev(c), h, 0, 0)),
                  pl.BlockSpec((CHUNK, GDN_DV * hb), lambda c, h: (rev(c), h))],
        out_specs=[pl.BlockSpec((CHUNK, GDN_DV * hb), lambda c, h: (rev(c), gate_blk + h)),
                   pl.BlockSpec((CHUNK, GDN_HC * hb), lambda c, h: (rev(c), h)),
                   pl.BlockSpec((CHUNK, LANES), lambda c, h: (rev(c), 0)),
                   _full((1, GDN_DV)), _full((1, LANES)), _full((1, LANES))],
        out_shape=[jax.ShapeDtypeStruct(dproj_main.shape, dproj_main.dtype), jax.ShapeDtypeStruct((t, GDN_CONV), F32),
                   jax.ShapeDtypeStruct((t, LANES), F32), jax.ShapeDtypeStruct((1, GDN_DV), F32),
                   jax.ShapeDtypeStruct((1, LANES), F32), jax.ShapeDtypeStruct((1, LANES), F32)],
        scratch_shapes=[pltpu.VMEM((GDN_HEADS, GDN_DK, GDN_DV), F32)],
        input_output_aliases={0: 0},
        compiler_params=_params(("arbitrary", "arbitrary")),
    )(dproj_main, conv_gdn, proj_main, proj_small, normw, alog, dtb, hist, t_inv, dy)


def out_proj_loss(x, y_ssd, y_gdn, w_out, final_w, target):
    t = x.shape[0]
    tm = min(256, t)

    def body(x_ref, ys_ref, yg_ref, wo_ref, fw_ref, tg_ref, loss_ref, dhid_ref, dys_ref, dyg_ref, dwo_ref, dfw_ref):
        i = pl.program_id(0)
        ys, yg = ys_ref[...], yg_ref[...]
        wo_s, wo_g = wo_ref[:SSD_WIDTH, :], wo_ref[SSD_WIDTH:, :]
        hid = x_ref[...] + _raw_dot(ys, wo_s, 1, 0) + _raw_dot(yg, wo_g, 1, 0)
        out, vjp = jax.vjp(rmsnorm, hid, fw_ref[...])
        err = out - tg_ref[...]
        loss = 0.5 * jnp.sum(jnp.mean(err * err, axis=-1, keepdims=True), axis=0, keepdims=True)
        dhid, dfw = vjp(err * (1.0 / D_MODEL))
        dhid_ref[...] = dhid
        dys_ref[...] = _raw_dot(dhid, wo_s, 1, 1)
        dyg_ref[...] = _raw_dot(dhid, wo_g, 1, 1)
        first = i == 0
        _accumulate(loss_ref, first, jnp.broadcast_to(loss, loss_ref.shape))
        _accumulate(dfw_ref, first, dfw)

        @pl.when(first)
        def _():
            dwo_ref[:SSD_WIDTH, :] = _raw_dot(ys, dhid, 0, 0)
            dwo_ref[SSD_WIDTH:, :] = _raw_dot(yg, dhid, 0, 0)

        @pl.when(i > 0)
        def _():
            dwo_ref[:SSD_WIDTH, :] += _raw_dot(ys, dhid, 0, 0)
            dwo_ref[SSD_WIDTH:, :] += _raw_dot(yg, dhid, 0, 0)

    row = lambda w: pl.BlockSpec((tm, w), lambda i: (i, 0))
    return pl.pallas_call(
        body, name="out_proj_loss", grid=(t // tm,),
        in_specs=[row(D_MODEL), row(SSD_WIDTH), row(GDN_W), _full((SSD_WIDTH + GDN_W, D_MODEL)), _full((1, D_MODEL)),
                  row(D_MODEL)],
        out_specs=[_full((8, LANES)), row(D_MODEL), row(SSD_WIDTH), row(GDN_W), _full((SSD_WIDTH + GDN_W, D_MODEL)),
                   _full((1, D_MODEL))],
        out_shape=[jax.ShapeDtypeStruct((8, LANES), F32), jax.ShapeDtypeStruct((t, D_MODEL), F32),
                   jax.ShapeDtypeStruct((t, SSD_WIDTH), F32), jax.ShapeDtypeStruct((t, GDN_W), F32),
                   jax.ShapeDtypeStruct((SSD_WIDTH + GDN_W, D_MODEL), F32), jax.ShapeDtypeStruct((1, D_MODEL), F32)],
        compiler_params=_params(("arbitrary",)),
    )(x, y_ssd, y_gdn, w_out, final_w, target)


def in_proj_bwd_x(x, normw, w_main, w_small, dproj_main, dsmall_a, dsmall_b, dhid, slabbed):
    t = x.shape[0]
    tm, tk = min(1024, t), 512
    nk = MAIN // tk
    ni = t // tm
    ns = len(slabbed)

    def body(x_ref, nw_ref, wm_ref, ws_ref, dp_ref, da_ref, db_ref, dh_ref, *rest):
        slab_refs, (gx_ref, dnw_ref), land_refs = rest[:ns], rest[ns:ns + 2], rest[ns + 2:2 * ns + 2]
        acc_ref, sems = rest[2 * ns + 2], rest[2 * ns + 3:]
        i, k = pl.program_id(0), pl.program_id(1)
        start, finish = _slab_exchange(slab_refs, land_refs, ns, *sems)

        @pl.when(jnp.logical_and(i == 0, k == 0))
        def _():
            start()

        part = _raw_dot(dp_ref[...], wm_ref[...], 1, 1)

        @pl.when(k == 0)
        def _():
            acc_ref[...] = part + _raw_dot(da_ref[...] + db_ref[...], ws_ref[...], 1, 1)

        @pl.when(k > 0)
        def _():
            acc_ref[...] += part

        @pl.when(k == nk - 1)
        def _():
            _, vjp = jax.vjp(rmsnorm, x_ref[...], nw_ref[...])
            dx, dnw = vjp(acc_ref[...])
            gx_ref[...] = dx + dh_ref[...]
            _accumulate(dnw_ref, i == 0, dnw)

        @pl.when(jnp.logical_and(i == ni - 1, k == nk - 1))
        def _():
            finish()

    row = lambda w: pl.BlockSpec((tm, w), lambda i, k: (i, 0))
    out = pl.pallas_call(
        body, name="in_proj_bwd_x", grid=(ni, nk),
        in_specs=[row(D_MODEL), _full((1, D_MODEL)), pl.BlockSpec((D_MODEL, tk), lambda i, k: (0, k)),
                  _full((D_MODEL, LANES)), pl.BlockSpec((tm, tk), lambda i, k: (i, k)), row(LANES), row(LANES),
                  row(D_MODEL)] + [HBM] * ns,
        out_specs=[row(D_MODEL), _full((1, D_MODEL))] + [HBM] * ns,
        out_shape=[jax.ShapeDtypeStruct((t, D_MODEL), F32), jax.ShapeDtypeStruct((1, D_MODEL), F32)]
        + _slab_exchange_shapes(slabbed, []),
        scratch_shapes=[pltpu.VMEM((tm, D_MODEL), F32)] + _slab_exchange_sems(ns),
        compiler_params=_params(("arbitrary", "arbitrary")),
    )(x, normw, w_main, w_small, dproj_main, dsmall_a, dsmall_b, dhid, *slabbed)
    return out[0], out[1], out[2:]


def in_proj_bwd_w(u, dproj_main, dsmall_a, dsmall_b):
    t = u.shape[0]
    tm, tn = min(512, t), MAIN // 4

    def body(u_ref, dp_ref, da_ref, db_ref, dwm_ref, dws_ref):
        j, i = pl.program_id(0), pl.program_id(1)
        uu = u_ref[...]
        _accumulate(dwm_ref, i == 0, _raw_dot(uu, dp_ref[...], 0, 0))

        @pl.when(j == 0)
        def _():
            _accumulate(dws_ref, i == 0, _raw_dot(uu, da_ref[...] + db_ref[...], 0, 0))

    return pl.pallas_call(
        body, name="in_proj_bwd_w", grid=(MAIN // tn, t // tm),
        in_specs=[pl.BlockSpec((tm, D_MODEL), lambda j, i: (i, 0)), pl.BlockSpec((tm, tn), lambda j, i: (i, j)),
                  pl.BlockSpec((tm, LANES), lambda j, i: (i, 0)), pl.BlockSpec((tm, LANES), lambda j, i: (i, 0))],
        out_specs=[pl.BlockSpec((D_MODEL, tn), lambda j, i: (0, j)), _full((D_MODEL, LANES))],
        out_shape=[jax.ShapeDtypeStruct((D_MODEL, MAIN), F32), jax.ShapeDtypeStruct((D_MODEL, LANES), F32)],
        compiler_params=_params(("arbitrary", "arbitrary")),
    )(u, dproj_main, dsmall_a, dsmall_b)


def sum_slabs(a, name):
    n, rows, cols = a.shape
    tr = 64 if rows % 64 == 0 else rows

    def body(a_ref, o_ref):
        acc = a_ref[0].astype(F32)
        for d in range(1, n):
            acc = acc + a_ref[d].astype(F32)
        o_ref[...] = acc

    return pl.pallas_call(
        body, name=name, grid=(rows // tr,),
        in_specs=[pl.BlockSpec((n, tr, cols), lambda i: (0, i, 0))],
        out_specs=pl.BlockSpec((tr, cols), lambda i: (i, 0)),
        out_shape=jax.ShapeDtypeStruct((rows, cols), F32),
        compiler_params=_params(("arbitrary",)),
    )(a)


def adamw(w, g, m, v, name):
    rows, cols = w.shape
    tr = 128 if rows % 128 == 0 else rows

    def body(w_ref, g_ref, m_ref, v_ref, d_ref, nm_ref, nv_ref):
        gg = g_ref[...]
        nm = ADAM_B1 * m_ref[...] + (1.0 - ADAM_B1) * gg
        nv = ADAM_B2 * v_ref[...] + (1.0 - ADAM_B2) * (gg * gg)
        m_hat = nm / (1.0 - ADAM_B1 ** ADAM_STEP)
        v_hat = nv / (1.0 - ADAM_B2 ** ADAM_STEP)
        d_ref[...] = -ADAM_LR * (m_hat / (jnp.sqrt(v_hat) + ADAM_EPS) + ADAM_WD * w_ref[...])
        nm_ref[...] = nm
        nv_ref[...] = nv

    spec = pl.BlockSpec((tr, cols), lambda i: (i, 0))
    shp = jax.ShapeDtypeStruct((rows, cols), F32)
    return pl.pallas_call(
        body, name=name, grid=(rows // tr,), in_specs=[spec] * 4, out_specs=[spec] * 3, out_shape=[shp] * 3,
        compiler_params=_params(("arbitrary",)),
    )(w, g, m, v)


def _my_place():
    return lax.axis_index("x"), lax.axis_index("y"), lax.axis_index("c")


def gather_weights(big, small):
    nb, n = len(big), len(big) + len(small)
    parts = 4

    def body(*refs):
        srcs, outs = refs[:n], refs[n:2 * n]
        land_a, land_b = refs[2 * n:2 * n + nb], refs[2 * n + nb:2 * n + 2 * nb]
        send_sems, recv_sems, fwd_send, fwd_recv, local_sems = refs[2 * n + 2 * nb:]
        x, y, c = _my_place()
        me = 2 * x + y
        chips = [(1 - x, y), (x, 1 - y), (1 - x, 1 - y)]
        half = [a.shape[0] // 2 for a in big]

        def ici(j, i):
            px, py = chips[j]
            if i < nb:
                src, dst = srcs[i].at[pl.ds(c * half[i], half[i])], land_a[i].at[j]
            else:
                src, dst = srcs[i], outs[i].at[me]
            return pltpu.make_async_remote_copy(src_ref=src, dst_ref=dst, send_sem=send_sems.at[j * n + i],
                                                recv_sem=recv_sems.at[j * n + i], device_id=(px, py, c),
                                                device_id_type=MESH)

        def ici_arrival(j, i):
            px, py = chips[j]
            dst = land_a[i].at[j] if i < nb else outs[i].at[2 * px + py]
            return pltpu.make_async_remote_copy(src_ref=dst, dst_ref=dst, send_sem=send_sems.at[j * n + i],
                                                recv_sem=recv_sems.at[j * n + i], device_id=(px, py, c),
                                                device_id_type=MESH)

        def forward(j, i, p):
            rows = half[i] // parts
            k = (j * nb + i) * parts + p
            return pltpu.make_async_remote_copy(
                src_ref=land_a[i].at[j, pl.ds(p * rows, rows)], dst_ref=land_b[i].at[j, pl.ds(p * rows, rows)],
                send_sem=fwd_send.at[k], recv_sem=fwd_recv.at[k], device_id=(x, y, 1 - c), device_id_type=MESH)

        def store(j, i, from_sibling):
            px, py = chips[j]
            buf, h = (land_b, 1 - c) if from_sibling else (land_a, c)
            k = n + (j * nb + i) * 2 + (1 if from_sibling else 0)
            return pltpu.make_async_copy(buf[i].at[j], outs[i].at[2 * px + py, pl.ds(h * half[i], half[i])],
                                         local_sems.at[k])

        own = [pltpu.make_async_copy(srcs[i], outs[i].at[me], local_sems.at[i]) for i in range(n)]
        sends = [ici(j, i) for j in range(3) for i in range(n)]
        for cp in own + sends:
            cp.start()
        pending = []
        for j in range(3):
            for i in range(n):
                ici_arrival(j, i).wait_recv()
                if i < nb:
                    fw = [forward(j, i, p) for p in range(parts)]
                    st = store(j, i, False)
                    for cp in fw + [st]:
                        cp.start()
                    pending += [cp.wait_send for cp in fw] + [st.wait]
        for j in range(3):
            for i in range(nb):
                for p in range(parts):
                    forward(j, i, p).wait_recv()
                st = store(j, i, True)
                st.start()
                pending.append(st.wait)
        for cp in sends:
            cp.wait_send()
        for wait in pending:
            wait()
        for cp in own:
            cp.wait()

    shards = list(big) + list(small)
    lands = [pltpu.VMEM((3, a.shape[0] // 2) + a.shape[1:], a.dtype) for a in big]
    return pl.pallas_call(
        body, name="gather_weights",
        in_specs=[HBM] * n, out_specs=[HBM] * n,
        out_shape=[jax.ShapeDtypeStruct((N_CHIP,) + s.shape, s.dtype) for s in shards],
        scratch_shapes=lands + lands + [
            pltpu.SemaphoreType.DMA((3 * n,)), pltpu.SemaphoreType.DMA((3 * n,)),
            pltpu.SemaphoreType.DMA((3 * nb * parts,)), pltpu.SemaphoreType.DMA((3 * nb * parts,)),
            pltpu.SemaphoreType.DMA((n + 6 * nb,))],
        compiler_params=pltpu.CompilerParams(vmem_limit_bytes=VMEM_LIMIT),
    )(*shards)


def _peer(x, y, c, mask):
    mx, my, mc = (mask >> 2) & 1, (mask >> 1) & 1, mask & 1
    return (x ^ mx if mx else x, y ^ my if my else y, c ^ mc if mc else c)


def _slab_exchange_shapes(slabbed, replicated):
    return ([jax.ShapeDtypeStruct(a.shape, a.dtype) for a in slabbed]
            + [jax.ShapeDtypeStruct((N_DEV,) + a.shape, a.dtype) for a in replicated])


def _slab_exchange_sems(n):
    return [pltpu.SemaphoreType.DMA((7 * n,)), pltpu.SemaphoreType.DMA((7 * n,)), pltpu.SemaphoreType.DMA((n,))]


def _slab_exchange(srcs, outs, ns, send_sems, recv_sems, local_sems):
    n = len(srcs)
    x, y, c = _my_place()
    me = 4 * x + 2 * y + c

    def piece(i, dev):
        return srcs[i].at[dev] if i < ns else srcs[i]

    def copies(arriving):
        out = []
        for mask in range(1, N_DEV):
            px, py, pc = _peer(x, y, c, mask)
            dev = 4 * px + 2 * py + pc
            for i in range(n):
                k = (mask - 1) * n + i
                out.append(pltpu.make_async_remote_copy(
                    src_ref=piece(i, dev), dst_ref=outs[i].at[dev if arriving else me], send_sem=send_sems.at[k],
                    recv_sem=recv_sems.at[k], device_id=(px, py, pc), device_id_type=MESH))
        return out

    def local():
        return [pltpu.make_async_copy(piece(i, me), outs[i].at[me], local_sems.at[i]) for i in range(n)]

    def start():
        for cp in local() + copies(False):
            cp.start()

    def finish():
        for cp in copies(True):
            cp.wait_recv()
        for cp in copies(False):
            cp.wait_send()
        for cp in local():
            cp.wait()

    return start, finish


def exchange_halves(halves, replicated):
    n, nr = len(halves), len(replicated)
    streams = 8

    def body(*refs):
        srcs, rep_srcs, outs, rep_outs = refs[:n], refs[n:n + nr], refs[n + nr:2 * n + nr], refs[2 * n + nr:2 * (n + nr)]
        refs = refs[2 * (n + nr):]
        mine, theirs = refs[:n], refs[n:2 * n]
        send_sems, recv_sems, in_sems, out_sems = refs[2 * n:2 * n + 4]
        rep_start, rep_finish = _slab_exchange(rep_srcs, rep_outs, 0, *refs[2 * n + 4:])
        rep_start()
        x, y, c = _my_place()
        loads = [pltpu.make_async_copy(srcs[i], mine[i], in_sems.at[i]) for i in range(n)]
        for cp in loads:
            cp.start()
        for cp in loads:
            cp.wait()

        def chunk_copy(i, s):
            rows = halves[i].shape[0] // streams
            k = i * streams + s
            return pltpu.make_async_remote_copy(
                src_ref=mine[i].at[pl.ds(s * rows, rows)], dst_ref=theirs[i].at[pl.ds(s * rows, rows)],
                send_sem=send_sems.at[k], recv_sem=recv_sems.at[k], device_id=(x, y, 1 - c), device_id_type=MESH)

        sends = [chunk_copy(i, s) for i in range(n) for s in range(streams)]
        for cp in sends:
            cp.start()
        own = [pltpu.make_async_copy(mine[i], outs[i].at[c], out_sems.at[i]) for i in range(n)]
        for cp in own:
            cp.start()
        for cp in sends:
            cp.wait_recv()
        got = [pltpu.make_async_copy(theirs[i], outs[i].at[1 - c], out_sems.at[n + i]) for i in range(n)]
        for cp in got:
            cp.start()
        for cp in sends:
            cp.wait_send()
        for cp in own + got:
            cp.wait()
        rep_finish()

    vmem = [pltpu.VMEM(a.shape, a.dtype) for a in halves]
    out = pl.pallas_call(
        body, name="exchange_halves",
        in_specs=[HBM] * (n + nr), out_specs=[HBM] * (n + nr),
        out_shape=[jax.ShapeDtypeStruct((2,) + a.shape, a.dtype) for a in halves]
        + _slab_exchange_shapes([], replicated),
        scratch_shapes=vmem + vmem + [pltpu.SemaphoreType.DMA((n * streams,)), pltpu.SemaphoreType.DMA((n * streams,)),
                                      pltpu.SemaphoreType.DMA((n,)), pltpu.SemaphoreType.DMA((2 * n,))]
        + _slab_exchange_sems(nr),
        compiler_params=pltpu.CompilerParams(vmem_limit_bytes=VMEM_LIMIT),
    )(*halves, *replicated)
    return out[:n], out[n:]


def _pack_cols(pieces):
    offs, pos = [], 0
    for a in pieces:
        offs.append(pos)
        pos += a.shape[1]
    rows8 = [jnp.pad(a.astype(F32), ((0, 8 - a.shape[0]), (0, 0))) for a in pieces]
    return jnp.concatenate(rows8, axis=1), offs


def adamw_many(ws, gs, ms, vs):
    n = len(ws)

    def body(*refs):
        w_r, g_r, m_r, v_r = refs[:n], refs[n:2 * n], refs[2 * n:3 * n], refs[3 * n:4 * n]
        d_o, m_o, v_o = refs[4 * n:5 * n], refs[5 * n:6 * n], refs[6 * n:7 * n]
        for i in range(n):
            gg = g_r[i][...]
            nm = ADAM_B1 * m_r[i][...] + (1.0 - ADAM_B1) * gg
            nv = ADAM_B2 * v_r[i][...] + (1.0 - ADAM_B2) * (gg * gg)
            m_hat = nm / (1.0 - ADAM_B1 ** ADAM_STEP)
            v_hat = nv / (1.0 - ADAM_B2 ** ADAM_STEP)
            d_o[i][...] = -ADAM_LR * (m_hat / (jnp.sqrt(v_hat) + ADAM_EPS) + ADAM_WD * w_r[i][...])
            m_o[i][...] = nm
            v_o[i][...] = nv

    shapes = [jax.ShapeDtypeStruct(w.shape, F32) for w in ws]
    out = pl.pallas_call(body, name="adamw_small", out_shape=shapes * 3,
                         compiler_params=pltpu.CompilerParams(vmem_limit_bytes=VMEM_LIMIT))(*ws, *gs, *ms, *vs)
    return out[:n], out[n:2 * n], out[2 * n:]


def _lanes(vec, start):
    n = vec.shape[-1]
    return jnp.pad(vec.reshape(1, n).astype(F32), ((0, 0), (start, LANES - start - n)))


def kernel(x, norm_w, w_in, ssd_conv_w, ssd_conv_b, ssd_dt_bias, ssd_a_log, ssd_d, ssd_norm_w, gdn_conv_w, gdn_dt_bias, gdn_a_log, gdn_norm_w, w_out, final_norm_w, loss_target, m_norm_w, m_w_in, m_ssd_conv_w, m_ssd_conv_b, m_ssd_dt_bias, m_ssd_a_log, m_ssd_d, m_ssd_norm_w, m_gdn_conv_w, m_gdn_dt_bias, m_gdn_a_log, m_gdn_norm_w, m_w_out, m_final_norm_w, v_norm_w, v_w_in, v_ssd_conv_w, v_ssd_conv_b, v_ssd_dt_bias, v_ssd_a_log, v_ssd_d, v_ssd_norm_w, v_gdn_conv_w, v_gdn_dt_bias, v_gdn_a_log, v_gdn_norm_w, v_w_out, v_final_norm_w):
    xs = x[0]
    target = loss_target[0]
    chip = 2 * lax.axis_index("x") + lax.axis_index("y")
    w_in_shard, w_out_shard = w_in[0], w_out[0]
    in_cols = w_in_shard.shape[1]
    out_rows = w_out_shard.shape[0]

    g_in, g_out, g_cs, g_cg = gather_weights(
        [w_in_shard.astype(MXU_DTYPE), w_out_shard.astype(MXU_DTYPE)], [ssd_conv_w[0], gdn_conv_w[0]])
    w_in_full = jnp.transpose(g_in, (1, 0, 2)).reshape(D_MODEL, IN_DIM)
    w_out_full = g_out.reshape(N_CHIP * out_rows, D_MODEL)
    cw_ssd = _ssd_perm(jnp.transpose(g_cs, (1, 0, 2)).reshape(4, SSD_CONV))
    cw_gdn = _gdn_perm(jnp.transpose(g_cg, (1, 0, 2)).reshape(4, GDN_CONV))
    cb_ssd = _ssd_perm(ssd_conv_b)
    cb_gdn = jnp.zeros((1, GDN_CONV), F32)
    o_xbc, o_dt, o_gate, o_qkv, o_ab = 1024, 2560, 2576, 3600, 6672
    w_main = jnp.concatenate([w_in_full[:, :o_xbc], w_in_full[:, o_gate:o_qkv], _ssd_perm(w_in_full[:, o_xbc:o_dt]),
                              _gdn_perm(w_in_full[:, o_qkv:o_ab])], axis=1)
    w_small = jnp.concatenate([w_in_full[:, o_dt:o_gate], w_in_full[:, o_ab:],
                               jnp.zeros((D_MODEL, LANES - 32), MXU_DTYPE)], axis=1)
    alog = _lanes(ssd_a_log, 0) + _lanes(gdn_a_log, LANE_GA)
    dtb = _lanes(ssd_dt_bias, 0) + _lanes(gdn_dt_bias, LANE_GA)
    dvec = _lanes(ssd_d, 0)
    fw = final_norm_w.reshape(1, D_MODEL)

    proj_main, proj_small, u = in_proj(xs, norm_w, w_main, w_small)
    conv_ssd = conv_fwd(proj_main, COL_SSD, SSD_CONV, cw_ssd, cb_ssd, "conv_fwd_ssd")
    conv_gdn = conv_fwd(proj_main, COL_GDN, GDN_CONV, cw_gdn, cb_gdn, "conv_fwd_gdn")
    y_ssd, hist_ssd = ssd_fwd(conv_ssd, proj_main, proj_small, ssd_norm_w, alog, dtb, dvec)
    y_gdn, hist_gdn, tinv_gdn = gdn_fwd(conv_gdn, proj_main, proj_small, gdn_norm_w, alog, dtb)

    loss_blk, dhid, dy_ssd, dy_gdn, d_w_out, d_fw = out_proj_loss(xs, y_ssd, y_gdn, w_out_full, fw, target)
    dconv_ssd, dproj_main, dsmall_ssd, d_ssd_nw, d_alog_s, d_dtb_s, d_dvec = ssd_bwd(
        conv_ssd, proj_main, proj_small, ssd_norm_w, alog, dtb, dvec, hist_ssd, dy_ssd)
    dproj_main, dconv_gdn, dsmall_gdn, d_gdn_nw, d_alog_g, d_dtb_g = gdn_bwd(
        dproj_main, conv_gdn, proj_main, proj_small, gdn_norm_w, alog, dtb, hist_gdn, tinv_gdn, dy_gdn)
    dproj_main, dwb_ssd = conv_bwd(dproj_main, proj_main, COL_SSD, SSD_CONV, cw_ssd, cb_ssd, dconv_ssd, "conv_bwd_ssd")
    dproj_main, dwb_gdn = conv_bwd(dproj_main, proj_main, COL_GDN, GDN_CONV, cw_gdn, cb_gdn, dconv_gdn, "conv_bwd_gdn")
    d_w_main, d_w_small = in_proj_bwd_w(u, dproj_main, dsmall_ssd, dsmall_gdn)

    d_w_in = jnp.concatenate([d_w_main[:, :COL_GATE], _ssd_unperm(d_w_main[:, COL_SSD:COL_GDN]), d_w_small[:, 0:16],
                              d_w_main[:, COL_GATE:COL_SSD], _gdn_unperm(d_w_main[:, COL_GDN:]), d_w_small[:, 16:32]],
                             axis=1)
    d_w_in = jnp.transpose(d_w_in.reshape(D_MODEL, N_CHIP, in_cols), (1, 0, 2))
    slabs = [d_w_in.reshape(N_DEV, D_MODEL // 2, in_cols).astype(COMM_DTYPE),
             d_w_out.reshape(N_DEV, out_rows // 2, D_MODEL).astype(COMM_DTYPE)]

    grad_x, d_norm_w, (r_in, r_out) = in_proj_bwd_x(xs, norm_w, w_main, w_small, dproj_main, dsmall_ssd, dsmall_gdn,
                                                     dhid, slabs)
    d_alog, d_dtb = d_alog_s + d_alog_g, d_dtb_s + d_dtb_g
    packed, (o_nw, o_cs, o_cg, o_snw, o_fw, o_al, o_db, o_dv, o_gnw, o_loss) = _pack_cols([
        d_norm_w, _ssd_unperm(dwb_ssd), _gdn_unperm(dwb_gdn),
        d_ssd_nw.reshape(1, SSD_WIDTH), d_fw, d_alog, d_dtb, d_dvec, d_gdn_nw, loss_blk])

    half_in = sum_slabs(r_in, "sum_w_in")
    half_out = sum_slabs(r_out, "sum_w_out")
    (full_in, full_out), (r_small,) = exchange_halves([half_in, half_out], [packed])
    tot = sum_slabs(r_small, "sum_small")
    grad_w_in = full_in.reshape(D_MODEL, in_cols)
    grad_w_out = full_out.reshape(out_rows, D_MODEL)
    loss = tot[0, o_loss]
    sc, gc = ssd_conv_w.shape[2], gdn_conv_w.shape[2]
    row = lambda off, n, r=0: tot[r:r + 1, off:off + n]
    gs = [row(o_nw, D_MODEL),
          lax.dynamic_slice(tot, (0, o_cs + chip * sc), (4, sc)),
          row(o_cs, SSD_CONV, 4),
          row(o_db, SSD_HEADS), row(o_al, SSD_HEADS), row(o_dv, SSD_HEADS),
          row(o_snw, SSD_WIDTH),
          lax.dynamic_slice(tot, (0, o_cg + chip * gc), (4, gc)),
          row(o_db + LANE_GA, GDN_HEADS), row(o_al + LANE_GA, GDN_HEADS),
          row(o_gnw, GDN_DV), row(o_fw, D_MODEL)]

    names = ["norm_w", "ssd_conv_w", "ssd_conv_b", "ssd_dt_bias", "ssd_a_log", "ssd_d", "ssd_norm_w", "gdn_conv_w",
             "gdn_dt_bias", "gdn_a_log", "gdn_norm_w", "final_norm_w"]
    ws = [norm_w, ssd_conv_w, ssd_conv_b, ssd_dt_bias, ssd_a_log, ssd_d, ssd_norm_w, gdn_conv_w, gdn_dt_bias,
          gdn_a_log, gdn_norm_w, final_norm_w]
    ms = [m_norm_w, m_ssd_conv_w, m_ssd_conv_b, m_ssd_dt_bias, m_ssd_a_log, m_ssd_d, m_ssd_norm_w, m_gdn_conv_w,
          m_gdn_dt_bias, m_gdn_a_log, m_gdn_norm_w, m_final_norm_w]
    vs = [v_norm_w, v_ssd_conv_w, v_ssd_conv_b, v_ssd_dt_bias, v_ssd_a_log, v_ssd_d, v_ssd_norm_w, v_gdn_conv_w,
          v_gdn_dt_bias, v_gdn_a_log, v_gdn_norm_w, v_final_norm_w]
    shapes = [w.shape for w in ws]
    flat = lambda arrs: [a.reshape(g.shape) for a, g in zip(arrs, gs)]
    d_s, m_s, v_s = adamw_many(flat(ws), gs, flat(ms), flat(vs))
    back = lambda arrs: dict(zip(names, [a.reshape(s) for a, s in zip(arrs, shapes)]))
    delta, new_m, new_v, grads = back(d_s), back(m_s), back(v_s), back(gs)
    d_in, m_in, v_in = adamw(w_in_shard, grad_w_in, m_w_in[0], v_w_in[0], "adamw_w_in")
    d_out, m_out, v_out = adamw(w_out_shard, grad_w_out, m_w_out[0], v_w_out[0], "adamw_w_out")
    for tbl, a_in, a_out in ((grads, grad_w_in, grad_w_out), (delta, d_in, d_out), (new_m, m_in, m_out),
                             (new_v, v_in, v_out)):
        tbl["w_in"] = a_in[None]
        tbl["w_out"] = a_out[None]

    order = ["norm_w", "w_in", "ssd_conv_w", "ssd_conv_b", "ssd_dt_bias", "ssd_a_log", "ssd_d", "ssd_norm_w",
             "gdn_conv_w", "gdn_dt_bias", "gdn_a_log", "gdn_norm_w", "w_out", "final_norm_w"]
    return (loss.reshape(()), grad_x[None], *[grads[k] for k in order], *[delta[k] for k in order],
            *[new_m[k] for k in order], *[new_v[k] for k in order])
```

```python
import functools

import jax
import jax.numpy as jnp
from jax import lax
from jax.experimental import pallas as pl
from jax.experimental.pallas import tpu as pltpu

F32 = jnp.float32
MXU_DTYPE = jnp.bfloat16
COMM_DTYPE = jnp.bfloat16
MESH = pl.DeviceIdType.MESH

D_MODEL = 1024
CHUNK = 64
EPS = 1e-6
SSD_HEADS, SSD_GROUPS, SSD_STATE = 16, 2, 128
SSD_WIDTH, SSD_CONV = 1024, 1536
SSD_GW = SSD_WIDTH // SSD_GROUPS
SSD_GC = SSD_GW + 2 * SSD_STATE
GDN_HEADS, GDN_DK, GDN_DV = 8, 128, 128
GDN_W, GDN_CONV = 1024, 3072
GDN_HC = 2 * GDN_DK + GDN_DV
IN_DIM = 6688
MAIN = 6656
LANES = 128
COL_Z, COL_GATE, COL_GDN, COL_SSD = 0, 1024, 2048, 5120
COL_CONV = COL_GDN
GDN_HB = 8
LANE_GA, LANE_GB = 16, 24
N_DEV, N_CHIP = 8, 4
VMEM_LIMIT = 52 * 1024 * 1024

ADAM_LR, ADAM_B1, ADAM_B2, ADAM_EPS, ADAM_WD, ADAM_STEP = 0.001, 0.9, 0.999, 1e-08, 0.01, 10


def _ssd_perm(a):
    lead, nb = a.shape[:-1], SSD_GROUPS * SSD_STATE
    x = a[..., :SSD_WIDTH].reshape(*lead, SSD_GROUPS, SSD_GW)
    b = a[..., SSD_WIDTH:SSD_WIDTH + nb].reshape(*lead, SSD_GROUPS, SSD_STATE)
    c = a[..., SSD_WIDTH + nb:].reshape(*lead, SSD_GROUPS, SSD_STATE)
    return jnp.concatenate([x, b, c], axis=-1).reshape(*lead, SSD_CONV)


def _ssd_unperm(a):
    lead = a.shape[:-1]
    g = a.reshape(*lead, SSD_GROUPS, SSD_GC)
    parts = [g[..., :SSD_GW], g[..., SSD_GW:SSD_GW + SSD_STATE], g[..., SSD_GW + SSD_STATE:]]
    return jnp.concatenate([p.reshape(*lead, -1) for p in parts], axis=-1)


def _gdn_perm(a):
    lead = a.shape[:-1]
    return jnp.swapaxes(a.reshape(*lead, 3, GDN_HEADS, GDN_DK), -3, -2).reshape(*lead, GDN_CONV)


def _gdn_unperm(a):
    lead = a.shape[:-1]
    return jnp.swapaxes(a.reshape(*lead, GDN_HEADS, 3, GDN_DK), -3, -2).reshape(*lead, GDN_CONV)


def _split(a, n):
    parts, rest = [], a.astype(F32)
    for i in range(n):
        p = rest.astype(MXU_DTYPE)
        parts.append(p)
        if i < n - 1:
            rest = rest - p.astype(F32)
    return parts


def _raw_dot(a, b, ca, cb, mode="bf16"):
    d = lambda u, v: lax.dot_general(u, v, (((ca,), (cb,)), ((), ())), preferred_element_type=F32)
    if mode == "bf16":
        return d(a.astype(MXU_DTYPE), b.astype(MXU_DTYPE))
    if mode == "x3":
        (ah, al), (bh, bl) = _split(a, 2), _split(b, 2)
        return d(ah, bh) + (d(ah, bl) + d(al, bh))
    if mode == "sel_a":
        a0 = a.astype(MXU_DTYPE)
        b1, b2, b3 = _split(b, 3)
        return d(a0, b1) + (d(a0, b2) + d(a0, b3))
    assert mode == "sel_b", mode
    b0 = b.astype(MXU_DTYPE)
    a1, a2, a3 = _split(a, 3)
    return d(a1, b0) + (d(a2, b0) + d(a3, b0))


@functools.partial(jax.custom_vjp, nondiff_argnums=(2,))
def mm_nn(a, b, mode="bf16"):
    return _raw_dot(a, b, 1, 0, mode)


@functools.partial(jax.custom_vjp, nondiff_argnums=(2,))
def mm_nt(a, b, mode="bf16"):
    return _raw_dot(a, b, 1, 1, mode)


@functools.partial(jax.custom_vjp, nondiff_argnums=(2,))
def mm_tn(a, b, mode="bf16"):
    return _raw_dot(a, b, 0, 0, mode)


_SAME = {"bf16": ("bf16", "bf16"), "x3": ("x3", "x3")}
_NN_BWD = dict(_SAME, sel_a=("bf16", "sel_a"), sel_b=("sel_b", "bf16"))
_NT_BWD = dict(_SAME, sel_a=("bf16", "sel_b"), sel_b=("sel_b", "bf16"))
_TN_BWD = dict(_SAME, sel_a=("bf16", "sel_a"), sel_b=("sel_a", "bf16"))
mm_nn.defvjp(lambda a, b, m: (_raw_dot(a, b, 1, 0, m), (a, b)),
             lambda m, r, g: (mm_nt(g, r[1], _NN_BWD[m][0]), mm_tn(r[0], g, _NN_BWD[m][1])))
mm_nt.defvjp(lambda a, b, m: (_raw_dot(a, b, 1, 1, m), (a, b)),
             lambda m, r, g: (mm_nn(g, r[1], _NT_BWD[m][0]), mm_tn(g, r[0], _NT_BWD[m][1])))
mm_tn.defvjp(lambda a, b, m: (_raw_dot(a, b, 0, 0, m), (a, b)),
             lambda m, r, g: (mm_nt(r[1], g, _TN_BWD[m][0]), mm_nn(r[0], g, _TN_BWD[m][1])))


@jax.custom_jvp
def sigmoid(x):
    return 1.0 / (1.0 + jnp.exp(-x))


@sigmoid.defjvp
def _sigmoid_jvp(p, t):
    s = sigmoid(p[0])
    return s, t[0] * s * (1.0 - s)


@jax.custom_jvp
def softplus(x):
    return jnp.maximum(x, 0.0) + jnp.log(1.0 + jnp.exp(-jnp.abs(x)))


@softplus.defjvp
def _softplus_jvp(p, t):
    return softplus(p[0]), t[0] * sigmoid(p[0])


def silu(x):
    return x * sigmoid(x)


def rmsnorm(x, w):
    return x * lax.rsqrt(jnp.mean(x * x, axis=-1, keepdims=True) + EPS) * w


def _iota(shape, dim):
    return lax.broadcasted_iota(jnp.int32, shape, dim)


def _tri_inv_impl(mats):
    n = mats[0].shape[0]
    r, c = _iota((n, n), 0), _iota((n, n), 1)
    eye = jnp.where(r == c, 1.0, 0.0).astype(F32)
    blockdiag = (r >> 4) == (c >> 4)
    dot = lambda u, v: _raw_dot(u, v, 1, 0, "x3")
    dot1 = lambda u, v: _raw_dot(u, v, 1, 0)
    each = lambda f, *ls: [f(*xs) for xs in zip(*ls)]
    dg = each(lambda a: jnp.where(blockdiag, a, 0.0), mats)
    off = each(lambda a, d: a - d, mats, dg)
    m = each(lambda d: -d, dg)
    p = each(lambda x: eye + x, m)
    pw = m
    for _ in range(3):
        pw = each(lambda x: dot1(x, x), pw)
        p = each(lambda x, y: x + dot1(x, y), p, pw)
    e = each(dot, p, off)
    e2 = each(lambda x: dot1(x, x), e)
    q = each(lambda x: eye - x, e)
    q = each(lambda x, y: x + dot1(x, y), q, e2)
    return each(dot, q, p)


def _tri_inv_bwd(ts, gs):
    x = [mm_nt(g, t) for g, t in zip(gs, ts)]
    return [-mm_tn(t, y) for t, y in zip(ts, x)]


@jax.custom_vjp
def tri_inv(mats):
    return _tri_inv_impl(mats)


def _tri_inv_fwd(mats):
    ts = _tri_inv_impl(mats)
    return ts, ts


tri_inv.defvjp(_tri_inv_fwd, lambda ts, gs: (_tri_inv_bwd(ts, gs),))


@jax.custom_vjp
def tri_inv_saved(mats, ts):
    del mats
    return ts


tri_inv_saved.defvjp(lambda mats, ts: (ts, ts),
                     lambda ts, gs: (_tri_inv_bwd(ts, gs), [jnp.zeros_like(t) for t in ts]))


def _chunk_masks():
    r, c = _iota((CHUNK, CHUNK), 0), _iota((CHUNK, CHUNK), 1)
    return r >= c, r > c, r == c


def _log_decay_cumsum(small, alog, dtb, incl):
    sp = softplus(small + dtb)
    la = -jnp.exp(alog) * sp
    tri = jnp.where(incl, 1.0, 0.0).astype(F32)
    return sp, mm_nn(tri, la, "sel_a")


def _col_of(x, lane_mask):
    return jnp.sum(jnp.where(lane_mask, x, 0.0), axis=1, keepdims=True)


def _decay_matrix(col, incl, eye):
    row = jnp.sum(jnp.where(eye, col, 0.0), axis=0, keepdims=True)
    return jnp.where(incl, jnp.exp(jnp.where(incl, col - row, 0.0)), 0.0)


def gdn_chunk(h0, qs, ks, vs, small, gates, normw, alog, dtb, states, saved_t=None):
    incl, strict, eye = _chunk_masks()
    lane = _iota((1, LANES), 1)
    last = _iota((CHUNK, 1), 0) == CHUNK - 1
    _, lac = _log_decay_cumsum(small, alog, dtb, incl)
    heads = range(len(qs))
    each = lambda f, *ls: [f(*xs) for xs in zip(*ls)]
    gc = [_col_of(lac, lane == LANE_GA + h0 + j) for j in heads]
    beta = [sigmoid(_col_of(small, lane == LANE_GB + h0 + j)) for j in heads]
    decay = each(lambda x: _decay_matrix(x, incl, eye), gc)
    gl = each(lambda x: jnp.sum(jnp.where(last, x, 0.0), axis=0, keepdims=True), gc)
    q = each(lambda x: x * lax.rsqrt(jnp.sum(x * x, axis=-1, keepdims=True) + EPS) * (GDN_DK ** -0.5), qs)
    k = each(lambda x: x * lax.rsqrt(jnp.sum(x * x, axis=-1, keepdims=True) + EPS), ks)
    kb = each(lambda x, b: x * b, k, beta)
    a = each(lambda x, y, d: jnp.where(strict, mm_nt(x, y) * d, 0.0), kb, k, decay)
    t = tri_inv(a) if saved_t is None else tri_inv_saved(a, saved_t)
    eg = each(jnp.exp, gc)
    u = each(lambda x, v, b: mm_nn(x, v * b), t, vs, beta)
    w = each(lambda x, y, e: mm_nn(x, y * e), t, kb, eg)
    attn = each(lambda x, y, d: mm_nt(x, y) * d, q, k, decay)
    v_new = each(lambda x, y, s: x - mm_nn(y, s), u, w, states)
    o = each(lambda x, e, s, at, vn: mm_nn(x * e, s) + mm_nn(at, vn), q, eg, states, attn, v_new)
    new_states = each(lambda s, x, y, l, c: s * jnp.exp(l) + mm_tn(y * jnp.exp(l - c), x), states, v_new, k, gl, gc)
    ys = each(lambda x, gt: rmsnorm(x, normw) * silu(gt), o, gates)
    return ys, new_states, t


@jax.custom_vjp
def split_lanes(x):
    return [x[:, i * LANES:(i + 1) * LANES] for i in range(x.shape[1] // LANES)]


@jax.custom_vjp
def join_lanes(xs):
    return jnp.concatenate(xs, axis=1)


split_lanes.defvjp(lambda x: (split_lanes(x), None), lambda _, gs: (join_lanes(gs),))
join_lanes.defvjp(lambda xs: (join_lanes(xs), None), lambda _, g: (split_lanes(g),))


def ssd_chunk(xs, bm, cm, z, small, normw, alog, dtb, dvec, state):
    incl, _, eye = _chunk_masks()
    lane = _iota((1, LANES), 1)
    last = _iota((CHUNK, 1), 0) == CHUNK - 1
    hpg = SSD_HEADS // SSD_GROUPS
    groups = range(len(xs))
    each = lambda f, *ls: [f(*a) for a in zip(*ls)]
    sp, lac = _log_decay_cumsum(small, alog, dtb, incl)
    lac_last = jnp.sum(jnp.where(last, lac, 0.0), axis=0, keepdims=True)
    sel = [jnp.where(_iota((LANES, SSD_GW), 0) == g * hpg + (_iota((LANES, SSD_GW), 1) >> 6), 1.0, 0.0).astype(F32)
           for g in groups]
    expand = lambda v: [mm_nn(v, s, "sel_b") for s in sel]
    dt_e, elac_e, toend_e = expand(sp), expand(jnp.exp(lac)), expand(jnp.exp(lac_last - lac))
    row8, row8e = _iota((8, LANES), 0), _iota((8, SSD_GW), 0)
    two_e = expand(jnp.where(row8 == 0, dvec, 0.0) + jnp.where(row8 == 1, jnp.exp(lac_last), 0.0))
    d_e = each(lambda v: jnp.sum(jnp.where(row8e == 0, v, 0.0), axis=0, keepdims=True), two_e)
    chunk_e = each(lambda v: jnp.sum(jnp.where(row8e == 1, v, 0.0), axis=0, keepdims=True), two_e)
    xdt = each(lambda a, b: a * b, xs, dt_e)
    cb = each(mm_nt, cm, bm)
    y = each(lambda c_, st, el, x_, d_: mm_nn(c_, st) * el + x_ * d_, cm, state, elac_e, xs, d_e)
    x_pairs = each(split_lanes, xdt)
    half = _iota((1, LANES), 1) >> 6
    lms = [[_decay_matrix(_col_of(lac, lane == g * hpg + j), incl, eye) for j in range(hpg)] for g in groups]
    terms = [[mm_nn(cb[g] * lms[g][j], jnp.where(half == j % 2, x_pairs[g][j // 2], 0.0)) for j in range(hpg)]
             for g in groups]
    y = [y[g] + join_lanes([terms[g][2 * p] + terms[g][2 * p + 1] for p in range(hpg // 2)]) for g in groups]
    new_state = each(lambda st, ce, b_, xd, te: st * ce + mm_tn(b_, xd * te), state, chunk_e, bm, xdt, toend_e)
    out = each(lambda y_, z_, nw: rmsnorm(y_ * silu(z_), nw), y, z, normw)
    return out, new_state


def _params(sem=None):
    return pltpu.CompilerParams(dimension_semantics=sem, vmem_limit_bytes=VMEM_LIMIT)


def _full(shape):
    n = len(shape)
    return pl.BlockSpec(shape, lambda *_: (0,) * n)


ANY = pl.BlockSpec(memory_space=pl.ANY)
HBM = pl.BlockSpec(memory_space=pltpu.HBM)


def in_proj(x, normw, w_main, w_small):
    t = x.shape[0]
    tm, tn = min(1024, t), 512

    def body(x_ref, nw_ref, wm_ref, ws_ref, pm_ref, ps_ref, u_ref):
        @pl.when(pl.program_id(1) == 0)
        def _():
            u = rmsnorm(x_ref[...], nw_ref[...]).astype(MXU_DTYPE)
            u_ref[...] = u
            ps_ref[...] = _raw_dot(u, ws_ref[...], 1, 0)
        pm_ref[...] = _raw_dot(u_ref[...], wm_ref[...], 1, 0)

    return pl.pallas_call(
        body, name="in_proj", grid=(t // tm, COL_CONV // tn),
        in_specs=[pl.BlockSpec((tm, D_MODEL), lambda i, j: (i, 0)), _full((1, D_MODEL)),
                  pl.BlockSpec((D_MODEL, tn), lambda i, j: (0, j)), _full((D_MODEL, LANES))],
        out_specs=[pl.BlockSpec((tm, tn), lambda i, j: (i, j)), pl.BlockSpec((tm, LANES), lambda i, j: (i, 0)),
                   pl.BlockSpec((tm, D_MODEL), lambda i, j: (i, 0))],
        out_shape=[jax.ShapeDtypeStruct((t, MAIN), F32), jax.ShapeDtypeStruct((t, LANES), F32),
                   jax.ShapeDtypeStruct((t, D_MODEL), MXU_DTYPE)],
        compiler_params=_params(("arbitrary", "arbitrary")),
    )(x, normw, w_main, w_small)


CONV_TC = 512
HALO = 8


def _shift_down(cur, prev, s):
    rolled = pltpu.roll(cur, s, 0)
    top = jnp.where(_iota((HALO, cur.shape[1]), 0) < s, pltpu.roll(prev, s, 0), rolled[:HALO])
    if cur.shape[0] == HALO:
        return top
    return jnp.concatenate([top, rolled[HALO:]], axis=0)


def _shift_up(cur, nxt, s):
    n = cur.shape[0]
    rolled = pltpu.roll(cur, n - s, 0)
    bot = jnp.where(_iota((HALO, cur.shape[1]), 0) >= HALO - s, pltpu.roll(nxt, HALO - s, 0), rolled[n - HALO:])
    return jnp.concatenate([rolled[:n - HALO], bot], axis=0)


def _conv_pre(cur, prev, w_ref, b):
    acc = cur * w_ref[3:4, :] + b
    shifted = [cur]
    for s in (1, 2, 3):
        sh = _shift_down(cur, prev, s)
        shifted.append(sh)
        acc = acc + sh * w_ref[3 - s:4 - s, :]
    return acc, shifted


def in_proj_conv(proj_main, u, w_main, w, b):
    t = u.shape[0]
    tm, tn = min(1024, t), CONV_TC
    rc = min(256, tm)
    c0, nj = COL_CONV // tn, (MAIN - COL_CONV) // tn

    def body(alias_ref, u_ref, wm_ref, w_ref, b_ref, pm_ref, out_ref, halo_ref):
        del alias_ref
        j = pl.program_id(1)

        @pl.when(pl.program_id(0) == 0)
        def _():
            halo_ref[j] = jnp.zeros((HALO, tn), F32)

        prev = halo_ref[j]
        for r in range(tm // rc):
            rows = pl.ds(r * rc, rc)
            p = _raw_dot(u_ref[rows, :], wm_ref[...], 1, 0)
            pm_ref[rows, :] = p
            pre, _ = _conv_pre(p, prev, w_ref, b_ref[...])
            out_ref[rows, :] = silu(pre)
            prev = p[rc - HALO:]
        halo_ref[j] = prev

    return pl.pallas_call(
        body, name="in_proj_conv", grid=(t // tm, nj),
        in_specs=[ANY, pl.BlockSpec((tm, D_MODEL), lambda i, j: (i, 0)),
                  pl.BlockSpec((D_MODEL, tn), lambda i, j: (0, c0 + j)),
                  pl.BlockSpec((4, tn), lambda i, j: (0, j)), pl.BlockSpec((1, tn), lambda i, j: (0, j))],
        out_specs=[pl.BlockSpec((tm, tn), lambda i, j: (i, c0 + j)), pl.BlockSpec((tm, tn), lambda i, j: (i, j))],
        out_shape=[jax.ShapeDtypeStruct(proj_main.shape, F32), jax.ShapeDtypeStruct((t, MAIN - COL_CONV), F32)],
        scratch_shapes=[pltpu.VMEM((nj, HALO, tn), F32)],
        input_output_aliases={0: 0},
        compiler_params=_params(("arbitrary", "arbitrary")),
    )(proj_main, u, w_main, w, b)


def _dsilu(pre):
    sg = sigmoid(pre)
    return sg * (1.0 + pre * (1.0 - sg))


def conv_bwd(dproj_main, proj_main, col0, width, w, b, dout, name):
    t = proj_main.shape[0]
    tt, c0 = min(512, t), col0 // CONV_TC
    nt = t // tt
    after = lambda i: jnp.minimum((i + 1) * (tt // HALO), t // HALO - 1)

    def body(alias_ref, cur_ref, prev_ref, nxt_ref, w_ref, b_ref, do_ref, do_nxt_ref, dx_ref, dwb_ref):
        del alias_ref
        i = pl.program_id(1)
        cur, bias = cur_ref[...], b_ref[...]
        prev = jnp.where(i > 0, prev_ref[...], 0.0)
        pre, shifted = _conv_pre(cur, prev, w_ref, bias)
        dpre = do_ref[...] * _dsilu(pre)
        pre_nxt, _ = _conv_pre(nxt_ref[...], cur[tt - HALO:], w_ref, bias)
        dpre_nxt = jnp.where(i < nt - 1, do_nxt_ref[...] * _dsilu(pre_nxt), 0.0)
        dx = dpre * w_ref[3:4, :]
        for s in (1, 2, 3):
            dx = dx + _shift_up(dpre, dpre_nxt, s) * w_ref[3 - s:4 - s, :]
        dx_ref[...] = dx.astype(dx_ref.dtype)
        row = _iota((HALO, CONV_TC), 0)
        upd = jnp.where(row == 4, jnp.sum(dpre, axis=0, keepdims=True), 0.0)
        for s in range(4):
            upd = upd + jnp.where(row == 3 - s, jnp.sum(dpre * shifted[s], axis=0, keepdims=True), 0.0)
        _accumulate(dwb_ref, i == 0, upd)

    return pl.pallas_call(
        body, name=name, grid=(width // CONV_TC, nt),
        in_specs=[ANY, pl.BlockSpec((tt, CONV_TC), lambda j, i: (i, c0 + j)),
                  pl.BlockSpec((HALO, CONV_TC), lambda j, i: (jnp.maximum(i * (tt // HALO) - 1, 0), c0 + j)),
                  pl.BlockSpec((HALO, CONV_TC), lambda j, i: (after(i), c0 + j)),
                  pl.BlockSpec((4, CONV_TC), lambda j, i: (0, j)), pl.BlockSpec((1, CONV_TC), lambda j, i: (0, j)),
                  pl.BlockSpec((tt, CONV_TC), lambda j, i: (i, j)),
                  pl.BlockSpec((HALO, CONV_TC), lambda j, i: (after(i), j))],
        out_specs=[pl.BlockSpec((tt, CONV_TC), lambda j, i: (i, c0 + j)),
                   pl.BlockSpec((HALO, CONV_TC), lambda j, i: (0, j))],
        out_shape=[jax.ShapeDtypeStruct(dproj_main.shape, dproj_main.dtype), jax.ShapeDtypeStruct((HALO, width), F32)],
        input_output_aliases={0: 0},
        compiler_params=_params(("arbitrary", "arbitrary")),
    )(dproj_main, proj_main, proj_main, proj_main, w, b, dout, dout)


def _ssd_parts(xbc_ref):
    part = lambda o, w: [xbc_ref[:, g * SSD_GC + o:g * SSD_GC + o + w] for g in range(SSD_GROUPS)]
    return part(0, SSD_GW), part(SSD_GW, SSD_STATE), part(SSD_GW + SSD_STATE, SSD_STATE)


def _group_cols(ref):
    return [ref[:, g * SSD_GW:(g + 1) * SSD_GW] for g in range(SSD_GROUPS)]


def _gdn_parts(qkv_ref):
    part = lambda o: [qkv_ref[:, j * GDN_HC + o:j * GDN_HC + o + GDN_DK] for j in range(GDN_HB)]
    return part(0), part(GDN_DK), part(2 * GDN_DK)


def _head_cols(ref):
    return [ref[:, j * GDN_DV:(j + 1) * GDN_DV] for j in range(GDN_HB)]


def _first_head():
    return 0 if GDN_HB == GDN_HEADS else pl.program_id(1) * GDN_HB


def ssd_fwd(conv_ssd, proj_main, proj_small, normw, alog, dtb, dvec):
    t = conv_ssd.shape[0]
    nc = t // CHUNK

    groups = range(SSD_GROUPS)

    def body(xbc_ref, z_ref, sm_ref, nw_ref, al_ref, db_ref, dv_ref, y_ref, hist_ref, state_ref):
        @pl.when(pl.program_id(0) == 0)
        def _():
            state_ref[...] = jnp.zeros(state_ref.shape, F32)

        states = [state_ref[g] for g in groups]
        for g in groups:
            hist_ref[0, g] = states[g]
        ys, new_states = ssd_chunk(*_ssd_parts(xbc_ref), _group_cols(z_ref), sm_ref[...], _group_cols(nw_ref),
                                   al_ref[...], db_ref[...], dv_ref[...], states)
        for g in groups:
            y_ref[:, g * SSD_GW:(g + 1) * SSD_GW] = ys[g].astype(MXU_DTYPE)
            state_ref[g] = new_states[g]

    return pl.pallas_call(
        body, name="ssd_fwd", grid=(nc,),
        in_specs=[pl.BlockSpec((CHUNK, SSD_CONV), lambda c: (c, (COL_SSD - COL_CONV) // SSD_CONV)),
                  pl.BlockSpec((CHUNK, SSD_WIDTH), lambda c: (c, COL_Z // SSD_WIDTH)),
                  pl.BlockSpec((CHUNK, LANES), lambda c: (c, 0)),
                  _full((1, SSD_WIDTH)), _full((1, LANES)), _full((1, LANES)), _full((1, LANES))],
        out_specs=[pl.BlockSpec((CHUNK, SSD_WIDTH), lambda c: (c, 0)),
                   pl.BlockSpec((1, SSD_GROUPS, SSD_STATE, SSD_GW), lambda c: (c, 0, 0, 0))],
        out_shape=[jax.ShapeDtypeStruct((t, SSD_WIDTH), MXU_DTYPE),
                   jax.ShapeDtypeStruct((nc, SSD_GROUPS, SSD_STATE, SSD_GW), F32)],
        scratch_shapes=[pltpu.VMEM((SSD_GROUPS, SSD_STATE, SSD_GW), F32)],
        compiler_params=_params(("arbitrary",)),
    )(conv_ssd, proj_main, proj_small, normw, alog, dtb, dvec)


def _accumulate(ref, first, value):
    @pl.when(first)
    def _():
        ref[...] = value

    @pl.when(jnp.logical_not(first))
    def _():
        ref[...] += value


def ssd_bwd(conv_ssd, proj_main, proj_small, normw, alog, dtb, dvec, hist, dy):
    t = conv_ssd.shape[0]
    nc = t // CHUNK
    rev = lambda c: nc - 1 - c
    groups = range(SSD_GROUPS)

    def body(xbc_ref, z_ref, sm_ref, nw_ref, al_ref, db_ref, dv_ref, hist_ref, dy_ref,
             dxbc_ref, dz_ref, dsm_ref, dnw_ref, dal_ref, ddb_ref, ddv_ref, dstate_ref):
        first = pl.program_id(0) == 0

        @pl.when(first)
        def _():
            dstate_ref[...] = jnp.zeros(dstate_ref.shape, F32)

        _, vjp = jax.vjp(ssd_chunk, *_ssd_parts(xbc_ref), _group_cols(z_ref), sm_ref[...], _group_cols(nw_ref),
                         al_ref[...], db_ref[...], dv_ref[...], [hist_ref[0, g] for g in groups])
        dxs, dbm, dcm, dz, dsm, dnw, dal, ddb, ddv, dstate = vjp(
            (_group_cols(dy_ref), [dstate_ref[g] for g in groups]))
        for g in groups:
            base = g * SSD_GC
            dxbc_ref[:, base:base + SSD_GW] = dxs[g]
            dxbc_ref[:, base + SSD_GW:base + SSD_GW + SSD_STATE] = dbm[g]
            dxbc_ref[:, base + SSD_GW + SSD_STATE:base + SSD_GC] = dcm[g]
            dz_ref[:, g * SSD_GW:(g + 1) * SSD_GW] = dz[g].astype(dz_ref.dtype)
            dstate_ref[g] = dstate[g]
        dsm_ref[...] = dsm
        _accumulate(dnw_ref, first, join_lanes(dnw))
        _accumulate(dal_ref, first, dal)
        _accumulate(ddb_ref, first, ddb)
        _accumulate(ddv_ref, first, ddv)

    return pl.pallas_call(
        body, name="ssd_bwd", grid=(nc,),
        in_specs=[pl.BlockSpec((CHUNK, SSD_CONV), lambda c: (rev(c), (COL_SSD - COL_CONV) // SSD_CONV)),
                  pl.BlockSpec((CHUNK, SSD_WIDTH), lambda c: (rev(c), COL_Z // SSD_WIDTH)),
                  pl.BlockSpec((CHUNK, LANES), lambda c: (rev(c), 0)),
                  _full((1, SSD_WIDTH)), _full((1, LANES)), _full((1, LANES)), _full((1, LANES)),
                  pl.BlockSpec((1, SSD_GROUPS, SSD_STATE, SSD_GW), lambda c: (rev(c), 0, 0, 0)),
                  pl.BlockSpec((CHUNK, SSD_WIDTH), lambda c: (rev(c), 0))],
        out_specs=[pl.BlockSpec((CHUNK, SSD_CONV), lambda c: (rev(c), 0)),
                   pl.BlockSpec((CHUNK, SSD_WIDTH), lambda c: (rev(c), COL_Z // SSD_WIDTH)),
                   pl.BlockSpec((CHUNK, LANES), lambda c: (rev(c), 0)),
                   _full((1, SSD_WIDTH)), _full((1, LANES)), _full((1, LANES)), _full((1, LANES))],
        out_shape=[jax.ShapeDtypeStruct((t, SSD_CONV), F32), jax.ShapeDtypeStruct((t, MAIN), MXU_DTYPE),
                   jax.ShapeDtypeStruct((t, LANES), F32), jax.ShapeDtypeStruct((1, SSD_WIDTH), F32),
                   jax.ShapeDtypeStruct((1, LANES), F32), jax.ShapeDtypeStruct((1, LANES), F32),
                   jax.ShapeDtypeStruct((1, LANES), F32)],
        scratch_shapes=[pltpu.VMEM((SSD_GROUPS, SSD_STATE, SSD_GW), F32)],
        compiler_params=_params(("arbitrary",)),
    )(conv_ssd, proj_main, proj_small, normw, alog, dtb, dvec, hist, dy)


def gdn_fwd(conv_gdn, proj_main, proj_small, normw, alog, dtb):
    t = conv_gdn.shape[0]
    nc = t // CHUNK

    hb = GDN_HB
    gate_blk = COL_GATE // (GDN_DV * hb)

    def body(qkv_ref, gate_ref, sm_ref, nw_ref, al_ref, db_ref, y_ref, hist_ref, t_ref, state_ref):
        h0 = _first_head()

        @pl.when(pl.program_id(0) == 0)
        def _():
            for j in range(hb):
                state_ref[h0 + j] = jnp.zeros((GDN_DK, GDN_DV), F32)

        states = [state_ref[h0 + j] for j in range(hb)]
        for j in range(hb):
            hist_ref[0, j] = states[j]
        qs, ks, vs = _gdn_parts(qkv_ref)
        ys, new_states, ts = gdn_chunk(h0, qs, ks, vs, sm_ref[...], _head_cols(gate_ref), nw_ref[...], al_ref[...],
                                       db_ref[...], states)
        for j in range(hb):
            y_ref[:, j * GDN_DV:(j + 1) * GDN_DV] = ys[j].astype(MXU_DTYPE)
            state_ref[h0 + j] = new_states[j]
            t_ref[0, j] = ts[j]

    return pl.pallas_call(
        body, name="gdn_fwd", grid=(nc, GDN_HEADS // hb),
        in_specs=[pl.BlockSpec((CHUNK, GDN_HC * hb), lambda c, h: (c, h)),
                  pl.BlockSpec((CHUNK, GDN_DV * hb), lambda c, h: (c, gate_blk + h)),
                  pl.BlockSpec((CHUNK, LANES), lambda c, h: (c, 0)),
                  _full((1, GDN_DV)), _full((1, LANES)), _full((1, LANES))],
        out_specs=[pl.BlockSpec((CHUNK, GDN_DV * hb), lambda c, h: (c, h)),
                   pl.BlockSpec((1, hb, GDN_DK, GDN_DV), lambda c, h: (c, h, 0, 0)),
                   pl.BlockSpec((1, hb, CHUNK, CHUNK), lambda c, h: (c, h, 0, 0))],
        out_shape=[jax.ShapeDtypeStruct((t, GDN_W), MXU_DTYPE),
                   jax.ShapeDtypeStruct((nc, GDN_HEADS, GDN_DK, GDN_DV), F32),
                   jax.ShapeDtypeStruct((nc, GDN_HEADS, CHUNK, CHUNK), F32)],
        scratch_shapes=[pltpu.VMEM((GDN_HEADS, GDN_DK, GDN_DV), F32)],
        compiler_params=_params(("arbitrary", "arbitrary")),
    )(conv_gdn, proj_main, proj_small, normw, alog, dtb)


def gdn_bwd(dproj_main, conv_gdn, proj_main, proj_small, normw, alog, dtb, hist, t_inv, dy):
    t = conv_gdn.shape[0]
    nc = t // CHUNK
    rev = lambda c: nc - 1 - c
    hb = GDN_HB
    gate_blk = COL_GATE // (GDN_DV * hb)

    def body(alias_ref, qkv_ref, gate_ref, sm_ref, nw_ref, al_ref, db_ref, hist_ref, t_ref, dy_ref,
             dgate_ref, dqkv_ref, dsm_ref, dnw_ref, dal_ref, ddb_ref, dstate_ref):
        del alias_ref
        c, h = pl.program_id(0), pl.program_id(1)
        h0 = _first_head()

        @pl.when(c == 0)
        def _():
            for j in range(hb):
                dstate_ref[h0 + j] = jnp.zeros((GDN_DK, GDN_DV), F32)

        saved = [t_ref[0, j] for j in range(hb)]

        def fn(qs, ks, vs, small, gates, nw, al, db, states):
            return gdn_chunk(h0, qs, ks, vs, small, gates, nw, al, db, states, saved)[:2]

        qs, ks, vs = _gdn_parts(qkv_ref)
        _, vjp = jax.vjp(fn, qs, ks, vs, sm_ref[...], _head_cols(gate_ref), nw_ref[...], al_ref[...], db_ref[...],
                         [hist_ref[0, j] for j in range(hb)])
        dqs, dks, dvs, dsm, dgates, dnw, dal, ddb, dstates = vjp(
            (_head_cols(dy_ref), [dstate_ref[h0 + j] for j in range(hb)]))
        for j in range(hb):
            base = j * GDN_HC
            dqkv_ref[:, base:base + GDN_DK] = dqs[j]
            dqkv_ref[:, base + GDN_DK:base + 2 * GDN_DK] = dks[j]
            dqkv_ref[:, base + 2 * GDN_DK:base + GDN_HC] = dvs[j]
            dgate_ref[:, j * GDN_DV:(j + 1) * GDN_DV] = dgates[j].astype(dgate_ref.dtype)
            dstate_ref[h0 + j] = dstates[j]
        _accumulate(dsm_ref, h == 0, dsm)
        first = jnp.logical_and(c == 0, h == 0)
        _accumulate(dnw_ref, first, dnw)
        _accumulate(dal_ref, first, dal)
        _accumulate(ddb_ref, first, ddb)

    return pl.pallas_call(
        body, name="gdn_bwd", grid=(nc, GDN_HEADS // hb),
        in_specs=[ANY, pl.BlockSpec((CHUNK, GDN_HC * hb), lambda c, h: (rev(c), h)),
                  pl.BlockSpec((CHUNK, GDN_DV * hb), lambda c, h: (rev(c), gate_blk + h)),
                  pl.BlockSpec((CHUNK, LANES), lambda c, h: (rev(c), 0)),
                  _full((1, GDN_DV)), _full((1, LANES)), _full((1, LANES)),
                  pl.BlockSpec((1, hb, GDN_DK, GDN_DV), lambda c, h: (rev(c), h, 0, 0)),
                  pl.BlockSpec((1, hb, CHUNK, CHUNK), lambda c, h: (rev(c), h, 0, 0)),
                  pl.BlockSpec((CHUNK, GDN_DV * hb), lambda c, h: (rev(c), h))],
        out_specs=[pl.BlockSpec((CHUNK, GDN_DV * hb), lambda c, h: (rev(c), gate_blk + h)),
                   pl.BlockSpec((CHUNK, GDN_HC * hb), lambda c, h: (rev(c), h)),
                   pl.BlockSpec((CHUNK, LANES), lambda c, h: (rev(c), 0)),
                   _full((1, GDN_DV)), _full((1, LANES)), _full((1, LANES))],
        out_shape=[jax.ShapeDtypeStruct(dproj_main.shape, dproj_main.dtype), jax.ShapeDtypeStruct((t, GDN_CONV), F32),
                   jax.ShapeDtypeStruct((t, LANES), F32), jax.ShapeDtypeStruct((1, GDN_DV), F32),
                   jax.ShapeDtypeStruct((1, LANES), F32), jax.ShapeDtypeStruct((1, LANES), F32)],
        scratch_shapes=[pltpu.VMEM((GDN_HEADS, GDN_DK, GDN_DV), F32)],
        input_output_aliases={0: 0},
        compiler_params=_params(("arbitrary", "arbitrary")),
    )(dproj_main, conv_gdn, proj_main, proj_small, normw, alog, dtb, hist, t_inv, dy)


def out_proj_loss(x, y_ssd, y_gdn, w_out, final_w, target):
    t = x.shape[0]
    tm = min(256, t)

    def body(x_ref, ys_ref, yg_ref, wo_ref, fw_ref, tg_ref, loss_ref, dhid_ref, dys_ref, dyg_ref, dwo_ref, dfw_ref):
        i = pl.program_id(0)
        ys, yg = ys_ref[...], yg_ref[...]
        wo_s, wo_g = wo_ref[:SSD_WIDTH, :], wo_ref[SSD_WIDTH:, :]
        hid = x_ref[...] + _raw_dot(ys, wo_s, 1, 0) + _raw_dot(yg, wo_g, 1, 0)
        out, vjp = jax.vjp(rmsnorm, hid, fw_ref[...])
        err = out - tg_ref[...]
        loss = 0.5 * jnp.sum(jnp.mean(err * err, axis=-1, keepdims=True), axis=0, keepdims=True)
        dhid, dfw = vjp(err * (1.0 / D_MODEL))
        dhid_ref[...] = dhid
        dys_ref[...] = _raw_dot(dhid, wo_s, 1, 1)
        dyg_ref[...] = _raw_dot(dhid, wo_g, 1, 1)
        first = i == 0
        _accumulate(loss_ref, first, jnp.broadcast_to(loss, loss_ref.shape))
        _accumulate(dfw_ref, first, dfw)

        @pl.when(first)
        def _():
            dwo_ref[:SSD_WIDTH, :] = _raw_dot(ys, dhid, 0, 0)
            dwo_ref[SSD_WIDTH:, :] = _raw_dot(yg, dhid, 0, 0)

        @pl.when(i > 0)
        def _():
            dwo_ref[:SSD_WIDTH, :] += _raw_dot(ys, dhid, 0, 0)
            dwo_ref[SSD_WIDTH:, :] += _raw_dot(yg, dhid, 0, 0)

    row = lambda w: pl.BlockSpec((tm, w), lambda i: (i, 0))
    return pl.pallas_call(
        body, name="out_proj_loss", grid=(t // tm,),
        in_specs=[row(D_MODEL), row(SSD_WIDTH), row(GDN_W), _full((SSD_WIDTH + GDN_W, D_MODEL)), _full((1, D_MODEL)),
                  row(D_MODEL)],
        out_specs=[_full((8, LANES)), row(D_MODEL), row(SSD_WIDTH), row(GDN_W), _full((SSD_WIDTH + GDN_W, D_MODEL)),
                   _full((1, D_MODEL))],
        out_shape=[jax.ShapeDtypeStruct((8, LANES), F32), jax.ShapeDtypeStruct((t, D_MODEL), F32),
                   jax.ShapeDtypeStruct((t, SSD_WIDTH), F32), jax.ShapeDtypeStruct((t, GDN_W), F32),
                   jax.ShapeDtypeStruct((SSD_WIDTH + GDN_W, D_MODEL), F32), jax.ShapeDtypeStruct((1, D_MODEL), F32)],
        compiler_params=_params(("arbitrary",)),
    )(x, y_ssd, y_gdn, w_out, final_w, target)


def in_proj_bwd_x(x, normw, w_main, w_small, dproj_main, dsmall_a, dsmall_b, dhid, slabbed):
    t = x.shape[0]
    tm, tk = min(1024, t), 512
    nk = MAIN // tk
    ni = t // tm
    ns = len(slabbed)

    def body(x_ref, nw_ref, wm_ref, ws_ref, dp_ref, da_ref, db_ref, dh_ref, *rest):
        slab_refs, (gx_ref, dnw_ref), land_refs = rest[:ns], rest[ns:ns + 2], rest[ns + 2:2 * ns + 2]
        acc_ref, sems = rest[2 * ns + 2], rest[2 * ns + 3:]
        i, k = pl.program_id(0), pl.program_id(1)
        start, finish = _slab_exchange(slab_refs, land_refs, ns, *sems)

        @pl.when(jnp.logical_and(i == 0, k == 0))
        def _():
            start()

        part = _raw_dot(dp_ref[...], wm_ref[...], 1, 1)

        @pl.when(k == 0)
        def _():
            acc_ref[...] = part + _raw_dot(da_ref[...] + db_ref[...], ws_ref[...], 1, 1)

        @pl.when(k > 0)
        def _():
            acc_ref[...] += part

        @pl.when(k == nk - 1)
        def _():
            _, vjp = jax.vjp(rmsnorm, x_ref[...], nw_ref[...])
            dx, dnw = vjp(acc_ref[...])
            gx_ref[...] = dx + dh_ref[...]
            _accumulate(dnw_ref, i == 0, dnw)

        @pl.when(jnp.logical_and(i == ni - 1, k == nk - 1))
        def _():
            finish()

    row = lambda w: pl.BlockSpec((tm, w), lambda i, k: (i, 0))
    out = pl.pallas_call(
        body, name="in_proj_bwd_x", grid=(ni, nk),
        in_specs=[row(D_MODEL), _full((1, D_MODEL)), pl.BlockSpec((D_MODEL, tk), lambda i, k: (0, k)),
                  _full((D_MODEL, LANES)), pl.BlockSpec((tm, tk), lambda i, k: (i, k)), row(LANES), row(LANES),
                  row(D_MODEL)] + [HBM] * ns,
        out_specs=[row(D_MODEL), _full((1, D_MODEL))] + [HBM] * ns,
        out_shape=[jax.ShapeDtypeStruct((t, D_MODEL), F32), jax.ShapeDtypeStruct((1, D_MODEL), F32)]
        + _slab_exchange_shapes(slabbed, []),
        scratch_shapes=[pltpu.VMEM((tm, D_MODEL), F32)] + _slab_exchange_sems(ns),
        compiler_params=_params(("arbitrary", "arbitrary")),
    )(x, normw, w_main, w_small, dproj_main, dsmall_a, dsmall_b, dhid, *slabbed)
    return out[0], out[1], out[2:]


def in_proj_bwd_w(u, dproj_main, dsmall_a, dsmall_b):
    t = u.shape[0]
    tm, tn = min(512, t), MAIN // 4

    def body(u_ref, dp_ref, da_ref, db_ref, dwm_ref, dws_ref):
        j, i = pl.program_id(0), pl.program_id(1)
        uu = u_ref[...]
        _accumulate(dwm_ref, i == 0, _raw_dot(uu, dp_ref[...], 0, 0))

        @pl.when(j == 0)
        def _():
            _accumulate(dws_ref, i == 0, _raw_dot(uu, da_ref[...] + db_ref[...], 0, 0))

    return pl.pallas_call(
        body, name="in_proj_bwd_w", grid=(MAIN // tn, t // tm),
        in_specs=[pl.BlockSpec((tm, D_MODEL), lambda j, i: (i, 0)), pl.BlockSpec((tm, tn), lambda j, i: (i, j)),
                  pl.BlockSpec((tm, LANES), lambda j, i: (i, 0)), pl.BlockSpec((tm, LANES), lambda j, i: (i, 0))],
        out_specs=[pl.BlockSpec((D_MODEL, tn), lambda j, i: (0, j)), _full((D_MODEL, LANES))],
        out_shape=[jax.ShapeDtypeStruct((D_MODEL, MAIN), F32), jax.ShapeDtypeStruct((D_MODEL, LANES), F32)],
        compiler_params=_params(("arbitrary", "arbitrary")),
    )(u, dproj_main, dsmall_a, dsmall_b)


def sum_slabs(a, name):
    n, rows, cols = a.shape
    tr = 64 if rows % 64 == 0 else rows

    def body(a_ref, o_ref):
        acc = a_ref[0].astype(F32)
        for d in range(1, n):
            acc = acc + a_ref[d].astype(F32)
        o_ref[...] = acc

    return pl.pallas_call(
        body, name=name, grid=(rows // tr,),
        in_specs=[pl.BlockSpec((n, tr, cols), lambda i: (0, i, 0))],
        out_specs=pl.BlockSpec((tr, cols), lambda i: (i, 0)),
        out_shape=jax.ShapeDtypeStruct((rows, cols), F32),
        compiler_params=_params(("arbitrary",)),
    )(a)


def adamw(w, g, m, v, name):
    rows, cols = w.shape
    tr = 128 if rows % 128 == 0 else rows

    def body(w_ref, g_ref, m_ref, v_ref, d_ref, nm_ref, nv_ref):
        gg = g_ref[...]
        nm = ADAM_B1 * m_ref[...] + (1.0 - ADAM_B1) * gg
        nv = ADAM_B2 * v_ref[...] + (1.0 - ADAM_B2) * (gg * gg)
        m_hat = nm / (1.0 - ADAM_B1 ** ADAM_STEP)
        v_hat = nv / (1.0 - ADAM_B2 ** ADAM_STEP)
        d_ref[...] = -ADAM_LR * (m_hat / (jnp.sqrt(v_hat) + ADAM_EPS) + ADAM_WD * w_ref[...])
        nm_ref[...] = nm
        nv_ref[...] = nv

    spec = pl.BlockSpec((tr, cols), lambda i: (i, 0))
    shp = jax.ShapeDtypeStruct((rows, cols), F32)
    return pl.pallas_call(
        body, name=name, grid=(rows // tr,), in_specs=[spec] * 4, out_specs=[spec] * 3, out_shape=[shp] * 3,
        compiler_params=_params(("arbitrary",)),
    )(w, g, m, v)


def _my_place():
    return lax.axis_index("x"), lax.axis_index("y"), lax.axis_index("c")


def gather_weights(big, small):
    nb, n = len(big), len(big) + len(small)
    parts = 4

    def body(*refs):
        srcs, outs = refs[:n], refs[n:2 * n]
        land_a, land_b = refs[2 * n:2 * n + nb], refs[2 * n + nb:2 * n + 2 * nb]
        send_sems, recv_sems, fwd_send, fwd_recv, local_sems = refs[2 * n + 2 * nb:]
        x, y, c = _my_place()
        me = 2 * x + y
        chips = [(1 - x, y), (x, 1 - y), (1 - x, 1 - y)]
        half = [a.shape[0] // 2 for a in big]

        def ici(j, i):
            px, py = chips[j]
            if i < nb:
                src, dst = srcs[i].at[pl.ds(c * half[i], half[i])], land_a[i].at[j]
            else:
                src, dst = srcs[i], outs[i].at[me]
            return pltpu.make_async_remote_copy(src_ref=src, dst_ref=dst, send_sem=send_sems.at[j * n + i],
                                                recv_sem=recv_sems.at[j * n + i], device_id=(px, py, c),
                                                device_id_type=MESH)

        def ici_arrival(j, i):
            px, py = chips[j]
            dst = land_a[i].at[j] if i < nb else outs[i].at[2 * px + py]
            return pltpu.make_async_remote_copy(src_ref=dst, dst_ref=dst, send_sem=send_sems.at[j * n + i],
                                                recv_sem=recv_sems.at[j * n + i], device_id=(px, py, c),
                                                device_id_type=MESH)

        def forward(j, i, p):
            rows = half[i] // parts
            k = (j * nb + i) * parts + p
            return pltpu.make_async_remote_copy(
                src_ref=land_a[i].at[j, pl.ds(p * rows, rows)], dst_ref=land_b[i].at[j, pl.ds(p * rows, rows)],
                send_sem=fwd_send.at[k], recv_sem=fwd_recv.at[k], device_id=(x, y, 1 - c), device_id_type=MESH)

        def store(j, i, from_sibling):
            px, py = chips[j]
            buf, h = (land_b, 1 - c) if from_sibling else (land_a, c)
            k = n + (j * nb + i) * 2 + (1 if from_sibling else 0)
            return pltpu.make_async_copy(buf[i].at[j], outs[i].at[2 * px + py, pl.ds(h * half[i], half[i])],
                                         local_sems.at[k])

        own = [pltpu.make_async_copy(srcs[i], outs[i].at[me], local_sems.at[i]) for i in range(n)]
        sends = [ici(j, i) for j in range(3) for i in range(n)]
        for cp in own + sends:
            cp.start()
        pending = []
        for j in range(3):
            for i in range(n):
                ici_arrival(j, i).wait_recv()
                if i < nb:
                    fw = [forward(j, i, p) for p in range(parts)]
                    st = store(j, i, False)
                    for cp in fw + [st]:
                        cp.start()
                    pending += [cp.wait_send for cp in fw] + [st.wait]
        for j in range(3):
            for i in range(nb):
                for p in range(parts):
                    forward(j, i, p).wait_recv()
                st = store(j, i, True)
                st.start()
                pending.append(st.wait)
        for cp in sends:
            cp.wait_send()
        for wait in pending:
            wait()
        for cp in own:
            cp.wait()

    shards = list(big) + list(small)
    lands = [pltpu.VMEM((3, a.shape[0] // 2) + a.shape[1:], a.dtype) for a in big]
    return pl.pallas_call(
        body, name="gather_weights",
        in_specs=[HBM] * n, out_specs=[HBM] * n,
        out_shape=[jax.ShapeDtypeStruct((N_CHIP,) + s.shape, s.dtype) for s in shards],
        scratch_shapes=lands + lands + [
            pltpu.SemaphoreType.DMA((3 * n,)), pltpu.SemaphoreType.DMA((3 * n,)),
            pltpu.SemaphoreType.DMA((3 * nb * parts,)), pltpu.SemaphoreType.DMA((3 * nb * parts,)),
            pltpu.SemaphoreType.DMA((n + 6 * nb,))],
        compiler_params=pltpu.CompilerParams(vmem_limit_bytes=VMEM_LIMIT),
    )(*shards)


def _peer(x, y, c, mask):
    mx, my, mc = (mask >> 2) & 1, (mask >> 1) & 1, mask & 1
    return (x ^ mx if mx else x, y ^ my if my else y, c ^ mc if mc else c)


def _slab_exchange_shapes(slabbed, replicated):
    return ([jax.ShapeDtypeStruct(a.shape, a.dtype) for a in slabbed]
            + [jax.ShapeDtypeStruct((N_DEV,) + a.shape, a.dtype) for a in replicated])


def _slab_exchange_sems(n):
    return [pltpu.SemaphoreType.DMA((7 * n,)), pltpu.SemaphoreType.DMA((7 * n,)), pltpu.SemaphoreType.DMA((n,))]


def _slab_exchange(srcs, outs, ns, send_sems, recv_sems, local_sems):
    n = len(srcs)
    x, y, c = _my_place()
    me = 4 * x + 2 * y + c

    def piece(i, dev):
        return srcs[i].at[dev] if i < ns else srcs[i]

    def copies(arriving):
        out = []
        for mask in range(1, N_DEV):
            px, py, pc = _peer(x, y, c, mask)
            dev = 4 * px + 2 * py + pc
            for i in range(n):
                k = (mask - 1) * n + i
                out.append(pltpu.make_async_remote_copy(
                    src_ref=piece(i, dev), dst_ref=outs[i].at[dev if arriving else me], send_sem=send_sems.at[k],
                    recv_sem=recv_sems.at[k], device_id=(px, py, pc), device_id_type=MESH))
        return out

    def local():
        return [pltpu.make_async_copy(piece(i, me), outs[i].at[me], local_sems.at[i]) for i in range(n)]

    def start():
        for cp in local() + copies(False):
            cp.start()

    def finish():
        for cp in copies(True):
            cp.wait_recv()
        for cp in copies(False):
            cp.wait_send()
        for cp in local():
            cp.wait()

    return start, finish


def exchange_halves(halves, replicated):
    n, nr = len(halves), len(replicated)
    streams = 8

    def body(*refs):
        srcs, rep_srcs, outs, rep_outs = refs[:n], refs[n:n + nr], refs[n + nr:2 * n + nr], refs[2 * n + nr:2 * (n + nr)]
        refs = refs[2 * (n + nr):]
        mine, theirs = refs[:n], refs[n:2 * n]
        send_sems, recv_sems, in_sems, out_sems = refs[2 * n:2 * n + 4]
        rep_start, rep_finish = _slab_exchange(rep_srcs, rep_outs, 0, *refs[2 * n + 4:])
        rep_start()
        x, y, c = _my_place()
        loads = [pltpu.make_async_copy(srcs[i], mine[i], in_sems.at[i]) for i in range(n)]
        for cp in loads:
            cp.start()
        for cp in loads:
            cp.wait()

        def chunk_copy(i, s):
            rows = halves[i].shape[0] // streams
            k = i * streams + s
            return pltpu.make_async_remote_copy(
                src_ref=mine[i].at[pl.ds(s * rows, rows)], dst_ref=theirs[i].at[pl.ds(s * rows, rows)],
                send_sem=send_sems.at[k], recv_sem=recv_sems.at[k], device_id=(x, y, 1 - c), device_id_type=MESH)

        sends = [chunk_copy(i, s) for i in range(n) for s in range(streams)]
        for cp in sends:
            cp.start()
        own = [pltpu.make_async_copy(mine[i], outs[i].at[c], out_sems.at[i]) for i in range(n)]
        for cp in own:
            cp.start()
        for cp in sends:
            cp.wait_recv()
        got = [pltpu.make_async_copy(theirs[i], outs[i].at[1 - c], out_sems.at[n + i]) for i in range(n)]
        for cp in got:
            cp.start()
        for cp in sends:
            cp.wait_send()
        for cp in own + got:
            cp.wait()
        rep_finish()

    vmem = [pltpu.VMEM(a.shape, a.dtype) for a in halves]
    out = pl.pallas_call(
        body, name="exchange_halves",
        in_specs=[HBM] * (n + nr), out_specs=[HBM] * (n + nr),
        out_shape=[jax.ShapeDtypeStruct((2,) + a.shape, a.dtype) for a in halves]
        + _slab_exchange_shapes([], replicated),
        scratch_shapes=vmem + vmem + [pltpu.SemaphoreType.DMA((n * streams,)), pltpu.SemaphoreType.DMA((n * streams,)),
                                      pltpu.SemaphoreType.DMA((n,)), pltpu.SemaphoreType.DMA((2 * n,))]
        + _slab_exchange_sems(nr),
        compiler_params=pltpu.CompilerParams(vmem_limit_bytes=VMEM_LIMIT),
    )(*halves, *replicated)
    return out[:n], out[n:]


def _pack_cols(pieces):
    offs, pos = [], 0
    for a in pieces:
        offs.append(pos)
        pos += a.shape[1]
    rows8 = [jnp.pad(a.astype(F32), ((0, 8 - a.shape[0]), (0, 0))) for a in pieces]
    return jnp.concatenate(rows8, axis=1), offs


def adamw_many(ws, gs, ms, vs):
    n = len(ws)

    def body(*refs):
        w_r, g_r, m_r, v_r = refs[:n], refs[n:2 * n], refs[2 * n:3 * n], refs[3 * n:4 * n]
        d_o, m_o, v_o = refs[4 * n:5 * n], refs[5 * n:6 * n], refs[6 * n:7 * n]
        for i in range(n):
            gg = g_r[i][...]
            nm = ADAM_B1 * m_r[i][...] + (1.0 - ADAM_B1) * gg
            nv = ADAM_B2 * v_r[i][...] + (1.0 - ADAM_B2) * (gg * gg)
            m_hat = nm / (1.0 - ADAM_B1 ** ADAM_STEP)
            v_hat = nv / (1.0 - ADAM_B2 ** ADAM_STEP)
            d_o[i][...] = -ADAM_LR * (m_hat / (jnp.sqrt(v_hat) + ADAM_EPS) + ADAM_WD * w_r[i][...])
            m_o[i][...] = nm
            v_o[i][...] = nv

    shapes = [jax.ShapeDtypeStruct(w.shape, F32) for w in ws]
    out = pl.pallas_call(body, name="adamw_small", out_shape=shapes * 3,
                         compiler_params=pltpu.CompilerParams(vmem_limit_bytes=VMEM_LIMIT))(*ws, *gs, *ms, *vs)
    return out[:n], out[n:2 * n], out[2 * n:]


def _lanes(vec, start):
    n = vec.shape[-1]
    return jnp.pad(vec.reshape(1, n).astype(F32), ((0, 0), (start, LANES - start - n)))


def kernel(x, norm_w, w_in, ssd_conv_w, ssd_conv_b, ssd_dt_bias, ssd_a_log, ssd_d, ssd_norm_w, gdn_conv_w, gdn_dt_bias, gdn_a_log, gdn_norm_w, w_out, final_norm_w, loss_target, m_norm_w, m_w_in, m_ssd_conv_w, m_ssd_conv_b, m_ssd_dt_bias, m_ssd_a_log, m_ssd_d, m_ssd_norm_w, m_gdn_conv_w, m_gdn_dt_bias, m_gdn_a_log, m_gdn_norm_w, m_w_out, m_final_norm_w, v_norm_w, v_w_in, v_ssd_conv_w, v_ssd_conv_b, v_ssd_dt_bias, v_ssd_a_log, v_ssd_d, v_ssd_norm_w, v_gdn_conv_w, v_gdn_dt_bias, v_gdn_a_log, v_gdn_norm_w, v_w_out, v_final_norm_w):
    xs = x[0]
    target = loss_target[0]
    chip = 2 * lax.axis_index("x") + lax.axis_index("y")
    w_in_shard, w_out_shard = w_in[0], w_out[0]
    in_cols = w_in_shard.shape[1]
    out_rows = w_out_shard.shape[0]

    g_in, g_out, g_cs, g_cg = gather_weights(
        [w_in_shard.astype(MXU_DTYPE), w_out_shard.astype(MXU_DTYPE)], [ssd_conv_w[0], gdn_conv_w[0]])
    w_in_full = jnp.transpose(g_in, (1, 0, 2)).reshape(D_MODEL, IN_DIM)
    w_out_full = g_out.reshape(N_CHIP * out_rows, D_MODEL)
    cw_ssd = _ssd_perm(jnp.transpose(g_cs, (1, 0, 2)).reshape(4, SSD_CONV))
    cw_gdn = _gdn_perm(jnp.transpose(g_cg, (1, 0, 2)).reshape(4, GDN_CONV))
    cb_ssd = _ssd_perm(ssd_conv_b)
    cb_gdn = jnp.zeros((1, GDN_CONV), F32)
    o_xbc, o_dt, o_gate, o_qkv, o_ab = 1024, 2560, 2576, 3600, 6672
    w_main = jnp.concatenate([w_in_full[:, :o_xbc], w_in_full[:, o_gate:o_qkv], _gdn_perm(w_in_full[:, o_qkv:o_ab]),
                              _ssd_perm(w_in_full[:, o_xbc:o_dt])], axis=1)
    w_small = jnp.concatenate([w_in_full[:, o_dt:o_gate], w_in_full[:, o_ab:],
                               jnp.zeros((D_MODEL, LANES - 32), MXU_DTYPE)], axis=1)
    alog = _lanes(ssd_a_log, 0) + _lanes(gdn_a_log, LANE_GA)
    dtb = _lanes(ssd_dt_bias, 0) + _lanes(gdn_dt_bias, LANE_GA)
    dvec = _lanes(ssd_d, 0)
    fw = final_norm_w.reshape(1, D_MODEL)

    proj_main, proj_small, u = in_proj(xs, norm_w, w_main, w_small)
    proj_main, conv_out = in_proj_conv(proj_main, u, w_main, jnp.concatenate([cw_gdn, cw_ssd], axis=1),
                                       jnp.concatenate([cb_gdn, cb_ssd], axis=1))
    conv_ssd = conv_gdn = conv_out
    y_ssd, hist_ssd = ssd_fwd(conv_ssd, proj_main, proj_small, ssd_norm_w, alog, dtb, dvec)
    y_gdn, hist_gdn, tinv_gdn = gdn_fwd(conv_gdn, proj_main, proj_small, gdn_norm_w, alog, dtb)

    loss_blk, dhid, dy_ssd, dy_gdn, d_w_out, d_fw = out_proj_loss(xs, y_ssd, y_gdn, w_out_full, fw, target)
    dconv_ssd, dproj_main, dsmall_ssd, d_ssd_nw, d_alog_s, d_dtb_s, d_dvec = ssd_bwd(
        conv_ssd, proj_main, proj_small, ssd_norm_w, alog, dtb, dvec, hist_ssd, dy_ssd)
    dproj_main, dconv_gdn, dsmall_gdn, d_gdn_nw, d_alog_g, d_dtb_g = gdn_bwd(
        dproj_main, conv_gdn, proj_main, proj_small, gdn_norm_w, alog, dtb, hist_gdn, tinv_gdn, dy_gdn)
    dproj_main, dwb_ssd = conv_bwd(dproj_main, proj_main, COL_SSD, SSD_CONV, cw_ssd, cb_ssd, dconv_ssd, "conv_bwd_ssd")
    dproj_main, dwb_gdn = conv_bwd(dproj_main, proj_main, COL_GDN, GDN_CONV, cw_gdn, cb_gdn, dconv_gdn, "conv_bwd_gdn")
    d_w_main, d_w_small = in_proj_bwd_w(u, dproj_main, dsmall_ssd, dsmall_gdn)

    d_w_in = jnp.concatenate([d_w_main[:, :COL_GATE], _ssd_unperm(d_w_main[:, COL_SSD:]), d_w_small[:, 0:16],
                              d_w_main[:, COL_GATE:COL_GDN], _gdn_unperm(d_w_main[:, COL_GDN:COL_SSD]),
                              d_w_small[:, 16:32]], axis=1)
    d_w_in = jnp.transpose(d_w_in.reshape(D_MODEL, N_CHIP, in_cols), (1, 0, 2))
    slabs = [d_w_in.reshape(N_DEV, D_MODEL // 2, in_cols).astype(COMM_DTYPE),
             d_w_out.reshape(N_DEV, out_rows // 2, D_MODEL).astype(COMM_DTYPE)]

    grad_x, d_norm_w, (r_in, r_out) = in_proj_bwd_x(xs, norm_w, w_main, w_small, dproj_main, dsmall_ssd, dsmall_gdn,
                                                     dhid, slabs)
    d_alog, d_dtb = d_alog_s + d_alog_g, d_dtb_s + d_dtb_g
    packed, (o_nw, o_cs, o_cg, o_snw, o_fw, o_al, o_db, o_dv, o_gnw, o_loss) = _pack_cols([
        d_norm_w, _ssd_unperm(dwb_ssd), _gdn_unperm(dwb_gdn),
        d_ssd_nw.reshape(1, SSD_WIDTH), d_fw, d_alog, d_dtb, d_dvec, d_gdn_nw, loss_blk])

    half_in = sum_slabs(r_in, "sum_w_in")
    half_out = sum_slabs(r_out, "sum_w_out")
    (full_in, full_out), (r_small,) = exchange_halves([half_in, half_out], [packed])
    tot = sum_slabs(r_small, "sum_small")
    grad_w_in = full_in.reshape(D_MODEL, in_cols)
    grad_w_out = full_out.reshape(out_rows, D_MODEL)
    loss = tot[0, o_loss]
    sc, gc = ssd_conv_w.shape[2], gdn_conv_w.shape[2]
    row = lambda off, n, r=0: tot[r:r + 1, off:off + n]
    gs = [row(o_nw, D_MODEL),
          lax.dynamic_slice(tot, (0, o_cs + chip * sc), (4, sc)),
          row(o_cs, SSD_CONV, 4),
          row(o_db, SSD_HEADS), row(o_al, SSD_HEADS), row(o_dv, SSD_HEADS),
          row(o_snw, SSD_WIDTH),
          lax.dynamic_slice(tot, (0, o_cg + chip * gc), (4, gc)),
          row(o_db + LANE_GA, GDN_HEADS), row(o_al + LANE_GA, GDN_HEADS),
          row(o_gnw, GDN_DV), row(o_fw, D_MODEL)]

    names = ["norm_w", "ssd_conv_w", "ssd_conv_b", "ssd_dt_bias", "ssd_a_log", "ssd_d", "ssd_norm_w", "gdn_conv_w",
             "gdn_dt_bias", "gdn_a_log", "gdn_norm_w", "final_norm_w"]
    ws = [norm_w, ssd_conv_w, ssd_conv_b, ssd_dt_bias, ssd_a_log, ssd_d, ssd_norm_w, gdn_conv_w, gdn_dt_bias,
          gdn_a_log, gdn_norm_w, final_norm_w]
    ms = [m_norm_w, m_ssd_conv_w, m_ssd_conv_b, m_ssd_dt_bias, m_ssd_a_log, m_ssd_d, m_ssd_norm_w, m_gdn_conv_w,
          m_gdn_dt_bias, m_gdn_a_log, m_gdn_norm_w, m_final_norm_w]
    vs = [v_norm_w, v_ssd_conv_w, v_ssd_conv_b, v_ssd_dt_bias, v_ssd_a_log, v_ssd_d, v_ssd_norm_w, v_gdn_conv_w,
          v_gdn_dt_bias, v_gdn_a_log, v_gdn_norm_w, v_final_norm_w]
    shapes = [w.shape for w in ws]
    flat = lambda arrs: [a.reshape(g.shape) for a, g in zip(arrs, gs)]
    d_s, m_s, v_s = adamw_many(flat(ws), gs, flat(ms), flat(vs))
    back = lambda arrs: dict(zip(names, [a.reshape(s) for a, s in zip(arrs, shapes)]))
    delta, new_m, new_v, grads = back(d_s), back(m_s), back(v_s), back(gs)
    d_in, m_in, v_in = adamw(w_in_shard, grad_w_in, m_w_in[0], v_w_in[0], "adamw_w_in")
    d_out, m_out, v_out = adamw(w_out_shard, grad_w_out, m_w_out[0], v_w_out[0], "adamw_w_out")
    for tbl, a_in, a_out in ((grads, grad_w_in, grad_w_out), (delta, d_in, d_out), (new_m, m_in, m_out),
                             (new_v, v_in, v_out)):
        tbl["w_in"] = a_in[None]
        tbl["w_out"] = a_out[None]

    order = ["norm_w", "w_in", "ssd_conv_w", "ssd_conv_b", "ssd_dt_bias", "ssd_a_log", "ssd_d", "ssd_norm_w",
             "gdn_conv_w", "gdn_dt_bias", "gdn_a_log", "gdn_norm_w", "w_out", "final_norm_w"]
    return (loss.reshape(()), grad_x[None], *[grads[k] for k in order], *[delta[k] for k in order],
            *[new_m[k] for k in order], *[new_v[k] for k in order])
```

```python
import functools

import jax
import jax.numpy as jnp
from jax import lax
from jax.experimental import pallas as pl
from jax.experimental.pallas import tpu as pltpu

F32 = jnp.float32
MXU_DTYPE = jnp.bfloat16
COMM_DTYPE = jnp.bfloat16
MESH = pl.DeviceIdType.MESH

D_MODEL = 1024
CHUNK = 64
EPS = 1e-6
SSD_HEADS, SSD_GROUPS, SSD_STATE = 16, 2, 128
SSD_WIDTH, SSD_CONV = 1024, 1536
SSD_GW = SSD_WIDTH // SSD_GROUPS
SSD_GC = SSD_GW + 2 * SSD_STATE
GDN_HEADS, GDN_DK, GDN_DV = 8, 128, 128
GDN_W, GDN_CONV = 1024, 3072
GDN_HC = 2 * GDN_DK + GDN_DV
IN_DIM = 6688
MAIN = 6656
LANES = 128
COL_Z, COL_GATE, COL_GDN, COL_SSD = 0, 1024, 2048, 5120
COL_CONV = COL_GDN
GDN_HB = 8
LANE_GA, LANE_GB = 16, 24
N_DEV, N_CHIP = 8, 4
VMEM_LIMIT = 52 * 1024 * 1024

ADAM_LR, ADAM_B1, ADAM_B2, ADAM_EPS, ADAM_WD, ADAM_STEP = 0.001, 0.9, 0.999, 1e-08, 0.01, 10


def _ssd_perm(a):
    lead, nb = a.shape[:-1], SSD_GROUPS * SSD_STATE
    x = a[..., :SSD_WIDTH].reshape(*lead, SSD_GROUPS, SSD_GW)
    b = a[..., SSD_WIDTH:SSD_WIDTH + nb].reshape(*lead, SSD_GROUPS, SSD_STATE)
    c = a[..., SSD_WIDTH + nb:].reshape(*lead, SSD_GROUPS, SSD_STATE)
    return jnp.concatenate([x, b, c], axis=-1).reshape(*lead, SSD_CONV)


def _ssd_unperm(a):
    lead = a.shape[:-1]
    g = a.reshape(*lead, SSD_GROUPS, SSD_GC)
    parts = [g[..., :SSD_GW], g[..., SSD_GW:SSD_GW + SSD_STATE], g[..., SSD_GW + SSD_STATE:]]
    return jnp.concatenate([p.reshape(*lead, -1) for p in parts], axis=-1)


def _gdn_perm(a):
    lead = a.shape[:-1]
    return jnp.swapaxes(a.reshape(*lead, 3, GDN_HEADS, GDN_DK), -3, -2).reshape(*lead, GDN_CONV)


def _gdn_unperm(a):
    lead = a.shape[:-1]
    return jnp.swapaxes(a.reshape(*lead, GDN_HEADS, 3, GDN_DK), -3, -2).reshape(*lead, GDN_CONV)


def _split(a, n):
    parts, rest = [], a.astype(F32)
    for i in range(n):
        p = rest.astype(MXU_DTYPE)
        parts.append(p)
        if i < n - 1:
            rest = rest - p.astype(F32)
    return parts


def _raw_dot(a, b, ca, cb, mode="bf16"):
    d = lambda u, v: lax.dot_general(u, v, (((ca,), (cb,)), ((), ())), preferred_element_type=F32)
    if mode == "bf16":
        return d(a.astype(MXU_DTYPE), b.astype(MXU_DTYPE))
    if mode == "x3":
        (ah, al), (bh, bl) = _split(a, 2), _split(b, 2)
        return d(ah, bh) + (d(ah, bl) + d(al, bh))
    if mode == "sel_a":
        a0 = a.astype(MXU_DTYPE)
        b1, b2, b3 = _split(b, 3)
        return d(a0, b1) + (d(a0, b2) + d(a0, b3))
    assert mode == "sel_b", mode
    b0 = b.astype(MXU_DTYPE)
    a1, a2, a3 = _split(a, 3)
    return d(a1, b0) + (d(a2, b0) + d(a3, b0))


@functools.partial(jax.custom_vjp, nondiff_argnums=(2,))
def mm_nn(a, b, mode="bf16"):
    return _raw_dot(a, b, 1, 0, mode)


@functools.partial(jax.custom_vjp, nondiff_argnums=(2,))
def mm_nt(a, b, mode="bf16"):
    return _raw_dot(a, b, 1, 1, mode)


@functools.partial(jax.custom_vjp, nondiff_argnums=(2,))
def mm_tn(a, b, mode="bf16"):
    return _raw_dot(a, b, 0, 0, mode)


_SAME = {"bf16": ("bf16", "bf16"), "x3": ("x3", "x3")}
_NN_BWD = dict(_SAME, sel_a=("bf16", "sel_a"), sel_b=("sel_b", "bf16"))
_NT_BWD = dict(_SAME, sel_a=("bf16", "sel_b"), sel_b=("sel_b", "bf16"))
_TN_BWD = dict(_SAME, sel_a=("bf16", "sel_a"), sel_b=("sel_a", "bf16"))
mm_nn.defvjp(lambda a, b, m: (_raw_dot(a, b, 1, 0, m), (a, b)),
             lambda m, r, g: (mm_nt(g, r[1], _NN_BWD[m][0]), mm_tn(r[0], g, _NN_BWD[m][1])))
mm_nt.defvjp(lambda a, b, m: (_raw_dot(a, b, 1, 1, m), (a, b)),
             lambda m, r, g: (mm_nn(g, r[1], _NT_BWD[m][0]), mm_tn(g, r[0], _NT_BWD[m][1])))
mm_tn.defvjp(lambda a, b, m: (_raw_dot(a, b, 0, 0, m), (a, b)),
             lambda m, r, g: (mm_nt(r[1], g, _TN_BWD[m][0]), mm_nn(r[0], g, _TN_BWD[m][1])))


@jax.custom_jvp
def sigmoid(x):
    return 1.0 / (1.0 + jnp.exp(-x))


@sigmoid.defjvp
def _sigmoid_jvp(p, t):
    s = sigmoid(p[0])
    return s, t[0] * s * (1.0 - s)


@jax.custom_jvp
def softplus(x):
    return jnp.maximum(x, 0.0) + jnp.log(1.0 + jnp.exp(-jnp.abs(x)))


@softplus.defjvp
def _softplus_jvp(p, t):
    return softplus(p[0]), t[0] * sigmoid(p[0])


def silu(x):
    return x * sigmoid(x)


def rmsnorm(x, w):
    return x * lax.rsqrt(jnp.mean(x * x, axis=-1, keepdims=True) + EPS) * w


def _iota(shape, dim):
    return lax.broadcasted_iota(jnp.int32, shape, dim)


def _tri_inv_impl(mats):
    n = mats[0].shape[0]
    r, c = _iota((n, n), 0), _iota((n, n), 1)
    eye = jnp.where(r == c, 1.0, 0.0).astype(F32)
    blockdiag = (r >> 4) == (c >> 4)
    dot = lambda u, v: _raw_dot(u, v, 1, 0, "x3")
    dot1 = lambda u, v: _raw_dot(u, v, 1, 0)
    each = lambda f, *ls: [f(*xs) for xs in zip(*ls)]
    dg = each(lambda a: jnp.where(blockdiag, a, 0.0), mats)
    off = each(lambda a, d: a - d, mats, dg)
    m = each(lambda d: -d, dg)
    p = each(lambda x: eye + x, m)
    pw = m
    for _ in range(3):
        pw = each(lambda x: dot1(x, x), pw)
        p = each(lambda x, y: x + dot1(x, y), p, pw)
    e = each(dot, p, off)
    e2 = each(lambda x: dot1(x, x), e)
    q = each(lambda x: eye - x, e)
    q = each(lambda x, y: x + dot1(x, y), q, e2)
    return each(dot, q, p)


def _tri_inv_bwd(ts, gs):
    x = [mm_nt(g, t) for g, t in zip(gs, ts)]
    return [-mm_tn(t, y) for t, y in zip(ts, x)]


@jax.custom_vjp
def tri_inv(mats):
    return _tri_inv_impl(mats)


def _tri_inv_fwd(mats):
    ts = _tri_inv_impl(mats)
    return ts, ts


tri_inv.defvjp(_tri_inv_fwd, lambda ts, gs: (_tri_inv_bwd(ts, gs),))


@jax.custom_vjp
def tri_inv_saved(mats, ts):
    del mats
    return ts


tri_inv_saved.defvjp(lambda mats, ts: (ts, ts),
                     lambda ts, gs: (_tri_inv_bwd(ts, gs), [jnp.zeros_like(t) for t in ts]))


def _chunk_masks():
    r, c = _iota((CHUNK, CHUNK), 0), _iota((CHUNK, CHUNK), 1)
    return r >= c, r > c, r == c


def _log_decay_cumsum(small, alog, dtb, incl):
    sp = softplus(small + dtb)
    la = -jnp.exp(alog) * sp
    tri = jnp.where(incl, 1.0, 0.0).astype(F32)
    return sp, mm_nn(tri, la, "sel_a")


def _col_of(x, lane_mask):
    return jnp.sum(jnp.where(lane_mask, x, 0.0), axis=1, keepdims=True)


def _decay_matrix(col, incl, eye):
    row = jnp.sum(jnp.where(eye, col, 0.0), axis=0, keepdims=True)
    return jnp.where(incl, jnp.exp(jnp.where(incl, col - row, 0.0)), 0.0)


def gdn_chunk(h0, qs, ks, vs, small, gates, normw, alog, dtb, states, saved_t=None):
    incl, strict, eye = _chunk_masks()
    lane = _iota((1, LANES), 1)
    last = _iota((CHUNK, 1), 0) == CHUNK - 1
    _, lac = _log_decay_cumsum(small, alog, dtb, incl)
    heads = range(len(qs))
    each = lambda f, *ls: [f(*xs) for xs in zip(*ls)]
    gc = [_col_of(lac, lane == LANE_GA + h0 + j) for j in heads]
    beta = [sigmoid(_col_of(small, lane == LANE_GB + h0 + j)) for j in heads]
    decay = each(lambda x: _decay_matrix(x, incl, eye), gc)
    gl = each(lambda x: jnp.sum(jnp.where(last, x, 0.0), axis=0, keepdims=True), gc)
    q = each(lambda x: x * lax.rsqrt(jnp.sum(x * x, axis=-1, keepdims=True) + EPS) * (GDN_DK ** -0.5), qs)
    k = each(lambda x: x * lax.rsqrt(jnp.sum(x * x, axis=-1, keepdims=True) + EPS), ks)
    kb = each(lambda x, b: x * b, k, beta)
    a = each(lambda x, y, d: jnp.where(strict, mm_nt(x, y) * d, 0.0), kb, k, decay)
    t = tri_inv(a) if saved_t is None else tri_inv_saved(a, saved_t)
    eg = each(jnp.exp, gc)
    u = each(lambda x, v, b: mm_nn(x, v * b), t, vs, beta)
    w = each(lambda x, y, e: mm_nn(x, y * e), t, kb, eg)
    attn = each(lambda x, y, d: mm_nt(x, y) * d, q, k, decay)
    v_new = each(lambda x, y, s: x - mm_nn(y, s), u, w, states)
    o = each(lambda x, e, s, at, vn: mm_nn(x * e, s) + mm_nn(at, vn), q, eg, states, attn, v_new)
    new_states = each(lambda s, x, y, l, c: s * jnp.exp(l) + mm_tn(y * jnp.exp(l - c), x), states, v_new, k, gl, gc)
    ys = each(lambda x, gt: rmsnorm(x, normw) * silu(gt), o, gates)
    return ys, new_states, t


@jax.custom_vjp
def split_lanes(x):
    return [x[:, i * LANES:(i + 1) * LANES] for i in range(x.shape[1] // LANES)]


@jax.custom_vjp
def join_lanes(xs):
    return jnp.concatenate(xs, axis=1)


split_lanes.defvjp(lambda x: (split_lanes(x), None), lambda _, gs: (join_lanes(gs),))
join_lanes.defvjp(lambda xs: (join_lanes(xs), None), lambda _, g: (split_lanes(g),))


def ssd_chunk(xs, bm, cm, z, small, normw, alog, dtb, dvec, state):
    incl, _, eye = _chunk_masks()
    lane = _iota((1, LANES), 1)
    last = _iota((CHUNK, 1), 0) == CHUNK - 1
    hpg = SSD_HEADS // SSD_GROUPS
    groups = range(len(xs))
    each = lambda f, *ls: [f(*a) for a in zip(*ls)]
    sp, lac = _log_decay_cumsum(small, alog, dtb, incl)
    lac_last = jnp.sum(jnp.where(last, lac, 0.0), axis=0, keepdims=True)
    sel = [jnp.where(_iota((LANES, SSD_GW), 0) == g * hpg + (_iota((LANES, SSD_GW), 1) >> 6), 1.0, 0.0).astype(F32)
           for g in groups]
    expand = lambda v: [mm_nn(v, s, "sel_b") for s in sel]
    dt_e, elac_e, toend_e = expand(sp), expand(jnp.exp(lac)), expand(jnp.exp(lac_last - lac))
    row8, row8e = _iota((8, LANES), 0), _iota((8, SSD_GW), 0)
    two_e = expand(jnp.where(row8 == 0, dvec, 0.0) + jnp.where(row8 == 1, jnp.exp(lac_last), 0.0))
    d_e = each(lambda v: jnp.sum(jnp.where(row8e == 0, v, 0.0), axis=0, keepdims=True), two_e)
    chunk_e = each(lambda v: jnp.sum(jnp.where(row8e == 1, v, 0.0), axis=0, keepdims=True), two_e)
    xdt = each(lambda a, b: a * b, xs, dt_e)
    cb = each(mm_nt, cm, bm)
    y = each(lambda c_, st, el, x_, d_: mm_nn(c_, st) * el + x_ * d_, cm, state, elac_e, xs, d_e)
    x_pairs = each(split_lanes, xdt)
    half = _iota((1, LANES), 1) >> 6
    lms = [[_decay_matrix(_col_of(lac, lane == g * hpg + j), incl, eye) for j in range(hpg)] for g in groups]
    terms = [[mm_nn(cb[g] * lms[g][j], jnp.where(half == j % 2, x_pairs[g][j // 2], 0.0)) for j in range(hpg)]
             for g in groups]
    y = [y[g] + join_lanes([terms[g][2 * p] + terms[g][2 * p + 1] for p in range(hpg // 2)]) for g in groups]
    new_state = each(lambda st, ce, b_, xd, te: st * ce + mm_tn(b_, xd * te), state, chunk_e, bm, xdt, toend_e)
    out = each(lambda y_, z_, nw: rmsnorm(y_ * silu(z_), nw), y, z, normw)
    return out, new_state


def _params(sem=None):
    return pltpu.CompilerParams(dimension_semantics=sem, vmem_limit_bytes=VMEM_LIMIT)


def _full(shape):
    n = len(shape)
    return pl.BlockSpec(shape, lambda *_: (0,) * n)


ANY = pl.BlockSpec(memory_space=pl.ANY)
HBM = pl.BlockSpec(memory_space=pltpu.HBM)


def in_proj(x, normw, w_main, w_small):
    t = x.shape[0]
    tm, tn = min(1024, t), 512

    def body(x_ref, nw_ref, wm_ref, ws_ref, pm_ref, ps_ref, u_ref):
        @pl.when(pl.program_id(1) == 0)
        def _():
            u = rmsnorm(x_ref[...], nw_ref[...]).astype(MXU_DTYPE)
            u_ref[...] = u
            ps_ref[...] = _raw_dot(u, ws_ref[...], 1, 0)
        pm_ref[...] = _raw_dot(u_ref[...], wm_ref[...], 1, 0)

    return pl.pallas_call(
        body, name="in_proj", grid=(t // tm, COL_CONV // tn),
        in_specs=[pl.BlockSpec((tm, D_MODEL), lambda i, j: (i, 0)), _full((1, D_MODEL)),
                  pl.BlockSpec((D_MODEL, tn), lambda i, j: (0, j)), _full((D_MODEL, LANES))],
        out_specs=[pl.BlockSpec((tm, tn), lambda i, j: (i, j)), pl.BlockSpec((tm, LANES), lambda i, j: (i, 0)),
                   pl.BlockSpec((tm, D_MODEL), lambda i, j: (i, 0))],
        out_shape=[jax.ShapeDtypeStruct((t, MAIN), F32), jax.ShapeDtypeStruct((t, LANES), F32),
                   jax.ShapeDtypeStruct((t, D_MODEL), MXU_DTYPE)],
        compiler_params=_params(("arbitrary", "arbitrary")),
    )(x, normw, w_main, w_small)


CONV_TC = 512
HALO = 8


def _shift_down(cur, prev, s):
    rolled = pltpu.roll(cur, s, 0)
    top = jnp.where(_iota((HALO, cur.shape[1]), 0) < s, pltpu.roll(prev, s, 0), rolled[:HALO])
    if cur.shape[0] == HALO:
        return top
    return jnp.concatenate([top, rolled[HALO:]], axis=0)


def _shift_up(cur, nxt, s):
    n = cur.shape[0]
    rolled = pltpu.roll(cur, n - s, 0)
    bot = jnp.where(_iota((HALO, cur.shape[1]), 0) >= HALO - s, pltpu.roll(nxt, HALO - s, 0), rolled[n - HALO:])
    return jnp.concatenate([rolled[:n - HALO], bot], axis=0)


def _conv_pre(cur, prev, w_ref, b):
    acc = cur * w_ref[3:4, :] + b
    shifted = [cur]
    for s in (1, 2, 3):
        sh = _shift_down(cur, prev, s)
        shifted.append(sh)
        acc = acc + sh * w_ref[3 - s:4 - s, :]
    return acc, shifted


def in_proj_conv(proj_main, u, w_main, w, b):
    t = u.shape[0]
    tm, tn = min(1024, t), CONV_TC
    rc = min(256, tm)
    c0, nj = COL_CONV // tn, (MAIN - COL_CONV) // tn

    def body(alias_ref, u_ref, wm_ref, w_ref, b_ref, pm_ref, out_ref, halo_ref):
        del alias_ref
        j = pl.program_id(1)

        @pl.when(pl.program_id(0) == 0)
        def _():
            halo_ref[j] = jnp.zeros((HALO, tn), F32)

        prev = halo_ref[j]
        for r in range(tm // rc):
            rows = pl.ds(r * rc, rc)
            p = _raw_dot(u_ref[rows, :], wm_ref[...], 1, 0)
            pm_ref[rows, :] = p
            pre, _ = _conv_pre(p, prev, w_ref, b_ref[...])
            out_ref[rows, :] = silu(pre)
            prev = p[rc - HALO:]
        halo_ref[j] = prev

    return pl.pallas_call(
        body, name="in_proj_conv", grid=(t // tm, nj),
        in_specs=[ANY, pl.BlockSpec((tm, D_MODEL), lambda i, j: (i, 0)),
                  pl.BlockSpec((D_MODEL, tn), lambda i, j: (0, c0 + j)),
                  pl.BlockSpec((4, tn), lambda i, j: (0, j)), pl.BlockSpec((1, tn), lambda i, j: (0, j))],
        out_specs=[pl.BlockSpec((tm, tn), lambda i, j: (i, c0 + j)), pl.BlockSpec((tm, tn), lambda i, j: (i, j))],
        out_shape=[jax.ShapeDtypeStruct(proj_main.shape, F32), jax.ShapeDtypeStruct((t, MAIN - COL_CONV), F32)],
        scratch_shapes=[pltpu.VMEM((nj, HALO, tn), F32)],
        input_output_aliases={0: 0},
        compiler_params=_params(("arbitrary", "arbitrary")),
    )(proj_main, u, w_main, w, b)


def _dsilu(pre):
    sg = sigmoid(pre)
    return sg * (1.0 + pre * (1.0 - sg))


def conv_bwd(dproj_main, proj_main, col0, width, w, b, dout, name):
    t = proj_main.shape[0]
    tt, c0 = min(512, t), col0 // CONV_TC
    nt = t // tt
    after = lambda i: jnp.minimum((i + 1) * (tt // HALO), t // HALO - 1)

    def body(alias_ref, cur_ref, prev_ref, nxt_ref, w_ref, b_ref, do_ref, do_nxt_ref, dx_ref, dwb_ref):
        del alias_ref
        i = pl.program_id(1)
        cur, bias = cur_ref[...], b_ref[...]
        prev = jnp.where(i > 0, prev_ref[...], 0.0)
        pre, shifted = _conv_pre(cur, prev, w_ref, bias)
        dpre = do_ref[...] * _dsilu(pre)
        pre_nxt, _ = _conv_pre(nxt_ref[...], cur[tt - HALO:], w_ref, bias)
        dpre_nxt = jnp.where(i < nt - 1, do_nxt_ref[...] * _dsilu(pre_nxt), 0.0)
        dx = dpre * w_ref[3:4, :]
        for s in (1, 2, 3):
            dx = dx + _shift_up(dpre, dpre_nxt, s) * w_ref[3 - s:4 - s, :]
        dx_ref[...] = dx.astype(dx_ref.dtype)
        row = _iota((HALO, CONV_TC), 0)
        upd = jnp.where(row == 4, jnp.sum(dpre, axis=0, keepdims=True), 0.0)
        for s in range(4):
            upd = upd + jnp.where(row == 3 - s, jnp.sum(dpre * shifted[s], axis=0, keepdims=True), 0.0)
        _accumulate(dwb_ref, i == 0, upd)

    return pl.pallas_call(
        body, name=name, grid=(width // CONV_TC, nt),
        in_specs=[ANY, pl.BlockSpec((tt, CONV_TC), lambda j, i: (i, c0 + j)),
                  pl.BlockSpec((HALO, CONV_TC), lambda j, i: (jnp.maximum(i * (tt // HALO) - 1, 0), c0 + j)),
                  pl.BlockSpec((HALO, CONV_TC), lambda j, i: (after(i), c0 + j)),
                  pl.BlockSpec((4, CONV_TC), lambda j, i: (0, j)), pl.BlockSpec((1, CONV_TC), lambda j, i: (0, j)),
                  pl.BlockSpec((tt, CONV_TC), lambda j, i: (i, j)),
                  pl.BlockSpec((HALO, CONV_TC), lambda j, i: (after(i), j))],
        out_specs=[pl.BlockSpec((tt, CONV_TC), lambda j, i: (i, c0 + j)),
                   pl.BlockSpec((HALO, CONV_TC), lambda j, i: (0, j))],
        out_shape=[jax.ShapeDtypeStruct(dproj_main.shape, dproj_main.dtype), jax.ShapeDtypeStruct((HALO, width), F32)],
        input_output_aliases={0: 0},
        compiler_params=_params(("arbitrary", "arbitrary")),
    )(dproj_main, proj_main, proj_main, proj_main, w, b, dout, dout)


def _ssd_parts(xbc_ref):
    part = lambda o, w: [xbc_ref[:, g * SSD_GC + o:g * SSD_GC + o + w] for g in range(SSD_GROUPS)]
    return part(0, SSD_GW), part(SSD_GW, SSD_STATE), part(SSD_GW + SSD_STATE, SSD_STATE)


def _group_cols(ref):
    return [ref[:, g * SSD_GW:(g + 1) * SSD_GW] for g in range(SSD_GROUPS)]


def _gdn_parts(qkv_ref):
    part = lambda o: [qkv_ref[:, j * GDN_HC + o:j * GDN_HC + o + GDN_DK] for j in range(GDN_HB)]
    return part(0), part(GDN_DK), part(2 * GDN_DK)


def _head_cols(ref):
    return [ref[:, j * GDN_DV:(j + 1) * GDN_DV] for j in range(GDN_HB)]


def _first_head():
    return 0 if GDN_HB == GDN_HEADS else pl.program_id(1) * GDN_HB


def ssd_fwd(conv_ssd, proj_main, proj_small, normw, alog, dtb, dvec):
    t = conv_ssd.shape[0]
    nc = t // CHUNK

    groups = range(SSD_GROUPS)

    def body(xbc_ref, z_ref, sm_ref, nw_ref, al_ref, db_ref, dv_ref, y_ref, hist_ref, state_ref):
        @pl.when(pl.program_id(0) == 0)
        def _():
            state_ref[...] = jnp.zeros(state_ref.shape, F32)

        states = [state_ref[g] for g in groups]
        for g in groups:
            hist_ref[0, g] = states[g]
        ys, new_states = ssd_chunk(*_ssd_parts(xbc_ref), _group_cols(z_ref), sm_ref[...], _group_cols(nw_ref),
                                   al_ref[...], db_ref[...], dv_ref[...], states)
        for g in groups:
            y_ref[:, g * SSD_GW:(g + 1) * SSD_GW] = ys[g].astype(MXU_DTYPE)
            state_ref[g] = new_states[g]

    return pl.pallas_call(
        body, name="ssd_fwd", grid=(nc,),
        in_specs=[pl.BlockSpec((CHUNK, SSD_CONV), lambda c: (c, (COL_SSD - COL_CONV) // SSD_CONV)),
                  pl.BlockSpec((CHUNK, SSD_WIDTH), lambda c: (c, COL_Z // SSD_WIDTH)),
                  pl.BlockSpec((CHUNK, LANES), lambda c: (c, 0)),
                  _full((1, SSD_WIDTH)), _full((1, LANES)), _full((1, LANES)), _full((1, LANES))],
        out_specs=[pl.BlockSpec((CHUNK, SSD_WIDTH), lambda c: (c, 0)),
                   pl.BlockSpec((1, SSD_GROUPS, SSD_STATE, SSD_GW), lambda c: (c, 0, 0, 0))],
        out_shape=[jax.ShapeDtypeStruct((t, SSD_WIDTH), MXU_DTYPE),
                   jax.ShapeDtypeStruct((nc, SSD_GROUPS, SSD_STATE, SSD_GW), F32)],
        scratch_shapes=[pltpu.VMEM((SSD_GROUPS, SSD_STATE, SSD_GW), F32)],
        compiler_params=_params(("arbitrary",)),
    )(conv_ssd, proj_main, proj_small, normw, alog, dtb, dvec)


def _accumulate(ref, first, value):
    @pl.when(first)
    def _():
        ref[...] = value

    @pl.when(jnp.logical_not(first))
    def _():
        ref[...] += value


def ssd_bwd(conv_ssd, proj_main, proj_small, normw, alog, dtb, dvec, hist, dy):
    t = conv_ssd.shape[0]
    nc = t // CHUNK
    rev = lambda c: nc - 1 - c
    groups = range(SSD_GROUPS)

    def body(xbc_ref, z_ref, sm_ref, nw_ref, al_ref, db_ref, dv_ref, hist_ref, dy_ref,
             dxbc_ref, dz_ref, dsm_ref, dnw_ref, dal_ref, ddb_ref, ddv_ref, dstate_ref):
        first = pl.program_id(0) == 0

        @pl.when(first)
        def _():
            dstate_ref[...] = jnp.zeros(dstate_ref.shape, F32)

        _, vjp = jax.vjp(ssd_chunk, *_ssd_parts(xbc_ref), _group_cols(z_ref), sm_ref[...], _group_cols(nw_ref),
                         al_ref[...], db_ref[...], dv_ref[...], [hist_ref[0, g] for g in groups])
        dxs, dbm, dcm, dz, dsm, dnw, dal, ddb, ddv, dstate = vjp(
            (_group_cols(dy_ref), [dstate_ref[g] for g in groups]))
        for g in groups:
            base = g * SSD_GC
            dxbc_ref[:, base:base + SSD_GW] = dxs[g]
            dxbc_ref[:, base + SSD_GW:base + SSD_GW + SSD_STATE] = dbm[g]
            dxbc_ref[:, base + SSD_GW + SSD_STATE:base + SSD_GC] = dcm[g]
            dz_ref[:, g * SSD_GW:(g + 1) * SSD_GW] = dz[g].astype(dz_ref.dtype)
            dstate_ref[g] = dstate[g]
        dsm_ref[...] = dsm
        _accumulate(dnw_ref, first, join_lanes(dnw))
        _accumulate(dal_ref, first, dal)
        _accumulate(ddb_ref, first, ddb)
        _accumulate(ddv_ref, first, ddv)

    return pl.pallas_call(
        body, name="ssd_bwd", grid=(nc,),
        in_specs=[pl.BlockSpec((CHUNK, SSD_CONV), lambda c: (rev(c), (COL_SSD - COL_CONV) // SSD_CONV)),
                  pl.BlockSpec((CHUNK, SSD_WIDTH), lambda c: (rev(c), COL_Z // SSD_WIDTH)),
                  pl.BlockSpec((CHUNK, LANES), lambda c: (rev(c), 0)),
                  _full((1, SSD_WIDTH)), _full((1, LANES)), _full((1, LANES)), _full((1, LANES)),
                  pl.BlockSpec((1, SSD_GROUPS, SSD_STATE, SSD_GW), lambda c: (rev(c), 0, 0, 0)),
                  pl.BlockSpec((CHUNK, SSD_WIDTH), lambda c: (rev(c), 0))],
        out_specs=[pl.BlockSpec((CHUNK, SSD_CONV), lambda c: (rev(c), 0)),
                   pl.BlockSpec((CHUNK, SSD_WIDTH), lambda c: (rev(c), COL_Z // SSD_WIDTH)),
                   pl.BlockSpec((CHUNK, LANES), lambda c: (rev(c), 0)),
                   _full((1, SSD_WIDTH)), _full((1, LANES)), _full((1, LANES)), _full((1, LANES))],
        out_shape=[jax.ShapeDtypeStruct((t, SSD_CONV), F32), jax.ShapeDtypeStruct((t, MAIN), MXU_DTYPE),
                   jax.ShapeDtypeStruct((t, LANES), F32), jax.ShapeDtypeStruct((1, SSD_WIDTH), F32),
                   jax.ShapeDtypeStruct((1, LANES), F32), jax.ShapeDtypeStruct((1, LANES), F32),
                   jax.ShapeDtypeStruct((1, LANES), F32)],
        scratch_shapes=[pltpu.VMEM((SSD_GROUPS, SSD_STATE, SSD_GW), F32)],
        compiler_params=_params(("arbitrary",)),
    )(conv_ssd, proj_main, proj_small, normw, alog, dtb, dvec, hist, dy)


def gdn_fwd(conv_gdn, proj_main, proj_small, normw, alog, dtb):
    t = conv_gdn.shape[0]
    nc = t // CHUNK

    hb = GDN_HB
    gate_blk = COL_GATE // (GDN_DV * hb)

    def body(qkv_ref, gate_ref, sm_ref, nw_ref, al_ref, db_ref, y_ref, hist_ref, t_ref, state_ref):
        h0 = _first_head()

        @pl.when(pl.program_id(0) == 0)
        def _():
            for j in range(hb):
                state_ref[h0 + j] = jnp.zeros((GDN_DK, GDN_DV), F32)

        states = [state_ref[h0 + j] for j in range(hb)]
        for j in range(hb):
            hist_ref[0, j] = states[j]
        qs, ks, vs = _gdn_parts(qkv_ref)
        ys, new_states, ts = gdn_chunk(h0, qs, ks, vs, sm_ref[...], _head_cols(gate_ref), nw_ref[...], al_ref[...],
                                       db_ref[...], states)
        for j in range(hb):
            y_ref[:, j * GDN_DV:(j + 1) * GDN_DV] = ys[j].astype(MXU_DTYPE)
            state_ref[h0 + j] = new_states[j]
            t_ref[0, j] = ts[j]

    return pl.pallas_call(
        body, name="gdn_fwd", grid=(nc, GDN_HEADS // hb),
        in_specs=[pl.BlockSpec((CHUNK, GDN_HC * hb), lambda c, h: (c, h)),
                  pl.BlockSpec((CHUNK, GDN_DV * hb), lambda c, h: (c, gate_blk + h)),
                  pl.BlockSpec((CHUNK, LANES), lambda c, h: (c, 0)),
                  _full((1, GDN_DV)), _full((1, LANES)), _full((1, LANES))],
        out_specs=[pl.BlockSpec((CHUNK, GDN_DV * hb), lambda c, h: (c, h)),
                   pl.BlockSpec((1, hb, GDN_DK, GDN_DV), lambda c, h: (c, h, 0, 0)),
                   pl.BlockSpec((1, hb, CHUNK, CHUNK), lambda c, h: (c, h, 0, 0))],
        out_shape=[jax.ShapeDtypeStruct((t, GDN_W), MXU_DTYPE),
                   jax.ShapeDtypeStruct((nc, GDN_HEADS, GDN_DK, GDN_DV), F32),
                   jax.ShapeDtypeStruct((nc, GDN_HEADS, CHUNK, CHUNK), F32)],
        scratch_shapes=[pltpu.VMEM((GDN_HEADS, GDN_DK, GDN_DV), F32)],
        compiler_params=_params(("arbitrary", "arbitrary")),
    )(conv_gdn, proj_main, proj_small, normw, alog, dtb)


def gdn_bwd(dproj_main, conv_gdn, proj_main, proj_small, normw, alog, dtb, hist, t_inv, dy):
    t = conv_gdn.shape[0]
    nc = t // CHUNK
    rev = lambda c: nc - 1 - c
    hb = GDN_HB
    gate_blk = COL_GATE // (GDN_DV * hb)

    def body(alias_ref, qkv_ref, gate_ref, sm_ref, nw_ref, al_ref, db_ref, hist_ref, t_ref, dy_ref,
             dgate_ref, dqkv_ref, dsm_ref, dnw_ref, dal_ref, ddb_ref, dstate_ref):
        del alias_ref
        c, h = pl.program_id(0), pl.program_id(1)
        h0 = _first_head()

        @pl.when(c == 0)
        def _():
            for j in range(hb):
                dstate_ref[h0 + j] = jnp.zeros((GDN_DK, GDN_DV), F32)

        saved = [t_ref[0, j] for j in range(hb)]

        def fn(qs, ks, vs, small, gates, nw, al, db, states):
            return gdn_chunk(h0, qs, ks, vs, small, gates, nw, al, db, states, saved)[:2]

        qs, ks, vs = _gdn_parts(qkv_ref)
        _, vjp = jax.vjp(fn, qs, ks, vs, sm_ref[...], _head_cols(gate_ref), nw_ref[...], al_ref[...], db_ref[...],
                         [hist_ref[0, j] for j in range(hb)])
        dqs, dks, dvs, dsm, dgates, dnw, dal, ddb, dstates = vjp(
            (_head_cols(dy_ref), [dstate_ref[h0 + j] for j in range(hb)]))
        for j in range(hb):
            base = j * GDN_HC
            dqkv_ref[:, base:base + GDN_DK] = dqs[j]
            dqkv_ref[:, base + GDN_DK:base + 2 * GDN_DK] = dks[j]
            dqkv_ref[:, base + 2 * GDN_DK:base + GDN_HC] = dvs[j]
            dgate_ref[:, j * GDN_DV:(j + 1) * GDN_DV] = dgates[j].astype(dgate_ref.dtype)
            dstate_ref[h0 + j] = dstates[j]
        _accumulate(dsm_ref, h == 0, dsm)
        first = jnp.logical_and(c == 0, h == 0)
        _accumulate(dnw_ref, first, dnw)
        _accumulate(dal_ref, first, dal)
        _accumulate(ddb_ref, first, ddb)

    return pl.pallas_call(
        body, name="gdn_bwd", grid=(nc, GDN_HEADS // hb),
        in_specs=[ANY, pl.BlockSpec((CHUNK, GDN_HC * hb), lambda c, h: (rev(c), h)),
                  pl.BlockSpec((CHUNK, GDN_DV * hb), lambda c, h: (rev(c), gate_blk + h)),
                  pl.BlockSpec((CHUNK, LANES), lambda c, h: (rev(c), 0)),
                  _full((1, GDN_DV)), _full((1, LANES)), _full((1, LANES)),
                  pl.BlockSpec((1, hb, GDN_DK, GDN_DV), lambda c, h: (rev(c), h, 0, 0)),
                  pl.BlockSpec((1, hb, CHUNK, CHUNK), lambda c, h: (rev(c), h, 0, 0)),
                  pl.BlockSpec((CHUNK, GDN_DV * hb), lambda c, h: (rev(c), h))],
        out_specs=[pl.BlockSpec((CHUNK, GDN_DV * hb), lambda c, h: (rev(c), gate_blk + h)),
                   pl.BlockSpec((CHUNK, GDN_HC * hb), lambda c, h: (rev(c), h)),
                   pl.BlockSpec((CHUNK, LANES), lambda c, h: (rev(c), 0)),
                   _full((1, GDN_DV)), _full((1, LANES)), _full((1, LANES))],
        out_shape=[jax.ShapeDtypeStruct(dproj_main.shape, dproj_main.dtype), jax.ShapeDtypeStruct((t, GDN_CONV), F32),
                   jax.ShapeDtypeStruct((t, LANES), F32), jax.ShapeDtypeStruct((1, GDN_DV), F32),
                   jax.ShapeDtypeStruct((1, LANES), F32), jax.ShapeDtypeStruct((1, LANES), F32)],
        scratch_shapes=[pltpu.VMEM((GDN_HEADS, GDN_DK, GDN_DV), F32)],
        input_output_aliases={0: 0},
        compiler_params=_params(("arbitrary", "arbitrary")),
    )(dproj_main, conv_gdn, proj_main, proj_small, normw, alog, dtb, hist, t_inv, dy)


def out_proj_loss(x, y_ssd, y_gdn, w_out, final_w, target):
    t = x.shape[0]
    tm = min(256, t)

    def body(x_ref, ys_ref, yg_ref, wo_ref, fw_ref, tg_ref, loss_ref, dhid_ref, dys_ref, dyg_ref, dwo_ref, dfw_ref):
        i = pl.program_id(0)
        ys, yg = ys_ref[...], yg_ref[...]
        wo_s, wo_g = wo_ref[:SSD_WIDTH, :], wo_ref[SSD_WIDTH:, :]
        hid = x_ref[...] + _raw_dot(ys, wo_s, 1, 0) + _raw_dot(yg, wo_g, 1, 0)
        out, vjp = jax.vjp(rmsnorm, hid, fw_ref[...])
        err = out - tg_ref[...]
        loss = 0.5 * jnp.sum(jnp.mean(err * err, axis=-1, keepdims=True), axis=0, keepdims=True)
        dhid, dfw = vjp(err * (1.0 / D_MODEL))
        dhid_ref[...] = dhid
        dys_ref[...] = _raw_dot(dhid, wo_s, 1, 1)
        dyg_ref[...] = _raw_dot(dhid, wo_g, 1, 1)
        first = i == 0
        _accumulate(loss_ref, first, jnp.broadcast_to(loss, loss_ref.shape))
        _accumulate(dfw_ref, first, dfw)

        @pl.when(first)
        def _():
            dwo_ref[:SSD_WIDTH, :] = _raw_dot(ys, dhid, 0, 0)
            dwo_ref[SSD_WIDTH:, :] = _raw_dot(yg, dhid, 0, 0)

        @pl.when(i > 0)
        def _():
            dwo_ref[:SSD_WIDTH, :] += _raw_dot(ys, dhid, 0, 0)
            dwo_ref[SSD_WIDTH:, :] += _raw_dot(yg, dhid, 0, 0)

    row = lambda w: pl.BlockSpec((tm, w), lambda i: (i, 0))
    return pl.pallas_call(
        body, name="out_proj_loss", grid=(t // tm,),
        in_specs=[row(D_MODEL), row(SSD_WIDTH), row(GDN_W), _full((SSD_WIDTH + GDN_W, D_MODEL)), _full((1, D_MODEL)),
                  row(D_MODEL)],
        out_specs=[_full((8, LANES)), row(D_MODEL), row(SSD_WIDTH), row(GDN_W), _full((SSD_WIDTH + GDN_W, D_MODEL)),
                   _full((1, D_MODEL))],
        out_shape=[jax.ShapeDtypeStruct((8, LANES), F32), jax.ShapeDtypeStruct((t, D_MODEL), F32),
                   jax.ShapeDtypeStruct((t, SSD_WIDTH), F32), jax.ShapeDtypeStruct((t, GDN_W), F32),
                   jax.ShapeDtypeStruct((SSD_WIDTH + GDN_W, D_MODEL), F32), jax.ShapeDtypeStruct((1, D_MODEL), F32)],
        compiler_params=_params(("arbitrary",)),
    )(x, y_ssd, y_gdn, w_out, final_w, target)


def in_proj_bwd_x(x, normw, w_main, w_small, dproj_main, dsmall_a, dsmall_b, dhid, slabbed):
    t = x.shape[0]
    tm = min(256, t)
    ni = t // tm
    ns = len(slabbed)

    def body(x_ref, nw_ref, wm_ref, ws_ref, dp_ref, da_ref, db_ref, dh_ref, *rest):
        slab_refs, (gx_ref, dnw_ref), land_refs = rest[:ns], rest[ns:ns + 2], rest[ns + 2:2 * ns + 2]
        sems = rest[2 * ns + 2:]
        i = pl.program_id(0)
        start, finish = _slab_exchange(slab_refs, land_refs, ns, *sems)

        @pl.when(i == 0)
        def _():
            start()

        du = _raw_dot(dp_ref[...], wm_ref[...], 1, 1) + _raw_dot(da_ref[...] + db_ref[...], ws_ref[...], 1, 1)
        _, vjp = jax.vjp(rmsnorm, x_ref[...], nw_ref[...])
        dx, dnw = vjp(du)
        gx_ref[...] = dx + dh_ref[...]
        _accumulate(dnw_ref, i == 0, dnw)

        @pl.when(i == ni - 1)
        def _():
            finish()

    row = lambda w: pl.BlockSpec((tm, w), lambda i: (i, 0))
    out = pl.pallas_call(
        body, name="in_proj_bwd_x", grid=(ni,),
        in_specs=[row(D_MODEL), _full((1, D_MODEL)), _full((D_MODEL, MAIN)), _full((D_MODEL, LANES)), row(MAIN),
                  row(LANES), row(LANES), row(D_MODEL)] + [HBM] * ns,
        out_specs=[row(D_MODEL), _full((1, D_MODEL))] + [HBM] * ns,
        out_shape=[jax.ShapeDtypeStruct((t, D_MODEL), F32), jax.ShapeDtypeStruct((1, D_MODEL), F32)]
        + _slab_exchange_shapes(slabbed, []),
        scratch_shapes=_slab_exchange_sems(ns),
        compiler_params=_params(("arbitrary",)),
    )(x, normw, w_main, w_small, dproj_main, dsmall_a, dsmall_b, dhid, *slabbed)
    return out[0], out[1], out[2:]


def in_proj_bwd_w(u, dproj_main, dsmall_a, dsmall_b):
    t = u.shape[0]
    tm, tn = min(1024, t), MAIN // 4

    def body(u_ref, dp_ref, da_ref, db_ref, dwm_ref, dws_ref):
        j, i = pl.program_id(0), pl.program_id(1)
        uu = u_ref[...]
        _accumulate(dwm_ref, i == 0, _raw_dot(uu, dp_ref[...], 0, 0))

        @pl.when(j == 0)
        def _():
            _accumulate(dws_ref, i == 0, _raw_dot(uu, da_ref[...] + db_ref[...], 0, 0))

    return pl.pallas_call(
        body, name="in_proj_bwd_w", grid=(MAIN // tn, t // tm),
        in_specs=[pl.BlockSpec((tm, D_MODEL), lambda j, i: (i, 0)), pl.BlockSpec((tm, tn), lambda j, i: (i, j)),
                  pl.BlockSpec((tm, LANES), lambda j, i: (i, 0)), pl.BlockSpec((tm, LANES), lambda j, i: (i, 0))],
        out_specs=[pl.BlockSpec((D_MODEL, tn), lambda j, i: (0, j)), _full((D_MODEL, LANES))],
        out_shape=[jax.ShapeDtypeStruct((D_MODEL, MAIN), F32), jax.ShapeDtypeStruct((D_MODEL, LANES), F32)],
        compiler_params=_params(("arbitrary", "arbitrary")),
    )(u, dproj_main, dsmall_a, dsmall_b)


def sum_slabs(a, name):
    n, rows, cols = a.shape
    tr = 64 if rows % 64 == 0 else rows

    def body(a_ref, o_ref):
        acc = a_ref[0].astype(F32)
        for d in range(1, n):
            acc = acc + a_ref[d].astype(F32)
        o_ref[...] = acc

    return pl.pallas_call(
        body, name=name, grid=(rows // tr,),
        in_specs=[pl.BlockSpec((n, tr, cols), lambda i: (0, i, 0))],
        out_specs=pl.BlockSpec((tr, cols), lambda i: (i, 0)),
        out_shape=jax.ShapeDtypeStruct((rows, cols), F32),
        compiler_params=_params(("arbitrary",)),
    )(a)


def adamw(w, g, m, v, name):
    rows, cols = w.shape
    tr = 128 if rows % 128 == 0 else rows

    def body(w_ref, g_ref, m_ref, v_ref, d_ref, nm_ref, nv_ref):
        gg = g_ref[...]
        nm = ADAM_B1 * m_ref[...] + (1.0 - ADAM_B1) * gg
        nv = ADAM_B2 * v_ref[...] + (1.0 - ADAM_B2) * (gg * gg)
        m_hat = nm / (1.0 - ADAM_B1 ** ADAM_STEP)
        v_hat = nv / (1.0 - ADAM_B2 ** ADAM_STEP)
        d_ref[...] = -ADAM_LR * (m_hat / (jnp.sqrt(v_hat) + ADAM_EPS) + ADAM_WD * w_ref[...])
        nm_ref[...] = nm
        nv_ref[...] = nv

    spec = pl.BlockSpec((tr, cols), lambda i: (i, 0))
    shp = jax.ShapeDtypeStruct((rows, cols), F32)
    return pl.pallas_call(
        body, name=name, grid=(rows // tr,), in_specs=[spec] * 4, out_specs=[spec] * 3, out_shape=[shp] * 3,
        compiler_params=_params(("arbitrary",)),
    )(w, g, m, v)


def _my_place():
    return lax.axis_index("x"), lax.axis_index("y"), lax.axis_index("c")


def gather_weights(big, small):
    nb, n = len(big), len(big) + len(small)
    parts = 4

    def body(*refs):
        srcs, outs = refs[:n], refs[n:2 * n]
        land_a, land_b = refs[2 * n:2 * n + nb], refs[2 * n + nb:2 * n + 2 * nb]
        send_sems, recv_sems, fwd_send, fwd_recv, local_sems = refs[2 * n + 2 * nb:]
        x, y, c = _my_place()
        me = 2 * x + y
        chips = [(1 - x, y), (x, 1 - y), (1 - x, 1 - y)]
        half = [a.shape[0] // 2 for a in big]

        def ici(j, i):
            px, py = chips[j]
            if i < nb:
                src, dst = srcs[i].at[pl.ds(c * half[i], half[i])], land_a[i].at[j]
            else:
                src, dst = srcs[i], outs[i].at[me]
            return pltpu.make_async_remote_copy(src_ref=src, dst_ref=dst, send_sem=send_sems.at[j * n + i],
                                                recv_sem=recv_sems.at[j * n + i], device_id=(px, py, c),
                                                device_id_type=MESH)

        def ici_arrival(j, i):
            px, py = chips[j]
            dst = land_a[i].at[j] if i < nb else outs[i].at[2 * px + py]
            return pltpu.make_async_remote_copy(src_ref=dst, dst_ref=dst, send_sem=send_sems.at[j * n + i],
                                                recv_sem=recv_sems.at[j * n + i], device_id=(px, py, c),
                                                device_id_type=MESH)

        def forward(j, i, p):
            rows = half[i] // parts
            k = (j * nb + i) * parts + p
            return pltpu.make_async_remote_copy(
                src_ref=land_a[i].at[j, pl.ds(p * rows, rows)], dst_ref=land_b[i].at[j, pl.ds(p * rows, rows)],
                send_sem=fwd_send.at[k], recv_sem=fwd_recv.at[k], device_id=(x, y, 1 - c), device_id_type=MESH)

        def store(j, i, from_sibling):
            px, py = chips[j]
            buf, h = (land_b, 1 - c) if from_sibling else (land_a, c)
            k = n + (j * nb + i) * 2 + (1 if from_sibling else 0)
            return pltpu.make_async_copy(buf[i].at[j], outs[i].at[2 * px + py, pl.ds(h * half[i], half[i])],
                                         local_sems.at[k])

        own = [pltpu.make_async_copy(srcs[i], outs[i].at[me], local_sems.at[i]) for i in range(n)]
        sends = [ici(j, i) for j in range(3) for i in range(n)]
        for cp in own + sends:
            cp.start()
        pending = []
        for j in range(3):
            for i in range(n):
                ici_arrival(j, i).wait_recv()
                if i < nb:
                    fw = [forward(j, i, p) for p in range(parts)]
                    st = store(j, i, False)
                    for cp in fw + [st]:
                        cp.start()
                    pending += [cp.wait_send for cp in fw] + [st.wait]
        for j in range(3):
            for i in range(nb):
                for p in range(parts):
                    forward(j, i, p).wait_recv()
                st = store(j, i, True)
                st.start()
                pending.append(st.wait)
        for cp in sends:
            cp.wait_send()
        for wait in pending:
            wait()
        for cp in own:
            cp.wait()

    shards = list(big) + list(small)
    lands = [pltpu.VMEM((3, a.shape[0] // 2) + a.shape[1:], a.dtype) for a in big]
    return pl.pallas_call(
        body, name="gather_weights",
        in_specs=[HBM] * n, out_specs=[HBM] * n,
        out_shape=[jax.ShapeDtypeStruct((N_CHIP,) + s.shape, s.dtype) for s in shards],
        scratch_shapes=lands + lands + [
            pltpu.SemaphoreType.DMA((3 * n,)), pltpu.SemaphoreType.DMA((3 * n,)),
            pltpu.SemaphoreType.DMA((3 * nb * parts,)), pltpu.SemaphoreType.DMA((3 * nb * parts,)),
            pltpu.SemaphoreType.DMA((n + 6 * nb,))],
        compiler_params=pltpu.CompilerParams(vmem_limit_bytes=VMEM_LIMIT),
    )(*shards)


def _peer(x, y, c, mask):
    mx, my, mc = (mask >> 2) & 1, (mask >> 1) & 1, mask & 1
    return (x ^ mx if mx else x, y ^ my if my else y, c ^ mc if mc else c)


def _slab_exchange_shapes(slabbed, replicated):
    return ([jax.ShapeDtypeStruct(a.shape, a.dtype) for a in slabbed]
            + [jax.ShapeDtypeStruct((N_DEV,) + a.shape, a.dtype) for a in replicated])


def _slab_exchange_sems(n):
    return [pltpu.SemaphoreType.DMA((7 * n,)), pltpu.SemaphoreType.DMA((7 * n,)), pltpu.SemaphoreType.DMA((n,))]


def _slab_exchange(srcs, outs, ns, send_sems, recv_sems, local_sems):
    n = len(srcs)
    x, y, c = _my_place()
    me = 4 * x + 2 * y + c

    def piece(i, dev):
        return srcs[i].at[dev] if i < ns else srcs[i]

    def copies(arriving):
        out = []
        for mask in range(1, N_DEV):
            px, py, pc = _peer(x, y, c, mask)
            dev = 4 * px + 2 * py + pc
            for i in range(n):
                k = (mask - 1) * n + i
                out.append(pltpu.make_async_remote_copy(
                    src_ref=piece(i, dev), dst_ref=outs[i].at[dev if arriving else me], send_sem=send_sems.at[k],
                    recv_sem=recv_sems.at[k], device_id=(px, py, pc), device_id_type=MESH))
        return out

    def local():
        return [pltpu.make_async_copy(piece(i, me), outs[i].at[me], local_sems.at[i]) for i in range(n)]

    def start():
        for cp in local() + copies(False):
            cp.start()

    def finish():
        for cp in copies(True):
            cp.wait_recv()
        for cp in copies(False):
            cp.wait_send()
        for cp in local():
            cp.wait()

    return start, finish


def exchange_halves(halves, replicated):
    n, nr = len(halves), len(replicated)
    streams = 8

    def body(*refs):
        srcs, rep_srcs, outs, rep_outs = refs[:n], refs[n:n + nr], refs[n + nr:2 * n + nr], refs[2 * n + nr:2 * (n + nr)]
        refs = refs[2 * (n + nr):]
        mine, theirs = refs[:n], refs[n:2 * n]
        send_sems, recv_sems, in_sems, out_sems = refs[2 * n:2 * n + 4]
        rep_start, rep_finish = _slab_exchange(rep_srcs, rep_outs, 0, *refs[2 * n + 4:])
        rep_start()
        x, y, c = _my_place()
        loads = [pltpu.make_async_copy(srcs[i], mine[i], in_sems.at[i]) for i in range(n)]
        for cp in loads:
            cp.start()
        for cp in loads:
            cp.wait()

        def chunk_copy(i, s):
            rows = halves[i].shape[0] // streams
            k = i * streams + s
            return pltpu.make_async_remote_copy(
                src_ref=mine[i].at[pl.ds(s * rows, rows)], dst_ref=theirs[i].at[pl.ds(s * rows, rows)],
                send_sem=send_sems.at[k], recv_sem=recv_sems.at[k], device_id=(x, y, 1 - c), device_id_type=MESH)

        sends = [chunk_copy(i, s) for i in range(n) for s in range(streams)]
        for cp in sends:
            cp.start()
        own = [pltpu.make_async_copy(mine[i], outs[i].at[c], out_sems.at[i]) for i in range(n)]
        for cp in own:
            cp.start()
        for cp in sends:
            cp.wait_recv()
        got = [pltpu.make_async_copy(theirs[i], outs[i].at[1 - c], out_sems.at[n + i]) for i in range(n)]
        for cp in got:
            cp.start()
        for cp in sends:
            cp.wait_send()
        for cp in own + got:
            cp.wait()
        rep_finish()

    vmem = [pltpu.VMEM(a.shape, a.dtype) for a in halves]
    out = pl.pallas_call(
        body, name="exchange_halves",
        in_specs=[HBM] * (n + nr), out_specs=[HBM] * (n + nr),
        out_shape=[jax.ShapeDtypeStruct((2,) + a.shape, a.dtype) for a in halves]
        + _slab_exchange_shapes([], replicated),
        scratch_shapes=vmem + vmem + [pltpu.SemaphoreType.DMA((n * streams,)), pltpu.SemaphoreType.DMA((n * streams,)),
                                      pltpu.SemaphoreType.DMA((n,)), pltpu.SemaphoreType.DMA((2 * n,))]
        + _slab_exchange_sems(nr),
        compiler_params=pltpu.CompilerParams(vmem_limit_bytes=VMEM_LIMIT),
    )(*halves, *replicated)
    return out[:n], out[n:]


def _pack_cols(pieces):
    offs, pos = [], 0
    for a in pieces:
        offs.append(pos)
        pos += a.shape[1]
    rows8 = [jnp.pad(a.astype(F32), ((0, 8 - a.shape[0]), (0, 0))) for a in pieces]
    return jnp.concatenate(rows8, axis=1), offs


def adamw_many(ws, gs, ms, vs):
    n = len(ws)

    def body(*refs):
        w_r, g_r, m_r, v_r = refs[:n], refs[n:2 * n], refs[2 * n:3 * n], refs[3 * n:4 * n]
        d_o, m_o, v_o = refs[4 * n:5 * n], refs[5 * n:6 * n], refs[6 * n:7 * n]
        for i in range(n):
            gg = g_r[i][...]
            nm = ADAM_B1 * m_r[i][...] + (1.0 - ADAM_B1) * gg
            nv = ADAM_B2 * v_r[i][...] + (1.0 - ADAM_B2) * (gg * gg)
            m_hat = nm / (1.0 - ADAM_B1 ** ADAM_STEP)
            v_hat = nv / (1.0 - ADAM_B2 ** ADAM_STEP)
            d_o[i][...] = -ADAM_LR * (m_hat / (jnp.sqrt(v_hat) + ADAM_EPS) + ADAM_WD * w_r[i][...])
            m_o[i][...] = nm
            v_o[i][...] = nv

    shapes = [jax.ShapeDtypeStruct(w.shape, F32) for w in ws]
    out = pl.pallas_call(body, name="adamw_small", out_shape=shapes * 3,
                         compiler_params=pltpu.CompilerParams(vmem_limit_bytes=VMEM_LIMIT))(*ws, *gs, *ms, *vs)
    return out[:n], out[n:2 * n], out[2 * n:]


def _lanes(vec, start):
    n = vec.shape[-1]
    return jnp.pad(vec.reshape(1, n).astype(F32), ((0, 0), (start, LANES - start - n)))


def kernel(x, norm_w, w_in, ssd_conv_w, ssd_conv_b, ssd_dt_bias, ssd_a_log, ssd_d, ssd_norm_w, gdn_conv_w, gdn_dt_bias, gdn_a_log, gdn_norm_w, w_out, final_norm_w, loss_target, m_norm_w, m_w_in, m_ssd_conv_w, m_ssd_conv_b, m_ssd_dt_bias, m_ssd_a_log, m_ssd_d, m_ssd_norm_w, m_gdn_conv_w, m_gdn_dt_bias, m_gdn_a_log, m_gdn_norm_w, m_w_out, m_final_norm_w, v_norm_w, v_w_in, v_ssd_conv_w, v_ssd_conv_b, v_ssd_dt_bias, v_ssd_a_log, v_ssd_d, v_ssd_norm_w, v_gdn_conv_w, v_gdn_dt_bias, v_gdn_a_log, v_gdn_norm_w, v_w_out, v_final_norm_w):
    xs = x[0]
    target = loss_target[0]
    chip = 2 * lax.axis_index("x") + lax.axis_index("y")
    w_in_shard, w_out_shard = w_in[0], w_out[0]
    in_cols = w_in_shard.shape[1]
    out_rows = w_out_shard.shape[0]

    g_in, g_out, g_cs, g_cg = gather_weights(
        [w_in_shard.astype(MXU_DTYPE), w_out_shard.astype(MXU_DTYPE)], [ssd_conv_w[0], gdn_conv_w[0]])
    w_in_full = jnp.transpose(g_in, (1, 0, 2)).reshape(D_MODEL, IN_DIM)
    w_out_full = g_out.reshape(N_CHIP * out_rows, D_MODEL)
    cw_ssd = _ssd_perm(jnp.transpose(g_cs, (1, 0, 2)).reshape(4, SSD_CONV))
    cw_gdn = _gdn_perm(jnp.transpose(g_cg, (1, 0, 2)).reshape(4, GDN_CONV))
    cb_ssd = _ssd_perm(ssd_conv_b)
    cb_gdn = jnp.zeros((1, GDN_CONV), F32)
    o_xbc, o_dt, o_gate, o_qkv, o_ab = 1024, 2560, 2576, 3600, 6672
    w_main = jnp.concatenate([w_in_full[:, :o_xbc], w_in_full[:, o_gate:o_qkv], _gdn_perm(w_in_full[:, o_qkv:o_ab]),
                              _ssd_perm(w_in_full[:, o_xbc:o_dt])], axis=1)
    w_small = jnp.concatenate([w_in_full[:, o_dt:o_gate], w_in_full[:, o_ab:],
                               jnp.zeros((D_MODEL, LANES - 32), MXU_DTYPE)], axis=1)
    alog = _lanes(ssd_a_log, 0) + _lanes(gdn_a_log, LANE_GA)
    dtb = _lanes(ssd_dt_bias, 0) + _lanes(gdn_dt_bias, LANE_GA)
    dvec = _lanes(ssd_d, 0)
    fw = final_norm_w.reshape(1, D_MODEL)

    proj_main, proj_small, u = in_proj(xs, norm_w, w_main, w_small)
    proj_main, conv_out = in_proj_conv(proj_main, u, w_main, jnp.concatenate([cw_gdn, cw_ssd], axis=1),
                                       jnp.concatenate([cb_gdn, cb_ssd], axis=1))
    conv_ssd = conv_gdn = conv_out
    y_ssd, hist_ssd = ssd_fwd(conv_ssd, proj_main, proj_small, ssd_norm_w, alog, dtb, dvec)
    y_gdn, hist_gdn, tinv_gdn = gdn_fwd(conv_gdn, proj_main, proj_small, gdn_norm_w, alog, dtb)

    loss_blk, dhid, dy_ssd, dy_gdn, d_w_out, d_fw = out_proj_loss(xs, y_ssd, y_gdn, w_out_full, fw, target)
    dconv_ssd, dproj_main, dsmall_ssd, d_ssd_nw, d_alog_s, d_dtb_s, d_dvec = ssd_bwd(
        conv_ssd, proj_main, proj_small, ssd_norm_w, alog, dtb, dvec, hist_ssd, dy_ssd)
    dproj_main, dconv_gdn, dsmall_gdn, d_gdn_nw, d_alog_g, d_dtb_g = gdn_bwd(
        dproj_main, conv_gdn, proj_main, proj_small, gdn_norm_w, alog, dtb, hist_gdn, tinv_gdn, dy_gdn)
    dproj_main, dwb_ssd = conv_bwd(dproj_main, proj_main, COL_SSD, SSD_CONV, cw_ssd, cb_ssd, dconv_ssd, "conv_bwd_ssd")
    dproj_main, dwb_gdn = conv_bwd(dproj_main, proj_main, COL_GDN, GDN_CONV, cw_gdn, cb_gdn, dconv_gdn, "conv_bwd_gdn")
    d_w_main, d_w_small = in_proj_bwd_w(u, dproj_main, dsmall_ssd, dsmall_gdn)

    d_w_in = jnp.concatenate([d_w_main[:, :COL_GATE], _ssd_unperm(d_w_main[:, COL_SSD:]), d_w_small[:, 0:16],
                              d_w_main[:, COL_GATE:COL_GDN], _gdn_unperm(d_w_main[:, COL_GDN:COL_SSD]),
                              d_w_small[:, 16:32]], axis=1)
    d_w_in = jnp.transpose(d_w_in.reshape(D_MODEL, N_CHIP, in_cols), (1, 0, 2))
    slabs = [d_w_in.reshape(N_DEV, D_MODEL // 2, in_cols).astype(COMM_DTYPE),
             d_w_out.reshape(N_DEV, out_rows // 2, D_MODEL).astype(COMM_DTYPE)]

    grad_x, d_norm_w, (r_in, r_out) = in_proj_bwd_x(xs, norm_w, w_main, w_small, dproj_main, dsmall_ssd, dsmall_gdn,
                                                     dhid, slabs)
    d_alog, d_dtb = d_alog_s + d_alog_g, d_dtb_s + d_dtb_g
    packed, (o_nw, o_cs, o_cg, o_snw, o_fw, o_al, o_db, o_dv, o_gnw, o_loss) = _pack_cols([
        d_norm_w, _ssd_unperm(dwb_ssd), _gdn_unperm(dwb_gdn),
        d_ssd_nw.reshape(1, SSD_WIDTH), d_fw, d_alog, d_dtb, d_dvec, d_gdn_nw, loss_blk])

    half_in = sum_slabs(r_in, "sum_w_in")
    half_out = sum_slabs(r_out, "sum_w_out")
    (full_in, full_out), (r_small,) = exchange_halves([half_in, half_out], [packed])
    tot = sum_slabs(r_small, "sum_small")
    grad_w_in = full_in.reshape(D_MODEL, in_cols)
    grad_w_out = full_out.reshape(out_rows, D_MODEL)
    loss = tot[0, o_loss]
    sc, gc = ssd_conv_w.shape[2], gdn_conv_w.shape[2]
    row = lambda off, n, r=0: tot[r:r + 1, off:off + n]
    gs = [row(o_nw, D_MODEL),
          lax.dynamic_slice(tot, (0, o_cs + chip * sc), (4, sc)),
          row(o_cs, SSD_CONV, 4),
          row(o_db, SSD_HEADS), row(o_al, SSD_HEADS), row(o_dv, SSD_HEADS),
          row(o_snw, SSD_WIDTH),
          lax.dynamic_slice(tot, (0, o_cg + chip * gc), (4, gc)),
          row(o_db + LANE_GA, GDN_HEADS), row(o_al + LANE_GA, GDN_HEADS),
          row(o_gnw, GDN_DV), row(o_fw, D_MODEL)]

    names = ["norm_w", "ssd_conv_w", "ssd_conv_b", "ssd_dt_bias", "ssd_a_log", "ssd_d", "ssd_norm_w", "gdn_conv_w",
             "gdn_dt_bias", "gdn_a_log", "gdn_norm_w", "final_norm_w"]
    ws = [norm_w, ssd_conv_w, ssd_conv_b, ssd_dt_bias, ssd_a_log, ssd_d, ssd_norm_w, gdn_conv_w, gdn_dt_bias,
          gdn_a_log, gdn_norm_w, final_norm_w]
    ms = [m_norm_w, m_ssd_conv_w, m_ssd_conv_b, m_ssd_dt_bias, m_ssd_a_log, m_ssd_d, m_ssd_norm_w, m_gdn_conv_w,
          m_gdn_dt_bias, m_gdn_a_log, m_gdn_norm_w, m_final_norm_w]
    vs = [v_norm_w, v_ssd_conv_w, v_ssd_conv_b, v_ssd_dt_bias, v_ssd_a_log, v_ssd_d, v_ssd_norm_w, v_gdn_conv_w,
          v_gdn_dt_bias, v_gdn_a_log, v_gdn_norm_w, v_final_norm_w]
    shapes = [w.shape for w in ws]
    flat = lambda arrs: [a.reshape(g.shape) for a, g in zip(arrs, gs)]
    d_s, m_s, v_s = adamw_many(flat(ws), gs, flat(ms), flat(vs))
    back = lambda arrs: dict(zip(names, [a.reshape(s) for a, s in zip(arrs, shapes)]))
    delta, new_m, new_v, grads = back(d_s), back(m_s), back(v_s), back(gs)
    d_in, m_in, v_in = adamw(w_in_shard, grad_w_in, m_w_in[0], v_w_in[0], "adamw_w_in")
    d_out, m_out, v_out = adamw(w_out_shard, grad_w_out, m_w_out[0], v_w_out[0], "adamw_w_out")
    for tbl, a_in, a_out in ((grads, grad_w_in, grad_w_out), (delta, d_in, d_out), (new_m, m_in, m_out),
                             (new_v, v_in, v_out)):
        tbl["w_in"] = a_in[None]
        tbl["w_out"] = a_out[None]

    order = ["norm_w", "w_in", "ssd_conv_w", "ssd_conv_b", "ssd_dt_bias", "ssd_a_log", "ssd_d", "ssd_norm_w",
             "gdn_conv_w", "gdn_dt_bias", "gdn_a_log", "gdn_norm_w", "w_out", "final_norm_w"]
    return (loss.reshape(()), grad_x[None], *[grads[k] for k in order], *[delta[k] for k in order],
            *[new_m[k] for k in order], *[new_v[k] for k in order])
```

```python
import functools

import jax
import jax.numpy as jnp
from jax import lax
from jax.experimental import pallas as pl
from jax.experimental.pallas import tpu as pltpu

F32 = jnp.float32
MXU_DTYPE = jnp.bfloat16
COMM_DTYPE = jnp.bfloat16
MESH = pl.DeviceIdType.MESH

D_MODEL = 1024
CHUNK = 64
EPS = 1e-6
SSD_HEADS, SSD_GROUPS, SSD_STATE = 16, 2, 128
SSD_WIDTH, SSD_CONV = 1024, 1536
SSD_GW = SSD_WIDTH // SSD_GROUPS
SSD_GC = SSD_GW + 2 * SSD_STATE
GDN_HEADS, GDN_DK, GDN_DV = 8, 128, 128
GDN_W, GDN_CONV = 1024, 3072
GDN_HC = 2 * GDN_DK + GDN_DV
IN_DIM = 6688
MAIN = 6656
LANES = 128
COL_Z, COL_GATE, COL_GDN, COL_SSD = 0, 1024, 2048, 5120
COL_CONV = COL_GDN
GDN_HB = 8
LANE_GA, LANE_GB = 16, 24
N_DEV, N_CHIP = 8, 4
VMEM_LIMIT = 52 * 1024 * 1024

ADAM_LR, ADAM_B1, ADAM_B2, ADAM_EPS, ADAM_WD, ADAM_STEP = 0.001, 0.9, 0.999, 1e-08, 0.01, 10


def _ssd_perm(a):
    lead, nb = a.shape[:-1], SSD_GROUPS * SSD_STATE
    x = a[..., :SSD_WIDTH].reshape(*lead, SSD_GROUPS, SSD_GW)
    b = a[..., SSD_WIDTH:SSD_WIDTH + nb].reshape(*lead, SSD_GROUPS, SSD_STATE)
    c = a[..., SSD_WIDTH + nb:].reshape(*lead, SSD_GROUPS, SSD_STATE)
    return jnp.concatenate([x, b, c], axis=-1).reshape(*lead, SSD_CONV)


def _ssd_unperm(a):
    lead = a.shape[:-1]
    g = a.reshape(*lead, SSD_GROUPS, SSD_GC)
    parts = [g[..., :SSD_GW], g[..., SSD_GW:SSD_GW + SSD_STATE], g[..., SSD_GW + SSD_STATE:]]
    return jnp.concatenate([p.reshape(*lead, -1) for p in parts], axis=-1)


def _gdn_perm(a):
    lead = a.shape[:-1]
    return jnp.swapaxes(a.reshape(*lead, 3, GDN_HEADS, GDN_DK), -3, -2).reshape(*lead, GDN_CONV)


def _gdn_unperm(a):
    lead = a.shape[:-1]
    return jnp.swapaxes(a.reshape(*lead, GDN_HEADS, 3, GDN_DK), -3, -2).reshape(*lead, GDN_CONV)


def _split(a, n):
    parts, rest = [], a.astype(F32)
    for i in range(n):
        p = rest.astype(MXU_DTYPE)
        parts.append(p)
        if i < n - 1:
            rest = rest - p.astype(F32)
    return parts


def _raw_dot(a, b, ca, cb, mode="bf16"):
    d = lambda u, v: lax.dot_general(u, v, (((ca,), (cb,)), ((), ())), preferred_element_type=F32)
    if mode == "bf16":
        return d(a.astype(MXU_DTYPE), b.astype(MXU_DTYPE))
    if mode == "x3":
        (ah, al), (bh, bl) = _split(a, 2), _split(b, 2)
        return d(ah, bh) + (d(ah, bl) + d(al, bh))
    if mode == "sel_a":
        a0 = a.astype(MXU_DTYPE)
        b1, b2, b3 = _split(b, 3)
        return d(a0, b1) + (d(a0, b2) + d(a0, b3))
    assert mode == "sel_b", mode
    b0 = b.astype(MXU_DTYPE)
    a1, a2, a3 = _split(a, 3)
    return d(a1, b0) + (d(a2, b0) + d(a3, b0))


@functools.partial(jax.custom_vjp, nondiff_argnums=(2,))
def mm_nn(a, b, mode="bf16"):
    return _raw_dot(a, b, 1, 0, mode)


@functools.partial(jax.custom_vjp, nondiff_argnums=(2,))
def mm_nt(a, b, mode="bf16"):
    return _raw_dot(a, b, 1, 1, mode)


@functools.partial(jax.custom_vjp, nondiff_argnums=(2,))
def mm_tn(a, b, mode="bf16"):
    return _raw_dot(a, b, 0, 0, mode)


_SAME = {"bf16": ("bf16", "bf16"), "x3": ("x3", "x3")}
_NN_BWD = dict(_SAME, sel_a=("bf16", "sel_a"), sel_b=("sel_b", "bf16"))
_NT_BWD = dict(_SAME, sel_a=("bf16", "sel_b"), sel_b=("sel_b", "bf16"))
_TN_BWD = dict(_SAME, sel_a=("bf16", "sel_a"), sel_b=("sel_a", "bf16"))
mm_nn.defvjp(lambda a, b, m: (_raw_dot(a, b, 1, 0, m), (a, b)),
             lambda m, r, g: (mm_nt(g, r[1], _NN_BWD[m][0]), mm_tn(r[0], g, _NN_BWD[m][1])))
mm_nt.defvjp(lambda a, b, m: (_raw_dot(a, b, 1, 1, m), (a, b)),
             lambda m, r, g: (mm_nn(g, r[1], _NT_BWD[m][0]), mm_tn(g, r[0], _NT_BWD[m][1])))
mm_tn.defvjp(lambda a, b, m: (_raw_dot(a, b, 0, 0, m), (a, b)),
             lambda m, r, g: (mm_nt(r[1], g, _TN_BWD[m][0]), mm_nn(r[0], g, _TN_BWD[m][1])))


@jax.custom_jvp
def sigmoid(x):
    return 1.0 / (1.0 + jnp.exp(-x))


@sigmoid.defjvp
def _sigmoid_jvp(p, t):
    s = sigmoid(p[0])
    return s, t[0] * s * (1.0 - s)


@jax.custom_jvp
def softplus(x):
    return jnp.maximum(x, 0.0) + jnp.log(1.0 + jnp.exp(-jnp.abs(x)))


@softplus.defjvp
def _softplus_jvp(p, t):
    return softplus(p[0]), t[0] * sigmoid(p[0])


def silu(x):
    return x * sigmoid(x)


def rmsnorm(x, w):
    return x * lax.rsqrt(jnp.mean(x * x, axis=-1, keepdims=True) + EPS) * w


def _iota(shape, dim):
    return lax.broadcasted_iota(jnp.int32, shape, dim)


def _tri_inv_impl(mats):
    n = mats[0].shape[0]
    r, c = _iota((n, n), 0), _iota((n, n), 1)
    eye = jnp.where(r == c, 1.0, 0.0).astype(F32)
    blockdiag = (r >> 4) == (c >> 4)
    dot = lambda u, v: _raw_dot(u, v, 1, 0, "x3")
    dot1 = lambda u, v: _raw_dot(u, v, 1, 0)
    each = lambda f, *ls: [f(*xs) for xs in zip(*ls)]
    dg = each(lambda a: jnp.where(blockdiag, a, 0.0), mats)
    off = each(lambda a, d: a - d, mats, dg)
    m = each(lambda d: -d, dg)
    p = each(lambda x: eye + x, m)
    pw = m
    for _ in range(3):
        pw = each(lambda x: dot1(x, x), pw)
        p = each(lambda x, y: x + dot1(x, y), p, pw)
    e = each(dot, p, off)
    e2 = each(lambda x: dot1(x, x), e)
    q = each(lambda x: eye - x, e)
    q = each(lambda x, y: x + dot1(x, y), q, e2)
    return each(dot, q, p)


def _tri_inv_bwd(ts, gs):
    x = [mm_nt(g, t) for g, t in zip(gs, ts)]
    return [-mm_tn(t, y) for t, y in zip(ts, x)]


@jax.custom_vjp
def tri_inv(mats):
    return _tri_inv_impl(mats)


def _tri_inv_fwd(mats):
    ts = _tri_inv_impl(mats)
    return ts, ts


tri_inv.defvjp(_tri_inv_fwd, lambda ts, gs: (_tri_inv_bwd(ts, gs),))


@jax.custom_vjp
def tri_inv_saved(mats, ts):
    del mats
    return ts


tri_inv_saved.defvjp(lambda mats, ts: (ts, ts),
                     lambda ts, gs: (_tri_inv_bwd(ts, gs), [jnp.zeros_like(t) for t in ts]))


def _chunk_masks():
    r, c = _iota((CHUNK, CHUNK), 0), _iota((CHUNK, CHUNK), 1)
    return r >= c, r > c, r == c


def _log_decay_cumsum(small, alog, dtb, incl):
    sp = softplus(small + dtb)
    la = -jnp.exp(alog) * sp
    tri = jnp.where(incl, 1.0, 0.0).astype(F32)
    return sp, mm_nn(tri, la, "sel_a")


def _col_of(x, lane_mask):
    return jnp.sum(jnp.where(lane_mask, x, 0.0), axis=1, keepdims=True)


def _decay_matrix(col, incl, eye):
    row = jnp.sum(jnp.where(eye, col, 0.0), axis=0, keepdims=True)
    return jnp.where(incl, jnp.exp(jnp.where(incl, col - row, 0.0)), 0.0)


def gdn_chunk(h0, qs, ks, vs, small, gates, normw, alog, dtb, states, saved_t=None):
    incl, strict, eye = _chunk_masks()
    lane = _iota((1, LANES), 1)
    last = _iota((CHUNK, 1), 0) == CHUNK - 1
    _, lac = _log_decay_cumsum(small, alog, dtb, incl)
    heads = range(len(qs))
    each = lambda f, *ls: [f(*xs) for xs in zip(*ls)]
    gc = [_col_of(lac, lane == LANE_GA + h0 + j) for j in heads]
    beta = [sigmoid(_col_of(small, lane == LANE_GB + h0 + j)) for j in heads]
    decay = each(lambda x: _decay_matrix(x, incl, eye), gc)
    gl = each(lambda x: jnp.sum(jnp.where(last, x, 0.0), axis=0, keepdims=True), gc)
    q = each(lambda x: x * lax.rsqrt(jnp.sum(x * x, axis=-1, keepdims=True) + EPS) * (GDN_DK ** -0.5), qs)
    k = each(lambda x: x * lax.rsqrt(jnp.sum(x * x, axis=-1, keepdims=True) + EPS), ks)
    kb = each(lambda x, b: x * b, k, beta)
    a = each(lambda x, y, d: jnp.where(strict, mm_nt(x, y) * d, 0.0), kb, k, decay)
    t = tri_inv(a) if saved_t is None else tri_inv_saved(a, saved_t)
    eg = each(jnp.exp, gc)
    u = each(lambda x, v, b: mm_nn(x, v * b), t, vs, beta)
    w = each(lambda x, y, e: mm_nn(x, y * e), t, kb, eg)
    attn = each(lambda x, y, d: mm_nt(x, y) * d, q, k, decay)
    v_new = each(lambda x, y, s: x - mm_nn(y, s), u, w, states)
    o = each(lambda x, e, s, at, vn: mm_nn(x * e, s) + mm_nn(at, vn), q, eg, states, attn, v_new)
    new_states = each(lambda s, x, y, l, c: s * jnp.exp(l) + mm_tn(y * jnp.exp(l - c), x), states, v_new, k, gl, gc)
    ys = each(lambda x, gt: rmsnorm(x, normw) * silu(gt), o, gates)
    return ys, new_states, t


@jax.custom_vjp
def split_lanes(x):
    return [x[:, i * LANES:(i + 1) * LANES] for i in range(x.shape[1] // LANES)]


@jax.custom_vjp
def join_lanes(xs):
    return jnp.concatenate(xs, axis=1)


split_lanes.defvjp(lambda x: (split_lanes(x), None), lambda _, gs: (join_lanes(gs),))
join_lanes.defvjp(lambda xs: (join_lanes(xs), None), lambda _, g: (split_lanes(g),))


def ssd_chunk(xs, bm, cm, z, small, normw, alog, dtb, dvec, state):
    incl, _, eye = _chunk_masks()
    lane = _iota((1, LANES), 1)
    last = _iota((CHUNK, 1), 0) == CHUNK - 1
    hpg = SSD_HEADS // SSD_GROUPS
    groups = range(len(xs))
    each = lambda f, *ls: [f(*a) for a in zip(*ls)]
    sp, lac = _log_decay_cumsum(small, alog, dtb, incl)
    lac_last = jnp.sum(jnp.where(last, lac, 0.0), axis=0, keepdims=True)
    sel = [jnp.where(_iota((LANES, SSD_GW), 0) == g * hpg + (_iota((LANES, SSD_GW), 1) >> 6), 1.0, 0.0).astype(F32)
           for g in groups]
    expand = lambda v: [mm_nn(v, s, "sel_b") for s in sel]
    dt_e, elac_e, toend_e = expand(sp), expand(jnp.exp(lac)), expand(jnp.exp(lac_last - lac))
    row8, row8e = _iota((8, LANES), 0), _iota((8, SSD_GW), 0)
    two_e = expand(jnp.where(row8 == 0, dvec, 0.0) + jnp.where(row8 == 1, jnp.exp(lac_last), 0.0))
    d_e = each(lambda v: jnp.sum(jnp.where(row8e == 0, v, 0.0), axis=0, keepdims=True), two_e)
    chunk_e = each(lambda v: jnp.sum(jnp.where(row8e == 1, v, 0.0), axis=0, keepdims=True), two_e)
    xdt = each(lambda a, b: a * b, xs, dt_e)
    cb = each(mm_nt, cm, bm)
    y = each(lambda c_, st, el, x_, d_: mm_nn(c_, st) * el + x_ * d_, cm, state, elac_e, xs, d_e)
    x_pairs = each(split_lanes, xdt)
    half = _iota((1, LANES), 1) >> 6
    lms = [[_decay_matrix(_col_of(lac, lane == g * hpg + j), incl, eye) for j in range(hpg)] for g in groups]
    terms = [[mm_nn(cb[g] * lms[g][j], jnp.where(half == j % 2, x_pairs[g][j // 2], 0.0)) for j in range(hpg)]
             for g in groups]
    y = [y[g] + join_lanes([terms[g][2 * p] + terms[g][2 * p + 1] for p in range(hpg // 2)]) for g in groups]
    new_state = each(lambda st, ce, b_, xd, te: st * ce + mm_tn(b_, xd * te), state, chunk_e, bm, xdt, toend_e)
    out = each(lambda y_, z_, nw: rmsnorm(y_ * silu(z_), nw), y, z, normw)
    return out, new_state


def _params(sem=None):
    return pltpu.CompilerParams(dimension_semantics=sem, vmem_limit_bytes=VMEM_LIMIT)


def _full(shape):
    n = len(shape)
    return pl.BlockSpec(shape, lambda *_: (0,) * n)


ANY = pl.BlockSpec(memory_space=pl.ANY)
HBM = pl.BlockSpec(memory_space=pltpu.HBM)


def in_proj(x, normw, w_main, w_small):
    t = x.shape[0]
    tm, tn = min(1024, t), 512

    def body(x_ref, nw_ref, wm_ref, ws_ref, pm_ref, ps_ref, u_ref):
        @pl.when(pl.program_id(1) == 0)
        def _():
            u = rmsnorm(x_ref[...], nw_ref[...]).astype(MXU_DTYPE)
            u_ref[...] = u
            ps_ref[...] = _raw_dot(u, ws_ref[...], 1, 0)
        pm_ref[...] = _raw_dot(u_ref[...], wm_ref[...], 1, 0)

    return pl.pallas_call(
        body, name="in_proj", grid=(t // tm, COL_CONV // tn),
        in_specs=[pl.BlockSpec((tm, D_MODEL), lambda i, j: (i, 0)), _full((1, D_MODEL)),
                  pl.BlockSpec((D_MODEL, tn), lambda i, j: (0, j)), _full((D_MODEL, LANES))],
        out_specs=[pl.BlockSpec((tm, tn), lambda i, j: (i, j)), pl.BlockSpec((tm, LANES), lambda i, j: (i, 0)),
                   pl.BlockSpec((tm, D_MODEL), lambda i, j: (i, 0))],
        out_shape=[jax.ShapeDtypeStruct((t, MAIN), F32), jax.ShapeDtypeStruct((t, LANES), F32),
                   jax.ShapeDtypeStruct((t, D_MODEL), MXU_DTYPE)],
        compiler_params=_params(("arbitrary", "arbitrary")),
    )(x, normw, w_main, w_small)


CONV_TC = 512
HALO = 8


def _shift_down(cur, prev, s):
    rolled = pltpu.roll(cur, s, 0)
    top = jnp.where(_iota((HALO, cur.shape[1]), 0) < s, pltpu.roll(prev, s, 0), rolled[:HALO])
    if cur.shape[0] == HALO:
        return top
    return jnp.concatenate([top, rolled[HALO:]], axis=0)


def _shift_up(cur, nxt, s):
    n = cur.shape[0]
    rolled = pltpu.roll(cur, n - s, 0)
    bot = jnp.where(_iota((HALO, cur.shape[1]), 0) >= HALO - s, pltpu.roll(nxt, HALO - s, 0), rolled[n - HALO:])
    return jnp.concatenate([rolled[:n - HALO], bot], axis=0)


def _conv_pre(cur, prev, w_ref, b):
    acc = cur * w_ref[3:4, :] + b
    shifted = [cur]
    for s in (1, 2, 3):
        sh = _shift_down(cur, prev, s)
        shifted.append(sh)
        acc = acc + sh * w_ref[3 - s:4 - s, :]
    return acc, shifted


def in_proj_conv(proj_main, u, w_main, w, b):
    t = u.shape[0]
    tm, tn = min(1024, t), CONV_TC
    rc = min(256, tm)
    c0, nj = COL_CONV // tn, (MAIN - COL_CONV) // tn

    def body(alias_ref, u_ref, wm_ref, w_ref, b_ref, pm_ref, out_ref, halo_ref):
        del alias_ref
        j = pl.program_id(1)

        @pl.when(pl.program_id(0) == 0)
        def _():
            halo_ref[j] = jnp.zeros((HALO, tn), F32)

        prev = halo_ref[j]
        for r in range(tm // rc):
            rows = pl.ds(r * rc, rc)
            p = _raw_dot(u_ref[rows, :], wm_ref[...], 1, 0)
            pm_ref[rows, :] = p
            pre, _ = _conv_pre(p, prev, w_ref, b_ref[...])
            out_ref[rows, :] = silu(pre)
            prev = p[rc - HALO:]
        halo_ref[j] = prev

    return pl.pallas_call(
        body, name="in_proj_conv", grid=(t // tm, nj),
        in_specs=[ANY, pl.BlockSpec((tm, D_MODEL), lambda i, j: (i, 0)),
                  pl.BlockSpec((D_MODEL, tn), lambda i, j: (0, c0 + j)),
                  pl.BlockSpec((4, tn), lambda i, j: (0, j)), pl.BlockSpec((1, tn), lambda i, j: (0, j))],
        out_specs=[pl.BlockSpec((tm, tn), lambda i, j: (i, c0 + j)), pl.BlockSpec((tm, tn), lambda i, j: (i, j))],
        out_shape=[jax.ShapeDtypeStruct(proj_main.shape, F32), jax.ShapeDtypeStruct((t, MAIN - COL_CONV), F32)],
        scratch_shapes=[pltpu.VMEM((nj, HALO, tn), F32)],
        input_output_aliases={0: 0},
        compiler_params=_params(("arbitrary", "arbitrary")),
    )(proj_main, u, w_main, w, b)


def _dsilu(pre):
    sg = sigmoid(pre)
    return sg * (1.0 + pre * (1.0 - sg))


def conv_bwd(dproj_main, proj_main, col0, width, w, b, dout, name):
    t = proj_main.shape[0]
    tt, c0 = min(512, t), col0 // CONV_TC
    nt = t // tt
    after = lambda i: jnp.minimum((i + 1) * (tt // HALO), t // HALO - 1)

    def body(alias_ref, cur_ref, prev_ref, nxt_ref, w_ref, b_ref, do_ref, do_nxt_ref, dx_ref, dwb_ref):
        del alias_ref
        i = pl.program_id(1)
        cur, bias = cur_ref[...], b_ref[...]
        prev = jnp.where(i > 0, prev_ref[...], 0.0)
        pre, shifted = _conv_pre(cur, prev, w_ref, bias)
        dpre = do_ref[...] * _dsilu(pre)
        pre_nxt, _ = _conv_pre(nxt_ref[...], cur[tt - HALO:], w_ref, bias)
        dpre_nxt = jnp.where(i < nt - 1, do_nxt_ref[...] * _dsilu(pre_nxt), 0.0)
        dx = dpre * w_ref[3:4, :]
        for s in (1, 2, 3):
            dx = dx + _shift_up(dpre, dpre_nxt, s) * w_ref[3 - s:4 - s, :]
        dx_ref[...] = dx.astype(dx_ref.dtype)
        row = _iota((HALO, CONV_TC), 0)
        upd = jnp.where(row == 4, jnp.sum(dpre, axis=0, keepdims=True), 0.0)
        for s in range(4):
            upd = upd + jnp.where(row == 3 - s, jnp.sum(dpre * shifted[s], axis=0, keepdims=True), 0.0)
        _accumulate(dwb_ref, i == 0, upd)

    return pl.pallas_call(
        body, name=name, grid=(width // CONV_TC, nt),
        in_specs=[ANY, pl.BlockSpec((tt, CONV_TC), lambda j, i: (i, c0 + j)),
                  pl.BlockSpec((HALO, CONV_TC), lambda j, i: (jnp.maximum(i * (tt // HALO) - 1, 0), c0 + j)),
                  pl.BlockSpec((HALO, CONV_TC), lambda j, i: (after(i), c0 + j)),
                  pl.BlockSpec((4, CONV_TC), lambda j, i: (0, j)), pl.BlockSpec((1, CONV_TC), lambda j, i: (0, j)),
                  pl.BlockSpec((tt, CONV_TC), lambda j, i: (i, j)),
                  pl.BlockSpec((HALO, CONV_TC), lambda j, i: (after(i), j))],
        out_specs=[pl.BlockSpec((tt, CONV_TC), lambda j, i: (i, c0 + j)),
                   pl.BlockSpec((HALO, CONV_TC), lambda j, i: (0, j))],
        out_shape=[jax.ShapeDtypeStruct(dproj_main.shape, dproj_main.dtype), jax.ShapeDtypeStruct((HALO, width), F32)],
        input_output_aliases={0: 0},
        compiler_params=_params(("arbitrary", "arbitrary")),
    )(dproj_main, proj_main, proj_main, proj_main, w, b, dout, dout)


def _ssd_parts(xbc_ref):
    part = lambda o, w: [xbc_ref[:, g * SSD_GC + o:g * SSD_GC + o + w] for g in range(SSD_GROUPS)]
    return part(0, SSD_GW), part(SSD_GW, SSD_STATE), part(SSD_GW + SSD_STATE, SSD_STATE)


def _group_cols(ref):
    return [ref[:, g * SSD_GW:(g + 1) * SSD_GW] for g in range(SSD_GROUPS)]


def _gdn_parts(qkv_ref):
    part = lambda o: [qkv_ref[:, j * GDN_HC + o:j * GDN_HC + o + GDN_DK] for j in range(GDN_HB)]
    return part(0), part(GDN_DK), part(2 * GDN_DK)


def _head_cols(ref):
    return [ref[:, j * GDN_DV:(j + 1) * GDN_DV] for j in range(GDN_HB)]


def _first_head():
    return 0 if GDN_HB == GDN_HEADS else pl.program_id(1) * GDN_HB


def ssd_fwd(conv_ssd, proj_main, proj_small, normw, alog, dtb, dvec):
    t = conv_ssd.shape[0]
    nc = t // CHUNK

    groups = range(SSD_GROUPS)

    def body(xbc_ref, z_ref, sm_ref, nw_ref, al_ref, db_ref, dv_ref, y_ref, hist_ref, state_ref):
        @pl.when(pl.program_id(0) == 0)
        def _():
            state_ref[...] = jnp.zeros(state_ref.shape, F32)

        states = [state_ref[g] for g in groups]
        for g in groups:
            hist_ref[0, g] = states[g]
        ys, new_states = ssd_chunk(*_ssd_parts(xbc_ref), _group_cols(z_ref), sm_ref[...], _group_cols(nw_ref),
                                   al_ref[...], db_ref[...], dv_ref[...], states)
        for g in groups:
            y_ref[:, g * SSD_GW:(g + 1) * SSD_GW] = ys[g].astype(MXU_DTYPE)
            state_ref[g] = new_states[g]

    return pl.pallas_call(
        body, name="ssd_fwd", grid=(nc,),
        in_specs=[pl.BlockSpec((CHUNK, SSD_CONV), lambda c: (c, (COL_SSD - COL_CONV) // SSD_CONV)),
                  pl.BlockSpec((CHUNK, SSD_WIDTH), lambda c: (c, COL_Z // SSD_WIDTH)),
                  pl.BlockSpec((CHUNK, LANES), lambda c: (c, 0)),
                  _full((1, SSD_WIDTH)), _full((1, LANES)), _full((1, LANES)), _full((1, LANES))],
        out_specs=[pl.BlockSpec((CHUNK, SSD_WIDTH), lambda c: (c, 0)),
                   pl.BlockSpec((1, SSD_GROUPS, SSD_STATE, SSD_GW), lambda c: (c, 0, 0, 0))],
        out_shape=[jax.ShapeDtypeStruct((t, SSD_WIDTH), MXU_DTYPE),
                   jax.ShapeDtypeStruct((nc, SSD_GROUPS, SSD_STATE, SSD_GW), F32)],
        scratch_shapes=[pltpu.VMEM((SSD_GROUPS, SSD_STATE, SSD_GW), F32)],
        compiler_params=_params(("arbitrary",)),
    )(conv_ssd, proj_main, proj_small, normw, alog, dtb, dvec)


def _accumulate(ref, first, value):
    @pl.when(first)
    def _():
        ref[...] = value

    @pl.when(jnp.logical_not(first))
    def _():
        ref[...] += value


def ssd_bwd(conv_ssd, proj_main, proj_small, normw, alog, dtb, dvec, hist, dy):
    t = conv_ssd.shape[0]
    nc = t // CHUNK
    rev = lambda c: nc - 1 - c
    groups = range(SSD_GROUPS)

    def body(xbc_ref, z_ref, sm_ref, nw_ref, al_ref, db_ref, dv_ref, hist_ref, dy_ref,
             dxbc_ref, dz_ref, dsm_ref, dnw_ref, dal_ref, ddb_ref, ddv_ref, dstate_ref):
        first = pl.program_id(0) == 0

        @pl.when(first)
        def _():
            dstate_ref[...] = jnp.zeros(dstate_ref.shape, F32)

        _, vjp = jax.vjp(ssd_chunk, *_ssd_parts(xbc_ref), _group_cols(z_ref), sm_ref[...], _group_cols(nw_ref),
                         al_ref[...], db_ref[...], dv_ref[...], [hist_ref[0, g] for g in groups])
        dxs, dbm, dcm, dz, dsm, dnw, dal, ddb, ddv, dstate = vjp(
            (_group_cols(dy_ref), [dstate_ref[g] for g in groups]))
        for g in groups:
            base = g * SSD_GC
            dxbc_ref[:, base:base + SSD_GW] = dxs[g]
            dxbc_ref[:, base + SSD_GW:base + SSD_GW + SSD_STATE] = dbm[g]
            dxbc_ref[:, base + SSD_GW + SSD_STATE:base + SSD_GC] = dcm[g]
            dz_ref[:, g * SSD_GW:(g + 1) * SSD_GW] = dz[g].astype(dz_ref.dtype)
            dstate_ref[g] = dstate[g]
        dsm_ref[...] = dsm
        _accumulate(dnw_ref, first, join_lanes(dnw))
        _accumulate(dal_ref, first, dal)
        _accumulate(ddb_ref, first, ddb)
        _accumulate(ddv_ref, first, ddv)

    return pl.pallas_call(
        body, name="ssd_bwd", grid=(nc,),
        in_specs=[pl.BlockSpec((CHUNK, SSD_CONV), lambda c: (rev(c), (COL_SSD - COL_CONV) // SSD_CONV)),
                  pl.BlockSpec((CHUNK, SSD_WIDTH), lambda c: (rev(c), COL_Z // SSD_WIDTH)),
                  pl.BlockSpec((CHUNK, LANES), lambda c: (rev(c), 0)),
                  _full((1, SSD_WIDTH)), _full((1, LANES)), _full((1, LANES)), _full((1, LANES)),
                  pl.BlockSpec((1, SSD_GROUPS, SSD_STATE, SSD_GW), lambda c: (rev(c), 0, 0, 0)),
                  pl.BlockSpec((CHUNK, SSD_WIDTH), lambda c: (rev(c), 0))],
        out_specs=[pl.BlockSpec((CHUNK, SSD_CONV), lambda c: (rev(c), 0)),
                   pl.BlockSpec((CHUNK, SSD_WIDTH), lambda c: (rev(c), COL_Z // SSD_WIDTH)),
                   pl.BlockSpec((CHUNK, LANES), lambda c: (rev(c), 0)),
                   _full((1, SSD_WIDTH)), _full((1, LANES)), _full((1, LANES)), _full((1, LANES))],
        out_shape=[jax.ShapeDtypeStruct((t, SSD_CONV), F32), jax.ShapeDtypeStruct((t, MAIN), MXU_DTYPE),
                   jax.ShapeDtypeStruct((t, LANES), F32), jax.ShapeDtypeStruct((1, SSD_WIDTH), F32),
                   jax.ShapeDtypeStruct((1, LANES), F32), jax.ShapeDtypeStruct((1, LANES), F32),
                   jax.ShapeDtypeStruct((1, LANES), F32)],
        scratch_shapes=[pltpu.VMEM((SSD_GROUPS, SSD_STATE, SSD_GW), F32)],
        compiler_params=_params(("arbitrary",)),
    )(conv_ssd, proj_main, proj_small, normw, alog, dtb, dvec, hist, dy)


def gdn_fwd(conv_gdn, proj_main, proj_small, normw, alog, dtb):
    t = conv_gdn.shape[0]
    nc = t // CHUNK

    hb = GDN_HB
    gate_blk = COL_GATE // (GDN_DV * hb)

    def body(qkv_ref, gate_ref, sm_ref, nw_ref, al_ref, db_ref, y_ref, hist_ref, t_ref, state_ref):
        h0 = _first_head()

        @pl.when(pl.program_id(0) == 0)
        def _():
            for j in range(hb):
                state_ref[h0 + j] = jnp.zeros((GDN_DK, GDN_DV), F32)

        states = [state_ref[h0 + j] for j in range(hb)]
        for j in range(hb):
            hist_ref[0, j] = states[j]
        qs, ks, vs = _gdn_parts(qkv_ref)
        ys, new_states, ts = gdn_chunk(h0, qs, ks, vs, sm_ref[...], _head_cols(gate_ref), nw_ref[...], al_ref[...],
                                       db_ref[...], states)
        for j in range(hb):
            y_ref[:, j * GDN_DV:(j + 1) * GDN_DV] = ys[j].astype(MXU_DTYPE)
            state_ref[h0 + j] = new_states[j]
            t_ref[0, j] = ts[j]

    return pl.pallas_call(
        body, name="gdn_fwd", grid=(nc, GDN_HEADS // hb),
        in_specs=[pl.BlockSpec((CHUNK, GDN_HC * hb), lambda c, h: (c, h)),
                  pl.BlockSpec((CHUNK, GDN_DV * hb), lambda c, h: (c, gate_blk + h)),
                  pl.BlockSpec((CHUNK, LANES), lambda c, h: (c, 0)),
                  _full((1, GDN_DV)), _full((1, LANES)), _full((1, LANES))],
        out_specs=[pl.BlockSpec((CHUNK, GDN_DV * hb), lambda c, h: (c, h)),
                   pl.BlockSpec((1, hb, GDN_DK, GDN_DV), lambda c, h: (c, h, 0, 0)),
                   pl.BlockSpec((1, hb, CHUNK, CHUNK), lambda c, h: (c, h, 0, 0))],
        out_shape=[jax.ShapeDtypeStruct((t, GDN_W), MXU_DTYPE),
                   jax.ShapeDtypeStruct((nc, GDN_HEADS, GDN_DK, GDN_DV), F32),
                   jax.ShapeDtypeStruct((nc, GDN_HEADS, CHUNK, CHUNK), F32)],
        scratch_shapes=[pltpu.VMEM((GDN_HEADS, GDN_DK, GDN_DV), F32)],
        compiler_params=_params(("arbitrary", "arbitrary")),
    )(conv_gdn, proj_main, proj_small, normw, alog, dtb)


def gdn_bwd(dproj_main, conv_gdn, proj_main, proj_small, normw, alog, dtb, hist, t_inv, dy):
    t = conv_gdn.shape[0]
    nc = t // CHUNK
    rev = lambda c: nc - 1 - c
    hb = GDN_HB
    gate_blk = COL_GATE // (GDN_DV * hb)

    def body(alias_ref, qkv_ref, gate_ref, sm_ref, nw_ref, al_ref, db_ref, hist_ref, t_ref, dy_ref,
             dgate_ref, dqkv_ref, dsm_ref, dnw_ref, dal_ref, ddb_ref, dstate_ref):
        del alias_ref
        c, h = pl.program_id(0), pl.program_id(1)
        h0 = _first_head()

        @pl.when(c == 0)
        def _():
            for j in range(hb):
                dstate_ref[h0 + j] = jnp.zeros((GDN_DK, GDN_DV), F32)

        saved = [t_ref[0, j] for j in range(hb)]

        def fn(qs, ks, vs, small, gates, nw, al, db, states):
            return gdn_chunk(h0, qs, ks, vs, small, gates, nw, al, db, states, saved)[:2]

        qs, ks, vs = _gdn_parts(qkv_ref)
        _, vjp = jax.vjp(fn, qs, ks, vs, sm_ref[...], _head_cols(gate_ref), nw_ref[...], al_ref[...], db_ref[...],
                         [hist_ref[0, j] for j in range(hb)])
        dqs, dks, dvs, dsm, dgates, dnw, dal, ddb, dstates = vjp(
            (_head_cols(dy_ref), [dstate_ref[h0 + j] for j in range(hb)]))
        for j in range(hb):
            base = j * GDN_HC
            dqkv_ref[:, base:base + GDN_DK] = dqs[j]
            dqkv_ref[:, base + GDN_DK:base + 2 * GDN_DK] = dks[j]
            dqkv_ref[:, base + 2 * GDN_DK:base + GDN_HC] = dvs[j]
            dgate_ref[:, j * GDN_DV:(j + 1) * GDN_DV] = dgates[j].astype(dgate_ref.dtype)
            dstate_ref[h0 + j] = dstates[j]
        _accumulate(dsm_ref, h == 0, dsm)
        first = jnp.logical_and(c == 0, h == 0)
        _accumulate(dnw_ref, first, dnw)
        _accumulate(dal_ref, first, dal)
        _accumulate(ddb_ref, first, ddb)

    return pl.pallas_call(
        body, name="gdn_bwd", grid=(nc, GDN_HEADS // hb),
        in_specs=[ANY, pl.BlockSpec((CHUNK, GDN_HC * hb), lambda c, h: (rev(c), h)),
                  pl.BlockSpec((CHUNK, GDN_DV * hb), lambda c, h: (rev(c), gate_blk + h)),
                  pl.BlockSpec((CHUNK, LANES), lambda c, h: (rev(c), 0)),
                  _full((1, GDN_DV)), _full((1, LANES)), _full((1, LANES)),
                  pl.BlockSpec((1, hb, GDN_DK, GDN_DV), lambda c, h: (rev(c), h, 0, 0)),
                  pl.BlockSpec((1, hb, CHUNK, CHUNK), lambda c, h: (rev(c), h, 0, 0)),
                  pl.BlockSpec((CHUNK, GDN_DV * hb), lambda c, h: (rev(c), h))],
        out_specs=[pl.BlockSpec((CHUNK, GDN_DV * hb), lambda c, h: (rev(c), gate_blk + h)),
                   pl.BlockSpec((CHUNK, GDN_HC * hb), lambda c, h: (rev(c), h)),
                   pl.BlockSpec((CHUNK, LANES), lambda c, h: (rev(c), 0)),
                   _full((1, GDN_DV)), _full((1, LANES)), _full((1, LANES))],
        out_shape=[jax.ShapeDtypeStruct(dproj_main.shape, dproj_main.dtype), jax.ShapeDtypeStruct((t, GDN_CONV), F32),
                   jax.ShapeDtypeStruct((t, LANES), F32), jax.ShapeDtypeStruct((1, GDN_DV), F32),
                   jax.ShapeDtypeStruct((1, LANES), F32), jax.ShapeDtypeStruct((1, LANES), F32)],
        scratch_shapes=[pltpu.VMEM((GDN_HEADS, GDN_DK, GDN_DV), F32)],
        input_output_aliases={0: 0},
        compiler_params=_params(("arbitrary", "arbitrary")),
    )(dproj_main, conv_gdn, proj_main, proj_small, normw, alog, dtb, hist, t_inv, dy)


def out_proj_loss(x, y_ssd, y_gdn, w_out, final_w, target):
    t = x.shape[0]
    tm = min(256, t)

    def body(x_ref, ys_ref, yg_ref, wo_ref, fw_ref, tg_ref, loss_ref, dhid_ref, dys_ref, dyg_ref, dwo_ref, dfw_ref):
        i = pl.program_id(0)
        ys, yg = ys_ref[...], yg_ref[...]
        wo_s, wo_g = wo_ref[:SSD_WIDTH, :], wo_ref[SSD_WIDTH:, :]
        hid = x_ref[...] + _raw_dot(ys, wo_s, 1, 0) + _raw_dot(yg, wo_g, 1, 0)
        out, vjp = jax.vjp(rmsnorm, hid, fw_ref[...])
        err = out - tg_ref[...]
        loss = 0.5 * jnp.sum(jnp.mean(err * err, axis=-1, keepdims=True), axis=0, keepdims=True)
        dhid, dfw = vjp(err * (1.0 / D_MODEL))
        dhid_ref[...] = dhid
        dys_ref[...] = _raw_dot(dhid, wo_s, 1, 1)
        dyg_ref[...] = _raw_dot(dhid, wo_g, 1, 1)
        first = i == 0
        _accumulate(loss_ref, first, jnp.broadcast_to(loss, loss_ref.shape))
        _accumulate(dfw_ref, first, dfw)

        @pl.when(first)
        def _():
            dwo_ref[:SSD_WIDTH, :] = _raw_dot(ys, dhid, 0, 0)
            dwo_ref[SSD_WIDTH:, :] = _raw_dot(yg, dhid, 0, 0)

        @pl.when(i > 0)
        def _():
            dwo_ref[:SSD_WIDTH, :] += _raw_dot(ys, dhid, 0, 0)
            dwo_ref[SSD_WIDTH:, :] += _raw_dot(yg, dhid, 0, 0)

    row = lambda w: pl.BlockSpec((tm, w), lambda i: (i, 0))
    return pl.pallas_call(
        body, name="out_proj_loss", grid=(t // tm,),
        in_specs=[row(D_MODEL), row(SSD_WIDTH), row(GDN_W), _full((SSD_WIDTH + GDN_W, D_MODEL)), _full((1, D_MODEL)),
                  row(D_MODEL)],
        out_specs=[_full((8, LANES)), row(D_MODEL), row(SSD_WIDTH), row(GDN_W), _full((SSD_WIDTH + GDN_W, D_MODEL)),
                   _full((1, D_MODEL))],
        out_shape=[jax.ShapeDtypeStruct((8, LANES), F32), jax.ShapeDtypeStruct((t, D_MODEL), F32),
                   jax.ShapeDtypeStruct((t, SSD_WIDTH), F32), jax.ShapeDtypeStruct((t, GDN_W), F32),
                   jax.ShapeDtypeStruct((SSD_WIDTH + GDN_W, D_MODEL), F32), jax.ShapeDtypeStruct((1, D_MODEL), F32)],
        compiler_params=_params(("arbitrary",)),
    )(x, y_ssd, y_gdn, w_out, final_w, target)


def in_proj_bwd_x(x, normw, w_main, w_small, dproj_main, dsmall_a, dsmall_b, dhid, slabbed):
    t = x.shape[0]
    tm = min(256, t)
    ni = t // tm
    ns = len(slabbed)

    def body(x_ref, nw_ref, wm_ref, ws_ref, dp_ref, da_ref, db_ref, dh_ref, *rest):
        slab_refs, (gx_ref, dnw_ref), land_refs = rest[:ns], rest[ns:ns + 2], rest[ns + 2:2 * ns + 2]
        sems = rest[2 * ns + 2:]
        i = pl.program_id(0)
        start, finish = _slab_exchange(slab_refs, land_refs, ns, *sems)

        @pl.when(i == 0)
        def _():
            start()

        du = _raw_dot(dp_ref[...], wm_ref[...], 1, 1) + _raw_dot(da_ref[...] + db_ref[...], ws_ref[...], 1, 1)
        _, vjp = jax.vjp(rmsnorm, x_ref[...], nw_ref[...])
        dx, dnw = vjp(du)
        gx_ref[...] = dx + dh_ref[...]
        _accumulate(dnw_ref, i == 0, dnw)

        @pl.when(i == ni - 1)
        def _():
            finish()

    row = lambda w: pl.BlockSpec((tm, w), lambda i: (i, 0))
    out = pl.pallas_call(
        body, name="in_proj_bwd_x", grid=(ni,),
        in_specs=[row(D_MODEL), _full((1, D_MODEL)), _full((D_MODEL, MAIN)), _full((D_MODEL, LANES)), row(MAIN),
                  row(LANES), row(LANES), row(D_MODEL)] + [HBM] * ns,
        out_specs=[row(D_MODEL), _full((1, D_MODEL))] + [HBM] * ns,
        out_shape=[jax.ShapeDtypeStruct((t, D_MODEL), F32), jax.ShapeDtypeStruct((1, D_MODEL), F32)]
        + _slab_exchange_shapes(slabbed, []),
        scratch_shapes=_slab_exchange_sems(ns),
        compiler_params=_params(("arbitrary",)),
    )(x, normw, w_main, w_small, dproj_main, dsmall_a, dsmall_b, dhid, *slabbed)
    return out[0], out[1], out[2:]


def in_proj_bwd_w(u, dproj_main, dsmall_a, dsmall_b, slabbed):
    t = u.shape[0]
    tm, tn = min(1024, t), MAIN // 4
    nj, ni = MAIN // tn, t // tm
    ns = len(slabbed)

    def body(u_ref, dp_ref, da_ref, db_ref, *rest):
        slab_refs, (dwm_ref, dws_ref), land_refs, sems = rest[:ns], rest[ns:ns + 2], rest[ns + 2:2 * ns + 2], rest[2 * ns + 2:]
        j, i = pl.program_id(0), pl.program_id(1)
        start, finish = _slab_exchange(slab_refs, land_refs, ns, *sems)

        @pl.when(jnp.logical_and(j == 0, i == 0))
        def _():
            start()

        uu = u_ref[...]
        _accumulate(dwm_ref, i == 0, _raw_dot(uu, dp_ref[...], 0, 0))

        @pl.when(j == 0)
        def _():
            _accumulate(dws_ref, i == 0, _raw_dot(uu, da_ref[...] + db_ref[...], 0, 0))

        @pl.when(jnp.logical_and(j == nj - 1, i == ni - 1))
        def _():
            finish()

    out = pl.pallas_call(
        body, name="in_proj_bwd_w", grid=(nj, ni),
        in_specs=[pl.BlockSpec((tm, D_MODEL), lambda j, i: (i, 0)), pl.BlockSpec((tm, tn), lambda j, i: (i, j)),
                  pl.BlockSpec((tm, LANES), lambda j, i: (i, 0)), pl.BlockSpec((tm, LANES), lambda j, i: (i, 0))]
        + [HBM] * ns,
        out_specs=[pl.BlockSpec((D_MODEL, tn), lambda j, i: (0, j)), _full((D_MODEL, LANES))] + [HBM] * ns,
        out_shape=[jax.ShapeDtypeStruct((D_MODEL, MAIN), F32), jax.ShapeDtypeStruct((D_MODEL, LANES), F32)]
        + _slab_exchange_shapes(slabbed, []),
        scratch_shapes=_slab_exchange_sems(ns),
        compiler_params=_params(("arbitrary", "arbitrary")),
    )(u, dproj_main, dsmall_a, dsmall_b, *slabbed)
    return out[0], out[1], out[2:]


def sum_slabs(a, name):
    n, rows, cols = a.shape
    tr = 64 if rows % 64 == 0 else rows

    def body(a_ref, o_ref):
        acc = a_ref[0].astype(F32)
        for d in range(1, n):
            acc = acc + a_ref[d].astype(F32)
        o_ref[...] = acc

    return pl.pallas_call(
        body, name=name, grid=(rows // tr,),
        in_specs=[pl.BlockSpec((n, tr, cols), lambda i: (0, i, 0))],
        out_specs=pl.BlockSpec((tr, cols), lambda i: (i, 0)),
        out_shape=jax.ShapeDtypeStruct((rows, cols), F32),
        compiler_params=_params(("arbitrary",)),
    )(a)


def adamw(w, g, m, v, name):
    rows, cols = w.shape
    tr = 128 if rows % 128 == 0 else rows

    def body(w_ref, g_ref, m_ref, v_ref, d_ref, nm_ref, nv_ref):
        gg = g_ref[...]
        nm = ADAM_B1 * m_ref[...] + (1.0 - ADAM_B1) * gg
        nv = ADAM_B2 * v_ref[...] + (1.0 - ADAM_B2) * (gg * gg)
        m_hat = nm / (1.0 - ADAM_B1 ** ADAM_STEP)
        v_hat = nv / (1.0 - ADAM_B2 ** ADAM_STEP)
        d_ref[...] = -ADAM_LR * (m_hat / (jnp.sqrt(v_hat) + ADAM_EPS) + ADAM_WD * w_ref[...])
        nm_ref[...] = nm
        nv_ref[...] = nv

    spec = pl.BlockSpec((tr, cols), lambda i: (i, 0))
    shp = jax.ShapeDtypeStruct((rows, cols), F32)
    return pl.pallas_call(
        body, name=name, grid=(rows // tr,), in_specs=[spec] * 4, out_specs=[spec] * 3, out_shape=[shp] * 3,
        compiler_params=_params(("arbitrary",)),
    )(w, g, m, v)


def _my_place():
    return lax.axis_index("x"), lax.axis_index("y"), lax.axis_index("c")


def gather_weights(big, small):
    nb, n = len(big), len(big) + len(small)
    parts = 4

    def body(*refs):
        srcs, outs = refs[:n], refs[n:2 * n]
        land_a, land_b = refs[2 * n:2 * n + nb], refs[2 * n + nb:2 * n + 2 * nb]
        send_sems, recv_sems, fwd_send, fwd_recv, local_sems = refs[2 * n + 2 * nb:]
        x, y, c = _my_place()
        me = 2 * x + y
        chips = [(1 - x, y), (x, 1 - y), (1 - x, 1 - y)]
        half = [a.shape[0] // 2 for a in big]

        def ici(j, i):
            px, py = chips[j]
            if i < nb:
                src, dst = srcs[i].at[pl.ds(c * half[i], half[i])], land_a[i].at[j]
            else:
                src, dst = srcs[i], outs[i].at[me]
            return pltpu.make_async_remote_copy(src_ref=src, dst_ref=dst, send_sem=send_sems.at[j * n + i],
                                                recv_sem=recv_sems.at[j * n + i], device_id=(px, py, c),
                                                device_id_type=MESH)

        def ici_arrival(j, i):
            px, py = chips[j]
            dst = land_a[i].at[j] if i < nb else outs[i].at[2 * px + py]
            return pltpu.make_async_remote_copy(src_ref=dst, dst_ref=dst, send_sem=send_sems.at[j * n + i],
                                                recv_sem=recv_sems.at[j * n + i], device_id=(px, py, c),
                                                device_id_type=MESH)

        def forward(j, i, p):
            rows = half[i] // parts
            k = (j * nb + i) * parts + p
            return pltpu.make_async_remote_copy(
                src_ref=land_a[i].at[j, pl.ds(p * rows, rows)], dst_ref=land_b[i].at[j, pl.ds(p * rows, rows)],
                send_sem=fwd_send.at[k], recv_sem=fwd_recv.at[k], device_id=(x, y, 1 - c), device_id_type=MESH)

        def store(j, i, from_sibling):
            px, py = chips[j]
            buf, h = (land_b, 1 - c) if from_sibling else (land_a, c)
            k = n + (j * nb + i) * 2 + (1 if from_sibling else 0)
            return pltpu.make_async_copy(buf[i].at[j], outs[i].at[2 * px + py, pl.ds(h * half[i], half[i])],
                                         local_sems.at[k])

        own = [pltpu.make_async_copy(srcs[i], outs[i].at[me], local_sems.at[i]) for i in range(n)]
        sends = [ici(j, i) for j in range(3) for i in range(n)]
        for cp in own + sends:
            cp.start()
        pending = []
        for j in range(3):
            for i in range(n):
                ici_arrival(j, i).wait_recv()
                if i < nb:
                    fw = [forward(j, i, p) for p in range(parts)]
                    st = store(j, i, False)
                    for cp in fw + [st]:
                        cp.start()
                    pending += [cp.wait_send for cp in fw] + [st.wait]
        for j in range(3):
            for i in range(nb):
                for p in range(parts):
                    forward(j, i, p).wait_recv()
                st = store(j, i, True)
                st.start()
                pending.append(st.wait)
        for cp in sends:
            cp.wait_send()
        for wait in pending:
            wait()
        for cp in own:
            cp.wait()

    shards = list(big) + list(small)
    lands = [pltpu.VMEM((3, a.shape[0] // 2) + a.shape[1:], a.dtype) for a in big]
    return pl.pallas_call(
        body, name="gather_weights",
        in_specs=[HBM] * n, out_specs=[HBM] * n,
        out_shape=[jax.ShapeDtypeStruct((N_CHIP,) + s.shape, s.dtype) for s in shards],
        scratch_shapes=lands + lands + [
            pltpu.SemaphoreType.DMA((3 * n,)), pltpu.SemaphoreType.DMA((3 * n,)),
            pltpu.SemaphoreType.DMA((3 * nb * parts,)), pltpu.SemaphoreType.DMA((3 * nb * parts,)),
            pltpu.SemaphoreType.DMA((n + 6 * nb,))],
        compiler_params=pltpu.CompilerParams(vmem_limit_bytes=VMEM_LIMIT),
    )(*shards)


def _peer(x, y, c, mask):
    mx, my, mc = (mask >> 2) & 1, (mask >> 1) & 1, mask & 1
    return (x ^ mx if mx else x, y ^ my if my else y, c ^ mc if mc else c)


def _slab_exchange_shapes(slabbed, replicated):
    return ([jax.ShapeDtypeStruct(a.shape, a.dtype) for a in slabbed]
            + [jax.ShapeDtypeStruct((N_DEV,) + a.shape, a.dtype) for a in replicated])


def _slab_exchange_sems(n):
    return [pltpu.SemaphoreType.DMA((7 * n,)), pltpu.SemaphoreType.DMA((7 * n,)), pltpu.SemaphoreType.DMA((n,))]


def _slab_exchange(srcs, outs, ns, send_sems, recv_sems, local_sems):
    n = len(srcs)
    x, y, c = _my_place()
    me = 4 * x + 2 * y + c

    def piece(i, dev):
        return srcs[i].at[dev] if i < ns else srcs[i]

    def copies(arriving):
        out = []
        for mask in range(1, N_DEV):
            px, py, pc = _peer(x, y, c, mask)
            dev = 4 * px + 2 * py + pc
            for i in range(n):
                k = (mask - 1) * n + i
                out.append(pltpu.make_async_remote_copy(
                    src_ref=piece(i, dev), dst_ref=outs[i].at[dev if arriving else me], send_sem=send_sems.at[k],
                    recv_sem=recv_sems.at[k], device_id=(px, py, pc), device_id_type=MESH))
        return out

    def local():
        return [pltpu.make_async_copy(piece(i, me), outs[i].at[me], local_sems.at[i]) for i in range(n)]

    def start():
        for cp in local() + copies(False):
            cp.start()

    def finish():
        for cp in copies(True):
            cp.wait_recv()
        for cp in copies(False):
            cp.wait_send()
        for cp in local():
            cp.wait()

    return start, finish


def exchange_halves(halves, replicated):
    n, nr = len(halves), len(replicated)
    streams = 8

    def body(*refs):
        srcs, rep_srcs, outs, rep_outs = refs[:n], refs[n:n + nr], refs[n + nr:2 * n + nr], refs[2 * n + nr:2 * (n + nr)]
        refs = refs[2 * (n + nr):]
        mine, theirs = refs[:n], refs[n:2 * n]
        send_sems, recv_sems, in_sems, out_sems = refs[2 * n:2 * n + 4]
        rep_start, rep_finish = _slab_exchange(rep_srcs, rep_outs, 0, *refs[2 * n + 4:])
        rep_start()
        x, y, c = _my_place()
        loads = [pltpu.make_async_copy(srcs[i], mine[i], in_sems.at[i]) for i in range(n)]
        for cp in loads:
            cp.start()
        for cp in loads:
            cp.wait()

        def chunk_copy(i, s):
            rows = halves[i].shape[0] // streams
            k = i * streams + s
            return pltpu.make_async_remote_copy(
                src_ref=mine[i].at[pl.ds(s * rows, rows)], dst_ref=theirs[i].at[pl.ds(s * rows, rows)],
                send_sem=send_sems.at[k], recv_sem=recv_sems.at[k], device_id=(x, y, 1 - c), device_id_type=MESH)

        sends = [chunk_copy(i, s) for i in range(n) for s in range(streams)]
        for cp in sends:
            cp.start()
        own = [pltpu.make_async_copy(mine[i], outs[i].at[c], out_sems.at[i]) for i in range(n)]
        for cp in own:
            cp.start()
        for cp in sends:
            cp.wait_recv()
        got = [pltpu.make_async_copy(theirs[i], outs[i].at[1 - c], out_sems.at[n + i]) for i in range(n)]
        for cp in got:
            cp.start()
        for cp in sends:
            cp.wait_send()
        for cp in own + got:
            cp.wait()
        rep_finish()

    vmem = [pltpu.VMEM(a.shape, a.dtype) for a in halves]
    out = pl.pallas_call(
        body, name="exchange_halves",
        in_specs=[HBM] * (n + nr), out_specs=[HBM] * (n + nr),
        out_shape=[jax.ShapeDtypeStruct((2,) + a.shape, a.dtype) for a in halves]
        + _slab_exchange_shapes([], replicated),
        scratch_shapes=vmem + vmem + [pltpu.SemaphoreType.DMA((n * streams,)), pltpu.SemaphoreType.DMA((n * streams,)),
                                      pltpu.SemaphoreType.DMA((n,)), pltpu.SemaphoreType.DMA((2 * n,))]
        + _slab_exchange_sems(nr),
        compiler_params=pltpu.CompilerParams(vmem_limit_bytes=VMEM_LIMIT),
    )(*halves, *replicated)
    return out[:n], out[n:]


def _pack_cols(pieces):
    offs, pos = [], 0
    for a in pieces:
        offs.append(pos)
        pos += a.shape[1]
    rows8 = [jnp.pad(a.astype(F32), ((0, 8 - a.shape[0]), (0, 0))) for a in pieces]
    return jnp.concatenate(rows8, axis=1), offs


def adamw_many(ws, gs, ms, vs):
    n = len(ws)

    def body(*refs):
        w_r, g_r, m_r, v_r = refs[:n], refs[n:2 * n], refs[2 * n:3 * n], refs[3 * n:4 * n]
        d_o, m_o, v_o = refs[4 * n:5 * n], refs[5 * n:6 * n], refs[6 * n:7 * n]
        for i in range(n):
            gg = g_r[i][...]
            nm = ADAM_B1 * m_r[i][...] + (1.0 - ADAM_B1) * gg
            nv = ADAM_B2 * v_r[i][...] + (1.0 - ADAM_B2) * (gg * gg)
            m_hat = nm / (1.0 - ADAM_B1 ** ADAM_STEP)
            v_hat = nv / (1.0 - ADAM_B2 ** ADAM_STEP)
            d_o[i][...] = -ADAM_LR * (m_hat / (jnp.sqrt(v_hat) + ADAM_EPS) + ADAM_WD * w_r[i][...])
            m_o[i][...] = nm
            v_o[i][...] = nv

    shapes = [jax.ShapeDtypeStruct(w.shape, F32) for w in ws]
    out = pl.pallas_call(body, name="adamw_small", out_shape=shapes * 3,
                         compiler_params=pltpu.CompilerParams(vmem_limit_bytes=VMEM_LIMIT))(*ws, *gs, *ms, *vs)
    return out[:n], out[n:2 * n], out[2 * n:]


def _lanes(vec, start):
    n = vec.shape[-1]
    return jnp.pad(vec.reshape(1, n).astype(F32), ((0, 0), (start, LANES - start - n)))


def kernel(x, norm_w, w_in, ssd_conv_w, ssd_conv_b, ssd_dt_bias, ssd_a_log, ssd_d, ssd_norm_w, gdn_conv_w, gdn_dt_bias, gdn_a_log, gdn_norm_w, w_out, final_norm_w, loss_target, m_norm_w, m_w_in, m_ssd_conv_w, m_ssd_conv_b, m_ssd_dt_bias, m_ssd_a_log, m_ssd_d, m_ssd_norm_w, m_gdn_conv_w, m_gdn_dt_bias, m_gdn_a_log, m_gdn_norm_w, m_w_out, m_final_norm_w, v_norm_w, v_w_in, v_ssd_conv_w, v_ssd_conv_b, v_ssd_dt_bias, v_ssd_a_log, v_ssd_d, v_ssd_norm_w, v_gdn_conv_w, v_gdn_dt_bias, v_gdn_a_log, v_gdn_norm_w, v_w_out, v_final_norm_w):
    xs = x[0]
    target = loss_target[0]
    chip = 2 * lax.axis_index("x") + lax.axis_index("y")
    w_in_shard, w_out_shard = w_in[0], w_out[0]
    in_cols = w_in_shard.shape[1]
    out_rows = w_out_shard.shape[0]

    g_in, g_out, g_cs, g_cg = gather_weights(
        [w_in_shard.astype(MXU_DTYPE), w_out_shard.astype(MXU_DTYPE)], [ssd_conv_w[0], gdn_conv_w[0]])
    w_in_full = jnp.transpose(g_in, (1, 0, 2)).reshape(D_MODEL, IN_DIM)
    w_out_full = g_out.reshape(N_CHIP * out_rows, D_MODEL)
    cw_ssd = _ssd_perm(jnp.transpose(g_cs, (1, 0, 2)).reshape(4, SSD_CONV))
    cw_gdn = _gdn_perm(jnp.transpose(g_cg, (1, 0, 2)).reshape(4, GDN_CONV))
    cb_ssd = _ssd_perm(ssd_conv_b)
    cb_gdn = jnp.zeros((1, GDN_CONV), F32)
    o_xbc, o_dt, o_gate, o_qkv, o_ab = 1024, 2560, 2576, 3600, 6672
    w_main = jnp.concatenate([w_in_full[:, :o_xbc], w_in_full[:, o_gate:o_qkv], _gdn_perm(w_in_full[:, o_qkv:o_ab]),
                              _ssd_perm(w_in_full[:, o_xbc:o_dt])], axis=1)
    w_small = jnp.concatenate([w_in_full[:, o_dt:o_gate], w_in_full[:, o_ab:],
                               jnp.zeros((D_MODEL, LANES - 32), MXU_DTYPE)], axis=1)
    alog = _lanes(ssd_a_log, 0) + _lanes(gdn_a_log, LANE_GA)
    dtb = _lanes(ssd_dt_bias, 0) + _lanes(gdn_dt_bias, LANE_GA)
    dvec = _lanes(ssd_d, 0)
    fw = final_norm_w.reshape(1, D_MODEL)

    proj_main, proj_small, u = in_proj(xs, norm_w, w_main, w_small)
    proj_main, conv_out = in_proj_conv(proj_main, u, w_main, jnp.concatenate([cw_gdn, cw_ssd], axis=1),
                                       jnp.concatenate([cb_gdn, cb_ssd], axis=1))
    conv_ssd = conv_gdn = conv_out
    y_ssd, hist_ssd = ssd_fwd(conv_ssd, proj_main, proj_small, ssd_norm_w, alog, dtb, dvec)
    y_gdn, hist_gdn, tinv_gdn = gdn_fwd(conv_gdn, proj_main, proj_small, gdn_norm_w, alog, dtb)

    loss_blk, dhid, dy_ssd, dy_gdn, d_w_out, d_fw = out_proj_loss(xs, y_ssd, y_gdn, w_out_full, fw, target)
    dconv_ssd, dproj_main, dsmall_ssd, d_ssd_nw, d_alog_s, d_dtb_s, d_dvec = ssd_bwd(
        conv_ssd, proj_main, proj_small, ssd_norm_w, alog, dtb, dvec, hist_ssd, dy_ssd)
    dproj_main, dconv_gdn, dsmall_gdn, d_gdn_nw, d_alog_g, d_dtb_g = gdn_bwd(
        dproj_main, conv_gdn, proj_main, proj_small, gdn_norm_w, alog, dtb, hist_gdn, tinv_gdn, dy_gdn)
    dproj_main, dwb_ssd = conv_bwd(dproj_main, proj_main, COL_SSD, SSD_CONV, cw_ssd, cb_ssd, dconv_ssd, "conv_bwd_ssd")
    dproj_main, dwb_gdn = conv_bwd(dproj_main, proj_main, COL_GDN, GDN_CONV, cw_gdn, cb_gdn, dconv_gdn, "conv_bwd_gdn")
    slabs_out = d_w_out.reshape(N_DEV, out_rows // 2, D_MODEL).astype(COMM_DTYPE)
    d_w_main, d_w_small, (r_out,) = in_proj_bwd_w(u, dproj_main, dsmall_ssd, dsmall_gdn, [slabs_out])
    d_w_in = jnp.concatenate([d_w_main[:, :COL_GATE], _ssd_unperm(d_w_main[:, COL_SSD:]), d_w_small[:, 0:16],
                              d_w_main[:, COL_GATE:COL_GDN], _gdn_unperm(d_w_main[:, COL_GDN:COL_SSD]),
                              d_w_small[:, 16:32]], axis=1)
    d_w_in = jnp.transpose(d_w_in.reshape(D_MODEL, N_CHIP, in_cols), (1, 0, 2))
    slabs_in = d_w_in.reshape(N_DEV, D_MODEL // 2, in_cols).astype(COMM_DTYPE)
    grad_x, d_norm_w, (r_in,) = in_proj_bwd_x(xs, norm_w, w_main, w_small, dproj_main, dsmall_ssd, dsmall_gdn,
                                               dhid, [slabs_in])
    d_alog, d_dtb = d_alog_s + d_alog_g, d_dtb_s + d_dtb_g
    packed, (o_nw, o_cs, o_cg, o_snw, o_fw, o_al, o_db, o_dv, o_gnw, o_loss) = _pack_cols([
        d_norm_w, _ssd_unperm(dwb_ssd), _gdn_unperm(dwb_gdn),
        d_ssd_nw.reshape(1, SSD_WIDTH), d_fw, d_alog, d_dtb, d_dvec, d_gdn_nw, loss_blk])

    half_in = sum_slabs(r_in, "sum_w_in")
    half_out = sum_slabs(r_out, "sum_w_out")
    (full_in, full_out), (r_small,) = exchange_halves([half_in, half_out], [packed])
    tot = sum_slabs(r_small, "sum_small")
    grad_w_in = full_in.reshape(D_MODEL, in_cols)
    grad_w_out = full_out.reshape(out_rows, D_MODEL)
    loss = tot[0, o_loss]
    sc, gc = ssd_conv_w.shape[2], gdn_conv_w.shape[2]
    row = lambda off, n, r=0: tot[r:r + 1, off:off + n]
    gs = [row(o_nw, D_MODEL),
          lax.dynamic_slice(tot, (0, o_cs + chip * sc), (4, sc)),
          row(o_cs, SSD_CONV, 4),
          row(o_db, SSD_HEADS), row(o_al, SSD_HEADS), row(o_dv, SSD_HEADS),
          row(o_snw, SSD_WIDTH),
          lax.dynamic_slice(tot, (0, o_cg + chip * gc), (4, gc)),
          row(o_db + LANE_GA, GDN_HEADS), row(o_al + LANE_GA, GDN_HEADS),
          row(o_gnw, GDN_DV), row(o_fw, D_MODEL)]

    names = ["norm_w", "ssd_conv_w", "ssd_conv_b", "ssd_dt_bias", "ssd_a_log", "ssd_d", "ssd_norm_w", "gdn_conv_w",
             "gdn_dt_bias", "gdn_a_log", "gdn_norm_w", "final_norm_w"]
    ws = [norm_w, ssd_conv_w, ssd_conv_b, ssd_dt_bias, ssd_a_log, ssd_d, ssd_norm_w, gdn_conv_w, gdn_dt_bias,
          gdn_a_log, gdn_norm_w, final_norm_w]
    ms = [m_norm_w, m_ssd_conv_w, m_ssd_conv_b, m_ssd_dt_bias, m_ssd_a_log, m_ssd_d, m_ssd_norm_w, m_gdn_conv_w,
          m_gdn_dt_bias, m_gdn_a_log, m_gdn_norm_w, m_final_norm_w]
    vs = [v_norm_w, v_ssd_conv_w, v_ssd_conv_b, v_ssd_dt_bias, v_ssd_a_log, v_ssd_d, v_ssd_norm_w, v_gdn_conv_w,
          v_gdn_dt_bias, v_gdn_a_log, v_gdn_norm_w, v_final_norm_w]
    shapes = [w.shape for w in ws]
    flat = lambda arrs: [a.reshape(g.shape) for a, g in zip(arrs, gs)]
    d_s, m_s, v_s = adamw_many(flat(ws), gs, flat(ms), flat(vs))
    back = lambda arrs: dict(zip(names, [a.reshape(s) for a, s in zip(arrs, shapes)]))
    delta, new_m, new_v, grads = back(d_s), back(m_s), back(v_s), back(gs)
    d_in, m_in, v_in = adamw(w_in_shard, grad_w_in, m_w_in[0], v_w_in[0], "adamw_w_in")
    d_out, m_out, v_out = adamw(w_out_shard, grad_w_out, m_w_out[0], v_w_out[0], "adamw_w_out")
    for tbl, a_in, a_out in ((grads, grad_w_in, grad_w_out), (delta, d_in, d_out), (new_m, m_in, m_out),
                             (new_v, v_in, v_out)):
        tbl["w_in"] = a_in[None]
        tbl["w_out"] = a_out[None]

    order = ["norm_w", "w_in", "ssd_conv_w", "ssd_conv_b", "ssd_dt_bias", "ssd_a_log", "ssd_d", "ssd_norm_w",
             "gdn_conv_w", "gdn_dt_bias", "gdn_a_log", "gdn_norm_w", "w_out", "final_norm_w"]
    return (loss.reshape(()), grad_x[None], *[grads[k] for k in order], *[delta[k] for k in order],
            *[new_m[k] for k in order], *[new_v[k] for k in order])
```

```python
import functools

import jax
import jax.numpy as jnp
from jax import lax
from jax.experimental import pallas as pl
from jax.experimental.pallas import tpu as pltpu

F32 = jnp.float32
MXU_DTYPE = jnp.bfloat16
COMM_DTYPE = jnp.bfloat16
MESH = pl.DeviceIdType.MESH

D_MODEL = 1024
CHUNK = 64
EPS = 1e-6
SSD_HEADS, SSD_GROUPS, SSD_STATE = 16, 2, 128
SSD_WIDTH, SSD_CONV = 1024, 1536
SSD_GW = SSD_WIDTH // SSD_GROUPS
SSD_GC = SSD_GW + 2 * SSD_STATE
GDN_HEADS, GDN_DK, GDN_DV = 8, 128, 128
GDN_W, GDN_CONV = 1024, 3072
GDN_HC = 2 * GDN_DK + GDN_DV
IN_DIM = 6688
MAIN = 6656
LANES = 128
COL_Z, COL_GATE, COL_GDN, COL_SSD = 0, 1024, 2048, 5120
COL_CONV = COL_GDN
GDN_HB = 8
LANE_GA, LANE_GB = 16, 24
N_DEV, N_CHIP = 8, 4
VMEM_LIMIT = 52 * 1024 * 1024

ADAM_LR, ADAM_B1, ADAM_B2, ADAM_EPS, ADAM_WD, ADAM_STEP = 0.001, 0.9, 0.999, 1e-08, 0.01, 10


def _ssd_perm(a):
    lead, nb = a.shape[:-1], SSD_GROUPS * SSD_STATE
    x = a[..., :SSD_WIDTH].reshape(*lead, SSD_GROUPS, SSD_GW)
    b = a[..., SSD_WIDTH:SSD_WIDTH + nb].reshape(*lead, SSD_GROUPS, SSD_STATE)
    c = a[..., SSD_WIDTH + nb:].reshape(*lead, SSD_GROUPS, SSD_STATE)
    return jnp.concatenate([x, b, c], axis=-1).reshape(*lead, SSD_CONV)


def _ssd_unperm(a):
    lead = a.shape[:-1]
    g = a.reshape(*lead, SSD_GROUPS, SSD_GC)
    parts = [g[..., :SSD_GW], g[..., SSD_GW:SSD_GW + SSD_STATE], g[..., SSD_GW + SSD_STATE:]]
    return jnp.concatenate([p.reshape(*lead, -1) for p in parts], axis=-1)


def _gdn_perm(a):
    lead = a.shape[:-1]
    return jnp.swapaxes(a.reshape(*lead, 3, GDN_HEADS, GDN_DK), -3, -2).reshape(*lead, GDN_CONV)


def _gdn_unperm(a):
    lead = a.shape[:-1]
    return jnp.swapaxes(a.reshape(*lead, GDN_HEADS, 3, GDN_DK), -3, -2).reshape(*lead, GDN_CONV)


def _split(a, n):
    parts, rest = [], a.astype(F32)
    for i in range(n):
        p = rest.astype(MXU_DTYPE)
        parts.append(p)
        if i < n - 1:
            rest = rest - p.astype(F32)
    return parts


def _raw_dot(a, b, ca, cb, mode="bf16"):
    d = lambda u, v: lax.dot_general(u, v, (((ca,), (cb,)), ((), ())), preferred_element_type=F32)
    if mode == "bf16":
        return d(a.astype(MXU_DTYPE), b.astype(MXU_DTYPE))
    if mode == "x3":
        (ah, al), (bh, bl) = _split(a, 2), _split(b, 2)
        return d(ah, bh) + (d(ah, bl) + d(al, bh))
    if mode == "sel_a":
        a0 = a.astype(MXU_DTYPE)
        b1, b2, b3 = _split(b, 3)
        return d(a0, b1) + (d(a0, b2) + d(a0, b3))
    assert mode == "sel_b", mode
    b0 = b.astype(MXU_DTYPE)
    a1, a2, a3 = _split(a, 3)
    return d(a1, b0) + (d(a2, b0) + d(a3, b0))


@functools.partial(jax.custom_vjp, nondiff_argnums=(2,))
def mm_nn(a, b, mode="bf16"):
    return _raw_dot(a, b, 1, 0, mode)


@functools.partial(jax.custom_vjp, nondiff_argnums=(2,))
def mm_nt(a, b, mode="bf16"):
    return _raw_dot(a, b, 1, 1, mode)


@functools.partial(jax.custom_vjp, nondiff_argnums=(2,))
def mm_tn(a, b, mode="bf16"):
    return _raw_dot(a, b, 0, 0, mode)


_SAME = {"bf16": ("bf16", "bf16"), "x3": ("x3", "x3")}
_NN_BWD = dict(_SAME, sel_a=("bf16", "sel_a"), sel_b=("sel_b", "bf16"))
_NT_BWD = dict(_SAME, sel_a=("bf16", "sel_b"), sel_b=("sel_b", "bf16"))
_TN_BWD = dict(_SAME, sel_a=("bf16", "sel_a"), sel_b=("sel_a", "bf16"))
mm_nn.defvjp(lambda a, b, m: (_raw_dot(a, b, 1, 0, m), (a, b)),
             lambda m, r, g: (mm_nt(g, r[1], _NN_BWD[m][0]), mm_tn(r[0], g, _NN_BWD[m][1])))
mm_nt.defvjp(lambda a, b, m: (_raw_dot(a, b, 1, 1, m), (a, b)),
             lambda m, r, g: (mm_nn(g, r[1], _NT_BWD[m][0]), mm_tn(g, r[0], _NT_BWD[m][1])))
mm_tn.defvjp(lambda a, b, m: (_raw_dot(a, b, 0, 0, m), (a, b)),
             lambda m, r, g: (mm_nt(r[1], g, _TN_BWD[m][0]), mm_nn(r[0], g, _TN_BWD[m][1])))


@jax.custom_jvp
def sigmoid(x):
    return 1.0 / (1.0 + jnp.exp(-x))


@sigmoid.defjvp
def _sigmoid_jvp(p, t):
    s = sigmoid(p[0])
    return s, t[0] * s * (1.0 - s)


@jax.custom_jvp
def softplus(x):
    return jnp.maximum(x, 0.0) + jnp.log(1.0 + jnp.exp(-jnp.abs(x)))


@softplus.defjvp
def _softplus_jvp(p, t):
    return softplus(p[0]), t[0] * sigmoid(p[0])


def silu(x):
    return x * sigmoid(x)


def rmsnorm(x, w):
    return x * lax.rsqrt(jnp.mean(x * x, axis=-1, keepdims=True) + EPS) * w


def _iota(shape, dim):
    return lax.broadcasted_iota(jnp.int32, shape, dim)


def _tri_inv_impl(mats):
    n = mats[0].shape[0]
    r, c = _iota((n, n), 0), _iota((n, n), 1)
    eye = _ind(r == c)
    blockdiag = _ind((r >> 4) == (c >> 4))
    dot = lambda u, v: _raw_dot(u, v, 1, 0, "x3")
    dot1 = lambda u, v: _raw_dot(u, v, 1, 0)
    each = lambda f, *ls: [f(*xs) for xs in zip(*ls)]
    dg = each(lambda a: a * blockdiag, mats)
    off = each(lambda a, d: a - d, mats, dg)
    m = each(lambda d: -d, dg)
    p = each(lambda x: eye + x, m)
    pw = m
    for _ in range(3):
        pw = each(lambda x: dot1(x, x), pw)
        p = each(lambda x, y: x + dot1(x, y), p, pw)
    e = each(dot, p, off)
    e2 = each(lambda x: dot1(x, x), e)
    q = each(lambda x: eye - x, e)
    q = each(lambda x, y: x + dot1(x, y), q, e2)
    return each(dot, q, p)


def _tri_inv_bwd(ts, gs):
    x = [mm_nt(g, t) for g, t in zip(gs, ts)]
    return [-mm_tn(t, y) for t, y in zip(ts, x)]


@jax.custom_vjp
def tri_inv(mats):
    return _tri_inv_impl(mats)


def _tri_inv_fwd(mats):
    ts = _tri_inv_impl(mats)
    return ts, ts


tri_inv.defvjp(_tri_inv_fwd, lambda ts, gs: (_tri_inv_bwd(ts, gs),))


@jax.custom_vjp
def tri_inv_saved(mats, ts):
    del mats
    return ts


tri_inv_saved.defvjp(lambda mats, ts: (ts, ts),
                     lambda ts, gs: (_tri_inv_bwd(ts, gs), [jnp.zeros_like(t) for t in ts]))


def _ind(cond):
    return jnp.where(cond, 1.0, 0.0).astype(F32)


def _chunk_masks():
    r, c = _iota((CHUNK, CHUNK), 0), _iota((CHUNK, CHUNK), 1)
    return _ind(r >= c), _ind(r > c), _ind(r == c), _ind(_iota((CHUNK, 1), 0) == CHUNK - 1)


def _log_decay_cumsum(small, alog, dtb, tri):
    sp = softplus(small + dtb)
    la = -jnp.exp(alog) * sp
    return sp, mm_nn(tri, la, "sel_a")


def _col_of(x, lane):
    return jnp.sum(x * _ind(_iota((1, LANES), 1) == lane), axis=1, keepdims=True)


def _decay_matrix(col, tri, eye):
    row = jnp.sum(col * eye, axis=0, keepdims=True)
    return jnp.exp((col - row) * tri) * tri


def gdn_chunk(h0, qs, ks, vs, small, gates, normw, alog, dtb, states, saved_t=None):
    tri, strict, eye, last = _chunk_masks()
    _, lac = _log_decay_cumsum(small, alog, dtb, tri)
    heads = range(len(qs))
    each = lambda f, *ls: [f(*xs) for xs in zip(*ls)]
    gc = [_col_of(lac, LANE_GA + h0 + j) for j in heads]
    beta = [sigmoid(_col_of(small, LANE_GB + h0 + j)) for j in heads]
    decay = each(lambda x: _decay_matrix(x, tri, eye), gc)
    gl = each(lambda x: jnp.sum(x * last, axis=0, keepdims=True), gc)
    q = each(lambda x: x * lax.rsqrt(jnp.sum(x * x, axis=-1, keepdims=True) + EPS) * (GDN_DK ** -0.5), qs)
    k = each(lambda x: x * lax.rsqrt(jnp.sum(x * x, axis=-1, keepdims=True) + EPS), ks)
    kb = each(lambda x, b: x * b, k, beta)
    a = each(lambda x, y, d: mm_nt(x, y) * (d * strict), kb, k, decay)
    t = tri_inv(a) if saved_t is None else tri_inv_saved(a, saved_t)
    eg = each(jnp.exp, gc)
    u = each(lambda x, v, b: mm_nn(x, v * b), t, vs, beta)
    w = each(lambda x, y, e: mm_nn(x, y * e), t, kb, eg)
    attn = each(lambda x, y, d: mm_nt(x, y) * d, q, k, decay)
    v_new = each(lambda x, y, s: x - mm_nn(y, s), u, w, states)
    o = each(lambda x, e, s, at, vn: mm_nn(x * e, s) + mm_nn(at, vn), q, eg, states, attn, v_new)
    new_states = each(lambda s, x, y, l, c: s * jnp.exp(l) + mm_tn(y * jnp.exp(l - c), x), states, v_new, k, gl, gc)
    ys = each(lambda x, gt: rmsnorm(x, normw) * silu(gt), o, gates)
    return ys, new_states, t


@jax.custom_vjp
def split_lanes(x):
    return [x[:, i * LANES:(i + 1) * LANES] for i in range(x.shape[1] // LANES)]


@jax.custom_vjp
def join_lanes(xs):
    return jnp.concatenate(xs, axis=1)


split_lanes.defvjp(lambda x: (split_lanes(x), None), lambda _, gs: (join_lanes(gs),))
join_lanes.defvjp(lambda xs: (join_lanes(xs), None), lambda _, g: (split_lanes(g),))


def ssd_chunk(xs, bm, cm, z, small, normw, alog, dtb, dvec, state):
    tri, _, eye, last = _chunk_masks()
    hpg = SSD_HEADS // SSD_GROUPS
    groups = range(len(xs))
    each = lambda f, *ls: [f(*a) for a in zip(*ls)]
    sp, lac = _log_decay_cumsum(small, alog, dtb, tri)
    lac_last = jnp.sum(lac * last, axis=0, keepdims=True)
    sel = [_ind(_iota((LANES, SSD_GW), 0) == g * hpg + (_iota((LANES, SSD_GW), 1) >> 6)) for g in groups]
    expand = lambda v: [mm_nn(v, s, "sel_b") for s in sel]
    dt_e, elac_e, toend_e = expand(sp), expand(jnp.exp(lac)), expand(jnp.exp(lac_last - lac))
    row8, row8e = _iota((8, 1), 0), _iota((8, 1), 0)
    two_e = expand(_ind(row8 == 0) * dvec + _ind(row8 == 1) * jnp.exp(lac_last))
    d_e = each(lambda v: jnp.sum(v * _ind(row8e == 0), axis=0, keepdims=True), two_e)
    chunk_e = each(lambda v: jnp.sum(v * _ind(row8e == 1), axis=0, keepdims=True), two_e)
    xdt = each(lambda a, b: a * b, xs, dt_e)
    cb = each(mm_nt, cm, bm)
    y = each(lambda c_, st, el, x_, d_: mm_nn(c_, st) * el + x_ * d_, cm, state, elac_e, xs, d_e)
    x_pairs = each(split_lanes, xdt)
    half = [_ind((_iota((1, LANES), 1) >> 6) == s) for s in range(2)]
    lms = [[_decay_matrix(_col_of(lac, g * hpg + j), tri, eye) for j in range(hpg)] for g in groups]
    terms = [[mm_nn(cb[g] * lms[g][j], x_pairs[g][j // 2] * half[j % 2]) for j in range(hpg)] for g in groups]
    y = [y[g] + join_lanes([terms[g][2 * p] + terms[g][2 * p + 1] for p in range(hpg // 2)]) for g in groups]
    new_state = each(lambda st, ce, b_, xd, te: st * ce + mm_tn(b_, xd * te), state, chunk_e, bm, xdt, toend_e)
    out = each(lambda y_, z_, nw: rmsnorm(y_ * silu(z_), nw), y, z, normw)
    return out, new_state


def _params(sem=None):
    return pltpu.CompilerParams(dimension_semantics=sem, vmem_limit_bytes=VMEM_LIMIT)


def _full(shape):
    n = len(shape)
    return pl.BlockSpec(shape, lambda *_: (0,) * n)


ANY = pl.BlockSpec(memory_space=pl.ANY)
HBM = pl.BlockSpec(memory_space=pltpu.HBM)


def in_proj(x, normw, w_main, w_small):
    t = x.shape[0]
    tm, tn = min(1024, t), 512

    def body(x_ref, nw_ref, wm_ref, ws_ref, pm_ref, ps_ref, u_ref):
        @pl.when(pl.program_id(1) == 0)
        def _():
            u = rmsnorm(x_ref[...], nw_ref[...]).astype(MXU_DTYPE)
            u_ref[...] = u
            ps_ref[...] = _raw_dot(u, ws_ref[...], 1, 0)
        pm_ref[...] = _raw_dot(u_ref[...], wm_ref[...], 1, 0)

    return pl.pallas_call(
        body, name="in_proj", grid=(t // tm, COL_CONV // tn),
        in_specs=[pl.BlockSpec((tm, D_MODEL), lambda i, j: (i, 0)), _full((1, D_MODEL)),
                  pl.BlockSpec((D_MODEL, tn), lambda i, j: (0, j)), _full((D_MODEL, LANES))],
        out_specs=[pl.BlockSpec((tm, tn), lambda i, j: (i, j)), pl.BlockSpec((tm, LANES), lambda i, j: (i, 0)),
                   pl.BlockSpec((tm, D_MODEL), lambda i, j: (i, 0))],
        out_shape=[jax.ShapeDtypeStruct((t, MAIN), F32), jax.ShapeDtypeStruct((t, LANES), F32),
                   jax.ShapeDtypeStruct((t, D_MODEL), MXU_DTYPE)],
        compiler_params=_params(("arbitrary", "arbitrary")),
    )(x, normw, w_main, w_small)


CONV_TC = 512
HALO = 8


def _shift_down(cur, prev, s):
    rolled = pltpu.roll(cur, s, 0)
    top = jnp.where(_iota((HALO, cur.shape[1]), 0) < s, pltpu.roll(prev, s, 0), rolled[:HALO])
    if cur.shape[0] == HALO:
        return top
    return jnp.concatenate([top, rolled[HALO:]], axis=0)


def _shift_up(cur, nxt, s):
    n = cur.shape[0]
    rolled = pltpu.roll(cur, n - s, 0)
    bot = jnp.where(_iota((HALO, cur.shape[1]), 0) >= HALO - s, pltpu.roll(nxt, HALO - s, 0), rolled[n - HALO:])
    return jnp.concatenate([rolled[:n - HALO], bot], axis=0)


def _conv_pre(cur, prev, w_ref, b):
    acc = cur * w_ref[3:4, :] + b
    shifted = [cur]
    for s in (1, 2, 3):
        sh = _shift_down(cur, prev, s)
        shifted.append(sh)
        acc = acc + sh * w_ref[3 - s:4 - s, :]
    return acc, shifted


def in_proj_conv(proj_main, u, w_main, w, b):
    t = u.shape[0]
    tm, tn = min(1024, t), CONV_TC
    rc = min(256, tm)
    c0, nj = COL_CONV // tn, (MAIN - COL_CONV) // tn

    def body(alias_ref, u_ref, wm_ref, w_ref, b_ref, pm_ref, out_ref, halo_ref):
        del alias_ref
        j = pl.program_id(1)

        @pl.when(pl.program_id(0) == 0)
        def _():
            halo_ref[j] = jnp.zeros((HALO, tn), F32)

        prev = halo_ref[j]
        for r in range(tm // rc):
            rows = pl.ds(r * rc, rc)
            p = _raw_dot(u_ref[rows, :], wm_ref[...], 1, 0)
            pm_ref[rows, :] = p
            pre, _ = _conv_pre(p, prev, w_ref, b_ref[...])
            out_ref[rows, :] = silu(pre)
            prev = p[rc - HALO:]
        halo_ref[j] = prev

    return pl.pallas_call(
        body, name="in_proj_conv", grid=(t // tm, nj),
        in_specs=[ANY, pl.BlockSpec((tm, D_MODEL), lambda i, j: (i, 0)),
                  pl.BlockSpec((D_MODEL, tn), lambda i, j: (0, c0 + j)),
                  pl.BlockSpec((4, tn), lambda i, j: (0, j)), pl.BlockSpec((1, tn), lambda i, j: (0, j))],
        out_specs=[pl.BlockSpec((tm, tn), lambda i, j: (i, c0 + j)), pl.BlockSpec((tm, tn), lambda i, j: (i, j))],
        out_shape=[jax.ShapeDtypeStruct(proj_main.shape, F32), jax.ShapeDtypeStruct((t, MAIN - COL_CONV), F32)],
        scratch_shapes=[pltpu.VMEM((nj, HALO, tn), F32)],
        input_output_aliases={0: 0},
        compiler_params=_params(("arbitrary", "arbitrary")),
    )(proj_main, u, w_main, w, b)


def _dsilu(pre):
    sg = sigmoid(pre)
    return sg * (1.0 + pre * (1.0 - sg))


def conv_bwd(dproj_main, proj_main, col0, width, w, b, dout, name):
    t = proj_main.shape[0]
    tt, c0 = min(512, t), col0 // CONV_TC
    nt = t // tt
    after = lambda i: jnp.minimum((i + 1) * (tt // HALO), t // HALO - 1)

    def body(alias_ref, cur_ref, prev_ref, nxt_ref, w_ref, b_ref, do_ref, do_nxt_ref, dx_ref, dwb_ref):
        del alias_ref
        i = pl.program_id(1)
        cur, bias = cur_ref[...], b_ref[...]
        prev = jnp.where(i > 0, prev_ref[...], 0.0)
        pre, shifted = _conv_pre(cur, prev, w_ref, bias)
        dpre = do_ref[...] * _dsilu(pre)
        pre_nxt, _ = _conv_pre(nxt_ref[...], cur[tt - HALO:], w_ref, bias)
        dpre_nxt = jnp.where(i < nt - 1, do_nxt_ref[...] * _dsilu(pre_nxt), 0.0)
        dx = dpre * w_ref[3:4, :]
        for s in (1, 2, 3):
            dx = dx + _shift_up(dpre, dpre_nxt, s) * w_ref[3 - s:4 - s, :]
        dx_ref[...] = dx.astype(dx_ref.dtype)
        row = _iota((HALO, CONV_TC), 0)
        upd = jnp.where(row == 4, jnp.sum(dpre, axis=0, keepdims=True), 0.0)
        for s in range(4):
            upd = upd + jnp.where(row == 3 - s, jnp.sum(dpre * shifted[s], axis=0, keepdims=True), 0.0)
        _accumulate(dwb_ref, i == 0, upd)

    return pl.pallas_call(
        body, name=name, grid=(width // CONV_TC, nt),
        in_specs=[ANY, pl.BlockSpec((tt, CONV_TC), lambda j, i: (i, c0 + j)),
                  pl.BlockSpec((HALO, CONV_TC), lambda j, i: (jnp.maximum(i * (tt // HALO) - 1, 0), c0 + j)),
                  pl.BlockSpec((HALO, CONV_TC), lambda j, i: (after(i), c0 + j)),
                  pl.BlockSpec((4, CONV_TC), lambda j, i: (0, j)), pl.BlockSpec((1, CONV_TC), lambda j, i: (0, j)),
                  pl.BlockSpec((tt, CONV_TC), lambda j, i: (i, j)),
                  pl.BlockSpec((HALO, CONV_TC), lambda j, i: (after(i), j))],
        out_specs=[pl.BlockSpec((tt, CONV_TC), lambda j, i: (i, c0 + j)),
                   pl.BlockSpec((HALO, CONV_TC), lambda j, i: (0, j))],
        out_shape=[jax.ShapeDtypeStruct(dproj_main.shape, dproj_main.dtype), jax.ShapeDtypeStruct((HALO, width), F32)],
        input_output_aliases={0: 0},
        compiler_params=_params(("arbitrary", "arbitrary")),
    )(dproj_main, proj_main, proj_main, proj_main, w, b, dout, dout)


def _ssd_parts(xbc_ref):
    part = lambda o, w: [xbc_ref[:, g * SSD_GC + o:g * SSD_GC + o + w] for g in range(SSD_GROUPS)]
    return part(0, SSD_GW), part(SSD_GW, SSD_STATE), part(SSD_GW + SSD_STATE, SSD_STATE)


def _group_cols(ref):
    return [ref[:, g * SSD_GW:(g + 1) * SSD_GW] for g in range(SSD_GROUPS)]


def _gdn_parts(qkv_ref):
    part = lambda o: [qkv_ref[:, j * GDN_HC + o:j * GDN_HC + o + GDN_DK] for j in range(GDN_HB)]
    return part(0), part(GDN_DK), part(2 * GDN_DK)


def _head_cols(ref):
    return [ref[:, j * GDN_DV:(j + 1) * GDN_DV] for j in range(GDN_HB)]


def _first_head():
    return 0 if GDN_HB == GDN_HEADS else pl.program_id(1) * GDN_HB


def ssd_fwd(conv_ssd, proj_main, proj_small, normw, alog, dtb, dvec):
    t = conv_ssd.shape[0]
    nc = t // CHUNK

    groups = range(SSD_GROUPS)

    def body(xbc_ref, z_ref, sm_ref, nw_ref, al_ref, db_ref, dv_ref, y_ref, hist_ref, state_ref):
        @pl.when(pl.program_id(0) == 0)
        def _():
            state_ref[...] = jnp.zeros(state_ref.shape, F32)

        states = [state_ref[g] for g in groups]
        for g in groups:
            hist_ref[0, g] = states[g]
        ys, new_states = ssd_chunk(*_ssd_parts(xbc_ref), _group_cols(z_ref), sm_ref[...], _group_cols(nw_ref),
                                   al_ref[...], db_ref[...], dv_ref[...], states)
        for g in groups:
            y_ref[:, g * SSD_GW:(g + 1) * SSD_GW] = ys[g].astype(MXU_DTYPE)
            state_ref[g] = new_states[g]

    return pl.pallas_call(
        body, name="ssd_fwd", grid=(nc,),
        in_specs=[pl.BlockSpec((CHUNK, SSD_CONV), lambda c: (c, (COL_SSD - COL_CONV) // SSD_CONV)),
                  pl.BlockSpec((CHUNK, SSD_WIDTH), lambda c: (c, COL_Z // SSD_WIDTH)),
                  pl.BlockSpec((CHUNK, LANES), lambda c: (c, 0)),
                  _full((1, SSD_WIDTH)), _full((1, LANES)), _full((1, LANES)), _full((1, LANES))],
        out_specs=[pl.BlockSpec((CHUNK, SSD_WIDTH), lambda c: (c, 0)),
                   pl.BlockSpec((1, SSD_GROUPS, SSD_STATE, SSD_GW), lambda c: (c, 0, 0, 0))],
        out_shape=[jax.ShapeDtypeStruct((t, SSD_WIDTH), MXU_DTYPE),
                   jax.ShapeDtypeStruct((nc, SSD_GROUPS, SSD_STATE, SSD_GW), F32)],
        scratch_shapes=[pltpu.VMEM((SSD_GROUPS, SSD_STATE, SSD_GW), F32)],
        compiler_params=_params(("arbitrary",)),
    )(conv_ssd, proj_main, proj_small, normw, alog, dtb, dvec)


def _accumulate(ref, first, value):
    @pl.when(first)
    def _():
        ref[...] = value

    @pl.when(jnp.logical_not(first))
    def _():
        ref[...] += value


def ssd_bwd(conv_ssd, proj_main, proj_small, normw, alog, dtb, dvec, hist, dy):
    t = conv_ssd.shape[0]
    nc = t // CHUNK
    rev = lambda c: nc - 1 - c
    groups = range(SSD_GROUPS)

    def body(xbc_ref, z_ref, sm_ref, nw_ref, al_ref, db_ref, dv_ref, hist_ref, dy_ref,
             dxbc_ref, dz_ref, dsm_ref, dnw_ref, dal_ref, ddb_ref, ddv_ref, dstate_ref):
        first = pl.program_id(0) == 0

        @pl.when(first)
        def _():
            dstate_ref[...] = jnp.zeros(dstate_ref.shape, F32)

        _, vjp = jax.vjp(ssd_chunk, *_ssd_parts(xbc_ref), _group_cols(z_ref), sm_ref[...], _group_cols(nw_ref),
                         al_ref[...], db_ref[...], dv_ref[...], [hist_ref[0, g] for g in groups])
        dxs, dbm, dcm, dz, dsm, dnw, dal, ddb, ddv, dstate = vjp(
            (_group_cols(dy_ref), [dstate_ref[g] for g in groups]))
        for g in groups:
            base = g * SSD_GC
            dxbc_ref[:, base:base + SSD_GW] = dxs[g]
            dxbc_ref[:, base + SSD_GW:base + SSD_GW + SSD_STATE] = dbm[g]
            dxbc_ref[:, base + SSD_GW + SSD_STATE:base + SSD_GC] = dcm[g]
            dz_ref[:, g * SSD_GW:(g + 1) * SSD_GW] = dz[g].astype(dz_ref.dtype)
            dstate_ref[g] = dstate[g]
        dsm_ref[...] = dsm
        _accumulate(dnw_ref, first, join_lanes(dnw))
        _accumulate(dal_ref, first, dal)
        _accumulate(ddb_ref, first, ddb)
        _accumulate(ddv_ref, first, ddv)

    return pl.pallas_call(
        body, name="ssd_bwd", grid=(nc,),
        in_specs=[pl.BlockSpec((CHUNK, SSD_CONV), lambda c: (rev(c), (COL_SSD - COL_CONV) // SSD_CONV)),
                  pl.BlockSpec((CHUNK, SSD_WIDTH), lambda c: (rev(c), COL_Z // SSD_WIDTH)),
                  pl.BlockSpec((CHUNK, LANES), lambda c: (rev(c), 0)),
                  _full((1, SSD_WIDTH)), _full((1, LANES)), _full((1, LANES)), _full((1, LANES)),
                  pl.BlockSpec((1, SSD_GROUPS, SSD_STATE, SSD_GW), lambda c: (rev(c), 0, 0, 0)),
                  pl.BlockSpec((CHUNK, SSD_WIDTH), lambda c: (rev(c), 0))],
        out_specs=[pl.BlockSpec((CHUNK, SSD_CONV), lambda c: (rev(c), 0)),
                   pl.BlockSpec((CHUNK, SSD_WIDTH), lambda c: (rev(c), COL_Z // SSD_WIDTH)),
                   pl.BlockSpec((CHUNK, LANES), lambda c: (rev(c), 0)),
                   _full((1, SSD_WIDTH)), _full((1, LANES)), _full((1, LANES)), _full((1, LANES))],
        out_shape=[jax.ShapeDtypeStruct((t, SSD_CONV), F32), jax.ShapeDtypeStruct((t, MAIN), MXU_DTYPE),
                   jax.ShapeDtypeStruct((t, LANES), F32), jax.ShapeDtypeStruct((1, SSD_WIDTH), F32),
                   jax.ShapeDtypeStruct((1, LANES), F32), jax.ShapeDtypeStruct((1, LANES), F32),
                   jax.ShapeDtypeStruct((1, LANES), F32)],
        scratch_shapes=[pltpu.VMEM((SSD_GROUPS, SSD_STATE, SSD_GW), F32)],
        compiler_params=_params(("arbitrary",)),
    )(conv_ssd, proj_main, proj_small, normw, alog, dtb, dvec, hist, dy)


def gdn_fwd(conv_gdn, proj_main, proj_small, normw, alog, dtb):
    t = conv_gdn.shape[0]
    nc = t // CHUNK

    hb = GDN_HB
    gate_blk = COL_GATE // (GDN_DV * hb)

    def body(qkv_ref, gate_ref, sm_ref, nw_ref, al_ref, db_ref, y_ref, hist_ref, t_ref, state_ref):
        h0 = _first_head()

        @pl.when(pl.program_id(0) == 0)
        def _():
            for j in range(hb):
                state_ref[h0 + j] = jnp.zeros((GDN_DK, GDN_DV), F32)

        states = [state_ref[h0 + j] for j in range(hb)]
        for j in range(hb):
            hist_ref[0, j] = states[j]
        qs, ks, vs = _gdn_parts(qkv_ref)
        ys, new_states, ts = gdn_chunk(h0, qs, ks, vs, sm_ref[...], _head_cols(gate_ref), nw_ref[...], al_ref[...],
                                       db_ref[...], states)
        for j in range(hb):
            y_ref[:, j * GDN_DV:(j + 1) * GDN_DV] = ys[j].astype(MXU_DTYPE)
            state_ref[h0 + j] = new_states[j]
            t_ref[0, j] = ts[j]

    return pl.pallas_call(
        body, name="gdn_fwd", grid=(nc, GDN_HEADS // hb),
        in_specs=[pl.BlockSpec((CHUNK, GDN_HC * hb), lambda c, h: (c, h)),
                  pl.BlockSpec((CHUNK, GDN_DV * hb), lambda c, h: (c, gate_blk + h)),
                  pl.BlockSpec((CHUNK, LANES), lambda c, h: (c, 0)),
                  _full((1, GDN_DV)), _full((1, LANES)), _full((1, LANES))],
        out_specs=[pl.BlockSpec((CHUNK, GDN_DV * hb), lambda c, h: (c, h)),
                   pl.BlockSpec((1, hb, GDN_DK, GDN_DV), lambda c, h: (c, h, 0, 0)),
                   pl.BlockSpec((1, hb, CHUNK, CHUNK), lambda c, h: (c, h, 0, 0))],
        out_shape=[jax.ShapeDtypeStruct((t, GDN_W), MXU_DTYPE),
                   jax.ShapeDtypeStruct((nc, GDN_HEADS, GDN_DK, GDN_DV), F32),
                   jax.ShapeDtypeStruct((nc, GDN_HEADS, CHUNK, CHUNK), F32)],
        scratch_shapes=[pltpu.VMEM((GDN_HEADS, GDN_DK, GDN_DV), F32)],
        compiler_params=_params(("arbitrary", "arbitrary")),
    )(conv_gdn, proj_main, proj_small, normw, alog, dtb)


def gdn_bwd(dproj_main, conv_gdn, proj_main, proj_small, normw, alog, dtb, hist, t_inv, dy):
    t = conv_gdn.shape[0]
    nc = t // CHUNK
    rev = lambda c: nc - 1 - c
    hb = GDN_HB
    gate_blk = COL_GATE // (GDN_DV * hb)

    def body(alias_ref, qkv_ref, gate_ref, sm_ref, nw_ref, al_ref, db_ref, hist_ref, t_ref, dy_ref,
             dgate_ref, dqkv_ref, dsm_ref, dnw_ref, dal_ref, ddb_ref, dstate_ref):
        del alias_ref
        c, h = pl.program_id(0), pl.program_id(1)
        h0 = _first_head()

        @pl.when(c == 0)
        def _():
            for j in range(hb):
                dstate_ref[h0 + j] = jnp.zeros((GDN_DK, GDN_DV), F32)

        saved = [t_ref[0, j] for j in range(hb)]

        def fn(qs, ks, vs, small, gates, nw, al, db, states):
            return gdn_chunk(h0, qs, ks, vs, small, gates, nw, al, db, states, saved)[:2]

        qs, ks, vs = _gdn_parts(qkv_ref)
        _, vjp = jax.vjp(fn, qs, ks, vs, sm_ref[...], _head_cols(gate_ref), nw_ref[...], al_ref[...], db_ref[...],
                         [hist_ref[0, j] for j in range(hb)])
        dqs, dks, dvs, dsm, dgates, dnw, dal, ddb, dstates = vjp(
            (_head_cols(dy_ref), [dstate_ref[h0 + j] for j in range(hb)]))
        for j in range(hb):
            base = j * GDN_HC
            dqkv_ref[:, base:base + GDN_DK] = dqs[j]
            dqkv_ref[:, base + GDN_DK:base + 2 * GDN_DK] = dks[j]
            dqkv_ref[:, base + 2 * GDN_DK:base + GDN_HC] = dvs[j]
            dgate_ref[:, j * GDN_DV:(j + 1) * GDN_DV] = dgates[j].astype(dgate_ref.dtype)
            dstate_ref[h0 + j] = dstates[j]
        _accumulate(dsm_ref, h == 0, dsm)
        first = jnp.logical_and(c == 0, h == 0)
        _accumulate(dnw_ref, first, dnw)
        _accumulate(dal_ref, first, dal)
        _accumulate(ddb_ref, first, ddb)

    return pl.pallas_call(
        body, name="gdn_bwd", grid=(nc, GDN_HEADS // hb),
        in_specs=[ANY, pl.BlockSpec((CHUNK, GDN_HC * hb), lambda c, h: (rev(c), h)),
                  pl.BlockSpec((CHUNK, GDN_DV * hb), lambda c, h: (rev(c), gate_blk + h)),
                  pl.BlockSpec((CHUNK, LANES), lambda c, h: (rev(c), 0)),
                  _full((1, GDN_DV)), _full((1, LANES)), _full((1, LANES)),
                  pl.BlockSpec((1, hb, GDN_DK, GDN_DV), lambda c, h: (rev(c), h, 0, 0)),
                  pl.BlockSpec((1, hb, CHUNK, CHUNK), lambda c, h: (rev(c), h, 0, 0)),
                  pl.BlockSpec((CHUNK, GDN_DV * hb), lambda c, h: (rev(c), h))],
        out_specs=[pl.BlockSpec((CHUNK, GDN_DV * hb), lambda c, h: (rev(c), gate_blk + h)),
                   pl.BlockSpec((CHUNK, GDN_HC * hb), lambda c, h: (rev(c), h)),
                   pl.BlockSpec((CHUNK, LANES), lambda c, h: (rev(c), 0)),
                   _full((1, GDN_DV)), _full((1, LANES)), _full((1, LANES))],
        out_shape=[jax.ShapeDtypeStruct(dproj_main.shape, dproj_main.dtype), jax.ShapeDtypeStruct((t, GDN_CONV), F32),
                   jax.ShapeDtypeStruct((t, LANES), F32), jax.ShapeDtypeStruct((1, GDN_DV), F32),
                   jax.ShapeDtypeStruct((1, LANES), F32), jax.ShapeDtypeStruct((1, LANES), F32)],
        scratch_shapes=[pltpu.VMEM((GDN_HEADS, GDN_DK, GDN_DV), F32)],
        input_output_aliases={0: 0},
        compiler_params=_params(("arbitrary", "arbitrary")),
    )(dproj_main, conv_gdn, proj_main, proj_small, normw, alog, dtb, hist, t_inv, dy)


def out_proj_loss(x, y_ssd, y_gdn, w_out, final_w, target):
    t = x.shape[0]
    tm = min(256, t)

    def body(x_ref, ys_ref, yg_ref, wo_ref, fw_ref, tg_ref, loss_ref, dhid_ref, dys_ref, dyg_ref, dwo_ref, dfw_ref):
        i = pl.program_id(0)
        ys, yg = ys_ref[...], yg_ref[...]
        wo_s, wo_g = wo_ref[:SSD_WIDTH, :], wo_ref[SSD_WIDTH:, :]
        hid = x_ref[...] + _raw_dot(ys, wo_s, 1, 0) + _raw_dot(yg, wo_g, 1, 0)
        out, vjp = jax.vjp(rmsnorm, hid, fw_ref[...])
        err = out - tg_ref[...]
        loss = 0.5 * jnp.sum(jnp.mean(err * err, axis=-1, keepdims=True), axis=0, keepdims=True)
        dhid, dfw = vjp(err * (1.0 / D_MODEL))
        dhid_ref[...] = dhid
        dys_ref[...] = _raw_dot(dhid, wo_s, 1, 1)
        dyg_ref[...] = _raw_dot(dhid, wo_g, 1, 1)
        first = i == 0
        _accumulate(loss_ref, first, jnp.broadcast_to(loss, loss_ref.shape))
        _accumulate(dfw_ref, first, dfw)

        @pl.when(first)
        def _():
            dwo_ref[:SSD_WIDTH, :] = _raw_dot(ys, dhid, 0, 0)
            dwo_ref[SSD_WIDTH:, :] = _raw_dot(yg, dhid, 0, 0)

        @pl.when(i > 0)
        def _():
            dwo_ref[:SSD_WIDTH, :] += _raw_dot(ys, dhid, 0, 0)
            dwo_ref[SSD_WIDTH:, :] += _raw_dot(yg, dhid, 0, 0)

    row = lambda w: pl.BlockSpec((tm, w), lambda i: (i, 0))
    return pl.pallas_call(
        body, name="out_proj_loss", grid=(t // tm,),
        in_specs=[row(D_MODEL), row(SSD_WIDTH), row(GDN_W), _full((SSD_WIDTH + GDN_W, D_MODEL)), _full((1, D_MODEL)),
                  row(D_MODEL)],
        out_specs=[_full((8, LANES)), row(D_MODEL), row(SSD_WIDTH), row(GDN_W), _full((SSD_WIDTH + GDN_W, D_MODEL)),
                   _full((1, D_MODEL))],
        out_shape=[jax.ShapeDtypeStruct((8, LANES), F32), jax.ShapeDtypeStruct((t, D_MODEL), F32),
                   jax.ShapeDtypeStruct((t, SSD_WIDTH), F32), jax.ShapeDtypeStruct((t, GDN_W), F32),
                   jax.ShapeDtypeStruct((SSD_WIDTH + GDN_W, D_MODEL), F32), jax.ShapeDtypeStruct((1, D_MODEL), F32)],
        compiler_params=_params(("arbitrary",)),
    )(x, y_ssd, y_gdn, w_out, final_w, target)


def in_proj_bwd_x(x, normw, w_main, w_small, dproj_main, dsmall_a, dsmall_b, dhid, slabbed):
    t = x.shape[0]
    tm = min(256, t)
    ni = t // tm
    ns = len(slabbed)

    def body(x_ref, nw_ref, wm_ref, ws_ref, dp_ref, da_ref, db_ref, dh_ref, *rest):
        slab_refs, (gx_ref, dnw_ref), land_refs = rest[:ns], rest[ns:ns + 2], rest[ns + 2:2 * ns + 2]
        sems = rest[2 * ns + 2:]
        i = pl.program_id(0)
        start, finish = _slab_exchange(slab_refs, land_refs, ns, *sems)

        @pl.when(i == 0)
        def _():
            start()

        du = _raw_dot(dp_ref[...], wm_ref[...], 1, 1) + _raw_dot(da_ref[...] + db_ref[...], ws_ref[...], 1, 1)
        _, vjp = jax.vjp(rmsnorm, x_ref[...], nw_ref[...])
        dx, dnw = vjp(du)
        gx_ref[...] = dx + dh_ref[...]
        _accumulate(dnw_ref, i == 0, dnw)

        @pl.when(i == ni - 1)
        def _():
            finish()

    row = lambda w: pl.BlockSpec((tm, w), lambda i: (i, 0))
    out = pl.pallas_call(
        body, name="in_proj_bwd_x", grid=(ni,),
        in_specs=[row(D_MODEL), _full((1, D_MODEL)), _full((D_MODEL, MAIN)), _full((D_MODEL, LANES)), row(MAIN),
                  row(LANES), row(LANES), row(D_MODEL)] + [HBM] * ns,
        out_specs=[row(D_MODEL), _full((1, D_MODEL))] + [HBM] * ns,
        out_shape=[jax.ShapeDtypeStruct((t, D_MODEL), F32), jax.ShapeDtypeStruct((1, D_MODEL), F32)]
        + _slab_exchange_shapes(slabbed, []),
        scratch_shapes=_slab_exchange_sems(ns),
        compiler_params=_params(("arbitrary",)),
    )(x, normw, w_main, w_small, dproj_main, dsmall_a, dsmall_b, dhid, *slabbed)
    return out[0], out[1], out[2:]


def in_proj_bwd_w(u, dproj_main, dsmall_a, dsmall_b, slabbed):
    t = u.shape[0]
    tm, tn = min(1024, t), MAIN // 4
    nj, ni = MAIN // tn, t // tm
    ns = len(slabbed)

    def body(u_ref, dp_ref, da_ref, db_ref, *rest):
        slab_refs, (dwm_ref, dws_ref), land_refs, sems = rest[:ns], rest[ns:ns + 2], rest[ns + 2:2 * ns + 2], rest[2 * ns + 2:]
        j, i = pl.program_id(0), pl.program_id(1)
        start, finish = _slab_exchange(slab_refs, land_refs, ns, *sems)

        @pl.when(jnp.logical_and(j == 0, i == 0))
        def _():
            start()

        uu = u_ref[...]
        _accumulate(dwm_ref, i == 0, _raw_dot(uu, dp_ref[...], 0, 0))

        @pl.when(j == 0)
        def _():
            _accumulate(dws_ref, i == 0, _raw_dot(uu, da_ref[...] + db_ref[...], 0, 0))

        @pl.when(jnp.logical_and(j == nj - 1, i == ni - 1))
        def _():
            finish()

    out = pl.pallas_call(
        body, name="in_proj_bwd_w", grid=(nj, ni),
        in_specs=[pl.BlockSpec((tm, D_MODEL), lambda j, i: (i, 0)), pl.BlockSpec((tm, tn), lambda j, i: (i, j)),
                  pl.BlockSpec((tm, LANES), lambda j, i: (i, 0)), pl.BlockSpec((tm, LANES), lambda j, i: (i, 0))]
        + [HBM] * ns,
        out_specs=[pl.BlockSpec((D_MODEL, tn), lambda j, i: (0, j)), _full((D_MODEL, LANES))] + [HBM] * ns,
        out_shape=[jax.ShapeDtypeStruct((D_MODEL, MAIN), F32), jax.ShapeDtypeStruct((D_MODEL, LANES), F32)]
        + _slab_exchange_shapes(slabbed, []),
        scratch_shapes=_slab_exchange_sems(ns),
        compiler_params=_params(("arbitrary", "arbitrary")),
    )(u, dproj_main, dsmall_a, dsmall_b, *slabbed)
    return out[0], out[1], out[2:]


def sum_slabs(a, name):
    n, rows, cols = a.shape
    tr = 64 if rows % 64 == 0 else rows

    def body(a_ref, o_ref):
        acc = a_ref[0].astype(F32)
        for d in range(1, n):
            acc = acc + a_ref[d].astype(F32)
        o_ref[...] = acc

    return pl.pallas_call(
        body, name=name, grid=(rows // tr,),
        in_specs=[pl.BlockSpec((n, tr, cols), lambda i: (0, i, 0))],
        out_specs=pl.BlockSpec((tr, cols), lambda i: (i, 0)),
        out_shape=jax.ShapeDtypeStruct((rows, cols), F32),
        compiler_params=_params(("arbitrary",)),
    )(a)


def adamw(w, g, m, v, name):
    rows, cols = w.shape
    tr = 128 if rows % 128 == 0 else rows

    def body(w_ref, g_ref, m_ref, v_ref, d_ref, nm_ref, nv_ref):
        gg = g_ref[...]
        nm = ADAM_B1 * m_ref[...] + (1.0 - ADAM_B1) * gg
        nv = ADAM_B2 * v_ref[...] + (1.0 - ADAM_B2) * (gg * gg)
        m_hat = nm / (1.0 - ADAM_B1 ** ADAM_STEP)
        v_hat = nv / (1.0 - ADAM_B2 ** ADAM_STEP)
        d_ref[...] = -ADAM_LR * (m_hat / (jnp.sqrt(v_hat) + ADAM_EPS) + ADAM_WD * w_ref[...])
        nm_ref[...] = nm
        nv_ref[...] = nv

    spec = pl.BlockSpec((tr, cols), lambda i: (i, 0))
    shp = jax.ShapeDtypeStruct((rows, cols), F32)
    return pl.pallas_call(
        body, name=name, grid=(rows // tr,), in_specs=[spec] * 4, out_specs=[spec] * 3, out_shape=[shp] * 3,
        compiler_params=_params(("arbitrary",)),
    )(w, g, m, v)


def _my_place():
    return lax.axis_index("x"), lax.axis_index("y"), lax.axis_index("c")


def gather_weights(big, small):
    nb, n = len(big), len(big) + len(small)
    parts = 4

    def body(*refs):
        srcs, outs = refs[:n], refs[n:2 * n]
        land_a, land_b = refs[2 * n:2 * n + nb], refs[2 * n + nb:2 * n + 2 * nb]
        send_sems, recv_sems, fwd_send, fwd_recv, local_sems = refs[2 * n + 2 * nb:]
        x, y, c = _my_place()
        me = 2 * x + y
        chips = [(1 - x, y), (x, 1 - y), (1 - x, 1 - y)]
        half = [a.shape[0] // 2 for a in big]

        def ici(j, i):
            px, py = chips[j]
            if i < nb:
                src, dst = srcs[i].at[pl.ds(c * half[i], half[i])], land_a[i].at[j]
            else:
                src, dst = srcs[i], outs[i].at[me]
            return pltpu.make_async_remote_copy(src_ref=src, dst_ref=dst, send_sem=send_sems.at[j * n + i],
                                                recv_sem=recv_sems.at[j * n + i], device_id=(px, py, c),
                                                device_id_type=MESH)

        def ici_arrival(j, i):
            px, py = chips[j]
            dst = land_a[i].at[j] if i < nb else outs[i].at[2 * px + py]
            return pltpu.make_async_remote_copy(src_ref=dst, dst_ref=dst, send_sem=send_sems.at[j * n + i],
                                                recv_sem=recv_sems.at[j * n + i], device_id=(px, py, c),
                                                device_id_type=MESH)

        def forward(j, i, p):
            rows = half[i] // parts
            k = (j * nb + i) * parts + p
            return pltpu.make_async_remote_copy(
                src_ref=land_a[i].at[j, pl.ds(p * rows, rows)], dst_ref=land_b[i].at[j, pl.ds(p * rows, rows)],
                send_sem=fwd_send.at[k], recv_sem=fwd_recv.at[k], device_id=(x, y, 1 - c), device_id_type=MESH)

        def store(j, i, from_sibling):
            px, py = chips[j]
            buf, h = (land_b, 1 - c) if from_sibling else (land_a, c)
            k = n + (j * nb + i) * 2 + (1 if from_sibling else 0)
            return pltpu.make_async_copy(buf[i].at[j], outs[i].at[2 * px + py, pl.ds(h * half[i], half[i])],
                                         local_sems.at[k])

        own = [pltpu.make_async_copy(srcs[i], outs[i].at[me], local_sems.at[i]) for i in range(n)]
        sends = [ici(j, i) for j in range(3) for i in range(n)]
        for cp in own + sends:
            cp.start()
        pending = []
        for j in range(3):
            for i in range(n):
                ici_arrival(j, i).wait_recv()
                if i < nb:
                    fw = [forward(j, i, p) for p in range(parts)]
                    st = store(j, i, False)
                    for cp in fw + [st]:
                        cp.start()
                    pending += [cp.wait_send for cp in fw] + [st.wait]
        for j in range(3):
            for i in range(nb):
                for p in range(parts):
                    forward(j, i, p).wait_recv()
                st = store(j, i, True)
                st.start()
                pending.append(st.wait)
        for cp in sends:
            cp.wait_send()
        for wait in pending:
            wait()
        for cp in own:
            cp.wait()

    shards = list(big) + list(small)
    lands = [pltpu.VMEM((3, a.shape[0] // 2) + a.shape[1:], a.dtype) for a in big]
    return pl.pallas_call(
        body, name="gather_weights",
        in_specs=[HBM] * n, out_specs=[HBM] * n,
        out_shape=[jax.ShapeDtypeStruct((N_CHIP,) + s.shape, s.dtype) for s in shards],
        scratch_shapes=lands + lands + [
            pltpu.SemaphoreType.DMA((3 * n,)), pltpu.SemaphoreType.DMA((3 * n,)),
            pltpu.SemaphoreType.DMA((3 * nb * parts,)), pltpu.SemaphoreType.DMA((3 * nb * parts,)),
            pltpu.SemaphoreType.DMA((n + 6 * nb,))],
        compiler_params=pltpu.CompilerParams(vmem_limit_bytes=VMEM_LIMIT),
    )(*shards)


def _peer(x, y, c, mask):
    mx, my, mc = (mask >> 2) & 1, (mask >> 1) & 1, mask & 1
    return (x ^ mx if mx else x, y ^ my if my else y, c ^ mc if mc else c)


def _slab_exchange_shapes(slabbed, replicated):
    return ([jax.ShapeDtypeStruct(a.shape, a.dtype) for a in slabbed]
            + [jax.ShapeDtypeStruct((N_DEV,) + a.shape, a.dtype) for a in replicated])


def _slab_exchange_sems(n):
    return [pltpu.SemaphoreType.DMA((7 * n,)), pltpu.SemaphoreType.DMA((7 * n,)), pltpu.SemaphoreType.DMA((n,))]


def _slab_exchange(srcs, outs, ns, send_sems, recv_sems, local_sems):
    n = len(srcs)
    x, y, c = _my_place()
    me = 4 * x + 2 * y + c

    def piece(i, dev):
        return srcs[i].at[dev] if i < ns else srcs[i]

    def copies(arriving):
        out = []
        for mask in range(1, N_DEV):
            px, py, pc = _peer(x, y, c, mask)
            dev = 4 * px + 2 * py + pc
            for i in range(n):
                k = (mask - 1) * n + i
                out.append(pltpu.make_async_remote_copy(
                    src_ref=piece(i, dev), dst_ref=outs[i].at[dev if arriving else me], send_sem=send_sems.at[k],
                    recv_sem=recv_sems.at[k], device_id=(px, py, pc), device_id_type=MESH))
        return out

    def local():
        return [pltpu.make_async_copy(piece(i, me), outs[i].at[me], local_sems.at[i]) for i in range(n)]

    def start():
        for cp in local() + copies(False):
            cp.start()

    def finish():
        for cp in copies(True):
            cp.wait_recv()
        for cp in copies(False):
            cp.wait_send()
        for cp in local():
            cp.wait()

    return start, finish


def exchange_halves(halves, replicated):
    n, nr = len(halves), len(replicated)
    streams = 8

    def body(*refs):
        srcs, rep_srcs, outs, rep_outs = refs[:n], refs[n:n + nr], refs[n + nr:2 * n + nr], refs[2 * n + nr:2 * (n + nr)]
        refs = refs[2 * (n + nr):]
        mine, theirs = refs[:n], refs[n:2 * n]
        send_sems, recv_sems, in_sems, out_sems = refs[2 * n:2 * n + 4]
        rep_start, rep_finish = _slab_exchange(rep_srcs, rep_outs, 0, *refs[2 * n + 4:])
        rep_start()
        x, y, c = _my_place()
        loads = [pltpu.make_async_copy(srcs[i], mine[i], in_sems.at[i]) for i in range(n)]
        for cp in loads:
            cp.start()
        for cp in loads:
            cp.wait()

        def chunk_copy(i, s):
            rows = halves[i].shape[0] // streams
            k = i * streams + s
            return pltpu.make_async_remote_copy(
                src_ref=mine[i].at[pl.ds(s * rows, rows)], dst_ref=theirs[i].at[pl.ds(s * rows, rows)],
                send_sem=send_sems.at[k], recv_sem=recv_sems.at[k], device_id=(x, y, 1 - c), device_id_type=MESH)

        sends = [chunk_copy(i, s) for i in range(n) for s in range(streams)]
        for cp in sends:
            cp.start()
        own = [pltpu.make_async_copy(mine[i], outs[i].at[c], out_sems.at[i]) for i in range(n)]
        for cp in own:
            cp.start()
        for cp in sends:
            cp.wait_recv()
        got = [pltpu.make_async_copy(theirs[i], outs[i].at[1 - c], out_sems.at[n + i]) for i in range(n)]
        for cp in got:
            cp.start()
        for cp in sends:
            cp.wait_send()
        for cp in own + got:
            cp.wait()
        rep_finish()

    vmem = [pltpu.VMEM(a.shape, a.dtype) for a in halves]
    out = pl.pallas_call(
        body, name="exchange_halves",
        in_specs=[HBM] * (n + nr), out_specs=[HBM] * (n + nr),
        out_shape=[jax.ShapeDtypeStruct((2,) + a.shape, a.dtype) for a in halves]
        + _slab_exchange_shapes([], replicated),
        scratch_shapes=vmem + vmem + [pltpu.SemaphoreType.DMA((n * streams,)), pltpu.SemaphoreType.DMA((n * streams,)),
                                      pltpu.SemaphoreType.DMA((n,)), pltpu.SemaphoreType.DMA((2 * n,))]
        + _slab_exchange_sems(nr),
        compiler_params=pltpu.CompilerParams(vmem_limit_bytes=VMEM_LIMIT),
    )(*halves, *replicated)
    return out[:n], out[n:]


def _pack_cols(pieces):
    offs, pos = [], 0
    for a in pieces:
        offs.append(pos)
        pos += a.shape[1]
    rows8 = [jnp.pad(a.astype(F32), ((0, 8 - a.shape[0]), (0, 0))) for a in pieces]
    return jnp.concatenate(rows8, axis=1), offs


def adamw_many(ws, gs, ms, vs):
    n = len(ws)

    def body(*refs):
        w_r, g_r, m_r, v_r = refs[:n], refs[n:2 * n], refs[2 * n:3 * n], refs[3 * n:4 * n]
        d_o, m_o, v_o = refs[4 * n:5 * n], refs[5 * n:6 * n], refs[6 * n:7 * n]
        for i in range(n):
            gg = g_r[i][...]
            nm = ADAM_B1 * m_r[i][...] + (1.0 - ADAM_B1) * gg
            nv = ADAM_B2 * v_r[i][...] + (1.0 - ADAM_B2) * (gg * gg)
            m_hat = nm / (1.0 - ADAM_B1 ** ADAM_STEP)
            v_hat = nv / (1.0 - ADAM_B2 ** ADAM_STEP)
            d_o[i][...] = -ADAM_LR * (m_hat / (jnp.sqrt(v_hat) + ADAM_EPS) + ADAM_WD * w_r[i][...])
            m_o[i][...] = nm
            v_o[i][...] = nv

    shapes = [jax.ShapeDtypeStruct(w.shape, F32) for w in ws]
    out = pl.pallas_call(body, name="adamw_small", out_shape=shapes * 3,
                         compiler_params=pltpu.CompilerParams(vmem_limit_bytes=VMEM_LIMIT))(*ws, *gs, *ms, *vs)
    return out[:n], out[n:2 * n], out[2 * n:]


def _lanes(vec, start):
    n = vec.shape[-1]
    return jnp.pad(vec.reshape(1, n).astype(F32), ((0, 0), (start, LANES - start - n)))


def kernel(x, norm_w, w_in, ssd_conv_w, ssd_conv_b, ssd_dt_bias, ssd_a_log, ssd_d, ssd_norm_w, gdn_conv_w, gdn_dt_bias, gdn_a_log, gdn_norm_w, w_out, final_norm_w, loss_target, m_norm_w, m_w_in, m_ssd_conv_w, m_ssd_conv_b, m_ssd_dt_bias, m_ssd_a_log, m_ssd_d, m_ssd_norm_w, m_gdn_conv_w, m_gdn_dt_bias, m_gdn_a_log, m_gdn_norm_w, m_w_out, m_final_norm_w, v_norm_w, v_w_in, v_ssd_conv_w, v_ssd_conv_b, v_ssd_dt_bias, v_ssd_a_log, v_ssd_d, v_ssd_norm_w, v_gdn_conv_w, v_gdn_dt_bias, v_gdn_a_log, v_gdn_norm_w, v_w_out, v_final_norm_w):
    xs = x[0]
    target = loss_target[0]
    chip = 2 * lax.axis_index("x") + lax.axis_index("y")
    w_in_shard, w_out_shard = w_in[0], w_out[0]
    in_cols = w_in_shard.shape[1]
    out_rows = w_out_shard.shape[0]

    g_in, g_out, g_cs, g_cg = gather_weights(
        [w_in_shard.astype(MXU_DTYPE), w_out_shard.astype(MXU_DTYPE)], [ssd_conv_w[0], gdn_conv_w[0]])
    w_in_full = jnp.concatenate([g_in[k] for k in range(N_CHIP)], axis=1)
    w_out_full = g_out.reshape(N_CHIP * out_rows, D_MODEL)
    cw_ssd = _ssd_perm(jnp.transpose(g_cs, (1, 0, 2)).reshape(4, SSD_CONV))
    cw_gdn = _gdn_perm(jnp.transpose(g_cg, (1, 0, 2)).reshape(4, GDN_CONV))
    cb_ssd = _ssd_perm(ssd_conv_b)
    cb_gdn = jnp.zeros((1, GDN_CONV), F32)
    o_xbc, o_dt, o_gate, o_qkv, o_ab = 1024, 2560, 2576, 3600, 6672
    w_main = jnp.concatenate([w_in_full[:, :o_xbc], w_in_full[:, o_gate:o_qkv], _gdn_perm(w_in_full[:, o_qkv:o_ab]),
                              _ssd_perm(w_in_full[:, o_xbc:o_dt])], axis=1)
    w_small = jnp.concatenate([w_in_full[:, o_dt:o_gate], w_in_full[:, o_ab:],
                               jnp.zeros((D_MODEL, LANES - 32), MXU_DTYPE)], axis=1)
    alog = _lanes(ssd_a_log, 0) + _lanes(gdn_a_log, LANE_GA)
    dtb = _lanes(ssd_dt_bias, 0) + _lanes(gdn_dt_bias, LANE_GA)
    dvec = _lanes(ssd_d, 0)
    fw = final_norm_w.reshape(1, D_MODEL)

    proj_main, proj_small, u = in_proj(xs, norm_w, w_main, w_small)
    proj_main, conv_out = in_proj_conv(proj_main, u, w_main, jnp.concatenate([cw_gdn, cw_ssd], axis=1),
                                       jnp.concatenate([cb_gdn, cb_ssd], axis=1))
    conv_ssd = conv_gdn = conv_out
    y_ssd, hist_ssd = ssd_fwd(conv_ssd, proj_main, proj_small, ssd_norm_w, alog, dtb, dvec)
    y_gdn, hist_gdn, tinv_gdn = gdn_fwd(conv_gdn, proj_main, proj_small, gdn_norm_w, alog, dtb)

    loss_blk, dhid, dy_ssd, dy_gdn, d_w_out, d_fw = out_proj_loss(xs, y_ssd, y_gdn, w_out_full, fw, target)
    dconv_ssd, dproj_main, dsmall_ssd, d_ssd_nw, d_alog_s, d_dtb_s, d_dvec = ssd_bwd(
        conv_ssd, proj_main, proj_small, ssd_norm_w, alog, dtb, dvec, hist_ssd, dy_ssd)
    dproj_main, dconv_gdn, dsmall_gdn, d_gdn_nw, d_alog_g, d_dtb_g = gdn_bwd(
        dproj_main, conv_gdn, proj_main, proj_small, gdn_norm_w, alog, dtb, hist_gdn, tinv_gdn, dy_gdn)
    dproj_main, dwb_ssd = conv_bwd(dproj_main, proj_main, COL_SSD, SSD_CONV, cw_ssd, cb_ssd, dconv_ssd, "conv_bwd_ssd")
    dproj_main, dwb_gdn = conv_bwd(dproj_main, proj_main, COL_GDN, GDN_CONV, cw_gdn, cb_gdn, dconv_gdn, "conv_bwd_gdn")
    slabs_out = d_w_out.reshape(N_DEV, out_rows // 2, D_MODEL).astype(COMM_DTYPE)
    d_w_main, d_w_small, (r_out,) = in_proj_bwd_w(u, dproj_main, dsmall_ssd, dsmall_gdn, [slabs_out])
    d_w_in = jnp.concatenate([d_w_main[:, :COL_GATE], _ssd_unperm(d_w_main[:, COL_SSD:]), d_w_small[:, 0:16],
                              d_w_main[:, COL_GATE:COL_GDN], _gdn_unperm(d_w_main[:, COL_GDN:COL_SSD]),
                              d_w_small[:, 16:32]], axis=1)
    d_w_in = jnp.stack([d_w_in[:, k * in_cols:(k + 1) * in_cols] for k in range(N_CHIP)])
    slabs_in = d_w_in.reshape(N_DEV, D_MODEL // 2, in_cols).astype(COMM_DTYPE)
    grad_x, d_norm_w, (r_in,) = in_proj_bwd_x(xs, norm_w, w_main, w_small, dproj_main, dsmall_ssd, dsmall_gdn,
                                               dhid, [slabs_in])
    d_alog, d_dtb = d_alog_s + d_alog_g, d_dtb_s + d_dtb_g
    packed, (o_nw, o_cs, o_cg, o_snw, o_fw, o_al, o_db, o_dv, o_gnw, o_loss) = _pack_cols([
        d_norm_w, _ssd_unperm(dwb_ssd), _gdn_unperm(dwb_gdn),
        d_ssd_nw.reshape(1, SSD_WIDTH), d_fw, d_alog, d_dtb, d_dvec, d_gdn_nw, loss_blk])

    half_in = sum_slabs(r_in, "sum_w_in")
    half_out = sum_slabs(r_out, "sum_w_out")
    (full_in, full_out), (r_small,) = exchange_halves([half_in, half_out], [packed])
    tot = sum_slabs(r_small, "sum_small")
    grad_w_in = full_in.reshape(D_MODEL, in_cols)
    grad_w_out = full_out.reshape(out_rows, D_MODEL)
    loss = tot[0, o_loss]
    sc, gc = ssd_conv_w.shape[2], gdn_conv_w.shape[2]
    row = lambda off, n, r=0: tot[r:r + 1, off:off + n]
    gs = [row(o_nw, D_MODEL),
          lax.dynamic_slice(tot, (0, o_cs + chip * sc), (4, sc)),
          row(o_cs, SSD_CONV, 4),
          row(o_db, SSD_HEADS), row(o_al, SSD_HEADS), row(o_dv, SSD_HEADS),
          row(o_snw, SSD_WIDTH),
          lax.dynamic_slice(tot, (0, o_cg + chip * gc), (4, gc)),
          row(o_db + LANE_GA, GDN_HEADS), row(o_al + LANE_GA, GDN_HEADS),
          row(o_gnw, GDN_DV), row(o_fw, D_MODEL)]

    names = ["norm_w", "ssd_conv_w", "ssd_conv_b", "ssd_dt_bias", "ssd_a_log", "ssd_d", "ssd_norm_w", "gdn_conv_w",
             "gdn_dt_bias", "gdn_a_log", "gdn_norm_w", "final_norm_w"]
    ws = [norm_w, ssd_conv_w, ssd_conv_b, ssd_dt_bias, ssd_a_log, ssd_d, ssd_norm_w, gdn_conv_w, gdn_dt_bias,
          gdn_a_log, gdn_norm_w, final_norm_w]
    ms = [m_norm_w, m_ssd_conv_w, m_ssd_conv_b, m_ssd_dt_bias, m_ssd_a_log, m_ssd_d, m_ssd_norm_w, m_gdn_conv_w,
          m_gdn_dt_bias, m_gdn_a_log, m_gdn_norm_w, m_final_norm_w]
    vs = [v_norm_w, v_ssd_conv_w, v_ssd_conv_b, v_ssd_dt_bias, v_ssd_a_log, v_ssd_d, v_ssd_norm_w, v_gdn_conv_w,
          v_gdn_dt_bias, v_gdn_a_log, v_gdn_norm_w, v_final_norm_w]
    shapes = [w.shape for w in ws]
    flat = lambda arrs: [a.reshape(g.shape) for a, g in zip(arrs, gs)]
    d_s, m_s, v_s = adamw_many(flat(ws), gs, flat(ms), flat(vs))
    back = lambda arrs: dict(zip(names, [a.reshape(s) for a, s in zip(arrs, shapes)]))
    delta, new_m, new_v, grads = back(d_s), back(m_s), back(v_s), back(gs)
    d_in, m_in, v_in = adamw(w_in_shard, grad_w_in, m_w_in[0], v_w_in[0], "adamw_w_in")
    d_out, m_out, v_out = adamw(w_out_shard, grad_w_out, m_w_out[0], v_w_out[0], "adamw_w_out")
    for tbl, a_in, a_out in ((grads, grad_w_in, grad_w_out), (delta, d_in, d_out), (new_m, m_in, m_out),
                             (new_v, v_in, v_out)):
        tbl["w_in"] = a_in[None]
        tbl["w_out"] = a_out[None]

    order = ["norm_w", "w_in", "ssd_conv_w", "ssd_conv_b", "ssd_dt_bias", "ssd_a_log", "ssd_d", "ssd_norm_w",
             "gdn_conv_w", "gdn_dt_bias", "gdn_a_log", "gdn_norm_w", "w_out", "final_norm_w"]
    return (loss.reshape(()), grad_x[None], *[grads[k] for k in order], *[delta[k] for k in order],
            *[new_m[k] for k in order], *[new_v[k] for k in order])
```

```python
import functools

import jax
import jax.numpy as jnp
from jax import lax
from jax.experimental import pallas as pl
from jax.experimental.pallas import tpu as pltpu

F32 = jnp.float32
MXU_DTYPE = jnp.bfloat16
COMM_DTYPE = jnp.bfloat16
MESH = pl.DeviceIdType.MESH

D_MODEL = 1024
CHUNK = 64
EPS = 1e-6
SSD_HEADS, SSD_GROUPS, SSD_STATE = 16, 2, 128
SSD_WIDTH, SSD_CONV = 1024, 1536
SSD_GW = SSD_WIDTH // SSD_GROUPS
SSD_GC = SSD_GW + 2 * SSD_STATE
GDN_HEADS, GDN_DK, GDN_DV = 8, 128, 128
GDN_W, GDN_CONV = 1024, 3072
GDN_HC = 2 * GDN_DK + GDN_DV
IN_DIM = 6688
MAIN = 6656
LANES = 128
COL_Z, COL_GATE, COL_GDN, COL_SSD = 0, 1024, 2048, 5120
COL_CONV = COL_GDN
GDN_HB = 8
LANE_GA, LANE_GB = 16, 24
N_DEV, N_CHIP = 8, 4
VMEM_LIMIT = 52 * 1024 * 1024

ADAM_LR, ADAM_B1, ADAM_B2, ADAM_EPS, ADAM_WD, ADAM_STEP = 0.001, 0.9, 0.999, 1e-08, 0.01, 10


def _ssd_perm(a):
    lead, nb = a.shape[:-1], SSD_GROUPS * SSD_STATE
    x = a[..., :SSD_WIDTH].reshape(*lead, SSD_GROUPS, SSD_GW)
    b = a[..., SSD_WIDTH:SSD_WIDTH + nb].reshape(*lead, SSD_GROUPS, SSD_STATE)
    c = a[..., SSD_WIDTH + nb:].reshape(*lead, SSD_GROUPS, SSD_STATE)
    return jnp.concatenate([x, b, c], axis=-1).reshape(*lead, SSD_CONV)


def _ssd_unperm(a):
    lead = a.shape[:-1]
    g = a.reshape(*lead, SSD_GROUPS, SSD_GC)
    parts = [g[..., :SSD_GW], g[..., SSD_GW:SSD_GW + SSD_STATE], g[..., SSD_GW + SSD_STATE:]]
    return jnp.concatenate([p.reshape(*lead, -1) for p in parts], axis=-1)


def _gdn_perm(a):
    lead = a.shape[:-1]
    return jnp.swapaxes(a.reshape(*lead, 3, GDN_HEADS, GDN_DK), -3, -2).reshape(*lead, GDN_CONV)


def _gdn_unperm(a):
    lead = a.shape[:-1]
    return jnp.swapaxes(a.reshape(*lead, GDN_HEADS, 3, GDN_DK), -3, -2).reshape(*lead, GDN_CONV)


def _split(a, n):
    parts, rest = [], a.astype(F32)
    for i in range(n):
        p = rest.astype(MXU_DTYPE)
        parts.append(p)
        if i < n - 1:
            rest = rest - p.astype(F32)
    return parts


def _raw_dot(a, b, ca, cb, mode="bf16"):
    d = lambda u, v: lax.dot_general(u, v, (((ca,), (cb,)), ((), ())), preferred_element_type=F32)
    if mode == "bf16":
        return d(a.astype(MXU_DTYPE), b.astype(MXU_DTYPE))
    if mode == "x3":
        (ah, al), (bh, bl) = _split(a, 2), _split(b, 2)
        return d(ah, bh) + (d(ah, bl) + d(al, bh))
    if mode == "sel_a":
        a0 = a.astype(MXU_DTYPE)
        b1, b2, b3 = _split(b, 3)
        return d(a0, b1) + (d(a0, b2) + d(a0, b3))
    assert mode == "sel_b", mode
    b0 = b.astype(MXU_DTYPE)
    a1, a2, a3 = _split(a, 3)
    return d(a1, b0) + (d(a2, b0) + d(a3, b0))


@functools.partial(jax.custom_vjp, nondiff_argnums=(2,))
def mm_nn(a, b, mode="bf16"):
    return _raw_dot(a, b, 1, 0, mode)


@functools.partial(jax.custom_vjp, nondiff_argnums=(2,))
def mm_nt(a, b, mode="bf16"):
    return _raw_dot(a, b, 1, 1, mode)


@functools.partial(jax.custom_vjp, nondiff_argnums=(2,))
def mm_tn(a, b, mode="bf16"):
    return _raw_dot(a, b, 0, 0, mode)


_SAME = {"bf16": ("bf16", "bf16"), "x3": ("x3", "x3")}
_NN_BWD = dict(_SAME, sel_a=("bf16", "sel_a"), sel_b=("sel_b", "bf16"))
_NT_BWD = dict(_SAME, sel_a=("bf16", "sel_b"), sel_b=("sel_b", "bf16"))
_TN_BWD = dict(_SAME, sel_a=("bf16", "sel_a"), sel_b=("sel_a", "bf16"))
mm_nn.defvjp(lambda a, b, m: (_raw_dot(a, b, 1, 0, m), (a, b)),
             lambda m, r, g: (mm_nt(g, r[1], _NN_BWD[m][0]), mm_tn(r[0], g, _NN_BWD[m][1])))
mm_nt.defvjp(lambda a, b, m: (_raw_dot(a, b, 1, 1, m), (a, b)),
             lambda m, r, g: (mm_nn(g, r[1], _NT_BWD[m][0]), mm_tn(g, r[0], _NT_BWD[m][1])))
mm_tn.defvjp(lambda a, b, m: (_raw_dot(a, b, 0, 0, m), (a, b)),
             lambda m, r, g: (mm_nt(r[1], g, _TN_BWD[m][0]), mm_nn(r[0], g, _TN_BWD[m][1])))


@jax.custom_jvp
def sigmoid(x):
    return 1.0 / (1.0 + jnp.exp(-x))


@sigmoid.defjvp
def _sigmoid_jvp(p, t):
    s = sigmoid(p[0])
    return s, t[0] * s * (1.0 - s)


@jax.custom_jvp
def softplus(x):
    return jnp.maximum(x, 0.0) + jnp.log(1.0 + jnp.exp(-jnp.abs(x)))


@softplus.defjvp
def _softplus_jvp(p, t):
    return softplus(p[0]), t[0] * sigmoid(p[0])


def silu(x):
    return x * sigmoid(x)


def rmsnorm(x, w):
    return x * lax.rsqrt(jnp.mean(x * x, axis=-1, keepdims=True) + EPS) * w


def _iota(shape, dim):
    return lax.broadcasted_iota(jnp.int32, shape, dim)


def _tri_inv_impl(mats):
    n = mats[0].shape[0]
    r, c = _iota((n, n), 0), _iota((n, n), 1)
    eye = _ind(r == c)
    blockdiag = _ind((r >> 4) == (c >> 4))
    dot = lambda u, v: _raw_dot(u, v, 1, 0, "x3")
    dot1 = lambda u, v: _raw_dot(u, v, 1, 0)
    each = lambda f, *ls: [f(*xs) for xs in zip(*ls)]
    dg = each(lambda a: a * blockdiag, mats)
    off = each(lambda a, d: a - d, mats, dg)
    m = each(lambda d: -d, dg)
    p = each(lambda x: eye + x, m)
    pw = m
    for _ in range(3):
        pw = each(lambda x: dot1(x, x), pw)
        p = each(lambda x, y: x + dot1(x, y), p, pw)
    e = each(dot, p, off)
    e2 = each(lambda x: dot1(x, x), e)
    q = each(lambda x: eye - x, e)
    q = each(lambda x, y: x + dot1(x, y), q, e2)
    return each(dot, q, p)


def _tri_inv_bwd(ts, gs):
    x = [mm_nt(g, t) for g, t in zip(gs, ts)]
    return [-mm_tn(t, y) for t, y in zip(ts, x)]


@jax.custom_vjp
def tri_inv(mats):
    return _tri_inv_impl(mats)


def _tri_inv_fwd(mats):
    ts = _tri_inv_impl(mats)
    return ts, ts


tri_inv.defvjp(_tri_inv_fwd, lambda ts, gs: (_tri_inv_bwd(ts, gs),))


@jax.custom_vjp
def tri_inv_saved(mats, ts):
    del mats
    return ts


tri_inv_saved.defvjp(lambda mats, ts: (ts, ts),
                     lambda ts, gs: (_tri_inv_bwd(ts, gs), [jnp.zeros_like(t) for t in ts]))


def _ind(cond):
    return jnp.where(cond, 1.0, 0.0).astype(F32)


def _chunk_masks():
    r, c = _iota((CHUNK, CHUNK), 0), _iota((CHUNK, CHUNK), 1)
    return _ind(r >= c), _ind(r > c), _ind(r == c), _ind(_iota((CHUNK, 1), 0) == CHUNK - 1)


def _log_decay_cumsum(small, alog, dtb, tri):
    sp = softplus(small + dtb)
    la = -jnp.exp(alog) * sp
    return sp, mm_nn(tri, la, "sel_a")


def _col_of(x, lane):
    return jnp.sum(x * _ind(_iota((1, LANES), 1) == lane), axis=1, keepdims=True)


def _decay_matrix(col, tri, eye):
    row = jnp.sum(col * eye, axis=0, keepdims=True)
    return jnp.exp((col - row) * tri) * tri


def gdn_chunk(h0, qs, ks, vs, small, gates, normw, alog, dtb, states, saved_t=None):
    tri, strict, eye, last = _chunk_masks()
    _, lac = _log_decay_cumsum(small, alog, dtb, tri)
    heads = range(len(qs))
    each = lambda f, *ls: [f(*xs) for xs in zip(*ls)]
    gc = [_col_of(lac, LANE_GA + h0 + j) for j in heads]
    beta = [sigmoid(_col_of(small, LANE_GB + h0 + j)) for j in heads]
    decay = each(lambda x: _decay_matrix(x, tri, eye), gc)
    gl = each(lambda x: jnp.sum(x * last, axis=0, keepdims=True), gc)
    q = each(lambda x: x * lax.rsqrt(jnp.sum(x * x, axis=-1, keepdims=True) + EPS) * (GDN_DK ** -0.5), qs)
    k = each(lambda x: x * lax.rsqrt(jnp.sum(x * x, axis=-1, keepdims=True) + EPS), ks)
    kb = each(lambda x, b: x * b, k, beta)
    a = each(lambda x, y, d: mm_nt(x, y) * (d * strict), kb, k, decay)
    t = tri_inv(a) if saved_t is None else tri_inv_saved(a, saved_t)
    eg = each(jnp.exp, gc)
    u = each(lambda x, v, b: mm_nn(x, v * b), t, vs, beta)
    w = each(lambda x, y, e: mm_nn(x, y * e), t, kb, eg)
    attn = each(lambda x, y, d: mm_nt(x, y) * d, q, k, decay)
    v_new = each(lambda x, y, s: x - mm_nn(y, s), u, w, states)
    o = each(lambda x, e, s, at, vn: mm_nn(x * e, s) + mm_nn(at, vn), q, eg, states, attn, v_new)
    new_states = each(lambda s, x, y, l, c: s * jnp.exp(l) + mm_tn(y * jnp.exp(l - c), x), states, v_new, k, gl, gc)
    ys = each(lambda x, gt: rmsnorm(x, normw) * silu(gt), o, gates)
    return ys, new_states, t


@jax.custom_vjp
def split_lanes(x):
    return [x[:, i * LANES:(i + 1) * LANES] for i in range(x.shape[1] // LANES)]


@jax.custom_vjp
def join_lanes(xs):
    return jnp.concatenate(xs, axis=1)


split_lanes.defvjp(lambda x: (split_lanes(x), None), lambda _, gs: (join_lanes(gs),))
join_lanes.defvjp(lambda xs: (join_lanes(xs), None), lambda _, g: (split_lanes(g),))


def ssd_chunk(xs, bm, cm, z, small, normw, alog, dtb, dvec, state):
    tri, _, eye, last = _chunk_masks()
    hpg = SSD_HEADS // SSD_GROUPS
    groups = range(len(xs))
    each = lambda f, *ls: [f(*a) for a in zip(*ls)]
    sp, lac = _log_decay_cumsum(small, alog, dtb, tri)
    lac_last = jnp.sum(lac * last, axis=0, keepdims=True)
    sel = [_ind(_iota((LANES, SSD_GW), 0) == g * hpg + (_iota((LANES, SSD_GW), 1) >> 6)) for g in groups]
    expand = lambda v, mode="sel_b": [mm_nn(v, s, mode) for s in sel]
    dt_e, elac_e, toend_e = expand(sp, "bf16"), expand(jnp.exp(lac), "bf16"), expand(jnp.exp(lac_last - lac), "bf16")
    row8, row8e = _iota((8, 1), 0), _iota((8, 1), 0)
    two_e = expand(_ind(row8 == 0) * dvec + _ind(row8 == 1) * jnp.exp(lac_last))
    d_e = each(lambda v: jnp.sum(v * _ind(row8e == 0), axis=0, keepdims=True), two_e)
    chunk_e = each(lambda v: jnp.sum(v * _ind(row8e == 1), axis=0, keepdims=True), two_e)
    xdt = each(lambda a, b: a * b, xs, dt_e)
    y = each(lambda c_, st, el, x_, d_: mm_nn(c_, st) * el + x_ * d_, cm, state, elac_e, xs, d_e)
    r, c = _iota((CHUNK, LANES), 0), _iota((CHUNK, LANES), 1)
    tri_w = _ind(r >= (c & (CHUNK - 1)))
    eye_w = [_ind(c == r), _ind(c == r + CHUNK)]
    half = [_ind((_iota((1, LANES), 1) >> 6) == s) for s in range(2)]

    def decay_pair(col_a, col_b):
        col = col_a * half[0] + col_b * half[1]
        row = jnp.sum(col_a * eye_w[0] + col_b * eye_w[1], axis=0, keepdims=True)
        return jnp.exp((col - row) * tri_w) * tri_w

    pairs = range(hpg // 2)
    cb_w = each(lambda c_, b_: mm_nt(c_, jnp.concatenate([b_, b_], axis=0)), cm, bm)
    x_pairs = each(split_lanes, xdt)
    lms = [[decay_pair(_col_of(lac, g * hpg + 2 * p), _col_of(lac, g * hpg + 2 * p + 1)) for p in pairs]
           for g in groups]
    stacked = [[jnp.concatenate([x_pairs[g][p] * half[0], x_pairs[g][p] * half[1]], axis=0) for p in pairs]
               for g in groups]
    terms = [[mm_nn(cb_w[g] * lms[g][p], stacked[g][p]) for p in pairs] for g in groups]
    y = [y[g] + join_lanes(terms[g]) for g in groups]
    new_state = each(lambda st, ce, b_, xd, te: st * ce + mm_tn(b_, xd * te), state, chunk_e, bm, xdt, toend_e)
    out = each(lambda y_, z_, nw: rmsnorm(y_ * silu(z_), nw), y, z, normw)
    return out, new_state


def _params(sem=None):
    return pltpu.CompilerParams(dimension_semantics=sem, vmem_limit_bytes=VMEM_LIMIT)


def _full(shape):
    n = len(shape)
    return pl.BlockSpec(shape, lambda *_: (0,) * n)


ANY = pl.BlockSpec(memory_space=pl.ANY)
HBM = pl.BlockSpec(memory_space=pltpu.HBM)


def in_proj(x, normw, w_main, w_small):
    t = x.shape[0]
    tm, tn = min(1024, t), 512

    def body(x_ref, nw_ref, wm_ref, ws_ref, pm_ref, ps_ref, u_ref):
        @pl.when(pl.program_id(1) == 0)
        def _():
            u = rmsnorm(x_ref[...], nw_ref[...]).astype(MXU_DTYPE)
            u_ref[...] = u
            ps_ref[...] = _raw_dot(u, ws_ref[...], 1, 0)
        pm_ref[...] = _raw_dot(u_ref[...], wm_ref[...], 1, 0)

    return pl.pallas_call(
        body, name="in_proj", grid=(t // tm, COL_CONV // tn),
        in_specs=[pl.BlockSpec((tm, D_MODEL), lambda i, j: (i, 0)), _full((1, D_MODEL)),
                  pl.BlockSpec((D_MODEL, tn), lambda i, j: (0, j)), _full((D_MODEL, LANES))],
        out_specs=[pl.BlockSpec((tm, tn), lambda i, j: (i, j)), pl.BlockSpec((tm, LANES), lambda i, j: (i, 0)),
                   pl.BlockSpec((tm, D_MODEL), lambda i, j: (i, 0))],
        out_shape=[jax.ShapeDtypeStruct((t, MAIN), F32), jax.ShapeDtypeStruct((t, LANES), F32),
                   jax.ShapeDtypeStruct((t, D_MODEL), MXU_DTYPE)],
        compiler_params=_params(("arbitrary", "arbitrary")),
    )(x, normw, w_main, w_small)


CONV_TC = 512
HALO = 8


def _shift_down(cur, prev, s):
    rolled = pltpu.roll(cur, s, 0)
    top = jnp.where(_iota((HALO, cur.shape[1]), 0) < s, pltpu.roll(prev, s, 0), rolled[:HALO])
    if cur.shape[0] == HALO:
        return top
    return jnp.concatenate([top, rolled[HALO:]], axis=0)


def _shift_up(cur, nxt, s):
    n = cur.shape[0]
    rolled = pltpu.roll(cur, n - s, 0)
    bot = jnp.where(_iota((HALO, cur.shape[1]), 0) >= HALO - s, pltpu.roll(nxt, HALO - s, 0), rolled[n - HALO:])
    return jnp.concatenate([rolled[:n - HALO], bot], axis=0)


def _conv_pre(cur, prev, w_ref, b):
    acc = cur * w_ref[3:4, :] + b
    shifted = [cur]
    for s in (1, 2, 3):
        sh = _shift_down(cur, prev, s)
        shifted.append(sh)
        acc = acc + sh * w_ref[3 - s:4 - s, :]
    return acc, shifted


def in_proj_conv(proj_main, u, w_main, w, b):
    t = u.shape[0]
    tm, tn = min(1024, t), CONV_TC
    rc = min(256, tm)
    c0, nj = COL_CONV // tn, (MAIN - COL_CONV) // tn

    def body(alias_ref, u_ref, wm_ref, w_ref, b_ref, pm_ref, out_ref, halo_ref):
        del alias_ref
        j = pl.program_id(1)

        @pl.when(pl.program_id(0) == 0)
        def _():
            halo_ref[j] = jnp.zeros((HALO, tn), F32)

        prev = halo_ref[j]
        for r in range(tm // rc):
            rows = pl.ds(r * rc, rc)
            p = _raw_dot(u_ref[rows, :], wm_ref[...], 1, 0)
            pm_ref[rows, :] = p
            pre, _ = _conv_pre(p, prev, w_ref, b_ref[...])
            out_ref[rows, :] = silu(pre)
            prev = p[rc - HALO:]
        halo_ref[j] = prev

    return pl.pallas_call(
        body, name="in_proj_conv", grid=(t // tm, nj),
        in_specs=[ANY, pl.BlockSpec((tm, D_MODEL), lambda i, j: (i, 0)),
                  pl.BlockSpec((D_MODEL, tn), lambda i, j: (0, c0 + j)),
                  pl.BlockSpec((4, tn), lambda i, j: (0, j)), pl.BlockSpec((1, tn), lambda i, j: (0, j))],
        out_specs=[pl.BlockSpec((tm, tn), lambda i, j: (i, c0 + j)), pl.BlockSpec((tm, tn), lambda i, j: (i, j))],
        out_shape=[jax.ShapeDtypeStruct(proj_main.shape, F32), jax.ShapeDtypeStruct((t, MAIN - COL_CONV), F32)],
        scratch_shapes=[pltpu.VMEM((nj, HALO, tn), F32)],
        input_output_aliases={0: 0},
        compiler_params=_params(("arbitrary", "arbitrary")),
    )(proj_main, u, w_main, w, b)


def _dsilu(pre):
    sg = sigmoid(pre)
    return sg * (1.0 + pre * (1.0 - sg))


def conv_bwd(dproj_main, proj_main, col0, width, w, b, dout, name):
    t = proj_main.shape[0]
    tt, c0 = min(512, t), col0 // CONV_TC
    nt = t // tt
    after = lambda i: jnp.minimum((i + 1) * (tt // HALO), t // HALO - 1)

    def body(alias_ref, cur_ref, prev_ref, nxt_ref, w_ref, b_ref, do_ref, do_nxt_ref, dx_ref, dwb_ref):
        del alias_ref
        i = pl.program_id(1)
        cur, bias = cur_ref[...], b_ref[...]
        prev = jnp.where(i > 0, prev_ref[...], 0.0)
        pre, shifted = _conv_pre(cur, prev, w_ref, bias)
        dpre = do_ref[...] * _dsilu(pre)
        pre_nxt, _ = _conv_pre(nxt_ref[...], cur[tt - HALO:], w_ref, bias)
        dpre_nxt = jnp.where(i < nt - 1, do_nxt_ref[...] * _dsilu(pre_nxt), 0.0)
        dx = dpre * w_ref[3:4, :]
        for s in (1, 2, 3):
            dx = dx + _shift_up(dpre, dpre_nxt, s) * w_ref[3 - s:4 - s, :]
        dx_ref[...] = dx.astype(dx_ref.dtype)
        row = _iota((HALO, CONV_TC), 0)
        upd = jnp.where(row == 4, jnp.sum(dpre, axis=0, keepdims=True), 0.0)
        for s in range(4):
            upd = upd + jnp.where(row == 3 - s, jnp.sum(dpre * shifted[s], axis=0, keepdims=True), 0.0)
        _accumulate(dwb_ref, i == 0, upd)

    return pl.pallas_call(
        body, name=name, grid=(width // CONV_TC, nt),
        in_specs=[ANY, pl.BlockSpec((tt, CONV_TC), lambda j, i: (i, c0 + j)),
                  pl.BlockSpec((HALO, CONV_TC), lambda j, i: (jnp.maximum(i * (tt // HALO) - 1, 0), c0 + j)),
                  pl.BlockSpec((HALO, CONV_TC), lambda j, i: (after(i), c0 + j)),
                  pl.BlockSpec((4, CONV_TC), lambda j, i: (0, j)), pl.BlockSpec((1, CONV_TC), lambda j, i: (0, j)),
                  pl.BlockSpec((tt, CONV_TC), lambda j, i: (i, j)),
                  pl.BlockSpec((HALO, CONV_TC), lambda j, i: (after(i), j))],
        out_specs=[pl.BlockSpec((tt, CONV_TC), lambda j, i: (i, c0 + j)),
                   pl.BlockSpec((HALO, CONV_TC), lambda j, i: (0, j))],
        out_shape=[jax.ShapeDtypeStruct(dproj_main.shape, dproj_main.dtype), jax.ShapeDtypeStruct((HALO, width), F32)],
        input_output_aliases={0: 0},
        compiler_params=_params(("arbitrary", "arbitrary")),
    )(dproj_main, proj_main, proj_main, proj_main, w, b, dout, dout)


def _ssd_parts(xbc_ref):
    part = lambda o, w: [xbc_ref[:, g * SSD_GC + o:g * SSD_GC + o + w] for g in range(SSD_GROUPS)]
    return part(0, SSD_GW), part(SSD_GW, SSD_STATE), part(SSD_GW + SSD_STATE, SSD_STATE)


def _group_cols(ref):
    return [ref[:, g * SSD_GW:(g + 1) * SSD_GW] for g in range(SSD_GROUPS)]


def _gdn_parts(qkv_ref):
    part = lambda o: [qkv_ref[:, j * GDN_HC + o:j * GDN_HC + o + GDN_DK] for j in range(GDN_HB)]
    return part(0), part(GDN_DK), part(2 * GDN_DK)


def _head_cols(ref):
    return [ref[:, j * GDN_DV:(j + 1) * GDN_DV] for j in range(GDN_HB)]


def _first_head():
    return 0 if GDN_HB == GDN_HEADS else pl.program_id(1) * GDN_HB


def ssd_fwd(conv_ssd, proj_main, proj_small, normw, alog, dtb, dvec):
    t = conv_ssd.shape[0]
    nc = t // CHUNK

    groups = range(SSD_GROUPS)

    def body(xbc_ref, z_ref, sm_ref, nw_ref, al_ref, db_ref, dv_ref, y_ref, hist_ref, state_ref):
        @pl.when(pl.program_id(0) == 0)
        def _():
            state_ref[...] = jnp.zeros(state_ref.shape, F32)

        states = [state_ref[g] for g in groups]
        for g in groups:
            hist_ref[0, g] = states[g]
        ys, new_states = ssd_chunk(*_ssd_parts(xbc_ref), _group_cols(z_ref), sm_ref[...], _group_cols(nw_ref),
                                   al_ref[...], db_ref[...], dv_ref[...], states)
        for g in groups:
            y_ref[:, g * SSD_GW:(g + 1) * SSD_GW] = ys[g].astype(MXU_DTYPE)
            state_ref[g] = new_states[g]

    return pl.pallas_call(
        body, name="ssd_fwd", grid=(nc,),
        in_specs=[pl.BlockSpec((CHUNK, SSD_CONV), lambda c: (c, (COL_SSD - COL_CONV) // SSD_CONV)),
                  pl.BlockSpec((CHUNK, SSD_WIDTH), lambda c: (c, COL_Z // SSD_WIDTH)),
                  pl.BlockSpec((CHUNK, LANES), lambda c: (c, 0)),
                  _full((1, SSD_WIDTH)), _full((1, LANES)), _full((1, LANES)), _full((1, LANES))],
        out_specs=[pl.BlockSpec((CHUNK, SSD_WIDTH), lambda c: (c, 0)),
                   pl.BlockSpec((1, SSD_GROUPS, SSD_STATE, SSD_GW), lambda c: (c, 0, 0, 0))],
        out_shape=[jax.ShapeDtypeStruct((t, SSD_WIDTH), MXU_DTYPE),
                   jax.ShapeDtypeStruct((nc, SSD_GROUPS, SSD_STATE, SSD_GW), F32)],
        scratch_shapes=[pltpu.VMEM((SSD_GROUPS, SSD_STATE, SSD_GW), F32)],
        compiler_params=_params(("arbitrary",)),
    )(conv_ssd, proj_main, proj_small, normw, alog, dtb, dvec)


def _accumulate(ref, first, value):
    @pl.when(first)
    def _():
        ref[...] = value

    @pl.when(jnp.logical_not(first))
    def _():
        ref[...] += value


def ssd_bwd(conv_ssd, proj_main, proj_small, normw, alog, dtb, dvec, hist, dy):
    t = conv_ssd.shape[0]
    nc = t // CHUNK
    rev = lambda c: nc - 1 - c
    groups = range(SSD_GROUPS)

    def body(xbc_ref, z_ref, sm_ref, nw_ref, al_ref, db_ref, dv_ref, hist_ref, dy_ref,
             dxbc_ref, dz_ref, dsm_ref, dnw_ref, dal_ref, ddb_ref, ddv_ref, dstate_ref):
        first = pl.program_id(0) == 0

        @pl.when(first)
        def _():
            dstate_ref[...] = jnp.zeros(dstate_ref.shape, F32)

        _, vjp = jax.vjp(ssd_chunk, *_ssd_parts(xbc_ref), _group_cols(z_ref), sm_ref[...], _group_cols(nw_ref),
                         al_ref[...], db_ref[...], dv_ref[...], [hist_ref[0, g] for g in groups])
        dxs, dbm, dcm, dz, dsm, dnw, dal, ddb, ddv, dstate = vjp(
            (_group_cols(dy_ref), [dstate_ref[g] for g in groups]))
        for g in groups:
            base = g * SSD_GC
            dxbc_ref[:, base:base + SSD_GW] = dxs[g]
            dxbc_ref[:, base + SSD_GW:base + SSD_GW + SSD_STATE] = dbm[g]
            dxbc_ref[:, base + SSD_GW + SSD_STATE:base + SSD_GC] = dcm[g]
            dz_ref[:, g * SSD_GW:(g + 1) * SSD_GW] = dz[g].astype(dz_ref.dtype)
            dstate_ref[g] = dstate[g]
        dsm_ref[...] = dsm
        _accumulate(dnw_ref, first, join_lanes(dnw))
        _accumulate(dal_ref, first, dal)
        _accumulate(ddb_ref, first, ddb)
        _accumulate(ddv_ref, first, ddv)

    return pl.pallas_call(
        body, name="ssd_bwd", grid=(nc,),
        in_specs=[pl.BlockSpec((CHUNK, SSD_CONV), lambda c: (rev(c), (COL_SSD - COL_CONV) // SSD_CONV)),
                  pl.BlockSpec((CHUNK, SSD_WIDTH), lambda c: (rev(c), COL_Z // SSD_WIDTH)),
                  pl.BlockSpec((CHUNK, LANES), lambda c: (rev(c), 0)),
                  _full((1, SSD_WIDTH)), _full((1, LANES)), _full((1, LANES)), _full((1, LANES)),
                  pl.BlockSpec((1, SSD_GROUPS, SSD_STATE, SSD_GW), lambda c: (rev(c), 0, 0, 0)),
                  pl.BlockSpec((CHUNK, SSD_WIDTH), lambda c: (rev(c), 0))],
        out_specs=[pl.BlockSpec((CHUNK, SSD_CONV), lambda c: (rev(c), 0)),
                   pl.BlockSpec((CHUNK, SSD_WIDTH), lambda c: (rev(c), COL_Z // SSD_WIDTH)),
                   pl.BlockSpec((CHUNK, LANES), lambda c: (rev(c), 0)),
                   _full((1, SSD_WIDTH)), _full((1, LANES)), _full((1, LANES)), _full((1, LANES))],
        out_shape=[jax.ShapeDtypeStruct((t, SSD_CONV), F32), jax.ShapeDtypeStruct((t, MAIN), MXU_DTYPE),
                   jax.ShapeDtypeStruct((t, LANES), F32), jax.ShapeDtypeStruct((1, SSD_WIDTH), F32),
                   jax.ShapeDtypeStruct((1, LANES), F32), jax.ShapeDtypeStruct((1, LANES), F32),
                   jax.ShapeDtypeStruct((1, LANES), F32)],
        scratch_shapes=[pltpu.VMEM((SSD_GROUPS, SSD_STATE, SSD_GW), F32)],
        compiler_params=_params(("arbitrary",)),
    )(conv_ssd, proj_main, proj_small, normw, alog, dtb, dvec, hist, dy)


def gdn_fwd(conv_gdn, proj_main, proj_small, normw, alog, dtb):
    t = conv_gdn.shape[0]
    nc = t // CHUNK

    hb = GDN_HB
    gate_blk = COL_GATE // (GDN_DV * hb)

    def body(qkv_ref, gate_ref, sm_ref, nw_ref, al_ref, db_ref, y_ref, hist_ref, t_ref, state_ref):
        h0 = _first_head()

        @pl.when(pl.program_id(0) == 0)
        def _():
            for j in range(hb):
                state_ref[h0 + j] = jnp.zeros((GDN_DK, GDN_DV), F32)

        states = [state_ref[h0 + j] for j in range(hb)]
        for j in range(hb):
            hist_ref[0, j] = states[j]
        qs, ks, vs = _gdn_parts(qkv_ref)
        ys, new_states, ts = gdn_chunk(h0, qs, ks, vs, sm_ref[...], _head_cols(gate_ref), nw_ref[...], al_ref[...],
                                       db_ref[...], states)
        for j in range(hb):
            y_ref[:, j * GDN_DV:(j + 1) * GDN_DV] = ys[j].astype(MXU_DTYPE)
            state_ref[h0 + j] = new_states[j]
            t_ref[0, j] = ts[j]

    return pl.pallas_call(
        body, name="gdn_fwd", grid=(nc, GDN_HEADS // hb),
        in_specs=[pl.BlockSpec((CHUNK, GDN_HC * hb), lambda c, h: (c, h)),
                  pl.BlockSpec((CHUNK, GDN_DV * hb), lambda c, h: (c, gate_blk + h)),
                  pl.BlockSpec((CHUNK, LANES), lambda c, h: (c, 0)),
                  _full((1, GDN_DV)), _full((1, LANES)), _full((1, LANES))],
        out_specs=[pl.BlockSpec((CHUNK, GDN_DV * hb), lambda c, h: (c, h)),
                   pl.BlockSpec((1, hb, GDN_DK, GDN_DV), lambda c, h: (c, h, 0, 0)),
                   pl.BlockSpec((1, hb, CHUNK, CHUNK), lambda c, h: (c, h, 0, 0))],
        out_shape=[jax.ShapeDtypeStruct((t, GDN_W), MXU_DTYPE),
                   jax.ShapeDtypeStruct((nc, GDN_HEADS, GDN_DK, GDN_DV), F32),
                   jax.ShapeDtypeStruct((nc, GDN_HEADS, CHUNK, CHUNK), F32)],
        scratch_shapes=[pltpu.VMEM((GDN_HEADS, GDN_DK, GDN_DV), F32)],
        compiler_params=_params(("arbitrary", "arbitrary")),
    )(conv_gdn, proj_main, proj_small, normw, alog, dtb)


def gdn_bwd(dproj_main, conv_gdn, proj_main, proj_small, normw, alog, dtb, hist, t_inv, dy):
    t = conv_gdn.shape[0]
    nc = t // CHUNK
    rev = lambda c: nc - 1 - c
    hb = GDN_HB
    gate_blk = COL_GATE // (GDN_DV * hb)

    def body(alias_ref, qkv_ref, gate_ref, sm_ref, nw_ref, al_ref, db_ref, hist_ref, t_ref, dy_ref,
             dgate_ref, dqkv_ref, dsm_ref, dnw_ref, dal_ref, ddb_ref, dstate_ref):
        del alias_ref
        c, h = pl.program_id(0), pl.program_id(1)
        h0 = _first_head()

        @pl.when(c == 0)
        def _():
            for j in range(hb):
                dstate_ref[h0 + j] = jnp.zeros((GDN_DK, GDN_DV), F32)

        saved = [t_ref[0, j] for j in range(hb)]

        def fn(qs, ks, vs, small, gates, nw, al, db, states):
            return gdn_chunk(h0, qs, ks, vs, small, gates, nw, al, db, states, saved)[:2]

        qs, ks, vs = _gdn_parts(qkv_ref)
        _, vjp = jax.vjp(fn, qs, ks, vs, sm_ref[...], _head_cols(gate_ref), nw_ref[...], al_ref[...], db_ref[...],
                         [hist_ref[0, j] for j in range(hb)])
        dqs, dks, dvs, dsm, dgates, dnw, dal, ddb, dstates = vjp(
            (_head_cols(dy_ref), [dstate_ref[h0 + j] for j in range(hb)]))
        for j in range(hb):
            base = j * GDN_HC
            dqkv_ref[:, base:base + GDN_DK] = dqs[j]
            dqkv_ref[:, base + GDN_DK:base + 2 * GDN_DK] = dks[j]
            dqkv_ref[:, base + 2 * GDN_DK:base + GDN_HC] = dvs[j]
            dgate_ref[:, j * GDN_DV:(j + 1) * GDN_DV] = dgates[j].astype(dgate_ref.dtype)
            dstate_ref[h0 + j] = dstates[j]
        _accumulate(dsm_ref, h == 0, dsm)
        first = jnp.logical_and(c == 0, h == 0)
        _accumulate(dnw_ref, first, dnw)
        _accumulate(dal_ref, first, dal)
        _accumulate(ddb_ref, first, ddb)

    return pl.pallas_call(
        body, name="gdn_bwd", grid=(nc, GDN_HEADS // hb),
        in_specs=[ANY, pl.BlockSpec((CHUNK, GDN_HC * hb), lambda c, h: (rev(c), h)),
                  pl.BlockSpec((CHUNK, GDN_DV * hb), lambda c, h: (rev(c), gate_blk + h)),
                  pl.BlockSpec((CHUNK, LANES), lambda c, h: (rev(c), 0)),
                  _full((1, GDN_DV)), _full((1, LANES)), _full((1, LANES)),
                  pl.BlockSpec((1, hb, GDN_DK, GDN_DV), lambda c, h: (rev(c), h, 0, 0)),
                  pl.BlockSpec((1, hb, CHUNK, CHUNK), lambda c, h: (rev(c), h, 0, 0)),
                  pl.BlockSpec((CHUNK, GDN_DV * hb), lambda c, h: (rev(c), h))],
        out_specs=[pl.BlockSpec((CHUNK, GDN_DV * hb), lambda c, h: (rev(c), gate_blk + h)),
                   pl.BlockSpec((CHUNK, GDN_HC * hb), lambda c, h: (rev(c), h)),
                   pl.BlockSpec((CHUNK, LANES), lambda c, h: (rev(c), 0)),
                   _full((1, GDN_DV)), _full((1, LANES)), _full((1, LANES))],
        out_shape=[jax.ShapeDtypeStruct(dproj_main.shape, dproj_main.dtype), jax.ShapeDtypeStruct((t, GDN_CONV), F32),
                   jax.ShapeDtypeStruct((t, LANES), F32), jax.ShapeDtypeStruct((1, GDN_DV), F32),
                   jax.ShapeDtypeStruct((1, LANES), F32), jax.ShapeDtypeStruct((1, LANES), F32)],
        scratch_shapes=[pltpu.VMEM((GDN_HEADS, GDN_DK, GDN_DV), F32)],
        input_output_aliases={0: 0},
        compiler_params=_params(("arbitrary", "arbitrary")),
    )(dproj_main, conv_gdn, proj_main, proj_small, normw, alog, dtb, hist, t_inv, dy)


def out_proj_loss(x, y_ssd, y_gdn, w_out, final_w, target):
    t = x.shape[0]
    tm = min(256, t)

    def body(x_ref, ys_ref, yg_ref, wo_ref, fw_ref, tg_ref, loss_ref, dhid_ref, dys_ref, dyg_ref, dwo_ref, dfw_ref):
        i = pl.program_id(0)
        ys, yg = ys_ref[...], yg_ref[...]
        wo_s, wo_g = wo_ref[:SSD_WIDTH, :], wo_ref[SSD_WIDTH:, :]
        hid = x_ref[...] + _raw_dot(ys, wo_s, 1, 0) + _raw_dot(yg, wo_g, 1, 0)
        out, vjp = jax.vjp(rmsnorm, hid, fw_ref[...])
        err = out - tg_ref[...]
        loss = 0.5 * jnp.sum(jnp.mean(err * err, axis=-1, keepdims=True), axis=0, keepdims=True)
        dhid, dfw = vjp(err * (1.0 / D_MODEL))
        dhid_ref[...] = dhid
        dys_ref[...] = _raw_dot(dhid, wo_s, 1, 1)
        dyg_ref[...] = _raw_dot(dhid, wo_g, 1, 1)
        first = i == 0
        _accumulate(loss_ref, first, jnp.broadcast_to(loss, loss_ref.shape))
        _accumulate(dfw_ref, first, dfw)

        @pl.when(first)
        def _():
            dwo_ref[:SSD_WIDTH, :] = _raw_dot(ys, dhid, 0, 0)
            dwo_ref[SSD_WIDTH:, :] = _raw_dot(yg, dhid, 0, 0)

        @pl.when(i > 0)
        def _():
            dwo_ref[:SSD_WIDTH, :] += _raw_dot(ys, dhid, 0, 0)
            dwo_ref[SSD_WIDTH:, :] += _raw_dot(yg, dhid, 0, 0)

    row = lambda w: pl.BlockSpec((tm, w), lambda i: (i, 0))
    return pl.pallas_call(
        body, name="out_proj_loss", grid=(t // tm,),
        in_specs=[row(D_MODEL), row(SSD_WIDTH), row(GDN_W), _full((SSD_WIDTH + GDN_W, D_MODEL)), _full((1, D_MODEL)),
                  row(D_MODEL)],
        out_specs=[_full((8, LANES)), row(D_MODEL), row(SSD_WIDTH), row(GDN_W), _full((SSD_WIDTH + GDN_W, D_MODEL)),
                   _full((1, D_MODEL))],
        out_shape=[jax.ShapeDtypeStruct((8, LANES), F32), jax.ShapeDtypeStruct((t, D_MODEL), F32),
                   jax.ShapeDtypeStruct((t, SSD_WIDTH), F32), jax.ShapeDtypeStruct((t, GDN_W), F32),
                   jax.ShapeDtypeStruct((SSD_WIDTH + GDN_W, D_MODEL), F32), jax.ShapeDtypeStruct((1, D_MODEL), F32)],
        compiler_params=_params(("arbitrary",)),
    )(x, y_ssd, y_gdn, w_out, final_w, target)


def in_proj_bwd_x(x, normw, w_main, w_small, dproj_main, dsmall_a, dsmall_b, dhid, slabbed):
    t = x.shape[0]
    tm = min(256, t)
    ni = t // tm
    ns = len(slabbed)

    def body(x_ref, nw_ref, wm_ref, ws_ref, dp_ref, da_ref, db_ref, dh_ref, *rest):
        slab_refs, (gx_ref, dnw_ref), land_refs = rest[:ns], rest[ns:ns + 2], rest[ns + 2:2 * ns + 2]
        sems = rest[2 * ns + 2:]
        i = pl.program_id(0)
        start, finish = _slab_exchange(slab_refs, land_refs, ns, *sems)

        @pl.when(i == 0)
        def _():
            start()

        du = _raw_dot(dp_ref[...], wm_ref[...], 1, 1) + _raw_dot(da_ref[...] + db_ref[...], ws_ref[...], 1, 1)
        _, vjp = jax.vjp(rmsnorm, x_ref[...], nw_ref[...])
        dx, dnw = vjp(du)
        gx_ref[...] = dx + dh_ref[...]
        _accumulate(dnw_ref, i == 0, dnw)

        @pl.when(i == ni - 1)
        def _():
            finish()

    row = lambda w: pl.BlockSpec((tm, w), lambda i: (i, 0))
    out = pl.pallas_call(
        body, name="in_proj_bwd_x", grid=(ni,),
        in_specs=[row(D_MODEL), _full((1, D_MODEL)), _full((D_MODEL, MAIN)), _full((D_MODEL, LANES)), row(MAIN),
                  row(LANES), row(LANES), row(D_MODEL)] + [HBM] * ns,
        out_specs=[row(D_MODEL), _full((1, D_MODEL))] + [HBM] * ns,
        out_shape=[jax.ShapeDtypeStruct((t, D_MODEL), F32), jax.ShapeDtypeStruct((1, D_MODEL), F32)]
        + _slab_exchange_shapes(slabbed, []),
        scratch_shapes=_slab_exchange_sems(ns),
        compiler_params=_params(("arbitrary",)),
    )(x, normw, w_main, w_small, dproj_main, dsmall_a, dsmall_b, dhid, *slabbed)
    return out[0], out[1], out[2:]


def in_proj_bwd_w(u, dproj_main, dsmall_a, dsmall_b, slabbed):
    t = u.shape[0]
    tm, tn = min(1024, t), MAIN // 4
    nj, ni = MAIN // tn, t // tm
    ns = len(slabbed)

    def body(u_ref, dp_ref, da_ref, db_ref, *rest):
        slab_refs, (dwm_ref, dws_ref), land_refs, sems = rest[:ns], rest[ns:ns + 2], rest[ns + 2:2 * ns + 2], rest[2 * ns + 2:]
        j, i = pl.program_id(0), pl.program_id(1)
        start, finish = _slab_exchange(slab_refs, land_refs, ns, *sems)

        @pl.when(jnp.logical_and(j == 0, i == 0))
        def _():
            start()

        uu = u_ref[...]
        _accumulate(dwm_ref, i == 0, _raw_dot(uu, dp_ref[...], 0, 0))

        @pl.when(j == 0)
        def _():
            _accumulate(dws_ref, i == 0, _raw_dot(uu, da_ref[...] + db_ref[...], 0, 0))

        @pl.when(jnp.logical_and(j == nj - 1, i == ni - 1))
        def _():
            finish()

    out = pl.pallas_call(
        body, name="in_proj_bwd_w", grid=(nj, ni),
        in_specs=[pl.BlockSpec((tm, D_MODEL), lambda j, i: (i, 0)), pl.BlockSpec((tm, tn), lambda j, i: (i, j)),
                  pl.BlockSpec((tm, LANES), lambda j, i: (i, 0)), pl.BlockSpec((tm, LANES), lambda j, i: (i, 0))]
        + [HBM] * ns,
        out_specs=[pl.BlockSpec((D_MODEL, tn), lambda j, i: (0, j)), _full((D_MODEL, LANES))] + [HBM] * ns,
        out_shape=[jax.ShapeDtypeStruct((D_MODEL, MAIN), F32), jax.ShapeDtypeStruct((D_MODEL, LANES), F32)]
        + _slab_exchange_shapes(slabbed, []),
        scratch_shapes=_slab_exchange_sems(ns),
        compiler_params=_params(("arbitrary", "arbitrary")),
    )(u, dproj_main, dsmall_a, dsmall_b, *slabbed)
    return out[0], out[1], out[2:]


def sum_slabs(a, name):
    n, rows, cols = a.shape
    tr = 64 if rows % 64 == 0 else rows

    def body(a_ref, o_ref):
        acc = a_ref[0].astype(F32)
        for d in range(1, n):
            acc = acc + a_ref[d].astype(F32)
        o_ref[...] = acc

    return pl.pallas_call(
        body, name=name, grid=(rows // tr,),
        in_specs=[pl.BlockSpec((n, tr, cols), lambda i: (0, i, 0))],
        out_specs=pl.BlockSpec((tr, cols), lambda i: (i, 0)),
        out_shape=jax.ShapeDtypeStruct((rows, cols), F32),
        compiler_params=_params(("arbitrary",)),
    )(a)


def adamw(w, g, m, v, name):
    rows, cols = w.shape
    tr = 128 if rows % 128 == 0 else rows

    def body(w_ref, g_ref, m_ref, v_ref, d_ref, nm_ref, nv_ref):
        gg = g_ref[...]
        nm = ADAM_B1 * m_ref[...] + (1.0 - ADAM_B1) * gg
        nv = ADAM_B2 * v_ref[...] + (1.0 - ADAM_B2) * (gg * gg)
        m_hat = nm / (1.0 - ADAM_B1 ** ADAM_STEP)
        v_hat = nv / (1.0 - ADAM_B2 ** ADAM_STEP)
        d_ref[...] = -ADAM_LR * (m_hat / (jnp.sqrt(v_hat) + ADAM_EPS) + ADAM_WD * w_ref[...])
        nm_ref[...] = nm
        nv_ref[...] = nv

    spec = pl.BlockSpec((tr, cols), lambda i: (i, 0))
    shp = jax.ShapeDtypeStruct((rows, cols), F32)
    return pl.pallas_call(
        body, name=name, grid=(rows // tr,), in_specs=[spec] * 4, out_specs=[spec] * 3, out_shape=[shp] * 3,
        compiler_params=_params(("arbitrary",)),
    )(w, g, m, v)


def _my_place():
    return lax.axis_index("x"), lax.axis_index("y"), lax.axis_index("c")


def gather_weights(big, small):
    nb, n = len(big), len(big) + len(small)
    parts = 4

    def body(*refs):
        srcs, outs = refs[:n], refs[n:2 * n]
        land_a, land_b = refs[2 * n:2 * n + nb], refs[2 * n + nb:2 * n + 2 * nb]
        send_sems, recv_sems, fwd_send, fwd_recv, local_sems = refs[2 * n + 2 * nb:]
        x, y, c = _my_place()
        me = 2 * x + y
        chips = [(1 - x, y), (x, 1 - y), (1 - x, 1 - y)]
        half = [a.shape[0] // 2 for a in big]

        def ici(j, i):
            px, py = chips[j]
            if i < nb:
                src, dst = srcs[i].at[pl.ds(c * half[i], half[i])], land_a[i].at[j]
            else:
                src, dst = srcs[i], outs[i].at[me]
            return pltpu.make_async_remote_copy(src_ref=src, dst_ref=dst, send_sem=send_sems.at[j * n + i],
                                                recv_sem=recv_sems.at[j * n + i], device_id=(px, py, c),
                                                device_id_type=MESH)

        def ici_arrival(j, i):
            px, py = chips[j]
            dst = land_a[i].at[j] if i < nb else outs[i].at[2 * px + py]
            return pltpu.make_async_remote_copy(src_ref=dst, dst_ref=dst, send_sem=send_sems.at[j * n + i],
                                                recv_sem=recv_sems.at[j * n + i], device_id=(px, py, c),
                                                device_id_type=MESH)

        def forward(j, i, p):
            rows = half[i] // parts
            k = (j * nb + i) * parts + p
            return pltpu.make_async_remote_copy(
                src_ref=land_a[i].at[j, pl.ds(p * rows, rows)], dst_ref=land_b[i].at[j, pl.ds(p * rows, rows)],
                send_sem=fwd_send.at[k], recv_sem=fwd_recv.at[k], device_id=(x, y, 1 - c), device_id_type=MESH)

        def store(j, i, from_sibling):
            px, py = chips[j]
            buf, h = (land_b, 1 - c) if from_sibling else (land_a, c)
            k = n + (j * nb + i) * 2 + (1 if from_sibling else 0)
            return pltpu.make_async_copy(buf[i].at[j], outs[i].at[2 * px + py, pl.ds(h * half[i], half[i])],
                                         local_sems.at[k])

        own = [pltpu.make_async_copy(srcs[i], outs[i].at[me], local_sems.at[i]) for i in range(n)]
        sends = [ici(j, i) for j in range(3) for i in range(n)]
        for cp in own + sends:
            cp.start()
        pending = []
        for j in range(3):
            for i in range(n):
                ici_arrival(j, i).wait_recv()
                if i < nb:
                    fw = [forward(j, i, p) for p in range(parts)]
                    st = store(j, i, False)
                    for cp in fw + [st]:
                        cp.start()
                    pending += [cp.wait_send for cp in fw] + [st.wait]
        for j in range(3):
            for i in range(nb):
                for p in range(parts):
                    forward(j, i, p).wait_recv()
                st = store(j, i, True)
                st.start()
                pending.append(st.wait)
        for cp in sends:
            cp.wait_send()
        for wait in pending:
            wait()
        for cp in own:
            cp.wait()

    shards = list(big) + list(small)
    lands = [pltpu.VMEM((3, a.shape[0] // 2) + a.shape[1:], a.dtype) for a in big]
    return pl.pallas_call(
        body, name="gather_weights",
        in_specs=[HBM] * n, out_specs=[HBM] * n,
        out_shape=[jax.ShapeDtypeStruct((N_CHIP,) + s.shape, s.dtype) for s in shards],
        scratch_shapes=lands + lands + [
            pltpu.SemaphoreType.DMA((3 * n,)), pltpu.SemaphoreType.DMA((3 * n,)),
            pltpu.SemaphoreType.DMA((3 * nb * parts,)), pltpu.SemaphoreType.DMA((3 * nb * parts,)),
            pltpu.SemaphoreType.DMA((n + 6 * nb,))],
        compiler_params=pltpu.CompilerParams(vmem_limit_bytes=VMEM_LIMIT),
    )(*shards)


def _peer(x, y, c, mask):
    mx, my, mc = (mask >> 2) & 1, (mask >> 1) & 1, mask & 1
    return (x ^ mx if mx else x, y ^ my if my else y, c ^ mc if mc else c)


def _slab_exchange_shapes(slabbed, replicated):
    return ([jax.ShapeDtypeStruct(a.shape, a.dtype) for a in slabbed]
            + [jax.ShapeDtypeStruct((N_DEV,) + a.shape, a.dtype) for a in replicated])


def _slab_exchange_sems(n):
    return [pltpu.SemaphoreType.DMA((7 * n,)), pltpu.SemaphoreType.DMA((7 * n,)), pltpu.SemaphoreType.DMA((n,))]


def _slab_exchange(srcs, outs, ns, send_sems, recv_sems, local_sems):
    n = len(srcs)
    x, y, c = _my_place()
    me = 4 * x + 2 * y + c

    def piece(i, dev):
        return srcs[i].at[dev] if i < ns else srcs[i]

    def copies(arriving):
        out = []
        for mask in range(1, N_DEV):
            px, py, pc = _peer(x, y, c, mask)
            dev = 4 * px + 2 * py + pc
            for i in range(n):
                k = (mask - 1) * n + i
                out.append(pltpu.make_async_remote_copy(
                    src_ref=piece(i, dev), dst_ref=outs[i].at[dev if arriving else me], send_sem=send_sems.at[k],
                    recv_sem=recv_sems.at[k], device_id=(px, py, pc), device_id_type=MESH))
        return out

    def local():
        return [pltpu.make_async_copy(piece(i, me), outs[i].at[me], local_sems.at[i]) for i in range(n)]

    def start():
        for cp in local() + copies(False):
            cp.start()

    def finish():
        for cp in copies(True):
            cp.wait_recv()
        for cp in copies(False):
            cp.wait_send()
        for cp in local():
            cp.wait()

    return start, finish


def exchange_halves(halves, replicated):
    n, nr = len(halves), len(replicated)
    streams = 8

    def body(*refs):
        srcs, rep_srcs, outs, rep_outs = refs[:n], refs[n:n + nr], refs[n + nr:2 * n + nr], refs[2 * n + nr:2 * (n + nr)]
        refs = refs[2 * (n + nr):]
        mine, theirs = refs[:n], refs[n:2 * n]
        send_sems, recv_sems, in_sems, out_sems = refs[2 * n:2 * n + 4]
        rep_start, rep_finish = _slab_exchange(rep_srcs, rep_outs, 0, *refs[2 * n + 4:])
        rep_start()
        x, y, c = _my_place()
        loads = [pltpu.make_async_copy(srcs[i], mine[i], in_sems.at[i]) for i in range(n)]
        for cp in loads:
            cp.start()
        for cp in loads:
            cp.wait()

        def chunk_copy(i, s):
            rows = halves[i].shape[0] // streams
            k = i * streams + s
            return pltpu.make_async_remote_copy(
                src_ref=mine[i].at[pl.ds(s * rows, rows)], dst_ref=theirs[i].at[pl.ds(s * rows, rows)],
                send_sem=send_sems.at[k], recv_sem=recv_sems.at[k], device_id=(x, y, 1 - c), device_id_type=MESH)

        sends = [chunk_copy(i, s) for i in range(n) for s in range(streams)]
        for cp in sends:
            cp.start()
        own = [pltpu.make_async_copy(mine[i], outs[i].at[c], out_sems.at[i]) for i in range(n)]
        for cp in own:
            cp.start()
        for cp in sends:
            cp.wait_recv()
        got = [pltpu.make_async_copy(theirs[i], outs[i].at[1 - c], out_sems.at[n + i]) for i in range(n)]
        for cp in got:
            cp.start()
        for cp in sends:
            cp.wait_send()
        for cp in own + got:
            cp.wait()
        rep_finish()

    vmem = [pltpu.VMEM(a.shape, a.dtype) for a in halves]
    out = pl.pallas_call(
        body, name="exchange_halves",
        in_specs=[HBM] * (n + nr), out_specs=[HBM] * (n + nr),
        out_shape=[jax.ShapeDtypeStruct((2,) + a.shape, a.dtype) for a in halves]
        + _slab_exchange_shapes([], replicated),
        scratch_shapes=vmem + vmem + [pltpu.SemaphoreType.DMA((n * streams,)), pltpu.SemaphoreType.DMA((n * streams,)),
                                      pltpu.SemaphoreType.DMA((n,)), pltpu.SemaphoreType.DMA((2 * n,))]
        + _slab_exchange_sems(nr),
        compiler_params=pltpu.CompilerParams(vmem_limit_bytes=VMEM_LIMIT),
    )(*halves, *replicated)
    return out[:n], out[n:]


def _pack_cols(pieces):
    offs, pos = [], 0
    for a in pieces:
        offs.append(pos)
        pos += a.shape[1]
    rows8 = [jnp.pad(a.astype(F32), ((0, 8 - a.shape[0]), (0, 0))) for a in pieces]
    return jnp.concatenate(rows8, axis=1), offs


def adamw_many(ws, gs, ms, vs):
    n = len(ws)

    def body(*refs):
        w_r, g_r, m_r, v_r = refs[:n], refs[n:2 * n], refs[2 * n:3 * n], refs[3 * n:4 * n]
        d_o, m_o, v_o = refs[4 * n:5 * n], refs[5 * n:6 * n], refs[6 * n:7 * n]
        for i in range(n):
            gg = g_r[i][...]
            nm = ADAM_B1 * m_r[i][...] + (1.0 - ADAM_B1) * gg
            nv = ADAM_B2 * v_r[i][...] + (1.0 - ADAM_B2) * (gg * gg)
            m_hat = nm / (1.0 - ADAM_B1 ** ADAM_STEP)
            v_hat = nv / (1.0 - ADAM_B2 ** ADAM_STEP)
            d_o[i][...] = -ADAM_LR * (m_hat / (jnp.sqrt(v_hat) + ADAM_EPS) + ADAM_WD * w_r[i][...])
            m_o[i][...] = nm
            v_o[i][...] = nv

    shapes = [jax.ShapeDtypeStruct(w.shape, F32) for w in ws]
    out = pl.pallas_call(body, name="adamw_small", out_shape=shapes * 3,
                         compiler_params=pltpu.CompilerParams(vmem_limit_bytes=VMEM_LIMIT))(*ws, *gs, *ms, *vs)
    return out[:n], out[n:2 * n], out[2 * n:]


def _lanes(vec, start):
    n = vec.shape[-1]
    return jnp.pad(vec.reshape(1, n).astype(F32), ((0, 0), (start, LANES - start - n)))


def kernel(x, norm_w, w_in, ssd_conv_w, ssd_conv_b, ssd_dt_bias, ssd_a_log, ssd_d, ssd_norm_w, gdn_conv_w, gdn_dt_bias, gdn_a_log, gdn_norm_w, w_out, final_norm_w, loss_target, m_norm_w, m_w_in, m_ssd_conv_w, m_ssd_conv_b, m_ssd_dt_bias, m_ssd_a_log, m_ssd_d, m_ssd_norm_w, m_gdn_conv_w, m_gdn_dt_bias, m_gdn_a_log, m_gdn_norm_w, m_w_out, m_final_norm_w, v_norm_w, v_w_in, v_ssd_conv_w, v_ssd_conv_b, v_ssd_dt_bias, v_ssd_a_log, v_ssd_d, v_ssd_norm_w, v_gdn_conv_w, v_gdn_dt_bias, v_gdn_a_log, v_gdn_norm_w, v_w_out, v_final_norm_w):
    xs = x[0]
    target = loss_target[0]
    chip = 2 * lax.axis_index("x") + lax.axis_index("y")
    w_in_shard, w_out_shard = w_in[0], w_out[0]
    in_cols = w_in_shard.shape[1]
    out_rows = w_out_shard.shape[0]

    g_in, g_out, g_cs, g_cg = gather_weights(
        [w_in_shard.astype(MXU_DTYPE), w_out_shard.astype(MXU_DTYPE)], [ssd_conv_w[0], gdn_conv_w[0]])
    w_in_full = jnp.concatenate([g_in[k] for k in range(N_CHIP)], axis=1)
    w_out_full = g_out.reshape(N_CHIP * out_rows, D_MODEL)
    cw_ssd = _ssd_perm(jnp.transpose(g_cs, (1, 0, 2)).reshape(4, SSD_CONV))
    cw_gdn = _gdn_perm(jnp.transpose(g_cg, (1, 0, 2)).reshape(4, GDN_CONV))
    cb_ssd = _ssd_perm(ssd_conv_b)
    cb_gdn = jnp.zeros((1, GDN_CONV), F32)
    o_xbc, o_dt, o_gate, o_qkv, o_ab = 1024, 2560, 2576, 3600, 6672
    w_main = jnp.concatenate([w_in_full[:, :o_xbc], w_in_full[:, o_gate:o_qkv], _gdn_perm(w_in_full[:, o_qkv:o_ab]),
                              _ssd_perm(w_in_full[:, o_xbc:o_dt])], axis=1)
    w_small = jnp.concatenate([w_in_full[:, o_dt:o_gate], w_in_full[:, o_ab:],
                               jnp.zeros((D_MODEL, LANES - 32), MXU_DTYPE)], axis=1)
    alog = _lanes(ssd_a_log, 0) + _lanes(gdn_a_log, LANE_GA)
    dtb = _lanes(ssd_dt_bias, 0) + _lanes(gdn_dt_bias, LANE_GA)
    dvec = _lanes(ssd_d, 0)
    fw = final_norm_w.reshape(1, D_MODEL)

    proj_main, proj_small, u = in_proj(xs, norm_w, w_main, w_small)
    proj_main, conv_out = in_proj_conv(proj_main, u, w_main, jnp.concatenate([cw_gdn, cw_ssd], axis=1),
                                       jnp.concatenate([cb_gdn, cb_ssd], axis=1))
    conv_ssd = conv_gdn = conv_out
    y_ssd, hist_ssd = ssd_fwd(conv_ssd, proj_main, proj_small, ssd_norm_w, alog, dtb, dvec)
    y_gdn, hist_gdn, tinv_gdn = gdn_fwd(conv_gdn, proj_main, proj_small, gdn_norm_w, alog, dtb)

    loss_blk, dhid, dy_ssd, dy_gdn, d_w_out, d_fw = out_proj_loss(xs, y_ssd, y_gdn, w_out_full, fw, target)
    dconv_ssd, dproj_main, dsmall_ssd, d_ssd_nw, d_alog_s, d_dtb_s, d_dvec = ssd_bwd(
        conv_ssd, proj_main, proj_small, ssd_norm_w, alog, dtb, dvec, hist_ssd, dy_ssd)
    dproj_main, dconv_gdn, dsmall_gdn, d_gdn_nw, d_alog_g, d_dtb_g = gdn_bwd(
        dproj_main, conv_gdn, proj_main, proj_small, gdn_norm_w, alog, dtb, hist_gdn, tinv_gdn, dy_gdn)
    dproj_main, dwb_ssd = conv_bwd(dproj_main, proj_main, COL_SSD, SSD_CONV, cw_ssd, cb_ssd, dconv_ssd, "conv_bwd_ssd")
    dproj_main, dwb_gdn = conv_bwd(dproj_main, proj_main, COL_GDN, GDN_CONV, cw_gdn, cb_gdn, dconv_gdn, "conv_bwd_gdn")
    slabs_out = d_w_out.reshape(N_DEV, out_rows // 2, D_MODEL).astype(COMM_DTYPE)
    d_w_main, d_w_small, (r_out,) = in_proj_bwd_w(u, dproj_main, dsmall_ssd, dsmall_gdn, [slabs_out])
    d_w_in = jnp.concatenate([d_w_main[:, :COL_GATE], _ssd_unperm(d_w_main[:, COL_SSD:]), d_w_small[:, 0:16],
                              d_w_main[:, COL_GATE:COL_GDN], _gdn_unperm(d_w_main[:, COL_GDN:COL_SSD]),
                              d_w_small[:, 16:32]], axis=1)
    d_w_in = jnp.stack([d_w_in[:, k * in_cols:(k + 1) * in_cols] for k in range(N_CHIP)])
    slabs_in = d_w_in.reshape(N_DEV, D_MODEL // 2, in_cols).astype(COMM_DTYPE)
    grad_x, d_norm_w, (r_in,) = in_proj_bwd_x(xs, norm_w, w_main, w_small, dproj_main, dsmall_ssd, dsmall_gdn,
                                               dhid, [slabs_in])
    d_alog, d_dtb = d_alog_s + d_alog_g, d_dtb_s + d_dtb_g
    packed, (o_nw, o_cs, o_cg, o_snw, o_fw, o_al, o_db, o_dv, o_gnw, o_loss) = _pack_cols([
        d_norm_w, _ssd_unperm(dwb_ssd), _gdn_unperm(dwb_gdn),
        d_ssd_nw.reshape(1, SSD_WIDTH), d_fw, d_alog, d_dtb, d_dvec, d_gdn_nw, loss_blk])

    half_in = sum_slabs(r_in, "sum_w_in")
    half_out = sum_slabs(r_out, "sum_w_out")
    (full_in, full_out), (r_small,) = exchange_halves([half_in, half_out], [packed])
    tot = sum_slabs(r_small, "sum_small")
    grad_w_in = full_in.reshape(D_MODEL, in_cols)
    grad_w_out = full_out.reshape(out_rows, D_MODEL)
    loss = tot[0, o_loss]
    sc, gc = ssd_conv_w.shape[2], gdn_conv_w.shape[2]
    row = lambda off, n, r=0: tot[r:r + 1, off:off + n]
    gs = [row(o_nw, D_MODEL),
          lax.dynamic_slice(tot, (0, o_cs + chip * sc), (4, sc)),
          row(o_cs, SSD_CONV, 4),
          row(o_db, SSD_HEADS), row(o_al, SSD_HEADS), row(o_dv, SSD_HEADS),
          row(o_snw, SSD_WIDTH),
          lax.dynamic_slice(tot, (0, o_cg + chip * gc), (4, gc)),
          row(o_db + LANE_GA, GDN_HEADS), row(o_al + LANE_GA, GDN_HEADS),
          row(o_gnw, GDN_DV), row(o_fw, D_MODEL)]

    names = ["norm_w", "ssd_conv_w", "ssd_conv_b", "ssd_dt_bias", "ssd_a_log", "ssd_d", "ssd_norm_w", "gdn_conv_w",
             "gdn_dt_bias", "gdn_a_log", "gdn_norm_w", "final_norm_w"]
    ws = [norm_w, ssd_conv_w, ssd_conv_b, ssd_dt_bias, ssd_a_log, ssd_d, ssd_norm_w, gdn_conv_w, gdn_dt_bias,
          gdn_a_log, gdn_norm_w, final_norm_w]
    ms = [m_norm_w, m_ssd_conv_w, m_ssd_conv_b, m_ssd_dt_bias, m_ssd_a_log, m_ssd_d, m_ssd_norm_w, m_gdn_conv_w,
          m_gdn_dt_bias, m_gdn_a_log, m_gdn_norm_w, m_final_norm_w]
    vs = [v_norm_w, v_ssd_conv_w, v_ssd_conv_b, v_ssd_dt_bias, v_ssd_a_log, v_ssd_d, v_ssd_norm_w, v_gdn_conv_w,
          v_gdn_dt_bias, v_gdn_a_log, v_gdn_norm_w, v_final_norm_w]
    shapes = [w.shape for w in ws]
    flat = lambda arrs: [a.reshape(g.shape) for a, g in zip(arrs, gs)]
    d_s, m_s, v_s = adamw_many(flat(ws), gs, flat(ms), flat(vs))
    back = lambda arrs: dict(zip(names, [a.reshape(s) for a, s in zip(arrs, shapes)]))
    delta, new_m, new_v, grads = back(d_s), back(m_s), back(v_s), back(gs)
    d_in, m_in, v_in = adamw(w_in_shard, grad_w_in, m_w_in[0], v_w_in[0], "adamw_w_in")
    d_out, m_out, v_out = adamw(w_out_shard, grad_w_out, m_w_out[0], v_w_out[0], "adamw_w_out")
    for tbl, a_in, a_out in ((grads, grad_w_in, grad_w_out), (delta, d_in, d_out), (new_m, m_in, m_out),
                             (new_v, v_in, v_out)):
        tbl["w_in"] = a_in[None]
        tbl["w_out"] = a_out[None]

    order = ["norm_w", "w_in", "ssd_conv_w", "ssd_conv_b", "ssd_dt_bias", "ssd_a_log", "ssd_d", "ssd_norm_w",
             "gdn_conv_w", "gdn_dt_bias", "gdn_a_log", "gdn_norm_w", "w_out", "final_norm_w"]
    return (loss.reshape(()), grad_x[None], *[grads[k] for k in order], *[delta[k] for k in order],
            *[new_m[k] for k in order], *[new_v[k] for k in order])
```

```python
import functools

import jax
import jax.numpy as jnp
from jax import lax
from jax.experimental import pallas as pl
from jax.experimental.pallas import tpu as pltpu

F32 = jnp.float32
MXU_DTYPE = jnp.bfloat16
COMM_DTYPE = jnp.bfloat16
MESH = pl.DeviceIdType.MESH

D_MODEL = 1024
CHUNK = 64
EPS = 1e-6
SSD_HEADS, SSD_GROUPS, SSD_STATE = 16, 2, 128
SSD_WIDTH, SSD_CONV = 1024, 1536
SSD_GW = SSD_WIDTH // SSD_GROUPS
SSD_GC = SSD_GW + 2 * SSD_STATE
GDN_HEADS, GDN_DK, GDN_DV = 8, 128, 128
GDN_W, GDN_CONV = 1024, 3072
GDN_HC = 2 * GDN_DK + GDN_DV
IN_DIM = 6688
MAIN = 6656
LANES = 128
COL_Z, COL_GATE, COL_GDN, COL_SSD = 0, 1024, 2048, 5120
COL_CONV = COL_GDN
GDN_HB = 8
GDN_CB = 2
LANE_GA, LANE_GB = 16, 24
N_DEV, N_CHIP = 8, 4
VMEM_LIMIT = 52 * 1024 * 1024

ADAM_LR, ADAM_B1, ADAM_B2, ADAM_EPS, ADAM_WD, ADAM_STEP = 0.001, 0.9, 0.999, 1e-08, 0.01, 10


def _ssd_perm(a):
    lead, nb = a.shape[:-1], SSD_GROUPS * SSD_STATE
    x = a[..., :SSD_WIDTH].reshape(*lead, SSD_GROUPS, SSD_GW)
    b = a[..., SSD_WIDTH:SSD_WIDTH + nb].reshape(*lead, SSD_GROUPS, SSD_STATE)
    c = a[..., SSD_WIDTH + nb:].reshape(*lead, SSD_GROUPS, SSD_STATE)
    return jnp.concatenate([x, b, c], axis=-1).reshape(*lead, SSD_CONV)


def _ssd_unperm(a):
    lead = a.shape[:-1]
    g = a.reshape(*lead, SSD_GROUPS, SSD_GC)
    parts = [g[..., :SSD_GW], g[..., SSD_GW:SSD_GW + SSD_STATE], g[..., SSD_GW + SSD_STATE:]]
    return jnp.concatenate([p.reshape(*lead, -1) for p in parts], axis=-1)


def _gdn_perm(a):
    lead = a.shape[:-1]
    return jnp.swapaxes(a.reshape(*lead, 3, GDN_HEADS, GDN_DK), -3, -2).reshape(*lead, GDN_CONV)


def _gdn_unperm(a):
    lead = a.shape[:-1]
    return jnp.swapaxes(a.reshape(*lead, GDN_HEADS, 3, GDN_DK), -3, -2).reshape(*lead, GDN_CONV)


def _split(a, n):
    parts, rest = [], a.astype(F32)
    for i in range(n):
        p = rest.astype(MXU_DTYPE)
        parts.append(p)
        if i < n - 1:
            rest = rest - p.astype(F32)
    return parts


def _raw_dot(a, b, ca, cb, mode="bf16"):
    d = lambda u, v: lax.dot_general(u, v, (((ca,), (cb,)), ((), ())), preferred_element_type=F32)
    if mode == "bf16":
        return d(a.astype(MXU_DTYPE), b.astype(MXU_DTYPE))
    if mode == "x3":
        (ah, al), (bh, bl) = _split(a, 2), _split(b, 2)
        return d(ah, bh) + (d(ah, bl) + d(al, bh))
    if mode == "sel_a":
        a0 = a.astype(MXU_DTYPE)
        b1, b2, b3 = _split(b, 3)
        return d(a0, b1) + (d(a0, b2) + d(a0, b3))
    assert mode == "sel_b", mode
    b0 = b.astype(MXU_DTYPE)
    a1, a2, a3 = _split(a, 3)
    return d(a1, b0) + (d(a2, b0) + d(a3, b0))


@functools.partial(jax.custom_vjp, nondiff_argnums=(2,))
def mm_nn(a, b, mode="bf16"):
    return _raw_dot(a, b, 1, 0, mode)


@functools.partial(jax.custom_vjp, nondiff_argnums=(2,))
def mm_nt(a, b, mode="bf16"):
    return _raw_dot(a, b, 1, 1, mode)


@functools.partial(jax.custom_vjp, nondiff_argnums=(2,))
def mm_tn(a, b, mode="bf16"):
    return _raw_dot(a, b, 0, 0, mode)


_SAME = {"bf16": ("bf16", "bf16"), "x3": ("x3", "x3")}
_NN_BWD = dict(_SAME, sel_a=("bf16", "sel_a"), sel_b=("sel_b", "bf16"))
_NT_BWD = dict(_SAME, sel_a=("bf16", "sel_b"), sel_b=("sel_b", "bf16"))
_TN_BWD = dict(_SAME, sel_a=("bf16", "sel_a"), sel_b=("sel_a", "bf16"))
mm_nn.defvjp(lambda a, b, m: (_raw_dot(a, b, 1, 0, m), (a, b)),
             lambda m, r, g: (mm_nt(g, r[1], _NN_BWD[m][0]), mm_tn(r[0], g, _NN_BWD[m][1])))
mm_nt.defvjp(lambda a, b, m: (_raw_dot(a, b, 1, 1, m), (a, b)),
             lambda m, r, g: (mm_nn(g, r[1], _NT_BWD[m][0]), mm_tn(g, r[0], _NT_BWD[m][1])))
mm_tn.defvjp(lambda a, b, m: (_raw_dot(a, b, 0, 0, m), (a, b)),
             lambda m, r, g: (mm_nt(r[1], g, _TN_BWD[m][0]), mm_nn(r[0], g, _TN_BWD[m][1])))


@jax.custom_jvp
def sigmoid(x):
    return 1.0 / (1.0 + jnp.exp(-x))


@sigmoid.defjvp
def _sigmoid_jvp(p, t):
    s = sigmoid(p[0])
    return s, t[0] * s * (1.0 - s)


@jax.custom_jvp
def softplus(x):
    return jnp.maximum(x, 0.0) + jnp.log(1.0 + jnp.exp(-jnp.abs(x)))


@softplus.defjvp
def _softplus_jvp(p, t):
    return softplus(p[0]), t[0] * sigmoid(p[0])


def silu(x):
    return x * sigmoid(x)


def rmsnorm(x, w):
    return x * lax.rsqrt(jnp.mean(x * x, axis=-1, keepdims=True) + EPS) * w


def _iota(shape, dim):
    return lax.broadcasted_iota(jnp.int32, shape, dim)


def _halves():
    lane = _iota((1, LANES), 1) >> 6
    return _ind(lane == 0), _ind(lane == 1)


def _block_diag(pair):
    h0, h1 = _halves()
    return jnp.concatenate([pair * h0, pair * h1], axis=0)


def _tri_inv_impl(mats):
    r, c = _iota((CHUNK, LANES), 0), _iota((CHUNK, LANES), 1) & (CHUNK - 1)
    eye = _ind(r == c)
    blockdiag = _ind((r >> 4) == (c >> 4))
    dot = lambda u, v: _raw_dot(u, _block_diag(v), 1, 0, "x3")
    dot1 = lambda u, v: _raw_dot(u, _block_diag(v), 1, 0)
    each = lambda f, *ls: [f(*xs) for xs in zip(*ls)]
    dg = each(lambda a: a * blockdiag, mats)
    off = each(lambda a, d: a - d, mats, dg)
    m = each(lambda d: -d, dg)
    p = each(lambda x: eye + x, m)
    pw = m
    for _ in range(3):
        pw = each(lambda x: dot1(x, x), pw)
        p = each(lambda x, y: x + dot1(x, y), p, pw)
    e = each(dot, p, off)
    e2 = each(lambda x: dot1(x, x), e)
    q = each(lambda x: eye - x, e)
    q = each(lambda x, y: x + dot1(x, y), q, e2)
    return each(dot, q, p)


def _tri_inv_bwd(ts, gs):
    h0, h1 = _halves()
    x = [mm_nt(g, _block_diag(t)) for g, t in zip(gs, ts)]
    full = [mm_tn(t, y) for t, y in zip(ts, x)]
    return [-(f[:CHUNK] * h0 + f[CHUNK:] * h1) for f in full]


@jax.custom_vjp
def tri_inv(mats):
    return _tri_inv_impl(mats)


def _tri_inv_fwd(mats):
    ts = _tri_inv_impl(mats)
    return ts, ts


tri_inv.defvjp(_tri_inv_fwd, lambda ts, gs: (_tri_inv_bwd(ts, gs),))


@jax.custom_vjp
def tri_inv_saved(mats, ts):
    del mats
    return ts


tri_inv_saved.defvjp(lambda mats, ts: (ts, ts),
                     lambda ts, gs: (_tri_inv_bwd(ts, gs), [jnp.zeros_like(t) for t in ts]))


def _ind(cond):
    return jnp.where(cond, 1.0, 0.0).astype(F32)


def _chunk_masks():
    r, c = _iota((CHUNK, CHUNK), 0), _iota((CHUNK, CHUNK), 1)
    return _ind(r >= c), _ind(r > c), _ind(r == c), _ind(_iota((CHUNK, 1), 0) == CHUNK - 1)


def _log_decay_cumsum(small, alog, dtb, tri):
    sp = softplus(small + dtb)
    la = -jnp.exp(alog) * sp
    return sp, mm_nn(tri, la, "sel_a")


def _col_of(x, lane):
    return jnp.sum(x * _ind(_iota((1, LANES), 1) == lane), axis=1, keepdims=True)


def _decay_matrix(col, tri, eye):
    row = jnp.sum(col * eye, axis=0, keepdims=True)
    return jnp.exp((col - row) * tri) * tri


def _pair_masks():
    r, c = _iota((CHUNK, LANES), 0), _iota((CHUNK, LANES), 1)
    c6 = c & (CHUNK - 1)
    return _ind(r >= c6), _ind(r > c6), (_ind(c == r), _ind(c == r + CHUNK))


def _decay_pair(col_a, col_b, tri_w, eye_w):
    h0, h1 = _halves()
    col = col_a * h0 + col_b * h1
    row = jnp.sum(col_a * eye_w[0] + col_b * eye_w[1], axis=0, keepdims=True)
    return jnp.exp((col - row) * tri_w) * tri_w


def gdn_chunk(h0, qs, ks, vs, smalls, gates, normw, alog, dtb, states, saved_t=None):
    tri, _, _, last = _chunk_masks()
    tri_w, strict_w, eye_w = _pair_masks()
    nh = len(qs[0])
    flat = lambda xss: [x for xs in xss for x in xs]
    lacs = [_log_decay_cumsum(sm, alog, dtb, tri)[1] for sm in smalls]
    qs, ks, vs, gates = flat(qs), flat(ks), flat(vs), flat(gates)
    heads, pairs = range(len(qs)), range(len(qs) // 2)
    each = lambda f, *ls: [f(*xs) for xs in zip(*ls)]
    ab = lambda xs, p: (xs[2 * p], xs[2 * p + 1])
    stack = lambda xs: jnp.concatenate(xs, axis=0)
    gc = [_col_of(lacs[i // nh], LANE_GA + h0 + i % nh) for i in heads]
    beta = [sigmoid(_col_of(smalls[i // nh], LANE_GB + h0 + i % nh)) for i in heads]
    decay = [_decay_pair(*ab(gc, p), tri_w, eye_w) for p in pairs]
    gl = each(lambda x: jnp.sum(x * last, axis=0, keepdims=True), gc)
    q = each(lambda x: x * lax.rsqrt(jnp.sum(x * x, axis=-1, keepdims=True) + EPS) * (GDN_DK ** -0.5), qs)
    k = each(lambda x: x * lax.rsqrt(jnp.sum(x * x, axis=-1, keepdims=True) + EPS), ks)
    kb = each(lambda x, b: x * b, k, beta)
    eg = each(jnp.exp, gc)
    zero = jnp.zeros((CHUNK, GDN_DK), F32)
    k_bd = [stack([join_lanes([k[2 * p], zero]), join_lanes([zero, k[2 * p + 1]])]) for p in pairs]
    a = [mm_nt(join_lanes(list(ab(kb, p))), k_bd[p]) * (decay[p] * strict_w) for p in pairs]
    t = tri_inv(a) if saved_t is None else tri_inv_saved(a, saved_t)
    attn = [mm_nt(join_lanes(list(ab(q, p))), k_bd[p]) * decay[p] for p in pairs]
    rhs = [stack([join_lanes([vs[h] * beta[h], kb[h] * eg[h]]) for h in (2 * p, 2 * p + 1)]) for p in pairs]
    uw = [mm_nn(_block_diag(t[p]), rhs[p]) for p in pairs]
    uw = [x for p in pairs for x in split_rows(uw[p])]
    u, w = zip(*[split_lanes(x) for x in uw])
    ys = []
    for c in range(len(smalls)):
        hs = range(c * nh, (c + 1) * nh)
        v_new = [u[i] - mm_nn(w[i], states[i % nh]) for i in hs]
        av = [mm_nn(_block_diag(attn[c * nh // 2 + p]), stack(list(ab(v_new, p)))) for p in range(nh // 2)]
        av = [x for y in av for x in split_rows(y)]
        o = [mm_nn(q[i] * eg[i], states[i % nh]) + av[i % nh] for i in hs]
        states = [states[i % nh] * jnp.exp(gl[i]) + mm_tn(k[i] * jnp.exp(gl[i] - gc[i]), v_new[i % nh]) for i in hs]
        ys.append([rmsnorm(o[i % nh], normw) * silu(gates[i]) for i in hs])
    return ys, states, t


@jax.custom_vjp
def split_rows(x):
    n = x.shape[0] // 2
    return [x[:n], x[n:]]


split_rows.defvjp(lambda x: (split_rows(x), None), lambda _, gs: (jnp.concatenate(gs, axis=0),))


@jax.custom_vjp
def split_lanes(x):
    return [x[:, i * LANES:(i + 1) * LANES] for i in range(x.shape[1] // LANES)]


@jax.custom_vjp
def join_lanes(xs):
    return jnp.concatenate(xs, axis=1)


split_lanes.defvjp(lambda x: (split_lanes(x), None), lambda _, gs: (join_lanes(gs),))
join_lanes.defvjp(lambda xs: (join_lanes(xs), None), lambda _, g: (split_lanes(g),))


def ssd_chunk(xs, bm, cm, z, small, normw, alog, dtb, dvec, state):
    tri, _, eye, last = _chunk_masks()
    hpg = SSD_HEADS // SSD_GROUPS
    groups = range(len(xs))
    each = lambda f, *ls: [f(*a) for a in zip(*ls)]
    sp, lac = _log_decay_cumsum(small, alog, dtb, tri)
    lac_last = jnp.sum(lac * last, axis=0, keepdims=True)
    sel = [_ind(_iota((LANES, SSD_GW), 0) == g * hpg + (_iota((LANES, SSD_GW), 1) >> 6)) for g in groups]
    expand = lambda v, mode="sel_b": [mm_nn(v, s, mode) for s in sel]
    dt_e, elac_e, toend_e = expand(sp, "bf16"), expand(jnp.exp(lac), "bf16"), expand(jnp.exp(lac_last - lac), "bf16")
    row8, row8e = _iota((8, 1), 0), _iota((8, 1), 0)
    two_e = expand(_ind(row8 == 0) * dvec + _ind(row8 == 1) * jnp.exp(lac_last))
    d_e = each(lambda v: jnp.sum(v * _ind(row8e == 0), axis=0, keepdims=True), two_e)
    chunk_e = each(lambda v: jnp.sum(v * _ind(row8e == 1), axis=0, keepdims=True), two_e)
    xdt = each(lambda a, b: a * b, xs, dt_e)
    y = each(lambda c_, st, el, x_, d_: mm_nn(c_, st) * el + x_ * d_, cm, state, elac_e, xs, d_e)
    r, c = _iota((CHUNK, LANES), 0), _iota((CHUNK, LANES), 1)
    tri_w = _ind(r >= (c & (CHUNK - 1)))
    eye_w = [_ind(c == r), _ind(c == r + CHUNK)]
    half = [_ind((_iota((1, LANES), 1) >> 6) == s) for s in range(2)]

    def decay_pair(col_a, col_b):
        col = col_a * half[0] + col_b * half[1]
        row = jnp.sum(col_a * eye_w[0] + col_b * eye_w[1], axis=0, keepdims=True)
        return jnp.exp((col - row) * tri_w) * tri_w

    pairs = range(hpg // 2)
    cb_w = each(lambda c_, b_: mm_nt(c_, jnp.concatenate([b_, b_], axis=0)), cm, bm)
    x_pairs = each(split_lanes, xdt)
    lms = [[decay_pair(_col_of(lac, g * hpg + 2 * p), _col_of(lac, g * hpg + 2 * p + 1)) for p in pairs]
           for g in groups]
    stacked = [[jnp.concatenate([x_pairs[g][p] * half[0], x_pairs[g][p] * half[1]], axis=0) for p in pairs]
               for g in groups]
    terms = [[mm_nn(cb_w[g] * lms[g][p], stacked[g][p]) for p in pairs] for g in groups]
    y = [y[g] + join_lanes(terms[g]) for g in groups]
    new_state = each(lambda st, ce, b_, xd, te: st * ce + mm_tn(b_, xd * te), state, chunk_e, bm, xdt, toend_e)
    out = each(lambda y_, z_, nw: rmsnorm(y_ * silu(z_), nw), y, z, normw)
    return out, new_state


def _params(sem=None):
    return pltpu.CompilerParams(dimension_semantics=sem, vmem_limit_bytes=VMEM_LIMIT)


def _full(shape):
    n = len(shape)
    return pl.BlockSpec(shape, lambda *_: (0,) * n)


ANY = pl.BlockSpec(memory_space=pl.ANY)
HBM = pl.BlockSpec(memory_space=pltpu.HBM)


def in_proj(x, normw, w_main, w_small):
    t = x.shape[0]
    tm, tn = min(1024, t), 512

    def body(x_ref, nw_ref, wm_ref, ws_ref, pm_ref, ps_ref, u_ref):
        @pl.when(pl.program_id(1) == 0)
        def _():
            u = rmsnorm(x_ref[...], nw_ref[...]).astype(MXU_DTYPE)
            u_ref[...] = u
            ps_ref[...] = _raw_dot(u, ws_ref[...], 1, 0)
        pm_ref[...] = _raw_dot(u_ref[...], wm_ref[...], 1, 0)

    return pl.pallas_call(
        body, name="in_proj", grid=(t // tm, COL_CONV // tn),
        in_specs=[pl.BlockSpec((tm, D_MODEL), lambda i, j: (i, 0)), _full((1, D_MODEL)),
                  pl.BlockSpec((D_MODEL, tn), lambda i, j: (0, j)), _full((D_MODEL, LANES))],
        out_specs=[pl.BlockSpec((tm, tn), lambda i, j: (i, j)), pl.BlockSpec((tm, LANES), lambda i, j: (i, 0)),
                   pl.BlockSpec((tm, D_MODEL), lambda i, j: (i, 0))],
        out_shape=[jax.ShapeDtypeStruct((t, MAIN), F32), jax.ShapeDtypeStruct((t, LANES), F32),
                   jax.ShapeDtypeStruct((t, D_MODEL), MXU_DTYPE)],
        compiler_params=_params(("arbitrary", "arbitrary")),
    )(x, normw, w_main, w_small)


CONV_TC = 512
HALO = 8


def _shift_down(cur, prev, s):
    rolled = pltpu.roll(cur, s, 0)
    top = jnp.where(_iota((HALO, cur.shape[1]), 0) < s, pltpu.roll(prev, s, 0), rolled[:HALO])
    if cur.shape[0] == HALO:
        return top
    return jnp.concatenate([top, rolled[HALO:]], axis=0)


def _shift_up(cur, nxt, s):
    n = cur.shape[0]
    rolled = pltpu.roll(cur, n - s, 0)
    bot = jnp.where(_iota((HALO, cur.shape[1]), 0) >= HALO - s, pltpu.roll(nxt, HALO - s, 0), rolled[n - HALO:])
    return jnp.concatenate([rolled[:n - HALO], bot], axis=0)


def _conv_pre(cur, prev, w_ref, b):
    acc = cur * w_ref[3:4, :] + b
    shifted = [cur]
    for s in (1, 2, 3):
        sh = _shift_down(cur, prev, s)
        shifted.append(sh)
        acc = acc + sh * w_ref[3 - s:4 - s, :]
    return acc, shifted


def in_proj_conv(proj_main, u, w_main, w, b):
    t = u.shape[0]
    tm, tn = min(1024, t), CONV_TC
    rc = min(256, tm)
    c0, nj = COL_CONV // tn, (MAIN - COL_CONV) // tn

    def body(alias_ref, u_ref, wm_ref, w_ref, b_ref, pm_ref, out_ref, halo_ref):
        del alias_ref
        j = pl.program_id(1)

        @pl.when(pl.program_id(0) == 0)
        def _():
            halo_ref[j] = jnp.zeros((HALO, tn), F32)

        prev = halo_ref[j]
        for r in range(tm // rc):
            rows = pl.ds(r * rc, rc)
            p = _raw_dot(u_ref[rows, :], wm_ref[...], 1, 0)
            pm_ref[rows, :] = p
            pre, _ = _conv_pre(p, prev, w_ref, b_ref[...])
            out_ref[rows, :] = silu(pre)
            prev = p[rc - HALO:]
        halo_ref[j] = prev

    return pl.pallas_call(
        body, name="in_proj_conv", grid=(t // tm, nj),
        in_specs=[ANY, pl.BlockSpec((tm, D_MODEL), lambda i, j: (i, 0)),
                  pl.BlockSpec((D_MODEL, tn), lambda i, j: (0, c0 + j)),
                  pl.BlockSpec((4, tn), lambda i, j: (0, j)), pl.BlockSpec((1, tn), lambda i, j: (0, j))],
        out_specs=[pl.BlockSpec((tm, tn), lambda i, j: (i, c0 + j)), pl.BlockSpec((tm, tn), lambda i, j: (i, j))],
        out_shape=[jax.ShapeDtypeStruct(proj_main.shape, F32), jax.ShapeDtypeStruct((t, MAIN - COL_CONV), F32)],
        scratch_shapes=[pltpu.VMEM((nj, HALO, tn), F32)],
        input_output_aliases={0: 0},
        compiler_params=_params(("arbitrary", "arbitrary")),
    )(proj_main, u, w_main, w, b)


def _dsilu(pre):
    sg = sigmoid(pre)
    return sg * (1.0 + pre * (1.0 - sg))


def conv_bwd(dproj_main, proj_main, col0, width, w, b, dout, name):
    t = proj_main.shape[0]
    tt, c0 = min(512, t), col0 // CONV_TC
    nt = t // tt
    after = lambda i: jnp.minimum((i + 1) * (tt // HALO), t // HALO - 1)

    def body(alias_ref, cur_ref, prev_ref, nxt_ref, w_ref, b_ref, do_ref, do_nxt_ref, dx_ref, dwb_ref):
        del alias_ref
        i = pl.program_id(1)
        cur, bias = cur_ref[...], b_ref[...]
        prev = jnp.where(i > 0, prev_ref[...], 0.0)
        pre, shifted = _conv_pre(cur, prev, w_ref, bias)
        dpre = do_ref[...] * _dsilu(pre)
        pre_nxt, _ = _conv_pre(nxt_ref[...], cur[tt - HALO:], w_ref, bias)
        dpre_nxt = jnp.where(i < nt - 1, do_nxt_ref[...] * _dsilu(pre_nxt), 0.0)
        dx = dpre * w_ref[3:4, :]
        for s in (1, 2, 3):
            dx = dx + _shift_up(dpre, dpre_nxt, s) * w_ref[3 - s:4 - s, :]
        dx_ref[...] = dx.astype(dx_ref.dtype)
        row = _iota((HALO, CONV_TC), 0)
        upd = jnp.where(row == 4, jnp.sum(dpre, axis=0, keepdims=True), 0.0)
        for s in range(4):
            upd = upd + jnp.where(row == 3 - s, jnp.sum(dpre * shifted[s], axis=0, keepdims=True), 0.0)
        _accumulate(dwb_ref, i == 0, upd)

    return pl.pallas_call(
        body, name=name, grid=(width // CONV_TC, nt),
        in_specs=[ANY, pl.BlockSpec((tt, CONV_TC), lambda j, i: (i, c0 + j)),
                  pl.BlockSpec((HALO, CONV_TC), lambda j, i: (jnp.maximum(i * (tt // HALO) - 1, 0), c0 + j)),
                  pl.BlockSpec((HALO, CONV_TC), lambda j, i: (after(i), c0 + j)),
                  pl.BlockSpec((4, CONV_TC), lambda j, i: (0, j)), pl.BlockSpec((1, CONV_TC), lambda j, i: (0, j)),
                  pl.BlockSpec((tt, CONV_TC), lambda j, i: (i, j)),
                  pl.BlockSpec((HALO, CONV_TC), lambda j, i: (after(i), j))],
        out_specs=[pl.BlockSpec((tt, CONV_TC), lambda j, i: (i, c0 + j)),
                   pl.BlockSpec((HALO, CONV_TC), lambda j, i: (0, j))],
        out_shape=[jax.ShapeDtypeStruct(dproj_main.shape, dproj_main.dtype), jax.ShapeDtypeStruct((HALO, width), F32)],
        input_output_aliases={0: 0},
        compiler_params=_params(("arbitrary", "arbitrary")),
    )(dproj_main, proj_main, proj_main, proj_main, w, b, dout, dout)


def _ssd_parts(xbc_ref):
    part = lambda o, w: [xbc_ref[:, g * SSD_GC + o:g * SSD_GC + o + w] for g in range(SSD_GROUPS)]
    return part(0, SSD_GW), part(SSD_GW, SSD_STATE), part(SSD_GW + SSD_STATE, SSD_STATE)


def _group_cols(ref):
    return [ref[:, g * SSD_GW:(g + 1) * SSD_GW] for g in range(SSD_GROUPS)]


def _chunk_rows(c):
    return slice(c * CHUNK, (c + 1) * CHUNK)


def _gdn_parts(qkv_ref):
    part = lambda o: [[qkv_ref[_chunk_rows(c), j * GDN_HC + o:j * GDN_HC + o + GDN_DK] for j in range(GDN_HB)]
                      for c in range(GDN_CB)]
    return part(0), part(GDN_DK), part(2 * GDN_DK)


def _head_cols(ref):
    return [[ref[_chunk_rows(c), j * GDN_DV:(j + 1) * GDN_DV] for j in range(GDN_HB)] for c in range(GDN_CB)]


def _chunk_blocks(ref):
    return [ref[_chunk_rows(c), :] for c in range(GDN_CB)]


def _first_head():
    return 0 if GDN_HB == GDN_HEADS else pl.program_id(1) * GDN_HB


def ssd_fwd(conv_ssd, proj_main, proj_small, normw, alog, dtb, dvec):
    t = conv_ssd.shape[0]
    nc = t // CHUNK

    groups = range(SSD_GROUPS)

    def body(xbc_ref, z_ref, sm_ref, nw_ref, al_ref, db_ref, dv_ref, y_ref, hist_ref, state_ref):
        @pl.when(pl.program_id(0) == 0)
        def _():
            state_ref[...] = jnp.zeros(state_ref.shape, F32)

        states = [state_ref[g] for g in groups]
        for g in groups:
            hist_ref[0, g] = states[g]
        ys, new_states = ssd_chunk(*_ssd_parts(xbc_ref), _group_cols(z_ref), sm_ref[...], _group_cols(nw_ref),
                                   al_ref[...], db_ref[...], dv_ref[...], states)
        for g in groups:
            y_ref[:, g * SSD_GW:(g + 1) * SSD_GW] = ys[g].astype(MXU_DTYPE)
            state_ref[g] = new_states[g]

    return pl.pallas_call(
        body, name="ssd_fwd", grid=(nc,),
        in_specs=[pl.BlockSpec((CHUNK, SSD_CONV), lambda c: (c, (COL_SSD - COL_CONV) // SSD_CONV)),
                  pl.BlockSpec((CHUNK, SSD_WIDTH), lambda c: (c, COL_Z // SSD_WIDTH)),
                  pl.BlockSpec((CHUNK, LANES), lambda c: (c, 0)),
                  _full((1, SSD_WIDTH)), _full((1, LANES)), _full((1, LANES)), _full((1, LANES))],
        out_specs=[pl.BlockSpec((CHUNK, SSD_WIDTH), lambda c: (c, 0)),
                   pl.BlockSpec((1, SSD_GROUPS, SSD_STATE, SSD_GW), lambda c: (c, 0, 0, 0))],
        out_shape=[jax.ShapeDtypeStruct((t, SSD_WIDTH), MXU_DTYPE),
                   jax.ShapeDtypeStruct((nc, SSD_GROUPS, SSD_STATE, SSD_GW), F32)],
        scratch_shapes=[pltpu.VMEM((SSD_GROUPS, SSD_STATE, SSD_GW), F32)],
        compiler_params=_params(("arbitrary",)),
    )(conv_ssd, proj_main, proj_small, normw, alog, dtb, dvec)


def _accumulate(ref, first, value):
    @pl.when(first)
    def _():
        ref[...] = value

    @pl.when(jnp.logical_not(first))
    def _():
        ref[...] += value


def ssd_bwd(conv_ssd, proj_main, proj_small, normw, alog, dtb, dvec, hist, dy):
    t = conv_ssd.shape[0]
    nc = t // CHUNK
    rev = lambda c: nc - 1 - c
    groups = range(SSD_GROUPS)

    def body(xbc_ref, z_ref, sm_ref, nw_ref, al_ref, db_ref, dv_ref, hist_ref, dy_ref,
             dxbc_ref, dz_ref, dsm_ref, dnw_ref, dal_ref, ddb_ref, ddv_ref, dstate_ref):
        first = pl.program_id(0) == 0

        @pl.when(first)
        def _():
            dstate_ref[...] = jnp.zeros(dstate_ref.shape, F32)

        _, vjp = jax.vjp(ssd_chunk, *_ssd_parts(xbc_ref), _group_cols(z_ref), sm_ref[...], _group_cols(nw_ref),
                         al_ref[...], db_ref[...], dv_ref[...], [hist_ref[0, g] for g in groups])
        dxs, dbm, dcm, dz, dsm, dnw, dal, ddb, ddv, dstate = vjp(
            (_group_cols(dy_ref), [dstate_ref[g] for g in groups]))
        for g in groups:
            base = g * SSD_GC
            dxbc_ref[:, base:base + SSD_GW] = dxs[g]
            dxbc_ref[:, base + SSD_GW:base + SSD_GW + SSD_STATE] = dbm[g]
            dxbc_ref[:, base + SSD_GW + SSD_STATE:base + SSD_GC] = dcm[g]
            dz_ref[:, g * SSD_GW:(g + 1) * SSD_GW] = dz[g].astype(dz_ref.dtype)
            dstate_ref[g] = dstate[g]
        dsm_ref[...] = dsm
        _accumulate(dnw_ref, first, join_lanes(dnw))
        _accumulate(dal_ref, first, dal)
        _accumulate(ddb_ref, first, ddb)
        _accumulate(ddv_ref, first, ddv)

    return pl.pallas_call(
        body, name="ssd_bwd", grid=(nc,),
        in_specs=[pl.BlockSpec((CHUNK, SSD_CONV), lambda c: (rev(c), (COL_SSD - COL_CONV) // SSD_CONV)),
                  pl.BlockSpec((CHUNK, SSD_WIDTH), lambda c: (rev(c), COL_Z // SSD_WIDTH)),
                  pl.BlockSpec((CHUNK, LANES), lambda c: (rev(c), 0)),
                  _full((1, SSD_WIDTH)), _full((1, LANES)), _full((1, LANES)), _full((1, LANES)),
                  pl.BlockSpec((1, SSD_GROUPS, SSD_STATE, SSD_GW), lambda c: (rev(c), 0, 0, 0)),
                  pl.BlockSpec((CHUNK, SSD_WIDTH), lambda c: (rev(c), 0))],
        out_specs=[pl.BlockSpec((CHUNK, SSD_CONV), lambda c: (rev(c), 0)),
                   pl.BlockSpec((CHUNK, SSD_WIDTH), lambda c: (rev(c), COL_Z // SSD_WIDTH)),
                   pl.BlockSpec((CHUNK, LANES), lambda c: (rev(c), 0)),
                   _full((1, SSD_WIDTH)), _full((1, LANES)), _full((1, LANES)), _full((1, LANES))],
        out_shape=[jax.ShapeDtypeStruct((t, SSD_CONV), F32), jax.ShapeDtypeStruct((t, MAIN), MXU_DTYPE),
                   jax.ShapeDtypeStruct((t, LANES), F32), jax.ShapeDtypeStruct((1, SSD_WIDTH), F32),
                   jax.ShapeDtypeStruct((1, LANES), F32), jax.ShapeDtypeStruct((1, LANES), F32),
                   jax.ShapeDtypeStruct((1, LANES), F32)],
        scratch_shapes=[pltpu.VMEM((SSD_GROUPS, SSD_STATE, SSD_GW), F32)],
        compiler_params=_params(("arbitrary",)),
    )(conv_ssd, proj_main, proj_small, normw, alog, dtb, dvec, hist, dy)


def gdn_fwd(conv_gdn, proj_main, proj_small, normw, alog, dtb):
    t = conv_gdn.shape[0]
    hb, cb = GDN_HB, GDN_CB
    rows = CHUNK * cb
    ns = t // rows
    gate_blk = COL_GATE // (GDN_DV * hb)

    def body(qkv_ref, gate_ref, sm_ref, nw_ref, al_ref, db_ref, y_ref, hist_ref, t_ref, state_ref):
        h0 = _first_head()

        @pl.when(pl.program_id(0) == 0)
        def _():
            for j in range(hb):
                state_ref[h0 + j] = jnp.zeros((GDN_DK, GDN_DV), F32)

        states = [state_ref[h0 + j] for j in range(hb)]
        for j in range(hb):
            hist_ref[0, j] = states[j]
        qs, ks, vs = _gdn_parts(qkv_ref)
        ys, new_states, ts = gdn_chunk(h0, qs, ks, vs, _chunk_blocks(sm_ref), _head_cols(gate_ref), nw_ref[...],
                                       al_ref[...], db_ref[...], states)
        for c in range(cb):
            for j in range(hb):
                y_ref[_chunk_rows(c), j * GDN_DV:(j + 1) * GDN_DV] = ys[c][j].astype(MXU_DTYPE)
        for j in range(hb):
            state_ref[h0 + j] = new_states[j]
        for p in range(cb * hb // 2):
            t_ref[0, p] = ts[p]

    return pl.pallas_call(
        body, name="gdn_fwd", grid=(ns, GDN_HEADS // hb),
        in_specs=[pl.BlockSpec((rows, GDN_HC * hb), lambda c, h: (c, h)),
                  pl.BlockSpec((rows, GDN_DV * hb), lambda c, h: (c, gate_blk + h)),
                  pl.BlockSpec((rows, LANES), lambda c, h: (c, 0)),
                  _full((1, GDN_DV)), _full((1, LANES)), _full((1, LANES))],
        out_specs=[pl.BlockSpec((rows, GDN_DV * hb), lambda c, h: (c, h)),
                   pl.BlockSpec((1, hb, GDN_DK, GDN_DV), lambda c, h: (c, h, 0, 0)),
                   pl.BlockSpec((1, cb * hb // 2, CHUNK, LANES), lambda c, h: (c, h, 0, 0))],
        out_shape=[jax.ShapeDtypeStruct((t, GDN_W), MXU_DTYPE),
                   jax.ShapeDtypeStruct((ns, GDN_HEADS, GDN_DK, GDN_DV), F32),
                   jax.ShapeDtypeStruct((ns, cb * GDN_HEADS // 2, CHUNK, LANES), F32)],
        scratch_shapes=[pltpu.VMEM((GDN_HEADS, GDN_DK, GDN_DV), F32)],
        compiler_params=_params(("arbitrary", "arbitrary")),
    )(conv_gdn, proj_main, proj_small, normw, alog, dtb)


def gdn_bwd(dproj_main, conv_gdn, proj_main, proj_small, normw, alog, dtb, hist, t_inv, dy):
    t = conv_gdn.shape[0]
    hb, cb = GDN_HB, GDN_CB
    rows = CHUNK * cb
    ns = t // rows
    rev = lambda c: ns - 1 - c
    gate_blk = COL_GATE // (GDN_DV * hb)

    def body(alias_ref, qkv_ref, gate_ref, sm_ref, nw_ref, al_ref, db_ref, hist_ref, t_ref, dy_ref,
             dgate_ref, dqkv_ref, dsm_ref, dnw_ref, dal_ref, ddb_ref, dstate_ref):
        del alias_ref
        c, h = pl.program_id(0), pl.program_id(1)
        h0 = _first_head()

        @pl.when(c == 0)
        def _():
            for j in range(hb):
                dstate_ref[h0 + j] = jnp.zeros((GDN_DK, GDN_DV), F32)

        saved = [t_ref[0, p] for p in range(cb * hb // 2)]

        def fn(qs, ks, vs, smalls, gates, nw, al, db, states):
            return gdn_chunk(h0, qs, ks, vs, smalls, gates, nw, al, db, states, saved)[:2]

        qs, ks, vs = _gdn_parts(qkv_ref)
        _, vjp = jax.vjp(fn, qs, ks, vs, _chunk_blocks(sm_ref), _head_cols(gate_ref), nw_ref[...], al_ref[...],
                         db_ref[...], [hist_ref[0, j] for j in range(hb)])
        dqs, dks, dvs, dsm, dgates, dnw, dal, ddb, dstates = vjp(
            (_head_cols(dy_ref), [dstate_ref[h0 + j] for j in range(hb)]))
        for k in range(cb):
            rk = _chunk_rows(k)
            for j in range(hb):
                base = j * GDN_HC
                dqkv_ref[rk, base:base + GDN_DK] = dqs[k][j]
                dqkv_ref[rk, base + GDN_DK:base + 2 * GDN_DK] = dks[k][j]
                dqkv_ref[rk, base + 2 * GDN_DK:base + GDN_HC] = dvs[k][j]
                dgate_ref[rk, j * GDN_DV:(j + 1) * GDN_DV] = dgates[k][j].astype(dgate_ref.dtype)
        for j in range(hb):
            dstate_ref[h0 + j] = dstates[j]
        _accumulate(dsm_ref, h == 0, jnp.concatenate(dsm, axis=0))
        first = jnp.logical_and(c == 0, h == 0)
        _accumulate(dnw_ref, first, dnw)
        _accumulate(dal_ref, first, dal)
        _accumulate(ddb_ref, first, ddb)

    return pl.pallas_call(
        body, name="gdn_bwd", grid=(ns, GDN_HEADS // hb),
        in_specs=[ANY, pl.BlockSpec((rows, GDN_HC * hb), lambda c, h: (rev(c), h)),
                  pl.BlockSpec((rows, GDN_DV * hb), lambda c, h: (rev(c), gate_blk + h)),
                  pl.BlockSpec((rows, LANES), lambda c, h: (rev(c), 0)),
                  _full((1, GDN_DV)), _full((1, LANES)), _full((1, LANES)),
                  pl.BlockSpec((1, hb, GDN_DK, GDN_DV), lambda c, h: (rev(c), h, 0, 0)),
                  pl.BlockSpec((1, cb * hb // 2, CHUNK, LANES), lambda c, h: (rev(c), h, 0, 0)),
                  pl.BlockSpec((rows, GDN_DV * hb), lambda c, h: (rev(c), h))],
        out_specs=[pl.BlockSpec((rows, GDN_DV * hb), lambda c, h: (rev(c), gate_blk + h)),
                   pl.BlockSpec((rows, GDN_HC * hb), lambda c, h: (rev(c), h)),
                   pl.BlockSpec((rows, LANES), lambda c, h: (rev(c), 0)),
                   _full((1, GDN_DV)), _full((1, LANES)), _full((1, LANES))],
        out_shape=[jax.ShapeDtypeStruct(dproj_main.shape, dproj_main.dtype), jax.ShapeDtypeStruct((t, GDN_CONV), F32),
                   jax.ShapeDtypeStruct((t, LANES), F32), jax.ShapeDtypeStruct((1, GDN_DV), F32),
                   jax.ShapeDtypeStruct((1, LANES), F32), jax.ShapeDtypeStruct((1, LANES), F32)],
        scratch_shapes=[pltpu.VMEM((GDN_HEADS, GDN_DK, GDN_DV), F32)],
        input_output_aliases={0: 0},
        compiler_params=_params(("arbitrary", "arbitrary")),
    )(dproj_main, conv_gdn, proj_main, proj_small, normw, alog, dtb, hist, t_inv, dy)


def out_proj_loss(x, y_ssd, y_gdn, w_out, final_w, target):
    t = x.shape[0]
    tm = min(256, t)

    def body(x_ref, ys_ref, yg_ref, wo_ref, fw_ref, tg_ref, loss_ref, dhid_ref, dys_ref, dyg_ref, dwo_ref, dfw_ref):
        i = pl.program_id(0)
        ys, yg = ys_ref[...], yg_ref[...]
        wo_s, wo_g = wo_ref[:SSD_WIDTH, :], wo_ref[SSD_WIDTH:, :]
        hid = x_ref[...] + _raw_dot(ys, wo_s, 1, 0) + _raw_dot(yg, wo_g, 1, 0)
        out, vjp = jax.vjp(rmsnorm, hid, fw_ref[...])
        err = out - tg_ref[...]
        loss = 0.5 * jnp.sum(jnp.mean(err * err, axis=-1, keepdims=True), axis=0, keepdims=True)
        dhid, dfw = vjp(err * (1.0 / D_MODEL))
        dhid_ref[...] = dhid
        dys_ref[...] = _raw_dot(dhid, wo_s, 1, 1)
        dyg_ref[...] = _raw_dot(dhid, wo_g, 1, 1)
        first = i == 0
        _accumulate(loss_ref, first, jnp.broadcast_to(loss, loss_ref.shape))
        _accumulate(dfw_ref, first, dfw)

        @pl.when(first)
        def _():
            dwo_ref[:SSD_WIDTH, :] = _raw_dot(ys, dhid, 0, 0)
            dwo_ref[SSD_WIDTH:, :] = _raw_dot(yg, dhid, 0, 0)

        @pl.when(i > 0)
        def _():
            dwo_ref[:SSD_WIDTH, :] += _raw_dot(ys, dhid, 0, 0)
            dwo_ref[SSD_WIDTH:, :] += _raw_dot(yg, dhid, 0, 0)

    row = lambda w: pl.BlockSpec((tm, w), lambda i: (i, 0))
    return pl.pallas_call(
        body, name="out_proj_loss", grid=(t // tm,),
        in_specs=[row(D_MODEL), row(SSD_WIDTH), row(GDN_W), _full((SSD_WIDTH + GDN_W, D_MODEL)), _full((1, D_MODEL)),
                  row(D_MODEL)],
        out_specs=[_full((8, LANES)), row(D_MODEL), row(SSD_WIDTH), row(GDN_W), _full((SSD_WIDTH + GDN_W, D_MODEL)),
                   _full((1, D_MODEL))],
        out_shape=[jax.ShapeDtypeStruct((8, LANES), F32), jax.ShapeDtypeStruct((t, D_MODEL), F32),
                   jax.ShapeDtypeStruct((t, SSD_WIDTH), F32), jax.ShapeDtypeStruct((t, GDN_W), F32),
                   jax.ShapeDtypeStruct((SSD_WIDTH + GDN_W, D_MODEL), F32), jax.ShapeDtypeStruct((1, D_MODEL), F32)],
        compiler_params=_params(("arbitrary",)),
    )(x, y_ssd, y_gdn, w_out, final_w, target)


def in_proj_bwd_x(x, normw, w_main, w_small, dproj_main, dsmall_a, dsmall_b, dhid, slabbed):
    t = x.shape[0]
    tm = min(256, t)
    ni = t // tm
    ns = len(slabbed)

    def body(x_ref, nw_ref, wm_ref, ws_ref, dp_ref, da_ref, db_ref, dh_ref, *rest):
        slab_refs, (gx_ref, dnw_ref), land_refs = rest[:ns], rest[ns:ns + 2], rest[ns + 2:2 * ns + 2]
        sems = rest[2 * ns + 2:]
        i = pl.program_id(0)
        start, finish = _slab_exchange(slab_refs, land_refs, ns, *sems)

        @pl.when(i == 0)
        def _():
            start()

        du = _raw_dot(dp_ref[...], wm_ref[...], 1, 1) + _raw_dot(da_ref[...] + db_ref[...], ws_ref[...], 1, 1)
        _, vjp = jax.vjp(rmsnorm, x_ref[...], nw_ref[...])
        dx, dnw = vjp(du)
        gx_ref[...] = dx + dh_ref[...]
        _accumulate(dnw_ref, i == 0, dnw)

        @pl.when(i == ni - 1)
        def _():
            finish()

    row = lambda w: pl.BlockSpec((tm, w), lambda i: (i, 0))
    out = pl.pallas_call(
        body, name="in_proj_bwd_x", grid=(ni,),
        in_specs=[row(D_MODEL), _full((1, D_MODEL)), _full((D_MODEL, MAIN)), _full((D_MODEL, LANES)), row(MAIN),
                  row(LANES), row(LANES), row(D_MODEL)] + [HBM] * ns,
        out_specs=[row(D_MODEL), _full((1, D_MODEL))] + [HBM] * ns,
        out_shape=[jax.ShapeDtypeStruct((t, D_MODEL), F32), jax.ShapeDtypeStruct((1, D_MODEL), F32)]
        + _slab_exchange_shapes(slabbed, []),
        scratch_shapes=_slab_exchange_sems(ns),
        compiler_params=_params(("arbitrary",)),
    )(x, normw, w_main, w_small, dproj_main, dsmall_a, dsmall_b, dhid, *slabbed)
    return out[0], out[1], out[2:]


def in_proj_bwd_w(u, dproj_main, dsmall_a, dsmall_b, slabbed):
    t = u.shape[0]
    tm, tn = min(1024, t), MAIN // 4
    nj, ni = MAIN // tn, t // tm
    ns = len(slabbed)

    def body(u_ref, dp_ref, da_ref, db_ref, *rest):
        slab_refs, (dwm_ref, dws_ref), land_refs, sems = rest[:ns], rest[ns:ns + 2], rest[ns + 2:2 * ns + 2], rest[2 * ns + 2:]
        j, i = pl.program_id(0), pl.program_id(1)
        start, finish = _slab_exchange(slab_refs, land_refs, ns, *sems)

        @pl.when(jnp.logical_and(j == 0, i == 0))
        def _():
            start()

        uu = u_ref[...]
        _accumulate(dwm_ref, i == 0, _raw_dot(uu, dp_ref[...], 0, 0))

        @pl.when(j == 0)
        def _():
            _accumulate(dws_ref, i == 0, _raw_dot(uu, da_ref[...] + db_ref[...], 0, 0))

        @pl.when(jnp.logical_and(j == nj - 1, i == ni - 1))
        def _():
            finish()

    out = pl.pallas_call(
        body, name="in_proj_bwd_w", grid=(nj, ni),
        in_specs=[pl.BlockSpec((tm, D_MODEL), lambda j, i: (i, 0)), pl.BlockSpec((tm, tn), lambda j, i: (i, j)),
                  pl.BlockSpec((tm, LANES), lambda j, i: (i, 0)), pl.BlockSpec((tm, LANES), lambda j, i: (i, 0))]
        + [HBM] * ns,
        out_specs=[pl.BlockSpec((D_MODEL, tn), lambda j, i: (0, j)), _full((D_MODEL, LANES))] + [HBM] * ns,
        out_shape=[jax.ShapeDtypeStruct((D_MODEL, MAIN), F32), jax.ShapeDtypeStruct((D_MODEL, LANES), F32)]
        + _slab_exchange_shapes(slabbed, []),
        scratch_shapes=_slab_exchange_sems(ns),
        compiler_params=_params(("arbitrary", "arbitrary")),
    )(u, dproj_main, dsmall_a, dsmall_b, *slabbed)
    return out[0], out[1], out[2:]


def sum_slabs(a, name):
    n, rows, cols = a.shape
    tr = 64 if rows % 64 == 0 else rows

    def body(a_ref, o_ref):
        acc = a_ref[0].astype(F32)
        for d in range(1, n):
            acc = acc + a_ref[d].astype(F32)
        o_ref[...] = acc

    return pl.pallas_call(
        body, name=name, grid=(rows // tr,),
        in_specs=[pl.BlockSpec((n, tr, cols), lambda i: (0, i, 0))],
        out_specs=pl.BlockSpec((tr, cols), lambda i: (i, 0)),
        out_shape=jax.ShapeDtypeStruct((rows, cols), F32),
        compiler_params=_params(("arbitrary",)),
    )(a)


def adamw(w, g, m, v, name):
    rows, cols = w.shape
    tr = 128 if rows % 128 == 0 else rows

    def body(w_ref, g_ref, m_ref, v_ref, d_ref, nm_ref, nv_ref):
        gg = g_ref[...]
        nm = ADAM_B1 * m_ref[...] + (1.0 - ADAM_B1) * gg
        nv = ADAM_B2 * v_ref[...] + (1.0 - ADAM_B2) * (gg * gg)
        m_hat = nm / (1.0 - ADAM_B1 ** ADAM_STEP)
        v_hat = nv / (1.0 - ADAM_B2 ** ADAM_STEP)
        d_ref[...] = -ADAM_LR * (m_hat / (jnp.sqrt(v_hat) + ADAM_EPS) + ADAM_WD * w_ref[...])
        nm_ref[...] = nm
        nv_ref[...] = nv

    spec = pl.BlockSpec((tr, cols), lambda i: (i, 0))
    shp = jax.ShapeDtypeStruct((rows, cols), F32)
    return pl.pallas_call(
        body, name=name, grid=(rows // tr,), in_specs=[spec] * 4, out_specs=[spec] * 3, out_shape=[shp] * 3,
        compiler_params=_params(("arbitrary",)),
    )(w, g, m, v)


def _my_place():
    return lax.axis_index("x"), lax.axis_index("y"), lax.axis_index("c")


def gather_weights(big, small):
    nb, n = len(big), len(big) + len(small)
    parts = 4

    def body(*refs):
        srcs, outs = refs[:n], refs[n:2 * n]
        land_a, land_b = refs[2 * n:2 * n + nb], refs[2 * n + nb:2 * n + 2 * nb]
        send_sems, recv_sems, fwd_send, fwd_recv, local_sems = refs[2 * n + 2 * nb:]
        x, y, c = _my_place()
        me = 2 * x + y
        chips = [(1 - x, y), (x, 1 - y), (1 - x, 1 - y)]
        half = [a.shape[0] // 2 for a in big]

        def ici(j, i):
            px, py = chips[j]
            if i < nb:
                src, dst = srcs[i].at[pl.ds(c * half[i], half[i])], land_a[i].at[j]
            else:
                src, dst = srcs[i], outs[i].at[me]
            return pltpu.make_async_remote_copy(src_ref=src, dst_ref=dst, send_sem=send_sems.at[j * n + i],
                                                recv_sem=recv_sems.at[j * n + i], device_id=(px, py, c),
                                                device_id_type=MESH)

        def ici_arrival(j, i):
            px, py = chips[j]
            dst = land_a[i].at[j] if i < nb else outs[i].at[2 * px + py]
            return pltpu.make_async_remote_copy(src_ref=dst, dst_ref=dst, send_sem=send_sems.at[j * n + i],
                                                recv_sem=recv_sems.at[j * n + i], device_id=(px, py, c),
                                                device_id_type=MESH)

        def forward(j, i, p):
            rows = half[i] // parts
            k = (j * nb + i) * parts + p
            return pltpu.make_async_remote_copy(
                src_ref=land_a[i].at[j, pl.ds(p * rows, rows)], dst_ref=land_b[i].at[j, pl.ds(p * rows, rows)],
                send_sem=fwd_send.at[k], recv_sem=fwd_recv.at[k], device_id=(x, y, 1 - c), device_id_type=MESH)

        def store(j, i, from_sibling):
            px, py = chips[j]
            buf, h = (land_b, 1 - c) if from_sibling else (land_a, c)
            k = n + (j * nb + i) * 2 + (1 if from_sibling else 0)
            return pltpu.make_async_copy(buf[i].at[j], outs[i].at[2 * px + py, pl.ds(h * half[i], half[i])],
                                         local_sems.at[k])

        own = [pltpu.make_async_copy(srcs[i], outs[i].at[me], local_sems.at[i]) for i in range(n)]
        sends = [ici(j, i) for j in range(3) for i in range(n)]
        for cp in own + sends:
            cp.start()
        pending = []
        for j in range(3):
            for i in range(n):
                ici_arrival(j, i).wait_recv()
                if i < nb:
                    fw = [forward(j, i, p) for p in range(parts)]
                    st = store(j, i, False)
                    for cp in fw + [st]:
                        cp.start()
                    pending += [cp.wait_send for cp in fw] + [st.wait]
        for j in range(3):
            for i in range(nb):
                for p in range(parts):
                    forward(j, i, p).wait_recv()
                st = store(j, i, True)
                st.start()
                pending.append(st.wait)
        for cp in sends:
            cp.wait_send()
        for wait in pending:
            wait()
        for cp in own:
            cp.wait()

    shards = list(big) + list(small)
    lands = [pltpu.VMEM((3, a.shape[0] // 2) + a.shape[1:], a.dtype) for a in big]
    return pl.pallas_call(
        body, name="gather_weights",
        in_specs=[HBM] * n, out_specs=[HBM] * n,
        out_shape=[jax.ShapeDtypeStruct((N_CHIP,) + s.shape, s.dtype) for s in shards],
        scratch_shapes=lands + lands + [
            pltpu.SemaphoreType.DMA((3 * n,)), pltpu.SemaphoreType.DMA((3 * n,)),
            pltpu.SemaphoreType.DMA((3 * nb * parts,)), pltpu.SemaphoreType.DMA((3 * nb * parts,)),
            pltpu.SemaphoreType.DMA((n + 6 * nb,))],
        compiler_params=pltpu.CompilerParams(vmem_limit_bytes=VMEM_LIMIT),
    )(*shards)


def _peer(x, y, c, mask):
    mx, my, mc = (mask >> 2) & 1, (mask >> 1) & 1, mask & 1
    return (x ^ mx if mx else x, y ^ my if my else y, c ^ mc if mc else c)


def _slab_exchange_shapes(slabbed, replicated):
    return ([jax.ShapeDtypeStruct(a.shape, a.dtype) for a in slabbed]
            + [jax.ShapeDtypeStruct((N_DEV,) + a.shape, a.dtype) for a in replicated])


def _slab_exchange_sems(n):
    return [pltpu.SemaphoreType.DMA((7 * n,)), pltpu.SemaphoreType.DMA((7 * n,)), pltpu.SemaphoreType.DMA((n,))]


def _slab_exchange(srcs, outs, ns, send_sems, recv_sems, local_sems):
    n = len(srcs)
    x, y, c = _my_place()
    me = 4 * x + 2 * y + c

    def piece(i, dev):
        return srcs[i].at[dev] if i < ns else srcs[i]

    def copies(arriving):
        out = []
        for mask in range(1, N_DEV):
            px, py, pc = _peer(x, y, c, mask)
            dev = 4 * px + 2 * py + pc
            for i in range(n):
                k = (mask - 1) * n + i
                out.append(pltpu.make_async_remote_copy(
                    src_ref=piece(i, dev), dst_ref=outs[i].at[dev if arriving else me], send_sem=send_sems.at[k],
                    recv_sem=recv_sems.at[k], device_id=(px, py, pc), device_id_type=MESH))
        return out

    def local():
        return [pltpu.make_async_copy(piece(i, me), outs[i].at[me], local_sems.at[i]) for i in range(n)]

    def start():
        for cp in local() + copies(False):
            cp.start()

    def finish():
        for cp in copies(True):
            cp.wait_recv()
        for cp in copies(False):
            cp.wait_send()
        for cp in local():
            cp.wait()

    return start, finish


def exchange_halves(halves, replicated):
    n, nr = len(halves), len(replicated)
    streams = 8

    def body(*refs):
        srcs, rep_srcs, outs, rep_outs = refs[:n], refs[n:n + nr], refs[n + nr:2 * n + nr], refs[2 * n + nr:2 * (n + nr)]
        refs = refs[2 * (n + nr):]
        mine, theirs = refs[:n], refs[n:2 * n]
        send_sems, recv_sems, in_sems, out_sems = refs[2 * n:2 * n + 4]
        rep_start, rep_finish = _slab_exchange(rep_srcs, rep_outs, 0, *refs[2 * n + 4:])
        rep_start()
        x, y, c = _my_place()
        loads = [pltpu.make_async_copy(srcs[i], mine[i], in_sems.at[i]) for i in range(n)]
        for cp in loads:
            cp.start()
        for cp in loads:
            cp.wait()

        def chunk_copy(i, s):
            rows = halves[i].shape[0] // streams
            k = i * streams + s
            return pltpu.make_async_remote_copy(
                src_ref=mine[i].at[pl.ds(s * rows, rows)], dst_ref=theirs[i].at[pl.ds(s * rows, rows)],
                send_sem=send_sems.at[k], recv_sem=recv_sems.at[k], device_id=(x, y, 1 - c), device_id_type=MESH)

        sends = [chunk_copy(i, s) for i in range(n) for s in range(streams)]
        for cp in sends:
            cp.start()
        own = [pltpu.make_async_copy(mine[i], outs[i].at[c], out_sems.at[i]) for i in range(n)]
        for cp in own:
            cp.start()
        for cp in sends:
            cp.wait_recv()
        got = [pltpu.make_async_copy(theirs[i], outs[i].at[1 - c], out_sems.at[n + i]) for i in range(n)]
        for cp in got:
            cp.start()
        for cp in sends:
            cp.wait_send()
        for cp in own + got:
            cp.wait()
        rep_finish()

    vmem = [pltpu.VMEM(a.shape, a.dtype) for a in halves]
    out = pl.pallas_call(
        body, name="exchange_halves",
        in_specs=[HBM] * (n + nr), out_specs=[HBM] * (n + nr),
        out_shape=[jax.ShapeDtypeStruct((2,) + a.shape, a.dtype) for a in halves]
        + _slab_exchange_shapes([], replicated),
        scratch_shapes=vmem + vmem + [pltpu.SemaphoreType.DMA((n * streams,)), pltpu.SemaphoreType.DMA((n * streams,)),
                                      pltpu.SemaphoreType.DMA((n,)), pltpu.SemaphoreType.DMA((2 * n,))]
        + _slab_exchange_sems(nr),
        compiler_params=pltpu.CompilerParams(vmem_limit_bytes=VMEM_LIMIT),
    )(*halves, *replicated)
    return out[:n], out[n:]


def _pack_cols(pieces):
    offs, pos = [], 0
    for a in pieces:
        offs.append(pos)
        pos += a.shape[1]
    rows8 = [jnp.pad(a.astype(F32), ((0, 8 - a.shape[0]), (0, 0))) for a in pieces]
    return jnp.concatenate(rows8, axis=1), offs


def adamw_many(ws, gs, ms, vs):
    n = len(ws)

    def body(*refs):
        w_r, g_r, m_r, v_r = refs[:n], refs[n:2 * n], refs[2 * n:3 * n], refs[3 * n:4 * n]
        d_o, m_o, v_o = refs[4 * n:5 * n], refs[5 * n:6 * n], refs[6 * n:7 * n]
        for i in range(n):
            gg = g_r[i][...]
            nm = ADAM_B1 * m_r[i][...] + (1.0 - ADAM_B1) * gg
            nv = ADAM_B2 * v_r[i][...] + (1.0 - ADAM_B2) * (gg * gg)
            m_hat = nm / (1.0 - ADAM_B1 ** ADAM_STEP)
            v_hat = nv / (1.0 - ADAM_B2 ** ADAM_STEP)
            d_o[i][...] = -ADAM_LR * (m_hat / (jnp.sqrt(v_hat) + ADAM_EPS) + ADAM_WD * w_r[i][...])
            m_o[i][...] = nm
            v_o[i][...] = nv

    shapes = [jax.ShapeDtypeStruct(w.shape, F32) for w in ws]
    out = pl.pallas_call(body, name="adamw_small", out_shape=shapes * 3,
                         compiler_params=pltpu.CompilerParams(vmem_limit_bytes=VMEM_LIMIT))(*ws, *gs, *ms, *vs)
    return out[:n], out[n:2 * n], out[2 * n:]


def _lanes(vec, start):
    n = vec.shape[-1]
    return jnp.pad(vec.reshape(1, n).astype(F32), ((0, 0), (start, LANES - start - n)))


def kernel(x, norm_w, w_in, ssd_conv_w, ssd_conv_b, ssd_dt_bias, ssd_a_log, ssd_d, ssd_norm_w, gdn_conv_w, gdn_dt_bias, gdn_a_log, gdn_norm_w, w_out, final_norm_w, loss_target, m_norm_w, m_w_in, m_ssd_conv_w, m_ssd_conv_b, m_ssd_dt_bias, m_ssd_a_log, m_ssd_d, m_ssd_norm_w, m_gdn_conv_w, m_gdn_dt_bias, m_gdn_a_log, m_gdn_norm_w, m_w_out, m_final_norm_w, v_norm_w, v_w_in, v_ssd_conv_w, v_ssd_conv_b, v_ssd_dt_bias, v_ssd_a_log, v_ssd_d, v_ssd_norm_w, v_gdn_conv_w, v_gdn_dt_bias, v_gdn_a_log, v_gdn_norm_w, v_w_out, v_final_norm_w):
    xs = x[0]
    target = loss_target[0]
    chip = 2 * lax.axis_index("x") + lax.axis_index("y")
    w_in_shard, w_out_shard = w_in[0], w_out[0]
    in_cols = w_in_shard.shape[1]
    out_rows = w_out_shard.shape[0]

    g_in, g_out, g_cs, g_cg = gather_weights(
        [w_in_shard.astype(MXU_DTYPE), w_out_shard.astype(MXU_DTYPE)], [ssd_conv_w[0], gdn_conv_w[0]])
    w_in_full = jnp.concatenate([g_in[k] for k in range(N_CHIP)], axis=1)
    w_out_full = g_out.reshape(N_CHIP * out_rows, D_MODEL)
    cw_ssd = _ssd_perm(jnp.transpose(g_cs, (1, 0, 2)).reshape(4, SSD_CONV))
    cw_gdn = _gdn_perm(jnp.transpose(g_cg, (1, 0, 2)).reshape(4, GDN_CONV))
    cb_ssd = _ssd_perm(ssd_conv_b)
    cb_gdn = jnp.zeros((1, GDN_CONV), F32)
    o_xbc, o_dt, o_gate, o_qkv, o_ab = 1024, 2560, 2576, 3600, 6672
    w_main = jnp.concatenate([w_in_full[:, :o_xbc], w_in_full[:, o_gate:o_qkv], _gdn_perm(w_in_full[:, o_qkv:o_ab]),
                              _ssd_perm(w_in_full[:, o_xbc:o_dt])], axis=1)
    w_small = jnp.concatenate([w_in_full[:, o_dt:o_gate], w_in_full[:, o_ab:],
                               jnp.zeros((D_MODEL, LANES - 32), MXU_DTYPE)], axis=1)
    alog = _lanes(ssd_a_log, 0) + _lanes(gdn_a_log, LANE_GA)
    dtb = _lanes(ssd_dt_bias, 0) + _lanes(gdn_dt_bias, LANE_GA)
    dvec = _lanes(ssd_d, 0)
    fw = final_norm_w.reshape(1, D_MODEL)

    proj_main, proj_small, u = in_proj(xs, norm_w, w_main, w_small)
    proj_main, conv_out = in_proj_conv(proj_main, u, w_main, jnp.concatenate([cw_gdn, cw_ssd], axis=1),
                                       jnp.concatenate([cb_gdn, cb_ssd], axis=1))
    conv_ssd = conv_gdn = conv_out
    y_ssd, hist_ssd = ssd_fwd(conv_ssd, proj_main, proj_small, ssd_norm_w, alog, dtb, dvec)
    y_gdn, hist_gdn, tinv_gdn = gdn_fwd(conv_gdn, proj_main, proj_small, gdn_norm_w, alog, dtb)

    loss_blk, dhid, dy_ssd, dy_gdn, d_w_out, d_fw = out_proj_loss(xs, y_ssd, y_gdn, w_out_full, fw, target)
    dconv_ssd, dproj_main, dsmall_ssd, d_ssd_nw, d_alog_s, d_dtb_s, d_dvec = ssd_bwd(
        conv_ssd, proj_main, proj_small, ssd_norm_w, alog, dtb, dvec, hist_ssd, dy_ssd)
    dproj_main, dconv_gdn, dsmall_gdn, d_gdn_nw, d_alog_g, d_dtb_g = gdn_bwd(
        dproj_main, conv_gdn, proj_main, proj_small, gdn_norm_w, alog, dtb, hist_gdn, tinv_gdn, dy_gdn)
    dproj_main, dwb_ssd = conv_bwd(dproj_main, proj_main, COL_SSD, SSD_CONV, cw_ssd, cb_ssd, dconv_ssd, "conv_bwd_ssd")
    dproj_main, dwb_gdn = conv_bwd(dproj_main, proj_main, COL_GDN, GDN_CONV, cw_gdn, cb_gdn, dconv_gdn, "conv_bwd_gdn")
    slabs_out = d_w_out.reshape(N_DEV, out_rows // 2, D_MODEL).astype(COMM_DTYPE)
    d_w_main, d_w_small, (r_out,) = in_proj_bwd_w(u, dproj_main, dsmall_ssd, dsmall_gdn, [slabs_out])
    d_w_in = jnp.concatenate([d_w_main[:, :COL_GATE], _ssd_unperm(d_w_main[:, COL_SSD:]), d_w_small[:, 0:16],
                              d_w_main[:, COL_GATE:COL_GDN], _gdn_unperm(d_w_main[:, COL_GDN:COL_SSD]),
                              d_w_small[:, 16:32]], axis=1)
    d_w_in = jnp.stack([d_w_in[:, k * in_cols:(k + 1) * in_cols] for k in range(N_CHIP)])
    slabs_in = d_w_in.reshape(N_DEV, D_MODEL // 2, in_cols).astype(COMM_DTYPE)
    grad_x, d_norm_w, (r_in,) = in_proj_bwd_x(xs, norm_w, w_main, w_small, dproj_main, dsmall_ssd, dsmall_gdn,
                                               dhid, [slabs_in])
    d_alog, d_dtb = d_alog_s + d_alog_g, d_dtb_s + d_dtb_g
    packed, (o_nw, o_cs, o_cg, o_snw, o_fw, o_al, o_db, o_dv, o_gnw, o_loss) = _pack_cols([
        d_norm_w, _ssd_unperm(dwb_ssd), _gdn_unperm(dwb_gdn),
        d_ssd_nw.reshape(1, SSD_WIDTH), d_fw, d_alog, d_dtb, d_dvec, d_gdn_nw, loss_blk])

    half_in = sum_slabs(r_in, "sum_w_in")
    half_out = sum_slabs(r_out, "sum_w_out")
    (full_in, full_out), (r_small,) = exchange_halves([half_in, half_out], [packed])
    tot = sum_slabs(r_small, "sum_small")
    grad_w_in = full_in.reshape(D_MODEL, in_cols)
    grad_w_out = full_out.reshape(out_rows, D_MODEL)
    loss = tot[0, o_loss]
    sc, gc = ssd_conv_w.shape[2], gdn_conv_w.shape[2]
    row = lambda off, n, r=0: tot[r:r + 1, off:off + n]
    gs = [row(o_nw, D_MODEL),
          lax.dynamic_slice(tot, (0, o_cs + chip * sc), (4, sc)),
          row(o_cs, SSD_CONV, 4),
          row(o_db, SSD_HEADS), row(o_al, SSD_HEADS), row(o_dv, SSD_HEADS),
          row(o_snw, SSD_WIDTH),
          lax.dynamic_slice(tot, (0, o_cg + chip * gc), (4, gc)),
          row(o_db + LANE_GA, GDN_HEADS), row(o_al + LANE_GA, GDN_HEADS),
          row(o_gnw, GDN_DV), row(o_fw, D_MODEL)]

    names = ["norm_w", "ssd_conv_w", "ssd_conv_b", "ssd_dt_bias", "ssd_a_log", "ssd_d", "ssd_norm_w", "gdn_conv_w",
             "gdn_dt_bias", "gdn_a_log", "gdn_norm_w", "final_norm_w"]
    ws = [norm_w, ssd_conv_w, ssd_conv_b, ssd_dt_bias, ssd_a_log, ssd_d, ssd_norm_w, gdn_conv_w, gdn_dt_bias,
          gdn_a_log, gdn_norm_w, final_norm_w]
    ms = [m_norm_w, m_ssd_conv_w, m_ssd_conv_b, m_ssd_dt_bias, m_ssd_a_log, m_ssd_d, m_ssd_norm_w, m_gdn_conv_w,
          m_gdn_dt_bias, m_gdn_a_log, m_gdn_norm_w, m_final_norm_w]
    vs = [v_norm_w, v_ssd_conv_w, v_ssd_conv_b, v_ssd_dt_bias, v_ssd_a_log, v_ssd_d, v_ssd_norm_w, v_gdn_conv_w,
          v_gdn_dt_bias, v_gdn_a_log, v_gdn_norm_w, v_final_norm_w]
    shapes = [w.shape for w in ws]
    flat = lambda arrs: [a.reshape(g.shape) for a, g in zip(arrs, gs)]
    d_s, m_s, v_s = adamw_many(flat(ws), gs, flat(ms), flat(vs))
    back = lambda arrs: dict(zip(names, [a.reshape(s) for a, s in zip(arrs, shapes)]))
    delta, new_m, new_v, grads = back(d_s), back(m_s), back(v_s), back(gs)
    d_in, m_in, v_in = adamw(w_in_shard, grad_w_in, m_w_in[0], v_w_in[0], "adamw_w_in")
    d_out, m_out, v_out = adamw(w_out_shard, grad_w_out, m_w_out[0], v_w_out[0], "adamw_w_out")
    for tbl, a_in, a_out in ((grads, grad_w_in, grad_w_out), (delta, d_in, d_out), (new_m, m_in, m_out),
                             (new_v, v_in, v_out)):
        tbl["w_in"] = a_in[None]
        tbl["w_out"] = a_out[None]

    order = ["norm_w", "w_in", "ssd_conv_w", "ssd_conv_b", "ssd_dt_bias", "ssd_a_log", "ssd_d", "ssd_norm_w",
             "gdn_conv_w", "gdn_dt_bias", "gdn_a_log", "gdn_norm_w", "w_out", "final_norm_w"]
    return (loss.reshape(()), grad_x[None], *[grads[k] for k in order], *[delta[k] for k in order],
            *[new_m[k] for k in order], *[new_v[k] for k in order])
```

```python
import functools

import jax
import jax.numpy as jnp
from jax import lax
from jax.experimental import pallas as pl
from jax.experimental.pallas import tpu as pltpu

F32 = jnp.float32
MXU_DTYPE = jnp.bfloat16
COMM_DTYPE = jnp.bfloat16
MESH = pl.DeviceIdType.MESH

D_MODEL = 1024
CHUNK = 64
EPS = 1e-6
SSD_HEADS, SSD_GROUPS, SSD_STATE = 16, 2, 128
SSD_WIDTH, SSD_CONV = 1024, 1536
SSD_GW = SSD_WIDTH // SSD_GROUPS
SSD_GC = SSD_GW + 2 * SSD_STATE
GDN_HEADS, GDN_DK, GDN_DV = 8, 128, 128
GDN_W, GDN_CONV = 1024, 3072
GDN_HC = 2 * GDN_DK + GDN_DV
IN_DIM = 6688
MAIN = 6656
LANES = 128
COL_Z, COL_GATE, COL_GDN, COL_SSD = 0, 1024, 2048, 5120
COL_CONV = COL_GDN
GDN_HB = 8
GDN_CB = 2
LANE_GA, LANE_GB = 16, 24
N_DEV, N_CHIP = 8, 4
VMEM_LIMIT = 52 * 1024 * 1024

ADAM_LR, ADAM_B1, ADAM_B2, ADAM_EPS, ADAM_WD, ADAM_STEP = 0.001, 0.9, 0.999, 1e-08, 0.01, 10


def _split(a, n):
    parts, rest = [], a.astype(F32)
    for i in range(n):
        p = rest.astype(MXU_DTYPE)
        parts.append(p)
        if i < n - 1:
            rest = rest - p.astype(F32)
    return parts


def _raw_dot(a, b, ca, cb, mode="bf16"):
    d = lambda u, v: lax.dot_general(u, v, (((ca,), (cb,)), ((), ())), preferred_element_type=F32)
    if mode == "bf16":
        return d(a.astype(MXU_DTYPE), b.astype(MXU_DTYPE))
    if mode == "x3":
        (ah, al), (bh, bl) = _split(a, 2), _split(b, 2)
        return d(ah, bh) + (d(ah, bl) + d(al, bh))
    if mode == "sel_a":
        a0 = a.astype(MXU_DTYPE)
        b1, b2, b3 = _split(b, 3)
        return d(a0, b1) + (d(a0, b2) + d(a0, b3))
    assert mode == "sel_b", mode
    b0 = b.astype(MXU_DTYPE)
    a1, a2, a3 = _split(a, 3)
    return d(a1, b0) + (d(a2, b0) + d(a3, b0))


@functools.partial(jax.custom_vjp, nondiff_argnums=(2,))
def mm_nn(a, b, mode="bf16"):
    return _raw_dot(a, b, 1, 0, mode)


@functools.partial(jax.custom_vjp, nondiff_argnums=(2,))
def mm_nt(a, b, mode="bf16"):
    return _raw_dot(a, b, 1, 1, mode)


@functools.partial(jax.custom_vjp, nondiff_argnums=(2,))
def mm_tn(a, b, mode="bf16"):
    return _raw_dot(a, b, 0, 0, mode)


_SAME = {"bf16": ("bf16", "bf16"), "x3": ("x3", "x3")}
_NN_BWD = dict(_SAME, sel_a=("bf16", "sel_a"), sel_b=("sel_b", "bf16"))
_NT_BWD = dict(_SAME, sel_a=("bf16", "sel_b"), sel_b=("sel_b", "bf16"))
_TN_BWD = dict(_SAME, sel_a=("bf16", "sel_a"), sel_b=("sel_a", "bf16"))
mm_nn.defvjp(lambda a, b, m: (_raw_dot(a, b, 1, 0, m), (a, b)),
             lambda m, r, g: (mm_nt(g, r[1], _NN_BWD[m][0]), mm_tn(r[0], g, _NN_BWD[m][1])))
mm_nt.defvjp(lambda a, b, m: (_raw_dot(a, b, 1, 1, m), (a, b)),
             lambda m, r, g: (mm_nn(g, r[1], _NT_BWD[m][0]), mm_tn(g, r[0], _NT_BWD[m][1])))
mm_tn.defvjp(lambda a, b, m: (_raw_dot(a, b, 0, 0, m), (a, b)),
             lambda m, r, g: (mm_nt(r[1], g, _TN_BWD[m][0]), mm_nn(r[0], g, _TN_BWD[m][1])))


@jax.custom_jvp
def sigmoid(x):
    return 1.0 / (1.0 + jnp.exp(-x))


@sigmoid.defjvp
def _sigmoid_jvp(p, t):
    s = sigmoid(p[0])
    return s, t[0] * s * (1.0 - s)


@jax.custom_jvp
def softplus(x):
    return jnp.maximum(x, 0.0) + jnp.log(1.0 + jnp.exp(-jnp.abs(x)))


@softplus.defjvp
def _softplus_jvp(p, t):
    return softplus(p[0]), t[0] * sigmoid(p[0])


def silu(x):
    return x * sigmoid(x)


def rmsnorm(x, w):
    return x * lax.rsqrt(jnp.mean(x * x, axis=-1, keepdims=True) + EPS) * w


def _iota(shape, dim):
    return lax.broadcasted_iota(jnp.int32, shape, dim)


def _halves():
    lane = _iota((1, LANES), 1) >> 6
    return _ind(lane == 0), _ind(lane == 1)


def _block_diag(pair):
    h0, h1 = _halves()
    return jnp.concatenate([pair * h0, pair * h1], axis=0)


def _tri_inv_impl(mats):
    r, c = _iota((CHUNK, LANES), 0), _iota((CHUNK, LANES), 1) & (CHUNK - 1)
    eye = _ind(r == c)
    blockdiag = _ind((r >> 4) == (c >> 4))
    dot = lambda u, v: _raw_dot(u, _block_diag(v), 1, 0, "x3")
    dot1 = lambda u, v: _raw_dot(u, _block_diag(v), 1, 0)
    each = lambda f, *ls: [f(*xs) for xs in zip(*ls)]
    dg = each(lambda a: a * blockdiag, mats)
    off = each(lambda a, d: a - d, mats, dg)
    m = each(lambda d: -d, dg)
    p = each(lambda x: eye + x, m)
    pw = m
    for _ in range(3):
        pw = each(lambda x: dot1(x, x), pw)
        p = each(lambda x, y: x + dot1(x, y), p, pw)
    e = each(dot, p, off)
    e2 = each(lambda x: dot1(x, x), e)
    q = each(lambda x: eye - x, e)
    q = each(lambda x, y: x + dot1(x, y), q, e2)
    return each(dot, q, p)


def _tri_inv_bwd(ts, gs):
    h0, h1 = _halves()
    x = [mm_nt(g, _block_diag(t)) for g, t in zip(gs, ts)]
    full = [mm_tn(t, y) for t, y in zip(ts, x)]
    return [-(f[:CHUNK] * h0 + f[CHUNK:] * h1) for f in full]


@jax.custom_vjp
def tri_inv(mats):
    return _tri_inv_impl(mats)


def _tri_inv_fwd(mats):
    ts = _tri_inv_impl(mats)
    return ts, ts


tri_inv.defvjp(_tri_inv_fwd, lambda ts, gs: (_tri_inv_bwd(ts, gs),))


@jax.custom_vjp
def tri_inv_saved(mats, ts):
    del mats
    return ts


tri_inv_saved.defvjp(lambda mats, ts: (ts, ts),
                     lambda ts, gs: (_tri_inv_bwd(ts, gs), [jnp.zeros_like(t) for t in ts]))


def _ind(cond):
    return jnp.where(cond, 1.0, 0.0).astype(F32)


def _chunk_masks():
    r, c = _iota((CHUNK, CHUNK), 0), _iota((CHUNK, CHUNK), 1)
    return _ind(r >= c), _ind(r > c), _ind(r == c), _ind(_iota((CHUNK, 1), 0) == CHUNK - 1)


def _log_decay_cumsum(small, alog, dtb, tri):
    sp = softplus(small + dtb)
    la = -jnp.exp(alog) * sp
    return sp, mm_nn(tri, la, "sel_a")


def _col_of(x, lane):
    return jnp.sum(x * _ind(_iota((1, LANES), 1) == lane), axis=1, keepdims=True)


def _decay_matrix(col, tri, eye):
    row = jnp.sum(col * eye, axis=0, keepdims=True)
    return jnp.exp((col - row) * tri) * tri


def _pair_masks():
    r, c = _iota((CHUNK, LANES), 0), _iota((CHUNK, LANES), 1)
    c6 = c & (CHUNK - 1)
    return _ind(r >= c6), _ind(r > c6), (_ind(c == r), _ind(c == r + CHUNK))


def _decay_pair(col_a, col_b, tri_w, eye_w):
    h0, h1 = _halves()
    col = col_a * h0 + col_b * h1
    row = jnp.sum(col_a * eye_w[0] + col_b * eye_w[1], axis=0, keepdims=True)
    return jnp.exp((col - row) * tri_w) * tri_w


def gdn_chunk(h0, qs, ks, vs, smalls, gates, normw, alog, dtb, states, saved_t=None):
    tri, _, _, last = _chunk_masks()
    tri_w, strict_w, eye_w = _pair_masks()
    nh = len(qs[0])
    flat = lambda xss: [x for xs in xss for x in xs]
    lacs = [_log_decay_cumsum(sm, alog, dtb, tri)[1] for sm in smalls]
    qs, ks, vs, gates = flat(qs), flat(ks), flat(vs), flat(gates)
    heads, pairs = range(len(qs)), range(len(qs) // 2)
    each = lambda f, *ls: [f(*xs) for xs in zip(*ls)]
    ab = lambda xs, p: (xs[2 * p], xs[2 * p + 1])
    stack = lambda xs: jnp.concatenate(xs, axis=0)
    gc = [_col_of(lacs[i // nh], LANE_GA + h0 + i % nh) for i in heads]
    beta = [sigmoid(_col_of(smalls[i // nh], LANE_GB + h0 + i % nh)) for i in heads]
    decay = [_decay_pair(*ab(gc, p), tri_w, eye_w) for p in pairs]
    gl = each(lambda x: jnp.sum(x * last, axis=0, keepdims=True), gc)
    q = each(lambda x: x * lax.rsqrt(jnp.sum(x * x, axis=-1, keepdims=True) + EPS) * (GDN_DK ** -0.5), qs)
    k = each(lambda x: x * lax.rsqrt(jnp.sum(x * x, axis=-1, keepdims=True) + EPS), ks)
    kb = each(lambda x, b: x * b, k, beta)
    eg = each(jnp.exp, gc)
    zero = jnp.zeros((CHUNK, GDN_DK), F32)
    k_bd = [stack([join_lanes([k[2 * p], zero]), join_lanes([zero, k[2 * p + 1]])]) for p in pairs]
    a = [mm_nt(join_lanes(list(ab(kb, p))), k_bd[p]) * (decay[p] * strict_w) for p in pairs]
    t = tri_inv(a) if saved_t is None else tri_inv_saved(a, saved_t)
    attn = [mm_nt(join_lanes(list(ab(q, p))), k_bd[p]) * decay[p] for p in pairs]
    rhs = [stack([join_lanes([vs[h] * beta[h], kb[h] * eg[h]]) for h in (2 * p, 2 * p + 1)]) for p in pairs]
    uw = [mm_nn(_block_diag(t[p]), rhs[p]) for p in pairs]
    uw = [x for p in pairs for x in split_rows(uw[p])]
    u, w = zip(*[split_lanes(x) for x in uw])
    ys = []
    for c in range(len(smalls)):
        hs = range(c * nh, (c + 1) * nh)
        v_new = [u[i] - mm_nn(w[i], states[i % nh]) for i in hs]
        av = [mm_nn(_block_diag(attn[c * nh // 2 + p]), stack(list(ab(v_new, p)))) for p in range(nh // 2)]
        av = [x for y in av for x in split_rows(y)]
        o = [mm_nn(q[i] * eg[i], states[i % nh]) + av[i % nh] for i in hs]
        states = [states[i % nh] * jnp.exp(gl[i]) + mm_tn(k[i] * jnp.exp(gl[i] - gc[i]), v_new[i % nh]) for i in hs]
        ys.append([rmsnorm(o[i % nh], normw) * silu(gates[i]) for i in hs])
    return ys, states, t


@jax.custom_vjp
def split_rows(x):
    n = x.shape[0] // 2
    return [x[:n], x[n:]]


split_rows.defvjp(lambda x: (split_rows(x), None), lambda _, gs: (jnp.concatenate(gs, axis=0),))


@jax.custom_vjp
def split_lanes(x):
    return [x[:, i * LANES:(i + 1) * LANES] for i in range(x.shape[1] // LANES)]


@jax.custom_vjp
def join_lanes(xs):
    return jnp.concatenate(xs, axis=1)


split_lanes.defvjp(lambda x: (split_lanes(x), None), lambda _, gs: (join_lanes(gs),))
join_lanes.defvjp(lambda xs: (join_lanes(xs), None), lambda _, g: (split_lanes(g),))


def ssd_chunk(xs, bm, cm, z, small, normw, alog, dtb, dvec, state):
    tri, _, eye, last = _chunk_masks()
    hpg = SSD_HEADS // SSD_GROUPS
    groups = range(len(xs))
    each = lambda f, *ls: [f(*a) for a in zip(*ls)]
    sp, lac = _log_decay_cumsum(small, alog, dtb, tri)
    lac_last = jnp.sum(lac * last, axis=0, keepdims=True)
    sel = [_ind(_iota((LANES, SSD_GW), 0) == g * hpg + (_iota((LANES, SSD_GW), 1) >> 6)) for g in groups]
    expand = lambda v, mode="sel_b": [mm_nn(v, s, mode) for s in sel]
    dt_e, elac_e, toend_e = expand(sp, "bf16"), expand(jnp.exp(lac), "bf16"), expand(jnp.exp(lac_last - lac), "bf16")
    row8, row8e = _iota((8, 1), 0), _iota((8, 1), 0)
    two_e = expand(_ind(row8 == 0) * dvec + _ind(row8 == 1) * jnp.exp(lac_last))
    d_e = each(lambda v: jnp.sum(v * _ind(row8e == 0), axis=0, keepdims=True), two_e)
    chunk_e = each(lambda v: jnp.sum(v * _ind(row8e == 1), axis=0, keepdims=True), two_e)
    xdt = each(lambda a, b: a * b, xs, dt_e)
    y = each(lambda c_, st, el, x_, d_: mm_nn(c_, st) * el + x_ * d_, cm, state, elac_e, xs, d_e)
    r, c = _iota((CHUNK, LANES), 0), _iota((CHUNK, LANES), 1)
    tri_w = _ind(r >= (c & (CHUNK - 1)))
    eye_w = [_ind(c == r), _ind(c == r + CHUNK)]
    half = [_ind((_iota((1, LANES), 1) >> 6) == s) for s in range(2)]

    def decay_pair(col_a, col_b):
        col = col_a * half[0] + col_b * half[1]
        row = jnp.sum(col_a * eye_w[0] + col_b * eye_w[1], axis=0, keepdims=True)
        return jnp.exp((col - row) * tri_w) * tri_w

    pairs = range(hpg // 2)
    cb_w = each(lambda c_, b_: mm_nt(c_, jnp.concatenate([b_, b_], axis=0)), cm, bm)
    x_pairs = each(split_lanes, xdt)
    lms = [[decay_pair(_col_of(lac, g * hpg + 2 * p), _col_of(lac, g * hpg + 2 * p + 1)) for p in pairs]
           for g in groups]
    stacked = [[jnp.concatenate([x_pairs[g][p] * half[0], x_pairs[g][p] * half[1]], axis=0) for p in pairs]
               for g in groups]
    terms = [[mm_nn(cb_w[g] * lms[g][p], stacked[g][p]) for p in pairs] for g in groups]
    y = [y[g] + join_lanes(terms[g]) for g in groups]
    new_state = each(lambda st, ce, b_, xd, te: st * ce + mm_tn(b_, xd * te), state, chunk_e, bm, xdt, toend_e)
    out = each(lambda y_, z_, nw: rmsnorm(y_ * silu(z_), nw), y, z, normw)
    return out, new_state


def _params(sem=None):
    return pltpu.CompilerParams(dimension_semantics=sem, vmem_limit_bytes=VMEM_LIMIT)


def _full(shape):
    n = len(shape)
    return pl.BlockSpec(shape, lambda *_: (0,) * n)


ANY = pl.BlockSpec(memory_space=pl.ANY)
HBM = pl.BlockSpec(memory_space=pltpu.HBM)


def in_proj(x, normw, w_main, w_small):
    t = x.shape[0]
    tm, tn = min(1024, t), 512

    def body(x_ref, nw_ref, wm_ref, ws_ref, pm_ref, ps_ref, u_ref):
        @pl.when(pl.program_id(1) == 0)
        def _():
            u = rmsnorm(x_ref[...], nw_ref[...]).astype(MXU_DTYPE)
            u_ref[...] = u
            ps_ref[...] = _raw_dot(u, ws_ref[...], 1, 0)
        pm_ref[...] = _raw_dot(u_ref[...], wm_ref[...], 1, 0)

    return pl.pallas_call(
        body, name="in_proj", grid=(t // tm, COL_CONV // tn),
        in_specs=[pl.BlockSpec((tm, D_MODEL), lambda i, j: (i, 0)), _full((1, D_MODEL)),
                  pl.BlockSpec((D_MODEL, tn), lambda i, j: (0, j)), _full((D_MODEL, LANES))],
        out_specs=[pl.BlockSpec((tm, tn), lambda i, j: (i, j)), pl.BlockSpec((tm, LANES), lambda i, j: (i, 0)),
                   pl.BlockSpec((tm, D_MODEL), lambda i, j: (i, 0))],
        out_shape=[jax.ShapeDtypeStruct((t, MAIN), F32), jax.ShapeDtypeStruct((t, LANES), F32),
                   jax.ShapeDtypeStruct((t, D_MODEL), MXU_DTYPE)],
        compiler_params=_params(("arbitrary", "arbitrary")),
    )(x, normw, w_main, w_small)


CONV_TC = 512
HALO = 8


def _shift_down(cur, prev, s):
    rolled = pltpu.roll(cur, s, 0)
    top = jnp.where(_iota((HALO, cur.shape[1]), 0) < s, pltpu.roll(prev, s, 0), rolled[:HALO])
    if cur.shape[0] == HALO:
        return top
    return jnp.concatenate([top, rolled[HALO:]], axis=0)


def _shift_up(cur, nxt, s):
    n = cur.shape[0]
    rolled = pltpu.roll(cur, n - s, 0)
    bot = jnp.where(_iota((HALO, cur.shape[1]), 0) >= HALO - s, pltpu.roll(nxt, HALO - s, 0), rolled[n - HALO:])
    return jnp.concatenate([rolled[:n - HALO], bot], axis=0)


def _conv_pre(cur, prev, w_ref, b):
    acc = cur * w_ref[3:4, :] + b
    shifted = [cur]
    for s in (1, 2, 3):
        sh = _shift_down(cur, prev, s)
        shifted.append(sh)
        acc = acc + sh * w_ref[3 - s:4 - s, :]
    return acc, shifted


def in_proj_conv(proj_main, u, w_main, w, b):
    t = u.shape[0]
    tm, tn = min(1024, t), CONV_TC
    rc = min(256, tm)
    c0, nj = COL_CONV // tn, (MAIN - COL_CONV) // tn

    def body(alias_ref, u_ref, wm_ref, w_ref, b_ref, pm_ref, out_ref, halo_ref):
        del alias_ref
        j = pl.program_id(1)

        @pl.when(pl.program_id(0) == 0)
        def _():
            halo_ref[j] = jnp.zeros((HALO, tn), F32)

        prev = halo_ref[j]
        for r in range(tm // rc):
            rows = pl.ds(r * rc, rc)
            p = _raw_dot(u_ref[rows, :], wm_ref[...], 1, 0)
            pm_ref[rows, :] = p
            pre, _ = _conv_pre(p, prev, w_ref, b_ref[...])
            out_ref[rows, :] = silu(pre)
            prev = p[rc - HALO:]
        halo_ref[j] = prev

    return pl.pallas_call(
        body, name="in_proj_conv", grid=(t // tm, nj),
        in_specs=[ANY, pl.BlockSpec((tm, D_MODEL), lambda i, j: (i, 0)),
                  pl.BlockSpec((D_MODEL, tn), lambda i, j: (0, c0 + j)),
                  pl.BlockSpec((4, tn), lambda i, j: (0, j)), pl.BlockSpec((1, tn), lambda i, j: (0, j))],
        out_specs=[pl.BlockSpec((tm, tn), lambda i, j: (i, c0 + j)), pl.BlockSpec((tm, tn), lambda i, j: (i, j))],
        out_shape=[jax.ShapeDtypeStruct(proj_main.shape, F32), jax.ShapeDtypeStruct((t, MAIN - COL_CONV), F32)],
        scratch_shapes=[pltpu.VMEM((nj, HALO, tn), F32)],
        input_output_aliases={0: 0},
        compiler_params=_params(("arbitrary", "arbitrary")),
    )(proj_main, u, w_main, w, b)


def _dsilu(pre):
    sg = sigmoid(pre)
    return sg * (1.0 + pre * (1.0 - sg))


def conv_bwd(dproj_main, proj_main, col0, width, w, b, dout, name):
    t = proj_main.shape[0]
    tt, c0 = min(512, t), col0 // CONV_TC
    nt = t // tt
    after = lambda i: jnp.minimum((i + 1) * (tt // HALO), t // HALO - 1)

    def body(alias_ref, cur_ref, prev_ref, nxt_ref, w_ref, b_ref, do_ref, do_nxt_ref, dx_ref, dwb_ref):
        del alias_ref
        i = pl.program_id(1)
        cur, bias = cur_ref[...], b_ref[...]
        prev = jnp.where(i > 0, prev_ref[...], 0.0)
        pre, shifted = _conv_pre(cur, prev, w_ref, bias)
        dpre = do_ref[...] * _dsilu(pre)
        pre_nxt, _ = _conv_pre(nxt_ref[...], cur[tt - HALO:], w_ref, bias)
        dpre_nxt = jnp.where(i < nt - 1, do_nxt_ref[...] * _dsilu(pre_nxt), 0.0)
        dx = dpre * w_ref[3:4, :]
        for s in (1, 2, 3):
            dx = dx + _shift_up(dpre, dpre_nxt, s) * w_ref[3 - s:4 - s, :]
        dx_ref[...] = dx.astype(dx_ref.dtype)
        row = _iota((HALO, CONV_TC), 0)
        upd = jnp.where(row == 4, jnp.sum(dpre, axis=0, keepdims=True), 0.0)
        for s in range(4):
            upd = upd + jnp.where(row == 3 - s, jnp.sum(dpre * shifted[s], axis=0, keepdims=True), 0.0)
        _accumulate(dwb_ref, i == 0, upd)

    return pl.pallas_call(
        body, name=name, grid=(width // CONV_TC, nt),
        in_specs=[ANY, pl.BlockSpec((tt, CONV_TC), lambda j, i: (i, c0 + j)),
                  pl.BlockSpec((HALO, CONV_TC), lambda j, i: (jnp.maximum(i * (tt // HALO) - 1, 0), c0 + j)),
                  pl.BlockSpec((HALO, CONV_TC), lambda j, i: (after(i), c0 + j)),
                  pl.BlockSpec((4, CONV_TC), lambda j, i: (0, j)), pl.BlockSpec((1, CONV_TC), lambda j, i: (0, j)),
                  pl.BlockSpec((tt, CONV_TC), lambda j, i: (i, j)),
                  pl.BlockSpec((HALO, CONV_TC), lambda j, i: (after(i), j))],
        out_specs=[pl.BlockSpec((tt, CONV_TC), lambda j, i: (i, c0 + j)),
                   pl.BlockSpec((HALO, CONV_TC), lambda j, i: (0, j))],
        out_shape=[jax.ShapeDtypeStruct(dproj_main.shape, dproj_main.dtype), jax.ShapeDtypeStruct((HALO, width), F32)],
        input_output_aliases={0: 0},
        compiler_params=_params(("arbitrary", "arbitrary")),
    )(dproj_main, proj_main, proj_main, proj_main, w, b, dout, dout)


def _ssd_cols(g):
    b0 = SSD_WIDTH + g * SSD_STATE
    c0 = SSD_WIDTH + SSD_GROUPS * SSD_STATE + g * SSD_STATE
    return slice(g * SSD_GW, (g + 1) * SSD_GW), slice(b0, b0 + SSD_STATE), slice(c0, c0 + SSD_STATE)


def _gdn_cols(j):
    return tuple(slice(s * GDN_W + j * GDN_DK, s * GDN_W + (j + 1) * GDN_DK) for s in range(3))


def _ssd_parts(xbc_ref):
    return tuple([xbc_ref[:, _ssd_cols(g)[s]] for g in range(SSD_GROUPS)] for s in range(3))


def _group_cols(ref):
    return [ref[:, g * SSD_GW:(g + 1) * SSD_GW] for g in range(SSD_GROUPS)]


def _chunk_rows(c):
    return slice(c * CHUNK, (c + 1) * CHUNK)


def _gdn_parts(qkv_ref):
    assert GDN_HB == GDN_HEADS, "the conv block is read whole: one grid step holds every head"
    return tuple([[qkv_ref[_chunk_rows(c), _gdn_cols(j)[s]] for j in range(GDN_HB)] for c in range(GDN_CB)]
                 for s in range(3))


def _head_cols(ref):
    return [[ref[_chunk_rows(c), j * GDN_DV:(j + 1) * GDN_DV] for j in range(GDN_HB)] for c in range(GDN_CB)]


def _chunk_blocks(ref):
    return [ref[_chunk_rows(c), :] for c in range(GDN_CB)]


def _first_head():
    return 0 if GDN_HB == GDN_HEADS else pl.program_id(1) * GDN_HB


def ssd_fwd(conv_ssd, proj_main, proj_small, normw, alog, dtb, dvec):
    t = conv_ssd.shape[0]
    nc = t // CHUNK

    groups = range(SSD_GROUPS)

    def body(xbc_ref, z_ref, sm_ref, nw_ref, al_ref, db_ref, dv_ref, y_ref, hist_ref, state_ref):
        @pl.when(pl.program_id(0) == 0)
        def _():
            state_ref[...] = jnp.zeros(state_ref.shape, F32)

        states = [state_ref[g] for g in groups]
        for g in groups:
            hist_ref[0, g] = states[g]
        ys, new_states = ssd_chunk(*_ssd_parts(xbc_ref), _group_cols(z_ref), sm_ref[...], _group_cols(nw_ref),
                                   al_ref[...], db_ref[...], dv_ref[...], states)
        for g in groups:
            y_ref[:, g * SSD_GW:(g + 1) * SSD_GW] = ys[g].astype(MXU_DTYPE)
            state_ref[g] = new_states[g]

    return pl.pallas_call(
        body, name="ssd_fwd", grid=(nc,),
        in_specs=[pl.BlockSpec((CHUNK, SSD_CONV), lambda c: (c, (COL_SSD - COL_CONV) // SSD_CONV)),
                  pl.BlockSpec((CHUNK, SSD_WIDTH), lambda c: (c, COL_Z // SSD_WIDTH)),
                  pl.BlockSpec((CHUNK, LANES), lambda c: (c, 0)),
                  _full((1, SSD_WIDTH)), _full((1, LANES)), _full((1, LANES)), _full((1, LANES))],
        out_specs=[pl.BlockSpec((CHUNK, SSD_WIDTH), lambda c: (c, 0)),
                   pl.BlockSpec((1, SSD_GROUPS, SSD_STATE, SSD_GW), lambda c: (c, 0, 0, 0))],
        out_shape=[jax.ShapeDtypeStruct((t, SSD_WIDTH), MXU_DTYPE),
                   jax.ShapeDtypeStruct((nc, SSD_GROUPS, SSD_STATE, SSD_GW), F32)],
        scratch_shapes=[pltpu.VMEM((SSD_GROUPS, SSD_STATE, SSD_GW), F32)],
        compiler_params=_params(("arbitrary",)),
    )(conv_ssd, proj_main, proj_small, normw, alog, dtb, dvec)


def _accumulate(ref, first, value):
    @pl.when(first)
    def _():
        ref[...] = value

    @pl.when(jnp.logical_not(first))
    def _():
        ref[...] += value


def ssd_bwd(conv_ssd, proj_main, proj_small, normw, alog, dtb, dvec, hist, dy):
    t = conv_ssd.shape[0]
    nc = t // CHUNK
    rev = lambda c: nc - 1 - c
    groups = range(SSD_GROUPS)

    def body(xbc_ref, z_ref, sm_ref, nw_ref, al_ref, db_ref, dv_ref, hist_ref, dy_ref,
             dxbc_ref, dz_ref, dsm_ref, dnw_ref, dal_ref, ddb_ref, ddv_ref, dstate_ref):
        first = pl.program_id(0) == 0

        @pl.when(first)
        def _():
            dstate_ref[...] = jnp.zeros(dstate_ref.shape, F32)

        _, vjp = jax.vjp(ssd_chunk, *_ssd_parts(xbc_ref), _group_cols(z_ref), sm_ref[...], _group_cols(nw_ref),
                         al_ref[...], db_ref[...], dv_ref[...], [hist_ref[0, g] for g in groups])
        dxs, dbm, dcm, dz, dsm, dnw, dal, ddb, ddv, dstate = vjp(
            (_group_cols(dy_ref), [dstate_ref[g] for g in groups]))
        for g in groups:
            xc, bc, cc = _ssd_cols(g)
            dxbc_ref[:, xc] = dxs[g]
            dxbc_ref[:, bc] = dbm[g]
            dxbc_ref[:, cc] = dcm[g]
            dz_ref[:, g * SSD_GW:(g + 1) * SSD_GW] = dz[g].astype(dz_ref.dtype)
            dstate_ref[g] = dstate[g]
        dsm_ref[...] = dsm
        _accumulate(dnw_ref, first, join_lanes(dnw))
        _accumulate(dal_ref, first, dal)
        _accumulate(ddb_ref, first, ddb)
        _accumulate(ddv_ref, first, ddv)

    return pl.pallas_call(
        body, name="ssd_bwd", grid=(nc,),
        in_specs=[pl.BlockSpec((CHUNK, SSD_CONV), lambda c: (rev(c), (COL_SSD - COL_CONV) // SSD_CONV)),
                  pl.BlockSpec((CHUNK, SSD_WIDTH), lambda c: (rev(c), COL_Z // SSD_WIDTH)),
                  pl.BlockSpec((CHUNK, LANES), lambda c: (rev(c), 0)),
                  _full((1, SSD_WIDTH)), _full((1, LANES)), _full((1, LANES)), _full((1, LANES)),
                  pl.BlockSpec((1, SSD_GROUPS, SSD_STATE, SSD_GW), lambda c: (rev(c), 0, 0, 0)),
                  pl.BlockSpec((CHUNK, SSD_WIDTH), lambda c: (rev(c), 0))],
        out_specs=[pl.BlockSpec((CHUNK, SSD_CONV), lambda c: (rev(c), 0)),
                   pl.BlockSpec((CHUNK, SSD_WIDTH), lambda c: (rev(c), COL_Z // SSD_WIDTH)),
                   pl.BlockSpec((CHUNK, LANES), lambda c: (rev(c), 0)),
                   _full((1, SSD_WIDTH)), _full((1, LANES)), _full((1, LANES)), _full((1, LANES))],
        out_shape=[jax.ShapeDtypeStruct((t, SSD_CONV), F32), jax.ShapeDtypeStruct((t, MAIN), MXU_DTYPE),
                   jax.ShapeDtypeStruct((t, LANES), F32), jax.ShapeDtypeStruct((1, SSD_WIDTH), F32),
                   jax.ShapeDtypeStruct((1, LANES), F32), jax.ShapeDtypeStruct((1, LANES), F32),
                   jax.ShapeDtypeStruct((1, LANES), F32)],
        scratch_shapes=[pltpu.VMEM((SSD_GROUPS, SSD_STATE, SSD_GW), F32)],
        compiler_params=_params(("arbitrary",)),
    )(conv_ssd, proj_main, proj_small, normw, alog, dtb, dvec, hist, dy)


def gdn_fwd(conv_gdn, proj_main, proj_small, normw, alog, dtb):
    t = conv_gdn.shape[0]
    hb, cb = GDN_HB, GDN_CB
    rows = CHUNK * cb
    ns = t // rows
    gate_blk = COL_GATE // (GDN_DV * hb)

    def body(qkv_ref, gate_ref, sm_ref, nw_ref, al_ref, db_ref, y_ref, hist_ref, t_ref, state_ref):
        h0 = _first_head()

        @pl.when(pl.program_id(0) == 0)
        def _():
            for j in range(hb):
                state_ref[h0 + j] = jnp.zeros((GDN_DK, GDN_DV), F32)

        states = [state_ref[h0 + j] for j in range(hb)]
        for j in range(hb):
            hist_ref[0, j] = states[j]
        qs, ks, vs = _gdn_parts(qkv_ref)
        ys, new_states, ts = gdn_chunk(h0, qs, ks, vs, _chunk_blocks(sm_ref), _head_cols(gate_ref), nw_ref[...],
                                       al_ref[...], db_ref[...], states)
        for c in range(cb):
            for j in range(hb):
                y_ref[_chunk_rows(c), j * GDN_DV:(j + 1) * GDN_DV] = ys[c][j].astype(MXU_DTYPE)
        for j in range(hb):
            state_ref[h0 + j] = new_states[j]
        for p in range(cb * hb // 2):
            t_ref[0, p] = ts[p]

    return pl.pallas_call(
        body, name="gdn_fwd", grid=(ns, GDN_HEADS // hb),
        in_specs=[pl.BlockSpec((rows, GDN_HC * hb), lambda c, h: (c, h)),
                  pl.BlockSpec((rows, GDN_DV * hb), lambda c, h: (c, gate_blk + h)),
                  pl.BlockSpec((rows, LANES), lambda c, h: (c, 0)),
                  _full((1, GDN_DV)), _full((1, LANES)), _full((1, LANES))],
        out_specs=[pl.BlockSpec((rows, GDN_DV * hb), lambda c, h: (c, h)),
                   pl.BlockSpec((1, hb, GDN_DK, GDN_DV), lambda c, h: (c, h, 0, 0)),
                   pl.BlockSpec((1, cb * hb // 2, CHUNK, LANES), lambda c, h: (c, h, 0, 0))],
        out_shape=[jax.ShapeDtypeStruct((t, GDN_W), MXU_DTYPE),
                   jax.ShapeDtypeStruct((ns, GDN_HEADS, GDN_DK, GDN_DV), F32),
                   jax.ShapeDtypeStruct((ns, cb * GDN_HEADS // 2, CHUNK, LANES), F32)],
        scratch_shapes=[pltpu.VMEM((GDN_HEADS, GDN_DK, GDN_DV), F32)],
        compiler_params=_params(("arbitrary", "arbitrary")),
    )(conv_gdn, proj_main, proj_small, normw, alog, dtb)


def gdn_bwd(dproj_main, conv_gdn, proj_main, proj_small, normw, alog, dtb, hist, t_inv, dy):
    t = conv_gdn.shape[0]
    hb, cb = GDN_HB, GDN_CB
    rows = CHUNK * cb
    ns = t // rows
    rev = lambda c: ns - 1 - c
    gate_blk = COL_GATE // (GDN_DV * hb)

    def body(alias_ref, qkv_ref, gate_ref, sm_ref, nw_ref, al_ref, db_ref, hist_ref, t_ref, dy_ref,
             dgate_ref, dqkv_ref, dsm_ref, dnw_ref, dal_ref, ddb_ref, dstate_ref):
        del alias_ref
        c, h = pl.program_id(0), pl.program_id(1)
        h0 = _first_head()

        @pl.when(c == 0)
        def _():
            for j in range(hb):
                dstate_ref[h0 + j] = jnp.zeros((GDN_DK, GDN_DV), F32)

        saved = [t_ref[0, p] for p in range(cb * hb // 2)]

        def fn(qs, ks, vs, smalls, gates, nw, al, db, states):
            return gdn_chunk(h0, qs, ks, vs, smalls, gates, nw, al, db, states, saved)[:2]

        qs, ks, vs = _gdn_parts(qkv_ref)
        _, vjp = jax.vjp(fn, qs, ks, vs, _chunk_blocks(sm_ref), _head_cols(gate_ref), nw_ref[...], al_ref[...],
                         db_ref[...], [hist_ref[0, j] for j in range(hb)])
        dqs, dks, dvs, dsm, dgates, dnw, dal, ddb, dstates = vjp(
            (_head_cols(dy_ref), [dstate_ref[h0 + j] for j in range(hb)]))
        for k in range(cb):
            rk = _chunk_rows(k)
            for j in range(hb):
                qc, kc, vc = _gdn_cols(j)
                dqkv_ref[rk, qc] = dqs[k][j]
                dqkv_ref[rk, kc] = dks[k][j]
                dqkv_ref[rk, vc] = dvs[k][j]
                dgate_ref[rk, j * GDN_DV:(j + 1) * GDN_DV] = dgates[k][j].astype(dgate_ref.dtype)
        for j in range(hb):
            dstate_ref[h0 + j] = dstates[j]
        _accumulate(dsm_ref, h == 0, jnp.concatenate(dsm, axis=0))
        first = jnp.logical_and(c == 0, h == 0)
        _accumulate(dnw_ref, first, dnw)
        _accumulate(dal_ref, first, dal)
        _accumulate(ddb_ref, first, ddb)

    return pl.pallas_call(
        body, name="gdn_bwd", grid=(ns, GDN_HEADS // hb),
        in_specs=[ANY, pl.BlockSpec((rows, GDN_HC * hb), lambda c, h: (rev(c), h)),
                  pl.BlockSpec((rows, GDN_DV * hb), lambda c, h: (rev(c), gate_blk + h)),
                  pl.BlockSpec((rows, LANES), lambda c, h: (rev(c), 0)),
                  _full((1, GDN_DV)), _full((1, LANES)), _full((1, LANES)),
                  pl.BlockSpec((1, hb, GDN_DK, GDN_DV), lambda c, h: (rev(c), h, 0, 0)),
                  pl.BlockSpec((1, cb * hb // 2, CHUNK, LANES), lambda c, h: (rev(c), h, 0, 0)),
                  pl.BlockSpec((rows, GDN_DV * hb), lambda c, h: (rev(c), h))],
        out_specs=[pl.BlockSpec((rows, GDN_DV * hb), lambda c, h: (rev(c), gate_blk + h)),
                   pl.BlockSpec((rows, GDN_HC * hb), lambda c, h: (rev(c), h)),
                   pl.BlockSpec((rows, LANES), lambda c, h: (rev(c), 0)),
                   _full((1, GDN_DV)), _full((1, LANES)), _full((1, LANES))],
        out_shape=[jax.ShapeDtypeStruct(dproj_main.shape, dproj_main.dtype), jax.ShapeDtypeStruct((t, GDN_CONV), F32),
                   jax.ShapeDtypeStruct((t, LANES), F32), jax.ShapeDtypeStruct((1, GDN_DV), F32),
                   jax.ShapeDtypeStruct((1, LANES), F32), jax.ShapeDtypeStruct((1, LANES), F32)],
        scratch_shapes=[pltpu.VMEM((GDN_HEADS, GDN_DK, GDN_DV), F32)],
        input_output_aliases={0: 0},
        compiler_params=_params(("arbitrary", "arbitrary")),
    )(dproj_main, conv_gdn, proj_main, proj_small, normw, alog, dtb, hist, t_inv, dy)


def out_proj_loss(x, y_ssd, y_gdn, w_out, final_w, target):
    t = x.shape[0]
    tm = min(256, t)

    def body(x_ref, ys_ref, yg_ref, wo_ref, fw_ref, tg_ref, loss_ref, dhid_ref, dys_ref, dyg_ref, dwo_ref, dfw_ref):
        i = pl.program_id(0)
        ys, yg = ys_ref[...], yg_ref[...]
        wo_s, wo_g = wo_ref[:SSD_WIDTH, :], wo_ref[SSD_WIDTH:, :]
        hid = x_ref[...] + _raw_dot(ys, wo_s, 1, 0) + _raw_dot(yg, wo_g, 1, 0)
        out, vjp = jax.vjp(rmsnorm, hid, fw_ref[...])
        err = out - tg_ref[...]
        loss = 0.5 * jnp.sum(jnp.mean(err * err, axis=-1, keepdims=True), axis=0, keepdims=True)
        dhid, dfw = vjp(err * (1.0 / D_MODEL))
        dhid_ref[...] = dhid
        dys_ref[...] = _raw_dot(dhid, wo_s, 1, 1)
        dyg_ref[...] = _raw_dot(dhid, wo_g, 1, 1)
        first = i == 0
        _accumulate(loss_ref, first, jnp.broadcast_to(loss, loss_ref.shape))
        _accumulate(dfw_ref, first, dfw)

        @pl.when(first)
        def _():
            dwo_ref[:SSD_WIDTH, :] = _raw_dot(ys, dhid, 0, 0)
            dwo_ref[SSD_WIDTH:, :] = _raw_dot(yg, dhid, 0, 0)

        @pl.when(i > 0)
        def _():
            dwo_ref[:SSD_WIDTH, :] += _raw_dot(ys, dhid, 0, 0)
            dwo_ref[SSD_WIDTH:, :] += _raw_dot(yg, dhid, 0, 0)

    row = lambda w: pl.BlockSpec((tm, w), lambda i: (i, 0))
    return pl.pallas_call(
        body, name="out_proj_loss", grid=(t // tm,),
        in_specs=[row(D_MODEL), row(SSD_WIDTH), row(GDN_W), _full((SSD_WIDTH + GDN_W, D_MODEL)), _full((1, D_MODEL)),
                  row(D_MODEL)],
        out_specs=[_full((8, LANES)), row(D_MODEL), row(SSD_WIDTH), row(GDN_W), _full((SSD_WIDTH + GDN_W, D_MODEL)),
                   _full((1, D_MODEL))],
        out_shape=[jax.ShapeDtypeStruct((8, LANES), F32), jax.ShapeDtypeStruct((t, D_MODEL), F32),
                   jax.ShapeDtypeStruct((t, SSD_WIDTH), F32), jax.ShapeDtypeStruct((t, GDN_W), F32),
                   jax.ShapeDtypeStruct((SSD_WIDTH + GDN_W, D_MODEL), F32), jax.ShapeDtypeStruct((1, D_MODEL), F32)],
        compiler_params=_params(("arbitrary",)),
    )(x, y_ssd, y_gdn, w_out, final_w, target)


def in_proj_bwd_x(x, normw, w_main, w_small, dproj_main, dsmall_a, dsmall_b, dhid, slabbed):
    t = x.shape[0]
    tm = min(256, t)
    ni = t // tm
    ns = len(slabbed)

    def body(x_ref, nw_ref, wm_ref, ws_ref, dp_ref, da_ref, db_ref, dh_ref, *rest):
        slab_refs, (gx_ref, dnw_ref), land_refs = rest[:ns], rest[ns:ns + 2], rest[ns + 2:2 * ns + 2]
        sems = rest[2 * ns + 2:]
        i = pl.program_id(0)
        start, finish = _slab_exchange(slab_refs, land_refs, ns, *sems)

        @pl.when(i == 0)
        def _():
            start()

        du = _raw_dot(dp_ref[...], wm_ref[...], 1, 1) + _raw_dot(da_ref[...] + db_ref[...], ws_ref[...], 1, 1)
        _, vjp = jax.vjp(rmsnorm, x_ref[...], nw_ref[...])
        dx, dnw = vjp(du)
        gx_ref[...] = dx + dh_ref[...]
        _accumulate(dnw_ref, i == 0, dnw)

        @pl.when(i == ni - 1)
        def _():
            finish()

    row = lambda w: pl.BlockSpec((tm, w), lambda i: (i, 0))
    out = pl.pallas_call(
        body, name="in_proj_bwd_x", grid=(ni,),
        in_specs=[row(D_MODEL), _full((1, D_MODEL)), _full((D_MODEL, MAIN)), _full((D_MODEL, LANES)), row(MAIN),
                  row(LANES), row(LANES), row(D_MODEL)] + [HBM] * ns,
        out_specs=[row(D_MODEL), _full((1, D_MODEL))] + [HBM] * ns,
        out_shape=[jax.ShapeDtypeStruct((t, D_MODEL), F32), jax.ShapeDtypeStruct((1, D_MODEL), F32)]
        + _slab_exchange_shapes(slabbed, []),
        scratch_shapes=_slab_exchange_sems(ns),
        compiler_params=_params(("arbitrary",)),
    )(x, normw, w_main, w_small, dproj_main, dsmall_a, dsmall_b, dhid, *slabbed)
    return out[0], out[1], out[2:]


def in_proj_bwd_w(u, dproj_main, dsmall_a, dsmall_b, slabbed):
    t = u.shape[0]
    tm, tn = min(1024, t), MAIN // 4
    nj, ni = MAIN // tn, t // tm
    ns = len(slabbed)

    def body(u_ref, dp_ref, da_ref, db_ref, *rest):
        slab_refs, (dwm_ref, dws_ref), land_refs, sems = rest[:ns], rest[ns:ns + 2], rest[ns + 2:2 * ns + 2], rest[2 * ns + 2:]
        j, i = pl.program_id(0), pl.program_id(1)
        start, finish = _slab_exchange(slab_refs, land_refs, ns, *sems)

        @pl.when(jnp.logical_and(j == 0, i == 0))
        def _():
            start()

        uu = u_ref[...]
        _accumulate(dwm_ref, i == 0, _raw_dot(uu, dp_ref[...], 0, 0))

        @pl.when(j == 0)
        def _():
            _accumulate(dws_ref, i == 0, _raw_dot(uu, da_ref[...] + db_ref[...], 0, 0))

        @pl.when(jnp.logical_and(j == nj - 1, i == ni - 1))
        def _():
            finish()

    out = pl.pallas_call(
        body, name="in_proj_bwd_w", grid=(nj, ni),
        in_specs=[pl.BlockSpec((tm, D_MODEL), lambda j, i: (i, 0)), pl.BlockSpec((tm, tn), lambda j, i: (i, j)),
                  pl.BlockSpec((tm, LANES), lambda j, i: (i, 0)), pl.BlockSpec((tm, LANES), lambda j, i: (i, 0))]
        + [HBM] * ns,
        out_specs=[pl.BlockSpec((D_MODEL, tn), lambda j, i: (0, j)), _full((D_MODEL, LANES))] + [HBM] * ns,
        out_shape=[jax.ShapeDtypeStruct((D_MODEL, MAIN), F32), jax.ShapeDtypeStruct((D_MODEL, LANES), F32)]
        + _slab_exchange_shapes(slabbed, []),
        scratch_shapes=_slab_exchange_sems(ns),
        compiler_params=_params(("arbitrary", "arbitrary")),
    )(u, dproj_main, dsmall_a, dsmall_b, *slabbed)
    return out[0], out[1], out[2:]


def sum_slabs(a, name):
    n, rows, cols = a.shape
    tr = 64 if rows % 64 == 0 else rows

    def body(a_ref, o_ref):
        acc = a_ref[0].astype(F32)
        for d in range(1, n):
            acc = acc + a_ref[d].astype(F32)
        o_ref[...] = acc

    return pl.pallas_call(
        body, name=name, grid=(rows // tr,),
        in_specs=[pl.BlockSpec((n, tr, cols), lambda i: (0, i, 0))],
        out_specs=pl.BlockSpec((tr, cols), lambda i: (i, 0)),
        out_shape=jax.ShapeDtypeStruct((rows, cols), F32),
        compiler_params=_params(("arbitrary",)),
    )(a)


def adamw(w, g, m, v, name):
    rows, cols = w.shape
    tr = 128 if rows % 128 == 0 else rows

    def body(w_ref, g_ref, m_ref, v_ref, d_ref, nm_ref, nv_ref):
        gg = g_ref[...]
        nm = ADAM_B1 * m_ref[...] + (1.0 - ADAM_B1) * gg
        nv = ADAM_B2 * v_ref[...] + (1.0 - ADAM_B2) * (gg * gg)
        m_hat = nm / (1.0 - ADAM_B1 ** ADAM_STEP)
        v_hat = nv / (1.0 - ADAM_B2 ** ADAM_STEP)
        d_ref[...] = -ADAM_LR * (m_hat / (jnp.sqrt(v_hat) + ADAM_EPS) + ADAM_WD * w_ref[...])
        nm_ref[...] = nm
        nv_ref[...] = nv

    spec = pl.BlockSpec((tr, cols), lambda i: (i, 0))
    shp = jax.ShapeDtypeStruct((rows, cols), F32)
    return pl.pallas_call(
        body, name=name, grid=(rows // tr,), in_specs=[spec] * 4, out_specs=[spec] * 3, out_shape=[shp] * 3,
        compiler_params=_params(("arbitrary",)),
    )(w, g, m, v)


def _my_place():
    return lax.axis_index("x"), lax.axis_index("y"), lax.axis_index("c")


def gather_weights(big, small):
    nb, n = len(big), len(big) + len(small)
    parts = 4

    def body(*refs):
        srcs, outs = refs[:n], refs[n:2 * n]
        land_a, land_b = refs[2 * n:2 * n + nb], refs[2 * n + nb:2 * n + 2 * nb]
        send_sems, recv_sems, fwd_send, fwd_recv, local_sems = refs[2 * n + 2 * nb:]
        x, y, c = _my_place()
        me = 2 * x + y
        chips = [(1 - x, y), (x, 1 - y), (1 - x, 1 - y)]
        half = [a.shape[0] // 2 for a in big]

        def ici(j, i):
            px, py = chips[j]
            if i < nb:
                src, dst = srcs[i].at[pl.ds(c * half[i], half[i])], land_a[i].at[j]
            else:
                src, dst = srcs[i], outs[i].at[me]
            return pltpu.make_async_remote_copy(src_ref=src, dst_ref=dst, send_sem=send_sems.at[j * n + i],
                                                recv_sem=recv_sems.at[j * n + i], device_id=(px, py, c),
                                                device_id_type=MESH)

        def ici_arrival(j, i):
            px, py = chips[j]
            dst = land_a[i].at[j] if i < nb else outs[i].at[2 * px + py]
            return pltpu.make_async_remote_copy(src_ref=dst, dst_ref=dst, send_sem=send_sems.at[j * n + i],
                                                recv_sem=recv_sems.at[j * n + i], device_id=(px, py, c),
                                                device_id_type=MESH)

        def forward(j, i, p):
            rows = half[i] // parts
            k = (j * nb + i) * parts + p
            return pltpu.make_async_remote_copy(
                src_ref=land_a[i].at[j, pl.ds(p * rows, rows)], dst_ref=land_b[i].at[j, pl.ds(p * rows, rows)],
                send_sem=fwd_send.at[k], recv_sem=fwd_recv.at[k], device_id=(x, y, 1 - c), device_id_type=MESH)

        def store(j, i, from_sibling):
            px, py = chips[j]
            buf, h = (land_b, 1 - c) if from_sibling else (land_a, c)
            k = n + (j * nb + i) * 2 + (1 if from_sibling else 0)
            return pltpu.make_async_copy(buf[i].at[j], outs[i].at[2 * px + py, pl.ds(h * half[i], half[i])],
                                         local_sems.at[k])

        own = [pltpu.make_async_copy(srcs[i], outs[i].at[me], local_sems.at[i]) for i in range(n)]
        sends = [ici(j, i) for j in range(3) for i in range(n)]
        for cp in own + sends:
            cp.start()
        pending = []
        for j in range(3):
            for i in range(n):
                ici_arrival(j, i).wait_recv()
                if i < nb:
                    fw = [forward(j, i, p) for p in range(parts)]
                    st = store(j, i, False)
                    for cp in fw + [st]:
                        cp.start()
                    pending += [cp.wait_send for cp in fw] + [st.wait]
        for j in range(3):
            for i in range(nb):
                for p in range(parts):
                    forward(j, i, p).wait_recv()
                st = store(j, i, True)
                st.start()
                pending.append(st.wait)
        for cp in sends:
            cp.wait_send()
        for wait in pending:
            wait()
        for cp in own:
            cp.wait()

    shards = list(big) + list(small)
    lands = [pltpu.VMEM((3, a.shape[0] // 2) + a.shape[1:], a.dtype) for a in big]
    return pl.pallas_call(
        body, name="gather_weights",
        in_specs=[HBM] * n, out_specs=[HBM] * n,
        out_shape=[jax.ShapeDtypeStruct((N_CHIP,) + s.shape, s.dtype) for s in shards],
        scratch_shapes=lands + lands + [
            pltpu.SemaphoreType.DMA((3 * n,)), pltpu.SemaphoreType.DMA((3 * n,)),
            pltpu.SemaphoreType.DMA((3 * nb * parts,)), pltpu.SemaphoreType.DMA((3 * nb * parts,)),
            pltpu.SemaphoreType.DMA((n + 6 * nb,))],
        compiler_params=pltpu.CompilerParams(vmem_limit_bytes=VMEM_LIMIT),
    )(*shards)


def _peer(x, y, c, mask):
    mx, my, mc = (mask >> 2) & 1, (mask >> 1) & 1, mask & 1
    return (x ^ mx if mx else x, y ^ my if my else y, c ^ mc if mc else c)


def _slab_exchange_shapes(slabbed, replicated):
    return ([jax.ShapeDtypeStruct(a.shape, a.dtype) for a in slabbed]
            + [jax.ShapeDtypeStruct((N_DEV,) + a.shape, a.dtype) for a in replicated])


def _slab_exchange_sems(n):
    return [pltpu.SemaphoreType.DMA((7 * n,)), pltpu.SemaphoreType.DMA((7 * n,)), pltpu.SemaphoreType.DMA((n,))]


def _slab_exchange(srcs, outs, ns, send_sems, recv_sems, local_sems):
    n = len(srcs)
    x, y, c = _my_place()
    me = 4 * x + 2 * y + c

    def piece(i, dev):
        return srcs[i].at[dev] if i < ns else srcs[i]

    def copies(arriving):
        out = []
        for mask in range(1, N_DEV):
            px, py, pc = _peer(x, y, c, mask)
            dev = 4 * px + 2 * py + pc
            for i in range(n):
                k = (mask - 1) * n + i
                out.append(pltpu.make_async_remote_copy(
                    src_ref=piece(i, dev), dst_ref=outs[i].at[dev if arriving else me], send_sem=send_sems.at[k],
                    recv_sem=recv_sems.at[k], device_id=(px, py, pc), device_id_type=MESH))
        return out

    def local():
        return [pltpu.make_async_copy(piece(i, me), outs[i].at[me], local_sems.at[i]) for i in range(n)]

    def start():
        for cp in local() + copies(False):
            cp.start()

    def finish():
        for cp in copies(True):
            cp.wait_recv()
        for cp in copies(False):
            cp.wait_send()
        for cp in local():
            cp.wait()

    return start, finish


def exchange_halves(halves, replicated):
    n, nr = len(halves), len(replicated)
    streams = 8

    def body(*refs):
        srcs, rep_srcs, outs, rep_outs = refs[:n], refs[n:n + nr], refs[n + nr:2 * n + nr], refs[2 * n + nr:2 * (n + nr)]
        refs = refs[2 * (n + nr):]
        mine, theirs = refs[:n], refs[n:2 * n]
        send_sems, recv_sems, in_sems, out_sems = refs[2 * n:2 * n + 4]
        rep_start, rep_finish = _slab_exchange(rep_srcs, rep_outs, 0, *refs[2 * n + 4:])
        rep_start()
        x, y, c = _my_place()
        loads = [pltpu.make_async_copy(srcs[i], mine[i], in_sems.at[i]) for i in range(n)]
        for cp in loads:
            cp.start()
        for cp in loads:
            cp.wait()

        def chunk_copy(i, s):
            rows = halves[i].shape[0] // streams
            k = i * streams + s
            return pltpu.make_async_remote_copy(
                src_ref=mine[i].at[pl.ds(s * rows, rows)], dst_ref=theirs[i].at[pl.ds(s * rows, rows)],
                send_sem=send_sems.at[k], recv_sem=recv_sems.at[k], device_id=(x, y, 1 - c), device_id_type=MESH)

        sends = [chunk_copy(i, s) for i in range(n) for s in range(streams)]
        for cp in sends:
            cp.start()
        own = [pltpu.make_async_copy(mine[i], outs[i].at[c], out_sems.at[i]) for i in range(n)]
        for cp in own:
            cp.start()
        for cp in sends:
            cp.wait_recv()
        got = [pltpu.make_async_copy(theirs[i], outs[i].at[1 - c], out_sems.at[n + i]) for i in range(n)]
        for cp in got:
            cp.start()
        for cp in sends:
            cp.wait_send()
        for cp in own + got:
            cp.wait()
        rep_finish()

    vmem = [pltpu.VMEM(a.shape, a.dtype) for a in halves]
    out = pl.pallas_call(
        body, name="exchange_halves",
        in_specs=[HBM] * (n + nr), out_specs=[HBM] * (n + nr),
        out_shape=[jax.ShapeDtypeStruct((2,) + a.shape, a.dtype) for a in halves]
        + _slab_exchange_shapes([], replicated),
        scratch_shapes=vmem + vmem + [pltpu.SemaphoreType.DMA((n * streams,)), pltpu.SemaphoreType.DMA((n * streams,)),
                                      pltpu.SemaphoreType.DMA((n,)), pltpu.SemaphoreType.DMA((2 * n,))]
        + _slab_exchange_sems(nr),
        compiler_params=pltpu.CompilerParams(vmem_limit_bytes=VMEM_LIMIT),
    )(*halves, *replicated)
    return out[:n], out[n:]


def _pack_cols(pieces):
    offs, pos = [], 0
    for a in pieces:
        offs.append(pos)
        pos += a.shape[1]
    rows8 = [jnp.pad(a.astype(F32), ((0, 8 - a.shape[0]), (0, 0))) for a in pieces]
    return jnp.concatenate(rows8, axis=1), offs


def adamw_many(ws, gs, ms, vs):
    n = len(ws)

    def body(*refs):
        w_r, g_r, m_r, v_r = refs[:n], refs[n:2 * n], refs[2 * n:3 * n], refs[3 * n:4 * n]
        d_o, m_o, v_o = refs[4 * n:5 * n], refs[5 * n:6 * n], refs[6 * n:7 * n]
        for i in range(n):
            gg = g_r[i][...]
            nm = ADAM_B1 * m_r[i][...] + (1.0 - ADAM_B1) * gg
            nv = ADAM_B2 * v_r[i][...] + (1.0 - ADAM_B2) * (gg * gg)
            m_hat = nm / (1.0 - ADAM_B1 ** ADAM_STEP)
            v_hat = nv / (1.0 - ADAM_B2 ** ADAM_STEP)
            d_o[i][...] = -ADAM_LR * (m_hat / (jnp.sqrt(v_hat) + ADAM_EPS) + ADAM_WD * w_r[i][...])
            m_o[i][...] = nm
            v_o[i][...] = nv

    shapes = [jax.ShapeDtypeStruct(w.shape, F32) for w in ws]
    out = pl.pallas_call(body, name="adamw_small", out_shape=shapes * 3,
                         compiler_params=pltpu.CompilerParams(vmem_limit_bytes=VMEM_LIMIT))(*ws, *gs, *ms, *vs)
    return out[:n], out[n:2 * n], out[2 * n:]


def _lanes(vec, start):
    n = vec.shape[-1]
    return jnp.pad(vec.reshape(1, n).astype(F32), ((0, 0), (start, LANES - start - n)))


def kernel(x, norm_w, w_in, ssd_conv_w, ssd_conv_b, ssd_dt_bias, ssd_a_log, ssd_d, ssd_norm_w, gdn_conv_w, gdn_dt_bias, gdn_a_log, gdn_norm_w, w_out, final_norm_w, loss_target, m_norm_w, m_w_in, m_ssd_conv_w, m_ssd_conv_b, m_ssd_dt_bias, m_ssd_a_log, m_ssd_d, m_ssd_norm_w, m_gdn_conv_w, m_gdn_dt_bias, m_gdn_a_log, m_gdn_norm_w, m_w_out, m_final_norm_w, v_norm_w, v_w_in, v_ssd_conv_w, v_ssd_conv_b, v_ssd_dt_bias, v_ssd_a_log, v_ssd_d, v_ssd_norm_w, v_gdn_conv_w, v_gdn_dt_bias, v_gdn_a_log, v_gdn_norm_w, v_w_out, v_final_norm_w):
    xs = x[0]
    target = loss_target[0]
    chip = 2 * lax.axis_index("x") + lax.axis_index("y")
    w_in_shard, w_out_shard = w_in[0], w_out[0]
    in_cols = w_in_shard.shape[1]
    out_rows = w_out_shard.shape[0]

    g_in, g_out, g_cs, g_cg = gather_weights(
        [w_in_shard.astype(MXU_DTYPE), w_out_shard.astype(MXU_DTYPE)], [ssd_conv_w[0], gdn_conv_w[0]])
    w_in_full = jnp.concatenate([g_in[k] for k in range(N_CHIP)], axis=1)
    w_out_full = g_out.reshape(N_CHIP * out_rows, D_MODEL)
    cw_ssd = jnp.concatenate([g_cs[k] for k in range(N_CHIP)], axis=1)
    cw_gdn = jnp.concatenate([g_cg[k] for k in range(N_CHIP)], axis=1)
    cb_ssd, cb_gdn = ssd_conv_b, jnp.zeros((1, GDN_CONV), F32)
    o_xbc, o_dt, o_gate, o_qkv, o_ab = 1024, 2560, 2576, 3600, 6672
    w_main = jnp.concatenate([w_in_full[:, :o_xbc], w_in_full[:, o_gate:o_qkv], w_in_full[:, o_qkv:o_ab],
                              w_in_full[:, o_xbc:o_dt]], axis=1)
    w_small = jnp.concatenate([w_in_full[:, o_dt:o_gate], w_in_full[:, o_ab:],
                               jnp.zeros((D_MODEL, LANES - 32), MXU_DTYPE)], axis=1)
    alog = _lanes(ssd_a_log, 0) + _lanes(gdn_a_log, LANE_GA)
    dtb = _lanes(ssd_dt_bias, 0) + _lanes(gdn_dt_bias, LANE_GA)
    dvec = _lanes(ssd_d, 0)
    fw = final_norm_w.reshape(1, D_MODEL)

    proj_main, proj_small, u = in_proj(xs, norm_w, w_main, w_small)
    proj_main, conv_out = in_proj_conv(proj_main, u, w_main, jnp.concatenate([cw_gdn, cw_ssd], axis=1),
                                       jnp.concatenate([cb_gdn, cb_ssd], axis=1))
    conv_ssd = conv_gdn = conv_out
    y_ssd, hist_ssd = ssd_fwd(conv_ssd, proj_main, proj_small, ssd_norm_w, alog, dtb, dvec)
    y_gdn, hist_gdn, tinv_gdn = gdn_fwd(conv_gdn, proj_main, proj_small, gdn_norm_w, alog, dtb)

    loss_blk, dhid, dy_ssd, dy_gdn, d_w_out, d_fw = out_proj_loss(xs, y_ssd, y_gdn, w_out_full, fw, target)
    dconv_ssd, dproj_main, dsmall_ssd, d_ssd_nw, d_alog_s, d_dtb_s, d_dvec = ssd_bwd(
        conv_ssd, proj_main, proj_small, ssd_norm_w, alog, dtb, dvec, hist_ssd, dy_ssd)
    dproj_main, dconv_gdn, dsmall_gdn, d_gdn_nw, d_alog_g, d_dtb_g = gdn_bwd(
        dproj_main, conv_gdn, proj_main, proj_small, gdn_norm_w, alog, dtb, hist_gdn, tinv_gdn, dy_gdn)
    dproj_main, dwb_ssd = conv_bwd(dproj_main, proj_main, COL_SSD, SSD_CONV, cw_ssd, cb_ssd, dconv_ssd, "conv_bwd_ssd")
    dproj_main, dwb_gdn = conv_bwd(dproj_main, proj_main, COL_GDN, GDN_CONV, cw_gdn, cb_gdn, dconv_gdn, "conv_bwd_gdn")
    slabs_out = d_w_out.reshape(N_DEV, out_rows // 2, D_MODEL).astype(COMM_DTYPE)
    d_w_main, d_w_small, (r_out,) = in_proj_bwd_w(u, dproj_main, dsmall_ssd, dsmall_gdn, [slabs_out])
    d_w_in = jnp.concatenate([d_w_main[:, :COL_GATE], d_w_main[:, COL_SSD:], d_w_small[:, 0:16],
                              d_w_main[:, COL_GATE:COL_SSD], d_w_small[:, 16:32]], axis=1)
    d_w_in = jnp.stack([d_w_in[:, k * in_cols:(k + 1) * in_cols] for k in range(N_CHIP)])
    slabs_in = d_w_in.reshape(N_DEV, D_MODEL // 2, in_cols).astype(COMM_DTYPE)
    grad_x, d_norm_w, (r_in,) = in_proj_bwd_x(xs, norm_w, w_main, w_small, dproj_main, dsmall_ssd, dsmall_gdn,
                                               dhid, [slabs_in])
    d_alog, d_dtb = d_alog_s + d_alog_g, d_dtb_s + d_dtb_g
    packed, (o_nw, o_cs, o_cg, o_snw, o_fw, o_al, o_db, o_dv, o_gnw, o_loss) = _pack_cols([
        d_norm_w, dwb_ssd, dwb_gdn,
        d_ssd_nw.reshape(1, SSD_WIDTH), d_fw, d_alog, d_dtb, d_dvec, d_gdn_nw, loss_blk])

    half_in = sum_slabs(r_in, "sum_w_in")
    half_out = sum_slabs(r_out, "sum_w_out")
    (full_in, full_out), (r_small,) = exchange_halves([half_in, half_out], [packed])
    tot = sum_slabs(r_small, "sum_small")
    grad_w_in = full_in.reshape(D_MODEL, in_cols)
    grad_w_out = full_out.reshape(out_rows, D_MODEL)
    loss = tot[0, o_loss]
    sc, gc = ssd_conv_w.shape[2], gdn_conv_w.shape[2]
    row = lambda off, n, r=0: tot[r:r + 1, off:off + n]
    gs = [row(o_nw, D_MODEL),
          lax.dynamic_slice(tot, (0, o_cs + chip * sc), (4, sc)),
          row(o_cs, SSD_CONV, 4),
          row(o_db, SSD_HEADS), row(o_al, SSD_HEADS), row(o_dv, SSD_HEADS),
          row(o_snw, SSD_WIDTH),
          lax.dynamic_slice(tot, (0, o_cg + chip * gc), (4, gc)),
          row(o_db + LANE_GA, GDN_HEADS), row(o_al + LANE_GA, GDN_HEADS),
          row(o_gnw, GDN_DV), row(o_fw, D_MODEL)]

    names = ["norm_w", "ssd_conv_w", "ssd_conv_b", "ssd_dt_bias", "ssd_a_log", "ssd_d", "ssd_norm_w", "gdn_conv_w",
             "gdn_dt_bias", "gdn_a_log", "gdn_norm_w", "final_norm_w"]
    ws = [norm_w, ssd_conv_w, ssd_conv_b, ssd_dt_bias, ssd_a_log, ssd_d, ssd_norm_w, gdn_conv_w, gdn_dt_bias,
          gdn_a_log, gdn_norm_w, final_norm_w]
    ms = [m_norm_w, m_ssd_conv_w, m_ssd_conv_b, m_ssd_dt_bias, m_ssd_a_log, m_ssd_d, m_ssd_norm_w, m_gdn_conv_w,
          m_gdn_dt_bias, m_gdn_a_log, m_gdn_norm_w, m_final_norm_w]
    vs = [v_norm_w, v_ssd_conv_w, v_ssd_conv_b, v_ssd_dt_bias, v_ssd_a_log, v_ssd_d, v_ssd_norm_w, v_gdn_conv_w,
          v_gdn_dt_bias, v_gdn_a_log, v_gdn_norm_w, v_final_norm_w]
    shapes = [w.shape for w in ws]
    flat = lambda arrs: [a.reshape(g.shape) for a, g in zip(arrs, gs)]
    d_s, m_s, v_s = adamw_many(flat(ws), gs, flat(ms), flat(vs))
    back = lambda arrs: dict(zip(names, [a.reshape(s) for a, s in zip(arrs, shapes)]))
    delta, new_m, new_v, grads = back(d_s), back(m_s), back(v_s), back(gs)
    d_in, m_in, v_in = adamw(w_in_shard, grad_w_in, m_w_in[0], v_w_in[0], "adamw_w_in")
    d_out, m_out, v_out = adamw(w_out_shard, grad_w_out, m_w_out[0], v_w_out[0], "adamw_w_out")
    for tbl, a_in, a_out in ((grads, grad_w_in, grad_w_out), (delta, d_in, d_out), (new_m, m_in, m_out),
                             (new_v, v_in, v_out)):
        tbl["w_in"] = a_in[None]
        tbl["w_out"] = a_out[None]

    order = ["norm_w", "w_in", "ssd_conv_w", "ssd_conv_b", "ssd_dt_bias", "ssd_a_log", "ssd_d", "ssd_norm_w",
             "gdn_conv_w", "gdn_dt_bias", "gdn_a_log", "gdn_norm_w", "w_out", "final_norm_w"]
    return (loss.reshape(()), grad_x[None], *[grads[k] for k in order], *[delta[k] for k in order],
            *[new_m[k] for k in order], *[new_v[k] for k in order])
```

```python
import functools

import jax
import jax.numpy as jnp
from jax import lax
from jax.experimental import pallas as pl
from jax.experimental.pallas import tpu as pltpu

F32 = jnp.float32
MXU_DTYPE = jnp.bfloat16
COMM_DTYPE = jnp.bfloat16
MESH = pl.DeviceIdType.MESH

D_MODEL = 1024
CHUNK = 64
EPS = 1e-6
SSD_HEADS, SSD_GROUPS, SSD_STATE = 16, 2, 128
SSD_WIDTH, SSD_CONV = 1024, 1536
SSD_GW = SSD_WIDTH // SSD_GROUPS
SSD_GC = SSD_GW + 2 * SSD_STATE
GDN_HEADS, GDN_DK, GDN_DV = 8, 128, 128
GDN_W, GDN_CONV = 1024, 3072
GDN_HC = 2 * GDN_DK + GDN_DV
IN_DIM = 6688
MAIN = 6656
LANES = 128
COL_Z, COL_GATE, COL_GDN, COL_SSD = 0, 1024, 2048, 5120
COL_CONV = COL_GDN
GDN_HB = 8
GDN_CB = 2
LANE_GA, LANE_GB = 16, 24
N_DEV, N_CHIP = 8, 4
VMEM_LIMIT = 52 * 1024 * 1024

ADAM_LR, ADAM_B1, ADAM_B2, ADAM_EPS, ADAM_WD, ADAM_STEP = 0.001, 0.9, 0.999, 1e-08, 0.01, 10


def _split(a, n):
    parts, rest = [], a.astype(F32)
    for i in range(n):
        p = rest.astype(MXU_DTYPE)
        parts.append(p)
        if i < n - 1:
            rest = rest - p.astype(F32)
    return parts


def _raw_dot(a, b, ca, cb, mode="bf16"):
    d = lambda u, v: lax.dot_general(u, v, (((ca,), (cb,)), ((), ())), preferred_element_type=F32)
    if mode == "bf16":
        return d(a.astype(MXU_DTYPE), b.astype(MXU_DTYPE))
    if mode == "x3":
        (ah, al), (bh, bl) = _split(a, 2), _split(b, 2)
        return d(ah, bh) + (d(ah, bl) + d(al, bh))
    if mode == "sel_a":
        a0 = a.astype(MXU_DTYPE)
        b1, b2, b3 = _split(b, 3)
        return d(a0, b1) + (d(a0, b2) + d(a0, b3))
    assert mode == "sel_b", mode
    b0 = b.astype(MXU_DTYPE)
    a1, a2, a3 = _split(a, 3)
    return d(a1, b0) + (d(a2, b0) + d(a3, b0))


@functools.partial(jax.custom_vjp, nondiff_argnums=(2,))
def mm_nn(a, b, mode="bf16"):
    return _raw_dot(a, b, 1, 0, mode)


@functools.partial(jax.custom_vjp, nondiff_argnums=(2,))
def mm_nt(a, b, mode="bf16"):
    return _raw_dot(a, b, 1, 1, mode)


@functools.partial(jax.custom_vjp, nondiff_argnums=(2,))
def mm_tn(a, b, mode="bf16"):
    return _raw_dot(a, b, 0, 0, mode)


_SAME = {"bf16": ("bf16", "bf16"), "x3": ("x3", "x3")}
_NN_BWD = dict(_SAME, sel_a=("bf16", "sel_a"), sel_b=("sel_b", "bf16"))
_NT_BWD = dict(_SAME, sel_a=("bf16", "sel_b"), sel_b=("sel_b", "bf16"))
_TN_BWD = dict(_SAME, sel_a=("bf16", "sel_a"), sel_b=("sel_a", "bf16"))
mm_nn.defvjp(lambda a, b, m: (_raw_dot(a, b, 1, 0, m), (a, b)),
             lambda m, r, g: (mm_nt(g, r[1], _NN_BWD[m][0]), mm_tn(r[0], g, _NN_BWD[m][1])))
mm_nt.defvjp(lambda a, b, m: (_raw_dot(a, b, 1, 1, m), (a, b)),
             lambda m, r, g: (mm_nn(g, r[1], _NT_BWD[m][0]), mm_tn(g, r[0], _NT_BWD[m][1])))
mm_tn.defvjp(lambda a, b, m: (_raw_dot(a, b, 0, 0, m), (a, b)),
             lambda m, r, g: (mm_nt(r[1], g, _TN_BWD[m][0]), mm_nn(r[0], g, _TN_BWD[m][1])))


@jax.custom_jvp
def sigmoid(x):
    return 1.0 / (1.0 + jnp.exp(-x))


@sigmoid.defjvp
def _sigmoid_jvp(p, t):
    s = sigmoid(p[0])
    return s, t[0] * s * (1.0 - s)


@jax.custom_jvp
def softplus(x):
    return jnp.maximum(x, 0.0) + jnp.log(1.0 + jnp.exp(-jnp.abs(x)))


@softplus.defjvp
def _softplus_jvp(p, t):
    return softplus(p[0]), t[0] * sigmoid(p[0])


def silu(x):
    return x * sigmoid(x)


def rmsnorm(x, w):
    return x * lax.rsqrt(jnp.mean(x * x, axis=-1, keepdims=True) + EPS) * w


def _iota(shape, dim):
    return lax.broadcasted_iota(jnp.int32, shape, dim)


def _halves():
    lane = _iota((1, LANES), 1) >> 6
    return _ind(lane == 0), _ind(lane == 1)


def _block_diag(pair):
    h0, h1 = _halves()
    return jnp.concatenate([pair * h0, pair * h1], axis=0)


def _tri_inv_impl(mats):
    r, c = _iota((CHUNK, LANES), 0), _iota((CHUNK, LANES), 1) & (CHUNK - 1)
    eye = _ind(r == c)
    blockdiag = _ind((r >> 4) == (c >> 4))
    dot = lambda u, v: _raw_dot(u, _block_diag(v), 1, 0, "x3")
    dot1 = lambda u, v: _raw_dot(u, _block_diag(v), 1, 0)
    each = lambda f, *ls: [f(*xs) for xs in zip(*ls)]
    dg = each(lambda a: a * blockdiag, mats)
    off = each(lambda a, d: a - d, mats, dg)
    m = each(lambda d: -d, dg)
    p = each(lambda x: eye + x, m)
    pw = m
    for _ in range(3):
        pw = each(lambda x: dot1(x, x), pw)
        p = each(lambda x, y: x + dot1(x, y), p, pw)
    e = each(dot, p, off)
    e2 = each(lambda x: dot1(x, x), e)
    q = each(lambda x: eye - x, e)
    q = each(lambda x, y: x + dot1(x, y), q, e2)
    return each(dot, q, p)


def _tri_inv_bwd(ts, gs):
    h0, h1 = _halves()
    x = [mm_nt(g, _block_diag(t)) for g, t in zip(gs, ts)]
    full = [mm_tn(t, y) for t, y in zip(ts, x)]
    return [-(f[:CHUNK] * h0 + f[CHUNK:] * h1) for f in full]


@jax.custom_vjp
def tri_inv(mats):
    return _tri_inv_impl(mats)


def _tri_inv_fwd(mats):
    ts = _tri_inv_impl(mats)
    return ts, ts


tri_inv.defvjp(_tri_inv_fwd, lambda ts, gs: (_tri_inv_bwd(ts, gs),))


@jax.custom_vjp
def tri_inv_saved(mats, ts):
    del mats
    return ts


tri_inv_saved.defvjp(lambda mats, ts: (ts, ts),
                     lambda ts, gs: (_tri_inv_bwd(ts, gs), [jnp.zeros_like(t) for t in ts]))


def _ind(cond):
    return jnp.where(cond, 1.0, 0.0).astype(F32)


def _chunk_masks():
    r, c = _iota((CHUNK, CHUNK), 0), _iota((CHUNK, CHUNK), 1)
    return _ind(r >= c), _ind(r > c), _ind(r == c), _ind(_iota((CHUNK, 1), 0) == CHUNK - 1)


def _log_decay_cumsum(small, alog, dtb, tri):
    sp = softplus(small + dtb)
    la = -jnp.exp(alog) * sp
    return sp, mm_nn(tri, la, "sel_a")


def _col_of(x, lane):
    return jnp.sum(x * _ind(_iota((1, LANES), 1) == lane), axis=1, keepdims=True)


def _decay_matrix(col, tri, eye):
    row = jnp.sum(col * eye, axis=0, keepdims=True)
    return jnp.exp((col - row) * tri) * tri


def _pair_masks():
    r, c = _iota((CHUNK, LANES), 0), _iota((CHUNK, LANES), 1)
    c6 = c & (CHUNK - 1)
    return _ind(r >= c6), _ind(r > c6), (_ind(c == r), _ind(c == r + CHUNK))


def _decay_pair(col_a, col_b, tri_w, eye_w):
    h0, h1 = _halves()
    col = col_a * h0 + col_b * h1
    row = jnp.sum(col_a * eye_w[0] + col_b * eye_w[1], axis=0, keepdims=True)
    return jnp.exp((col - row) * tri_w) * tri_w


def gdn_chunk(h0, qs, ks, vs, smalls, gates, normw, alog, dtb, states, saved_t=None):
    tri, _, _, last = _chunk_masks()
    tri_w, strict_w, eye_w = _pair_masks()
    nh = len(qs[0])
    flat = lambda xss: [x for xs in xss for x in xs]
    lacs = [_log_decay_cumsum(sm, alog, dtb, tri)[1] for sm in smalls]
    qs, ks, vs, gates = flat(qs), flat(ks), flat(vs), flat(gates)
    heads, pairs = range(len(qs)), range(len(qs) // 2)
    each = lambda f, *ls: [f(*xs) for xs in zip(*ls)]
    ab = lambda xs, p: (xs[2 * p], xs[2 * p + 1])
    stack = lambda xs: jnp.concatenate(xs, axis=0)
    gc = [_col_of(lacs[i // nh], LANE_GA + h0 + i % nh) for i in heads]
    beta = [sigmoid(_col_of(smalls[i // nh], LANE_GB + h0 + i % nh)) for i in heads]
    decay = [_decay_pair(*ab(gc, p), tri_w, eye_w) for p in pairs]
    gl = each(lambda x: jnp.sum(x * last, axis=0, keepdims=True), gc)
    q = each(lambda x: x * lax.rsqrt(jnp.sum(x * x, axis=-1, keepdims=True) + EPS) * (GDN_DK ** -0.5), qs)
    k = each(lambda x: x * lax.rsqrt(jnp.sum(x * x, axis=-1, keepdims=True) + EPS), ks)
    kb = each(lambda x, b: x * b, k, beta)
    eg = each(jnp.exp, gc)
    zero = jnp.zeros((CHUNK, GDN_DK), F32)
    k_bd = [stack([join_lanes([k[2 * p], zero]), join_lanes([zero, k[2 * p + 1]])]) for p in pairs]
    a = [mm_nt(join_lanes(list(ab(kb, p))), k_bd[p]) * (decay[p] * strict_w) for p in pairs]
    t = tri_inv(a) if saved_t is None else tri_inv_saved(a, saved_t)
    attn = [mm_nt(join_lanes(list(ab(q, p))), k_bd[p]) * decay[p] for p in pairs]
    rhs = [stack([join_lanes([vs[h] * beta[h], kb[h] * eg[h]]) for h in (2 * p, 2 * p + 1)]) for p in pairs]
    uw = [mm_nn(_block_diag(t[p]), rhs[p]) for p in pairs]
    uw = [x for p in pairs for x in split_rows(uw[p])]
    u, w = zip(*[split_lanes(x) for x in uw])
    ys = []
    for c in range(len(smalls)):
        hs = range(c * nh, (c + 1) * nh)
        v_new = [u[i] - mm_nn(w[i], states[i % nh]) for i in hs]
        av = [mm_nn(_block_diag(attn[c * nh // 2 + p]), stack(list(ab(v_new, p)))) for p in range(nh // 2)]
        av = [x for y in av for x in split_rows(y)]
        o = [mm_nn(q[i] * eg[i], states[i % nh]) + av[i % nh] for i in hs]
        states = [states[i % nh] * jnp.exp(gl[i]) + mm_tn(k[i] * jnp.exp(gl[i] - gc[i]), v_new[i % nh]) for i in hs]
        ys.append([rmsnorm(o[i % nh], normw) * silu(gates[i]) for i in hs])
    return ys, states, t


@jax.custom_vjp
def split_rows(x):
    n = x.shape[0] // 2
    return [x[:n], x[n:]]


split_rows.defvjp(lambda x: (split_rows(x), None), lambda _, gs: (jnp.concatenate(gs, axis=0),))


@jax.custom_vjp
def split_lanes(x):
    return [x[:, i * LANES:(i + 1) * LANES] for i in range(x.shape[1] // LANES)]


@jax.custom_vjp
def join_lanes(xs):
    return jnp.concatenate(xs, axis=1)


split_lanes.defvjp(lambda x: (split_lanes(x), None), lambda _, gs: (join_lanes(gs),))
join_lanes.defvjp(lambda xs: (join_lanes(xs), None), lambda _, g: (split_lanes(g),))


def ssd_chunk(xs, bm, cm, z, small, normw, alog, dtb, dvec, state):
    tri, _, eye, last = _chunk_masks()
    hpg = SSD_HEADS // SSD_GROUPS
    groups = range(len(xs))
    each = lambda f, *ls: [f(*a) for a in zip(*ls)]
    sp, lac = _log_decay_cumsum(small, alog, dtb, tri)
    lac_last = jnp.sum(lac * last, axis=0, keepdims=True)
    sel = [_ind(_iota((LANES, SSD_GW), 0) == g * hpg + (_iota((LANES, SSD_GW), 1) >> 6)) for g in groups]
    expand = lambda v, mode="sel_b": [mm_nn(v, s, mode) for s in sel]
    dt_e, elac_e, toend_e = expand(sp, "bf16"), expand(jnp.exp(lac), "bf16"), expand(jnp.exp(lac_last - lac), "bf16")
    row8, row8e = _iota((8, 1), 0), _iota((8, 1), 0)
    two_e = expand(_ind(row8 == 0) * dvec + _ind(row8 == 1) * jnp.exp(lac_last))
    d_e = each(lambda v: jnp.sum(v * _ind(row8e == 0), axis=0, keepdims=True), two_e)
    chunk_e = each(lambda v: jnp.sum(v * _ind(row8e == 1), axis=0, keepdims=True), two_e)
    xdt = each(lambda a, b: a * b, xs, dt_e)
    y = each(lambda c_, st, el, x_, d_: mm_nn(c_, st) * el + x_ * d_, cm, state, elac_e, xs, d_e)
    r, c = _iota((CHUNK, LANES), 0), _iota((CHUNK, LANES), 1)
    tri_w = _ind(r >= (c & (CHUNK - 1)))
    eye_w = [_ind(c == r), _ind(c == r + CHUNK)]
    half = [_ind((_iota((1, LANES), 1) >> 6) == s) for s in range(2)]

    def decay_pair(col_a, col_b):
        col = col_a * half[0] + col_b * half[1]
        row = jnp.sum(col_a * eye_w[0] + col_b * eye_w[1], axis=0, keepdims=True)
        return jnp.exp((col - row) * tri_w) * tri_w

    pairs = range(hpg // 2)
    cb_w = each(lambda c_, b_: mm_nt(c_, jnp.concatenate([b_, b_], axis=0)), cm, bm)
    x_pairs = each(split_lanes, xdt)
    lms = [[decay_pair(_col_of(lac, g * hpg + 2 * p), _col_of(lac, g * hpg + 2 * p + 1)) for p in pairs]
           for g in groups]
    stacked = [[jnp.concatenate([x_pairs[g][p] * half[0], x_pairs[g][p] * half[1]], axis=0) for p in pairs]
               for g in groups]
    terms = [[mm_nn(cb_w[g] * lms[g][p], stacked[g][p]) for p in pairs] for g in groups]
    y = [y[g] + join_lanes(terms[g]) for g in groups]
    new_state = each(lambda st, ce, b_, xd, te: st * ce + mm_tn(b_, xd * te), state, chunk_e, bm, xdt, toend_e)
    out = each(lambda y_, z_, nw: rmsnorm(y_ * silu(z_), nw), y, z, normw)
    return out, new_state


def _params(sem=None):
    return pltpu.CompilerParams(dimension_semantics=sem, vmem_limit_bytes=VMEM_LIMIT)


def _full(shape):
    n = len(shape)
    return pl.BlockSpec(shape, lambda *_: (0,) * n)


ANY = pl.BlockSpec(memory_space=pl.ANY)
HBM = pl.BlockSpec(memory_space=pltpu.HBM)


def in_proj(x, normw, w_main, w_small):
    t = x.shape[0]
    tm, tn = min(1024, t), 512

    def body(x_ref, nw_ref, wm_ref, ws_ref, pm_ref, ps_ref, u_ref):
        @pl.when(pl.program_id(1) == 0)
        def _():
            u = rmsnorm(x_ref[...], nw_ref[...]).astype(MXU_DTYPE)
            u_ref[...] = u
            ps_ref[...] = _raw_dot(u, ws_ref[...], 1, 0)
        pm_ref[...] = _raw_dot(u_ref[...], wm_ref[...], 1, 0)

    return pl.pallas_call(
        body, name="in_proj", grid=(t // tm, COL_CONV // tn),
        in_specs=[pl.BlockSpec((tm, D_MODEL), lambda i, j: (i, 0)), _full((1, D_MODEL)),
                  pl.BlockSpec((D_MODEL, tn), lambda i, j: (0, j)), _full((D_MODEL, LANES))],
        out_specs=[pl.BlockSpec((tm, tn), lambda i, j: (i, j)), pl.BlockSpec((tm, LANES), lambda i, j: (i, 0)),
                   pl.BlockSpec((tm, D_MODEL), lambda i, j: (i, 0))],
        out_shape=[jax.ShapeDtypeStruct((t, MAIN), F32), jax.ShapeDtypeStruct((t, LANES), F32),
                   jax.ShapeDtypeStruct((t, D_MODEL), MXU_DTYPE)],
        compiler_params=_params(("arbitrary", "arbitrary")),
    )(x, normw, w_main, w_small)


CONV_TC = 512
HALO = 8


def _shift_down(cur, prev, s):
    rolled = pltpu.roll(cur, s, 0)
    top = jnp.where(_iota((HALO, cur.shape[1]), 0) < s, pltpu.roll(prev, s, 0), rolled[:HALO])
    if cur.shape[0] == HALO:
        return top
    return jnp.concatenate([top, rolled[HALO:]], axis=0)


def _shift_up(cur, nxt, s):
    n = cur.shape[0]
    rolled = pltpu.roll(cur, n - s, 0)
    bot = jnp.where(_iota((HALO, cur.shape[1]), 0) >= HALO - s, pltpu.roll(nxt, HALO - s, 0), rolled[n - HALO:])
    return jnp.concatenate([rolled[:n - HALO], bot], axis=0)


def _conv_pre(cur, prev, w_ref, b):
    acc = cur * w_ref[3:4, :] + b
    shifted = [cur]
    for s in (1, 2, 3):
        sh = _shift_down(cur, prev, s)
        shifted.append(sh)
        acc = acc + sh * w_ref[3 - s:4 - s, :]
    return acc, shifted


def in_proj_conv(proj_main, u, w_main, w, b):
    t = u.shape[0]
    tm, tn = min(1024, t), CONV_TC
    rc = min(256, tm)
    c0, nj = COL_CONV // tn, (MAIN - COL_CONV) // tn

    def body(alias_ref, u_ref, wm_ref, w_ref, b_ref, pm_ref, out_ref, halo_ref):
        del alias_ref
        j = pl.program_id(1)

        @pl.when(pl.program_id(0) == 0)
        def _():
            halo_ref[j] = jnp.zeros((HALO, tn), F32)

        prev = halo_ref[j]
        for r in range(tm // rc):
            rows = pl.ds(r * rc, rc)
            p = _raw_dot(u_ref[rows, :], wm_ref[...], 1, 0)
            pm_ref[rows, :] = p
            pre, _ = _conv_pre(p, prev, w_ref, b_ref[...])
            out_ref[rows, :] = silu(pre)
            prev = p[rc - HALO:]
        halo_ref[j] = prev

    return pl.pallas_call(
        body, name="in_proj_conv", grid=(t // tm, nj),
        in_specs=[ANY, pl.BlockSpec((tm, D_MODEL), lambda i, j: (i, 0)),
                  pl.BlockSpec((D_MODEL, tn), lambda i, j: (0, c0 + j)),
                  pl.BlockSpec((4, tn), lambda i, j: (0, j)), pl.BlockSpec((1, tn), lambda i, j: (0, j))],
        out_specs=[pl.BlockSpec((tm, tn), lambda i, j: (i, c0 + j)), pl.BlockSpec((tm, tn), lambda i, j: (i, j))],
        out_shape=[jax.ShapeDtypeStruct(proj_main.shape, F32), jax.ShapeDtypeStruct((t, MAIN - COL_CONV), F32)],
        scratch_shapes=[pltpu.VMEM((nj, HALO, tn), F32)],
        input_output_aliases={0: 0},
        compiler_params=_params(("arbitrary", "arbitrary")),
    )(proj_main, u, w_main, w, b)


def _dsilu(pre):
    sg = sigmoid(pre)
    return sg * (1.0 + pre * (1.0 - sg))


def conv_bwd(dproj_main, proj_main, col0, width, w, b, dout, name):
    t = proj_main.shape[0]
    tt, c0 = min(512, t), col0 // CONV_TC
    nt = t // tt
    after = lambda i: jnp.minimum((i + 1) * (tt // HALO), t // HALO - 1)

    def body(alias_ref, cur_ref, prev_ref, nxt_ref, w_ref, b_ref, do_ref, do_nxt_ref, dx_ref, dwb_ref):
        del alias_ref
        i = pl.program_id(1)
        cur, bias = cur_ref[...], b_ref[...]
        prev = jnp.where(i > 0, prev_ref[...], 0.0)
        pre, shifted = _conv_pre(cur, prev, w_ref, bias)
        dpre = do_ref[...] * _dsilu(pre)
        pre_nxt, _ = _conv_pre(nxt_ref[...], cur[tt - HALO:], w_ref, bias)
        dpre_nxt = jnp.where(i < nt - 1, do_nxt_ref[...] * _dsilu(pre_nxt), 0.0)
        dx = dpre * w_ref[3:4, :]
        for s in (1, 2, 3):
            dx = dx + _shift_up(dpre, dpre_nxt, s) * w_ref[3 - s:4 - s, :]
        dx_ref[...] = dx.astype(dx_ref.dtype)
        row = _iota((HALO, CONV_TC), 0)
        upd = jnp.where(row == 4, jnp.sum(dpre, axis=0, keepdims=True), 0.0)
        for s in range(4):
            upd = upd + jnp.where(row == 3 - s, jnp.sum(dpre * shifted[s], axis=0, keepdims=True), 0.0)
        _accumulate(dwb_ref, i == 0, upd)

    return pl.pallas_call(
        body, name=name, grid=(width // CONV_TC, nt),
        in_specs=[ANY, pl.BlockSpec((tt, CONV_TC), lambda j, i: (i, c0 + j)),
                  pl.BlockSpec((HALO, CONV_TC), lambda j, i: (jnp.maximum(i * (tt // HALO) - 1, 0), c0 + j)),
                  pl.BlockSpec((HALO, CONV_TC), lambda j, i: (after(i), c0 + j)),
                  pl.BlockSpec((4, CONV_TC), lambda j, i: (0, j)), pl.BlockSpec((1, CONV_TC), lambda j, i: (0, j)),
                  pl.BlockSpec((tt, CONV_TC), lambda j, i: (i, j)),
                  pl.BlockSpec((HALO, CONV_TC), lambda j, i: (after(i), j))],
        out_specs=[pl.BlockSpec((tt, CONV_TC), lambda j, i: (i, c0 + j)),
                   pl.BlockSpec((HALO, CONV_TC), lambda j, i: (0, j))],
        out_shape=[jax.ShapeDtypeStruct(dproj_main.shape, dproj_main.dtype), jax.ShapeDtypeStruct((HALO, width), F32)],
        input_output_aliases={0: 0},
        compiler_params=_params(("arbitrary", "arbitrary")),
    )(dproj_main, proj_main, proj_main, proj_main, w, b, dout, dout)


def _ssd_cols(g):
    b0 = SSD_WIDTH + g * SSD_STATE
    c0 = SSD_WIDTH + SSD_GROUPS * SSD_STATE + g * SSD_STATE
    return slice(g * SSD_GW, (g + 1) * SSD_GW), slice(b0, b0 + SSD_STATE), slice(c0, c0 + SSD_STATE)


def _gdn_cols(j):
    return tuple(slice(s * GDN_W + j * GDN_DK, s * GDN_W + (j + 1) * GDN_DK) for s in range(3))


def _ssd_parts(xbc_ref):
    return tuple([xbc_ref[:, _ssd_cols(g)[s]] for g in range(SSD_GROUPS)] for s in range(3))


def _group_cols(ref):
    return [ref[:, g * SSD_GW:(g + 1) * SSD_GW] for g in range(SSD_GROUPS)]


def _chunk_rows(c):
    return slice(c * CHUNK, (c + 1) * CHUNK)


def _gdn_parts(qkv_ref):
    assert GDN_HB == GDN_HEADS, "the conv block is read whole: one grid step holds every head"
    return tuple([[qkv_ref[_chunk_rows(c), _gdn_cols(j)[s]] for j in range(GDN_HB)] for c in range(GDN_CB)]
                 for s in range(3))


def _head_cols(ref):
    return [[ref[_chunk_rows(c), j * GDN_DV:(j + 1) * GDN_DV] for j in range(GDN_HB)] for c in range(GDN_CB)]


def _chunk_blocks(ref):
    return [ref[_chunk_rows(c), :] for c in range(GDN_CB)]


def _first_head():
    return 0 if GDN_HB == GDN_HEADS else pl.program_id(1) * GDN_HB


def ssd_fwd(conv_ssd, proj_main, proj_small, normw, alog, dtb, dvec):
    t = conv_ssd.shape[0]
    nc = t // CHUNK

    groups = range(SSD_GROUPS)

    def body(xbc_ref, z_ref, sm_ref, nw_ref, al_ref, db_ref, dv_ref, y_ref, hist_ref, state_ref):
        @pl.when(pl.program_id(0) == 0)
        def _():
            state_ref[...] = jnp.zeros(state_ref.shape, F32)

        states = [state_ref[g] for g in groups]
        for g in groups:
            hist_ref[0, g] = states[g]
        ys, new_states = ssd_chunk(*_ssd_parts(xbc_ref), _group_cols(z_ref), sm_ref[...], _group_cols(nw_ref),
                                   al_ref[...], db_ref[...], dv_ref[...], states)
        for g in groups:
            y_ref[:, g * SSD_GW:(g + 1) * SSD_GW] = ys[g].astype(MXU_DTYPE)
            state_ref[g] = new_states[g]

    return pl.pallas_call(
        body, name="ssd_fwd", grid=(nc,),
        in_specs=[pl.BlockSpec((CHUNK, SSD_CONV), lambda c: (c, (COL_SSD - COL_CONV) // SSD_CONV)),
                  pl.BlockSpec((CHUNK, SSD_WIDTH), lambda c: (c, COL_Z // SSD_WIDTH)),
                  pl.BlockSpec((CHUNK, LANES), lambda c: (c, 0)),
                  _full((1, SSD_WIDTH)), _full((1, LANES)), _full((1, LANES)), _full((1, LANES))],
        out_specs=[pl.BlockSpec((CHUNK, SSD_WIDTH), lambda c: (c, 0)),
                   pl.BlockSpec((1, SSD_GROUPS, SSD_STATE, SSD_GW), lambda c: (c, 0, 0, 0))],
        out_shape=[jax.ShapeDtypeStruct((t, SSD_WIDTH), MXU_DTYPE),
                   jax.ShapeDtypeStruct((nc, SSD_GROUPS, SSD_STATE, SSD_GW), F32)],
        scratch_shapes=[pltpu.VMEM((SSD_GROUPS, SSD_STATE, SSD_GW), F32)],
        compiler_params=_params(("arbitrary",)),
    )(conv_ssd, proj_main, proj_small, normw, alog, dtb, dvec)


def _accumulate(ref, first, value):
    @pl.when(first)
    def _():
        ref[...] = value

    @pl.when(jnp.logical_not(first))
    def _():
        ref[...] += value


def ssd_bwd(conv_ssd, proj_main, proj_small, normw, alog, dtb, dvec, hist, dy):
    t = conv_ssd.shape[0]
    nc = t // CHUNK
    rev = lambda c: nc - 1 - c
    groups = range(SSD_GROUPS)

    def body(xbc_ref, z_ref, sm_ref, nw_ref, al_ref, db_ref, dv_ref, hist_ref, dy_ref,
             dxbc_ref, dz_ref, dsm_ref, dnw_ref, dal_ref, ddb_ref, ddv_ref, dstate_ref):
        first = pl.program_id(0) == 0

        @pl.when(first)
        def _():
            dstate_ref[...] = jnp.zeros(dstate_ref.shape, F32)

        _, vjp = jax.vjp(ssd_chunk, *_ssd_parts(xbc_ref), _group_cols(z_ref), sm_ref[...], _group_cols(nw_ref),
                         al_ref[...], db_ref[...], dv_ref[...], [hist_ref[0, g] for g in groups])
        dxs, dbm, dcm, dz, dsm, dnw, dal, ddb, ddv, dstate = vjp(
            (_group_cols(dy_ref), [dstate_ref[g] for g in groups]))
        for g in groups:
            xc, bc, cc = _ssd_cols(g)
            dxbc_ref[:, xc] = dxs[g]
            dxbc_ref[:, bc] = dbm[g]
            dxbc_ref[:, cc] = dcm[g]
            dz_ref[:, g * SSD_GW:(g + 1) * SSD_GW] = dz[g].astype(dz_ref.dtype)
            dstate_ref[g] = dstate[g]
        dsm_ref[...] = dsm
        _accumulate(dnw_ref, first, join_lanes(dnw))
        _accumulate(dal_ref, first, dal)
        _accumulate(ddb_ref, first, ddb)
        _accumulate(ddv_ref, first, ddv)

    return pl.pallas_call(
        body, name="ssd_bwd", grid=(nc,),
        in_specs=[pl.BlockSpec((CHUNK, SSD_CONV), lambda c: (rev(c), (COL_SSD - COL_CONV) // SSD_CONV)),
                  pl.BlockSpec((CHUNK, SSD_WIDTH), lambda c: (rev(c), COL_Z // SSD_WIDTH)),
                  pl.BlockSpec((CHUNK, LANES), lambda c: (rev(c), 0)),
                  _full((1, SSD_WIDTH)), _full((1, LANES)), _full((1, LANES)), _full((1, LANES)),
                  pl.BlockSpec((1, SSD_GROUPS, SSD_STATE, SSD_GW), lambda c: (rev(c), 0, 0, 0)),
                  pl.BlockSpec((CHUNK, SSD_WIDTH), lambda c: (rev(c), 0))],
        out_specs=[pl.BlockSpec((CHUNK, SSD_CONV), lambda c: (rev(c), 0)),
                   pl.BlockSpec((CHUNK, SSD_WIDTH), lambda c: (rev(c), COL_Z // SSD_WIDTH)),
                   pl.BlockSpec((CHUNK, LANES), lambda c: (rev(c), 0)),
                   _full((1, SSD_WIDTH)), _full((1, LANES)), _full((1, LANES)), _full((1, LANES))],
        out_shape=[jax.ShapeDtypeStruct((t, SSD_CONV), F32), jax.ShapeDtypeStruct((t, MAIN), MXU_DTYPE),
                   jax.ShapeDtypeStruct((t, LANES), F32), jax.ShapeDtypeStruct((1, SSD_WIDTH), F32),
                   jax.ShapeDtypeStruct((1, LANES), F32), jax.ShapeDtypeStruct((1, LANES), F32),
                   jax.ShapeDtypeStruct((1, LANES), F32)],
        scratch_shapes=[pltpu.VMEM((SSD_GROUPS, SSD_STATE, SSD_GW), F32)],
        compiler_params=_params(("arbitrary",)),
    )(conv_ssd, proj_main, proj_small, normw, alog, dtb, dvec, hist, dy)


def gdn_fwd(conv_gdn, proj_main, proj_small, normw, alog, dtb):
    t = conv_gdn.shape[0]
    hb, cb = GDN_HB, GDN_CB
    rows = CHUNK * cb
    ns = t // rows
    gate_blk = COL_GATE // (GDN_DV * hb)

    def body(qkv_ref, gate_ref, sm_ref, nw_ref, al_ref, db_ref, y_ref, hist_ref, t_ref, state_ref):
        h0 = _first_head()

        @pl.when(pl.program_id(0) == 0)
        def _():
            for j in range(hb):
                state_ref[h0 + j] = jnp.zeros((GDN_DK, GDN_DV), F32)

        states = [state_ref[h0 + j] for j in range(hb)]
        for j in range(hb):
            hist_ref[0, j] = states[j]
        qs, ks, vs = _gdn_parts(qkv_ref)
        ys, new_states, ts = gdn_chunk(h0, qs, ks, vs, _chunk_blocks(sm_ref), _head_cols(gate_ref), nw_ref[...],
                                       al_ref[...], db_ref[...], states)
        for c in range(cb):
            for j in range(hb):
                y_ref[_chunk_rows(c), j * GDN_DV:(j + 1) * GDN_DV] = ys[c][j].astype(MXU_DTYPE)
        for j in range(hb):
            state_ref[h0 + j] = new_states[j]
        for p in range(cb * hb // 2):
            t_ref[0, p] = ts[p]

    return pl.pallas_call(
        body, name="gdn_fwd", grid=(ns, GDN_HEADS // hb),
        in_specs=[pl.BlockSpec((rows, GDN_HC * hb), lambda c, h: (c, h)),
                  pl.BlockSpec((rows, GDN_DV * hb), lambda c, h: (c, gate_blk + h)),
                  pl.BlockSpec((rows, LANES), lambda c, h: (c, 0)),
                  _full((1, GDN_DV)), _full((1, LANES)), _full((1, LANES))],
        out_specs=[pl.BlockSpec((rows, GDN_DV * hb), lambda c, h: (c, h)),
                   pl.BlockSpec((1, hb, GDN_DK, GDN_DV), lambda c, h: (c, h, 0, 0)),
                   pl.BlockSpec((1, cb * hb // 2, CHUNK, LANES), lambda c, h: (c, h, 0, 0))],
        out_shape=[jax.ShapeDtypeStruct((t, GDN_W), MXU_DTYPE),
                   jax.ShapeDtypeStruct((ns, GDN_HEADS, GDN_DK, GDN_DV), F32),
                   jax.ShapeDtypeStruct((ns, cb * GDN_HEADS // 2, CHUNK, LANES), F32)],
        scratch_shapes=[pltpu.VMEM((GDN_HEADS, GDN_DK, GDN_DV), F32)],
        compiler_params=_params(("arbitrary", "arbitrary")),
    )(conv_gdn, proj_main, proj_small, normw, alog, dtb)


def gdn_bwd(dproj_main, conv_gdn, proj_main, proj_small, normw, alog, dtb, hist, t_inv, dy):
    t = conv_gdn.shape[0]
    hb, cb = GDN_HB, GDN_CB
    rows = CHUNK * cb
    ns = t // rows
    rev = lambda c: ns - 1 - c
    gate_blk = COL_GATE // (GDN_DV * hb)

    def body(alias_ref, qkv_ref, gate_ref, sm_ref, nw_ref, al_ref, db_ref, hist_ref, t_ref, dy_ref,
             dgate_ref, dqkv_ref, dsm_ref, dnw_ref, dal_ref, ddb_ref, dstate_ref):
        del alias_ref
        c, h = pl.program_id(0), pl.program_id(1)
        h0 = _first_head()

        @pl.when(c == 0)
        def _():
            for j in range(hb):
                dstate_ref[h0 + j] = jnp.zeros((GDN_DK, GDN_DV), F32)

        saved = [t_ref[0, p] for p in range(cb * hb // 2)]

        def fn(qs, ks, vs, smalls, gates, nw, al, db, states):
            return gdn_chunk(h0, qs, ks, vs, smalls, gates, nw, al, db, states, saved)[:2]

        qs, ks, vs = _gdn_parts(qkv_ref)
        _, vjp = jax.vjp(fn, qs, ks, vs, _chunk_blocks(sm_ref), _head_cols(gate_ref), nw_ref[...], al_ref[...],
                         db_ref[...], [hist_ref[0, j] for j in range(hb)])
        dqs, dks, dvs, dsm, dgates, dnw, dal, ddb, dstates = vjp(
            (_head_cols(dy_ref), [dstate_ref[h0 + j] for j in range(hb)]))
        for k in range(cb):
            rk = _chunk_rows(k)
            for j in range(hb):
                qc, kc, vc = _gdn_cols(j)
                dqkv_ref[rk, qc] = dqs[k][j]
                dqkv_ref[rk, kc] = dks[k][j]
                dqkv_ref[rk, vc] = dvs[k][j]
                dgate_ref[rk, j * GDN_DV:(j + 1) * GDN_DV] = dgates[k][j].astype(dgate_ref.dtype)
        for j in range(hb):
            dstate_ref[h0 + j] = dstates[j]
        _accumulate(dsm_ref, h == 0, jnp.concatenate(dsm, axis=0))
        first = jnp.logical_and(c == 0, h == 0)
        _accumulate(dnw_ref, first, dnw)
        _accumulate(dal_ref, first, dal)
        _accumulate(ddb_ref, first, ddb)

    return pl.pallas_call(
        body, name="gdn_bwd", grid=(ns, GDN_HEADS // hb),
        in_specs=[ANY, pl.BlockSpec((rows, GDN_HC * hb), lambda c, h: (rev(c), h)),
                  pl.BlockSpec((rows, GDN_DV * hb), lambda c, h: (rev(c), gate_blk + h)),
                  pl.BlockSpec((rows, LANES), lambda c, h: (rev(c), 0)),
                  _full((1, GDN_DV)), _full((1, LANES)), _full((1, LANES)),
                  pl.BlockSpec((1, hb, GDN_DK, GDN_DV), lambda c, h: (rev(c), h, 0, 0)),
                  pl.BlockSpec((1, cb * hb // 2, CHUNK, LANES), lambda c, h: (rev(c), h, 0, 0)),
                  pl.BlockSpec((rows, GDN_DV * hb), lambda c, h: (rev(c), h))],
        out_specs=[pl.BlockSpec((rows, GDN_DV * hb), lambda c, h: (rev(c), gate_blk + h)),
                   pl.BlockSpec((rows, GDN_HC * hb), lambda c, h: (rev(c), h)),
                   pl.BlockSpec((rows, LANES), lambda c, h: (rev(c), 0)),
                   _full((1, GDN_DV)), _full((1, LANES)), _full((1, LANES))],
        out_shape=[jax.ShapeDtypeStruct(dproj_main.shape, dproj_main.dtype), jax.ShapeDtypeStruct((t, GDN_CONV), F32),
                   jax.ShapeDtypeStruct((t, LANES), F32), jax.ShapeDtypeStruct((1, GDN_DV), F32),
                   jax.ShapeDtypeStruct((1, LANES), F32), jax.ShapeDtypeStruct((1, LANES), F32)],
        scratch_shapes=[pltpu.VMEM((GDN_HEADS, GDN_DK, GDN_DV), F32)],
        input_output_aliases={0: 0},
        compiler_params=_params(("arbitrary", "arbitrary")),
    )(dproj_main, conv_gdn, proj_main, proj_small, normw, alog, dtb, hist, t_inv, dy)


def out_proj_loss(x, y_ssd, y_gdn, w_out, final_w, target):
    t = x.shape[0]
    tm = min(256, t)

    def body(x_ref, ys_ref, yg_ref, wo_ref, fw_ref, tg_ref, loss_ref, dhid_ref, dys_ref, dyg_ref, dwo_ref, dfw_ref):
        i = pl.program_id(0)
        ys, yg = ys_ref[...], yg_ref[...]
        wo_s, wo_g = wo_ref[:SSD_WIDTH, :], wo_ref[SSD_WIDTH:, :]
        hid = x_ref[...] + _raw_dot(ys, wo_s, 1, 0) + _raw_dot(yg, wo_g, 1, 0)
        out, vjp = jax.vjp(rmsnorm, hid, fw_ref[...])
        err = out - tg_ref[...]
        loss = 0.5 * jnp.sum(jnp.mean(err * err, axis=-1, keepdims=True), axis=0, keepdims=True)
        dhid, dfw = vjp(err * (1.0 / D_MODEL))
        dhid_ref[...] = dhid
        dys_ref[...] = _raw_dot(dhid, wo_s, 1, 1)
        dyg_ref[...] = _raw_dot(dhid, wo_g, 1, 1)
        first = i == 0
        _accumulate(loss_ref, first, jnp.broadcast_to(loss, loss_ref.shape))
        _accumulate(dfw_ref, first, dfw)

        @pl.when(first)
        def _():
            dwo_ref[:SSD_WIDTH, :] = _raw_dot(ys, dhid, 0, 0)
            dwo_ref[SSD_WIDTH:, :] = _raw_dot(yg, dhid, 0, 0)

        @pl.when(i > 0)
        def _():
            dwo_ref[:SSD_WIDTH, :] += _raw_dot(ys, dhid, 0, 0)
            dwo_ref[SSD_WIDTH:, :] += _raw_dot(yg, dhid, 0, 0)

    row = lambda w: pl.BlockSpec((tm, w), lambda i: (i, 0))
    return pl.pallas_call(
        body, name="out_proj_loss", grid=(t // tm,),
        in_specs=[row(D_MODEL), row(SSD_WIDTH), row(GDN_W), _full((SSD_WIDTH + GDN_W, D_MODEL)), _full((1, D_MODEL)),
                  row(D_MODEL)],
        out_specs=[_full((8, LANES)), row(D_MODEL), row(SSD_WIDTH), row(GDN_W), _full((SSD_WIDTH + GDN_W, D_MODEL)),
                   _full((1, D_MODEL))],
        out_shape=[jax.ShapeDtypeStruct((8, LANES), F32), jax.ShapeDtypeStruct((t, D_MODEL), F32),
                   jax.ShapeDtypeStruct((t, SSD_WIDTH), F32), jax.ShapeDtypeStruct((t, GDN_W), F32),
                   jax.ShapeDtypeStruct((SSD_WIDTH + GDN_W, D_MODEL), F32), jax.ShapeDtypeStruct((1, D_MODEL), F32)],
        compiler_params=_params(("arbitrary",)),
    )(x, y_ssd, y_gdn, w_out, final_w, target)


def in_proj_bwd_x(x, normw, w_main, w_small, dproj_main, dsmall_a, dsmall_b, dhid, slabbed):
    t = x.shape[0]
    tm = min(256, t)
    ni = t // tm
    ns = len(slabbed)

    def body(x_ref, nw_ref, wm_ref, ws_ref, dp_ref, da_ref, db_ref, dh_ref, *rest):
        slab_refs, (gx_ref, dnw_ref), land_refs = rest[:ns], rest[ns:ns + 2], rest[ns + 2:2 * ns + 2]
        sems = rest[2 * ns + 2:]
        i = pl.program_id(0)
        start, finish = _slab_exchange(slab_refs, land_refs, ns, *sems)

        @pl.when(i == 0)
        def _():
            start()

        du = _raw_dot(dp_ref[...], wm_ref[...], 1, 1) + _raw_dot(da_ref[...] + db_ref[...], ws_ref[...], 1, 1)
        _, vjp = jax.vjp(rmsnorm, x_ref[...], nw_ref[...])
        dx, dnw = vjp(du)
        gx_ref[...] = dx + dh_ref[...]
        _accumulate(dnw_ref, i == 0, dnw)

        @pl.when(i == ni - 1)
        def _():
            finish()

    row = lambda w: pl.BlockSpec((tm, w), lambda i: (i, 0))
    out = pl.pallas_call(
        body, name="in_proj_bwd_x", grid=(ni,),
        in_specs=[row(D_MODEL), _full((1, D_MODEL)), _full((D_MODEL, MAIN)), _full((D_MODEL, LANES)), row(MAIN),
                  row(LANES), row(LANES), row(D_MODEL)] + [HBM] * ns,
        out_specs=[row(D_MODEL), _full((1, D_MODEL))] + [HBM] * ns,
        out_shape=[jax.ShapeDtypeStruct((t, D_MODEL), F32), jax.ShapeDtypeStruct((1, D_MODEL), F32)]
        + _slab_exchange_shapes(slabbed, []),
        scratch_shapes=_slab_exchange_sems(ns),
        compiler_params=_params(("arbitrary",)),
    )(x, normw, w_main, w_small, dproj_main, dsmall_a, dsmall_b, dhid, *slabbed)
    return out[0], out[1], out[2:]


def in_proj_bwd_w(u, dproj_main, dsmall_a, dsmall_b, slabbed):
    t = u.shape[0]
    tm, tn = min(1024, t), MAIN // 4
    nj, ni = MAIN // tn, t // tm
    ns = len(slabbed)

    def body(u_ref, dp_ref, da_ref, db_ref, *rest):
        slab_refs, (dwm_ref, dws_ref), land_refs, sems = rest[:ns], rest[ns:ns + 2], rest[ns + 2:2 * ns + 2], rest[2 * ns + 2:]
        j, i = pl.program_id(0), pl.program_id(1)
        start, finish = _slab_exchange(slab_refs, land_refs, ns, *sems)

        @pl.when(jnp.logical_and(j == 0, i == 0))
        def _():
            start()

        uu = u_ref[...]
        _accumulate(dwm_ref, i == 0, _raw_dot(uu, dp_ref[...], 0, 0))

        @pl.when(j == 0)
        def _():
            _accumulate(dws_ref, i == 0, _raw_dot(uu, da_ref[...] + db_ref[...], 0, 0))

        @pl.when(jnp.logical_and(j == nj - 1, i == ni - 1))
        def _():
            finish()

    out = pl.pallas_call(
        body, name="in_proj_bwd_w", grid=(nj, ni),
        in_specs=[pl.BlockSpec((tm, D_MODEL), lambda j, i: (i, 0)), pl.BlockSpec((tm, tn), lambda j, i: (i, j)),
                  pl.BlockSpec((tm, LANES), lambda j, i: (i, 0)), pl.BlockSpec((tm, LANES), lambda j, i: (i, 0))]
        + [HBM] * ns,
        out_specs=[pl.BlockSpec((D_MODEL, tn), lambda j, i: (0, j)), _full((D_MODEL, LANES))] + [HBM] * ns,
        out_shape=[jax.ShapeDtypeStruct((D_MODEL, MAIN), F32), jax.ShapeDtypeStruct((D_MODEL, LANES), F32)]
        + _slab_exchange_shapes(slabbed, []),
        scratch_shapes=_slab_exchange_sems(ns),
        compiler_params=_params(("arbitrary", "arbitrary")),
    )(u, dproj_main, dsmall_a, dsmall_b, *slabbed)
    return out[0], out[1], out[2:]


def sum_slabs(a, name):
    n, rows, cols = a.shape
    tr = 64 if rows % 64 == 0 else rows

    def body(a_ref, o_ref):
        acc = a_ref[0].astype(F32)
        for d in range(1, n):
            acc = acc + a_ref[d].astype(F32)
        o_ref[...] = acc

    return pl.pallas_call(
        body, name=name, grid=(rows // tr,),
        in_specs=[pl.BlockSpec((n, tr, cols), lambda i: (0, i, 0))],
        out_specs=pl.BlockSpec((tr, cols), lambda i: (i, 0)),
        out_shape=jax.ShapeDtypeStruct((rows, cols), F32),
        compiler_params=_params(("arbitrary",)),
    )(a)


def adamw(w, g, m, v, name):
    rows, cols = w.shape
    tr = 128 if rows % 128 == 0 else rows

    def body(w_ref, g_ref, m_ref, v_ref, d_ref, nm_ref, nv_ref):
        gg = g_ref[...]
        nm = ADAM_B1 * m_ref[...] + (1.0 - ADAM_B1) * gg
        nv = ADAM_B2 * v_ref[...] + (1.0 - ADAM_B2) * (gg * gg)
        m_hat = nm / (1.0 - ADAM_B1 ** ADAM_STEP)
        v_hat = nv / (1.0 - ADAM_B2 ** ADAM_STEP)
        d_ref[...] = -ADAM_LR * (m_hat / (jnp.sqrt(v_hat) + ADAM_EPS) + ADAM_WD * w_ref[...])
        nm_ref[...] = nm
        nv_ref[...] = nv

    spec = pl.BlockSpec((tr, cols), lambda i: (i, 0))
    shp = jax.ShapeDtypeStruct((rows, cols), F32)
    return pl.pallas_call(
        body, name=name, grid=(rows // tr,), in_specs=[spec] * 4, out_specs=[spec] * 3, out_shape=[shp] * 3,
        compiler_params=_params(("arbitrary",)),
    )(w, g, m, v)


def _my_place():
    return lax.axis_index("x"), lax.axis_index("y"), lax.axis_index("c")


def gather_weights(big, small):
    nb, n = len(big), len(big) + len(small)
    parts = 4

    def body(*refs):
        srcs, outs = refs[:n], refs[n:2 * n]
        land_a, land_b = refs[2 * n:2 * n + nb], refs[2 * n + nb:2 * n + 2 * nb]
        send_sems, recv_sems, fwd_send, fwd_recv, local_sems = refs[2 * n + 2 * nb:]
        x, y, c = _my_place()
        me = 2 * x + y
        chips = [(1 - x, y), (x, 1 - y), (1 - x, 1 - y)]
        half = [a.shape[0] // 2 for a in big]

        def ici(j, i):
            px, py = chips[j]
            if i < nb:
                src, dst = srcs[i].at[pl.ds(c * half[i], half[i])], land_a[i].at[j]
            else:
                src, dst = srcs[i], outs[i].at[me]
            return pltpu.make_async_remote_copy(src_ref=src, dst_ref=dst, send_sem=send_sems.at[j * n + i],
                                                recv_sem=recv_sems.at[j * n + i], device_id=(px, py, c),
                                                device_id_type=MESH)

        def ici_arrival(j, i):
            px, py = chips[j]
            dst = land_a[i].at[j] if i < nb else outs[i].at[2 * px + py]
            return pltpu.make_async_remote_copy(src_ref=dst, dst_ref=dst, send_sem=send_sems.at[j * n + i],
                                                recv_sem=recv_sems.at[j * n + i], device_id=(px, py, c),
                                                device_id_type=MESH)

        def forward(j, i, p):
            rows = half[i] // parts
            k = (j * nb + i) * parts + p
            return pltpu.make_async_remote_copy(
                src_ref=land_a[i].at[j, pl.ds(p * rows, rows)], dst_ref=land_b[i].at[j, pl.ds(p * rows, rows)],
                send_sem=fwd_send.at[k], recv_sem=fwd_recv.at[k], device_id=(x, y, 1 - c), device_id_type=MESH)

        def store(j, i, from_sibling):
            px, py = chips[j]
            buf, h = (land_b, 1 - c) if from_sibling else (land_a, c)
            k = n + (j * nb + i) * 2 + (1 if from_sibling else 0)
            return pltpu.make_async_copy(buf[i].at[j], outs[i].at[2 * px + py, pl.ds(h * half[i], half[i])],
                                         local_sems.at[k])

        own = [pltpu.make_async_copy(srcs[i], outs[i].at[me], local_sems.at[i]) for i in range(n)]
        sends = [ici(j, i) for j in range(3) for i in range(n)]
        for cp in own + sends:
            cp.start()
        pending = []
        for j in range(3):
            for i in range(n):
                ici_arrival(j, i).wait_recv()
                if i < nb:
                    fw = [forward(j, i, p) for p in range(parts)]
                    st = store(j, i, False)
                    for cp in fw + [st]:
                        cp.start()
                    pending += [cp.wait_send for cp in fw] + [st.wait]
        for j in range(3):
            for i in range(nb):
                for p in range(parts):
                    forward(j, i, p).wait_recv()
                st = store(j, i, True)
                st.start()
                pending.append(st.wait)
        for cp in sends:
            cp.wait_send()
        for wait in pending:
            wait()
        for cp in own:
            cp.wait()

    shards = list(big) + list(small)
    lands = [pltpu.VMEM((3, a.shape[0] // 2) + a.shape[1:], a.dtype) for a in big]
    return pl.pallas_call(
        body, name="gather_weights",
        in_specs=[HBM] * n, out_specs=[HBM] * n,
        out_shape=[jax.ShapeDtypeStruct((N_CHIP,) + s.shape, s.dtype) for s in shards],
        scratch_shapes=lands + lands + [
            pltpu.SemaphoreType.DMA((3 * n,)), pltpu.SemaphoreType.DMA((3 * n,)),
            pltpu.SemaphoreType.DMA((3 * nb * parts,)), pltpu.SemaphoreType.DMA((3 * nb * parts,)),
            pltpu.SemaphoreType.DMA((n + 6 * nb,))],
        compiler_params=pltpu.CompilerParams(vmem_limit_bytes=VMEM_LIMIT),
    )(*shards)


def _peer(x, y, c, mask):
    mx, my, mc = (mask >> 2) & 1, (mask >> 1) & 1, mask & 1
    return (x ^ mx if mx else x, y ^ my if my else y, c ^ mc if mc else c)


def _slab_exchange_shapes(slabbed, replicated):
    return ([jax.ShapeDtypeStruct(a.shape, a.dtype) for a in slabbed]
            + [jax.ShapeDtypeStruct((N_DEV,) + a.shape, a.dtype) for a in replicated])


def _slab_exchange_sems(n):
    return [pltpu.SemaphoreType.DMA((7 * n,)), pltpu.SemaphoreType.DMA((7 * n,)), pltpu.SemaphoreType.DMA((n,))]


def _slab_exchange(srcs, outs, ns, send_sems, recv_sems, local_sems):
    n = len(srcs)
    x, y, c = _my_place()
    me = 4 * x + 2 * y + c

    def piece(i, dev):
        return srcs[i].at[dev] if i < ns else srcs[i]

    def copies(arriving):
        out = []
        for mask in range(1, N_DEV):
            px, py, pc = _peer(x, y, c, mask)
            dev = 4 * px + 2 * py + pc
            for i in range(n):
                k = (mask - 1) * n + i
                out.append(pltpu.make_async_remote_copy(
                    src_ref=piece(i, dev), dst_ref=outs[i].at[dev if arriving else me], send_sem=send_sems.at[k],
                    recv_sem=recv_sems.at[k], device_id=(px, py, pc), device_id_type=MESH))
        return out

    def local():
        return [pltpu.make_async_copy(piece(i, me), outs[i].at[me], local_sems.at[i]) for i in range(n)]

    def start():
        for cp in local() + copies(False):
            cp.start()

    def finish():
        for cp in copies(True):
            cp.wait_recv()
        for cp in copies(False):
            cp.wait_send()
        for cp in local():
            cp.wait()

    return start, finish


def exchange_halves(halves, replicated):
    n, nr = len(halves), len(replicated)
    streams = 8

    def body(*refs):
        srcs, rep_srcs, outs, rep_outs = refs[:n], refs[n:n + nr], refs[n + nr:2 * n + nr], refs[2 * n + nr:2 * (n + nr)]
        refs = refs[2 * (n + nr):]
        mine, theirs = refs[:n], refs[n:2 * n]
        send_sems, recv_sems, in_sems, out_sems = refs[2 * n:2 * n + 4]
        rep_start, rep_finish = _slab_exchange(rep_srcs, rep_outs, 0, *refs[2 * n + 4:])
        rep_start()
        x, y, c = _my_place()
        loads = [pltpu.make_async_copy(srcs[i], mine[i], in_sems.at[i]) for i in range(n)]
        for cp in loads:
            cp.start()
        for cp in loads:
            cp.wait()

        def chunk_copy(i, s):
            rows = halves[i].shape[0] // streams
            k = i * streams + s
            return pltpu.make_async_remote_copy(
                src_ref=mine[i].at[pl.ds(s * rows, rows)], dst_ref=theirs[i].at[pl.ds(s * rows, rows)],
                send_sem=send_sems.at[k], recv_sem=recv_sems.at[k], device_id=(x, y, 1 - c), device_id_type=MESH)

        sends = [chunk_copy(i, s) for i in range(n) for s in range(streams)]
        for cp in sends:
            cp.start()
        own = [pltpu.make_async_copy(mine[i], outs[i].at[c], out_sems.at[i]) for i in range(n)]
        for cp in own:
            cp.start()
        for cp in sends:
            cp.wait_recv()
        got = [pltpu.make_async_copy(theirs[i], outs[i].at[1 - c], out_sems.at[n + i]) for i in range(n)]
        for cp in got:
            cp.start()
        for cp in sends:
            cp.wait_send()
        for cp in own + got:
            cp.wait()
        rep_finish()

    vmem = [pltpu.VMEM(a.shape, a.dtype) for a in halves]
    out = pl.pallas_call(
        body, name="exchange_halves",
        in_specs=[HBM] * (n + nr), out_specs=[HBM] * (n + nr),
        out_shape=[jax.ShapeDtypeStruct((2,) + a.shape, a.dtype) for a in halves]
        + _slab_exchange_shapes([], replicated),
        scratch_shapes=vmem + vmem + [pltpu.SemaphoreType.DMA((n * streams,)), pltpu.SemaphoreType.DMA((n * streams,)),
                                      pltpu.SemaphoreType.DMA((n,)), pltpu.SemaphoreType.DMA((2 * n,))]
        + _slab_exchange_sems(nr),
        compiler_params=pltpu.CompilerParams(vmem_limit_bytes=VMEM_LIMIT),
    )(*halves, *replicated)
    return out[:n], out[n:]


def _pack_cols(pieces):
    offs, pos = [], 0
    for a in pieces:
        offs.append(pos)
        pos += a.shape[1]
    rows8 = [jnp.pad(a.astype(F32), ((0, 8 - a.shape[0]), (0, 0))) for a in pieces]
    return jnp.concatenate(rows8, axis=1), offs


def adamw_many(ws, gs, ms, vs):
    n = len(ws)

    def body(*refs):
        w_r, g_r, m_r, v_r = refs[:n], refs[n:2 * n], refs[2 * n:3 * n], refs[3 * n:4 * n]
        d_o, m_o, v_o = refs[4 * n:5 * n], refs[5 * n:6 * n], refs[6 * n:7 * n]
        for i in range(n):
            gg = g_r[i][...]
            nm = ADAM_B1 * m_r[i][...] + (1.0 - ADAM_B1) * gg
            nv = ADAM_B2 * v_r[i][...] + (1.0 - ADAM_B2) * (gg * gg)
            m_hat = nm / (1.0 - ADAM_B1 ** ADAM_STEP)
            v_hat = nv / (1.0 - ADAM_B2 ** ADAM_STEP)
            d_o[i][...] = -ADAM_LR * (m_hat / (jnp.sqrt(v_hat) + ADAM_EPS) + ADAM_WD * w_r[i][...])
            m_o[i][...] = nm
            v_o[i][...] = nv

    shapes = [jax.ShapeDtypeStruct(w.shape, F32) for w in ws]
    out = pl.pallas_call(body, name="adamw_small", out_shape=shapes * 3,
                         compiler_params=pltpu.CompilerParams(vmem_limit_bytes=VMEM_LIMIT))(*ws, *gs, *ms, *vs)
    return out[:n], out[n:2 * n], out[2 * n:]


def _lanes(vec, start):
    n = vec.shape[-1]
    return jnp.pad(vec.reshape(1, n).astype(F32), ((0, 0), (start, LANES - start - n)))


def kernel(x, norm_w, w_in, ssd_conv_w, ssd_conv_b, ssd_dt_bias, ssd_a_log, ssd_d, ssd_norm_w, gdn_conv_w, gdn_dt_bias, gdn_a_log, gdn_norm_w, w_out, final_norm_w, loss_target, m_norm_w, m_w_in, m_ssd_conv_w, m_ssd_conv_b, m_ssd_dt_bias, m_ssd_a_log, m_ssd_d, m_ssd_norm_w, m_gdn_conv_w, m_gdn_dt_bias, m_gdn_a_log, m_gdn_norm_w, m_w_out, m_final_norm_w, v_norm_w, v_w_in, v_ssd_conv_w, v_ssd_conv_b, v_ssd_dt_bias, v_ssd_a_log, v_ssd_d, v_ssd_norm_w, v_gdn_conv_w, v_gdn_dt_bias, v_gdn_a_log, v_gdn_norm_w, v_w_out, v_final_norm_w):
    xs = x[0]
    target = loss_target[0]
    chip = 2 * lax.axis_index("x") + lax.axis_index("y")
    w_in_shard, w_out_shard = w_in[0], w_out[0]
    in_cols = w_in_shard.shape[1]
    out_rows = w_out_shard.shape[0]

    g_in, g_out, g_cs, g_cg = gather_weights(
        [w_in_shard.astype(MXU_DTYPE), w_out_shard.astype(MXU_DTYPE)], [ssd_conv_w[0], gdn_conv_w[0]])
    w_in_full = jnp.concatenate([g_in[k] for k in range(N_CHIP)], axis=1)
    w_out_full = g_out.reshape(N_CHIP * out_rows, D_MODEL)
    cw_ssd = jnp.concatenate([g_cs[k] for k in range(N_CHIP)], axis=1)
    cw_gdn = jnp.concatenate([g_cg[k] for k in range(N_CHIP)], axis=1)
    cb_ssd, cb_gdn = ssd_conv_b, jnp.zeros((1, GDN_CONV), F32)
    o_xbc, o_dt, o_gate, o_qkv, o_ab = 1024, 2560, 2576, 3600, 6672
    w_main = jnp.concatenate([w_in_full[:, :o_xbc], w_in_full[:, o_gate:o_qkv], w_in_full[:, o_qkv:o_ab],
                              w_in_full[:, o_xbc:o_dt]], axis=1)
    w_small = jnp.concatenate([w_in_full[:, o_dt:o_gate], w_in_full[:, o_ab:],
                               jnp.zeros((D_MODEL, LANES - 32), MXU_DTYPE)], axis=1)
    alog = _lanes(ssd_a_log, 0) + _lanes(gdn_a_log, LANE_GA)
    dtb = _lanes(ssd_dt_bias, 0) + _lanes(gdn_dt_bias, LANE_GA)
    dvec = _lanes(ssd_d, 0)
    fw = final_norm_w.reshape(1, D_MODEL)

    proj_main, proj_small, u = in_proj(xs, norm_w, w_main, w_small)
    proj_main, conv_out = in_proj_conv(proj_main, u, w_main, jnp.concatenate([cw_gdn, cw_ssd], axis=1),
                                       jnp.concatenate([cb_gdn, cb_ssd], axis=1))
    conv_ssd = conv_gdn = conv_out
    y_ssd, hist_ssd = ssd_fwd(conv_ssd, proj_main, proj_small, ssd_norm_w, alog, dtb, dvec)
    y_gdn, hist_gdn, tinv_gdn = gdn_fwd(conv_gdn, proj_main, proj_small, gdn_norm_w, alog, dtb)

    loss_blk, dhid, dy_ssd, dy_gdn, d_w_out, d_fw = out_proj_loss(xs, y_ssd, y_gdn, w_out_full, fw, target)
    dconv_ssd, dproj_main, dsmall_ssd, d_ssd_nw, d_alog_s, d_dtb_s, d_dvec = ssd_bwd(
        conv_ssd, proj_main, proj_small, ssd_norm_w, alog, dtb, dvec, hist_ssd, dy_ssd)
    dproj_main, dconv_gdn, dsmall_gdn, d_gdn_nw, d_alog_g, d_dtb_g = gdn_bwd(
        dproj_main, conv_gdn, proj_main, proj_small, gdn_norm_w, alog, dtb, hist_gdn, tinv_gdn, dy_gdn)
    dproj_main, dwb_ssd = conv_bwd(dproj_main, proj_main, COL_SSD, SSD_CONV, cw_ssd, cb_ssd, dconv_ssd, "conv_bwd_ssd")
    dproj_main, dwb_gdn = conv_bwd(dproj_main, proj_main, COL_GDN, GDN_CONV, cw_gdn, cb_gdn, dconv_gdn, "conv_bwd_gdn")
    slabs_out = d_w_out.reshape(N_DEV, out_rows // 2, D_MODEL).astype(COMM_DTYPE)
    d_w_main, d_w_small, (r_out,) = in_proj_bwd_w(u, dproj_main, dsmall_ssd, dsmall_gdn, [slabs_out])
    order = [(d_w_main, 0, COL_GATE), (d_w_main, COL_SSD, MAIN), (d_w_small, 0, 16), (d_w_main, COL_GATE, COL_SSD),
             (d_w_small, 16, 32)]
    shards, pos = [[] for _ in range(N_CHIP)], 0
    for src, lo, hi in order:
        while lo < hi:
            k = pos // in_cols
            n = min(hi - lo, (k + 1) * in_cols - pos)
            shards[k].append(src[:, lo:lo + n].astype(COMM_DTYPE))
            lo, pos = lo + n, pos + n
    slabs_in = jnp.stack([jnp.concatenate(p, axis=1) for p in shards]).reshape(N_DEV, D_MODEL // 2, in_cols)
    grad_x, d_norm_w, (r_in,) = in_proj_bwd_x(xs, norm_w, w_main, w_small, dproj_main, dsmall_ssd, dsmall_gdn,
                                               dhid, [slabs_in])
    d_alog, d_dtb = d_alog_s + d_alog_g, d_dtb_s + d_dtb_g
    packed, (o_nw, o_cs, o_cg, o_snw, o_fw, o_al, o_db, o_dv, o_gnw, o_loss) = _pack_cols([
        d_norm_w, dwb_ssd, dwb_gdn,
        d_ssd_nw.reshape(1, SSD_WIDTH), d_fw, d_alog, d_dtb, d_dvec, d_gdn_nw, loss_blk])

    half_in = sum_slabs(r_in, "sum_w_in")
    half_out = sum_slabs(r_out, "sum_w_out")
    (full_in, full_out), (r_small,) = exchange_halves([half_in, half_out], [packed])
    tot = sum_slabs(r_small, "sum_small")
    grad_w_in = full_in.reshape(D_MODEL, in_cols)
    grad_w_out = full_out.reshape(out_rows, D_MODEL)
    loss = tot[0, o_loss]
    sc, gc = ssd_conv_w.shape[2], gdn_conv_w.shape[2]
    row = lambda off, n, r=0: tot[r:r + 1, off:off + n]
    gs = [row(o_nw, D_MODEL),
          lax.dynamic_slice(tot, (0, o_cs + chip * sc), (4, sc)),
          row(o_cs, SSD_CONV, 4),
          row(o_db, SSD_HEADS), row(o_al, SSD_HEADS), row(o_dv, SSD_HEADS),
          row(o_snw, SSD_WIDTH),
          lax.dynamic_slice(tot, (0, o_cg + chip * gc), (4, gc)),
          row(o_db + LANE_GA, GDN_HEADS), row(o_al + LANE_GA, GDN_HEADS),
          row(o_gnw, GDN_DV), row(o_fw, D_MODEL)]

    names = ["norm_w", "ssd_conv_w", "ssd_conv_b", "ssd_dt_bias", "ssd_a_log", "ssd_d", "ssd_norm_w", "gdn_conv_w",
             "gdn_dt_bias", "gdn_a_log", "gdn_norm_w", "final_norm_w"]
    ws = [norm_w, ssd_conv_w, ssd_conv_b, ssd_dt_bias, ssd_a_log, ssd_d, ssd_norm_w, gdn_conv_w, gdn_dt_bias,
          gdn_a_log, gdn_norm_w, final_norm_w]
    ms = [m_norm_w, m_ssd_conv_w, m_ssd_conv_b, m_ssd_dt_bias, m_ssd_a_log, m_ssd_d, m_ssd_norm_w, m_gdn_conv_w,
          m_gdn_dt_bias, m_gdn_a_log, m_gdn_norm_w, m_final_norm_w]
    vs = [v_norm_w, v_ssd_conv_w, v_ssd_conv_b, v_ssd_dt_bias, v_ssd_a_log, v_ssd_d, v_ssd_norm_w, v_gdn_conv_w,
          v_gdn_dt_bias, v_gdn_a_log, v_gdn_norm_w, v_final_norm_w]
    shapes = [w.shape for w in ws]
    flat = lambda arrs: [a.reshape(g.shape) for a, g in zip(arrs, gs)]
    d_s, m_s, v_s = adamw_many(flat(ws), gs, flat(ms), flat(vs))
    back = lambda arrs: dict(zip(names, [a.reshape(s) for a, s in zip(arrs, shapes)]))
    delta, new_m, new_v, grads = back(d_s), back(m_s), back(v_s), back(gs)
    d_in, m_in, v_in = adamw(w_in_shard, grad_w_in, m_w_in[0], v_w_in[0], "adamw_w_in")
    d_out, m_out, v_out = adamw(w_out_shard, grad_w_out, m_w_out[0], v_w_out[0], "adamw_w_out")
    for tbl, a_in, a_out in ((grads, grad_w_in, grad_w_out), (delta, d_in, d_out), (new_m, m_in, m_out),
                             (new_v, v_in, v_out)):
        tbl["w_in"] = a_in[None]
        tbl["w_out"] = a_out[None]

    order = ["norm_w", "w_in", "ssd_conv_w", "ssd_conv_b", "ssd_dt_bias", "ssd_a_log", "ssd_d", "ssd_norm_w",
             "gdn_conv_w", "gdn_dt_bias", "gdn_a_log", "gdn_norm_w", "w_out", "final_norm_w"]
    return (loss.reshape(()), grad_x[None], *[grads[k] for k in order], *[delta[k] for k in order],
            *[new_m[k] for k in order], *[new_v[k] for k in order])
```

```python
import functools

import jax
import jax.numpy as jnp
from jax import lax
from jax.experimental import pallas as pl
from jax.experimental.pallas import tpu as pltpu

F32 = jnp.float32
MXU_DTYPE = jnp.bfloat16
COMM_DTYPE = jnp.bfloat16
MESH = pl.DeviceIdType.MESH

D_MODEL = 1024
CHUNK = 64
EPS = 1e-6
SSD_HEADS, SSD_GROUPS, SSD_STATE = 16, 2, 128
SSD_WIDTH, SSD_CONV = 1024, 1536
SSD_GW = SSD_WIDTH // SSD_GROUPS
SSD_GC = SSD_GW + 2 * SSD_STATE
GDN_HEADS, GDN_DK, GDN_DV = 8, 128, 128
GDN_W, GDN_CONV = 1024, 3072
GDN_HC = 2 * GDN_DK + GDN_DV
IN_DIM = 6688
MAIN = 6656
LANES = 128
COL_Z, COL_GATE, COL_GDN, COL_SSD = 0, 1024, 2048, 5120
COL_CONV = COL_GDN
CONV_W = MAIN - COL_CONV
GDN_HB = 8
GDN_CB = 2
LANE_GA, LANE_GB = 16, 24
N_DEV, N_CHIP = 8, 4
VMEM_LIMIT = 52 * 1024 * 1024

ADAM_LR, ADAM_B1, ADAM_B2, ADAM_EPS, ADAM_WD, ADAM_STEP = 0.001, 0.9, 0.999, 1e-08, 0.01, 10


def _split(a, n):
    parts, rest = [], a.astype(F32)
    for i in range(n):
        p = rest.astype(MXU_DTYPE)
        parts.append(p)
        if i < n - 1:
            rest = rest - p.astype(F32)
    return parts


def _raw_dot(a, b, ca, cb, mode="bf16"):
    d = lambda u, v: lax.dot_general(u, v, (((ca,), (cb,)), ((), ())), preferred_element_type=F32)
    if mode == "bf16":
        return d(a.astype(MXU_DTYPE), b.astype(MXU_DTYPE))
    if mode == "x3":
        (ah, al), (bh, bl) = _split(a, 2), _split(b, 2)
        return d(ah, bh) + (d(ah, bl) + d(al, bh))
    if mode == "sel_a":
        a0 = a.astype(MXU_DTYPE)
        b1, b2, b3 = _split(b, 3)
        return d(a0, b1) + (d(a0, b2) + d(a0, b3))
    assert mode == "sel_b", mode
    b0 = b.astype(MXU_DTYPE)
    a1, a2, a3 = _split(a, 3)
    return d(a1, b0) + (d(a2, b0) + d(a3, b0))


@functools.partial(jax.custom_vjp, nondiff_argnums=(2,))
def mm_nn(a, b, mode="bf16"):
    return _raw_dot(a, b, 1, 0, mode)


@functools.partial(jax.custom_vjp, nondiff_argnums=(2,))
def mm_nt(a, b, mode="bf16"):
    return _raw_dot(a, b, 1, 1, mode)


@functools.partial(jax.custom_vjp, nondiff_argnums=(2,))
def mm_tn(a, b, mode="bf16"):
    return _raw_dot(a, b, 0, 0, mode)


_SAME = {"bf16": ("bf16", "bf16"), "x3": ("x3", "x3")}
_NN_BWD = dict(_SAME, sel_a=("bf16", "sel_a"), sel_b=("sel_b", "bf16"))
_NT_BWD = dict(_SAME, sel_a=("bf16", "sel_b"), sel_b=("sel_b", "bf16"))
_TN_BWD = dict(_SAME, sel_a=("bf16", "sel_a"), sel_b=("sel_a", "bf16"))
mm_nn.defvjp(lambda a, b, m: (_raw_dot(a, b, 1, 0, m), (a, b)),
             lambda m, r, g: (mm_nt(g, r[1], _NN_BWD[m][0]), mm_tn(r[0], g, _NN_BWD[m][1])))
mm_nt.defvjp(lambda a, b, m: (_raw_dot(a, b, 1, 1, m), (a, b)),
             lambda m, r, g: (mm_nn(g, r[1], _NT_BWD[m][0]), mm_tn(g, r[0], _NT_BWD[m][1])))
mm_tn.defvjp(lambda a, b, m: (_raw_dot(a, b, 0, 0, m), (a, b)),
             lambda m, r, g: (mm_nt(r[1], g, _TN_BWD[m][0]), mm_nn(r[0], g, _TN_BWD[m][1])))


@jax.custom_jvp
def sigmoid(x):
    return 1.0 / (1.0 + jnp.exp(-x))


@sigmoid.defjvp
def _sigmoid_jvp(p, t):
    s = sigmoid(p[0])
    return s, t[0] * s * (1.0 - s)


@jax.custom_jvp
def softplus(x):
    return jnp.maximum(x, 0.0) + jnp.log(1.0 + jnp.exp(-jnp.abs(x)))


@softplus.defjvp
def _softplus_jvp(p, t):
    return softplus(p[0]), t[0] * sigmoid(p[0])


def silu(x):
    return x * sigmoid(x)


def rmsnorm(x, w):
    return x * lax.rsqrt(jnp.mean(x * x, axis=-1, keepdims=True) + EPS) * w


def _iota(shape, dim):
    return lax.broadcasted_iota(jnp.int32, shape, dim)


def _halves():
    lane = _iota((1, LANES), 1) >> 6
    return _ind(lane == 0), _ind(lane == 1)


def _block_diag(pair):
    h0, h1 = _halves()
    return jnp.concatenate([pair * h0, pair * h1], axis=0)


def _tri_inv_impl(mats):
    r, c = _iota((CHUNK, LANES), 0), _iota((CHUNK, LANES), 1) & (CHUNK - 1)
    eye = _ind(r == c)
    blockdiag = _ind((r >> 4) == (c >> 4))
    dot = lambda u, v: _raw_dot(u, _block_diag(v), 1, 0, "x3")
    dot1 = lambda u, v: _raw_dot(u, _block_diag(v), 1, 0)
    each = lambda f, *ls: [f(*xs) for xs in zip(*ls)]
    dg = each(lambda a: a * blockdiag, mats)
    off = each(lambda a, d: a - d, mats, dg)
    m = each(lambda d: -d, dg)
    p = each(lambda x: eye + x, m)
    pw = m
    for _ in range(3):
        pw = each(lambda x: dot1(x, x), pw)
        p = each(lambda x, y: x + dot1(x, y), p, pw)
    e = each(dot, p, off)
    e2 = each(lambda x: dot1(x, x), e)
    q = each(lambda x: eye - x, e)
    q = each(lambda x, y: x + dot1(x, y), q, e2)
    return each(dot, q, p)


def _tri_inv_bwd(ts, gs):
    h0, h1 = _halves()
    x = [mm_nt(g, _block_diag(t)) for g, t in zip(gs, ts)]
    full = [mm_tn(t, y) for t, y in zip(ts, x)]
    return [-(f[:CHUNK] * h0 + f[CHUNK:] * h1) for f in full]


@jax.custom_vjp
def tri_inv(mats):
    return _tri_inv_impl(mats)


def _tri_inv_fwd(mats):
    ts = _tri_inv_impl(mats)
    return ts, ts


tri_inv.defvjp(_tri_inv_fwd, lambda ts, gs: (_tri_inv_bwd(ts, gs),))


@jax.custom_vjp
def tri_inv_saved(mats, ts):
    del mats
    return ts


tri_inv_saved.defvjp(lambda mats, ts: (ts, ts),
                     lambda ts, gs: (_tri_inv_bwd(ts, gs), [jnp.zeros_like(t) for t in ts]))


def _ind(cond):
    return jnp.where(cond, 1.0, 0.0).astype(F32)


def _chunk_masks():
    r, c = _iota((CHUNK, CHUNK), 0), _iota((CHUNK, CHUNK), 1)
    return _ind(r >= c), _ind(r > c), _ind(r == c), _ind(_iota((CHUNK, 1), 0) == CHUNK - 1)


def _log_decay_cumsum(small, alog, dtb, tri):
    sp = softplus(small + dtb)
    la = -jnp.exp(alog) * sp
    return sp, mm_nn(tri, la, "sel_a")


def _col_of(x, lane):
    return jnp.sum(x * _ind(_iota((1, LANES), 1) == lane), axis=1, keepdims=True)


def _decay_matrix(col, tri, eye):
    row = jnp.sum(col * eye, axis=0, keepdims=True)
    return jnp.exp((col - row) * tri) * tri


def _pair_masks():
    r, c = _iota((CHUNK, LANES), 0), _iota((CHUNK, LANES), 1)
    c6 = c & (CHUNK - 1)
    return _ind(r >= c6), _ind(r > c6), (_ind(c == r), _ind(c == r + CHUNK))


def _decay_pair(col_a, col_b, tri_w, eye_w):
    h0, h1 = _halves()
    col = col_a * h0 + col_b * h1
    row = jnp.sum(col_a * eye_w[0] + col_b * eye_w[1], axis=0, keepdims=True)
    return jnp.exp((col - row) * tri_w) * tri_w


def gdn_chunk(h0, qs, ks, vs, smalls, gates, normw, alog, dtb, states, saved_t=None):
    tri, _, _, last = _chunk_masks()
    tri_w, strict_w, eye_w = _pair_masks()
    nh = len(qs[0])
    flat = lambda xss: [x for xs in xss for x in xs]
    lacs = [_log_decay_cumsum(sm, alog, dtb, tri)[1] for sm in smalls]
    qs, ks, vs, gates = flat(qs), flat(ks), flat(vs), flat(gates)
    heads, pairs = range(len(qs)), range(len(qs) // 2)
    each = lambda f, *ls: [f(*xs) for xs in zip(*ls)]
    ab = lambda xs, p: (xs[2 * p], xs[2 * p + 1])
    stack = lambda xs: jnp.concatenate(xs, axis=0)
    gc = [_col_of(lacs[i // nh], LANE_GA + h0 + i % nh) for i in heads]
    beta = [sigmoid(_col_of(smalls[i // nh], LANE_GB + h0 + i % nh)) for i in heads]
    decay = [_decay_pair(*ab(gc, p), tri_w, eye_w) for p in pairs]
    gl = each(lambda x: jnp.sum(x * last, axis=0, keepdims=True), gc)
    q = each(lambda x: x * lax.rsqrt(jnp.sum(x * x, axis=-1, keepdims=True) + EPS) * (GDN_DK ** -0.5), qs)
    k = each(lambda x: x * lax.rsqrt(jnp.sum(x * x, axis=-1, keepdims=True) + EPS), ks)
    kb = each(lambda x, b: x * b, k, beta)
    eg = each(jnp.exp, gc)
    zero = jnp.zeros((CHUNK, GDN_DK), F32)
    k_bd = [stack([join_lanes([k[2 * p], zero]), join_lanes([zero, k[2 * p + 1]])]) for p in pairs]
    a = [mm_nt(join_lanes(list(ab(kb, p))), k_bd[p]) * (decay[p] * strict_w) for p in pairs]
    t = tri_inv(a) if saved_t is None else tri_inv_saved(a, saved_t)
    attn = [mm_nt(join_lanes(list(ab(q, p))), k_bd[p]) * decay[p] for p in pairs]
    rhs = [stack([join_lanes([vs[h] * beta[h], kb[h] * eg[h]]) for h in (2 * p, 2 * p + 1)]) for p in pairs]
    uw = [mm_nn(_block_diag(t[p]), rhs[p]) for p in pairs]
    uw = [x for p in pairs for x in split_rows(uw[p])]
    u, w = zip(*[split_lanes(x) for x in uw])
    ys = []
    for c in range(len(smalls)):
        hs = range(c * nh, (c + 1) * nh)
        v_new = [u[i] - mm_nn(w[i], states[i % nh]) for i in hs]
        av = [mm_nn(_block_diag(attn[c * nh // 2 + p]), stack(list(ab(v_new, p)))) for p in range(nh // 2)]
        av = [x for y in av for x in split_rows(y)]
        o = [mm_nn(q[i] * eg[i], states[i % nh]) + av[i % nh] for i in hs]
        states = [states[i % nh] * jnp.exp(gl[i]) + mm_tn(k[i] * jnp.exp(gl[i] - gc[i]), v_new[i % nh]) for i in hs]
        ys.append([rmsnorm(o[i % nh], normw) * silu(gates[i]) for i in hs])
    return ys, states, t


@jax.custom_vjp
def split_rows(x):
    n = x.shape[0] // 2
    return [x[:n], x[n:]]


split_rows.defvjp(lambda x: (split_rows(x), None), lambda _, gs: (jnp.concatenate(gs, axis=0),))


@jax.custom_vjp
def split_lanes(x):
    return [x[:, i * LANES:(i + 1) * LANES] for i in range(x.shape[1] // LANES)]


@jax.custom_vjp
def join_lanes(xs):
    return jnp.concatenate(xs, axis=1)


split_lanes.defvjp(lambda x: (split_lanes(x), None), lambda _, gs: (join_lanes(gs),))
join_lanes.defvjp(lambda xs: (join_lanes(xs), None), lambda _, g: (split_lanes(g),))


def ssd_chunk(xs, bm, cm, z, small, normw, alog, dtb, dvec, state):
    tri, _, eye, last = _chunk_masks()
    hpg = SSD_HEADS // SSD_GROUPS
    groups = range(len(xs))
    each = lambda f, *ls: [f(*a) for a in zip(*ls)]
    sp, lac = _log_decay_cumsum(small, alog, dtb, tri)
    lac_last = jnp.sum(lac * last, axis=0, keepdims=True)
    sel = [_ind(_iota((LANES, SSD_GW), 0) == g * hpg + (_iota((LANES, SSD_GW), 1) >> 6)) for g in groups]
    expand = lambda v, mode="sel_b": [mm_nn(v, s, mode) for s in sel]
    dt_e, elac_e, toend_e = expand(sp, "bf16"), expand(jnp.exp(lac), "bf16"), expand(jnp.exp(lac_last - lac), "bf16")
    row8, row8e = _iota((8, 1), 0), _iota((8, 1), 0)
    two_e = expand(_ind(row8 == 0) * dvec + _ind(row8 == 1) * jnp.exp(lac_last))
    d_e = each(lambda v: jnp.sum(v * _ind(row8e == 0), axis=0, keepdims=True), two_e)
    chunk_e = each(lambda v: jnp.sum(v * _ind(row8e == 1), axis=0, keepdims=True), two_e)
    xdt = each(lambda a, b: a * b, xs, dt_e)
    y = each(lambda c_, st, el, x_, d_: mm_nn(c_, st) * el + x_ * d_, cm, state, elac_e, xs, d_e)
    r, c = _iota((CHUNK, LANES), 0), _iota((CHUNK, LANES), 1)
    tri_w = _ind(r >= (c & (CHUNK - 1)))
    eye_w = [_ind(c == r), _ind(c == r + CHUNK)]
    half = [_ind((_iota((1, LANES), 1) >> 6) == s) for s in range(2)]

    def decay_pair(col_a, col_b):
        col = col_a * half[0] + col_b * half[1]
        row = jnp.sum(col_a * eye_w[0] + col_b * eye_w[1], axis=0, keepdims=True)
        return jnp.exp((col - row) * tri_w) * tri_w

    pairs = range(hpg // 2)
    cb_w = each(lambda c_, b_: mm_nt(c_, jnp.concatenate([b_, b_], axis=0)), cm, bm)
    x_pairs = each(split_lanes, xdt)
    lms = [[decay_pair(_col_of(lac, g * hpg + 2 * p), _col_of(lac, g * hpg + 2 * p + 1)) for p in pairs]
           for g in groups]
    stacked = [[jnp.concatenate([x_pairs[g][p] * half[0], x_pairs[g][p] * half[1]], axis=0) for p in pairs]
               for g in groups]
    terms = [[mm_nn(cb_w[g] * lms[g][p], stacked[g][p]) for p in pairs] for g in groups]
    y = [y[g] + join_lanes(terms[g]) for g in groups]
    new_state = each(lambda st, ce, b_, xd, te: st * ce + mm_tn(b_, xd * te), state, chunk_e, bm, xdt, toend_e)
    out = each(lambda y_, z_, nw: rmsnorm(y_ * silu(z_), nw), y, z, normw)
    return out, new_state


def _params(sem=None):
    return pltpu.CompilerParams(dimension_semantics=sem, vmem_limit_bytes=VMEM_LIMIT)


def _full(shape):
    n = len(shape)
    return pl.BlockSpec(shape, lambda *_: (0,) * n)


ANY = pl.BlockSpec(memory_space=pl.ANY)
HBM = pl.BlockSpec(memory_space=pltpu.HBM)


def in_proj(x, normw, w_main, w_small):
    t = x.shape[0]
    tm, tn = min(1024, t), 512

    def body(x_ref, nw_ref, wm_ref, ws_ref, pm_ref, ps_ref, u_ref):
        @pl.when(pl.program_id(1) == 0)
        def _():
            u = rmsnorm(x_ref[...], nw_ref[...]).astype(MXU_DTYPE)
            u_ref[...] = u
            ps_ref[...] = _raw_dot(u, ws_ref[...], 1, 0)
        pm_ref[...] = _raw_dot(u_ref[...], wm_ref[...], 1, 0)

    return pl.pallas_call(
        body, name="in_proj", grid=(t // tm, COL_CONV // tn),
        in_specs=[pl.BlockSpec((tm, D_MODEL), lambda i, j: (i, 0)), _full((1, D_MODEL)),
                  pl.BlockSpec((D_MODEL, tn), lambda i, j: (0, j)), _full((D_MODEL, LANES))],
        out_specs=[pl.BlockSpec((tm, tn), lambda i, j: (i, j)), pl.BlockSpec((tm, LANES), lambda i, j: (i, 0)),
                   pl.BlockSpec((tm, D_MODEL), lambda i, j: (i, 0))],
        out_shape=[jax.ShapeDtypeStruct((t, COL_CONV), F32), jax.ShapeDtypeStruct((t, LANES), F32),
                   jax.ShapeDtypeStruct((t, D_MODEL), MXU_DTYPE)],
        compiler_params=_params(("arbitrary", "arbitrary")),
    )(x, normw, w_main, w_small)


CONV_TC = 512
HALO = 8


def _shift_down(cur, prev, s):
    rolled = pltpu.roll(cur, s, 0)
    top = jnp.where(_iota((HALO, cur.shape[1]), 0) < s, pltpu.roll(prev, s, 0), rolled[:HALO])
    if cur.shape[0] == HALO:
        return top
    return jnp.concatenate([top, rolled[HALO:]], axis=0)


def _shift_up(cur, nxt, s):
    n = cur.shape[0]
    rolled = pltpu.roll(cur, n - s, 0)
    bot = jnp.where(_iota((HALO, cur.shape[1]), 0) >= HALO - s, pltpu.roll(nxt, HALO - s, 0), rolled[n - HALO:])
    return jnp.concatenate([rolled[:n - HALO], bot], axis=0)


def _conv_pre(cur, prev, w_ref, b, cols=slice(None)):
    acc = cur * w_ref[3:4, cols] + b
    shifted = [cur]
    for s in (1, 2, 3):
        sh = _shift_down(cur, prev, s)
        shifted.append(sh)
        acc = acc + sh * w_ref[3 - s:4 - s, cols]
    return acc, shifted


def in_proj_conv(u, w_main, w, b):
    t = u.shape[0]
    tm, tn = min(1024, t), CONV_TC
    rc = min(256, tm)
    c0, nj = COL_CONV // tn, CONV_W // tn

    def body(u_ref, wm_ref, w_ref, b_ref, pm_ref, out_ref, halo_ref):
        j = pl.program_id(1)

        @pl.when(pl.program_id(0) == 0)
        def _():
            halo_ref[j] = jnp.zeros((HALO, tn), F32)

        prev = halo_ref[j]
        for r in range(tm // rc):
            rows = pl.ds(r * rc, rc)
            p = _raw_dot(u_ref[rows, :], wm_ref[...], 1, 0)
            pm_ref[rows, :] = p
            pre, _ = _conv_pre(p, prev, w_ref, b_ref[...])
            out_ref[rows, :] = silu(pre)
            prev = p[rc - HALO:]
        halo_ref[j] = prev

    return pl.pallas_call(
        body, name="in_proj_conv", grid=(t // tm, nj),
        in_specs=[pl.BlockSpec((tm, D_MODEL), lambda i, j: (i, 0)),
                  pl.BlockSpec((D_MODEL, tn), lambda i, j: (0, c0 + j)),
                  pl.BlockSpec((4, tn), lambda i, j: (0, j)), pl.BlockSpec((1, tn), lambda i, j: (0, j))],
        out_specs=[pl.BlockSpec((tm, tn), lambda i, j: (i, j)), pl.BlockSpec((tm, tn), lambda i, j: (i, j))],
        out_shape=[jax.ShapeDtypeStruct((t, CONV_W), F32), jax.ShapeDtypeStruct((t, CONV_W), F32)],
        scratch_shapes=[pltpu.VMEM((nj, HALO, tn), F32)],
        compiler_params=_params(("arbitrary", "arbitrary")),
    )(u, w_main, w, b)


def _dsilu(pre):
    sg = sigmoid(pre)
    return sg * (1.0 + pre * (1.0 - sg))


def conv_bwd_w(u, proj_conv, w, b, dout, slabbed):
    t = u.shape[0]
    tt, tn = min(512, t), 3 * CONV_TC
    nt, nj = t // tt, CONV_W // tn
    ns = len(slabbed)
    after = lambda i: jnp.minimum((i + 1) * (tt // HALO), t // HALO - 1)

    def body(u_ref, cur_ref, prev_ref, nxt_ref, w_ref, b_ref, do_ref, do_nxt_ref, *rest):
        slab_refs, (dx_ref, dw_ref, dwb_ref) = rest[:ns], rest[ns:ns + 3]
        land_refs, sems = rest[ns + 3:2 * ns + 3], rest[2 * ns + 3:]
        j, i = pl.program_id(0), pl.program_id(1)
        start, finish = _slab_exchange(slab_refs, land_refs, ns, *sems)

        @pl.when(jnp.logical_and(j == 0, i == 0))
        def _():
            start()

        @pl.when(i == 0)
        def _():
            dw_ref[...] = jnp.zeros(dw_ref.shape, F32)
            dwb_ref[...] = jnp.zeros(dwb_ref.shape, F32)

        uu = u_ref[...]
        row = _iota((HALO, CONV_TC), 0)
        first, last = i == 0, i == nt - 1
        for piece in range(tn // CONV_TC):
            cols = slice(piece * CONV_TC, (piece + 1) * CONV_TC)
            cur, bias = cur_ref[:, cols], b_ref[:, cols]
            prev = jnp.where(first, 0.0, prev_ref[:, cols])
            pre, shifted = _conv_pre(cur, prev, w_ref, bias, cols)
            dpre = do_ref[:, cols] * _dsilu(pre)
            pre_nxt, _ = _conv_pre(nxt_ref[:, cols], cur[tt - HALO:], w_ref, bias, cols)
            dpre_nxt = jnp.where(last, 0.0, do_nxt_ref[:, cols] * _dsilu(pre_nxt))
            dx = dpre * w_ref[3:4, cols]
            for s in (1, 2, 3):
                dx = dx + _shift_up(dpre, dpre_nxt, s) * w_ref[3 - s:4 - s, cols]
            dx = dx.astype(dx_ref.dtype)
            dx_ref[:, cols] = dx
            dw_ref[:, cols] += _raw_dot(uu, dx, 0, 0)
            upd = jnp.where(row == 4, jnp.sum(dpre, axis=0, keepdims=True), 0.0)
            for s in range(4):
                upd = upd + jnp.where(row == 3 - s, jnp.sum(dpre * shifted[s], axis=0, keepdims=True), 0.0)
            dwb_ref[:, cols] += upd

        @pl.when(jnp.logical_and(j == nj - 1, last))
        def _():
            finish()

    out = pl.pallas_call(
        body, name="conv_bwd_w", grid=(nj, nt),
        in_specs=[pl.BlockSpec((tt, D_MODEL), lambda j, i: (i, 0)),
                  pl.BlockSpec((tt, tn), lambda j, i: (i, j)),
                  pl.BlockSpec((HALO, tn), lambda j, i: (jnp.maximum(i * (tt // HALO) - 1, 0), j)),
                  pl.BlockSpec((HALO, tn), lambda j, i: (after(i), j)),
                  pl.BlockSpec((4, tn), lambda j, i: (0, j)), pl.BlockSpec((1, tn), lambda j, i: (0, j)),
                  pl.BlockSpec((tt, tn), lambda j, i: (i, j)),
                  pl.BlockSpec((HALO, tn), lambda j, i: (after(i), j))] + [HBM] * ns,
        out_specs=[pl.BlockSpec((tt, tn), lambda j, i: (i, j)), pl.BlockSpec((D_MODEL, tn), lambda j, i: (0, j)),
                   pl.BlockSpec((HALO, tn), lambda j, i: (0, j))] + [HBM] * ns,
        out_shape=[jax.ShapeDtypeStruct((t, CONV_W), MXU_DTYPE), jax.ShapeDtypeStruct((D_MODEL, CONV_W), F32),
                   jax.ShapeDtypeStruct((HALO, CONV_W), F32)] + _slab_exchange_shapes(slabbed, []),
        scratch_shapes=_slab_exchange_sems(ns),
        compiler_params=_params(("arbitrary", "arbitrary")),
    )(u, proj_conv, proj_conv, proj_conv, w, b, dout, dout, *slabbed)
    return out[0], out[1], out[2], out[3:]


def _ssd_cols(g):
    b0 = SSD_WIDTH + g * SSD_STATE
    c0 = SSD_WIDTH + SSD_GROUPS * SSD_STATE + g * SSD_STATE
    return slice(g * SSD_GW, (g + 1) * SSD_GW), slice(b0, b0 + SSD_STATE), slice(c0, c0 + SSD_STATE)


def _gdn_cols(j):
    return tuple(slice(s * GDN_W + j * GDN_DK, s * GDN_W + (j + 1) * GDN_DK) for s in range(3))


def _ssd_parts(xbc_ref):
    return tuple([xbc_ref[:, _ssd_cols(g)[s]] for g in range(SSD_GROUPS)] for s in range(3))


def _group_cols(ref):
    return [ref[:, g * SSD_GW:(g + 1) * SSD_GW] for g in range(SSD_GROUPS)]


def _chunk_rows(c):
    return slice(c * CHUNK, (c + 1) * CHUNK)


def _gdn_parts(qkv_ref):
    assert GDN_HB == GDN_HEADS, "the conv block is read whole: one grid step holds every head"
    return tuple([[qkv_ref[_chunk_rows(c), _gdn_cols(j)[s]] for j in range(GDN_HB)] for c in range(GDN_CB)]
                 for s in range(3))


def _head_cols(ref):
    return [[ref[_chunk_rows(c), j * GDN_DV:(j + 1) * GDN_DV] for j in range(GDN_HB)] for c in range(GDN_CB)]


def _chunk_blocks(ref):
    return [ref[_chunk_rows(c), :] for c in range(GDN_CB)]


def _first_head():
    return 0 if GDN_HB == GDN_HEADS else pl.program_id(1) * GDN_HB


def ssd_fwd(conv_ssd, proj_main, proj_small, normw, alog, dtb, dvec):
    t = conv_ssd.shape[0]
    nc = t // CHUNK

    groups = range(SSD_GROUPS)

    def body(xbc_ref, z_ref, sm_ref, nw_ref, al_ref, db_ref, dv_ref, y_ref, hist_ref, state_ref):
        @pl.when(pl.program_id(0) == 0)
        def _():
            state_ref[...] = jnp.zeros(state_ref.shape, F32)

        states = [state_ref[g] for g in groups]
        for g in groups:
            hist_ref[0, g] = states[g]
        ys, new_states = ssd_chunk(*_ssd_parts(xbc_ref), _group_cols(z_ref), sm_ref[...], _group_cols(nw_ref),
                                   al_ref[...], db_ref[...], dv_ref[...], states)
        for g in groups:
            y_ref[:, g * SSD_GW:(g + 1) * SSD_GW] = ys[g].astype(MXU_DTYPE)
            state_ref[g] = new_states[g]

    return pl.pallas_call(
        body, name="ssd_fwd", grid=(nc,),
        in_specs=[pl.BlockSpec((CHUNK, SSD_CONV), lambda c: (c, (COL_SSD - COL_CONV) // SSD_CONV)),
                  pl.BlockSpec((CHUNK, SSD_WIDTH), lambda c: (c, COL_Z // SSD_WIDTH)),
                  pl.BlockSpec((CHUNK, LANES), lambda c: (c, 0)),
                  _full((1, SSD_WIDTH)), _full((1, LANES)), _full((1, LANES)), _full((1, LANES))],
        out_specs=[pl.BlockSpec((CHUNK, SSD_WIDTH), lambda c: (c, 0)),
                   pl.BlockSpec((1, SSD_GROUPS, SSD_STATE, SSD_GW), lambda c: (c, 0, 0, 0))],
        out_shape=[jax.ShapeDtypeStruct((t, SSD_WIDTH), MXU_DTYPE),
                   jax.ShapeDtypeStruct((nc, SSD_GROUPS, SSD_STATE, SSD_GW), F32)],
        scratch_shapes=[pltpu.VMEM((SSD_GROUPS, SSD_STATE, SSD_GW), F32)],
        compiler_params=_params(("arbitrary",)),
    )(conv_ssd, proj_main, proj_small, normw, alog, dtb, dvec)


def _accumulate(ref, first, value):
    @pl.when(first)
    def _():
        ref[...] = value

    @pl.when(jnp.logical_not(first))
    def _():
        ref[...] += value


def ssd_bwd(conv_ssd, proj_main, proj_small, normw, alog, dtb, dvec, hist, dy):
    t = conv_ssd.shape[0]
    nc = t // CHUNK
    rev = lambda c: nc - 1 - c
    groups = range(SSD_GROUPS)

    def body(xbc_ref, z_ref, sm_ref, nw_ref, al_ref, db_ref, dv_ref, hist_ref, dy_ref,
             dxbc_ref, dz_ref, dsm_ref, dnw_ref, dal_ref, ddb_ref, ddv_ref, dstate_ref):
        first = pl.program_id(0) == 0

        @pl.when(first)
        def _():
            dstate_ref[...] = jnp.zeros(dstate_ref.shape, F32)

        _, vjp = jax.vjp(ssd_chunk, *_ssd_parts(xbc_ref), _group_cols(z_ref), sm_ref[...], _group_cols(nw_ref),
                         al_ref[...], db_ref[...], dv_ref[...], [hist_ref[0, g] for g in groups])
        dxs, dbm, dcm, dz, dsm, dnw, dal, ddb, ddv, dstate = vjp(
            (_group_cols(dy_ref), [dstate_ref[g] for g in groups]))
        for g in groups:
            xc, bc, cc = _ssd_cols(g)
            dxbc_ref[:, xc] = dxs[g]
            dxbc_ref[:, bc] = dbm[g]
            dxbc_ref[:, cc] = dcm[g]
            dz_ref[:, g * SSD_GW:(g + 1) * SSD_GW] = dz[g].astype(dz_ref.dtype)
            dstate_ref[g] = dstate[g]
        dsm_ref[...] = dsm
        _accumulate(dnw_ref, first, join_lanes(dnw))
        _accumulate(dal_ref, first, dal)
        _accumulate(ddb_ref, first, ddb)
        _accumulate(ddv_ref, first, ddv)

    return pl.pallas_call(
        body, name="ssd_bwd", grid=(nc,),
        in_specs=[pl.BlockSpec((CHUNK, SSD_CONV), lambda c: (rev(c), (COL_SSD - COL_CONV) // SSD_CONV)),
                  pl.BlockSpec((CHUNK, SSD_WIDTH), lambda c: (rev(c), COL_Z // SSD_WIDTH)),
                  pl.BlockSpec((CHUNK, LANES), lambda c: (rev(c), 0)),
                  _full((1, SSD_WIDTH)), _full((1, LANES)), _full((1, LANES)), _full((1, LANES)),
                  pl.BlockSpec((1, SSD_GROUPS, SSD_STATE, SSD_GW), lambda c: (rev(c), 0, 0, 0)),
                  pl.BlockSpec((CHUNK, SSD_WIDTH), lambda c: (rev(c), 0))],
        out_specs=[pl.BlockSpec((CHUNK, SSD_CONV), lambda c: (rev(c), (COL_SSD - COL_CONV) // SSD_CONV)),
                   pl.BlockSpec((CHUNK, SSD_WIDTH), lambda c: (rev(c), COL_Z // SSD_WIDTH)),
                   pl.BlockSpec((CHUNK, LANES), lambda c: (rev(c), 0)),
                   _full((1, SSD_WIDTH)), _full((1, LANES)), _full((1, LANES)), _full((1, LANES))],
        out_shape=[jax.ShapeDtypeStruct((t, CONV_W), F32), jax.ShapeDtypeStruct((t, COL_CONV), MXU_DTYPE),
                   jax.ShapeDtypeStruct((t, LANES), F32), jax.ShapeDtypeStruct((1, SSD_WIDTH), F32),
                   jax.ShapeDtypeStruct((1, LANES), F32), jax.ShapeDtypeStruct((1, LANES), F32),
                   jax.ShapeDtypeStruct((1, LANES), F32)],
        scratch_shapes=[pltpu.VMEM((SSD_GROUPS, SSD_STATE, SSD_GW), F32)],
        compiler_params=_params(("arbitrary",)),
    )(conv_ssd, proj_main, proj_small, normw, alog, dtb, dvec, hist, dy)


def gdn_fwd(conv_gdn, proj_main, proj_small, normw, alog, dtb):
    t = conv_gdn.shape[0]
    hb, cb = GDN_HB, GDN_CB
    rows = CHUNK * cb
    ns = t // rows
    gate_blk = COL_GATE // (GDN_DV * hb)

    def body(qkv_ref, gate_ref, sm_ref, nw_ref, al_ref, db_ref, y_ref, hist_ref, t_ref, state_ref):
        h0 = _first_head()

        @pl.when(pl.program_id(0) == 0)
        def _():
            for j in range(hb):
                state_ref[h0 + j] = jnp.zeros((GDN_DK, GDN_DV), F32)

        states = [state_ref[h0 + j] for j in range(hb)]
        for j in range(hb):
            hist_ref[0, j] = states[j]
        qs, ks, vs = _gdn_parts(qkv_ref)
        ys, new_states, ts = gdn_chunk(h0, qs, ks, vs, _chunk_blocks(sm_ref), _head_cols(gate_ref), nw_ref[...],
                                       al_ref[...], db_ref[...], states)
        for c in range(cb):
            for j in range(hb):
                y_ref[_chunk_rows(c), j * GDN_DV:(j + 1) * GDN_DV] = ys[c][j].astype(MXU_DTYPE)
        for j in range(hb):
            state_ref[h0 + j] = new_states[j]
        for p in range(cb * hb // 2):
            t_ref[0, p] = ts[p]

    return pl.pallas_call(
        body, name="gdn_fwd", grid=(ns, GDN_HEADS // hb),
        in_specs=[pl.BlockSpec((rows, GDN_HC * hb), lambda c, h: (c, h)),
                  pl.BlockSpec((rows, GDN_DV * hb), lambda c, h: (c, gate_blk + h)),
                  pl.BlockSpec((rows, LANES), lambda c, h: (c, 0)),
                  _full((1, GDN_DV)), _full((1, LANES)), _full((1, LANES))],
        out_specs=[pl.BlockSpec((rows, GDN_DV * hb), lambda c, h: (c, h)),
                   pl.BlockSpec((1, hb, GDN_DK, GDN_DV), lambda c, h: (c, h, 0, 0)),
                   pl.BlockSpec((1, cb * hb // 2, CHUNK, LANES), lambda c, h: (c, h, 0, 0))],
        out_shape=[jax.ShapeDtypeStruct((t, GDN_W), MXU_DTYPE),
                   jax.ShapeDtypeStruct((ns, GDN_HEADS, GDN_DK, GDN_DV), F32),
                   jax.ShapeDtypeStruct((ns, cb * GDN_HEADS // 2, CHUNK, LANES), F32)],
        scratch_shapes=[pltpu.VMEM((GDN_HEADS, GDN_DK, GDN_DV), F32)],
        compiler_params=_params(("arbitrary", "arbitrary")),
    )(conv_gdn, proj_main, proj_small, normw, alog, dtb)


def gdn_bwd(dproj_main, dconv, conv_gdn, proj_main, proj_small, normw, alog, dtb, hist, t_inv, dy):
    t = conv_gdn.shape[0]
    hb, cb = GDN_HB, GDN_CB
    rows = CHUNK * cb
    ns = t // rows
    rev = lambda c: ns - 1 - c
    gate_blk = COL_GATE // (GDN_DV * hb)

    def body(alias_ref, alias2_ref, qkv_ref, gate_ref, sm_ref, nw_ref, al_ref, db_ref, hist_ref, t_ref, dy_ref,
             dgate_ref, dqkv_ref, dsm_ref, dnw_ref, dal_ref, ddb_ref, dstate_ref):
        del alias_ref, alias2_ref
        c, h = pl.program_id(0), pl.program_id(1)
        h0 = _first_head()

        @pl.when(c == 0)
        def _():
            for j in range(hb):
                dstate_ref[h0 + j] = jnp.zeros((GDN_DK, GDN_DV), F32)

        saved = [t_ref[0, p] for p in range(cb * hb // 2)]

        def fn(qs, ks, vs, smalls, gates, nw, al, db, states):
            return gdn_chunk(h0, qs, ks, vs, smalls, gates, nw, al, db, states, saved)[:2]

        qs, ks, vs = _gdn_parts(qkv_ref)
        _, vjp = jax.vjp(fn, qs, ks, vs, _chunk_blocks(sm_ref), _head_cols(gate_ref), nw_ref[...], al_ref[...],
                         db_ref[...], [hist_ref[0, j] for j in range(hb)])
        dqs, dks, dvs, dsm, dgates, dnw, dal, ddb, dstates = vjp(
            (_head_cols(dy_ref), [dstate_ref[h0 + j] for j in range(hb)]))
        for k in range(cb):
            rk = _chunk_rows(k)
            for j in range(hb):
                qc, kc, vc = _gdn_cols(j)
                dqkv_ref[rk, qc] = dqs[k][j]
                dqkv_ref[rk, kc] = dks[k][j]
                dqkv_ref[rk, vc] = dvs[k][j]
                dgate_ref[rk, j * GDN_DV:(j + 1) * GDN_DV] = dgates[k][j].astype(dgate_ref.dtype)
        for j in range(hb):
            dstate_ref[h0 + j] = dstates[j]
        _accumulate(dsm_ref, h == 0, jnp.concatenate(dsm, axis=0))
        first = jnp.logical_and(c == 0, h == 0)
        _accumulate(dnw_ref, first, dnw)
        _accumulate(dal_ref, first, dal)
        _accumulate(ddb_ref, first, ddb)

    return pl.pallas_call(
        body, name="gdn_bwd", grid=(ns, GDN_HEADS // hb),
        in_specs=[ANY, ANY, pl.BlockSpec((rows, GDN_HC * hb), lambda c, h: (rev(c), h)),
                  pl.BlockSpec((rows, GDN_DV * hb), lambda c, h: (rev(c), gate_blk + h)),
                  pl.BlockSpec((rows, LANES), lambda c, h: (rev(c), 0)),
                  _full((1, GDN_DV)), _full((1, LANES)), _full((1, LANES)),
                  pl.BlockSpec((1, hb, GDN_DK, GDN_DV), lambda c, h: (rev(c), h, 0, 0)),
                  pl.BlockSpec((1, cb * hb // 2, CHUNK, LANES), lambda c, h: (rev(c), h, 0, 0)),
                  pl.BlockSpec((rows, GDN_DV * hb), lambda c, h: (rev(c), h))],
        out_specs=[pl.BlockSpec((rows, GDN_DV * hb), lambda c, h: (rev(c), gate_blk + h)),
                   pl.BlockSpec((rows, GDN_HC * hb), lambda c, h: (rev(c), h)),
                   pl.BlockSpec((rows, LANES), lambda c, h: (rev(c), 0)),
                   _full((1, GDN_DV)), _full((1, LANES)), _full((1, LANES))],
        out_shape=[jax.ShapeDtypeStruct(dproj_main.shape, dproj_main.dtype),
                   jax.ShapeDtypeStruct(dconv.shape, dconv.dtype),
                   jax.ShapeDtypeStruct((t, LANES), F32), jax.ShapeDtypeStruct((1, GDN_DV), F32),
                   jax.ShapeDtypeStruct((1, LANES), F32), jax.ShapeDtypeStruct((1, LANES), F32)],
        scratch_shapes=[pltpu.VMEM((GDN_HEADS, GDN_DK, GDN_DV), F32)],
        input_output_aliases={0: 0, 1: 1},
        compiler_params=_params(("arbitrary", "arbitrary")),
    )(dproj_main, dconv, conv_gdn, proj_main, proj_small, normw, alog, dtb, hist, t_inv, dy)


def out_proj_loss(x, y_ssd, y_gdn, w_out, final_w, target):
    t = x.shape[0]
    tm = min(256, t)

    def body(x_ref, ys_ref, yg_ref, wo_ref, fw_ref, tg_ref, loss_ref, dhid_ref, dys_ref, dyg_ref, dwo_ref, dfw_ref):
        i = pl.program_id(0)
        ys, yg = ys_ref[...], yg_ref[...]
        wo_s, wo_g = wo_ref[:SSD_WIDTH, :], wo_ref[SSD_WIDTH:, :]
        hid = x_ref[...] + _raw_dot(ys, wo_s, 1, 0) + _raw_dot(yg, wo_g, 1, 0)
        out, vjp = jax.vjp(rmsnorm, hid, fw_ref[...])
        err = out - tg_ref[...]
        loss = 0.5 * jnp.sum(jnp.mean(err * err, axis=-1, keepdims=True), axis=0, keepdims=True)
        dhid, dfw = vjp(err * (1.0 / D_MODEL))
        dhid_ref[...] = dhid
        dys_ref[...] = _raw_dot(dhid, wo_s, 1, 1)
        dyg_ref[...] = _raw_dot(dhid, wo_g, 1, 1)
        first = i == 0
        _accumulate(loss_ref, first, jnp.broadcast_to(loss, loss_ref.shape))
        _accumulate(dfw_ref, first, dfw)

        @pl.when(first)
        def _():
            dwo_ref[:SSD_WIDTH, :] = _raw_dot(ys, dhid, 0, 0)
            dwo_ref[SSD_WIDTH:, :] = _raw_dot(yg, dhid, 0, 0)

        @pl.when(i > 0)
        def _():
            dwo_ref[:SSD_WIDTH, :] += _raw_dot(ys, dhid, 0, 0)
            dwo_ref[SSD_WIDTH:, :] += _raw_dot(yg, dhid, 0, 0)

    row = lambda w: pl.BlockSpec((tm, w), lambda i: (i, 0))
    return pl.pallas_call(
        body, name="out_proj_loss", grid=(t // tm,),
        in_specs=[row(D_MODEL), row(SSD_WIDTH), row(GDN_W), _full((SSD_WIDTH + GDN_W, D_MODEL)), _full((1, D_MODEL)),
                  row(D_MODEL)],
        out_specs=[_full((8, LANES)), row(D_MODEL), row(SSD_WIDTH), row(GDN_W), _full((SSD_WIDTH + GDN_W, D_MODEL)),
                   _full((1, D_MODEL))],
        out_shape=[jax.ShapeDtypeStruct((8, LANES), F32), jax.ShapeDtypeStruct((t, D_MODEL), F32),
                   jax.ShapeDtypeStruct((t, SSD_WIDTH), F32), jax.ShapeDtypeStruct((t, GDN_W), F32),
                   jax.ShapeDtypeStruct((SSD_WIDTH + GDN_W, D_MODEL), F32), jax.ShapeDtypeStruct((1, D_MODEL), F32)],
        compiler_params=_params(("arbitrary",)),
    )(x, y_ssd, y_gdn, w_out, final_w, target)


def in_proj_bwd_x(x, normw, w_main, w_small, dproj_main, dproj_conv, dsmall_a, dsmall_b, dhid, slabbed):
    t = x.shape[0]
    tm = min(256, t)
    ni = t // tm
    ns = len(slabbed)

    def body(x_ref, nw_ref, wm_ref, ws_ref, dp_ref, dc_ref, da_ref, db_ref, dh_ref, *rest):
        slab_refs, (gx_ref, dnw_ref), land_refs = rest[:ns], rest[ns:ns + 2], rest[ns + 2:2 * ns + 2]
        sems = rest[2 * ns + 2:]
        i = pl.program_id(0)
        start, finish = _slab_exchange(slab_refs, land_refs, ns, *sems)

        @pl.when(i == 0)
        def _():
            start()

        du = (_raw_dot(dp_ref[...], wm_ref[:, :COL_CONV], 1, 1) + _raw_dot(dc_ref[...], wm_ref[:, COL_CONV:], 1, 1)
              + _raw_dot(da_ref[...] + db_ref[...], ws_ref[...], 1, 1))
        _, vjp = jax.vjp(rmsnorm, x_ref[...], nw_ref[...])
        dx, dnw = vjp(du)
        gx_ref[...] = dx + dh_ref[...]
        _accumulate(dnw_ref, i == 0, dnw)

        @pl.when(i == ni - 1)
        def _():
            finish()

    row = lambda w: pl.BlockSpec((tm, w), lambda i: (i, 0))
    out = pl.pallas_call(
        body, name="in_proj_bwd_x", grid=(ni,),
        in_specs=[row(D_MODEL), _full((1, D_MODEL)), _full((D_MODEL, MAIN)), _full((D_MODEL, LANES)), row(COL_CONV),
                  row(CONV_W), row(LANES), row(LANES), row(D_MODEL)] + [HBM] * ns,
        out_specs=[row(D_MODEL), _full((1, D_MODEL))] + [HBM] * ns,
        out_shape=[jax.ShapeDtypeStruct((t, D_MODEL), F32), jax.ShapeDtypeStruct((1, D_MODEL), F32)]
        + _slab_exchange_shapes(slabbed, []),
        scratch_shapes=_slab_exchange_sems(ns),
        compiler_params=_params(("arbitrary",)),
    )(x, normw, w_main, w_small, dproj_main, dproj_conv, dsmall_a, dsmall_b, dhid, *slabbed)
    return out[0], out[1], out[2:]


def in_proj_bwd_w(u, dproj_main, dsmall_a, dsmall_b):
    t = u.shape[0]
    tm, tn = min(1024, t), COL_CONV // 2

    def body(u_ref, dp_ref, da_ref, db_ref, dwm_ref, dws_ref):
        j, i = pl.program_id(0), pl.program_id(1)
        uu = u_ref[...]
        _accumulate(dwm_ref, i == 0, _raw_dot(uu, dp_ref[...], 0, 0))

        @pl.when(j == 0)
        def _():
            _accumulate(dws_ref, i == 0, _raw_dot(uu, da_ref[...] + db_ref[...], 0, 0))

    return pl.pallas_call(
        body, name="in_proj_bwd_w", grid=(COL_CONV // tn, t // tm),
        in_specs=[pl.BlockSpec((tm, D_MODEL), lambda j, i: (i, 0)), pl.BlockSpec((tm, tn), lambda j, i: (i, j)),
                  pl.BlockSpec((tm, LANES), lambda j, i: (i, 0)), pl.BlockSpec((tm, LANES), lambda j, i: (i, 0))],
        out_specs=[pl.BlockSpec((D_MODEL, tn), lambda j, i: (0, j)), _full((D_MODEL, LANES))],
        out_shape=[jax.ShapeDtypeStruct((D_MODEL, COL_CONV), F32), jax.ShapeDtypeStruct((D_MODEL, LANES), F32)],
        compiler_params=_params(("arbitrary", "arbitrary")),
    )(u, dproj_main, dsmall_a, dsmall_b)


def sum_slabs(a, name):
    n, rows, cols = a.shape
    tr = 64 if rows % 64 == 0 else rows

    def body(a_ref, o_ref):
        acc = a_ref[0].astype(F32)
        for d in range(1, n):
            acc = acc + a_ref[d].astype(F32)
        o_ref[...] = acc

    return pl.pallas_call(
        body, name=name, grid=(rows // tr,),
        in_specs=[pl.BlockSpec((n, tr, cols), lambda i: (0, i, 0))],
        out_specs=pl.BlockSpec((tr, cols), lambda i: (i, 0)),
        out_shape=jax.ShapeDtypeStruct((rows, cols), F32),
        compiler_params=_params(("arbitrary",)),
    )(a)


def adamw(w, g, m, v, name):
    rows, cols = w.shape
    tr = 128 if rows % 128 == 0 else rows

    def body(w_ref, g_ref, m_ref, v_ref, d_ref, nm_ref, nv_ref):
        gg = g_ref[...]
        nm = ADAM_B1 * m_ref[...] + (1.0 - ADAM_B1) * gg
        nv = ADAM_B2 * v_ref[...] + (1.0 - ADAM_B2) * (gg * gg)
        m_hat = nm / (1.0 - ADAM_B1 ** ADAM_STEP)
        v_hat = nv / (1.0 - ADAM_B2 ** ADAM_STEP)
        d_ref[...] = -ADAM_LR * (m_hat / (jnp.sqrt(v_hat) + ADAM_EPS) + ADAM_WD * w_ref[...])
        nm_ref[...] = nm
        nv_ref[...] = nv

    spec = pl.BlockSpec((tr, cols), lambda i: (i, 0))
    shp = jax.ShapeDtypeStruct((rows, cols), F32)
    return pl.pallas_call(
        body, name=name, grid=(rows // tr,), in_specs=[spec] * 4, out_specs=[spec] * 3, out_shape=[shp] * 3,
        compiler_params=_params(("arbitrary",)),
    )(w, g, m, v)


def _my_place():
    return lax.axis_index("x"), lax.axis_index("y"), lax.axis_index("c")


def gather_weights(big, small):
    nb, n = len(big), len(big) + len(small)
    parts = 4

    def body(*refs):
        srcs, outs = refs[:n], refs[n:2 * n]
        land_a, land_b = refs[2 * n:2 * n + nb], refs[2 * n + nb:2 * n + 2 * nb]
        send_sems, recv_sems, fwd_send, fwd_recv, local_sems = refs[2 * n + 2 * nb:]
        x, y, c = _my_place()
        me = 2 * x + y
        chips = [(1 - x, y), (x, 1 - y), (1 - x, 1 - y)]
        half = [a.shape[0] // 2 for a in big]

        def ici(j, i):
            px, py = chips[j]
            if i < nb:
                src, dst = srcs[i].at[pl.ds(c * half[i], half[i])], land_a[i].at[j]
            else:
                src, dst = srcs[i], outs[i].at[me]
            return pltpu.make_async_remote_copy(src_ref=src, dst_ref=dst, send_sem=send_sems.at[j * n + i],
                                                recv_sem=recv_sems.at[j * n + i], device_id=(px, py, c),
                                                device_id_type=MESH)

        def ici_arrival(j, i):
            px, py = chips[j]
            dst = land_a[i].at[j] if i < nb else outs[i].at[2 * px + py]
            return pltpu.make_async_remote_copy(src_ref=dst, dst_ref=dst, send_sem=send_sems.at[j * n + i],
                                                recv_sem=recv_sems.at[j * n + i], device_id=(px, py, c),
                                                device_id_type=MESH)

        def forward(j, i, p):
            rows = half[i] // parts
            k = (j * nb + i) * parts + p
            return pltpu.make_async_remote_copy(
                src_ref=land_a[i].at[j, pl.ds(p * rows, rows)], dst_ref=land_b[i].at[j, pl.ds(p * rows, rows)],
                send_sem=fwd_send.at[k], recv_sem=fwd_recv.at[k], device_id=(x, y, 1 - c), device_id_type=MESH)

        def store(j, i, from_sibling):
            px, py = chips[j]
            buf, h = (land_b, 1 - c) if from_sibling else (land_a, c)
            k = n + (j * nb + i) * 2 + (1 if from_sibling else 0)
            return pltpu.make_async_copy(buf[i].at[j], outs[i].at[2 * px + py, pl.ds(h * half[i], half[i])],
                                         local_sems.at[k])

        own = [pltpu.make_async_copy(srcs[i], outs[i].at[me], local_sems.at[i]) for i in range(n)]
        sends = [ici(j, i) for j in range(3) for i in range(n)]
        for cp in own + sends:
            cp.start()
        pending = []
        for j in range(3):
            for i in range(n):
                ici_arrival(j, i).wait_recv()
                if i < nb:
                    fw = [forward(j, i, p) for p in range(parts)]
                    st = store(j, i, False)
                    for cp in fw + [st]:
                        cp.start()
                    pending += [cp.wait_send for cp in fw] + [st.wait]
        for j in range(3):
            for i in range(nb):
                for p in range(parts):
                    forward(j, i, p).wait_recv()
                st = store(j, i, True)
                st.start()
                pending.append(st.wait)
        for cp in sends:
            cp.wait_send()
        for wait in pending:
            wait()
        for cp in own:
            cp.wait()

    shards = list(big) + list(small)
    lands = [pltpu.VMEM((3, a.shape[0] // 2) + a.shape[1:], a.dtype) for a in big]
    return pl.pallas_call(
        body, name="gather_weights",
        in_specs=[HBM] * n, out_specs=[HBM] * n,
        out_shape=[jax.ShapeDtypeStruct((N_CHIP,) + s.shape, s.dtype) for s in shards],
        scratch_shapes=lands + lands + [
            pltpu.SemaphoreType.DMA((3 * n,)), pltpu.SemaphoreType.DMA((3 * n,)),
            pltpu.SemaphoreType.DMA((3 * nb * parts,)), pltpu.SemaphoreType.DMA((3 * nb * parts,)),
            pltpu.SemaphoreType.DMA((n + 6 * nb,))],
        compiler_params=pltpu.CompilerParams(vmem_limit_bytes=VMEM_LIMIT),
    )(*shards)


def _peer(x, y, c, mask):
    mx, my, mc = (mask >> 2) & 1, (mask >> 1) & 1, mask & 1
    return (x ^ mx if mx else x, y ^ my if my else y, c ^ mc if mc else c)


def _slab_exchange_shapes(slabbed, replicated):
    return ([jax.ShapeDtypeStruct(a.shape, a.dtype) for a in slabbed]
            + [jax.ShapeDtypeStruct((N_DEV,) + a.shape, a.dtype) for a in replicated])


def _slab_exchange_sems(n):
    return [pltpu.SemaphoreType.DMA((7 * n,)), pltpu.SemaphoreType.DMA((7 * n,)), pltpu.SemaphoreType.DMA((n,))]


def _slab_exchange(srcs, outs, ns, send_sems, recv_sems, local_sems):
    n = len(srcs)
    x, y, c = _my_place()
    me = 4 * x + 2 * y + c

    def piece(i, dev):
        return srcs[i].at[dev] if i < ns else srcs[i]

    def copies(arriving):
        out = []
        for mask in range(1, N_DEV):
            px, py, pc = _peer(x, y, c, mask)
            dev = 4 * px + 2 * py + pc
            for i in range(n):
                k = (mask - 1) * n + i
                out.append(pltpu.make_async_remote_copy(
                    src_ref=piece(i, dev), dst_ref=outs[i].at[dev if arriving else me], send_sem=send_sems.at[k],
                    recv_sem=recv_sems.at[k], device_id=(px, py, pc), device_id_type=MESH))
        return out

    def local():
        return [pltpu.make_async_copy(piece(i, me), outs[i].at[me], local_sems.at[i]) for i in range(n)]

    def start():
        for cp in local() + copies(False):
            cp.start()

    def finish():
        for cp in copies(True):
            cp.wait_recv()
        for cp in copies(False):
            cp.wait_send()
        for cp in local():
            cp.wait()

    return start, finish


def exchange_halves(halves, replicated):
    n, nr = len(halves), len(replicated)
    streams = 8

    def body(*refs):
        srcs, rep_srcs, outs, rep_outs = refs[:n], refs[n:n + nr], refs[n + nr:2 * n + nr], refs[2 * n + nr:2 * (n + nr)]
        refs = refs[2 * (n + nr):]
        mine, theirs = refs[:n], refs[n:2 * n]
        send_sems, recv_sems, in_sems, out_sems = refs[2 * n:2 * n + 4]
        rep_start, rep_finish = _slab_exchange(rep_srcs, rep_outs, 0, *refs[2 * n + 4:])
        rep_start()
        x, y, c = _my_place()
        loads = [pltpu.make_async_copy(srcs[i], mine[i], in_sems.at[i]) for i in range(n)]
        for cp in loads:
            cp.start()
        for cp in loads:
            cp.wait()

        def chunk_copy(i, s):
            rows = halves[i].shape[0] // streams
            k = i * streams + s
            return pltpu.make_async_remote_copy(
                src_ref=mine[i].at[pl.ds(s * rows, rows)], dst_ref=theirs[i].at[pl.ds(s * rows, rows)],
                send_sem=send_sems.at[k], recv_sem=recv_sems.at[k], device_id=(x, y, 1 - c), device_id_type=MESH)

        sends = [chunk_copy(i, s) for i in range(n) for s in range(streams)]
        for cp in sends:
            cp.start()
        own = [pltpu.make_async_copy(mine[i], outs[i].at[c], out_sems.at[i]) for i in range(n)]
        for cp in own:
            cp.start()
        for cp in sends:
            cp.wait_recv()
        got = [pltpu.make_async_copy(theirs[i], outs[i].at[1 - c], out_sems.at[n + i]) for i in range(n)]
        for cp in got:
            cp.start()
        for cp in sends:
            cp.wait_send()
        for cp in own + got:
            cp.wait()
        rep_finish()

    vmem = [pltpu.VMEM(a.shape, a.dtype) for a in halves]
    out = pl.pallas_call(
        body, name="exchange_halves",
        in_specs=[HBM] * (n + nr), out_specs=[HBM] * (n + nr),
        out_shape=[jax.ShapeDtypeStruct((2,) + a.shape, a.dtype) for a in halves]
        + _slab_exchange_shapes([], replicated),
        scratch_shapes=vmem + vmem + [pltpu.SemaphoreType.DMA((n * streams,)), pltpu.SemaphoreType.DMA((n * streams,)),
                                      pltpu.SemaphoreType.DMA((n,)), pltpu.SemaphoreType.DMA((2 * n,))]
        + _slab_exchange_sems(nr),
        compiler_params=pltpu.CompilerParams(vmem_limit_bytes=VMEM_LIMIT),
    )(*halves, *replicated)
    return out[:n], out[n:]


def _pack_cols(pieces):
    offs, pos = [], 0
    for a in pieces:
        offs.append(pos)
        pos += a.shape[1]
    rows8 = [jnp.pad(a.astype(F32), ((0, 8 - a.shape[0]), (0, 0))) for a in pieces]
    return jnp.concatenate(rows8, axis=1), offs


def adamw_many(ws, gs, ms, vs):
    n = len(ws)

    def body(*refs):
        w_r, g_r, m_r, v_r = refs[:n], refs[n:2 * n], refs[2 * n:3 * n], refs[3 * n:4 * n]
        d_o, m_o, v_o = refs[4 * n:5 * n], refs[5 * n:6 * n], refs[6 * n:7 * n]
        for i in range(n):
            gg = g_r[i][...]
            nm = ADAM_B1 * m_r[i][...] + (1.0 - ADAM_B1) * gg
            nv = ADAM_B2 * v_r[i][...] + (1.0 - ADAM_B2) * (gg * gg)
            m_hat = nm / (1.0 - ADAM_B1 ** ADAM_STEP)
            v_hat = nv / (1.0 - ADAM_B2 ** ADAM_STEP)
            d_o[i][...] = -ADAM_LR * (m_hat / (jnp.sqrt(v_hat) + ADAM_EPS) + ADAM_WD * w_r[i][...])
            m_o[i][...] = nm
            v_o[i][...] = nv

    shapes = [jax.ShapeDtypeStruct(w.shape, F32) for w in ws]
    out = pl.pallas_call(body, name="adamw_small", out_shape=shapes * 3,
                         compiler_params=pltpu.CompilerParams(vmem_limit_bytes=VMEM_LIMIT))(*ws, *gs, *ms, *vs)
    return out[:n], out[n:2 * n], out[2 * n:]


def _lanes(vec, start):
    n = vec.shape[-1]
    return jnp.pad(vec.reshape(1, n).astype(F32), ((0, 0), (start, LANES - start - n)))


def kernel(x, norm_w, w_in, ssd_conv_w, ssd_conv_b, ssd_dt_bias, ssd_a_log, ssd_d, ssd_norm_w, gdn_conv_w, gdn_dt_bias, gdn_a_log, gdn_norm_w, w_out, final_norm_w, loss_target, m_norm_w, m_w_in, m_ssd_conv_w, m_ssd_conv_b, m_ssd_dt_bias, m_ssd_a_log, m_ssd_d, m_ssd_norm_w, m_gdn_conv_w, m_gdn_dt_bias, m_gdn_a_log, m_gdn_norm_w, m_w_out, m_final_norm_w, v_norm_w, v_w_in, v_ssd_conv_w, v_ssd_conv_b, v_ssd_dt_bias, v_ssd_a_log, v_ssd_d, v_ssd_norm_w, v_gdn_conv_w, v_gdn_dt_bias, v_gdn_a_log, v_gdn_norm_w, v_w_out, v_final_norm_w):
    xs = x[0]
    target = loss_target[0]
    chip = 2 * lax.axis_index("x") + lax.axis_index("y")
    w_in_shard, w_out_shard = w_in[0], w_out[0]
    in_cols = w_in_shard.shape[1]
    out_rows = w_out_shard.shape[0]

    g_in, g_out, g_cs, g_cg = gather_weights(
        [w_in_shard.astype(MXU_DTYPE), w_out_shard.astype(MXU_DTYPE)], [ssd_conv_w[0], gdn_conv_w[0]])
    w_in_full = jnp.concatenate([g_in[k] for k in range(N_CHIP)], axis=1)
    w_out_full = g_out.reshape(N_CHIP * out_rows, D_MODEL)
    cw_ssd = jnp.concatenate([g_cs[k] for k in range(N_CHIP)], axis=1)
    cw_gdn = jnp.concatenate([g_cg[k] for k in range(N_CHIP)], axis=1)
    cb_ssd, cb_gdn = ssd_conv_b, jnp.zeros((1, GDN_CONV), F32)
    o_xbc, o_dt, o_gate, o_qkv, o_ab = 1024, 2560, 2576, 3600, 6672
    w_main = jnp.concatenate([w_in_full[:, :o_xbc], w_in_full[:, o_gate:o_qkv], w_in_full[:, o_qkv:o_ab],
                              w_in_full[:, o_xbc:o_dt]], axis=1)
    w_small = jnp.concatenate([w_in_full[:, o_dt:o_gate], w_in_full[:, o_ab:],
                               jnp.zeros((D_MODEL, LANES - 32), MXU_DTYPE)], axis=1)
    alog = _lanes(ssd_a_log, 0) + _lanes(gdn_a_log, LANE_GA)
    dtb = _lanes(ssd_dt_bias, 0) + _lanes(gdn_dt_bias, LANE_GA)
    dvec = _lanes(ssd_d, 0)
    fw = final_norm_w.reshape(1, D_MODEL)

    cw, cb = jnp.concatenate([cw_gdn, cw_ssd], axis=1), jnp.concatenate([cb_gdn, cb_ssd], axis=1)
    proj_main, proj_small, u = in_proj(xs, norm_w, w_main, w_small)
    proj_conv, conv_out = in_proj_conv(u, w_main, cw, cb)
    y_ssd, hist_ssd = ssd_fwd(conv_out, proj_main, proj_small, ssd_norm_w, alog, dtb, dvec)
    y_gdn, hist_gdn, tinv_gdn = gdn_fwd(conv_out, proj_main, proj_small, gdn_norm_w, alog, dtb)

    loss_blk, dhid, dy_ssd, dy_gdn, d_w_out, d_fw = out_proj_loss(xs, y_ssd, y_gdn, w_out_full, fw, target)
    dconv, dproj_main, dsmall_ssd, d_ssd_nw, d_alog_s, d_dtb_s, d_dvec = ssd_bwd(
        conv_out, proj_main, proj_small, ssd_norm_w, alog, dtb, dvec, hist_ssd, dy_ssd)
    dproj_main, dconv, dsmall_gdn, d_gdn_nw, d_alog_g, d_dtb_g = gdn_bwd(
        dproj_main, dconv, conv_out, proj_main, proj_small, gdn_norm_w, alog, dtb, hist_gdn, tinv_gdn, dy_gdn)
    slabs_out = d_w_out.reshape(N_DEV, out_rows // 2, D_MODEL).astype(COMM_DTYPE)
    dproj_conv, d_w_conv, dwb, (r_out,) = conv_bwd_w(u, proj_conv, cw, cb, dconv, [slabs_out])
    dwb_gdn, dwb_ssd = dwb[:, :GDN_CONV], dwb[:, GDN_CONV:]
    d_w_zg, d_w_small = in_proj_bwd_w(u, dproj_main, dsmall_ssd, dsmall_gdn)
    order = [(d_w_zg, 0, COL_GATE), (d_w_conv, COL_SSD - COL_CONV, CONV_W), (d_w_small, 0, 16),
             (d_w_zg, COL_GATE, COL_CONV), (d_w_conv, 0, COL_SSD - COL_CONV), (d_w_small, 16, 32)]
    shards, pos = [[] for _ in range(N_CHIP)], 0
    for src, lo, hi in order:
        while lo < hi:
            k = pos // in_cols
            n = min(hi - lo, (k + 1) * in_cols - pos)
            shards[k].append(src[:, lo:lo + n].astype(COMM_DTYPE))
            lo, pos = lo + n, pos + n
    slabs_in = jnp.stack([jnp.concatenate(p, axis=1) for p in shards]).reshape(N_DEV, D_MODEL // 2, in_cols)
    grad_x, d_norm_w, (r_in,) = in_proj_bwd_x(xs, norm_w, w_main, w_small, dproj_main, dproj_conv, dsmall_ssd,
                                               dsmall_gdn, dhid, [slabs_in])
    d_alog, d_dtb = d_alog_s + d_alog_g, d_dtb_s + d_dtb_g
    packed, (o_nw, o_cs, o_cg, o_snw, o_fw, o_al, o_db, o_dv, o_gnw, o_loss) = _pack_cols([
        d_norm_w, dwb_ssd, dwb_gdn,
        d_ssd_nw.reshape(1, SSD_WIDTH), d_fw, d_alog, d_dtb, d_dvec, d_gdn_nw, loss_blk])

    half_in = sum_slabs(r_in, "sum_w_in")
    half_out = sum_slabs(r_out, "sum_w_out")
    (full_in, full_out), (r_small,) = exchange_halves([half_in, half_out], [packed])
    tot = sum_slabs(r_small, "sum_small")
    grad_w_in = full_in.reshape(D_MODEL, in_cols)
    grad_w_out = full_out.reshape(out_rows, D_MODEL)
    loss = tot[0, o_loss]
    sc, gc = ssd_conv_w.shape[2], gdn_conv_w.shape[2]
    row = lambda off, n, r=0: tot[r:r + 1, off:off + n]
    gs = [row(o_nw, D_MODEL),
          lax.dynamic_slice(tot, (0, o_cs + chip * sc), (4, sc)),
          row(o_cs, SSD_CONV, 4),
          row(o_db, SSD_HEADS), row(o_al, SSD_HEADS), row(o_dv, SSD_HEADS),
          row(o_snw, SSD_WIDTH),
          lax.dynamic_slice(tot, (0, o_cg + chip * gc), (4, gc)),
          row(o_db + LANE_GA, GDN_HEADS), row(o_al + LANE_GA, GDN_HEADS),
          row(o_gnw, GDN_DV), row(o_fw, D_MODEL)]

    names = ["norm_w", "ssd_conv_w", "ssd_conv_b", "ssd_dt_bias", "ssd_a_log", "ssd_d", "ssd_norm_w", "gdn_conv_w",
             "gdn_dt_bias", "gdn_a_log", "gdn_norm_w", "final_norm_w"]
    ws = [norm_w, ssd_conv_w, ssd_conv_b, ssd_dt_bias, ssd_a_log, ssd_d, ssd_norm_w, gdn_conv_w, gdn_dt_bias,
          gdn_a_log, gdn_norm_w, final_norm_w]
    ms = [m_norm_w, m_ssd_conv_w, m_ssd_conv_b, m_ssd_dt_bias, m_ssd_a_log, m_ssd_d, m_ssd_norm_w, m_gdn_conv_w,
          m_gdn_dt_bias, m_gdn_a_log, m_gdn_norm_w, m_final_norm_w]
    vs = [v_norm_w, v_ssd_conv_w, v_ssd_conv_b, v_ssd_dt_bias, v_ssd_a_log, v_ssd_d, v_ssd_norm_w, v_gdn_conv_w,
          v_gdn_dt_bias, v_gdn_a_log, v_gdn_norm_w, v_final_norm_w]
    shapes = [w.shape for w in ws]
    flat = lambda arrs: [a.reshape(g.shape) for a, g in zip(arrs, gs)]
    d_s, m_s, v_s = adamw_many(flat(ws), gs, flat(ms), flat(vs))
    back = lambda arrs: dict(zip(names, [a.reshape(s) for a, s in zip(arrs, shapes)]))
    delta, new_m, new_v, grads = back(d_s), back(m_s), back(v_s), back(gs)
    d_in, m_in, v_in = adamw(w_in_shard, grad_w_in, m_w_in[0], v_w_in[0], "adamw_w_in")
    d_out, m_out, v_out = adamw(w_out_shard, grad_w_out, m_w_out[0], v_w_out[0], "adamw_w_out")
    for tbl, a_in, a_out in ((grads, grad_w_in, grad_w_out), (delta, d_in, d_out), (new_m, m_in, m_out),
                             (new_v, v_in, v_out)):
        tbl["w_in"] = a_in[None]
        tbl["w_out"] = a_out[None]

    order = ["norm_w", "w_in", "ssd_conv_w", "ssd_conv_b", "ssd_dt_bias", "ssd_a_log", "ssd_d", "ssd_norm_w",
             "gdn_conv_w", "gdn_dt_bias", "gdn_a_log", "gdn_norm_w", "w_out", "final_norm_w"]
    return (loss.reshape(()), grad_x[None], *[grads[k] for k in order], *[delta[k] for k in order],
            *[new_m[k] for k in order], *[new_v[k] for k in order])
```

```python
import functools

import jax
import jax.numpy as jnp
from jax import lax
from jax.experimental import pallas as pl
from jax.experimental.pallas import tpu as pltpu

F32 = jnp.float32
MXU_DTYPE = jnp.bfloat16
COMM_DTYPE = jnp.bfloat16
MESH = pl.DeviceIdType.MESH

D_MODEL = 1024
CHUNK = 64
EPS = 1e-6
SSD_HEADS, SSD_GROUPS, SSD_STATE = 16, 2, 128
SSD_WIDTH, SSD_CONV = 1024, 1536
SSD_GW = SSD_WIDTH // SSD_GROUPS
SSD_GC = SSD_GW + 2 * SSD_STATE
GDN_HEADS, GDN_DK, GDN_DV = 8, 128, 128
GDN_W, GDN_CONV = 1024, 3072
GDN_HC = 2 * GDN_DK + GDN_DV
IN_DIM = 6688
MAIN = 6656
LANES = 128
COL_Z, COL_GATE, COL_GDN, COL_SSD = 0, 1024, 2048, 5120
COL_CONV = COL_GDN
CONV_W = MAIN - COL_CONV
GDN_HB = 8
GDN_CB = 2
LANE_GA, LANE_GB = 16, 24
N_DEV, N_CHIP = 8, 4
VMEM_LIMIT = 52 * 1024 * 1024

ADAM_LR, ADAM_B1, ADAM_B2, ADAM_EPS, ADAM_WD, ADAM_STEP = 0.001, 0.9, 0.999, 1e-08, 0.01, 10


def _split(a, n):
    parts, rest = [], a.astype(F32)
    for i in range(n):
        p = rest.astype(MXU_DTYPE)
        parts.append(p)
        if i < n - 1:
            rest = rest - p.astype(F32)
    return parts


def _raw_dot(a, b, ca, cb, mode="bf16"):
    d = lambda u, v: lax.dot_general(u, v, (((ca,), (cb,)), ((), ())), preferred_element_type=F32)
    if mode == "bf16":
        return d(a.astype(MXU_DTYPE), b.astype(MXU_DTYPE))
    if mode == "x3":
        (ah, al), (bh, bl) = _split(a, 2), _split(b, 2)
        return d(ah, bh) + (d(ah, bl) + d(al, bh))
    if mode == "sel_a":
        a0 = a.astype(MXU_DTYPE)
        b1, b2, b3 = _split(b, 3)
        return d(a0, b1) + (d(a0, b2) + d(a0, b3))
    assert mode == "sel_b", mode
    b0 = b.astype(MXU_DTYPE)
    a1, a2, a3 = _split(a, 3)
    return d(a1, b0) + (d(a2, b0) + d(a3, b0))


@functools.partial(jax.custom_vjp, nondiff_argnums=(2,))
def mm_nn(a, b, mode="bf16"):
    return _raw_dot(a, b, 1, 0, mode)


@functools.partial(jax.custom_vjp, nondiff_argnums=(2,))
def mm_nt(a, b, mode="bf16"):
    return _raw_dot(a, b, 1, 1, mode)


@functools.partial(jax.custom_vjp, nondiff_argnums=(2,))
def mm_tn(a, b, mode="bf16"):
    return _raw_dot(a, b, 0, 0, mode)


_SAME = {"bf16": ("bf16", "bf16"), "x3": ("x3", "x3")}
_NN_BWD = dict(_SAME, sel_a=("bf16", "sel_a"), sel_b=("sel_b", "bf16"))
_NT_BWD = dict(_SAME, sel_a=("bf16", "sel_b"), sel_b=("sel_b", "bf16"))
_TN_BWD = dict(_SAME, sel_a=("bf16", "sel_a"), sel_b=("sel_a", "bf16"))
mm_nn.defvjp(lambda a, b, m: (_raw_dot(a, b, 1, 0, m), (a, b)),
             lambda m, r, g: (mm_nt(g, r[1], _NN_BWD[m][0]), mm_tn(r[0], g, _NN_BWD[m][1])))
mm_nt.defvjp(lambda a, b, m: (_raw_dot(a, b, 1, 1, m), (a, b)),
             lambda m, r, g: (mm_nn(g, r[1], _NT_BWD[m][0]), mm_tn(g, r[0], _NT_BWD[m][1])))
mm_tn.defvjp(lambda a, b, m: (_raw_dot(a, b, 0, 0, m), (a, b)),
             lambda m, r, g: (mm_nt(r[1], g, _TN_BWD[m][0]), mm_nn(r[0], g, _TN_BWD[m][1])))


@jax.custom_jvp
def sigmoid(x):
    return 1.0 / (1.0 + jnp.exp(-x))


@sigmoid.defjvp
def _sigmoid_jvp(p, t):
    s = sigmoid(p[0])
    return s, t[0] * s * (1.0 - s)


@jax.custom_jvp
def softplus(x):
    return jnp.maximum(x, 0.0) + jnp.log(1.0 + jnp.exp(-jnp.abs(x)))


@softplus.defjvp
def _softplus_jvp(p, t):
    return softplus(p[0]), t[0] * sigmoid(p[0])


def silu(x):
    return x * sigmoid(x)


def rmsnorm(x, w):
    return x * lax.rsqrt(jnp.mean(x * x, axis=-1, keepdims=True) + EPS) * w


def _iota(shape, dim):
    return lax.broadcasted_iota(jnp.int32, shape, dim)


def _halves():
    lane = _iota((1, LANES), 1) >> 6
    return _ind(lane == 0), _ind(lane == 1)


def _block_diag(pair):
    h0, h1 = _halves()
    return jnp.concatenate([pair * h0, pair * h1], axis=0)


def _tri_inv_impl(mats):
    r, c = _iota((CHUNK, LANES), 0), _iota((CHUNK, LANES), 1) & (CHUNK - 1)
    eye = _ind(r == c)
    blockdiag = _ind((r >> 4) == (c >> 4))
    dot = lambda u, v: _raw_dot(u, _block_diag(v), 1, 0, "x3")
    dot1 = lambda u, v: _raw_dot(u, _block_diag(v), 1, 0)
    each = lambda f, *ls: [f(*xs) for xs in zip(*ls)]
    dg = each(lambda a: a * blockdiag, mats)
    off = each(lambda a, d: a - d, mats, dg)
    m = each(lambda d: -d, dg)
    p = each(lambda x: eye + x, m)
    pw = m
    for _ in range(3):
        pw = each(lambda x: dot1(x, x), pw)
        p = each(lambda x, y: x + dot1(x, y), p, pw)
    e = each(dot, p, off)
    e2 = each(lambda x: dot1(x, x), e)
    q = each(lambda x: eye - x, e)
    q = each(lambda x, y: x + dot1(x, y), q, e2)
    return each(dot, q, p)


def _tri_inv_bwd(ts, gs):
    h0, h1 = _halves()
    x = [mm_nt(g, _block_diag(t)) for g, t in zip(gs, ts)]
    full = [mm_tn(t, y) for t, y in zip(ts, x)]
    return [-(f[:CHUNK] * h0 + f[CHUNK:] * h1) for f in full]


@jax.custom_vjp
def tri_inv(mats):
    return _tri_inv_impl(mats)


def _tri_inv_fwd(mats):
    ts = _tri_inv_impl(mats)
    return ts, ts


tri_inv.defvjp(_tri_inv_fwd, lambda ts, gs: (_tri_inv_bwd(ts, gs),))


@jax.custom_vjp
def tri_inv_saved(mats, ts):
    del mats
    return ts


tri_inv_saved.defvjp(lambda mats, ts: (ts, ts),
                     lambda ts, gs: (_tri_inv_bwd(ts, gs), [jnp.zeros_like(t) for t in ts]))


def _ind(cond):
    return jnp.where(cond, 1.0, 0.0).astype(F32)


def _chunk_masks():
    r, c = _iota((CHUNK, CHUNK), 0), _iota((CHUNK, CHUNK), 1)
    return _ind(r >= c), _ind(r > c), _ind(r == c), _ind(_iota((CHUNK, 1), 0) == CHUNK - 1)


def _log_decay_cumsum(small, alog, dtb, tri):
    sp = softplus(small + dtb)
    la = -jnp.exp(alog) * sp
    return sp, mm_nn(tri, la, "sel_a")


def _col_of(x, lane):
    return jnp.sum(x * _ind(_iota((1, LANES), 1) == lane), axis=1, keepdims=True)


def _decay_matrix(col, tri, eye):
    row = jnp.sum(col * eye, axis=0, keepdims=True)
    return jnp.exp((col - row) * tri) * tri


def _pair_masks():
    r, c = _iota((CHUNK, LANES), 0), _iota((CHUNK, LANES), 1)
    c6 = c & (CHUNK - 1)
    return _ind(r >= c6), _ind(r > c6), (_ind(c == r), _ind(c == r + CHUNK))


def _decay_pair(col_a, col_b, tri_w, eye_w):
    h0, h1 = _halves()
    col = col_a * h0 + col_b * h1
    row = jnp.sum(col_a * eye_w[0] + col_b * eye_w[1], axis=0, keepdims=True)
    return jnp.exp((col - row) * tri_w) * tri_w


def gdn_chunk(h0, qs, ks, vs, smalls, gates, normw, alog, dtb, states, saved_t=None):
    tri, _, _, last = _chunk_masks()
    tri_w, strict_w, eye_w = _pair_masks()
    nh = len(qs[0])
    flat = lambda xss: [x for xs in xss for x in xs]
    lacs = [_log_decay_cumsum(sm, alog, dtb, tri)[1] for sm in smalls]
    qs, ks, vs, gates = flat(qs), flat(ks), flat(vs), flat(gates)
    heads, pairs = range(len(qs)), range(len(qs) // 2)
    each = lambda f, *ls: [f(*xs) for xs in zip(*ls)]
    ab = lambda xs, p: (xs[2 * p], xs[2 * p + 1])
    stack = lambda xs: jnp.concatenate(xs, axis=0)
    gc = [_col_of(lacs[i // nh], LANE_GA + h0 + i % nh) for i in heads]
    beta = [sigmoid(_col_of(smalls[i // nh], LANE_GB + h0 + i % nh)) for i in heads]
    decay = [_decay_pair(*ab(gc, p), tri_w, eye_w) for p in pairs]
    gl = each(lambda x: jnp.sum(x * last, axis=0, keepdims=True), gc)
    q = each(lambda x: x * lax.rsqrt(jnp.sum(x * x, axis=-1, keepdims=True) + EPS) * (GDN_DK ** -0.5), qs)
    k = each(lambda x: x * lax.rsqrt(jnp.sum(x * x, axis=-1, keepdims=True) + EPS), ks)
    kb = each(lambda x, b: x * b, k, beta)
    eg = each(jnp.exp, gc)
    zero = jnp.zeros((CHUNK, GDN_DK), F32)
    k_bd = [stack([join_lanes([k[2 * p], zero]), join_lanes([zero, k[2 * p + 1]])]) for p in pairs]
    a = [mm_nt(join_lanes(list(ab(kb, p))), k_bd[p]) * (decay[p] * strict_w) for p in pairs]
    t = tri_inv(a) if saved_t is None else tri_inv_saved(a, saved_t)
    attn = [mm_nt(join_lanes(list(ab(q, p))), k_bd[p]) * decay[p] for p in pairs]
    rhs = [stack([join_lanes([vs[h] * beta[h], kb[h] * eg[h]]) for h in (2 * p, 2 * p + 1)]) for p in pairs]
    uw = [mm_nn(_block_diag(t[p]), rhs[p]) for p in pairs]
    uw = [x for p in pairs for x in split_rows(uw[p])]
    u, w = zip(*[split_lanes(x) for x in uw])
    ys = []
    for c in range(len(smalls)):
        hs = range(c * nh, (c + 1) * nh)
        v_new = [u[i] - mm_nn(w[i], states[i % nh]) for i in hs]
        av = [mm_nn(_block_diag(attn[c * nh // 2 + p]), stack(list(ab(v_new, p)))) for p in range(nh // 2)]
        av = [x for y in av for x in split_rows(y)]
        o = [mm_nn(q[i] * eg[i], states[i % nh]) + av[i % nh] for i in hs]
        states = [states[i % nh] * jnp.exp(gl[i]) + mm_tn(k[i] * jnp.exp(gl[i] - gc[i]), v_new[i % nh]) for i in hs]
        ys.append([rmsnorm(o[i % nh], normw) * silu(gates[i]) for i in hs])
    return ys, states, t


@jax.custom_vjp
def split_rows(x):
    n = x.shape[0] // 2
    return [x[:n], x[n:]]


split_rows.defvjp(lambda x: (split_rows(x), None), lambda _, gs: (jnp.concatenate(gs, axis=0),))


@jax.custom_vjp
def split_lanes(x):
    return [x[:, i * LANES:(i + 1) * LANES] for i in range(x.shape[1] // LANES)]


@jax.custom_vjp
def join_lanes(xs):
    return jnp.concatenate(xs, axis=1)


split_lanes.defvjp(lambda x: (split_lanes(x), None), lambda _, gs: (join_lanes(gs),))
join_lanes.defvjp(lambda xs: (join_lanes(xs), None), lambda _, g: (split_lanes(g),))


def ssd_chunk(xs, bm, cm, z, small, normw, alog, dtb, dvec, state):
    tri, _, eye, last = _chunk_masks()
    hpg = SSD_HEADS // SSD_GROUPS
    groups = range(len(xs))
    each = lambda f, *ls: [f(*a) for a in zip(*ls)]
    sp, lac = _log_decay_cumsum(small, alog, dtb, tri)
    lac_last = jnp.sum(lac * last, axis=0, keepdims=True)
    sel = [_ind(_iota((LANES, SSD_GW), 0) == g * hpg + (_iota((LANES, SSD_GW), 1) >> 6)) for g in groups]
    expand = lambda v, mode="sel_b": [mm_nn(v, s, mode) for s in sel]
    dt_e, elac_e, toend_e = expand(sp, "bf16"), expand(jnp.exp(lac), "bf16"), expand(jnp.exp(lac_last - lac), "bf16")
    row8, row8e = _iota((8, 1), 0), _iota((8, 1), 0)
    two_e = expand(_ind(row8 == 0) * dvec + _ind(row8 == 1) * jnp.exp(lac_last))
    d_e = each(lambda v: jnp.sum(v * _ind(row8e == 0), axis=0, keepdims=True), two_e)
    chunk_e = each(lambda v: jnp.sum(v * _ind(row8e == 1), axis=0, keepdims=True), two_e)
    xdt = each(lambda a, b: a * b, xs, dt_e)
    y = each(lambda c_, st, el, x_, d_: mm_nn(c_, st) * el + x_ * d_, cm, state, elac_e, xs, d_e)
    r, c = _iota((CHUNK, LANES), 0), _iota((CHUNK, LANES), 1)
    tri_w = _ind(r >= (c & (CHUNK - 1)))
    eye_w = [_ind(c == r), _ind(c == r + CHUNK)]
    half = [_ind((_iota((1, LANES), 1) >> 6) == s) for s in range(2)]

    def decay_pair(col_a, col_b):
        col = col_a * half[0] + col_b * half[1]
        row = jnp.sum(col_a * eye_w[0] + col_b * eye_w[1], axis=0, keepdims=True)
        return jnp.exp((col - row) * tri_w) * tri_w

    pairs = range(hpg // 2)
    cb_w = each(lambda c_, b_: mm_nt(c_, jnp.concatenate([b_, b_], axis=0)), cm, bm)
    x_pairs = each(split_lanes, xdt)
    lms = [[decay_pair(_col_of(lac, g * hpg + 2 * p), _col_of(lac, g * hpg + 2 * p + 1)) for p in pairs]
           for g in groups]
    stacked = [[jnp.concatenate([x_pairs[g][p] * half[0], x_pairs[g][p] * half[1]], axis=0) for p in pairs]
               for g in groups]
    terms = [[mm_nn(cb_w[g] * lms[g][p], stacked[g][p]) for p in pairs] for g in groups]
    y = [y[g] + join_lanes(terms[g]) for g in groups]
    new_state = each(lambda st, ce, b_, xd, te: st * ce + mm_tn(b_, xd * te), state, chunk_e, bm, xdt, toend_e)
    out = each(lambda y_, z_, nw: rmsnorm(y_ * silu(z_), nw), y, z, normw)
    return out, new_state


def _params(sem=None):
    return pltpu.CompilerParams(dimension_semantics=sem, vmem_limit_bytes=VMEM_LIMIT)


def _full(shape):
    n = len(shape)
    return pl.BlockSpec(shape, lambda *_: (0,) * n)


ANY = pl.BlockSpec(memory_space=pl.ANY)
HBM = pl.BlockSpec(memory_space=pltpu.HBM)


def in_proj(x, normw, w_main, w_small):
    t = x.shape[0]
    tm, tn = min(1024, t), 512

    def body(x_ref, nw_ref, wm_ref, ws_ref, pm_ref, ps_ref, u_ref):
        @pl.when(pl.program_id(1) == 0)
        def _():
            u = rmsnorm(x_ref[...], nw_ref[...]).astype(MXU_DTYPE)
            u_ref[...] = u
            ps_ref[...] = _raw_dot(u, ws_ref[...], 1, 0)
        pm_ref[...] = _raw_dot(u_ref[...], wm_ref[...], 1, 0)

    return pl.pallas_call(
        body, name="in_proj", grid=(t // tm, COL_CONV // tn),
        in_specs=[pl.BlockSpec((tm, D_MODEL), lambda i, j: (i, 0)), _full((1, D_MODEL)),
                  pl.BlockSpec((D_MODEL, tn), lambda i, j: (0, j)), _full((D_MODEL, LANES))],
        out_specs=[pl.BlockSpec((tm, tn), lambda i, j: (i, j)), pl.BlockSpec((tm, LANES), lambda i, j: (i, 0)),
                   pl.BlockSpec((tm, D_MODEL), lambda i, j: (i, 0))],
        out_shape=[jax.ShapeDtypeStruct((t, COL_CONV), F32), jax.ShapeDtypeStruct((t, LANES), F32),
                   jax.ShapeDtypeStruct((t, D_MODEL), MXU_DTYPE)],
        compiler_params=_params(("arbitrary", "arbitrary")),
    )(x, normw, w_main, w_small)


CONV_TC = 512
HALO = 8


def _shift_down(cur, prev, s):
    rolled = pltpu.roll(cur, s, 0)
    top = jnp.where(_iota((HALO, cur.shape[1]), 0) < s, pltpu.roll(prev, s, 0), rolled[:HALO])
    if cur.shape[0] == HALO:
        return top
    return jnp.concatenate([top, rolled[HALO:]], axis=0)


def _shift_up(cur, nxt, s):
    n = cur.shape[0]
    rolled = pltpu.roll(cur, n - s, 0)
    bot = jnp.where(_iota((HALO, cur.shape[1]), 0) >= HALO - s, pltpu.roll(nxt, HALO - s, 0), rolled[n - HALO:])
    return jnp.concatenate([rolled[:n - HALO], bot], axis=0)


def _conv_pre(cur, prev, w_ref, b, cols=slice(None)):
    acc = cur * w_ref[3:4, cols] + b
    shifted = [cur]
    for s in (1, 2, 3):
        sh = _shift_down(cur, prev, s)
        shifted.append(sh)
        acc = acc + sh * w_ref[3 - s:4 - s, cols]
    return acc, shifted


def in_proj_conv(u, w_main, w, b):
    t = u.shape[0]
    tm, tn = min(1024, t), CONV_TC
    rc = min(256, tm)
    c0, nj = COL_CONV // tn, CONV_W // tn

    def body(u_ref, wm_ref, w_ref, b_ref, pm_ref, out_ref, halo_ref):
        j = pl.program_id(1)

        @pl.when(pl.program_id(0) == 0)
        def _():
            halo_ref[j] = jnp.zeros((HALO, tn), F32)

        prev = halo_ref[j]
        for r in range(tm // rc):
            rows = pl.ds(r * rc, rc)
            p = _raw_dot(u_ref[rows, :], wm_ref[...], 1, 0)
            pm_ref[rows, :] = p
            pre, _ = _conv_pre(p, prev, w_ref, b_ref[...])
            out_ref[rows, :] = silu(pre)
            prev = p[rc - HALO:]
        halo_ref[j] = prev

    return pl.pallas_call(
        body, name="in_proj_conv", grid=(t // tm, nj),
        in_specs=[pl.BlockSpec((tm, D_MODEL), lambda i, j: (i, 0)),
                  pl.BlockSpec((D_MODEL, tn), lambda i, j: (0, c0 + j)),
                  pl.BlockSpec((4, tn), lambda i, j: (0, j)), pl.BlockSpec((1, tn), lambda i, j: (0, j))],
        out_specs=[pl.BlockSpec((tm, tn), lambda i, j: (i, j)), pl.BlockSpec((tm, tn), lambda i, j: (i, j))],
        out_shape=[jax.ShapeDtypeStruct((t, CONV_W), F32), jax.ShapeDtypeStruct((t, CONV_W), F32)],
        scratch_shapes=[pltpu.VMEM((nj, HALO, tn), F32)],
        compiler_params=_params(("arbitrary", "arbitrary")),
    )(u, w_main, w, b)


def _dsilu(pre):
    sg = sigmoid(pre)
    return sg * (1.0 + pre * (1.0 - sg))


def conv_bwd_w(u, proj_conv, w, b, dout, slabbed):
    t = u.shape[0]
    tt, tn = min(512, t), 3 * CONV_TC
    nt, nj = t // tt, CONV_W // tn
    ns = len(slabbed)
    after = lambda i: jnp.minimum((i + 1) * (tt // HALO), t // HALO - 1)

    def body(u_ref, cur_ref, prev_ref, nxt_ref, w_ref, b_ref, do_ref, do_nxt_ref, *rest):
        slab_refs, (dx_ref, dw_ref, dwb_ref) = rest[:ns], rest[ns:ns + 3]
        land_refs, sems = rest[ns + 3:2 * ns + 3], rest[2 * ns + 3:]
        j, i = pl.program_id(0), pl.program_id(1)
        start, finish = _slab_exchange(slab_refs, land_refs, ns, *sems)

        @pl.when(jnp.logical_and(j == 0, i == 0))
        def _():
            start()

        @pl.when(i == 0)
        def _():
            dw_ref[...] = jnp.zeros(dw_ref.shape, F32)
            dwb_ref[...] = jnp.zeros(dwb_ref.shape, F32)

        uu = u_ref[...]
        row = _iota((HALO, CONV_TC), 0)
        first, last = i == 0, i == nt - 1
        for piece in range(tn // CONV_TC):
            cols = slice(piece * CONV_TC, (piece + 1) * CONV_TC)
            cur, bias = cur_ref[:, cols], b_ref[:, cols]
            prev = jnp.where(first, 0.0, prev_ref[:, cols])
            pre, shifted = _conv_pre(cur, prev, w_ref, bias, cols)
            dpre = do_ref[:, cols] * _dsilu(pre)
            pre_nxt, _ = _conv_pre(nxt_ref[:, cols], cur[tt - HALO:], w_ref, bias, cols)
            dpre_nxt = jnp.where(last, 0.0, do_nxt_ref[:, cols] * _dsilu(pre_nxt))
            dx = dpre * w_ref[3:4, cols]
            for s in (1, 2, 3):
                dx = dx + _shift_up(dpre, dpre_nxt, s) * w_ref[3 - s:4 - s, cols]
            dx = dx.astype(dx_ref.dtype)
            dx_ref[:, cols] = dx
            dw_ref[:, cols] += _raw_dot(uu, dx, 0, 0)
            upd = jnp.where(row == 4, jnp.sum(dpre, axis=0, keepdims=True), 0.0)
            for s in range(4):
                upd = upd + jnp.where(row == 3 - s, jnp.sum(dpre * shifted[s], axis=0, keepdims=True), 0.0)
            dwb_ref[:, cols] += upd

        @pl.when(jnp.logical_and(j == nj - 1, last))
        def _():
            finish()

    out = pl.pallas_call(
        body, name="conv_bwd_w", grid=(nj, nt),
        in_specs=[pl.BlockSpec((tt, D_MODEL), lambda j, i: (i, 0)),
                  pl.BlockSpec((tt, tn), lambda j, i: (i, j)),
                  pl.BlockSpec((HALO, tn), lambda j, i: (jnp.maximum(i * (tt // HALO) - 1, 0), j)),
                  pl.BlockSpec((HALO, tn), lambda j, i: (after(i), j)),
                  pl.BlockSpec((4, tn), lambda j, i: (0, j)), pl.BlockSpec((1, tn), lambda j, i: (0, j)),
                  pl.BlockSpec((tt, tn), lambda j, i: (i, j)),
                  pl.BlockSpec((HALO, tn), lambda j, i: (after(i), j))] + [HBM] * ns,
        out_specs=[pl.BlockSpec((tt, tn), lambda j, i: (i, j)), pl.BlockSpec((D_MODEL, tn), lambda j, i: (0, j)),
                   pl.BlockSpec((HALO, tn), lambda j, i: (0, j))] + [HBM] * ns,
        out_shape=[jax.ShapeDtypeStruct((t, CONV_W), MXU_DTYPE), jax.ShapeDtypeStruct((D_MODEL, CONV_W), F32),
                   jax.ShapeDtypeStruct((HALO, CONV_W), F32)] + _slab_exchange_shapes(slabbed, []),
        scratch_shapes=_slab_exchange_sems(ns),
        compiler_params=_params(("arbitrary", "arbitrary")),
    )(u, proj_conv, proj_conv, proj_conv, w, b, dout, dout, *slabbed)
    return out[0], out[1], out[2], out[3:]


def _ssd_cols(g):
    b0 = SSD_WIDTH + g * SSD_STATE
    c0 = SSD_WIDTH + SSD_GROUPS * SSD_STATE + g * SSD_STATE
    return slice(g * SSD_GW, (g + 1) * SSD_GW), slice(b0, b0 + SSD_STATE), slice(c0, c0 + SSD_STATE)


def _gdn_cols(j):
    return tuple(slice(s * GDN_W + j * GDN_DK, s * GDN_W + (j + 1) * GDN_DK) for s in range(3))


def _ssd_parts(xbc_ref):
    return tuple([xbc_ref[:, _ssd_cols(g)[s]] for g in range(SSD_GROUPS)] for s in range(3))


def _group_cols(ref):
    return [ref[:, g * SSD_GW:(g + 1) * SSD_GW] for g in range(SSD_GROUPS)]


def _chunk_rows(c):
    return slice(c * CHUNK, (c + 1) * CHUNK)


def _gdn_parts(qkv_ref):
    assert GDN_HB == GDN_HEADS, "the conv block is read whole: one grid step holds every head"
    return tuple([[qkv_ref[_chunk_rows(c), _gdn_cols(j)[s]] for j in range(GDN_HB)] for c in range(GDN_CB)]
                 for s in range(3))


def _head_cols(ref):
    return [[ref[_chunk_rows(c), j * GDN_DV:(j + 1) * GDN_DV] for j in range(GDN_HB)] for c in range(GDN_CB)]


def _chunk_blocks(ref):
    return [ref[_chunk_rows(c), :] for c in range(GDN_CB)]


def _first_head():
    return 0 if GDN_HB == GDN_HEADS else pl.program_id(1) * GDN_HB


def ssd_fwd(conv_ssd, proj_main, proj_small, normw, alog, dtb, dvec):
    t = conv_ssd.shape[0]
    nc = t // CHUNK

    groups = range(SSD_GROUPS)

    def body(xbc_ref, z_ref, sm_ref, nw_ref, al_ref, db_ref, dv_ref, y_ref, hist_ref, state_ref):
        @pl.when(pl.program_id(0) == 0)
        def _():
            state_ref[...] = jnp.zeros(state_ref.shape, F32)

        states = [state_ref[g] for g in groups]
        for g in groups:
            hist_ref[0, g] = states[g]
        ys, new_states = ssd_chunk(*_ssd_parts(xbc_ref), _group_cols(z_ref), sm_ref[...], _group_cols(nw_ref),
                                   al_ref[...], db_ref[...], dv_ref[...], states)
        for g in groups:
            y_ref[:, g * SSD_GW:(g + 1) * SSD_GW] = ys[g].astype(MXU_DTYPE)
            state_ref[g] = new_states[g]

    return pl.pallas_call(
        body, name="ssd_fwd", grid=(nc,),
        in_specs=[pl.BlockSpec((CHUNK, SSD_CONV), lambda c: (c, (COL_SSD - COL_CONV) // SSD_CONV)),
                  pl.BlockSpec((CHUNK, SSD_WIDTH), lambda c: (c, COL_Z // SSD_WIDTH)),
                  pl.BlockSpec((CHUNK, LANES), lambda c: (c, 0)),
                  _full((1, SSD_WIDTH)), _full((1, LANES)), _full((1, LANES)), _full((1, LANES))],
        out_specs=[pl.BlockSpec((CHUNK, SSD_WIDTH), lambda c: (c, 0)),
                   pl.BlockSpec((1, SSD_GROUPS, SSD_STATE, SSD_GW), lambda c: (c, 0, 0, 0))],
        out_shape=[jax.ShapeDtypeStruct((t, SSD_WIDTH), MXU_DTYPE),
                   jax.ShapeDtypeStruct((nc, SSD_GROUPS, SSD_STATE, SSD_GW), F32)],
        scratch_shapes=[pltpu.VMEM((SSD_GROUPS, SSD_STATE, SSD_GW), F32)],
        compiler_params=_params(("arbitrary",)),
    )(conv_ssd, proj_main, proj_small, normw, alog, dtb, dvec)


def _accumulate(ref, first, value):
    @pl.when(first)
    def _():
        ref[...] = value

    @pl.when(jnp.logical_not(first))
    def _():
        ref[...] += value


def ssd_bwd(conv_ssd, proj_main, proj_small, normw, alog, dtb, dvec, hist, dy):
    t = conv_ssd.shape[0]
    nc = t // CHUNK
    rev = lambda c: nc - 1 - c
    groups = range(SSD_GROUPS)

    def body(xbc_ref, z_ref, sm_ref, nw_ref, al_ref, db_ref, dv_ref, hist_ref, dy_ref,
             dxbc_ref, dz_ref, dsm_ref, dnw_ref, dal_ref, ddb_ref, ddv_ref, dstate_ref):
        first = pl.program_id(0) == 0

        @pl.when(first)
        def _():
            dstate_ref[...] = jnp.zeros(dstate_ref.shape, F32)

        _, vjp = jax.vjp(ssd_chunk, *_ssd_parts(xbc_ref), _group_cols(z_ref), sm_ref[...], _group_cols(nw_ref),
                         al_ref[...], db_ref[...], dv_ref[...], [hist_ref[0, g] for g in groups])
        dxs, dbm, dcm, dz, dsm, dnw, dal, ddb, ddv, dstate = vjp(
            (_group_cols(dy_ref), [dstate_ref[g] for g in groups]))
        for g in groups:
            xc, bc, cc = _ssd_cols(g)
            dxbc_ref[:, xc] = dxs[g]
            dxbc_ref[:, bc] = dbm[g]
            dxbc_ref[:, cc] = dcm[g]
            dz_ref[:, g * SSD_GW:(g + 1) * SSD_GW] = dz[g].astype(dz_ref.dtype)
            dstate_ref[g] = dstate[g]
        dsm_ref[...] = dsm
        _accumulate(dnw_ref, first, join_lanes(dnw))
        _accumulate(dal_ref, first, dal)
        _accumulate(ddb_ref, first, ddb)
        _accumulate(ddv_ref, first, ddv)

    return pl.pallas_call(
        body, name="ssd_bwd", grid=(nc,),
        in_specs=[pl.BlockSpec((CHUNK, SSD_CONV), lambda c: (rev(c), (COL_SSD - COL_CONV) // SSD_CONV)),
                  pl.BlockSpec((CHUNK, SSD_WIDTH), lambda c: (rev(c), COL_Z // SSD_WIDTH)),
                  pl.BlockSpec((CHUNK, LANES), lambda c: (rev(c), 0)),
                  _full((1, SSD_WIDTH)), _full((1, LANES)), _full((1, LANES)), _full((1, LANES)),
                  pl.BlockSpec((1, SSD_GROUPS, SSD_STATE, SSD_GW), lambda c: (rev(c), 0, 0, 0)),
                  pl.BlockSpec((CHUNK, SSD_WIDTH), lambda c: (rev(c), 0))],
        out_specs=[pl.BlockSpec((CHUNK, SSD_CONV), lambda c: (rev(c), (COL_SSD - COL_CONV) // SSD_CONV)),
                   pl.BlockSpec((CHUNK, SSD_WIDTH), lambda c: (rev(c), COL_Z // SSD_WIDTH)),
                   pl.BlockSpec((CHUNK, LANES), lambda c: (rev(c), 0)),
                   _full((1, SSD_WIDTH)), _full((1, LANES)), _full((1, LANES)), _full((1, LANES))],
        out_shape=[jax.ShapeDtypeStruct((t, CONV_W), F32), jax.ShapeDtypeStruct((t, COL_CONV), MXU_DTYPE),
                   jax.ShapeDtypeStruct((t, LANES), F32), jax.ShapeDtypeStruct((1, SSD_WIDTH), F32),
                   jax.ShapeDtypeStruct((1, LANES), F32), jax.ShapeDtypeStruct((1, LANES), F32),
                   jax.ShapeDtypeStruct((1, LANES), F32)],
        scratch_shapes=[pltpu.VMEM((SSD_GROUPS, SSD_STATE, SSD_GW), F32)],
        compiler_params=_params(("arbitrary",)),
    )(conv_ssd, proj_main, proj_small, normw, alog, dtb, dvec, hist, dy)


def gdn_fwd(conv_gdn, proj_main, proj_small, normw, alog, dtb):
    t = conv_gdn.shape[0]
    hb, cb = GDN_HB, GDN_CB
    rows = CHUNK * cb
    ns = t // rows
    gate_blk = COL_GATE // (GDN_DV * hb)

    def body(qkv_ref, gate_ref, sm_ref, nw_ref, al_ref, db_ref, y_ref, hist_ref, t_ref, state_ref):
        h0 = _first_head()

        @pl.when(pl.program_id(0) == 0)
        def _():
            for j in range(hb):
                state_ref[h0 + j] = jnp.zeros((GDN_DK, GDN_DV), F32)

        states = [state_ref[h0 + j] for j in range(hb)]
        for j in range(hb):
            hist_ref[0, j] = states[j]
        qs, ks, vs = _gdn_parts(qkv_ref)
        ys, new_states, ts = gdn_chunk(h0, qs, ks, vs, _chunk_blocks(sm_ref), _head_cols(gate_ref), nw_ref[...],
                                       al_ref[...], db_ref[...], states)
        for c in range(cb):
            for j in range(hb):
                y_ref[_chunk_rows(c), j * GDN_DV:(j + 1) * GDN_DV] = ys[c][j].astype(MXU_DTYPE)
        for j in range(hb):
            state_ref[h0 + j] = new_states[j]
        for p in range(cb * hb // 2):
            t_ref[0, p] = ts[p]

    return pl.pallas_call(
        body, name="gdn_fwd", grid=(ns, GDN_HEADS // hb),
        in_specs=[pl.BlockSpec((rows, GDN_HC * hb), lambda c, h: (c, h)),
                  pl.BlockSpec((rows, GDN_DV * hb), lambda c, h: (c, gate_blk + h)),
                  pl.BlockSpec((rows, LANES), lambda c, h: (c, 0)),
                  _full((1, GDN_DV)), _full((1, LANES)), _full((1, LANES))],
        out_specs=[pl.BlockSpec((rows, GDN_DV * hb), lambda c, h: (c, h)),
                   pl.BlockSpec((1, hb, GDN_DK, GDN_DV), lambda c, h: (c, h, 0, 0)),
                   pl.BlockSpec((1, cb * hb // 2, CHUNK, LANES), lambda c, h: (c, h, 0, 0))],
        out_shape=[jax.ShapeDtypeStruct((t, GDN_W), MXU_DTYPE),
                   jax.ShapeDtypeStruct((ns, GDN_HEADS, GDN_DK, GDN_DV), F32),
                   jax.ShapeDtypeStruct((ns, cb * GDN_HEADS // 2, CHUNK, LANES), F32)],
        scratch_shapes=[pltpu.VMEM((GDN_HEADS, GDN_DK, GDN_DV), F32)],
        compiler_params=_params(("arbitrary", "arbitrary")),
    )(conv_gdn, proj_main, proj_small, normw, alog, dtb)


def gdn_bwd(dproj_main, dconv, conv_gdn, proj_main, proj_small, normw, alog, dtb, hist, t_inv, dy):
    t = conv_gdn.shape[0]
    hb, cb = GDN_HB, GDN_CB
    rows = CHUNK * cb
    ns = t // rows
    rev = lambda c: ns - 1 - c
    gate_blk = COL_GATE // (GDN_DV * hb)

    def body(alias_ref, alias2_ref, qkv_ref, gate_ref, sm_ref, nw_ref, al_ref, db_ref, hist_ref, t_ref, dy_ref,
             dgate_ref, dqkv_ref, dsm_ref, dnw_ref, dal_ref, ddb_ref, dstate_ref):
        del alias_ref, alias2_ref
        c, h = pl.program_id(0), pl.program_id(1)
        h0 = _first_head()

        @pl.when(c == 0)
        def _():
            for j in range(hb):
                dstate_ref[h0 + j] = jnp.zeros((GDN_DK, GDN_DV), F32)

        saved = [t_ref[0, p] for p in range(cb * hb // 2)]

        def fn(qs, ks, vs, smalls, gates, nw, al, db, states):
            return gdn_chunk(h0, qs, ks, vs, smalls, gates, nw, al, db, states, saved)[:2]

        qs, ks, vs = _gdn_parts(qkv_ref)
        _, vjp = jax.vjp(fn, qs, ks, vs, _chunk_blocks(sm_ref), _head_cols(gate_ref), nw_ref[...], al_ref[...],
                         db_ref[...], [hist_ref[0, j] for j in range(hb)])
        dqs, dks, dvs, dsm, dgates, dnw, dal, ddb, dstates = vjp(
            (_head_cols(dy_ref), [dstate_ref[h0 + j] for j in range(hb)]))
        for k in range(cb):
            rk = _chunk_rows(k)
            for j in range(hb):
                qc, kc, vc = _gdn_cols(j)
                dqkv_ref[rk, qc] = dqs[k][j]
                dqkv_ref[rk, kc] = dks[k][j]
                dqkv_ref[rk, vc] = dvs[k][j]
                dgate_ref[rk, j * GDN_DV:(j + 1) * GDN_DV] = dgates[k][j].astype(dgate_ref.dtype)
        for j in range(hb):
            dstate_ref[h0 + j] = dstates[j]
        _accumulate(dsm_ref, h == 0, jnp.concatenate(dsm, axis=0))
        first = jnp.logical_and(c == 0, h == 0)
        _accumulate(dnw_ref, first, dnw)
        _accumulate(dal_ref, first, dal)
        _accumulate(ddb_ref, first, ddb)

    return pl.pallas_call(
        body, name="gdn_bwd", grid=(ns, GDN_HEADS // hb),
        in_specs=[ANY, ANY, pl.BlockSpec((rows, GDN_HC * hb), lambda c, h: (rev(c), h)),
                  pl.BlockSpec((rows, GDN_DV * hb), lambda c, h: (rev(c), gate_blk + h)),
                  pl.BlockSpec((rows, LANES), lambda c, h: (rev(c), 0)),
                  _full((1, GDN_DV)), _full((1, LANES)), _full((1, LANES)),
                  pl.BlockSpec((1, hb, GDN_DK, GDN_DV), lambda c, h: (rev(c), h, 0, 0)),
                  pl.BlockSpec((1, cb * hb // 2, CHUNK, LANES), lambda c, h: (rev(c), h, 0, 0)),
                  pl.BlockSpec((rows, GDN_DV * hb), lambda c, h: (rev(c), h))],
        out_specs=[pl.BlockSpec((rows, GDN_DV * hb), lambda c, h: (rev(c), gate_blk + h)),
                   pl.BlockSpec((rows, GDN_HC * hb), lambda c, h: (rev(c), h)),
                   pl.BlockSpec((rows, LANES), lambda c, h: (rev(c), 0)),
                   _full((1, GDN_DV)), _full((1, LANES)), _full((1, LANES))],
        out_shape=[jax.ShapeDtypeStruct(dproj_main.shape, dproj_main.dtype),
                   jax.ShapeDtypeStruct(dconv.shape, dconv.dtype),
                   jax.ShapeDtypeStruct((t, LANES), F32), jax.ShapeDtypeStruct((1, GDN_DV), F32),
                   jax.ShapeDtypeStruct((1, LANES), F32), jax.ShapeDtypeStruct((1, LANES), F32)],
        scratch_shapes=[pltpu.VMEM((GDN_HEADS, GDN_DK, GDN_DV), F32)],
        input_output_aliases={0: 0, 1: 1},
        compiler_params=_params(("arbitrary", "arbitrary")),
    )(dproj_main, dconv, conv_gdn, proj_main, proj_small, normw, alog, dtb, hist, t_inv, dy)


def out_proj_loss(x, y_ssd, y_gdn, w_out, final_w, target):
    t = x.shape[0]
    tm = min(256, t)

    def body(x_ref, ys_ref, yg_ref, wo_ref, fw_ref, tg_ref, loss_ref, dhid_ref, dys_ref, dyg_ref, dwo_ref, dfw_ref):
        i = pl.program_id(0)
        ys, yg = ys_ref[...], yg_ref[...]
        wo_s, wo_g = wo_ref[:SSD_WIDTH, :], wo_ref[SSD_WIDTH:, :]
        hid = x_ref[...] + _raw_dot(ys, wo_s, 1, 0) + _raw_dot(yg, wo_g, 1, 0)
        out, vjp = jax.vjp(rmsnorm, hid, fw_ref[...])
        err = out - tg_ref[...]
        loss = 0.5 * jnp.sum(jnp.mean(err * err, axis=-1, keepdims=True), axis=0, keepdims=True)
        dhid, dfw = vjp(err * (1.0 / D_MODEL))
        dhid_ref[...] = dhid
        dys_ref[...] = _raw_dot(dhid, wo_s, 1, 1)
        dyg_ref[...] = _raw_dot(dhid, wo_g, 1, 1)
        first = i == 0
        _accumulate(loss_ref, first, jnp.broadcast_to(loss, loss_ref.shape))
        _accumulate(dfw_ref, first, dfw)

        @pl.when(first)
        def _():
            dwo_ref[:SSD_WIDTH, :] = _raw_dot(ys, dhid, 0, 0)
            dwo_ref[SSD_WIDTH:, :] = _raw_dot(yg, dhid, 0, 0)

        @pl.when(i > 0)
        def _():
            dwo_ref[:SSD_WIDTH, :] += _raw_dot(ys, dhid, 0, 0)
            dwo_ref[SSD_WIDTH:, :] += _raw_dot(yg, dhid, 0, 0)

    row = lambda w: pl.BlockSpec((tm, w), lambda i: (i, 0))
    return pl.pallas_call(
        body, name="out_proj_loss", grid=(t // tm,),
        in_specs=[row(D_MODEL), row(SSD_WIDTH), row(GDN_W), _full((SSD_WIDTH + GDN_W, D_MODEL)), _full((1, D_MODEL)),
                  row(D_MODEL)],
        out_specs=[_full((8, LANES)), row(D_MODEL), row(SSD_WIDTH), row(GDN_W), _full((SSD_WIDTH + GDN_W, D_MODEL)),
                   _full((1, D_MODEL))],
        out_shape=[jax.ShapeDtypeStruct((8, LANES), F32), jax.ShapeDtypeStruct((t, D_MODEL), F32),
                   jax.ShapeDtypeStruct((t, SSD_WIDTH), F32), jax.ShapeDtypeStruct((t, GDN_W), F32),
                   jax.ShapeDtypeStruct((SSD_WIDTH + GDN_W, D_MODEL), F32), jax.ShapeDtypeStruct((1, D_MODEL), F32)],
        compiler_params=_params(("arbitrary",)),
    )(x, y_ssd, y_gdn, w_out, final_w, target)


def in_proj_bwd_x(x, normw, w_main, w_small, dproj_main, dproj_conv, dsmall_a, dsmall_b, dhid, slabbed):
    t = x.shape[0]
    tm = min(256, t)
    ni = t // tm
    ns = len(slabbed)

    def body(x_ref, nw_ref, wm_ref, ws_ref, dp_ref, dc_ref, da_ref, db_ref, dh_ref, *rest):
        slab_refs, (gx_ref, dnw_ref), land_refs = rest[:ns], rest[ns:ns + 2], rest[ns + 2:2 * ns + 2]
        sems = rest[2 * ns + 2:]
        i = pl.program_id(0)
        start, finish = _slab_exchange(slab_refs, land_refs, ns, *sems)

        @pl.when(i == 0)
        def _():
            start()

        du = (_raw_dot(dp_ref[...], wm_ref[:, :COL_CONV], 1, 1) + _raw_dot(dc_ref[...], wm_ref[:, COL_CONV:], 1, 1)
              + _raw_dot(da_ref[...] + db_ref[...], ws_ref[...], 1, 1))
        _, vjp = jax.vjp(rmsnorm, x_ref[...], nw_ref[...])
        dx, dnw = vjp(du)
        gx_ref[...] = dx + dh_ref[...]
        _accumulate(dnw_ref, i == 0, dnw)

        @pl.when(i == ni - 1)
        def _():
            finish()

    row = lambda w: pl.BlockSpec((tm, w), lambda i: (i, 0))
    out = pl.pallas_call(
        body, name="in_proj_bwd_x", grid=(ni,),
        in_specs=[row(D_MODEL), _full((1, D_MODEL)), _full((D_MODEL, MAIN)), _full((D_MODEL, LANES)), row(COL_CONV),
                  row(CONV_W), row(LANES), row(LANES), row(D_MODEL)] + [HBM] * ns,
        out_specs=[row(D_MODEL), _full((1, D_MODEL))] + [HBM] * ns,
        out_shape=[jax.ShapeDtypeStruct((t, D_MODEL), F32), jax.ShapeDtypeStruct((1, D_MODEL), F32)]
        + _slab_exchange_shapes(slabbed, []),
        scratch_shapes=_slab_exchange_sems(ns),
        compiler_params=_params(("arbitrary",)),
    )(x, normw, w_main, w_small, dproj_main, dproj_conv, dsmall_a, dsmall_b, dhid, *slabbed)
    return out[0], out[1], out[2:]


def in_proj_bwd_w(u, dproj_main, dsmall_a, dsmall_b):
    t = u.shape[0]
    tm, tn = min(1024, t), COL_CONV // 2

    def body(u_ref, dp_ref, da_ref, db_ref, dwm_ref, dws_ref):
        j, i = pl.program_id(0), pl.program_id(1)
        uu = u_ref[...]
        _accumulate(dwm_ref, i == 0, _raw_dot(uu, dp_ref[...], 0, 0))

        @pl.when(j == 0)
        def _():
            _accumulate(dws_ref, i == 0, _raw_dot(uu, da_ref[...] + db_ref[...], 0, 0))

    return pl.pallas_call(
        body, name="in_proj_bwd_w", grid=(COL_CONV // tn, t // tm),
        in_specs=[pl.BlockSpec((tm, D_MODEL), lambda j, i: (i, 0)), pl.BlockSpec((tm, tn), lambda j, i: (i, j)),
                  pl.BlockSpec((tm, LANES), lambda j, i: (i, 0)), pl.BlockSpec((tm, LANES), lambda j, i: (i, 0))],
        out_specs=[pl.BlockSpec((D_MODEL, tn), lambda j, i: (0, j)), _full((D_MODEL, LANES))],
        out_shape=[jax.ShapeDtypeStruct((D_MODEL, COL_CONV), F32), jax.ShapeDtypeStruct((D_MODEL, LANES), F32)],
        compiler_params=_params(("arbitrary", "arbitrary")),
    )(u, dproj_main, dsmall_a, dsmall_b)


def sum_slabs(a, name):
    n, rows, cols = a.shape
    tr = 64 if rows % 64 == 0 else rows

    def body(a_ref, o_ref):
        acc = a_ref[0].astype(F32)
        for d in range(1, n):
            acc = acc + a_ref[d].astype(F32)
        o_ref[...] = acc

    return pl.pallas_call(
        body, name=name, grid=(rows // tr,),
        in_specs=[pl.BlockSpec((n, tr, cols), lambda i: (0, i, 0))],
        out_specs=pl.BlockSpec((tr, cols), lambda i: (i, 0)),
        out_shape=jax.ShapeDtypeStruct((rows, cols), F32),
        compiler_params=_params(("arbitrary",)),
    )(a)


def adamw(w, g, m, v, name):
    _, rows, cols = w.shape
    tr = 128 if rows % 128 == 0 else rows

    def body(w_ref, g_ref, m_ref, v_ref, d_ref, nm_ref, nv_ref):
        gg = g_ref[...]
        nm = ADAM_B1 * m_ref[...] + (1.0 - ADAM_B1) * gg
        nv = ADAM_B2 * v_ref[...] + (1.0 - ADAM_B2) * (gg * gg)
        m_hat = nm / (1.0 - ADAM_B1 ** ADAM_STEP)
        v_hat = nv / (1.0 - ADAM_B2 ** ADAM_STEP)
        d_ref[...] = -ADAM_LR * (m_hat / (jnp.sqrt(v_hat) + ADAM_EPS) + ADAM_WD * w_ref[...])
        nm_ref[...] = nm
        nv_ref[...] = nv

    spec = pl.BlockSpec((1, tr, cols), lambda i: (0, i, 0))
    shp = jax.ShapeDtypeStruct((1, rows, cols), F32)
    return pl.pallas_call(
        body, name=name, grid=(rows // tr,), in_specs=[spec] * 4, out_specs=[spec] * 3, out_shape=[shp] * 3,
        compiler_params=_params(("arbitrary",)),
    )(w, g.reshape(w.shape), m, v)


def _my_place():
    return lax.axis_index("x"), lax.axis_index("y"), lax.axis_index("c")


def gather_weights(big, small):
    nb, n = len(big), len(big) + len(small)
    parts = 4

    def body(*refs):
        srcs, outs = refs[:n], refs[n:2 * n]
        land_a, land_b = refs[2 * n:2 * n + nb], refs[2 * n + nb:2 * n + 2 * nb]
        send_sems, recv_sems, fwd_send, fwd_recv, local_sems = refs[2 * n + 2 * nb:]
        x, y, c = _my_place()
        me = 2 * x + y
        chips = [(1 - x, y), (x, 1 - y), (1 - x, 1 - y)]
        half = [a.shape[0] // 2 for a in big]

        def ici(j, i):
            px, py = chips[j]
            if i < nb:
                src, dst = srcs[i].at[pl.ds(c * half[i], half[i])], land_a[i].at[j]
            else:
                src, dst = srcs[i], outs[i].at[me]
            return pltpu.make_async_remote_copy(src_ref=src, dst_ref=dst, send_sem=send_sems.at[j * n + i],
                                                recv_sem=recv_sems.at[j * n + i], device_id=(px, py, c),
                                                device_id_type=MESH)

        def ici_arrival(j, i):
            px, py = chips[j]
            dst = land_a[i].at[j] if i < nb else outs[i].at[2 * px + py]
            return pltpu.make_async_remote_copy(src_ref=dst, dst_ref=dst, send_sem=send_sems.at[j * n + i],
                                                recv_sem=recv_sems.at[j * n + i], device_id=(px, py, c),
                                                device_id_type=MESH)

        def forward(j, i, p):
            rows = half[i] // parts
            k = (j * nb + i) * parts + p
            return pltpu.make_async_remote_copy(
                src_ref=land_a[i].at[j, pl.ds(p * rows, rows)], dst_ref=land_b[i].at[j, pl.ds(p * rows, rows)],
                send_sem=fwd_send.at[k], recv_sem=fwd_recv.at[k], device_id=(x, y, 1 - c), device_id_type=MESH)

        def store(j, i, from_sibling):
            px, py = chips[j]
            buf, h = (land_b, 1 - c) if from_sibling else (land_a, c)
            k = n + (j * nb + i) * 2 + (1 if from_sibling else 0)
            return pltpu.make_async_copy(buf[i].at[j], outs[i].at[2 * px + py, pl.ds(h * half[i], half[i])],
                                         local_sems.at[k])

        own = [pltpu.make_async_copy(srcs[i], outs[i].at[me], local_sems.at[i]) for i in range(n)]
        sends = [ici(j, i) for j in range(3) for i in range(n)]
        for cp in own + sends:
            cp.start()
        pending = []
        for j in range(3):
            for i in range(n):
                ici_arrival(j, i).wait_recv()
                if i < nb:
                    fw = [forward(j, i, p) for p in range(parts)]
                    st = store(j, i, False)
                    for cp in fw + [st]:
                        cp.start()
                    pending += [cp.wait_send for cp in fw] + [st.wait]
        for j in range(3):
            for i in range(nb):
                for p in range(parts):
                    forward(j, i, p).wait_recv()
                st = store(j, i, True)
                st.start()
                pending.append(st.wait)
        for cp in sends:
            cp.wait_send()
        for wait in pending:
            wait()
        for cp in own:
            cp.wait()

    shards = list(big) + list(small)
    lands = [pltpu.VMEM((3, a.shape[0] // 2) + a.shape[1:], a.dtype) for a in big]
    return pl.pallas_call(
        body, name="gather_weights",
        in_specs=[HBM] * n, out_specs=[HBM] * n,
        out_shape=[jax.ShapeDtypeStruct((N_CHIP,) + s.shape, s.dtype) for s in shards],
        scratch_shapes=lands + lands + [
            pltpu.SemaphoreType.DMA((3 * n,)), pltpu.SemaphoreType.DMA((3 * n,)),
            pltpu.SemaphoreType.DMA((3 * nb * parts,)), pltpu.SemaphoreType.DMA((3 * nb * parts,)),
            pltpu.SemaphoreType.DMA((n + 6 * nb,))],
        compiler_params=pltpu.CompilerParams(vmem_limit_bytes=VMEM_LIMIT),
    )(*shards)


def _peer(x, y, c, mask):
    mx, my, mc = (mask >> 2) & 1, (mask >> 1) & 1, mask & 1
    return (x ^ mx if mx else x, y ^ my if my else y, c ^ mc if mc else c)


def _slab_exchange_shapes(slabbed, replicated):
    return ([jax.ShapeDtypeStruct(a.shape, a.dtype) for a in slabbed]
            + [jax.ShapeDtypeStruct((N_DEV,) + a.shape, a.dtype) for a in replicated])


def _slab_exchange_sems(n):
    return [pltpu.SemaphoreType.DMA((7 * n,)), pltpu.SemaphoreType.DMA((7 * n,)), pltpu.SemaphoreType.DMA((n,))]


def _slab_exchange(srcs, outs, ns, send_sems, recv_sems, local_sems):
    n = len(srcs)
    x, y, c = _my_place()
    me = 4 * x + 2 * y + c

    def piece(i, dev):
        return srcs[i].at[dev] if i < ns else srcs[i]

    def copies(arriving):
        out = []
        for mask in range(1, N_DEV):
            px, py, pc = _peer(x, y, c, mask)
            dev = 4 * px + 2 * py + pc
            for i in range(n):
                k = (mask - 1) * n + i
                out.append(pltpu.make_async_remote_copy(
                    src_ref=piece(i, dev), dst_ref=outs[i].at[dev if arriving else me], send_sem=send_sems.at[k],
                    recv_sem=recv_sems.at[k], device_id=(px, py, pc), device_id_type=MESH))
        return out

    def local():
        return [pltpu.make_async_copy(piece(i, me), outs[i].at[me], local_sems.at[i]) for i in range(n)]

    def start():
        for cp in local() + copies(False):
            cp.start()

    def finish():
        for cp in copies(True):
            cp.wait_recv()
        for cp in copies(False):
            cp.wait_send()
        for cp in local():
            cp.wait()

    return start, finish


def exchange_halves(halves, replicated):
    n, nr = len(halves), len(replicated)
    streams = 8

    def body(*refs):
        srcs, rep_srcs, outs, rep_outs = refs[:n], refs[n:n + nr], refs[n + nr:2 * n + nr], refs[2 * n + nr:2 * (n + nr)]
        refs = refs[2 * (n + nr):]
        mine, theirs = refs[:n], refs[n:2 * n]
        send_sems, recv_sems, in_sems, out_sems = refs[2 * n:2 * n + 4]
        rep_start, rep_finish = _slab_exchange(rep_srcs, rep_outs, 0, *refs[2 * n + 4:])
        rep_start()
        x, y, c = _my_place()
        loads = [pltpu.make_async_copy(srcs[i], mine[i], in_sems.at[i]) for i in range(n)]
        for cp in loads:
            cp.start()
        for cp in loads:
            cp.wait()

        def chunk_copy(i, s):
            rows = halves[i].shape[0] // streams
            k = i * streams + s
            return pltpu.make_async_remote_copy(
                src_ref=mine[i].at[pl.ds(s * rows, rows)], dst_ref=theirs[i].at[pl.ds(s * rows, rows)],
                send_sem=send_sems.at[k], recv_sem=recv_sems.at[k], device_id=(x, y, 1 - c), device_id_type=MESH)

        sends = [chunk_copy(i, s) for i in range(n) for s in range(streams)]
        for cp in sends:
            cp.start()
        own = [pltpu.make_async_copy(mine[i], outs[i].at[c], out_sems.at[i]) for i in range(n)]
        for cp in own:
            cp.start()
        for cp in sends:
            cp.wait_recv()
        got = [pltpu.make_async_copy(theirs[i], outs[i].at[1 - c], out_sems.at[n + i]) for i in range(n)]
        for cp in got:
            cp.start()
        for cp in sends:
            cp.wait_send()
        for cp in own + got:
            cp.wait()
        rep_finish()

    vmem = [pltpu.VMEM(a.shape, a.dtype) for a in halves]
    out = pl.pallas_call(
        body, name="exchange_halves",
        in_specs=[HBM] * (n + nr), out_specs=[HBM] * (n + nr),
        out_shape=[jax.ShapeDtypeStruct((2,) + a.shape, a.dtype) for a in halves]
        + _slab_exchange_shapes([], replicated),
        scratch_shapes=vmem + vmem + [pltpu.SemaphoreType.DMA((n * streams,)), pltpu.SemaphoreType.DMA((n * streams,)),
                                      pltpu.SemaphoreType.DMA((n,)), pltpu.SemaphoreType.DMA((2 * n,))]
        + _slab_exchange_sems(nr),
        compiler_params=pltpu.CompilerParams(vmem_limit_bytes=VMEM_LIMIT),
    )(*halves, *replicated)
    return out[:n], out[n:]


def _pack_cols(pieces):
    offs, pos = [], 0
    for a in pieces:
        offs.append(pos)
        pos += a.shape[1]
    rows8 = [jnp.pad(a.astype(F32), ((0, 8 - a.shape[0]), (0, 0))) for a in pieces]
    return jnp.concatenate(rows8, axis=1), offs


def adamw_many(ws, gs, ms, vs):
    n = len(ws)

    def body(*refs):
        w_r, g_r, m_r, v_r = refs[:n], refs[n:2 * n], refs[2 * n:3 * n], refs[3 * n:4 * n]
        d_o, m_o, v_o = refs[4 * n:5 * n], refs[5 * n:6 * n], refs[6 * n:7 * n]
        for i in range(n):
            gg = g_r[i][...]
            nm = ADAM_B1 * m_r[i][...] + (1.0 - ADAM_B1) * gg
            nv = ADAM_B2 * v_r[i][...] + (1.0 - ADAM_B2) * (gg * gg)
            m_hat = nm / (1.0 - ADAM_B1 ** ADAM_STEP)
            v_hat = nv / (1.0 - ADAM_B2 ** ADAM_STEP)
            d_o[i][...] = -ADAM_LR * (m_hat / (jnp.sqrt(v_hat) + ADAM_EPS) + ADAM_WD * w_r[i][...])
            m_o[i][...] = nm
            v_o[i][...] = nv

    shapes = [jax.ShapeDtypeStruct(w.shape, F32) for w in ws]
    out = pl.pallas_call(body, name="adamw_small", out_shape=shapes * 3,
                         compiler_params=pltpu.CompilerParams(vmem_limit_bytes=VMEM_LIMIT))(*ws, *gs, *ms, *vs)
    return out[:n], out[n:2 * n], out[2 * n:]


def _lanes(vec, start):
    n = vec.shape[-1]
    return jnp.pad(vec.reshape(1, n).astype(F32), ((0, 0), (start, LANES - start - n)))


def kernel(x, norm_w, w_in, ssd_conv_w, ssd_conv_b, ssd_dt_bias, ssd_a_log, ssd_d, ssd_norm_w, gdn_conv_w, gdn_dt_bias, gdn_a_log, gdn_norm_w, w_out, final_norm_w, loss_target, m_norm_w, m_w_in, m_ssd_conv_w, m_ssd_conv_b, m_ssd_dt_bias, m_ssd_a_log, m_ssd_d, m_ssd_norm_w, m_gdn_conv_w, m_gdn_dt_bias, m_gdn_a_log, m_gdn_norm_w, m_w_out, m_final_norm_w, v_norm_w, v_w_in, v_ssd_conv_w, v_ssd_conv_b, v_ssd_dt_bias, v_ssd_a_log, v_ssd_d, v_ssd_norm_w, v_gdn_conv_w, v_gdn_dt_bias, v_gdn_a_log, v_gdn_norm_w, v_w_out, v_final_norm_w):
    xs = x[0]
    target = loss_target[0]
    chip = 2 * lax.axis_index("x") + lax.axis_index("y")
    w_in_shard, w_out_shard = w_in[0], w_out[0]
    in_cols = w_in_shard.shape[1]
    out_rows = w_out_shard.shape[0]

    g_in, g_out, g_cs, g_cg = gather_weights(
        [w_in_shard.astype(MXU_DTYPE), w_out_shard.astype(MXU_DTYPE)], [ssd_conv_w[0], gdn_conv_w[0]])
    w_in_full = jnp.concatenate([g_in[k] for k in range(N_CHIP)], axis=1)
    w_out_full = g_out.reshape(N_CHIP * out_rows, D_MODEL)
    cw_ssd = jnp.concatenate([g_cs[k] for k in range(N_CHIP)], axis=1)
    cw_gdn = jnp.concatenate([g_cg[k] for k in range(N_CHIP)], axis=1)
    cb_ssd, cb_gdn = ssd_conv_b, jnp.zeros((1, GDN_CONV), F32)
    o_xbc, o_dt, o_gate, o_qkv, o_ab = 1024, 2560, 2576, 3600, 6672
    w_main = jnp.concatenate([w_in_full[:, :o_xbc], w_in_full[:, o_gate:o_qkv], w_in_full[:, o_qkv:o_ab],
                              w_in_full[:, o_xbc:o_dt]], axis=1)
    w_small = jnp.concatenate([w_in_full[:, o_dt:o_gate], w_in_full[:, o_ab:],
                               jnp.zeros((D_MODEL, LANES - 32), MXU_DTYPE)], axis=1)
    alog = _lanes(ssd_a_log, 0) + _lanes(gdn_a_log, LANE_GA)
    dtb = _lanes(ssd_dt_bias, 0) + _lanes(gdn_dt_bias, LANE_GA)
    dvec = _lanes(ssd_d, 0)
    fw = final_norm_w.reshape(1, D_MODEL)

    cw, cb = jnp.concatenate([cw_gdn, cw_ssd], axis=1), jnp.concatenate([cb_gdn, cb_ssd], axis=1)
    proj_main, proj_small, u = in_proj(xs, norm_w, w_main, w_small)
    proj_conv, conv_out = in_proj_conv(u, w_main, cw, cb)
    y_ssd, hist_ssd = ssd_fwd(conv_out, proj_main, proj_small, ssd_norm_w, alog, dtb, dvec)
    y_gdn, hist_gdn, tinv_gdn = gdn_fwd(conv_out, proj_main, proj_small, gdn_norm_w, alog, dtb)

    loss_blk, dhid, dy_ssd, dy_gdn, d_w_out, d_fw = out_proj_loss(xs, y_ssd, y_gdn, w_out_full, fw, target)
    dconv, dproj_main, dsmall_ssd, d_ssd_nw, d_alog_s, d_dtb_s, d_dvec = ssd_bwd(
        conv_out, proj_main, proj_small, ssd_norm_w, alog, dtb, dvec, hist_ssd, dy_ssd)
    dproj_main, dconv, dsmall_gdn, d_gdn_nw, d_alog_g, d_dtb_g = gdn_bwd(
        dproj_main, dconv, conv_out, proj_main, proj_small, gdn_norm_w, alog, dtb, hist_gdn, tinv_gdn, dy_gdn)
    slabs_out = d_w_out.reshape(N_DEV, out_rows // 2, D_MODEL).astype(COMM_DTYPE)
    dproj_conv, d_w_conv, dwb, (r_out,) = conv_bwd_w(u, proj_conv, cw, cb, dconv, [slabs_out])
    dwb_gdn, dwb_ssd = dwb[:, :GDN_CONV], dwb[:, GDN_CONV:]
    d_w_zg, d_w_small = in_proj_bwd_w(u, dproj_main, dsmall_ssd, dsmall_gdn)
    order = [(d_w_zg, 0, COL_GATE), (d_w_conv, COL_SSD - COL_CONV, CONV_W), (d_w_small, 0, 16),
             (d_w_zg, COL_GATE, COL_CONV), (d_w_conv, 0, COL_SSD - COL_CONV), (d_w_small, 16, 32)]
    shards, pos = [[] for _ in range(N_CHIP)], 0
    for src, lo, hi in order:
        while lo < hi:
            k = pos // in_cols
            n = min(hi - lo, (k + 1) * in_cols - pos)
            shards[k].append(src[:, lo:lo + n].astype(COMM_DTYPE))
            lo, pos = lo + n, pos + n
    slabs_in = jnp.stack([jnp.concatenate(p, axis=1) for p in shards]).reshape(N_DEV, D_MODEL // 2, in_cols)
    grad_x, d_norm_w, (r_in,) = in_proj_bwd_x(xs, norm_w, w_main, w_small, dproj_main, dproj_conv, dsmall_ssd,
                                               dsmall_gdn, dhid, [slabs_in])
    d_alog, d_dtb = d_alog_s + d_alog_g, d_dtb_s + d_dtb_g
    packed, (o_nw, o_cs, o_cg, o_snw, o_fw, o_al, o_db, o_dv, o_gnw, o_loss) = _pack_cols([
        d_norm_w, dwb_ssd, dwb_gdn,
        d_ssd_nw.reshape(1, SSD_WIDTH), d_fw, d_alog, d_dtb, d_dvec, d_gdn_nw, loss_blk])

    half_in = sum_slabs(r_in, "sum_w_in")
    half_out = sum_slabs(r_out, "sum_w_out")
    (full_in, full_out), (r_small,) = exchange_halves([half_in, half_out], [packed])
    tot = sum_slabs(r_small, "sum_small")
    grad_w_in = full_in.reshape(D_MODEL, in_cols)
    grad_w_out = full_out.reshape(out_rows, D_MODEL)
    loss = tot[0, o_loss]
    sc, gc = ssd_conv_w.shape[2], gdn_conv_w.shape[2]
    row = lambda off, n, r=0: tot[r:r + 1, off:off + n]
    gs = [row(o_nw, D_MODEL),
          lax.dynamic_slice(tot, (0, o_cs + chip * sc), (4, sc)),
          row(o_cs, SSD_CONV, 4),
          row(o_db, SSD_HEADS), row(o_al, SSD_HEADS), row(o_dv, SSD_HEADS),
          row(o_snw, SSD_WIDTH),
          lax.dynamic_slice(tot, (0, o_cg + chip * gc), (4, gc)),
          row(o_db + LANE_GA, GDN_HEADS), row(o_al + LANE_GA, GDN_HEADS),
          row(o_gnw, GDN_DV), row(o_fw, D_MODEL)]

    names = ["norm_w", "ssd_conv_w", "ssd_conv_b", "ssd_dt_bias", "ssd_a_log", "ssd_d", "ssd_norm_w", "gdn_conv_w",
             "gdn_dt_bias", "gdn_a_log", "gdn_norm_w", "final_norm_w"]
    ws = [norm_w, ssd_conv_w, ssd_conv_b, ssd_dt_bias, ssd_a_log, ssd_d, ssd_norm_w, gdn_conv_w, gdn_dt_bias,
          gdn_a_log, gdn_norm_w, final_norm_w]
    ms = [m_norm_w, m_ssd_conv_w, m_ssd_conv_b, m_ssd_dt_bias, m_ssd_a_log, m_ssd_d, m_ssd_norm_w, m_gdn_conv_w,
          m_gdn_dt_bias, m_gdn_a_log, m_gdn_norm_w, m_final_norm_w]
    vs = [v_norm_w, v_ssd_conv_w, v_ssd_conv_b, v_ssd_dt_bias, v_ssd_a_log, v_ssd_d, v_ssd_norm_w, v_gdn_conv_w,
          v_gdn_dt_bias, v_gdn_a_log, v_gdn_norm_w, v_final_norm_w]
    shapes = [w.shape for w in ws]
    flat = lambda arrs: [a.reshape(g.shape) for a, g in zip(arrs, gs)]
    d_s, m_s, v_s = adamw_many(flat(ws), gs, flat(ms), flat(vs))
    back = lambda arrs: dict(zip(names, [a.reshape(s) for a, s in zip(arrs, shapes)]))
    delta, new_m, new_v, grads = back(d_s), back(m_s), back(v_s), back(gs)
    d_in, m_in, v_in = adamw(w_in, grad_w_in, m_w_in, v_w_in, "adamw_w_in")
    d_out, m_out, v_out = adamw(w_out, grad_w_out, m_w_out, v_w_out, "adamw_w_out")
    for tbl, a_in, a_out in ((grads, grad_w_in[None], grad_w_out[None]), (delta, d_in, d_out), (new_m, m_in, m_out),
                             (new_v, v_in, v_out)):
        tbl["w_in"] = a_in
        tbl["w_out"] = a_out

    order = ["norm_w", "w_in", "ssd_conv_w", "ssd_conv_b", "ssd_dt_bias", "ssd_a_log", "ssd_d", "ssd_norm_w",
             "gdn_conv_w", "gdn_dt_bias", "gdn_a_log", "gdn_norm_w", "w_out", "final_norm_w"]
    return (loss.reshape(()), grad_x[None], *[grads[k] for k in order], *[delta[k] for k in order],
            *[new_m[k] for k in order], *[new_v[k] for k in order])
```

```python
import functools

import jax
import jax.numpy as jnp
from jax import lax
from jax.experimental import pallas as pl
from jax.experimental.pallas import tpu as pltpu

F32 = jnp.float32
MXU_DTYPE = jnp.bfloat16
COMM_DTYPE = jnp.bfloat16
MESH = pl.DeviceIdType.MESH

D_MODEL = 1024
CHUNK = 64
EPS = 1e-6
SSD_HEADS, SSD_GROUPS, SSD_STATE = 16, 2, 128
SSD_WIDTH, SSD_CONV = 1024, 1536
SSD_GW = SSD_WIDTH // SSD_GROUPS
SSD_GC = SSD_GW + 2 * SSD_STATE
GDN_HEADS, GDN_DK, GDN_DV = 8, 128, 128
GDN_W, GDN_CONV = 1024, 3072
GDN_HC = 2 * GDN_DK + GDN_DV
IN_DIM = 6688
MAIN = 6656
LANES = 128
COL_Z, COL_GATE, COL_GDN, COL_SSD = 0, 1024, 2048, 5120
COL_CONV = COL_GDN
CONV_W = MAIN - COL_CONV
GDN_HB = 8
GDN_CB = 2
SSD_CB = 2
LANE_GA, LANE_GB = 16, 24
N_DEV, N_CHIP = 8, 4
VMEM_LIMIT = 52 * 1024 * 1024

ADAM_LR, ADAM_B1, ADAM_B2, ADAM_EPS, ADAM_WD, ADAM_STEP = 0.001, 0.9, 0.999, 1e-08, 0.01, 10


def _split(a, n):
    parts, rest = [], a.astype(F32)
    for i in range(n):
        p = rest.astype(MXU_DTYPE)
        parts.append(p)
        if i < n - 1:
            rest = rest - p.astype(F32)
    return parts


def _raw_dot(a, b, ca, cb, mode="bf16"):
    d = lambda u, v: lax.dot_general(u, v, (((ca,), (cb,)), ((), ())), preferred_element_type=F32)
    if mode == "bf16":
        return d(a.astype(MXU_DTYPE), b.astype(MXU_DTYPE))
    if mode == "x3":
        (ah, al), (bh, bl) = _split(a, 2), _split(b, 2)
        return d(ah, bh) + (d(ah, bl) + d(al, bh))
    if mode == "sel_a":
        a0 = a.astype(MXU_DTYPE)
        b1, b2, b3 = _split(b, 3)
        return d(a0, b1) + (d(a0, b2) + d(a0, b3))
    assert mode == "sel_b", mode
    b0 = b.astype(MXU_DTYPE)
    a1, a2, a3 = _split(a, 3)
    return d(a1, b0) + (d(a2, b0) + d(a3, b0))


@functools.partial(jax.custom_vjp, nondiff_argnums=(2,))
def mm_nn(a, b, mode="bf16"):
    return _raw_dot(a, b, 1, 0, mode)


@functools.partial(jax.custom_vjp, nondiff_argnums=(2,))
def mm_nt(a, b, mode="bf16"):
    return _raw_dot(a, b, 1, 1, mode)


@functools.partial(jax.custom_vjp, nondiff_argnums=(2,))
def mm_tn(a, b, mode="bf16"):
    return _raw_dot(a, b, 0, 0, mode)


_SAME = {"bf16": ("bf16", "bf16"), "x3": ("x3", "x3")}
_NN_BWD = dict(_SAME, sel_a=("bf16", "sel_a"), sel_b=("sel_b", "bf16"))
_NT_BWD = dict(_SAME, sel_a=("bf16", "sel_b"), sel_b=("sel_b", "bf16"))
_TN_BWD = dict(_SAME, sel_a=("bf16", "sel_a"), sel_b=("sel_a", "bf16"))
mm_nn.defvjp(lambda a, b, m: (_raw_dot(a, b, 1, 0, m), (a, b)),
             lambda m, r, g: (mm_nt(g, r[1], _NN_BWD[m][0]), mm_tn(r[0], g, _NN_BWD[m][1])))
mm_nt.defvjp(lambda a, b, m: (_raw_dot(a, b, 1, 1, m), (a, b)),
             lambda m, r, g: (mm_nn(g, r[1], _NT_BWD[m][0]), mm_tn(g, r[0], _NT_BWD[m][1])))
mm_tn.defvjp(lambda a, b, m: (_raw_dot(a, b, 0, 0, m), (a, b)),
             lambda m, r, g: (mm_nt(r[1], g, _TN_BWD[m][0]), mm_nn(r[0], g, _TN_BWD[m][1])))


@jax.custom_jvp
def sigmoid(x):
    return 1.0 / (1.0 + jnp.exp(-x))


@sigmoid.defjvp
def _sigmoid_jvp(p, t):
    s = sigmoid(p[0])
    return s, t[0] * s * (1.0 - s)


@jax.custom_jvp
def softplus(x):
    return jnp.maximum(x, 0.0) + jnp.log(1.0 + jnp.exp(-jnp.abs(x)))


@softplus.defjvp
def _softplus_jvp(p, t):
    return softplus(p[0]), t[0] * sigmoid(p[0])


def silu(x):
    return x * sigmoid(x)


def rmsnorm(x, w):
    return x * lax.rsqrt(jnp.mean(x * x, axis=-1, keepdims=True) + EPS) * w


def _iota(shape, dim):
    return lax.broadcasted_iota(jnp.int32, shape, dim)


def _halves():
    lane = _iota((1, LANES), 1) >> 6
    return _ind(lane == 0), _ind(lane == 1)


def _block_diag(pair):
    h0, h1 = _halves()
    return jnp.concatenate([pair * h0, pair * h1], axis=0)


def _tri_inv_impl(mats):
    r, c = _iota((CHUNK, LANES), 0), _iota((CHUNK, LANES), 1) & (CHUNK - 1)
    eye = _ind(r == c)
    blockdiag = _ind((r >> 4) == (c >> 4))
    dot = lambda u, v: _raw_dot(u, _block_diag(v), 1, 0, "x3")
    dot1 = lambda u, v: _raw_dot(u, _block_diag(v), 1, 0)
    each = lambda f, *ls: [f(*xs) for xs in zip(*ls)]
    dg = each(lambda a: a * blockdiag, mats)
    off = each(lambda a, d: a - d, mats, dg)
    m = each(lambda d: -d, dg)
    p = each(lambda x: eye + x, m)
    pw = m
    for _ in range(3):
        pw = each(lambda x: dot1(x, x), pw)
        p = each(lambda x, y: x + dot1(x, y), p, pw)
    e = each(dot, p, off)
    e2 = each(lambda x: dot1(x, x), e)
    q = each(lambda x: eye - x, e)
    q = each(lambda x, y: x + dot1(x, y), q, e2)
    return each(dot, q, p)


def _tri_inv_bwd(ts, gs):
    h0, h1 = _halves()
    x = [mm_nt(g, _block_diag(t)) for g, t in zip(gs, ts)]
    full = [mm_tn(t, y) for t, y in zip(ts, x)]
    return [-(f[:CHUNK] * h0 + f[CHUNK:] * h1) for f in full]


@jax.custom_vjp
def tri_inv(mats):
    return _tri_inv_impl(mats)


def _tri_inv_fwd(mats):
    ts = _tri_inv_impl(mats)
    return ts, ts


tri_inv.defvjp(_tri_inv_fwd, lambda ts, gs: (_tri_inv_bwd(ts, gs),))


@jax.custom_vjp
def tri_inv_saved(mats, ts):
    del mats
    return ts


tri_inv_saved.defvjp(lambda mats, ts: (ts, ts),
                     lambda ts, gs: (_tri_inv_bwd(ts, gs), [jnp.zeros_like(t) for t in ts]))


def _ind(cond):
    return jnp.where(cond, 1.0, 0.0).astype(F32)


def _chunk_masks():
    r, c = _iota((CHUNK, CHUNK), 0), _iota((CHUNK, CHUNK), 1)
    return _ind(r >= c), _ind(r > c), _ind(r == c), _ind(_iota((CHUNK, 1), 0) == CHUNK - 1)


def _log_decay_cumsum(small, alog, dtb, tri):
    sp = softplus(small + dtb)
    la = -jnp.exp(alog) * sp
    return sp, mm_nn(tri, la, "sel_a")


def _col_of(x, lane):
    return jnp.sum(x * _ind(_iota((1, LANES), 1) == lane), axis=1, keepdims=True)


def _decay_matrix(col, tri, eye):
    row = jnp.sum(col * eye, axis=0, keepdims=True)
    return jnp.exp((col - row) * tri) * tri


def _pair_masks():
    r, c = _iota((CHUNK, LANES), 0), _iota((CHUNK, LANES), 1)
    c6 = c & (CHUNK - 1)
    return _ind(r >= c6), _ind(r > c6), (_ind(c == r), _ind(c == r + CHUNK))


def _decay_pair(col_a, col_b, tri_w, eye_w):
    h0, h1 = _halves()
    col = col_a * h0 + col_b * h1
    row = jnp.sum(col_a * eye_w[0] + col_b * eye_w[1], axis=0, keepdims=True)
    return jnp.exp((col - row) * tri_w) * tri_w


def gdn_chunk(h0, qs, ks, vs, smalls, gates, normw, alog, dtb, states, saved_t=None):
    tri, _, _, last = _chunk_masks()
    tri_w, strict_w, eye_w = _pair_masks()
    nh = len(qs[0])
    flat = lambda xss: [x for xs in xss for x in xs]
    lacs = [_log_decay_cumsum(sm, alog, dtb, tri)[1] for sm in smalls]
    qs, ks, vs, gates = flat(qs), flat(ks), flat(vs), flat(gates)
    heads, pairs = range(len(qs)), range(len(qs) // 2)
    each = lambda f, *ls: [f(*xs) for xs in zip(*ls)]
    ab = lambda xs, p: (xs[2 * p], xs[2 * p + 1])
    stack = lambda xs: jnp.concatenate(xs, axis=0)
    gc = [_col_of(lacs[i // nh], LANE_GA + h0 + i % nh) for i in heads]
    beta = [sigmoid(_col_of(smalls[i // nh], LANE_GB + h0 + i % nh)) for i in heads]
    decay = [_decay_pair(*ab(gc, p), tri_w, eye_w) for p in pairs]
    gl = each(lambda x: jnp.sum(x * last, axis=0, keepdims=True), gc)
    q = each(lambda x: x * lax.rsqrt(jnp.sum(x * x, axis=-1, keepdims=True) + EPS) * (GDN_DK ** -0.5), qs)
    k = each(lambda x: x * lax.rsqrt(jnp.sum(x * x, axis=-1, keepdims=True) + EPS), ks)
    kb = each(lambda x, b: x * b, k, beta)
    eg = each(jnp.exp, gc)
    zero = jnp.zeros((CHUNK, GDN_DK), F32)
    k_bd = [stack([join_lanes([k[2 * p], zero]), join_lanes([zero, k[2 * p + 1]])]) for p in pairs]
    a = [mm_nt(join_lanes(list(ab(kb, p))), k_bd[p]) * (decay[p] * strict_w) for p in pairs]
    t = tri_inv(a) if saved_t is None else tri_inv_saved(a, saved_t)
    attn = [mm_nt(join_lanes(list(ab(q, p))), k_bd[p]) * decay[p] for p in pairs]
    rhs = [stack([join_lanes([vs[h] * beta[h], kb[h] * eg[h]]) for h in (2 * p, 2 * p + 1)]) for p in pairs]
    uw = [mm_nn(_block_diag(t[p]), rhs[p]) for p in pairs]
    uw = [x for p in pairs for x in split_rows(uw[p])]
    u, w = zip(*[split_lanes(x) for x in uw])
    ys = []
    for c in range(len(smalls)):
        hs = range(c * nh, (c + 1) * nh)
        v_new = [u[i] - mm_nn(w[i], states[i % nh]) for i in hs]
        av = [mm_nn(_block_diag(attn[c * nh // 2 + p]), stack(list(ab(v_new, p)))) for p in range(nh // 2)]
        av = [x for y in av for x in split_rows(y)]
        o = [mm_nn(q[i] * eg[i], states[i % nh]) + av[i % nh] for i in hs]
        states = [states[i % nh] * jnp.exp(gl[i]) + mm_tn(k[i] * jnp.exp(gl[i] - gc[i]), v_new[i % nh]) for i in hs]
        ys.append([rmsnorm(o[i % nh], normw) * silu(gates[i]) for i in hs])
    return ys, states, t


@jax.custom_vjp
def split_rows(x):
    n = x.shape[0] // 2
    return [x[:n], x[n:]]


split_rows.defvjp(lambda x: (split_rows(x), None), lambda _, gs: (jnp.concatenate(gs, axis=0),))


@jax.custom_vjp
def split_lanes(x):
    return [x[:, i * LANES:(i + 1) * LANES] for i in range(x.shape[1] // LANES)]


@jax.custom_vjp
def join_lanes(xs):
    return jnp.concatenate(xs, axis=1)


split_lanes.defvjp(lambda x: (split_lanes(x), None), lambda _, gs: (join_lanes(gs),))
join_lanes.defvjp(lambda xs: (join_lanes(xs), None), lambda _, g: (split_lanes(g),))


def ssd_chunk(xs, bm, cm, z, smalls, normw, alog, dtb, dvec, state):
    tri, _, _, last = _chunk_masks()
    tri_w, _, eye_w = _pair_masks()
    h0, h1 = _halves()
    hpg = SSD_HEADS // SSD_GROUPS
    ng = len(normw)
    flat = lambda xss: [x for xs_ in xss for x in xs_]
    xs, bm, cm, z = flat(xs), flat(bm), flat(cm), flat(z)
    units, pairs = range(len(xs)), range(hpg // 2)
    each = lambda f, *ls: [f(*a) for a in zip(*ls)]
    sp_lac = [_log_decay_cumsum(sm, alog, dtb, tri) for sm in smalls]
    lac_last = [jnp.sum(lac * last, axis=0, keepdims=True) for _, lac in sp_lac]
    sel = [_ind(_iota((LANES, SSD_GW), 0) == g * hpg + (_iota((LANES, SSD_GW), 1) >> 6)) for g in range(ng)]
    expand = lambda vs, mode: [mm_nn(vs[i // ng], sel[i % ng], mode) for i in units]
    dt_e = expand([sp for sp, _ in sp_lac], "bf16")
    elac_e = expand([jnp.exp(lac) for _, lac in sp_lac], "bf16")
    toend_e = expand([jnp.exp(ll - lac) for (_, lac), ll in zip(sp_lac, lac_last)], "bf16")
    row8 = _iota((8, 1), 0)
    two_e = expand([_ind(row8 == 0) * dvec + _ind(row8 == 1) * jnp.exp(ll) for ll in lac_last], "sel_b")
    d_e = each(lambda v: jnp.sum(v * _ind(row8 == 0), axis=0, keepdims=True), two_e)
    chunk_e = each(lambda v: jnp.sum(v * _ind(row8 == 1), axis=0, keepdims=True), two_e)
    xdt = each(lambda a, b: a * b, xs, dt_e)
    cb_w = each(lambda c_, b_: mm_nt(c_, jnp.concatenate([b_, b_], axis=0)), cm, bm)
    x_pairs = each(split_lanes, xdt)
    col = lambda i, j: _col_of(sp_lac[i // ng][1], (i % ng) * hpg + j)
    lms = [[_decay_pair(col(i, 2 * p), col(i, 2 * p + 1), tri_w, eye_w) for p in pairs] for i in units]
    stacked = [[jnp.concatenate([x_pairs[i][p] * h0, x_pairs[i][p] * h1], axis=0) for p in pairs] for i in units]
    terms = [[mm_nn(cb_w[i] * lms[i][p], stacked[i][p]) for p in pairs] for i in units]
    y_in = [join_lanes(terms[i]) + xs[i] * d_e[i] for i in units]
    state_in = each(lambda b_, xd, te: mm_tn(b_, xd * te), bm, xdt, toend_e)
    outs = []
    for c in range(len(smalls)):
        us = range(c * ng, (c + 1) * ng)
        y = [mm_nn(cm[i], state[i % ng]) * elac_e[i] + y_in[i] for i in us]
        state = [state[i % ng] * chunk_e[i] + state_in[i] for i in us]
        outs.append([rmsnorm(y[i % ng] * silu(z[i]), normw[i % ng]) for i in us])
    return outs, state


def _params(sem=None):
    return pltpu.CompilerParams(dimension_semantics=sem, vmem_limit_bytes=VMEM_LIMIT)


def _full(shape):
    n = len(shape)
    return pl.BlockSpec(shape, lambda *_: (0,) * n)


ANY = pl.BlockSpec(memory_space=pl.ANY)
HBM = pl.BlockSpec(memory_space=pltpu.HBM)


def in_proj(x, normw, w_main, w_small):
    t = x.shape[0]
    tm, tn = min(1024, t), 512

    def body(x_ref, nw_ref, wm_ref, ws_ref, pm_ref, ps_ref, u_ref):
        @pl.when(pl.program_id(1) == 0)
        def _():
            u = rmsnorm(x_ref[...], nw_ref[...]).astype(MXU_DTYPE)
            u_ref[...] = u
            ps_ref[...] = _raw_dot(u, ws_ref[...], 1, 0)
        pm_ref[...] = _raw_dot(u_ref[...], wm_ref[...], 1, 0)

    return pl.pallas_call(
        body, name="in_proj", grid=(t // tm, COL_CONV // tn),
        in_specs=[pl.BlockSpec((tm, D_MODEL), lambda i, j: (i, 0)), _full((1, D_MODEL)),
                  pl.BlockSpec((D_MODEL, tn), lambda i, j: (0, j)), _full((D_MODEL, LANES))],
        out_specs=[pl.BlockSpec((tm, tn), lambda i, j: (i, j)), pl.BlockSpec((tm, LANES), lambda i, j: (i, 0)),
                   pl.BlockSpec((tm, D_MODEL), lambda i, j: (i, 0))],
        out_shape=[jax.ShapeDtypeStruct((t, COL_CONV), F32), jax.ShapeDtypeStruct((t, LANES), F32),
                   jax.ShapeDtypeStruct((t, D_MODEL), MXU_DTYPE)],
        compiler_params=_params(("arbitrary", "arbitrary")),
    )(x, normw, w_main, w_small)


CONV_TC = 512
HALO = 8


def _shift_down(cur, prev, s):
    rolled = pltpu.roll(cur, s, 0)
    top = jnp.where(_iota((HALO, cur.shape[1]), 0) < s, pltpu.roll(prev, s, 0), rolled[:HALO])
    if cur.shape[0] == HALO:
        return top
    return jnp.concatenate([top, rolled[HALO:]], axis=0)


def _shift_up(cur, nxt, s):
    n = cur.shape[0]
    rolled = pltpu.roll(cur, n - s, 0)
    bot = jnp.where(_iota((HALO, cur.shape[1]), 0) >= HALO - s, pltpu.roll(nxt, HALO - s, 0), rolled[n - HALO:])
    return jnp.concatenate([rolled[:n - HALO], bot], axis=0)


def _conv_pre(cur, prev, w_ref, b, cols=slice(None)):
    acc = cur * w_ref[3:4, cols] + b
    shifted = [cur]
    for s in (1, 2, 3):
        sh = _shift_down(cur, prev, s)
        shifted.append(sh)
        acc = acc + sh * w_ref[3 - s:4 - s, cols]
    return acc, shifted


def in_proj_conv(u, w_main, w, b):
    t = u.shape[0]
    tm, tn = min(1024, t), CONV_TC
    rc = min(256, tm)
    c0, nj = COL_CONV // tn, CONV_W // tn

    def body(u_ref, wm_ref, w_ref, b_ref, pm_ref, out_ref, halo_ref):
        j = pl.program_id(1)

        @pl.when(pl.program_id(0) == 0)
        def _():
            halo_ref[j] = jnp.zeros((HALO, tn), F32)

        prev = halo_ref[j]
        for r in range(tm // rc):
            rows = pl.ds(r * rc, rc)
            p = _raw_dot(u_ref[rows, :], wm_ref[...], 1, 0)
            pm_ref[rows, :] = p
            pre, _ = _conv_pre(p, prev, w_ref, b_ref[...])
            out_ref[rows, :] = silu(pre)
            prev = p[rc - HALO:]
        halo_ref[j] = prev

    return pl.pallas_call(
        body, name="in_proj_conv", grid=(t // tm, nj),
        in_specs=[pl.BlockSpec((tm, D_MODEL), lambda i, j: (i, 0)),
                  pl.BlockSpec((D_MODEL, tn), lambda i, j: (0, c0 + j)),
                  pl.BlockSpec((4, tn), lambda i, j: (0, j)), pl.BlockSpec((1, tn), lambda i, j: (0, j))],
        out_specs=[pl.BlockSpec((tm, tn), lambda i, j: (i, j)), pl.BlockSpec((tm, tn), lambda i, j: (i, j))],
        out_shape=[jax.ShapeDtypeStruct((t, CONV_W), F32), jax.ShapeDtypeStruct((t, CONV_W), F32)],
        scratch_shapes=[pltpu.VMEM((nj, HALO, tn), F32)],
        compiler_params=_params(("arbitrary", "arbitrary")),
    )(u, w_main, w, b)


def _dsilu(pre):
    sg = sigmoid(pre)
    return sg * (1.0 + pre * (1.0 - sg))


def conv_bwd_w(u, proj_conv, w, b, dout, slabbed):
    t = u.shape[0]
    tt, tn = min(512, t), 3 * CONV_TC
    nt, nj = t // tt, CONV_W // tn
    ns = len(slabbed)
    after = lambda i: jnp.minimum((i + 1) * (tt // HALO), t // HALO - 1)

    def body(u_ref, cur_ref, prev_ref, nxt_ref, w_ref, b_ref, do_ref, do_nxt_ref, *rest):
        slab_refs, (dx_ref, dw_ref, dwb_ref) = rest[:ns], rest[ns:ns + 3]
        land_refs, sems = rest[ns + 3:2 * ns + 3], rest[2 * ns + 3:]
        j, i = pl.program_id(0), pl.program_id(1)
        start, finish = _slab_exchange(slab_refs, land_refs, ns, *sems)

        @pl.when(jnp.logical_and(j == 0, i == 0))
        def _():
            start()

        @pl.when(i == 0)
        def _():
            dw_ref[...] = jnp.zeros(dw_ref.shape, F32)
            dwb_ref[...] = jnp.zeros(dwb_ref.shape, F32)

        uu = u_ref[...]
        row = _iota((HALO, CONV_TC), 0)
        first, last = i == 0, i == nt - 1
        for piece in range(tn // CONV_TC):
            cols = slice(piece * CONV_TC, (piece + 1) * CONV_TC)
            cur, bias = cur_ref[:, cols], b_ref[:, cols]
            prev = jnp.where(first, 0.0, prev_ref[:, cols])
            pre, shifted = _conv_pre(cur, prev, w_ref, bias, cols)
            dpre = do_ref[:, cols] * _dsilu(pre)
            pre_nxt, _ = _conv_pre(nxt_ref[:, cols], cur[tt - HALO:], w_ref, bias, cols)
            dpre_nxt = jnp.where(last, 0.0, do_nxt_ref[:, cols] * _dsilu(pre_nxt))
            dx = dpre * w_ref[3:4, cols]
            for s in (1, 2, 3):
                dx = dx + _shift_up(dpre, dpre_nxt, s) * w_ref[3 - s:4 - s, cols]
            dx = dx.astype(dx_ref.dtype)
            dx_ref[:, cols] = dx
            dw_ref[:, cols] += _raw_dot(uu, dx, 0, 0)
            upd = jnp.where(row == 4, jnp.sum(dpre, axis=0, keepdims=True), 0.0)
            for s in range(4):
                upd = upd + jnp.where(row == 3 - s, jnp.sum(dpre * shifted[s], axis=0, keepdims=True), 0.0)
            dwb_ref[:, cols] += upd

        @pl.when(jnp.logical_and(j == nj - 1, last))
        def _():
            finish()

    out = pl.pallas_call(
        body, name="conv_bwd_w", grid=(nj, nt),
        in_specs=[pl.BlockSpec((tt, D_MODEL), lambda j, i: (i, 0)),
                  pl.BlockSpec((tt, tn), lambda j, i: (i, j)),
                  pl.BlockSpec((HALO, tn), lambda j, i: (jnp.maximum(i * (tt // HALO) - 1, 0), j)),
                  pl.BlockSpec((HALO, tn), lambda j, i: (after(i), j)),
                  pl.BlockSpec((4, tn), lambda j, i: (0, j)), pl.BlockSpec((1, tn), lambda j, i: (0, j)),
                  pl.BlockSpec((tt, tn), lambda j, i: (i, j)),
                  pl.BlockSpec((HALO, tn), lambda j, i: (after(i), j))] + [HBM] * ns,
        out_specs=[pl.BlockSpec((tt, tn), lambda j, i: (i, j)), pl.BlockSpec((D_MODEL, tn), lambda j, i: (0, j)),
                   pl.BlockSpec((HALO, tn), lambda j, i: (0, j))] + [HBM] * ns,
        out_shape=[jax.ShapeDtypeStruct((t, CONV_W), MXU_DTYPE), jax.ShapeDtypeStruct((D_MODEL, CONV_W), F32),
                   jax.ShapeDtypeStruct((HALO, CONV_W), F32)] + _slab_exchange_shapes(slabbed, []),
        scratch_shapes=_slab_exchange_sems(ns),
        compiler_params=_params(("arbitrary", "arbitrary")),
    )(u, proj_conv, proj_conv, proj_conv, w, b, dout, dout, *slabbed)
    return out[0], out[1], out[2], out[3:]


def _ssd_cols(g):
    b0 = SSD_WIDTH + g * SSD_STATE
    c0 = SSD_WIDTH + SSD_GROUPS * SSD_STATE + g * SSD_STATE
    return slice(g * SSD_GW, (g + 1) * SSD_GW), slice(b0, b0 + SSD_STATE), slice(c0, c0 + SSD_STATE)


def _gdn_cols(j):
    return tuple(slice(s * GDN_W + j * GDN_DK, s * GDN_W + (j + 1) * GDN_DK) for s in range(3))


def _ssd_parts(xbc_ref):
    return tuple([[xbc_ref[_chunk_rows(c), _ssd_cols(g)[s]] for g in range(SSD_GROUPS)] for c in range(SSD_CB)]
                 for s in range(3))


def _group_cols(ref):
    return [[ref[_chunk_rows(c), g * SSD_GW:(g + 1) * SSD_GW] for g in range(SSD_GROUPS)] for c in range(SSD_CB)]


def _chunk_rows(c):
    return slice(c * CHUNK, (c + 1) * CHUNK)


def _gdn_parts(qkv_ref):
    assert GDN_HB == GDN_HEADS, "the conv block is read whole: one grid step holds every head"
    return tuple([[qkv_ref[_chunk_rows(c), _gdn_cols(j)[s]] for j in range(GDN_HB)] for c in range(GDN_CB)]
                 for s in range(3))


def _head_cols(ref):
    return [[ref[_chunk_rows(c), j * GDN_DV:(j + 1) * GDN_DV] for j in range(GDN_HB)] for c in range(GDN_CB)]


def _chunk_blocks(ref, n=GDN_CB):
    return [ref[_chunk_rows(c), :] for c in range(n)]


def _first_head():
    return 0 if GDN_HB == GDN_HEADS else pl.program_id(1) * GDN_HB


def ssd_fwd(conv_ssd, proj_main, proj_small, normw, alog, dtb, dvec):
    t = conv_ssd.shape[0]
    rows = CHUNK * SSD_CB
    nc = t // rows
    groups = range(SSD_GROUPS)
    norm_cols = lambda ref: [ref[:, g * SSD_GW:(g + 1) * SSD_GW] for g in groups]

    def body(xbc_ref, z_ref, sm_ref, nw_ref, al_ref, db_ref, dv_ref, y_ref, hist_ref, state_ref):
        @pl.when(pl.program_id(0) == 0)
        def _():
            state_ref[...] = jnp.zeros(state_ref.shape, F32)

        states = [state_ref[g] for g in groups]
        for g in groups:
            hist_ref[0, g] = states[g]
        ys, new_states = ssd_chunk(*_ssd_parts(xbc_ref), _group_cols(z_ref), _chunk_blocks(sm_ref, SSD_CB),
                                   norm_cols(nw_ref), al_ref[...], db_ref[...], dv_ref[...], states)
        for c in range(SSD_CB):
            for g in groups:
                y_ref[_chunk_rows(c), g * SSD_GW:(g + 1) * SSD_GW] = ys[c][g].astype(MXU_DTYPE)
        for g in groups:
            state_ref[g] = new_states[g]

    return pl.pallas_call(
        body, name="ssd_fwd", grid=(nc,),
        in_specs=[pl.BlockSpec((rows, SSD_CONV), lambda c: (c, (COL_SSD - COL_CONV) // SSD_CONV)),
                  pl.BlockSpec((rows, SSD_WIDTH), lambda c: (c, COL_Z // SSD_WIDTH)),
                  pl.BlockSpec((rows, LANES), lambda c: (c, 0)),
                  _full((1, SSD_WIDTH)), _full((1, LANES)), _full((1, LANES)), _full((1, LANES))],
        out_specs=[pl.BlockSpec((rows, SSD_WIDTH), lambda c: (c, 0)),
                   pl.BlockSpec((1, SSD_GROUPS, SSD_STATE, SSD_GW), lambda c: (c, 0, 0, 0))],
        out_shape=[jax.ShapeDtypeStruct((t, SSD_WIDTH), MXU_DTYPE),
                   jax.ShapeDtypeStruct((nc, SSD_GROUPS, SSD_STATE, SSD_GW), F32)],
        scratch_shapes=[pltpu.VMEM((SSD_GROUPS, SSD_STATE, SSD_GW), F32)],
        compiler_params=_params(("arbitrary",)),
    )(conv_ssd, proj_main, proj_small, normw, alog, dtb, dvec)


def _accumulate(ref, first, value):
    @pl.when(first)
    def _():
        ref[...] = value

    @pl.when(jnp.logical_not(first))
    def _():
        ref[...] += value


def ssd_bwd(conv_ssd, proj_main, proj_small, normw, alog, dtb, dvec, hist, dy):
    t = conv_ssd.shape[0]
    rows = CHUNK * SSD_CB
    nc = t // rows
    rev = lambda c: nc - 1 - c
    groups = range(SSD_GROUPS)
    norm_cols = lambda ref: [ref[:, g * SSD_GW:(g + 1) * SSD_GW] for g in groups]

    def body(xbc_ref, z_ref, sm_ref, nw_ref, al_ref, db_ref, dv_ref, hist_ref, dy_ref,
             dxbc_ref, dz_ref, dsm_ref, dnw_ref, dal_ref, ddb_ref, ddv_ref, dstate_ref):
        first = pl.program_id(0) == 0

        @pl.when(first)
        def _():
            dstate_ref[...] = jnp.zeros(dstate_ref.shape, F32)

        _, vjp = jax.vjp(ssd_chunk, *_ssd_parts(xbc_ref), _group_cols(z_ref), _chunk_blocks(sm_ref, SSD_CB),
                         norm_cols(nw_ref), al_ref[...], db_ref[...], dv_ref[...], [hist_ref[0, g] for g in groups])
        dxs, dbm, dcm, dz, dsm, dnw, dal, ddb, ddv, dstate = vjp(
            (_group_cols(dy_ref), [dstate_ref[g] for g in groups]))
        for k in range(SSD_CB):
            rk = _chunk_rows(k)
            for g in groups:
                xc, bc, cc = _ssd_cols(g)
                dxbc_ref[rk, xc] = dxs[k][g]
                dxbc_ref[rk, bc] = dbm[k][g]
                dxbc_ref[rk, cc] = dcm[k][g]
                dz_ref[rk, g * SSD_GW:(g + 1) * SSD_GW] = dz[k][g].astype(dz_ref.dtype)
            dsm_ref[rk, :] = dsm[k]
        for g in groups:
            dstate_ref[g] = dstate[g]
        _accumulate(dnw_ref, first, join_lanes(dnw))
        _accumulate(dal_ref, first, dal)
        _accumulate(ddb_ref, first, ddb)
        _accumulate(ddv_ref, first, ddv)

    return pl.pallas_call(
        body, name="ssd_bwd", grid=(nc,),
        in_specs=[pl.BlockSpec((rows, SSD_CONV), lambda c: (rev(c), (COL_SSD - COL_CONV) // SSD_CONV)),
                  pl.BlockSpec((rows, SSD_WIDTH), lambda c: (rev(c), COL_Z // SSD_WIDTH)),
                  pl.BlockSpec((rows, LANES), lambda c: (rev(c), 0)),
                  _full((1, SSD_WIDTH)), _full((1, LANES)), _full((1, LANES)), _full((1, LANES)),
                  pl.BlockSpec((1, SSD_GROUPS, SSD_STATE, SSD_GW), lambda c: (rev(c), 0, 0, 0)),
                  pl.BlockSpec((rows, SSD_WIDTH), lambda c: (rev(c), 0))],
        out_specs=[pl.BlockSpec((rows, SSD_CONV), lambda c: (rev(c), (COL_SSD - COL_CONV) // SSD_CONV)),
                   pl.BlockSpec((rows, SSD_WIDTH), lambda c: (rev(c), COL_Z // SSD_WIDTH)),
                   pl.BlockSpec((rows, LANES), lambda c: (rev(c), 0)),
                   _full((1, SSD_WIDTH)), _full((1, LANES)), _full((1, LANES)), _full((1, LANES))],
        out_shape=[jax.ShapeDtypeStruct((t, CONV_W), F32), jax.ShapeDtypeStruct((t, COL_CONV), MXU_DTYPE),
                   jax.ShapeDtypeStruct((t, LANES), F32), jax.ShapeDtypeStruct((1, SSD_WIDTH), F32),
                   jax.ShapeDtypeStruct((1, LANES), F32), jax.ShapeDtypeStruct((1, LANES), F32),
                   jax.ShapeDtypeStruct((1, LANES), F32)],
        scratch_shapes=[pltpu.VMEM((SSD_GROUPS, SSD_STATE, SSD_GW), F32)],
        compiler_params=_params(("arbitrary",)),
    )(conv_ssd, proj_main, proj_small, normw, alog, dtb, dvec, hist, dy)


def gdn_fwd(conv_gdn, proj_main, proj_small, normw, alog, dtb):
    t = conv_gdn.shape[0]
    hb, cb = GDN_HB, GDN_CB
    rows = CHUNK * cb
    ns = t // rows
    gate_blk = COL_GATE // (GDN_DV * hb)

    def body(qkv_ref, gate_ref, sm_ref, nw_ref, al_ref, db_ref, y_ref, hist_ref, t_ref, state_ref):
        h0 = _first_head()

        @pl.when(pl.program_id(0) == 0)
        def _():
            for j in range(hb):
                state_ref[h0 + j] = jnp.zeros((GDN_DK, GDN_DV), F32)

        states = [state_ref[h0 + j] for j in range(hb)]
        for j in range(hb):
            hist_ref[0, j] = states[j]
        qs, ks, vs = _gdn_parts(qkv_ref)
        ys, new_states, ts = gdn_chunk(h0, qs, ks, vs, _chunk_blocks(sm_ref), _head_cols(gate_ref), nw_ref[...],
                                       al_ref[...], db_ref[...], states)
        for c in range(cb):
            for j in range(hb):
                y_ref[_chunk_rows(c), j * GDN_DV:(j + 1) * GDN_DV] = ys[c][j].astype(MXU_DTYPE)
        for j in range(hb):
            state_ref[h0 + j] = new_states[j]
        for p in range(cb * hb // 2):
            t_ref[0, p] = ts[p]

    return pl.pallas_call(
        body, name="gdn_fwd", grid=(ns, GDN_HEADS // hb),
        in_specs=[pl.BlockSpec((rows, GDN_HC * hb), lambda c, h: (c, h)),
                  pl.BlockSpec((rows, GDN_DV * hb), lambda c, h: (c, gate_blk + h)),
                  pl.BlockSpec((rows, LANES), lambda c, h: (c, 0)),
                  _full((1, GDN_DV)), _full((1, LANES)), _full((1, LANES))],
        out_specs=[pl.BlockSpec((rows, GDN_DV * hb), lambda c, h: (c, h)),
                   pl.BlockSpec((1, hb, GDN_DK, GDN_DV), lambda c, h: (c, h, 0, 0)),
                   pl.BlockSpec((1, cb * hb // 2, CHUNK, LANES), lambda c, h: (c, h, 0, 0))],
        out_shape=[jax.ShapeDtypeStruct((t, GDN_W), MXU_DTYPE),
                   jax.ShapeDtypeStruct((ns, GDN_HEADS, GDN_DK, GDN_DV), F32),
                   jax.ShapeDtypeStruct((ns, cb * GDN_HEADS // 2, CHUNK, LANES), F32)],
        scratch_shapes=[pltpu.VMEM((GDN_HEADS, GDN_DK, GDN_DV), F32)],
        compiler_params=_params(("arbitrary", "arbitrary")),
    )(conv_gdn, proj_main, proj_small, normw, alog, dtb)


def gdn_bwd(dproj_main, dconv, conv_gdn, proj_main, proj_small, normw, alog, dtb, hist, t_inv, dy):
    t = conv_gdn.shape[0]
    hb, cb = GDN_HB, GDN_CB
    rows = CHUNK * cb
    ns = t // rows
    rev = lambda c: ns - 1 - c
    gate_blk = COL_GATE // (GDN_DV * hb)

    def body(alias_ref, alias2_ref, qkv_ref, gate_ref, sm_ref, nw_ref, al_ref, db_ref, hist_ref, t_ref, dy_ref,
             dgate_ref, dqkv_ref, dsm_ref, dnw_ref, dal_ref, ddb_ref, dstate_ref):
        del alias_ref, alias2_ref
        c, h = pl.program_id(0), pl.program_id(1)
        h0 = _first_head()

        @pl.when(c == 0)
        def _():
            for j in range(hb):
                dstate_ref[h0 + j] = jnp.zeros((GDN_DK, GDN_DV), F32)

        saved = [t_ref[0, p] for p in range(cb * hb // 2)]

        def fn(qs, ks, vs, smalls, gates, nw, al, db, states):
            return gdn_chunk(h0, qs, ks, vs, smalls, gates, nw, al, db, states, saved)[:2]

        qs, ks, vs = _gdn_parts(qkv_ref)
        _, vjp = jax.vjp(fn, qs, ks, vs, _chunk_blocks(sm_ref), _head_cols(gate_ref), nw_ref[...], al_ref[...],
                         db_ref[...], [hist_ref[0, j] for j in range(hb)])
        dqs, dks, dvs, dsm, dgates, dnw, dal, ddb, dstates = vjp(
            (_head_cols(dy_ref), [dstate_ref[h0 + j] for j in range(hb)]))
        for k in range(cb):
            rk = _chunk_rows(k)
            for j in range(hb):
                qc, kc, vc = _gdn_cols(j)
                dqkv_ref[rk, qc] = dqs[k][j]
                dqkv_ref[rk, kc] = dks[k][j]
                dqkv_ref[rk, vc] = dvs[k][j]
                dgate_ref[rk, j * GDN_DV:(j + 1) * GDN_DV] = dgates[k][j].astype(dgate_ref.dtype)
        for j in range(hb):
            dstate_ref[h0 + j] = dstates[j]
        _accumulate(dsm_ref, h == 0, jnp.concatenate(dsm, axis=0))
        first = jnp.logical_and(c == 0, h == 0)
        _accumulate(dnw_ref, first, dnw)
        _accumulate(dal_ref, first, dal)
        _accumulate(ddb_ref, first, ddb)

    return pl.pallas_call(
        body, name="gdn_bwd", grid=(ns, GDN_HEADS // hb),
        in_specs=[ANY, ANY, pl.BlockSpec((rows, GDN_HC * hb), lambda c, h: (rev(c), h)),
                  pl.BlockSpec((rows, GDN_DV * hb), lambda c, h: (rev(c), gate_blk + h)),
                  pl.BlockSpec((rows, LANES), lambda c, h: (rev(c), 0)),
                  _full((1, GDN_DV)), _full((1, LANES)), _full((1, LANES)),
                  pl.BlockSpec((1, hb, GDN_DK, GDN_DV), lambda c, h: (rev(c), h, 0, 0)),
                  pl.BlockSpec((1, cb * hb // 2, CHUNK, LANES), lambda c, h: (rev(c), h, 0, 0)),
                  pl.BlockSpec((rows, GDN_DV * hb), lambda c, h: (rev(c), h))],
        out_specs=[pl.BlockSpec((rows, GDN_DV * hb), lambda c, h: (rev(c), gate_blk + h)),
                   pl.BlockSpec((rows, GDN_HC * hb), lambda c, h: (rev(c), h)),
                   pl.BlockSpec((rows, LANES), lambda c, h: (rev(c), 0)),
                   _full((1, GDN_DV)), _full((1, LANES)), _full((1, LANES))],
        out_shape=[jax.ShapeDtypeStruct(dproj_main.shape, dproj_main.dtype),
                   jax.ShapeDtypeStruct(dconv.shape, dconv.dtype),
                   jax.ShapeDtypeStruct((t, LANES), F32), jax.ShapeDtypeStruct((1, GDN_DV), F32),
                   jax.ShapeDtypeStruct((1, LANES), F32), jax.ShapeDtypeStruct((1, LANES), F32)],
        scratch_shapes=[pltpu.VMEM((GDN_HEADS, GDN_DK, GDN_DV), F32)],
        input_output_aliases={0: 0, 1: 1},
        compiler_params=_params(("arbitrary", "arbitrary")),
    )(dproj_main, dconv, conv_gdn, proj_main, proj_small, normw, alog, dtb, hist, t_inv, dy)


def out_proj_loss(x, y_ssd, y_gdn, w_out, final_w, target):
    t = x.shape[0]
    tm = min(512, t)

    def body(x_ref, ys_ref, yg_ref, wo_ref, fw_ref, tg_ref, loss_ref, dhid_ref, dys_ref, dyg_ref, dwo_ref, dfw_ref):
        i = pl.program_id(0)
        ys, yg = ys_ref[...], yg_ref[...]
        wo_s, wo_g = wo_ref[:SSD_WIDTH, :], wo_ref[SSD_WIDTH:, :]
        hid = x_ref[...] + _raw_dot(ys, wo_s, 1, 0) + _raw_dot(yg, wo_g, 1, 0)
        out, vjp = jax.vjp(rmsnorm, hid, fw_ref[...])
        err = out - tg_ref[...]
        loss = 0.5 * jnp.sum(jnp.mean(err * err, axis=-1, keepdims=True), axis=0, keepdims=True)
        dhid, dfw = vjp(err * (1.0 / D_MODEL))
        dhid_ref[...] = dhid
        dys_ref[...] = _raw_dot(dhid, wo_s, 1, 1)
        dyg_ref[...] = _raw_dot(dhid, wo_g, 1, 1)
        first = i == 0
        _accumulate(loss_ref, first, jnp.broadcast_to(loss, loss_ref.shape))
        _accumulate(dfw_ref, first, dfw)

        @pl.when(first)
        def _():
            dwo_ref[:SSD_WIDTH, :] = _raw_dot(ys, dhid, 0, 0)
            dwo_ref[SSD_WIDTH:, :] = _raw_dot(yg, dhid, 0, 0)

        @pl.when(i > 0)
        def _():
            dwo_ref[:SSD_WIDTH, :] += _raw_dot(ys, dhid, 0, 0)
            dwo_ref[SSD_WIDTH:, :] += _raw_dot(yg, dhid, 0, 0)

    row = lambda w: pl.BlockSpec((tm, w), lambda i: (i, 0))
    return pl.pallas_call(
        body, name="out_proj_loss", grid=(t // tm,),
        in_specs=[row(D_MODEL), row(SSD_WIDTH), row(GDN_W), _full((SSD_WIDTH + GDN_W, D_MODEL)), _full((1, D_MODEL)),
                  row(D_MODEL)],
        out_specs=[_full((8, LANES)), row(D_MODEL), row(SSD_WIDTH), row(GDN_W), _full((SSD_WIDTH + GDN_W, D_MODEL)),
                   _full((1, D_MODEL))],
        out_shape=[jax.ShapeDtypeStruct((8, LANES), F32), jax.ShapeDtypeStruct((t, D_MODEL), F32),
                   jax.ShapeDtypeStruct((t, SSD_WIDTH), F32), jax.ShapeDtypeStruct((t, GDN_W), F32),
                   jax.ShapeDtypeStruct((SSD_WIDTH + GDN_W, D_MODEL), F32), jax.ShapeDtypeStruct((1, D_MODEL), F32)],
        compiler_params=_params(("arbitrary",)),
    )(x, y_ssd, y_gdn, w_out, final_w, target)


def in_proj_bwd_x(x, normw, w_main, w_small, dproj_main, dproj_conv, dsmall_a, dsmall_b, dhid, slabbed):
    t = x.shape[0]
    tm = min(256, t)
    ni = t // tm
    ns = len(slabbed)

    def body(x_ref, nw_ref, wm_ref, ws_ref, dp_ref, dc_ref, da_ref, db_ref, dh_ref, *rest):
        slab_refs, (gx_ref, dnw_ref), land_refs = rest[:ns], rest[ns:ns + 2], rest[ns + 2:2 * ns + 2]
        sems = rest[2 * ns + 2:]
        i = pl.program_id(0)
        start, finish = _slab_exchange(slab_refs, land_refs, ns, *sems)

        @pl.when(i == 0)
        def _():
            start()

        du = (_raw_dot(dp_ref[...], wm_ref[:, :COL_CONV], 1, 1) + _raw_dot(dc_ref[...], wm_ref[:, COL_CONV:], 1, 1)
              + _raw_dot(da_ref[...] + db_ref[...], ws_ref[...], 1, 1))
        _, vjp = jax.vjp(rmsnorm, x_ref[...], nw_ref[...])
        dx, dnw = vjp(du)
        gx_ref[...] = dx + dh_ref[...]
        _accumulate(dnw_ref, i == 0, dnw)

        @pl.when(i == ni - 1)
        def _():
            finish()

    row = lambda w: pl.BlockSpec((tm, w), lambda i: (i, 0))
    out = pl.pallas_call(
        body, name="in_proj_bwd_x", grid=(ni,),
        in_specs=[row(D_MODEL), _full((1, D_MODEL)), _full((D_MODEL, MAIN)), _full((D_MODEL, LANES)), row(COL_CONV),
                  row(CONV_W), row(LANES), row(LANES), row(D_MODEL)] + [HBM] * ns,
        out_specs=[row(D_MODEL), _full((1, D_MODEL))] + [HBM] * ns,
        out_shape=[jax.ShapeDtypeStruct((t, D_MODEL), F32), jax.ShapeDtypeStruct((1, D_MODEL), F32)]
        + _slab_exchange_shapes(slabbed, []),
        scratch_shapes=_slab_exchange_sems(ns),
        compiler_params=_params(("arbitrary",)),
    )(x, normw, w_main, w_small, dproj_main, dproj_conv, dsmall_a, dsmall_b, dhid, *slabbed)
    return out[0], out[1], out[2:]


def in_proj_bwd_w(u, dproj_main, dsmall_a, dsmall_b):
    t = u.shape[0]
    tm, tn = min(1024, t), COL_CONV // 2

    def body(u_ref, dp_ref, da_ref, db_ref, dwm_ref, dws_ref):
        j, i = pl.program_id(0), pl.program_id(1)
        uu = u_ref[...]
        _accumulate(dwm_ref, i == 0, _raw_dot(uu, dp_ref[...], 0, 0))

        @pl.when(j == 0)
        def _():
            _accumulate(dws_ref, i == 0, _raw_dot(uu, da_ref[...] + db_ref[...], 0, 0))

    return pl.pallas_call(
        body, name="in_proj_bwd_w", grid=(COL_CONV // tn, t // tm),
        in_specs=[pl.BlockSpec((tm, D_MODEL), lambda j, i: (i, 0)), pl.BlockSpec((tm, tn), lambda j, i: (i, j)),
                  pl.BlockSpec((tm, LANES), lambda j, i: (i, 0)), pl.BlockSpec((tm, LANES), lambda j, i: (i, 0))],
        out_specs=[pl.BlockSpec((D_MODEL, tn), lambda j, i: (0, j)), _full((D_MODEL, LANES))],
        out_shape=[jax.ShapeDtypeStruct((D_MODEL, COL_CONV), F32), jax.ShapeDtypeStruct((D_MODEL, LANES), F32)],
        compiler_params=_params(("arbitrary", "arbitrary")),
    )(u, dproj_main, dsmall_a, dsmall_b)


def sum_slabs(a, name):
    n, rows, cols = a.shape
    tr = 64 if rows % 64 == 0 else rows

    def body(a_ref, o_ref):
        acc = a_ref[0].astype(F32)
        for d in range(1, n):
            acc = acc + a_ref[d].astype(F32)
        o_ref[...] = acc

    return pl.pallas_call(
        body, name=name, grid=(rows // tr,),
        in_specs=[pl.BlockSpec((n, tr, cols), lambda i: (0, i, 0))],
        out_specs=pl.BlockSpec((tr, cols), lambda i: (i, 0)),
        out_shape=jax.ShapeDtypeStruct((rows, cols), F32),
        compiler_params=_params(("arbitrary",)),
    )(a)


def adamw(w, g, m, v, name):
    _, rows, cols = w.shape
    tr = 128 if rows % 128 == 0 else rows

    def body(w_ref, g_ref, m_ref, v_ref, d_ref, nm_ref, nv_ref):
        gg = g_ref[...]
        nm = ADAM_B1 * m_ref[...] + (1.0 - ADAM_B1) * gg
        nv = ADAM_B2 * v_ref[...] + (1.0 - ADAM_B2) * (gg * gg)
        m_hat = nm / (1.0 - ADAM_B1 ** ADAM_STEP)
        v_hat = nv / (1.0 - ADAM_B2 ** ADAM_STEP)
        d_ref[...] = -ADAM_LR * (m_hat / (jnp.sqrt(v_hat) + ADAM_EPS) + ADAM_WD * w_ref[...])
        nm_ref[...] = nm
        nv_ref[...] = nv

    spec = pl.BlockSpec((1, tr, cols), lambda i: (0, i, 0))
    shp = jax.ShapeDtypeStruct((1, rows, cols), F32)
    return pl.pallas_call(
        body, name=name, grid=(rows // tr,), in_specs=[spec] * 4, out_specs=[spec] * 3, out_shape=[shp] * 3,
        compiler_params=_params(("arbitrary",)),
    )(w, g.reshape(w.shape), m, v)


def _my_place():
    return lax.axis_index("x"), lax.axis_index("y"), lax.axis_index("c")


def gather_weights(big, small):
    nb, n = len(big), len(big) + len(small)
    parts = 4

    def body(*refs):
        srcs, outs = refs[:n], refs[n:2 * n]
        land_a, land_b = refs[2 * n:2 * n + nb], refs[2 * n + nb:2 * n + 2 * nb]
        send_sems, recv_sems, fwd_send, fwd_recv, local_sems = refs[2 * n + 2 * nb:]
        x, y, c = _my_place()
        me = 2 * x + y
        chips = [(1 - x, y), (x, 1 - y), (1 - x, 1 - y)]
        half = [a.shape[0] // 2 for a in big]

        def ici(j, i):
            px, py = chips[j]
            if i < nb:
                src, dst = srcs[i].at[pl.ds(c * half[i], half[i])], land_a[i].at[j]
            else:
                src, dst = srcs[i], outs[i].at[me]
            return pltpu.make_async_remote_copy(src_ref=src, dst_ref=dst, send_sem=send_sems.at[j * n + i],
                                                recv_sem=recv_sems.at[j * n + i], device_id=(px, py, c),
                                                device_id_type=MESH)

        def ici_arrival(j, i):
            px, py = chips[j]
            dst = land_a[i].at[j] if i < nb else outs[i].at[2 * px + py]
            return pltpu.make_async_remote_copy(src_ref=dst, dst_ref=dst, send_sem=send_sems.at[j * n + i],
                                                recv_sem=recv_sems.at[j * n + i], device_id=(px, py, c),
                                                device_id_type=MESH)

        def forward(j, i, p):
            rows = half[i] // parts
            k = (j * nb + i) * parts + p
            return pltpu.make_async_remote_copy(
                src_ref=land_a[i].at[j, pl.ds(p * rows, rows)], dst_ref=land_b[i].at[j, pl.ds(p * rows, rows)],
                send_sem=fwd_send.at[k], recv_sem=fwd_recv.at[k], device_id=(x, y, 1 - c), device_id_type=MESH)

        def store(j, i, from_sibling):
            px, py = chips[j]
            buf, h = (land_b, 1 - c) if from_sibling else (land_a, c)
            k = n + (j * nb + i) * 2 + (1 if from_sibling else 0)
            return pltpu.make_async_copy(buf[i].at[j], outs[i].at[2 * px + py, pl.ds(h * half[i], half[i])],
                                         local_sems.at[k])

        own = [pltpu.make_async_copy(srcs[i], outs[i].at[me], local_sems.at[i]) for i in range(n)]
        sends = [ici(j, i) for j in range(3) for i in range(n)]
        for cp in own + sends:
            cp.start()
        pending = []
        for j in range(3):
            for i in range(n):
                ici_arrival(j, i).wait_recv()
                if i < nb:
                    fw = [forward(j, i, p) for p in range(parts)]
                    st = store(j, i, False)
                    for cp in fw + [st]:
                        cp.start()
                    pending += [cp.wait_send for cp in fw] + [st.wait]
        for j in range(3):
            for i in range(nb):
                for p in range(parts):
                    forward(j, i, p).wait_recv()
                st = store(j, i, True)
                st.start()
                pending.append(st.wait)
        for cp in sends:
            cp.wait_send()
        for wait in pending:
            wait()
        for cp in own:
            cp.wait()

    shards = list(big) + list(small)
    lands = [pltpu.VMEM((3, a.shape[0] // 2) + a.shape[1:], a.dtype) for a in big]
    return pl.pallas_call(
        body, name="gather_weights",
        in_specs=[HBM] * n, out_specs=[HBM] * n,
        out_shape=[jax.ShapeDtypeStruct((N_CHIP,) + s.shape, s.dtype) for s in shards],
        scratch_shapes=lands + lands + [
            pltpu.SemaphoreType.DMA((3 * n,)), pltpu.SemaphoreType.DMA((3 * n,)),
            pltpu.SemaphoreType.DMA((3 * nb * parts,)), pltpu.SemaphoreType.DMA((3 * nb * parts,)),
            pltpu.SemaphoreType.DMA((n + 6 * nb,))],
        compiler_params=pltpu.CompilerParams(vmem_limit_bytes=VMEM_LIMIT),
    )(*shards)


def _peer(x, y, c, mask):
    mx, my, mc = (mask >> 2) & 1, (mask >> 1) & 1, mask & 1
    return (x ^ mx if mx else x, y ^ my if my else y, c ^ mc if mc else c)


def _slab_exchange_shapes(slabbed, replicated):
    return ([jax.ShapeDtypeStruct(a.shape, a.dtype) for a in slabbed]
            + [jax.ShapeDtypeStruct((N_DEV,) + a.shape, a.dtype) for a in replicated])


def _slab_exchange_sems(n):
    return [pltpu.SemaphoreType.DMA((7 * n,)), pltpu.SemaphoreType.DMA((7 * n,)), pltpu.SemaphoreType.DMA((n,))]


def _slab_exchange(srcs, outs, ns, send_sems, recv_sems, local_sems):
    n = len(srcs)
    x, y, c = _my_place()
    me = 4 * x + 2 * y + c

    def piece(i, dev):
        return srcs[i].at[dev] if i < ns else srcs[i]

    def copies(arriving):
        out = []
        for mask in range(1, N_DEV):
            px, py, pc = _peer(x, y, c, mask)
            dev = 4 * px + 2 * py + pc
            for i in range(n):
                k = (mask - 1) * n + i
                out.append(pltpu.make_async_remote_copy(
                    src_ref=piece(i, dev), dst_ref=outs[i].at[dev if arriving else me], send_sem=send_sems.at[k],
                    recv_sem=recv_sems.at[k], device_id=(px, py, pc), device_id_type=MESH))
        return out

    def local():
        return [pltpu.make_async_copy(piece(i, me), outs[i].at[me], local_sems.at[i]) for i in range(n)]

    def start():
        for cp in local() + copies(False):
            cp.start()

    def finish():
        for cp in copies(True):
            cp.wait_recv()
        for cp in copies(False):
            cp.wait_send()
        for cp in local():
            cp.wait()

    return start, finish


def exchange_halves(halves, replicated):
    n, nr = len(halves), len(replicated)
    streams = 8

    def body(*refs):
        srcs, rep_srcs, outs, rep_outs = refs[:n], refs[n:n + nr], refs[n + nr:2 * n + nr], refs[2 * n + nr:2 * (n + nr)]
        refs = refs[2 * (n + nr):]
        mine, theirs = refs[:n], refs[n:2 * n]
        send_sems, recv_sems, in_sems, out_sems = refs[2 * n:2 * n + 4]
        rep_start, rep_finish = _slab_exchange(rep_srcs, rep_outs, 0, *refs[2 * n + 4:])
        rep_start()
        x, y, c = _my_place()
        loads = [pltpu.make_async_copy(srcs[i], mine[i], in_sems.at[i]) for i in range(n)]
        for cp in loads:
            cp.start()
        for cp in loads:
            cp.wait()

        def chunk_copy(i, s):
            rows = halves[i].shape[0] // streams
            k = i * streams + s
            return pltpu.make_async_remote_copy(
                src_ref=mine[i].at[pl.ds(s * rows, rows)], dst_ref=theirs[i].at[pl.ds(s * rows, rows)],
                send_sem=send_sems.at[k], recv_sem=recv_sems.at[k], device_id=(x, y, 1 - c), device_id_type=MESH)

        sends = [chunk_copy(i, s) for i in range(n) for s in range(streams)]
        for cp in sends:
            cp.start()
        own = [pltpu.make_async_copy(mine[i], outs[i].at[c], out_sems.at[i]) for i in range(n)]
        for cp in own:
            cp.start()
        for cp in sends:
            cp.wait_recv()
        got = [pltpu.make_async_copy(theirs[i], outs[i].at[1 - c], out_sems.at[n + i]) for i in range(n)]
        for cp in got:
            cp.start()
        for cp in sends:
            cp.wait_send()
        for cp in own + got:
            cp.wait()
        rep_finish()

    vmem = [pltpu.VMEM(a.shape, a.dtype) for a in halves]
    out = pl.pallas_call(
        body, name="exchange_halves",
        in_specs=[HBM] * (n + nr), out_specs=[HBM] * (n + nr),
        out_shape=[jax.ShapeDtypeStruct((2,) + a.shape, a.dtype) for a in halves]
        + _slab_exchange_shapes([], replicated),
        scratch_shapes=vmem + vmem + [pltpu.SemaphoreType.DMA((n * streams,)), pltpu.SemaphoreType.DMA((n * streams,)),
                                      pltpu.SemaphoreType.DMA((n,)), pltpu.SemaphoreType.DMA((2 * n,))]
        + _slab_exchange_sems(nr),
        compiler_params=pltpu.CompilerParams(vmem_limit_bytes=VMEM_LIMIT),
    )(*halves, *replicated)
    return out[:n], out[n:]


def _pack_cols(pieces):
    offs, pos = [], 0
    for a in pieces:
        offs.append(pos)
        pos += a.shape[1]
    rows8 = [jnp.pad(a.astype(F32), ((0, 8 - a.shape[0]), (0, 0))) for a in pieces]
    return jnp.concatenate(rows8, axis=1), offs


def adamw_many(ws, gs, ms, vs):
    n = len(ws)

    def body(*refs):
        w_r, g_r, m_r, v_r = refs[:n], refs[n:2 * n], refs[2 * n:3 * n], refs[3 * n:4 * n]
        d_o, m_o, v_o = refs[4 * n:5 * n], refs[5 * n:6 * n], refs[6 * n:7 * n]
        for i in range(n):
            gg = g_r[i][...]
            nm = ADAM_B1 * m_r[i][...] + (1.0 - ADAM_B1) * gg
            nv = ADAM_B2 * v_r[i][...] + (1.0 - ADAM_B2) * (gg * gg)
            m_hat = nm / (1.0 - ADAM_B1 ** ADAM_STEP)
            v_hat = nv / (1.0 - ADAM_B2 ** ADAM_STEP)
            d_o[i][...] = -ADAM_LR * (m_hat / (jnp.sqrt(v_hat) + ADAM_EPS) + ADAM_WD * w_r[i][...])
            m_o[i][...] = nm
            v_o[i][...] = nv

    shapes = [jax.ShapeDtypeStruct(w.shape, F32) for w in ws]
    out = pl.pallas_call(body, name="adamw_small", out_shape=shapes * 3,
                         compiler_params=pltpu.CompilerParams(vmem_limit_bytes=VMEM_LIMIT))(*ws, *gs, *ms, *vs)
    return out[:n], out[n:2 * n], out[2 * n:]


def _lanes(vec, start):
    n = vec.shape[-1]
    return jnp.pad(vec.reshape(1, n).astype(F32), ((0, 0), (start, LANES - start - n)))


def kernel(x, norm_w, w_in, ssd_conv_w, ssd_conv_b, ssd_dt_bias, ssd_a_log, ssd_d, ssd_norm_w, gdn_conv_w, gdn_dt_bias, gdn_a_log, gdn_norm_w, w_out, final_norm_w, loss_target, m_norm_w, m_w_in, m_ssd_conv_w, m_ssd_conv_b, m_ssd_dt_bias, m_ssd_a_log, m_ssd_d, m_ssd_norm_w, m_gdn_conv_w, m_gdn_dt_bias, m_gdn_a_log, m_gdn_norm_w, m_w_out, m_final_norm_w, v_norm_w, v_w_in, v_ssd_conv_w, v_ssd_conv_b, v_ssd_dt_bias, v_ssd_a_log, v_ssd_d, v_ssd_norm_w, v_gdn_conv_w, v_gdn_dt_bias, v_gdn_a_log, v_gdn_norm_w, v_w_out, v_final_norm_w):
    xs = x[0]
    target = loss_target[0]
    chip = 2 * lax.axis_index("x") + lax.axis_index("y")
    w_in_shard, w_out_shard = w_in[0], w_out[0]
    in_cols = w_in_shard.shape[1]
    out_rows = w_out_shard.shape[0]

    g_in, g_out, g_cs, g_cg = gather_weights(
        [w_in_shard.astype(MXU_DTYPE), w_out_shard.astype(MXU_DTYPE)], [ssd_conv_w[0], gdn_conv_w[0]])
    w_in_full = jnp.concatenate([g_in[k] for k in range(N_CHIP)], axis=1)
    w_out_full = g_out.reshape(N_CHIP * out_rows, D_MODEL)
    cw_ssd = jnp.concatenate([g_cs[k] for k in range(N_CHIP)], axis=1)
    cw_gdn = jnp.concatenate([g_cg[k] for k in range(N_CHIP)], axis=1)
    cb_ssd, cb_gdn = ssd_conv_b, jnp.zeros((1, GDN_CONV), F32)
    o_xbc, o_dt, o_gate, o_qkv, o_ab = 1024, 2560, 2576, 3600, 6672
    w_main = jnp.concatenate([w_in_full[:, :o_xbc], w_in_full[:, o_gate:o_qkv], w_in_full[:, o_qkv:o_ab],
                              w_in_full[:, o_xbc:o_dt]], axis=1)
    w_small = jnp.concatenate([w_in_full[:, o_dt:o_gate], w_in_full[:, o_ab:],
                               jnp.zeros((D_MODEL, LANES - 32), MXU_DTYPE)], axis=1)
    alog = _lanes(ssd_a_log, 0) + _lanes(gdn_a_log, LANE_GA)
    dtb = _lanes(ssd_dt_bias, 0) + _lanes(gdn_dt_bias, LANE_GA)
    dvec = _lanes(ssd_d, 0)
    fw = final_norm_w.reshape(1, D_MODEL)

    cw, cb = jnp.concatenate([cw_gdn, cw_ssd], axis=1), jnp.concatenate([cb_gdn, cb_ssd], axis=1)
    proj_main, proj_small, u = in_proj(xs, norm_w, w_main, w_small)
    proj_conv, conv_out = in_proj_conv(u, w_main, cw, cb)
    y_ssd, hist_ssd = ssd_fwd(conv_out, proj_main, proj_small, ssd_norm_w, alog, dtb, dvec)
    y_gdn, hist_gdn, tinv_gdn = gdn_fwd(conv_out, proj_main, proj_small, gdn_norm_w, alog, dtb)

    loss_blk, dhid, dy_ssd, dy_gdn, d_w_out, d_fw = out_proj_loss(xs, y_ssd, y_gdn, w_out_full, fw, target)
    dconv, dproj_main, dsmall_ssd, d_ssd_nw, d_alog_s, d_dtb_s, d_dvec = ssd_bwd(
        conv_out, proj_main, proj_small, ssd_norm_w, alog, dtb, dvec, hist_ssd, dy_ssd)
    dproj_main, dconv, dsmall_gdn, d_gdn_nw, d_alog_g, d_dtb_g = gdn_bwd(
        dproj_main, dconv, conv_out, proj_main, proj_small, gdn_norm_w, alog, dtb, hist_gdn, tinv_gdn, dy_gdn)
    slabs_out = d_w_out.reshape(N_DEV, out_rows // 2, D_MODEL).astype(COMM_DTYPE)
    dproj_conv, d_w_conv, dwb, (r_out,) = conv_bwd_w(u, proj_conv, cw, cb, dconv, [slabs_out])
    dwb_gdn, dwb_ssd = dwb[:, :GDN_CONV], dwb[:, GDN_CONV:]
    d_w_zg, d_w_small = in_proj_bwd_w(u, dproj_main, dsmall_ssd, dsmall_gdn)
    order = [(d_w_zg, 0, COL_GATE), (d_w_conv, COL_SSD - COL_CONV, CONV_W), (d_w_small, 0, 16),
             (d_w_zg, COL_GATE, COL_CONV), (d_w_conv, 0, COL_SSD - COL_CONV), (d_w_small, 16, 32)]
    shards, pos = [[] for _ in range(N_CHIP)], 0
    for src, lo, hi in order:
        while lo < hi:
            k = pos // in_cols
            n = min(hi - lo, (k + 1) * in_cols - pos)
            shards[k].append(src[:, lo:lo + n].astype(COMM_DTYPE))
            lo, pos = lo + n, pos + n
    slabs_in = jnp.stack([jnp.concatenate(p, axis=1) for p in shards]).reshape(N_DEV, D_MODEL // 2, in_cols)
    grad_x, d_norm_w, (r_in,) = in_proj_bwd_x(xs, norm_w, w_main, w_small, dproj_main, dproj_conv, dsmall_ssd,
                                               dsmall_gdn, dhid, [slabs_in])
    d_alog, d_dtb = d_alog_s + d_alog_g, d_dtb_s + d_dtb_g
    packed, (o_nw, o_cs, o_cg, o_snw, o_fw, o_al, o_db, o_dv, o_gnw, o_loss) = _pack_cols([
        d_norm_w, dwb_ssd, dwb_gdn,
        d_ssd_nw.reshape(1, SSD_WIDTH), d_fw, d_alog, d_dtb, d_dvec, d_gdn_nw, loss_blk])

    half_in = sum_slabs(r_in, "sum_w_in")
    half_out = sum_slabs(r_out, "sum_w_out")
    (full_in, full_out), (r_small,) = exchange_halves([half_in, half_out], [packed])
    tot = sum_slabs(r_small, "sum_small")
    grad_w_in = full_in.reshape(D_MODEL, in_cols)
    grad_w_out = full_out.reshape(out_rows, D_MODEL)
    loss = tot[0, o_loss]
    sc, gc = ssd_conv_w.shape[2], gdn_conv_w.shape[2]
    row = lambda off, n, r=0: tot[r:r + 1, off:off + n]
    gs = [row(o_nw, D_MODEL),
          lax.dynamic_slice(tot, (0, o_cs + chip * sc), (4, sc)),
          row(o_cs, SSD_CONV, 4),
          row(o_db, SSD_HEADS), row(o_al, SSD_HEADS), row(o_dv, SSD_HEADS),
          row(o_snw, SSD_WIDTH),
          lax.dynamic_slice(tot, (0, o_cg + chip * gc), (4, gc)),
          row(o_db + LANE_GA, GDN_HEADS), row(o_al + LANE_GA, GDN_HEADS),
          row(o_gnw, GDN_DV), row(o_fw, D_MODEL)]

    names = ["norm_w", "ssd_conv_w", "ssd_conv_b", "ssd_dt_bias", "ssd_a_log", "ssd_d", "ssd_norm_w", "gdn_conv_w",
             "gdn_dt_bias", "gdn_a_log", "gdn_norm_w", "final_norm_w"]
    ws = [norm_w, ssd_conv_w, ssd_conv_b, ssd_dt_bias, ssd_a_log, ssd_d, ssd_norm_w, gdn_conv_w, gdn_dt_bias,
          gdn_a_log, gdn_norm_w, final_norm_w]
    ms = [m_norm_w, m_ssd_conv_w, m_ssd_conv_b, m_ssd_dt_bias, m_ssd_a_log, m_ssd_d, m_ssd_norm_w, m_gdn_conv_w,
          m_gdn_dt_bias, m_gdn_a_log, m_gdn_norm_w, m_final_norm_w]
    vs = [v_norm_w, v_ssd_conv_w, v_ssd_conv_b, v_ssd_dt_bias, v_ssd_a_log, v_ssd_d, v_ssd_norm_w, v_gdn_conv_w,
          v_gdn_dt_bias, v_gdn_a_log, v_gdn_norm_w, v_final_norm_w]
    shapes = [w.shape for w in ws]
    flat = lambda arrs: [a.reshape(g.shape) for a, g in zip(arrs, gs)]
    d_s, m_s, v_s = adamw_many(flat(ws), gs, flat(ms), flat(vs))
    back = lambda arrs: dict(zip(names, [a.reshape(s) for a, s in zip(arrs, shapes)]))
    delta, new_m, new_v, grads = back(d_s), back(m_s), back(v_s), back(gs)
    d_in, m_in, v_in = adamw(w_in, grad_w_in, m_w_in, v_w_in, "adamw_w_in")
    d_out, m_out, v_out = adamw(w_out, grad_w_out, m_w_out, v_w_out, "adamw_w_out")
    for tbl, a_in, a_out in ((grads, grad_w_in[None], grad_w_out[None]), (delta, d_in, d_out), (new_m, m_in, m_out),
                             (new_v, v_in, v_out)):
        tbl["w_in"] = a_in
        tbl["w_out"] = a_out

    order = ["norm_w", "w_in", "ssd_conv_w", "ssd_conv_b", "ssd_dt_bias", "ssd_a_log", "ssd_d", "ssd_norm_w",
             "gdn_conv_w", "gdn_dt_bias", "gdn_a_log", "gdn_norm_w", "w_out", "final_norm_w"]
    return (loss.reshape(()), grad_x[None], *[grads[k] for k in order], *[delta[k] for k in order],
            *[new_m[k] for k in order], *[new_v[k] for k in order])
```

```python
import functools

import jax
import jax.numpy as jnp
from jax import lax
from jax.experimental import pallas as pl
from jax.experimental.pallas import tpu as pltpu

F32 = jnp.float32
MXU_DTYPE = jnp.bfloat16
COMM_DTYPE = jnp.bfloat16
MESH = pl.DeviceIdType.MESH

D_MODEL = 1024
CHUNK = 64
EPS = 1e-6
SSD_HEADS, SSD_GROUPS, SSD_STATE = 16, 2, 128
SSD_WIDTH, SSD_CONV = 1024, 1536
SSD_GW = SSD_WIDTH // SSD_GROUPS
SSD_GC = SSD_GW + 2 * SSD_STATE
GDN_HEADS, GDN_DK, GDN_DV = 8, 128, 128
GDN_W, GDN_CONV = 1024, 3072
GDN_HC = 2 * GDN_DK + GDN_DV
IN_DIM = 6688
MAIN = 6656
LANES = 128
COL_Z, COL_GATE, COL_GDN, COL_SSD = 0, 1024, 2048, 5120
COL_CONV = COL_GDN
CONV_W = MAIN - COL_CONV
GDN_HB = 8
GDN_CB = 2
SSD_CB = 2
LANE_GA, LANE_GB = 16, 24
N_DEV, N_CHIP = 8, 4
VMEM_LIMIT = 52 * 1024 * 1024

ADAM_LR, ADAM_B1, ADAM_B2, ADAM_EPS, ADAM_WD, ADAM_STEP = 0.001, 0.9, 0.999, 1e-08, 0.01, 10


def _split(a, n):
    parts, rest = [], a.astype(F32)
    for i in range(n):
        p = rest.astype(MXU_DTYPE)
        parts.append(p)
        if i < n - 1:
            rest = rest - p.astype(F32)
    return parts


def _raw_dot(a, b, ca, cb, mode="bf16"):
    d = lambda u, v: lax.dot_general(u, v, (((ca,), (cb,)), ((), ())), preferred_element_type=F32)
    if mode == "bf16":
        return d(a.astype(MXU_DTYPE), b.astype(MXU_DTYPE))
    if mode == "x3":
        (ah, al), (bh, bl) = _split(a, 2), _split(b, 2)
        return d(ah, bh) + (d(ah, bl) + d(al, bh))
    if mode == "sel_a":
        a0 = a.astype(MXU_DTYPE)
        b1, b2, b3 = _split(b, 3)
        return d(a0, b1) + (d(a0, b2) + d(a0, b3))
    assert mode == "sel_b", mode
    b0 = b.astype(MXU_DTYPE)
    a1, a2, a3 = _split(a, 3)
    return d(a1, b0) + (d(a2, b0) + d(a3, b0))


@functools.partial(jax.custom_vjp, nondiff_argnums=(2,))
def mm_nn(a, b, mode="bf16"):
    return _raw_dot(a, b, 1, 0, mode)


@functools.partial(jax.custom_vjp, nondiff_argnums=(2,))
def mm_nt(a, b, mode="bf16"):
    return _raw_dot(a, b, 1, 1, mode)


@functools.partial(jax.custom_vjp, nondiff_argnums=(2,))
def mm_tn(a, b, mode="bf16"):
    return _raw_dot(a, b, 0, 0, mode)


_SAME = {"bf16": ("bf16", "bf16"), "x3": ("x3", "x3")}
_NN_BWD = dict(_SAME, sel_a=("bf16", "sel_a"), sel_b=("sel_b", "bf16"))
_NT_BWD = dict(_SAME, sel_a=("bf16", "sel_b"), sel_b=("sel_b", "bf16"))
_TN_BWD = dict(_SAME, sel_a=("bf16", "sel_a"), sel_b=("sel_a", "bf16"))
mm_nn.defvjp(lambda a, b, m: (_raw_dot(a, b, 1, 0, m), (a, b)),
             lambda m, r, g: (mm_nt(g, r[1], _NN_BWD[m][0]), mm_tn(r[0], g, _NN_BWD[m][1])))
mm_nt.defvjp(lambda a, b, m: (_raw_dot(a, b, 1, 1, m), (a, b)),
             lambda m, r, g: (mm_nn(g, r[1], _NT_BWD[m][0]), mm_tn(g, r[0], _NT_BWD[m][1])))
mm_tn.defvjp(lambda a, b, m: (_raw_dot(a, b, 0, 0, m), (a, b)),
             lambda m, r, g: (mm_nt(r[1], g, _TN_BWD[m][0]), mm_nn(r[0], g, _TN_BWD[m][1])))


@jax.custom_jvp
def sigmoid(x):
    return 1.0 / (1.0 + jnp.exp(-x))


@sigmoid.defjvp
def _sigmoid_jvp(p, t):
    s = sigmoid(p[0])
    return s, t[0] * s * (1.0 - s)


@jax.custom_jvp
def softplus(x):
    return jnp.maximum(x, 0.0) + jnp.log(1.0 + jnp.exp(-jnp.abs(x)))


@softplus.defjvp
def _softplus_jvp(p, t):
    return softplus(p[0]), t[0] * sigmoid(p[0])


def silu(x):
    return x * sigmoid(x)


def rmsnorm(x, w):
    return x * lax.rsqrt(jnp.mean(x * x, axis=-1, keepdims=True) + EPS) * w


def _iota(shape, dim):
    return lax.broadcasted_iota(jnp.int32, shape, dim)


def _halves():
    lane = _iota((1, LANES), 1) >> 6
    return _ind(lane == 0), _ind(lane == 1)


def _block_diag(pair):
    h0, h1 = _halves()
    return jnp.concatenate([pair * h0, pair * h1], axis=0)


def _tri_inv_impl(mats):
    r, c = _iota((CHUNK, LANES), 0), _iota((CHUNK, LANES), 1) & (CHUNK - 1)
    eye = _ind(r == c)
    blockdiag = _ind((r >> 4) == (c >> 4))
    dot = lambda u, v: _raw_dot(u, _block_diag(v), 1, 0, "x3")
    dot1 = lambda u, v: _raw_dot(u, _block_diag(v), 1, 0)
    each = lambda f, *ls: [f(*xs) for xs in zip(*ls)]
    dg = each(lambda a: a * blockdiag, mats)
    off = each(lambda a, d: a - d, mats, dg)
    m = each(lambda d: -d, dg)
    p = each(lambda x: eye + x, m)
    pw = m
    for _ in range(3):
        pw = each(lambda x: dot1(x, x), pw)
        p = each(lambda x, y: x + dot1(x, y), p, pw)
    e = each(dot, p, off)
    e2 = each(lambda x: dot1(x, x), e)
    q = each(lambda x: eye - x, e)
    q = each(lambda x, y: x + dot1(x, y), q, e2)
    return each(dot, q, p)


def _tri_inv_bwd(ts, gs):
    h0, h1 = _halves()
    x = [mm_nt(g, _block_diag(t)) for g, t in zip(gs, ts)]
    full = [mm_tn(t, y) for t, y in zip(ts, x)]
    return [-(f[:CHUNK] * h0 + f[CHUNK:] * h1) for f in full]


@jax.custom_vjp
def tri_inv(mats):
    return _tri_inv_impl(mats)


def _tri_inv_fwd(mats):
    ts = _tri_inv_impl(mats)
    return ts, ts


tri_inv.defvjp(_tri_inv_fwd, lambda ts, gs: (_tri_inv_bwd(ts, gs),))


@jax.custom_vjp
def tri_inv_saved(mats, ts):
    del mats
    return ts


tri_inv_saved.defvjp(lambda mats, ts: (ts, ts),
                     lambda ts, gs: (_tri_inv_bwd(ts, gs), [jnp.zeros_like(t) for t in ts]))


def _ind(cond):
    return jnp.where(cond, 1.0, 0.0).astype(F32)


def _chunk_masks():
    r, c = _iota((CHUNK, CHUNK), 0), _iota((CHUNK, CHUNK), 1)
    return _ind(r >= c), _ind(r > c), _ind(r == c), _ind(_iota((CHUNK, 1), 0) == CHUNK - 1)


def _log_decay_cumsum(small, alog, dtb, tri):
    sp = softplus(small + dtb)
    la = -jnp.exp(alog) * sp
    return sp, mm_nn(tri, la, "sel_a")


def _col_of(x, lane):
    return jnp.sum(x * _ind(_iota((1, LANES), 1) == lane), axis=1, keepdims=True)


def _decay_matrix(col, tri, eye):
    row = jnp.sum(col * eye, axis=0, keepdims=True)
    return jnp.exp((col - row) * tri) * tri


def _pair_masks():
    r, c = _iota((CHUNK, LANES), 0), _iota((CHUNK, LANES), 1)
    c6 = c & (CHUNK - 1)
    return _ind(r >= c6), _ind(r > c6), (_ind(c == r), _ind(c == r + CHUNK))


def _decay_pair(col_a, col_b, tri_w, eye_w):
    h0, h1 = _halves()
    col = col_a * h0 + col_b * h1
    row = jnp.sum(col_a * eye_w[0] + col_b * eye_w[1], axis=0, keepdims=True)
    return jnp.exp((col - row) * tri_w) * tri_w


def gdn_chunk(h0, qs, ks, vs, smalls, gates, normw, alog, dtb, states, saved_t=None):
    tri, _, _, last = _chunk_masks()
    tri_w, strict_w, eye_w = _pair_masks()
    nh = len(qs[0])
    flat = lambda xss: [x for xs in xss for x in xs]
    lacs = [_log_decay_cumsum(sm, alog, dtb, tri)[1] for sm in smalls]
    qs, ks, vs, gates = flat(qs), flat(ks), flat(vs), flat(gates)
    heads, pairs = range(len(qs)), range(len(qs) // 2)
    each = lambda f, *ls: [f(*xs) for xs in zip(*ls)]
    ab = lambda xs, p: (xs[2 * p], xs[2 * p + 1])
    stack = lambda xs: jnp.concatenate(xs, axis=0)
    gc = [_col_of(lacs[i // nh], LANE_GA + h0 + i % nh) for i in heads]
    beta = [sigmoid(_col_of(smalls[i // nh], LANE_GB + h0 + i % nh)) for i in heads]
    decay = [_decay_pair(*ab(gc, p), tri_w, eye_w) for p in pairs]
    gl = each(lambda x: jnp.sum(x * last, axis=0, keepdims=True), gc)
    q = each(lambda x: x * lax.rsqrt(jnp.sum(x * x, axis=-1, keepdims=True) + EPS) * (GDN_DK ** -0.5), qs)
    k = each(lambda x: x * lax.rsqrt(jnp.sum(x * x, axis=-1, keepdims=True) + EPS), ks)
    kb = each(lambda x, b: x * b, k, beta)
    eg = each(jnp.exp, gc)
    zero = jnp.zeros((CHUNK, GDN_DK), F32)
    k_bd = [stack([join_lanes([k[2 * p], zero]), join_lanes([zero, k[2 * p + 1]])]) for p in pairs]
    a = [mm_nt(join_lanes(list(ab(kb, p))), k_bd[p]) * (decay[p] * strict_w) for p in pairs]
    t = tri_inv(a) if saved_t is None else tri_inv_saved(a, saved_t)
    attn = [mm_nt(join_lanes(list(ab(q, p))), k_bd[p]) * decay[p] for p in pairs]
    rhs = [stack([join_lanes([vs[h] * beta[h], kb[h] * eg[h]]) for h in (2 * p, 2 * p + 1)]) for p in pairs]
    uw = [mm_nn(_block_diag(t[p]), rhs[p]) for p in pairs]
    uw = [x for p in pairs for x in split_rows(uw[p])]
    u, w = zip(*[split_lanes(x) for x in uw])
    ys = []
    for c in range(len(smalls)):
        hs = range(c * nh, (c + 1) * nh)
        v_new = [u[i] - mm_nn(w[i], states[i % nh]) for i in hs]
        av = [mm_nn(_block_diag(attn[c * nh // 2 + p]), stack(list(ab(v_new, p)))) for p in range(nh // 2)]
        av = [x for y in av for x in split_rows(y)]
        o = [mm_nn(q[i] * eg[i], states[i % nh]) + av[i % nh] for i in hs]
        states = [states[i % nh] * jnp.exp(gl[i]) + mm_tn(k[i] * jnp.exp(gl[i] - gc[i]), v_new[i % nh]) for i in hs]
        ys.append([rmsnorm(o[i % nh], normw) * silu(gates[i]) for i in hs])
    return ys, states, t


@jax.custom_vjp
def split_rows(x):
    n = x.shape[0] // 2
    return [x[:n], x[n:]]


split_rows.defvjp(lambda x: (split_rows(x), None), lambda _, gs: (jnp.concatenate(gs, axis=0),))


@jax.custom_vjp
def split_lanes(x):
    return [x[:, i * LANES:(i + 1) * LANES] for i in range(x.shape[1] // LANES)]


@jax.custom_vjp
def join_lanes(xs):
    return jnp.concatenate(xs, axis=1)


split_lanes.defvjp(lambda x: (split_lanes(x), None), lambda _, gs: (join_lanes(gs),))
join_lanes.defvjp(lambda xs: (join_lanes(xs), None), lambda _, g: (split_lanes(g),))


def ssd_chunk(xs, bm, cm, z, smalls, normw, alog, dtb, dvec, state):
    tri, _, _, last = _chunk_masks()
    tri_w, _, eye_w = _pair_masks()
    h0, h1 = _halves()
    hpg = SSD_HEADS // SSD_GROUPS
    ng = len(normw)
    flat = lambda xss: [x for xs_ in xss for x in xs_]
    xs, bm, cm, z = flat(xs), flat(bm), flat(cm), flat(z)
    units, pairs = range(len(xs)), range(hpg // 2)
    each = lambda f, *ls: [f(*a) for a in zip(*ls)]
    sp_lac = [_log_decay_cumsum(sm, alog, dtb, tri) for sm in smalls]
    lac_last = [jnp.sum(lac * last, axis=0, keepdims=True) for _, lac in sp_lac]
    sel = [_ind(_iota((LANES, SSD_GW), 0) == g * hpg + (_iota((LANES, SSD_GW), 1) >> 6)) for g in range(ng)]
    expand = lambda vs, mode: [mm_nn(vs[i // ng], sel[i % ng], mode) for i in units]
    dt_e = expand([sp for sp, _ in sp_lac], "bf16")
    elac_e = expand([jnp.exp(lac) for _, lac in sp_lac], "bf16")
    toend_e = expand([jnp.exp(ll - lac) for (_, lac), ll in zip(sp_lac, lac_last)], "bf16")
    row8 = _iota((8, 1), 0)
    two_e = expand([_ind(row8 == 0) * dvec + _ind(row8 == 1) * jnp.exp(ll) for ll in lac_last], "sel_b")
    d_e = each(lambda v: jnp.sum(v * _ind(row8 == 0), axis=0, keepdims=True), two_e)
    chunk_e = each(lambda v: jnp.sum(v * _ind(row8 == 1), axis=0, keepdims=True), two_e)
    xdt = each(lambda a, b: a * b, xs, dt_e)
    cb_w = each(lambda c_, b_: mm_nt(c_, jnp.concatenate([b_, b_], axis=0)), cm, bm)
    x_pairs = each(split_lanes, xdt)
    col = lambda i, j: _col_of(sp_lac[i // ng][1], (i % ng) * hpg + j)
    lms = [[_decay_pair(col(i, 2 * p), col(i, 2 * p + 1), tri_w, eye_w) for p in pairs] for i in units]
    stacked = [[jnp.concatenate([x_pairs[i][p] * h0, x_pairs[i][p] * h1], axis=0) for p in pairs] for i in units]
    terms = [[mm_nn(cb_w[i] * lms[i][p], stacked[i][p]) for p in pairs] for i in units]
    y_in = [join_lanes(terms[i]) + xs[i] * d_e[i] for i in units]
    state_in = each(lambda b_, xd, te: mm_tn(b_, xd * te), bm, xdt, toend_e)
    outs = []
    for c in range(len(smalls)):
        us = range(c * ng, (c + 1) * ng)
        y = [mm_nn(cm[i], state[i % ng]) * elac_e[i] + y_in[i] for i in us]
        state = [state[i % ng] * chunk_e[i] + state_in[i] for i in us]
        outs.append([rmsnorm(y[i % ng] * silu(z[i]), normw[i % ng]) for i in us])
    return outs, state


def _params(sem=None):
    return pltpu.CompilerParams(dimension_semantics=sem, vmem_limit_bytes=VMEM_LIMIT)


def _full(shape):
    n = len(shape)
    return pl.BlockSpec(shape, lambda *_: (0,) * n)


ANY = pl.BlockSpec(memory_space=pl.ANY)
HBM = pl.BlockSpec(memory_space=pltpu.HBM)


def in_proj(x, normw, w_main, w_small):
    t = x.shape[0]
    tm, tn = min(1024, t), 512

    def body(x_ref, nw_ref, wm_ref, ws_ref, pm_ref, ps_ref, u_ref):
        @pl.when(pl.program_id(1) == 0)
        def _():
            u = rmsnorm(x_ref[...], nw_ref[...]).astype(MXU_DTYPE)
            u_ref[...] = u
            ps_ref[...] = _raw_dot(u, ws_ref[...], 1, 0)
        pm_ref[...] = _raw_dot(u_ref[...], wm_ref[...], 1, 0)

    return pl.pallas_call(
        body, name="in_proj", grid=(t // tm, COL_CONV // tn),
        in_specs=[pl.BlockSpec((tm, D_MODEL), lambda i, j: (i, 0)), _full((1, D_MODEL)),
                  pl.BlockSpec((D_MODEL, tn), lambda i, j: (0, j)), _full((D_MODEL, LANES))],
        out_specs=[pl.BlockSpec((tm, tn), lambda i, j: (i, j)), pl.BlockSpec((tm, LANES), lambda i, j: (i, 0)),
                   pl.BlockSpec((tm, D_MODEL), lambda i, j: (i, 0))],
        out_shape=[jax.ShapeDtypeStruct((t, COL_CONV), F32), jax.ShapeDtypeStruct((t, LANES), F32),
                   jax.ShapeDtypeStruct((t, D_MODEL), MXU_DTYPE)],
        compiler_params=_params(("arbitrary", "arbitrary")),
    )(x, normw, w_main, w_small)


CONV_TC = 512
HALO = 8


def _shift_down(cur, prev, s):
    rolled = pltpu.roll(cur, s, 0)
    top = jnp.where(_iota((HALO, cur.shape[1]), 0) < s, pltpu.roll(prev, s, 0), rolled[:HALO])
    if cur.shape[0] == HALO:
        return top
    return jnp.concatenate([top, rolled[HALO:]], axis=0)


def _shift_up(cur, nxt, s):
    n = cur.shape[0]
    rolled = pltpu.roll(cur, n - s, 0)
    bot = jnp.where(_iota((HALO, cur.shape[1]), 0) >= HALO - s, pltpu.roll(nxt, HALO - s, 0), rolled[n - HALO:])
    return jnp.concatenate([rolled[:n - HALO], bot], axis=0)


def _conv_pre(cur, prev, w_ref, b, cols=slice(None)):
    acc = cur * w_ref[3:4, cols] + b
    shifted = [cur]
    for s in (1, 2, 3):
        sh = _shift_down(cur, prev, s)
        shifted.append(sh)
        acc = acc + sh * w_ref[3 - s:4 - s, cols]
    return acc, shifted


def in_proj_conv(u, w_main, w, b):
    t = u.shape[0]
    tm, tn = min(2048, t), CONV_TC
    rc = min(256, tm)
    c0, nj = COL_CONV // tn, CONV_W // tn

    def body(u_ref, wm_ref, w_ref, b_ref, pm_ref, out_ref, halo_ref):
        j = pl.program_id(1)

        @pl.when(pl.program_id(0) == 0)
        def _():
            halo_ref[j] = jnp.zeros((HALO, tn), F32)

        prev = halo_ref[j]
        for r in range(tm // rc):
            rows = pl.ds(r * rc, rc)
            p = _raw_dot(u_ref[rows, :], wm_ref[...], 1, 0)
            pm_ref[rows, :] = p
            pre, _ = _conv_pre(p, prev, w_ref, b_ref[...])
            out_ref[rows, :] = silu(pre)
            prev = p[rc - HALO:]
        halo_ref[j] = prev

    return pl.pallas_call(
        body, name="in_proj_conv", grid=(t // tm, nj),
        in_specs=[pl.BlockSpec((tm, D_MODEL), lambda i, j: (i, 0)),
                  pl.BlockSpec((D_MODEL, tn), lambda i, j: (0, c0 + j)),
                  pl.BlockSpec((4, tn), lambda i, j: (0, j)), pl.BlockSpec((1, tn), lambda i, j: (0, j))],
        out_specs=[pl.BlockSpec((tm, tn), lambda i, j: (i, j)), pl.BlockSpec((tm, tn), lambda i, j: (i, j))],
        out_shape=[jax.ShapeDtypeStruct((t, CONV_W), F32), jax.ShapeDtypeStruct((t, CONV_W), F32)],
        scratch_shapes=[pltpu.VMEM((nj, HALO, tn), F32)],
        compiler_params=_params(("arbitrary", "arbitrary")),
    )(u, w_main, w, b)


def _dsilu(pre):
    sg = sigmoid(pre)
    return sg * (1.0 + pre * (1.0 - sg))


def conv_bwd_w(u, proj_conv, w, b, dout, slabbed):
    t = u.shape[0]
    tt, tn = min(512, t), 3 * CONV_TC
    nt, nj = t // tt, CONV_W // tn
    ns = len(slabbed)
    after = lambda i: jnp.minimum((i + 1) * (tt // HALO), t // HALO - 1)

    def body(u_ref, cur_ref, prev_ref, nxt_ref, w_ref, b_ref, do_ref, do_nxt_ref, *rest):
        slab_refs, (dx_ref, dw_ref, dwb_ref) = rest[:ns], rest[ns:ns + 3]
        land_refs, sems = rest[ns + 3:2 * ns + 3], rest[2 * ns + 3:]
        j, i = pl.program_id(0), pl.program_id(1)
        start, finish = _slab_exchange(slab_refs, land_refs, ns, *sems)

        @pl.when(jnp.logical_and(j == 0, i == 0))
        def _():
            start()

        @pl.when(i == 0)
        def _():
            dw_ref[...] = jnp.zeros(dw_ref.shape, F32)
            dwb_ref[...] = jnp.zeros(dwb_ref.shape, F32)

        uu = u_ref[...]
        row = _iota((HALO, CONV_TC), 0)
        first, last = i == 0, i == nt - 1
        for piece in range(tn // CONV_TC):
            cols = slice(piece * CONV_TC, (piece + 1) * CONV_TC)
            cur, bias = cur_ref[:, cols], b_ref[:, cols]
            prev = jnp.where(first, 0.0, prev_ref[:, cols])
            pre, shifted = _conv_pre(cur, prev, w_ref, bias, cols)
            dpre = do_ref[:, cols] * _dsilu(pre)
            pre_nxt, _ = _conv_pre(nxt_ref[:, cols], cur[tt - HALO:], w_ref, bias, cols)
            dpre_nxt = jnp.where(last, 0.0, do_nxt_ref[:, cols] * _dsilu(pre_nxt))
            dx = dpre * w_ref[3:4, cols]
            for s in (1, 2, 3):
                dx = dx + _shift_up(dpre, dpre_nxt, s) * w_ref[3 - s:4 - s, cols]
            dx = dx.astype(dx_ref.dtype)
            dx_ref[:, cols] = dx
            dw_ref[:, cols] += _raw_dot(uu, dx, 0, 0)
            upd = jnp.where(row == 4, jnp.sum(dpre, axis=0, keepdims=True), 0.0)
            for s in range(4):
                upd = upd + jnp.where(row == 3 - s, jnp.sum(dpre * shifted[s], axis=0, keepdims=True), 0.0)
            dwb_ref[:, cols] += upd

        @pl.when(jnp.logical_and(j == nj - 1, last))
        def _():
            finish()

    out = pl.pallas_call(
        body, name="conv_bwd_w", grid=(nj, nt),
        in_specs=[pl.BlockSpec((tt, D_MODEL), lambda j, i: (i, 0)),
                  pl.BlockSpec((tt, tn), lambda j, i: (i, j)),
                  pl.BlockSpec((HALO, tn), lambda j, i: (jnp.maximum(i * (tt // HALO) - 1, 0), j)),
                  pl.BlockSpec((HALO, tn), lambda j, i: (after(i), j)),
                  pl.BlockSpec((4, tn), lambda j, i: (0, j)), pl.BlockSpec((1, tn), lambda j, i: (0, j)),
                  pl.BlockSpec((tt, tn), lambda j, i: (i, j)),
                  pl.BlockSpec((HALO, tn), lambda j, i: (after(i), j))] + [HBM] * ns,
        out_specs=[pl.BlockSpec((tt, tn), lambda j, i: (i, j)), pl.BlockSpec((D_MODEL, tn), lambda j, i: (0, j)),
                   pl.BlockSpec((HALO, tn), lambda j, i: (0, j))] + [HBM] * ns,
        out_shape=[jax.ShapeDtypeStruct((t, CONV_W), MXU_DTYPE), jax.ShapeDtypeStruct((D_MODEL, CONV_W), F32),
                   jax.ShapeDtypeStruct((HALO, CONV_W), F32)] + _slab_exchange_shapes(slabbed, []),
        scratch_shapes=_slab_exchange_sems(ns),
        compiler_params=_params(("arbitrary", "arbitrary")),
    )(u, proj_conv, proj_conv, proj_conv, w, b, dout, dout, *slabbed)
    return out[0], out[1], out[2], out[3:]


def _ssd_cols(g):
    b0 = SSD_WIDTH + g * SSD_STATE
    c0 = SSD_WIDTH + SSD_GROUPS * SSD_STATE + g * SSD_STATE
    return slice(g * SSD_GW, (g + 1) * SSD_GW), slice(b0, b0 + SSD_STATE), slice(c0, c0 + SSD_STATE)


def _gdn_cols(j):
    return tuple(slice(s * GDN_W + j * GDN_DK, s * GDN_W + (j + 1) * GDN_DK) for s in range(3))


def _ssd_parts(xbc_ref):
    return tuple([[xbc_ref[_chunk_rows(c), _ssd_cols(g)[s]] for g in range(SSD_GROUPS)] for c in range(SSD_CB)]
                 for s in range(3))


def _group_cols(ref):
    return [[ref[_chunk_rows(c), g * SSD_GW:(g + 1) * SSD_GW] for g in range(SSD_GROUPS)] for c in range(SSD_CB)]


def _chunk_rows(c):
    return slice(c * CHUNK, (c + 1) * CHUNK)


def _gdn_parts(qkv_ref):
    assert GDN_HB == GDN_HEADS, "the conv block is read whole: one grid step holds every head"
    return tuple([[qkv_ref[_chunk_rows(c), _gdn_cols(j)[s]] for j in range(GDN_HB)] for c in range(GDN_CB)]
                 for s in range(3))


def _head_cols(ref):
    return [[ref[_chunk_rows(c), j * GDN_DV:(j + 1) * GDN_DV] for j in range(GDN_HB)] for c in range(GDN_CB)]


def _chunk_blocks(ref, n=GDN_CB):
    return [ref[_chunk_rows(c), :] for c in range(n)]


def _first_head():
    return 0 if GDN_HB == GDN_HEADS else pl.program_id(1) * GDN_HB


def ssd_fwd(conv_ssd, proj_main, proj_small, normw, alog, dtb, dvec):
    t = conv_ssd.shape[0]
    rows = CHUNK * SSD_CB
    nc = t // rows
    groups = range(SSD_GROUPS)
    norm_cols = lambda ref: [ref[:, g * SSD_GW:(g + 1) * SSD_GW] for g in groups]

    def body(xbc_ref, z_ref, sm_ref, nw_ref, al_ref, db_ref, dv_ref, y_ref, hist_ref, state_ref):
        @pl.when(pl.program_id(0) == 0)
        def _():
            state_ref[...] = jnp.zeros(state_ref.shape, F32)

        states = [state_ref[g] for g in groups]
        for g in groups:
            hist_ref[0, g] = states[g]
        ys, new_states = ssd_chunk(*_ssd_parts(xbc_ref), _group_cols(z_ref), _chunk_blocks(sm_ref, SSD_CB),
                                   norm_cols(nw_ref), al_ref[...], db_ref[...], dv_ref[...], states)
        for c in range(SSD_CB):
            for g in groups:
                y_ref[_chunk_rows(c), g * SSD_GW:(g + 1) * SSD_GW] = ys[c][g].astype(MXU_DTYPE)
        for g in groups:
            state_ref[g] = new_states[g]

    return pl.pallas_call(
        body, name="ssd_fwd", grid=(nc,),
        in_specs=[pl.BlockSpec((rows, SSD_CONV), lambda c: (c, (COL_SSD - COL_CONV) // SSD_CONV)),
                  pl.BlockSpec((rows, SSD_WIDTH), lambda c: (c, COL_Z // SSD_WIDTH)),
                  pl.BlockSpec((rows, LANES), lambda c: (c, 0)),
                  _full((1, SSD_WIDTH)), _full((1, LANES)), _full((1, LANES)), _full((1, LANES))],
        out_specs=[pl.BlockSpec((rows, SSD_WIDTH), lambda c: (c, 0)),
                   pl.BlockSpec((1, SSD_GROUPS, SSD_STATE, SSD_GW), lambda c: (c, 0, 0, 0))],
        out_shape=[jax.ShapeDtypeStruct((t, SSD_WIDTH), MXU_DTYPE),
                   jax.ShapeDtypeStruct((nc, SSD_GROUPS, SSD_STATE, SSD_GW), F32)],
        scratch_shapes=[pltpu.VMEM((SSD_GROUPS, SSD_STATE, SSD_GW), F32)],
        compiler_params=_params(("arbitrary",)),
    )(conv_ssd, proj_main, proj_small, normw, alog, dtb, dvec)


def _accumulate(ref, first, value):
    @pl.when(first)
    def _():
        ref[...] = value

    @pl.when(jnp.logical_not(first))
    def _():
        ref[...] += value


def ssd_bwd(conv_ssd, proj_main, proj_small, normw, alog, dtb, dvec, hist, dy):
    t = conv_ssd.shape[0]
    rows = CHUNK * SSD_CB
    nc = t // rows
    rev = lambda c: nc - 1 - c
    groups = range(SSD_GROUPS)
    norm_cols = lambda ref: [ref[:, g * SSD_GW:(g + 1) * SSD_GW] for g in groups]

    def body(xbc_ref, z_ref, sm_ref, nw_ref, al_ref, db_ref, dv_ref, hist_ref, dy_ref,
             dxbc_ref, dz_ref, dsm_ref, dnw_ref, dal_ref, ddb_ref, ddv_ref, dstate_ref):
        first = pl.program_id(0) == 0

        @pl.when(first)
        def _():
            dstate_ref[...] = jnp.zeros(dstate_ref.shape, F32)

        _, vjp = jax.vjp(ssd_chunk, *_ssd_parts(xbc_ref), _group_cols(z_ref), _chunk_blocks(sm_ref, SSD_CB),
                         norm_cols(nw_ref), al_ref[...], db_ref[...], dv_ref[...], [hist_ref[0, g] for g in groups])
        dxs, dbm, dcm, dz, dsm, dnw, dal, ddb, ddv, dstate = vjp(
            (_group_cols(dy_ref), [dstate_ref[g] for g in groups]))
        for k in range(SSD_CB):
            rk = _chunk_rows(k)
            for g in groups:
                xc, bc, cc = _ssd_cols(g)
                dxbc_ref[rk, xc] = dxs[k][g]
                dxbc_ref[rk, bc] = dbm[k][g]
                dxbc_ref[rk, cc] = dcm[k][g]
                dz_ref[rk, g * SSD_GW:(g + 1) * SSD_GW] = dz[k][g].astype(dz_ref.dtype)
            dsm_ref[rk, :] = dsm[k]
        for g in groups:
            dstate_ref[g] = dstate[g]
        _accumulate(dnw_ref, first, join_lanes(dnw))
        _accumulate(dal_ref, first, dal)
        _accumulate(ddb_ref, first, ddb)
        _accumulate(ddv_ref, first, ddv)

    return pl.pallas_call(
        body, name="ssd_bwd", grid=(nc,),
        in_specs=[pl.BlockSpec((rows, SSD_CONV), lambda c: (rev(c), (COL_SSD - COL_CONV) // SSD_CONV)),
                  pl.BlockSpec((rows, SSD_WIDTH), lambda c: (rev(c), COL_Z // SSD_WIDTH)),
                  pl.BlockSpec((rows, LANES), lambda c: (rev(c), 0)),
                  _full((1, SSD_WIDTH)), _full((1, LANES)), _full((1, LANES)), _full((1, LANES)),
                  pl.BlockSpec((1, SSD_GROUPS, SSD_STATE, SSD_GW), lambda c: (rev(c), 0, 0, 0)),
                  pl.BlockSpec((rows, SSD_WIDTH), lambda c: (rev(c), 0))],
        out_specs=[pl.BlockSpec((rows, SSD_CONV), lambda c: (rev(c), (COL_SSD - COL_CONV) // SSD_CONV)),
                   pl.BlockSpec((rows, SSD_WIDTH), lambda c: (rev(c), COL_Z // SSD_WIDTH)),
                   pl.BlockSpec((rows, LANES), lambda c: (rev(c), 0)),
                   _full((1, SSD_WIDTH)), _full((1, LANES)), _full((1, LANES)), _full((1, LANES))],
        out_shape=[jax.ShapeDtypeStruct((t, CONV_W), F32), jax.ShapeDtypeStruct((t, COL_CONV), MXU_DTYPE),
                   jax.ShapeDtypeStruct((t, LANES), F32), jax.ShapeDtypeStruct((1, SSD_WIDTH), F32),
                   jax.ShapeDtypeStruct((1, LANES), F32), jax.ShapeDtypeStruct((1, LANES), F32),
                   jax.ShapeDtypeStruct((1, LANES), F32)],
        scratch_shapes=[pltpu.VMEM((SSD_GROUPS, SSD_STATE, SSD_GW), F32)],
        compiler_params=_params(("arbitrary",)),
    )(conv_ssd, proj_main, proj_small, normw, alog, dtb, dvec, hist, dy)


def gdn_fwd(conv_gdn, proj_main, proj_small, normw, alog, dtb):
    t = conv_gdn.shape[0]
    hb, cb = GDN_HB, GDN_CB
    rows = CHUNK * cb
    ns = t // rows
    gate_blk = COL_GATE // (GDN_DV * hb)

    def body(qkv_ref, gate_ref, sm_ref, nw_ref, al_ref, db_ref, y_ref, hist_ref, t_ref, state_ref):
        h0 = _first_head()

        @pl.when(pl.program_id(0) == 0)
        def _():
            for j in range(hb):
                state_ref[h0 + j] = jnp.zeros((GDN_DK, GDN_DV), F32)

        states = [state_ref[h0 + j] for j in range(hb)]
        for j in range(hb):
            hist_ref[0, j] = states[j]
        qs, ks, vs = _gdn_parts(qkv_ref)
        ys, new_states, ts = gdn_chunk(h0, qs, ks, vs, _chunk_blocks(sm_ref), _head_cols(gate_ref), nw_ref[...],
                                       al_ref[...], db_ref[...], states)
        for c in range(cb):
            for j in range(hb):
                y_ref[_chunk_rows(c), j * GDN_DV:(j + 1) * GDN_DV] = ys[c][j].astype(MXU_DTYPE)
        for j in range(hb):
            state_ref[h0 + j] = new_states[j]
        for p in range(cb * hb // 2):
            t_ref[0, p] = ts[p]

    return pl.pallas_call(
        body, name="gdn_fwd", grid=(ns, GDN_HEADS // hb),
        in_specs=[pl.BlockSpec((rows, GDN_HC * hb), lambda c, h: (c, h)),
                  pl.BlockSpec((rows, GDN_DV * hb), lambda c, h: (c, gate_blk + h)),
                  pl.BlockSpec((rows, LANES), lambda c, h: (c, 0)),
                  _full((1, GDN_DV)), _full((1, LANES)), _full((1, LANES))],
        out_specs=[pl.BlockSpec((rows, GDN_DV * hb), lambda c, h: (c, h)),
                   pl.BlockSpec((1, hb, GDN_DK, GDN_DV), lambda c, h: (c, h, 0, 0)),
                   pl.BlockSpec((1, cb * hb // 2, CHUNK, LANES), lambda c, h: (c, h, 0, 0))],
        out_shape=[jax.ShapeDtypeStruct((t, GDN_W), MXU_DTYPE),
                   jax.ShapeDtypeStruct((ns, GDN_HEADS, GDN_DK, GDN_DV), F32),
                   jax.ShapeDtypeStruct((ns, cb * GDN_HEADS // 2, CHUNK, LANES), F32)],
        scratch_shapes=[pltpu.VMEM((GDN_HEADS, GDN_DK, GDN_DV), F32)],
        compiler_params=_params(("arbitrary", "arbitrary")),
    )(conv_gdn, proj_main, proj_small, normw, alog, dtb)


def gdn_bwd(dproj_main, dconv, conv_gdn, proj_main, proj_small, normw, alog, dtb, hist, t_inv, dy):
    t = conv_gdn.shape[0]
    hb, cb = GDN_HB, GDN_CB
    rows = CHUNK * cb
    ns = t // rows
    rev = lambda c: ns - 1 - c
    gate_blk = COL_GATE // (GDN_DV * hb)

    def body(alias_ref, alias2_ref, qkv_ref, gate_ref, sm_ref, nw_ref, al_ref, db_ref, hist_ref, t_ref, dy_ref,
             dgate_ref, dqkv_ref, dsm_ref, dnw_ref, dal_ref, ddb_ref, dstate_ref):
        del alias_ref, alias2_ref
        c, h = pl.program_id(0), pl.program_id(1)
        h0 = _first_head()

        @pl.when(c == 0)
        def _():
            for j in range(hb):
                dstate_ref[h0 + j] = jnp.zeros((GDN_DK, GDN_DV), F32)

        saved = [t_ref[0, p] for p in range(cb * hb // 2)]

        def fn(qs, ks, vs, smalls, gates, nw, al, db, states):
            return gdn_chunk(h0, qs, ks, vs, smalls, gates, nw, al, db, states, saved)[:2]

        qs, ks, vs = _gdn_parts(qkv_ref)
        _, vjp = jax.vjp(fn, qs, ks, vs, _chunk_blocks(sm_ref), _head_cols(gate_ref), nw_ref[...], al_ref[...],
                         db_ref[...], [hist_ref[0, j] for j in range(hb)])
        dqs, dks, dvs, dsm, dgates, dnw, dal, ddb, dstates = vjp(
            (_head_cols(dy_ref), [dstate_ref[h0 + j] for j in range(hb)]))
        for k in range(cb):
            rk = _chunk_rows(k)
            for j in range(hb):
                qc, kc, vc = _gdn_cols(j)
                dqkv_ref[rk, qc] = dqs[k][j]
                dqkv_ref[rk, kc] = dks[k][j]
                dqkv_ref[rk, vc] = dvs[k][j]
                dgate_ref[rk, j * GDN_DV:(j + 1) * GDN_DV] = dgates[k][j].astype(dgate_ref.dtype)
        for j in range(hb):
            dstate_ref[h0 + j] = dstates[j]
        _accumulate(dsm_ref, h == 0, jnp.concatenate(dsm, axis=0))
        first = jnp.logical_and(c == 0, h == 0)
        _accumulate(dnw_ref, first, dnw)
        _accumulate(dal_ref, first, dal)
        _accumulate(ddb_ref, first, ddb)

    return pl.pallas_call(
        body, name="gdn_bwd", grid=(ns, GDN_HEADS // hb),
        in_specs=[ANY, ANY, pl.BlockSpec((rows, GDN_HC * hb), lambda c, h: (rev(c), h)),
                  pl.BlockSpec((rows, GDN_DV * hb), lambda c, h: (rev(c), gate_blk + h)),
                  pl.BlockSpec((rows, LANES), lambda c, h: (rev(c), 0)),
                  _full((1, GDN_DV)), _full((1, LANES)), _full((1, LANES)),
                  pl.BlockSpec((1, hb, GDN_DK, GDN_DV), lambda c, h: (rev(c), h, 0, 0)),
                  pl.BlockSpec((1, cb * hb // 2, CHUNK, LANES), lambda c, h: (rev(c), h, 0, 0)),
                  pl.BlockSpec((rows, GDN_DV * hb), lambda c, h: (rev(c), h))],
        out_specs=[pl.BlockSpec((rows, GDN_DV * hb), lambda c, h: (rev(c), gate_blk + h)),
                   pl.BlockSpec((rows, GDN_HC * hb), lambda c, h: (rev(c), h)),
                   pl.BlockSpec((rows, LANES), lambda c, h: (rev(c), 0)),
                   _full((1, GDN_DV)), _full((1, LANES)), _full((1, LANES))],
        out_shape=[jax.ShapeDtypeStruct(dproj_main.shape, dproj_main.dtype),
                   jax.ShapeDtypeStruct(dconv.shape, dconv.dtype),
                   jax.ShapeDtypeStruct((t, LANES), F32), jax.ShapeDtypeStruct((1, GDN_DV), F32),
                   jax.ShapeDtypeStruct((1, LANES), F32), jax.ShapeDtypeStruct((1, LANES), F32)],
        scratch_shapes=[pltpu.VMEM((GDN_HEADS, GDN_DK, GDN_DV), F32)],
        input_output_aliases={0: 0, 1: 1},
        compiler_params=_params(("arbitrary", "arbitrary")),
    )(dproj_main, dconv, conv_gdn, proj_main, proj_small, normw, alog, dtb, hist, t_inv, dy)


def out_proj_loss(x, y_ssd, y_gdn, w_out, final_w, target):
    t = x.shape[0]
    tm = min(512, t)

    def body(x_ref, ys_ref, yg_ref, wo_ref, fw_ref, tg_ref, loss_ref, dhid_ref, dys_ref, dyg_ref, dwo_ref, dfw_ref):
        i = pl.program_id(0)
        ys, yg = ys_ref[...], yg_ref[...]
        wo_s, wo_g = wo_ref[:SSD_WIDTH, :], wo_ref[SSD_WIDTH:, :]
        hid = x_ref[...] + _raw_dot(ys, wo_s, 1, 0) + _raw_dot(yg, wo_g, 1, 0)
        out, vjp = jax.vjp(rmsnorm, hid, fw_ref[...])
        err = out - tg_ref[...]
        loss = 0.5 * jnp.sum(jnp.mean(err * err, axis=-1, keepdims=True), axis=0, keepdims=True)
        dhid, dfw = vjp(err * (1.0 / D_MODEL))
        dhid_ref[...] = dhid
        dys_ref[...] = _raw_dot(dhid, wo_s, 1, 1)
        dyg_ref[...] = _raw_dot(dhid, wo_g, 1, 1)
        first = i == 0
        _accumulate(loss_ref, first, jnp.broadcast_to(loss, loss_ref.shape))
        _accumulate(dfw_ref, first, dfw)

        @pl.when(first)
        def _():
            dwo_ref[:SSD_WIDTH, :] = _raw_dot(ys, dhid, 0, 0)
            dwo_ref[SSD_WIDTH:, :] = _raw_dot(yg, dhid, 0, 0)

        @pl.when(i > 0)
        def _():
            dwo_ref[:SSD_WIDTH, :] += _raw_dot(ys, dhid, 0, 0)
            dwo_ref[SSD_WIDTH:, :] += _raw_dot(yg, dhid, 0, 0)

    row = lambda w: pl.BlockSpec((tm, w), lambda i: (i, 0))
    return pl.pallas_call(
        body, name="out_proj_loss", grid=(t // tm,),
        in_specs=[row(D_MODEL), row(SSD_WIDTH), row(GDN_W), _full((SSD_WIDTH + GDN_W, D_MODEL)), _full((1, D_MODEL)),
                  row(D_MODEL)],
        out_specs=[_full((8, LANES)), row(D_MODEL), row(SSD_WIDTH), row(GDN_W), _full((SSD_WIDTH + GDN_W, D_MODEL)),
                   _full((1, D_MODEL))],
        out_shape=[jax.ShapeDtypeStruct((8, LANES), F32), jax.ShapeDtypeStruct((t, D_MODEL), F32),
                   jax.ShapeDtypeStruct((t, SSD_WIDTH), F32), jax.ShapeDtypeStruct((t, GDN_W), F32),
                   jax.ShapeDtypeStruct((SSD_WIDTH + GDN_W, D_MODEL), F32), jax.ShapeDtypeStruct((1, D_MODEL), F32)],
        compiler_params=_params(("arbitrary",)),
    )(x, y_ssd, y_gdn, w_out, final_w, target)


def in_proj_bwd_x(x, normw, w_main, w_small, dproj_main, dproj_conv, dsmall_a, dsmall_b, dhid, slabbed):
    t = x.shape[0]
    tm = min(256, t)
    ni = t // tm
    ns = len(slabbed)

    def body(x_ref, nw_ref, wm_ref, ws_ref, dp_ref, dc_ref, da_ref, db_ref, dh_ref, *rest):
        slab_refs, (gx_ref, dnw_ref), land_refs = rest[:ns], rest[ns:ns + 2], rest[ns + 2:2 * ns + 2]
        sems = rest[2 * ns + 2:]
        i = pl.program_id(0)
        start, finish = _slab_exchange(slab_refs, land_refs, ns, *sems)

        @pl.when(i == 0)
        def _():
            start()

        du = (_raw_dot(dp_ref[...], wm_ref[:, :COL_CONV], 1, 1) + _raw_dot(dc_ref[...], wm_ref[:, COL_CONV:], 1, 1)
              + _raw_dot(da_ref[...] + db_ref[...], ws_ref[...], 1, 1))
        _, vjp = jax.vjp(rmsnorm, x_ref[...], nw_ref[...])
        dx, dnw = vjp(du)
        gx_ref[...] = dx + dh_ref[...]
        _accumulate(dnw_ref, i == 0, dnw)

        @pl.when(i == ni - 1)
        def _():
            finish()

    row = lambda w: pl.BlockSpec((tm, w), lambda i: (i, 0))
    out = pl.pallas_call(
        body, name="in_proj_bwd_x", grid=(ni,),
        in_specs=[row(D_MODEL), _full((1, D_MODEL)), _full((D_MODEL, MAIN)), _full((D_MODEL, LANES)), row(COL_CONV),
                  row(CONV_W), row(LANES), row(LANES), row(D_MODEL)] + [HBM] * ns,
        out_specs=[row(D_MODEL), _full((1, D_MODEL))] + [HBM] * ns,
        out_shape=[jax.ShapeDtypeStruct((t, D_MODEL), F32), jax.ShapeDtypeStruct((1, D_MODEL), F32)]
        + _slab_exchange_shapes(slabbed, []),
        scratch_shapes=_slab_exchange_sems(ns),
        compiler_params=_params(("arbitrary",)),
    )(x, normw, w_main, w_small, dproj_main, dproj_conv, dsmall_a, dsmall_b, dhid, *slabbed)
    return out[0], out[1], out[2:]


def in_proj_bwd_w(u, dproj_main, dsmall_a, dsmall_b):
    t = u.shape[0]
    tm, tn = min(1024, t), COL_CONV // 2

    def body(u_ref, dp_ref, da_ref, db_ref, dwm_ref, dws_ref):
        j, i = pl.program_id(0), pl.program_id(1)
        uu = u_ref[...]
        _accumulate(dwm_ref, i == 0, _raw_dot(uu, dp_ref[...], 0, 0))

        @pl.when(j == 0)
        def _():
            _accumulate(dws_ref, i == 0, _raw_dot(uu, da_ref[...] + db_ref[...], 0, 0))

    return pl.pallas_call(
        body, name="in_proj_bwd_w", grid=(COL_CONV // tn, t // tm),
        in_specs=[pl.BlockSpec((tm, D_MODEL), lambda j, i: (i, 0)), pl.BlockSpec((tm, tn), lambda j, i: (i, j)),
                  pl.BlockSpec((tm, LANES), lambda j, i: (i, 0)), pl.BlockSpec((tm, LANES), lambda j, i: (i, 0))],
        out_specs=[pl.BlockSpec((D_MODEL, tn), lambda j, i: (0, j)), _full((D_MODEL, LANES))],
        out_shape=[jax.ShapeDtypeStruct((D_MODEL, COL_CONV), F32), jax.ShapeDtypeStruct((D_MODEL, LANES), F32)],
        compiler_params=_params(("arbitrary", "arbitrary")),
    )(u, dproj_main, dsmall_a, dsmall_b)


def sum_slabs(a, name):
    n, rows, cols = a.shape
    tr = 64 if rows % 64 == 0 else rows

    def body(a_ref, o_ref):
        acc = a_ref[0].astype(F32)
        for d in range(1, n):
            acc = acc + a_ref[d].astype(F32)
        o_ref[...] = acc

    return pl.pallas_call(
        body, name=name, grid=(rows // tr,),
        in_specs=[pl.BlockSpec((n, tr, cols), lambda i: (0, i, 0))],
        out_specs=pl.BlockSpec((tr, cols), lambda i: (i, 0)),
        out_shape=jax.ShapeDtypeStruct((rows, cols), F32),
        compiler_params=_params(("arbitrary",)),
    )(a)


def adamw(w, g, m, v, name):
    _, rows, cols = w.shape
    tr = 128 if rows % 128 == 0 else rows

    def body(w_ref, g_ref, m_ref, v_ref, d_ref, nm_ref, nv_ref):
        gg = g_ref[...]
        nm = ADAM_B1 * m_ref[...] + (1.0 - ADAM_B1) * gg
        nv = ADAM_B2 * v_ref[...] + (1.0 - ADAM_B2) * (gg * gg)
        m_hat = nm / (1.0 - ADAM_B1 ** ADAM_STEP)
        v_hat = nv / (1.0 - ADAM_B2 ** ADAM_STEP)
        d_ref[...] = -ADAM_LR * (m_hat / (jnp.sqrt(v_hat) + ADAM_EPS) + ADAM_WD * w_ref[...])
        nm_ref[...] = nm
        nv_ref[...] = nv

    spec = pl.BlockSpec((1, tr, cols), lambda i: (0, i, 0))
    shp = jax.ShapeDtypeStruct((1, rows, cols), F32)
    return pl.pallas_call(
        body, name=name, grid=(rows // tr,), in_specs=[spec] * 4, out_specs=[spec] * 3, out_shape=[shp] * 3,
        compiler_params=_params(("arbitrary",)),
    )(w, g.reshape(w.shape), m, v)


def _my_place():
    return lax.axis_index("x"), lax.axis_index("y"), lax.axis_index("c")


def gather_weights(big, small):
    nb, n = len(big), len(big) + len(small)
    parts = 4

    def body(*refs):
        srcs, outs = refs[:n], refs[n:2 * n]
        land_a, land_b = refs[2 * n:2 * n + nb], refs[2 * n + nb:2 * n + 2 * nb]
        send_sems, recv_sems, fwd_send, fwd_recv, local_sems = refs[2 * n + 2 * nb:]
        x, y, c = _my_place()
        me = 2 * x + y
        chips = [(1 - x, y), (x, 1 - y), (1 - x, 1 - y)]
        half = [a.shape[0] // 2 for a in big]

        def ici(j, i):
            px, py = chips[j]
            if i < nb:
                src, dst = srcs[i].at[pl.ds(c * half[i], half[i])], land_a[i].at[j]
            else:
                src, dst = srcs[i], outs[i].at[me]
            return pltpu.make_async_remote_copy(src_ref=src, dst_ref=dst, send_sem=send_sems.at[j * n + i],
                                                recv_sem=recv_sems.at[j * n + i], device_id=(px, py, c),
                                                device_id_type=MESH)

        def ici_arrival(j, i):
            px, py = chips[j]
            dst = land_a[i].at[j] if i < nb else outs[i].at[2 * px + py]
            return pltpu.make_async_remote_copy(src_ref=dst, dst_ref=dst, send_sem=send_sems.at[j * n + i],
                                                recv_sem=recv_sems.at[j * n + i], device_id=(px, py, c),
                                                device_id_type=MESH)

        def forward(j, i, p):
            rows = half[i] // parts
            k = (j * nb + i) * parts + p
            return pltpu.make_async_remote_copy(
                src_ref=land_a[i].at[j, pl.ds(p * rows, rows)], dst_ref=land_b[i].at[j, pl.ds(p * rows, rows)],
                send_sem=fwd_send.at[k], recv_sem=fwd_recv.at[k], device_id=(x, y, 1 - c), device_id_type=MESH)

        def store(j, i, from_sibling):
            px, py = chips[j]
            buf, h = (land_b, 1 - c) if from_sibling else (land_a, c)
            k = n + (j * nb + i) * 2 + (1 if from_sibling else 0)
            return pltpu.make_async_copy(buf[i].at[j], outs[i].at[2 * px + py, pl.ds(h * half[i], half[i])],
                                         local_sems.at[k])

        own = [pltpu.make_async_copy(srcs[i], outs[i].at[me], local_sems.at[i]) for i in range(n)]
        sends = [ici(j, i) for j in range(3) for i in range(n)]
        for cp in own + sends:
            cp.start()
        pending = []
        for j in range(3):
            for i in range(n):
                ici_arrival(j, i).wait_recv()
                if i < nb:
                    fw = [forward(j, i, p) for p in range(parts)]
                    st = store(j, i, False)
                    for cp in fw + [st]:
                        cp.start()
                    pending += [cp.wait_send for cp in fw] + [st.wait]
        for j in range(3):
            for i in range(nb):
                for p in range(parts):
                    forward(j, i, p).wait_recv()
                st = store(j, i, True)
                st.start()
                pending.append(st.wait)
        for cp in sends:
            cp.wait_send()
        for wait in pending:
            wait()
        for cp in own:
            cp.wait()

    shards = list(big) + list(small)
    lands = [pltpu.VMEM((3, a.shape[0] // 2) + a.shape[1:], a.dtype) for a in big]
    return pl.pallas_call(
        body, name="gather_weights",
        in_specs=[HBM] * n, out_specs=[HBM] * n,
        out_shape=[jax.ShapeDtypeStruct((N_CHIP,) + s.shape, s.dtype) for s in shards],
        scratch_shapes=lands + lands + [
            pltpu.SemaphoreType.DMA((3 * n,)), pltpu.SemaphoreType.DMA((3 * n,)),
            pltpu.SemaphoreType.DMA((3 * nb * parts,)), pltpu.SemaphoreType.DMA((3 * nb * parts,)),
            pltpu.SemaphoreType.DMA((n + 6 * nb,))],
        compiler_params=pltpu.CompilerParams(vmem_limit_bytes=VMEM_LIMIT),
    )(*shards)


def _peer(x, y, c, mask):
    mx, my, mc = (mask >> 2) & 1, (mask >> 1) & 1, mask & 1
    return (x ^ mx if mx else x, y ^ my if my else y, c ^ mc if mc else c)


def _slab_exchange_shapes(slabbed, replicated):
    return ([jax.ShapeDtypeStruct(a.shape, a.dtype) for a in slabbed]
            + [jax.ShapeDtypeStruct((N_DEV,) + a.shape, a.dtype) for a in replicated])


def _slab_exchange_sems(n):
    return [pltpu.SemaphoreType.DMA((7 * n,)), pltpu.SemaphoreType.DMA((7 * n,)), pltpu.SemaphoreType.DMA((n,))]


def _slab_exchange(srcs, outs, ns, send_sems, recv_sems, local_sems):
    n = len(srcs)
    x, y, c = _my_place()
    me = 4 * x + 2 * y + c

    def piece(i, dev):
        return srcs[i].at[dev] if i < ns else srcs[i]

    def copies(arriving):
        out = []
        for mask in range(1, N_DEV):
            px, py, pc = _peer(x, y, c, mask)
            dev = 4 * px + 2 * py + pc
            for i in range(n):
                k = (mask - 1) * n + i
                out.append(pltpu.make_async_remote_copy(
                    src_ref=piece(i, dev), dst_ref=outs[i].at[dev if arriving else me], send_sem=send_sems.at[k],
                    recv_sem=recv_sems.at[k], device_id=(px, py, pc), device_id_type=MESH))
        return out

    def local():
        return [pltpu.make_async_copy(piece(i, me), outs[i].at[me], local_sems.at[i]) for i in range(n)]

    def start():
        for cp in local() + copies(False):
            cp.start()

    def finish():
        for cp in copies(True):
            cp.wait_recv()
        for cp in copies(False):
            cp.wait_send()
        for cp in local():
            cp.wait()

    return start, finish


def exchange_halves(landed, replicated):
    n, nr = len(landed), len(replicated)
    streams = 8
    halves = [jax.ShapeDtypeStruct(a.shape[1:], F32) for a in landed]
    sum_rows = 64

    def body(*refs):
        srcs, rep_srcs, outs, rep_outs = refs[:n], refs[n:n + nr], refs[n + nr:2 * n + nr], refs[2 * n + nr:2 * (n + nr)]
        refs = refs[2 * (n + nr):]
        slabs, mine, theirs = refs[:n], refs[n:2 * n], refs[2 * n:3 * n]
        send_sems, recv_sems, in_sems, out_sems = refs[3 * n:3 * n + 4]
        rep_start, rep_finish = _slab_exchange(rep_srcs, rep_outs, 0, *refs[3 * n + 4:])
        rep_start()
        x, y, c = _my_place()
        loads = [pltpu.make_async_copy(srcs[i], slabs[i], in_sems.at[i]) for i in range(n)]
        for cp in loads:
            cp.start()
        for i in range(n):
            loads[i].wait()
            for r in range(0, halves[i].shape[0], sum_rows):
                rows = pl.ds(r, sum_rows)
                acc = slabs[i][0, rows, :].astype(F32)
                for d in range(1, N_DEV):
                    acc = acc + slabs[i][d, rows, :].astype(F32)
                mine[i][rows, :] = acc

        def chunk_copy(i, s):
            rows = halves[i].shape[0] // streams
            k = i * streams + s
            return pltpu.make_async_remote_copy(
                src_ref=mine[i].at[pl.ds(s * rows, rows)], dst_ref=theirs[i].at[pl.ds(s * rows, rows)],
                send_sem=send_sems.at[k], recv_sem=recv_sems.at[k], device_id=(x, y, 1 - c), device_id_type=MESH)

        sends = [chunk_copy(i, s) for i in range(n) for s in range(streams)]
        for cp in sends:
            cp.start()
        own = [pltpu.make_async_copy(mine[i], outs[i].at[c], out_sems.at[i]) for i in range(n)]
        for cp in own:
            cp.start()
        for cp in sends:
            cp.wait_recv()
        got = [pltpu.make_async_copy(theirs[i], outs[i].at[1 - c], out_sems.at[n + i]) for i in range(n)]
        for cp in got:
            cp.start()
        for cp in sends:
            cp.wait_send()
        for cp in own + got:
            cp.wait()
        rep_finish()

    vmem = [pltpu.VMEM(a.shape, a.dtype) for a in halves]
    out = pl.pallas_call(
        body, name="exchange_halves",
        in_specs=[HBM] * (n + nr), out_specs=[HBM] * (n + nr),
        out_shape=[jax.ShapeDtypeStruct((2,) + a.shape, a.dtype) for a in halves]
        + _slab_exchange_shapes([], replicated),
        scratch_shapes=[pltpu.VMEM(a.shape, a.dtype) for a in landed] + vmem + vmem
        + [pltpu.SemaphoreType.DMA((n * streams,)), pltpu.SemaphoreType.DMA((n * streams,)),
           pltpu.SemaphoreType.DMA((n,)), pltpu.SemaphoreType.DMA((2 * n,))] + _slab_exchange_sems(nr),
        compiler_params=pltpu.CompilerParams(vmem_limit_bytes=VMEM_LIMIT),
    )(*landed, *replicated)
    return out[:n], out[n:]


def _pack_cols(pieces):
    offs, pos = [], 0
    for a in pieces:
        offs.append(pos)
        pos += a.shape[1]
    rows8 = [jnp.pad(a.astype(F32), ((0, 8 - a.shape[0]), (0, 0))) for a in pieces]
    return jnp.concatenate(rows8, axis=1), offs


def adamw_many(ws, gs, ms, vs):
    n = len(ws)

    def body(*refs):
        w_r, g_r, m_r, v_r = refs[:n], refs[n:2 * n], refs[2 * n:3 * n], refs[3 * n:4 * n]
        d_o, m_o, v_o = refs[4 * n:5 * n], refs[5 * n:6 * n], refs[6 * n:7 * n]
        for i in range(n):
            gg = g_r[i][...]
            nm = ADAM_B1 * m_r[i][...] + (1.0 - ADAM_B1) * gg
            nv = ADAM_B2 * v_r[i][...] + (1.0 - ADAM_B2) * (gg * gg)
            m_hat = nm / (1.0 - ADAM_B1 ** ADAM_STEP)
            v_hat = nv / (1.0 - ADAM_B2 ** ADAM_STEP)
            d_o[i][...] = -ADAM_LR * (m_hat / (jnp.sqrt(v_hat) + ADAM_EPS) + ADAM_WD * w_r[i][...])
            m_o[i][...] = nm
            v_o[i][...] = nv

    shapes = [jax.ShapeDtypeStruct(w.shape, F32) for w in ws]
    out = pl.pallas_call(body, name="adamw_small", out_shape=shapes * 3,
                         compiler_params=pltpu.CompilerParams(vmem_limit_bytes=VMEM_LIMIT))(*ws, *gs, *ms, *vs)
    return out[:n], out[n:2 * n], out[2 * n:]


def _lanes(vec, start):
    n = vec.shape[-1]
    return jnp.pad(vec.reshape(1, n).astype(F32), ((0, 0), (start, LANES - start - n)))


def kernel(x, norm_w, w_in, ssd_conv_w, ssd_conv_b, ssd_dt_bias, ssd_a_log, ssd_d, ssd_norm_w, gdn_conv_w, gdn_dt_bias, gdn_a_log, gdn_norm_w, w_out, final_norm_w, loss_target, m_norm_w, m_w_in, m_ssd_conv_w, m_ssd_conv_b, m_ssd_dt_bias, m_ssd_a_log, m_ssd_d, m_ssd_norm_w, m_gdn_conv_w, m_gdn_dt_bias, m_gdn_a_log, m_gdn_norm_w, m_w_out, m_final_norm_w, v_norm_w, v_w_in, v_ssd_conv_w, v_ssd_conv_b, v_ssd_dt_bias, v_ssd_a_log, v_ssd_d, v_ssd_norm_w, v_gdn_conv_w, v_gdn_dt_bias, v_gdn_a_log, v_gdn_norm_w, v_w_out, v_final_norm_w):
    xs = x[0]
    target = loss_target[0]
    chip = 2 * lax.axis_index("x") + lax.axis_index("y")
    w_in_shard, w_out_shard = w_in[0], w_out[0]
    in_cols = w_in_shard.shape[1]
    out_rows = w_out_shard.shape[0]

    g_in, g_out, g_cs, g_cg = gather_weights(
        [w_in_shard.astype(MXU_DTYPE), w_out_shard.astype(MXU_DTYPE)], [ssd_conv_w[0], gdn_conv_w[0]])
    w_in_full = jnp.concatenate([g_in[k] for k in range(N_CHIP)], axis=1)
    w_out_full = g_out.reshape(N_CHIP * out_rows, D_MODEL)
    cw_ssd = jnp.concatenate([g_cs[k] for k in range(N_CHIP)], axis=1)
    cw_gdn = jnp.concatenate([g_cg[k] for k in range(N_CHIP)], axis=1)
    cb_ssd, cb_gdn = ssd_conv_b, jnp.zeros((1, GDN_CONV), F32)
    o_xbc, o_dt, o_gate, o_qkv, o_ab = 1024, 2560, 2576, 3600, 6672
    w_main = jnp.concatenate([w_in_full[:, :o_xbc], w_in_full[:, o_gate:o_qkv], w_in_full[:, o_qkv:o_ab],
                              w_in_full[:, o_xbc:o_dt]], axis=1)
    w_small = jnp.concatenate([w_in_full[:, o_dt:o_gate], w_in_full[:, o_ab:],
                               jnp.zeros((D_MODEL, LANES - 32), MXU_DTYPE)], axis=1)
    alog = _lanes(ssd_a_log, 0) + _lanes(gdn_a_log, LANE_GA)
    dtb = _lanes(ssd_dt_bias, 0) + _lanes(gdn_dt_bias, LANE_GA)
    dvec = _lanes(ssd_d, 0)
    fw = final_norm_w.reshape(1, D_MODEL)

    cw, cb = jnp.concatenate([cw_gdn, cw_ssd], axis=1), jnp.concatenate([cb_gdn, cb_ssd], axis=1)
    proj_main, proj_small, u = in_proj(xs, norm_w, w_main, w_small)
    proj_conv, conv_out = in_proj_conv(u, w_main, cw, cb)
    y_ssd, hist_ssd = ssd_fwd(conv_out, proj_main, proj_small, ssd_norm_w, alog, dtb, dvec)
    y_gdn, hist_gdn, tinv_gdn = gdn_fwd(conv_out, proj_main, proj_small, gdn_norm_w, alog, dtb)

    loss_blk, dhid, dy_ssd, dy_gdn, d_w_out, d_fw = out_proj_loss(xs, y_ssd, y_gdn, w_out_full, fw, target)
    dconv, dproj_main, dsmall_ssd, d_ssd_nw, d_alog_s, d_dtb_s, d_dvec = ssd_bwd(
        conv_out, proj_main, proj_small, ssd_norm_w, alog, dtb, dvec, hist_ssd, dy_ssd)
    dproj_main, dconv, dsmall_gdn, d_gdn_nw, d_alog_g, d_dtb_g = gdn_bwd(
        dproj_main, dconv, conv_out, proj_main, proj_small, gdn_norm_w, alog, dtb, hist_gdn, tinv_gdn, dy_gdn)
    slabs_out = d_w_out.reshape(N_DEV, out_rows // 2, D_MODEL).astype(COMM_DTYPE)
    dproj_conv, d_w_conv, dwb, (r_out,) = conv_bwd_w(u, proj_conv, cw, cb, dconv, [slabs_out])
    dwb_gdn, dwb_ssd = dwb[:, :GDN_CONV], dwb[:, GDN_CONV:]
    d_w_zg, d_w_small = in_proj_bwd_w(u, dproj_main, dsmall_ssd, dsmall_gdn)
    order = [(d_w_zg, 0, COL_GATE), (d_w_conv, COL_SSD - COL_CONV, CONV_W), (d_w_small, 0, 16),
             (d_w_zg, COL_GATE, COL_CONV), (d_w_conv, 0, COL_SSD - COL_CONV), (d_w_small, 16, 32)]
    shards, pos = [[] for _ in range(N_CHIP)], 0
    for src, lo, hi in order:
        while lo < hi:
            k = pos // in_cols
            n = min(hi - lo, (k + 1) * in_cols - pos)
            shards[k].append(src[:, lo:lo + n].astype(COMM_DTYPE))
            lo, pos = lo + n, pos + n
    slabs_in = jnp.stack([jnp.concatenate(p, axis=1) for p in shards]).reshape(N_DEV, D_MODEL // 2, in_cols)
    grad_x, d_norm_w, (r_in,) = in_proj_bwd_x(xs, norm_w, w_main, w_small, dproj_main, dproj_conv, dsmall_ssd,
                                               dsmall_gdn, dhid, [slabs_in])
    d_alog, d_dtb = d_alog_s + d_alog_g, d_dtb_s + d_dtb_g
    packed, (o_nw, o_cs, o_cg, o_snw, o_fw, o_al, o_db, o_dv, o_gnw, o_loss) = _pack_cols([
        d_norm_w, dwb_ssd, dwb_gdn,
        d_ssd_nw.reshape(1, SSD_WIDTH), d_fw, d_alog, d_dtb, d_dvec, d_gdn_nw, loss_blk])

    (full_in, full_out), (r_small,) = exchange_halves([r_in, r_out], [packed])
    tot = sum_slabs(r_small, "sum_small")
    grad_w_in = full_in.reshape(D_MODEL, in_cols)
    grad_w_out = full_out.reshape(out_rows, D_MODEL)
    loss = tot[0, o_loss]
    sc, gc = ssd_conv_w.shape[2], gdn_conv_w.shape[2]
    row = lambda off, n, r=0: tot[r:r + 1, off:off + n]
    gs = [row(o_nw, D_MODEL),
          lax.dynamic_slice(tot, (0, o_cs + chip * sc), (4, sc)),
          row(o_cs, SSD_CONV, 4),
          row(o_db, SSD_HEADS), row(o_al, SSD_HEADS), row(o_dv, SSD_HEADS),
          row(o_snw, SSD_WIDTH),
          lax.dynamic_slice(tot, (0, o_cg + chip * gc), (4, gc)),
          row(o_db + LANE_GA, GDN_HEADS), row(o_al + LANE_GA, GDN_HEADS),
          row(o_gnw, GDN_DV), row(o_fw, D_MODEL)]

    names = ["norm_w", "ssd_conv_w", "ssd_conv_b", "ssd_dt_bias", "ssd_a_log", "ssd_d", "ssd_norm_w", "gdn_conv_w",
             "gdn_dt_bias", "gdn_a_log", "gdn_norm_w", "final_norm_w"]
    ws = [norm_w, ssd_conv_w, ssd_conv_b, ssd_dt_bias, ssd_a_log, ssd_d, ssd_norm_w, gdn_conv_w, gdn_dt_bias,
          gdn_a_log, gdn_norm_w, final_norm_w]
    ms = [m_norm_w, m_ssd_conv_w, m_ssd_conv_b, m_ssd_dt_bias, m_ssd_a_log, m_ssd_d, m_ssd_norm_w, m_gdn_conv_w,
          m_gdn_dt_bias, m_gdn_a_log, m_gdn_norm_w, m_final_norm_w]
    vs = [v_norm_w, v_ssd_conv_w, v_ssd_conv_b, v_ssd_dt_bias, v_ssd_a_log, v_ssd_d, v_ssd_norm_w, v_gdn_conv_w,
          v_gdn_dt_bias, v_gdn_a_log, v_gdn_norm_w, v_final_norm_w]
    shapes = [w.shape for w in ws]
    flat = lambda arrs: [a.reshape(g.shape) for a, g in zip(arrs, gs)]
    d_s, m_s, v_s = adamw_many(flat(ws), gs, flat(ms), flat(vs))
    back = lambda arrs: dict(zip(names, [a.reshape(s) for a, s in zip(arrs, shapes)]))
    delta, new_m, new_v, grads = back(d_s), back(m_s), back(v_s), back(gs)
    d_in, m_in, v_in = adamw(w_in, grad_w_in, m_w_in, v_w_in, "adamw_w_in")
    d_out, m_out, v_out = adamw(w_out, grad_w_out, m_w_out, v_w_out, "adamw_w_out")
    for tbl, a_in, a_out in ((grads, grad_w_in[None], grad_w_out[None]), (delta, d_in, d_out), (new_m, m_in, m_out),
                             (new_v, v_in, v_out)):
        tbl["w_in"] = a_in
        tbl["w_out"] = a_out

    order = ["norm_w", "w_in", "ssd_conv_w", "ssd_conv_b", "ssd_dt_bias", "ssd_a_log", "ssd_d", "ssd_norm_w",
             "gdn_conv_w", "gdn_dt_bias", "gdn_a_log", "gdn_norm_w", "w_out", "final_norm_w"]
    return (loss.reshape(()), grad_x[None], *[grads[k] for k in order], *[delta[k] for k in order],
            *[new_m[k] for k in order], *[new_v[k] for k in order])
```

```python
import functools

import jax
import jax.numpy as jnp
from jax import lax
from jax.experimental import pallas as pl
from jax.experimental.pallas import tpu as pltpu

F32 = jnp.float32
MXU_DTYPE = jnp.bfloat16
COMM_DTYPE = jnp.bfloat16
MESH = pl.DeviceIdType.MESH

D_MODEL = 1024
CHUNK = 64
EPS = 1e-6
SSD_HEADS, SSD_GROUPS, SSD_STATE = 16, 2, 128
SSD_WIDTH, SSD_CONV = 1024, 1536
SSD_GW = SSD_WIDTH // SSD_GROUPS
GDN_HEADS, GDN_DK, GDN_DV = 8, 128, 128
GDN_W, GDN_CONV = 1024, 3072
GDN_HC = 2 * GDN_DK + GDN_DV
IN_DIM = 6688
MAIN = 6656
LANES = 128
COL_Z, COL_GATE, COL_GDN, COL_SSD = 0, 1024, 2048, 5120
COL_CONV = COL_GDN
CONV_W = MAIN - COL_CONV
GDN_HB = 8
GDN_CB = 4
SSD_CB = 4
LANE_GA, LANE_GB = 16, 24
N_DEV, N_CHIP = 8, 4
VMEM_LIMIT = 52 * 1024 * 1024

ADAM_LR, ADAM_B1, ADAM_B2, ADAM_EPS, ADAM_WD, ADAM_STEP = 0.001, 0.9, 0.999, 1e-08, 0.01, 10


def _split(a, n):
    parts, rest = [], a.astype(F32)
    for i in range(n):
        p = rest.astype(MXU_DTYPE)
        parts.append(p)
        if i < n - 1:
            rest = rest - p.astype(F32)
    return parts


def _raw_dot(a, b, ca, cb, mode="bf16"):
    d = lambda u, v: lax.dot_general(u, v, (((ca,), (cb,)), ((), ())), preferred_element_type=F32)
    if mode == "bf16":
        return d(a.astype(MXU_DTYPE), b.astype(MXU_DTYPE))
    if mode == "x3":
        (ah, al), (bh, bl) = _split(a, 2), _split(b, 2)
        return d(ah, bh) + (d(ah, bl) + d(al, bh))
    if mode == "sel_a":
        a0 = a.astype(MXU_DTYPE)
        b1, b2, b3 = _split(b, 3)
        return d(a0, b1) + (d(a0, b2) + d(a0, b3))
    assert mode == "sel_b", mode
    b0 = b.astype(MXU_DTYPE)
    a1, a2, a3 = _split(a, 3)
    return d(a1, b0) + (d(a2, b0) + d(a3, b0))


@functools.partial(jax.custom_vjp, nondiff_argnums=(2,))
def mm_nn(a, b, mode="bf16"):
    return _raw_dot(a, b, 1, 0, mode)


@functools.partial(jax.custom_vjp, nondiff_argnums=(2,))
def mm_nt(a, b, mode="bf16"):
    return _raw_dot(a, b, 1, 1, mode)


@functools.partial(jax.custom_vjp, nondiff_argnums=(2,))
def mm_tn(a, b, mode="bf16"):
    return _raw_dot(a, b, 0, 0, mode)


_SAME = {"bf16": ("bf16", "bf16"), "x3": ("x3", "x3")}
_NN_BWD = dict(_SAME, sel_a=("bf16", "sel_a"), sel_b=("sel_b", "bf16"))
_NT_BWD = dict(_SAME, sel_a=("bf16", "sel_b"), sel_b=("sel_b", "bf16"))
_TN_BWD = dict(_SAME, sel_a=("bf16", "sel_a"), sel_b=("sel_a", "bf16"))
mm_nn.defvjp(lambda a, b, m: (_raw_dot(a, b, 1, 0, m), (a, b)),
             lambda m, r, g: (mm_nt(g, r[1], _NN_BWD[m][0]), mm_tn(r[0], g, _NN_BWD[m][1])))
mm_nt.defvjp(lambda a, b, m: (_raw_dot(a, b, 1, 1, m), (a, b)),
             lambda m, r, g: (mm_nn(g, r[1], _NT_BWD[m][0]), mm_tn(g, r[0], _NT_BWD[m][1])))
mm_tn.defvjp(lambda a, b, m: (_raw_dot(a, b, 0, 0, m), (a, b)),
             lambda m, r, g: (mm_nt(r[1], g, _TN_BWD[m][0]), mm_nn(r[0], g, _TN_BWD[m][1])))


@jax.custom_jvp
def sigmoid(x):
    return 1.0 / (1.0 + jnp.exp(-x))


@sigmoid.defjvp
def _sigmoid_jvp(p, t):
    s = sigmoid(p[0])
    return s, t[0] * s * (1.0 - s)


@jax.custom_jvp
def softplus(x):
    return jnp.maximum(x, 0.0) + jnp.log(1.0 + jnp.exp(-jnp.abs(x)))


@softplus.defjvp
def _softplus_jvp(p, t):
    return softplus(p[0]), t[0] * sigmoid(p[0])


def silu(x):
    return x * sigmoid(x)


def rmsnorm(x, w):
    return x * lax.rsqrt(jnp.mean(x * x, axis=-1, keepdims=True) + EPS) * w


def _iota(shape, dim):
    return lax.broadcasted_iota(jnp.int32, shape, dim)


def _halves():
    lane = _iota((1, LANES), 1) >> 6
    return _ind(lane == 0), _ind(lane == 1)


def _block_diag(pair):
    h0, h1 = _halves()
    return jnp.concatenate([pair * h0, pair * h1], axis=0)


def _tri_inv_impl(mats):
    r, c = _iota((CHUNK, LANES), 0), _iota((CHUNK, LANES), 1) & (CHUNK - 1)
    eye = _ind(r == c)
    blockdiag = _ind((r >> 4) == (c >> 4))
    dot = lambda u, v: _raw_dot(u, _block_diag(v), 1, 0, "x3")
    dot1 = lambda u, v: _raw_dot(u, _block_diag(v), 1, 0)
    each = lambda f, *ls: [f(*xs) for xs in zip(*ls)]
    dg = each(lambda a: a * blockdiag, mats)
    off = each(lambda a, d: a - d, mats, dg)
    m = each(lambda d: -d, dg)
    p = each(lambda x: eye + x, m)
    pw = m
    for _ in range(3):
        pw = each(lambda x: dot1(x, x), pw)
        p = each(lambda x, y: x + dot1(x, y), p, pw)
    e = each(dot, p, off)
    e2 = each(lambda x: dot1(x, x), e)
    q = each(lambda x: eye - x, e)
    q = each(lambda x, y: x + dot1(x, y), q, e2)
    return each(dot, q, p)


def _tri_inv_bwd(ts, gs):
    h0, h1 = _halves()
    x = [mm_nt(g, _block_diag(t)) for g, t in zip(gs, ts)]
    full = [mm_tn(t, y) for t, y in zip(ts, x)]
    return [-(f[:CHUNK] * h0 + f[CHUNK:] * h1) for f in full]


@jax.custom_vjp
def tri_inv(mats):
    return _tri_inv_impl(mats)


def _tri_inv_fwd(mats):
    ts = _tri_inv_impl(mats)
    return ts, ts


tri_inv.defvjp(_tri_inv_fwd, lambda ts, gs: (_tri_inv_bwd(ts, gs),))


@jax.custom_vjp
def tri_inv_saved(mats, ts):
    del mats
    return ts


tri_inv_saved.defvjp(lambda mats, ts: (ts, ts),
                     lambda ts, gs: (_tri_inv_bwd(ts, gs), [jnp.zeros_like(t) for t in ts]))


def _ind(cond):
    return jnp.where(cond, 1.0, 0.0).astype(F32)


def _chunk_masks():
    r, c = _iota((CHUNK, CHUNK), 0), _iota((CHUNK, CHUNK), 1)
    return _ind(r >= c), _ind(r > c), _ind(r == c), _ind(_iota((CHUNK, 1), 0) == CHUNK - 1)


def _log_decay_cumsum(small, alog, dtb, tri):
    sp = softplus(small + dtb)
    la = -jnp.exp(alog) * sp
    return sp, mm_nn(tri, la, "sel_a")


def _col_of(x, lane):
    return jnp.sum(x * _ind(_iota((1, LANES), 1) == lane), axis=1, keepdims=True)


def _pair_masks():
    r, c = _iota((CHUNK, LANES), 0), _iota((CHUNK, LANES), 1)
    c6 = c & (CHUNK - 1)
    return _ind(r >= c6), _ind(r > c6), (_ind(c == r), _ind(c == r + CHUNK))


def _decay_pair(col_a, col_b, tri_w, eye_w):
    h0, h1 = _halves()
    col = col_a * h0 + col_b * h1
    row = jnp.sum(col_a * eye_w[0] + col_b * eye_w[1], axis=0, keepdims=True)
    return jnp.exp((col - row) * tri_w) * tri_w


def gdn_chunk(h0, qs, ks, vs, smalls, gates, normw, alog, dtb, states, saved_t=None):
    tri, _, _, last = _chunk_masks()
    tri_w, strict_w, eye_w = _pair_masks()
    nh = len(qs[0])
    flat = lambda xss: [x for xs in xss for x in xs]
    lacs = [_log_decay_cumsum(sm, alog, dtb, tri)[1] for sm in smalls]
    qs, ks, vs, gates = flat(qs), flat(ks), flat(vs), flat(gates)
    heads, pairs = range(len(qs)), range(len(qs) // 2)
    each = lambda f, *ls: [f(*xs) for xs in zip(*ls)]
    ab = lambda xs, p: (xs[2 * p], xs[2 * p + 1])
    stack = lambda xs: jnp.concatenate(xs, axis=0)
    gc = [_col_of(lacs[i // nh], LANE_GA + h0 + i % nh) for i in heads]
    beta = [sigmoid(_col_of(smalls[i // nh], LANE_GB + h0 + i % nh)) for i in heads]
    decay = [_decay_pair(*ab(gc, p), tri_w, eye_w) for p in pairs]
    gl = each(lambda x: jnp.sum(x * last, axis=0, keepdims=True), gc)
    q = each(lambda x: x * lax.rsqrt(jnp.sum(x * x, axis=-1, keepdims=True) + EPS) * (GDN_DK ** -0.5), qs)
    k = each(lambda x: x * lax.rsqrt(jnp.sum(x * x, axis=-1, keepdims=True) + EPS), ks)
    kb = each(lambda x, b: x * b, k, beta)
    eg = each(jnp.exp, gc)
    zero = jnp.zeros((CHUNK, GDN_DK), F32)
    k_bd = [stack([join_lanes([k[2 * p], zero]), join_lanes([zero, k[2 * p + 1]])]) for p in pairs]
    a = [mm_nt(join_lanes(list(ab(kb, p))), k_bd[p]) * (decay[p] * strict_w) for p in pairs]
    t = tri_inv(a) if saved_t is None else tri_inv_saved(a, saved_t)
    attn = [mm_nt(join_lanes(list(ab(q, p))), k_bd[p]) * decay[p] for p in pairs]
    rhs = [stack([join_lanes([vs[h] * beta[h], kb[h] * eg[h]]) for h in (2 * p, 2 * p + 1)]) for p in pairs]
    uw = [mm_nn(_block_diag(t[p]), rhs[p]) for p in pairs]
    uw = [x for p in pairs for x in split_rows(uw[p])]
    u, w = zip(*[split_lanes(x) for x in uw])
    ys = []
    for c in range(len(smalls)):
        hs = range(c * nh, (c + 1) * nh)
        v_new = [u[i] - mm_nn(w[i], states[i % nh]) for i in hs]
        av = [mm_nn(_block_diag(attn[c * nh // 2 + p]), stack(list(ab(v_new, p)))) for p in range(nh // 2)]
        av = [x for y in av for x in split_rows(y)]
        o = [mm_nn(q[i] * eg[i], states[i % nh]) + av[i % nh] for i in hs]
        states = [states[i % nh] * jnp.exp(gl[i]) + mm_tn(k[i] * jnp.exp(gl[i] - gc[i]), v_new[i % nh]) for i in hs]
        ys.append([rmsnorm(o[i % nh], normw) * silu(gates[i]) for i in hs])
    return ys, states, t


@jax.custom_vjp
def split_rows(x):
    n = x.shape[0] // 2
    return [x[:n], x[n:]]


split_rows.defvjp(lambda x: (split_rows(x), None), lambda _, gs: (jnp.concatenate(gs, axis=0),))


@jax.custom_vjp
def split_lanes(x):
    return [x[:, i * LANES:(i + 1) * LANES] for i in range(x.shape[1] // LANES)]


@jax.custom_vjp
def join_lanes(xs):
    return jnp.concatenate(xs, axis=1)


split_lanes.defvjp(lambda x: (split_lanes(x), None), lambda _, gs: (join_lanes(gs),))
join_lanes.defvjp(lambda xs: (join_lanes(xs), None), lambda _, g: (split_lanes(g),))


def ssd_chunk(xs, bm, cm, z, smalls, normw, alog, dtb, dvec, state):
    tri, _, _, last = _chunk_masks()
    tri_w, _, eye_w = _pair_masks()
    h0, h1 = _halves()
    hpg = SSD_HEADS // SSD_GROUPS
    ng = len(normw)
    flat = lambda xss: [x for xs_ in xss for x in xs_]
    xs, bm, cm, z = flat(xs), flat(bm), flat(cm), flat(z)
    units, pairs = range(len(xs)), range(hpg // 2)
    each = lambda f, *ls: [f(*a) for a in zip(*ls)]
    sp_lac = [_log_decay_cumsum(sm, alog, dtb, tri) for sm in smalls]
    lac_last = [jnp.sum(lac * last, axis=0, keepdims=True) for _, lac in sp_lac]
    sel = [_ind(_iota((LANES, SSD_GW), 0) == g * hpg + (_iota((LANES, SSD_GW), 1) >> 6)) for g in range(ng)]
    expand = lambda vs, mode: [mm_nn(vs[i // ng], sel[i % ng], mode) for i in units]
    dt_e = expand([sp for sp, _ in sp_lac], "bf16")
    elac_e = expand([jnp.exp(lac) for _, lac in sp_lac], "bf16")
    toend_e = expand([jnp.exp(ll - lac) for (_, lac), ll in zip(sp_lac, lac_last)], "bf16")
    row8 = _iota((8, 1), 0)
    two_e = expand([_ind(row8 == 0) * dvec + _ind(row8 == 1) * jnp.exp(ll) for ll in lac_last], "sel_b")
    d_e = each(lambda v: jnp.sum(v * _ind(row8 == 0), axis=0, keepdims=True), two_e)
    chunk_e = each(lambda v: jnp.sum(v * _ind(row8 == 1), axis=0, keepdims=True), two_e)
    xdt = each(lambda a, b: a * b, xs, dt_e)
    cb_w = each(lambda c_, b_: mm_nt(c_, jnp.concatenate([b_, b_], axis=0)), cm, bm)
    x_pairs = each(split_lanes, xdt)
    col = lambda i, j: _col_of(sp_lac[i // ng][1], (i % ng) * hpg + j)
    lms = [[_decay_pair(col(i, 2 * p), col(i, 2 * p + 1), tri_w, eye_w) for p in pairs] for i in units]
    stacked = [[jnp.concatenate([x_pairs[i][p] * h0, x_pairs[i][p] * h1], axis=0) for p in pairs] for i in units]
    terms = [[mm_nn(cb_w[i] * lms[i][p], stacked[i][p]) for p in pairs] for i in units]
    y_in = [join_lanes(terms[i]) + xs[i] * d_e[i] for i in units]
    state_in = each(lambda b_, xd, te: mm_tn(b_, xd * te), bm, xdt, toend_e)
    outs = []
    for c in range(len(smalls)):
        us = range(c * ng, (c + 1) * ng)
        y = [mm_nn(cm[i], state[i % ng]) * elac_e[i] + y_in[i] for i in us]
        state = [state[i % ng] * chunk_e[i] + state_in[i] for i in us]
        outs.append([rmsnorm(y[i % ng] * silu(z[i]), normw[i % ng]) for i in us])
    return outs, state


def _params(sem=None):
    return pltpu.CompilerParams(dimension_semantics=sem, vmem_limit_bytes=VMEM_LIMIT)


def _full(shape):
    n = len(shape)
    return pl.BlockSpec(shape, lambda *_: (0,) * n)


ANY = pl.BlockSpec(memory_space=pl.ANY)
HBM = pl.BlockSpec(memory_space=pltpu.HBM)


def in_proj(x, normw, w_main, w_small):
    t = x.shape[0]
    tm, tn = min(1024, t), 512

    def body(x_ref, nw_ref, wm_ref, ws_ref, pm_ref, ps_ref, u_ref):
        @pl.when(pl.program_id(1) == 0)
        def _():
            u = rmsnorm(x_ref[...], nw_ref[...]).astype(MXU_DTYPE)
            u_ref[...] = u
            ps_ref[...] = _raw_dot(u, ws_ref[...], 1, 0)
        pm_ref[...] = _raw_dot(u_ref[...], wm_ref[...], 1, 0)

    return pl.pallas_call(
        body, name="in_proj", grid=(t // tm, COL_CONV // tn),
        in_specs=[pl.BlockSpec((tm, D_MODEL), lambda i, j: (i, 0)), _full((1, D_MODEL)),
                  pl.BlockSpec((D_MODEL, tn), lambda i, j: (0, j)), _full((D_MODEL, LANES))],
        out_specs=[pl.BlockSpec((tm, tn), lambda i, j: (i, j)), pl.BlockSpec((tm, LANES), lambda i, j: (i, 0)),
                   pl.BlockSpec((tm, D_MODEL), lambda i, j: (i, 0))],
        out_shape=[jax.ShapeDtypeStruct((t, COL_CONV), F32), jax.ShapeDtypeStruct((t, LANES), F32),
                   jax.ShapeDtypeStruct((t, D_MODEL), MXU_DTYPE)],
        compiler_params=_params(("arbitrary", "arbitrary")),
    )(x, normw, w_main, w_small)


CONV_TC = 512
HALO = 8


def _shift_down(cur, prev, s):
    rolled = pltpu.roll(cur, s, 0)
    top = jnp.where(_iota((HALO, cur.shape[1]), 0) < s, pltpu.roll(prev, s, 0), rolled[:HALO])
    if cur.shape[0] == HALO:
        return top
    return jnp.concatenate([top, rolled[HALO:]], axis=0)


def _shift_up(cur, nxt, s):
    n = cur.shape[0]
    rolled = pltpu.roll(cur, n - s, 0)
    bot = jnp.where(_iota((HALO, cur.shape[1]), 0) >= HALO - s, pltpu.roll(nxt, HALO - s, 0), rolled[n - HALO:])
    return jnp.concatenate([rolled[:n - HALO], bot], axis=0)


def _conv_pre(cur, prev, w_ref, b, cols=slice(None)):
    acc = cur * w_ref[3:4, cols] + b
    shifted = [cur]
    for s in (1, 2, 3):
        sh = _shift_down(cur, prev, s)
        shifted.append(sh)
        acc = acc + sh * w_ref[3 - s:4 - s, cols]
    return acc, shifted


def in_proj_conv(u, w_main, w, b):
    t = u.shape[0]
    tm, tn = min(2048, t), CONV_TC
    rc = min(256, tm)
    c0, nj = COL_CONV // tn, CONV_W // tn

    def body(u_ref, wm_ref, w_ref, b_ref, pm_ref, out_ref, halo_ref):
        j = pl.program_id(1)

        @pl.when(pl.program_id(0) == 0)
        def _():
            halo_ref[j] = jnp.zeros((HALO, tn), F32)

        prev = halo_ref[j]
        for r in range(tm // rc):
            rows = pl.ds(r * rc, rc)
            p = _raw_dot(u_ref[rows, :], wm_ref[...], 1, 0)
            pm_ref[rows, :] = p
            pre, _ = _conv_pre(p, prev, w_ref, b_ref[...])
            out_ref[rows, :] = silu(pre)
            prev = p[rc - HALO:]
        halo_ref[j] = prev

    return pl.pallas_call(
        body, name="in_proj_conv", grid=(t // tm, nj),
        in_specs=[pl.BlockSpec((tm, D_MODEL), lambda i, j: (i, 0)),
                  pl.BlockSpec((D_MODEL, tn), lambda i, j: (0, c0 + j)),
                  pl.BlockSpec((4, tn), lambda i, j: (0, j)), pl.BlockSpec((1, tn), lambda i, j: (0, j))],
        out_specs=[pl.BlockSpec((tm, tn), lambda i, j: (i, j)), pl.BlockSpec((tm, tn), lambda i, j: (i, j))],
        out_shape=[jax.ShapeDtypeStruct((t, CONV_W), F32), jax.ShapeDtypeStruct((t, CONV_W), F32)],
        scratch_shapes=[pltpu.VMEM((nj, HALO, tn), F32)],
        compiler_params=_params(("arbitrary", "arbitrary")),
    )(u, w_main, w, b)


def _dsilu(pre):
    sg = sigmoid(pre)
    return sg * (1.0 + pre * (1.0 - sg))


def conv_bwd_w(u, proj_conv, w, b, dout, slabbed):
    t = u.shape[0]
    tt, tn = min(512, t), 3 * CONV_TC
    nt, nj = t // tt, CONV_W // tn
    ns = len(slabbed)
    after = lambda i: jnp.minimum((i + 1) * (tt // HALO), t // HALO - 1)

    def body(u_ref, cur_ref, prev_ref, nxt_ref, w_ref, b_ref, do_ref, do_nxt_ref, *rest):
        slab_refs, (dx_ref, dw_ref, dwb_ref) = rest[:ns], rest[ns:ns + 3]
        land_refs, sems = rest[ns + 3:2 * ns + 3], rest[2 * ns + 3:]
        j, i = pl.program_id(0), pl.program_id(1)
        start, finish = _slab_exchange(slab_refs, land_refs, ns, *sems)

        @pl.when(jnp.logical_and(j == 0, i == 0))
        def _():
            start()

        @pl.when(i == 0)
        def _():
            dw_ref[...] = jnp.zeros(dw_ref.shape, F32)
            dwb_ref[...] = jnp.zeros(dwb_ref.shape, F32)

        uu = u_ref[...]
        row = _iota((HALO, CONV_TC), 0)
        first, last = i == 0, i == nt - 1
        for piece in range(tn // CONV_TC):
            cols = slice(piece * CONV_TC, (piece + 1) * CONV_TC)
            cur, bias = cur_ref[:, cols], b_ref[:, cols]
            prev = jnp.where(first, 0.0, prev_ref[:, cols])
            pre, shifted = _conv_pre(cur, prev, w_ref, bias, cols)
            dpre = do_ref[:, cols] * _dsilu(pre)
            pre_nxt, _ = _conv_pre(nxt_ref[:, cols], cur[tt - HALO:], w_ref, bias, cols)
            dpre_nxt = jnp.where(last, 0.0, do_nxt_ref[:, cols] * _dsilu(pre_nxt))
            dx = dpre * w_ref[3:4, cols]
            for s in (1, 2, 3):
                dx = dx + _shift_up(dpre, dpre_nxt, s) * w_ref[3 - s:4 - s, cols]
            dx = dx.astype(dx_ref.dtype)
            dx_ref[:, cols] = dx
            dw_ref[:, cols] += _raw_dot(uu, dx, 0, 0)
            upd = jnp.where(row == 4, jnp.sum(dpre, axis=0, keepdims=True), 0.0)
            for s in range(4):
                upd = upd + jnp.where(row == 3 - s, jnp.sum(dpre * shifted[s], axis=0, keepdims=True), 0.0)
            dwb_ref[:, cols] += upd

        @pl.when(jnp.logical_and(j == nj - 1, last))
        def _():
            finish()

    out = pl.pallas_call(
        body, name="conv_bwd_w", grid=(nj, nt),
        in_specs=[pl.BlockSpec((tt, D_MODEL), lambda j, i: (i, 0)),
                  pl.BlockSpec((tt, tn), lambda j, i: (i, j)),
                  pl.BlockSpec((HALO, tn), lambda j, i: (jnp.maximum(i * (tt // HALO) - 1, 0), j)),
                  pl.BlockSpec((HALO, tn), lambda j, i: (after(i), j)),
                  pl.BlockSpec((4, tn), lambda j, i: (0, j)), pl.BlockSpec((1, tn), lambda j, i: (0, j)),
                  pl.BlockSpec((tt, tn), lambda j, i: (i, j)),
                  pl.BlockSpec((HALO, tn), lambda j, i: (after(i), j))] + [HBM] * ns,
        out_specs=[pl.BlockSpec((tt, tn), lambda j, i: (i, j)), pl.BlockSpec((D_MODEL, tn), lambda j, i: (0, j)),
                   pl.BlockSpec((HALO, tn), lambda j, i: (0, j))] + [HBM] * ns,
        out_shape=[jax.ShapeDtypeStruct((t, CONV_W), MXU_DTYPE), jax.ShapeDtypeStruct((D_MODEL, CONV_W), F32),
                   jax.ShapeDtypeStruct((HALO, CONV_W), F32)] + _slab_exchange_shapes(slabbed, []),
        scratch_shapes=_slab_exchange_sems(ns),
        compiler_params=_params(("arbitrary", "arbitrary")),
    )(u, proj_conv, proj_conv, proj_conv, w, b, dout, dout, *slabbed)
    return out[0], out[1], out[2], out[3:]


def _ssd_cols(g):
    b0 = SSD_WIDTH + g * SSD_STATE
    c0 = SSD_WIDTH + SSD_GROUPS * SSD_STATE + g * SSD_STATE
    return slice(g * SSD_GW, (g + 1) * SSD_GW), slice(b0, b0 + SSD_STATE), slice(c0, c0 + SSD_STATE)


def _gdn_cols(j):
    return tuple(slice(s * GDN_W + j * GDN_DK, s * GDN_W + (j + 1) * GDN_DK) for s in range(3))


def _ssd_parts(xbc_ref):
    return tuple([[xbc_ref[_chunk_rows(c), _ssd_cols(g)[s]] for g in range(SSD_GROUPS)] for c in range(SSD_CB)]
                 for s in range(3))


def _group_cols(ref):
    return [[ref[_chunk_rows(c), g * SSD_GW:(g + 1) * SSD_GW] for g in range(SSD_GROUPS)] for c in range(SSD_CB)]


def _chunk_rows(c):
    return slice(c * CHUNK, (c + 1) * CHUNK)


def _gdn_parts(qkv_ref):
    assert GDN_HB == GDN_HEADS, "the conv block is read whole: one grid step holds every head"
    return tuple([[qkv_ref[_chunk_rows(c), _gdn_cols(j)[s]] for j in range(GDN_HB)] for c in range(GDN_CB)]
                 for s in range(3))


def _head_cols(ref):
    return [[ref[_chunk_rows(c), j * GDN_DV:(j + 1) * GDN_DV] for j in range(GDN_HB)] for c in range(GDN_CB)]


def _chunk_blocks(ref, n=GDN_CB):
    return [ref[_chunk_rows(c), :] for c in range(n)]


def _first_head():
    return 0 if GDN_HB == GDN_HEADS else pl.program_id(1) * GDN_HB


def ssd_fwd(conv_ssd, proj_main, proj_small, normw, alog, dtb, dvec):
    t = conv_ssd.shape[0]
    rows = CHUNK * SSD_CB
    nc = t // rows
    groups = range(SSD_GROUPS)
    norm_cols = lambda ref: [ref[:, g * SSD_GW:(g + 1) * SSD_GW] for g in groups]

    def body(xbc_ref, z_ref, sm_ref, nw_ref, al_ref, db_ref, dv_ref, y_ref, hist_ref, state_ref):
        @pl.when(pl.program_id(0) == 0)
        def _():
            state_ref[...] = jnp.zeros(state_ref.shape, F32)

        states = [state_ref[g] for g in groups]
        for g in groups:
            hist_ref[0, g] = states[g]
        ys, new_states = ssd_chunk(*_ssd_parts(xbc_ref), _group_cols(z_ref), _chunk_blocks(sm_ref, SSD_CB),
                                   norm_cols(nw_ref), al_ref[...], db_ref[...], dv_ref[...], states)
        for c in range(SSD_CB):
            for g in groups:
                y_ref[_chunk_rows(c), g * SSD_GW:(g + 1) * SSD_GW] = ys[c][g].astype(MXU_DTYPE)
        for g in groups:
            state_ref[g] = new_states[g]

    return pl.pallas_call(
        body, name="ssd_fwd", grid=(nc,),
        in_specs=[pl.BlockSpec((rows, SSD_CONV), lambda c: (c, (COL_SSD - COL_CONV) // SSD_CONV)),
                  pl.BlockSpec((rows, SSD_WIDTH), lambda c: (c, COL_Z // SSD_WIDTH)),
                  pl.BlockSpec((rows, LANES), lambda c: (c, 0)),
                  _full((1, SSD_WIDTH)), _full((1, LANES)), _full((1, LANES)), _full((1, LANES))],
        out_specs=[pl.BlockSpec((rows, SSD_WIDTH), lambda c: (c, 0)),
                   pl.BlockSpec((1, SSD_GROUPS, SSD_STATE, SSD_GW), lambda c: (c, 0, 0, 0))],
        out_shape=[jax.ShapeDtypeStruct((t, SSD_WIDTH), MXU_DTYPE),
                   jax.ShapeDtypeStruct((nc, SSD_GROUPS, SSD_STATE, SSD_GW), F32)],
        scratch_shapes=[pltpu.VMEM((SSD_GROUPS, SSD_STATE, SSD_GW), F32)],
        compiler_params=_params(("arbitrary",)),
    )(conv_ssd, proj_main, proj_small, normw, alog, dtb, dvec)


def _accumulate(ref, first, value):
    @pl.when(first)
    def _():
        ref[...] = value

    @pl.when(jnp.logical_not(first))
    def _():
        ref[...] += value


def ssd_bwd(conv_ssd, proj_main, proj_small, normw, alog, dtb, dvec, hist, dy):
    t = conv_ssd.shape[0]
    rows = CHUNK * SSD_CB
    nc = t // rows
    rev = lambda c: nc - 1 - c
    groups = range(SSD_GROUPS)
    norm_cols = lambda ref: [ref[:, g * SSD_GW:(g + 1) * SSD_GW] for g in groups]

    def body(xbc_ref, z_ref, sm_ref, nw_ref, al_ref, db_ref, dv_ref, hist_ref, dy_ref,
             dxbc_ref, dz_ref, dsm_ref, dnw_ref, dal_ref, ddb_ref, ddv_ref, dstate_ref):
        first = pl.program_id(0) == 0

        @pl.when(first)
        def _():
            dstate_ref[...] = jnp.zeros(dstate_ref.shape, F32)

        _, vjp = jax.vjp(ssd_chunk, *_ssd_parts(xbc_ref), _group_cols(z_ref), _chunk_blocks(sm_ref, SSD_CB),
                         norm_cols(nw_ref), al_ref[...], db_ref[...], dv_ref[...], [hist_ref[0, g] for g in groups])
        dxs, dbm, dcm, dz, dsm, dnw, dal, ddb, ddv, dstate = vjp(
            (_group_cols(dy_ref), [dstate_ref[g] for g in groups]))
        for k in range(SSD_CB):
            rk = _chunk_rows(k)
            for g in groups:
                xc, bc, cc = _ssd_cols(g)
                dxbc_ref[rk, xc] = dxs[k][g]
                dxbc_ref[rk, bc] = dbm[k][g]
                dxbc_ref[rk, cc] = dcm[k][g]
                dz_ref[rk, g * SSD_GW:(g + 1) * SSD_GW] = dz[k][g].astype(dz_ref.dtype)
            dsm_ref[rk, :] = dsm[k]
        for g in groups:
            dstate_ref[g] = dstate[g]
        _accumulate(dnw_ref, first, join_lanes(dnw))
        _accumulate(dal_ref, first, dal)
        _accumulate(ddb_ref, first, ddb)
        _accumulate(ddv_ref, first, ddv)

    return pl.pallas_call(
        body, name="ssd_bwd", grid=(nc,),
        in_specs=[pl.BlockSpec((rows, SSD_CONV), lambda c: (rev(c), (COL_SSD - COL_CONV) // SSD_CONV)),
                  pl.BlockSpec((rows, SSD_WIDTH), lambda c: (rev(c), COL_Z // SSD_WIDTH)),
                  pl.BlockSpec((rows, LANES), lambda c: (rev(c), 0)),
                  _full((1, SSD_WIDTH)), _full((1, LANES)), _full((1, LANES)), _full((1, LANES)),
                  pl.BlockSpec((1, SSD_GROUPS, SSD_STATE, SSD_GW), lambda c: (rev(c), 0, 0, 0)),
                  pl.BlockSpec((rows, SSD_WIDTH), lambda c: (rev(c), 0))],
        out_specs=[pl.BlockSpec((rows, SSD_CONV), lambda c: (rev(c), (COL_SSD - COL_CONV) // SSD_CONV)),
                   pl.BlockSpec((rows, SSD_WIDTH), lambda c: (rev(c), COL_Z // SSD_WIDTH)),
                   pl.BlockSpec((rows, LANES), lambda c: (rev(c), 0)),
                   _full((1, SSD_WIDTH)), _full((1, LANES)), _full((1, LANES)), _full((1, LANES))],
        out_shape=[jax.ShapeDtypeStruct((t, CONV_W), F32), jax.ShapeDtypeStruct((t, COL_CONV), MXU_DTYPE),
                   jax.ShapeDtypeStruct((t, LANES), F32), jax.ShapeDtypeStruct((1, SSD_WIDTH), F32),
                   jax.ShapeDtypeStruct((1, LANES), F32), jax.ShapeDtypeStruct((1, LANES), F32),
                   jax.ShapeDtypeStruct((1, LANES), F32)],
        scratch_shapes=[pltpu.VMEM((SSD_GROUPS, SSD_STATE, SSD_GW), F32)],
        compiler_params=_params(("arbitrary",)),
    )(conv_ssd, proj_main, proj_small, normw, alog, dtb, dvec, hist, dy)


def gdn_fwd(conv_gdn, proj_main, proj_small, normw, alog, dtb):
    t = conv_gdn.shape[0]
    hb, cb = GDN_HB, GDN_CB
    rows = CHUNK * cb
    ns = t // rows
    gate_blk = COL_GATE // (GDN_DV * hb)

    def body(qkv_ref, gate_ref, sm_ref, nw_ref, al_ref, db_ref, y_ref, hist_ref, t_ref, state_ref):
        h0 = _first_head()

        @pl.when(pl.program_id(0) == 0)
        def _():
            for j in range(hb):
                state_ref[h0 + j] = jnp.zeros((GDN_DK, GDN_DV), F32)

        states = [state_ref[h0 + j] for j in range(hb)]
        for j in range(hb):
            hist_ref[0, j] = states[j]
        qs, ks, vs = _gdn_parts(qkv_ref)
        ys, new_states, ts = gdn_chunk(h0, qs, ks, vs, _chunk_blocks(sm_ref), _head_cols(gate_ref), nw_ref[...],
                                       al_ref[...], db_ref[...], states)
        for c in range(cb):
            for j in range(hb):
                y_ref[_chunk_rows(c), j * GDN_DV:(j + 1) * GDN_DV] = ys[c][j].astype(MXU_DTYPE)
        for j in range(hb):
            state_ref[h0 + j] = new_states[j]
        for p in range(cb * hb // 2):
            t_ref[0, p] = ts[p]

    return pl.pallas_call(
        body, name="gdn_fwd", grid=(ns, GDN_HEADS // hb),
        in_specs=[pl.BlockSpec((rows, GDN_HC * hb), lambda c, h: (c, h)),
                  pl.BlockSpec((rows, GDN_DV * hb), lambda c, h: (c, gate_blk + h)),
                  pl.BlockSpec((rows, LANES), lambda c, h: (c, 0)),
                  _full((1, GDN_DV)), _full((1, LANES)), _full((1, LANES))],
        out_specs=[pl.BlockSpec((rows, GDN_DV * hb), lambda c, h: (c, h)),
                   pl.BlockSpec((1, hb, GDN_DK, GDN_DV), lambda c, h: (c, h, 0, 0)),
                   pl.BlockSpec((1, cb * hb // 2, CHUNK, LANES), lambda c, h: (c, h, 0, 0))],
        out_shape=[jax.ShapeDtypeStruct((t, GDN_W), MXU_DTYPE),
                   jax.ShapeDtypeStruct((ns, GDN_HEADS, GDN_DK, GDN_DV), F32),
                   jax.ShapeDtypeStruct((ns, cb * GDN_HEADS // 2, CHUNK, LANES), F32)],
        scratch_shapes=[pltpu.VMEM((GDN_HEADS, GDN_DK, GDN_DV), F32)],
        compiler_params=_params(("arbitrary", "arbitrary")),
    )(conv_gdn, proj_main, proj_small, normw, alog, dtb)


def gdn_bwd(dproj_main, dconv, conv_gdn, proj_main, proj_small, normw, alog, dtb, hist, t_inv, dy):
    t = conv_gdn.shape[0]
    hb, cb = GDN_HB, GDN_CB
    rows = CHUNK * cb
    ns = t // rows
    rev = lambda c: ns - 1 - c
    gate_blk = COL_GATE // (GDN_DV * hb)

    def body(alias_ref, alias2_ref, qkv_ref, gate_ref, sm_ref, nw_ref, al_ref, db_ref, hist_ref, t_ref, dy_ref,
             dgate_ref, dqkv_ref, dsm_ref, dnw_ref, dal_ref, ddb_ref, dstate_ref):
        del alias_ref, alias2_ref
        c, h = pl.program_id(0), pl.program_id(1)
        h0 = _first_head()

        @pl.when(c == 0)
        def _():
            for j in range(hb):
                dstate_ref[h0 + j] = jnp.zeros((GDN_DK, GDN_DV), F32)

        saved = [t_ref[0, p] for p in range(cb * hb // 2)]

        def fn(qs, ks, vs, smalls, gates, nw, al, db, states):
            return gdn_chunk(h0, qs, ks, vs, smalls, gates, nw, al, db, states, saved)[:2]

        qs, ks, vs = _gdn_parts(qkv_ref)
        _, vjp = jax.vjp(fn, qs, ks, vs, _chunk_blocks(sm_ref), _head_cols(gate_ref), nw_ref[...], al_ref[...],
                         db_ref[...], [hist_ref[0, j] for j in range(hb)])
        dqs, dks, dvs, dsm, dgates, dnw, dal, ddb, dstates = vjp(
            (_head_cols(dy_ref), [dstate_ref[h0 + j] for j in range(hb)]))
        for k in range(cb):
            rk = _chunk_rows(k)
            for j in range(hb):
                qc, kc, vc = _gdn_cols(j)
                dqkv_ref[rk, qc] = dqs[k][j]
                dqkv_ref[rk, kc] = dks[k][j]
                dqkv_ref[rk, vc] = dvs[k][j]
                dgate_ref[rk, j * GDN_DV:(j + 1) * GDN_DV] = dgates[k][j].astype(dgate_ref.dtype)
        for j in range(hb):
            dstate_ref[h0 + j] = dstates[j]
        _accumulate(dsm_ref, h == 0, jnp.concatenate(dsm, axis=0))
        first = jnp.logical_and(c == 0, h == 0)
        _accumulate(dnw_ref, first, dnw)
        _accumulate(dal_ref, first, dal)
        _accumulate(ddb_ref, first, ddb)

    return pl.pallas_call(
        body, name="gdn_bwd", grid=(ns, GDN_HEADS // hb),
        in_specs=[ANY, ANY, pl.BlockSpec((rows, GDN_HC * hb), lambda c, h: (rev(c), h)),
                  pl.BlockSpec((rows, GDN_DV * hb), lambda c, h: (rev(c), gate_blk + h)),
                  pl.BlockSpec((rows, LANES), lambda c, h: (rev(c), 0)),
                  _full((1, GDN_DV)), _full((1, LANES)), _full((1, LANES)),
                  pl.BlockSpec((1, hb, GDN_DK, GDN_DV), lambda c, h: (rev(c), h, 0, 0)),
                  pl.BlockSpec((1, cb * hb // 2, CHUNK, LANES), lambda c, h: (rev(c), h, 0, 0)),
                  pl.BlockSpec((rows, GDN_DV * hb), lambda c, h: (rev(c), h))],
        out_specs=[pl.BlockSpec((rows, GDN_DV * hb), lambda c, h: (rev(c), gate_blk + h)),
                   pl.BlockSpec((rows, GDN_HC * hb), lambda c, h: (rev(c), h)),
                   pl.BlockSpec((rows, LANES), lambda c, h: (rev(c), 0)),
                   _full((1, GDN_DV)), _full((1, LANES)), _full((1, LANES))],
        out_shape=[jax.ShapeDtypeStruct(dproj_main.shape, dproj_main.dtype),
                   jax.ShapeDtypeStruct(dconv.shape, dconv.dtype),
                   jax.ShapeDtypeStruct((t, LANES), F32), jax.ShapeDtypeStruct((1, GDN_DV), F32),
                   jax.ShapeDtypeStruct((1, LANES), F32), jax.ShapeDtypeStruct((1, LANES), F32)],
        scratch_shapes=[pltpu.VMEM((GDN_HEADS, GDN_DK, GDN_DV), F32)],
        input_output_aliases={0: 0, 1: 1},
        compiler_params=_params(("arbitrary", "arbitrary")),
    )(dproj_main, dconv, conv_gdn, proj_main, proj_small, normw, alog, dtb, hist, t_inv, dy)


def out_proj_loss(x, y_ssd, y_gdn, w_out, final_w, target):
    t = x.shape[0]
    tm = min(512, t)

    def body(x_ref, ys_ref, yg_ref, wo_ref, fw_ref, tg_ref, loss_ref, dhid_ref, dys_ref, dyg_ref, dwo_ref, dfw_ref):
        i = pl.program_id(0)
        ys, yg = ys_ref[...], yg_ref[...]
        wo_s, wo_g = wo_ref[:SSD_WIDTH, :], wo_ref[SSD_WIDTH:, :]
        hid = x_ref[...] + _raw_dot(ys, wo_s, 1, 0) + _raw_dot(yg, wo_g, 1, 0)
        out, vjp = jax.vjp(rmsnorm, hid, fw_ref[...])
        err = out - tg_ref[...]
        loss = 0.5 * jnp.sum(jnp.mean(err * err, axis=-1, keepdims=True), axis=0, keepdims=True)
        dhid, dfw = vjp(err * (1.0 / D_MODEL))
        dhid_ref[...] = dhid
        dys_ref[...] = _raw_dot(dhid, wo_s, 1, 1)
        dyg_ref[...] = _raw_dot(dhid, wo_g, 1, 1)
        first = i == 0
        _accumulate(loss_ref, first, jnp.broadcast_to(loss, loss_ref.shape))
        _accumulate(dfw_ref, first, dfw)

        @pl.when(first)
        def _():
            dwo_ref[:SSD_WIDTH, :] = _raw_dot(ys, dhid, 0, 0)
            dwo_ref[SSD_WIDTH:, :] = _raw_dot(yg, dhid, 0, 0)

        @pl.when(i > 0)
        def _():
            dwo_ref[:SSD_WIDTH, :] += _raw_dot(ys, dhid, 0, 0)
            dwo_ref[SSD_WIDTH:, :] += _raw_dot(yg, dhid, 0, 0)

    row = lambda w: pl.BlockSpec((tm, w), lambda i: (i, 0))
    return pl.pallas_call(
        body, name="out_proj_loss", grid=(t // tm,),
        in_specs=[row(D_MODEL), row(SSD_WIDTH), row(GDN_W), _full((SSD_WIDTH + GDN_W, D_MODEL)), _full((1, D_MODEL)),
                  row(D_MODEL)],
        out_specs=[_full((8, LANES)), row(D_MODEL), row(SSD_WIDTH), row(GDN_W), _full((SSD_WIDTH + GDN_W, D_MODEL)),
                   _full((1, D_MODEL))],
        out_shape=[jax.ShapeDtypeStruct((8, LANES), F32), jax.ShapeDtypeStruct((t, D_MODEL), F32),
                   jax.ShapeDtypeStruct((t, SSD_WIDTH), F32), jax.ShapeDtypeStruct((t, GDN_W), F32),
                   jax.ShapeDtypeStruct((SSD_WIDTH + GDN_W, D_MODEL), F32), jax.ShapeDtypeStruct((1, D_MODEL), F32)],
        compiler_params=_params(("arbitrary",)),
    )(x, y_ssd, y_gdn, w_out, final_w, target)


def in_proj_bwd_x(x, normw, w_main, w_small, dproj_main, dproj_conv, dsmall_a, dsmall_b, dhid, slabbed):
    t = x.shape[0]
    tm = min(256, t)
    ni = t // tm
    ns = len(slabbed)

    def body(x_ref, nw_ref, wm_ref, ws_ref, dp_ref, dc_ref, da_ref, db_ref, dh_ref, *rest):
        slab_refs, (gx_ref, dnw_ref), land_refs = rest[:ns], rest[ns:ns + 2], rest[ns + 2:2 * ns + 2]
        sems = rest[2 * ns + 2:]
        i = pl.program_id(0)
        start, finish = _slab_exchange(slab_refs, land_refs, ns, *sems)

        @pl.when(i == 0)
        def _():
            start()

        du = (_raw_dot(dp_ref[...], wm_ref[:, :COL_CONV], 1, 1) + _raw_dot(dc_ref[...], wm_ref[:, COL_CONV:], 1, 1)
              + _raw_dot(da_ref[...] + db_ref[...], ws_ref[...], 1, 1))
        _, vjp = jax.vjp(rmsnorm, x_ref[...], nw_ref[...])
        dx, dnw = vjp(du)
        gx_ref[...] = dx + dh_ref[...]
        _accumulate(dnw_ref, i == 0, dnw)

        @pl.when(i == ni - 1)
        def _():
            finish()

    row = lambda w: pl.BlockSpec((tm, w), lambda i: (i, 0))
    out = pl.pallas_call(
        body, name="in_proj_bwd_x", grid=(ni,),
        in_specs=[row(D_MODEL), _full((1, D_MODEL)), _full((D_MODEL, MAIN)), _full((D_MODEL, LANES)), row(COL_CONV),
                  row(CONV_W), row(LANES), row(LANES), row(D_MODEL)] + [HBM] * ns,
        out_specs=[row(D_MODEL), _full((1, D_MODEL))] + [HBM] * ns,
        out_shape=[jax.ShapeDtypeStruct((t, D_MODEL), F32), jax.ShapeDtypeStruct((1, D_MODEL), F32)]
        + _slab_exchange_shapes(slabbed, []),
        scratch_shapes=_slab_exchange_sems(ns),
        compiler_params=_params(("arbitrary",)),
    )(x, normw, w_main, w_small, dproj_main, dproj_conv, dsmall_a, dsmall_b, dhid, *slabbed)
    return out[0], out[1], out[2:]


def in_proj_bwd_w(u, dproj_main, dsmall_a, dsmall_b):
    t = u.shape[0]
    tm, tn = min(1024, t), COL_CONV // 2

    def body(u_ref, dp_ref, da_ref, db_ref, dwm_ref, dws_ref):
        j, i = pl.program_id(0), pl.program_id(1)
        uu = u_ref[...]
        _accumulate(dwm_ref, i == 0, _raw_dot(uu, dp_ref[...], 0, 0))

        @pl.when(j == 0)
        def _():
            _accumulate(dws_ref, i == 0, _raw_dot(uu, da_ref[...] + db_ref[...], 0, 0))

    return pl.pallas_call(
        body, name="in_proj_bwd_w", grid=(COL_CONV // tn, t // tm),
        in_specs=[pl.BlockSpec((tm, D_MODEL), lambda j, i: (i, 0)), pl.BlockSpec((tm, tn), lambda j, i: (i, j)),
                  pl.BlockSpec((tm, LANES), lambda j, i: (i, 0)), pl.BlockSpec((tm, LANES), lambda j, i: (i, 0))],
        out_specs=[pl.BlockSpec((D_MODEL, tn), lambda j, i: (0, j)), _full((D_MODEL, LANES))],
        out_shape=[jax.ShapeDtypeStruct((D_MODEL, COL_CONV), F32), jax.ShapeDtypeStruct((D_MODEL, LANES), F32)],
        compiler_params=_params(("arbitrary", "arbitrary")),
    )(u, dproj_main, dsmall_a, dsmall_b)


def sum_slabs(a, name):
    n, rows, cols = a.shape
    tr = 64 if rows % 64 == 0 else rows

    def body(a_ref, o_ref):
        acc = a_ref[0].astype(F32)
        for d in range(1, n):
            acc = acc + a_ref[d].astype(F32)
        o_ref[...] = acc

    return pl.pallas_call(
        body, name=name, grid=(rows // tr,),
        in_specs=[pl.BlockSpec((n, tr, cols), lambda i: (0, i, 0))],
        out_specs=pl.BlockSpec((tr, cols), lambda i: (i, 0)),
        out_shape=jax.ShapeDtypeStruct((rows, cols), F32),
        compiler_params=_params(("arbitrary",)),
    )(a)


def adamw(w, g, m, v, name):
    _, rows, cols = w.shape
    tr = 128 if rows % 128 == 0 else rows

    def body(w_ref, g_ref, m_ref, v_ref, d_ref, nm_ref, nv_ref):
        gg = g_ref[...]
        nm = ADAM_B1 * m_ref[...] + (1.0 - ADAM_B1) * gg
        nv = ADAM_B2 * v_ref[...] + (1.0 - ADAM_B2) * (gg * gg)
        m_hat = nm / (1.0 - ADAM_B1 ** ADAM_STEP)
        v_hat = nv / (1.0 - ADAM_B2 ** ADAM_STEP)
        d_ref[...] = -ADAM_LR * (m_hat / (jnp.sqrt(v_hat) + ADAM_EPS) + ADAM_WD * w_ref[...])
        nm_ref[...] = nm
        nv_ref[...] = nv

    spec = pl.BlockSpec((1, tr, cols), lambda i: (0, i, 0))
    shp = jax.ShapeDtypeStruct((1, rows, cols), F32)
    return pl.pallas_call(
        body, name=name, grid=(rows // tr,), in_specs=[spec] * 4, out_specs=[spec] * 3, out_shape=[shp] * 3,
        compiler_params=_params(("arbitrary",)),
    )(w, g.reshape(w.shape), m, v)


def _my_place():
    return lax.axis_index("x"), lax.axis_index("y"), lax.axis_index("c")


def gather_weights(big, small):
    nb, n = len(big), len(big) + len(small)
    parts = 4

    def body(*refs):
        srcs, outs = refs[:n], refs[n:2 * n]
        land_a, land_b = refs[2 * n:2 * n + nb], refs[2 * n + nb:2 * n + 2 * nb]
        send_sems, recv_sems, fwd_send, fwd_recv, local_sems = refs[2 * n + 2 * nb:]
        x, y, c = _my_place()
        me = 2 * x + y
        chips = [(1 - x, y), (x, 1 - y), (1 - x, 1 - y)]
        half = [a.shape[0] // 2 for a in big]

        def ici(j, i):
            px, py = chips[j]
            if i < nb:
                src, dst = srcs[i].at[pl.ds(c * half[i], half[i])], land_a[i].at[j]
            else:
                src, dst = srcs[i], outs[i].at[me]
            return pltpu.make_async_remote_copy(src_ref=src, dst_ref=dst, send_sem=send_sems.at[j * n + i],
                                                recv_sem=recv_sems.at[j * n + i], device_id=(px, py, c),
                                                device_id_type=MESH)

        def ici_arrival(j, i):
            px, py = chips[j]
            dst = land_a[i].at[j] if i < nb else outs[i].at[2 * px + py]
            return pltpu.make_async_remote_copy(src_ref=dst, dst_ref=dst, send_sem=send_sems.at[j * n + i],
                                                recv_sem=recv_sems.at[j * n + i], device_id=(px, py, c),
                                                device_id_type=MESH)

        def forward(j, i, p):
            rows = half[i] // parts
            k = (j * nb + i) * parts + p
            return pltpu.make_async_remote_copy(
                src_ref=land_a[i].at[j, pl.ds(p * rows, rows)], dst_ref=land_b[i].at[j, pl.ds(p * rows, rows)],
                send_sem=fwd_send.at[k], recv_sem=fwd_recv.at[k], device_id=(x, y, 1 - c), device_id_type=MESH)

        def store(j, i, from_sibling):
            px, py = chips[j]
            buf, h = (land_b, 1 - c) if from_sibling else (land_a, c)
            k = n + (j * nb + i) * 2 + (1 if from_sibling else 0)
            return pltpu.make_async_copy(buf[i].at[j], outs[i].at[2 * px + py, pl.ds(h * half[i], half[i])],
                                         local_sems.at[k])

        own = [pltpu.make_async_copy(srcs[i], outs[i].at[me], local_sems.at[i]) for i in range(n)]
        sends = [ici(j, i) for j in range(3) for i in range(n)]
        for cp in own + sends:
            cp.start()
        pending = []
        for j in range(3):
            for i in range(n):
                ici_arrival(j, i).wait_recv()
                if i < nb:
                    fw = [forward(j, i, p) for p in range(parts)]
                    st = store(j, i, False)
                    for cp in fw + [st]:
                        cp.start()
                    pending += [cp.wait_send for cp in fw] + [st.wait]
        for j in range(3):
            for i in range(nb):
                for p in range(parts):
                    forward(j, i, p).wait_recv()
                st = store(j, i, True)
                st.start()
                pending.append(st.wait)
        for cp in sends:
            cp.wait_send()
        for wait in pending:
            wait()
        for cp in own:
            cp.wait()

    shards = list(big) + list(small)
    lands = [pltpu.VMEM((3, a.shape[0] // 2) + a.shape[1:], a.dtype) for a in big]
    return pl.pallas_call(
        body, name="gather_weights",
        in_specs=[HBM] * n, out_specs=[HBM] * n,
        out_shape=[jax.ShapeDtypeStruct((N_CHIP,) + s.shape, s.dtype) for s in shards],
        scratch_shapes=lands + lands + [
            pltpu.SemaphoreType.DMA((3 * n,)), pltpu.SemaphoreType.DMA((3 * n,)),
            pltpu.SemaphoreType.DMA((3 * nb * parts,)), pltpu.SemaphoreType.DMA((3 * nb * parts,)),
            pltpu.SemaphoreType.DMA((n + 6 * nb,))],
        compiler_params=pltpu.CompilerParams(vmem_limit_bytes=VMEM_LIMIT),
    )(*shards)


def _peer(x, y, c, mask):
    mx, my, mc = (mask >> 2) & 1, (mask >> 1) & 1, mask & 1
    return (x ^ mx if mx else x, y ^ my if my else y, c ^ mc if mc else c)


def _slab_exchange_shapes(slabbed, replicated):
    return ([jax.ShapeDtypeStruct(a.shape, a.dtype) for a in slabbed]
            + [jax.ShapeDtypeStruct((N_DEV,) + a.shape, a.dtype) for a in replicated])


def _slab_exchange_sems(n):
    return [pltpu.SemaphoreType.DMA((7 * n,)), pltpu.SemaphoreType.DMA((7 * n,)), pltpu.SemaphoreType.DMA((n,))]


def _slab_exchange(srcs, outs, ns, send_sems, recv_sems, local_sems):
    n = len(srcs)
    x, y, c = _my_place()
    me = 4 * x + 2 * y + c

    def piece(i, dev):
        return srcs[i].at[dev] if i < ns else srcs[i]

    def copies(arriving):
        out = []
        for mask in range(1, N_DEV):
            px, py, pc = _peer(x, y, c, mask)
            dev = 4 * px + 2 * py + pc
            for i in range(n):
                k = (mask - 1) * n + i
                out.append(pltpu.make_async_remote_copy(
                    src_ref=piece(i, dev), dst_ref=outs[i].at[dev if arriving else me], send_sem=send_sems.at[k],
                    recv_sem=recv_sems.at[k], device_id=(px, py, pc), device_id_type=MESH))
        return out

    def local():
        return [pltpu.make_async_copy(piece(i, me), outs[i].at[me], local_sems.at[i]) for i in range(n)]

    def start():
        for cp in local() + copies(False):
            cp.start()

    def finish():
        for cp in copies(True):
            cp.wait_recv()
        for cp in copies(False):
            cp.wait_send()
        for cp in local():
            cp.wait()

    return start, finish


def exchange_halves(landed, replicated):
    n, nr = len(landed), len(replicated)
    streams = 8
    halves = [jax.ShapeDtypeStruct(a.shape[1:], F32) for a in landed]
    sum_rows = 64

    def body(*refs):
        srcs, rep_srcs, outs, rep_outs = refs[:n], refs[n:n + nr], refs[n + nr:2 * n + nr], refs[2 * n + nr:2 * (n + nr)]
        refs = refs[2 * (n + nr):]
        slabs, mine, theirs = refs[:n], refs[n:2 * n], refs[2 * n:3 * n]
        send_sems, recv_sems, in_sems, out_sems = refs[3 * n:3 * n + 4]
        rep_start, rep_finish = _slab_exchange(rep_srcs, rep_outs, 0, *refs[3 * n + 4:])
        rep_start()
        x, y, c = _my_place()
        loads = [pltpu.make_async_copy(srcs[i], slabs[i], in_sems.at[i]) for i in range(n)]
        for cp in loads:
            cp.start()
        for i in range(n):
            loads[i].wait()
            for r in range(0, halves[i].shape[0], sum_rows):
                rows = pl.ds(r, sum_rows)
                acc = slabs[i][0, rows, :].astype(F32)
                for d in range(1, N_DEV):
                    acc = acc + slabs[i][d, rows, :].astype(F32)
                mine[i][rows, :] = acc

        def chunk_copy(i, s):
            rows = halves[i].shape[0] // streams
            k = i * streams + s
            return pltpu.make_async_remote_copy(
                src_ref=mine[i].at[pl.ds(s * rows, rows)], dst_ref=theirs[i].at[pl.ds(s * rows, rows)],
                send_sem=send_sems.at[k], recv_sem=recv_sems.at[k], device_id=(x, y, 1 - c), device_id_type=MESH)

        sends = [chunk_copy(i, s) for i in range(n) for s in range(streams)]
        for cp in sends:
            cp.start()
        own = [pltpu.make_async_copy(mine[i], outs[i].at[c], out_sems.at[i]) for i in range(n)]
        for cp in own:
            cp.start()
        for cp in sends:
            cp.wait_recv()
        got = [pltpu.make_async_copy(theirs[i], outs[i].at[1 - c], out_sems.at[n + i]) for i in range(n)]
        for cp in got:
            cp.start()
        for cp in sends:
            cp.wait_send()
        for cp in own + got:
            cp.wait()
        rep_finish()

    vmem = [pltpu.VMEM(a.shape, a.dtype) for a in halves]
    out = pl.pallas_call(
        body, name="exchange_halves",
        in_specs=[HBM] * (n + nr), out_specs=[HBM] * (n + nr),
        out_shape=[jax.ShapeDtypeStruct((2,) + a.shape, a.dtype) for a in halves]
        + _slab_exchange_shapes([], replicated),
        scratch_shapes=[pltpu.VMEM(a.shape, a.dtype) for a in landed] + vmem + vmem
        + [pltpu.SemaphoreType.DMA((n * streams,)), pltpu.SemaphoreType.DMA((n * streams,)),
           pltpu.SemaphoreType.DMA((n,)), pltpu.SemaphoreType.DMA((2 * n,))] + _slab_exchange_sems(nr),
        compiler_params=pltpu.CompilerParams(vmem_limit_bytes=VMEM_LIMIT),
    )(*landed, *replicated)
    return out[:n], out[n:]


def _pack_cols(pieces):
    offs, pos = [], 0
    for a in pieces:
        offs.append(pos)
        pos += a.shape[1]
    rows8 = [jnp.pad(a.astype(F32), ((0, 8 - a.shape[0]), (0, 0))) for a in pieces]
    return jnp.concatenate(rows8, axis=1), offs


def adamw_many(ws, gs, ms, vs):
    n = len(ws)

    def body(*refs):
        w_r, g_r, m_r, v_r = refs[:n], refs[n:2 * n], refs[2 * n:3 * n], refs[3 * n:4 * n]
        d_o, m_o, v_o = refs[4 * n:5 * n], refs[5 * n:6 * n], refs[6 * n:7 * n]
        for i in range(n):
            gg = g_r[i][...]
            nm = ADAM_B1 * m_r[i][...] + (1.0 - ADAM_B1) * gg
            nv = ADAM_B2 * v_r[i][...] + (1.0 - ADAM_B2) * (gg * gg)
            m_hat = nm / (1.0 - ADAM_B1 ** ADAM_STEP)
            v_hat = nv / (1.0 - ADAM_B2 ** ADAM_STEP)
            d_o[i][...] = -ADAM_LR * (m_hat / (jnp.sqrt(v_hat) + ADAM_EPS) + ADAM_WD * w_r[i][...])
            m_o[i][...] = nm
            v_o[i][...] = nv

    shapes = [jax.ShapeDtypeStruct(w.shape, F32) for w in ws]
    out = pl.pallas_call(body, name="adamw_small", out_shape=shapes * 3,
                         compiler_params=pltpu.CompilerParams(vmem_limit_bytes=VMEM_LIMIT))(*ws, *gs, *ms, *vs)
    return out[:n], out[n:2 * n], out[2 * n:]


def _lanes(vec, start):
    n = vec.shape[-1]
    return jnp.pad(vec.reshape(1, n).astype(F32), ((0, 0), (start, LANES - start - n)))


def kernel(x, norm_w, w_in, ssd_conv_w, ssd_conv_b, ssd_dt_bias, ssd_a_log, ssd_d, ssd_norm_w, gdn_conv_w, gdn_dt_bias, gdn_a_log, gdn_norm_w, w_out, final_norm_w, loss_target, m_norm_w, m_w_in, m_ssd_conv_w, m_ssd_conv_b, m_ssd_dt_bias, m_ssd_a_log, m_ssd_d, m_ssd_norm_w, m_gdn_conv_w, m_gdn_dt_bias, m_gdn_a_log, m_gdn_norm_w, m_w_out, m_final_norm_w, v_norm_w, v_w_in, v_ssd_conv_w, v_ssd_conv_b, v_ssd_dt_bias, v_ssd_a_log, v_ssd_d, v_ssd_norm_w, v_gdn_conv_w, v_gdn_dt_bias, v_gdn_a_log, v_gdn_norm_w, v_w_out, v_final_norm_w):
    xs = x[0]
    target = loss_target[0]
    chip = 2 * lax.axis_index("x") + lax.axis_index("y")
    w_in_shard, w_out_shard = w_in[0], w_out[0]
    in_cols = w_in_shard.shape[1]
    out_rows = w_out_shard.shape[0]

    g_in, g_out, g_cs, g_cg = gather_weights(
        [w_in_shard.astype(MXU_DTYPE), w_out_shard.astype(MXU_DTYPE)], [ssd_conv_w[0], gdn_conv_w[0]])
    w_in_full = jnp.concatenate([g_in[k] for k in range(N_CHIP)], axis=1)
    w_out_full = g_out.reshape(N_CHIP * out_rows, D_MODEL)
    cw_ssd = jnp.concatenate([g_cs[k] for k in range(N_CHIP)], axis=1)
    cw_gdn = jnp.concatenate([g_cg[k] for k in range(N_CHIP)], axis=1)
    cb_ssd, cb_gdn = ssd_conv_b, jnp.zeros((1, GDN_CONV), F32)
    o_xbc, o_dt, o_gate, o_qkv, o_ab = 1024, 2560, 2576, 3600, 6672
    w_main = jnp.concatenate([w_in_full[:, :o_xbc], w_in_full[:, o_gate:o_qkv], w_in_full[:, o_qkv:o_ab],
                              w_in_full[:, o_xbc:o_dt]], axis=1)
    w_small = jnp.concatenate([w_in_full[:, o_dt:o_gate], w_in_full[:, o_ab:],
                               jnp.zeros((D_MODEL, LANES - 32), MXU_DTYPE)], axis=1)
    alog = _lanes(ssd_a_log, 0) + _lanes(gdn_a_log, LANE_GA)
    dtb = _lanes(ssd_dt_bias, 0) + _lanes(gdn_dt_bias, LANE_GA)
    dvec = _lanes(ssd_d, 0)
    fw = final_norm_w.reshape(1, D_MODEL)

    cw, cb = jnp.concatenate([cw_gdn, cw_ssd], axis=1), jnp.concatenate([cb_gdn, cb_ssd], axis=1)
    proj_main, proj_small, u = in_proj(xs, norm_w, w_main, w_small)
    proj_conv, conv_out = in_proj_conv(u, w_main, cw, cb)
    y_ssd, hist_ssd = ssd_fwd(conv_out, proj_main, proj_small, ssd_norm_w, alog, dtb, dvec)
    y_gdn, hist_gdn, tinv_gdn = gdn_fwd(conv_out, proj_main, proj_small, gdn_norm_w, alog, dtb)

    loss_blk, dhid, dy_ssd, dy_gdn, d_w_out, d_fw = out_proj_loss(xs, y_ssd, y_gdn, w_out_full, fw, target)
    dconv, dproj_main, dsmall_ssd, d_ssd_nw, d_alog_s, d_dtb_s, d_dvec = ssd_bwd(
        conv_out, proj_main, proj_small, ssd_norm_w, alog, dtb, dvec, hist_ssd, dy_ssd)
    dproj_main, dconv, dsmall_gdn, d_gdn_nw, d_alog_g, d_dtb_g = gdn_bwd(
        dproj_main, dconv, conv_out, proj_main, proj_small, gdn_norm_w, alog, dtb, hist_gdn, tinv_gdn, dy_gdn)
    slabs_out = d_w_out.reshape(N_DEV, out_rows // 2, D_MODEL).astype(COMM_DTYPE)
    dproj_conv, d_w_conv, dwb, (r_out,) = conv_bwd_w(u, proj_conv, cw, cb, dconv, [slabs_out])
    dwb_gdn, dwb_ssd = dwb[:, :GDN_CONV], dwb[:, GDN_CONV:]
    d_w_zg, d_w_small = in_proj_bwd_w(u, dproj_main, dsmall_ssd, dsmall_gdn)
    order = [(d_w_zg, 0, COL_GATE), (d_w_conv, COL_SSD - COL_CONV, CONV_W), (d_w_small, 0, 16),
             (d_w_zg, COL_GATE, COL_CONV), (d_w_conv, 0, COL_SSD - COL_CONV), (d_w_small, 16, 32)]
    shards, pos = [[] for _ in range(N_CHIP)], 0
    for src, lo, hi in order:
        while lo < hi:
            k = pos // in_cols
            n = min(hi - lo, (k + 1) * in_cols - pos)
            shards[k].append(src[:, lo:lo + n].astype(COMM_DTYPE))
            lo, pos = lo + n, pos + n
    slabs_in = jnp.stack([jnp.concatenate(p, axis=1) for p in shards]).reshape(N_DEV, D_MODEL // 2, in_cols)
    grad_x, d_norm_w, (r_in,) = in_proj_bwd_x(xs, norm_w, w_main, w_small, dproj_main, dproj_conv, dsmall_ssd,
                                               dsmall_gdn, dhid, [slabs_in])
    d_alog, d_dtb = d_alog_s + d_alog_g, d_dtb_s + d_dtb_g
    packed, (o_nw, o_cs, o_cg, o_snw, o_fw, o_al, o_db, o_dv, o_gnw, o_loss) = _pack_cols([
        d_norm_w, dwb_ssd, dwb_gdn,
        d_ssd_nw.reshape(1, SSD_WIDTH), d_fw, d_alog, d_dtb, d_dvec, d_gdn_nw, loss_blk])

    (full_in, full_out), (r_small,) = exchange_halves([r_in, r_out], [packed])
    tot = sum_slabs(r_small, "sum_small")
    grad_w_in = full_in.reshape(D_MODEL, in_cols)
    grad_w_out = full_out.reshape(out_rows, D_MODEL)
    loss = tot[0, o_loss]
    sc, gc = ssd_conv_w.shape[2], gdn_conv_w.shape[2]
    row = lambda off, n, r=0: tot[r:r + 1, off:off + n]
    gs = [row(o_nw, D_MODEL),
          lax.dynamic_slice(tot, (0, o_cs + chip * sc), (4, sc)),
          row(o_cs, SSD_CONV, 4),
          row(o_db, SSD_HEADS), row(o_al, SSD_HEADS), row(o_dv, SSD_HEADS),
          row(o_snw, SSD_WIDTH),
          lax.dynamic_slice(tot, (0, o_cg + chip * gc), (4, gc)),
          row(o_db + LANE_GA, GDN_HEADS), row(o_al + LANE_GA, GDN_HEADS),
          row(o_gnw, GDN_DV), row(o_fw, D_MODEL)]

    names = ["norm_w", "ssd_conv_w", "ssd_conv_b", "ssd_dt_bias", "ssd_a_log", "ssd_d", "ssd_norm_w", "gdn_conv_w",
             "gdn_dt_bias", "gdn_a_log", "gdn_norm_w", "final_norm_w"]
    ws = [norm_w, ssd_conv_w, ssd_conv_b, ssd_dt_bias, ssd_a_log, ssd_d, ssd_norm_w, gdn_conv_w, gdn_dt_bias,
          gdn_a_log, gdn_norm_w, final_norm_w]
    ms = [m_norm_w, m_ssd_conv_w, m_ssd_conv_b, m_ssd_dt_bias, m_ssd_a_log, m_ssd_d, m_ssd_norm_w, m_gdn_conv_w,
          m_gdn_dt_bias, m_gdn_a_log, m_gdn_norm_w, m_final_norm_w]
    vs = [v_norm_w, v_ssd_conv_w, v_ssd_conv_b, v_ssd_dt_bias, v_ssd_a_log, v_ssd_d, v_ssd_norm_w, v_gdn_conv_w,
          v_gdn_dt_bias, v_gdn_a_log, v_gdn_norm_w, v_final_norm_w]
    shapes = [w.shape for w in ws]
    flat = lambda arrs: [a.reshape(g.shape) for a, g in zip(arrs, gs)]
    d_s, m_s, v_s = adamw_many(flat(ws), gs, flat(ms), flat(vs))
    back = lambda arrs: dict(zip(names, [a.reshape(s) for a, s in zip(arrs, shapes)]))
    delta, new_m, new_v, grads = back(d_s), back(m_s), back(v_s), back(gs)
    d_in, m_in, v_in = adamw(w_in, grad_w_in, m_w_in, v_w_in, "adamw_w_in")
    d_out, m_out, v_out = adamw(w_out, grad_w_out, m_w_out, v_w_out, "adamw_w_out")
    for tbl, a_in, a_out in ((grads, grad_w_in[None], grad_w_out[None]), (delta, d_in, d_out), (new_m, m_in, m_out),
                             (new_v, v_in, v_out)):
        tbl["w_in"] = a_in
        tbl["w_out"] = a_out

    order = ["norm_w", "w_in", "ssd_conv_w", "ssd_conv_b", "ssd_dt_bias", "ssd_a_log", "ssd_d", "ssd_norm_w",
             "gdn_conv_w", "gdn_dt_bias", "gdn_a_log", "gdn_norm_w", "w_out", "final_norm_w"]
    return (loss.reshape(()), grad_x[None], *[grads[k] for k in order], *[delta[k] for k in order],
            *[new_m[k] for k in order], *[new_v[k] for k in order])
```

```python
import functools

import jax
import jax.numpy as jnp
from jax import lax
from jax.experimental import pallas as pl
from jax.experimental.pallas import tpu as pltpu

F32 = jnp.float32
MXU_DTYPE = jnp.bfloat16
COMM_DTYPE = jnp.bfloat16
MESH = pl.DeviceIdType.MESH

D_MODEL = 1024
CHUNK = 64
EPS = 1e-6
SSD_HEADS, SSD_GROUPS, SSD_STATE = 16, 2, 128
SSD_WIDTH, SSD_CONV = 1024, 1536
SSD_GW = SSD_WIDTH // SSD_GROUPS
GDN_HEADS, GDN_DK, GDN_DV = 8, 128, 128
GDN_W, GDN_CONV = 1024, 3072
GDN_HC = 2 * GDN_DK + GDN_DV
IN_DIM = 6688
MAIN = 6656
LANES = 128
COL_Z, COL_GATE, COL_GDN, COL_SSD = 0, 1024, 2048, 5120
COL_CONV = COL_GDN
CONV_W = MAIN - COL_CONV
GDN_HB = 8
GDN_CB = 4
SSD_CB = 4
LANE_GA, LANE_GB = 16, 24
N_DEV, N_CHIP = 8, 4
VMEM_LIMIT = 52 * 1024 * 1024

ADAM_LR, ADAM_B1, ADAM_B2, ADAM_EPS, ADAM_WD, ADAM_STEP = 0.001, 0.9, 0.999, 1e-08, 0.01, 10


def _split(a, n):
    parts, rest = [], a.astype(F32)
    for i in range(n):
        p = rest.astype(MXU_DTYPE)
        parts.append(p)
        if i < n - 1:
            rest = rest - p.astype(F32)
    return parts


def _raw_dot(a, b, ca, cb, mode="bf16"):
    d = lambda u, v: lax.dot_general(u, v, (((ca,), (cb,)), ((), ())), preferred_element_type=F32)
    if mode == "bf16":
        return d(a.astype(MXU_DTYPE), b.astype(MXU_DTYPE))
    if mode == "x3":
        (ah, al), (bh, bl) = _split(a, 2), _split(b, 2)
        return d(ah, bh) + (d(ah, bl) + d(al, bh))
    if mode == "sel_a":
        a0 = a.astype(MXU_DTYPE)
        b1, b2, b3 = _split(b, 3)
        return d(a0, b1) + (d(a0, b2) + d(a0, b3))
    assert mode == "sel_b", mode
    b0 = b.astype(MXU_DTYPE)
    a1, a2, a3 = _split(a, 3)
    return d(a1, b0) + (d(a2, b0) + d(a3, b0))


@functools.partial(jax.custom_vjp, nondiff_argnums=(2,))
def mm_nn(a, b, mode="bf16"):
    return _raw_dot(a, b, 1, 0, mode)


@functools.partial(jax.custom_vjp, nondiff_argnums=(2,))
def mm_nt(a, b, mode="bf16"):
    return _raw_dot(a, b, 1, 1, mode)


@functools.partial(jax.custom_vjp, nondiff_argnums=(2,))
def mm_tn(a, b, mode="bf16"):
    return _raw_dot(a, b, 0, 0, mode)


_SAME = {"bf16": ("bf16", "bf16"), "x3": ("x3", "x3")}
_NN_BWD = dict(_SAME, sel_a=("bf16", "sel_a"), sel_b=("sel_b", "bf16"))
_NT_BWD = dict(_SAME, sel_a=("bf16", "sel_b"), sel_b=("sel_b", "bf16"))
_TN_BWD = dict(_SAME, sel_a=("bf16", "sel_a"), sel_b=("sel_a", "bf16"))
mm_nn.defvjp(lambda a, b, m: (_raw_dot(a, b, 1, 0, m), (a, b)),
             lambda m, r, g: (mm_nt(g, r[1], _NN_BWD[m][0]), mm_tn(r[0], g, _NN_BWD[m][1])))
mm_nt.defvjp(lambda a, b, m: (_raw_dot(a, b, 1, 1, m), (a, b)),
             lambda m, r, g: (mm_nn(g, r[1], _NT_BWD[m][0]), mm_tn(g, r[0], _NT_BWD[m][1])))
mm_tn.defvjp(lambda a, b, m: (_raw_dot(a, b, 0, 0, m), (a, b)),
             lambda m, r, g: (mm_nt(r[1], g, _TN_BWD[m][0]), mm_nn(r[0], g, _TN_BWD[m][1])))


@jax.custom_jvp
def sigmoid(x):
    return 1.0 / (1.0 + jnp.exp(-x))


@sigmoid.defjvp
def _sigmoid_jvp(p, t):
    s = sigmoid(p[0])
    return s, t[0] * s * (1.0 - s)


@jax.custom_jvp
def softplus(x):
    return jnp.maximum(x, 0.0) + jnp.log(1.0 + jnp.exp(-jnp.abs(x)))


@softplus.defjvp
def _softplus_jvp(p, t):
    return softplus(p[0]), t[0] * sigmoid(p[0])


def silu(x):
    return x * sigmoid(x)


def rmsnorm(x, w):
    return x * lax.rsqrt(jnp.mean(x * x, axis=-1, keepdims=True) + EPS) * w


def _iota(shape, dim):
    return lax.broadcasted_iota(jnp.int32, shape, dim)


def _halves():
    lane = _iota((1, LANES), 1) >> 6
    return _ind(lane == 0), _ind(lane == 1)


def _block_diag(pair):
    h0, h1 = _halves()
    return jnp.concatenate([pair * h0, pair * h1], axis=0)


def _tri_inv_impl(mats):
    r, c = _iota((CHUNK, LANES), 0), _iota((CHUNK, LANES), 1) & (CHUNK - 1)
    eye = _ind(r == c)
    blockdiag = _ind((r >> 4) == (c >> 4))
    dot = lambda u, v: _raw_dot(u, _block_diag(v), 1, 0, "x3")
    dot1 = lambda u, v: _raw_dot(u, _block_diag(v), 1, 0)
    each = lambda f, *ls: [f(*xs) for xs in zip(*ls)]
    dg = each(lambda a: a * blockdiag, mats)
    off = each(lambda a, d: a - d, mats, dg)
    m = each(lambda d: -d, dg)
    p = each(lambda x: eye + x, m)
    pw = m
    for _ in range(3):
        pw = each(lambda x: dot1(x, x), pw)
        p = each(lambda x, y: x + dot1(x, y), p, pw)
    e = each(dot, p, off)
    e2 = each(lambda x: dot1(x, x), e)
    q = each(lambda x: eye - x, e)
    q = each(lambda x, y: x + dot1(x, y), q, e2)
    return each(dot, q, p)


def _tri_inv_bwd(ts, gs):
    h0, h1 = _halves()
    x = [mm_nt(g, _block_diag(t)) for g, t in zip(gs, ts)]
    full = [mm_tn(t, y) for t, y in zip(ts, x)]
    return [-(f[:CHUNK] * h0 + f[CHUNK:] * h1) for f in full]


@jax.custom_vjp
def tri_inv(mats):
    return _tri_inv_impl(mats)


def _tri_inv_fwd(mats):
    ts = _tri_inv_impl(mats)
    return ts, ts


tri_inv.defvjp(_tri_inv_fwd, lambda ts, gs: (_tri_inv_bwd(ts, gs),))


@jax.custom_vjp
def tri_inv_saved(mats, ts):
    del mats
    return ts


tri_inv_saved.defvjp(lambda mats, ts: (ts, ts),
                     lambda ts, gs: (_tri_inv_bwd(ts, gs), [jnp.zeros_like(t) for t in ts]))


def _ind(cond):
    return jnp.where(cond, 1.0, 0.0).astype(F32)


def _chunk_masks():
    r, c = _iota((CHUNK, CHUNK), 0), _iota((CHUNK, CHUNK), 1)
    return _ind(r >= c), _ind(r > c), _ind(r == c), _ind(_iota((CHUNK, 1), 0) == CHUNK - 1)


def _log_decay_cumsum(small, alog, dtb, tri):
    sp = softplus(small + dtb)
    la = -jnp.exp(alog) * sp
    return sp, mm_nn(tri, la, "sel_a")


def _col_of(x, lane):
    return jnp.sum(x * _ind(_iota((1, LANES), 1) == lane), axis=1, keepdims=True)


def _pair_masks():
    r, c = _iota((CHUNK, LANES), 0), _iota((CHUNK, LANES), 1)
    c6 = c & (CHUNK - 1)
    return _ind(r >= c6), _ind(r > c6), (_ind(c == r), _ind(c == r + CHUNK))


def _decay_pair(col_a, col_b, tri_w, eye_w):
    h0, h1 = _halves()
    col = col_a * h0 + col_b * h1
    row = jnp.sum(col_a * eye_w[0] + col_b * eye_w[1], axis=0, keepdims=True)
    return jnp.exp((col - row) * tri_w) * tri_w


def gdn_chunk(h0, qs, ks, vs, smalls, gates, normw, alog, dtb, states, saved_t=None):
    tri, _, _, last = _chunk_masks()
    tri_w, strict_w, eye_w = _pair_masks()
    nh = len(qs[0])
    flat = lambda xss: [x for xs in xss for x in xs]
    lacs = [_log_decay_cumsum(sm, alog, dtb, tri)[1] for sm in smalls]
    qs, ks, vs, gates = flat(qs), flat(ks), flat(vs), flat(gates)
    heads, pairs = range(len(qs)), range(len(qs) // 2)
    each = lambda f, *ls: [f(*xs) for xs in zip(*ls)]
    ab = lambda xs, p: (xs[2 * p], xs[2 * p + 1])
    stack = lambda xs: jnp.concatenate(xs, axis=0)
    gc = [_col_of(lacs[i // nh], LANE_GA + h0 + i % nh) for i in heads]
    beta = [sigmoid(_col_of(smalls[i // nh], LANE_GB + h0 + i % nh)) for i in heads]
    decay = [_decay_pair(*ab(gc, p), tri_w, eye_w) for p in pairs]
    gl = each(lambda x: jnp.sum(x * last, axis=0, keepdims=True), gc)
    q = each(lambda x: x * lax.rsqrt(jnp.sum(x * x, axis=-1, keepdims=True) + EPS) * (GDN_DK ** -0.5), qs)
    k = each(lambda x: x * lax.rsqrt(jnp.sum(x * x, axis=-1, keepdims=True) + EPS), ks)
    kb = each(lambda x, b: x * b, k, beta)
    eg = each(jnp.exp, gc)
    zero = jnp.zeros((CHUNK, GDN_DK), F32)
    k_bd = [stack([join_lanes([k[2 * p], zero]), join_lanes([zero, k[2 * p + 1]])]) for p in pairs]
    a = [mm_nt(join_lanes(list(ab(kb, p))), k_bd[p]) * (decay[p] * strict_w) for p in pairs]
    t = tri_inv(a) if saved_t is None else tri_inv_saved(a, saved_t)
    attn = [mm_nt(join_lanes(list(ab(q, p))), k_bd[p]) * decay[p] for p in pairs]
    rhs = [stack([join_lanes([vs[h] * beta[h], kb[h] * eg[h]]) for h in (2 * p, 2 * p + 1)]) for p in pairs]
    uw = [mm_nn(_block_diag(t[p]), rhs[p]) for p in pairs]
    uw = [x for p in pairs for x in split_rows(uw[p])]
    u, w = zip(*[split_lanes(x) for x in uw])
    ys = []
    for c in range(len(smalls)):
        hs = range(c * nh, (c + 1) * nh)
        v_new = [u[i] - mm_nn(w[i], states[i % nh]) for i in hs]
        av = [mm_nn(_block_diag(attn[c * nh // 2 + p]), stack(list(ab(v_new, p)))) for p in range(nh // 2)]
        av = [x for y in av for x in split_rows(y)]
        o = [mm_nn(q[i] * eg[i], states[i % nh]) + av[i % nh] for i in hs]
        states = [states[i % nh] * jnp.exp(gl[i]) + mm_tn(k[i] * jnp.exp(gl[i] - gc[i]), v_new[i % nh]) for i in hs]
        ys.append([rmsnorm(o[i % nh], normw) * silu(gates[i]) for i in hs])
    return ys, states, t


@jax.custom_vjp
def split_rows(x):
    n = x.shape[0] // 2
    return [x[:n], x[n:]]


split_rows.defvjp(lambda x: (split_rows(x), None), lambda _, gs: (jnp.concatenate(gs, axis=0),))


@jax.custom_vjp
def split_lanes(x):
    return [x[:, i * LANES:(i + 1) * LANES] for i in range(x.shape[1] // LANES)]


@jax.custom_vjp
def join_lanes(xs):
    return jnp.concatenate(xs, axis=1)


split_lanes.defvjp(lambda x: (split_lanes(x), None), lambda _, gs: (join_lanes(gs),))
join_lanes.defvjp(lambda xs: (join_lanes(xs), None), lambda _, g: (split_lanes(g),))


def ssd_chunk(xs, bm, cm, z, smalls, normw, alog, dtb, dvec, state):
    tri, _, _, last = _chunk_masks()
    tri_w, _, eye_w = _pair_masks()
    h0, h1 = _halves()
    hpg = SSD_HEADS // SSD_GROUPS
    ng = len(normw)
    flat = lambda xss: [x for xs_ in xss for x in xs_]
    xs, bm, cm, z = flat(xs), flat(bm), flat(cm), flat(z)
    units, pairs = range(len(xs)), range(hpg // 2)
    each = lambda f, *ls: [f(*a) for a in zip(*ls)]
    sp_lac = [_log_decay_cumsum(sm, alog, dtb, tri) for sm in smalls]
    lac_last = [jnp.sum(lac * last, axis=0, keepdims=True) for _, lac in sp_lac]
    sel = [_ind(_iota((LANES, SSD_GW), 0) == g * hpg + (_iota((LANES, SSD_GW), 1) >> 6)) for g in range(ng)]
    expand = lambda vs, mode: [mm_nn(vs[i // ng], sel[i % ng], mode) for i in units]
    dt_e = expand([sp for sp, _ in sp_lac], "bf16")
    elac_e = expand([jnp.exp(lac) for _, lac in sp_lac], "bf16")
    toend_e = expand([jnp.exp(ll - lac) for (_, lac), ll in zip(sp_lac, lac_last)], "bf16")
    row8 = _iota((8, 1), 0)
    two_e = expand([_ind(row8 == 0) * dvec + _ind(row8 == 1) * jnp.exp(ll) for ll in lac_last], "sel_b")
    d_e = each(lambda v: jnp.sum(v * _ind(row8 == 0), axis=0, keepdims=True), two_e)
    chunk_e = each(lambda v: jnp.sum(v * _ind(row8 == 1), axis=0, keepdims=True), two_e)
    xdt = each(lambda a, b: a * b, xs, dt_e)
    cb_w = each(lambda c_, b_: mm_nt(c_, jnp.concatenate([b_, b_], axis=0)), cm, bm)
    x_pairs = each(split_lanes, xdt)
    col = lambda i, j: _col_of(sp_lac[i // ng][1], (i % ng) * hpg + j)
    lms = [[_decay_pair(col(i, 2 * p), col(i, 2 * p + 1), tri_w, eye_w) for p in pairs] for i in units]
    stacked = [[jnp.concatenate([x_pairs[i][p] * h0, x_pairs[i][p] * h1], axis=0) for p in pairs] for i in units]
    terms = [[mm_nn(cb_w[i] * lms[i][p], stacked[i][p]) for p in pairs] for i in units]
    y_in = [join_lanes(terms[i]) + xs[i] * d_e[i] for i in units]
    state_in = each(lambda b_, xd, te: mm_tn(b_, xd * te), bm, xdt, toend_e)
    outs = []
    for c in range(len(smalls)):
        us = range(c * ng, (c + 1) * ng)
        y = [mm_nn(cm[i], state[i % ng]) * elac_e[i] + y_in[i] for i in us]
        state = [state[i % ng] * chunk_e[i] + state_in[i] for i in us]
        outs.append([rmsnorm(y[i % ng] * silu(z[i]), normw[i % ng]) for i in us])
    return outs, state


def _params(sem=None):
    return pltpu.CompilerParams(dimension_semantics=sem, vmem_limit_bytes=VMEM_LIMIT)


def _full(shape):
    n = len(shape)
    return pl.BlockSpec(shape, lambda *_: (0,) * n)


ANY = pl.BlockSpec(memory_space=pl.ANY)
HBM = pl.BlockSpec(memory_space=pltpu.HBM)


def in_proj(x, normw, w_main, w_small):
    t = x.shape[0]
    tm, tn = min(1024, t), 512

    def body(x_ref, nw_ref, wm_ref, ws_ref, pm_ref, ps_ref, u_ref):
        @pl.when(pl.program_id(1) == 0)
        def _():
            u = rmsnorm(x_ref[...], nw_ref[...]).astype(MXU_DTYPE)
            u_ref[...] = u
            ps_ref[...] = _raw_dot(u, ws_ref[...], 1, 0)
        pm_ref[...] = _raw_dot(u_ref[...], wm_ref[...], 1, 0)

    return pl.pallas_call(
        body, name="in_proj", grid=(t // tm, COL_CONV // tn),
        in_specs=[pl.BlockSpec((tm, D_MODEL), lambda i, j: (i, 0)), _full((1, D_MODEL)),
                  pl.BlockSpec((D_MODEL, tn), lambda i, j: (0, j)), _full((D_MODEL, LANES))],
        out_specs=[pl.BlockSpec((tm, tn), lambda i, j: (i, j)), pl.BlockSpec((tm, LANES), lambda i, j: (i, 0)),
                   pl.BlockSpec((tm, D_MODEL), lambda i, j: (i, 0))],
        out_shape=[jax.ShapeDtypeStruct((t, COL_CONV), F32), jax.ShapeDtypeStruct((t, LANES), F32),
                   jax.ShapeDtypeStruct((t, D_MODEL), MXU_DTYPE)],
        compiler_params=_params(("arbitrary", "arbitrary")),
    )(x, normw, w_main, w_small)


CONV_TC = 512
HALO = 8


def _shift_down(cur, prev, s):
    rolled = pltpu.roll(cur, s, 0)
    top = jnp.where(_iota((HALO, cur.shape[1]), 0) < s, pltpu.roll(prev, s, 0), rolled[:HALO])
    if cur.shape[0] == HALO:
        return top
    return jnp.concatenate([top, rolled[HALO:]], axis=0)


def _shift_up(cur, nxt, s):
    n = cur.shape[0]
    rolled = pltpu.roll(cur, n - s, 0)
    bot = jnp.where(_iota((HALO, cur.shape[1]), 0) >= HALO - s, pltpu.roll(nxt, HALO - s, 0), rolled[n - HALO:])
    return jnp.concatenate([rolled[:n - HALO], bot], axis=0)


def _conv_pre(cur, prev, w_ref, b, cols=slice(None)):
    acc = cur * w_ref[3:4, cols] + b
    shifted = [cur]
    for s in (1, 2, 3):
        sh = _shift_down(cur, prev, s)
        shifted.append(sh)
        acc = acc + sh * w_ref[3 - s:4 - s, cols]
    return acc, shifted


def in_proj_conv(u, w_main, w, b):
    t = u.shape[0]
    tm, tn = min(2048, t), CONV_TC
    rc = min(256, tm)
    c0, nj = COL_CONV // tn, CONV_W // tn

    def body(u_ref, wm_ref, w_ref, b_ref, out_ref, x_ref, ds_ref, halo_ref):
        j = pl.program_id(1)

        @pl.when(pl.program_id(0) == 0)
        def _():
            halo_ref[j] = jnp.zeros((HALO, tn), F32)

        prev = halo_ref[j]
        for r in range(tm // rc):
            rows = pl.ds(r * rc, rc)
            p = _raw_dot(u_ref[rows, :], wm_ref[...], 1, 0)
            x_ref[rows, :] = p.astype(x_ref.dtype)
            pre, _ = _conv_pre(p, prev, w_ref, b_ref[...])
            sg = sigmoid(pre)
            out_ref[rows, :] = pre * sg
            ds_ref[rows, :] = (sg * (1.0 + pre * (1.0 - sg))).astype(ds_ref.dtype)
            prev = p[rc - HALO:]
        halo_ref[j] = prev

    blk = pl.BlockSpec((tm, tn), lambda i, j: (i, j))
    return pl.pallas_call(
        body, name="in_proj_conv", grid=(t // tm, nj),
        in_specs=[pl.BlockSpec((tm, D_MODEL), lambda i, j: (i, 0)),
                  pl.BlockSpec((D_MODEL, tn), lambda i, j: (0, c0 + j)),
                  pl.BlockSpec((4, tn), lambda i, j: (0, j)), pl.BlockSpec((1, tn), lambda i, j: (0, j))],
        out_specs=[blk, blk, blk],
        out_shape=[jax.ShapeDtypeStruct((t, CONV_W), F32), jax.ShapeDtypeStruct((t, CONV_W), MXU_DTYPE),
                   jax.ShapeDtypeStruct((t, CONV_W), MXU_DTYPE)],
        scratch_shapes=[pltpu.VMEM((nj, HALO, tn), F32)],
        compiler_params=_params(("arbitrary", "arbitrary")),
    )(u, w_main, w, b)


def conv_bwd_w(u, x_conv, dsilu, w, dout, slabbed):
    t = u.shape[0]
    tt, tn = min(512, t), 3 * CONV_TC
    nt, nj = t // tt, CONV_W // tn
    ns = len(slabbed)
    halo_op = 2 * HALO
    after = lambda i, h: jnp.minimum((i + 1) * (tt // h), t // h - 1)

    def body(u_ref, x_ref, ds_ref, ds_nxt_ref, w_ref, do_ref, do_nxt_ref, *rest):
        slab_refs, (dx_ref, dw_ref, dwb_ref) = rest[:ns], rest[ns:ns + 3]
        land_refs, sems = rest[ns + 3:2 * ns + 3], rest[2 * ns + 3:]
        j, i = pl.program_id(0), pl.program_id(1)
        start, finish = _slab_exchange(slab_refs, land_refs, ns, *sems)

        @pl.when(jnp.logical_and(j == 0, i == 0))
        def _():
            start()

        @pl.when(i == 0)
        def _():
            dw_ref[...] = jnp.zeros(dw_ref.shape, F32)
            dwb_ref[...] = jnp.zeros(dwb_ref.shape, F32)

        uu = u_ref[...]
        row = _iota((HALO, CONV_TC), 0)
        last = i == nt - 1
        for piece in range(tn // CONV_TC):
            cols = slice(piece * CONV_TC, (piece + 1) * CONV_TC)
            x = x_ref[:, cols].astype(F32)
            dpre = do_ref[:, cols] * ds_ref[:, cols].astype(F32)
            dpre_nxt = jnp.where(last, 0.0, do_nxt_ref[:, cols] * ds_nxt_ref[:, cols].astype(F32)[:HALO])
            ups = [dpre] + [_shift_up(dpre, dpre_nxt, s) for s in (1, 2, 3)]
            dx = ups[0] * w_ref[3:4, cols]
            upd = jnp.where(row == 4, jnp.sum(dpre, axis=0, keepdims=True), 0.0)
            for s in range(4):
                if s:
                    dx = dx + ups[s] * w_ref[3 - s:4 - s, cols]
                upd = upd + jnp.where(row == 3 - s, jnp.sum(ups[s] * x, axis=0, keepdims=True), 0.0)
            dx = dx.astype(dx_ref.dtype)
            dx_ref[:, cols] = dx
            dw_ref[:, cols] += _raw_dot(uu, dx, 0, 0)
            dwb_ref[:, cols] += upd

        @pl.when(jnp.logical_and(j == nj - 1, last))
        def _():
            finish()

    out = pl.pallas_call(
        body, name="conv_bwd_w", grid=(nj, nt),
        in_specs=[pl.BlockSpec((tt, D_MODEL), lambda j, i: (i, 0)),
                  pl.BlockSpec((tt, tn), lambda j, i: (i, j)),
                  pl.BlockSpec((tt, tn), lambda j, i: (i, j)),
                  pl.BlockSpec((halo_op, tn), lambda j, i: (after(i, halo_op), j)),
                  pl.BlockSpec((4, tn), lambda j, i: (0, j)),
                  pl.BlockSpec((tt, tn), lambda j, i: (i, j)),
                  pl.BlockSpec((HALO, tn), lambda j, i: (after(i, HALO), j))] + [HBM] * ns,
        out_specs=[pl.BlockSpec((tt, tn), lambda j, i: (i, j)), pl.BlockSpec((D_MODEL, tn), lambda j, i: (0, j)),
                   pl.BlockSpec((HALO, tn), lambda j, i: (0, j))] + [HBM] * ns,
        out_shape=[jax.ShapeDtypeStruct((t, CONV_W), MXU_DTYPE), jax.ShapeDtypeStruct((D_MODEL, CONV_W), F32),
                   jax.ShapeDtypeStruct((HALO, CONV_W), F32)] + _slab_exchange_shapes(slabbed, []),
        scratch_shapes=_slab_exchange_sems(ns),
        compiler_params=_params(("arbitrary", "arbitrary")),
    )(u, x_conv, dsilu, dsilu, w, dout, dout, *slabbed)
    return out[0], out[1], out[2], out[3:]


def _ssd_cols(g):
    b0 = SSD_WIDTH + g * SSD_STATE
    c0 = SSD_WIDTH + SSD_GROUPS * SSD_STATE + g * SSD_STATE
    return slice(g * SSD_GW, (g + 1) * SSD_GW), slice(b0, b0 + SSD_STATE), slice(c0, c0 + SSD_STATE)


def _gdn_cols(j):
    return tuple(slice(s * GDN_W + j * GDN_DK, s * GDN_W + (j + 1) * GDN_DK) for s in range(3))


def _ssd_parts(xbc_ref):
    return tuple([[xbc_ref[_chunk_rows(c), _ssd_cols(g)[s]] for g in range(SSD_GROUPS)] for c in range(SSD_CB)]
                 for s in range(3))


def _group_cols(ref):
    return [[ref[_chunk_rows(c), g * SSD_GW:(g + 1) * SSD_GW] for g in range(SSD_GROUPS)] for c in range(SSD_CB)]


def _chunk_rows(c):
    return slice(c * CHUNK, (c + 1) * CHUNK)


def _gdn_parts(qkv_ref):
    assert GDN_HB == GDN_HEADS, "the conv block is read whole: one grid step holds every head"
    return tuple([[qkv_ref[_chunk_rows(c), _gdn_cols(j)[s]] for j in range(GDN_HB)] for c in range(GDN_CB)]
                 for s in range(3))


def _head_cols(ref):
    return [[ref[_chunk_rows(c), j * GDN_DV:(j + 1) * GDN_DV] for j in range(GDN_HB)] for c in range(GDN_CB)]


def _chunk_blocks(ref, n=GDN_CB):
    return [ref[_chunk_rows(c), :] for c in range(n)]


def _first_head():
    return 0 if GDN_HB == GDN_HEADS else pl.program_id(1) * GDN_HB


def ssd_fwd(conv_ssd, proj_main, proj_small, normw, alog, dtb, dvec):
    t = conv_ssd.shape[0]
    rows = CHUNK * SSD_CB
    nc = t // rows
    groups = range(SSD_GROUPS)
    norm_cols = lambda ref: [ref[:, g * SSD_GW:(g + 1) * SSD_GW] for g in groups]

    def body(xbc_ref, z_ref, sm_ref, nw_ref, al_ref, db_ref, dv_ref, y_ref, hist_ref, state_ref):
        @pl.when(pl.program_id(0) == 0)
        def _():
            state_ref[...] = jnp.zeros(state_ref.shape, F32)

        states = [state_ref[g] for g in groups]
        for g in groups:
            hist_ref[0, g] = states[g]
        ys, new_states = ssd_chunk(*_ssd_parts(xbc_ref), _group_cols(z_ref), _chunk_blocks(sm_ref, SSD_CB),
                                   norm_cols(nw_ref), al_ref[...], db_ref[...], dv_ref[...], states)
        for c in range(SSD_CB):
            for g in groups:
                y_ref[_chunk_rows(c), g * SSD_GW:(g + 1) * SSD_GW] = ys[c][g].astype(MXU_DTYPE)
        for g in groups:
            state_ref[g] = new_states[g]

    return pl.pallas_call(
        body, name="ssd_fwd", grid=(nc,),
        in_specs=[pl.BlockSpec((rows, SSD_CONV), lambda c: (c, (COL_SSD - COL_CONV) // SSD_CONV)),
                  pl.BlockSpec((rows, SSD_WIDTH), lambda c: (c, COL_Z // SSD_WIDTH)),
                  pl.BlockSpec((rows, LANES), lambda c: (c, 0)),
                  _full((1, SSD_WIDTH)), _full((1, LANES)), _full((1, LANES)), _full((1, LANES))],
        out_specs=[pl.BlockSpec((rows, SSD_WIDTH), lambda c: (c, 0)),
                   pl.BlockSpec((1, SSD_GROUPS, SSD_STATE, SSD_GW), lambda c: (c, 0, 0, 0))],
        out_shape=[jax.ShapeDtypeStruct((t, SSD_WIDTH), MXU_DTYPE),
                   jax.ShapeDtypeStruct((nc, SSD_GROUPS, SSD_STATE, SSD_GW), F32)],
        scratch_shapes=[pltpu.VMEM((SSD_GROUPS, SSD_STATE, SSD_GW), F32)],
        compiler_params=_params(("arbitrary",)),
    )(conv_ssd, proj_main, proj_small, normw, alog, dtb, dvec)


def _accumulate(ref, first, value):
    @pl.when(first)
    def _():
        ref[...] = value

    @pl.when(jnp.logical_not(first))
    def _():
        ref[...] += value


def ssd_bwd(conv_ssd, proj_main, proj_small, normw, alog, dtb, dvec, hist, dy):
    t = conv_ssd.shape[0]
    rows = CHUNK * SSD_CB
    nc = t // rows
    rev = lambda c: nc - 1 - c
    groups = range(SSD_GROUPS)
    norm_cols = lambda ref: [ref[:, g * SSD_GW:(g + 1) * SSD_GW] for g in groups]

    def body(xbc_ref, z_ref, sm_ref, nw_ref, al_ref, db_ref, dv_ref, hist_ref, dy_ref,
             dxbc_ref, dz_ref, dsm_ref, dnw_ref, dal_ref, ddb_ref, ddv_ref, dstate_ref):
        first = pl.program_id(0) == 0

        @pl.when(first)
        def _():
            dstate_ref[...] = jnp.zeros(dstate_ref.shape, F32)

        _, vjp = jax.vjp(ssd_chunk, *_ssd_parts(xbc_ref), _group_cols(z_ref), _chunk_blocks(sm_ref, SSD_CB),
                         norm_cols(nw_ref), al_ref[...], db_ref[...], dv_ref[...], [hist_ref[0, g] for g in groups])
        dxs, dbm, dcm, dz, dsm, dnw, dal, ddb, ddv, dstate = vjp(
            (_group_cols(dy_ref), [dstate_ref[g] for g in groups]))
        for k in range(SSD_CB):
            rk = _chunk_rows(k)
            for g in groups:
                xc, bc, cc = _ssd_cols(g)
                dxbc_ref[rk, xc] = dxs[k][g]
                dxbc_ref[rk, bc] = dbm[k][g]
                dxbc_ref[rk, cc] = dcm[k][g]
                dz_ref[rk, g * SSD_GW:(g + 1) * SSD_GW] = dz[k][g].astype(dz_ref.dtype)
            dsm_ref[rk, :] = dsm[k]
        for g in groups:
            dstate_ref[g] = dstate[g]
        _accumulate(dnw_ref, first, join_lanes(dnw))
        _accumulate(dal_ref, first, dal)
        _accumulate(ddb_ref, first, ddb)
        _accumulate(ddv_ref, first, ddv)

    return pl.pallas_call(
        body, name="ssd_bwd", grid=(nc,),
        in_specs=[pl.BlockSpec((rows, SSD_CONV), lambda c: (rev(c), (COL_SSD - COL_CONV) // SSD_CONV)),
                  pl.BlockSpec((rows, SSD_WIDTH), lambda c: (rev(c), COL_Z // SSD_WIDTH)),
                  pl.BlockSpec((rows, LANES), lambda c: (rev(c), 0)),
                  _full((1, SSD_WIDTH)), _full((1, LANES)), _full((1, LANES)), _full((1, LANES)),
                  pl.BlockSpec((1, SSD_GROUPS, SSD_STATE, SSD_GW), lambda c: (rev(c), 0, 0, 0)),
                  pl.BlockSpec((rows, SSD_WIDTH), lambda c: (rev(c), 0))],
        out_specs=[pl.BlockSpec((rows, SSD_CONV), lambda c: (rev(c), (COL_SSD - COL_CONV) // SSD_CONV)),
                   pl.BlockSpec((rows, SSD_WIDTH), lambda c: (rev(c), COL_Z // SSD_WIDTH)),
                   pl.BlockSpec((rows, LANES), lambda c: (rev(c), 0)),
                   _full((1, SSD_WIDTH)), _full((1, LANES)), _full((1, LANES)), _full((1, LANES))],
        out_shape=[jax.ShapeDtypeStruct((t, CONV_W), F32), jax.ShapeDtypeStruct((t, COL_CONV), MXU_DTYPE),
                   jax.ShapeDtypeStruct((t, LANES), F32), jax.ShapeDtypeStruct((1, SSD_WIDTH), F32),
                   jax.ShapeDtypeStruct((1, LANES), F32), jax.ShapeDtypeStruct((1, LANES), F32),
                   jax.ShapeDtypeStruct((1, LANES), F32)],
        scratch_shapes=[pltpu.VMEM((SSD_GROUPS, SSD_STATE, SSD_GW), F32)],
        compiler_params=_params(("arbitrary",)),
    )(conv_ssd, proj_main, proj_small, normw, alog, dtb, dvec, hist, dy)


def gdn_fwd(conv_gdn, proj_main, proj_small, normw, alog, dtb):
    t = conv_gdn.shape[0]
    hb, cb = GDN_HB, GDN_CB
    rows = CHUNK * cb
    ns = t // rows
    gate_blk = COL_GATE // (GDN_DV * hb)

    def body(qkv_ref, gate_ref, sm_ref, nw_ref, al_ref, db_ref, y_ref, hist_ref, t_ref, state_ref):
        h0 = _first_head()

        @pl.when(pl.program_id(0) == 0)
        def _():
            for j in range(hb):
                state_ref[h0 + j] = jnp.zeros((GDN_DK, GDN_DV), F32)

        states = [state_ref[h0 + j] for j in range(hb)]
        for j in range(hb):
            hist_ref[0, j] = states[j]
        qs, ks, vs = _gdn_parts(qkv_ref)
        ys, new_states, ts = gdn_chunk(h0, qs, ks, vs, _chunk_blocks(sm_ref), _head_cols(gate_ref), nw_ref[...],
                                       al_ref[...], db_ref[...], states)
        for c in range(cb):
            for j in range(hb):
                y_ref[_chunk_rows(c), j * GDN_DV:(j + 1) * GDN_DV] = ys[c][j].astype(MXU_DTYPE)
        for j in range(hb):
            state_ref[h0 + j] = new_states[j]
        for p in range(cb * hb // 2):
            t_ref[0, p] = ts[p]

    return pl.pallas_call(
        body, name="gdn_fwd", grid=(ns, GDN_HEADS // hb),
        in_specs=[pl.BlockSpec((rows, GDN_HC * hb), lambda c, h: (c, h)),
                  pl.BlockSpec((rows, GDN_DV * hb), lambda c, h: (c, gate_blk + h)),
                  pl.BlockSpec((rows, LANES), lambda c, h: (c, 0)),
                  _full((1, GDN_DV)), _full((1, LANES)), _full((1, LANES))],
        out_specs=[pl.BlockSpec((rows, GDN_DV * hb), lambda c, h: (c, h)),
                   pl.BlockSpec((1, hb, GDN_DK, GDN_DV), lambda c, h: (c, h, 0, 0)),
                   pl.BlockSpec((1, cb * hb // 2, CHUNK, LANES), lambda c, h: (c, h, 0, 0))],
        out_shape=[jax.ShapeDtypeStruct((t, GDN_W), MXU_DTYPE),
                   jax.ShapeDtypeStruct((ns, GDN_HEADS, GDN_DK, GDN_DV), F32),
                   jax.ShapeDtypeStruct((ns, cb * GDN_HEADS // 2, CHUNK, LANES), F32)],
        scratch_shapes=[pltpu.VMEM((GDN_HEADS, GDN_DK, GDN_DV), F32)],
        compiler_params=_params(("arbitrary", "arbitrary")),
    )(conv_gdn, proj_main, proj_small, normw, alog, dtb)


def gdn_bwd(dproj_main, dconv, conv_gdn, proj_main, proj_small, normw, alog, dtb, hist, t_inv, dy):
    t = conv_gdn.shape[0]
    hb, cb = GDN_HB, GDN_CB
    rows = CHUNK * cb
    ns = t // rows
    rev = lambda c: ns - 1 - c
    gate_blk = COL_GATE // (GDN_DV * hb)

    def body(alias_ref, alias2_ref, qkv_ref, gate_ref, sm_ref, nw_ref, al_ref, db_ref, hist_ref, t_ref, dy_ref,
             dgate_ref, dqkv_ref, dsm_ref, dnw_ref, dal_ref, ddb_ref, dstate_ref):
        del alias_ref, alias2_ref
        c, h = pl.program_id(0), pl.program_id(1)
        h0 = _first_head()

        @pl.when(c == 0)
        def _():
            for j in range(hb):
                dstate_ref[h0 + j] = jnp.zeros((GDN_DK, GDN_DV), F32)

        saved = [t_ref[0, p] for p in range(cb * hb // 2)]

        def fn(qs, ks, vs, smalls, gates, nw, al, db, states):
            return gdn_chunk(h0, qs, ks, vs, smalls, gates, nw, al, db, states, saved)[:2]

        qs, ks, vs = _gdn_parts(qkv_ref)
        _, vjp = jax.vjp(fn, qs, ks, vs, _chunk_blocks(sm_ref), _head_cols(gate_ref), nw_ref[...], al_ref[...],
                         db_ref[...], [hist_ref[0, j] for j in range(hb)])
        dqs, dks, dvs, dsm, dgates, dnw, dal, ddb, dstates = vjp(
            (_head_cols(dy_ref), [dstate_ref[h0 + j] for j in range(hb)]))
        for k in range(cb):
            rk = _chunk_rows(k)
            for j in range(hb):
                qc, kc, vc = _gdn_cols(j)
                dqkv_ref[rk, qc] = dqs[k][j]
                dqkv_ref[rk, kc] = dks[k][j]
                dqkv_ref[rk, vc] = dvs[k][j]
                dgate_ref[rk, j * GDN_DV:(j + 1) * GDN_DV] = dgates[k][j].astype(dgate_ref.dtype)
        for j in range(hb):
            dstate_ref[h0 + j] = dstates[j]
        _accumulate(dsm_ref, h == 0, jnp.concatenate(dsm, axis=0))
        first = jnp.logical_and(c == 0, h == 0)
        _accumulate(dnw_ref, first, dnw)
        _accumulate(dal_ref, first, dal)
        _accumulate(ddb_ref, first, ddb)

    return pl.pallas_call(
        body, name="gdn_bwd", grid=(ns, GDN_HEADS // hb),
        in_specs=[ANY, ANY, pl.BlockSpec((rows, GDN_HC * hb), lambda c, h: (rev(c), h)),
                  pl.BlockSpec((rows, GDN_DV * hb), lambda c, h: (rev(c), gate_blk + h)),
                  pl.BlockSpec((rows, LANES), lambda c, h: (rev(c), 0)),
                  _full((1, GDN_DV)), _full((1, LANES)), _full((1, LANES)),
                  pl.BlockSpec((1, hb, GDN_DK, GDN_DV), lambda c, h: (rev(c), h, 0, 0)),
                  pl.BlockSpec((1, cb * hb // 2, CHUNK, LANES), lambda c, h: (rev(c), h, 0, 0)),
                  pl.BlockSpec((rows, GDN_DV * hb), lambda c, h: (rev(c), h))],
        out_specs=[pl.BlockSpec((rows, GDN_DV * hb), lambda c, h: (rev(c), gate_blk + h)),
                   pl.BlockSpec((rows, GDN_HC * hb), lambda c, h: (rev(c), h)),
                   pl.BlockSpec((rows, LANES), lambda c, h: (rev(c), 0)),
                   _full((1, GDN_DV)), _full((1, LANES)), _full((1, LANES))],
        out_shape=[jax.ShapeDtypeStruct(dproj_main.shape, dproj_main.dtype),
                   jax.ShapeDtypeStruct(dconv.shape, dconv.dtype),
                   jax.ShapeDtypeStruct((t, LANES), F32), jax.ShapeDtypeStruct((1, GDN_DV), F32),
                   jax.ShapeDtypeStruct((1, LANES), F32), jax.ShapeDtypeStruct((1, LANES), F32)],
        scratch_shapes=[pltpu.VMEM((GDN_HEADS, GDN_DK, GDN_DV), F32)],
        input_output_aliases={0: 0, 1: 1},
        compiler_params=_params(("arbitrary", "arbitrary")),
    )(dproj_main, dconv, conv_gdn, proj_main, proj_small, normw, alog, dtb, hist, t_inv, dy)


def out_proj_loss(x, y_ssd, y_gdn, w_out, final_w, target):
    t = x.shape[0]
    tm = min(512, t)

    def body(x_ref, ys_ref, yg_ref, wo_ref, fw_ref, tg_ref, loss_ref, dhid_ref, dys_ref, dyg_ref, dwo_ref, dfw_ref):
        i = pl.program_id(0)
        ys, yg = ys_ref[...], yg_ref[...]
        wo_s, wo_g = wo_ref[:SSD_WIDTH, :], wo_ref[SSD_WIDTH:, :]
        hid = x_ref[...] + _raw_dot(ys, wo_s, 1, 0) + _raw_dot(yg, wo_g, 1, 0)
        out, vjp = jax.vjp(rmsnorm, hid, fw_ref[...])
        err = out - tg_ref[...]
        loss = 0.5 * jnp.sum(jnp.mean(err * err, axis=-1, keepdims=True), axis=0, keepdims=True)
        dhid, dfw = vjp(err * (1.0 / D_MODEL))
        dhid_ref[...] = dhid
        dys_ref[...] = _raw_dot(dhid, wo_s, 1, 1)
        dyg_ref[...] = _raw_dot(dhid, wo_g, 1, 1)
        first = i == 0
        _accumulate(loss_ref, first, jnp.broadcast_to(loss, loss_ref.shape))
        _accumulate(dfw_ref, first, dfw)

        @pl.when(first)
        def _():
            dwo_ref[:SSD_WIDTH, :] = _raw_dot(ys, dhid, 0, 0)
            dwo_ref[SSD_WIDTH:, :] = _raw_dot(yg, dhid, 0, 0)

        @pl.when(i > 0)
        def _():
            dwo_ref[:SSD_WIDTH, :] += _raw_dot(ys, dhid, 0, 0)
            dwo_ref[SSD_WIDTH:, :] += _raw_dot(yg, dhid, 0, 0)

    row = lambda w: pl.BlockSpec((tm, w), lambda i: (i, 0))
    return pl.pallas_call(
        body, name="out_proj_loss", grid=(t // tm,),
        in_specs=[row(D_MODEL), row(SSD_WIDTH), row(GDN_W), _full((SSD_WIDTH + GDN_W, D_MODEL)), _full((1, D_MODEL)),
                  row(D_MODEL)],
        out_specs=[_full((8, LANES)), row(D_MODEL), row(SSD_WIDTH), row(GDN_W), _full((SSD_WIDTH + GDN_W, D_MODEL)),
                   _full((1, D_MODEL))],
        out_shape=[jax.ShapeDtypeStruct((8, LANES), F32), jax.ShapeDtypeStruct((t, D_MODEL), F32),
                   jax.ShapeDtypeStruct((t, SSD_WIDTH), F32), jax.ShapeDtypeStruct((t, GDN_W), F32),
                   jax.ShapeDtypeStruct((SSD_WIDTH + GDN_W, D_MODEL), F32), jax.ShapeDtypeStruct((1, D_MODEL), F32)],
        compiler_params=_params(("arbitrary",)),
    )(x, y_ssd, y_gdn, w_out, final_w, target)


def in_proj_bwd_x(x, normw, w_main, w_small, dproj_main, dproj_conv, dsmall_a, dsmall_b, dhid, slabbed):
    t = x.shape[0]
    tm = min(256, t)
    ni = t // tm
    ns = len(slabbed)

    def body(x_ref, nw_ref, wm_ref, ws_ref, dp_ref, dc_ref, da_ref, db_ref, dh_ref, *rest):
        slab_refs, (gx_ref, dnw_ref), land_refs = rest[:ns], rest[ns:ns + 2], rest[ns + 2:2 * ns + 2]
        sems = rest[2 * ns + 2:]
        i = pl.program_id(0)
        start, finish = _slab_exchange(slab_refs, land_refs, ns, *sems)

        @pl.when(i == 0)
        def _():
            start()

        du = (_raw_dot(dp_ref[...], wm_ref[:, :COL_CONV], 1, 1) + _raw_dot(dc_ref[...], wm_ref[:, COL_CONV:], 1, 1)
              + _raw_dot(da_ref[...] + db_ref[...], ws_ref[...], 1, 1))
        _, vjp = jax.vjp(rmsnorm, x_ref[...], nw_ref[...])
        dx, dnw = vjp(du)
        gx_ref[...] = dx + dh_ref[...]
        _accumulate(dnw_ref, i == 0, dnw)

        @pl.when(i == ni - 1)
        def _():
            finish()

    row = lambda w: pl.BlockSpec((tm, w), lambda i: (i, 0))
    out = pl.pallas_call(
        body, name="in_proj_bwd_x", grid=(ni,),
        in_specs=[row(D_MODEL), _full((1, D_MODEL)), _full((D_MODEL, MAIN)), _full((D_MODEL, LANES)), row(COL_CONV),
                  row(CONV_W), row(LANES), row(LANES), row(D_MODEL)] + [HBM] * ns,
        out_specs=[row(D_MODEL), _full((1, D_MODEL))] + [HBM] * ns,
        out_shape=[jax.ShapeDtypeStruct((t, D_MODEL), F32), jax.ShapeDtypeStruct((1, D_MODEL), F32)]
        + _slab_exchange_shapes(slabbed, []),
        scratch_shapes=_slab_exchange_sems(ns),
        compiler_params=_params(("arbitrary",)),
    )(x, normw, w_main, w_small, dproj_main, dproj_conv, dsmall_a, dsmall_b, dhid, *slabbed)
    return out[0], out[1], out[2:]


def in_proj_bwd_w(u, dproj_main, dsmall_a, dsmall_b):
    t = u.shape[0]
    tm, tn = min(1024, t), COL_CONV // 2

    def body(u_ref, dp_ref, da_ref, db_ref, dwm_ref, dws_ref):
        j, i = pl.program_id(0), pl.program_id(1)
        uu = u_ref[...]
        _accumulate(dwm_ref, i == 0, _raw_dot(uu, dp_ref[...], 0, 0))

        @pl.when(j == 0)
        def _():
            _accumulate(dws_ref, i == 0, _raw_dot(uu, da_ref[...] + db_ref[...], 0, 0))

    return pl.pallas_call(
        body, name="in_proj_bwd_w", grid=(COL_CONV // tn, t // tm),
        in_specs=[pl.BlockSpec((tm, D_MODEL), lambda j, i: (i, 0)), pl.BlockSpec((tm, tn), lambda j, i: (i, j)),
                  pl.BlockSpec((tm, LANES), lambda j, i: (i, 0)), pl.BlockSpec((tm, LANES), lambda j, i: (i, 0))],
        out_specs=[pl.BlockSpec((D_MODEL, tn), lambda j, i: (0, j)), _full((D_MODEL, LANES))],
        out_shape=[jax.ShapeDtypeStruct((D_MODEL, COL_CONV), F32), jax.ShapeDtypeStruct((D_MODEL, LANES), F32)],
        compiler_params=_params(("arbitrary", "arbitrary")),
    )(u, dproj_main, dsmall_a, dsmall_b)


def sum_slabs(a, name):
    n, rows, cols = a.shape
    tr = 64 if rows % 64 == 0 else rows

    def body(a_ref, o_ref):
        acc = a_ref[0].astype(F32)
        for d in range(1, n):
            acc = acc + a_ref[d].astype(F32)
        o_ref[...] = acc

    return pl.pallas_call(
        body, name=name, grid=(rows // tr,),
        in_specs=[pl.BlockSpec((n, tr, cols), lambda i: (0, i, 0))],
        out_specs=pl.BlockSpec((tr, cols), lambda i: (i, 0)),
        out_shape=jax.ShapeDtypeStruct((rows, cols), F32),
        compiler_params=_params(("arbitrary",)),
    )(a)


def adamw(w, g, m, v, name):
    _, rows, cols = w.shape
    tr = 128 if rows % 128 == 0 else rows

    def body(w_ref, g_ref, m_ref, v_ref, d_ref, nm_ref, nv_ref):
        gg = g_ref[...]
        nm = ADAM_B1 * m_ref[...] + (1.0 - ADAM_B1) * gg
        nv = ADAM_B2 * v_ref[...] + (1.0 - ADAM_B2) * (gg * gg)
        m_hat = nm / (1.0 - ADAM_B1 ** ADAM_STEP)
        v_hat = nv / (1.0 - ADAM_B2 ** ADAM_STEP)
        d_ref[...] = -ADAM_LR * (m_hat / (jnp.sqrt(v_hat) + ADAM_EPS) + ADAM_WD * w_ref[...])
        nm_ref[...] = nm
        nv_ref[...] = nv

    spec = pl.BlockSpec((1, tr, cols), lambda i: (0, i, 0))
    shp = jax.ShapeDtypeStruct((1, rows, cols), F32)
    return pl.pallas_call(
        body, name=name, grid=(rows // tr,), in_specs=[spec] * 4, out_specs=[spec] * 3, out_shape=[shp] * 3,
        compiler_params=_params(("arbitrary",)),
    )(w, g.reshape(w.shape), m, v)


def _my_place():
    return lax.axis_index("x"), lax.axis_index("y"), lax.axis_index("c")


def gather_weights(big, small):
    nb, n = len(big), len(big) + len(small)
    parts = 4

    def body(*refs):
        srcs, outs = refs[:n], refs[n:2 * n]
        land_a, land_b = refs[2 * n:2 * n + nb], refs[2 * n + nb:2 * n + 2 * nb]
        send_sems, recv_sems, fwd_send, fwd_recv, local_sems = refs[2 * n + 2 * nb:]
        x, y, c = _my_place()
        me = 2 * x + y
        chips = [(1 - x, y), (x, 1 - y), (1 - x, 1 - y)]
        half = [a.shape[0] // 2 for a in big]

        def ici(j, i):
            px, py = chips[j]
            if i < nb:
                src, dst = srcs[i].at[pl.ds(c * half[i], half[i])], land_a[i].at[j]
            else:
                src, dst = srcs[i], outs[i].at[me]
            return pltpu.make_async_remote_copy(src_ref=src, dst_ref=dst, send_sem=send_sems.at[j * n + i],
                                                recv_sem=recv_sems.at[j * n + i], device_id=(px, py, c),
                                                device_id_type=MESH)

        def ici_arrival(j, i):
            px, py = chips[j]
            dst = land_a[i].at[j] if i < nb else outs[i].at[2 * px + py]
            return pltpu.make_async_remote_copy(src_ref=dst, dst_ref=dst, send_sem=send_sems.at[j * n + i],
                                                recv_sem=recv_sems.at[j * n + i], device_id=(px, py, c),
                                                device_id_type=MESH)

        def forward(j, i, p):
            rows = half[i] // parts
            k = (j * nb + i) * parts + p
            return pltpu.make_async_remote_copy(
                src_ref=land_a[i].at[j, pl.ds(p * rows, rows)], dst_ref=land_b[i].at[j, pl.ds(p * rows, rows)],
                send_sem=fwd_send.at[k], recv_sem=fwd_recv.at[k], device_id=(x, y, 1 - c), device_id_type=MESH)

        def store(j, i, from_sibling):
            px, py = chips[j]
            buf, h = (land_b, 1 - c) if from_sibling else (land_a, c)
            k = n + (j * nb + i) * 2 + (1 if from_sibling else 0)
            return pltpu.make_async_copy(buf[i].at[j], outs[i].at[2 * px + py, pl.ds(h * half[i], half[i])],
                                         local_sems.at[k])

        own = [pltpu.make_async_copy(srcs[i], outs[i].at[me], local_sems.at[i]) for i in range(n)]
        sends = [ici(j, i) for j in range(3) for i in range(n)]
        for cp in own + sends:
            cp.start()
        pending = []
        for j in range(3):
            for i in range(n):
                ici_arrival(j, i).wait_recv()
                if i < nb:
                    fw = [forward(j, i, p) for p in range(parts)]
                    st = store(j, i, False)
                    for cp in fw + [st]:
                        cp.start()
                    pending += [cp.wait_send for cp in fw] + [st.wait]
        for j in range(3):
            for i in range(nb):
                for p in range(parts):
                    forward(j, i, p).wait_recv()
                st = store(j, i, True)
                st.start()
                pending.append(st.wait)
        for cp in sends:
            cp.wait_send()
        for wait in pending:
            wait()
        for cp in own:
            cp.wait()

    shards = list(big) + list(small)
    lands = [pltpu.VMEM((3, a.shape[0] // 2) + a.shape[1:], a.dtype) for a in big]
    return pl.pallas_call(
        body, name="gather_weights",
        in_specs=[HBM] * n, out_specs=[HBM] * n,
        out_shape=[jax.ShapeDtypeStruct((N_CHIP,) + s.shape, s.dtype) for s in shards],
        scratch_shapes=lands + lands + [
            pltpu.SemaphoreType.DMA((3 * n,)), pltpu.SemaphoreType.DMA((3 * n,)),
            pltpu.SemaphoreType.DMA((3 * nb * parts,)), pltpu.SemaphoreType.DMA((3 * nb * parts,)),
            pltpu.SemaphoreType.DMA((n + 6 * nb,))],
        compiler_params=pltpu.CompilerParams(vmem_limit_bytes=VMEM_LIMIT),
    )(*shards)


def _peer(x, y, c, mask):
    mx, my, mc = (mask >> 2) & 1, (mask >> 1) & 1, mask & 1
    return (x ^ mx if mx else x, y ^ my if my else y, c ^ mc if mc else c)


def _slab_exchange_shapes(slabbed, replicated):
    return ([jax.ShapeDtypeStruct(a.shape, a.dtype) for a in slabbed]
            + [jax.ShapeDtypeStruct((N_DEV,) + a.shape, a.dtype) for a in replicated])


def _slab_exchange_sems(n):
    return [pltpu.SemaphoreType.DMA((7 * n,)), pltpu.SemaphoreType.DMA((7 * n,)), pltpu.SemaphoreType.DMA((n,))]


def _slab_exchange(srcs, outs, ns, send_sems, recv_sems, local_sems):
    n = len(srcs)
    x, y, c = _my_place()
    me = 4 * x + 2 * y + c

    def piece(i, dev):
        return srcs[i].at[dev] if i < ns else srcs[i]

    def copies(arriving):
        out = []
        for mask in range(1, N_DEV):
            px, py, pc = _peer(x, y, c, mask)
            dev = 4 * px + 2 * py + pc
            for i in range(n):
                k = (mask - 1) * n + i
                out.append(pltpu.make_async_remote_copy(
                    src_ref=piece(i, dev), dst_ref=outs[i].at[dev if arriving else me], send_sem=send_sems.at[k],
                    recv_sem=recv_sems.at[k], device_id=(px, py, pc), device_id_type=MESH))
        return out

    def local():
        return [pltpu.make_async_copy(piece(i, me), outs[i].at[me], local_sems.at[i]) for i in range(n)]

    def start():
        for cp in local() + copies(False):
            cp.start()

    def finish():
        for cp in copies(True):
            cp.wait_recv()
        for cp in copies(False):
            cp.wait_send()
        for cp in local():
            cp.wait()

    return start, finish


def exchange_halves(landed, replicated):
    n, nr = len(landed), len(replicated)
    streams = 8
    halves = [jax.ShapeDtypeStruct(a.shape[1:], F32) for a in landed]
    sum_rows = 64

    def body(*refs):
        srcs, rep_srcs, outs, rep_outs = refs[:n], refs[n:n + nr], refs[n + nr:2 * n + nr], refs[2 * n + nr:2 * (n + nr)]
        refs = refs[2 * (n + nr):]
        slabs, mine, theirs = refs[:n], refs[n:2 * n], refs[2 * n:3 * n]
        send_sems, recv_sems, in_sems, out_sems = refs[3 * n:3 * n + 4]
        rep_start, rep_finish = _slab_exchange(rep_srcs, rep_outs, 0, *refs[3 * n + 4:])
        rep_start()
        x, y, c = _my_place()
        loads = [pltpu.make_async_copy(srcs[i], slabs[i], in_sems.at[i]) for i in range(n)]
        for cp in loads:
            cp.start()
        for i in range(n):
            loads[i].wait()
            for r in range(0, halves[i].shape[0], sum_rows):
                rows = pl.ds(r, sum_rows)
                acc = slabs[i][0, rows, :].astype(F32)
                for d in range(1, N_DEV):
                    acc = acc + slabs[i][d, rows, :].astype(F32)
                mine[i][rows, :] = acc

        def chunk_copy(i, s):
            rows = halves[i].shape[0] // streams
            k = i * streams + s
            return pltpu.make_async_remote_copy(
                src_ref=mine[i].at[pl.ds(s * rows, rows)], dst_ref=theirs[i].at[pl.ds(s * rows, rows)],
                send_sem=send_sems.at[k], recv_sem=recv_sems.at[k], device_id=(x, y, 1 - c), device_id_type=MESH)

        sends = [chunk_copy(i, s) for i in range(n) for s in range(streams)]
        for cp in sends:
            cp.start()
        own = [pltpu.make_async_copy(mine[i], outs[i].at[c], out_sems.at[i]) for i in range(n)]
        for cp in own:
            cp.start()
        for cp in sends:
            cp.wait_recv()
        got = [pltpu.make_async_copy(theirs[i], outs[i].at[1 - c], out_sems.at[n + i]) for i in range(n)]
        for cp in got:
            cp.start()
        for cp in sends:
            cp.wait_send()
        for cp in own + got:
            cp.wait()
        rep_finish()

    vmem = [pltpu.VMEM(a.shape, a.dtype) for a in halves]
    out = pl.pallas_call(
        body, name="exchange_halves",
        in_specs=[HBM] * (n + nr), out_specs=[HBM] * (n + nr),
        out_shape=[jax.ShapeDtypeStruct((2,) + a.shape, a.dtype) for a in halves]
        + _slab_exchange_shapes([], replicated),
        scratch_shapes=[pltpu.VMEM(a.shape, a.dtype) for a in landed] + vmem + vmem
        + [pltpu.SemaphoreType.DMA((n * streams,)), pltpu.SemaphoreType.DMA((n * streams,)),
           pltpu.SemaphoreType.DMA((n,)), pltpu.SemaphoreType.DMA((2 * n,))] + _slab_exchange_sems(nr),
        compiler_params=pltpu.CompilerParams(vmem_limit_bytes=VMEM_LIMIT),
    )(*landed, *replicated)
    return out[:n], out[n:]


def _pack_cols(pieces):
    offs, pos = [], 0
    for a in pieces:
        offs.append(pos)
        pos += a.shape[1]
    rows8 = [jnp.pad(a.astype(F32), ((0, 8 - a.shape[0]), (0, 0))) for a in pieces]
    return jnp.concatenate(rows8, axis=1), offs


def adamw_many(ws, gs, ms, vs):
    n = len(ws)

    def body(*refs):
        w_r, g_r, m_r, v_r = refs[:n], refs[n:2 * n], refs[2 * n:3 * n], refs[3 * n:4 * n]
        d_o, m_o, v_o = refs[4 * n:5 * n], refs[5 * n:6 * n], refs[6 * n:7 * n]
        for i in range(n):
            gg = g_r[i][...]
            nm = ADAM_B1 * m_r[i][...] + (1.0 - ADAM_B1) * gg
            nv = ADAM_B2 * v_r[i][...] + (1.0 - ADAM_B2) * (gg * gg)
            m_hat = nm / (1.0 - ADAM_B1 ** ADAM_STEP)
            v_hat = nv / (1.0 - ADAM_B2 ** ADAM_STEP)
            d_o[i][...] = -ADAM_LR * (m_hat / (jnp.sqrt(v_hat) + ADAM_EPS) + ADAM_WD * w_r[i][...])
            m_o[i][...] = nm
            v_o[i][...] = nv

    shapes = [jax.ShapeDtypeStruct(w.shape, F32) for w in ws]
    out = pl.pallas_call(body, name="adamw_small", out_shape=shapes * 3,
                         compiler_params=pltpu.CompilerParams(vmem_limit_bytes=VMEM_LIMIT))(*ws, *gs, *ms, *vs)
    return out[:n], out[n:2 * n], out[2 * n:]


def _lanes(vec, start):
    n = vec.shape[-1]
    return jnp.pad(vec.reshape(1, n).astype(F32), ((0, 0), (start, LANES - start - n)))


def kernel(x, norm_w, w_in, ssd_conv_w, ssd_conv_b, ssd_dt_bias, ssd_a_log, ssd_d, ssd_norm_w, gdn_conv_w, gdn_dt_bias, gdn_a_log, gdn_norm_w, w_out, final_norm_w, loss_target, m_norm_w, m_w_in, m_ssd_conv_w, m_ssd_conv_b, m_ssd_dt_bias, m_ssd_a_log, m_ssd_d, m_ssd_norm_w, m_gdn_conv_w, m_gdn_dt_bias, m_gdn_a_log, m_gdn_norm_w, m_w_out, m_final_norm_w, v_norm_w, v_w_in, v_ssd_conv_w, v_ssd_conv_b, v_ssd_dt_bias, v_ssd_a_log, v_ssd_d, v_ssd_norm_w, v_gdn_conv_w, v_gdn_dt_bias, v_gdn_a_log, v_gdn_norm_w, v_w_out, v_final_norm_w):
    xs = x[0]
    target = loss_target[0]
    chip = 2 * lax.axis_index("x") + lax.axis_index("y")
    w_in_shard, w_out_shard = w_in[0], w_out[0]
    in_cols = w_in_shard.shape[1]
    out_rows = w_out_shard.shape[0]

    g_in, g_out, g_cs, g_cg = gather_weights(
        [w_in_shard.astype(MXU_DTYPE), w_out_shard.astype(MXU_DTYPE)], [ssd_conv_w[0], gdn_conv_w[0]])
    w_in_full = jnp.concatenate([g_in[k] for k in range(N_CHIP)], axis=1)
    w_out_full = g_out.reshape(N_CHIP * out_rows, D_MODEL)
    cw_ssd = jnp.concatenate([g_cs[k] for k in range(N_CHIP)], axis=1)
    cw_gdn = jnp.concatenate([g_cg[k] for k in range(N_CHIP)], axis=1)
    cb_ssd, cb_gdn = ssd_conv_b, jnp.zeros((1, GDN_CONV), F32)
    o_xbc, o_dt, o_gate, o_qkv, o_ab = 1024, 2560, 2576, 3600, 6672
    w_main = jnp.concatenate([w_in_full[:, :o_xbc], w_in_full[:, o_gate:o_qkv], w_in_full[:, o_qkv:o_ab],
                              w_in_full[:, o_xbc:o_dt]], axis=1)
    w_small = jnp.concatenate([w_in_full[:, o_dt:o_gate], w_in_full[:, o_ab:],
                               jnp.zeros((D_MODEL, LANES - 32), MXU_DTYPE)], axis=1)
    alog = _lanes(ssd_a_log, 0) + _lanes(gdn_a_log, LANE_GA)
    dtb = _lanes(ssd_dt_bias, 0) + _lanes(gdn_dt_bias, LANE_GA)
    dvec = _lanes(ssd_d, 0)
    fw = final_norm_w.reshape(1, D_MODEL)

    cw, cb = jnp.concatenate([cw_gdn, cw_ssd], axis=1), jnp.concatenate([cb_gdn, cb_ssd], axis=1)
    proj_main, proj_small, u = in_proj(xs, norm_w, w_main, w_small)
    conv_out, x_conv, dsilu_conv = in_proj_conv(u, w_main, cw, cb)
    y_ssd, hist_ssd = ssd_fwd(conv_out, proj_main, proj_small, ssd_norm_w, alog, dtb, dvec)
    y_gdn, hist_gdn, tinv_gdn = gdn_fwd(conv_out, proj_main, proj_small, gdn_norm_w, alog, dtb)

    loss_blk, dhid, dy_ssd, dy_gdn, d_w_out, d_fw = out_proj_loss(xs, y_ssd, y_gdn, w_out_full, fw, target)
    dconv, dproj_main, dsmall_ssd, d_ssd_nw, d_alog_s, d_dtb_s, d_dvec = ssd_bwd(
        conv_out, proj_main, proj_small, ssd_norm_w, alog, dtb, dvec, hist_ssd, dy_ssd)
    dproj_main, dconv, dsmall_gdn, d_gdn_nw, d_alog_g, d_dtb_g = gdn_bwd(
        dproj_main, dconv, conv_out, proj_main, proj_small, gdn_norm_w, alog, dtb, hist_gdn, tinv_gdn, dy_gdn)
    slabs_out = d_w_out.reshape(N_DEV, out_rows // 2, D_MODEL).astype(COMM_DTYPE)
    dproj_conv, d_w_conv, dwb, (r_out,) = conv_bwd_w(u, x_conv, dsilu_conv, cw, dconv, [slabs_out])
    dwb_gdn, dwb_ssd = dwb[:, :GDN_CONV], dwb[:, GDN_CONV:]
    d_w_zg, d_w_small = in_proj_bwd_w(u, dproj_main, dsmall_ssd, dsmall_gdn)
    order = [(d_w_zg, 0, COL_GATE), (d_w_conv, COL_SSD - COL_CONV, CONV_W), (d_w_small, 0, 16),
             (d_w_zg, COL_GATE, COL_CONV), (d_w_conv, 0, COL_SSD - COL_CONV), (d_w_small, 16, 32)]
    shards, pos = [[] for _ in range(N_CHIP)], 0
    for src, lo, hi in order:
        while lo < hi:
            k = pos // in_cols
            n = min(hi - lo, (k + 1) * in_cols - pos)
            shards[k].append(src[:, lo:lo + n].astype(COMM_DTYPE))
            lo, pos = lo + n, pos + n
    slabs_in = jnp.stack([jnp.concatenate(p, axis=1) for p in shards]).reshape(N_DEV, D_MODEL // 2, in_cols)
    grad_x, d_norm_w, (r_in,) = in_proj_bwd_x(xs, norm_w, w_main, w_small, dproj_main, dproj_conv, dsmall_ssd,
                                               dsmall_gdn, dhid, [slabs_in])
    d_alog, d_dtb = d_alog_s + d_alog_g, d_dtb_s + d_dtb_g
    packed, (o_nw, o_cs, o_cg, o_snw, o_fw, o_al, o_db, o_dv, o_gnw, o_loss) = _pack_cols([
        d_norm_w, dwb_ssd, dwb_gdn,
        d_ssd_nw.reshape(1, SSD_WIDTH), d_fw, d_alog, d_dtb, d_dvec, d_gdn_nw, loss_blk])

    (full_in, full_out), (r_small,) = exchange_halves([r_in, r_out], [packed])
    tot = sum_slabs(r_small, "sum_small")
    grad_w_in = full_in.reshape(D_MODEL, in_cols)
    grad_w_out = full_out.reshape(out_rows, D_MODEL)
    loss = tot[0, o_loss]
    sc, gc = ssd_conv_w.shape[2], gdn_conv_w.shape[2]
    row = lambda off, n, r=0: tot[r:r + 1, off:off + n]
    gs = [row(o_nw, D_MODEL),
          lax.dynamic_slice(tot, (0, o_cs + chip * sc), (4, sc)),
          row(o_cs, SSD_CONV, 4),
          row(o_db, SSD_HEADS), row(o_al, SSD_HEADS), row(o_dv, SSD_HEADS),
          row(o_snw, SSD_WIDTH),
          lax.dynamic_slice(tot, (0, o_cg + chip * gc), (4, gc)),
          row(o_db + LANE_GA, GDN_HEADS), row(o_al + LANE_GA, GDN_HEADS),
          row(o_gnw, GDN_DV), row(o_fw, D_MODEL)]

    names = ["norm_w", "ssd_conv_w", "ssd_conv_b", "ssd_dt_bias", "ssd_a_log", "ssd_d", "ssd_norm_w", "gdn_conv_w",
             "gdn_dt_bias", "gdn_a_log", "gdn_norm_w", "final_norm_w"]
    ws = [norm_w, ssd_conv_w, ssd_conv_b, ssd_dt_bias, ssd_a_log, ssd_d, ssd_norm_w, gdn_conv_w, gdn_dt_bias,
          gdn_a_log, gdn_norm_w, final_norm_w]
    ms = [m_norm_w, m_ssd_conv_w, m_ssd_conv_b, m_ssd_dt_bias, m_ssd_a_log, m_ssd_d, m_ssd_norm_w, m_gdn_conv_w,
          m_gdn_dt_bias, m_gdn_a_log, m_gdn_norm_w, m_final_norm_w]
    vs = [v_norm_w, v_ssd_conv_w, v_ssd_conv_b, v_ssd_dt_bias, v_ssd_a_log, v_ssd_d, v_ssd_norm_w, v_gdn_conv_w,
          v_gdn_dt_bias, v_gdn_a_log, v_gdn_norm_w, v_final_norm_w]
    shapes = [w.shape for w in ws]
    flat = lambda arrs: [a.reshape(g.shape) for a, g in zip(arrs, gs)]
    d_s, m_s, v_s = adamw_many(flat(ws), gs, flat(ms), flat(vs))
    back = lambda arrs: dict(zip(names, [a.reshape(s) for a, s in zip(arrs, shapes)]))
    delta, new_m, new_v, grads = back(d_s), back(m_s), back(v_s), back(gs)
    d_in, m_in, v_in = adamw(w_in, grad_w_in, m_w_in, v_w_in, "adamw_w_in")
    d_out, m_out, v_out = adamw(w_out, grad_w_out, m_w_out, v_w_out, "adamw_w_out")
    for tbl, a_in, a_out in ((grads, grad_w_in[None], grad_w_out[None]), (delta, d_in, d_out), (new_m, m_in, m_out),
                             (new_v, v_in, v_out)):
        tbl["w_in"] = a_in
        tbl["w_out"] = a_out

    order = ["norm_w", "w_in", "ssd_conv_w", "ssd_conv_b", "ssd_dt_bias", "ssd_a_log", "ssd_d", "ssd_norm_w",
             "gdn_conv_w", "gdn_dt_bias", "gdn_a_log", "gdn_norm_w", "w_out", "final_norm_w"]
    return (loss.reshape(()), grad_x[None], *[grads[k] for k in order], *[delta[k] for k in order],
            *[new_m[k] for k in order], *[new_v[k] for k in order])
```

```python
import functools

import jax
import jax.numpy as jnp
from jax import lax
from jax.experimental import pallas as pl
from jax.experimental.pallas import tpu as pltpu

F32 = jnp.float32
MXU_DTYPE = jnp.bfloat16
COMM_DTYPE = jnp.bfloat16
MESH = pl.DeviceIdType.MESH

D_MODEL = 1024
CHUNK = 64
EPS = 1e-6
SSD_HEADS, SSD_GROUPS, SSD_STATE = 16, 2, 128
SSD_WIDTH, SSD_CONV = 1024, 1536
SSD_GW = SSD_WIDTH // SSD_GROUPS
GDN_HEADS, GDN_DK, GDN_DV = 8, 128, 128
GDN_W, GDN_CONV = 1024, 3072
GDN_HC = 2 * GDN_DK + GDN_DV
IN_DIM = 6688
MAIN = 6656
LANES = 128
COL_Z, COL_GATE, COL_GDN, COL_SSD = 0, 1024, 2048, 5120
COL_CONV = COL_GDN
CONV_W = MAIN - COL_CONV
GDN_HB = 8
GDN_CB = 4
SSD_CB = 4
LANE_GA, LANE_GB = 16, 24
N_DEV, N_CHIP = 8, 4
VMEM_LIMIT = 52 * 1024 * 1024

ADAM_LR, ADAM_B1, ADAM_B2, ADAM_EPS, ADAM_WD, ADAM_STEP = 0.001, 0.9, 0.999, 1e-08, 0.01, 10


def _split(a, n):
    parts, rest = [], a.astype(F32)
    for i in range(n):
        p = rest.astype(MXU_DTYPE)
        parts.append(p)
        if i < n - 1:
            rest = rest - p.astype(F32)
    return parts


def _raw_dot(a, b, ca, cb, mode="bf16"):
    d = lambda u, v: lax.dot_general(u, v, (((ca,), (cb,)), ((), ())), preferred_element_type=F32)
    if mode == "bf16":
        return d(a.astype(MXU_DTYPE), b.astype(MXU_DTYPE))
    if mode == "x3":
        (ah, al), (bh, bl) = _split(a, 2), _split(b, 2)
        return d(ah, bh) + (d(ah, bl) + d(al, bh))
    if mode == "sel_a":
        a0 = a.astype(MXU_DTYPE)
        b1, b2, b3 = _split(b, 3)
        return d(a0, b1) + (d(a0, b2) + d(a0, b3))
    assert mode == "sel_b", mode
    b0 = b.astype(MXU_DTYPE)
    a1, a2, a3 = _split(a, 3)
    return d(a1, b0) + (d(a2, b0) + d(a3, b0))


@functools.partial(jax.custom_vjp, nondiff_argnums=(2,))
def mm_nn(a, b, mode="bf16"):
    return _raw_dot(a, b, 1, 0, mode)


@functools.partial(jax.custom_vjp, nondiff_argnums=(2,))
def mm_nt(a, b, mode="bf16"):
    return _raw_dot(a, b, 1, 1, mode)


@functools.partial(jax.custom_vjp, nondiff_argnums=(2,))
def mm_tn(a, b, mode="bf16"):
    return _raw_dot(a, b, 0, 0, mode)


_SAME = {"bf16": ("bf16", "bf16"), "x3": ("x3", "x3")}
_NN_BWD = dict(_SAME, sel_a=("bf16", "sel_a"), sel_b=("sel_b", "bf16"))
_NT_BWD = dict(_SAME, sel_a=("bf16", "sel_b"), sel_b=("sel_b", "bf16"))
_TN_BWD = dict(_SAME, sel_a=("bf16", "sel_a"), sel_b=("sel_a", "bf16"))
mm_nn.defvjp(lambda a, b, m: (_raw_dot(a, b, 1, 0, m), (a, b)),
             lambda m, r, g: (mm_nt(g, r[1], _NN_BWD[m][0]), mm_tn(r[0], g, _NN_BWD[m][1])))
mm_nt.defvjp(lambda a, b, m: (_raw_dot(a, b, 1, 1, m), (a, b)),
             lambda m, r, g: (mm_nn(g, r[1], _NT_BWD[m][0]), mm_tn(g, r[0], _NT_BWD[m][1])))
mm_tn.defvjp(lambda a, b, m: (_raw_dot(a, b, 0, 0, m), (a, b)),
             lambda m, r, g: (mm_nt(r[1], g, _TN_BWD[m][0]), mm_nn(r[0], g, _TN_BWD[m][1])))


@jax.custom_jvp
def sigmoid(x):
    return 1.0 / (1.0 + jnp.exp(-x))


@sigmoid.defjvp
def _sigmoid_jvp(p, t):
    s = sigmoid(p[0])
    return s, t[0] * s * (1.0 - s)


@jax.custom_jvp
def softplus(x):
    return jnp.maximum(x, 0.0) + jnp.log(1.0 + jnp.exp(-jnp.abs(x)))


@softplus.defjvp
def _softplus_jvp(p, t):
    return softplus(p[0]), t[0] * sigmoid(p[0])


def silu(x):
    return x * sigmoid(x)


def rmsnorm(x, w):
    return x * lax.rsqrt(jnp.mean(x * x, axis=-1, keepdims=True) + EPS) * w


def _iota(shape, dim):
    return lax.broadcasted_iota(jnp.int32, shape, dim)


def _halves():
    lane = _iota((1, LANES), 1) >> 6
    return _ind(lane == 0), _ind(lane == 1)


def _block_diag(pair):
    h0, h1 = _halves()
    return jnp.concatenate([pair * h0, pair * h1], axis=0)


def _tri_inv_impl(mats):
    r, c = _iota((CHUNK, LANES), 0), _iota((CHUNK, LANES), 1) & (CHUNK - 1)
    eye = _ind(r == c)
    blockdiag = _ind((r >> 4) == (c >> 4))
    dot = lambda u, v: _raw_dot(u, _block_diag(v), 1, 0, "x3")
    dot1 = lambda u, v: _raw_dot(u, _block_diag(v), 1, 0)
    each = lambda f, *ls: [f(*xs) for xs in zip(*ls)]
    dg = each(lambda a: a * blockdiag, mats)
    off = each(lambda a, d: a - d, mats, dg)
    m = each(lambda d: -d, dg)
    p = each(lambda x: eye + x, m)
    pw = m
    for _ in range(3):
        pw = each(lambda x: dot1(x, x), pw)
        p = each(lambda x, y: x + dot1(x, y), p, pw)
    e = each(dot, p, off)
    e2 = each(lambda x: dot1(x, x), e)
    q = each(lambda x: eye - x, e)
    q = each(lambda x, y: x + dot1(x, y), q, e2)
    return each(dot, q, p)


def _tri_inv_bwd(ts, gs):
    h0, h1 = _halves()
    x = [mm_nt(g, _block_diag(t)) for g, t in zip(gs, ts)]
    full = [mm_tn(t, y) for t, y in zip(ts, x)]
    return [-(f[:CHUNK] * h0 + f[CHUNK:] * h1) for f in full]


@jax.custom_vjp
def tri_inv(mats):
    return _tri_inv_impl(mats)


def _tri_inv_fwd(mats):
    ts = _tri_inv_impl(mats)
    return ts, ts


tri_inv.defvjp(_tri_inv_fwd, lambda ts, gs: (_tri_inv_bwd(ts, gs),))


@jax.custom_vjp
def tri_inv_saved(mats, ts):
    del mats
    return ts


tri_inv_saved.defvjp(lambda mats, ts: (ts, ts),
                     lambda ts, gs: (_tri_inv_bwd(ts, gs), [jnp.zeros_like(t) for t in ts]))


def _ind(cond):
    return jnp.where(cond, 1.0, 0.0).astype(F32)


def _chunk_masks():
    r, c = _iota((CHUNK, CHUNK), 0), _iota((CHUNK, CHUNK), 1)
    return _ind(r >= c), _ind(r > c), _ind(r == c), _ind(_iota((CHUNK, 1), 0) == CHUNK - 1)


def _log_decay_cumsum(small, alog, dtb, tri):
    sp = softplus(small + dtb)
    la = -jnp.exp(alog) * sp
    return sp, mm_nn(tri, la, "sel_a")


def _col_of(x, lane):
    return jnp.sum(x * _ind(_iota((1, LANES), 1) == lane), axis=1, keepdims=True)


def _pair_masks():
    r, c = _iota((CHUNK, LANES), 0), _iota((CHUNK, LANES), 1)
    c6 = c & (CHUNK - 1)
    return _ind(r >= c6), _ind(r > c6), (_ind(c == r), _ind(c == r + CHUNK))


def _decay_pair(col_a, col_b, tri_w, eye_w):
    h0, h1 = _halves()
    col = col_a * h0 + col_b * h1
    row = jnp.sum(col_a * eye_w[0] + col_b * eye_w[1], axis=0, keepdims=True)
    return jnp.exp((col - row) * tri_w) * tri_w


def gdn_chunk(h0, qs, ks, vs, smalls, gates, normw, alog, dtb, states, saved_t=None):
    tri, _, _, last = _chunk_masks()
    tri_w, strict_w, eye_w = _pair_masks()
    nh = len(qs[0])
    flat = lambda xss: [x for xs in xss for x in xs]
    lacs = [_log_decay_cumsum(sm, alog, dtb, tri)[1] for sm in smalls]
    qs, ks, vs, gates = flat(qs), flat(ks), flat(vs), flat(gates)
    heads, pairs = range(len(qs)), range(len(qs) // 2)
    each = lambda f, *ls: [f(*xs) for xs in zip(*ls)]
    ab = lambda xs, p: (xs[2 * p], xs[2 * p + 1])
    stack = lambda xs: jnp.concatenate(xs, axis=0)
    gc = [_col_of(lacs[i // nh], LANE_GA + h0 + i % nh) for i in heads]
    beta = [sigmoid(_col_of(smalls[i // nh], LANE_GB + h0 + i % nh)) for i in heads]
    decay = [_decay_pair(*ab(gc, p), tri_w, eye_w) for p in pairs]
    gl = each(lambda x: jnp.sum(x * last, axis=0, keepdims=True), gc)
    q = each(lambda x: x * lax.rsqrt(jnp.sum(x * x, axis=-1, keepdims=True) + EPS) * (GDN_DK ** -0.5), qs)
    k = each(lambda x: x * lax.rsqrt(jnp.sum(x * x, axis=-1, keepdims=True) + EPS), ks)
    kb = each(lambda x, b: x * b, k, beta)
    eg = each(jnp.exp, gc)
    zero = jnp.zeros((CHUNK, GDN_DK), F32)
    k_bd = [stack([join_lanes([k[2 * p], zero]), join_lanes([zero, k[2 * p + 1]])]) for p in pairs]
    a = [mm_nt(join_lanes(list(ab(kb, p))), k_bd[p]) * (decay[p] * strict_w) for p in pairs]
    t = tri_inv(a) if saved_t is None else tri_inv_saved(a, saved_t)
    attn = [mm_nt(join_lanes(list(ab(q, p))), k_bd[p]) * decay[p] for p in pairs]
    rhs = [stack([join_lanes([vs[h] * beta[h], kb[h] * eg[h]]) for h in (2 * p, 2 * p + 1)]) for p in pairs]
    uw = [mm_nn(_block_diag(t[p]), rhs[p]) for p in pairs]
    uw = [x for p in pairs for x in split_rows(uw[p])]
    u, w = zip(*[split_lanes(x) for x in uw])
    ys = []
    for c in range(len(smalls)):
        hs = range(c * nh, (c + 1) * nh)
        v_new = [u[i] - mm_nn(w[i], states[i % nh]) for i in hs]
        av = [mm_nn(_block_diag(attn[c * nh // 2 + p]), stack(list(ab(v_new, p)))) for p in range(nh // 2)]
        av = [x for y in av for x in split_rows(y)]
        o = [mm_nn(q[i] * eg[i], states[i % nh]) + av[i % nh] for i in hs]
        states = [states[i % nh] * jnp.exp(gl[i]) + mm_tn(k[i] * jnp.exp(gl[i] - gc[i]), v_new[i % nh]) for i in hs]
        ys.append([rmsnorm(o[i % nh], normw) * silu(gates[i]) for i in hs])
    return ys, states, t


@jax.custom_vjp
def split_rows(x):
    n = x.shape[0] // 2
    return [x[:n], x[n:]]


split_rows.defvjp(lambda x: (split_rows(x), None), lambda _, gs: (jnp.concatenate(gs, axis=0),))


@jax.custom_vjp
def split_lanes(x):
    return [x[:, i * LANES:(i + 1) * LANES] for i in range(x.shape[1] // LANES)]


@jax.custom_vjp
def join_lanes(xs):
    return jnp.concatenate(xs, axis=1)


split_lanes.defvjp(lambda x: (split_lanes(x), None), lambda _, gs: (join_lanes(gs),))
join_lanes.defvjp(lambda xs: (join_lanes(xs), None), lambda _, g: (split_lanes(g),))


def ssd_chunk(xs, bm, cm, z, smalls, normw, alog, dtb, dvec, state):
    tri, _, _, last = _chunk_masks()
    tri_w, _, eye_w = _pair_masks()
    h0, h1 = _halves()
    hpg = SSD_HEADS // SSD_GROUPS
    ng = len(normw)
    flat = lambda xss: [x for xs_ in xss for x in xs_]
    xs, bm, cm, z = flat(xs), flat(bm), flat(cm), flat(z)
    units, pairs = range(len(xs)), range(hpg // 2)
    each = lambda f, *ls: [f(*a) for a in zip(*ls)]
    sp_lac = [_log_decay_cumsum(sm, alog, dtb, tri) for sm in smalls]
    lac_last = [jnp.sum(lac * last, axis=0, keepdims=True) for _, lac in sp_lac]
    sel = [_ind(_iota((LANES, SSD_GW), 0) == g * hpg + (_iota((LANES, SSD_GW), 1) >> 6)) for g in range(ng)]
    expand = lambda vs, mode: [mm_nn(vs[i // ng], sel[i % ng], mode) for i in units]
    dt_e = expand([sp for sp, _ in sp_lac], "bf16")
    elac_e = expand([jnp.exp(lac) for _, lac in sp_lac], "bf16")
    toend_e = expand([jnp.exp(ll - lac) for (_, lac), ll in zip(sp_lac, lac_last)], "bf16")
    row8 = _iota((8, 1), 0)
    two_e = expand([_ind(row8 == 0) * dvec + _ind(row8 == 1) * jnp.exp(ll) for ll in lac_last], "sel_b")
    d_e = each(lambda v: jnp.sum(v * _ind(row8 == 0), axis=0, keepdims=True), two_e)
    chunk_e = each(lambda v: jnp.sum(v * _ind(row8 == 1), axis=0, keepdims=True), two_e)
    xdt = each(lambda a, b: a * b, xs, dt_e)
    cb_w = each(lambda c_, b_: mm_nt(c_, jnp.concatenate([b_, b_], axis=0)), cm, bm)
    x_pairs = each(split_lanes, xdt)
    col = lambda i, j: _col_of(sp_lac[i // ng][1], (i % ng) * hpg + j)
    lms = [[_decay_pair(col(i, 2 * p), col(i, 2 * p + 1), tri_w, eye_w) for p in pairs] for i in units]
    stacked = [[jnp.concatenate([x_pairs[i][p] * h0, x_pairs[i][p] * h1], axis=0) for p in pairs] for i in units]
    terms = [[mm_nn(cb_w[i] * lms[i][p], stacked[i][p]) for p in pairs] for i in units]
    y_in = [join_lanes(terms[i]) + xs[i] * d_e[i] for i in units]
    state_in = each(lambda b_, xd, te: mm_tn(b_, xd * te), bm, xdt, toend_e)
    outs = []
    for c in range(len(smalls)):
        us = range(c * ng, (c + 1) * ng)
        y = [mm_nn(cm[i], state[i % ng]) * elac_e[i] + y_in[i] for i in us]
        state = [state[i % ng] * chunk_e[i] + state_in[i] for i in us]
        outs.append([rmsnorm(y[i % ng] * silu(z[i]), normw[i % ng]) for i in us])
    return outs, state


def _params(sem=None):
    return pltpu.CompilerParams(dimension_semantics=sem, vmem_limit_bytes=VMEM_LIMIT)


def _full(shape):
    n = len(shape)
    return pl.BlockSpec(shape, lambda *_: (0,) * n)


ANY = pl.BlockSpec(memory_space=pl.ANY)
HBM = pl.BlockSpec(memory_space=pltpu.HBM)


def in_proj(x, normw, w_main, w_small):
    t = x.shape[0]
    tm, tn = min(1024, t), 512

    def body(x_ref, nw_ref, wm_ref, ws_ref, pm_ref, ps_ref, u_ref):
        @pl.when(pl.program_id(1) == 0)
        def _():
            u = rmsnorm(x_ref[...], nw_ref[...]).astype(MXU_DTYPE)
            u_ref[...] = u
            ps_ref[...] = _raw_dot(u, ws_ref[...], 1, 0)
        pm_ref[...] = _raw_dot(u_ref[...], wm_ref[...], 1, 0)

    return pl.pallas_call(
        body, name="in_proj", grid=(t // tm, COL_CONV // tn),
        in_specs=[pl.BlockSpec((tm, D_MODEL), lambda i, j: (i, 0)), _full((1, D_MODEL)),
                  pl.BlockSpec((D_MODEL, tn), lambda i, j: (0, j)), _full((D_MODEL, LANES))],
        out_specs=[pl.BlockSpec((tm, tn), lambda i, j: (i, j)), pl.BlockSpec((tm, LANES), lambda i, j: (i, 0)),
                   pl.BlockSpec((tm, D_MODEL), lambda i, j: (i, 0))],
        out_shape=[jax.ShapeDtypeStruct((t, COL_CONV), F32), jax.ShapeDtypeStruct((t, LANES), F32),
                   jax.ShapeDtypeStruct((t, D_MODEL), MXU_DTYPE)],
        compiler_params=_params(("arbitrary", "arbitrary")),
    )(x, normw, w_main, w_small)


CONV_TC = 512
HALO = 8


def _shift_down(cur, prev, s):
    rolled = pltpu.roll(cur, s, 0)
    top = jnp.where(_iota((HALO, cur.shape[1]), 0) < s, pltpu.roll(prev, s, 0), rolled[:HALO])
    if cur.shape[0] == HALO:
        return top
    return jnp.concatenate([top, rolled[HALO:]], axis=0)


def _shift_up(cur, nxt, s):
    n = cur.shape[0]
    rolled = pltpu.roll(cur, n - s, 0)
    bot = jnp.where(_iota((HALO, cur.shape[1]), 0) >= HALO - s, pltpu.roll(nxt, HALO - s, 0), rolled[n - HALO:])
    return jnp.concatenate([rolled[:n - HALO], bot], axis=0)


def _conv_pre(cur, prev, w_ref, b, cols=slice(None)):
    acc = cur * w_ref[3:4, cols] + b
    shifted = [cur]
    for s in (1, 2, 3):
        sh = _shift_down(cur, prev, s)
        shifted.append(sh)
        acc = acc + sh * w_ref[3 - s:4 - s, cols]
    return acc, shifted


def in_proj_conv(u, w_main, w, b):
    t = u.shape[0]
    tm, tn = min(2048, t), CONV_TC
    rc = min(256, tm)
    c0, nj = COL_CONV // tn, CONV_W // tn

    def body(u_ref, wm_ref, w_ref, b_ref, out_ref, x_ref, ds_ref, halo_ref):
        j = pl.program_id(1)

        @pl.when(pl.program_id(0) == 0)
        def _():
            halo_ref[j] = jnp.zeros((HALO, tn), F32)

        prev = halo_ref[j]
        for r in range(tm // rc):
            rows = pl.ds(r * rc, rc)
            p = _raw_dot(u_ref[rows, :], wm_ref[...], 1, 0)
            x_ref[rows, :] = p.astype(x_ref.dtype)
            pre, _ = _conv_pre(p, prev, w_ref, b_ref[...])
            sg = sigmoid(pre)
            out_ref[rows, :] = pre * sg
            ds_ref[rows, :] = (sg * (1.0 + pre * (1.0 - sg))).astype(ds_ref.dtype)
            prev = p[rc - HALO:]
        halo_ref[j] = prev

    blk = pl.BlockSpec((tm, tn), lambda i, j: (i, j))
    return pl.pallas_call(
        body, name="in_proj_conv", grid=(t // tm, nj),
        in_specs=[pl.BlockSpec((tm, D_MODEL), lambda i, j: (i, 0)),
                  pl.BlockSpec((D_MODEL, tn), lambda i, j: (0, c0 + j)),
                  pl.BlockSpec((4, tn), lambda i, j: (0, j)), pl.BlockSpec((1, tn), lambda i, j: (0, j))],
        out_specs=[blk, blk, blk],
        out_shape=[jax.ShapeDtypeStruct((t, CONV_W), F32), jax.ShapeDtypeStruct((t, CONV_W), MXU_DTYPE),
                   jax.ShapeDtypeStruct((t, CONV_W), MXU_DTYPE)],
        scratch_shapes=[pltpu.VMEM((nj, HALO, tn), F32)],
        compiler_params=_params(("arbitrary", "arbitrary")),
    )(u, w_main, w, b)


def conv_bwd_w(u, x_conv, dsilu, w, dout, slabbed):
    t = u.shape[0]
    tt, tn = min(512, t), 3 * CONV_TC
    nt, nj = t // tt, CONV_W // tn
    ns = len(slabbed)
    halo_op = 2 * HALO
    after = lambda i, h: jnp.minimum((i + 1) * (tt // h), t // h - 1)

    def body(u_ref, x_ref, ds_ref, ds_nxt_ref, w_ref, do_ref, do_nxt_ref, *rest):
        slab_refs, (dx_ref, dw_ref, dwb_ref) = rest[:ns], rest[ns:ns + 3]
        land_refs, sems = rest[ns + 3:2 * ns + 3], rest[2 * ns + 3:]
        j, i = pl.program_id(0), pl.program_id(1)
        start, finish = _slab_exchange(slab_refs, land_refs, ns, *sems)

        @pl.when(jnp.logical_and(j == 0, i == 0))
        def _():
            start()

        @pl.when(i == 0)
        def _():
            dw_ref[...] = jnp.zeros(dw_ref.shape, F32)
            dwb_ref[...] = jnp.zeros(dwb_ref.shape, F32)

        uu = u_ref[...]
        row = _iota((HALO, CONV_TC), 0)
        last = i == nt - 1
        for piece in range(tn // CONV_TC):
            cols = slice(piece * CONV_TC, (piece + 1) * CONV_TC)
            x = x_ref[:, cols].astype(F32)
            dpre = do_ref[:, cols].astype(F32) * ds_ref[:, cols].astype(F32)
            dpre_nxt = do_nxt_ref[:, cols].astype(F32)[:HALO] * ds_nxt_ref[:, cols].astype(F32)[:HALO]
            dpre_nxt = jnp.where(last, 0.0, dpre_nxt)
            ups = [dpre] + [_shift_up(dpre, dpre_nxt, s) for s in (1, 2, 3)]
            dx = ups[0] * w_ref[3:4, cols]
            upd = jnp.where(row == 4, jnp.sum(dpre, axis=0, keepdims=True), 0.0)
            for s in range(4):
                if s:
                    dx = dx + ups[s] * w_ref[3 - s:4 - s, cols]
                upd = upd + jnp.where(row == 3 - s, jnp.sum(ups[s] * x, axis=0, keepdims=True), 0.0)
            dx = dx.astype(dx_ref.dtype)
            dx_ref[:, cols] = dx
            dw_ref[:, cols] += _raw_dot(uu, dx, 0, 0)
            dwb_ref[:, cols] += upd

        @pl.when(jnp.logical_and(j == nj - 1, last))
        def _():
            finish()

    out = pl.pallas_call(
        body, name="conv_bwd_w", grid=(nj, nt),
        in_specs=[pl.BlockSpec((tt, D_MODEL), lambda j, i: (i, 0)),
                  pl.BlockSpec((tt, tn), lambda j, i: (i, j)),
                  pl.BlockSpec((tt, tn), lambda j, i: (i, j)),
                  pl.BlockSpec((halo_op, tn), lambda j, i: (after(i, halo_op), j)),
                  pl.BlockSpec((4, tn), lambda j, i: (0, j)),
                  pl.BlockSpec((tt, tn), lambda j, i: (i, j)),
                  pl.BlockSpec((halo_op, tn), lambda j, i: (after(i, halo_op), j))] + [HBM] * ns,
        out_specs=[pl.BlockSpec((tt, tn), lambda j, i: (i, j)), pl.BlockSpec((D_MODEL, tn), lambda j, i: (0, j)),
                   pl.BlockSpec((HALO, tn), lambda j, i: (0, j))] + [HBM] * ns,
        out_shape=[jax.ShapeDtypeStruct((t, CONV_W), MXU_DTYPE), jax.ShapeDtypeStruct((D_MODEL, CONV_W), F32),
                   jax.ShapeDtypeStruct((HALO, CONV_W), F32)] + _slab_exchange_shapes(slabbed, []),
        scratch_shapes=_slab_exchange_sems(ns),
        compiler_params=_params(("arbitrary", "arbitrary")),
    )(u, x_conv, dsilu, dsilu, w, dout, dout, *slabbed)
    return out[0], out[1], out[2], out[3:]


def _ssd_cols(g):
    b0 = SSD_WIDTH + g * SSD_STATE
    c0 = SSD_WIDTH + SSD_GROUPS * SSD_STATE + g * SSD_STATE
    return slice(g * SSD_GW, (g + 1) * SSD_GW), slice(b0, b0 + SSD_STATE), slice(c0, c0 + SSD_STATE)


def _gdn_cols(j):
    return tuple(slice(s * GDN_W + j * GDN_DK, s * GDN_W + (j + 1) * GDN_DK) for s in range(3))


def _ssd_parts(xbc_ref):
    return tuple([[xbc_ref[_chunk_rows(c), _ssd_cols(g)[s]] for g in range(SSD_GROUPS)] for c in range(SSD_CB)]
                 for s in range(3))


def _group_cols(ref):
    return [[ref[_chunk_rows(c), g * SSD_GW:(g + 1) * SSD_GW] for g in range(SSD_GROUPS)] for c in range(SSD_CB)]


def _chunk_rows(c):
    return slice(c * CHUNK, (c + 1) * CHUNK)


def _gdn_parts(qkv_ref):
    assert GDN_HB == GDN_HEADS, "the conv block is read whole: one grid step holds every head"
    return tuple([[qkv_ref[_chunk_rows(c), _gdn_cols(j)[s]] for j in range(GDN_HB)] for c in range(GDN_CB)]
                 for s in range(3))


def _head_cols(ref):
    return [[ref[_chunk_rows(c), j * GDN_DV:(j + 1) * GDN_DV] for j in range(GDN_HB)] for c in range(GDN_CB)]


def _chunk_blocks(ref, n=GDN_CB):
    return [ref[_chunk_rows(c), :] for c in range(n)]


def _first_head():
    return 0 if GDN_HB == GDN_HEADS else pl.program_id(1) * GDN_HB


def ssd_fwd(conv_ssd, proj_main, proj_small, normw, alog, dtb, dvec):
    t = conv_ssd.shape[0]
    rows = CHUNK * SSD_CB
    nc = t // rows
    groups = range(SSD_GROUPS)
    norm_cols = lambda ref: [ref[:, g * SSD_GW:(g + 1) * SSD_GW] for g in groups]

    def body(xbc_ref, z_ref, sm_ref, nw_ref, al_ref, db_ref, dv_ref, y_ref, hist_ref, state_ref):
        @pl.when(pl.program_id(0) == 0)
        def _():
            state_ref[...] = jnp.zeros(state_ref.shape, F32)

        states = [state_ref[g] for g in groups]
        for g in groups:
            hist_ref[0, g] = states[g]
        ys, new_states = ssd_chunk(*_ssd_parts(xbc_ref), _group_cols(z_ref), _chunk_blocks(sm_ref, SSD_CB),
                                   norm_cols(nw_ref), al_ref[...], db_ref[...], dv_ref[...], states)
        for c in range(SSD_CB):
            for g in groups:
                y_ref[_chunk_rows(c), g * SSD_GW:(g + 1) * SSD_GW] = ys[c][g].astype(MXU_DTYPE)
        for g in groups:
            state_ref[g] = new_states[g]

    return pl.pallas_call(
        body, name="ssd_fwd", grid=(nc,),
        in_specs=[pl.BlockSpec((rows, SSD_CONV), lambda c: (c, (COL_SSD - COL_CONV) // SSD_CONV)),
                  pl.BlockSpec((rows, SSD_WIDTH), lambda c: (c, COL_Z // SSD_WIDTH)),
                  pl.BlockSpec((rows, LANES), lambda c: (c, 0)),
                  _full((1, SSD_WIDTH)), _full((1, LANES)), _full((1, LANES)), _full((1, LANES))],
        out_specs=[pl.BlockSpec((rows, SSD_WIDTH), lambda c: (c, 0)),
                   pl.BlockSpec((1, SSD_GROUPS, SSD_STATE, SSD_GW), lambda c: (c, 0, 0, 0))],
        out_shape=[jax.ShapeDtypeStruct((t, SSD_WIDTH), MXU_DTYPE),
                   jax.ShapeDtypeStruct((nc, SSD_GROUPS, SSD_STATE, SSD_GW), F32)],
        scratch_shapes=[pltpu.VMEM((SSD_GROUPS, SSD_STATE, SSD_GW), F32)],
        compiler_params=_params(("arbitrary",)),
    )(conv_ssd, proj_main, proj_small, normw, alog, dtb, dvec)


def _accumulate(ref, first, value):
    @pl.when(first)
    def _():
        ref[...] = value

    @pl.when(jnp.logical_not(first))
    def _():
        ref[...] += value


def ssd_bwd(conv_ssd, proj_main, proj_small, normw, alog, dtb, dvec, hist, dy):
    t = conv_ssd.shape[0]
    rows = CHUNK * SSD_CB
    nc = t // rows
    rev = lambda c: nc - 1 - c
    groups = range(SSD_GROUPS)
    norm_cols = lambda ref: [ref[:, g * SSD_GW:(g + 1) * SSD_GW] for g in groups]

    def body(xbc_ref, z_ref, sm_ref, nw_ref, al_ref, db_ref, dv_ref, hist_ref, dy_ref,
             dxbc_ref, dz_ref, dsm_ref, dnw_ref, dal_ref, ddb_ref, ddv_ref, dstate_ref):
        first = pl.program_id(0) == 0

        @pl.when(first)
        def _():
            dstate_ref[...] = jnp.zeros(dstate_ref.shape, F32)

        _, vjp = jax.vjp(ssd_chunk, *_ssd_parts(xbc_ref), _group_cols(z_ref), _chunk_blocks(sm_ref, SSD_CB),
                         norm_cols(nw_ref), al_ref[...], db_ref[...], dv_ref[...], [hist_ref[0, g] for g in groups])
        dxs, dbm, dcm, dz, dsm, dnw, dal, ddb, ddv, dstate = vjp(
            (_group_cols(dy_ref), [dstate_ref[g] for g in groups]))
        for k in range(SSD_CB):
            rk = _chunk_rows(k)
            for g in groups:
                xc, bc, cc = _ssd_cols(g)
                dxbc_ref[rk, xc] = dxs[k][g].astype(dxbc_ref.dtype)
                dxbc_ref[rk, bc] = dbm[k][g].astype(dxbc_ref.dtype)
                dxbc_ref[rk, cc] = dcm[k][g].astype(dxbc_ref.dtype)
                dz_ref[rk, g * SSD_GW:(g + 1) * SSD_GW] = dz[k][g].astype(dz_ref.dtype)
            dsm_ref[rk, :] = dsm[k]
        for g in groups:
            dstate_ref[g] = dstate[g]
        _accumulate(dnw_ref, first, join_lanes(dnw))
        _accumulate(dal_ref, first, dal)
        _accumulate(ddb_ref, first, ddb)
        _accumulate(ddv_ref, first, ddv)

    return pl.pallas_call(
        body, name="ssd_bwd", grid=(nc,),
        in_specs=[pl.BlockSpec((rows, SSD_CONV), lambda c: (rev(c), (COL_SSD - COL_CONV) // SSD_CONV)),
                  pl.BlockSpec((rows, SSD_WIDTH), lambda c: (rev(c), COL_Z // SSD_WIDTH)),
                  pl.BlockSpec((rows, LANES), lambda c: (rev(c), 0)),
                  _full((1, SSD_WIDTH)), _full((1, LANES)), _full((1, LANES)), _full((1, LANES)),
                  pl.BlockSpec((1, SSD_GROUPS, SSD_STATE, SSD_GW), lambda c: (rev(c), 0, 0, 0)),
                  pl.BlockSpec((rows, SSD_WIDTH), lambda c: (rev(c), 0))],
        out_specs=[pl.BlockSpec((rows, SSD_CONV), lambda c: (rev(c), (COL_SSD - COL_CONV) // SSD_CONV)),
                   pl.BlockSpec((rows, SSD_WIDTH), lambda c: (rev(c), COL_Z // SSD_WIDTH)),
                   pl.BlockSpec((rows, LANES), lambda c: (rev(c), 0)),
                   _full((1, SSD_WIDTH)), _full((1, LANES)), _full((1, LANES)), _full((1, LANES))],
        out_shape=[jax.ShapeDtypeStruct((t, CONV_W), MXU_DTYPE), jax.ShapeDtypeStruct((t, COL_CONV), MXU_DTYPE),
                   jax.ShapeDtypeStruct((t, LANES), F32), jax.ShapeDtypeStruct((1, SSD_WIDTH), F32),
                   jax.ShapeDtypeStruct((1, LANES), F32), jax.ShapeDtypeStruct((1, LANES), F32),
                   jax.ShapeDtypeStruct((1, LANES), F32)],
        scratch_shapes=[pltpu.VMEM((SSD_GROUPS, SSD_STATE, SSD_GW), F32)],
        compiler_params=_params(("arbitrary",)),
    )(conv_ssd, proj_main, proj_small, normw, alog, dtb, dvec, hist, dy)


def gdn_fwd(conv_gdn, proj_main, proj_small, normw, alog, dtb):
    t = conv_gdn.shape[0]
    hb, cb = GDN_HB, GDN_CB
    rows = CHUNK * cb
    ns = t // rows
    gate_blk = COL_GATE // (GDN_DV * hb)

    def body(qkv_ref, gate_ref, sm_ref, nw_ref, al_ref, db_ref, y_ref, hist_ref, t_ref, state_ref):
        h0 = _first_head()

        @pl.when(pl.program_id(0) == 0)
        def _():
            for j in range(hb):
                state_ref[h0 + j] = jnp.zeros((GDN_DK, GDN_DV), F32)

        states = [state_ref[h0 + j] for j in range(hb)]
        for j in range(hb):
            hist_ref[0, j] = states[j]
        qs, ks, vs = _gdn_parts(qkv_ref)
        ys, new_states, ts = gdn_chunk(h0, qs, ks, vs, _chunk_blocks(sm_ref), _head_cols(gate_ref), nw_ref[...],
                                       al_ref[...], db_ref[...], states)
        for c in range(cb):
            for j in range(hb):
                y_ref[_chunk_rows(c), j * GDN_DV:(j + 1) * GDN_DV] = ys[c][j].astype(MXU_DTYPE)
        for j in range(hb):
            state_ref[h0 + j] = new_states[j]
        for p in range(cb * hb // 2):
            t_ref[0, p] = ts[p]

    return pl.pallas_call(
        body, name="gdn_fwd", grid=(ns, GDN_HEADS // hb),
        in_specs=[pl.BlockSpec((rows, GDN_HC * hb), lambda c, h: (c, h)),
                  pl.BlockSpec((rows, GDN_DV * hb), lambda c, h: (c, gate_blk + h)),
                  pl.BlockSpec((rows, LANES), lambda c, h: (c, 0)),
                  _full((1, GDN_DV)), _full((1, LANES)), _full((1, LANES))],
        out_specs=[pl.BlockSpec((rows, GDN_DV * hb), lambda c, h: (c, h)),
                   pl.BlockSpec((1, hb, GDN_DK, GDN_DV), lambda c, h: (c, h, 0, 0)),
                   pl.BlockSpec((1, cb * hb // 2, CHUNK, LANES), lambda c, h: (c, h, 0, 0))],
        out_shape=[jax.ShapeDtypeStruct((t, GDN_W), MXU_DTYPE),
                   jax.ShapeDtypeStruct((ns, GDN_HEADS, GDN_DK, GDN_DV), F32),
                   jax.ShapeDtypeStruct((ns, cb * GDN_HEADS // 2, CHUNK, LANES), F32)],
        scratch_shapes=[pltpu.VMEM((GDN_HEADS, GDN_DK, GDN_DV), F32)],
        compiler_params=_params(("arbitrary", "arbitrary")),
    )(conv_gdn, proj_main, proj_small, normw, alog, dtb)


def gdn_bwd(dproj_main, dconv, conv_gdn, proj_main, proj_small, normw, alog, dtb, hist, t_inv, dy):
    t = conv_gdn.shape[0]
    hb, cb = GDN_HB, GDN_CB
    rows = CHUNK * cb
    ns = t // rows
    rev = lambda c: ns - 1 - c
    gate_blk = COL_GATE // (GDN_DV * hb)

    def body(alias_ref, alias2_ref, qkv_ref, gate_ref, sm_ref, nw_ref, al_ref, db_ref, hist_ref, t_ref, dy_ref,
             dgate_ref, dqkv_ref, dsm_ref, dnw_ref, dal_ref, ddb_ref, dstate_ref):
        del alias_ref, alias2_ref
        c, h = pl.program_id(0), pl.program_id(1)
        h0 = _first_head()

        @pl.when(c == 0)
        def _():
            for j in range(hb):
                dstate_ref[h0 + j] = jnp.zeros((GDN_DK, GDN_DV), F32)

        saved = [t_ref[0, p] for p in range(cb * hb // 2)]

        def fn(qs, ks, vs, smalls, gates, nw, al, db, states):
            return gdn_chunk(h0, qs, ks, vs, smalls, gates, nw, al, db, states, saved)[:2]

        qs, ks, vs = _gdn_parts(qkv_ref)
        _, vjp = jax.vjp(fn, qs, ks, vs, _chunk_blocks(sm_ref), _head_cols(gate_ref), nw_ref[...], al_ref[...],
                         db_ref[...], [hist_ref[0, j] for j in range(hb)])
        dqs, dks, dvs, dsm, dgates, dnw, dal, ddb, dstates = vjp(
            (_head_cols(dy_ref), [dstate_ref[h0 + j] for j in range(hb)]))
        for k in range(cb):
            rk = _chunk_rows(k)
            for j in range(hb):
                qc, kc, vc = _gdn_cols(j)
                dqkv_ref[rk, qc] = dqs[k][j].astype(dqkv_ref.dtype)
                dqkv_ref[rk, kc] = dks[k][j].astype(dqkv_ref.dtype)
                dqkv_ref[rk, vc] = dvs[k][j].astype(dqkv_ref.dtype)
                dgate_ref[rk, j * GDN_DV:(j + 1) * GDN_DV] = dgates[k][j].astype(dgate_ref.dtype)
        for j in range(hb):
            dstate_ref[h0 + j] = dstates[j]
        _accumulate(dsm_ref, h == 0, jnp.concatenate(dsm, axis=0))
        first = jnp.logical_and(c == 0, h == 0)
        _accumulate(dnw_ref, first, dnw)
        _accumulate(dal_ref, first, dal)
        _accumulate(ddb_ref, first, ddb)

    return pl.pallas_call(
        body, name="gdn_bwd", grid=(ns, GDN_HEADS // hb),
        in_specs=[ANY, ANY, pl.BlockSpec((rows, GDN_HC * hb), lambda c, h: (rev(c), h)),
                  pl.BlockSpec((rows, GDN_DV * hb), lambda c, h: (rev(c), gate_blk + h)),
                  pl.BlockSpec((rows, LANES), lambda c, h: (rev(c), 0)),
                  _full((1, GDN_DV)), _full((1, LANES)), _full((1, LANES)),
                  pl.BlockSpec((1, hb, GDN_DK, GDN_DV), lambda c, h: (rev(c), h, 0, 0)),
                  pl.BlockSpec((1, cb * hb // 2, CHUNK, LANES), lambda c, h: (rev(c), h, 0, 0)),
                  pl.BlockSpec((rows, GDN_DV * hb), lambda c, h: (rev(c), h))],
        out_specs=[pl.BlockSpec((rows, GDN_DV * hb), lambda c, h: (rev(c), gate_blk + h)),
                   pl.BlockSpec((rows, GDN_HC * hb), lambda c, h: (rev(c), h)),
                   pl.BlockSpec((rows, LANES), lambda c, h: (rev(c), 0)),
                   _full((1, GDN_DV)), _full((1, LANES)), _full((1, LANES))],
        out_shape=[jax.ShapeDtypeStruct(dproj_main.shape, dproj_main.dtype),
                   jax.ShapeDtypeStruct(dconv.shape, dconv.dtype),
                   jax.ShapeDtypeStruct((t, LANES), F32), jax.ShapeDtypeStruct((1, GDN_DV), F32),
                   jax.ShapeDtypeStruct((1, LANES), F32), jax.ShapeDtypeStruct((1, LANES), F32)],
        scratch_shapes=[pltpu.VMEM((GDN_HEADS, GDN_DK, GDN_DV), F32)],
        input_output_aliases={0: 0, 1: 1},
        compiler_params=_params(("arbitrary", "arbitrary")),
    )(dproj_main, dconv, conv_gdn, proj_main, proj_small, normw, alog, dtb, hist, t_inv, dy)


def out_proj_loss(x, y_ssd, y_gdn, w_out, final_w, target):
    t = x.shape[0]
    tm = min(512, t)

    def body(x_ref, ys_ref, yg_ref, wo_ref, fw_ref, tg_ref, loss_ref, dhid_ref, dys_ref, dyg_ref, dwo_ref, dfw_ref):
        i = pl.program_id(0)
        ys, yg = ys_ref[...], yg_ref[...]
        wo_s, wo_g = wo_ref[:SSD_WIDTH, :], wo_ref[SSD_WIDTH:, :]
        hid = x_ref[...] + _raw_dot(ys, wo_s, 1, 0) + _raw_dot(yg, wo_g, 1, 0)
        out, vjp = jax.vjp(rmsnorm, hid, fw_ref[...])
        err = out - tg_ref[...]
        loss = 0.5 * jnp.sum(jnp.mean(err * err, axis=-1, keepdims=True), axis=0, keepdims=True)
        dhid, dfw = vjp(err * (1.0 / D_MODEL))
        dhid_ref[...] = dhid
        dys_ref[...] = _raw_dot(dhid, wo_s, 1, 1)
        dyg_ref[...] = _raw_dot(dhid, wo_g, 1, 1)
        first = i == 0
        _accumulate(loss_ref, first, jnp.broadcast_to(loss, loss_ref.shape))
        _accumulate(dfw_ref, first, dfw)

        @pl.when(first)
        def _():
            dwo_ref[:SSD_WIDTH, :] = _raw_dot(ys, dhid, 0, 0)
            dwo_ref[SSD_WIDTH:, :] = _raw_dot(yg, dhid, 0, 0)

        @pl.when(i > 0)
        def _():
            dwo_ref[:SSD_WIDTH, :] += _raw_dot(ys, dhid, 0, 0)
            dwo_ref[SSD_WIDTH:, :] += _raw_dot(yg, dhid, 0, 0)

    row = lambda w: pl.BlockSpec((tm, w), lambda i: (i, 0))
    return pl.pallas_call(
        body, name="out_proj_loss", grid=(t // tm,),
        in_specs=[row(D_MODEL), row(SSD_WIDTH), row(GDN_W), _full((SSD_WIDTH + GDN_W, D_MODEL)), _full((1, D_MODEL)),
                  row(D_MODEL)],
        out_specs=[_full((8, LANES)), row(D_MODEL), row(SSD_WIDTH), row(GDN_W), _full((SSD_WIDTH + GDN_W, D_MODEL)),
                   _full((1, D_MODEL))],
        out_shape=[jax.ShapeDtypeStruct((8, LANES), F32), jax.ShapeDtypeStruct((t, D_MODEL), F32),
                   jax.ShapeDtypeStruct((t, SSD_WIDTH), F32), jax.ShapeDtypeStruct((t, GDN_W), F32),
                   jax.ShapeDtypeStruct((SSD_WIDTH + GDN_W, D_MODEL), F32), jax.ShapeDtypeStruct((1, D_MODEL), F32)],
        compiler_params=_params(("arbitrary",)),
    )(x, y_ssd, y_gdn, w_out, final_w, target)


def in_proj_bwd_x(x, normw, w_main, w_small, dproj_main, dproj_conv, dsmall_a, dsmall_b, dhid, slabbed):
    t = x.shape[0]
    tm = min(256, t)
    ni = t // tm
    ns = len(slabbed)

    def body(x_ref, nw_ref, wm_ref, ws_ref, dp_ref, dc_ref, da_ref, db_ref, dh_ref, *rest):
        slab_refs, (gx_ref, dnw_ref), land_refs = rest[:ns], rest[ns:ns + 2], rest[ns + 2:2 * ns + 2]
        sems = rest[2 * ns + 2:]
        i = pl.program_id(0)
        start, finish = _slab_exchange(slab_refs, land_refs, ns, *sems)

        @pl.when(i == 0)
        def _():
            start()

        du = (_raw_dot(dp_ref[...], wm_ref[:, :COL_CONV], 1, 1) + _raw_dot(dc_ref[...], wm_ref[:, COL_CONV:], 1, 1)
              + _raw_dot(da_ref[...] + db_ref[...], ws_ref[...], 1, 1))
        _, vjp = jax.vjp(rmsnorm, x_ref[...], nw_ref[...])
        dx, dnw = vjp(du)
        gx_ref[...] = dx + dh_ref[...]
        _accumulate(dnw_ref, i == 0, dnw)

        @pl.when(i == ni - 1)
        def _():
            finish()

    row = lambda w: pl.BlockSpec((tm, w), lambda i: (i, 0))
    out = pl.pallas_call(
        body, name="in_proj_bwd_x", grid=(ni,),
        in_specs=[row(D_MODEL), _full((1, D_MODEL)), _full((D_MODEL, MAIN)), _full((D_MODEL, LANES)), row(COL_CONV),
                  row(CONV_W), row(LANES), row(LANES), row(D_MODEL)] + [HBM] * ns,
        out_specs=[row(D_MODEL), _full((1, D_MODEL))] + [HBM] * ns,
        out_shape=[jax.ShapeDtypeStruct((t, D_MODEL), F32), jax.ShapeDtypeStruct((1, D_MODEL), F32)]
        + _slab_exchange_shapes(slabbed, []),
        scratch_shapes=_slab_exchange_sems(ns),
        compiler_params=_params(("arbitrary",)),
    )(x, normw, w_main, w_small, dproj_main, dproj_conv, dsmall_a, dsmall_b, dhid, *slabbed)
    return out[0], out[1], out[2:]


def in_proj_bwd_w(u, dproj_main, dsmall_a, dsmall_b):
    t = u.shape[0]
    tm, tn = min(1024, t), COL_CONV // 2

    def body(u_ref, dp_ref, da_ref, db_ref, dwm_ref, dws_ref):
        j, i = pl.program_id(0), pl.program_id(1)
        uu = u_ref[...]
        _accumulate(dwm_ref, i == 0, _raw_dot(uu, dp_ref[...], 0, 0))

        @pl.when(j == 0)
        def _():
            _accumulate(dws_ref, i == 0, _raw_dot(uu, da_ref[...] + db_ref[...], 0, 0))

    return pl.pallas_call(
        body, name="in_proj_bwd_w", grid=(COL_CONV // tn, t // tm),
        in_specs=[pl.BlockSpec((tm, D_MODEL), lambda j, i: (i, 0)), pl.BlockSpec((tm, tn), lambda j, i: (i, j)),
                  pl.BlockSpec((tm, LANES), lambda j, i: (i, 0)), pl.BlockSpec((tm, LANES), lambda j, i: (i, 0))],
        out_specs=[pl.BlockSpec((D_MODEL, tn), lambda j, i: (0, j)), _full((D_MODEL, LANES))],
        out_shape=[jax.ShapeDtypeStruct((D_MODEL, COL_CONV), F32), jax.ShapeDtypeStruct((D_MODEL, LANES), F32)],
        compiler_params=_params(("arbitrary", "arbitrary")),
    )(u, dproj_main, dsmall_a, dsmall_b)


def sum_slabs(a, name):
    n, rows, cols = a.shape
    tr = 64 if rows % 64 == 0 else rows

    def body(a_ref, o_ref):
        acc = a_ref[0].astype(F32)
        for d in range(1, n):
            acc = acc + a_ref[d].astype(F32)
        o_ref[...] = acc

    return pl.pallas_call(
        body, name=name, grid=(rows // tr,),
        in_specs=[pl.BlockSpec((n, tr, cols), lambda i: (0, i, 0))],
        out_specs=pl.BlockSpec((tr, cols), lambda i: (i, 0)),
        out_shape=jax.ShapeDtypeStruct((rows, cols), F32),
        compiler_params=_params(("arbitrary",)),
    )(a)


def adamw(w, g, m, v, name):
    _, rows, cols = w.shape
    tr = 128 if rows % 128 == 0 else rows

    def body(w_ref, g_ref, m_ref, v_ref, d_ref, nm_ref, nv_ref):
        gg = g_ref[...]
        nm = ADAM_B1 * m_ref[...] + (1.0 - ADAM_B1) * gg
        nv = ADAM_B2 * v_ref[...] + (1.0 - ADAM_B2) * (gg * gg)
        m_hat = nm / (1.0 - ADAM_B1 ** ADAM_STEP)
        v_hat = nv / (1.0 - ADAM_B2 ** ADAM_STEP)
        d_ref[...] = -ADAM_LR * (m_hat / (jnp.sqrt(v_hat) + ADAM_EPS) + ADAM_WD * w_ref[...])
        nm_ref[...] = nm
        nv_ref[...] = nv

    spec = pl.BlockSpec((1, tr, cols), lambda i: (0, i, 0))
    shp = jax.ShapeDtypeStruct((1, rows, cols), F32)
    return pl.pallas_call(
        body, name=name, grid=(rows // tr,), in_specs=[spec] * 4, out_specs=[spec] * 3, out_shape=[shp] * 3,
        compiler_params=_params(("arbitrary",)),
    )(w, g.reshape(w.shape), m, v)


def _my_place():
    return lax.axis_index("x"), lax.axis_index("y"), lax.axis_index("c")


def gather_weights(big, small):
    nb, n = len(big), len(big) + len(small)
    parts = 4

    def body(*refs):
        srcs, outs = refs[:n], refs[n:2 * n]
        land_a, land_b = refs[2 * n:2 * n + nb], refs[2 * n + nb:2 * n + 2 * nb]
        send_sems, recv_sems, fwd_send, fwd_recv, local_sems = refs[2 * n + 2 * nb:]
        x, y, c = _my_place()
        me = 2 * x + y
        chips = [(1 - x, y), (x, 1 - y), (1 - x, 1 - y)]
        half = [a.shape[0] // 2 for a in big]

        def ici(j, i):
            px, py = chips[j]
            if i < nb:
                src, dst = srcs[i].at[pl.ds(c * half[i], half[i])], land_a[i].at[j]
            else:
                src, dst = srcs[i], outs[i].at[me]
            return pltpu.make_async_remote_copy(src_ref=src, dst_ref=dst, send_sem=send_sems.at[j * n + i],
                                                recv_sem=recv_sems.at[j * n + i], device_id=(px, py, c),
                                                device_id_type=MESH)

        def ici_arrival(j, i):
            px, py = chips[j]
            dst = land_a[i].at[j] if i < nb else outs[i].at[2 * px + py]
            return pltpu.make_async_remote_copy(src_ref=dst, dst_ref=dst, send_sem=send_sems.at[j * n + i],
                                                recv_sem=recv_sems.at[j * n + i], device_id=(px, py, c),
                                                device_id_type=MESH)

        def forward(j, i, p):
            rows = half[i] // parts
            k = (j * nb + i) * parts + p
            return pltpu.make_async_remote_copy(
                src_ref=land_a[i].at[j, pl.ds(p * rows, rows)], dst_ref=land_b[i].at[j, pl.ds(p * rows, rows)],
                send_sem=fwd_send.at[k], recv_sem=fwd_recv.at[k], device_id=(x, y, 1 - c), device_id_type=MESH)

        def store(j, i, from_sibling):
            px, py = chips[j]
            buf, h = (land_b, 1 - c) if from_sibling else (land_a, c)
            k = n + (j * nb + i) * 2 + (1 if from_sibling else 0)
            return pltpu.make_async_copy(buf[i].at[j], outs[i].at[2 * px + py, pl.ds(h * half[i], half[i])],
                                         local_sems.at[k])

        own = [pltpu.make_async_copy(srcs[i], outs[i].at[me], local_sems.at[i]) for i in range(n)]
        sends = [ici(j, i) for j in range(3) for i in range(n)]
        for cp in own + sends:
            cp.start()
        pending = []
        for j in range(3):
            for i in range(n):
                ici_arrival(j, i).wait_recv()
                if i < nb:
                    fw = [forward(j, i, p) for p in range(parts)]
                    st = store(j, i, False)
                    for cp in fw + [st]:
                        cp.start()
                    pending += [cp.wait_send for cp in fw] + [st.wait]
        for j in range(3):
            for i in range(nb):
                for p in range(parts):
                    forward(j, i, p).wait_recv()
                st = store(j, i, True)
                st.start()
                pending.append(st.wait)
        for cp in sends:
            cp.wait_send()
        for wait in pending:
            wait()
        for cp in own:
            cp.wait()

    shards = list(big) + list(small)
    lands = [pltpu.VMEM((3, a.shape[0] // 2) + a.shape[1:], a.dtype) for a in big]
    return pl.pallas_call(
        body, name="gather_weights",
        in_specs=[HBM] * n, out_specs=[HBM] * n,
        out_shape=[jax.ShapeDtypeStruct((N_CHIP,) + s.shape, s.dtype) for s in shards],
        scratch_shapes=lands + lands + [
            pltpu.SemaphoreType.DMA((3 * n,)), pltpu.SemaphoreType.DMA((3 * n,)),
            pltpu.SemaphoreType.DMA((3 * nb * parts,)), pltpu.SemaphoreType.DMA((3 * nb * parts,)),
            pltpu.SemaphoreType.DMA((n + 6 * nb,))],
        compiler_params=pltpu.CompilerParams(vmem_limit_bytes=VMEM_LIMIT),
    )(*shards)


def _peer(x, y, c, mask):
    mx, my, mc = (mask >> 2) & 1, (mask >> 1) & 1, mask & 1
    return (x ^ mx if mx else x, y ^ my if my else y, c ^ mc if mc else c)


def _slab_exchange_shapes(slabbed, replicated):
    return ([jax.ShapeDtypeStruct(a.shape, a.dtype) for a in slabbed]
            + [jax.ShapeDtypeStruct((N_DEV,) + a.shape, a.dtype) for a in replicated])


def _slab_exchange_sems(n):
    return [pltpu.SemaphoreType.DMA((7 * n,)), pltpu.SemaphoreType.DMA((7 * n,)), pltpu.SemaphoreType.DMA((n,))]


def _slab_exchange(srcs, outs, ns, send_sems, recv_sems, local_sems):
    n = len(srcs)
    x, y, c = _my_place()
    me = 4 * x + 2 * y + c

    def piece(i, dev):
        return srcs[i].at[dev] if i < ns else srcs[i]

    def copies(arriving):
        out = []
        for mask in range(1, N_DEV):
            px, py, pc = _peer(x, y, c, mask)
            dev = 4 * px + 2 * py + pc
            for i in range(n):
                k = (mask - 1) * n + i
                out.append(pltpu.make_async_remote_copy(
                    src_ref=piece(i, dev), dst_ref=outs[i].at[dev if arriving else me], send_sem=send_sems.at[k],
                    recv_sem=recv_sems.at[k], device_id=(px, py, pc), device_id_type=MESH))
        return out

    def local():
        return [pltpu.make_async_copy(piece(i, me), outs[i].at[me], local_sems.at[i]) for i in range(n)]

    def start():
        for cp in local() + copies(False):
            cp.start()

    def finish():
        for cp in copies(True):
            cp.wait_recv()
        for cp in copies(False):
            cp.wait_send()
        for cp in local():
            cp.wait()

    return start, finish


def exchange_halves(landed, replicated):
    n, nr = len(landed), len(replicated)
    streams = 8
    halves = [jax.ShapeDtypeStruct(a.shape[1:], F32) for a in landed]
    sum_rows = 64

    def body(*refs):
        srcs, rep_srcs, outs, rep_outs = refs[:n], refs[n:n + nr], refs[n + nr:2 * n + nr], refs[2 * n + nr:2 * (n + nr)]
        refs = refs[2 * (n + nr):]
        slabs, mine, theirs = refs[:n], refs[n:2 * n], refs[2 * n:3 * n]
        send_sems, recv_sems, in_sems, out_sems = refs[3 * n:3 * n + 4]
        rep_start, rep_finish = _slab_exchange(rep_srcs, rep_outs, 0, *refs[3 * n + 4:])
        rep_start()
        x, y, c = _my_place()
        loads = [pltpu.make_async_copy(srcs[i], slabs[i], in_sems.at[i]) for i in range(n)]
        for cp in loads:
            cp.start()
        for i in range(n):
            loads[i].wait()
            for r in range(0, halves[i].shape[0], sum_rows):
                rows = pl.ds(r, sum_rows)
                acc = slabs[i][0, rows, :].astype(F32)
                for d in range(1, N_DEV):
                    acc = acc + slabs[i][d, rows, :].astype(F32)
                mine[i][rows, :] = acc

        def chunk_copy(i, s):
            rows = halves[i].shape[0] // streams
            k = i * streams + s
            return pltpu.make_async_remote_copy(
                src_ref=mine[i].at[pl.ds(s * rows, rows)], dst_ref=theirs[i].at[pl.ds(s * rows, rows)],
                send_sem=send_sems.at[k], recv_sem=recv_sems.at[k], device_id=(x, y, 1 - c), device_id_type=MESH)

        sends = [chunk_copy(i, s) for i in range(n) for s in range(streams)]
        for cp in sends:
            cp.start()
        own = [pltpu.make_async_copy(mine[i], outs[i].at[c], out_sems.at[i]) for i in range(n)]
        for cp in own:
            cp.start()
        for cp in sends:
            cp.wait_recv()
        got = [pltpu.make_async_copy(theirs[i], outs[i].at[1 - c], out_sems.at[n + i]) for i in range(n)]
        for cp in got:
            cp.start()
        for cp in sends:
            cp.wait_send()
        for cp in own + got:
            cp.wait()
        rep_finish()

    vmem = [pltpu.VMEM(a.shape, a.dtype) for a in halves]
    out = pl.pallas_call(
        body, name="exchange_halves",
        in_specs=[HBM] * (n + nr), out_specs=[HBM] * (n + nr),
        out_shape=[jax.ShapeDtypeStruct((2,) + a.shape, a.dtype) for a in halves]
        + _slab_exchange_shapes([], replicated),
        scratch_shapes=[pltpu.VMEM(a.shape, a.dtype) for a in landed] + vmem + vmem
        + [pltpu.SemaphoreType.DMA((n * streams,)), pltpu.SemaphoreType.DMA((n * streams,)),
           pltpu.SemaphoreType.DMA((n,)), pltpu.SemaphoreType.DMA((2 * n,))] + _slab_exchange_sems(nr),
        compiler_params=pltpu.CompilerParams(vmem_limit_bytes=VMEM_LIMIT),
    )(*landed, *replicated)
    return out[:n], out[n:]


def _pack_cols(pieces):
    offs, pos = [], 0
    for a in pieces:
        offs.append(pos)
        pos += a.shape[1]
    rows8 = [jnp.pad(a.astype(F32), ((0, 8 - a.shape[0]), (0, 0))) for a in pieces]
    return jnp.concatenate(rows8, axis=1), offs


def adamw_many(ws, gs, ms, vs):
    n = len(ws)

    def body(*refs):
        w_r, g_r, m_r, v_r = refs[:n], refs[n:2 * n], refs[2 * n:3 * n], refs[3 * n:4 * n]
        d_o, m_o, v_o = refs[4 * n:5 * n], refs[5 * n:6 * n], refs[6 * n:7 * n]
        for i in range(n):
            gg = g_r[i][...]
            nm = ADAM_B1 * m_r[i][...] + (1.0 - ADAM_B1) * gg
            nv = ADAM_B2 * v_r[i][...] + (1.0 - ADAM_B2) * (gg * gg)
            m_hat = nm / (1.0 - ADAM_B1 ** ADAM_STEP)
            v_hat = nv / (1.0 - ADAM_B2 ** ADAM_STEP)
            d_o[i][...] = -ADAM_LR * (m_hat / (jnp.sqrt(v_hat) + ADAM_EPS) + ADAM_WD * w_r[i][...])
            m_o[i][...] = nm
            v_o[i][...] = nv

    shapes = [jax.ShapeDtypeStruct(w.shape, F32) for w in ws]
    out = pl.pallas_call(body, name="adamw_small", out_shape=shapes * 3,
                         compiler_params=pltpu.CompilerParams(vmem_limit_bytes=VMEM_LIMIT))(*ws, *gs, *ms, *vs)
    return out[:n], out[n:2 * n], out[2 * n:]


def _lanes(vec, start):
    n = vec.shape[-1]
    return jnp.pad(vec.reshape(1, n).astype(F32), ((0, 0), (start, LANES - start - n)))


def kernel(x, norm_w, w_in, ssd_conv_w, ssd_conv_b, ssd_dt_bias, ssd_a_log, ssd_d, ssd_norm_w, gdn_conv_w, gdn_dt_bias, gdn_a_log, gdn_norm_w, w_out, final_norm_w, loss_target, m_norm_w, m_w_in, m_ssd_conv_w, m_ssd_conv_b, m_ssd_dt_bias, m_ssd_a_log, m_ssd_d, m_ssd_norm_w, m_gdn_conv_w, m_gdn_dt_bias, m_gdn_a_log, m_gdn_norm_w, m_w_out, m_final_norm_w, v_norm_w, v_w_in, v_ssd_conv_w, v_ssd_conv_b, v_ssd_dt_bias, v_ssd_a_log, v_ssd_d, v_ssd_norm_w, v_gdn_conv_w, v_gdn_dt_bias, v_gdn_a_log, v_gdn_norm_w, v_w_out, v_final_norm_w):
    xs = x[0]
    target = loss_target[0]
    chip = 2 * lax.axis_index("x") + lax.axis_index("y")
    w_in_shard, w_out_shard = w_in[0], w_out[0]
    in_cols = w_in_shard.shape[1]
    out_rows = w_out_shard.shape[0]

    g_in, g_out, g_cs, g_cg = gather_weights(
        [w_in_shard.astype(MXU_DTYPE), w_out_shard.astype(MXU_DTYPE)], [ssd_conv_w[0], gdn_conv_w[0]])
    w_in_full = jnp.concatenate([g_in[k] for k in range(N_CHIP)], axis=1)
    w_out_full = g_out.reshape(N_CHIP * out_rows, D_MODEL)
    cw_ssd = jnp.concatenate([g_cs[k] for k in range(N_CHIP)], axis=1)
    cw_gdn = jnp.concatenate([g_cg[k] for k in range(N_CHIP)], axis=1)
    cb_ssd, cb_gdn = ssd_conv_b, jnp.zeros((1, GDN_CONV), F32)
    o_xbc, o_dt, o_gate, o_qkv, o_ab = 1024, 2560, 2576, 3600, 6672
    w_main = jnp.concatenate([w_in_full[:, :o_xbc], w_in_full[:, o_gate:o_qkv], w_in_full[:, o_qkv:o_ab],
                              w_in_full[:, o_xbc:o_dt]], axis=1)
    w_small = jnp.concatenate([w_in_full[:, o_dt:o_gate], w_in_full[:, o_ab:],
                               jnp.zeros((D_MODEL, LANES - 32), MXU_DTYPE)], axis=1)
    alog = _lanes(ssd_a_log, 0) + _lanes(gdn_a_log, LANE_GA)
    dtb = _lanes(ssd_dt_bias, 0) + _lanes(gdn_dt_bias, LANE_GA)
    dvec = _lanes(ssd_d, 0)
    fw = final_norm_w.reshape(1, D_MODEL)

    cw, cb = jnp.concatenate([cw_gdn, cw_ssd], axis=1), jnp.concatenate([cb_gdn, cb_ssd], axis=1)
    proj_main, proj_small, u = in_proj(xs, norm_w, w_main, w_small)
    conv_out, x_conv, dsilu_conv = in_proj_conv(u, w_main, cw, cb)
    y_ssd, hist_ssd = ssd_fwd(conv_out, proj_main, proj_small, ssd_norm_w, alog, dtb, dvec)
    y_gdn, hist_gdn, tinv_gdn = gdn_fwd(conv_out, proj_main, proj_small, gdn_norm_w, alog, dtb)

    loss_blk, dhid, dy_ssd, dy_gdn, d_w_out, d_fw = out_proj_loss(xs, y_ssd, y_gdn, w_out_full, fw, target)
    dconv, dproj_main, dsmall_ssd, d_ssd_nw, d_alog_s, d_dtb_s, d_dvec = ssd_bwd(
        conv_out, proj_main, proj_small, ssd_norm_w, alog, dtb, dvec, hist_ssd, dy_ssd)
    dproj_main, dconv, dsmall_gdn, d_gdn_nw, d_alog_g, d_dtb_g = gdn_bwd(
        dproj_main, dconv, conv_out, proj_main, proj_small, gdn_norm_w, alog, dtb, hist_gdn, tinv_gdn, dy_gdn)
    slabs_out = d_w_out.reshape(N_DEV, out_rows // 2, D_MODEL).astype(COMM_DTYPE)
    dproj_conv, d_w_conv, dwb, (r_out,) = conv_bwd_w(u, x_conv, dsilu_conv, cw, dconv, [slabs_out])
    dwb_gdn, dwb_ssd = dwb[:, :GDN_CONV], dwb[:, GDN_CONV:]
    d_w_zg, d_w_small = in_proj_bwd_w(u, dproj_main, dsmall_ssd, dsmall_gdn)
    order = [(d_w_zg, 0, COL_GATE), (d_w_conv, COL_SSD - COL_CONV, CONV_W), (d_w_small, 0, 16),
             (d_w_zg, COL_GATE, COL_CONV), (d_w_conv, 0, COL_SSD - COL_CONV), (d_w_small, 16, 32)]
    shards, pos = [[] for _ in range(N_CHIP)], 0
    for src, lo, hi in order:
        while lo < hi:
            k = pos // in_cols
            n = min(hi - lo, (k + 1) * in_cols - pos)
            shards[k].append(src[:, lo:lo + n].astype(COMM_DTYPE))
            lo, pos = lo + n, pos + n
    slabs_in = jnp.stack([jnp.concatenate(p, axis=1) for p in shards]).reshape(N_DEV, D_MODEL // 2, in_cols)
    grad_x, d_norm_w, (r_in,) = in_proj_bwd_x(xs, norm_w, w_main, w_small, dproj_main, dproj_conv, dsmall_ssd,
                                               dsmall_gdn, dhid, [slabs_in])
    d_alog, d_dtb = d_alog_s + d_alog_g, d_dtb_s + d_dtb_g
    packed, (o_nw, o_cs, o_cg, o_snw, o_fw, o_al, o_db, o_dv, o_gnw, o_loss) = _pack_cols([
        d_norm_w, dwb_ssd, dwb_gdn,
        d_ssd_nw.reshape(1, SSD_WIDTH), d_fw, d_alog, d_dtb, d_dvec, d_gdn_nw, loss_blk])

    (full_in, full_out), (r_small,) = exchange_halves([r_in, r_out], [packed])
    tot = sum_slabs(r_small, "sum_small")
    grad_w_in = full_in.reshape(D_MODEL, in_cols)
    grad_w_out = full_out.reshape(out_rows, D_MODEL)
    loss = tot[0, o_loss]
    sc, gc = ssd_conv_w.shape[2], gdn_conv_w.shape[2]
    row = lambda off, n, r=0: tot[r:r + 1, off:off + n]
    gs = [row(o_nw, D_MODEL),
          lax.dynamic_slice(tot, (0, o_cs + chip * sc), (4, sc)),
          row(o_cs, SSD_CONV, 4),
          row(o_db, SSD_HEADS), row(o_al, SSD_HEADS), row(o_dv, SSD_HEADS),
          row(o_snw, SSD_WIDTH),
          lax.dynamic_slice(tot, (0, o_cg + chip * gc), (4, gc)),
          row(o_db + LANE_GA, GDN_HEADS), row(o_al + LANE_GA, GDN_HEADS),
          row(o_gnw, GDN_DV), row(o_fw, D_MODEL)]

    names = ["norm_w", "ssd_conv_w", "ssd_conv_b", "ssd_dt_bias", "ssd_a_log", "ssd_d", "ssd_norm_w", "gdn_conv_w",
             "gdn_dt_bias", "gdn_a_log", "gdn_norm_w", "final_norm_w"]
    ws = [norm_w, ssd_conv_w, ssd_conv_b, ssd_dt_bias, ssd_a_log, ssd_d, ssd_norm_w, gdn_conv_w, gdn_dt_bias,
          gdn_a_log, gdn_norm_w, final_norm_w]
    ms = [m_norm_w, m_ssd_conv_w, m_ssd_conv_b, m_ssd_dt_bias, m_ssd_a_log, m_ssd_d, m_ssd_norm_w, m_gdn_conv_w,
          m_gdn_dt_bias, m_gdn_a_log, m_gdn_norm_w, m_final_norm_w]
    vs = [v_norm_w, v_ssd_conv_w, v_ssd_conv_b, v_ssd_dt_bias, v_ssd_a_log, v_ssd_d, v_ssd_norm_w, v_gdn_conv_w,
          v_gdn_dt_bias, v_gdn_a_log, v_gdn_norm_w, v_final_norm_w]
    shapes = [w.shape for w in ws]
    flat = lambda arrs: [a.reshape(g.shape) for a, g in zip(arrs, gs)]
    d_s, m_s, v_s = adamw_many(flat(ws), gs, flat(ms), flat(vs))
    back = lambda arrs: dict(zip(names, [a.reshape(s) for a, s in zip(arrs, shapes)]))
    delta, new_m, new_v, grads = back(d_s), back(m_s), back(v_s), back(gs)
    d_in, m_in, v_in = adamw(w_in, grad_w_in, m_w_in, v_w_in, "adamw_w_in")
    d_out, m_out, v_out = adamw(w_out, grad_w_out, m_w_out, v_w_out, "adamw_w_out")
    for tbl, a_in, a_out in ((grads, grad_w_in[None], grad_w_out[None]), (delta, d_in, d_out), (new_m, m_in, m_out),
                             (new_v, v_in, v_out)):
        tbl["w_in"] = a_in
        tbl["w_out"] = a_out

    order = ["norm_w", "w_in", "ssd_conv_w", "ssd_conv_b", "ssd_dt_bias", "ssd_a_log", "ssd_d", "ssd_norm_w",
             "gdn_conv_w", "gdn_dt_bias", "gdn_a_log", "gdn_norm_w", "w_out", "final_norm_w"]
    return (loss.reshape(()), grad_x[None], *[grads[k] for k in order], *[delta[k] for k in order],
            *[new_m[k] for k in order], *[new_v[k] for k in order])
```

```python
import functools

import jax
import jax.numpy as jnp
from jax import lax
from jax.experimental import pallas as pl
from jax.experimental.pallas import tpu as pltpu

F32 = jnp.float32
MXU_DTYPE = jnp.bfloat16
COMM_DTYPE = jnp.bfloat16
MESH = pl.DeviceIdType.MESH

D_MODEL = 1024
CHUNK = 64
EPS = 1e-6
SSD_HEADS, SSD_GROUPS, SSD_STATE = 16, 2, 128
SSD_WIDTH, SSD_CONV = 1024, 1536
SSD_GW = SSD_WIDTH // SSD_GROUPS
GDN_HEADS, GDN_DK, GDN_DV = 8, 128, 128
GDN_W, GDN_CONV = 1024, 3072
GDN_HC = 2 * GDN_DK + GDN_DV
IN_DIM = 6688
MAIN = 6656
LANES = 128
COL_Z, COL_GATE, COL_GDN, COL_SSD = 0, 1024, 2048, 5120
COL_CONV = COL_GDN
CONV_W = MAIN - COL_CONV
GDN_HB = 8
GDN_CB = 4
SSD_CB = 4
LANE_GA, LANE_GB = 16, 24
N_DEV, N_CHIP = 8, 4
VMEM_LIMIT = 52 * 1024 * 1024

ADAM_LR, ADAM_B1, ADAM_B2, ADAM_EPS, ADAM_WD, ADAM_STEP = 0.001, 0.9, 0.999, 1e-08, 0.01, 10


def _split(a, n):
    parts, rest = [], a.astype(F32)
    for i in range(n):
        p = rest.astype(MXU_DTYPE)
        parts.append(p)
        if i < n - 1:
            rest = rest - p.astype(F32)
    return parts


def _raw_dot(a, b, ca, cb, mode="bf16"):
    d = lambda u, v: lax.dot_general(u, v, (((ca,), (cb,)), ((), ())), preferred_element_type=F32)
    if mode == "bf16":
        return d(a.astype(MXU_DTYPE), b.astype(MXU_DTYPE))
    if mode == "x3":
        (ah, al), (bh, bl) = _split(a, 2), _split(b, 2)
        return d(ah, bh) + (d(ah, bl) + d(al, bh))
    if mode == "sel_a":
        a0 = a.astype(MXU_DTYPE)
        b1, b2, b3 = _split(b, 3)
        return d(a0, b1) + (d(a0, b2) + d(a0, b3))
    assert mode == "sel_b", mode
    b0 = b.astype(MXU_DTYPE)
    a1, a2, a3 = _split(a, 3)
    return d(a1, b0) + (d(a2, b0) + d(a3, b0))


@functools.partial(jax.custom_vjp, nondiff_argnums=(2,))
def mm_nn(a, b, mode="bf16"):
    return _raw_dot(a, b, 1, 0, mode)


@functools.partial(jax.custom_vjp, nondiff_argnums=(2,))
def mm_nt(a, b, mode="bf16"):
    return _raw_dot(a, b, 1, 1, mode)


@functools.partial(jax.custom_vjp, nondiff_argnums=(2,))
def mm_tn(a, b, mode="bf16"):
    return _raw_dot(a, b, 0, 0, mode)


_SAME = {"bf16": ("bf16", "bf16"), "x3": ("x3", "x3")}
_NN_BWD = dict(_SAME, sel_a=("bf16", "sel_a"), sel_b=("sel_b", "bf16"))
_NT_BWD = dict(_SAME, sel_a=("bf16", "sel_b"), sel_b=("sel_b", "bf16"))
_TN_BWD = dict(_SAME, sel_a=("bf16", "sel_a"), sel_b=("sel_a", "bf16"))
mm_nn.defvjp(lambda a, b, m: (_raw_dot(a, b, 1, 0, m), (a, b)),
             lambda m, r, g: (mm_nt(g, r[1], _NN_BWD[m][0]), mm_tn(r[0], g, _NN_BWD[m][1])))
mm_nt.defvjp(lambda a, b, m: (_raw_dot(a, b, 1, 1, m), (a, b)),
             lambda m, r, g: (mm_nn(g, r[1], _NT_BWD[m][0]), mm_tn(g, r[0], _NT_BWD[m][1])))
mm_tn.defvjp(lambda a, b, m: (_raw_dot(a, b, 0, 0, m), (a, b)),
             lambda m, r, g: (mm_nt(r[1], g, _TN_BWD[m][0]), mm_nn(r[0], g, _TN_BWD[m][1])))


@jax.custom_jvp
def sigmoid(x):
    return 1.0 / (1.0 + jnp.exp(-x))


@sigmoid.defjvp
def _sigmoid_jvp(p, t):
    s = sigmoid(p[0])
    return s, t[0] * s * (1.0 - s)


@jax.custom_jvp
def softplus(x):
    return jnp.maximum(x, 0.0) + jnp.log(1.0 + jnp.exp(-jnp.abs(x)))


@softplus.defjvp
def _softplus_jvp(p, t):
    return softplus(p[0]), t[0] * sigmoid(p[0])


def silu(x):
    return x * sigmoid(x)


def rmsnorm(x, w):
    return x * lax.rsqrt(jnp.mean(x * x, axis=-1, keepdims=True) + EPS) * w


def _iota(shape, dim):
    return lax.broadcasted_iota(jnp.int32, shape, dim)


def _halves():
    lane = _iota((1, LANES), 1) >> 6
    return _ind(lane == 0), _ind(lane == 1)


def _block_diag(pair):
    h0, h1 = _halves()
    return jnp.concatenate([pair * h0, pair * h1], axis=0)


def _tri_inv_impl(mats):
    r, c = _iota((CHUNK, LANES), 0), _iota((CHUNK, LANES), 1) & (CHUNK - 1)
    eye = _ind(r == c)
    blockdiag = _ind((r >> 4) == (c >> 4))
    dot = lambda u, v: _raw_dot(u, _block_diag(v), 1, 0, "x3")
    dot1 = lambda u, v: _raw_dot(u, _block_diag(v), 1, 0)
    each = lambda f, *ls: [f(*xs) for xs in zip(*ls)]
    dg = each(lambda a: a * blockdiag, mats)
    off = each(lambda a, d: a - d, mats, dg)
    m = each(lambda d: -d, dg)
    p = each(lambda x: eye + x, m)
    pw = m
    for _ in range(3):
        pw = each(lambda x: dot1(x, x), pw)
        p = each(lambda x, y: x + dot1(x, y), p, pw)
    e = each(dot, p, off)
    e2 = each(lambda x: dot1(x, x), e)
    q = each(lambda x: eye - x, e)
    q = each(lambda x, y: x + dot1(x, y), q, e2)
    return each(dot, q, p)


def _tri_inv_bwd(ts, gs):
    h0, h1 = _halves()
    x = [mm_nt(g, _block_diag(t)) for g, t in zip(gs, ts)]
    full = [mm_tn(t, y) for t, y in zip(ts, x)]
    return [-(f[:CHUNK] * h0 + f[CHUNK:] * h1) for f in full]


@jax.custom_vjp
def tri_inv(mats):
    return _tri_inv_impl(mats)


def _tri_inv_fwd(mats):
    ts = _tri_inv_impl(mats)
    return ts, ts


tri_inv.defvjp(_tri_inv_fwd, lambda ts, gs: (_tri_inv_bwd(ts, gs),))


@jax.custom_vjp
def tri_inv_saved(mats, ts):
    del mats
    return ts


tri_inv_saved.defvjp(lambda mats, ts: (ts, ts),
                     lambda ts, gs: (_tri_inv_bwd(ts, gs), [jnp.zeros_like(t) for t in ts]))


def _ind(cond):
    return jnp.where(cond, 1.0, 0.0).astype(F32)


def _chunk_masks():
    r, c = _iota((CHUNK, CHUNK), 0), _iota((CHUNK, CHUNK), 1)
    return _ind(r >= c), _ind(r > c), _ind(r == c), _ind(_iota((CHUNK, 1), 0) == CHUNK - 1)


def _log_decay_cumsum(small, alog, dtb, tri):
    sp = softplus(small + dtb)
    la = -jnp.exp(alog) * sp
    return sp, mm_nn(tri, la, "sel_a")


def _col_of(x, lane):
    return jnp.sum(x * _ind(_iota((1, LANES), 1) == lane), axis=1, keepdims=True)


def _pair_masks():
    r, c = _iota((CHUNK, LANES), 0), _iota((CHUNK, LANES), 1)
    c6 = c & (CHUNK - 1)
    return _ind(r >= c6), _ind(r > c6), (_ind(c == r), _ind(c == r + CHUNK))


def _decay_pair(col_a, col_b, tri_w, eye_w):
    h0, h1 = _halves()
    col = col_a * h0 + col_b * h1
    row = jnp.sum(col_a * eye_w[0] + col_b * eye_w[1], axis=0, keepdims=True)
    return jnp.exp((col - row) * tri_w) * tri_w


def gdn_chunk(h0, qs, ks, vs, smalls, gates, normw, alog, dtb, states, saved_t=None):
    tri, _, _, last = _chunk_masks()
    tri_w, strict_w, eye_w = _pair_masks()
    nh = len(qs[0])
    flat = lambda xss: [x for xs in xss for x in xs]
    lacs = [_log_decay_cumsum(sm, alog, dtb, tri)[1] for sm in smalls]
    qs, ks, vs, gates = flat(qs), flat(ks), flat(vs), flat(gates)
    heads, pairs = range(len(qs)), range(len(qs) // 2)
    each = lambda f, *ls: [f(*xs) for xs in zip(*ls)]
    ab = lambda xs, p: (xs[2 * p], xs[2 * p + 1])
    stack = lambda xs: jnp.concatenate(xs, axis=0)
    gc = [_col_of(lacs[i // nh], LANE_GA + h0 + i % nh) for i in heads]
    beta = [sigmoid(_col_of(smalls[i // nh], LANE_GB + h0 + i % nh)) for i in heads]
    decay = [_decay_pair(*ab(gc, p), tri_w, eye_w) for p in pairs]
    gl = each(lambda x: jnp.sum(x * last, axis=0, keepdims=True), gc)
    q = each(lambda x: x * lax.rsqrt(jnp.sum(x * x, axis=-1, keepdims=True) + EPS) * (GDN_DK ** -0.5), qs)
    k = each(lambda x: x * lax.rsqrt(jnp.sum(x * x, axis=-1, keepdims=True) + EPS), ks)
    kb = each(lambda x, b: x * b, k, beta)
    eg = each(jnp.exp, gc)
    zero = jnp.zeros((CHUNK, GDN_DK), F32)
    k_bd = [stack([join_lanes([k[2 * p], zero]), join_lanes([zero, k[2 * p + 1]])]) for p in pairs]
    a = [mm_nt(join_lanes(list(ab(kb, p))), k_bd[p]) * (decay[p] * strict_w) for p in pairs]
    t = tri_inv(a) if saved_t is None else tri_inv_saved(a, saved_t)
    attn = [mm_nt(join_lanes(list(ab(q, p))), k_bd[p]) * decay[p] for p in pairs]
    rhs = [stack([join_lanes([vs[h] * beta[h], kb[h] * eg[h]]) for h in (2 * p, 2 * p + 1)]) for p in pairs]
    uw = [mm_nn(_block_diag(t[p]), rhs[p]) for p in pairs]
    uw = [x for p in pairs for x in split_rows(uw[p])]
    u, w = zip(*[split_lanes(x) for x in uw])
    ys = []
    for c in range(len(smalls)):
        hs = range(c * nh, (c + 1) * nh)
        v_new = [u[i] - mm_nn(w[i], states[i % nh]) for i in hs]
        av = [mm_nn(_block_diag(attn[c * nh // 2 + p]), stack(list(ab(v_new, p)))) for p in range(nh // 2)]
        av = [x for y in av for x in split_rows(y)]
        o = [mm_nn(q[i] * eg[i], states[i % nh]) + av[i % nh] for i in hs]
        states = [states[i % nh] * jnp.exp(gl[i]) + mm_tn(k[i] * jnp.exp(gl[i] - gc[i]), v_new[i % nh]) for i in hs]
        ys.append([rmsnorm(o[i % nh], normw) * silu(gates[i]) for i in hs])
    return ys, states, t


@jax.custom_vjp
def split_rows(x):
    n = x.shape[0] // 2
    return [x[:n], x[n:]]


split_rows.defvjp(lambda x: (split_rows(x), None), lambda _, gs: (jnp.concatenate(gs, axis=0),))


@jax.custom_vjp
def split_lanes(x):
    return [x[:, i * LANES:(i + 1) * LANES] for i in range(x.shape[1] // LANES)]


@jax.custom_vjp
def join_lanes(xs):
    return jnp.concatenate(xs, axis=1)


split_lanes.defvjp(lambda x: (split_lanes(x), None), lambda _, gs: (join_lanes(gs),))
join_lanes.defvjp(lambda xs: (join_lanes(xs), None), lambda _, g: (split_lanes(g),))


def ssd_chunk(xs, bm, cm, z, smalls, normw, alog, dtb, dvec, state):
    tri, _, _, last = _chunk_masks()
    tri_w, _, eye_w = _pair_masks()
    h0, h1 = _halves()
    hpg = SSD_HEADS // SSD_GROUPS
    ng = len(normw)
    flat = lambda xss: [x for xs_ in xss for x in xs_]
    xs, bm, cm, z = flat(xs), flat(bm), flat(cm), flat(z)
    units, pairs = range(len(xs)), range(hpg // 2)
    each = lambda f, *ls: [f(*a) for a in zip(*ls)]
    sp_lac = [_log_decay_cumsum(sm, alog, dtb, tri) for sm in smalls]
    lac_last = [jnp.sum(lac * last, axis=0, keepdims=True) for _, lac in sp_lac]
    sel = [_ind(_iota((LANES, SSD_GW), 0) == g * hpg + (_iota((LANES, SSD_GW), 1) >> 6)) for g in range(ng)]
    expand = lambda vs, mode: [mm_nn(vs[i // ng], sel[i % ng], mode) for i in units]
    dt_e = expand([sp for sp, _ in sp_lac], "bf16")
    elac_e = expand([jnp.exp(lac) for _, lac in sp_lac], "bf16")
    toend_e = expand([jnp.exp(ll - lac) for (_, lac), ll in zip(sp_lac, lac_last)], "bf16")
    row8 = _iota((8, 1), 0)
    two_e = expand([_ind(row8 == 0) * dvec + _ind(row8 == 1) * jnp.exp(ll) for ll in lac_last], "sel_b")
    d_e = each(lambda v: jnp.sum(v * _ind(row8 == 0), axis=0, keepdims=True), two_e)
    chunk_e = each(lambda v: jnp.sum(v * _ind(row8 == 1), axis=0, keepdims=True), two_e)
    xdt = each(lambda a, b: a * b, xs, dt_e)
    cb_w = each(lambda c_, b_: mm_nt(c_, jnp.concatenate([b_, b_], axis=0)), cm, bm)
    x_pairs = each(split_lanes, xdt)
    col = lambda i, j: _col_of(sp_lac[i // ng][1], (i % ng) * hpg + j)
    lms = [[_decay_pair(col(i, 2 * p), col(i, 2 * p + 1), tri_w, eye_w) for p in pairs] for i in units]
    stacked = [[jnp.concatenate([x_pairs[i][p] * h0, x_pairs[i][p] * h1], axis=0) for p in pairs] for i in units]
    terms = [[mm_nn(cb_w[i] * lms[i][p], stacked[i][p]) for p in pairs] for i in units]
    y_in = [join_lanes(terms[i]) + xs[i] * d_e[i] for i in units]
    state_in = each(lambda b_, xd, te: mm_tn(b_, xd * te), bm, xdt, toend_e)
    outs = []
    for c in range(len(smalls)):
        us = range(c * ng, (c + 1) * ng)
        y = [mm_nn(cm[i], state[i % ng]) * elac_e[i] + y_in[i] for i in us]
        state = [state[i % ng] * chunk_e[i] + state_in[i] for i in us]
        outs.append([rmsnorm(y[i % ng] * silu(z[i]), normw[i % ng]) for i in us])
    return outs, state


def _params(sem=None):
    return pltpu.CompilerParams(dimension_semantics=sem, vmem_limit_bytes=VMEM_LIMIT)


def _full(shape):
    n = len(shape)
    return pl.BlockSpec(shape, lambda *_: (0,) * n)


ANY = pl.BlockSpec(memory_space=pl.ANY)
HBM = pl.BlockSpec(memory_space=pltpu.HBM)


def in_proj(x, normw, w_main, w_small):
    t = x.shape[0]
    tm, tn = min(2048, t), 512

    def body(x_ref, nw_ref, wm_ref, ws_ref, pm_ref, ps_ref, u_ref):
        @pl.when(pl.program_id(1) == 0)
        def _():
            u = rmsnorm(x_ref[...], nw_ref[...]).astype(MXU_DTYPE)
            u_ref[...] = u
            ps_ref[...] = _raw_dot(u, ws_ref[...], 1, 0)
        pm_ref[...] = _raw_dot(u_ref[...], wm_ref[...], 1, 0)

    return pl.pallas_call(
        body, name="in_proj", grid=(t // tm, COL_CONV // tn),
        in_specs=[pl.BlockSpec((tm, D_MODEL), lambda i, j: (i, 0)), _full((1, D_MODEL)),
                  pl.BlockSpec((D_MODEL, tn), lambda i, j: (0, j)), _full((D_MODEL, LANES))],
        out_specs=[pl.BlockSpec((tm, tn), lambda i, j: (i, j)), pl.BlockSpec((tm, LANES), lambda i, j: (i, 0)),
                   pl.BlockSpec((tm, D_MODEL), lambda i, j: (i, 0))],
        out_shape=[jax.ShapeDtypeStruct((t, COL_CONV), F32), jax.ShapeDtypeStruct((t, LANES), F32),
                   jax.ShapeDtypeStruct((t, D_MODEL), MXU_DTYPE)],
        compiler_params=_params(("arbitrary", "arbitrary")),
    )(x, normw, w_main, w_small)


CONV_TC = 512
HALO = 8


def _shift_down(cur, prev, s):
    rolled = pltpu.roll(cur, s, 0)
    top = jnp.where(_iota((HALO, cur.shape[1]), 0) < s, pltpu.roll(prev, s, 0), rolled[:HALO])
    if cur.shape[0] == HALO:
        return top
    return jnp.concatenate([top, rolled[HALO:]], axis=0)


def _shift_up(cur, nxt, s):
    n = cur.shape[0]
    rolled = pltpu.roll(cur, n - s, 0)
    bot = jnp.where(_iota((HALO, cur.shape[1]), 0) >= HALO - s, pltpu.roll(nxt, HALO - s, 0), rolled[n - HALO:])
    return jnp.concatenate([rolled[:n - HALO], bot], axis=0)


def _conv_pre(cur, prev, w_ref, b, cols=slice(None)):
    acc = cur * w_ref[3:4, cols] + b
    shifted = [cur]
    for s in (1, 2, 3):
        sh = _shift_down(cur, prev, s)
        shifted.append(sh)
        acc = acc + sh * w_ref[3 - s:4 - s, cols]
    return acc, shifted


def in_proj_conv(u, w_main, w, b):
    t = u.shape[0]
    tm, tn = min(2048, t), CONV_TC
    rc = min(256, tm)
    c0, nj = COL_CONV // tn, CONV_W // tn

    def body(u_ref, wm_ref, w_ref, b_ref, out_ref, x_ref, ds_ref, halo_ref):
        j = pl.program_id(1)

        @pl.when(pl.program_id(0) == 0)
        def _():
            halo_ref[j] = jnp.zeros((HALO, tn), F32)

        prev = halo_ref[j]
        for r in range(tm // rc):
            rows = pl.ds(r * rc, rc)
            p = _raw_dot(u_ref[rows, :], wm_ref[...], 1, 0)
            x_ref[rows, :] = p.astype(x_ref.dtype)
            pre, _ = _conv_pre(p, prev, w_ref, b_ref[...])
            sg = sigmoid(pre)
            out_ref[rows, :] = pre * sg
            ds_ref[rows, :] = (sg * (1.0 + pre * (1.0 - sg))).astype(ds_ref.dtype)
            prev = p[rc - HALO:]
        halo_ref[j] = prev

    blk = pl.BlockSpec((tm, tn), lambda i, j: (i, j))
    return pl.pallas_call(
        body, name="in_proj_conv", grid=(t // tm, nj),
        in_specs=[pl.BlockSpec((tm, D_MODEL), lambda i, j: (i, 0)),
                  pl.BlockSpec((D_MODEL, tn), lambda i, j: (0, c0 + j)),
                  pl.BlockSpec((4, tn), lambda i, j: (0, j)), pl.BlockSpec((1, tn), lambda i, j: (0, j))],
        out_specs=[blk, blk, blk],
        out_shape=[jax.ShapeDtypeStruct((t, CONV_W), F32), jax.ShapeDtypeStruct((t, CONV_W), MXU_DTYPE),
                   jax.ShapeDtypeStruct((t, CONV_W), MXU_DTYPE)],
        scratch_shapes=[pltpu.VMEM((nj, HALO, tn), F32)],
        compiler_params=_params(("arbitrary", "arbitrary")),
    )(u, w_main, w, b)


def conv_bwd_w(u, x_conv, dsilu, w, dout, slabbed):
    t = u.shape[0]
    tt, tn = min(512, t), 3 * CONV_TC
    nt, nj = t // tt, CONV_W // tn
    ns = len(slabbed)
    halo_op = 2 * HALO
    after = lambda i, h: jnp.minimum((i + 1) * (tt // h), t // h - 1)

    def body(u_ref, x_ref, ds_ref, ds_nxt_ref, w_ref, do_ref, do_nxt_ref, *rest):
        slab_refs, (dx_ref, dw_ref, dwb_ref) = rest[:ns], rest[ns:ns + 3]
        land_refs, sems = rest[ns + 3:2 * ns + 3], rest[2 * ns + 3:]
        j, i = pl.program_id(0), pl.program_id(1)
        start, finish = _slab_exchange(slab_refs, land_refs, ns, *sems)

        @pl.when(jnp.logical_and(j == 0, i == 0))
        def _():
            start()

        @pl.when(i == 0)
        def _():
            dw_ref[...] = jnp.zeros(dw_ref.shape, F32)
            dwb_ref[...] = jnp.zeros(dwb_ref.shape, F32)

        uu = u_ref[...]
        row = _iota((HALO, CONV_TC), 0)
        last = i == nt - 1
        for piece in range(tn // CONV_TC):
            cols = slice(piece * CONV_TC, (piece + 1) * CONV_TC)
            x = x_ref[:, cols].astype(F32)
            dpre = do_ref[:, cols] * ds_ref[:, cols].astype(F32)
            dpre_nxt = jnp.where(last, 0.0, do_nxt_ref[:, cols] * ds_nxt_ref[:, cols].astype(F32)[:HALO])
            ups = [dpre] + [_shift_up(dpre, dpre_nxt, s) for s in (1, 2, 3)]
            dx = ups[0] * w_ref[3:4, cols]
            upd = jnp.where(row == 4, jnp.sum(dpre, axis=0, keepdims=True), 0.0)
            for s in range(4):
                if s:
                    dx = dx + ups[s] * w_ref[3 - s:4 - s, cols]
                upd = upd + jnp.where(row == 3 - s, jnp.sum(ups[s] * x, axis=0, keepdims=True), 0.0)
            dx = dx.astype(dx_ref.dtype)
            dx_ref[:, cols] = dx
            dw_ref[:, cols] += _raw_dot(uu, dx, 0, 0)
            dwb_ref[:, cols] += upd

        @pl.when(jnp.logical_and(j == nj - 1, last))
        def _():
            finish()

    out = pl.pallas_call(
        body, name="conv_bwd_w", grid=(nj, nt),
        in_specs=[pl.BlockSpec((tt, D_MODEL), lambda j, i: (i, 0)),
                  pl.BlockSpec((tt, tn), lambda j, i: (i, j)),
                  pl.BlockSpec((tt, tn), lambda j, i: (i, j)),
                  pl.BlockSpec((halo_op, tn), lambda j, i: (after(i, halo_op), j)),
                  pl.BlockSpec((4, tn), lambda j, i: (0, j)),
                  pl.BlockSpec((tt, tn), lambda j, i: (i, j)),
                  pl.BlockSpec((HALO, tn), lambda j, i: (after(i, HALO), j))] + [HBM] * ns,
        out_specs=[pl.BlockSpec((tt, tn), lambda j, i: (i, j)), pl.BlockSpec((D_MODEL, tn), lambda j, i: (0, j)),
                   pl.BlockSpec((HALO, tn), lambda j, i: (0, j))] + [HBM] * ns,
        out_shape=[jax.ShapeDtypeStruct((t, CONV_W), MXU_DTYPE), jax.ShapeDtypeStruct((D_MODEL, CONV_W), F32),
                   jax.ShapeDtypeStruct((HALO, CONV_W), F32)] + _slab_exchange_shapes(slabbed, []),
        scratch_shapes=_slab_exchange_sems(ns),
        compiler_params=_params(("arbitrary", "arbitrary")),
    )(u, x_conv, dsilu, dsilu, w, dout, dout, *slabbed)
    return out[0], out[1], out[2], out[3:]


def _ssd_cols(g):
    b0 = SSD_WIDTH + g * SSD_STATE
    c0 = SSD_WIDTH + SSD_GROUPS * SSD_STATE + g * SSD_STATE
    return slice(g * SSD_GW, (g + 1) * SSD_GW), slice(b0, b0 + SSD_STATE), slice(c0, c0 + SSD_STATE)


def _gdn_cols(j):
    return tuple(slice(s * GDN_W + j * GDN_DK, s * GDN_W + (j + 1) * GDN_DK) for s in range(3))


def _ssd_parts(xbc_ref):
    return tuple([[xbc_ref[_chunk_rows(c), _ssd_cols(g)[s]] for g in range(SSD_GROUPS)] for c in range(SSD_CB)]
                 for s in range(3))


def _group_cols(ref):
    return [[ref[_chunk_rows(c), g * SSD_GW:(g + 1) * SSD_GW] for g in range(SSD_GROUPS)] for c in range(SSD_CB)]


def _chunk_rows(c):
    return slice(c * CHUNK, (c + 1) * CHUNK)


def _gdn_parts(qkv_ref):
    assert GDN_HB == GDN_HEADS, "the conv block is read whole: one grid step holds every head"
    return tuple([[qkv_ref[_chunk_rows(c), _gdn_cols(j)[s]] for j in range(GDN_HB)] for c in range(GDN_CB)]
                 for s in range(3))


def _head_cols(ref):
    return [[ref[_chunk_rows(c), j * GDN_DV:(j + 1) * GDN_DV] for j in range(GDN_HB)] for c in range(GDN_CB)]


def _chunk_blocks(ref, n=GDN_CB):
    return [ref[_chunk_rows(c), :] for c in range(n)]


def _first_head():
    return 0 if GDN_HB == GDN_HEADS else pl.program_id(1) * GDN_HB


def ssd_fwd(conv_ssd, proj_main, proj_small, normw, alog, dtb, dvec):
    t = conv_ssd.shape[0]
    rows = CHUNK * SSD_CB
    nc = t // rows
    groups = range(SSD_GROUPS)
    norm_cols = lambda ref: [ref[:, g * SSD_GW:(g + 1) * SSD_GW] for g in groups]

    def body(xbc_ref, z_ref, sm_ref, nw_ref, al_ref, db_ref, dv_ref, y_ref, hist_ref, state_ref):
        @pl.when(pl.program_id(0) == 0)
        def _():
            state_ref[...] = jnp.zeros(state_ref.shape, F32)

        states = [state_ref[g] for g in groups]
        for g in groups:
            hist_ref[0, g] = states[g]
        ys, new_states = ssd_chunk(*_ssd_parts(xbc_ref), _group_cols(z_ref), _chunk_blocks(sm_ref, SSD_CB),
                                   norm_cols(nw_ref), al_ref[...], db_ref[...], dv_ref[...], states)
        for c in range(SSD_CB):
            for g in groups:
                y_ref[_chunk_rows(c), g * SSD_GW:(g + 1) * SSD_GW] = ys[c][g].astype(MXU_DTYPE)
        for g in groups:
            state_ref[g] = new_states[g]

    return pl.pallas_call(
        body, name="ssd_fwd", grid=(nc,),
        in_specs=[pl.BlockSpec((rows, SSD_CONV), lambda c: (c, (COL_SSD - COL_CONV) // SSD_CONV)),
                  pl.BlockSpec((rows, SSD_WIDTH), lambda c: (c, COL_Z // SSD_WIDTH)),
                  pl.BlockSpec((rows, LANES), lambda c: (c, 0)),
                  _full((1, SSD_WIDTH)), _full((1, LANES)), _full((1, LANES)), _full((1, LANES))],
        out_specs=[pl.BlockSpec((rows, SSD_WIDTH), lambda c: (c, 0)),
                   pl.BlockSpec((1, SSD_GROUPS, SSD_STATE, SSD_GW), lambda c: (c, 0, 0, 0))],
        out_shape=[jax.ShapeDtypeStruct((t, SSD_WIDTH), MXU_DTYPE),
                   jax.ShapeDtypeStruct((nc, SSD_GROUPS, SSD_STATE, SSD_GW), F32)],
        scratch_shapes=[pltpu.VMEM((SSD_GROUPS, SSD_STATE, SSD_GW), F32)],
        compiler_params=_params(("arbitrary",)),
    )(conv_ssd, proj_main, proj_small, normw, alog, dtb, dvec)


def _accumulate(ref, first, value):
    @pl.when(first)
    def _():
        ref[...] = value

    @pl.when(jnp.logical_not(first))
    def _():
        ref[...] += value


def ssd_bwd(conv_ssd, proj_main, proj_small, normw, alog, dtb, dvec, hist, dy):
    t = conv_ssd.shape[0]
    rows = CHUNK * SSD_CB
    nc = t // rows
    rev = lambda c: nc - 1 - c
    groups = range(SSD_GROUPS)
    norm_cols = lambda ref: [ref[:, g * SSD_GW:(g + 1) * SSD_GW] for g in groups]

    def body(xbc_ref, z_ref, sm_ref, nw_ref, al_ref, db_ref, dv_ref, hist_ref, dy_ref,
             dxbc_ref, dz_ref, dsm_ref, dnw_ref, dal_ref, ddb_ref, ddv_ref, dstate_ref):
        first = pl.program_id(0) == 0

        @pl.when(first)
        def _():
            dstate_ref[...] = jnp.zeros(dstate_ref.shape, F32)

        _, vjp = jax.vjp(ssd_chunk, *_ssd_parts(xbc_ref), _group_cols(z_ref), _chunk_blocks(sm_ref, SSD_CB),
                         norm_cols(nw_ref), al_ref[...], db_ref[...], dv_ref[...], [hist_ref[0, g] for g in groups])
        dxs, dbm, dcm, dz, dsm, dnw, dal, ddb, ddv, dstate = vjp(
            (_group_cols(dy_ref), [dstate_ref[g] for g in groups]))
        for k in range(SSD_CB):
            rk = _chunk_rows(k)
            for g in groups:
                xc, bc, cc = _ssd_cols(g)
                dxbc_ref[rk, xc] = dxs[k][g]
                dxbc_ref[rk, bc] = dbm[k][g]
                dxbc_ref[rk, cc] = dcm[k][g]
                dz_ref[rk, g * SSD_GW:(g + 1) * SSD_GW] = dz[k][g].astype(dz_ref.dtype)
            dsm_ref[rk, :] = dsm[k]
        for g in groups:
            dstate_ref[g] = dstate[g]
        _accumulate(dnw_ref, first, join_lanes(dnw))
        _accumulate(dal_ref, first, dal)
        _accumulate(ddb_ref, first, ddb)
        _accumulate(ddv_ref, first, ddv)

    return pl.pallas_call(
        body, name="ssd_bwd", grid=(nc,),
        in_specs=[pl.BlockSpec((rows, SSD_CONV), lambda c: (rev(c), (COL_SSD - COL_CONV) // SSD_CONV)),
                  pl.BlockSpec((rows, SSD_WIDTH), lambda c: (rev(c), COL_Z // SSD_WIDTH)),
                  pl.BlockSpec((rows, LANES), lambda c: (rev(c), 0)),
                  _full((1, SSD_WIDTH)), _full((1, LANES)), _full((1, LANES)), _full((1, LANES)),
                  pl.BlockSpec((1, SSD_GROUPS, SSD_STATE, SSD_GW), lambda c: (rev(c), 0, 0, 0)),
                  pl.BlockSpec((rows, SSD_WIDTH), lambda c: (rev(c), 0))],
        out_specs=[pl.BlockSpec((rows, SSD_CONV), lambda c: (rev(c), (COL_SSD - COL_CONV) // SSD_CONV)),
                   pl.BlockSpec((rows, SSD_WIDTH), lambda c: (rev(c), COL_Z // SSD_WIDTH)),
                   pl.BlockSpec((rows, LANES), lambda c: (rev(c), 0)),
                   _full((1, SSD_WIDTH)), _full((1, LANES)), _full((1, LANES)), _full((1, LANES))],
        out_shape=[jax.ShapeDtypeStruct((t, CONV_W), F32), jax.ShapeDtypeStruct((t, COL_CONV), MXU_DTYPE),
                   jax.ShapeDtypeStruct((t, LANES), F32), jax.ShapeDtypeStruct((1, SSD_WIDTH), F32),
                   jax.ShapeDtypeStruct((1, LANES), F32), jax.ShapeDtypeStruct((1, LANES), F32),
                   jax.ShapeDtypeStruct((1, LANES), F32)],
        scratch_shapes=[pltpu.VMEM((SSD_GROUPS, SSD_STATE, SSD_GW), F32)],
        compiler_params=_params(("arbitrary",)),
    )(conv_ssd, proj_main, proj_small, normw, alog, dtb, dvec, hist, dy)


def gdn_fwd(conv_gdn, proj_main, proj_small, normw, alog, dtb):
    t = conv_gdn.shape[0]
    hb, cb = GDN_HB, GDN_CB
    rows = CHUNK * cb
    ns = t // rows
    gate_blk = COL_GATE // (GDN_DV * hb)

    def body(qkv_ref, gate_ref, sm_ref, nw_ref, al_ref, db_ref, y_ref, hist_ref, t_ref, state_ref):
        h0 = _first_head()

        @pl.when(pl.program_id(0) == 0)
        def _():
            for j in range(hb):
                state_ref[h0 + j] = jnp.zeros((GDN_DK, GDN_DV), F32)

        states = [state_ref[h0 + j] for j in range(hb)]
        for j in range(hb):
            hist_ref[0, j] = states[j]
        qs, ks, vs = _gdn_parts(qkv_ref)
        ys, new_states, ts = gdn_chunk(h0, qs, ks, vs, _chunk_blocks(sm_ref), _head_cols(gate_ref), nw_ref[...],
                                       al_ref[...], db_ref[...], states)
        for c in range(cb):
            for j in range(hb):
                y_ref[_chunk_rows(c), j * GDN_DV:(j + 1) * GDN_DV] = ys[c][j].astype(MXU_DTYPE)
        for j in range(hb):
            state_ref[h0 + j] = new_states[j]
        for p in range(cb * hb // 2):
            t_ref[0, p] = ts[p]

    return pl.pallas_call(
        body, name="gdn_fwd", grid=(ns, GDN_HEADS // hb),
        in_specs=[pl.BlockSpec((rows, GDN_HC * hb), lambda c, h: (c, h)),
                  pl.BlockSpec((rows, GDN_DV * hb), lambda c, h: (c, gate_blk + h)),
                  pl.BlockSpec((rows, LANES), lambda c, h: (c, 0)),
                  _full((1, GDN_DV)), _full((1, LANES)), _full((1, LANES))],
        out_specs=[pl.BlockSpec((rows, GDN_DV * hb), lambda c, h: (c, h)),
                   pl.BlockSpec((1, hb, GDN_DK, GDN_DV), lambda c, h: (c, h, 0, 0)),
                   pl.BlockSpec((1, cb * hb // 2, CHUNK, LANES), lambda c, h: (c, h, 0, 0))],
        out_shape=[jax.ShapeDtypeStruct((t, GDN_W), MXU_DTYPE),
                   jax.ShapeDtypeStruct((ns, GDN_HEADS, GDN_DK, GDN_DV), F32),
                   jax.ShapeDtypeStruct((ns, cb * GDN_HEADS // 2, CHUNK, LANES), F32)],
        scratch_shapes=[pltpu.VMEM((GDN_HEADS, GDN_DK, GDN_DV), F32)],
        compiler_params=_params(("arbitrary", "arbitrary")),
    )(conv_gdn, proj_main, proj_small, normw, alog, dtb)


def gdn_bwd(dproj_main, dconv, conv_gdn, proj_main, proj_small, normw, alog, dtb, hist, t_inv, dy):
    t = conv_gdn.shape[0]
    hb, cb = GDN_HB, GDN_CB
    rows = CHUNK * cb
    ns = t // rows
    rev = lambda c: ns - 1 - c
    gate_blk = COL_GATE // (GDN_DV * hb)

    def body(alias_ref, alias2_ref, qkv_ref, gate_ref, sm_ref, nw_ref, al_ref, db_ref, hist_ref, t_ref, dy_ref,
             dgate_ref, dqkv_ref, dsm_ref, dnw_ref, dal_ref, ddb_ref, dstate_ref):
        del alias_ref, alias2_ref
        c, h = pl.program_id(0), pl.program_id(1)
        h0 = _first_head()

        @pl.when(c == 0)
        def _():
            for j in range(hb):
                dstate_ref[h0 + j] = jnp.zeros((GDN_DK, GDN_DV), F32)

        saved = [t_ref[0, p] for p in range(cb * hb // 2)]

        def fn(qs, ks, vs, smalls, gates, nw, al, db, states):
            return gdn_chunk(h0, qs, ks, vs, smalls, gates, nw, al, db, states, saved)[:2]

        qs, ks, vs = _gdn_parts(qkv_ref)
        _, vjp = jax.vjp(fn, qs, ks, vs, _chunk_blocks(sm_ref), _head_cols(gate_ref), nw_ref[...], al_ref[...],
                         db_ref[...], [hist_ref[0, j] for j in range(hb)])
        dqs, dks, dvs, dsm, dgates, dnw, dal, ddb, dstates = vjp(
            (_head_cols(dy_ref), [dstate_ref[h0 + j] for j in range(hb)]))
        for k in range(cb):
            rk = _chunk_rows(k)
            for j in range(hb):
                qc, kc, vc = _gdn_cols(j)
                dqkv_ref[rk, qc] = dqs[k][j]
                dqkv_ref[rk, kc] = dks[k][j]
                dqkv_ref[rk, vc] = dvs[k][j]
                dgate_ref[rk, j * GDN_DV:(j + 1) * GDN_DV] = dgates[k][j].astype(dgate_ref.dtype)
        for j in range(hb):
            dstate_ref[h0 + j] = dstates[j]
        _accumulate(dsm_ref, h == 0, jnp.concatenate(dsm, axis=0))
        first = jnp.logical_and(c == 0, h == 0)
        _accumulate(dnw_ref, first, dnw)
        _accumulate(dal_ref, first, dal)
        _accumulate(ddb_ref, first, ddb)

    return pl.pallas_call(
        body, name="gdn_bwd", grid=(ns, GDN_HEADS // hb),
        in_specs=[ANY, ANY, pl.BlockSpec((rows, GDN_HC * hb), lambda c, h: (rev(c), h)),
                  pl.BlockSpec((rows, GDN_DV * hb), lambda c, h: (rev(c), gate_blk + h)),
                  pl.BlockSpec((rows, LANES), lambda c, h: (rev(c), 0)),
                  _full((1, GDN_DV)), _full((1, LANES)), _full((1, LANES)),
                  pl.BlockSpec((1, hb, GDN_DK, GDN_DV), lambda c, h: (rev(c), h, 0, 0)),
                  pl.BlockSpec((1, cb * hb // 2, CHUNK, LANES), lambda c, h: (rev(c), h, 0, 0)),
                  pl.BlockSpec((rows, GDN_DV * hb), lambda c, h: (rev(c), h))],
        out_specs=[pl.BlockSpec((rows, GDN_DV * hb), lambda c, h: (rev(c), gate_blk + h)),
                   pl.BlockSpec((rows, GDN_HC * hb), lambda c, h: (rev(c), h)),
                   pl.BlockSpec((rows, LANES), lambda c, h: (rev(c), 0)),
                   _full((1, GDN_DV)), _full((1, LANES)), _full((1, LANES))],
        out_shape=[jax.ShapeDtypeStruct(dproj_main.shape, dproj_main.dtype),
                   jax.ShapeDtypeStruct(dconv.shape, dconv.dtype),
                   jax.ShapeDtypeStruct((t, LANES), F32), jax.ShapeDtypeStruct((1, GDN_DV), F32),
                   jax.ShapeDtypeStruct((1, LANES), F32), jax.ShapeDtypeStruct((1, LANES), F32)],
        scratch_shapes=[pltpu.VMEM((GDN_HEADS, GDN_DK, GDN_DV), F32)],
        input_output_aliases={0: 0, 1: 1},
        compiler_params=_params(("arbitrary", "arbitrary")),
    )(dproj_main, dconv, conv_gdn, proj_main, proj_small, normw, alog, dtb, hist, t_inv, dy)


def out_proj_loss(x, y_ssd, y_gdn, w_out, final_w, target):
    t = x.shape[0]
    tm = min(512, t)

    def body(x_ref, ys_ref, yg_ref, wo_ref, fw_ref, tg_ref, loss_ref, dhid_ref, dys_ref, dyg_ref, dwo_ref, dfw_ref):
        i = pl.program_id(0)
        ys, yg = ys_ref[...], yg_ref[...]
        wo_s, wo_g = wo_ref[:SSD_WIDTH, :], wo_ref[SSD_WIDTH:, :]
        hid = x_ref[...] + _raw_dot(ys, wo_s, 1, 0) + _raw_dot(yg, wo_g, 1, 0)
        out, vjp = jax.vjp(rmsnorm, hid, fw_ref[...])
        err = out - tg_ref[...]
        loss = 0.5 * jnp.sum(jnp.mean(err * err, axis=-1, keepdims=True), axis=0, keepdims=True)
        dhid, dfw = vjp(err * (1.0 / D_MODEL))
        dhid_ref[...] = dhid
        dys_ref[...] = _raw_dot(dhid, wo_s, 1, 1)
        dyg_ref[...] = _raw_dot(dhid, wo_g, 1, 1)
        first = i == 0
        _accumulate(loss_ref, first, jnp.broadcast_to(loss, loss_ref.shape))
        _accumulate(dfw_ref, first, dfw)

        @pl.when(first)
        def _():
            dwo_ref[:SSD_WIDTH, :] = _raw_dot(ys, dhid, 0, 0)
            dwo_ref[SSD_WIDTH:, :] = _raw_dot(yg, dhid, 0, 0)

        @pl.when(i > 0)
        def _():
            dwo_ref[:SSD_WIDTH, :] += _raw_dot(ys, dhid, 0, 0)
            dwo_ref[SSD_WIDTH:, :] += _raw_dot(yg, dhid, 0, 0)

    row = lambda w: pl.BlockSpec((tm, w), lambda i: (i, 0))
    return pl.pallas_call(
        body, name="out_proj_loss", grid=(t // tm,),
        in_specs=[row(D_MODEL), row(SSD_WIDTH), row(GDN_W), _full((SSD_WIDTH + GDN_W, D_MODEL)), _full((1, D_MODEL)),
                  row(D_MODEL)],
        out_specs=[_full((8, LANES)), row(D_MODEL), row(SSD_WIDTH), row(GDN_W), _full((SSD_WIDTH + GDN_W, D_MODEL)),
                   _full((1, D_MODEL))],
        out_shape=[jax.ShapeDtypeStruct((8, LANES), F32), jax.ShapeDtypeStruct((t, D_MODEL), F32),
                   jax.ShapeDtypeStruct((t, SSD_WIDTH), F32), jax.ShapeDtypeStruct((t, GDN_W), F32),
                   jax.ShapeDtypeStruct((SSD_WIDTH + GDN_W, D_MODEL), F32), jax.ShapeDtypeStruct((1, D_MODEL), F32)],
        compiler_params=_params(("arbitrary",)),
    )(x, y_ssd, y_gdn, w_out, final_w, target)


def in_proj_bwd_x(x, normw, w_main, w_small, dproj_main, dproj_conv, dsmall_a, dsmall_b, dhid, slabbed):
    t = x.shape[0]
    tm = min(256, t)
    ni = t // tm
    ns = len(slabbed)

    def body(x_ref, nw_ref, wm_ref, ws_ref, dp_ref, dc_ref, da_ref, db_ref, dh_ref, *rest):
        slab_refs, (gx_ref, dnw_ref), land_refs = rest[:ns], rest[ns:ns + 2], rest[ns + 2:2 * ns + 2]
        sems = rest[2 * ns + 2:]
        i = pl.program_id(0)
        start, finish = _slab_exchange(slab_refs, land_refs, ns, *sems)

        @pl.when(i == 0)
        def _():
            start()

        du = (_raw_dot(dp_ref[...], wm_ref[:, :COL_CONV], 1, 1) + _raw_dot(dc_ref[...], wm_ref[:, COL_CONV:], 1, 1)
              + _raw_dot(da_ref[...] + db_ref[...], ws_ref[...], 1, 1))
        _, vjp = jax.vjp(rmsnorm, x_ref[...], nw_ref[...])
        dx, dnw = vjp(du)
        gx_ref[...] = dx + dh_ref[...]
        _accumulate(dnw_ref, i == 0, dnw)

        @pl.when(i == ni - 1)
        def _():
            finish()

    row = lambda w: pl.BlockSpec((tm, w), lambda i: (i, 0))
    out = pl.pallas_call(
        body, name="in_proj_bwd_x", grid=(ni,),
        in_specs=[row(D_MODEL), _full((1, D_MODEL)), _full((D_MODEL, MAIN)), _full((D_MODEL, LANES)), row(COL_CONV),
                  row(CONV_W), row(LANES), row(LANES), row(D_MODEL)] + [HBM] * ns,
        out_specs=[row(D_MODEL), _full((1, D_MODEL))] + [HBM] * ns,
        out_shape=[jax.ShapeDtypeStruct((t, D_MODEL), F32), jax.ShapeDtypeStruct((1, D_MODEL), F32)]
        + _slab_exchange_shapes(slabbed, []),
        scratch_shapes=_slab_exchange_sems(ns),
        compiler_params=_params(("arbitrary",)),
    )(x, normw, w_main, w_small, dproj_main, dproj_conv, dsmall_a, dsmall_b, dhid, *slabbed)
    return out[0], out[1], out[2:]


def in_proj_bwd_w(u, dproj_main, dsmall_a, dsmall_b):
    t = u.shape[0]
    tm, tn = min(1024, t), COL_CONV // 2

    def body(u_ref, dp_ref, da_ref, db_ref, dwm_ref, dws_ref):
        j, i = pl.program_id(0), pl.program_id(1)
        uu = u_ref[...]
        _accumulate(dwm_ref, i == 0, _raw_dot(uu, dp_ref[...], 0, 0))

        @pl.when(j == 0)
        def _():
            _accumulate(dws_ref, i == 0, _raw_dot(uu, da_ref[...] + db_ref[...], 0, 0))

    return pl.pallas_call(
        body, name="in_proj_bwd_w", grid=(COL_CONV // tn, t // tm),
        in_specs=[pl.BlockSpec((tm, D_MODEL), lambda j, i: (i, 0)), pl.BlockSpec((tm, tn), lambda j, i: (i, j)),
                  pl.BlockSpec((tm, LANES), lambda j, i: (i, 0)), pl.BlockSpec((tm, LANES), lambda j, i: (i, 0))],
        out_specs=[pl.BlockSpec((D_MODEL, tn), lambda j, i: (0, j)), _full((D_MODEL, LANES))],
        out_shape=[jax.ShapeDtypeStruct((D_MODEL, COL_CONV), F32), jax.ShapeDtypeStruct((D_MODEL, LANES), F32)],
        compiler_params=_params(("arbitrary", "arbitrary")),
    )(u, dproj_main, dsmall_a, dsmall_b)


def sum_slabs(a, name):
    n, rows, cols = a.shape
    tr = 64 if rows % 64 == 0 else rows

    def body(a_ref, o_ref):
        acc = a_ref[0].astype(F32)
        for d in range(1, n):
            acc = acc + a_ref[d].astype(F32)
        o_ref[...] = acc

    return pl.pallas_call(
        body, name=name, grid=(rows // tr,),
        in_specs=[pl.BlockSpec((n, tr, cols), lambda i: (0, i, 0))],
        out_specs=pl.BlockSpec((tr, cols), lambda i: (i, 0)),
        out_shape=jax.ShapeDtypeStruct((rows, cols), F32),
        compiler_params=_params(("arbitrary",)),
    )(a)


def adamw(w, g, m, v, name):
    _, rows, cols = w.shape
    tr = 128 if rows % 128 == 0 else rows

    def body(w_ref, g_ref, m_ref, v_ref, d_ref, nm_ref, nv_ref):
        gg = g_ref[...]
        nm = ADAM_B1 * m_ref[...] + (1.0 - ADAM_B1) * gg
        nv = ADAM_B2 * v_ref[...] + (1.0 - ADAM_B2) * (gg * gg)
        m_hat = nm / (1.0 - ADAM_B1 ** ADAM_STEP)
        v_hat = nv / (1.0 - ADAM_B2 ** ADAM_STEP)
        d_ref[...] = -ADAM_LR * (m_hat / (jnp.sqrt(v_hat) + ADAM_EPS) + ADAM_WD * w_ref[...])
        nm_ref[...] = nm
        nv_ref[...] = nv

    spec = pl.BlockSpec((1, tr, cols), lambda i: (0, i, 0))
    shp = jax.ShapeDtypeStruct((1, rows, cols), F32)
    return pl.pallas_call(
        body, name=name, grid=(rows // tr,), in_specs=[spec] * 4, out_specs=[spec] * 3, out_shape=[shp] * 3,
        compiler_params=_params(("arbitrary",)),
    )(w, g.reshape(w.shape), m, v)


def _my_place():
    return lax.axis_index("x"), lax.axis_index("y"), lax.axis_index("c")


def gather_weights(big, small):
    nb, n = len(big), len(big) + len(small)
    parts = 4

    def body(*refs):
        srcs, outs = refs[:n], refs[n:2 * n]
        land_a, land_b = refs[2 * n:2 * n + nb], refs[2 * n + nb:2 * n + 2 * nb]
        send_sems, recv_sems, fwd_send, fwd_recv, local_sems = refs[2 * n + 2 * nb:]
        x, y, c = _my_place()
        me = 2 * x + y
        chips = [(1 - x, y), (x, 1 - y), (1 - x, 1 - y)]
        half = [a.shape[0] // 2 for a in big]

        def ici(j, i):
            px, py = chips[j]
            if i < nb:
                src, dst = srcs[i].at[pl.ds(c * half[i], half[i])], land_a[i].at[j]
            else:
                src, dst = srcs[i], outs[i].at[me]
            return pltpu.make_async_remote_copy(src_ref=src, dst_ref=dst, send_sem=send_sems.at[j * n + i],
                                                recv_sem=recv_sems.at[j * n + i], device_id=(px, py, c),
                                                device_id_type=MESH)

        def ici_arrival(j, i):
            px, py = chips[j]
            dst = land_a[i].at[j] if i < nb else outs[i].at[2 * px + py]
            return pltpu.make_async_remote_copy(src_ref=dst, dst_ref=dst, send_sem=send_sems.at[j * n + i],
                                                recv_sem=recv_sems.at[j * n + i], device_id=(px, py, c),
                                                device_id_type=MESH)

        def forward(j, i, p):
            rows = half[i] // parts
            k = (j * nb + i) * parts + p
            return pltpu.make_async_remote_copy(
                src_ref=land_a[i].at[j, pl.ds(p * rows, rows)], dst_ref=land_b[i].at[j, pl.ds(p * rows, rows)],
                send_sem=fwd_send.at[k], recv_sem=fwd_recv.at[k], device_id=(x, y, 1 - c), device_id_type=MESH)

        def store(j, i, from_sibling):
            px, py = chips[j]
            buf, h = (land_b, 1 - c) if from_sibling else (land_a, c)
            k = n + (j * nb + i) * 2 + (1 if from_sibling else 0)
            return pltpu.make_async_copy(buf[i].at[j], outs[i].at[2 * px + py, pl.ds(h * half[i], half[i])],
                                         local_sems.at[k])

        own = [pltpu.make_async_copy(srcs[i], outs[i].at[me], local_sems.at[i]) for i in range(n)]
        sends = [ici(j, i) for j in range(3) for i in range(n)]
        for cp in own + sends:
            cp.start()
        pending = []
        for j in range(3):
            for i in range(n):
                ici_arrival(j, i).wait_recv()
                if i < nb:
                    fw = [forward(j, i, p) for p in range(parts)]
                    st = store(j, i, False)
                    for cp in fw + [st]:
                        cp.start()
                    pending += [cp.wait_send for cp in fw] + [st.wait]
        for j in range(3):
            for i in range(nb):
                for p in range(parts):
                    forward(j, i, p).wait_recv()
                st = store(j, i, True)
                st.start()
                pending.append(st.wait)
        for cp in sends:
            cp.wait_send()
        for wait in pending:
            wait()
        for cp in own:
            cp.wait()

    shards = list(big) + list(small)
    lands = [pltpu.VMEM((3, a.shape[0] // 2) + a.shape[1:], a.dtype) for a in big]
    return pl.pallas_call(
        body, name="gather_weights",
        in_specs=[HBM] * n, out_specs=[HBM] * n,
        out_shape=[jax.ShapeDtypeStruct((N_CHIP,) + s.shape, s.dtype) for s in shards],
        scratch_shapes=lands + lands + [
            pltpu.SemaphoreType.DMA((3 * n,)), pltpu.SemaphoreType.DMA((3 * n,)),
            pltpu.SemaphoreType.DMA((3 * nb * parts,)), pltpu.SemaphoreType.DMA((3 * nb * parts,)),
            pltpu.SemaphoreType.DMA((n + 6 * nb,))],
        compiler_params=pltpu.CompilerParams(vmem_limit_bytes=VMEM_LIMIT),
    )(*shards)


def _peer(x, y, c, mask):
    mx, my, mc = (mask >> 2) & 1, (mask >> 1) & 1, mask & 1
    return (x ^ mx if mx else x, y ^ my if my else y, c ^ mc if mc else c)


def _slab_exchange_shapes(slabbed, replicated):
    return ([jax.ShapeDtypeStruct(a.shape, a.dtype) for a in slabbed]
            + [jax.ShapeDtypeStruct((N_DEV,) + a.shape, a.dtype) for a in replicated])


def _slab_exchange_sems(n):
    return [pltpu.SemaphoreType.DMA((7 * n,)), pltpu.SemaphoreType.DMA((7 * n,)), pltpu.SemaphoreType.DMA((n,))]


def _slab_exchange(srcs, outs, ns, send_sems, recv_sems, local_sems):
    n = len(srcs)
    x, y, c = _my_place()
    me = 4 * x + 2 * y + c

    def piece(i, dev):
        return srcs[i].at[dev] if i < ns else srcs[i]

    def copies(arriving):
        out = []
        for mask in range(1, N_DEV):
            px, py, pc = _peer(x, y, c, mask)
            dev = 4 * px + 2 * py + pc
            for i in range(n):
                k = (mask - 1) * n + i
                out.append(pltpu.make_async_remote_copy(
                    src_ref=piece(i, dev), dst_ref=outs[i].at[dev if arriving else me], send_sem=send_sems.at[k],
                    recv_sem=recv_sems.at[k], device_id=(px, py, pc), device_id_type=MESH))
        return out

    def local():
        return [pltpu.make_async_copy(piece(i, me), outs[i].at[me], local_sems.at[i]) for i in range(n)]

    def start():
        for cp in local() + copies(False):
            cp.start()

    def finish():
        for cp in copies(True):
            cp.wait_recv()
        for cp in copies(False):
            cp.wait_send()
        for cp in local():
            cp.wait()

    return start, finish


def exchange_halves(landed, replicated):
    n, nr = len(landed), len(replicated)
    streams = 8
    halves = [jax.ShapeDtypeStruct(a.shape[1:], F32) for a in landed]
    sum_rows = 64

    def body(*refs):
        srcs, rep_srcs, outs, rep_outs = refs[:n], refs[n:n + nr], refs[n + nr:2 * n + nr], refs[2 * n + nr:2 * (n + nr)]
        refs = refs[2 * (n + nr):]
        slabs, mine, theirs = refs[:n], refs[n:2 * n], refs[2 * n:3 * n]
        send_sems, recv_sems, in_sems, out_sems = refs[3 * n:3 * n + 4]
        rep_start, rep_finish = _slab_exchange(rep_srcs, rep_outs, 0, *refs[3 * n + 4:])
        rep_start()
        x, y, c = _my_place()
        loads = [pltpu.make_async_copy(srcs[i], slabs[i], in_sems.at[i]) for i in range(n)]
        for cp in loads:
            cp.start()
        for i in range(n):
            loads[i].wait()
            for r in range(0, halves[i].shape[0], sum_rows):
                rows = pl.ds(r, sum_rows)
                acc = slabs[i][0, rows, :].astype(F32)
                for d in range(1, N_DEV):
                    acc = acc + slabs[i][d, rows, :].astype(F32)
                mine[i][rows, :] = acc

        def chunk_copy(i, s):
            rows = halves[i].shape[0] // streams
            k = i * streams + s
            return pltpu.make_async_remote_copy(
                src_ref=mine[i].at[pl.ds(s * rows, rows)], dst_ref=theirs[i].at[pl.ds(s * rows, rows)],
                send_sem=send_sems.at[k], recv_sem=recv_sems.at[k], device_id=(x, y, 1 - c), device_id_type=MESH)

        sends = [chunk_copy(i, s) for i in range(n) for s in range(streams)]
        for cp in sends:
            cp.start()
        own = [pltpu.make_async_copy(mine[i], outs[i].at[c], out_sems.at[i]) for i in range(n)]
        for cp in own:
            cp.start()
        for cp in sends:
            cp.wait_recv()
        got = [pltpu.make_async_copy(theirs[i], outs[i].at[1 - c], out_sems.at[n + i]) for i in range(n)]
        for cp in got:
            cp.start()
        for cp in sends:
            cp.wait_send()
        for cp in own + got:
            cp.wait()
        rep_finish()

    vmem = [pltpu.VMEM(a.shape, a.dtype) for a in halves]
    out = pl.pallas_call(
        body, name="exchange_halves",
        in_specs=[HBM] * (n + nr), out_specs=[HBM] * (n + nr),
        out_shape=[jax.ShapeDtypeStruct((2,) + a.shape, a.dtype) for a in halves]
        + _slab_exchange_shapes([], replicated),
        scratch_shapes=[pltpu.VMEM(a.shape, a.dtype) for a in landed] + vmem + vmem
        + [pltpu.SemaphoreType.DMA((n * streams,)), pltpu.SemaphoreType.DMA((n * streams,)),
           pltpu.SemaphoreType.DMA((n,)), pltpu.SemaphoreType.DMA((2 * n,))] + _slab_exchange_sems(nr),
        compiler_params=pltpu.CompilerParams(vmem_limit_bytes=VMEM_LIMIT),
    )(*landed, *replicated)
    return out[:n], out[n:]


def _pack_cols(pieces):
    offs, pos = [], 0
    for a in pieces:
        offs.append(pos)
        pos += a.shape[1]
    rows8 = [jnp.pad(a.astype(F32), ((0, 8 - a.shape[0]), (0, 0))) for a in pieces]
    return jnp.concatenate(rows8, axis=1), offs


def adamw_many(ws, gs, ms, vs):
    n = len(ws)

    def body(*refs):
        w_r, g_r, m_r, v_r = refs[:n], refs[n:2 * n], refs[2 * n:3 * n], refs[3 * n:4 * n]
        d_o, m_o, v_o = refs[4 * n:5 * n], refs[5 * n:6 * n], refs[6 * n:7 * n]
        for i in range(n):
            gg = g_r[i][...]
            nm = ADAM_B1 * m_r[i][...] + (1.0 - ADAM_B1) * gg
            nv = ADAM_B2 * v_r[i][...] + (1.0 - ADAM_B2) * (gg * gg)
            m_hat = nm / (1.0 - ADAM_B1 ** ADAM_STEP)
            v_hat = nv / (1.0 - ADAM_B2 ** ADAM_STEP)
            d_o[i][...] = -ADAM_LR * (m_hat / (jnp.sqrt(v_hat) + ADAM_EPS) + ADAM_WD * w_r[i][...])
            m_o[i][...] = nm
            v_o[i][...] = nv

    shapes = [jax.ShapeDtypeStruct(w.shape, F32) for w in ws]
    out = pl.pallas_call(body, name="adamw_small", out_shape=shapes * 3,
                         compiler_params=pltpu.CompilerParams(vmem_limit_bytes=VMEM_LIMIT))(*ws, *gs, *ms, *vs)
    return out[:n], out[n:2 * n], out[2 * n:]


def _lanes(vec, start):
    n = vec.shape[-1]
    return jnp.pad(vec.reshape(1, n).astype(F32), ((0, 0), (start, LANES - start - n)))


def kernel(x, norm_w, w_in, ssd_conv_w, ssd_conv_b, ssd_dt_bias, ssd_a_log, ssd_d, ssd_norm_w, gdn_conv_w, gdn_dt_bias, gdn_a_log, gdn_norm_w, w_out, final_norm_w, loss_target, m_norm_w, m_w_in, m_ssd_conv_w, m_ssd_conv_b, m_ssd_dt_bias, m_ssd_a_log, m_ssd_d, m_ssd_norm_w, m_gdn_conv_w, m_gdn_dt_bias, m_gdn_a_log, m_gdn_norm_w, m_w_out, m_final_norm_w, v_norm_w, v_w_in, v_ssd_conv_w, v_ssd_conv_b, v_ssd_dt_bias, v_ssd_a_log, v_ssd_d, v_ssd_norm_w, v_gdn_conv_w, v_gdn_dt_bias, v_gdn_a_log, v_gdn_norm_w, v_w_out, v_final_norm_w):
    xs = x[0]
    target = loss_target[0]
    chip = 2 * lax.axis_index("x") + lax.axis_index("y")
    w_in_shard, w_out_shard = w_in[0], w_out[0]
    in_cols = w_in_shard.shape[1]
    out_rows = w_out_shard.shape[0]

    g_in, g_out, g_cs, g_cg = gather_weights(
        [w_in_shard.astype(MXU_DTYPE), w_out_shard.astype(MXU_DTYPE)], [ssd_conv_w[0], gdn_conv_w[0]])
    w_in_full = jnp.concatenate([g_in[k] for k in range(N_CHIP)], axis=1)
    w_out_full = g_out.reshape(N_CHIP * out_rows, D_MODEL)
    cw_ssd = jnp.concatenate([g_cs[k] for k in range(N_CHIP)], axis=1)
    cw_gdn = jnp.concatenate([g_cg[k] for k in range(N_CHIP)], axis=1)
    cb_ssd, cb_gdn = ssd_conv_b, jnp.zeros((1, GDN_CONV), F32)
    o_xbc, o_dt, o_gate, o_qkv, o_ab = 1024, 2560, 2576, 3600, 6672
    w_main = jnp.concatenate([w_in_full[:, :o_xbc], w_in_full[:, o_gate:o_qkv], w_in_full[:, o_qkv:o_ab],
                              w_in_full[:, o_xbc:o_dt]], axis=1)
    w_small = jnp.concatenate([w_in_full[:, o_dt:o_gate], w_in_full[:, o_ab:],
                               jnp.zeros((D_MODEL, LANES - 32), MXU_DTYPE)], axis=1)
    alog = _lanes(ssd_a_log, 0) + _lanes(gdn_a_log, LANE_GA)
    dtb = _lanes(ssd_dt_bias, 0) + _lanes(gdn_dt_bias, LANE_GA)
    dvec = _lanes(ssd_d, 0)
    fw = final_norm_w.reshape(1, D_MODEL)

    cw, cb = jnp.concatenate([cw_gdn, cw_ssd], axis=1), jnp.concatenate([cb_gdn, cb_ssd], axis=1)
    proj_main, proj_small, u = in_proj(xs, norm_w, w_main, w_small)
    conv_out, x_conv, dsilu_conv = in_proj_conv(u, w_main, cw, cb)
    y_ssd, hist_ssd = ssd_fwd(conv_out, proj_main, proj_small, ssd_norm_w, alog, dtb, dvec)
    y_gdn, hist_gdn, tinv_gdn = gdn_fwd(conv_out, proj_main, proj_small, gdn_norm_w, alog, dtb)

    loss_blk, dhid, dy_ssd, dy_gdn, d_w_out, d_fw = out_proj_loss(xs, y_ssd, y_gdn, w_out_full, fw, target)
    dconv, dproj_main, dsmall_ssd, d_ssd_nw, d_alog_s, d_dtb_s, d_dvec = ssd_bwd(
        conv_out, proj_main, proj_small, ssd_norm_w, alog, dtb, dvec, hist_ssd, dy_ssd)
    dproj_main, dconv, dsmall_gdn, d_gdn_nw, d_alog_g, d_dtb_g = gdn_bwd(
        dproj_main, dconv, conv_out, proj_main, proj_small, gdn_norm_w, alog, dtb, hist_gdn, tinv_gdn, dy_gdn)
    slabs_out = d_w_out.reshape(N_DEV, out_rows // 2, D_MODEL).astype(COMM_DTYPE)
    dproj_conv, d_w_conv, dwb, (r_out,) = conv_bwd_w(u, x_conv, dsilu_conv, cw, dconv, [slabs_out])
    dwb_gdn, dwb_ssd = dwb[:, :GDN_CONV], dwb[:, GDN_CONV:]
    d_w_zg, d_w_small = in_proj_bwd_w(u, dproj_main, dsmall_ssd, dsmall_gdn)
    order = [(d_w_zg, 0, COL_GATE), (d_w_conv, COL_SSD - COL_CONV, CONV_W), (d_w_small, 0, 16),
             (d_w_zg, COL_GATE, COL_CONV), (d_w_conv, 0, COL_SSD - COL_CONV), (d_w_small, 16, 32)]
    shards, pos = [[] for _ in range(N_CHIP)], 0
    for src, lo, hi in order:
        while lo < hi:
            k = pos // in_cols
            n = min(hi - lo, (k + 1) * in_cols - pos)
            shards[k].append(src[:, lo:lo + n].astype(COMM_DTYPE))
            lo, pos = lo + n, pos + n
    slabs_in = jnp.stack([jnp.concatenate(p, axis=1) for p in shards]).reshape(N_DEV, D_MODEL // 2, in_cols)
    grad_x, d_norm_w, (r_in,) = in_proj_bwd_x(xs, norm_w, w_main, w_small, dproj_main, dproj_conv, dsmall_ssd,
                                               dsmall_gdn, dhid, [slabs_in])
    d_alog, d_dtb = d_alog_s + d_alog_g, d_dtb_s + d_dtb_g
    packed, (o_nw, o_cs, o_cg, o_snw, o_fw, o_al, o_db, o_dv, o_gnw, o_loss) = _pack_cols([
        d_norm_w, dwb_ssd, dwb_gdn,
        d_ssd_nw.reshape(1, SSD_WIDTH), d_fw, d_alog, d_dtb, d_dvec, d_gdn_nw, loss_blk])

    (full_in, full_out), (r_small,) = exchange_halves([r_in, r_out], [packed])
    tot = sum_slabs(r_small, "sum_small")
    grad_w_in = full_in.reshape(D_MODEL, in_cols)
    grad_w_out = full_out.reshape(out_rows, D_MODEL)
    loss = tot[0, o_loss]
    sc, gc = ssd_conv_w.shape[2], gdn_conv_w.shape[2]
    row = lambda off, n, r=0: tot[r:r + 1, off:off + n]
    gs = [row(o_nw, D_MODEL),
          lax.dynamic_slice(tot, (0, o_cs + chip * sc), (4, sc)),
          row(o_cs, SSD_CONV, 4),
          row(o_db, SSD_HEADS), row(o_al, SSD_HEADS), row(o_dv, SSD_HEADS),
          row(o_snw, SSD_WIDTH),
          lax.dynamic_slice(tot, (0, o_cg + chip * gc), (4, gc)),
          row(o_db + LANE_GA, GDN_HEADS), row(o_al + LANE_GA, GDN_HEADS),
          row(o_gnw, GDN_DV), row(o_fw, D_MODEL)]

    names = ["norm_w", "ssd_conv_w", "ssd_conv_b", "ssd_dt_bias", "ssd_a_log", "ssd_d", "ssd_norm_w", "gdn_conv_w",
             "gdn_dt_bias", "gdn_a_log", "gdn_norm_w", "final_norm_w"]
    ws = [norm_w, ssd_conv_w, ssd_conv_b, ssd_dt_bias, ssd_a_log, ssd_d, ssd_norm_w, gdn_conv_w, gdn_dt_bias,
          gdn_a_log, gdn_norm_w, final_norm_w]
    ms = [m_norm_w, m_ssd_conv_w, m_ssd_conv_b, m_ssd_dt_bias, m_ssd_a_log, m_ssd_d, m_ssd_norm_w, m_gdn_conv_w,
          m_gdn_dt_bias, m_gdn_a_log, m_gdn_norm_w, m_final_norm_w]
    vs = [v_norm_w, v_ssd_conv_w, v_ssd_conv_b, v_ssd_dt_bias, v_ssd_a_log, v_ssd_d, v_ssd_norm_w, v_gdn_conv_w,
          v_gdn_dt_bias, v_gdn_a_log, v_gdn_norm_w, v_final_norm_w]
    shapes = [w.shape for w in ws]
    flat = lambda arrs: [a.reshape(g.shape) for a, g in zip(arrs, gs)]
    d_s, m_s, v_s = adamw_many(flat(ws), gs, flat(ms), flat(vs))
    back = lambda arrs: dict(zip(names, [a.reshape(s) for a, s in zip(arrs, shapes)]))
    delta, new_m, new_v, grads = back(d_s), back(m_s), back(v_s), back(gs)
    d_in, m_in, v_in = adamw(w_in, grad_w_in, m_w_in, v_w_in, "adamw_w_in")
    d_out, m_out, v_out = adamw(w_out, grad_w_out, m_w_out, v_w_out, "adamw_w_out")
    for tbl, a_in, a_out in ((grads, grad_w_in[None], grad_w_out[None]), (delta, d_in, d_out), (new_m, m_in, m_out),
                             (new_v, v_in, v_out)):
        tbl["w_in"] = a_in
        tbl["w_out"] = a_out

    order = ["norm_w", "w_in", "ssd_conv_w", "ssd_conv_b", "ssd_dt_bias", "ssd_a_log", "ssd_d", "ssd_norm_w",
             "gdn_conv_w", "gdn_dt_bias", "gdn_a_log", "gdn_norm_w", "w_out", "final_norm_w"]
    return (loss.reshape(()), grad_x[None], *[grads[k] for k in order], *[delta[k] for k in order],
            *[new_m[k] for k in order], *[new_v[k] for k in order])
```

```python
import functools

import jax
import jax.numpy as jnp
from jax import lax
from jax.experimental import pallas as pl
from jax.experimental.pallas import tpu as pltpu

F32 = jnp.float32
MXU_DTYPE = jnp.bfloat16
COMM_DTYPE = jnp.bfloat16
MESH = pl.DeviceIdType.MESH

D_MODEL = 1024
CHUNK = 64
EPS = 1e-6
SSD_HEADS, SSD_GROUPS, SSD_STATE = 16, 2, 128
SSD_WIDTH, SSD_CONV = 1024, 1536
SSD_GW = SSD_WIDTH // SSD_GROUPS
GDN_HEADS, GDN_DK, GDN_DV = 8, 128, 128
GDN_W, GDN_CONV = 1024, 3072
GDN_HC = 2 * GDN_DK + GDN_DV
IN_DIM = 6688
MAIN = 6656
LANES = 128
COL_Z, COL_GATE, COL_GDN, COL_SSD = 0, 1024, 2048, 5120
COL_CONV = COL_GDN
CONV_W = MAIN - COL_CONV
GDN_HB = 8
GDN_CB = 4
SSD_CB = 4
LANE_GA, LANE_GB = 16, 24
N_DEV, N_CHIP = 8, 4
VMEM_LIMIT = 52 * 1024 * 1024

ADAM_LR, ADAM_B1, ADAM_B2, ADAM_EPS, ADAM_WD, ADAM_STEP = 0.001, 0.9, 0.999, 1e-08, 0.01, 10


def _split(a, n):
    parts, rest = [], a.astype(F32)
    for i in range(n):
        p = rest.astype(MXU_DTYPE)
        parts.append(p)
        if i < n - 1:
            rest = rest - p.astype(F32)
    return parts


def _raw_dot(a, b, ca, cb, mode="bf16"):
    d = lambda u, v: lax.dot_general(u, v, (((ca,), (cb,)), ((), ())), preferred_element_type=F32)
    if mode == "bf16":
        return d(a.astype(MXU_DTYPE), b.astype(MXU_DTYPE))
    if mode == "x3":
        (ah, al), (bh, bl) = _split(a, 2), _split(b, 2)
        return d(ah, bh) + (d(ah, bl) + d(al, bh))
    if mode == "sel_a":
        a0 = a.astype(MXU_DTYPE)
        b1, b2, b3 = _split(b, 3)
        return d(a0, b1) + (d(a0, b2) + d(a0, b3))
    assert mode == "sel_b", mode
    b0 = b.astype(MXU_DTYPE)
    a1, a2, a3 = _split(a, 3)
    return d(a1, b0) + (d(a2, b0) + d(a3, b0))


@functools.partial(jax.custom_vjp, nondiff_argnums=(2,))
def mm_nn(a, b, mode="bf16"):
    return _raw_dot(a, b, 1, 0, mode)


@functools.partial(jax.custom_vjp, nondiff_argnums=(2,))
def mm_nt(a, b, mode="bf16"):
    return _raw_dot(a, b, 1, 1, mode)


@functools.partial(jax.custom_vjp, nondiff_argnums=(2,))
def mm_tn(a, b, mode="bf16"):
    return _raw_dot(a, b, 0, 0, mode)


_SAME = {"bf16": ("bf16", "bf16"), "x3": ("x3", "x3")}
_NN_BWD = dict(_SAME, sel_a=("bf16", "sel_a"), sel_b=("sel_b", "bf16"))
_NT_BWD = dict(_SAME, sel_a=("bf16", "sel_b"), sel_b=("sel_b", "bf16"))
_TN_BWD = dict(_SAME, sel_a=("bf16", "sel_a"), sel_b=("sel_a", "bf16"))
mm_nn.defvjp(lambda a, b, m: (_raw_dot(a, b, 1, 0, m), (a, b)),
             lambda m, r, g: (mm_nt(g, r[1], _NN_BWD[m][0]), mm_tn(r[0], g, _NN_BWD[m][1])))
mm_nt.defvjp(lambda a, b, m: (_raw_dot(a, b, 1, 1, m), (a, b)),
             lambda m, r, g: (mm_nn(g, r[1], _NT_BWD[m][0]), mm_tn(g, r[0], _NT_BWD[m][1])))
mm_tn.defvjp(lambda a, b, m: (_raw_dot(a, b, 0, 0, m), (a, b)),
             lambda m, r, g: (mm_nt(r[1], g, _TN_BWD[m][0]), mm_nn(r[0], g, _TN_BWD[m][1])))


@jax.custom_jvp
def sigmoid(x):
    return 1.0 / (1.0 + jnp.exp(-x))


@sigmoid.defjvp
def _sigmoid_jvp(p, t):
    s = sigmoid(p[0])
    return s, t[0] * s * (1.0 - s)


@jax.custom_jvp
def softplus(x):
    return jnp.maximum(x, 0.0) + jnp.log(1.0 + jnp.exp(-jnp.abs(x)))


@softplus.defjvp
def _softplus_jvp(p, t):
    return softplus(p[0]), t[0] * sigmoid(p[0])


def silu(x):
    return x * sigmoid(x)


def rmsnorm(x, w):
    return x * lax.rsqrt(jnp.mean(x * x, axis=-1, keepdims=True) + EPS) * w


def _iota(shape, dim):
    return lax.broadcasted_iota(jnp.int32, shape, dim)


def _halves():
    lane = _iota((1, LANES), 1) >> 6
    return _ind(lane == 0), _ind(lane == 1)


def _block_diag(pair):
    h0, h1 = _halves()
    return jnp.concatenate([pair * h0, pair * h1], axis=0)


def _tri_inv_impl(mats):
    r, c = _iota((CHUNK, LANES), 0), _iota((CHUNK, LANES), 1) & (CHUNK - 1)
    eye = _ind(r == c)
    blockdiag = _ind((r >> 4) == (c >> 4))
    dot = lambda u, v: _raw_dot(u, _block_diag(v), 1, 0, "x3")
    dot1 = lambda u, v: _raw_dot(u, _block_diag(v), 1, 0)
    each = lambda f, *ls: [f(*xs) for xs in zip(*ls)]
    dg = each(lambda a: a * blockdiag, mats)
    off = each(lambda a, d: a - d, mats, dg)
    m = each(lambda d: -d, dg)
    p = each(lambda x: eye + x, m)
    pw = m
    for _ in range(3):
        pw = each(lambda x: dot1(x, x), pw)
        p = each(lambda x, y: x + dot1(x, y), p, pw)
    e = each(dot, p, off)
    e2 = each(lambda x: dot1(x, x), e)
    q = each(lambda x: eye - x, e)
    q = each(lambda x, y: x + dot1(x, y), q, e2)
    return each(dot, q, p)


def _tri_inv_bwd(ts, gs):
    h0, h1 = _halves()
    x = [mm_nt(g, _block_diag(t)) for g, t in zip(gs, ts)]
    full = [mm_tn(t, y) for t, y in zip(ts, x)]
    return [-(f[:CHUNK] * h0 + f[CHUNK:] * h1) for f in full]


@jax.custom_vjp
def tri_inv(mats):
    return _tri_inv_impl(mats)


def _tri_inv_fwd(mats):
    ts = _tri_inv_impl(mats)
    return ts, ts


tri_inv.defvjp(_tri_inv_fwd, lambda ts, gs: (_tri_inv_bwd(ts, gs),))


@jax.custom_vjp
def tri_inv_saved(mats, ts):
    del mats
    return ts


tri_inv_saved.defvjp(lambda mats, ts: (ts, ts),
                     lambda ts, gs: (_tri_inv_bwd(ts, gs), [jnp.zeros_like(t) for t in ts]))


def _ind(cond):
    return jnp.where(cond, 1.0, 0.0).astype(F32)


def _chunk_masks():
    r, c = _iota((CHUNK, CHUNK), 0), _iota((CHUNK, CHUNK), 1)
    return _ind(r >= c), _ind(r > c), _ind(r == c), _ind(_iota((CHUNK, 1), 0) == CHUNK - 1)


def _log_decay_cumsum(small, alog, dtb, tri):
    sp = softplus(small + dtb)
    la = -jnp.exp(alog) * sp
    return sp, mm_nn(tri, la, "sel_a")


def _col_of(x, lane):
    return jnp.sum(x * _ind(_iota((1, LANES), 1) == lane), axis=1, keepdims=True)


def _pair_masks():
    r, c = _iota((CHUNK, LANES), 0), _iota((CHUNK, LANES), 1)
    c6 = c & (CHUNK - 1)
    return _ind(r >= c6), _ind(r > c6), (_ind(c == r), _ind(c == r + CHUNK))


def _decay_pair(col_a, col_b, tri_w, eye_w):
    h0, h1 = _halves()
    col = col_a * h0 + col_b * h1
    row = jnp.sum(col_a * eye_w[0] + col_b * eye_w[1], axis=0, keepdims=True)
    return jnp.exp((col - row) * tri_w) * tri_w


def gdn_chunk(h0, qs, ks, vs, smalls, gates, normw, alog, dtb, states, saved_t=None):
    tri, _, _, last = _chunk_masks()
    tri_w, strict_w, eye_w = _pair_masks()
    nh = len(qs[0])
    flat = lambda xss: [x for xs in xss for x in xs]
    lacs = [_log_decay_cumsum(sm, alog, dtb, tri)[1] for sm in smalls]
    qs, ks, vs, gates = flat(qs), flat(ks), flat(vs), flat(gates)
    heads, pairs = range(len(qs)), range(len(qs) // 2)
    each = lambda f, *ls: [f(*xs) for xs in zip(*ls)]
    ab = lambda xs, p: (xs[2 * p], xs[2 * p + 1])
    stack = lambda xs: jnp.concatenate(xs, axis=0)
    gc = [_col_of(lacs[i // nh], LANE_GA + h0 + i % nh) for i in heads]
    beta = [sigmoid(_col_of(smalls[i // nh], LANE_GB + h0 + i % nh)) for i in heads]
    decay = [_decay_pair(*ab(gc, p), tri_w, eye_w) for p in pairs]
    gl = each(lambda x: jnp.sum(x * last, axis=0, keepdims=True), gc)
    q = each(lambda x: x * lax.rsqrt(jnp.sum(x * x, axis=-1, keepdims=True) + EPS) * (GDN_DK ** -0.5), qs)
    k = each(lambda x: x * lax.rsqrt(jnp.sum(x * x, axis=-1, keepdims=True) + EPS), ks)
    kb = each(lambda x, b: x * b, k, beta)
    eg = each(jnp.exp, gc)
    zero = jnp.zeros((CHUNK, GDN_DK), F32)
    k_bd = [stack([join_lanes([k[2 * p], zero]), join_lanes([zero, k[2 * p + 1]])]) for p in pairs]
    a = [mm_nt(join_lanes(list(ab(kb, p))), k_bd[p]) * (decay[p] * strict_w) for p in pairs]
    t = tri_inv(a) if saved_t is None else tri_inv_saved(a, saved_t)
    attn = [mm_nt(join_lanes(list(ab(q, p))), k_bd[p]) * decay[p] for p in pairs]
    rhs = [stack([join_lanes([vs[h] * beta[h], kb[h] * eg[h]]) for h in (2 * p, 2 * p + 1)]) for p in pairs]
    uw = [mm_nn(_block_diag(t[p]), rhs[p]) for p in pairs]
    uw = [x for p in pairs for x in split_rows(uw[p])]
    u, w = zip(*[split_lanes(x) for x in uw])
    ys = []
    for c in range(len(smalls)):
        hs = range(c * nh, (c + 1) * nh)
        v_new = [u[i] - mm_nn(w[i], states[i % nh]) for i in hs]
        av = [mm_nn(_block_diag(attn[c * nh // 2 + p]), stack(list(ab(v_new, p)))) for p in range(nh // 2)]
        av = [x for y in av for x in split_rows(y)]
        o = [mm_nn(q[i] * eg[i], states[i % nh]) + av[i % nh] for i in hs]
        states = [states[i % nh] * jnp.exp(gl[i]) + mm_tn(k[i] * jnp.exp(gl[i] - gc[i]), v_new[i % nh]) for i in hs]
        ys.append([rmsnorm(o[i % nh], normw) * silu(gates[i]) for i in hs])
    return ys, states, t


@jax.custom_vjp
def split_rows(x):
    n = x.shape[0] // 2
    return [x[:n], x[n:]]


split_rows.defvjp(lambda x: (split_rows(x), None), lambda _, gs: (jnp.concatenate(gs, axis=0),))


@jax.custom_vjp
def split_lanes(x):
    return [x[:, i * LANES:(i + 1) * LANES] for i in range(x.shape[1] // LANES)]


@jax.custom_vjp
def join_lanes(xs):
    return jnp.concatenate(xs, axis=1)


split_lanes.defvjp(lambda x: (split_lanes(x), None), lambda _, gs: (join_lanes(gs),))
join_lanes.defvjp(lambda xs: (join_lanes(xs), None), lambda _, g: (split_lanes(g),))


def ssd_chunk(xs, bm, cm, z, smalls, normw, alog, dtb, dvec, state):
    tri, _, _, last = _chunk_masks()
    tri_w, _, eye_w = _pair_masks()
    h0, h1 = _halves()
    hpg = SSD_HEADS // SSD_GROUPS
    ng = len(normw)
    flat = lambda xss: [x for xs_ in xss for x in xs_]
    xs, bm, cm, z = flat(xs), flat(bm), flat(cm), flat(z)
    units, pairs = range(len(xs)), range(hpg // 2)
    each = lambda f, *ls: [f(*a) for a in zip(*ls)]
    sp_lac = [_log_decay_cumsum(sm, alog, dtb, tri) for sm in smalls]
    lac_last = [jnp.sum(lac * last, axis=0, keepdims=True) for _, lac in sp_lac]
    sel = [_ind(_iota((LANES, SSD_GW), 0) == g * hpg + (_iota((LANES, SSD_GW), 1) >> 6)) for g in range(ng)]
    expand = lambda vs, mode: [mm_nn(vs[i // ng], sel[i % ng], mode) for i in units]
    dt_e = expand([sp for sp, _ in sp_lac], "bf16")
    elac_e = expand([jnp.exp(lac) for _, lac in sp_lac], "bf16")
    toend_e = expand([jnp.exp(ll - lac) for (_, lac), ll in zip(sp_lac, lac_last)], "bf16")
    row8 = _iota((8, 1), 0)
    two_e = expand([_ind(row8 == 0) * dvec + _ind(row8 == 1) * jnp.exp(ll) for ll in lac_last], "sel_b")
    d_e = each(lambda v: jnp.sum(v * _ind(row8 == 0), axis=0, keepdims=True), two_e)
    chunk_e = each(lambda v: jnp.sum(v * _ind(row8 == 1), axis=0, keepdims=True), two_e)
    xdt = each(lambda a, b: a * b, xs, dt_e)
    cb_w = each(lambda c_, b_: mm_nt(c_, jnp.concatenate([b_, b_], axis=0)), cm, bm)
    x_pairs = each(split_lanes, xdt)
    col = lambda i, j: _col_of(sp_lac[i // ng][1], (i % ng) * hpg + j)
    lms = [[_decay_pair(col(i, 2 * p), col(i, 2 * p + 1), tri_w, eye_w) for p in pairs] for i in units]
    stacked = [[jnp.concatenate([x_pairs[i][p] * h0, x_pairs[i][p] * h1], axis=0) for p in pairs] for i in units]
    terms = [[mm_nn(cb_w[i] * lms[i][p], stacked[i][p]) for p in pairs] for i in units]
    y_in = [join_lanes(terms[i]) + xs[i] * d_e[i] for i in units]
    state_in = each(lambda b_, xd, te: mm_tn(b_, xd * te), bm, xdt, toend_e)
    outs = []
    for c in range(len(smalls)):
        us = range(c * ng, (c + 1) * ng)
        y = [mm_nn(cm[i], state[i % ng]) * elac_e[i] + y_in[i] for i in us]
        state = [state[i % ng] * chunk_e[i] + state_in[i] for i in us]
        outs.append([rmsnorm(y[i % ng] * silu(z[i]), normw[i % ng]) for i in us])
    return outs, state


def _params(sem=None):
    return pltpu.CompilerParams(dimension_semantics=sem, vmem_limit_bytes=VMEM_LIMIT)


def _full(shape):
    n = len(shape)
    return pl.BlockSpec(shape, lambda *_: (0,) * n)


ANY = pl.BlockSpec(memory_space=pl.ANY)
HBM = pl.BlockSpec(memory_space=pltpu.HBM)


def in_proj(x, normw, w_main, w_small):
    t = x.shape[0]
    tm, tn = min(2048, t), 512

    def body(x_ref, nw_ref, wm_ref, ws_ref, pm_ref, ps_ref, u_ref):
        @pl.when(pl.program_id(1) == 0)
        def _():
            u = rmsnorm(x_ref[...], nw_ref[...]).astype(MXU_DTYPE)
            u_ref[...] = u
            ps_ref[...] = _raw_dot(u, ws_ref[...], 1, 0)
        pm_ref[...] = _raw_dot(u_ref[...], wm_ref[...], 1, 0)

    return pl.pallas_call(
        body, name="in_proj", grid=(t // tm, COL_CONV // tn),
        in_specs=[pl.BlockSpec((tm, D_MODEL), lambda i, j: (i, 0)), _full((1, D_MODEL)),
                  pl.BlockSpec((D_MODEL, tn), lambda i, j: (0, j)), _full((D_MODEL, LANES))],
        out_specs=[pl.BlockSpec((tm, tn), lambda i, j: (i, j)), pl.BlockSpec((tm, LANES), lambda i, j: (i, 0)),
                   pl.BlockSpec((tm, D_MODEL), lambda i, j: (i, 0))],
        out_shape=[jax.ShapeDtypeStruct((t, COL_CONV), F32), jax.ShapeDtypeStruct((t, LANES), F32),
                   jax.ShapeDtypeStruct((t, D_MODEL), MXU_DTYPE)],
        compiler_params=_params(("arbitrary", "arbitrary")),
    )(x, normw, w_main, w_small)


CONV_TC = 512
HALO = 8


def _shift_down(cur, prev, s):
    rolled = pltpu.roll(cur, s, 0)
    top = jnp.where(_iota((HALO, cur.shape[1]), 0) < s, pltpu.roll(prev, s, 0), rolled[:HALO])
    if cur.shape[0] == HALO:
        return top
    return jnp.concatenate([top, rolled[HALO:]], axis=0)


def _shift_up(cur, nxt, s):
    n = cur.shape[0]
    rolled = pltpu.roll(cur, n - s, 0)
    bot = jnp.where(_iota((HALO, cur.shape[1]), 0) >= HALO - s, pltpu.roll(nxt, HALO - s, 0), rolled[n - HALO:])
    return jnp.concatenate([rolled[:n - HALO], bot], axis=0)


def _conv_pre(cur, prev, w_ref, b, cols=slice(None)):
    acc = cur * w_ref[3:4, cols] + b
    shifted = [cur]
    for s in (1, 2, 3):
        sh = _shift_down(cur, prev, s)
        shifted.append(sh)
        acc = acc + sh * w_ref[3 - s:4 - s, cols]
    return acc, shifted


def in_proj_conv(u, w_main, w, b):
    t = u.shape[0]
    tm, tn = min(2048, t), CONV_TC
    rc = min(256, tm)
    c0, nj = COL_CONV // tn, CONV_W // tn

    def body(u_ref, wm_ref, w_ref, b_ref, out_ref, x_ref, ds_ref, halo_ref):
        j = pl.program_id(1)

        @pl.when(pl.program_id(0) == 0)
        def _():
            halo_ref[j] = jnp.zeros((HALO, tn), F32)

        prev = halo_ref[j]
        for r in range(tm // rc):
            rows = pl.ds(r * rc, rc)
            p = _raw_dot(u_ref[rows, :], wm_ref[...], 1, 0)
            x_ref[rows, :] = p.astype(x_ref.dtype)
            pre, _ = _conv_pre(p, prev, w_ref, b_ref[...])
            sg = sigmoid(pre)
            out_ref[rows, :] = pre * sg
            ds_ref[rows, :] = (sg * (1.0 + pre * (1.0 - sg))).astype(ds_ref.dtype)
            prev = p[rc - HALO:]
        halo_ref[j] = prev

    blk = pl.BlockSpec((tm, tn), lambda i, j: (i, j))
    return pl.pallas_call(
        body, name="in_proj_conv", grid=(t // tm, nj),
        in_specs=[pl.BlockSpec((tm, D_MODEL), lambda i, j: (i, 0)),
                  pl.BlockSpec((D_MODEL, tn), lambda i, j: (0, c0 + j)),
                  pl.BlockSpec((4, tn), lambda i, j: (0, j)), pl.BlockSpec((1, tn), lambda i, j: (0, j))],
        out_specs=[blk, blk, blk],
        out_shape=[jax.ShapeDtypeStruct((t, CONV_W), F32), jax.ShapeDtypeStruct((t, CONV_W), MXU_DTYPE),
                   jax.ShapeDtypeStruct((t, CONV_W), MXU_DTYPE)],
        scratch_shapes=[pltpu.VMEM((nj, HALO, tn), F32)],
        compiler_params=_params(("arbitrary", "arbitrary")),
    )(u, w_main, w, b)


def conv_bwd_w(u, x_conv, dsilu, w, dout, slabbed):
    t = u.shape[0]
    tt, tn = min(512, t), 3 * CONV_TC
    nt, nj = t // tt, CONV_W // tn
    ns = len(slabbed)
    halo_op = 2 * HALO
    after = lambda i, h: jnp.minimum((i + 1) * (tt // h), t // h - 1)

    def body(u_ref, x_ref, ds_ref, ds_nxt_ref, w_ref, do_ref, do_nxt_ref, *rest):
        slab_refs, (dx_ref, dw_ref, dwb_ref) = rest[:ns], rest[ns:ns + 3]
        land_refs, sems = rest[ns + 3:2 * ns + 3], rest[2 * ns + 3:]
        j, i = pl.program_id(0), pl.program_id(1)
        start, finish = _slab_exchange(slab_refs, land_refs, ns, *sems)

        @pl.when(jnp.logical_and(j == 0, i == 0))
        def _():
            start()

        @pl.when(i == 0)
        def _():
            dw_ref[...] = jnp.zeros(dw_ref.shape, F32)
            dwb_ref[...] = jnp.zeros(dwb_ref.shape, F32)

        uu = u_ref[...]
        row = _iota((HALO, CONV_TC), 0)
        last = i == nt - 1
        for piece in range(tn // CONV_TC):
            cols = slice(piece * CONV_TC, (piece + 1) * CONV_TC)
            x = x_ref[:, cols].astype(F32)
            dpre = do_ref[:, cols] * ds_ref[:, cols].astype(F32)
            dpre_nxt = jnp.where(last, 0.0, do_nxt_ref[:, cols] * ds_nxt_ref[:, cols].astype(F32)[:HALO])
            ups = [dpre] + [_shift_up(dpre, dpre_nxt, s) for s in (1, 2, 3)]
            dx = ups[0] * w_ref[3:4, cols]
            upd = jnp.where(row == 4, jnp.sum(dpre, axis=0, keepdims=True), 0.0)
            for s in range(4):
                if s:
                    dx = dx + ups[s] * w_ref[3 - s:4 - s, cols]
                upd = upd + jnp.where(row == 3 - s, jnp.sum(ups[s] * x, axis=0, keepdims=True), 0.0)
            dx = dx.astype(dx_ref.dtype)
            dx_ref[:, cols] = dx
            dw_ref[:, cols] += _raw_dot(uu, dx, 0, 0)
            dwb_ref[:, cols] += upd

        @pl.when(jnp.logical_and(j == nj - 1, last))
        def _():
            finish()

    out = pl.pallas_call(
        body, name="conv_bwd_w", grid=(nj, nt),
        in_specs=[pl.BlockSpec((tt, D_MODEL), lambda j, i: (i, 0)),
                  pl.BlockSpec((tt, tn), lambda j, i: (i, j)),
                  pl.BlockSpec((tt, tn), lambda j, i: (i, j)),
                  pl.BlockSpec((halo_op, tn), lambda j, i: (after(i, halo_op), j)),
                  pl.BlockSpec((4, tn), lambda j, i: (0, j)),
                  pl.BlockSpec((tt, tn), lambda j, i: (i, j)),
                  pl.BlockSpec((HALO, tn), lambda j, i: (after(i, HALO), j))] + [HBM] * ns,
        out_specs=[pl.BlockSpec((tt, tn), lambda j, i: (i, j)), pl.BlockSpec((D_MODEL, tn), lambda j, i: (0, j)),
                   pl.BlockSpec((HALO, tn), lambda j, i: (0, j))] + [HBM] * ns,
        out_shape=[jax.ShapeDtypeStruct((t, CONV_W), MXU_DTYPE), jax.ShapeDtypeStruct((D_MODEL, CONV_W), F32),
                   jax.ShapeDtypeStruct((HALO, CONV_W), F32)] + _slab_exchange_shapes(slabbed, []),
        scratch_shapes=_slab_exchange_sems(ns),
        compiler_params=_params(("arbitrary", "arbitrary")),
    )(u, x_conv, dsilu, dsilu, w, dout, dout, *slabbed)
    return out[0], out[1], out[2], out[3:]


def _ssd_cols(g):
    b0 = SSD_WIDTH + g * SSD_STATE
    c0 = SSD_WIDTH + SSD_GROUPS * SSD_STATE + g * SSD_STATE
    return slice(g * SSD_GW, (g + 1) * SSD_GW), slice(b0, b0 + SSD_STATE), slice(c0, c0 + SSD_STATE)


def _gdn_cols(j):
    return tuple(slice(s * GDN_W + j * GDN_DK, s * GDN_W + (j + 1) * GDN_DK) for s in range(3))


def _ssd_parts(xbc_ref):
    return tuple([[xbc_ref[_chunk_rows(c), _ssd_cols(g)[s]] for g in range(SSD_GROUPS)] for c in range(SSD_CB)]
                 for s in range(3))


def _group_cols(ref):
    return [[ref[_chunk_rows(c), g * SSD_GW:(g + 1) * SSD_GW] for g in range(SSD_GROUPS)] for c in range(SSD_CB)]


def _chunk_rows(c):
    return slice(c * CHUNK, (c + 1) * CHUNK)


def _gdn_parts(qkv_ref):
    assert GDN_HB == GDN_HEADS, "the conv block is read whole: one grid step holds every head"
    return tuple([[qkv_ref[_chunk_rows(c), _gdn_cols(j)[s]] for j in range(GDN_HB)] for c in range(GDN_CB)]
                 for s in range(3))


def _head_cols(ref):
    return [[ref[_chunk_rows(c), j * GDN_DV:(j + 1) * GDN_DV] for j in range(GDN_HB)] for c in range(GDN_CB)]


def _chunk_blocks(ref, n=GDN_CB):
    return [ref[_chunk_rows(c), :] for c in range(n)]


def _first_head():
    return 0 if GDN_HB == GDN_HEADS else pl.program_id(1) * GDN_HB


def ssd_fwd(conv_ssd, proj_main, proj_small, normw, alog, dtb, dvec):
    t = conv_ssd.shape[0]
    rows = CHUNK * SSD_CB
    nc = t // rows
    groups = range(SSD_GROUPS)
    norm_cols = lambda ref: [ref[:, g * SSD_GW:(g + 1) * SSD_GW] for g in groups]

    def body(xbc_ref, z_ref, sm_ref, nw_ref, al_ref, db_ref, dv_ref, y_ref, hist_ref, state_ref):
        @pl.when(pl.program_id(0) == 0)
        def _():
            state_ref[...] = jnp.zeros(state_ref.shape, F32)

        states = [state_ref[g] for g in groups]
        for g in groups:
            hist_ref[0, g] = states[g]
        ys, new_states = ssd_chunk(*_ssd_parts(xbc_ref), _group_cols(z_ref), _chunk_blocks(sm_ref, SSD_CB),
                                   norm_cols(nw_ref), al_ref[...], db_ref[...], dv_ref[...], states)
        for c in range(SSD_CB):
            for g in groups:
                y_ref[_chunk_rows(c), g * SSD_GW:(g + 1) * SSD_GW] = ys[c][g].astype(MXU_DTYPE)
        for g in groups:
            state_ref[g] = new_states[g]

    return pl.pallas_call(
        body, name="ssd_fwd", grid=(nc,),
        in_specs=[pl.BlockSpec((rows, SSD_CONV), lambda c: (c, (COL_SSD - COL_CONV) // SSD_CONV)),
                  pl.BlockSpec((rows, SSD_WIDTH), lambda c: (c, COL_Z // SSD_WIDTH)),
                  pl.BlockSpec((rows, LANES), lambda c: (c, 0)),
                  _full((1, SSD_WIDTH)), _full((1, LANES)), _full((1, LANES)), _full((1, LANES))],
        out_specs=[pl.BlockSpec((rows, SSD_WIDTH), lambda c: (c, 0)),
                   pl.BlockSpec((1, SSD_GROUPS, SSD_STATE, SSD_GW), lambda c: (c, 0, 0, 0))],
        out_shape=[jax.ShapeDtypeStruct((t, SSD_WIDTH), MXU_DTYPE),
                   jax.ShapeDtypeStruct((nc, SSD_GROUPS, SSD_STATE, SSD_GW), F32)],
        scratch_shapes=[pltpu.VMEM((SSD_GROUPS, SSD_STATE, SSD_GW), F32)],
        compiler_params=_params(("arbitrary",)),
    )(conv_ssd, proj_main, proj_small, normw, alog, dtb, dvec)


def _accumulate(ref, first, value):
    @pl.when(first)
    def _():
        ref[...] = value

    @pl.when(jnp.logical_not(first))
    def _():
        ref[...] += value


def ssd_bwd(conv_ssd, proj_main, proj_small, normw, alog, dtb, dvec, hist, dy):
    t = conv_ssd.shape[0]
    rows = CHUNK * SSD_CB
    nc = t // rows
    rev = lambda c: nc - 1 - c
    groups = range(SSD_GROUPS)
    norm_cols = lambda ref: [ref[:, g * SSD_GW:(g + 1) * SSD_GW] for g in groups]

    def body(xbc_ref, z_ref, sm_ref, nw_ref, al_ref, db_ref, dv_ref, hist_ref, dy_ref,
             dxbc_ref, dz_ref, dsm_ref, dnw_ref, dal_ref, ddb_ref, ddv_ref, dstate_ref):
        first = pl.program_id(0) == 0

        @pl.when(first)
        def _():
            dstate_ref[...] = jnp.zeros(dstate_ref.shape, F32)

        _, vjp = jax.vjp(ssd_chunk, *_ssd_parts(xbc_ref), _group_cols(z_ref), _chunk_blocks(sm_ref, SSD_CB),
                         norm_cols(nw_ref), al_ref[...], db_ref[...], dv_ref[...], [hist_ref[0, g] for g in groups])
        dxs, dbm, dcm, dz, dsm, dnw, dal, ddb, ddv, dstate = vjp(
            (_group_cols(dy_ref), [dstate_ref[g] for g in groups]))
        for k in range(SSD_CB):
            rk = _chunk_rows(k)
            for g in groups:
                xc, bc, cc = _ssd_cols(g)
                dxbc_ref[rk, xc] = dxs[k][g]
                dxbc_ref[rk, bc] = dbm[k][g]
                dxbc_ref[rk, cc] = dcm[k][g]
                dz_ref[rk, g * SSD_GW:(g + 1) * SSD_GW] = dz[k][g].astype(dz_ref.dtype)
            dsm_ref[rk, :] = dsm[k]
        for g in groups:
            dstate_ref[g] = dstate[g]
        _accumulate(dnw_ref, first, join_lanes(dnw))
        _accumulate(dal_ref, first, dal)
        _accumulate(ddb_ref, first, ddb)
        _accumulate(ddv_ref, first, ddv)

    return pl.pallas_call(
        body, name="ssd_bwd", grid=(nc,),
        in_specs=[pl.BlockSpec((rows, SSD_CONV), lambda c: (rev(c), (COL_SSD - COL_CONV) // SSD_CONV)),
                  pl.BlockSpec((rows, SSD_WIDTH), lambda c: (rev(c), COL_Z // SSD_WIDTH)),
                  pl.BlockSpec((rows, LANES), lambda c: (rev(c), 0)),
                  _full((1, SSD_WIDTH)), _full((1, LANES)), _full((1, LANES)), _full((1, LANES)),
                  pl.BlockSpec((1, SSD_GROUPS, SSD_STATE, SSD_GW), lambda c: (rev(c), 0, 0, 0)),
                  pl.BlockSpec((rows, SSD_WIDTH), lambda c: (rev(c), 0))],
        out_specs=[pl.BlockSpec((rows, SSD_CONV), lambda c: (rev(c), (COL_SSD - COL_CONV) // SSD_CONV)),
                   pl.BlockSpec((rows, SSD_WIDTH), lambda c: (rev(c), COL_Z // SSD_WIDTH)),
                   pl.BlockSpec((rows, LANES), lambda c: (rev(c), 0)),
                   _full((1, SSD_WIDTH)), _full((1, LANES)), _full((1, LANES)), _full((1, LANES))],
        out_shape=[jax.ShapeDtypeStruct((t, CONV_W), F32), jax.ShapeDtypeStruct((t, COL_CONV), MXU_DTYPE),
                   jax.ShapeDtypeStruct((t, LANES), F32), jax.ShapeDtypeStruct((1, SSD_WIDTH), F32),
                   jax.ShapeDtypeStruct((1, LANES), F32), jax.ShapeDtypeStruct((1, LANES), F32),
                   jax.ShapeDtypeStruct((1, LANES), F32)],
        scratch_shapes=[pltpu.VMEM((SSD_GROUPS, SSD_STATE, SSD_GW), F32)],
        compiler_params=_params(("arbitrary",)),
    )(conv_ssd, proj_main, proj_small, normw, alog, dtb, dvec, hist, dy)


def gdn_fwd(conv_gdn, proj_main, proj_small, normw, alog, dtb):
    t = conv_gdn.shape[0]
    hb, cb = GDN_HB, GDN_CB
    rows = CHUNK * cb
    ns = t // rows
    gate_blk = COL_GATE // (GDN_DV * hb)

    def body(qkv_ref, gate_ref, sm_ref, nw_ref, al_ref, db_ref, y_ref, hist_ref, t_ref, state_ref):
        h0 = _first_head()

        @pl.when(pl.program_id(0) == 0)
        def _():
            for j in range(hb):
                state_ref[h0 + j] = jnp.zeros((GDN_DK, GDN_DV), F32)

        states = [state_ref[h0 + j] for j in range(hb)]
        for j in range(hb):
            hist_ref[0, j] = states[j]
        qs, ks, vs = _gdn_parts(qkv_ref)
        ys, new_states, ts = gdn_chunk(h0, qs, ks, vs, _chunk_blocks(sm_ref), _head_cols(gate_ref), nw_ref[...],
                                       al_ref[...], db_ref[...], states)
        for c in range(cb):
            for j in range(hb):
                y_ref[_chunk_rows(c), j * GDN_DV:(j + 1) * GDN_DV] = ys[c][j].astype(MXU_DTYPE)
        for j in range(hb):
            state_ref[h0 + j] = new_states[j]
        for p in range(cb * hb // 2):
            t_ref[0, p] = ts[p]

    return pl.pallas_call(
        body, name="gdn_fwd", grid=(ns, GDN_HEADS // hb),
        in_specs=[pl.BlockSpec((rows, GDN_HC * hb), lambda c, h: (c, h)),
                  pl.BlockSpec((rows, GDN_DV * hb), lambda c, h: (c, gate_blk + h)),
                  pl.BlockSpec((rows, LANES), lambda c, h: (c, 0)),
                  _full((1, GDN_DV)), _full((1, LANES)), _full((1, LANES))],
        out_specs=[pl.BlockSpec((rows, GDN_DV * hb), lambda c, h: (c, h)),
                   pl.BlockSpec((1, hb, GDN_DK, GDN_DV), lambda c, h: (c, h, 0, 0)),
                   pl.BlockSpec((1, cb * hb // 2, CHUNK, LANES), lambda c, h: (c, h, 0, 0))],
        out_shape=[jax.ShapeDtypeStruct((t, GDN_W), MXU_DTYPE),
                   jax.ShapeDtypeStruct((ns, GDN_HEADS, GDN_DK, GDN_DV), F32),
                   jax.ShapeDtypeStruct((ns, cb * GDN_HEADS // 2, CHUNK, LANES), F32)],
        scratch_shapes=[pltpu.VMEM((GDN_HEADS, GDN_DK, GDN_DV), F32)],
        compiler_params=_params(("arbitrary", "arbitrary")),
    )(conv_gdn, proj_main, proj_small, normw, alog, dtb)


def gdn_bwd(dproj_main, dconv, conv_gdn, proj_main, proj_small, normw, alog, dtb, hist, t_inv, dy):
    t = conv_gdn.shape[0]
    hb, cb = GDN_HB, GDN_CB
    rows = CHUNK * cb
    ns = t // rows
    rev = lambda c: ns - 1 - c
    gate_blk = COL_GATE // (GDN_DV * hb)

    def body(alias_ref, alias2_ref, qkv_ref, gate_ref, sm_ref, nw_ref, al_ref, db_ref, hist_ref, t_ref, dy_ref,
             dgate_ref, dqkv_ref, dsm_ref, dnw_ref, dal_ref, ddb_ref, dstate_ref):
        del alias_ref, alias2_ref
        c, h = pl.program_id(0), pl.program_id(1)
        h0 = _first_head()

        @pl.when(c == 0)
        def _():
            for j in range(hb):
                dstate_ref[h0 + j] = jnp.zeros((GDN_DK, GDN_DV), F32)

        saved = [t_ref[0, p] for p in range(cb * hb // 2)]

        def fn(qs, ks, vs, smalls, gates, nw, al, db, states):
            return gdn_chunk(h0, qs, ks, vs, smalls, gates, nw, al, db, states, saved)[:2]

        qs, ks, vs = _gdn_parts(qkv_ref)
        _, vjp = jax.vjp(fn, qs, ks, vs, _chunk_blocks(sm_ref), _head_cols(gate_ref), nw_ref[...], al_ref[...],
                         db_ref[...], [hist_ref[0, j] for j in range(hb)])
        dqs, dks, dvs, dsm, dgates, dnw, dal, ddb, dstates = vjp(
            (_head_cols(dy_ref), [dstate_ref[h0 + j] for j in range(hb)]))
        for k in range(cb):
            rk = _chunk_rows(k)
            for j in range(hb):
                qc, kc, vc = _gdn_cols(j)
                dqkv_ref[rk, qc] = dqs[k][j]
                dqkv_ref[rk, kc] = dks[k][j]
                dqkv_ref[rk, vc] = dvs[k][j]
                dgate_ref[rk, j * GDN_DV:(j + 1) * GDN_DV] = dgates[k][j].astype(dgate_ref.dtype)
        for j in range(hb):
            dstate_ref[h0 + j] = dstates[j]
        _accumulate(dsm_ref, h == 0, jnp.concatenate(dsm, axis=0))
        first = jnp.logical_and(c == 0, h == 0)
        _accumulate(dnw_ref, first, dnw)
        _accumulate(dal_ref, first, dal)
        _accumulate(ddb_ref, first, ddb)

    return pl.pallas_call(
        body, name="gdn_bwd", grid=(ns, GDN_HEADS // hb),
        in_specs=[ANY, ANY, pl.BlockSpec((rows, GDN_HC * hb), lambda c, h: (rev(c), h)),
                  pl.BlockSpec((rows, GDN_DV * hb), lambda c, h: (rev(c), gate_blk + h)),
                  pl.BlockSpec((rows, LANES), lambda c, h: (rev(c), 0)),
                  _full((1, GDN_DV)), _full((1, LANES)), _full((1, LANES)),
                  pl.BlockSpec((1, hb, GDN_DK, GDN_DV), lambda c, h: (rev(c), h, 0, 0)),
                  pl.BlockSpec((1, cb * hb // 2, CHUNK, LANES), lambda c, h: (rev(c), h, 0, 0)),
                  pl.BlockSpec((rows, GDN_DV * hb), lambda c, h: (rev(c), h))],
        out_specs=[pl.BlockSpec((rows, GDN_DV * hb), lambda c, h: (rev(c), gate_blk + h)),
                   pl.BlockSpec((rows, GDN_HC * hb), lambda c, h: (rev(c), h)),
                   pl.BlockSpec((rows, LANES), lambda c, h: (rev(c), 0)),
                   _full((1, GDN_DV)), _full((1, LANES)), _full((1, LANES))],
        out_shape=[jax.ShapeDtypeStruct(dproj_main.shape, dproj_main.dtype),
                   jax.ShapeDtypeStruct(dconv.shape, dconv.dtype),
                   jax.ShapeDtypeStruct((t, LANES), F32), jax.ShapeDtypeStruct((1, GDN_DV), F32),
                   jax.ShapeDtypeStruct((1, LANES), F32), jax.ShapeDtypeStruct((1, LANES), F32)],
        scratch_shapes=[pltpu.VMEM((GDN_HEADS, GDN_DK, GDN_DV), F32)],
        input_output_aliases={0: 0, 1: 1},
        compiler_params=_params(("arbitrary", "arbitrary")),
    )(dproj_main, dconv, conv_gdn, proj_main, proj_small, normw, alog, dtb, hist, t_inv, dy)


def out_proj_loss(x, y_ssd, y_gdn, w_out, final_w, target):
    t = x.shape[0]
    tm = min(512, t)

    def body(x_ref, ys_ref, yg_ref, wo_ref, fw_ref, tg_ref, loss_ref, dhid_ref, dys_ref, dyg_ref, dwo_ref, dfw_ref):
        i = pl.program_id(0)
        ys, yg = ys_ref[...], yg_ref[...]
        wo_s, wo_g = wo_ref[:SSD_WIDTH, :], wo_ref[SSD_WIDTH:, :]
        hid = x_ref[...] + _raw_dot(ys, wo_s, 1, 0) + _raw_dot(yg, wo_g, 1, 0)
        out, vjp = jax.vjp(rmsnorm, hid, fw_ref[...])
        err = out - tg_ref[...]
        loss = 0.5 * jnp.sum(jnp.mean(err * err, axis=-1, keepdims=True), axis=0, keepdims=True)
        dhid, dfw = vjp(err * (1.0 / D_MODEL))
        dhid_ref[...] = dhid
        dys_ref[...] = _raw_dot(dhid, wo_s, 1, 1)
        dyg_ref[...] = _raw_dot(dhid, wo_g, 1, 1)
        first = i == 0
        _accumulate(loss_ref, first, jnp.broadcast_to(loss, loss_ref.shape))
        _accumulate(dfw_ref, first, dfw)

        @pl.when(first)
        def _():
            dwo_ref[:SSD_WIDTH, :] = _raw_dot(ys, dhid, 0, 0)
            dwo_ref[SSD_WIDTH:, :] = _raw_dot(yg, dhid, 0, 0)

        @pl.when(i > 0)
        def _():
            dwo_ref[:SSD_WIDTH, :] += _raw_dot(ys, dhid, 0, 0)
            dwo_ref[SSD_WIDTH:, :] += _raw_dot(yg, dhid, 0, 0)

    row = lambda w: pl.BlockSpec((tm, w), lambda i: (i, 0))
    return pl.pallas_call(
        body, name="out_proj_loss", grid=(t // tm,),
        in_specs=[row(D_MODEL), row(SSD_WIDTH), row(GDN_W), _full((SSD_WIDTH + GDN_W, D_MODEL)), _full((1, D_MODEL)),
                  row(D_MODEL)],
        out_specs=[_full((8, LANES)), row(D_MODEL), row(SSD_WIDTH), row(GDN_W), _full((SSD_WIDTH + GDN_W, D_MODEL)),
                   _full((1, D_MODEL))],
        out_shape=[jax.ShapeDtypeStruct((8, LANES), F32), jax.ShapeDtypeStruct((t, D_MODEL), F32),
                   jax.ShapeDtypeStruct((t, SSD_WIDTH), F32), jax.ShapeDtypeStruct((t, GDN_W), F32),
                   jax.ShapeDtypeStruct((SSD_WIDTH + GDN_W, D_MODEL), F32), jax.ShapeDtypeStruct((1, D_MODEL), F32)],
        compiler_params=_params(("arbitrary",)),
    )(x, y_ssd, y_gdn, w_out, final_w, target)


def in_proj_bwd_x(x, normw, w_main, w_small, dproj_main, dproj_conv, dsmall_a, dsmall_b, dhid, slabbed):
    t = x.shape[0]
    tm = min(256, t)
    ni = t // tm
    ns = len(slabbed)

    def body(x_ref, nw_ref, wm_ref, ws_ref, dp_ref, dc_ref, da_ref, db_ref, dh_ref, *rest):
        slab_refs, (gx_ref, dnw_ref), land_refs = rest[:ns], rest[ns:ns + 2], rest[ns + 2:2 * ns + 2]
        sems = rest[2 * ns + 2:]
        i = pl.program_id(0)
        start, finish = _slab_exchange(slab_refs, land_refs, ns, *sems)

        @pl.when(i == 0)
        def _():
            start()

        du = (_raw_dot(dp_ref[...], wm_ref[:, :COL_CONV], 1, 1) + _raw_dot(dc_ref[...], wm_ref[:, COL_CONV:], 1, 1)
              + _raw_dot(da_ref[...] + db_ref[...], ws_ref[...], 1, 1))
        _, vjp = jax.vjp(rmsnorm, x_ref[...], nw_ref[...])
        dx, dnw = vjp(du)
        gx_ref[...] = dx + dh_ref[...]
        _accumulate(dnw_ref, i == 0, dnw)

        @pl.when(i == ni - 1)
        def _():
            finish()

    row = lambda w: pl.BlockSpec((tm, w), lambda i: (i, 0))
    out = pl.pallas_call(
        body, name="in_proj_bwd_x", grid=(ni,),
        in_specs=[row(D_MODEL), _full((1, D_MODEL)), _full((D_MODEL, MAIN)), _full((D_MODEL, LANES)), row(COL_CONV),
                  row(CONV_W), row(LANES), row(LANES), row(D_MODEL)] + [HBM] * ns,
        out_specs=[row(D_MODEL), _full((1, D_MODEL))] + [HBM] * ns,
        out_shape=[jax.ShapeDtypeStruct((t, D_MODEL), F32), jax.ShapeDtypeStruct((1, D_MODEL), F32)]
        + _slab_exchange_shapes(slabbed, []),
        scratch_shapes=_slab_exchange_sems(ns),
        compiler_params=_params(("arbitrary",)),
    )(x, normw, w_main, w_small, dproj_main, dproj_conv, dsmall_a, dsmall_b, dhid, *slabbed)
    return out[0], out[1], out[2:]


def in_proj_bwd_w(u, dproj_main, dsmall_a, dsmall_b):
    t = u.shape[0]
    tm, tn = min(2048, t), COL_CONV // 2

    def body(u_ref, dp_ref, da_ref, db_ref, dwm_ref, dws_ref):
        j, i = pl.program_id(0), pl.program_id(1)
        uu = u_ref[...]
        _accumulate(dwm_ref, i == 0, _raw_dot(uu, dp_ref[...], 0, 0))

        @pl.when(j == 0)
        def _():
            _accumulate(dws_ref, i == 0, _raw_dot(uu, da_ref[...] + db_ref[...], 0, 0))

    return pl.pallas_call(
        body, name="in_proj_bwd_w", grid=(COL_CONV // tn, t // tm),
        in_specs=[pl.BlockSpec((tm, D_MODEL), lambda j, i: (i, 0)), pl.BlockSpec((tm, tn), lambda j, i: (i, j)),
                  pl.BlockSpec((tm, LANES), lambda j, i: (i, 0)), pl.BlockSpec((tm, LANES), lambda j, i: (i, 0))],
        out_specs=[pl.BlockSpec((D_MODEL, tn), lambda j, i: (0, j)), _full((D_MODEL, LANES))],
        out_shape=[jax.ShapeDtypeStruct((D_MODEL, COL_CONV), F32), jax.ShapeDtypeStruct((D_MODEL, LANES), F32)],
        compiler_params=_params(("arbitrary", "arbitrary")),
    )(u, dproj_main, dsmall_a, dsmall_b)


def sum_slabs(a, name):
    n, rows, cols = a.shape
    tr = 64 if rows % 64 == 0 else rows

    def body(a_ref, o_ref):
        acc = a_ref[0].astype(F32)
        for d in range(1, n):
            acc = acc + a_ref[d].astype(F32)
        o_ref[...] = acc

    return pl.pallas_call(
        body, name=name, grid=(rows // tr,),
        in_specs=[pl.BlockSpec((n, tr, cols), lambda i: (0, i, 0))],
        out_specs=pl.BlockSpec((tr, cols), lambda i: (i, 0)),
        out_shape=jax.ShapeDtypeStruct((rows, cols), F32),
        compiler_params=_params(("arbitrary",)),
    )(a)


def adamw(w, g, m, v, name):
    _, rows, cols = w.shape
    tr = 128 if rows % 128 == 0 else rows

    def body(w_ref, g_ref, m_ref, v_ref, d_ref, nm_ref, nv_ref):
        gg = g_ref[...]
        nm = ADAM_B1 * m_ref[...] + (1.0 - ADAM_B1) * gg
        nv = ADAM_B2 * v_ref[...] + (1.0 - ADAM_B2) * (gg * gg)
        m_hat = nm / (1.0 - ADAM_B1 ** ADAM_STEP)
        v_hat = nv / (1.0 - ADAM_B2 ** ADAM_STEP)
        d_ref[...] = -ADAM_LR * (m_hat / (jnp.sqrt(v_hat) + ADAM_EPS) + ADAM_WD * w_ref[...])
        nm_ref[...] = nm
        nv_ref[...] = nv

    spec = pl.BlockSpec((1, tr, cols), lambda i: (0, i, 0))
    shp = jax.ShapeDtypeStruct((1, rows, cols), F32)
    return pl.pallas_call(
        body, name=name, grid=(rows // tr,), in_specs=[spec] * 4, out_specs=[spec] * 3, out_shape=[shp] * 3,
        compiler_params=_params(("arbitrary",)),
    )(w, g.reshape(w.shape), m, v)


def _my_place():
    return lax.axis_index("x"), lax.axis_index("y"), lax.axis_index("c")


def gather_weights(big, small):
    nb, n = len(big), len(big) + len(small)
    parts = 4

    def body(*refs):
        srcs, outs = refs[:n], refs[n:2 * n]
        land_a, land_b = refs[2 * n:2 * n + nb], refs[2 * n + nb:2 * n + 2 * nb]
        send_sems, recv_sems, fwd_send, fwd_recv, local_sems = refs[2 * n + 2 * nb:]
        x, y, c = _my_place()
        me = 2 * x + y
        chips = [(1 - x, y), (x, 1 - y), (1 - x, 1 - y)]
        half = [a.shape[0] // 2 for a in big]

        def ici(j, i):
            px, py = chips[j]
            if i < nb:
                src, dst = srcs[i].at[pl.ds(c * half[i], half[i])], land_a[i].at[j]
            else:
                src, dst = srcs[i], outs[i].at[me]
            return pltpu.make_async_remote_copy(src_ref=src, dst_ref=dst, send_sem=send_sems.at[j * n + i],
                                                recv_sem=recv_sems.at[j * n + i], device_id=(px, py, c),
                                                device_id_type=MESH)

        def ici_arrival(j, i):
            px, py = chips[j]
            dst = land_a[i].at[j] if i < nb else outs[i].at[2 * px + py]
            return pltpu.make_async_remote_copy(src_ref=dst, dst_ref=dst, send_sem=send_sems.at[j * n + i],
                                                recv_sem=recv_sems.at[j * n + i], device_id=(px, py, c),
                                                device_id_type=MESH)

        def forward(j, i, p):
            rows = half[i] // parts
            k = (j * nb + i) * parts + p
            return pltpu.make_async_remote_copy(
                src_ref=land_a[i].at[j, pl.ds(p * rows, rows)], dst_ref=land_b[i].at[j, pl.ds(p * rows, rows)],
                send_sem=fwd_send.at[k], recv_sem=fwd_recv.at[k], device_id=(x, y, 1 - c), device_id_type=MESH)

        def store(j, i, from_sibling):
            px, py = chips[j]
            buf, h = (land_b, 1 - c) if from_sibling else (land_a, c)
            k = n + (j * nb + i) * 2 + (1 if from_sibling else 0)
            return pltpu.make_async_copy(buf[i].at[j], outs[i].at[2 * px + py, pl.ds(h * half[i], half[i])],
                                         local_sems.at[k])

        own = [pltpu.make_async_copy(srcs[i], outs[i].at[me], local_sems.at[i]) for i in range(n)]
        sends = [ici(j, i) for j in range(3) for i in range(n)]
        for cp in own + sends:
            cp.start()
        pending = []
        for j in range(3):
            for i in range(n):
                ici_arrival(j, i).wait_recv()
                if i < nb:
                    fw = [forward(j, i, p) for p in range(parts)]
                    st = store(j, i, False)
                    for cp in fw + [st]:
                        cp.start()
                    pending += [cp.wait_send for cp in fw] + [st.wait]
        for j in range(3):
            for i in range(nb):
                for p in range(parts):
                    forward(j, i, p).wait_recv()
                st = store(j, i, True)
                st.start()
                pending.append(st.wait)
        for cp in sends:
            cp.wait_send()
        for wait in pending:
            wait()
        for cp in own:
            cp.wait()

    shards = list(big) + list(small)
    lands = [pltpu.VMEM((3, a.shape[0] // 2) + a.shape[1:], a.dtype) for a in big]
    return pl.pallas_call(
        body, name="gather_weights",
        in_specs=[HBM] * n, out_specs=[HBM] * n,
        out_shape=[jax.ShapeDtypeStruct((N_CHIP,) + s.shape, s.dtype) for s in shards],
        scratch_shapes=lands + lands + [
            pltpu.SemaphoreType.DMA((3 * n,)), pltpu.SemaphoreType.DMA((3 * n,)),
            pltpu.SemaphoreType.DMA((3 * nb * parts,)), pltpu.SemaphoreType.DMA((3 * nb * parts,)),
            pltpu.SemaphoreType.DMA((n + 6 * nb,))],
        compiler_params=pltpu.CompilerParams(vmem_limit_bytes=VMEM_LIMIT),
    )(*shards)


def _peer(x, y, c, mask):
    mx, my, mc = (mask >> 2) & 1, (mask >> 1) & 1, mask & 1
    return (x ^ mx if mx else x, y ^ my if my else y, c ^ mc if mc else c)


def _slab_exchange_shapes(slabbed, replicated):
    return ([jax.ShapeDtypeStruct(a.shape, a.dtype) for a in slabbed]
            + [jax.ShapeDtypeStruct((N_DEV,) + a.shape, a.dtype) for a in replicated])


def _slab_exchange_sems(n):
    return [pltpu.SemaphoreType.DMA((7 * n,)), pltpu.SemaphoreType.DMA((7 * n,)), pltpu.SemaphoreType.DMA((n,))]


def _slab_exchange(srcs, outs, ns, send_sems, recv_sems, local_sems):
    n = len(srcs)
    x, y, c = _my_place()
    me = 4 * x + 2 * y + c

    def piece(i, dev):
        return srcs[i].at[dev] if i < ns else srcs[i]

    def copies(arriving):
        out = []
        for mask in range(1, N_DEV):
            px, py, pc = _peer(x, y, c, mask)
            dev = 4 * px + 2 * py + pc
            for i in range(n):
                k = (mask - 1) * n + i
                out.append(pltpu.make_async_remote_copy(
                    src_ref=piece(i, dev), dst_ref=outs[i].at[dev if arriving else me], send_sem=send_sems.at[k],
                    recv_sem=recv_sems.at[k], device_id=(px, py, pc), device_id_type=MESH))
        return out

    def local():
        return [pltpu.make_async_copy(piece(i, me), outs[i].at[me], local_sems.at[i]) for i in range(n)]

    def start():
        for cp in local() + copies(False):
            cp.start()

    def finish():
        for cp in copies(True):
            cp.wait_recv()
        for cp in copies(False):
            cp.wait_send()
        for cp in local():
            cp.wait()

    return start, finish


def exchange_halves(landed, replicated):
    n, nr = len(landed), len(replicated)
    streams = 8
    halves = [jax.ShapeDtypeStruct(a.shape[1:], F32) for a in landed]
    sum_rows = 64

    def body(*refs):
        srcs, rep_srcs, outs, rep_outs = refs[:n], refs[n:n + nr], refs[n + nr:2 * n + nr], refs[2 * n + nr:2 * (n + nr)]
        refs = refs[2 * (n + nr):]
        slabs, mine, theirs = refs[:n], refs[n:2 * n], refs[2 * n:3 * n]
        send_sems, recv_sems, in_sems, out_sems = refs[3 * n:3 * n + 4]
        rep_start, rep_finish = _slab_exchange(rep_srcs, rep_outs, 0, *refs[3 * n + 4:])
        rep_start()
        x, y, c = _my_place()
        loads = [pltpu.make_async_copy(srcs[i], slabs[i], in_sems.at[i]) for i in range(n)]
        for cp in loads:
            cp.start()
        for i in range(n):
            loads[i].wait()
            for r in range(0, halves[i].shape[0], sum_rows):
                rows = pl.ds(r, sum_rows)
                acc = slabs[i][0, rows, :].astype(F32)
                for d in range(1, N_DEV):
                    acc = acc + slabs[i][d, rows, :].astype(F32)
                mine[i][rows, :] = acc

        def chunk_copy(i, s):
            rows = halves[i].shape[0] // streams
            k = i * streams + s
            return pltpu.make_async_remote_copy(
                src_ref=mine[i].at[pl.ds(s * rows, rows)], dst_ref=theirs[i].at[pl.ds(s * rows, rows)],
                send_sem=send_sems.at[k], recv_sem=recv_sems.at[k], device_id=(x, y, 1 - c), device_id_type=MESH)

        sends = [chunk_copy(i, s) for i in range(n) for s in range(streams)]
        for cp in sends:
            cp.start()
        own = [pltpu.make_async_copy(mine[i], outs[i].at[c], out_sems.at[i]) for i in range(n)]
        for cp in own:
            cp.start()
        for cp in sends:
            cp.wait_recv()
        got = [pltpu.make_async_copy(theirs[i], outs[i].at[1 - c], out_sems.at[n + i]) for i in range(n)]
        for cp in got:
            cp.start()
        for cp in sends:
            cp.wait_send()
        for cp in own + got:
            cp.wait()
        rep_finish()

    vmem = [pltpu.VMEM(a.shape, a.dtype) for a in halves]
    out = pl.pallas_call(
        body, name="exchange_halves",
        in_specs=[HBM] * (n + nr), out_specs=[HBM] * (n + nr),
        out_shape=[jax.ShapeDtypeStruct((2,) + a.shape, a.dtype) for a in halves]
        + _slab_exchange_shapes([], replicated),
        scratch_shapes=[pltpu.VMEM(a.shape, a.dtype) for a in landed] + vmem + vmem
        + [pltpu.SemaphoreType.DMA((n * streams,)), pltpu.SemaphoreType.DMA((n * streams,)),
           pltpu.SemaphoreType.DMA((n,)), pltpu.SemaphoreType.DMA((2 * n,))] + _slab_exchange_sems(nr),
        compiler_params=pltpu.CompilerParams(vmem_limit_bytes=VMEM_LIMIT),
    )(*landed, *replicated)
    return out[:n], out[n:]


def _pack_cols(pieces):
    offs, pos = [], 0
    for a in pieces:
        offs.append(pos)
        pos += a.shape[1]
    rows8 = [jnp.pad(a.astype(F32), ((0, 8 - a.shape[0]), (0, 0))) for a in pieces]
    return jnp.concatenate(rows8, axis=1), offs


def adamw_many(ws, gs, ms, vs):
    n = len(ws)

    def body(*refs):
        w_r, g_r, m_r, v_r = refs[:n], refs[n:2 * n], refs[2 * n:3 * n], refs[3 * n:4 * n]
        d_o, m_o, v_o = refs[4 * n:5 * n], refs[5 * n:6 * n], refs[6 * n:7 * n]
        for i in range(n):
            gg = g_r[i][...]
            nm = ADAM_B1 * m_r[i][...] + (1.0 - ADAM_B1) * gg
            nv = ADAM_B2 * v_r[i][...] + (1.0 - ADAM_B2) * (gg * gg)
            m_hat = nm / (1.0 - ADAM_B1 ** ADAM_STEP)
            v_hat = nv / (1.0 - ADAM_B2 ** ADAM_STEP)
            d_o[i][...] = -ADAM_LR * (m_hat / (jnp.sqrt(v_hat) + ADAM_EPS) + ADAM_WD * w_r[i][...])
            m_o[i][...] = nm
            v_o[i][...] = nv

    shapes = [jax.ShapeDtypeStruct(w.shape, F32) for w in ws]
    out = pl.pallas_call(body, name="adamw_small", out_shape=shapes * 3,
                         compiler_params=pltpu.CompilerParams(vmem_limit_bytes=VMEM_LIMIT))(*ws, *gs, *ms, *vs)
    return out[:n], out[n:2 * n], out[2 * n:]


def _lanes(vec, start):
    n = vec.shape[-1]
    return jnp.pad(vec.reshape(1, n).astype(F32), ((0, 0), (start, LANES - start - n)))


def kernel(x, norm_w, w_in, ssd_conv_w, ssd_conv_b, ssd_dt_bias, ssd_a_log, ssd_d, ssd_norm_w, gdn_conv_w, gdn_dt_bias, gdn_a_log, gdn_norm_w, w_out, final_norm_w, loss_target, m_norm_w, m_w_in, m_ssd_conv_w, m_ssd_conv_b, m_ssd_dt_bias, m_ssd_a_log, m_ssd_d, m_ssd_norm_w, m_gdn_conv_w, m_gdn_dt_bias, m_gdn_a_log, m_gdn_norm_w, m_w_out, m_final_norm_w, v_norm_w, v_w_in, v_ssd_conv_w, v_ssd_conv_b, v_ssd_dt_bias, v_ssd_a_log, v_ssd_d, v_ssd_norm_w, v_gdn_conv_w, v_gdn_dt_bias, v_gdn_a_log, v_gdn_norm_w, v_w_out, v_final_norm_w):
    xs = x[0]
    target = loss_target[0]
    chip = 2 * lax.axis_index("x") + lax.axis_index("y")
    w_in_shard, w_out_shard = w_in[0], w_out[0]
    in_cols = w_in_shard.shape[1]
    out_rows = w_out_shard.shape[0]

    g_in, g_out, g_cs, g_cg = gather_weights(
        [w_in_shard.astype(MXU_DTYPE), w_out_shard.astype(MXU_DTYPE)], [ssd_conv_w[0], gdn_conv_w[0]])
    w_in_full = jnp.concatenate([g_in[k] for k in range(N_CHIP)], axis=1)
    w_out_full = g_out.reshape(N_CHIP * out_rows, D_MODEL)
    cw_ssd = jnp.concatenate([g_cs[k] for k in range(N_CHIP)], axis=1)
    cw_gdn = jnp.concatenate([g_cg[k] for k in range(N_CHIP)], axis=1)
    cb_ssd, cb_gdn = ssd_conv_b, jnp.zeros((1, GDN_CONV), F32)
    o_xbc, o_dt, o_gate, o_qkv, o_ab = 1024, 2560, 2576, 3600, 6672
    w_main = jnp.concatenate([w_in_full[:, :o_xbc], w_in_full[:, o_gate:o_qkv], w_in_full[:, o_qkv:o_ab],
                              w_in_full[:, o_xbc:o_dt]], axis=1)
    w_small = jnp.concatenate([w_in_full[:, o_dt:o_gate], w_in_full[:, o_ab:],
                               jnp.zeros((D_MODEL, LANES - 32), MXU_DTYPE)], axis=1)
    alog = _lanes(ssd_a_log, 0) + _lanes(gdn_a_log, LANE_GA)
    dtb = _lanes(ssd_dt_bias, 0) + _lanes(gdn_dt_bias, LANE_GA)
    dvec = _lanes(ssd_d, 0)
    fw = final_norm_w.reshape(1, D_MODEL)

    cw, cb = jnp.concatenate([cw_gdn, cw_ssd], axis=1), jnp.concatenate([cb_gdn, cb_ssd], axis=1)
    proj_main, proj_small, u = in_proj(xs, norm_w, w_main, w_small)
    conv_out, x_conv, dsilu_conv = in_proj_conv(u, w_main, cw, cb)
    y_ssd, hist_ssd = ssd_fwd(conv_out, proj_main, proj_small, ssd_norm_w, alog, dtb, dvec)
    y_gdn, hist_gdn, tinv_gdn = gdn_fwd(conv_out, proj_main, proj_small, gdn_norm_w, alog, dtb)

    loss_blk, dhid, dy_ssd, dy_gdn, d_w_out, d_fw = out_proj_loss(xs, y_ssd, y_gdn, w_out_full, fw, target)
    dconv, dproj_main, dsmall_ssd, d_ssd_nw, d_alog_s, d_dtb_s, d_dvec = ssd_bwd(
        conv_out, proj_main, proj_small, ssd_norm_w, alog, dtb, dvec, hist_ssd, dy_ssd)
    dproj_main, dconv, dsmall_gdn, d_gdn_nw, d_alog_g, d_dtb_g = gdn_bwd(
        dproj_main, dconv, conv_out, proj_main, proj_small, gdn_norm_w, alog, dtb, hist_gdn, tinv_gdn, dy_gdn)
    slabs_out = d_w_out.reshape(N_DEV, out_rows // 2, D_MODEL).astype(COMM_DTYPE)
    dproj_conv, d_w_conv, dwb, (r_out,) = conv_bwd_w(u, x_conv, dsilu_conv, cw, dconv, [slabs_out])
    dwb_gdn, dwb_ssd = dwb[:, :GDN_CONV], dwb[:, GDN_CONV:]
    d_w_zg, d_w_small = in_proj_bwd_w(u, dproj_main, dsmall_ssd, dsmall_gdn)
    order = [(d_w_zg, 0, COL_GATE), (d_w_conv, COL_SSD - COL_CONV, CONV_W), (d_w_small, 0, 16),
             (d_w_zg, COL_GATE, COL_CONV), (d_w_conv, 0, COL_SSD - COL_CONV), (d_w_small, 16, 32)]
    shards, pos = [[] for _ in range(N_CHIP)], 0
    for src, lo, hi in order:
        while lo < hi:
            k = pos // in_cols
            n = min(hi - lo, (k + 1) * in_cols - pos)
            shards[k].append(src[:, lo:lo + n].astype(COMM_DTYPE))
            lo, pos = lo + n, pos + n
    slabs_in = jnp.stack([jnp.concatenate(p, axis=1) for p in shards]).reshape(N_DEV, D_MODEL // 2, in_cols)
    grad_x, d_norm_w, (r_in,) = in_proj_bwd_x(xs, norm_w, w_main, w_small, dproj_main, dproj_conv, dsmall_ssd,
                                               dsmall_gdn, dhid, [slabs_in])
    d_alog, d_dtb = d_alog_s + d_alog_g, d_dtb_s + d_dtb_g
    packed, (o_nw, o_cs, o_cg, o_snw, o_fw, o_al, o_db, o_dv, o_gnw, o_loss) = _pack_cols([
        d_norm_w, dwb_ssd, dwb_gdn,
        d_ssd_nw.reshape(1, SSD_WIDTH), d_fw, d_alog, d_dtb, d_dvec, d_gdn_nw, loss_blk])

    (full_in, full_out), (r_small,) = exchange_halves([r_in, r_out], [packed])
    tot = sum_slabs(r_small, "sum_small")
    grad_w_in = full_in.reshape(D_MODEL, in_cols)
    grad_w_out = full_out.reshape(out_rows, D_MODEL)
    loss = tot[0, o_loss]
    sc, gc = ssd_conv_w.shape[2], gdn_conv_w.shape[2]
    row = lambda off, n, r=0: tot[r:r + 1, off:off + n]
    gs = [row(o_nw, D_MODEL),
          lax.dynamic_slice(tot, (0, o_cs + chip * sc), (4, sc)),
          row(o_cs, SSD_CONV, 4),
          row(o_db, SSD_HEADS), row(o_al, SSD_HEADS), row(o_dv, SSD_HEADS),
          row(o_snw, SSD_WIDTH),
          lax.dynamic_slice(tot, (0, o_cg + chip * gc), (4, gc)),
          row(o_db + LANE_GA, GDN_HEADS), row(o_al + LANE_GA, GDN_HEADS),
          row(o_gnw, GDN_DV), row(o_fw, D_MODEL)]

    names = ["norm_w", "ssd_conv_w", "ssd_conv_b", "ssd_dt_bias", "ssd_a_log", "ssd_d", "ssd_norm_w", "gdn_conv_w",
             "gdn_dt_bias", "gdn_a_log", "gdn_norm_w", "final_norm_w"]
    ws = [norm_w, ssd_conv_w, ssd_conv_b, ssd_dt_bias, ssd_a_log, ssd_d, ssd_norm_w, gdn_conv_w, gdn_dt_bias,
          gdn_a_log, gdn_norm_w, final_norm_w]
    ms = [m_norm_w, m_ssd_conv_w, m_ssd_conv_b, m_ssd_dt_bias, m_ssd_a_log, m_ssd_d, m_ssd_norm_w, m_gdn_conv_w,
          m_gdn_dt_bias, m_gdn_a_log, m_gdn_norm_w, m_final_norm_w]
    vs = [v_norm_w, v_ssd_conv_w, v_ssd_conv_b, v_ssd_dt_bias, v_ssd_a_log, v_ssd_d, v_ssd_norm_w, v_gdn_conv_w,
          v_gdn_dt_bias, v_gdn_a_log, v_gdn_norm_w, v_final_norm_w]
    shapes = [w.shape for w in ws]
    flat = lambda arrs: [a.reshape(g.shape) for a, g in zip(arrs, gs)]
    d_s, m_s, v_s = adamw_many(flat(ws), gs, flat(ms), flat(vs))
    back = lambda arrs: dict(zip(names, [a.reshape(s) for a, s in zip(arrs, shapes)]))
    delta, new_m, new_v, grads = back(d_s), back(m_s), back(v_s), back(gs)
    d_in, m_in, v_in = adamw(w_in, grad_w_in, m_w_in, v_w_in, "adamw_w_in")
    d_out, m_out, v_out = adamw(w_out, grad_w_out, m_w_out, v_w_out, "adamw_w_out")
    for tbl, a_in, a_out in ((grads, grad_w_in[None], grad_w_out[None]), (delta, d_in, d_out), (new_m, m_in, m_out),
                             (new_v, v_in, v_out)):
        tbl["w_in"] = a_in
        tbl["w_out"] = a_out

    order = ["norm_w", "w_in", "ssd_conv_w", "ssd_conv_b", "ssd_dt_bias", "ssd_a_log", "ssd_d", "ssd_norm_w",
             "gdn_conv_w", "gdn_dt_bias", "gdn_a_log", "gdn_norm_w", "w_out", "final_norm_w"]
    return (loss.reshape(()), grad_x[None], *[grads[k] for k in order], *[delta[k] for k in order],
            *[new_m[k] for k in order], *[new_v[k] for k in order])
```

```python
import functools

import jax
import jax.numpy as jnp
from jax import lax
from jax.experimental import pallas as pl
from jax.experimental.pallas import tpu as pltpu

F32 = jnp.float32
MXU_DTYPE = jnp.bfloat16
COMM_DTYPE = jnp.bfloat16
MESH = pl.DeviceIdType.MESH

D_MODEL = 1024
CHUNK = 64
EPS = 1e-6
SSD_HEADS, SSD_GROUPS, SSD_STATE = 16, 2, 128
SSD_WIDTH, SSD_CONV = 1024, 1536
SSD_GW = SSD_WIDTH // SSD_GROUPS
GDN_HEADS, GDN_DK, GDN_DV = 8, 128, 128
GDN_W, GDN_CONV = 1024, 3072
GDN_HC = 2 * GDN_DK + GDN_DV
IN_DIM = 6688
MAIN = 6656
LANES = 128
COL_Z, COL_GATE, COL_GDN, COL_SSD = 0, 1024, 2048, 5120
COL_CONV = COL_GDN
CONV_W = MAIN - COL_CONV
GDN_HB = 8
GDN_CB = 4
SSD_CB = 4
LANE_GA, LANE_GB = 16, 24
N_DEV, N_CHIP = 8, 4
VMEM_LIMIT = 52 * 1024 * 1024

ADAM_LR, ADAM_B1, ADAM_B2, ADAM_EPS, ADAM_WD, ADAM_STEP = 0.001, 0.9, 0.999, 1e-08, 0.01, 10


def _split(a, n):
    parts, rest = [], a.astype(F32)
    for i in range(n):
        p = rest.astype(MXU_DTYPE)
        parts.append(p)
        if i < n - 1:
            rest = rest - p.astype(F32)
    return parts


def _raw_dot(a, b, ca, cb, mode="bf16"):
    d = lambda u, v: lax.dot_general(u, v, (((ca,), (cb,)), ((), ())), preferred_element_type=F32)
    if mode == "bf16":
        return d(a.astype(MXU_DTYPE), b.astype(MXU_DTYPE))
    if mode == "x3":
        (ah, al), (bh, bl) = _split(a, 2), _split(b, 2)
        return d(ah, bh) + (d(ah, bl) + d(al, bh))
    if mode == "sel_a":
        a0 = a.astype(MXU_DTYPE)
        b1, b2, b3 = _split(b, 3)
        return d(a0, b1) + (d(a0, b2) + d(a0, b3))
    assert mode == "sel_b", mode
    b0 = b.astype(MXU_DTYPE)
    a1, a2, a3 = _split(a, 3)
    return d(a1, b0) + (d(a2, b0) + d(a3, b0))


@functools.partial(jax.custom_vjp, nondiff_argnums=(2,))
def mm_nn(a, b, mode="bf16"):
    return _raw_dot(a, b, 1, 0, mode)


@functools.partial(jax.custom_vjp, nondiff_argnums=(2,))
def mm_nt(a, b, mode="bf16"):
    return _raw_dot(a, b, 1, 1, mode)


@functools.partial(jax.custom_vjp, nondiff_argnums=(2,))
def mm_tn(a, b, mode="bf16"):
    return _raw_dot(a, b, 0, 0, mode)


_SAME = {"bf16": ("bf16", "bf16"), "x3": ("x3", "x3")}
_NN_BWD = dict(_SAME, sel_a=("bf16", "sel_a"), sel_b=("sel_b", "bf16"))
_NT_BWD = dict(_SAME, sel_a=("bf16", "sel_b"), sel_b=("sel_b", "bf16"))
_TN_BWD = dict(_SAME, sel_a=("bf16", "sel_a"), sel_b=("sel_a", "bf16"))
mm_nn.defvjp(lambda a, b, m: (_raw_dot(a, b, 1, 0, m), (a, b)),
             lambda m, r, g: (mm_nt(g, r[1], _NN_BWD[m][0]), mm_tn(r[0], g, _NN_BWD[m][1])))
mm_nt.defvjp(lambda a, b, m: (_raw_dot(a, b, 1, 1, m), (a, b)),
             lambda m, r, g: (mm_nn(g, r[1], _NT_BWD[m][0]), mm_tn(g, r[0], _NT_BWD[m][1])))
mm_tn.defvjp(lambda a, b, m: (_raw_dot(a, b, 0, 0, m), (a, b)),
             lambda m, r, g: (mm_nt(r[1], g, _TN_BWD[m][0]), mm_nn(r[0], g, _TN_BWD[m][1])))


@jax.custom_jvp
def sigmoid(x):
    return 1.0 / (1.0 + jnp.exp(-x))


@sigmoid.defjvp
def _sigmoid_jvp(p, t):
    s = sigmoid(p[0])
    return s, t[0] * s * (1.0 - s)


@jax.custom_jvp
def softplus(x):
    return jnp.maximum(x, 0.0) + jnp.log(1.0 + jnp.exp(-jnp.abs(x)))


@softplus.defjvp
def _softplus_jvp(p, t):
    return softplus(p[0]), t[0] * sigmoid(p[0])


def silu(x):
    return x * sigmoid(x)


def rmsnorm(x, w):
    return x * lax.rsqrt(jnp.mean(x * x, axis=-1, keepdims=True) + EPS) * w


def _iota(shape, dim):
    return lax.broadcasted_iota(jnp.int32, shape, dim)


def _halves():
    lane = _iota((1, LANES), 1) >> 6
    return _ind(lane == 0), _ind(lane == 1)


def _block_diag(pair):
    h0, h1 = _halves()
    return jnp.concatenate([pair * h0, pair * h1], axis=0)


def _tri_inv_impl(mats):
    r, c = _iota((CHUNK, LANES), 0), _iota((CHUNK, LANES), 1) & (CHUNK - 1)
    eye = _ind(r == c)
    blockdiag = _ind((r >> 4) == (c >> 4))
    dot = lambda u, v: _raw_dot(u, _block_diag(v), 1, 0, "x3")
    dot1 = lambda u, v: _raw_dot(u, _block_diag(v), 1, 0)
    each = lambda f, *ls: [f(*xs) for xs in zip(*ls)]
    dg = each(lambda a: a * blockdiag, mats)
    off = each(lambda a, d: a - d, mats, dg)
    m = each(lambda d: -d, dg)
    p = each(lambda x: eye + x, m)
    pw = m
    for _ in range(3):
        pw = each(lambda x: dot1(x, x), pw)
        p = each(lambda x, y: x + dot1(x, y), p, pw)
    e = each(dot, p, off)
    e2 = each(lambda x: dot1(x, x), e)
    q = each(lambda x: eye - x, e)
    q = each(lambda x, y: x + dot1(x, y), q, e2)
    return each(dot, q, p)


def _tri_inv_bwd(ts, gs):
    h0, h1 = _halves()
    x = [mm_nt(g, _block_diag(t)) for g, t in zip(gs, ts)]
    full = [mm_tn(t, y) for t, y in zip(ts, x)]
    return [-(f[:CHUNK] * h0 + f[CHUNK:] * h1) for f in full]


@jax.custom_vjp
def tri_inv(mats):
    return _tri_inv_impl(mats)


def _tri_inv_fwd(mats):
    ts = _tri_inv_impl(mats)
    return ts, ts


tri_inv.defvjp(_tri_inv_fwd, lambda ts, gs: (_tri_inv_bwd(ts, gs),))


@jax.custom_vjp
def tri_inv_saved(mats, ts):
    del mats
    return ts


tri_inv_saved.defvjp(lambda mats, ts: (ts, ts),
                     lambda ts, gs: (_tri_inv_bwd(ts, gs), [jnp.zeros_like(t) for t in ts]))


def _ind(cond):
    return jnp.where(cond, 1.0, 0.0).astype(F32)


def _chunk_masks():
    r, c = _iota((CHUNK, CHUNK), 0), _iota((CHUNK, CHUNK), 1)
    return _ind(r >= c), _ind(r > c), _ind(r == c), _ind(_iota((CHUNK, 1), 0) == CHUNK - 1)


def _log_decay_cumsum(small, alog, dtb, tri):
    sp = softplus(small + dtb)
    la = -jnp.exp(alog) * sp
    return sp, mm_nn(tri, la, "sel_a")


def _col_of(x, lane):
    return jnp.sum(x * _ind(_iota((1, LANES), 1) == lane), axis=1, keepdims=True)


def _pair_masks():
    r, c = _iota((CHUNK, LANES), 0), _iota((CHUNK, LANES), 1)
    c6 = c & (CHUNK - 1)
    return _ind(r >= c6), _ind(r > c6), (_ind(c == r), _ind(c == r + CHUNK))


def _decay_pair(col_a, col_b, tri_w, eye_w):
    h0, h1 = _halves()
    col = col_a * h0 + col_b * h1
    row = jnp.sum(col_a * eye_w[0] + col_b * eye_w[1], axis=0, keepdims=True)
    return jnp.exp((col - row) * tri_w) * tri_w


def gdn_chunk(h0, qs, ks, vs, smalls, gates, normw, alog, dtb, states, saved_t=None):
    tri, _, _, last = _chunk_masks()
    tri_w, strict_w, eye_w = _pair_masks()
    nh = len(qs[0])
    flat = lambda xss: [x for xs in xss for x in xs]
    lacs = [_log_decay_cumsum(sm, alog, dtb, tri)[1] for sm in smalls]
    qs, ks, vs, gates = flat(qs), flat(ks), flat(vs), flat(gates)
    heads, pairs = range(len(qs)), range(len(qs) // 2)
    each = lambda f, *ls: [f(*xs) for xs in zip(*ls)]
    ab = lambda xs, p: (xs[2 * p], xs[2 * p + 1])
    stack = lambda xs: jnp.concatenate(xs, axis=0)
    gc = [_col_of(lacs[i // nh], LANE_GA + h0 + i % nh) for i in heads]
    beta = [sigmoid(_col_of(smalls[i // nh], LANE_GB + h0 + i % nh)) for i in heads]
    decay = [_decay_pair(*ab(gc, p), tri_w, eye_w) for p in pairs]
    gl = each(lambda x: jnp.sum(x * last, axis=0, keepdims=True), gc)
    q = each(lambda x: x * lax.rsqrt(jnp.sum(x * x, axis=-1, keepdims=True) + EPS) * (GDN_DK ** -0.5), qs)
    k = each(lambda x: x * lax.rsqrt(jnp.sum(x * x, axis=-1, keepdims=True) + EPS), ks)
    kb = each(lambda x, b: x * b, k, beta)
    eg = each(jnp.exp, gc)
    zero = jnp.zeros((CHUNK, GDN_DK), F32)
    k_bd = [stack([join_lanes([k[2 * p], zero]), join_lanes([zero, k[2 * p + 1]])]) for p in pairs]
    a = [mm_nt(join_lanes(list(ab(kb, p))), k_bd[p]) * (decay[p] * strict_w) for p in pairs]
    t = tri_inv(a) if saved_t is None else tri_inv_saved(a, saved_t)
    attn = [mm_nt(join_lanes(list(ab(q, p))), k_bd[p]) * decay[p] for p in pairs]
    rhs = [stack([join_lanes([vs[h] * beta[h], kb[h] * eg[h]]) for h in (2 * p, 2 * p + 1)]) for p in pairs]
    uw = [mm_nn(_block_diag(t[p]), rhs[p]) for p in pairs]
    uw = [x for p in pairs for x in split_rows(uw[p])]
    u, w = zip(*[split_lanes(x) for x in uw])
    ys = []
    for c in range(len(smalls)):
        hs = range(c * nh, (c + 1) * nh)
        v_new = [u[i] - mm_nn(w[i], states[i % nh]) for i in hs]
        av = [mm_nn(_block_diag(attn[c * nh // 2 + p]), stack(list(ab(v_new, p)))) for p in range(nh // 2)]
        av = [x for y in av for x in split_rows(y)]
        o = [mm_nn(q[i] * eg[i], states[i % nh]) + av[i % nh] for i in hs]
        states = [states[i % nh] * jnp.exp(gl[i]) + mm_tn(k[i] * jnp.exp(gl[i] - gc[i]), v_new[i % nh]) for i in hs]
        ys.append([rmsnorm(o[i % nh], normw) * silu(gates[i]) for i in hs])
    return ys, states, t


@jax.custom_vjp
def split_rows(x):
    n = x.shape[0] // 2
    return [x[:n], x[n:]]


split_rows.defvjp(lambda x: (split_rows(x), None), lambda _, gs: (jnp.concatenate(gs, axis=0),))


@jax.custom_vjp
def split_lanes(x):
    return [x[:, i * LANES:(i + 1) * LANES] for i in range(x.shape[1] // LANES)]


@jax.custom_vjp
def join_lanes(xs):
    return jnp.concatenate(xs, axis=1)


split_lanes.defvjp(lambda x: (split_lanes(x), None), lambda _, gs: (join_lanes(gs),))
join_lanes.defvjp(lambda xs: (join_lanes(xs), None), lambda _, g: (split_lanes(g),))


def ssd_chunk(xs, bm, cm, z, smalls, normw, alog, dtb, dvec, state):
    tri, _, _, last = _chunk_masks()
    tri_w, _, eye_w = _pair_masks()
    h0, h1 = _halves()
    hpg = SSD_HEADS // SSD_GROUPS
    ng = len(normw)
    flat = lambda xss: [x for xs_ in xss for x in xs_]
    xs, bm, cm, z = flat(xs), flat(bm), flat(cm), flat(z)
    units, pairs = range(len(xs)), range(hpg // 2)
    each = lambda f, *ls: [f(*a) for a in zip(*ls)]
    sp_lac = [_log_decay_cumsum(sm, alog, dtb, tri) for sm in smalls]
    lac_last = [jnp.sum(lac * last, axis=0, keepdims=True) for _, lac in sp_lac]
    sel = [_ind(_iota((LANES, SSD_GW), 0) == g * hpg + (_iota((LANES, SSD_GW), 1) >> 6)) for g in range(ng)]
    expand = lambda vs, mode: [mm_nn(vs[i // ng], sel[i % ng], mode) for i in units]
    dt_e = expand([sp for sp, _ in sp_lac], "bf16")
    elac_e = expand([jnp.exp(lac) for _, lac in sp_lac], "bf16")
    toend_e = expand([jnp.exp(ll - lac) for (_, lac), ll in zip(sp_lac, lac_last)], "bf16")
    row8 = _iota((8, 1), 0)
    two_e = expand([_ind(row8 == 0) * dvec + _ind(row8 == 1) * jnp.exp(ll) for ll in lac_last], "sel_b")
    d_e = each(lambda v: jnp.sum(v * _ind(row8 == 0), axis=0, keepdims=True), two_e)
    chunk_e = each(lambda v: jnp.sum(v * _ind(row8 == 1), axis=0, keepdims=True), two_e)
    xdt = each(lambda a, b: a * b, xs, dt_e)
    cb_w = each(lambda c_, b_: mm_nt(c_, jnp.concatenate([b_, b_], axis=0)), cm, bm)
    x_pairs = each(split_lanes, xdt)
    col = lambda i, j: _col_of(sp_lac[i // ng][1], (i % ng) * hpg + j)
    lms = [[_decay_pair(col(i, 2 * p), col(i, 2 * p + 1), tri_w, eye_w) for p in pairs] for i in units]
    stacked = [[jnp.concatenate([x_pairs[i][p] * h0, x_pairs[i][p] * h1], axis=0) for p in pairs] for i in units]
    terms = [[mm_nn(cb_w[i] * lms[i][p], stacked[i][p]) for p in pairs] for i in units]
    y_in = [join_lanes(terms[i]) + xs[i] * d_e[i] for i in units]
    state_in = each(lambda b_, xd, te: mm_tn(b_, xd * te), bm, xdt, toend_e)
    outs = []
    for c in range(len(smalls)):
        us = range(c * ng, (c + 1) * ng)
        y = [mm_nn(cm[i], state[i % ng]) * elac_e[i] + y_in[i] for i in us]
        state = [state[i % ng] * chunk_e[i] + state_in[i] for i in us]
        outs.append([rmsnorm(y[i % ng] * silu(z[i]), normw[i % ng]) for i in us])
    return outs, state


def _params(sem=None):
    return pltpu.CompilerParams(dimension_semantics=sem, vmem_limit_bytes=VMEM_LIMIT)


def _full(shape):
    n = len(shape)
    return pl.BlockSpec(shape, lambda *_: (0,) * n)


ANY = pl.BlockSpec(memory_space=pl.ANY)
HBM = pl.BlockSpec(memory_space=pltpu.HBM)


def in_proj(x, normw, w_main, w_small):
    t = x.shape[0]
    tm, tn = min(2048, t), 512

    def body(x_ref, nw_ref, wm_ref, ws_ref, pm_ref, ps_ref, u_ref):
        @pl.when(pl.program_id(1) == 0)
        def _():
            u = rmsnorm(x_ref[...], nw_ref[...]).astype(MXU_DTYPE)
            u_ref[...] = u
            ps_ref[...] = _raw_dot(u, ws_ref[...], 1, 0)
        pm_ref[...] = _raw_dot(u_ref[...], wm_ref[...], 1, 0)

    return pl.pallas_call(
        body, name="in_proj", grid=(t // tm, COL_CONV // tn),
        in_specs=[pl.BlockSpec((tm, D_MODEL), lambda i, j: (i, 0)), _full((1, D_MODEL)),
                  pl.BlockSpec((D_MODEL, tn), lambda i, j: (0, j)), _full((D_MODEL, LANES))],
        out_specs=[pl.BlockSpec((tm, tn), lambda i, j: (i, j)), pl.BlockSpec((tm, LANES), lambda i, j: (i, 0)),
                   pl.BlockSpec((tm, D_MODEL), lambda i, j: (i, 0))],
        out_shape=[jax.ShapeDtypeStruct((t, COL_CONV), F32), jax.ShapeDtypeStruct((t, LANES), F32),
                   jax.ShapeDtypeStruct((t, D_MODEL), MXU_DTYPE)],
        compiler_params=_params(("arbitrary", "arbitrary")),
    )(x, normw, w_main, w_small)


CONV_TC = 512
HALO = 8


def _shift_down(cur, prev, s):
    rolled = pltpu.roll(cur, s, 0)
    top = jnp.where(_iota((HALO, cur.shape[1]), 0) < s, pltpu.roll(prev, s, 0), rolled[:HALO])
    if cur.shape[0] == HALO:
        return top
    return jnp.concatenate([top, rolled[HALO:]], axis=0)


def _shift_up(cur, nxt, s):
    n = cur.shape[0]
    rolled = pltpu.roll(cur, n - s, 0)
    bot = jnp.where(_iota((HALO, cur.shape[1]), 0) >= HALO - s, pltpu.roll(nxt, HALO - s, 0), rolled[n - HALO:])
    return jnp.concatenate([rolled[:n - HALO], bot], axis=0)


def _conv_pre(cur, prev, w_ref, b, cols=slice(None)):
    acc = cur * w_ref[3:4, cols] + b
    shifted = [cur]
    for s in (1, 2, 3):
        sh = _shift_down(cur, prev, s)
        shifted.append(sh)
        acc = acc + sh * w_ref[3 - s:4 - s, cols]
    return acc, shifted


def in_proj_conv(u, w_main, w, b):
    t = u.shape[0]
    tm, tn = min(2048, t), CONV_TC
    rc = min(128, tm)
    c0, nj = COL_CONV // tn, CONV_W // tn

    def body(u_ref, wm_ref, w_ref, b_ref, out_ref, x_ref, ds_ref, halo_ref):
        j = pl.program_id(1)

        @pl.when(pl.program_id(0) == 0)
        def _():
            halo_ref[j] = jnp.zeros((HALO, tn), F32)

        prev = halo_ref[j]
        for r in range(tm // rc):
            rows = pl.ds(r * rc, rc)
            p = _raw_dot(u_ref[rows, :], wm_ref[...], 1, 0)
            x_ref[rows, :] = p.astype(x_ref.dtype)
            pre, _ = _conv_pre(p, prev, w_ref, b_ref[...])
            sg = sigmoid(pre)
            out_ref[rows, :] = pre * sg
            ds_ref[rows, :] = (sg * (1.0 + pre * (1.0 - sg))).astype(ds_ref.dtype)
            prev = p[rc - HALO:]
        halo_ref[j] = prev

    blk = pl.BlockSpec((tm, tn), lambda i, j: (i, j))
    return pl.pallas_call(
        body, name="in_proj_conv", grid=(t // tm, nj),
        in_specs=[pl.BlockSpec((tm, D_MODEL), lambda i, j: (i, 0)),
                  pl.BlockSpec((D_MODEL, tn), lambda i, j: (0, c0 + j)),
                  pl.BlockSpec((4, tn), lambda i, j: (0, j)), pl.BlockSpec((1, tn), lambda i, j: (0, j))],
        out_specs=[blk, blk, blk],
        out_shape=[jax.ShapeDtypeStruct((t, CONV_W), F32), jax.ShapeDtypeStruct((t, CONV_W), MXU_DTYPE),
                   jax.ShapeDtypeStruct((t, CONV_W), MXU_DTYPE)],
        scratch_shapes=[pltpu.VMEM((nj, HALO, tn), F32)],
        compiler_params=_params(("arbitrary", "arbitrary")),
    )(u, w_main, w, b)


def conv_bwd_w(u, x_conv, dsilu, w, dout, slabbed):
    t = u.shape[0]
    tt, tn = min(512, t), 3 * CONV_TC
    nt, nj = t // tt, CONV_W // tn
    ns = len(slabbed)
    halo_op = 2 * HALO
    after = lambda i, h: jnp.minimum((i + 1) * (tt // h), t // h - 1)

    def body(u_ref, x_ref, ds_ref, ds_nxt_ref, w_ref, do_ref, do_nxt_ref, *rest):
        slab_refs, (dx_ref, dw_ref, dwb_ref) = rest[:ns], rest[ns:ns + 3]
        land_refs, sems = rest[ns + 3:2 * ns + 3], rest[2 * ns + 3:]
        j, i = pl.program_id(0), pl.program_id(1)
        start, finish = _slab_exchange(slab_refs, land_refs, ns, *sems)

        @pl.when(jnp.logical_and(j == 0, i == 0))
        def _():
            start()

        @pl.when(i == 0)
        def _():
            dw_ref[...] = jnp.zeros(dw_ref.shape, F32)
            dwb_ref[...] = jnp.zeros(dwb_ref.shape, F32)

        uu = u_ref[...]
        row = _iota((HALO, CONV_TC), 0)
        last = i == nt - 1
        for piece in range(tn // CONV_TC):
            cols = slice(piece * CONV_TC, (piece + 1) * CONV_TC)
            x = x_ref[:, cols].astype(F32)
            dpre = do_ref[:, cols] * ds_ref[:, cols].astype(F32)
            dpre_nxt = jnp.where(last, 0.0, do_nxt_ref[:, cols] * ds_nxt_ref[:, cols].astype(F32)[:HALO])
            ups = [dpre] + [_shift_up(dpre, dpre_nxt, s) for s in (1, 2, 3)]
            dx = ups[0] * w_ref[3:4, cols]
            upd = jnp.where(row == 4, jnp.sum(dpre, axis=0, keepdims=True), 0.0)
            for s in range(4):
                if s:
                    dx = dx + ups[s] * w_ref[3 - s:4 - s, cols]
                upd = upd + jnp.where(row == 3 - s, jnp.sum(ups[s] * x, axis=0, keepdims=True), 0.0)
            dx = dx.astype(dx_ref.dtype)
            dx_ref[:, cols] = dx
            dw_ref[:, cols] += _raw_dot(uu, dx, 0, 0)
            dwb_ref[:, cols] += upd

        @pl.when(jnp.logical_and(j == nj - 1, last))
        def _():
            finish()

    out = pl.pallas_call(
        body, name="conv_bwd_w", grid=(nj, nt),
        in_specs=[pl.BlockSpec((tt, D_MODEL), lambda j, i: (i, 0)),
                  pl.BlockSpec((tt, tn), lambda j, i: (i, j)),
                  pl.BlockSpec((tt, tn), lambda j, i: (i, j)),
                  pl.BlockSpec((halo_op, tn), lambda j, i: (after(i, halo_op), j)),
                  pl.BlockSpec((4, tn), lambda j, i: (0, j)),
                  pl.BlockSpec((tt, tn), lambda j, i: (i, j)),
                  pl.BlockSpec((HALO, tn), lambda j, i: (after(i, HALO), j))] + [HBM] * ns,
        out_specs=[pl.BlockSpec((tt, tn), lambda j, i: (i, j)), pl.BlockSpec((D_MODEL, tn), lambda j, i: (0, j)),
                   pl.BlockSpec((HALO, tn), lambda j, i: (0, j))] + [HBM] * ns,
        out_shape=[jax.ShapeDtypeStruct((t, CONV_W), MXU_DTYPE), jax.ShapeDtypeStruct((D_MODEL, CONV_W), F32),
                   jax.ShapeDtypeStruct((HALO, CONV_W), F32)] + _slab_exchange_shapes(slabbed, []),
        scratch_shapes=_slab_exchange_sems(ns),
        compiler_params=_params(("arbitrary", "arbitrary")),
    )(u, x_conv, dsilu, dsilu, w, dout, dout, *slabbed)
    return out[0], out[1], out[2], out[3:]


def _ssd_cols(g):
    b0 = SSD_WIDTH + g * SSD_STATE
    c0 = SSD_WIDTH + SSD_GROUPS * SSD_STATE + g * SSD_STATE
    return slice(g * SSD_GW, (g + 1) * SSD_GW), slice(b0, b0 + SSD_STATE), slice(c0, c0 + SSD_STATE)


def _gdn_cols(j):
    return tuple(slice(s * GDN_W + j * GDN_DK, s * GDN_W + (j + 1) * GDN_DK) for s in range(3))


def _ssd_parts(xbc_ref):
    return tuple([[xbc_ref[_chunk_rows(c), _ssd_cols(g)[s]] for g in range(SSD_GROUPS)] for c in range(SSD_CB)]
                 for s in range(3))


def _group_cols(ref):
    return [[ref[_chunk_rows(c), g * SSD_GW:(g + 1) * SSD_GW] for g in range(SSD_GROUPS)] for c in range(SSD_CB)]


def _chunk_rows(c):
    return slice(c * CHUNK, (c + 1) * CHUNK)


def _gdn_parts(qkv_ref):
    assert GDN_HB == GDN_HEADS, "the conv block is read whole: one grid step holds every head"
    return tuple([[qkv_ref[_chunk_rows(c), _gdn_cols(j)[s]] for j in range(GDN_HB)] for c in range(GDN_CB)]
                 for s in range(3))


def _head_cols(ref):
    return [[ref[_chunk_rows(c), j * GDN_DV:(j + 1) * GDN_DV] for j in range(GDN_HB)] for c in range(GDN_CB)]


def _chunk_blocks(ref, n=GDN_CB):
    return [ref[_chunk_rows(c), :] for c in range(n)]


def _first_head():
    return 0 if GDN_HB == GDN_HEADS else pl.program_id(1) * GDN_HB


def ssd_fwd(conv_ssd, proj_main, proj_small, normw, alog, dtb, dvec):
    t = conv_ssd.shape[0]
    rows = CHUNK * SSD_CB
    nc = t // rows
    groups = range(SSD_GROUPS)
    norm_cols = lambda ref: [ref[:, g * SSD_GW:(g + 1) * SSD_GW] for g in groups]

    def body(xbc_ref, z_ref, sm_ref, nw_ref, al_ref, db_ref, dv_ref, y_ref, hist_ref, state_ref):
        @pl.when(pl.program_id(0) == 0)
        def _():
            state_ref[...] = jnp.zeros(state_ref.shape, F32)

        states = [state_ref[g] for g in groups]
        for g in groups:
            hist_ref[0, g] = states[g]
        ys, new_states = ssd_chunk(*_ssd_parts(xbc_ref), _group_cols(z_ref), _chunk_blocks(sm_ref, SSD_CB),
                                   norm_cols(nw_ref), al_ref[...], db_ref[...], dv_ref[...], states)
        for c in range(SSD_CB):
            for g in groups:
                y_ref[_chunk_rows(c), g * SSD_GW:(g + 1) * SSD_GW] = ys[c][g].astype(MXU_DTYPE)
        for g in groups:
            state_ref[g] = new_states[g]

    return pl.pallas_call(
        body, name="ssd_fwd", grid=(nc,),
        in_specs=[pl.BlockSpec((rows, SSD_CONV), lambda c: (c, (COL_SSD - COL_CONV) // SSD_CONV)),
                  pl.BlockSpec((rows, SSD_WIDTH), lambda c: (c, COL_Z // SSD_WIDTH)),
                  pl.BlockSpec((rows, LANES), lambda c: (c, 0)),
                  _full((1, SSD_WIDTH)), _full((1, LANES)), _full((1, LANES)), _full((1, LANES))],
        out_specs=[pl.BlockSpec((rows, SSD_WIDTH), lambda c: (c, 0)),
                   pl.BlockSpec((1, SSD_GROUPS, SSD_STATE, SSD_GW), lambda c: (c, 0, 0, 0))],
        out_shape=[jax.ShapeDtypeStruct((t, SSD_WIDTH), MXU_DTYPE),
                   jax.ShapeDtypeStruct((nc, SSD_GROUPS, SSD_STATE, SSD_GW), F32)],
        scratch_shapes=[pltpu.VMEM((SSD_GROUPS, SSD_STATE, SSD_GW), F32)],
        compiler_params=_params(("arbitrary",)),
    )(conv_ssd, proj_main, proj_small, normw, alog, dtb, dvec)


def _accumulate(ref, first, value):
    @pl.when(first)
    def _():
        ref[...] = value

    @pl.when(jnp.logical_not(first))
    def _():
        ref[...] += value


def ssd_bwd(conv_ssd, proj_main, proj_small, normw, alog, dtb, dvec, hist, dy):
    t = conv_ssd.shape[0]
    rows = CHUNK * SSD_CB
    nc = t // rows
    rev = lambda c: nc - 1 - c
    groups = range(SSD_GROUPS)
    norm_cols = lambda ref: [ref[:, g * SSD_GW:(g + 1) * SSD_GW] for g in groups]

    def body(xbc_ref, z_ref, sm_ref, nw_ref, al_ref, db_ref, dv_ref, hist_ref, dy_ref,
             dxbc_ref, dz_ref, dsm_ref, dnw_ref, dal_ref, ddb_ref, ddv_ref, dstate_ref):
        first = pl.program_id(0) == 0

        @pl.when(first)
        def _():
            dstate_ref[...] = jnp.zeros(dstate_ref.shape, F32)

        _, vjp = jax.vjp(ssd_chunk, *_ssd_parts(xbc_ref), _group_cols(z_ref), _chunk_blocks(sm_ref, SSD_CB),
                         norm_cols(nw_ref), al_ref[...], db_ref[...], dv_ref[...], [hist_ref[0, g] for g in groups])
        dxs, dbm, dcm, dz, dsm, dnw, dal, ddb, ddv, dstate = vjp(
            (_group_cols(dy_ref), [dstate_ref[g] for g in groups]))
        for k in range(SSD_CB):
            rk = _chunk_rows(k)
            for g in groups:
                xc, bc, cc = _ssd_cols(g)
                dxbc_ref[rk, xc] = dxs[k][g]
                dxbc_ref[rk, bc] = dbm[k][g]
                dxbc_ref[rk, cc] = dcm[k][g]
                dz_ref[rk, g * SSD_GW:(g + 1) * SSD_GW] = dz[k][g].astype(dz_ref.dtype)
            dsm_ref[rk, :] = dsm[k]
        for g in groups:
            dstate_ref[g] = dstate[g]
        _accumulate(dnw_ref, first, join_lanes(dnw))
        _accumulate(dal_ref, first, dal)
        _accumulate(ddb_ref, first, ddb)
        _accumulate(ddv_ref, first, ddv)

    return pl.pallas_call(
        body, name="ssd_bwd", grid=(nc,),
        in_specs=[pl.BlockSpec((rows, SSD_CONV), lambda c: (rev(c), (COL_SSD - COL_CONV) // SSD_CONV)),
                  pl.BlockSpec((rows, SSD_WIDTH), lambda c: (rev(c), COL_Z // SSD_WIDTH)),
                  pl.BlockSpec((rows, LANES), lambda c: (rev(c), 0)),
                  _full((1, SSD_WIDTH)), _full((1, LANES)), _full((1, LANES)), _full((1, LANES)),
                  pl.BlockSpec((1, SSD_GROUPS, SSD_STATE, SSD_GW), lambda c: (rev(c), 0, 0, 0)),
                  pl.BlockSpec((rows, SSD_WIDTH), lambda c: (rev(c), 0))],
        out_specs=[pl.BlockSpec((rows, SSD_CONV), lambda c: (rev(c), (COL_SSD - COL_CONV) // SSD_CONV)),
                   pl.BlockSpec((rows, SSD_WIDTH), lambda c: (rev(c), COL_Z // SSD_WIDTH)),
                   pl.BlockSpec((rows, LANES), lambda c: (rev(c), 0)),
                   _full((1, SSD_WIDTH)), _full((1, LANES)), _full((1, LANES)), _full((1, LANES))],
        out_shape=[jax.ShapeDtypeStruct((t, CONV_W), F32), jax.ShapeDtypeStruct((t, COL_CONV), MXU_DTYPE),
                   jax.ShapeDtypeStruct((t, LANES), F32), jax.ShapeDtypeStruct((1, SSD_WIDTH), F32),
                   jax.ShapeDtypeStruct((1, LANES), F32), jax.ShapeDtypeStruct((1, LANES), F32),
                   jax.ShapeDtypeStruct((1, LANES), F32)],
        scratch_shapes=[pltpu.VMEM((SSD_GROUPS, SSD_STATE, SSD_GW), F32)],
        compiler_params=_params(("arbitrary",)),
    )(conv_ssd, proj_main, proj_small, normw, alog, dtb, dvec, hist, dy)


def gdn_fwd(conv_gdn, proj_main, proj_small, normw, alog, dtb):
    t = conv_gdn.shape[0]
    hb, cb = GDN_HB, GDN_CB
    rows = CHUNK * cb
    ns = t // rows
    gate_blk = COL_GATE // (GDN_DV * hb)

    def body(qkv_ref, gate_ref, sm_ref, nw_ref, al_ref, db_ref, y_ref, hist_ref, t_ref, state_ref):
        h0 = _first_head()

        @pl.when(pl.program_id(0) == 0)
        def _():
            for j in range(hb):
                state_ref[h0 + j] = jnp.zeros((GDN_DK, GDN_DV), F32)

        states = [state_ref[h0 + j] for j in range(hb)]
        for j in range(hb):
            hist_ref[0, j] = states[j]
        qs, ks, vs = _gdn_parts(qkv_ref)
        ys, new_states, ts = gdn_chunk(h0, qs, ks, vs, _chunk_blocks(sm_ref), _head_cols(gate_ref), nw_ref[...],
                                       al_ref[...], db_ref[...], states)
        for c in range(cb):
            for j in range(hb):
                y_ref[_chunk_rows(c), j * GDN_DV:(j + 1) * GDN_DV] = ys[c][j].astype(MXU_DTYPE)
        for j in range(hb):
            state_ref[h0 + j] = new_states[j]
        for p in range(cb * hb // 2):
            t_ref[0, p] = ts[p]

    return pl.pallas_call(
        body, name="gdn_fwd", grid=(ns, GDN_HEADS // hb),
        in_specs=[pl.BlockSpec((rows, GDN_HC * hb), lambda c, h: (c, h)),
                  pl.BlockSpec((rows, GDN_DV * hb), lambda c, h: (c, gate_blk + h)),
                  pl.BlockSpec((rows, LANES), lambda c, h: (c, 0)),
                  _full((1, GDN_DV)), _full((1, LANES)), _full((1, LANES))],
        out_specs=[pl.BlockSpec((rows, GDN_DV * hb), lambda c, h: (c, h)),
                   pl.BlockSpec((1, hb, GDN_DK, GDN_DV), lambda c, h: (c, h, 0, 0)),
                   pl.BlockSpec((1, cb * hb // 2, CHUNK, LANES), lambda c, h: (c, h, 0, 0))],
        out_shape=[jax.ShapeDtypeStruct((t, GDN_W), MXU_DTYPE),
                   jax.ShapeDtypeStruct((ns, GDN_HEADS, GDN_DK, GDN_DV), F32),
                   jax.ShapeDtypeStruct((ns, cb * GDN_HEADS // 2, CHUNK, LANES), F32)],
        scratch_shapes=[pltpu.VMEM((GDN_HEADS, GDN_DK, GDN_DV), F32)],
        compiler_params=_params(("arbitrary", "arbitrary")),
    )(conv_gdn, proj_main, proj_small, normw, alog, dtb)


def gdn_bwd(dproj_main, dconv, conv_gdn, proj_main, proj_small, normw, alog, dtb, hist, t_inv, dy):
    t = conv_gdn.shape[0]
    hb, cb = GDN_HB, GDN_CB
    rows = CHUNK * cb
    ns = t // rows
    rev = lambda c: ns - 1 - c
    gate_blk = COL_GATE // (GDN_DV * hb)

    def body(alias_ref, alias2_ref, qkv_ref, gate_ref, sm_ref, nw_ref, al_ref, db_ref, hist_ref, t_ref, dy_ref,
             dgate_ref, dqkv_ref, dsm_ref, dnw_ref, dal_ref, ddb_ref, dstate_ref):
        del alias_ref, alias2_ref
        c, h = pl.program_id(0), pl.program_id(1)
        h0 = _first_head()

        @pl.when(c == 0)
        def _():
            for j in range(hb):
                dstate_ref[h0 + j] = jnp.zeros((GDN_DK, GDN_DV), F32)

        saved = [t_ref[0, p] for p in range(cb * hb // 2)]

        def fn(qs, ks, vs, smalls, gates, nw, al, db, states):
            return gdn_chunk(h0, qs, ks, vs, smalls, gates, nw, al, db, states, saved)[:2]

        qs, ks, vs = _gdn_parts(qkv_ref)
        _, vjp = jax.vjp(fn, qs, ks, vs, _chunk_blocks(sm_ref), _head_cols(gate_ref), nw_ref[...], al_ref[...],
                         db_ref[...], [hist_ref[0, j] for j in range(hb)])
        dqs, dks, dvs, dsm, dgates, dnw, dal, ddb, dstates = vjp(
            (_head_cols(dy_ref), [dstate_ref[h0 + j] for j in range(hb)]))
        for k in range(cb):
            rk = _chunk_rows(k)
            for j in range(hb):
                qc, kc, vc = _gdn_cols(j)
                dqkv_ref[rk, qc] = dqs[k][j]
                dqkv_ref[rk, kc] = dks[k][j]
                dqkv_ref[rk, vc] = dvs[k][j]
                dgate_ref[rk, j * GDN_DV:(j + 1) * GDN_DV] = dgates[k][j].astype(dgate_ref.dtype)
        for j in range(hb):
            dstate_ref[h0 + j] = dstates[j]
        _accumulate(dsm_ref, h == 0, jnp.concatenate(dsm, axis=0))
        first = jnp.logical_and(c == 0, h == 0)
        _accumulate(dnw_ref, first, dnw)
        _accumulate(dal_ref, first, dal)
        _accumulate(ddb_ref, first, ddb)

    return pl.pallas_call(
        body, name="gdn_bwd", grid=(ns, GDN_HEADS // hb),
        in_specs=[ANY, ANY, pl.BlockSpec((rows, GDN_HC * hb), lambda c, h: (rev(c), h)),
                  pl.BlockSpec((rows, GDN_DV * hb), lambda c, h: (rev(c), gate_blk + h)),
                  pl.BlockSpec((rows, LANES), lambda c, h: (rev(c), 0)),
                  _full((1, GDN_DV)), _full((1, LANES)), _full((1, LANES)),
                  pl.BlockSpec((1, hb, GDN_DK, GDN_DV), lambda c, h: (rev(c), h, 0, 0)),
                  pl.BlockSpec((1, cb * hb // 2, CHUNK, LANES), lambda c, h: (rev(c), h, 0, 0)),
                  pl.BlockSpec((rows, GDN_DV * hb), lambda c, h: (rev(c), h))],
        out_specs=[pl.BlockSpec((rows, GDN_DV * hb), lambda c, h: (rev(c), gate_blk + h)),
                   pl.BlockSpec((rows, GDN_HC * hb), lambda c, h: (rev(c), h)),
                   pl.BlockSpec((rows, LANES), lambda c, h: (rev(c), 0)),
                   _full((1, GDN_DV)), _full((1, LANES)), _full((1, LANES))],
        out_shape=[jax.ShapeDtypeStruct(dproj_main.shape, dproj_main.dtype),
                   jax.ShapeDtypeStruct(dconv.shape, dconv.dtype),
                   jax.ShapeDtypeStruct((t, LANES), F32), jax.ShapeDtypeStruct((1, GDN_DV), F32),
                   jax.ShapeDtypeStruct((1, LANES), F32), jax.ShapeDtypeStruct((1, LANES), F32)],
        scratch_shapes=[pltpu.VMEM((GDN_HEADS, GDN_DK, GDN_DV), F32)],
        input_output_aliases={0: 0, 1: 1},
        compiler_params=_params(("arbitrary", "arbitrary")),
    )(dproj_main, dconv, conv_gdn, proj_main, proj_small, normw, alog, dtb, hist, t_inv, dy)


def out_proj_loss(x, y_ssd, y_gdn, w_out, final_w, target):
    t = x.shape[0]
    tm = min(512, t)

    def body(x_ref, ys_ref, yg_ref, wo_ref, fw_ref, tg_ref, loss_ref, dhid_ref, dys_ref, dyg_ref, dwo_ref, dfw_ref):
        i = pl.program_id(0)
        ys, yg = ys_ref[...], yg_ref[...]
        wo_s, wo_g = wo_ref[:SSD_WIDTH, :], wo_ref[SSD_WIDTH:, :]
        hid = x_ref[...] + _raw_dot(ys, wo_s, 1, 0) + _raw_dot(yg, wo_g, 1, 0)
        out, vjp = jax.vjp(rmsnorm, hid, fw_ref[...])
        err = out - tg_ref[...]
        loss = 0.5 * jnp.sum(jnp.mean(err * err, axis=-1, keepdims=True), axis=0, keepdims=True)
        dhid, dfw = vjp(err * (1.0 / D_MODEL))
        dhid_ref[...] = dhid
        dys_ref[...] = _raw_dot(dhid, wo_s, 1, 1)
        dyg_ref[...] = _raw_dot(dhid, wo_g, 1, 1)
        first = i == 0
        _accumulate(loss_ref, first, jnp.broadcast_to(loss, loss_ref.shape))
        _accumulate(dfw_ref, first, dfw)

        @pl.when(first)
        def _():
            dwo_ref[:SSD_WIDTH, :] = _raw_dot(ys, dhid, 0, 0)
            dwo_ref[SSD_WIDTH:, :] = _raw_dot(yg, dhid, 0, 0)

        @pl.when(i > 0)
        def _():
            dwo_ref[:SSD_WIDTH, :] += _raw_dot(ys, dhid, 0, 0)
            dwo_ref[SSD_WIDTH:, :] += _raw_dot(yg, dhid, 0, 0)

    row = lambda w: pl.BlockSpec((tm, w), lambda i: (i, 0))
    return pl.pallas_call(
        body, name="out_proj_loss", grid=(t // tm,),
        in_specs=[row(D_MODEL), row(SSD_WIDTH), row(GDN_W), _full((SSD_WIDTH + GDN_W, D_MODEL)), _full((1, D_MODEL)),
                  row(D_MODEL)],
        out_specs=[_full((8, LANES)), row(D_MODEL), row(SSD_WIDTH), row(GDN_W), _full((SSD_WIDTH + GDN_W, D_MODEL)),
                   _full((1, D_MODEL))],
        out_shape=[jax.ShapeDtypeStruct((8, LANES), F32), jax.ShapeDtypeStruct((t, D_MODEL), F32),
                   jax.ShapeDtypeStruct((t, SSD_WIDTH), F32), jax.ShapeDtypeStruct((t, GDN_W), F32),
                   jax.ShapeDtypeStruct((SSD_WIDTH + GDN_W, D_MODEL), F32), jax.ShapeDtypeStruct((1, D_MODEL), F32)],
        compiler_params=_params(("arbitrary",)),
    )(x, y_ssd, y_gdn, w_out, final_w, target)


def in_proj_bwd_x(x, normw, w_main, w_small, dproj_main, dproj_conv, dsmall_a, dsmall_b, dhid, slabbed):
    t = x.shape[0]
    tm = min(256, t)
    ni = t // tm
    ns = len(slabbed)

    def body(x_ref, nw_ref, wm_ref, ws_ref, dp_ref, dc_ref, da_ref, db_ref, dh_ref, *rest):
        slab_refs, (gx_ref, dnw_ref), land_refs = rest[:ns], rest[ns:ns + 2], rest[ns + 2:2 * ns + 2]
        sems = rest[2 * ns + 2:]
        i = pl.program_id(0)
        start, finish = _slab_exchange(slab_refs, land_refs, ns, *sems)

        @pl.when(i == 0)
        def _():
            start()

        du = (_raw_dot(dp_ref[...], wm_ref[:, :COL_CONV], 1, 1) + _raw_dot(dc_ref[...], wm_ref[:, COL_CONV:], 1, 1)
              + _raw_dot(da_ref[...] + db_ref[...], ws_ref[...], 1, 1))
        _, vjp = jax.vjp(rmsnorm, x_ref[...], nw_ref[...])
        dx, dnw = vjp(du)
        gx_ref[...] = dx + dh_ref[...]
        _accumulate(dnw_ref, i == 0, dnw)

        @pl.when(i == ni - 1)
        def _():
            finish()

    row = lambda w: pl.BlockSpec((tm, w), lambda i: (i, 0))
    out = pl.pallas_call(
        body, name="in_proj_bwd_x", grid=(ni,),
        in_specs=[row(D_MODEL), _full((1, D_MODEL)), _full((D_MODEL, MAIN)), _full((D_MODEL, LANES)), row(COL_CONV),
                  row(CONV_W), row(LANES), row(LANES), row(D_MODEL)] + [HBM] * ns,
        out_specs=[row(D_MODEL), _full((1, D_MODEL))] + [HBM] * ns,
        out_shape=[jax.ShapeDtypeStruct((t, D_MODEL), F32), jax.ShapeDtypeStruct((1, D_MODEL), F32)]
        + _slab_exchange_shapes(slabbed, []),
        scratch_shapes=_slab_exchange_sems(ns),
        compiler_params=_params(("arbitrary",)),
    )(x, normw, w_main, w_small, dproj_main, dproj_conv, dsmall_a, dsmall_b, dhid, *slabbed)
    return out[0], out[1], out[2:]


def in_proj_bwd_w(u, dproj_main, dsmall_a, dsmall_b):
    t = u.shape[0]
    tm, tn = min(2048, t), COL_CONV // 2

    def body(u_ref, dp_ref, da_ref, db_ref, dwm_ref, dws_ref):
        j, i = pl.program_id(0), pl.program_id(1)
        uu = u_ref[...]
        _accumulate(dwm_ref, i == 0, _raw_dot(uu, dp_ref[...], 0, 0))

        @pl.when(j == 0)
        def _():
            _accumulate(dws_ref, i == 0, _raw_dot(uu, da_ref[...] + db_ref[...], 0, 0))

    return pl.pallas_call(
        body, name="in_proj_bwd_w", grid=(COL_CONV // tn, t // tm),
        in_specs=[pl.BlockSpec((tm, D_MODEL), lambda j, i: (i, 0)), pl.BlockSpec((tm, tn), lambda j, i: (i, j)),
                  pl.BlockSpec((tm, LANES), lambda j, i: (i, 0)), pl.BlockSpec((tm, LANES), lambda j, i: (i, 0))],
        out_specs=[pl.BlockSpec((D_MODEL, tn), lambda j, i: (0, j)), _full((D_MODEL, LANES))],
        out_shape=[jax.ShapeDtypeStruct((D_MODEL, COL_CONV), F32), jax.ShapeDtypeStruct((D_MODEL, LANES), F32)],
        compiler_params=_params(("arbitrary", "arbitrary")),
    )(u, dproj_main, dsmall_a, dsmall_b)


def sum_slabs(a, name):
    n, rows, cols = a.shape
    tr = 64 if rows % 64 == 0 else rows

    def body(a_ref, o_ref):
        acc = a_ref[0].astype(F32)
        for d in range(1, n):
            acc = acc + a_ref[d].astype(F32)
        o_ref[...] = acc

    return pl.pallas_call(
        body, name=name, grid=(rows // tr,),
        in_specs=[pl.BlockSpec((n, tr, cols), lambda i: (0, i, 0))],
        out_specs=pl.BlockSpec((tr, cols), lambda i: (i, 0)),
        out_shape=jax.ShapeDtypeStruct((rows, cols), F32),
        compiler_params=_params(("arbitrary",)),
    )(a)


def adamw(w, g, m, v, name):
    _, rows, cols = w.shape
    tr = 128 if rows % 128 == 0 else rows

    def body(w_ref, g_ref, m_ref, v_ref, d_ref, nm_ref, nv_ref):
        gg = g_ref[...]
        nm = ADAM_B1 * m_ref[...] + (1.0 - ADAM_B1) * gg
        nv = ADAM_B2 * v_ref[...] + (1.0 - ADAM_B2) * (gg * gg)
        m_hat = nm / (1.0 - ADAM_B1 ** ADAM_STEP)
        v_hat = nv / (1.0 - ADAM_B2 ** ADAM_STEP)
        d_ref[...] = -ADAM_LR * (m_hat / (jnp.sqrt(v_hat) + ADAM_EPS) + ADAM_WD * w_ref[...])
        nm_ref[...] = nm
        nv_ref[...] = nv

    spec = pl.BlockSpec((1, tr, cols), lambda i: (0, i, 0))
    shp = jax.ShapeDtypeStruct((1, rows, cols), F32)
    return pl.pallas_call(
        body, name=name, grid=(rows // tr,), in_specs=[spec] * 4, out_specs=[spec] * 3, out_shape=[shp] * 3,
        compiler_params=_params(("arbitrary",)),
    )(w, g.reshape(w.shape), m, v)


def _my_place():
    return lax.axis_index("x"), lax.axis_index("y"), lax.axis_index("c")


def gather_weights(big, small):
    nb, n = len(big), len(big) + len(small)
    parts = 4

    def body(*refs):
        srcs, outs = refs[:n], refs[n:2 * n]
        land_a, land_b = refs[2 * n:2 * n + nb], refs[2 * n + nb:2 * n + 2 * nb]
        send_sems, recv_sems, fwd_send, fwd_recv, local_sems = refs[2 * n + 2 * nb:]
        x, y, c = _my_place()
        me = 2 * x + y
        chips = [(1 - x, y), (x, 1 - y), (1 - x, 1 - y)]
        half = [a.shape[0] // 2 for a in big]

        def ici(j, i):
            px, py = chips[j]
            if i < nb:
                src, dst = srcs[i].at[pl.ds(c * half[i], half[i])], land_a[i].at[j]
            else:
                src, dst = srcs[i], outs[i].at[me]
            return pltpu.make_async_remote_copy(src_ref=src, dst_ref=dst, send_sem=send_sems.at[j * n + i],
                                                recv_sem=recv_sems.at[j * n + i], device_id=(px, py, c),
                                                device_id_type=MESH)

        def ici_arrival(j, i):
            px, py = chips[j]
            dst = land_a[i].at[j] if i < nb else outs[i].at[2 * px + py]
            return pltpu.make_async_remote_copy(src_ref=dst, dst_ref=dst, send_sem=send_sems.at[j * n + i],
                                                recv_sem=recv_sems.at[j * n + i], device_id=(px, py, c),
                                                device_id_type=MESH)

        def forward(j, i, p):
            rows = half[i] // parts
            k = (j * nb + i) * parts + p
            return pltpu.make_async_remote_copy(
                src_ref=land_a[i].at[j, pl.ds(p * rows, rows)], dst_ref=land_b[i].at[j, pl.ds(p * rows, rows)],
                send_sem=fwd_send.at[k], recv_sem=fwd_recv.at[k], device_id=(x, y, 1 - c), device_id_type=MESH)

        def store(j, i, from_sibling):
            px, py = chips[j]
            buf, h = (land_b, 1 - c) if from_sibling else (land_a, c)
            k = n + (j * nb + i) * 2 + (1 if from_sibling else 0)
            return pltpu.make_async_copy(buf[i].at[j], outs[i].at[2 * px + py, pl.ds(h * half[i], half[i])],
                                         local_sems.at[k])

        own = [pltpu.make_async_copy(srcs[i], outs[i].at[me], local_sems.at[i]) for i in range(n)]
        sends = [ici(j, i) for j in range(3) for i in range(n)]
        for cp in own + sends:
            cp.start()
        pending = []
        for j in range(3):
            for i in range(n):
                ici_arrival(j, i).wait_recv()
                if i < nb:
                    fw = [forward(j, i, p) for p in range(parts)]
                    st = store(j, i, False)
                    for cp in fw + [st]:
                        cp.start()
                    pending += [cp.wait_send for cp in fw] + [st.wait]
        for j in range(3):
            for i in range(nb):
                for p in range(parts):
                    forward(j, i, p).wait_recv()
                st = store(j, i, True)
                st.start()
                pending.append(st.wait)
        for cp in sends:
            cp.wait_send()
        for wait in pending:
            wait()
        for cp in own:
            cp.wait()

    shards = list(big) + list(small)
    lands = [pltpu.VMEM((3, a.shape[0] // 2) + a.shape[1:], a.dtype) for a in big]
    return pl.pallas_call(
        body, name="gather_weights",
        in_specs=[HBM] * n, out_specs=[HBM] * n,
        out_shape=[jax.ShapeDtypeStruct((N_CHIP,) + s.shape, s.dtype) for s in shards],
        scratch_shapes=lands + lands + [
            pltpu.SemaphoreType.DMA((3 * n,)), pltpu.SemaphoreType.DMA((3 * n,)),
            pltpu.SemaphoreType.DMA((3 * nb * parts,)), pltpu.SemaphoreType.DMA((3 * nb * parts,)),
            pltpu.SemaphoreType.DMA((n + 6 * nb,))],
        compiler_params=pltpu.CompilerParams(vmem_limit_bytes=VMEM_LIMIT),
    )(*shards)


def _peer(x, y, c, mask):
    mx, my, mc = (mask >> 2) & 1, (mask >> 1) & 1, mask & 1
    return (x ^ mx if mx else x, y ^ my if my else y, c ^ mc if mc else c)


def _slab_exchange_shapes(slabbed, replicated):
    return ([jax.ShapeDtypeStruct(a.shape, a.dtype) for a in slabbed]
            + [jax.ShapeDtypeStruct((N_DEV,) + a.shape, a.dtype) for a in replicated])


def _slab_exchange_sems(n):
    return [pltpu.SemaphoreType.DMA((7 * n,)), pltpu.SemaphoreType.DMA((7 * n,)), pltpu.SemaphoreType.DMA((n,))]


def _slab_exchange(srcs, outs, ns, send_sems, recv_sems, local_sems):
    n = len(srcs)
    x, y, c = _my_place()
    me = 4 * x + 2 * y + c

    def piece(i, dev):
        return srcs[i].at[dev] if i < ns else srcs[i]

    def copies(arriving):
        out = []
        for mask in range(1, N_DEV):
            px, py, pc = _peer(x, y, c, mask)
            dev = 4 * px + 2 * py + pc
            for i in range(n):
                k = (mask - 1) * n + i
                out.append(pltpu.make_async_remote_copy(
                    src_ref=piece(i, dev), dst_ref=outs[i].at[dev if arriving else me], send_sem=send_sems.at[k],
                    recv_sem=recv_sems.at[k], device_id=(px, py, pc), device_id_type=MESH))
        return out

    def local():
        return [pltpu.make_async_copy(piece(i, me), outs[i].at[me], local_sems.at[i]) for i in range(n)]

    def start():
        for cp in local() + copies(False):
            cp.start()

    def finish():
        for cp in copies(True):
            cp.wait_recv()
        for cp in copies(False):
            cp.wait_send()
        for cp in local():
            cp.wait()

    return start, finish


def exchange_halves(landed, replicated):
    n, nr = len(landed), len(replicated)
    streams = 8
    halves = [jax.ShapeDtypeStruct(a.shape[1:], F32) for a in landed]
    sum_rows = 64

    def body(*refs):
        srcs, rep_srcs, outs, rep_outs = refs[:n], refs[n:n + nr], refs[n + nr:2 * n + nr], refs[2 * n + nr:2 * (n + nr)]
        refs = refs[2 * (n + nr):]
        slabs, mine, theirs = refs[:n], refs[n:2 * n], refs[2 * n:3 * n]
        send_sems, recv_sems, in_sems, out_sems = refs[3 * n:3 * n + 4]
        rep_start, rep_finish = _slab_exchange(rep_srcs, rep_outs, 0, *refs[3 * n + 4:])
        rep_start()
        x, y, c = _my_place()
        loads = [pltpu.make_async_copy(srcs[i], slabs[i], in_sems.at[i]) for i in range(n)]
        for cp in loads:
            cp.start()
        for i in range(n):
            loads[i].wait()
            for r in range(0, halves[i].shape[0], sum_rows):
                rows = pl.ds(r, sum_rows)
                acc = slabs[i][0, rows, :].astype(F32)
                for d in range(1, N_DEV):
                    acc = acc + slabs[i][d, rows, :].astype(F32)
                mine[i][rows, :] = acc

        def chunk_copy(i, s):
            rows = halves[i].shape[0] // streams
            k = i * streams + s
            return pltpu.make_async_remote_copy(
                src_ref=mine[i].at[pl.ds(s * rows, rows)], dst_ref=theirs[i].at[pl.ds(s * rows, rows)],
                send_sem=send_sems.at[k], recv_sem=recv_sems.at[k], device_id=(x, y, 1 - c), device_id_type=MESH)

        sends = [chunk_copy(i, s) for i in range(n) for s in range(streams)]
        for cp in sends:
            cp.start()
        own = [pltpu.make_async_copy(mine[i], outs[i].at[c], out_sems.at[i]) for i in range(n)]
        for cp in own:
            cp.start()
        for cp in sends:
            cp.wait_recv()
        got = [pltpu.make_async_copy(theirs[i], outs[i].at[1 - c], out_sems.at[n + i]) for i in range(n)]
        for cp in got:
            cp.start()
        for cp in sends:
            cp.wait_send()
        for cp in own + got:
            cp.wait()
        rep_finish()

    vmem = [pltpu.VMEM(a.shape, a.dtype) for a in halves]
    out = pl.pallas_call(
        body, name="exchange_halves",
        in_specs=[HBM] * (n + nr), out_specs=[HBM] * (n + nr),
        out_shape=[jax.ShapeDtypeStruct((2,) + a.shape, a.dtype) for a in halves]
        + _slab_exchange_shapes([], replicated),
        scratch_shapes=[pltpu.VMEM(a.shape, a.dtype) for a in landed] + vmem + vmem
        + [pltpu.SemaphoreType.DMA((n * streams,)), pltpu.SemaphoreType.DMA((n * streams,)),
           pltpu.SemaphoreType.DMA((n,)), pltpu.SemaphoreType.DMA((2 * n,))] + _slab_exchange_sems(nr),
        compiler_params=pltpu.CompilerParams(vmem_limit_bytes=VMEM_LIMIT),
    )(*landed, *replicated)
    return out[:n], out[n:]


def _pack_cols(pieces):
    offs, pos = [], 0
    for a in pieces:
        offs.append(pos)
        pos += a.shape[1]
    rows8 = [jnp.pad(a.astype(F32), ((0, 8 - a.shape[0]), (0, 0))) for a in pieces]
    return jnp.concatenate(rows8, axis=1), offs


def adamw_many(ws, gs, ms, vs):
    n = len(ws)

    def body(*refs):
        w_r, g_r, m_r, v_r = refs[:n], refs[n:2 * n], refs[2 * n:3 * n], refs[3 * n:4 * n]
        d_o, m_o, v_o = refs[4 * n:5 * n], refs[5 * n:6 * n], refs[6 * n:7 * n]
        for i in range(n):
            gg = g_r[i][...]
            nm = ADAM_B1 * m_r[i][...] + (1.0 - ADAM_B1) * gg
            nv = ADAM_B2 * v_r[i][...] + (1.0 - ADAM_B2) * (gg * gg)
            m_hat = nm / (1.0 - ADAM_B1 ** ADAM_STEP)
            v_hat = nv / (1.0 - ADAM_B2 ** ADAM_STEP)
            d_o[i][...] = -ADAM_LR * (m_hat / (jnp.sqrt(v_hat) + ADAM_EPS) + ADAM_WD * w_r[i][...])
            m_o[i][...] = nm
            v_o[i][...] = nv

    shapes = [jax.ShapeDtypeStruct(w.shape, F32) for w in ws]
    out = pl.pallas_call(body, name="adamw_small", out_shape=shapes * 3,
                         compiler_params=pltpu.CompilerParams(vmem_limit_bytes=VMEM_LIMIT))(*ws, *gs, *ms, *vs)
    return out[:n], out[n:2 * n], out[2 * n:]


def _lanes(vec, start):
    n = vec.shape[-1]
    return jnp.pad(vec.reshape(1, n).astype(F32), ((0, 0), (start, LANES - start - n)))


def kernel(x, norm_w, w_in, ssd_conv_w, ssd_conv_b, ssd_dt_bias, ssd_a_log, ssd_d, ssd_norm_w, gdn_conv_w, gdn_dt_bias, gdn_a_log, gdn_norm_w, w_out, final_norm_w, loss_target, m_norm_w, m_w_in, m_ssd_conv_w, m_ssd_conv_b, m_ssd_dt_bias, m_ssd_a_log, m_ssd_d, m_ssd_norm_w, m_gdn_conv_w, m_gdn_dt_bias, m_gdn_a_log, m_gdn_norm_w, m_w_out, m_final_norm_w, v_norm_w, v_w_in, v_ssd_conv_w, v_ssd_conv_b, v_ssd_dt_bias, v_ssd_a_log, v_ssd_d, v_ssd_norm_w, v_gdn_conv_w, v_gdn_dt_bias, v_gdn_a_log, v_gdn_norm_w, v_w_out, v_final_norm_w):
    xs = x[0]
    target = loss_target[0]
    chip = 2 * lax.axis_index("x") + lax.axis_index("y")
    w_in_shard, w_out_shard = w_in[0], w_out[0]
    in_cols = w_in_shard.shape[1]
    out_rows = w_out_shard.shape[0]

    g_in, g_out, g_cs, g_cg = gather_weights(
        [w_in_shard.astype(MXU_DTYPE), w_out_shard.astype(MXU_DTYPE)], [ssd_conv_w[0], gdn_conv_w[0]])
    w_in_full = jnp.concatenate([g_in[k] for k in range(N_CHIP)], axis=1)
    w_out_full = g_out.reshape(N_CHIP * out_rows, D_MODEL)
    cw_ssd = jnp.concatenate([g_cs[k] for k in range(N_CHIP)], axis=1)
    cw_gdn = jnp.concatenate([g_cg[k] for k in range(N_CHIP)], axis=1)
    cb_ssd, cb_gdn = ssd_conv_b, jnp.zeros((1, GDN_CONV), F32)
    o_xbc, o_dt, o_gate, o_qkv, o_ab = 1024, 2560, 2576, 3600, 6672
    w_main = jnp.concatenate([w_in_full[:, :o_xbc], w_in_full[:, o_gate:o_qkv], w_in_full[:, o_qkv:o_ab],
                              w_in_full[:, o_xbc:o_dt]], axis=1)
    w_small = jnp.concatenate([w_in_full[:, o_dt:o_gate], w_in_full[:, o_ab:],
                               jnp.zeros((D_MODEL, LANES - 32), MXU_DTYPE)], axis=1)
    alog = _lanes(ssd_a_log, 0) + _lanes(gdn_a_log, LANE_GA)
    dtb = _lanes(ssd_dt_bias, 0) + _lanes(gdn_dt_bias, LANE_GA)
    dvec = _lanes(ssd_d, 0)
    fw = final_norm_w.reshape(1, D_MODEL)

    cw, cb = jnp.concatenate([cw_gdn, cw_ssd], axis=1), jnp.concatenate([cb_gdn, cb_ssd], axis=1)
    proj_main, proj_small, u = in_proj(xs, norm_w, w_main, w_small)
    conv_out, x_conv, dsilu_conv = in_proj_conv(u, w_main, cw, cb)
    y_ssd, hist_ssd = ssd_fwd(conv_out, proj_main, proj_small, ssd_norm_w, alog, dtb, dvec)
    y_gdn, hist_gdn, tinv_gdn = gdn_fwd(conv_out, proj_main, proj_small, gdn_norm_w, alog, dtb)

    loss_blk, dhid, dy_ssd, dy_gdn, d_w_out, d_fw = out_proj_loss(xs, y_ssd, y_gdn, w_out_full, fw, target)
    dconv, dproj_main, dsmall_ssd, d_ssd_nw, d_alog_s, d_dtb_s, d_dvec = ssd_bwd(
        conv_out, proj_main, proj_small, ssd_norm_w, alog, dtb, dvec, hist_ssd, dy_ssd)
    dproj_main, dconv, dsmall_gdn, d_gdn_nw, d_alog_g, d_dtb_g = gdn_bwd(
        dproj_main, dconv, conv_out, proj_main, proj_small, gdn_norm_w, alog, dtb, hist_gdn, tinv_gdn, dy_gdn)
    slabs_out = d_w_out.reshape(N_DEV, out_rows // 2, D_MODEL).astype(COMM_DTYPE)
    dproj_conv, d_w_conv, dwb, (r_out,) = conv_bwd_w(u, x_conv, dsilu_conv, cw, dconv, [slabs_out])
    dwb_gdn, dwb_ssd = dwb[:, :GDN_CONV], dwb[:, GDN_CONV:]
    d_w_zg, d_w_small = in_proj_bwd_w(u, dproj_main, dsmall_ssd, dsmall_gdn)
    order = [(d_w_zg, 0, COL_GATE), (d_w_conv, COL_SSD - COL_CONV, CONV_W), (d_w_small, 0, 16),
             (d_w_zg, COL_GATE, COL_CONV), (d_w_conv, 0, COL_SSD - COL_CONV), (d_w_small, 16, 32)]
    shards, pos = [[] for _ in range(N_CHIP)], 0
    for src, lo, hi in order:
        while lo < hi:
            k = pos // in_cols
            n = min(hi - lo, (k + 1) * in_cols - pos)
            shards[k].append(src[:, lo:lo + n].astype(COMM_DTYPE))
            lo, pos = lo + n, pos + n
    slabs_in = jnp.stack([jnp.concatenate(p, axis=1) for p in shards]).reshape(N_DEV, D_MODEL // 2, in_cols)
    grad_x, d_norm_w, (r_in,) = in_proj_bwd_x(xs, norm_w, w_main, w_small, dproj_main, dproj_conv, dsmall_ssd,
                                               dsmall_gdn, dhid, [slabs_in])
    d_alog, d_dtb = d_alog_s + d_alog_g, d_dtb_s + d_dtb_g
    packed, (o_nw, o_cs, o_cg, o_snw, o_fw, o_al, o_db, o_dv, o_gnw, o_loss) = _pack_cols([
        d_norm_w, dwb_ssd, dwb_gdn,
        d_ssd_nw.reshape(1, SSD_WIDTH), d_fw, d_alog, d_dtb, d_dvec, d_gdn_nw, loss_blk])

    (full_in, full_out), (r_small,) = exchange_halves([r_in, r_out], [packed])
    tot = sum_slabs(r_small, "sum_small")
    grad_w_in = full_in.reshape(D_MODEL, in_cols)
    grad_w_out = full_out.reshape(out_rows, D_MODEL)
    loss = tot[0, o_loss]
    sc, gc = ssd_conv_w.shape[2], gdn_conv_w.shape[2]
    row = lambda off, n, r=0: tot[r:r + 1, off:off + n]
    gs = [row(o_nw, D_MODEL),
          lax.dynamic_slice(tot, (0, o_cs + chip * sc), (4, sc)),
          row(o_cs, SSD_CONV, 4),
          row(o_db, SSD_HEADS), row(o_al, SSD_HEADS), row(o_dv, SSD_HEADS),
          row(o_snw, SSD_WIDTH),
          lax.dynamic_slice(tot, (0, o_cg + chip * gc), (4, gc)),
          row(o_db + LANE_GA, GDN_HEADS), row(o_al + LANE_GA, GDN_HEADS),
          row(o_gnw, GDN_DV), row(o_fw, D_MODEL)]

    names = ["norm_w", "ssd_conv_w", "ssd_conv_b", "ssd_dt_bias", "ssd_a_log", "ssd_d", "ssd_norm_w", "gdn_conv_w",
             "gdn_dt_bias", "gdn_a_log", "gdn_norm_w", "final_norm_w"]
    ws = [norm_w, ssd_conv_w, ssd_conv_b, ssd_dt_bias, ssd_a_log, ssd_d, ssd_norm_w, gdn_conv_w, gdn_dt_bias,
          gdn_a_log, gdn_norm_w, final_norm_w]
    ms = [m_norm_w, m_ssd_conv_w, m_ssd_conv_b, m_ssd_dt_bias, m_ssd_a_log, m_ssd_d, m_ssd_norm_w, m_gdn_conv_w,
          m_gdn_dt_bias, m_gdn_a_log, m_gdn_norm_w, m_final_norm_w]
    vs = [v_norm_w, v_ssd_conv_w, v_ssd_conv_b, v_ssd_dt_bias, v_ssd_a_log, v_ssd_d, v_ssd_norm_w, v_gdn_conv_w,
          v_gdn_dt_bias, v_gdn_a_log, v_gdn_norm_w, v_final_norm_w]
    shapes = [w.shape for w in ws]
    flat = lambda arrs: [a.reshape(g.shape) for a, g in zip(arrs, gs)]
    d_s, m_s, v_s = adamw_many(flat(ws), gs, flat(ms), flat(vs))
    back = lambda arrs: dict(zip(names, [a.reshape(s) for a, s in zip(arrs, shapes)]))
    delta, new_m, new_v, grads = back(d_s), back(m_s), back(v_s), back(gs)
    d_in, m_in, v_in = adamw(w_in, grad_w_in, m_w_in, v_w_in, "adamw_w_in")
    d_out, m_out, v_out = adamw(w_out, grad_w_out, m_w_out, v_w_out, "adamw_w_out")
    for tbl, a_in, a_out in ((grads, grad_w_in[None], grad_w_out[None]), (delta, d_in, d_out), (new_m, m_in, m_out),
                             (new_v, v_in, v_out)):
        tbl["w_in"] = a_in
        tbl["w_out"] = a_out

    order = ["norm_w", "w_in", "ssd_conv_w", "ssd_conv_b", "ssd_dt_bias", "ssd_a_log", "ssd_d", "ssd_norm_w",
             "gdn_conv_w", "gdn_dt_bias", "gdn_a_log", "gdn_norm_w", "w_out", "final_norm_w"]
    return (loss.reshape(()), grad_x[None], *[grads[k] for k in order], *[delta[k] for k in order],
            *[new_m[k] for k in order], *[new_v[k] for k in order])
```

```python
import functools

import jax
import jax.numpy as jnp
from jax import lax
from jax.experimental import pallas as pl
from jax.experimental.pallas import tpu as pltpu

F32 = jnp.float32
MXU_DTYPE = jnp.bfloat16
COMM_DTYPE = jnp.bfloat16
MESH = pl.DeviceIdType.MESH

D_MODEL = 1024
CHUNK = 64
EPS = 1e-6
SSD_HEADS, SSD_GROUPS, SSD_STATE = 16, 2, 128
SSD_WIDTH, SSD_CONV = 1024, 1536
SSD_GW = SSD_WIDTH // SSD_GROUPS
GDN_HEADS, GDN_DK, GDN_DV = 8, 128, 128
GDN_W, GDN_CONV = 1024, 3072
GDN_HC = 2 * GDN_DK + GDN_DV
IN_DIM = 6688
MAIN = 6656
LANES = 128
COL_Z, COL_GATE, COL_GDN, COL_SSD = 0, 1024, 2048, 5120
COL_CONV = COL_GDN
CONV_W = MAIN - COL_CONV
GDN_HB = 8
GDN_CB = 4
SSD_CB = 4
LANE_GA, LANE_GB = 16, 24
N_DEV, N_CHIP = 8, 4
VMEM_LIMIT = 52 * 1024 * 1024

ADAM_LR, ADAM_B1, ADAM_B2, ADAM_EPS, ADAM_WD, ADAM_STEP = 0.001, 0.9, 0.999, 1e-08, 0.01, 10


def _split(a, n):
    parts, rest = [], a.astype(F32)
    for i in range(n):
        p = rest.astype(MXU_DTYPE)
        parts.append(p)
        if i < n - 1:
            rest = rest - p.astype(F32)
    return parts


def _raw_dot(a, b, ca, cb, mode="bf16"):
    d = lambda u, v: lax.dot_general(u, v, (((ca,), (cb,)), ((), ())), preferred_element_type=F32)
    if mode == "bf16":
        return d(a.astype(MXU_DTYPE), b.astype(MXU_DTYPE))
    if mode == "x3":
        (ah, al), (bh, bl) = _split(a, 2), _split(b, 2)
        return d(ah, bh) + (d(ah, bl) + d(al, bh))
    if mode == "sel_a":
        a0 = a.astype(MXU_DTYPE)
        b1, b2, b3 = _split(b, 3)
        return d(a0, b1) + (d(a0, b2) + d(a0, b3))
    assert mode == "sel_b", mode
    b0 = b.astype(MXU_DTYPE)
    a1, a2, a3 = _split(a, 3)
    return d(a1, b0) + (d(a2, b0) + d(a3, b0))


@functools.partial(jax.custom_vjp, nondiff_argnums=(2,))
def mm_nn(a, b, mode="bf16"):
    return _raw_dot(a, b, 1, 0, mode)


@functools.partial(jax.custom_vjp, nondiff_argnums=(2,))
def mm_nt(a, b, mode="bf16"):
    return _raw_dot(a, b, 1, 1, mode)


@functools.partial(jax.custom_vjp, nondiff_argnums=(2,))
def mm_tn(a, b, mode="bf16"):
    return _raw_dot(a, b, 0, 0, mode)


_SAME = {"bf16": ("bf16", "bf16"), "x3": ("x3", "x3")}
_NN_BWD = dict(_SAME, sel_a=("bf16", "sel_a"), sel_b=("sel_b", "bf16"))
_NT_BWD = dict(_SAME, sel_a=("bf16", "sel_b"), sel_b=("sel_b", "bf16"))
_TN_BWD = dict(_SAME, sel_a=("bf16", "sel_a"), sel_b=("sel_a", "bf16"))
mm_nn.defvjp(lambda a, b, m: (_raw_dot(a, b, 1, 0, m), (a, b)),
             lambda m, r, g: (mm_nt(g, r[1], _NN_BWD[m][0]), mm_tn(r[0], g, _NN_BWD[m][1])))
mm_nt.defvjp(lambda a, b, m: (_raw_dot(a, b, 1, 1, m), (a, b)),
             lambda m, r, g: (mm_nn(g, r[1], _NT_BWD[m][0]), mm_tn(g, r[0], _NT_BWD[m][1])))
mm_tn.defvjp(lambda a, b, m: (_raw_dot(a, b, 0, 0, m), (a, b)),
             lambda m, r, g: (mm_nt(r[1], g, _TN_BWD[m][0]), mm_nn(r[0], g, _TN_BWD[m][1])))


@jax.custom_jvp
def sigmoid(x):
    return 1.0 / (1.0 + jnp.exp(-x))


@sigmoid.defjvp
def _sigmoid_jvp(p, t):
    s = sigmoid(p[0])
    return s, t[0] * s * (1.0 - s)


@jax.custom_jvp
def softplus(x):
    return jnp.maximum(x, 0.0) + jnp.log(1.0 + jnp.exp(-jnp.abs(x)))


@softplus.defjvp
def _softplus_jvp(p, t):
    return softplus(p[0]), t[0] * sigmoid(p[0])


def silu(x):
    return x * sigmoid(x)


def rmsnorm(x, w):
    return x * lax.rsqrt(jnp.mean(x * x, axis=-1, keepdims=True) + EPS) * w


def _iota(shape, dim):
    return lax.broadcasted_iota(jnp.int32, shape, dim)


def _halves():
    lane = _iota((1, LANES), 1) >> 6
    return _ind(lane == 0), _ind(lane == 1)


def _block_diag(pair):
    h0, h1 = _halves()
    return jnp.concatenate([pair * h0, pair * h1], axis=0)


def _tri_inv_impl(mats):
    r, c = _iota((CHUNK, LANES), 0), _iota((CHUNK, LANES), 1) & (CHUNK - 1)
    eye = _ind(r == c)
    blockdiag = _ind((r >> 4) == (c >> 4))
    dot = lambda u, v: _raw_dot(u, _block_diag(v), 1, 0, "x3")
    dot1 = lambda u, v: _raw_dot(u, _block_diag(v), 1, 0)
    each = lambda f, *ls: [f(*xs) for xs in zip(*ls)]
    dg = each(lambda a: a * blockdiag, mats)
    off = each(lambda a, d: a - d, mats, dg)
    m = each(lambda d: -d, dg)
    p = each(lambda x: eye + x, m)
    pw = m
    for _ in range(3):
        pw = each(lambda x: dot1(x, x), pw)
        p = each(lambda x, y: x + dot1(x, y), p, pw)
    e = each(dot, p, off)
    e2 = each(lambda x: dot1(x, x), e)
    q = each(lambda x: eye - x, e)
    q = each(lambda x, y: x + dot1(x, y), q, e2)
    return each(dot, q, p)


def _tri_inv_bwd(ts, gs):
    h0, h1 = _halves()
    x = [mm_nt(g, _block_diag(t)) for g, t in zip(gs, ts)]
    full = [mm_tn(t, y) for t, y in zip(ts, x)]
    return [-(f[:CHUNK] * h0 + f[CHUNK:] * h1) for f in full]


@jax.custom_vjp
def tri_inv(mats):
    return _tri_inv_impl(mats)


def _tri_inv_fwd(mats):
    ts = _tri_inv_impl(mats)
    return ts, ts


tri_inv.defvjp(_tri_inv_fwd, lambda ts, gs: (_tri_inv_bwd(ts, gs),))


@jax.custom_vjp
def tri_inv_saved(mats, ts):
    del mats
    return ts


tri_inv_saved.defvjp(lambda mats, ts: (ts, ts),
                     lambda ts, gs: (_tri_inv_bwd(ts, gs), [jnp.zeros_like(t) for t in ts]))


def _ind(cond):
    return jnp.where(cond, 1.0, 0.0).astype(F32)


def _chunk_masks():
    r, c = _iota((CHUNK, CHUNK), 0), _iota((CHUNK, CHUNK), 1)
    return _ind(r >= c), _ind(r > c), _ind(r == c), _ind(_iota((CHUNK, 1), 0) == CHUNK - 1)


def _log_decay_cumsum(small, alog, dtb, tri):
    sp = softplus(small + dtb)
    la = -jnp.exp(alog) * sp
    return sp, mm_nn(tri, la, "sel_a")


def _col_of(x, lane):
    return jnp.sum(x * _ind(_iota((1, LANES), 1) == lane), axis=1, keepdims=True)


def _pair_masks():
    r, c = _iota((CHUNK, LANES), 0), _iota((CHUNK, LANES), 1)
    c6 = c & (CHUNK - 1)
    return _ind(r >= c6), _ind(r > c6), (_ind(c == r), _ind(c == r + CHUNK))


def _decay_pair(col_a, col_b, tri_w, eye_w):
    h0, h1 = _halves()
    col = col_a * h0 + col_b * h1
    row = jnp.sum(col_a * eye_w[0] + col_b * eye_w[1], axis=0, keepdims=True)
    return jnp.exp((col - row) * tri_w) * tri_w


def gdn_chunk(h0, qs, ks, vs, smalls, gates, normw, alog, dtb, states, saved_t=None):
    tri, _, _, last = _chunk_masks()
    tri_w, strict_w, eye_w = _pair_masks()
    nh = len(qs[0])
    flat = lambda xss: [x for xs in xss for x in xs]
    lacs = [_log_decay_cumsum(sm, alog, dtb, tri)[1] for sm in smalls]
    qs, ks, vs, gates = flat(qs), flat(ks), flat(vs), flat(gates)
    heads, pairs = range(len(qs)), range(len(qs) // 2)
    each = lambda f, *ls: [f(*xs) for xs in zip(*ls)]
    ab = lambda xs, p: (xs[2 * p], xs[2 * p + 1])
    stack = lambda xs: jnp.concatenate(xs, axis=0)
    gc = [_col_of(lacs[i // nh], LANE_GA + h0 + i % nh) for i in heads]
    beta = [sigmoid(_col_of(smalls[i // nh], LANE_GB + h0 + i % nh)) for i in heads]
    decay = [_decay_pair(*ab(gc, p), tri_w, eye_w) for p in pairs]
    gl = each(lambda x: jnp.sum(x * last, axis=0, keepdims=True), gc)
    q = each(lambda x: x * lax.rsqrt(jnp.sum(x * x, axis=-1, keepdims=True) + EPS) * (GDN_DK ** -0.5), qs)
    k = each(lambda x: x * lax.rsqrt(jnp.sum(x * x, axis=-1, keepdims=True) + EPS), ks)
    kb = each(lambda x, b: x * b, k, beta)
    eg = each(jnp.exp, gc)
    zero = jnp.zeros((CHUNK, GDN_DK), F32)
    k_bd = [stack([join_lanes([k[2 * p], zero]), join_lanes([zero, k[2 * p + 1]])]) for p in pairs]
    a = [mm_nt(join_lanes(list(ab(kb, p))), k_bd[p]) * (decay[p] * strict_w) for p in pairs]
    t = tri_inv(a) if saved_t is None else tri_inv_saved(a, saved_t)
    attn = [mm_nt(join_lanes(list(ab(q, p))), k_bd[p]) * decay[p] for p in pairs]
    rhs = [stack([join_lanes([vs[h] * beta[h], kb[h] * eg[h]]) for h in (2 * p, 2 * p + 1)]) for p in pairs]
    uw = [mm_nn(_block_diag(t[p]), rhs[p]) for p in pairs]
    uw = [x for p in pairs for x in split_rows(uw[p])]
    u, w = zip(*[split_lanes(x) for x in uw])
    ys = []
    for c in range(len(smalls)):
        hs = range(c * nh, (c + 1) * nh)
        v_new = [u[i] - mm_nn(w[i], states[i % nh]) for i in hs]
        av = [mm_nn(_block_diag(attn[c * nh // 2 + p]), stack(list(ab(v_new, p)))) for p in range(nh // 2)]
        av = [x for y in av for x in split_rows(y)]
        o = [mm_nn(q[i] * eg[i], states[i % nh]) + av[i % nh] for i in hs]
        states = [states[i % nh] * jnp.exp(gl[i]) + mm_tn(k[i] * jnp.exp(gl[i] - gc[i]), v_new[i % nh]) for i in hs]
        ys.append([rmsnorm(o[i % nh], normw) * silu(gates[i]) for i in hs])
    return ys, states, t


@jax.custom_vjp
def split_rows(x):
    n = x.shape[0] // 2
    return [x[:n], x[n:]]


split_rows.defvjp(lambda x: (split_rows(x), None), lambda _, gs: (jnp.concatenate(gs, axis=0),))


@jax.custom_vjp
def split_lanes(x):
    return [x[:, i * LANES:(i + 1) * LANES] for i in range(x.shape[1] // LANES)]


@jax.custom_vjp
def join_lanes(xs):
    return jnp.concatenate(xs, axis=1)


split_lanes.defvjp(lambda x: (split_lanes(x), None), lambda _, gs: (join_lanes(gs),))
join_lanes.defvjp(lambda xs: (join_lanes(xs), None), lambda _, g: (split_lanes(g),))


def ssd_chunk(xs, bm, cm, z, smalls, normw, alog, dtb, dvec, state):
    tri, _, _, last = _chunk_masks()
    tri_w, _, eye_w = _pair_masks()
    h0, h1 = _halves()
    hpg = SSD_HEADS // SSD_GROUPS
    ng = len(normw)
    flat = lambda xss: [x for xs_ in xss for x in xs_]
    xs, bm, cm, z = flat(xs), flat(bm), flat(cm), flat(z)
    units, pairs = range(len(xs)), range(hpg // 2)
    each = lambda f, *ls: [f(*a) for a in zip(*ls)]
    sp_lac = [_log_decay_cumsum(sm, alog, dtb, tri) for sm in smalls]
    lac_last = [jnp.sum(lac * last, axis=0, keepdims=True) for _, lac in sp_lac]
    sel = [_ind(_iota((LANES, SSD_GW), 0) == g * hpg + (_iota((LANES, SSD_GW), 1) >> 6)) for g in range(ng)]
    expand = lambda vs, mode: [mm_nn(vs[i // ng], sel[i % ng], mode) for i in units]
    dt_e = expand([sp for sp, _ in sp_lac], "bf16")
    elac_e = expand([jnp.exp(lac) for _, lac in sp_lac], "bf16")
    toend_e = expand([jnp.exp(ll - lac) for (_, lac), ll in zip(sp_lac, lac_last)], "bf16")
    row8 = _iota((8, 1), 0)
    two_e = expand([_ind(row8 == 0) * dvec + _ind(row8 == 1) * jnp.exp(ll) for ll in lac_last], "sel_b")
    d_e = each(lambda v: jnp.sum(v * _ind(row8 == 0), axis=0, keepdims=True), two_e)
    chunk_e = each(lambda v: jnp.sum(v * _ind(row8 == 1), axis=0, keepdims=True), two_e)
    xdt = each(lambda a, b: a * b, xs, dt_e)
    cb_w = each(lambda c_, b_: mm_nt(c_, jnp.concatenate([b_, b_], axis=0)), cm, bm)
    x_pairs = each(split_lanes, xdt)
    col = lambda i, j: _col_of(sp_lac[i // ng][1], (i % ng) * hpg + j)
    lms = [[_decay_pair(col(i, 2 * p), col(i, 2 * p + 1), tri_w, eye_w) for p in pairs] for i in units]
    stacked = [[jnp.concatenate([x_pairs[i][p] * h0, x_pairs[i][p] * h1], axis=0) for p in pairs] for i in units]
    terms = [[mm_nn(cb_w[i] * lms[i][p], stacked[i][p]) for p in pairs] for i in units]
    y_in = [join_lanes(terms[i]) + xs[i] * d_e[i] for i in units]
    state_in = each(lambda b_, xd, te: mm_tn(b_, xd * te), bm, xdt, toend_e)
    outs = []
    for c in range(len(smalls)):
        us = range(c * ng, (c + 1) * ng)
        y = [mm_nn(cm[i], state[i % ng]) * elac_e[i] + y_in[i] for i in us]
        state = [state[i % ng] * chunk_e[i] + state_in[i] for i in us]
        outs.append([rmsnorm(y[i % ng] * silu(z[i]), normw[i % ng]) for i in us])
    return outs, state


def _params(sem=None):
    return pltpu.CompilerParams(dimension_semantics=sem, vmem_limit_bytes=VMEM_LIMIT)


def _full(shape):
    n = len(shape)
    return pl.BlockSpec(shape, lambda *_: (0,) * n)


ANY = pl.BlockSpec(memory_space=pl.ANY)
HBM = pl.BlockSpec(memory_space=pltpu.HBM)


def in_proj(x, normw, w_main, w_small):
    t = x.shape[0]
    tm, tn = min(2048, t), 512

    def body(x_ref, nw_ref, wm_ref, ws_ref, pm_ref, ps_ref, u_ref):
        @pl.when(pl.program_id(1) == 0)
        def _():
            u = rmsnorm(x_ref[...], nw_ref[...]).astype(MXU_DTYPE)
            u_ref[...] = u
            ps_ref[...] = _raw_dot(u, ws_ref[...], 1, 0)
        pm_ref[...] = _raw_dot(u_ref[...], wm_ref[...], 1, 0)

    return pl.pallas_call(
        body, name="in_proj", grid=(t // tm, COL_CONV // tn),
        in_specs=[pl.BlockSpec((tm, D_MODEL), lambda i, j: (i, 0)), _full((1, D_MODEL)),
                  pl.BlockSpec((D_MODEL, tn), lambda i, j: (0, j)), _full((D_MODEL, LANES))],
        out_specs=[pl.BlockSpec((tm, tn), lambda i, j: (i, j)), pl.BlockSpec((tm, LANES), lambda i, j: (i, 0)),
                   pl.BlockSpec((tm, D_MODEL), lambda i, j: (i, 0))],
        out_shape=[jax.ShapeDtypeStruct((t, COL_CONV), F32), jax.ShapeDtypeStruct((t, LANES), F32),
                   jax.ShapeDtypeStruct((t, D_MODEL), MXU_DTYPE)],
        compiler_params=_params(("arbitrary", "arbitrary")),
    )(x, normw, w_main, w_small)


CONV_TC = 512
HALO = 8


def _shift_down(cur, prev, s):
    rolled = pltpu.roll(cur, s, 0)
    top = jnp.where(_iota((HALO, cur.shape[1]), 0) < s, pltpu.roll(prev, s, 0), rolled[:HALO])
    if cur.shape[0] == HALO:
        return top
    return jnp.concatenate([top, rolled[HALO:]], axis=0)


def _shift_up(cur, nxt, s):
    n = cur.shape[0]
    rolled = pltpu.roll(cur, n - s, 0)
    bot = jnp.where(_iota((HALO, cur.shape[1]), 0) >= HALO - s, pltpu.roll(nxt, HALO - s, 0), rolled[n - HALO:])
    return jnp.concatenate([rolled[:n - HALO], bot], axis=0)


def _conv_pre(cur, prev, w_ref, b, cols=slice(None)):
    acc = cur * w_ref[3:4, cols] + b
    shifted = [cur]
    for s in (1, 2, 3):
        sh = _shift_down(cur, prev, s)
        shifted.append(sh)
        acc = acc + sh * w_ref[3 - s:4 - s, cols]
    return acc, shifted


def in_proj_conv(u, w_main, w, b):
    t = u.shape[0]
    tm, tn = min(2048, t), CONV_TC
    rc = min(256, tm)
    c0, nj = COL_CONV // tn, CONV_W // tn

    def body(u_ref, wm_ref, w_ref, b_ref, out_ref, x_ref, ds_ref, halo_ref):
        j = pl.program_id(1)

        @pl.when(pl.program_id(0) == 0)
        def _():
            halo_ref[j] = jnp.zeros((HALO, tn), F32)

        prev = halo_ref[j]
        for r in range(tm // rc):
            rows = pl.ds(r * rc, rc)
            p = _raw_dot(u_ref[rows, :], wm_ref[...], 1, 0)
            x_ref[rows, :] = p.astype(x_ref.dtype)
            pre, _ = _conv_pre(p, prev, w_ref, b_ref[...])
            sg = sigmoid(pre)
            out_ref[rows, :] = pre * sg
            ds_ref[rows, :] = (sg * (1.0 + pre * (1.0 - sg))).astype(ds_ref.dtype)
            prev = p[rc - HALO:]
        halo_ref[j] = prev

    blk = pl.BlockSpec((tm, tn), lambda i, j: (i, j))
    return pl.pallas_call(
        body, name="in_proj_conv", grid=(t // tm, nj),
        in_specs=[pl.BlockSpec((tm, D_MODEL), lambda i, j: (i, 0)),
                  pl.BlockSpec((D_MODEL, tn), lambda i, j: (0, c0 + j)),
                  pl.BlockSpec((4, tn), lambda i, j: (0, j)), pl.BlockSpec((1, tn), lambda i, j: (0, j))],
        out_specs=[blk, blk, blk],
        out_shape=[jax.ShapeDtypeStruct((t, CONV_W), F32), jax.ShapeDtypeStruct((t, CONV_W), MXU_DTYPE),
                   jax.ShapeDtypeStruct((t, CONV_W), MXU_DTYPE)],
        scratch_shapes=[pltpu.VMEM((nj, HALO, tn), F32)],
        compiler_params=_params(("arbitrary", "arbitrary")),
    )(u, w_main, w, b)


def conv_bwd_w(u, x_conv, dsilu, w, dout, slabbed):
    t = u.shape[0]
    tt, tn = min(1024, t), 3 * CONV_TC
    nt, nj = t // tt, CONV_W // tn
    ns = len(slabbed)
    halo_op = 2 * HALO
    after = lambda i, h: jnp.minimum((i + 1) * (tt // h), t // h - 1)

    def body(u_ref, x_ref, ds_ref, ds_nxt_ref, w_ref, do_ref, do_nxt_ref, *rest):
        slab_refs, (dx_ref, dw_ref, dwb_ref) = rest[:ns], rest[ns:ns + 3]
        land_refs, sems = rest[ns + 3:2 * ns + 3], rest[2 * ns + 3:]
        j, i = pl.program_id(0), pl.program_id(1)
        start, finish = _slab_exchange(slab_refs, land_refs, ns, *sems)

        @pl.when(jnp.logical_and(j == 0, i == 0))
        def _():
            start()

        @pl.when(i == 0)
        def _():
            dw_ref[...] = jnp.zeros(dw_ref.shape, F32)
            dwb_ref[...] = jnp.zeros(dwb_ref.shape, F32)

        uu = u_ref[...]
        row = _iota((HALO, CONV_TC), 0)
        last = i == nt - 1
        for piece in range(tn // CONV_TC):
            cols = slice(piece * CONV_TC, (piece + 1) * CONV_TC)
            x = x_ref[:, cols].astype(F32)
            dpre = do_ref[:, cols].astype(F32) * ds_ref[:, cols].astype(F32)
            dpre_nxt = do_nxt_ref[:, cols].astype(F32)[:HALO] * ds_nxt_ref[:, cols].astype(F32)[:HALO]
            dpre_nxt = jnp.where(last, 0.0, dpre_nxt)
            ups = [dpre] + [_shift_up(dpre, dpre_nxt, s) for s in (1, 2, 3)]
            dx = ups[0] * w_ref[3:4, cols]
            upd = jnp.where(row == 4, jnp.sum(dpre, axis=0, keepdims=True), 0.0)
            for s in range(4):
                if s:
                    dx = dx + ups[s] * w_ref[3 - s:4 - s, cols]
                upd = upd + jnp.where(row == 3 - s, jnp.sum(ups[s] * x, axis=0, keepdims=True), 0.0)
            dx = dx.astype(dx_ref.dtype)
            dx_ref[:, cols] = dx
            dw_ref[:, cols] += _raw_dot(uu, dx, 0, 0)
            dwb_ref[:, cols] += upd

        @pl.when(jnp.logical_and(j == nj - 1, last))
        def _():
            finish()

    out = pl.pallas_call(
        body, name="conv_bwd_w", grid=(nj, nt),
        in_specs=[pl.BlockSpec((tt, D_MODEL), lambda j, i: (i, 0)),
                  pl.BlockSpec((tt, tn), lambda j, i: (i, j)),
                  pl.BlockSpec((tt, tn), lambda j, i: (i, j)),
                  pl.BlockSpec((halo_op, tn), lambda j, i: (after(i, halo_op), j)),
                  pl.BlockSpec((4, tn), lambda j, i: (0, j)),
                  pl.BlockSpec((tt, tn), lambda j, i: (i, j)),
                  pl.BlockSpec((halo_op, tn), lambda j, i: (after(i, halo_op), j))] + [HBM] * ns,
        out_specs=[pl.BlockSpec((tt, tn), lambda j, i: (i, j)), pl.BlockSpec((D_MODEL, tn), lambda j, i: (0, j)),
                   pl.BlockSpec((HALO, tn), lambda j, i: (0, j))] + [HBM] * ns,
        out_shape=[jax.ShapeDtypeStruct((t, CONV_W), MXU_DTYPE), jax.ShapeDtypeStruct((D_MODEL, CONV_W), F32),
                   jax.ShapeDtypeStruct((HALO, CONV_W), F32)] + _slab_exchange_shapes(slabbed, []),
        scratch_shapes=_slab_exchange_sems(ns),
        compiler_params=_params(("arbitrary", "arbitrary")),
    )(u, x_conv, dsilu, dsilu, w, dout, dout, *slabbed)
    return out[0], out[1], out[2], out[3:]


def _ssd_cols(g):
    b0 = SSD_WIDTH + g * SSD_STATE
    c0 = SSD_WIDTH + SSD_GROUPS * SSD_STATE + g * SSD_STATE
    return slice(g * SSD_GW, (g + 1) * SSD_GW), slice(b0, b0 + SSD_STATE), slice(c0, c0 + SSD_STATE)


def _gdn_cols(j):
    return tuple(slice(s * GDN_W + j * GDN_DK, s * GDN_W + (j + 1) * GDN_DK) for s in range(3))


def _ssd_parts(xbc_ref):
    return tuple([[xbc_ref[_chunk_rows(c), _ssd_cols(g)[s]] for g in range(SSD_GROUPS)] for c in range(SSD_CB)]
                 for s in range(3))


def _group_cols(ref):
    return [[ref[_chunk_rows(c), g * SSD_GW:(g + 1) * SSD_GW] for g in range(SSD_GROUPS)] for c in range(SSD_CB)]


def _chunk_rows(c):
    return slice(c * CHUNK, (c + 1) * CHUNK)


def _gdn_parts(qkv_ref):
    assert GDN_HB == GDN_HEADS, "the conv block is read whole: one grid step holds every head"
    return tuple([[qkv_ref[_chunk_rows(c), _gdn_cols(j)[s]] for j in range(GDN_HB)] for c in range(GDN_CB)]
                 for s in range(3))


def _head_cols(ref):
    return [[ref[_chunk_rows(c), j * GDN_DV:(j + 1) * GDN_DV] for j in range(GDN_HB)] for c in range(GDN_CB)]


def _chunk_blocks(ref, n=GDN_CB):
    return [ref[_chunk_rows(c), :] for c in range(n)]


def _first_head():
    return 0 if GDN_HB == GDN_HEADS else pl.program_id(1) * GDN_HB


def ssd_fwd(conv_ssd, proj_main, proj_small, normw, alog, dtb, dvec):
    t = conv_ssd.shape[0]
    rows = CHUNK * SSD_CB
    nc = t // rows
    groups = range(SSD_GROUPS)
    norm_cols = lambda ref: [ref[:, g * SSD_GW:(g + 1) * SSD_GW] for g in groups]

    def body(xbc_ref, z_ref, sm_ref, nw_ref, al_ref, db_ref, dv_ref, y_ref, hist_ref, state_ref):
        @pl.when(pl.program_id(0) == 0)
        def _():
            state_ref[...] = jnp.zeros(state_ref.shape, F32)

        states = [state_ref[g] for g in groups]
        for g in groups:
            hist_ref[0, g] = states[g]
        ys, new_states = ssd_chunk(*_ssd_parts(xbc_ref), _group_cols(z_ref), _chunk_blocks(sm_ref, SSD_CB),
                                   norm_cols(nw_ref), al_ref[...], db_ref[...], dv_ref[...], states)
        for c in range(SSD_CB):
            for g in groups:
                y_ref[_chunk_rows(c), g * SSD_GW:(g + 1) * SSD_GW] = ys[c][g].astype(MXU_DTYPE)
        for g in groups:
            state_ref[g] = new_states[g]

    return pl.pallas_call(
        body, name="ssd_fwd", grid=(nc,),
        in_specs=[pl.BlockSpec((rows, SSD_CONV), lambda c: (c, (COL_SSD - COL_CONV) // SSD_CONV)),
                  pl.BlockSpec((rows, SSD_WIDTH), lambda c: (c, COL_Z // SSD_WIDTH)),
                  pl.BlockSpec((rows, LANES), lambda c: (c, 0)),
                  _full((1, SSD_WIDTH)), _full((1, LANES)), _full((1, LANES)), _full((1, LANES))],
        out_specs=[pl.BlockSpec((rows, SSD_WIDTH), lambda c: (c, 0)),
                   pl.BlockSpec((1, SSD_GROUPS, SSD_STATE, SSD_GW), lambda c: (c, 0, 0, 0))],
        out_shape=[jax.ShapeDtypeStruct((t, SSD_WIDTH), MXU_DTYPE),
                   jax.ShapeDtypeStruct((nc, SSD_GROUPS, SSD_STATE, SSD_GW), F32)],
        scratch_shapes=[pltpu.VMEM((SSD_GROUPS, SSD_STATE, SSD_GW), F32)],
        compiler_params=_params(("arbitrary",)),
    )(conv_ssd, proj_main, proj_small, normw, alog, dtb, dvec)


def _accumulate(ref, first, value):
    @pl.when(first)
    def _():
        ref[...] = value

    @pl.when(jnp.logical_not(first))
    def _():
        ref[...] += value


def ssd_bwd(conv_ssd, proj_main, proj_small, normw, alog, dtb, dvec, hist, dy):
    t = conv_ssd.shape[0]
    rows = CHUNK * SSD_CB
    nc = t // rows
    rev = lambda c: nc - 1 - c
    groups = range(SSD_GROUPS)
    norm_cols = lambda ref: [ref[:, g * SSD_GW:(g + 1) * SSD_GW] for g in groups]

    def body(xbc_ref, z_ref, sm_ref, nw_ref, al_ref, db_ref, dv_ref, hist_ref, dy_ref,
             dxbc_ref, dz_ref, dsm_ref, dnw_ref, dal_ref, ddb_ref, ddv_ref, dstate_ref):
        first = pl.program_id(0) == 0

        @pl.when(first)
        def _():
            dstate_ref[...] = jnp.zeros(dstate_ref.shape, F32)

        _, vjp = jax.vjp(ssd_chunk, *_ssd_parts(xbc_ref), _group_cols(z_ref), _chunk_blocks(sm_ref, SSD_CB),
                         norm_cols(nw_ref), al_ref[...], db_ref[...], dv_ref[...], [hist_ref[0, g] for g in groups])
        dxs, dbm, dcm, dz, dsm, dnw, dal, ddb, ddv, dstate = vjp(
            (_group_cols(dy_ref), [dstate_ref[g] for g in groups]))
        for k in range(SSD_CB):
            rk = _chunk_rows(k)
            for g in groups:
                xc, bc, cc = _ssd_cols(g)
                dxbc_ref[rk, xc] = dxs[k][g].astype(dxbc_ref.dtype)
                dxbc_ref[rk, bc] = dbm[k][g].astype(dxbc_ref.dtype)
                dxbc_ref[rk, cc] = dcm[k][g].astype(dxbc_ref.dtype)
                dz_ref[rk, g * SSD_GW:(g + 1) * SSD_GW] = dz[k][g].astype(dz_ref.dtype)
            dsm_ref[rk, :] = dsm[k]
        for g in groups:
            dstate_ref[g] = dstate[g]
        _accumulate(dnw_ref, first, join_lanes(dnw))
        _accumulate(dal_ref, first, dal)
        _accumulate(ddb_ref, first, ddb)
        _accumulate(ddv_ref, first, ddv)

    return pl.pallas_call(
        body, name="ssd_bwd", grid=(nc,),
        in_specs=[pl.BlockSpec((rows, SSD_CONV), lambda c: (rev(c), (COL_SSD - COL_CONV) // SSD_CONV)),
                  pl.BlockSpec((rows, SSD_WIDTH), lambda c: (rev(c), COL_Z // SSD_WIDTH)),
                  pl.BlockSpec((rows, LANES), lambda c: (rev(c), 0)),
                  _full((1, SSD_WIDTH)), _full((1, LANES)), _full((1, LANES)), _full((1, LANES)),
                  pl.BlockSpec((1, SSD_GROUPS, SSD_STATE, SSD_GW), lambda c: (rev(c), 0, 0, 0)),
                  pl.BlockSpec((rows, SSD_WIDTH), lambda c: (rev(c), 0))],
        out_specs=[pl.BlockSpec((rows, SSD_CONV), lambda c: (rev(c), (COL_SSD - COL_CONV) // SSD_CONV)),
                   pl.BlockSpec((rows, SSD_WIDTH), lambda c: (rev(c), COL_Z // SSD_WIDTH)),
                   pl.BlockSpec((rows, LANES), lambda c: (rev(c), 0)),
                   _full((1, SSD_WIDTH)), _full((1, LANES)), _full((1, LANES)), _full((1, LANES))],
        out_shape=[jax.ShapeDtypeStruct((t, CONV_W), MXU_DTYPE), jax.ShapeDtypeStruct((t, COL_CONV), MXU_DTYPE),
                   jax.ShapeDtypeStruct((t, LANES), F32), jax.ShapeDtypeStruct((1, SSD_WIDTH), F32),
                   jax.ShapeDtypeStruct((1, LANES), F32), jax.ShapeDtypeStruct((1, LANES), F32),
                   jax.ShapeDtypeStruct((1, LANES), F32)],
        scratch_shapes=[pltpu.VMEM((SSD_GROUPS, SSD_STATE, SSD_GW), F32)],
        compiler_params=_params(("arbitrary",)),
    )(conv_ssd, proj_main, proj_small, normw, alog, dtb, dvec, hist, dy)


def gdn_fwd(conv_gdn, proj_main, proj_small, normw, alog, dtb):
    t = conv_gdn.shape[0]
    hb, cb = GDN_HB, GDN_CB
    rows = CHUNK * cb
    ns = t // rows
    gate_blk = COL_GATE // (GDN_DV * hb)

    def body(qkv_ref, gate_ref, sm_ref, nw_ref, al_ref, db_ref, y_ref, hist_ref, t_ref, state_ref):
        h0 = _first_head()

        @pl.when(pl.program_id(0) == 0)
        def _():
            for j in range(hb):
                state_ref[h0 + j] = jnp.zeros((GDN_DK, GDN_DV), F32)

        states = [state_ref[h0 + j] for j in range(hb)]
        for j in range(hb):
            hist_ref[0, j] = states[j]
        qs, ks, vs = _gdn_parts(qkv_ref)
        ys, new_states, ts = gdn_chunk(h0, qs, ks, vs, _chunk_blocks(sm_ref), _head_cols(gate_ref), nw_ref[...],
                                       al_ref[...], db_ref[...], states)
        for c in range(cb):
            for j in range(hb):
                y_ref[_chunk_rows(c), j * GDN_DV:(j + 1) * GDN_DV] = ys[c][j].astype(MXU_DTYPE)
        for j in range(hb):
            state_ref[h0 + j] = new_states[j]
        for p in range(cb * hb // 2):
            t_ref[0, p] = ts[p]

    return pl.pallas_call(
        body, name="gdn_fwd", grid=(ns, GDN_HEADS // hb),
        in_specs=[pl.BlockSpec((rows, GDN_HC * hb), lambda c, h: (c, h)),
                  pl.BlockSpec((rows, GDN_DV * hb), lambda c, h: (c, gate_blk + h)),
                  pl.BlockSpec((rows, LANES), lambda c, h: (c, 0)),
                  _full((1, GDN_DV)), _full((1, LANES)), _full((1, LANES))],
        out_specs=[pl.BlockSpec((rows, GDN_DV * hb), lambda c, h: (c, h)),
                   pl.BlockSpec((1, hb, GDN_DK, GDN_DV), lambda c, h: (c, h, 0, 0)),
                   pl.BlockSpec((1, cb * hb // 2, CHUNK, LANES), lambda c, h: (c, h, 0, 0))],
        out_shape=[jax.ShapeDtypeStruct((t, GDN_W), MXU_DTYPE),
                   jax.ShapeDtypeStruct((ns, GDN_HEADS, GDN_DK, GDN_DV), F32),
                   jax.ShapeDtypeStruct((ns, cb * GDN_HEADS // 2, CHUNK, LANES), F32)],
        scratch_shapes=[pltpu.VMEM((GDN_HEADS, GDN_DK, GDN_DV), F32)],
        compiler_params=_params(("arbitrary", "arbitrary")),
    )(conv_gdn, proj_main, proj_small, normw, alog, dtb)


def gdn_bwd(dproj_main, dconv, conv_gdn, proj_main, proj_small, normw, alog, dtb, hist, t_inv, dy):
    t = conv_gdn.shape[0]
    hb, cb = GDN_HB, GDN_CB
    rows = CHUNK * cb
    ns = t // rows
    rev = lambda c: ns - 1 - c
    gate_blk = COL_GATE // (GDN_DV * hb)

    def body(alias_ref, alias2_ref, qkv_ref, gate_ref, sm_ref, nw_ref, al_ref, db_ref, hist_ref, t_ref, dy_ref,
             dgate_ref, dqkv_ref, dsm_ref, dnw_ref, dal_ref, ddb_ref, dstate_ref):
        del alias_ref, alias2_ref
        c, h = pl.program_id(0), pl.program_id(1)
        h0 = _first_head()

        @pl.when(c == 0)
        def _():
            for j in range(hb):
                dstate_ref[h0 + j] = jnp.zeros((GDN_DK, GDN_DV), F32)

        saved = [t_ref[0, p] for p in range(cb * hb // 2)]

        def fn(qs, ks, vs, smalls, gates, nw, al, db, states):
            return gdn_chunk(h0, qs, ks, vs, smalls, gates, nw, al, db, states, saved)[:2]

        qs, ks, vs = _gdn_parts(qkv_ref)
        _, vjp = jax.vjp(fn, qs, ks, vs, _chunk_blocks(sm_ref), _head_cols(gate_ref), nw_ref[...], al_ref[...],
                         db_ref[...], [hist_ref[0, j] for j in range(hb)])
        dqs, dks, dvs, dsm, dgates, dnw, dal, ddb, dstates = vjp(
            (_head_cols(dy_ref), [dstate_ref[h0 + j] for j in range(hb)]))
        for k in range(cb):
            rk = _chunk_rows(k)
            for j in range(hb):
                qc, kc, vc = _gdn_cols(j)
                dqkv_ref[rk, qc] = dqs[k][j].astype(dqkv_ref.dtype)
                dqkv_ref[rk, kc] = dks[k][j].astype(dqkv_ref.dtype)
                dqkv_ref[rk, vc] = dvs[k][j].astype(dqkv_ref.dtype)
                dgate_ref[rk, j * GDN_DV:(j + 1) * GDN_DV] = dgates[k][j].astype(dgate_ref.dtype)
        for j in range(hb):
            dstate_ref[h0 + j] = dstates[j]
        _accumulate(dsm_ref, h == 0, jnp.concatenate(dsm, axis=0))
        first = jnp.logical_and(c == 0, h == 0)
        _accumulate(dnw_ref, first, dnw)
        _accumulate(dal_ref, first, dal)
        _accumulate(ddb_ref, first, ddb)

    return pl.pallas_call(
        body, name="gdn_bwd", grid=(ns, GDN_HEADS // hb),
        in_specs=[ANY, ANY, pl.BlockSpec((rows, GDN_HC * hb), lambda c, h: (rev(c), h)),
                  pl.BlockSpec((rows, GDN_DV * hb), lambda c, h: (rev(c), gate_blk + h)),
                  pl.BlockSpec((rows, LANES), lambda c, h: (rev(c), 0)),
                  _full((1, GDN_DV)), _full((1, LANES)), _full((1, LANES)),
                  pl.BlockSpec((1, hb, GDN_DK, GDN_DV), lambda c, h: (rev(c), h, 0, 0)),
                  pl.BlockSpec((1, cb * hb // 2, CHUNK, LANES), lambda c, h: (rev(c), h, 0, 0)),
                  pl.BlockSpec((rows, GDN_DV * hb), lambda c, h: (rev(c), h))],
        out_specs=[pl.BlockSpec((rows, GDN_DV * hb), lambda c, h: (rev(c), gate_blk + h)),
                   pl.BlockSpec((rows, GDN_HC * hb), lambda c, h: (rev(c), h)),
                   pl.BlockSpec((rows, LANES), lambda c, h: (rev(c), 0)),
                   _full((1, GDN_DV)), _full((1, LANES)), _full((1, LANES))],
        out_shape=[jax.ShapeDtypeStruct(dproj_main.shape, dproj_main.dtype),
                   jax.ShapeDtypeStruct(dconv.shape, dconv.dtype),
                   jax.ShapeDtypeStruct((t, LANES), F32), jax.ShapeDtypeStruct((1, GDN_DV), F32),
                   jax.ShapeDtypeStruct((1, LANES), F32), jax.ShapeDtypeStruct((1, LANES), F32)],
        scratch_shapes=[pltpu.VMEM((GDN_HEADS, GDN_DK, GDN_DV), F32)],
        input_output_aliases={0: 0, 1: 1},
        compiler_params=_params(("arbitrary", "arbitrary")),
    )(dproj_main, dconv, conv_gdn, proj_main, proj_small, normw, alog, dtb, hist, t_inv, dy)


def out_proj_loss(x, y_ssd, y_gdn, w_out, final_w, target):
    t = x.shape[0]
    tm = min(512, t)

    def body(x_ref, ys_ref, yg_ref, wo_ref, fw_ref, tg_ref, loss_ref, dhid_ref, dys_ref, dyg_ref, dwo_ref, dfw_ref):
        i = pl.program_id(0)
        ys, yg = ys_ref[...], yg_ref[...]
        wo_s, wo_g = wo_ref[:SSD_WIDTH, :], wo_ref[SSD_WIDTH:, :]
        hid = x_ref[...] + _raw_dot(ys, wo_s, 1, 0) + _raw_dot(yg, wo_g, 1, 0)
        out, vjp = jax.vjp(rmsnorm, hid, fw_ref[...])
        err = out - tg_ref[...]
        loss = 0.5 * jnp.sum(jnp.mean(err * err, axis=-1, keepdims=True), axis=0, keepdims=True)
        dhid, dfw = vjp(err * (1.0 / D_MODEL))
        dhid_ref[...] = dhid
        dys_ref[...] = _raw_dot(dhid, wo_s, 1, 1)
        dyg_ref[...] = _raw_dot(dhid, wo_g, 1, 1)
        first = i == 0
        _accumulate(loss_ref, first, jnp.broadcast_to(loss, loss_ref.shape))
        _accumulate(dfw_ref, first, dfw)

        @pl.when(first)
        def _():
            dwo_ref[:SSD_WIDTH, :] = _raw_dot(ys, dhid, 0, 0)
            dwo_ref[SSD_WIDTH:, :] = _raw_dot(yg, dhid, 0, 0)

        @pl.when(i > 0)
        def _():
            dwo_ref[:SSD_WIDTH, :] += _raw_dot(ys, dhid, 0, 0)
            dwo_ref[SSD_WIDTH:, :] += _raw_dot(yg, dhid, 0, 0)

    row = lambda w: pl.BlockSpec((tm, w), lambda i: (i, 0))
    return pl.pallas_call(
        body, name="out_proj_loss", grid=(t // tm,),
        in_specs=[row(D_MODEL), row(SSD_WIDTH), row(GDN_W), _full((SSD_WIDTH + GDN_W, D_MODEL)), _full((1, D_MODEL)),
                  row(D_MODEL)],
        out_specs=[_full((8, LANES)), row(D_MODEL), row(SSD_WIDTH), row(GDN_W), _full((SSD_WIDTH + GDN_W, D_MODEL)),
                   _full((1, D_MODEL))],
        out_shape=[jax.ShapeDtypeStruct((8, LANES), F32), jax.ShapeDtypeStruct((t, D_MODEL), F32),
                   jax.ShapeDtypeStruct((t, SSD_WIDTH), F32), jax.ShapeDtypeStruct((t, GDN_W), F32),
                   jax.ShapeDtypeStruct((SSD_WIDTH + GDN_W, D_MODEL), F32), jax.ShapeDtypeStruct((1, D_MODEL), F32)],
        compiler_params=_params(("arbitrary",)),
    )(x, y_ssd, y_gdn, w_out, final_w, target)


def in_proj_bwd_x(x, normw, w_main, w_small, dproj_main, dproj_conv, dsmall_a, dsmall_b, dhid, slabbed):
    t = x.shape[0]
    tm = min(256, t)
    ni = t // tm
    ns = len(slabbed)

    def body(x_ref, nw_ref, wm_ref, ws_ref, dp_ref, dc_ref, da_ref, db_ref, dh_ref, *rest):
        slab_refs, (gx_ref, dnw_ref), land_refs = rest[:ns], rest[ns:ns + 2], rest[ns + 2:2 * ns + 2]
        sems = rest[2 * ns + 2:]
        i = pl.program_id(0)
        start, finish = _slab_exchange(slab_refs, land_refs, ns, *sems)

        @pl.when(i == 0)
        def _():
            start()

        du = (_raw_dot(dp_ref[...], wm_ref[:, :COL_CONV], 1, 1) + _raw_dot(dc_ref[...], wm_ref[:, COL_CONV:], 1, 1)
              + _raw_dot(da_ref[...] + db_ref[...], ws_ref[...], 1, 1))
        _, vjp = jax.vjp(rmsnorm, x_ref[...], nw_ref[...])
        dx, dnw = vjp(du)
        gx_ref[...] = dx + dh_ref[...]
        _accumulate(dnw_ref, i == 0, dnw)

        @pl.when(i == ni - 1)
        def _():
            finish()

    row = lambda w: pl.BlockSpec((tm, w), lambda i: (i, 0))
    out = pl.pallas_call(
        body, name="in_proj_bwd_x", grid=(ni,),
        in_specs=[row(D_MODEL), _full((1, D_MODEL)), _full((D_MODEL, MAIN)), _full((D_MODEL, LANES)), row(COL_CONV),
                  row(CONV_W), row(LANES), row(LANES), row(D_MODEL)] + [HBM] * ns,
        out_specs=[row(D_MODEL), _full((1, D_MODEL))] + [HBM] * ns,
        out_shape=[jax.ShapeDtypeStruct((t, D_MODEL), F32), jax.ShapeDtypeStruct((1, D_MODEL), F32)]
        + _slab_exchange_shapes(slabbed, []),
        scratch_shapes=_slab_exchange_sems(ns),
        compiler_params=_params(("arbitrary",)),
    )(x, normw, w_main, w_small, dproj_main, dproj_conv, dsmall_a, dsmall_b, dhid, *slabbed)
    return out[0], out[1], out[2:]


def in_proj_bwd_w(u, dproj_main, dsmall_a, dsmall_b):
    t = u.shape[0]
    tm, tn = min(2048, t), COL_CONV // 2

    def body(u_ref, dp_ref, da_ref, db_ref, dwm_ref, dws_ref):
        j, i = pl.program_id(0), pl.program_id(1)
        uu = u_ref[...]
        _accumulate(dwm_ref, i == 0, _raw_dot(uu, dp_ref[...], 0, 0))

        @pl.when(j == 0)
        def _():
            _accumulate(dws_ref, i == 0, _raw_dot(uu, da_ref[...] + db_ref[...], 0, 0))

    return pl.pallas_call(
        body, name="in_proj_bwd_w", grid=(COL_CONV // tn, t // tm),
        in_specs=[pl.BlockSpec((tm, D_MODEL), lambda j, i: (i, 0)), pl.BlockSpec((tm, tn), lambda j, i: (i, j)),
                  pl.BlockSpec((tm, LANES), lambda j, i: (i, 0)), pl.BlockSpec((tm, LANES), lambda j, i: (i, 0))],
        out_specs=[pl.BlockSpec((D_MODEL, tn), lambda j, i: (0, j)), _full((D_MODEL, LANES))],
        out_shape=[jax.ShapeDtypeStruct((D_MODEL, COL_CONV), F32), jax.ShapeDtypeStruct((D_MODEL, LANES), F32)],
        compiler_params=_params(("arbitrary", "arbitrary")),
    )(u, dproj_main, dsmall_a, dsmall_b)


def sum_slabs(a, name):
    n, rows, cols = a.shape
    tr = 64 if rows % 64 == 0 else rows

    def body(a_ref, o_ref):
        acc = a_ref[0].astype(F32)
        for d in range(1, n):
            acc = acc + a_ref[d].astype(F32)
        o_ref[...] = acc

    return pl.pallas_call(
        body, name=name, grid=(rows // tr,),
        in_specs=[pl.BlockSpec((n, tr, cols), lambda i: (0, i, 0))],
        out_specs=pl.BlockSpec((tr, cols), lambda i: (i, 0)),
        out_shape=jax.ShapeDtypeStruct((rows, cols), F32),
        compiler_params=_params(("arbitrary",)),
    )(a)


def adamw(w, g, m, v, name):
    _, rows, cols = w.shape
    tr = 128 if rows % 128 == 0 else rows

    def body(w_ref, g_ref, m_ref, v_ref, d_ref, nm_ref, nv_ref):
        gg = g_ref[...]
        nm = ADAM_B1 * m_ref[...] + (1.0 - ADAM_B1) * gg
        nv = ADAM_B2 * v_ref[...] + (1.0 - ADAM_B2) * (gg * gg)
        m_hat = nm / (1.0 - ADAM_B1 ** ADAM_STEP)
        v_hat = nv / (1.0 - ADAM_B2 ** ADAM_STEP)
        d_ref[...] = -ADAM_LR * (m_hat / (jnp.sqrt(v_hat) + ADAM_EPS) + ADAM_WD * w_ref[...])
        nm_ref[...] = nm
        nv_ref[...] = nv

    spec = pl.BlockSpec((1, tr, cols), lambda i: (0, i, 0))
    shp = jax.ShapeDtypeStruct((1, rows, cols), F32)
    return pl.pallas_call(
        body, name=name, grid=(rows // tr,), in_specs=[spec] * 4, out_specs=[spec] * 3, out_shape=[shp] * 3,
        compiler_params=_params(("arbitrary",)),
    )(w, g.reshape(w.shape), m, v)


def _my_place():
    return lax.axis_index("x"), lax.axis_index("y"), lax.axis_index("c")


def gather_weights(big, small):
    nb, n = len(big), len(big) + len(small)
    parts = 4

    def body(*refs):
        srcs, outs = refs[:n], refs[n:2 * n]
        land_a, land_b = refs[2 * n:2 * n + nb], refs[2 * n + nb:2 * n + 2 * nb]
        send_sems, recv_sems, fwd_send, fwd_recv, local_sems = refs[2 * n + 2 * nb:]
        x, y, c = _my_place()
        me = 2 * x + y
        chips = [(1 - x, y), (x, 1 - y), (1 - x, 1 - y)]
        half = [a.shape[0] // 2 for a in big]

        def ici(j, i):
            px, py = chips[j]
            if i < nb:
                src, dst = srcs[i].at[pl.ds(c * half[i], half[i])], land_a[i].at[j]
            else:
                src, dst = srcs[i], outs[i].at[me]
            return pltpu.make_async_remote_copy(src_ref=src, dst_ref=dst, send_sem=send_sems.at[j * n + i],
                                                recv_sem=recv_sems.at[j * n + i], device_id=(px, py, c),
                                                device_id_type=MESH)

        def ici_arrival(j, i):
            px, py = chips[j]
            dst = land_a[i].at[j] if i < nb else outs[i].at[2 * px + py]
            return pltpu.make_async_remote_copy(src_ref=dst, dst_ref=dst, send_sem=send_sems.at[j * n + i],
                                                recv_sem=recv_sems.at[j * n + i], device_id=(px, py, c),
                                                device_id_type=MESH)

        def forward(j, i, p):
            rows = half[i] // parts
            k = (j * nb + i) * parts + p
            return pltpu.make_async_remote_copy(
                src_ref=land_a[i].at[j, pl.ds(p * rows, rows)], dst_ref=land_b[i].at[j, pl.ds(p * rows, rows)],
                send_sem=fwd_send.at[k], recv_sem=fwd_recv.at[k], device_id=(x, y, 1 - c), device_id_type=MESH)

        def store(j, i, from_sibling):
            px, py = chips[j]
            buf, h = (land_b, 1 - c) if from_sibling else (land_a, c)
            k = n + (j * nb + i) * 2 + (1 if from_sibling else 0)
            return pltpu.make_async_copy(buf[i].at[j], outs[i].at[2 * px + py, pl.ds(h * half[i], half[i])],
                                         local_sems.at[k])

        own = [pltpu.make_async_copy(srcs[i], outs[i].at[me], local_sems.at[i]) for i in range(n)]
        sends = [ici(j, i) for j in range(3) for i in range(n)]
        for cp in own + sends:
            cp.start()
        pending = []
        for j in range(3):
            for i in range(n):
                ici_arrival(j, i).wait_recv()
                if i < nb:
                    fw = [forward(j, i, p) for p in range(parts)]
                    st = store(j, i, False)
                    for cp in fw + [st]:
                        cp.start()
                    pending += [cp.wait_send for cp in fw] + [st.wait]
        for j in range(3):
            for i in range(nb):
                for p in range(parts):
                    forward(j, i, p).wait_recv()
                st = store(j, i, True)
                st.start()
                pending.append(st.wait)
        for cp in sends:
            cp.wait_send()
        for wait in pending:
            wait()
        for cp in own:
            cp.wait()

    shards = list(big) + list(small)
    lands = [pltpu.VMEM((3, a.shape[0] // 2) + a.shape[1:], a.dtype) for a in big]
    return pl.pallas_call(
        body, name="gather_weights",
        in_specs=[HBM] * n, out_specs=[HBM] * n,
        out_shape=[jax.ShapeDtypeStruct((N_CHIP,) + s.shape, s.dtype) for s in shards],
        scratch_shapes=lands + lands + [
            pltpu.SemaphoreType.DMA((3 * n,)), pltpu.SemaphoreType.DMA((3 * n,)),
            pltpu.SemaphoreType.DMA((3 * nb * parts,)), pltpu.SemaphoreType.DMA((3 * nb * parts,)),
            pltpu.SemaphoreType.DMA((n + 6 * nb,))],
        compiler_params=pltpu.CompilerParams(vmem_limit_bytes=VMEM_LIMIT),
    )(*shards)


def _peer(x, y, c, mask):
    mx, my, mc = (mask >> 2) & 1, (mask >> 1) & 1, mask & 1
    return (x ^ mx if mx else x, y ^ my if my else y, c ^ mc if mc else c)


def _slab_exchange_shapes(slabbed, replicated):
    return ([jax.ShapeDtypeStruct(a.shape, a.dtype) for a in slabbed]
            + [jax.ShapeDtypeStruct((N_DEV,) + a.shape, a.dtype) for a in replicated])


def _slab_exchange_sems(n):
    return [pltpu.SemaphoreType.DMA((7 * n,)), pltpu.SemaphoreType.DMA((7 * n,)), pltpu.SemaphoreType.DMA((n,))]


def _slab_exchange(srcs, outs, ns, send_sems, recv_sems, local_sems):
    n = len(srcs)
    x, y, c = _my_place()
    me = 4 * x + 2 * y + c

    def piece(i, dev):
        return srcs[i].at[dev] if i < ns else srcs[i]

    def copies(arriving):
        out = []
        for mask in range(1, N_DEV):
            px, py, pc = _peer(x, y, c, mask)
            dev = 4 * px + 2 * py + pc
            for i in range(n):
                k = (mask - 1) * n + i
                out.append(pltpu.make_async_remote_copy(
                    src_ref=piece(i, dev), dst_ref=outs[i].at[dev if arriving else me], send_sem=send_sems.at[k],
                    recv_sem=recv_sems.at[k], device_id=(px, py, pc), device_id_type=MESH))
        return out

    def local():
        return [pltpu.make_async_copy(piece(i, me), outs[i].at[me], local_sems.at[i]) for i in range(n)]

    def start():
        for cp in local() + copies(False):
            cp.start()

    def finish():
        for cp in copies(True):
            cp.wait_recv()
        for cp in copies(False):
            cp.wait_send()
        for cp in local():
            cp.wait()

    return start, finish


def exchange_halves(landed, replicated):
    n, nr = len(landed), len(replicated)
    streams = 8
    halves = [jax.ShapeDtypeStruct(a.shape[1:], F32) for a in landed]
    sum_rows = 64

    def body(*refs):
        srcs, rep_srcs, outs, rep_outs = refs[:n], refs[n:n + nr], refs[n + nr:2 * n + nr], refs[2 * n + nr:2 * (n + nr)]
        refs = refs[2 * (n + nr):]
        slabs, mine, theirs = refs[:n], refs[n:2 * n], refs[2 * n:3 * n]
        send_sems, recv_sems, in_sems, out_sems = refs[3 * n:3 * n + 4]
        rep_start, rep_finish = _slab_exchange(rep_srcs, rep_outs, 0, *refs[3 * n + 4:])
        rep_start()
        x, y, c = _my_place()
        loads = [pltpu.make_async_copy(srcs[i], slabs[i], in_sems.at[i]) for i in range(n)]
        for cp in loads:
            cp.start()
        for i in range(n):
            loads[i].wait()
            for r in range(0, halves[i].shape[0], sum_rows):
                rows = pl.ds(r, sum_rows)
                acc = slabs[i][0, rows, :].astype(F32)
                for d in range(1, N_DEV):
                    acc = acc + slabs[i][d, rows, :].astype(F32)
                mine[i][rows, :] = acc

        def chunk_copy(i, s):
            rows = halves[i].shape[0] // streams
            k = i * streams + s
            return pltpu.make_async_remote_copy(
                src_ref=mine[i].at[pl.ds(s * rows, rows)], dst_ref=theirs[i].at[pl.ds(s * rows, rows)],
                send_sem=send_sems.at[k], recv_sem=recv_sems.at[k], device_id=(x, y, 1 - c), device_id_type=MESH)

        sends = [chunk_copy(i, s) for i in range(n) for s in range(streams)]
        for cp in sends:
            cp.start()
        own = [pltpu.make_async_copy(mine[i], outs[i].at[c], out_sems.at[i]) for i in range(n)]
        for cp in own:
            cp.start()
        for cp in sends:
            cp.wait_recv()
        got = [pltpu.make_async_copy(theirs[i], outs[i].at[1 - c], out_sems.at[n + i]) for i in range(n)]
        for cp in got:
            cp.start()
        for cp in sends:
            cp.wait_send()
        for cp in own + got:
            cp.wait()
        rep_finish()

    vmem = [pltpu.VMEM(a.shape, a.dtype) for a in halves]
    out = pl.pallas_call(
        body, name="exchange_halves",
        in_specs=[HBM] * (n + nr), out_specs=[HBM] * (n + nr),
        out_shape=[jax.ShapeDtypeStruct((2,) + a.shape, a.dtype) for a in halves]
        + _slab_exchange_shapes([], replicated),
        scratch_shapes=[pltpu.VMEM(a.shape, a.dtype) for a in landed] + vmem + vmem
        + [pltpu.SemaphoreType.DMA((n * streams,)), pltpu.SemaphoreType.DMA((n * streams,)),
           pltpu.SemaphoreType.DMA((n,)), pltpu.SemaphoreType.DMA((2 * n,))] + _slab_exchange_sems(nr),
        compiler_params=pltpu.CompilerParams(vmem_limit_bytes=VMEM_LIMIT),
    )(*landed, *replicated)
    return out[:n], out[n:]


def _pack_cols(pieces):
    offs, pos = [], 0
    for a in pieces:
        offs.append(pos)
        pos += a.shape[1]
    rows8 = [jnp.pad(a.astype(F32), ((0, 8 - a.shape[0]), (0, 0))) for a in pieces]
    return jnp.concatenate(rows8, axis=1), offs


def adamw_many(ws, gs, ms, vs):
    n = len(ws)

    def body(*refs):
        w_r, g_r, m_r, v_r = refs[:n], refs[n:2 * n], refs[2 * n:3 * n], refs[3 * n:4 * n]
        d_o, m_o, v_o = refs[4 * n:5 * n], refs[5 * n:6 * n], refs[6 * n:7 * n]
        for i in range(n):
            gg = g_r[i][...]
            nm = ADAM_B1 * m_r[i][...] + (1.0 - ADAM_B1) * gg
            nv = ADAM_B2 * v_r[i][...] + (1.0 - ADAM_B2) * (gg * gg)
            m_hat = nm / (1.0 - ADAM_B1 ** ADAM_STEP)
            v_hat = nv / (1.0 - ADAM_B2 ** ADAM_STEP)
            d_o[i][...] = -ADAM_LR * (m_hat / (jnp.sqrt(v_hat) + ADAM_EPS) + ADAM_WD * w_r[i][...])
            m_o[i][...] = nm
            v_o[i][...] = nv

    shapes = [jax.ShapeDtypeStruct(w.shape, F32) for w in ws]
    out = pl.pallas_call(body, name="adamw_small", out_shape=shapes * 3,
                         compiler_params=pltpu.CompilerParams(vmem_limit_bytes=VMEM_LIMIT))(*ws, *gs, *ms, *vs)
    return out[:n], out[n:2 * n], out[2 * n:]


def _lanes(vec, start):
    n = vec.shape[-1]
    return jnp.pad(vec.reshape(1, n).astype(F32), ((0, 0), (start, LANES - start - n)))


def kernel(x, norm_w, w_in, ssd_conv_w, ssd_conv_b, ssd_dt_bias, ssd_a_log, ssd_d, ssd_norm_w, gdn_conv_w, gdn_dt_bias, gdn_a_log, gdn_norm_w, w_out, final_norm_w, loss_target, m_norm_w, m_w_in, m_ssd_conv_w, m_ssd_conv_b, m_ssd_dt_bias, m_ssd_a_log, m_ssd_d, m_ssd_norm_w, m_gdn_conv_w, m_gdn_dt_bias, m_gdn_a_log, m_gdn_norm_w, m_w_out, m_final_norm_w, v_norm_w, v_w_in, v_ssd_conv_w, v_ssd_conv_b, v_ssd_dt_bias, v_ssd_a_log, v_ssd_d, v_ssd_norm_w, v_gdn_conv_w, v_gdn_dt_bias, v_gdn_a_log, v_gdn_norm_w, v_w_out, v_final_norm_w):
    xs = x[0]
    target = loss_target[0]
    chip = 2 * lax.axis_index("x") + lax.axis_index("y")
    w_in_shard, w_out_shard = w_in[0], w_out[0]
    in_cols = w_in_shard.shape[1]
    out_rows = w_out_shard.shape[0]

    g_in, g_out, g_cs, g_cg = gather_weights(
        [w_in_shard.astype(MXU_DTYPE), w_out_shard.astype(MXU_DTYPE)], [ssd_conv_w[0], gdn_conv_w[0]])
    w_in_full = jnp.concatenate([g_in[k] for k in range(N_CHIP)], axis=1)
    w_out_full = g_out.reshape(N_CHIP * out_rows, D_MODEL)
    cw_ssd = jnp.concatenate([g_cs[k] for k in range(N_CHIP)], axis=1)
    cw_gdn = jnp.concatenate([g_cg[k] for k in range(N_CHIP)], axis=1)
    cb_ssd, cb_gdn = ssd_conv_b, jnp.zeros((1, GDN_CONV), F32)
    o_xbc, o_dt, o_gate, o_qkv, o_ab = 1024, 2560, 2576, 3600, 6672
    w_main = jnp.concatenate([w_in_full[:, :o_xbc], w_in_full[:, o_gate:o_qkv], w_in_full[:, o_qkv:o_ab],
                              w_in_full[:, o_xbc:o_dt]], axis=1)
    w_small = jnp.concatenate([w_in_full[:, o_dt:o_gate], w_in_full[:, o_ab:],
                               jnp.zeros((D_MODEL, LANES - 32), MXU_DTYPE)], axis=1)
    alog = _lanes(ssd_a_log, 0) + _lanes(gdn_a_log, LANE_GA)
    dtb = _lanes(ssd_dt_bias, 0) + _lanes(gdn_dt_bias, LANE_GA)
    dvec = _lanes(ssd_d, 0)
    fw = final_norm_w.reshape(1, D_MODEL)

    cw, cb = jnp.concatenate([cw_gdn, cw_ssd], axis=1), jnp.concatenate([cb_gdn, cb_ssd], axis=1)
    proj_main, proj_small, u = in_proj(xs, norm_w, w_main, w_small)
    conv_out, x_conv, dsilu_conv = in_proj_conv(u, w_main, cw, cb)
    y_ssd, hist_ssd = ssd_fwd(conv_out, proj_main, proj_small, ssd_norm_w, alog, dtb, dvec)
    y_gdn, hist_gdn, tinv_gdn = gdn_fwd(conv_out, proj_main, proj_small, gdn_norm_w, alog, dtb)

    loss_blk, dhid, dy_ssd, dy_gdn, d_w_out, d_fw = out_proj_loss(xs, y_ssd, y_gdn, w_out_full, fw, target)
    dconv, dproj_main, dsmall_ssd, d_ssd_nw, d_alog_s, d_dtb_s, d_dvec = ssd_bwd(
        conv_out, proj_main, proj_small, ssd_norm_w, alog, dtb, dvec, hist_ssd, dy_ssd)
    dproj_main, dconv, dsmall_gdn, d_gdn_nw, d_alog_g, d_dtb_g = gdn_bwd(
        dproj_main, dconv, conv_out, proj_main, proj_small, gdn_norm_w, alog, dtb, hist_gdn, tinv_gdn, dy_gdn)
    slabs_out = d_w_out.reshape(N_DEV, out_rows // 2, D_MODEL).astype(COMM_DTYPE)
    dproj_conv, d_w_conv, dwb, (r_out,) = conv_bwd_w(u, x_conv, dsilu_conv, cw, dconv, [slabs_out])
    dwb_gdn, dwb_ssd = dwb[:, :GDN_CONV], dwb[:, GDN_CONV:]
    d_w_zg, d_w_small = in_proj_bwd_w(u, dproj_main, dsmall_ssd, dsmall_gdn)
    order = [(d_w_zg, 0, COL_GATE), (d_w_conv, COL_SSD - COL_CONV, CONV_W), (d_w_small, 0, 16),
             (d_w_zg, COL_GATE, COL_CONV), (d_w_conv, 0, COL_SSD - COL_CONV), (d_w_small, 16, 32)]
    shards, pos = [[] for _ in range(N_CHIP)], 0
    for src, lo, hi in order:
        while lo < hi:
            k = pos // in_cols
            n = min(hi - lo, (k + 1) * in_cols - pos)
            shards[k].append(src[:, lo:lo + n].astype(COMM_DTYPE))
            lo, pos = lo + n, pos + n
    slabs_in = jnp.stack([jnp.concatenate(p, axis=1) for p in shards]).reshape(N_DEV, D_MODEL // 2, in_cols)
    grad_x, d_norm_w, (r_in,) = in_proj_bwd_x(xs, norm_w, w_main, w_small, dproj_main, dproj_conv, dsmall_ssd,
                                               dsmall_gdn, dhid, [slabs_in])
    d_alog, d_dtb = d_alog_s + d_alog_g, d_dtb_s + d_dtb_g
    packed, (o_nw, o_cs, o_cg, o_snw, o_fw, o_al, o_db, o_dv, o_gnw, o_loss) = _pack_cols([
        d_norm_w, dwb_ssd, dwb_gdn,
        d_ssd_nw.reshape(1, SSD_WIDTH), d_fw, d_alog, d_dtb, d_dvec, d_gdn_nw, loss_blk])

    (full_in, full_out), (r_small,) = exchange_halves([r_in, r_out], [packed])
    tot = sum_slabs(r_small, "sum_small")
    grad_w_in = full_in.reshape(D_MODEL, in_cols)
    grad_w_out = full_out.reshape(out_rows, D_MODEL)
    loss = tot[0, o_loss]
    sc, gc = ssd_conv_w.shape[2], gdn_conv_w.shape[2]
    row = lambda off, n, r=0: tot[r:r + 1, off:off + n]
    gs = [row(o_nw, D_MODEL),
          lax.dynamic_slice(tot, (0, o_cs + chip * sc), (4, sc)),
          row(o_cs, SSD_CONV, 4),
          row(o_db, SSD_HEADS), row(o_al, SSD_HEADS), row(o_dv, SSD_HEADS),
          row(o_snw, SSD_WIDTH),
          lax.dynamic_slice(tot, (0, o_cg + chip * gc), (4, gc)),
          row(o_db + LANE_GA, GDN_HEADS), row(o_al + LANE_GA, GDN_HEADS),
          row(o_gnw, GDN_DV), row(o_fw, D_MODEL)]

    names = ["norm_w", "ssd_conv_w", "ssd_conv_b", "ssd_dt_bias", "ssd_a_log", "ssd_d", "ssd_norm_w", "gdn_conv_w",
             "gdn_dt_bias", "gdn_a_log", "gdn_norm_w", "final_norm_w"]
    ws = [norm_w, ssd_conv_w, ssd_conv_b, ssd_dt_bias, ssd_a_log, ssd_d, ssd_norm_w, gdn_conv_w, gdn_dt_bias,
          gdn_a_log, gdn_norm_w, final_norm_w]
    ms = [m_norm_w, m_ssd_conv_w, m_ssd_conv_b, m_ssd_dt_bias, m_ssd_a_log, m_ssd_d, m_ssd_norm_w, m_gdn_conv_w,
          m_gdn_dt_bias, m_gdn_a_log, m_gdn_norm_w, m_final_norm_w]
    vs = [v_norm_w, v_ssd_conv_w, v_ssd_conv_b, v_ssd_dt_bias, v_ssd_a_log, v_ssd_d, v_ssd_norm_w, v_gdn_conv_w,
          v_gdn_dt_bias, v_gdn_a_log, v_gdn_norm_w, v_final_norm_w]
    shapes = [w.shape for w in ws]
    flat = lambda arrs: [a.reshape(g.shape) for a, g in zip(arrs, gs)]
    d_s, m_s, v_s = adamw_many(flat(ws), gs, flat(ms), flat(vs))
    back = lambda arrs: dict(zip(names, [a.reshape(s) for a, s in zip(arrs, shapes)]))
    delta, new_m, new_v, grads = back(d_s), back(m_s), back(v_s), back(gs)
    d_in, m_in, v_in = adamw(w_in, grad_w_in, m_w_in, v_w_in, "adamw_w_in")
    d_out, m_out, v_out = adamw(w_out, grad_w_out, m_w_out, v_w_out, "adamw_w_out")
    for tbl, a_in, a_out in ((grads, grad_w_in[None], grad_w_out[None]), (delta, d_in, d_out), (new_m, m_in, m_out),
                             (new_v, v_in, v_out)):
        tbl["w_in"] = a_in
        tbl["w_out"] = a_out

    order = ["norm_w", "w_in", "ssd_conv_w", "ssd_conv_b", "ssd_dt_bias", "ssd_a_log", "ssd_d", "ssd_norm_w",
             "gdn_conv_w", "gdn_dt_bias", "gdn_a_log", "gdn_norm_w", "w_out", "final_norm_w"]
    return (loss.reshape(()), grad_x[None], *[grads[k] for k in order], *[delta[k] for k in order],
            *[new_m[k] for k in order], *[new_v[k] for k in order])
```

```python
import functools

import jax
import jax.numpy as jnp
from jax import lax
from jax.experimental import pallas as pl
from jax.experimental.pallas import tpu as pltpu

F32 = jnp.float32
MXU_DTYPE = jnp.bfloat16
COMM_DTYPE = jnp.bfloat16
MESH = pl.DeviceIdType.MESH

D_MODEL = 1024
CHUNK = 64
EPS = 1e-6
SSD_HEADS, SSD_GROUPS, SSD_STATE = 16, 2, 128
SSD_WIDTH, SSD_CONV = 1024, 1536
SSD_GW = SSD_WIDTH // SSD_GROUPS
GDN_HEADS, GDN_DK, GDN_DV = 8, 128, 128
GDN_W, GDN_CONV = 1024, 3072
GDN_HC = 2 * GDN_DK + GDN_DV
IN_DIM = 6688
MAIN = 6656
LANES = 128
COL_Z, COL_GATE, COL_GDN, COL_SSD = 0, 1024, 2048, 5120
COL_CONV = COL_GDN
CONV_W = MAIN - COL_CONV
GDN_HB = 8
GDN_CB = 4
SSD_CB = 4
LANE_GA, LANE_GB = 16, 24
N_DEV, N_CHIP = 8, 4
VMEM_LIMIT = 52 * 1024 * 1024

ADAM_LR, ADAM_B1, ADAM_B2, ADAM_EPS, ADAM_WD, ADAM_STEP = 0.001, 0.9, 0.999, 1e-08, 0.01, 10


def _split(a, n):
    parts, rest = [], a.astype(F32)
    for i in range(n):
        p = rest.astype(MXU_DTYPE)
        parts.append(p)
        if i < n - 1:
            rest = rest - p.astype(F32)
    return parts


def _raw_dot(a, b, ca, cb, mode="bf16"):
    d = lambda u, v: lax.dot_general(u, v, (((ca,), (cb,)), ((), ())), preferred_element_type=F32)
    if mode == "bf16":
        return d(a.astype(MXU_DTYPE), b.astype(MXU_DTYPE))
    if mode == "x3":
        (ah, al), (bh, bl) = _split(a, 2), _split(b, 2)
        return d(ah, bh) + (d(ah, bl) + d(al, bh))
    if mode == "sel_a":
        a0 = a.astype(MXU_DTYPE)
        b1, b2, b3 = _split(b, 3)
        return d(a0, b1) + (d(a0, b2) + d(a0, b3))
    assert mode == "sel_b", mode
    b0 = b.astype(MXU_DTYPE)
    a1, a2, a3 = _split(a, 3)
    return d(a1, b0) + (d(a2, b0) + d(a3, b0))


@functools.partial(jax.custom_vjp, nondiff_argnums=(2,))
def mm_nn(a, b, mode="bf16"):
    return _raw_dot(a, b, 1, 0, mode)


@functools.partial(jax.custom_vjp, nondiff_argnums=(2,))
def mm_nt(a, b, mode="bf16"):
    return _raw_dot(a, b, 1, 1, mode)


@functools.partial(jax.custom_vjp, nondiff_argnums=(2,))
def mm_tn(a, b, mode="bf16"):
    return _raw_dot(a, b, 0, 0, mode)


_SAME = {"bf16": ("bf16", "bf16"), "x3": ("x3", "x3")}
_NN_BWD = dict(_SAME, sel_a=("bf16", "sel_a"), sel_b=("sel_b", "bf16"))
_NT_BWD = dict(_SAME, sel_a=("bf16", "sel_b"), sel_b=("sel_b", "bf16"))
_TN_BWD = dict(_SAME, sel_a=("bf16", "sel_a"), sel_b=("sel_a", "bf16"))
mm_nn.defvjp(lambda a, b, m: (_raw_dot(a, b, 1, 0, m), (a, b)),
             lambda m, r, g: (mm_nt(g, r[1], _NN_BWD[m][0]), mm_tn(r[0], g, _NN_BWD[m][1])))
mm_nt.defvjp(lambda a, b, m: (_raw_dot(a, b, 1, 1, m), (a, b)),
             lambda m, r, g: (mm_nn(g, r[1], _NT_BWD[m][0]), mm_tn(g, r[0], _NT_BWD[m][1])))
mm_tn.defvjp(lambda a, b, m: (_raw_dot(a, b, 0, 0, m), (a, b)),
             lambda m, r, g: (mm_nt(r[1], g, _TN_BWD[m][0]), mm_nn(r[0], g, _TN_BWD[m][1])))


@jax.custom_jvp
def sigmoid(x):
    return 1.0 / (1.0 + jnp.exp(-x))


@sigmoid.defjvp
def _sigmoid_jvp(p, t):
    s = sigmoid(p[0])
    return s, t[0] * s * (1.0 - s)


@jax.custom_jvp
def softplus(x):
    return jnp.maximum(x, 0.0) + jnp.log(1.0 + jnp.exp(-jnp.abs(x)))


@softplus.defjvp
def _softplus_jvp(p, t):
    return softplus(p[0]), t[0] * sigmoid(p[0])


def silu(x):
    return x * sigmoid(x)


def rmsnorm(x, w):
    return x * lax.rsqrt(jnp.mean(x * x, axis=-1, keepdims=True) + EPS) * w


def _iota(shape, dim):
    return lax.broadcasted_iota(jnp.int32, shape, dim)


def _halves():
    lane = _iota((1, LANES), 1) >> 6
    return _ind(lane == 0), _ind(lane == 1)


def _block_diag(pair):
    h0, h1 = _halves()
    return jnp.concatenate([pair * h0, pair * h1], axis=0)


def _tri_inv_impl(mats):
    r, c = _iota((CHUNK, LANES), 0), _iota((CHUNK, LANES), 1) & (CHUNK - 1)
    eye = _ind(r == c)
    blockdiag = _ind((r >> 4) == (c >> 4))
    dot = lambda u, v: _raw_dot(u, _block_diag(v), 1, 0, "x3")
    dot1 = lambda u, v: _raw_dot(u, _block_diag(v), 1, 0)
    each = lambda f, *ls: [f(*xs) for xs in zip(*ls)]
    dg = each(lambda a: a * blockdiag, mats)
    off = each(lambda a, d: a - d, mats, dg)
    m = each(lambda d: -d, dg)
    p = each(lambda x: eye + x, m)
    pw = m
    for _ in range(3):
        pw = each(lambda x: dot1(x, x), pw)
        p = each(lambda x, y: x + dot1(x, y), p, pw)
    e = each(dot, p, off)
    e2 = each(lambda x: dot1(x, x), e)
    q = each(lambda x: eye - x, e)
    q = each(lambda x, y: x + dot1(x, y), q, e2)
    return each(dot, q, p)


def _tri_inv_bwd(ts, gs):
    h0, h1 = _halves()
    x = [mm_nt(g, _block_diag(t)) for g, t in zip(gs, ts)]
    full = [mm_tn(t, y) for t, y in zip(ts, x)]
    return [-(f[:CHUNK] * h0 + f[CHUNK:] * h1) for f in full]


@jax.custom_vjp
def tri_inv(mats):
    return _tri_inv_impl(mats)


def _tri_inv_fwd(mats):
    ts = _tri_inv_impl(mats)
    return ts, ts


tri_inv.defvjp(_tri_inv_fwd, lambda ts, gs: (_tri_inv_bwd(ts, gs),))


@jax.custom_vjp
def tri_inv_saved(mats, ts):
    del mats
    return ts


tri_inv_saved.defvjp(lambda mats, ts: (ts, ts),
                     lambda ts, gs: (_tri_inv_bwd(ts, gs), [jnp.zeros_like(t) for t in ts]))


def _ind(cond):
    return jnp.where(cond, 1.0, 0.0).astype(F32)


def _chunk_masks():
    r, c = _iota((CHUNK, CHUNK), 0), _iota((CHUNK, CHUNK), 1)
    return _ind(r >= c), _ind(r > c), _ind(r == c), _ind(_iota((CHUNK, 1), 0) == CHUNK - 1)


def _log_decay_cumsum(small, alog, dtb, tri):
    sp = softplus(small + dtb)
    la = -jnp.exp(alog) * sp
    return sp, mm_nn(tri, la, "sel_a")


def _col_of(x, lane):
    return jnp.sum(x * _ind(_iota((1, LANES), 1) == lane), axis=1, keepdims=True)


def _pair_masks():
    r, c = _iota((CHUNK, LANES), 0), _iota((CHUNK, LANES), 1)
    c6 = c & (CHUNK - 1)
    return _ind(r >= c6), _ind(r > c6), (_ind(c == r), _ind(c == r + CHUNK))


def _decay_pair(col_a, col_b, tri_w, eye_w):
    h0, h1 = _halves()
    col = col_a * h0 + col_b * h1
    row = jnp.sum(col_a * eye_w[0] + col_b * eye_w[1], axis=0, keepdims=True)
    return jnp.exp((col - row) * tri_w) * tri_w


def gdn_chunk(h0, qs, ks, vs, smalls, gates, normw, alog, dtb, states, saved_t=None):
    tri, _, _, last = _chunk_masks()
    tri_w, strict_w, eye_w = _pair_masks()
    nh = len(qs[0])
    flat = lambda xss: [x for xs in xss for x in xs]
    lacs = [_log_decay_cumsum(sm, alog, dtb, tri)[1] for sm in smalls]
    qs, ks, vs, gates = flat(qs), flat(ks), flat(vs), flat(gates)
    heads, pairs = range(len(qs)), range(len(qs) // 2)
    each = lambda f, *ls: [f(*xs) for xs in zip(*ls)]
    ab = lambda xs, p: (xs[2 * p], xs[2 * p + 1])
    stack = lambda xs: jnp.concatenate(xs, axis=0)
    gc = [_col_of(lacs[i // nh], LANE_GA + h0 + i % nh) for i in heads]
    beta = [sigmoid(_col_of(smalls[i // nh], LANE_GB + h0 + i % nh)) for i in heads]
    decay = [_decay_pair(*ab(gc, p), tri_w, eye_w) for p in pairs]
    gl = each(lambda x: jnp.sum(x * last, axis=0, keepdims=True), gc)
    q = each(lambda x: x * lax.rsqrt(jnp.sum(x * x, axis=-1, keepdims=True) + EPS) * (GDN_DK ** -0.5), qs)
    k = each(lambda x: x * lax.rsqrt(jnp.sum(x * x, axis=-1, keepdims=True) + EPS), ks)
    kb = each(lambda x, b: x * b, k, beta)
    eg = each(jnp.exp, gc)
    zero = jnp.zeros((CHUNK, GDN_DK), F32)
    k_bd = [stack([join_lanes([k[2 * p], zero]), join_lanes([zero, k[2 * p + 1]])]) for p in pairs]
    a = [mm_nt(join_lanes(list(ab(kb, p))), k_bd[p]) * (decay[p] * strict_w) for p in pairs]
    t = tri_inv(a) if saved_t is None else tri_inv_saved(a, saved_t)
    attn = [mm_nt(join_lanes(list(ab(q, p))), k_bd[p]) * decay[p] for p in pairs]
    rhs = [stack([join_lanes([vs[h] * beta[h], kb[h] * eg[h]]) for h in (2 * p, 2 * p + 1)]) for p in pairs]
    uw = [mm_nn(_block_diag(t[p]), rhs[p]) for p in pairs]
    uw = [x for p in pairs for x in split_rows(uw[p])]
    u, w = zip(*[split_lanes(x) for x in uw])
    ys = []
    for c in range(len(smalls)):
        hs = range(c * nh, (c + 1) * nh)
        v_new = [u[i] - mm_nn(w[i], states[i % nh]) for i in hs]
        av = [mm_nn(_block_diag(attn[c * nh // 2 + p]), stack(list(ab(v_new, p)))) for p in range(nh // 2)]
        av = [x for y in av for x in split_rows(y)]
        o = [mm_nn(q[i] * eg[i], states[i % nh]) + av[i % nh] for i in hs]
        states = [states[i % nh] * jnp.exp(gl[i]) + mm_tn(k[i] * jnp.exp(gl[i] - gc[i]), v_new[i % nh]) for i in hs]
        ys.append([rmsnorm(o[i % nh], normw) * silu(gates[i]) for i in hs])
    return ys, states, t


@jax.custom_vjp
def split_rows(x):
    n = x.shape[0] // 2
    return [x[:n], x[n:]]


split_rows.defvjp(lambda x: (split_rows(x), None), lambda _, gs: (jnp.concatenate(gs, axis=0),))


@jax.custom_vjp
def split_lanes(x):
    return [x[:, i * LANES:(i + 1) * LANES] for i in range(x.shape[1] // LANES)]


@jax.custom_vjp
def join_lanes(xs):
    return jnp.concatenate(xs, axis=1)


split_lanes.defvjp(lambda x: (split_lanes(x), None), lambda _, gs: (join_lanes(gs),))
join_lanes.defvjp(lambda xs: (join_lanes(xs), None), lambda _, g: (split_lanes(g),))


def ssd_chunk(xs, bm, cm, z, smalls, normw, alog, dtb, dvec, state):
    tri, _, _, last = _chunk_masks()
    tri_w, _, eye_w = _pair_masks()
    h0, h1 = _halves()
    hpg = SSD_HEADS // SSD_GROUPS
    ng = len(normw)
    flat = lambda xss: [x for xs_ in xss for x in xs_]
    xs, bm, cm, z = flat(xs), flat(bm), flat(cm), flat(z)
    units, pairs = range(len(xs)), range(hpg // 2)
    each = lambda f, *ls: [f(*a) for a in zip(*ls)]
    sp_lac = [_log_decay_cumsum(sm, alog, dtb, tri) for sm in smalls]
    lac_last = [jnp.sum(lac * last, axis=0, keepdims=True) for _, lac in sp_lac]
    sel = [_ind(_iota((LANES, SSD_GW), 0) == g * hpg + (_iota((LANES, SSD_GW), 1) >> 6)) for g in range(ng)]
    expand = lambda vs, mode: [mm_nn(vs[i // ng], sel[i % ng], mode) for i in units]
    dt_e = expand([sp for sp, _ in sp_lac], "bf16")
    elac_e = expand([jnp.exp(lac) for _, lac in sp_lac], "bf16")
    toend_e = expand([jnp.exp(ll - lac) for (_, lac), ll in zip(sp_lac, lac_last)], "bf16")
    row8 = _iota((8, 1), 0)
    two_e = expand([_ind(row8 == 0) * dvec + _ind(row8 == 1) * jnp.exp(ll) for ll in lac_last], "sel_b")
    d_e = each(lambda v: jnp.sum(v * _ind(row8 == 0), axis=0, keepdims=True), two_e)
    chunk_e = each(lambda v: jnp.sum(v * _ind(row8 == 1), axis=0, keepdims=True), two_e)
    xdt = each(lambda a, b: a * b, xs, dt_e)
    cb_w = each(lambda c_, b_: mm_nt(c_, jnp.concatenate([b_, b_], axis=0)), cm, bm)
    x_pairs = each(split_lanes, xdt)
    col = lambda i, j: _col_of(sp_lac[i // ng][1], (i % ng) * hpg + j)
    lms = [[_decay_pair(col(i, 2 * p), col(i, 2 * p + 1), tri_w, eye_w) for p in pairs] for i in units]
    stacked = [[jnp.concatenate([x_pairs[i][p] * h0, x_pairs[i][p] * h1], axis=0) for p in pairs] for i in units]
    terms = [[mm_nn(cb_w[i] * lms[i][p], stacked[i][p]) for p in pairs] for i in units]
    y_in = [join_lanes(terms[i]) + xs[i] * d_e[i] for i in units]
    state_in = each(lambda b_, xd, te: mm_tn(b_, xd * te), bm, xdt, toend_e)
    outs = []
    for c in range(len(smalls)):
        us = range(c * ng, (c + 1) * ng)
        y = [mm_nn(cm[i], state[i % ng]) * elac_e[i] + y_in[i] for i in us]
        state = [state[i % ng] * chunk_e[i] + state_in[i] for i in us]
        outs.append([rmsnorm(y[i % ng] * silu(z[i]), normw[i % ng]) for i in us])
    return outs, state


def _params(sem=None):
    return pltpu.CompilerParams(dimension_semantics=sem, vmem_limit_bytes=VMEM_LIMIT)


def _full(shape):
    n = len(shape)
    return pl.BlockSpec(shape, lambda *_: (0,) * n)


ANY = pl.BlockSpec(memory_space=pl.ANY)
HBM = pl.BlockSpec(memory_space=pltpu.HBM)


def in_proj(x, normw, w_main, w_small):
    t = x.shape[0]
    tm, tn = min(2048, t), 512

    def body(x_ref, nw_ref, wm_ref, ws_ref, pm_ref, ps_ref, u_ref):
        @pl.when(pl.program_id(1) == 0)
        def _():
            u = rmsnorm(x_ref[...], nw_ref[...]).astype(MXU_DTYPE)
            u_ref[...] = u
            ps_ref[...] = _raw_dot(u, ws_ref[...], 1, 0)
        pm_ref[...] = _raw_dot(u_ref[...], wm_ref[...], 1, 0)

    return pl.pallas_call(
        body, name="in_proj", grid=(t // tm, COL_CONV // tn),
        in_specs=[pl.BlockSpec((tm, D_MODEL), lambda i, j: (i, 0)), _full((1, D_MODEL)),
                  pl.BlockSpec((D_MODEL, tn), lambda i, j: (0, j)), _full((D_MODEL, LANES))],
        out_specs=[pl.BlockSpec((tm, tn), lambda i, j: (i, j)), pl.BlockSpec((tm, LANES), lambda i, j: (i, 0)),
                   pl.BlockSpec((tm, D_MODEL), lambda i, j: (i, 0))],
        out_shape=[jax.ShapeDtypeStruct((t, COL_CONV), F32), jax.ShapeDtypeStruct((t, LANES), F32),
                   jax.ShapeDtypeStruct((t, D_MODEL), MXU_DTYPE)],
        compiler_params=_params(("arbitrary", "arbitrary")),
    )(x, normw, w_main, w_small)


CONV_TC = 512
HALO = 8


def _shift_down(cur, prev, s):
    rolled = pltpu.roll(cur, s, 0)
    top = jnp.where(_iota((HALO, cur.shape[1]), 0) < s, pltpu.roll(prev, s, 0), rolled[:HALO])
    if cur.shape[0] == HALO:
        return top
    return jnp.concatenate([top, rolled[HALO:]], axis=0)


def _shift_up(cur, nxt, s):
    n = cur.shape[0]
    rolled = pltpu.roll(cur, n - s, 0)
    bot = jnp.where(_iota((HALO, cur.shape[1]), 0) >= HALO - s, pltpu.roll(nxt, HALO - s, 0), rolled[n - HALO:])
    return jnp.concatenate([rolled[:n - HALO], bot], axis=0)


def _conv_pre(cur, prev, w_ref, b, cols=slice(None)):
    acc = cur * w_ref[3:4, cols] + b
    shifted = [cur]
    for s in (1, 2, 3):
        sh = _shift_down(cur, prev, s)
        shifted.append(sh)
        acc = acc + sh * w_ref[3 - s:4 - s, cols]
    return acc, shifted


def in_proj_conv(u, w_main, w, b):
    t = u.shape[0]
    tm, tn = min(2048, t), CONV_TC
    rc = min(256, tm)
    c0, nj = COL_CONV // tn, CONV_W // tn

    def body(u_ref, wm_ref, w_ref, b_ref, out_ref, x_ref, ds_ref, halo_ref):
        j = pl.program_id(1)

        @pl.when(pl.program_id(0) == 0)
        def _():
            halo_ref[j] = jnp.zeros((HALO, tn), F32)

        prev = halo_ref[j]
        for r in range(tm // rc):
            rows = pl.ds(r * rc, rc)
            p = _raw_dot(u_ref[rows, :], wm_ref[...], 1, 0)
            x_ref[rows, :] = p.astype(x_ref.dtype)
            pre, _ = _conv_pre(p, prev, w_ref, b_ref[...])
            sg = sigmoid(pre)
            out_ref[rows, :] = pre * sg
            ds_ref[rows, :] = (sg * (1.0 + pre * (1.0 - sg))).astype(ds_ref.dtype)
            prev = p[rc - HALO:]
        halo_ref[j] = prev

    blk = pl.BlockSpec((tm, tn), lambda i, j: (i, j))
    return pl.pallas_call(
        body, name="in_proj_conv", grid=(t // tm, nj),
        in_specs=[pl.BlockSpec((tm, D_MODEL), lambda i, j: (i, 0)),
                  pl.BlockSpec((D_MODEL, tn), lambda i, j: (0, c0 + j)),
                  pl.BlockSpec((4, tn), lambda i, j: (0, j)), pl.BlockSpec((1, tn), lambda i, j: (0, j))],
        out_specs=[blk, blk, blk],
        out_shape=[jax.ShapeDtypeStruct((t, CONV_W), F32), jax.ShapeDtypeStruct((t, CONV_W), MXU_DTYPE),
                   jax.ShapeDtypeStruct((t, CONV_W), MXU_DTYPE)],
        scratch_shapes=[pltpu.VMEM((nj, HALO, tn), F32)],
        compiler_params=_params(("arbitrary", "arbitrary")),
    )(u, w_main, w, b)


def conv_bwd_w(u, x_conv, dsilu, w, dout, slabbed):
    t = u.shape[0]
    tt, tn = min(1024, t), 3 * CONV_TC
    nt, nj = t // tt, CONV_W // tn
    ns = len(slabbed)
    halo_op = 2 * HALO
    after = lambda i, h: jnp.minimum((i + 1) * (tt // h), t // h - 1)

    def body(u_ref, x_ref, ds_ref, ds_nxt_ref, w_ref, do_ref, do_nxt_ref, *rest):
        slab_refs, (dx_ref, dw_ref, dwb_ref) = rest[:ns], rest[ns:ns + 3]
        land_refs, sems = rest[ns + 3:2 * ns + 3], rest[2 * ns + 3:]
        j, i = pl.program_id(0), pl.program_id(1)
        start, finish = _slab_exchange(slab_refs, land_refs, ns, *sems)

        @pl.when(jnp.logical_and(j == 0, i == 0))
        def _():
            start()

        @pl.when(i == 0)
        def _():
            dw_ref[...] = jnp.zeros(dw_ref.shape, F32)
            dwb_ref[...] = jnp.zeros(dwb_ref.shape, F32)

        uu = u_ref[...]
        row = _iota((HALO, CONV_TC), 0)
        last = i == nt - 1
        for piece in range(tn // CONV_TC):
            cols = slice(piece * CONV_TC, (piece + 1) * CONV_TC)
            x = x_ref[:, cols].astype(F32)
            dpre = do_ref[:, cols].astype(F32) * ds_ref[:, cols].astype(F32)
            dpre_nxt = do_nxt_ref[:, cols].astype(F32)[:HALO] * ds_nxt_ref[:, cols].astype(F32)[:HALO]
            dpre_nxt = jnp.where(last, 0.0, dpre_nxt)
            ups = [dpre] + [_shift_up(dpre, dpre_nxt, s) for s in (1, 2, 3)]
            dx = ups[0] * w_ref[3:4, cols]
            upd = jnp.where(row == 4, jnp.sum(dpre, axis=0, keepdims=True), 0.0)
            for s in range(4):
                if s:
                    dx = dx + ups[s] * w_ref[3 - s:4 - s, cols]
                upd = upd + jnp.where(row == 3 - s, jnp.sum(ups[s] * x, axis=0, keepdims=True), 0.0)
            dx = dx.astype(dx_ref.dtype)
            dx_ref[:, cols] = dx
            dw_ref[:, cols] += _raw_dot(uu, dx, 0, 0)
            dwb_ref[:, cols] += upd

        @pl.when(jnp.logical_and(j == nj - 1, last))
        def _():
            finish()

    out = pl.pallas_call(
        body, name="conv_bwd_w", grid=(nj, nt),
        in_specs=[pl.BlockSpec((tt, D_MODEL), lambda j, i: (i, 0)),
                  pl.BlockSpec((tt, tn), lambda j, i: (i, j)),
                  pl.BlockSpec((tt, tn), lambda j, i: (i, j)),
                  pl.BlockSpec((halo_op, tn), lambda j, i: (after(i, halo_op), j)),
                  pl.BlockSpec((4, tn), lambda j, i: (0, j)),
                  pl.BlockSpec((tt, tn), lambda j, i: (i, j)),
                  pl.BlockSpec((halo_op, tn), lambda j, i: (after(i, halo_op), j))] + [HBM] * ns,
        out_specs=[pl.BlockSpec((tt, tn), lambda j, i: (i, j)), pl.BlockSpec((D_MODEL, tn), lambda j, i: (0, j)),
                   pl.BlockSpec((HALO, tn), lambda j, i: (0, j))] + [HBM] * ns,
        out_shape=[jax.ShapeDtypeStruct((t, CONV_W), MXU_DTYPE), jax.ShapeDtypeStruct((D_MODEL, CONV_W), F32),
                   jax.ShapeDtypeStruct((HALO, CONV_W), F32)] + _slab_exchange_shapes(slabbed, []),
        scratch_shapes=_slab_exchange_sems(ns),
        compiler_params=_params(("arbitrary", "arbitrary")),
    )(u, x_conv, dsilu, dsilu, w, dout, dout, *slabbed)
    return out[0], out[1], out[2], out[3:]


def _ssd_cols(g):
    b0 = SSD_WIDTH + g * SSD_STATE
    c0 = SSD_WIDTH + SSD_GROUPS * SSD_STATE + g * SSD_STATE
    return slice(g * SSD_GW, (g + 1) * SSD_GW), slice(b0, b0 + SSD_STATE), slice(c0, c0 + SSD_STATE)


def _gdn_cols(j):
    return tuple(slice(s * GDN_W + j * GDN_DK, s * GDN_W + (j + 1) * GDN_DK) for s in range(3))


def _ssd_parts(xbc_ref):
    return tuple([[xbc_ref[_chunk_rows(c), _ssd_cols(g)[s]] for g in range(SSD_GROUPS)] for c in range(SSD_CB)]
                 for s in range(3))


def _group_cols(ref):
    return [[ref[_chunk_rows(c), g * SSD_GW:(g + 1) * SSD_GW] for g in range(SSD_GROUPS)] for c in range(SSD_CB)]


def _chunk_rows(c):
    return slice(c * CHUNK, (c + 1) * CHUNK)


def _gdn_parts(qkv_ref):
    assert GDN_HB == GDN_HEADS, "the conv block is read whole: one grid step holds every head"
    return tuple([[qkv_ref[_chunk_rows(c), _gdn_cols(j)[s]] for j in range(GDN_HB)] for c in range(GDN_CB)]
                 for s in range(3))


def _head_cols(ref):
    return [[ref[_chunk_rows(c), j * GDN_DV:(j + 1) * GDN_DV] for j in range(GDN_HB)] for c in range(GDN_CB)]


def _chunk_blocks(ref, n=GDN_CB):
    return [ref[_chunk_rows(c), :] for c in range(n)]


def _first_head():
    return 0 if GDN_HB == GDN_HEADS else pl.program_id(1) * GDN_HB


def ssd_fwd(conv_ssd, proj_main, proj_small, normw, alog, dtb, dvec):
    t = conv_ssd.shape[0]
    rows = CHUNK * SSD_CB
    nc = t // rows
    groups = range(SSD_GROUPS)
    norm_cols = lambda ref: [ref[:, g * SSD_GW:(g + 1) * SSD_GW] for g in groups]

    def body(xbc_ref, z_ref, sm_ref, nw_ref, al_ref, db_ref, dv_ref, y_ref, hist_ref, state_ref):
        @pl.when(pl.program_id(0) == 0)
        def _():
            state_ref[...] = jnp.zeros(state_ref.shape, F32)

        states = [state_ref[g] for g in groups]
        for g in groups:
            hist_ref[0, g] = states[g]
        ys, new_states = ssd_chunk(*_ssd_parts(xbc_ref), _group_cols(z_ref), _chunk_blocks(sm_ref, SSD_CB),
                                   norm_cols(nw_ref), al_ref[...], db_ref[...], dv_ref[...], states)
        for c in range(SSD_CB):
            for g in groups:
                y_ref[_chunk_rows(c), g * SSD_GW:(g + 1) * SSD_GW] = ys[c][g].astype(MXU_DTYPE)
        for g in groups:
            state_ref[g] = new_states[g]

    return pl.pallas_call(
        body, name="ssd_fwd", grid=(nc,),
        in_specs=[pl.BlockSpec((rows, SSD_CONV), lambda c: (c, (COL_SSD - COL_CONV) // SSD_CONV)),
                  pl.BlockSpec((rows, SSD_WIDTH), lambda c: (c, COL_Z // SSD_WIDTH)),
                  pl.BlockSpec((rows, LANES), lambda c: (c, 0)),
                  _full((1, SSD_WIDTH)), _full((1, LANES)), _full((1, LANES)), _full((1, LANES))],
        out_specs=[pl.BlockSpec((rows, SSD_WIDTH), lambda c: (c, 0)),
                   pl.BlockSpec((1, SSD_GROUPS, SSD_STATE, SSD_GW), lambda c: (c, 0, 0, 0))],
        out_shape=[jax.ShapeDtypeStruct((t, SSD_WIDTH), MXU_DTYPE),
                   jax.ShapeDtypeStruct((nc, SSD_GROUPS, SSD_STATE, SSD_GW), F32)],
        scratch_shapes=[pltpu.VMEM((SSD_GROUPS, SSD_STATE, SSD_GW), F32)],
        compiler_params=_params(("arbitrary",)),
    )(conv_ssd, proj_main, proj_small, normw, alog, dtb, dvec)


def _accumulate(ref, first, value):
    @pl.when(first)
    def _():
        ref[...] = value

    @pl.when(jnp.logical_not(first))
    def _():
        ref[...] += value


def ssd_bwd(conv_ssd, proj_main, proj_small, normw, alog, dtb, dvec, hist, dy):
    t = conv_ssd.shape[0]
    rows = CHUNK * SSD_CB
    nc = t // rows
    rev = lambda c: nc - 1 - c
    groups = range(SSD_GROUPS)
    norm_cols = lambda ref: [ref[:, g * SSD_GW:(g + 1) * SSD_GW] for g in groups]

    def body(xbc_ref, z_ref, sm_ref, nw_ref, al_ref, db_ref, dv_ref, hist_ref, dy_ref,
             dxbc_ref, dz_ref, dsm_ref, dnw_ref, dal_ref, ddb_ref, ddv_ref, dstate_ref):
        first = pl.program_id(0) == 0

        @pl.when(first)
        def _():
            dstate_ref[...] = jnp.zeros(dstate_ref.shape, F32)

        _, vjp = jax.vjp(ssd_chunk, *_ssd_parts(xbc_ref), _group_cols(z_ref), _chunk_blocks(sm_ref, SSD_CB),
                         norm_cols(nw_ref), al_ref[...], db_ref[...], dv_ref[...], [hist_ref[0, g] for g in groups])
        dxs, dbm, dcm, dz, dsm, dnw, dal, ddb, ddv, dstate = vjp(
            (_group_cols(dy_ref), [dstate_ref[g] for g in groups]))
        for k in range(SSD_CB):
            rk = _chunk_rows(k)
            for g in groups:
                xc, bc, cc = _ssd_cols(g)
                dxbc_ref[rk, xc] = dxs[k][g].astype(dxbc_ref.dtype)
                dxbc_ref[rk, bc] = dbm[k][g].astype(dxbc_ref.dtype)
                dxbc_ref[rk, cc] = dcm[k][g].astype(dxbc_ref.dtype)
                dz_ref[rk, g * SSD_GW:(g + 1) * SSD_GW] = dz[k][g].astype(dz_ref.dtype)
            dsm_ref[rk, :] = dsm[k]
        for g in groups:
            dstate_ref[g] = dstate[g]
        _accumulate(dnw_ref, first, join_lanes(dnw))
        _accumulate(dal_ref, first, dal)
        _accumulate(ddb_ref, first, ddb)
        _accumulate(ddv_ref, first, ddv)

    return pl.pallas_call(
        body, name="ssd_bwd", grid=(nc,),
        in_specs=[pl.BlockSpec((rows, SSD_CONV), lambda c: (rev(c), (COL_SSD - COL_CONV) // SSD_CONV)),
                  pl.BlockSpec((rows, SSD_WIDTH), lambda c: (rev(c), COL_Z // SSD_WIDTH)),
                  pl.BlockSpec((rows, LANES), lambda c: (rev(c), 0)),
                  _full((1, SSD_WIDTH)), _full((1, LANES)), _full((1, LANES)), _full((1, LANES)),
                  pl.BlockSpec((1, SSD_GROUPS, SSD_STATE, SSD_GW), lambda c: (rev(c), 0, 0, 0)),
                  pl.BlockSpec((rows, SSD_WIDTH), lambda c: (rev(c), 0))],
        out_specs=[pl.BlockSpec((rows, SSD_CONV), lambda c: (rev(c), (COL_SSD - COL_CONV) // SSD_CONV)),
                   pl.BlockSpec((rows, SSD_WIDTH), lambda c: (rev(c), COL_Z // SSD_WIDTH)),
                   pl.BlockSpec((rows, LANES), lambda c: (rev(c), 0)),
                   _full((1, SSD_WIDTH)), _full((1, LANES)), _full((1, LANES)), _full((1, LANES))],
        out_shape=[jax.ShapeDtypeStruct((t, CONV_W), MXU_DTYPE), jax.ShapeDtypeStruct((t, COL_CONV), MXU_DTYPE),
                   jax.ShapeDtypeStruct((t, LANES), F32), jax.ShapeDtypeStruct((1, SSD_WIDTH), F32),
                   jax.ShapeDtypeStruct((1, LANES), F32), jax.ShapeDtypeStruct((1, LANES), F32),
                   jax.ShapeDtypeStruct((1, LANES), F32)],
        scratch_shapes=[pltpu.VMEM((SSD_GROUPS, SSD_STATE, SSD_GW), F32)],
        compiler_params=_params(("arbitrary",)),
    )(conv_ssd, proj_main, proj_small, normw, alog, dtb, dvec, hist, dy)


def gdn_fwd(conv_gdn, proj_main, proj_small, normw, alog, dtb):
    t = conv_gdn.shape[0]
    hb, cb = GDN_HB, GDN_CB
    rows = CHUNK * cb
    ns = t // rows
    gate_blk = COL_GATE // (GDN_DV * hb)

    def body(qkv_ref, gate_ref, sm_ref, nw_ref, al_ref, db_ref, y_ref, hist_ref, t_ref, state_ref):
        h0 = _first_head()

        @pl.when(pl.program_id(0) == 0)
        def _():
            for j in range(hb):
                state_ref[h0 + j] = jnp.zeros((GDN_DK, GDN_DV), F32)

        states = [state_ref[h0 + j] for j in range(hb)]
        for j in range(hb):
            hist_ref[0, j] = states[j]
        qs, ks, vs = _gdn_parts(qkv_ref)
        ys, new_states, ts = gdn_chunk(h0, qs, ks, vs, _chunk_blocks(sm_ref), _head_cols(gate_ref), nw_ref[...],
                                       al_ref[...], db_ref[...], states)
        for c in range(cb):
            for j in range(hb):
                y_ref[_chunk_rows(c), j * GDN_DV:(j + 1) * GDN_DV] = ys[c][j].astype(MXU_DTYPE)
        for j in range(hb):
            state_ref[h0 + j] = new_states[j]
        for p in range(cb * hb // 2):
            t_ref[0, p] = ts[p]

    return pl.pallas_call(
        body, name="gdn_fwd", grid=(ns, GDN_HEADS // hb),
        in_specs=[pl.BlockSpec((rows, GDN_HC * hb), lambda c, h: (c, h)),
                  pl.BlockSpec((rows, GDN_DV * hb), lambda c, h: (c, gate_blk + h)),
                  pl.BlockSpec((rows, LANES), lambda c, h: (c, 0)),
                  _full((1, GDN_DV)), _full((1, LANES)), _full((1, LANES))],
        out_specs=[pl.BlockSpec((rows, GDN_DV * hb), lambda c, h: (c, h)),
                   pl.BlockSpec((1, hb, GDN_DK, GDN_DV), lambda c, h: (c, h, 0, 0)),
                   pl.BlockSpec((1, cb * hb // 2, CHUNK, LANES), lambda c, h: (c, h, 0, 0))],
        out_shape=[jax.ShapeDtypeStruct((t, GDN_W), MXU_DTYPE),
                   jax.ShapeDtypeStruct((ns, GDN_HEADS, GDN_DK, GDN_DV), F32),
                   jax.ShapeDtypeStruct((ns, cb * GDN_HEADS // 2, CHUNK, LANES), F32)],
        scratch_shapes=[pltpu.VMEM((GDN_HEADS, GDN_DK, GDN_DV), F32)],
        compiler_params=_params(("arbitrary", "arbitrary")),
    )(conv_gdn, proj_main, proj_small, normw, alog, dtb)


def gdn_bwd(dproj_main, dconv, conv_gdn, proj_main, proj_small, normw, alog, dtb, hist, t_inv, dy):
    t = conv_gdn.shape[0]
    hb, cb = GDN_HB, GDN_CB
    rows = CHUNK * cb
    ns = t // rows
    rev = lambda c: ns - 1 - c
    gate_blk = COL_GATE // (GDN_DV * hb)

    def body(alias_ref, alias2_ref, qkv_ref, gate_ref, sm_ref, nw_ref, al_ref, db_ref, hist_ref, t_ref, dy_ref,
             dgate_ref, dqkv_ref, dsm_ref, dnw_ref, dal_ref, ddb_ref, dstate_ref):
        del alias_ref, alias2_ref
        c, h = pl.program_id(0), pl.program_id(1)
        h0 = _first_head()

        @pl.when(c == 0)
        def _():
            for j in range(hb):
                dstate_ref[h0 + j] = jnp.zeros((GDN_DK, GDN_DV), F32)

        saved = [t_ref[0, p] for p in range(cb * hb // 2)]

        def fn(qs, ks, vs, smalls, gates, nw, al, db, states):
            return gdn_chunk(h0, qs, ks, vs, smalls, gates, nw, al, db, states, saved)[:2]

        qs, ks, vs = _gdn_parts(qkv_ref)
        _, vjp = jax.vjp(fn, qs, ks, vs, _chunk_blocks(sm_ref), _head_cols(gate_ref), nw_ref[...], al_ref[...],
                         db_ref[...], [hist_ref[0, j] for j in range(hb)])
        dqs, dks, dvs, dsm, dgates, dnw, dal, ddb, dstates = vjp(
            (_head_cols(dy_ref), [dstate_ref[h0 + j] for j in range(hb)]))
        for k in range(cb):
            rk = _chunk_rows(k)
            for j in range(hb):
                qc, kc, vc = _gdn_cols(j)
                dqkv_ref[rk, qc] = dqs[k][j].astype(dqkv_ref.dtype)
                dqkv_ref[rk, kc] = dks[k][j].astype(dqkv_ref.dtype)
                dqkv_ref[rk, vc] = dvs[k][j].astype(dqkv_ref.dtype)
                dgate_ref[rk, j * GDN_DV:(j + 1) * GDN_DV] = dgates[k][j].astype(dgate_ref.dtype)
        for j in range(hb):
            dstate_ref[h0 + j] = dstates[j]
        _accumulate(dsm_ref, h == 0, jnp.concatenate(dsm, axis=0))
        first = jnp.logical_and(c == 0, h == 0)
        _accumulate(dnw_ref, first, dnw)
        _accumulate(dal_ref, first, dal)
        _accumulate(ddb_ref, first, ddb)

    return pl.pallas_call(
        body, name="gdn_bwd", grid=(ns, GDN_HEADS // hb),
        in_specs=[ANY, ANY, pl.BlockSpec((rows, GDN_HC * hb), lambda c, h: (rev(c), h)),
                  pl.BlockSpec((rows, GDN_DV * hb), lambda c, h: (rev(c), gate_blk + h)),
                  pl.BlockSpec((rows, LANES), lambda c, h: (rev(c), 0)),
                  _full((1, GDN_DV)), _full((1, LANES)), _full((1, LANES)),
                  pl.BlockSpec((1, hb, GDN_DK, GDN_DV), lambda c, h: (rev(c), h, 0, 0)),
                  pl.BlockSpec((1, cb * hb // 2, CHUNK, LANES), lambda c, h: (rev(c), h, 0, 0)),
                  pl.BlockSpec((rows, GDN_DV * hb), lambda c, h: (rev(c), h))],
        out_specs=[pl.BlockSpec((rows, GDN_DV * hb), lambda c, h: (rev(c), gate_blk + h)),
                   pl.BlockSpec((rows, GDN_HC * hb), lambda c, h: (rev(c), h)),
                   pl.BlockSpec((rows, LANES), lambda c, h: (rev(c), 0)),
                   _full((1, GDN_DV)), _full((1, LANES)), _full((1, LANES))],
        out_shape=[jax.ShapeDtypeStruct(dproj_main.shape, dproj_main.dtype),
                   jax.ShapeDtypeStruct(dconv.shape, dconv.dtype),
                   jax.ShapeDtypeStruct((t, LANES), F32), jax.ShapeDtypeStruct((1, GDN_DV), F32),
                   jax.ShapeDtypeStruct((1, LANES), F32), jax.ShapeDtypeStruct((1, LANES), F32)],
        scratch_shapes=[pltpu.VMEM((GDN_HEADS, GDN_DK, GDN_DV), F32)],
        input_output_aliases={0: 0, 1: 1},
        compiler_params=_params(("arbitrary", "arbitrary")),
    )(dproj_main, dconv, conv_gdn, proj_main, proj_small, normw, alog, dtb, hist, t_inv, dy)


def out_proj_loss(x, y_ssd, y_gdn, w_out, final_w, target):
    t = x.shape[0]
    tm = min(512, t)

    def body(x_ref, ys_ref, yg_ref, wo_ref, fw_ref, tg_ref, loss_ref, dhid_ref, dys_ref, dyg_ref, dwo_ref, dfw_ref):
        i = pl.program_id(0)
        ys, yg = ys_ref[...], yg_ref[...]
        wo_s, wo_g = wo_ref[:SSD_WIDTH, :], wo_ref[SSD_WIDTH:, :]
        hid = x_ref[...] + _raw_dot(ys, wo_s, 1, 0) + _raw_dot(yg, wo_g, 1, 0)
        out, vjp = jax.vjp(rmsnorm, hid, fw_ref[...])
        err = out - tg_ref[...]
        loss = 0.5 * jnp.sum(jnp.mean(err * err, axis=-1, keepdims=True), axis=0, keepdims=True)
        dhid, dfw = vjp(err * (1.0 / D_MODEL))
        dhid_ref[...] = dhid
        dys_ref[...] = _raw_dot(dhid, wo_s, 1, 1)
        dyg_ref[...] = _raw_dot(dhid, wo_g, 1, 1)
        first = i == 0
        _accumulate(loss_ref, first, jnp.broadcast_to(loss, loss_ref.shape))
        _accumulate(dfw_ref, first, dfw)

        @pl.when(first)
        def _():
            dwo_ref[:SSD_WIDTH, :] = _raw_dot(ys, dhid, 0, 0)
            dwo_ref[SSD_WIDTH:, :] = _raw_dot(yg, dhid, 0, 0)

        @pl.when(i > 0)
        def _():
            dwo_ref[:SSD_WIDTH, :] += _raw_dot(ys, dhid, 0, 0)
            dwo_ref[SSD_WIDTH:, :] += _raw_dot(yg, dhid, 0, 0)

    row = lambda w: pl.BlockSpec((tm, w), lambda i: (i, 0))
    return pl.pallas_call(
        body, name="out_proj_loss", grid=(t // tm,),
        in_specs=[row(D_MODEL), row(SSD_WIDTH), row(GDN_W), _full((SSD_WIDTH + GDN_W, D_MODEL)), _full((1, D_MODEL)),
                  row(D_MODEL)],
        out_specs=[_full((8, LANES)), row(D_MODEL), row(SSD_WIDTH), row(GDN_W), _full((SSD_WIDTH + GDN_W, D_MODEL)),
                   _full((1, D_MODEL))],
        out_shape=[jax.ShapeDtypeStruct((8, LANES), F32), jax.ShapeDtypeStruct((t, D_MODEL), F32),
                   jax.ShapeDtypeStruct((t, SSD_WIDTH), F32), jax.ShapeDtypeStruct((t, GDN_W), F32),
                   jax.ShapeDtypeStruct((SSD_WIDTH + GDN_W, D_MODEL), F32), jax.ShapeDtypeStruct((1, D_MODEL), F32)],
        compiler_params=_params(("arbitrary",)),
    )(x, y_ssd, y_gdn, w_out, final_w, target)


def in_proj_bwd_x(x, normw, w_main, w_small, dproj_main, dproj_conv, dsmall_a, dsmall_b, dhid, slabbed):
    t = x.shape[0]
    tm = min(256, t)
    ni = t // tm
    ns = len(slabbed)

    def body(x_ref, nw_ref, wm_ref, ws_ref, dp_ref, dc_ref, da_ref, db_ref, dh_ref, *rest):
        slab_refs, (gx_ref, dnw_ref), land_refs = rest[:ns], rest[ns:ns + 2], rest[ns + 2:2 * ns + 2]
        sems = rest[2 * ns + 2:]
        i = pl.program_id(0)
        start, finish = _slab_exchange(slab_refs, land_refs, ns, *sems)

        @pl.when(i == 0)
        def _():
            start()

        du = (_raw_dot(dp_ref[...], wm_ref[:, :COL_CONV], 1, 1) + _raw_dot(dc_ref[...], wm_ref[:, COL_CONV:], 1, 1)
              + _raw_dot(da_ref[...] + db_ref[...], ws_ref[...], 1, 1))
        _, vjp = jax.vjp(rmsnorm, x_ref[...], nw_ref[...])
        dx, dnw = vjp(du)
        gx_ref[...] = dx + dh_ref[...]
        _accumulate(dnw_ref, i == 0, dnw)

        @pl.when(i == ni - 1)
        def _():
            finish()

    row = lambda w: pl.BlockSpec((tm, w), lambda i: (i, 0))
    out = pl.pallas_call(
        body, name="in_proj_bwd_x", grid=(ni,),
        in_specs=[row(D_MODEL), _full((1, D_MODEL)), _full((D_MODEL, MAIN)), _full((D_MODEL, LANES)), row(COL_CONV),
                  row(CONV_W), row(LANES), row(LANES), row(D_MODEL)] + [HBM] * ns,
        out_specs=[row(D_MODEL), _full((1, D_MODEL))] + [HBM] * ns,
        out_shape=[jax.ShapeDtypeStruct((t, D_MODEL), F32), jax.ShapeDtypeStruct((1, D_MODEL), F32)]
        + _slab_exchange_shapes(slabbed, []),
        scratch_shapes=_slab_exchange_sems(ns),
        compiler_params=_params(("arbitrary",)),
    )(x, normw, w_main, w_small, dproj_main, dproj_conv, dsmall_a, dsmall_b, dhid, *slabbed)
    return out[0], out[1], out[2:]


def in_proj_bwd_w(u, dproj_main, dsmall_a, dsmall_b):
    t = u.shape[0]
    tm = min(1024, t)
    n, half = t // tm, COL_CONV // 2
    nb = min(3, n)

    def body(u_hbm, dp_hbm, da_hbm, db_hbm, dwm_hbm, dws_ref, ubuf, dpbuf, dabuf, dbbuf, acc, in_sems, out_sems):
        pairs = ((u_hbm, ubuf), (dp_hbm, dpbuf), (da_hbm, dabuf), (db_hbm, dbbuf))

        def copies(s):
            slot = s % nb
            return [pltpu.make_async_copy(src.at[pl.ds(s * tm, tm)], buf.at[slot], in_sems.at[k, slot])
                    for k, (src, buf) in enumerate(pairs)]

        def write_back(h):
            cols = pl.ds(h * half, half)
            return pltpu.make_async_copy(acc.at[:, cols], dwm_hbm.at[:, cols], out_sems.at[h])

        for s in range(nb):
            for c in copies(s):
                c.start()
        for s in range(n):
            slot = s % nb
            for c in copies(s):
                c.wait()
            uu = ubuf[slot]
            for h in range(2):
                cols = pl.ds(h * half, half)
                val = _raw_dot(uu, dpbuf[slot, :, cols], 0, 0)
                if s == 0:
                    acc[:, cols] = val
                else:
                    acc[:, cols] += val
                if s == n - 1:
                    write_back(h).start()
            small = _raw_dot(uu, dabuf[slot] + dbbuf[slot], 0, 0)
            if s == 0:
                dws_ref[...] = small
            else:
                dws_ref[...] += small
            if s + nb < n:
                for c in copies(s + nb):
                    c.start()
        for h in range(2):
            write_back(h).wait()

    return pl.pallas_call(
        body, name="in_proj_bwd_w",
        in_specs=[ANY, ANY, ANY, ANY],
        out_specs=[ANY, pl.BlockSpec(memory_space=pltpu.VMEM)],
        out_shape=[jax.ShapeDtypeStruct((D_MODEL, COL_CONV), F32), jax.ShapeDtypeStruct((D_MODEL, LANES), F32)],
        scratch_shapes=[pltpu.VMEM((nb, tm, D_MODEL), u.dtype), pltpu.VMEM((nb, tm, COL_CONV), dproj_main.dtype),
                        pltpu.VMEM((nb, tm, LANES), dsmall_a.dtype), pltpu.VMEM((nb, tm, LANES), dsmall_b.dtype),
                        pltpu.VMEM((D_MODEL, COL_CONV), F32),
                        pltpu.SemaphoreType.DMA((4, nb)), pltpu.SemaphoreType.DMA((2,))],
        compiler_params=_params(),
    )(u, dproj_main, dsmall_a, dsmall_b)


def sum_slabs(a, name):
    n, rows, cols = a.shape
    tr = 64 if rows % 64 == 0 else rows

    def body(a_ref, o_ref):
        acc = a_ref[0].astype(F32)
        for d in range(1, n):
            acc = acc + a_ref[d].astype(F32)
        o_ref[...] = acc

    return pl.pallas_call(
        body, name=name, grid=(rows // tr,),
        in_specs=[pl.BlockSpec((n, tr, cols), lambda i: (0, i, 0))],
        out_specs=pl.BlockSpec((tr, cols), lambda i: (i, 0)),
        out_shape=jax.ShapeDtypeStruct((rows, cols), F32),
        compiler_params=_params(("arbitrary",)),
    )(a)


def adamw(w, g, m, v, name):
    _, rows, cols = w.shape
    tr = 128 if rows % 128 == 0 else rows

    def body(w_ref, g_ref, m_ref, v_ref, d_ref, nm_ref, nv_ref):
        gg = g_ref[...]
        nm = ADAM_B1 * m_ref[...] + (1.0 - ADAM_B1) * gg
        nv = ADAM_B2 * v_ref[...] + (1.0 - ADAM_B2) * (gg * gg)
        m_hat = nm / (1.0 - ADAM_B1 ** ADAM_STEP)
        v_hat = nv / (1.0 - ADAM_B2 ** ADAM_STEP)
        d_ref[...] = -ADAM_LR * (m_hat / (jnp.sqrt(v_hat) + ADAM_EPS) + ADAM_WD * w_ref[...])
        nm_ref[...] = nm
        nv_ref[...] = nv

    spec = pl.BlockSpec((1, tr, cols), lambda i: (0, i, 0))
    shp = jax.ShapeDtypeStruct((1, rows, cols), F32)
    return pl.pallas_call(
        body, name=name, grid=(rows // tr,), in_specs=[spec] * 4, out_specs=[spec] * 3, out_shape=[shp] * 3,
        compiler_params=_params(("arbitrary",)),
    )(w, g.reshape(w.shape), m, v)


def _my_place():
    return lax.axis_index("x"), lax.axis_index("y"), lax.axis_index("c")


def gather_weights(big, small):
    nb, n = len(big), len(big) + len(small)
    parts = 4

    def body(*refs):
        srcs, outs = refs[:n], refs[n:2 * n]
        land_a, land_b = refs[2 * n:2 * n + nb], refs[2 * n + nb:2 * n + 2 * nb]
        send_sems, recv_sems, fwd_send, fwd_recv, local_sems = refs[2 * n + 2 * nb:]
        x, y, c = _my_place()
        me = 2 * x + y
        chips = [(1 - x, y), (x, 1 - y), (1 - x, 1 - y)]
        half = [a.shape[0] // 2 for a in big]

        def ici(j, i):
            px, py = chips[j]
            if i < nb:
                src, dst = srcs[i].at[pl.ds(c * half[i], half[i])], land_a[i].at[j]
            else:
                src, dst = srcs[i], outs[i].at[me]
            return pltpu.make_async_remote_copy(src_ref=src, dst_ref=dst, send_sem=send_sems.at[j * n + i],
                                                recv_sem=recv_sems.at[j * n + i], device_id=(px, py, c),
                                                device_id_type=MESH)

        def ici_arrival(j, i):
            px, py = chips[j]
            dst = land_a[i].at[j] if i < nb else outs[i].at[2 * px + py]
            return pltpu.make_async_remote_copy(src_ref=dst, dst_ref=dst, send_sem=send_sems.at[j * n + i],
                                                recv_sem=recv_sems.at[j * n + i], device_id=(px, py, c),
                                                device_id_type=MESH)

        def forward(j, i, p):
            rows = half[i] // parts
            k = (j * nb + i) * parts + p
            return pltpu.make_async_remote_copy(
                src_ref=land_a[i].at[j, pl.ds(p * rows, rows)], dst_ref=land_b[i].at[j, pl.ds(p * rows, rows)],
                send_sem=fwd_send.at[k], recv_sem=fwd_recv.at[k], device_id=(x, y, 1 - c), device_id_type=MESH)

        def store(j, i, from_sibling):
            px, py = chips[j]
            buf, h = (land_b, 1 - c) if from_sibling else (land_a, c)
            k = n + (j * nb + i) * 2 + (1 if from_sibling else 0)
            return pltpu.make_async_copy(buf[i].at[j], outs[i].at[2 * px + py, pl.ds(h * half[i], half[i])],
                                         local_sems.at[k])

        own = [pltpu.make_async_copy(srcs[i], outs[i].at[me], local_sems.at[i]) for i in range(n)]
        sends = [ici(j, i) for j in range(3) for i in range(n)]
        for cp in own + sends:
            cp.start()
        pending = []
        for j in range(3):
            for i in range(n):
                ici_arrival(j, i).wait_recv()
                if i < nb:
                    fw = [forward(j, i, p) for p in range(parts)]
                    st = store(j, i, False)
                    for cp in fw + [st]:
                        cp.start()
                    pending += [cp.wait_send for cp in fw] + [st.wait]
        for j in range(3):
            for i in range(nb):
                for p in range(parts):
                    forward(j, i, p).wait_recv()
                st = store(j, i, True)
                st.start()
                pending.append(st.wait)
        for cp in sends:
            cp.wait_send()
        for wait in pending:
            wait()
        for cp in own:
            cp.wait()

    shards = list(big) + list(small)
    lands = [pltpu.VMEM((3, a.shape[0] // 2) + a.shape[1:], a.dtype) for a in big]
    return pl.pallas_call(
        body, name="gather_weights",
        in_specs=[HBM] * n, out_specs=[HBM] * n,
        out_shape=[jax.ShapeDtypeStruct((N_CHIP,) + s.shape, s.dtype) for s in shards],
        scratch_shapes=lands + lands + [
            pltpu.SemaphoreType.DMA((3 * n,)), pltpu.SemaphoreType.DMA((3 * n,)),
            pltpu.SemaphoreType.DMA((3 * nb * parts,)), pltpu.SemaphoreType.DMA((3 * nb * parts,)),
            pltpu.SemaphoreType.DMA((n + 6 * nb,))],
        compiler_params=pltpu.CompilerParams(vmem_limit_bytes=VMEM_LIMIT),
    )(*shards)


def _peer(x, y, c, mask):
    mx, my, mc = (mask >> 2) & 1, (mask >> 1) & 1, mask & 1
    return (x ^ mx if mx else x, y ^ my if my else y, c ^ mc if mc else c)


def _slab_exchange_shapes(slabbed, replicated):
    return ([jax.ShapeDtypeStruct(a.shape, a.dtype) for a in slabbed]
            + [jax.ShapeDtypeStruct((N_DEV,) + a.shape, a.dtype) for a in replicated])


def _slab_exchange_sems(n):
    return [pltpu.SemaphoreType.DMA((7 * n,)), pltpu.SemaphoreType.DMA((7 * n,)), pltpu.SemaphoreType.DMA((n,))]


def _slab_exchange(srcs, outs, ns, send_sems, recv_sems, local_sems):
    n = len(srcs)
    x, y, c = _my_place()
    me = 4 * x + 2 * y + c

    def piece(i, dev):
        return srcs[i].at[dev] if i < ns else srcs[i]

    def copies(arriving):
        out = []
        for mask in range(1, N_DEV):
            px, py, pc = _peer(x, y, c, mask)
            dev = 4 * px + 2 * py + pc
            for i in range(n):
                k = (mask - 1) * n + i
                out.append(pltpu.make_async_remote_copy(
                    src_ref=piece(i, dev), dst_ref=outs[i].at[dev if arriving else me], send_sem=send_sems.at[k],
                    recv_sem=recv_sems.at[k], device_id=(px, py, pc), device_id_type=MESH))
        return out

    def local():
        return [pltpu.make_async_copy(piece(i, me), outs[i].at[me], local_sems.at[i]) for i in range(n)]

    def start():
        for cp in local() + copies(False):
            cp.start()

    def finish():
        for cp in copies(True):
            cp.wait_recv()
        for cp in copies(False):
            cp.wait_send()
        for cp in local():
            cp.wait()

    return start, finish


def exchange_halves(landed, replicated):
    n, nr = len(landed), len(replicated)
    streams = 8
    halves = [jax.ShapeDtypeStruct(a.shape[1:], F32) for a in landed]
    sum_rows = 64

    def body(*refs):
        srcs, rep_srcs, outs, rep_outs = refs[:n], refs[n:n + nr], refs[n + nr:2 * n + nr], refs[2 * n + nr:2 * (n + nr)]
        refs = refs[2 * (n + nr):]
        slabs, mine, theirs = refs[:n], refs[n:2 * n], refs[2 * n:3 * n]
        send_sems, recv_sems, in_sems, out_sems = refs[3 * n:3 * n + 4]
        rep_start, rep_finish = _slab_exchange(rep_srcs, rep_outs, 0, *refs[3 * n + 4:])
        rep_start()
        x, y, c = _my_place()
        loads = [pltpu.make_async_copy(srcs[i], slabs[i], in_sems.at[i]) for i in range(n)]
        for cp in loads:
            cp.start()
        for i in range(n):
            loads[i].wait()
            for r in range(0, halves[i].shape[0], sum_rows):
                rows = pl.ds(r, sum_rows)
                acc = slabs[i][0, rows, :].astype(F32)
                for d in range(1, N_DEV):
                    acc = acc + slabs[i][d, rows, :].astype(F32)
                mine[i][rows, :] = acc

        def chunk_copy(i, s):
            rows = halves[i].shape[0] // streams
            k = i * streams + s
            return pltpu.make_async_remote_copy(
                src_ref=mine[i].at[pl.ds(s * rows, rows)], dst_ref=theirs[i].at[pl.ds(s * rows, rows)],
                send_sem=send_sems.at[k], recv_sem=recv_sems.at[k], device_id=(x, y, 1 - c), device_id_type=MESH)

        sends = [chunk_copy(i, s) for i in range(n) for s in range(streams)]
        for cp in sends:
            cp.start()
        own = [pltpu.make_async_copy(mine[i], outs[i].at[c], out_sems.at[i]) for i in range(n)]
        for cp in own:
            cp.start()
        for cp in sends:
            cp.wait_recv()
        got = [pltpu.make_async_copy(theirs[i], outs[i].at[1 - c], out_sems.at[n + i]) for i in range(n)]
        for cp in got:
            cp.start()
        for cp in sends:
            cp.wait_send()
        for cp in own + got:
            cp.wait()
        rep_finish()

    vmem = [pltpu.VMEM(a.shape, a.dtype) for a in halves]
    out = pl.pallas_call(
        body, name="exchange_halves",
        in_specs=[HBM] * (n + nr), out_specs=[HBM] * (n + nr),
        out_shape=[jax.ShapeDtypeStruct((2,) + a.shape, a.dtype) for a in halves]
        + _slab_exchange_shapes([], replicated),
        scratch_shapes=[pltpu.VMEM(a.shape, a.dtype) for a in landed] + vmem + vmem
        + [pltpu.SemaphoreType.DMA((n * streams,)), pltpu.SemaphoreType.DMA((n * streams,)),
           pltpu.SemaphoreType.DMA((n,)), pltpu.SemaphoreType.DMA((2 * n,))] + _slab_exchange_sems(nr),
        compiler_params=pltpu.CompilerParams(vmem_limit_bytes=VMEM_LIMIT),
    )(*landed, *replicated)
    return out[:n], out[n:]


def _pack_cols(pieces):
    offs, pos = [], 0
    for a in pieces:
        offs.append(pos)
        pos += a.shape[1]
    rows8 = [jnp.pad(a.astype(F32), ((0, 8 - a.shape[0]), (0, 0))) for a in pieces]
    return jnp.concatenate(rows8, axis=1), offs


def adamw_many(ws, gs, ms, vs):
    n = len(ws)

    def body(*refs):
        w_r, g_r, m_r, v_r = refs[:n], refs[n:2 * n], refs[2 * n:3 * n], refs[3 * n:4 * n]
        d_o, m_o, v_o = refs[4 * n:5 * n], refs[5 * n:6 * n], refs[6 * n:7 * n]
        for i in range(n):
            gg = g_r[i][...]
            nm = ADAM_B1 * m_r[i][...] + (1.0 - ADAM_B1) * gg
            nv = ADAM_B2 * v_r[i][...] + (1.0 - ADAM_B2) * (gg * gg)
            m_hat = nm / (1.0 - ADAM_B1 ** ADAM_STEP)
            v_hat = nv / (1.0 - ADAM_B2 ** ADAM_STEP)
            d_o[i][...] = -ADAM_LR * (m_hat / (jnp.sqrt(v_hat) + ADAM_EPS) + ADAM_WD * w_r[i][...])
            m_o[i][...] = nm
            v_o[i][...] = nv

    shapes = [jax.ShapeDtypeStruct(w.shape, F32) for w in ws]
    out = pl.pallas_call(body, name="adamw_small", out_shape=shapes * 3,
                         compiler_params=pltpu.CompilerParams(vmem_limit_bytes=VMEM_LIMIT))(*ws, *gs, *ms, *vs)
    return out[:n], out[n:2 * n], out[2 * n:]


def _lanes(vec, start):
    n = vec.shape[-1]
    return jnp.pad(vec.reshape(1, n).astype(F32), ((0, 0), (start, LANES - start - n)))


def kernel(x, norm_w, w_in, ssd_conv_w, ssd_conv_b, ssd_dt_bias, ssd_a_log, ssd_d, ssd_norm_w, gdn_conv_w, gdn_dt_bias, gdn_a_log, gdn_norm_w, w_out, final_norm_w, loss_target, m_norm_w, m_w_in, m_ssd_conv_w, m_ssd_conv_b, m_ssd_dt_bias, m_ssd_a_log, m_ssd_d, m_ssd_norm_w, m_gdn_conv_w, m_gdn_dt_bias, m_gdn_a_log, m_gdn_norm_w, m_w_out, m_final_norm_w, v_norm_w, v_w_in, v_ssd_conv_w, v_ssd_conv_b, v_ssd_dt_bias, v_ssd_a_log, v_ssd_d, v_ssd_norm_w, v_gdn_conv_w, v_gdn_dt_bias, v_gdn_a_log, v_gdn_norm_w, v_w_out, v_final_norm_w):
    xs = x[0]
    target = loss_target[0]
    chip = 2 * lax.axis_index("x") + lax.axis_index("y")
    w_in_shard, w_out_shard = w_in[0], w_out[0]
    in_cols = w_in_shard.shape[1]
    out_rows = w_out_shard.shape[0]

    g_in, g_out, g_cs, g_cg = gather_weights(
        [w_in_shard.astype(MXU_DTYPE), w_out_shard.astype(MXU_DTYPE)], [ssd_conv_w[0], gdn_conv_w[0]])
    w_in_full = jnp.concatenate([g_in[k] for k in range(N_CHIP)], axis=1)
    w_out_full = g_out.reshape(N_CHIP * out_rows, D_MODEL)
    cw_ssd = jnp.concatenate([g_cs[k] for k in range(N_CHIP)], axis=1)
    cw_gdn = jnp.concatenate([g_cg[k] for k in range(N_CHIP)], axis=1)
    cb_ssd, cb_gdn = ssd_conv_b, jnp.zeros((1, GDN_CONV), F32)
    o_xbc, o_dt, o_gate, o_qkv, o_ab = 1024, 2560, 2576, 3600, 6672
    w_main = jnp.concatenate([w_in_full[:, :o_xbc], w_in_full[:, o_gate:o_qkv], w_in_full[:, o_qkv:o_ab],
                              w_in_full[:, o_xbc:o_dt]], axis=1)
    w_small = jnp.concatenate([w_in_full[:, o_dt:o_gate], w_in_full[:, o_ab:],
                               jnp.zeros((D_MODEL, LANES - 32), MXU_DTYPE)], axis=1)
    alog = _lanes(ssd_a_log, 0) + _lanes(gdn_a_log, LANE_GA)
    dtb = _lanes(ssd_dt_bias, 0) + _lanes(gdn_dt_bias, LANE_GA)
    dvec = _lanes(ssd_d, 0)
    fw = final_norm_w.reshape(1, D_MODEL)

    cw, cb = jnp.concatenate([cw_gdn, cw_ssd], axis=1), jnp.concatenate([cb_gdn, cb_ssd], axis=1)
    proj_main, proj_small, u = in_proj(xs, norm_w, w_main, w_small)
    conv_out, x_conv, dsilu_conv = in_proj_conv(u, w_main, cw, cb)
    y_ssd, hist_ssd = ssd_fwd(conv_out, proj_main, proj_small, ssd_norm_w, alog, dtb, dvec)
    y_gdn, hist_gdn, tinv_gdn = gdn_fwd(conv_out, proj_main, proj_small, gdn_norm_w, alog, dtb)

    loss_blk, dhid, dy_ssd, dy_gdn, d_w_out, d_fw = out_proj_loss(xs, y_ssd, y_gdn, w_out_full, fw, target)
    dconv, dproj_main, dsmall_ssd, d_ssd_nw, d_alog_s, d_dtb_s, d_dvec = ssd_bwd(
        conv_out, proj_main, proj_small, ssd_norm_w, alog, dtb, dvec, hist_ssd, dy_ssd)
    dproj_main, dconv, dsmall_gdn, d_gdn_nw, d_alog_g, d_dtb_g = gdn_bwd(
        dproj_main, dconv, conv_out, proj_main, proj_small, gdn_norm_w, alog, dtb, hist_gdn, tinv_gdn, dy_gdn)
    slabs_out = d_w_out.reshape(N_DEV, out_rows // 2, D_MODEL).astype(COMM_DTYPE)
    dproj_conv, d_w_conv, dwb, (r_out,) = conv_bwd_w(u, x_conv, dsilu_conv, cw, dconv, [slabs_out])
    dwb_gdn, dwb_ssd = dwb[:, :GDN_CONV], dwb[:, GDN_CONV:]
    d_w_zg, d_w_small = in_proj_bwd_w(u, dproj_main, dsmall_ssd, dsmall_gdn)
    order = [(d_w_zg, 0, COL_GATE), (d_w_conv, COL_SSD - COL_CONV, CONV_W), (d_w_small, 0, 16),
             (d_w_zg, COL_GATE, COL_CONV), (d_w_conv, 0, COL_SSD - COL_CONV), (d_w_small, 16, 32)]
    shards, pos = [[] for _ in range(N_CHIP)], 0
    for src, lo, hi in order:
        while lo < hi:
            k = pos // in_cols
            n = min(hi - lo, (k + 1) * in_cols - pos)
            shards[k].append(src[:, lo:lo + n].astype(COMM_DTYPE))
            lo, pos = lo + n, pos + n
    slabs_in = jnp.stack([jnp.concatenate(p, axis=1) for p in shards]).reshape(N_DEV, D_MODEL // 2, in_cols)
    grad_x, d_norm_w, (r_in,) = in_proj_bwd_x(xs, norm_w, w_main, w_small, dproj_main, dproj_conv, dsmall_ssd,
                                               dsmall_gdn, dhid, [slabs_in])
    d_alog, d_dtb = d_alog_s + d_alog_g, d_dtb_s + d_dtb_g
    packed, (o_nw, o_cs, o_cg, o_snw, o_fw, o_al, o_db, o_dv, o_gnw, o_loss) = _pack_cols([
        d_norm_w, dwb_ssd, dwb_gdn,
        d_ssd_nw.reshape(1, SSD_WIDTH), d_fw, d_alog, d_dtb, d_dvec, d_gdn_nw, loss_blk])

    (full_in, full_out), (r_small,) = exchange_halves([r_in, r_out], [packed])
    tot = sum_slabs(r_small, "sum_small")
    grad_w_in = full_in.reshape(D_MODEL, in_cols)
    grad_w_out = full_out.reshape(out_rows, D_MODEL)
    loss = tot[0, o_loss]
    sc, gc = ssd_conv_w.shape[2], gdn_conv_w.shape[2]
    row = lambda off, n, r=0: tot[r:r + 1, off:off + n]
    gs = [row(o_nw, D_MODEL),
          lax.dynamic_slice(tot, (0, o_cs + chip * sc), (4, sc)),
          row(o_cs, SSD_CONV, 4),
          row(o_db, SSD_HEADS), row(o_al, SSD_HEADS), row(o_dv, SSD_HEADS),
          row(o_snw, SSD_WIDTH),
          lax.dynamic_slice(tot, (0, o_cg + chip * gc), (4, gc)),
          row(o_db + LANE_GA, GDN_HEADS), row(o_al + LANE_GA, GDN_HEADS),
          row(o_gnw, GDN_DV), row(o_fw, D_MODEL)]

    names = ["norm_w", "ssd_conv_w", "ssd_conv_b", "ssd_dt_bias", "ssd_a_log", "ssd_d", "ssd_norm_w", "gdn_conv_w",
             "gdn_dt_bias", "gdn_a_log", "gdn_norm_w", "final_norm_w"]
    ws = [norm_w, ssd_conv_w, ssd_conv_b, ssd_dt_bias, ssd_a_log, ssd_d, ssd_norm_w, gdn_conv_w, gdn_dt_bias,
          gdn_a_log, gdn_norm_w, final_norm_w]
    ms = [m_norm_w, m_ssd_conv_w, m_ssd_conv_b, m_ssd_dt_bias, m_ssd_a_log, m_ssd_d, m_ssd_norm_w, m_gdn_conv_w,
          m_gdn_dt_bias, m_gdn_a_log, m_gdn_norm_w, m_final_norm_w]
    vs = [v_norm_w, v_ssd_conv_w, v_ssd_conv_b, v_ssd_dt_bias, v_ssd_a_log, v_ssd_d, v_ssd_norm_w, v_gdn_conv_w,
          v_gdn_dt_bias, v_gdn_a_log, v_gdn_norm_w, v_final_norm_w]
    shapes = [w.shape for w in ws]
    flat = lambda arrs: [a.reshape(g.shape) for a, g in zip(arrs, gs)]
    d_s, m_s, v_s = adamw_many(flat(ws), gs, flat(ms), flat(vs))
    back = lambda arrs: dict(zip(names, [a.reshape(s) for a, s in zip(arrs, shapes)]))
    delta, new_m, new_v, grads = back(d_s), back(m_s), back(v_s), back(gs)
    d_in, m_in, v_in = adamw(w_in, grad_w_in, m_w_in, v_w_in, "adamw_w_in")
    d_out, m_out, v_out = adamw(w_out, grad_w_out, m_w_out, v_w_out, "adamw_w_out")
    for tbl, a_in, a_out in ((grads, grad_w_in[None], grad_w_out[None]), (delta, d_in, d_out), (new_m, m_in, m_out),
                             (new_v, v_in, v_out)):
        tbl["w_in"] = a_in
        tbl["w_out"] = a_out

    order = ["norm_w", "w_in", "ssd_conv_w", "ssd_conv_b", "ssd_dt_bias", "ssd_a_log", "ssd_d", "ssd_norm_w",
             "gdn_conv_w", "gdn_dt_bias", "gdn_a_log", "gdn_norm_w", "w_out", "final_norm_w"]
    return (loss.reshape(()), grad_x[None], *[grads[k] for k in order], *[delta[k] for k in order],
            *[new_m[k] for k in order], *[new_v[k] for k in order])
```

```python
import functools

import jax
import jax.numpy as jnp
from jax import lax
from jax.experimental import pallas as pl
from jax.experimental.pallas import tpu as pltpu

F32 = jnp.float32
MXU_DTYPE = jnp.bfloat16
COMM_DTYPE = jnp.bfloat16
MESH = pl.DeviceIdType.MESH

D_MODEL = 1024
CHUNK = 64
EPS = 1e-6
SSD_HEADS, SSD_GROUPS, SSD_STATE = 16, 2, 128
SSD_WIDTH, SSD_CONV = 1024, 1536
SSD_GW = SSD_WIDTH // SSD_GROUPS
GDN_HEADS, GDN_DK, GDN_DV = 8, 128, 128
GDN_W, GDN_CONV = 1024, 3072
GDN_HC = 2 * GDN_DK + GDN_DV
IN_DIM = 6688
MAIN = 6656
LANES = 128
COL_Z, COL_GATE, COL_GDN, COL_SSD = 0, 1024, 2048, 5120
COL_CONV = COL_GDN
CONV_W = MAIN - COL_CONV
GDN_HB = 8
GDN_CB = 4
SSD_CB = 4
LANE_GA, LANE_GB = 16, 24
N_DEV, N_CHIP = 8, 4
VMEM_LIMIT = 52 * 1024 * 1024

ADAM_LR, ADAM_B1, ADAM_B2, ADAM_EPS, ADAM_WD, ADAM_STEP = 0.001, 0.9, 0.999, 1e-08, 0.01, 10


def _split(a, n):
    parts, rest = [], a.astype(F32)
    for i in range(n):
        p = rest.astype(MXU_DTYPE)
        parts.append(p)
        if i < n - 1:
            rest = rest - p.astype(F32)
    return parts


def _raw_dot(a, b, ca, cb, mode="bf16"):
    d = lambda u, v: lax.dot_general(u, v, (((ca,), (cb,)), ((), ())), preferred_element_type=F32)
    if mode == "bf16":
        return d(a.astype(MXU_DTYPE), b.astype(MXU_DTYPE))
    if mode == "x3":
        (ah, al), (bh, bl) = _split(a, 2), _split(b, 2)
        return d(ah, bh) + (d(ah, bl) + d(al, bh))
    if mode == "sel_a":
        a0 = a.astype(MXU_DTYPE)
        b1, b2, b3 = _split(b, 3)
        return d(a0, b1) + (d(a0, b2) + d(a0, b3))
    assert mode == "sel_b", mode
    b0 = b.astype(MXU_DTYPE)
    a1, a2, a3 = _split(a, 3)
    return d(a1, b0) + (d(a2, b0) + d(a3, b0))


@functools.partial(jax.custom_vjp, nondiff_argnums=(2,))
def mm_nn(a, b, mode="bf16"):
    return _raw_dot(a, b, 1, 0, mode)


@functools.partial(jax.custom_vjp, nondiff_argnums=(2,))
def mm_nt(a, b, mode="bf16"):
    return _raw_dot(a, b, 1, 1, mode)


@functools.partial(jax.custom_vjp, nondiff_argnums=(2,))
def mm_tn(a, b, mode="bf16"):
    return _raw_dot(a, b, 0, 0, mode)


_SAME = {"bf16": ("bf16", "bf16"), "x3": ("x3", "x3")}
_NN_BWD = dict(_SAME, sel_a=("bf16", "sel_a"), sel_b=("sel_b", "bf16"))
_NT_BWD = dict(_SAME, sel_a=("bf16", "sel_b"), sel_b=("sel_b", "bf16"))
_TN_BWD = dict(_SAME, sel_a=("bf16", "sel_a"), sel_b=("sel_a", "bf16"))
mm_nn.defvjp(lambda a, b, m: (_raw_dot(a, b, 1, 0, m), (a, b)),
             lambda m, r, g: (mm_nt(g, r[1], _NN_BWD[m][0]), mm_tn(r[0], g, _NN_BWD[m][1])))
mm_nt.defvjp(lambda a, b, m: (_raw_dot(a, b, 1, 1, m), (a, b)),
             lambda m, r, g: (mm_nn(g, r[1], _NT_BWD[m][0]), mm_tn(g, r[0], _NT_BWD[m][1])))
mm_tn.defvjp(lambda a, b, m: (_raw_dot(a, b, 0, 0, m), (a, b)),
             lambda m, r, g: (mm_nt(r[1], g, _TN_BWD[m][0]), mm_nn(r[0], g, _TN_BWD[m][1])))


@jax.custom_jvp
def sigmoid(x):
    return 1.0 / (1.0 + jnp.exp(-x))


@sigmoid.defjvp
def _sigmoid_jvp(p, t):
    s = sigmoid(p[0])
    return s, t[0] * s * (1.0 - s)


@jax.custom_jvp
def softplus(x):
    return jnp.maximum(x, 0.0) + jnp.log(1.0 + jnp.exp(-jnp.abs(x)))


@softplus.defjvp
def _softplus_jvp(p, t):
    return softplus(p[0]), t[0] * sigmoid(p[0])


def silu(x):
    return x * sigmoid(x)


def rmsnorm(x, w):
    return x * lax.rsqrt(jnp.mean(x * x, axis=-1, keepdims=True) + EPS) * w


def _iota(shape, dim):
    return lax.broadcasted_iota(jnp.int32, shape, dim)


def _halves():
    lane = _iota((1, LANES), 1) >> 6
    return _ind(lane == 0), _ind(lane == 1)


def _block_diag(pair):
    h0, h1 = _halves()
    return jnp.concatenate([pair * h0, pair * h1], axis=0)


def _tri_inv_impl(mats):
    r, c = _iota((CHUNK, LANES), 0), _iota((CHUNK, LANES), 1) & (CHUNK - 1)
    eye = _ind(r == c)
    blockdiag = _ind((r >> 4) == (c >> 4))
    dot = lambda u, v: _raw_dot(u, _block_diag(v), 1, 0, "x3")
    dot1 = lambda u, v: _raw_dot(u, _block_diag(v), 1, 0)
    each = lambda f, *ls: [f(*xs) for xs in zip(*ls)]
    dg = each(lambda a: a * blockdiag, mats)
    off = each(lambda a, d: a - d, mats, dg)
    m = each(lambda d: -d, dg)
    p = each(lambda x: eye + x, m)
    pw = m
    for _ in range(3):
        pw = each(lambda x: dot1(x, x), pw)
        p = each(lambda x, y: x + dot1(x, y), p, pw)
    e = each(dot, p, off)
    e2 = each(lambda x: dot1(x, x), e)
    q = each(lambda x: eye - x, e)
    q = each(lambda x, y: x + dot1(x, y), q, e2)
    return each(dot, q, p)


def _tri_inv_bwd(ts, gs):
    h0, h1 = _halves()
    x = [mm_nt(g, _block_diag(t)) for g, t in zip(gs, ts)]
    full = [mm_tn(t, y) for t, y in zip(ts, x)]
    return [-(f[:CHUNK] * h0 + f[CHUNK:] * h1) for f in full]


@jax.custom_vjp
def tri_inv(mats):
    return _tri_inv_impl(mats)


def _tri_inv_fwd(mats):
    ts = _tri_inv_impl(mats)
    return ts, ts


tri_inv.defvjp(_tri_inv_fwd, lambda ts, gs: (_tri_inv_bwd(ts, gs),))


@jax.custom_vjp
def tri_inv_saved(mats, ts):
    del mats
    return ts


tri_inv_saved.defvjp(lambda mats, ts: (ts, ts),
                     lambda ts, gs: (_tri_inv_bwd(ts, gs), [jnp.zeros_like(t) for t in ts]))


def _ind(cond):
    return jnp.where(cond, 1.0, 0.0).astype(F32)


def _chunk_masks():
    r, c = _iota((CHUNK, CHUNK), 0), _iota((CHUNK, CHUNK), 1)
    return _ind(r >= c), _ind(r > c), _ind(r == c), _ind(_iota((CHUNK, 1), 0) == CHUNK - 1)


def _log_decay_cumsum(small, alog, dtb, tri):
    sp = softplus(small + dtb)
    la = -jnp.exp(alog) * sp
    return sp, mm_nn(tri, la, "sel_a")


def _col_of(x, lane):
    return jnp.sum(x * _ind(_iota((1, LANES), 1) == lane), axis=1, keepdims=True)


def _pair_masks():
    r, c = _iota((CHUNK, LANES), 0), _iota((CHUNK, LANES), 1)
    c6 = c & (CHUNK - 1)
    return _ind(r >= c6), _ind(r > c6), (_ind(c == r), _ind(c == r + CHUNK))


def _decay_pair(col_a, col_b, tri_w, eye_w):
    h0, h1 = _halves()
    col = col_a * h0 + col_b * h1
    row = jnp.sum(col_a * eye_w[0] + col_b * eye_w[1], axis=0, keepdims=True)
    return jnp.exp((col - row) * tri_w) * tri_w


def gdn_chunk(h0, qs, ks, vs, smalls, gates, normw, alog, dtb, states, saved_t=None):
    tri, _, _, last = _chunk_masks()
    tri_w, strict_w, eye_w = _pair_masks()
    nh = len(qs[0])
    flat = lambda xss: [x for xs in xss for x in xs]
    lacs = [_log_decay_cumsum(sm, alog, dtb, tri)[1] for sm in smalls]
    qs, ks, vs, gates = flat(qs), flat(ks), flat(vs), flat(gates)
    heads, pairs = range(len(qs)), range(len(qs) // 2)
    each = lambda f, *ls: [f(*xs) for xs in zip(*ls)]
    ab = lambda xs, p: (xs[2 * p], xs[2 * p + 1])
    stack = lambda xs: jnp.concatenate(xs, axis=0)
    gc = [_col_of(lacs[i // nh], LANE_GA + h0 + i % nh) for i in heads]
    beta = [sigmoid(_col_of(smalls[i // nh], LANE_GB + h0 + i % nh)) for i in heads]
    decay = [_decay_pair(*ab(gc, p), tri_w, eye_w) for p in pairs]
    gl = each(lambda x: jnp.sum(x * last, axis=0, keepdims=True), gc)
    q = each(lambda x: x * lax.rsqrt(jnp.sum(x * x, axis=-1, keepdims=True) + EPS) * (GDN_DK ** -0.5), qs)
    k = each(lambda x: x * lax.rsqrt(jnp.sum(x * x, axis=-1, keepdims=True) + EPS), ks)
    kb = each(lambda x, b: x * b, k, beta)
    eg = each(jnp.exp, gc)
    zero = jnp.zeros((CHUNK, GDN_DK), F32)
    k_bd = [stack([join_lanes([k[2 * p], zero]), join_lanes([zero, k[2 * p + 1]])]) for p in pairs]
    a = [mm_nt(join_lanes(list(ab(kb, p))), k_bd[p]) * (decay[p] * strict_w) for p in pairs]
    t = tri_inv(a) if saved_t is None else tri_inv_saved(a, saved_t)
    attn = [mm_nt(join_lanes(list(ab(q, p))), k_bd[p]) * decay[p] for p in pairs]
    rhs = [stack([join_lanes([vs[h] * beta[h], kb[h] * eg[h]]) for h in (2 * p, 2 * p + 1)]) for p in pairs]
    uw = [mm_nn(_block_diag(t[p]), rhs[p]) for p in pairs]
    uw = [x for p in pairs for x in split_rows(uw[p])]
    u, w = zip(*[split_lanes(x) for x in uw])
    ys = []
    for c in range(len(smalls)):
        hs = range(c * nh, (c + 1) * nh)
        v_new = [u[i] - mm_nn(w[i], states[i % nh]) for i in hs]
        av = [mm_nn(_block_diag(attn[c * nh // 2 + p]), stack(list(ab(v_new, p)))) for p in range(nh // 2)]
        av = [x for y in av for x in split_rows(y)]
        o = [mm_nn(q[i] * eg[i], states[i % nh]) + av[i % nh] for i in hs]
        states = [states[i % nh] * jnp.exp(gl[i]) + mm_tn(k[i] * jnp.exp(gl[i] - gc[i]), v_new[i % nh]) for i in hs]
        ys.append([rmsnorm(o[i % nh], normw) * silu(gates[i]) for i in hs])
    return ys, states, t


@jax.custom_vjp
def split_rows(x):
    n = x.shape[0] // 2
    return [x[:n], x[n:]]


split_rows.defvjp(lambda x: (split_rows(x), None), lambda _, gs: (jnp.concatenate(gs, axis=0),))


@jax.custom_vjp
def split_lanes(x):
    return [x[:, i * LANES:(i + 1) * LANES] for i in range(x.shape[1] // LANES)]


@jax.custom_vjp
def join_lanes(xs):
    return jnp.concatenate(xs, axis=1)


split_lanes.defvjp(lambda x: (split_lanes(x), None), lambda _, gs: (join_lanes(gs),))
join_lanes.defvjp(lambda xs: (join_lanes(xs), None), lambda _, g: (split_lanes(g),))


def ssd_chunk(xs, bm, cm, z, smalls, normw, alog, dtb, dvec, state):
    tri, _, _, last = _chunk_masks()
    tri_w, _, eye_w = _pair_masks()
    h0, h1 = _halves()
    hpg = SSD_HEADS // SSD_GROUPS
    ng = len(normw)
    flat = lambda xss: [x for xs_ in xss for x in xs_]
    xs, bm, cm, z = flat(xs), flat(bm), flat(cm), flat(z)
    units, pairs = range(len(xs)), range(hpg // 2)
    each = lambda f, *ls: [f(*a) for a in zip(*ls)]
    sp_lac = [_log_decay_cumsum(sm, alog, dtb, tri) for sm in smalls]
    lac_last = [jnp.sum(lac * last, axis=0, keepdims=True) for _, lac in sp_lac]
    sel = [_ind(_iota((LANES, SSD_GW), 0) == g * hpg + (_iota((LANES, SSD_GW), 1) >> 6)) for g in range(ng)]
    expand = lambda vs, mode: [mm_nn(vs[i // ng], sel[i % ng], mode) for i in units]
    dt_e = expand([sp for sp, _ in sp_lac], "bf16")
    elac_e = expand([jnp.exp(lac) for _, lac in sp_lac], "bf16")
    toend_e = expand([jnp.exp(ll - lac) for (_, lac), ll in zip(sp_lac, lac_last)], "bf16")
    row8 = _iota((8, 1), 0)
    two_e = expand([_ind(row8 == 0) * dvec + _ind(row8 == 1) * jnp.exp(ll) for ll in lac_last], "sel_b")
    d_e = each(lambda v: jnp.sum(v * _ind(row8 == 0), axis=0, keepdims=True), two_e)
    chunk_e = each(lambda v: jnp.sum(v * _ind(row8 == 1), axis=0, keepdims=True), two_e)
    xdt = each(lambda a, b: a * b, xs, dt_e)
    cb_w = each(lambda c_, b_: mm_nt(c_, jnp.concatenate([b_, b_], axis=0)), cm, bm)
    x_pairs = each(split_lanes, xdt)
    col = lambda i, j: _col_of(sp_lac[i // ng][1], (i % ng) * hpg + j)
    lms = [[_decay_pair(col(i, 2 * p), col(i, 2 * p + 1), tri_w, eye_w) for p in pairs] for i in units]
    stacked = [[jnp.concatenate([x_pairs[i][p] * h0, x_pairs[i][p] * h1], axis=0) for p in pairs] for i in units]
    terms = [[mm_nn(cb_w[i] * lms[i][p], stacked[i][p]) for p in pairs] for i in units]
    y_in = [join_lanes(terms[i]) + xs[i] * d_e[i] for i in units]
    state_in = each(lambda b_, xd, te: mm_tn(b_, xd * te), bm, xdt, toend_e)
    outs = []
    for c in range(len(smalls)):
        us = range(c * ng, (c + 1) * ng)
        y = [mm_nn(cm[i], state[i % ng]) * elac_e[i] + y_in[i] for i in us]
        state = [state[i % ng] * chunk_e[i] + state_in[i] for i in us]
        outs.append([rmsnorm(y[i % ng] * silu(z[i]), normw[i % ng]) for i in us])
    return outs, state


def _params(sem=None):
    return pltpu.CompilerParams(dimension_semantics=sem, vmem_limit_bytes=VMEM_LIMIT)


def _full(shape):
    n = len(shape)
    return pl.BlockSpec(shape, lambda *_: (0,) * n)


ANY = pl.BlockSpec(memory_space=pl.ANY)
HBM = pl.BlockSpec(memory_space=pltpu.HBM)


def in_proj(x, normw, w_main, w_small):
    t = x.shape[0]
    tm, tn = min(2048, t), 512

    def body(x_ref, nw_ref, wm_ref, ws_ref, pm_ref, ps_ref, u_ref):
        @pl.when(pl.program_id(1) == 0)
        def _():
            u = rmsnorm(x_ref[...], nw_ref[...]).astype(MXU_DTYPE)
            u_ref[...] = u
            ps_ref[...] = _raw_dot(u, ws_ref[...], 1, 0)
        pm_ref[...] = _raw_dot(u_ref[...], wm_ref[...], 1, 0)

    return pl.pallas_call(
        body, name="in_proj", grid=(t // tm, COL_CONV // tn),
        in_specs=[pl.BlockSpec((tm, D_MODEL), lambda i, j: (i, 0)), _full((1, D_MODEL)),
                  pl.BlockSpec((D_MODEL, tn), lambda i, j: (0, j)), _full((D_MODEL, LANES))],
        out_specs=[pl.BlockSpec((tm, tn), lambda i, j: (i, j)), pl.BlockSpec((tm, LANES), lambda i, j: (i, 0)),
                   pl.BlockSpec((tm, D_MODEL), lambda i, j: (i, 0))],
        out_shape=[jax.ShapeDtypeStruct((t, COL_CONV), F32), jax.ShapeDtypeStruct((t, LANES), F32),
                   jax.ShapeDtypeStruct((t, D_MODEL), MXU_DTYPE)],
        compiler_params=_params(("arbitrary", "arbitrary")),
    )(x, normw, w_main, w_small)


CONV_TC = 512
HALO = 8


def _shift_down(cur, prev, s):
    rolled = pltpu.roll(cur, s, 0)
    top = jnp.where(_iota((HALO, cur.shape[1]), 0) < s, pltpu.roll(prev, s, 0), rolled[:HALO])
    if cur.shape[0] == HALO:
        return top
    return jnp.concatenate([top, rolled[HALO:]], axis=0)


def _shift_up(cur, nxt, s):
    n = cur.shape[0]
    rolled = pltpu.roll(cur, n - s, 0)
    bot = jnp.where(_iota((HALO, cur.shape[1]), 0) >= HALO - s, pltpu.roll(nxt, HALO - s, 0), rolled[n - HALO:])
    return jnp.concatenate([rolled[:n - HALO], bot], axis=0)


def _conv_pre(cur, prev, w_ref, b, cols=slice(None)):
    acc = cur * w_ref[3:4, cols] + b
    shifted = [cur]
    for s in (1, 2, 3):
        sh = _shift_down(cur, prev, s)
        shifted.append(sh)
        acc = acc + sh * w_ref[3 - s:4 - s, cols]
    return acc, shifted


def in_proj_conv(u, w_main, w, b):
    t = u.shape[0]
    tm, tn = min(2048, t), CONV_TC
    rc = min(256, tm)
    c0, nj = COL_CONV // tn, CONV_W // tn

    def body(u_ref, wm_ref, w_ref, b_ref, out_ref, x_ref, ds_ref, halo_ref):
        j = pl.program_id(1)

        @pl.when(pl.program_id(0) == 0)
        def _():
            halo_ref[j] = jnp.zeros((HALO, tn), F32)

        prev = halo_ref[j]
        for r in range(tm // rc):
            rows = pl.ds(r * rc, rc)
            p = _raw_dot(u_ref[rows, :], wm_ref[...], 1, 0)
            x_ref[rows, :] = p.astype(x_ref.dtype)
            pre, _ = _conv_pre(p, prev, w_ref, b_ref[...])
            sg = sigmoid(pre)
            out_ref[rows, :] = pre * sg
            ds_ref[rows, :] = (sg * (1.0 + pre * (1.0 - sg))).astype(ds_ref.dtype)
            prev = p[rc - HALO:]
        halo_ref[j] = prev

    blk = pl.BlockSpec((tm, tn), lambda i, j: (i, j))
    return pl.pallas_call(
        body, name="in_proj_conv", grid=(t // tm, nj),
        in_specs=[pl.BlockSpec((tm, D_MODEL), lambda i, j: (i, 0)),
                  pl.BlockSpec((D_MODEL, tn), lambda i, j: (0, c0 + j)),
                  pl.BlockSpec((4, tn), lambda i, j: (0, j)), pl.BlockSpec((1, tn), lambda i, j: (0, j))],
        out_specs=[blk, blk, blk],
        out_shape=[jax.ShapeDtypeStruct((t, CONV_W), F32), jax.ShapeDtypeStruct((t, CONV_W), MXU_DTYPE),
                   jax.ShapeDtypeStruct((t, CONV_W), MXU_DTYPE)],
        scratch_shapes=[pltpu.VMEM((nj, HALO, tn), F32)],
        compiler_params=_params(("arbitrary", "arbitrary")),
    )(u, w_main, w, b)


def conv_bwd_w(u, x_conv, dsilu, w, dout, slabbed):
    t = u.shape[0]
    tt, tn = min(1024, t), 3 * CONV_TC
    nt, nj = t // tt, CONV_W // tn
    ns = len(slabbed)
    halo_op = 2 * HALO
    after = lambda i, h: jnp.minimum((i + 1) * (tt // h), t // h - 1)

    def body(u_ref, x_ref, ds_ref, ds_nxt_ref, w_ref, do_ref, do_nxt_ref, *rest):
        slab_refs, (dx_ref, dw_ref, dwb_ref) = rest[:ns], rest[ns:ns + 3]
        land_refs, sems = rest[ns + 3:2 * ns + 3], rest[2 * ns + 3:]
        j, i = pl.program_id(0), pl.program_id(1)
        start, finish = _slab_exchange(slab_refs, land_refs, ns, *sems)

        @pl.when(jnp.logical_and(j == 0, i == 0))
        def _():
            start()

        @pl.when(i == 0)
        def _():
            dw_ref[...] = jnp.zeros(dw_ref.shape, F32)
            dwb_ref[...] = jnp.zeros(dwb_ref.shape, F32)

        uu = u_ref[...]
        row = _iota((HALO, CONV_TC), 0)
        last = i == nt - 1
        for piece in range(tn // CONV_TC):
            cols = slice(piece * CONV_TC, (piece + 1) * CONV_TC)
            x = x_ref[:, cols].astype(F32)
            dpre = do_ref[:, cols].astype(F32) * ds_ref[:, cols].astype(F32)
            dpre_nxt = do_nxt_ref[:, cols].astype(F32)[:HALO] * ds_nxt_ref[:, cols].astype(F32)[:HALO]
            dpre_nxt = jnp.where(last, 0.0, dpre_nxt)
            ups = [dpre] + [_shift_up(dpre, dpre_nxt, s) for s in (1, 2, 3)]
            dx = ups[0] * w_ref[3:4, cols]
            upd = jnp.where(row == 4, jnp.sum(dpre, axis=0, keepdims=True), 0.0)
            for s in range(4):
                if s:
                    dx = dx + ups[s] * w_ref[3 - s:4 - s, cols]
                upd = upd + jnp.where(row == 3 - s, jnp.sum(ups[s] * x, axis=0, keepdims=True), 0.0)
            dx = dx.astype(dx_ref.dtype)
            dx_ref[:, cols] = dx
            dw_ref[:, cols] += _raw_dot(uu, dx, 0, 0)
            dwb_ref[:, cols] += upd

        @pl.when(jnp.logical_and(j == nj - 1, last))
        def _():
            finish()

    out = pl.pallas_call(
        body, name="conv_bwd_w", grid=(nj, nt),
        in_specs=[pl.BlockSpec((tt, D_MODEL), lambda j, i: (i, 0)),
                  pl.BlockSpec((tt, tn), lambda j, i: (i, j)),
                  pl.BlockSpec((tt, tn), lambda j, i: (i, j)),
                  pl.BlockSpec((halo_op, tn), lambda j, i: (after(i, halo_op), j)),
                  pl.BlockSpec((4, tn), lambda j, i: (0, j)),
                  pl.BlockSpec((tt, tn), lambda j, i: (i, j)),
                  pl.BlockSpec((halo_op, tn), lambda j, i: (after(i, halo_op), j))] + [HBM] * ns,
        out_specs=[pl.BlockSpec((tt, tn), lambda j, i: (i, j)), pl.BlockSpec((D_MODEL, tn), lambda j, i: (0, j)),
                   pl.BlockSpec((HALO, tn), lambda j, i: (0, j))] + [HBM] * ns,
        out_shape=[jax.ShapeDtypeStruct((t, CONV_W), MXU_DTYPE), jax.ShapeDtypeStruct((D_MODEL, CONV_W), F32),
                   jax.ShapeDtypeStruct((HALO, CONV_W), F32)] + _slab_exchange_shapes(slabbed, []),
        scratch_shapes=_slab_exchange_sems(ns),
        compiler_params=_params(("arbitrary", "arbitrary")),
    )(u, x_conv, dsilu, dsilu, w, dout, dout, *slabbed)
    return out[0], out[1], out[2], out[3:]


def _ssd_cols(g):
    b0 = SSD_WIDTH + g * SSD_STATE
    c0 = SSD_WIDTH + SSD_GROUPS * SSD_STATE + g * SSD_STATE
    return slice(g * SSD_GW, (g + 1) * SSD_GW), slice(b0, b0 + SSD_STATE), slice(c0, c0 + SSD_STATE)


def _gdn_cols(j):
    return tuple(slice(s * GDN_W + j * GDN_DK, s * GDN_W + (j + 1) * GDN_DK) for s in range(3))


def _ssd_parts(xbc_ref):
    return tuple([[xbc_ref[_chunk_rows(c), _ssd_cols(g)[s]] for g in range(SSD_GROUPS)] for c in range(SSD_CB)]
                 for s in range(3))


def _group_cols(ref):
    return [[ref[_chunk_rows(c), g * SSD_GW:(g + 1) * SSD_GW] for g in range(SSD_GROUPS)] for c in range(SSD_CB)]


def _chunk_rows(c):
    return slice(c * CHUNK, (c + 1) * CHUNK)


def _gdn_parts(qkv_ref):
    assert GDN_HB == GDN_HEADS, "the conv block is read whole: one grid step holds every head"
    return tuple([[qkv_ref[_chunk_rows(c), _gdn_cols(j)[s]] for j in range(GDN_HB)] for c in range(GDN_CB)]
                 for s in range(3))


def _head_cols(ref):
    return [[ref[_chunk_rows(c), j * GDN_DV:(j + 1) * GDN_DV] for j in range(GDN_HB)] for c in range(GDN_CB)]


def _chunk_blocks(ref, n=GDN_CB):
    return [ref[_chunk_rows(c), :] for c in range(n)]


def _first_head():
    return 0 if GDN_HB == GDN_HEADS else pl.program_id(1) * GDN_HB


def ssd_fwd(conv_ssd, proj_main, proj_small, normw, alog, dtb, dvec):
    t = conv_ssd.shape[0]
    rows = CHUNK * SSD_CB
    nc = t // rows
    groups = range(SSD_GROUPS)
    norm_cols = lambda ref: [ref[:, g * SSD_GW:(g + 1) * SSD_GW] for g in groups]

    def body(xbc_ref, z_ref, sm_ref, nw_ref, al_ref, db_ref, dv_ref, y_ref, hist_ref, state_ref):
        @pl.when(pl.program_id(0) == 0)
        def _():
            state_ref[...] = jnp.zeros(state_ref.shape, F32)

        states = [state_ref[g] for g in groups]
        for g in groups:
            hist_ref[0, g] = states[g]
        ys, new_states = ssd_chunk(*_ssd_parts(xbc_ref), _group_cols(z_ref), _chunk_blocks(sm_ref, SSD_CB),
                                   norm_cols(nw_ref), al_ref[...], db_ref[...], dv_ref[...], states)
        for c in range(SSD_CB):
            for g in groups:
                y_ref[_chunk_rows(c), g * SSD_GW:(g + 1) * SSD_GW] = ys[c][g].astype(MXU_DTYPE)
        for g in groups:
            state_ref[g] = new_states[g]

    return pl.pallas_call(
        body, name="ssd_fwd", grid=(nc,),
        in_specs=[pl.BlockSpec((rows, SSD_CONV), lambda c: (c, (COL_SSD - COL_CONV) // SSD_CONV)),
                  pl.BlockSpec((rows, SSD_WIDTH), lambda c: (c, COL_Z // SSD_WIDTH)),
                  pl.BlockSpec((rows, LANES), lambda c: (c, 0)),
                  _full((1, SSD_WIDTH)), _full((1, LANES)), _full((1, LANES)), _full((1, LANES))],
        out_specs=[pl.BlockSpec((rows, SSD_WIDTH), lambda c: (c, 0)),
                   pl.BlockSpec((1, SSD_GROUPS, SSD_STATE, SSD_GW), lambda c: (c, 0, 0, 0))],
        out_shape=[jax.ShapeDtypeStruct((t, SSD_WIDTH), MXU_DTYPE),
                   jax.ShapeDtypeStruct((nc, SSD_GROUPS, SSD_STATE, SSD_GW), F32)],
        scratch_shapes=[pltpu.VMEM((SSD_GROUPS, SSD_STATE, SSD_GW), F32)],
        compiler_params=_params(("arbitrary",)),
    )(conv_ssd, proj_main, proj_small, normw, alog, dtb, dvec)


def _accumulate(ref, first, value):
    @pl.when(first)
    def _():
        ref[...] = value

    @pl.when(jnp.logical_not(first))
    def _():
        ref[...] += value


def ssd_bwd(conv_ssd, proj_main, proj_small, normw, alog, dtb, dvec, hist, dy):
    t = conv_ssd.shape[0]
    rows = CHUNK * SSD_CB
    nc = t // rows
    rev = lambda c: nc - 1 - c
    groups = range(SSD_GROUPS)
    norm_cols = lambda ref: [ref[:, g * SSD_GW:(g + 1) * SSD_GW] for g in groups]

    def body(xbc_ref, z_ref, sm_ref, nw_ref, al_ref, db_ref, dv_ref, hist_ref, dy_ref,
             dxbc_ref, dz_ref, dsm_ref, dnw_ref, dal_ref, ddb_ref, ddv_ref, dstate_ref):
        first = pl.program_id(0) == 0

        @pl.when(first)
        def _():
            dstate_ref[...] = jnp.zeros(dstate_ref.shape, F32)

        _, vjp = jax.vjp(ssd_chunk, *_ssd_parts(xbc_ref), _group_cols(z_ref), _chunk_blocks(sm_ref, SSD_CB),
                         norm_cols(nw_ref), al_ref[...], db_ref[...], dv_ref[...], [hist_ref[0, g] for g in groups])
        dxs, dbm, dcm, dz, dsm, dnw, dal, ddb, ddv, dstate = vjp(
            (_group_cols(dy_ref), [dstate_ref[g] for g in groups]))
        for k in range(SSD_CB):
            rk = _chunk_rows(k)
            for g in groups:
                xc, bc, cc = _ssd_cols(g)
                dxbc_ref[rk, xc] = dxs[k][g].astype(dxbc_ref.dtype)
                dxbc_ref[rk, bc] = dbm[k][g].astype(dxbc_ref.dtype)
                dxbc_ref[rk, cc] = dcm[k][g].astype(dxbc_ref.dtype)
                dz_ref[rk, g * SSD_GW:(g + 1) * SSD_GW] = dz[k][g].astype(dz_ref.dtype)
            dsm_ref[rk, :] = dsm[k]
        for g in groups:
            dstate_ref[g] = dstate[g]
        _accumulate(dnw_ref, first, join_lanes(dnw))
        _accumulate(dal_ref, first, dal)
        _accumulate(ddb_ref, first, ddb)
        _accumulate(ddv_ref, first, ddv)

    return pl.pallas_call(
        body, name="ssd_bwd", grid=(nc,),
        in_specs=[pl.BlockSpec((rows, SSD_CONV), lambda c: (rev(c), (COL_SSD - COL_CONV) // SSD_CONV)),
                  pl.BlockSpec((rows, SSD_WIDTH), lambda c: (rev(c), COL_Z // SSD_WIDTH)),
                  pl.BlockSpec((rows, LANES), lambda c: (rev(c), 0)),
                  _full((1, SSD_WIDTH)), _full((1, LANES)), _full((1, LANES)), _full((1, LANES)),
                  pl.BlockSpec((1, SSD_GROUPS, SSD_STATE, SSD_GW), lambda c: (rev(c), 0, 0, 0)),
                  pl.BlockSpec((rows, SSD_WIDTH), lambda c: (rev(c), 0))],
        out_specs=[pl.BlockSpec((rows, SSD_CONV), lambda c: (rev(c), (COL_SSD - COL_CONV) // SSD_CONV)),
                   pl.BlockSpec((rows, SSD_WIDTH), lambda c: (rev(c), COL_Z // SSD_WIDTH)),
                   pl.BlockSpec((rows, LANES), lambda c: (rev(c), 0)),
                   _full((1, SSD_WIDTH)), _full((1, LANES)), _full((1, LANES)), _full((1, LANES))],
        out_shape=[jax.ShapeDtypeStruct((t, CONV_W), MXU_DTYPE), jax.ShapeDtypeStruct((t, COL_CONV), MXU_DTYPE),
                   jax.ShapeDtypeStruct((t, LANES), F32), jax.ShapeDtypeStruct((1, SSD_WIDTH), F32),
                   jax.ShapeDtypeStruct((1, LANES), F32), jax.ShapeDtypeStruct((1, LANES), F32),
                   jax.ShapeDtypeStruct((1, LANES), F32)],
        scratch_shapes=[pltpu.VMEM((SSD_GROUPS, SSD_STATE, SSD_GW), F32)],
        compiler_params=_params(("arbitrary",)),
    )(conv_ssd, proj_main, proj_small, normw, alog, dtb, dvec, hist, dy)


def gdn_fwd(conv_gdn, proj_main, proj_small, normw, alog, dtb):
    t = conv_gdn.shape[0]
    hb, cb = GDN_HB, GDN_CB
    rows = CHUNK * cb
    ns = t // rows
    gate_blk = COL_GATE // (GDN_DV * hb)

    def body(qkv_ref, gate_ref, sm_ref, nw_ref, al_ref, db_ref, y_ref, hist_ref, t_ref, state_ref):
        h0 = _first_head()

        @pl.when(pl.program_id(0) == 0)
        def _():
            for j in range(hb):
                state_ref[h0 + j] = jnp.zeros((GDN_DK, GDN_DV), F32)

        states = [state_ref[h0 + j] for j in range(hb)]
        for j in range(hb):
            hist_ref[0, j] = states[j]
        qs, ks, vs = _gdn_parts(qkv_ref)
        ys, new_states, ts = gdn_chunk(h0, qs, ks, vs, _chunk_blocks(sm_ref), _head_cols(gate_ref), nw_ref[...],
                                       al_ref[...], db_ref[...], states)
        for c in range(cb):
            for j in range(hb):
                y_ref[_chunk_rows(c), j * GDN_DV:(j + 1) * GDN_DV] = ys[c][j].astype(MXU_DTYPE)
        for j in range(hb):
            state_ref[h0 + j] = new_states[j]
        for p in range(cb * hb // 2):
            t_ref[0, p] = ts[p]

    return pl.pallas_call(
        body, name="gdn_fwd", grid=(ns, GDN_HEADS // hb),
        in_specs=[pl.BlockSpec((rows, GDN_HC * hb), lambda c, h: (c, h)),
                  pl.BlockSpec((rows, GDN_DV * hb), lambda c, h: (c, gate_blk + h)),
                  pl.BlockSpec((rows, LANES), lambda c, h: (c, 0)),
                  _full((1, GDN_DV)), _full((1, LANES)), _full((1, LANES))],
        out_specs=[pl.BlockSpec((rows, GDN_DV * hb), lambda c, h: (c, h)),
                   pl.BlockSpec((1, hb, GDN_DK, GDN_DV), lambda c, h: (c, h, 0, 0)),
                   pl.BlockSpec((1, cb * hb // 2, CHUNK, LANES), lambda c, h: (c, h, 0, 0))],
        out_shape=[jax.ShapeDtypeStruct((t, GDN_W), MXU_DTYPE),
                   jax.ShapeDtypeStruct((ns, GDN_HEADS, GDN_DK, GDN_DV), F32),
                   jax.ShapeDtypeStruct((ns, cb * GDN_HEADS // 2, CHUNK, LANES), F32)],
        scratch_shapes=[pltpu.VMEM((GDN_HEADS, GDN_DK, GDN_DV), F32)],
        compiler_params=_params(("arbitrary", "arbitrary")),
    )(conv_gdn, proj_main, proj_small, normw, alog, dtb)


def gdn_bwd(dproj_main, dconv, conv_gdn, proj_main, proj_small, normw, alog, dtb, hist, t_inv, dy):
    t = conv_gdn.shape[0]
    hb, cb = GDN_HB, GDN_CB
    rows = CHUNK * cb
    ns = t // rows
    rev = lambda c: ns - 1 - c
    gate_blk = COL_GATE // (GDN_DV * hb)

    def body(alias_ref, alias2_ref, qkv_ref, gate_ref, sm_ref, nw_ref, al_ref, db_ref, hist_ref, t_ref, dy_ref,
             dgate_ref, dqkv_ref, dsm_ref, dnw_ref, dal_ref, ddb_ref, dstate_ref):
        del alias_ref, alias2_ref
        c, h = pl.program_id(0), pl.program_id(1)
        h0 = _first_head()

        @pl.when(c == 0)
        def _():
            for j in range(hb):
                dstate_ref[h0 + j] = jnp.zeros((GDN_DK, GDN_DV), F32)

        saved = [t_ref[0, p] for p in range(cb * hb // 2)]

        def fn(qs, ks, vs, smalls, gates, nw, al, db, states):
            return gdn_chunk(h0, qs, ks, vs, smalls, gates, nw, al, db, states, saved)[:2]

        qs, ks, vs = _gdn_parts(qkv_ref)
        _, vjp = jax.vjp(fn, qs, ks, vs, _chunk_blocks(sm_ref), _head_cols(gate_ref), nw_ref[...], al_ref[...],
                         db_ref[...], [hist_ref[0, j] for j in range(hb)])
        dqs, dks, dvs, dsm, dgates, dnw, dal, ddb, dstates = vjp(
            (_head_cols(dy_ref), [dstate_ref[h0 + j] for j in range(hb)]))
        for k in range(cb):
            rk = _chunk_rows(k)
            for j in range(hb):
                qc, kc, vc = _gdn_cols(j)
                dqkv_ref[rk, qc] = dqs[k][j].astype(dqkv_ref.dtype)
                dqkv_ref[rk, kc] = dks[k][j].astype(dqkv_ref.dtype)
                dqkv_ref[rk, vc] = dvs[k][j].astype(dqkv_ref.dtype)
                dgate_ref[rk, j * GDN_DV:(j + 1) * GDN_DV] = dgates[k][j].astype(dgate_ref.dtype)
        for j in range(hb):
            dstate_ref[h0 + j] = dstates[j]
        _accumulate(dsm_ref, h == 0, jnp.concatenate(dsm, axis=0))
        first = jnp.logical_and(c == 0, h == 0)
        _accumulate(dnw_ref, first, dnw)
        _accumulate(dal_ref, first, dal)
        _accumulate(ddb_ref, first, ddb)

    return pl.pallas_call(
        body, name="gdn_bwd", grid=(ns, GDN_HEADS // hb),
        in_specs=[ANY, ANY, pl.BlockSpec((rows, GDN_HC * hb), lambda c, h: (rev(c), h)),
                  pl.BlockSpec((rows, GDN_DV * hb), lambda c, h: (rev(c), gate_blk + h)),
                  pl.BlockSpec((rows, LANES), lambda c, h: (rev(c), 0)),
                  _full((1, GDN_DV)), _full((1, LANES)), _full((1, LANES)),
                  pl.BlockSpec((1, hb, GDN_DK, GDN_DV), lambda c, h: (rev(c), h, 0, 0)),
                  pl.BlockSpec((1, cb * hb // 2, CHUNK, LANES), lambda c, h: (rev(c), h, 0, 0)),
                  pl.BlockSpec((rows, GDN_DV * hb), lambda c, h: (rev(c), h))],
        out_specs=[pl.BlockSpec((rows, GDN_DV * hb), lambda c, h: (rev(c), gate_blk + h)),
                   pl.BlockSpec((rows, GDN_HC * hb), lambda c, h: (rev(c), h)),
                   pl.BlockSpec((rows, LANES), lambda c, h: (rev(c), 0)),
                   _full((1, GDN_DV)), _full((1, LANES)), _full((1, LANES))],
        out_shape=[jax.ShapeDtypeStruct(dproj_main.shape, dproj_main.dtype),
                   jax.ShapeDtypeStruct(dconv.shape, dconv.dtype),
                   jax.ShapeDtypeStruct((t, LANES), F32), jax.ShapeDtypeStruct((1, GDN_DV), F32),
                   jax.ShapeDtypeStruct((1, LANES), F32), jax.ShapeDtypeStruct((1, LANES), F32)],
        scratch_shapes=[pltpu.VMEM((GDN_HEADS, GDN_DK, GDN_DV), F32)],
        input_output_aliases={0: 0, 1: 1},
        compiler_params=_params(("arbitrary", "arbitrary")),
    )(dproj_main, dconv, conv_gdn, proj_main, proj_small, normw, alog, dtb, hist, t_inv, dy)


def out_proj_loss(x, y_ssd, y_gdn, w_out, final_w, target):
    t = x.shape[0]
    tm = min(512, t)

    def body(x_ref, ys_ref, yg_ref, wo_ref, fw_ref, tg_ref, loss_ref, dhid_ref, dys_ref, dyg_ref, dwo_ref, dfw_ref):
        i = pl.program_id(0)
        ys, yg = ys_ref[...], yg_ref[...]
        wo_s, wo_g = wo_ref[:SSD_WIDTH, :], wo_ref[SSD_WIDTH:, :]
        hid = x_ref[...] + _raw_dot(ys, wo_s, 1, 0) + _raw_dot(yg, wo_g, 1, 0)
        out, vjp = jax.vjp(rmsnorm, hid, fw_ref[...])
        err = out - tg_ref[...]
        loss = 0.5 * jnp.sum(jnp.mean(err * err, axis=-1, keepdims=True), axis=0, keepdims=True)
        dhid, dfw = vjp(err * (1.0 / D_MODEL))
        dhid_ref[...] = dhid
        dys_ref[...] = _raw_dot(dhid, wo_s, 1, 1)
        dyg_ref[...] = _raw_dot(dhid, wo_g, 1, 1)
        first = i == 0
        _accumulate(loss_ref, first, jnp.broadcast_to(loss, loss_ref.shape))
        _accumulate(dfw_ref, first, dfw)

        @pl.when(first)
        def _():
            dwo_ref[:SSD_WIDTH, :] = _raw_dot(ys, dhid, 0, 0)
            dwo_ref[SSD_WIDTH:, :] = _raw_dot(yg, dhid, 0, 0)

        @pl.when(i > 0)
        def _():
            dwo_ref[:SSD_WIDTH, :] += _raw_dot(ys, dhid, 0, 0)
            dwo_ref[SSD_WIDTH:, :] += _raw_dot(yg, dhid, 0, 0)

    row = lambda w: pl.BlockSpec((tm, w), lambda i: (i, 0))
    return pl.pallas_call(
        body, name="out_proj_loss", grid=(t // tm,),
        in_specs=[row(D_MODEL), row(SSD_WIDTH), row(GDN_W), _full((SSD_WIDTH + GDN_W, D_MODEL)), _full((1, D_MODEL)),
                  row(D_MODEL)],
        out_specs=[_full((8, LANES)), row(D_MODEL), row(SSD_WIDTH), row(GDN_W), _full((SSD_WIDTH + GDN_W, D_MODEL)),
                   _full((1, D_MODEL))],
        out_shape=[jax.ShapeDtypeStruct((8, LANES), F32), jax.ShapeDtypeStruct((t, D_MODEL), F32),
                   jax.ShapeDtypeStruct((t, SSD_WIDTH), F32), jax.ShapeDtypeStruct((t, GDN_W), F32),
                   jax.ShapeDtypeStruct((SSD_WIDTH + GDN_W, D_MODEL), F32), jax.ShapeDtypeStruct((1, D_MODEL), F32)],
        compiler_params=_params(("arbitrary",)),
    )(x, y_ssd, y_gdn, w_out, final_w, target)


def in_proj_bwd_x(x, normw, w_main, w_small, dproj_main, dproj_conv, dsmall_a, dsmall_b, dhid, slabbed):
    t = x.shape[0]
    tm = min(256, t)
    ni = t // tm
    ns = len(slabbed)

    def body(x_ref, nw_ref, wm_ref, ws_ref, dp_ref, dc_ref, da_ref, db_ref, dh_ref, *rest):
        slab_refs, (gx_ref, dnw_ref), land_refs = rest[:ns], rest[ns:ns + 2], rest[ns + 2:2 * ns + 2]
        sems = rest[2 * ns + 2:]
        i = pl.program_id(0)
        start, finish = _slab_exchange(slab_refs, land_refs, ns, *sems)

        @pl.when(i == 0)
        def _():
            start()

        du = (_raw_dot(dp_ref[...], wm_ref[:, :COL_CONV], 1, 1) + _raw_dot(dc_ref[...], wm_ref[:, COL_CONV:], 1, 1)
              + _raw_dot(da_ref[...] + db_ref[...], ws_ref[...], 1, 1))
        _, vjp = jax.vjp(rmsnorm, x_ref[...], nw_ref[...])
        dx, dnw = vjp(du)
        gx_ref[...] = dx + dh_ref[...]
        _accumulate(dnw_ref, i == 0, dnw)

        @pl.when(i == ni - 1)
        def _():
            finish()

    row = lambda w: pl.BlockSpec((tm, w), lambda i: (i, 0))
    out = pl.pallas_call(
        body, name="in_proj_bwd_x", grid=(ni,),
        in_specs=[row(D_MODEL), _full((1, D_MODEL)), _full((D_MODEL, MAIN)), _full((D_MODEL, LANES)), row(COL_CONV),
                  row(CONV_W), row(LANES), row(LANES), row(D_MODEL)] + [HBM] * ns,
        out_specs=[row(D_MODEL), _full((1, D_MODEL))] + [HBM] * ns,
        out_shape=[jax.ShapeDtypeStruct((t, D_MODEL), F32), jax.ShapeDtypeStruct((1, D_MODEL), F32)]
        + _slab_exchange_shapes(slabbed, []),
        scratch_shapes=_slab_exchange_sems(ns),
        compiler_params=_params(("arbitrary",)),
    )(x, normw, w_main, w_small, dproj_main, dproj_conv, dsmall_a, dsmall_b, dhid, *slabbed)
    return out[0], out[1], out[2:]


def in_proj_bwd_w(u, dproj_main, dsmall_a, dsmall_b):
    t = u.shape[0]
    tm = min(2048, t)
    n, half = t // tm, COL_CONV // 2
    nb = min(2, n)

    def body(u_hbm, dp_hbm, da_hbm, db_hbm, dwm_hbm, dws_ref, ubuf, dpbuf, dabuf, dbbuf, acc, in_sems, out_sems):
        pairs = ((u_hbm, ubuf), (dp_hbm, dpbuf), (da_hbm, dabuf), (db_hbm, dbbuf))

        def copies(s):
            slot = s % nb
            return [pltpu.make_async_copy(src.at[pl.ds(s * tm, tm)], buf.at[slot], in_sems.at[k, slot])
                    for k, (src, buf) in enumerate(pairs)]

        def write_back(h):
            cols = pl.ds(h * half, half)
            return pltpu.make_async_copy(acc.at[:, cols], dwm_hbm.at[:, cols], out_sems.at[h])

        for s in range(nb):
            for c in copies(s):
                c.start()
        for s in range(n):
            slot = s % nb
            for c in copies(s):
                c.wait()
            uu = ubuf[slot]
            for h in range(2):
                cols = pl.ds(h * half, half)
                val = _raw_dot(uu, dpbuf[slot, :, cols], 0, 0)
                if s == 0:
                    acc[:, cols] = val
                else:
                    acc[:, cols] += val
                if s == n - 1:
                    write_back(h).start()
            small = _raw_dot(uu, dabuf[slot] + dbbuf[slot], 0, 0)
            if s == 0:
                dws_ref[...] = small
            else:
                dws_ref[...] += small
            if s + nb < n:
                for c in copies(s + nb):
                    c.start()
        for h in range(2):
            write_back(h).wait()

    return pl.pallas_call(
        body, name="in_proj_bwd_w",
        in_specs=[ANY, ANY, ANY, ANY],
        out_specs=[ANY, pl.BlockSpec(memory_space=pltpu.VMEM)],
        out_shape=[jax.ShapeDtypeStruct((D_MODEL, COL_CONV), F32), jax.ShapeDtypeStruct((D_MODEL, LANES), F32)],
        scratch_shapes=[pltpu.VMEM((nb, tm, D_MODEL), u.dtype), pltpu.VMEM((nb, tm, COL_CONV), dproj_main.dtype),
                        pltpu.VMEM((nb, tm, LANES), dsmall_a.dtype), pltpu.VMEM((nb, tm, LANES), dsmall_b.dtype),
                        pltpu.VMEM((D_MODEL, COL_CONV), F32),
                        pltpu.SemaphoreType.DMA((4, nb)), pltpu.SemaphoreType.DMA((2,))],
        compiler_params=_params(),
    )(u, dproj_main, dsmall_a, dsmall_b)


def sum_slabs(a, name):
    n, rows, cols = a.shape
    tr = 64 if rows % 64 == 0 else rows

    def body(a_ref, o_ref):
        acc = a_ref[0].astype(F32)
        for d in range(1, n):
            acc = acc + a_ref[d].astype(F32)
        o_ref[...] = acc

    return pl.pallas_call(
        body, name=name, grid=(rows // tr,),
        in_specs=[pl.BlockSpec((n, tr, cols), lambda i: (0, i, 0))],
        out_specs=pl.BlockSpec((tr, cols), lambda i: (i, 0)),
        out_shape=jax.ShapeDtypeStruct((rows, cols), F32),
        compiler_params=_params(("arbitrary",)),
    )(a)


def adamw(w, g, m, v, name):
    _, rows, cols = w.shape
    tr = 128 if rows % 128 == 0 else rows

    def body(w_ref, g_ref, m_ref, v_ref, d_ref, nm_ref, nv_ref):
        gg = g_ref[...]
        nm = ADAM_B1 * m_ref[...] + (1.0 - ADAM_B1) * gg
        nv = ADAM_B2 * v_ref[...] + (1.0 - ADAM_B2) * (gg * gg)
        m_hat = nm / (1.0 - ADAM_B1 ** ADAM_STEP)
        v_hat = nv / (1.0 - ADAM_B2 ** ADAM_STEP)
        d_ref[...] = -ADAM_LR * (m_hat / (jnp.sqrt(v_hat) + ADAM_EPS) + ADAM_WD * w_ref[...])
        nm_ref[...] = nm
        nv_ref[...] = nv

    spec = pl.BlockSpec((1, tr, cols), lambda i: (0, i, 0))
    shp = jax.ShapeDtypeStruct((1, rows, cols), F32)
    return pl.pallas_call(
        body, name=name, grid=(rows // tr,), in_specs=[spec] * 4, out_specs=[spec] * 3, out_shape=[shp] * 3,
        compiler_params=_params(("arbitrary",)),
    )(w, g.reshape(w.shape), m, v)


def _my_place():
    return lax.axis_index("x"), lax.axis_index("y"), lax.axis_index("c")


def gather_weights(big, small):
    nb, n = len(big), len(big) + len(small)
    parts = 4

    def body(*refs):
        srcs, outs = refs[:n], refs[n:2 * n]
        land_a, land_b = refs[2 * n:2 * n + nb], refs[2 * n + nb:2 * n + 2 * nb]
        send_sems, recv_sems, fwd_send, fwd_recv, local_sems = refs[2 * n + 2 * nb:]
        x, y, c = _my_place()
        me = 2 * x + y
        chips = [(1 - x, y), (x, 1 - y), (1 - x, 1 - y)]
        half = [a.shape[0] // 2 for a in big]

        def ici(j, i):
            px, py = chips[j]
            if i < nb:
                src, dst = srcs[i].at[pl.ds(c * half[i], half[i])], land_a[i].at[j]
            else:
                src, dst = srcs[i], outs[i].at[me]
            return pltpu.make_async_remote_copy(src_ref=src, dst_ref=dst, send_sem=send_sems.at[j * n + i],
                                                recv_sem=recv_sems.at[j * n + i], device_id=(px, py, c),
                                                device_id_type=MESH)

        def ici_arrival(j, i):
            px, py = chips[j]
            dst = land_a[i].at[j] if i < nb else outs[i].at[2 * px + py]
            return pltpu.make_async_remote_copy(src_ref=dst, dst_ref=dst, send_sem=send_sems.at[j * n + i],
                                                recv_sem=recv_sems.at[j * n + i], device_id=(px, py, c),
                                                device_id_type=MESH)

        def forward(j, i, p):
            rows = half[i] // parts
            k = (j * nb + i) * parts + p
            return pltpu.make_async_remote_copy(
                src_ref=land_a[i].at[j, pl.ds(p * rows, rows)], dst_ref=land_b[i].at[j, pl.ds(p * rows, rows)],
                send_sem=fwd_send.at[k], recv_sem=fwd_recv.at[k], device_id=(x, y, 1 - c), device_id_type=MESH)

        def store(j, i, from_sibling):
            px, py = chips[j]
            buf, h = (land_b, 1 - c) if from_sibling else (land_a, c)
            k = n + (j * nb + i) * 2 + (1 if from_sibling else 0)
            return pltpu.make_async_copy(buf[i].at[j], outs[i].at[2 * px + py, pl.ds(h * half[i], half[i])],
                                         local_sems.at[k])

        own = [pltpu.make_async_copy(srcs[i], outs[i].at[me], local_sems.at[i]) for i in range(n)]
        sends = [ici(j, i) for j in range(3) for i in range(n)]
        for cp in own + sends:
            cp.start()
        pending = []
        for j in range(3):
            for i in range(n):
                ici_arrival(j, i).wait_recv()
                if i < nb:
                    fw = [forward(j, i, p) for p in range(parts)]
                    st = store(j, i, False)
                    for cp in fw + [st]:
                        cp.start()
                    pending += [cp.wait_send for cp in fw] + [st.wait]
        for j in range(3):
            for i in range(nb):
                for p in range(parts):
                    forward(j, i, p).wait_recv()
                st = store(j, i, True)
                st.start()
                pending.append(st.wait)
        for cp in sends:
            cp.wait_send()
        for wait in pending:
            wait()
        for cp in own:
            cp.wait()

    shards = list(big) + list(small)
    lands = [pltpu.VMEM((3, a.shape[0] // 2) + a.shape[1:], a.dtype) for a in big]
    return pl.pallas_call(
        body, name="gather_weights",
        in_specs=[HBM] * n, out_specs=[HBM] * n,
        out_shape=[jax.ShapeDtypeStruct((N_CHIP,) + s.shape, s.dtype) for s in shards],
        scratch_shapes=lands + lands + [
            pltpu.SemaphoreType.DMA((3 * n,)), pltpu.SemaphoreType.DMA((3 * n,)),
            pltpu.SemaphoreType.DMA((3 * nb * parts,)), pltpu.SemaphoreType.DMA((3 * nb * parts,)),
            pltpu.SemaphoreType.DMA((n + 6 * nb,))],
        compiler_params=pltpu.CompilerParams(vmem_limit_bytes=VMEM_LIMIT),
    )(*shards)


def _peer(x, y, c, mask):
    mx, my, mc = (mask >> 2) & 1, (mask >> 1) & 1, mask & 1
    return (x ^ mx if mx else x, y ^ my if my else y, c ^ mc if mc else c)


def _slab_exchange_shapes(slabbed, replicated):
    return ([jax.ShapeDtypeStruct(a.shape, a.dtype) for a in slabbed]
            + [jax.ShapeDtypeStruct((N_DEV,) + a.shape, a.dtype) for a in replicated])


def _slab_exchange_sems(n):
    return [pltpu.SemaphoreType.DMA((7 * n,)), pltpu.SemaphoreType.DMA((7 * n,)), pltpu.SemaphoreType.DMA((n,))]


def _slab_exchange(srcs, outs, ns, send_sems, recv_sems, local_sems):
    n = len(srcs)
    x, y, c = _my_place()
    me = 4 * x + 2 * y + c

    def piece(i, dev):
        return srcs[i].at[dev] if i < ns else srcs[i]

    def copies(arriving):
        out = []
        for mask in range(1, N_DEV):
            px, py, pc = _peer(x, y, c, mask)
            dev = 4 * px + 2 * py + pc
            for i in range(n):
                k = (mask - 1) * n + i
                out.append(pltpu.make_async_remote_copy(
                    src_ref=piece(i, dev), dst_ref=outs[i].at[dev if arriving else me], send_sem=send_sems.at[k],
                    recv_sem=recv_sems.at[k], device_id=(px, py, pc), device_id_type=MESH))
        return out

    def local():
        return [pltpu.make_async_copy(piece(i, me), outs[i].at[me], local_sems.at[i]) for i in range(n)]

    def start():
        for cp in local() + copies(False):
            cp.start()

    def finish():
        for cp in copies(True):
            cp.wait_recv()
        for cp in copies(False):
            cp.wait_send()
        for cp in local():
            cp.wait()

    return start, finish


def exchange_halves(landed, replicated):
    n, nr = len(landed), len(replicated)
    streams = 8
    halves = [jax.ShapeDtypeStruct(a.shape[1:], F32) for a in landed]
    sum_rows = 64

    def body(*refs):
        srcs, rep_srcs, outs, rep_outs = refs[:n], refs[n:n + nr], refs[n + nr:2 * n + nr], refs[2 * n + nr:2 * (n + nr)]
        refs = refs[2 * (n + nr):]
        slabs, mine, theirs = refs[:n], refs[n:2 * n], refs[2 * n:3 * n]
        send_sems, recv_sems, in_sems, out_sems = refs[3 * n:3 * n + 4]
        rep_start, rep_finish = _slab_exchange(rep_srcs, rep_outs, 0, *refs[3 * n + 4:])
        rep_start()
        x, y, c = _my_place()
        loads = [pltpu.make_async_copy(srcs[i], slabs[i], in_sems.at[i]) for i in range(n)]
        for cp in loads:
            cp.start()
        for i in range(n):
            loads[i].wait()
            for r in range(0, halves[i].shape[0], sum_rows):
                rows = pl.ds(r, sum_rows)
                acc = slabs[i][0, rows, :].astype(F32)
                for d in range(1, N_DEV):
                    acc = acc + slabs[i][d, rows, :].astype(F32)
                mine[i][rows, :] = acc

        def chunk_copy(i, s):
            rows = halves[i].shape[0] // streams
            k = i * streams + s
            return pltpu.make_async_remote_copy(
                src_ref=mine[i].at[pl.ds(s * rows, rows)], dst_ref=theirs[i].at[pl.ds(s * rows, rows)],
                send_sem=send_sems.at[k], recv_sem=recv_sems.at[k], device_id=(x, y, 1 - c), device_id_type=MESH)

        sends = [chunk_copy(i, s) for i in range(n) for s in range(streams)]
        for cp in sends:
            cp.start()
        own = [pltpu.make_async_copy(mine[i], outs[i].at[c], out_sems.at[i]) for i in range(n)]
        for cp in own:
            cp.start()
        for cp in sends:
            cp.wait_recv()
        got = [pltpu.make_async_copy(theirs[i], outs[i].at[1 - c], out_sems.at[n + i]) for i in range(n)]
        for cp in got:
            cp.start()
        for cp in sends:
            cp.wait_send()
        for cp in own + got:
            cp.wait()
        rep_finish()

    vmem = [pltpu.VMEM(a.shape, a.dtype) for a in halves]
    out = pl.pallas_call(
        body, name="exchange_halves",
        in_specs=[HBM] * (n + nr), out_specs=[HBM] * (n + nr),
        out_shape=[jax.ShapeDtypeStruct((2,) + a.shape, a.dtype) for a in halves]
        + _slab_exchange_shapes([], replicated),
        scratch_shapes=[pltpu.VMEM(a.shape, a.dtype) for a in landed] + vmem + vmem
        + [pltpu.SemaphoreType.DMA((n * streams,)), pltpu.SemaphoreType.DMA((n * streams,)),
           pltpu.SemaphoreType.DMA((n,)), pltpu.SemaphoreType.DMA((2 * n,))] + _slab_exchange_sems(nr),
        compiler_params=pltpu.CompilerParams(vmem_limit_bytes=VMEM_LIMIT),
    )(*landed, *replicated)
    return out[:n], out[n:]


def _pack_cols(pieces):
    offs, pos = [], 0
    for a in pieces:
        offs.append(pos)
        pos += a.shape[1]
    rows8 = [jnp.pad(a.astype(F32), ((0, 8 - a.shape[0]), (0, 0))) for a in pieces]
    return jnp.concatenate(rows8, axis=1), offs


def adamw_many(ws, gs, ms, vs):
    n = len(ws)

    def body(*refs):
        w_r, g_r, m_r, v_r = refs[:n], refs[n:2 * n], refs[2 * n:3 * n], refs[3 * n:4 * n]
        d_o, m_o, v_o = refs[4 * n:5 * n], refs[5 * n:6 * n], refs[6 * n:7 * n]
        for i in range(n):
            gg = g_r[i][...]
            nm = ADAM_B1 * m_r[i][...] + (1.0 - ADAM_B1) * gg
            nv = ADAM_B2 * v_r[i][...] + (1.0 - ADAM_B2) * (gg * gg)
            m_hat = nm / (1.0 - ADAM_B1 ** ADAM_STEP)
            v_hat = nv / (1.0 - ADAM_B2 ** ADAM_STEP)
            d_o[i][...] = -ADAM_LR * (m_hat / (jnp.sqrt(v_hat) + ADAM_EPS) + ADAM_WD * w_r[i][...])
            m_o[i][...] = nm
            v_o[i][...] = nv

    shapes = [jax.ShapeDtypeStruct(w.shape, F32) for w in ws]
    out = pl.pallas_call(body, name="adamw_small", out_shape=shapes * 3,
                         compiler_params=pltpu.CompilerParams(vmem_limit_bytes=VMEM_LIMIT))(*ws, *gs, *ms, *vs)
    return out[:n], out[n:2 * n], out[2 * n:]


def _lanes(vec, start):
    n = vec.shape[-1]
    return jnp.pad(vec.reshape(1, n).astype(F32), ((0, 0), (start, LANES - start - n)))


def kernel(x, norm_w, w_in, ssd_conv_w, ssd_conv_b, ssd_dt_bias, ssd_a_log, ssd_d, ssd_norm_w, gdn_conv_w, gdn_dt_bias, gdn_a_log, gdn_norm_w, w_out, final_norm_w, loss_target, m_norm_w, m_w_in, m_ssd_conv_w, m_ssd_conv_b, m_ssd_dt_bias, m_ssd_a_log, m_ssd_d, m_ssd_norm_w, m_gdn_conv_w, m_gdn_dt_bias, m_gdn_a_log, m_gdn_norm_w, m_w_out, m_final_norm_w, v_norm_w, v_w_in, v_ssd_conv_w, v_ssd_conv_b, v_ssd_dt_bias, v_ssd_a_log, v_ssd_d, v_ssd_norm_w, v_gdn_conv_w, v_gdn_dt_bias, v_gdn_a_log, v_gdn_norm_w, v_w_out, v_final_norm_w):
    xs = x[0]
    target = loss_target[0]
    chip = 2 * lax.axis_index("x") + lax.axis_index("y")
    w_in_shard, w_out_shard = w_in[0], w_out[0]
    in_cols = w_in_shard.shape[1]
    out_rows = w_out_shard.shape[0]

    g_in, g_out, g_cs, g_cg = gather_weights(
        [w_in_shard.astype(MXU_DTYPE), w_out_shard.astype(MXU_DTYPE)], [ssd_conv_w[0], gdn_conv_w[0]])
    w_in_full = jnp.concatenate([g_in[k] for k in range(N_CHIP)], axis=1)
    w_out_full = g_out.reshape(N_CHIP * out_rows, D_MODEL)
    cw_ssd = jnp.concatenate([g_cs[k] for k in range(N_CHIP)], axis=1)
    cw_gdn = jnp.concatenate([g_cg[k] for k in range(N_CHIP)], axis=1)
    cb_ssd, cb_gdn = ssd_conv_b, jnp.zeros((1, GDN_CONV), F32)
    o_xbc, o_dt, o_gate, o_qkv, o_ab = 1024, 2560, 2576, 3600, 6672
    w_main = jnp.concatenate([w_in_full[:, :o_xbc], w_in_full[:, o_gate:o_qkv], w_in_full[:, o_qkv:o_ab],
                              w_in_full[:, o_xbc:o_dt]], axis=1)
    w_small = jnp.concatenate([w_in_full[:, o_dt:o_gate], w_in_full[:, o_ab:],
                               jnp.zeros((D_MODEL, LANES - 32), MXU_DTYPE)], axis=1)
    alog = _lanes(ssd_a_log, 0) + _lanes(gdn_a_log, LANE_GA)
    dtb = _lanes(ssd_dt_bias, 0) + _lanes(gdn_dt_bias, LANE_GA)
    dvec = _lanes(ssd_d, 0)
    fw = final_norm_w.reshape(1, D_MODEL)

    cw, cb = jnp.concatenate([cw_gdn, cw_ssd], axis=1), jnp.concatenate([cb_gdn, cb_ssd], axis=1)
    proj_main, proj_small, u = in_proj(xs, norm_w, w_main, w_small)
    conv_out, x_conv, dsilu_conv = in_proj_conv(u, w_main, cw, cb)
    y_ssd, hist_ssd = ssd_fwd(conv_out, proj_main, proj_small, ssd_norm_w, alog, dtb, dvec)
    y_gdn, hist_gdn, tinv_gdn = gdn_fwd(conv_out, proj_main, proj_small, gdn_norm_w, alog, dtb)

    loss_blk, dhid, dy_ssd, dy_gdn, d_w_out, d_fw = out_proj_loss(xs, y_ssd, y_gdn, w_out_full, fw, target)
    dconv, dproj_main, dsmall_ssd, d_ssd_nw, d_alog_s, d_dtb_s, d_dvec = ssd_bwd(
        conv_out, proj_main, proj_small, ssd_norm_w, alog, dtb, dvec, hist_ssd, dy_ssd)
    dproj_main, dconv, dsmall_gdn, d_gdn_nw, d_alog_g, d_dtb_g = gdn_bwd(
        dproj_main, dconv, conv_out, proj_main, proj_small, gdn_norm_w, alog, dtb, hist_gdn, tinv_gdn, dy_gdn)
    slabs_out = d_w_out.reshape(N_DEV, out_rows // 2, D_MODEL).astype(COMM_DTYPE)
    dproj_conv, d_w_conv, dwb, (r_out,) = conv_bwd_w(u, x_conv, dsilu_conv, cw, dconv, [slabs_out])
    dwb_gdn, dwb_ssd = dwb[:, :GDN_CONV], dwb[:, GDN_CONV:]
    d_w_zg, d_w_small = in_proj_bwd_w(u, dproj_main, dsmall_ssd, dsmall_gdn)
    order = [(d_w_zg, 0, COL_GATE), (d_w_conv, COL_SSD - COL_CONV, CONV_W), (d_w_small, 0, 16),
             (d_w_zg, COL_GATE, COL_CONV), (d_w_conv, 0, COL_SSD - COL_CONV), (d_w_small, 16, 32)]
    shards, pos = [[] for _ in range(N_CHIP)], 0
    for src, lo, hi in order:
        while lo < hi:
            k = pos // in_cols
            n = min(hi - lo, (k + 1) * in_cols - pos)
            shards[k].append(src[:, lo:lo + n].astype(COMM_DTYPE))
            lo, pos = lo + n, pos + n
    slabs_in = jnp.stack([jnp.concatenate(p, axis=1) for p in shards]).reshape(N_DEV, D_MODEL // 2, in_cols)
    grad_x, d_norm_w, (r_in,) = in_proj_bwd_x(xs, norm_w, w_main, w_small, dproj_main, dproj_conv, dsmall_ssd,
                                               dsmall_gdn, dhid, [slabs_in])
    d_alog, d_dtb = d_alog_s + d_alog_g, d_dtb_s + d_dtb_g
    packed, (o_nw, o_cs, o_cg, o_snw, o_fw, o_al, o_db, o_dv, o_gnw, o_loss) = _pack_cols([
        d_norm_w, dwb_ssd, dwb_gdn,
        d_ssd_nw.reshape(1, SSD_WIDTH), d_fw, d_alog, d_dtb, d_dvec, d_gdn_nw, loss_blk])

    (full_in, full_out), (r_small,) = exchange_halves([r_in, r_out], [packed])
    tot = sum_slabs(r_small, "sum_small")
    grad_w_in = full_in.reshape(D_MODEL, in_cols)
    grad_w_out = full_out.reshape(out_rows, D_MODEL)
    loss = tot[0, o_loss]
    sc, gc = ssd_conv_w.shape[2], gdn_conv_w.shape[2]
    row = lambda off, n, r=0: tot[r:r + 1, off:off + n]
    gs = [row(o_nw, D_MODEL),
          lax.dynamic_slice(tot, (0, o_cs + chip * sc), (4, sc)),
          row(o_cs, SSD_CONV, 4),
          row(o_db, SSD_HEADS), row(o_al, SSD_HEADS), row(o_dv, SSD_HEADS),
          row(o_snw, SSD_WIDTH),
          lax.dynamic_slice(tot, (0, o_cg + chip * gc), (4, gc)),
          row(o_db + LANE_GA, GDN_HEADS), row(o_al + LANE_GA, GDN_HEADS),
          row(o_gnw, GDN_DV), row(o_fw, D_MODEL)]

    names = ["norm_w", "ssd_conv_w", "ssd_conv_b", "ssd_dt_bias", "ssd_a_log", "ssd_d", "ssd_norm_w", "gdn_conv_w",
             "gdn_dt_bias", "gdn_a_log", "gdn_norm_w", "final_norm_w"]
    ws = [norm_w, ssd_conv_w, ssd_conv_b, ssd_dt_bias, ssd_a_log, ssd_d, ssd_norm_w, gdn_conv_w, gdn_dt_bias,
          gdn_a_log, gdn_norm_w, final_norm_w]
    ms = [m_norm_w, m_ssd_conv_w, m_ssd_conv_b, m_ssd_dt_bias, m_ssd_a_log, m_ssd_d, m_ssd_norm_w, m_gdn_conv_w,
          m_gdn_dt_bias, m_gdn_a_log, m_gdn_norm_w, m_final_norm_w]
    vs = [v_norm_w, v_ssd_conv_w, v_ssd_conv_b, v_ssd_dt_bias, v_ssd_a_log, v_ssd_d, v_ssd_norm_w, v_gdn_conv_w,
          v_gdn_dt_bias, v_gdn_a_log, v_gdn_norm_w, v_final_norm_w]
    shapes = [w.shape for w in ws]
    flat = lambda arrs: [a.reshape(g.shape) for a, g in zip(arrs, gs)]
    d_s, m_s, v_s = adamw_many(flat(ws), gs, flat(ms), flat(vs))
    back = lambda arrs: dict(zip(names, [a.reshape(s) for a, s in zip(arrs, shapes)]))
    delta, new_m, new_v, grads = back(d_s), back(m_s), back(v_s), back(gs)
    d_in, m_in, v_in = adamw(w_in, grad_w_in, m_w_in, v_w_in, "adamw_w_in")
    d_out, m_out, v_out = adamw(w_out, grad_w_out, m_w_out, v_w_out, "adamw_w_out")
    for tbl, a_in, a_out in ((grads, grad_w_in[None], grad_w_out[None]), (delta, d_in, d_out), (new_m, m_in, m_out),
                             (new_v, v_in, v_out)):
        tbl["w_in"] = a_in
        tbl["w_out"] = a_out

    order = ["norm_w", "w_in", "ssd_conv_w", "ssd_conv_b", "ssd_dt_bias", "ssd_a_log", "ssd_d", "ssd_norm_w",
             "gdn_conv_w", "gdn_dt_bias", "gdn_a_log", "gdn_norm_w", "w_out", "final_norm_w"]
    return (loss.reshape(()), grad_x[None], *[grads[k] for k in order], *[delta[k] for k in order],
            *[new_m[k] for k in order], *[new_v[k] for k in order])
```
